```python
import jax
import jax.numpy as jnp
from jax import lax
import numpy as np

D_MODEL = 1024
BATCH = 16
SEQ = 2048
DEPTH = 2

N_EVEN = (DEPTH + 1) // 2
N_ODD = DEPTH // 2
EPS = 1e-6
NEG_INF = -1e30

SG_HEADS = 4
SG_HEAD_DIM = D_MODEL // 8
SG_WIDTH = SG_HEADS * SG_HEAD_DIM
SG_CHUNK = 128
SC_HEADS = 4
SC_HEAD_DIM = D_MODEL // 8
SC_WIDTH = SC_HEADS * SC_HEAD_DIM
CONV_WIDTH = 3
EVEN_IN = 2 * SG_WIDTH + 3 * SC_WIDTH
EVEN_MIX = SG_WIDTH + SC_WIDTH

POOL_WINDOWS = (2, 4, 8, 16)
POOL_GROUPS = len(POOL_WINDOWS)
POOL_GROUP_DIM = D_MODEL // 16
POOL_WIDTH = POOL_GROUPS * POOL_GROUP_DIM
MLA_HEADS = 6
Q_LORA = 3 * D_MODEL // 8
KV_LORA = D_MODEL // 4
QK_NOPE = 128
QK_ROPE = 64
QK_DIM = QK_NOPE + QK_ROPE
V_DIM = 128
ROPE_THETA = 10000.0
Q_BLOCK = 128
ODD_IN = POOL_WIDTH + Q_LORA + KV_LORA + QK_ROPE
ODD_MIX = POOL_WIDTH + MLA_HEADS * V_DIM

D_FF = ((8 * D_MODEL + 3 * 256 - 1) // (3 * 256)) * 256

kernel_name = 'hybrid_sgu_conv_pool_mla_trunk'


def rms_norm(x, g):
    xf = x.astype(jnp.float32)
    y = xf * lax.rsqrt(jnp.mean(xf * xf, axis=-1, keepdims=True) + EPS)
    return (y * g.astype(jnp.float32)).astype(x.dtype)


def layer_norm(x, g):
    xf = x.astype(jnp.float32)
    mu = jnp.mean(xf, axis=-1, keepdims=True)
    xc = xf - mu
    y = xc * lax.rsqrt(jnp.mean(xc * xc, axis=-1, keepdims=True) + EPS)
    return (y * g.astype(jnp.float32)).astype(x.dtype)


def spatial_gating(u, v, ln_g, w_s, b_s):
    bsz, s, _ = v.shape
    n_chunks = s // SG_CHUNK
    v = layer_norm(v.reshape(bsz, s, SG_HEADS, SG_HEAD_DIM), ln_g.reshape(SG_HEADS, SG_HEAD_DIM))
    v = v.reshape(bsz, n_chunks, SG_CHUNK, SG_HEADS, SG_HEAD_DIM)
    causal = jnp.tril(jnp.ones((SG_CHUNK, SG_CHUNK), dtype=bool))
    w = jnp.where(causal[None], w_s, 0.0).astype(v.dtype)
    mixed = jnp.einsum('hts,bnshd->bnthd', w, v) + b_s.T.astype(v.dtype)[None, None, :, :, None]
    return u * mixed.reshape(bsz, s, SG_WIDTH)


def short_conv(b_gate, c_gate, h, conv_w):
    z = c_gate * h
    y = lax.conv_general_dilated(
        z, conv_w[:, None, :].astype(z.dtype), window_strides=(1,),
        padding=[(CONV_WIDTH - 1, 0)], dimension_numbers=('NWC', 'WIO', 'NWC'),
        feature_group_count=SC_WIDTH)
    return b_gate * y


def multiscale_pool(z, lin_w, scale):
    bsz, s, _ = z.shape
    zf = z.astype(jnp.float32)
    cs = jnp.pad(jnp.cumsum(zf, axis=1), ((0, 0), (1, 0), (0, 0)))
    t = jnp.arange(1, s + 1, dtype=jnp.float32)[None, :, None]
    groups = []
    for g, w in enumerate(POOL_WINDOWS):
        lo, hi = g * POOL_GROUP_DIM, (g + 1) * POOL_GROUP_DIM
        c = cs[..., lo:hi]
        lower = jnp.pad(c[:, :s + 1 - w], ((0, 0), (w - 1, 0), (0, 0)))
        mean = (c[:, 1:] - lower) / jnp.minimum(t, float(w))
        groups.append(mean - zf[..., lo:hi])
    pooled = jnp.stack(groups, axis=2).astype(z.dtype)
    out = jnp.einsum('bsgi,gio->bsgo', pooled, lin_w)
    return out.reshape(bsz, s, POOL_WIDTH) * scale


def rope_tables(positions):
    inv_freq = ROPE_THETA ** (-jnp.arange(0, QK_ROPE, 2, dtype=jnp.float32) / QK_ROPE)
    ang = positions.astype(jnp.float32)[..., None] * inv_freq
    return jnp.cos(ang), jnp.sin(ang)


def apply_rope(x, cos, sin):
    c = cos[:, :, None, :].astype(x.dtype)
    s = sin[:, :, None, :].astype(x.dtype)
    x1, x2 = jnp.split(x, 2, axis=-1)
    return jnp.concatenate([x1 * c - x2 * s, x2 * c + x1 * s], axis=-1)


def latent_attention(q_lat, kv_lat, k_rope, cos, sin, q_a_g, q_b, kv_a_g, kv_b, q_g, k_g):
    bsz, s, _ = q_lat.shape
    q = (rms_norm(q_lat, q_a_g) @ q_b).reshape(bsz, s, MLA_HEADS, QK_DIM)
    kv = (rms_norm(kv_lat, kv_a_g) @ kv_b).reshape(bsz, s, MLA_HEADS, QK_NOPE + V_DIM)
    k_nope, v = kv[..., :QK_NOPE], kv[..., QK_NOPE:]
    k = jnp.concatenate(
        [k_nope, jnp.broadcast_to(k_rope[:, :, None, :], (bsz, s, MLA_HEADS, QK_ROPE))], axis=-1)
    q = rms_norm(q, q_g)
    k = rms_norm(k, k_g)
    q = jnp.concatenate([q[..., :QK_NOPE], apply_rope(q[..., QK_NOPE:], cos, sin)], axis=-1)
    k = jnp.concatenate([k[..., :QK_NOPE], apply_rope(k[..., QK_NOPE:], cos, sin)], axis=-1)
    scale = QK_DIM ** -0.5
    outs = []
    for i in range(s // Q_BLOCK):
        q0, k_end = i * Q_BLOCK, (i + 1) * Q_BLOCK
        logits = jnp.einsum('bqhd,bkhd->bhqk', q[:, q0:k_end], k[:, :k_end]).astype(jnp.float32) * scale
        mask = (q0 + jnp.arange(Q_BLOCK))[:, None] >= jnp.arange(k_end)[None, :]
        p = jax.nn.softmax(jnp.where(mask, logits, NEG_INF), axis=-1).astype(v.dtype)
        outs.append(jnp.einsum('bhqk,bkhd->bqhd', p, v[:, :k_end]))
    return jnp.concatenate(outs, axis=1).reshape(bsz, s, MLA_HEADS * V_DIM)


def swiglu(h, w_gate, w_up, w_down):
    return (jax.nn.silu(h @ w_gate) * (h @ w_up)) @ w_down


def _fwd_setup_inputs(seed: int = 0) -> dict:
    key = jax.random.key(seed)
    k = jax.random.split(key, 23)
    f32 = jnp.float32

    def nrm(kk, shape, fan_in):
        return jax.random.normal(kk, shape, f32) * (fan_in ** -0.5)

    def gain(kk, shape, noise=0.02):
        return 1.0 + noise * jax.random.normal(kk, shape, f32)

    x = jax.random.normal(k[0], (BATCH, SEQ, D_MODEL), f32)
    positions = (jnp.arange(SEQ, dtype=jnp.int32)[None, :]
                 + jax.random.randint(k[1], (BATCH, 1), 0, SEQ, dtype=jnp.int32))
    return {
        'x': x,
        'positions': positions,
        'mix_norm': gain(k[2], (DEPTH, D_MODEL)),
        'ffn_norm': gain(k[3], (DEPTH, D_MODEL)),
        'even_w_in': nrm(k[4], (N_EVEN, D_MODEL, EVEN_IN), D_MODEL),
        'sg_ln_g': gain(k[5], (N_EVEN, SG_WIDTH)),
        'sg_w_s': nrm(k[6], (N_EVEN, SG_HEADS, SG_CHUNK, SG_CHUNK), SG_CHUNK),
        'sg_b_s': gain(k[7], (N_EVEN, SG_HEADS, SG_CHUNK), 0.1),
        'sc_conv_w': nrm(k[8], (N_EVEN, CONV_WIDTH, SC_WIDTH), CONV_WIDTH),
        'even_w_out': nrm(k[9], (N_EVEN, EVEN_MIX, D_MODEL), EVEN_MIX),
        'odd_w_in': nrm(k[10], (N_ODD, D_MODEL, ODD_IN), D_MODEL),
        'pool_w': nrm(k[11], (N_ODD, POOL_GROUPS, POOL_GROUP_DIM, POOL_GROUP_DIM), POOL_GROUP_DIM),
        'pool_scale': gain(k[12], (N_ODD, POOL_WIDTH), 0.1),
        'q_a_norm': gain(k[13], (N_ODD, Q_LORA)),
        'q_b': nrm(k[14], (N_ODD, Q_LORA, MLA_HEADS * QK_DIM), Q_LORA),
        'kv_a_norm': gain(k[15], (N_ODD, KV_LORA)),
        'kv_b': nrm(k[16], (N_ODD, KV_LORA, MLA_HEADS * (QK_NOPE + V_DIM)), KV_LORA),
        'q_norm': gain(k[17], (N_ODD, QK_DIM)),
        'k_norm': gain(k[18], (N_ODD, QK_DIM)),
        'odd_w_out': nrm(k[19], (N_ODD, ODD_MIX, D_MODEL), ODD_MIX),
        'ffn_w_gate': nrm(k[20], (DEPTH, D_MODEL, D_FF), D_MODEL),
        'ffn_w_up': nrm(k[21], (DEPTH, D_MODEL, D_FF), D_MODEL),
        'ffn_w_down': nrm(k[22], (DEPTH, D_FF, D_MODEL), D_FF),
    }


def _fwd_reference(x, positions, mix_norm, ffn_norm, even_w_in, sg_ln_g, sg_w_s, sg_b_s, sc_conv_w,
              even_w_out, odd_w_in, pool_w, pool_scale, q_a_norm, q_b, kv_a_norm, kv_b, q_norm,
              k_norm, odd_w_out, ffn_w_gate, ffn_w_up, ffn_w_down):
    cos, sin = rope_tables(positions)
    for layer in range(DEPTH):
        i = layer // 2
        h = rms_norm(x, mix_norm[layer])
        if layer % 2 == 0:
            proj = h @ even_w_in[i]
            u, v, b_gate, c_gate, hv = jnp.split(
                proj, [SG_WIDTH, 2 * SG_WIDTH, 2 * SG_WIDTH + SC_WIDTH, 2 * SG_WIDTH + 2 * SC_WIDTH], axis=-1)
            a_out = spatial_gating(jax.nn.gelu(u, approximate=False), jax.nn.gelu(v, approximate=False),
                                   sg_ln_g[i], sg_w_s[i], sg_b_s[i])
            b_out = short_conv(b_gate, c_gate, hv, sc_conv_w[i])
            x = x + jnp.concatenate([a_out, b_out], axis=-1) @ even_w_out[i]
        else:
            proj = h @ odd_w_in[i]
            z_pool, q_lat, kv_lat, k_rope = jnp.split(
                proj, [POOL_WIDTH, POOL_WIDTH + Q_LORA, POOL_WIDTH + Q_LORA + KV_LORA], axis=-1)
            c_out = multiscale_pool(z_pool, pool_w[i], pool_scale[i])
            d_out = latent_attention(q_lat, kv_lat, k_rope, cos, sin, q_a_norm[i], q_b[i],
                                     kv_a_norm[i], kv_b[i], q_norm[i], k_norm[i])
            x = x + jnp.concatenate([c_out, d_out], axis=-1) @ odd_w_out[i]
        h = rms_norm(x, ffn_norm[layer])
        x = x + swiglu(h, ffn_w_gate[layer], ffn_w_up[layer], ffn_w_down[layer])
    return x


import jax as _jax
import jax.numpy as _jnp

TWIN_FORMAT = 'train_step'
FWD_PARAMS = ['x', 'positions', 'mix_norm', 'ffn_norm', 'even_w_in', 'sg_ln_g', 'sg_w_s', 'sg_b_s', 'sc_conv_w', 'even_w_out', 'odd_w_in', 'pool_w', 'pool_scale', 'q_a_norm', 'q_b', 'kv_a_norm', 'kv_b', 'q_norm', 'k_norm', 'odd_w_out', 'ffn_w_gate', 'ffn_w_up', 'ffn_w_down']
TWIN_WEIGHTS = ['mix_norm', 'ffn_norm', 'even_w_in', 'sg_ln_g', 'sg_w_s', 'sg_b_s', 'sc_conv_w', 'even_w_out', 'odd_w_in', 'pool_w', 'pool_scale', 'q_a_norm', 'q_b', 'kv_a_norm', 'kv_b', 'q_norm', 'k_norm', 'odd_w_out', 'ffn_w_gate', 'ffn_w_up', 'ffn_w_down']
TWIN_DIFF_INPUT = 'x'
TWIN_INPUTS = ['x', 'positions', 'mix_norm', 'ffn_norm', 'even_w_in', 'sg_ln_g', 'sg_w_s', 'sg_b_s', 'sc_conv_w', 'even_w_out', 'odd_w_in', 'pool_w', 'pool_scale', 'q_a_norm', 'q_b', 'kv_a_norm', 'kv_b', 'q_norm', 'k_norm', 'odd_w_out', 'ffn_w_gate', 'ffn_w_up', 'ffn_w_down', 'loss_target', 'm_mix_norm', 'm_ffn_norm', 'm_even_w_in', 'm_sg_ln_g', 'm_sg_w_s', 'm_sg_b_s', 'm_sc_conv_w', 'm_even_w_out', 'm_odd_w_in', 'm_pool_w', 'm_pool_scale', 'm_q_a_norm', 'm_q_b', 'm_kv_a_norm', 'm_kv_b', 'm_q_norm', 'm_k_norm', 'm_odd_w_out', 'm_ffn_w_gate', 'm_ffn_w_up', 'm_ffn_w_down', 'v_mix_norm', 'v_ffn_norm', 'v_even_w_in', 'v_sg_ln_g', 'v_sg_w_s', 'v_sg_b_s', 'v_sc_conv_w', 'v_even_w_out', 'v_odd_w_in', 'v_pool_w', 'v_pool_scale', 'v_q_a_norm', 'v_q_b', 'v_kv_a_norm', 'v_kv_b', 'v_q_norm', 'v_k_norm', 'v_odd_w_out', 'v_ffn_w_gate', 'v_ffn_w_up', 'v_ffn_w_down']
TWIN_OUTPUTS = ['loss', 'grad_x', 'grad_mix_norm', 'grad_ffn_norm', 'grad_even_w_in', 'grad_sg_ln_g', 'grad_sg_w_s', 'grad_sg_b_s', 'grad_sc_conv_w', 'grad_even_w_out', 'grad_odd_w_in', 'grad_pool_w', 'grad_pool_scale', 'grad_q_a_norm', 'grad_q_b', 'grad_kv_a_norm', 'grad_kv_b', 'grad_q_norm', 'grad_k_norm', 'grad_odd_w_out', 'grad_ffn_w_gate', 'grad_ffn_w_up', 'grad_ffn_w_down', 'delta_mix_norm', 'delta_ffn_norm', 'delta_even_w_in', 'delta_sg_ln_g', 'delta_sg_w_s', 'delta_sg_b_s', 'delta_sc_conv_w', 'delta_even_w_out', 'delta_odd_w_in', 'delta_pool_w', 'delta_pool_scale', 'delta_q_a_norm', 'delta_q_b', 'delta_kv_a_norm', 'delta_kv_b', 'delta_q_norm', 'delta_k_norm', 'delta_odd_w_out', 'delta_ffn_w_gate', 'delta_ffn_w_up', 'delta_ffn_w_down', 'new_m_mix_norm', 'new_m_ffn_norm', 'new_m_even_w_in', 'new_m_sg_ln_g', 'new_m_sg_w_s', 'new_m_sg_b_s', 'new_m_sc_conv_w', 'new_m_even_w_out', 'new_m_odd_w_in', 'new_m_pool_w', 'new_m_pool_scale', 'new_m_q_a_norm', 'new_m_q_b', 'new_m_kv_a_norm', 'new_m_kv_b', 'new_m_q_norm', 'new_m_k_norm', 'new_m_odd_w_out', 'new_m_ffn_w_gate', 'new_m_ffn_w_up', 'new_m_ffn_w_down', 'new_v_mix_norm', 'new_v_ffn_norm', 'new_v_even_w_in', 'new_v_sg_ln_g', 'new_v_sg_w_s', 'new_v_sg_b_s', 'new_v_sc_conv_w', 'new_v_even_w_out', 'new_v_odd_w_in', 'new_v_pool_w', 'new_v_pool_scale', 'new_v_q_a_norm', 'new_v_q_b', 'new_v_kv_a_norm', 'new_v_kv_b', 'new_v_q_norm', 'new_v_k_norm', 'new_v_odd_w_out', 'new_v_ffn_w_gate', 'new_v_ffn_w_up', 'new_v_ffn_w_down']
TWIN_LEAF_KINDS = {'loss': 'loss', 'grad_x': 'grad_x', 'grad_mix_norm': 'grad_w', 'grad_ffn_norm': 'grad_w', 'grad_even_w_in': 'grad_w', 'grad_sg_ln_g': 'grad_w', 'grad_sg_w_s': 'grad_w', 'grad_sg_b_s': 'grad_w', 'grad_sc_conv_w': 'grad_w', 'grad_even_w_out': 'grad_w', 'grad_odd_w_in': 'grad_w', 'grad_pool_w': 'grad_w', 'grad_pool_scale': 'grad_w', 'grad_q_a_norm': 'grad_w', 'grad_q_b': 'grad_w', 'grad_kv_a_norm': 'grad_w', 'grad_kv_b': 'grad_w', 'grad_q_norm': 'grad_w', 'grad_k_norm': 'grad_w', 'grad_odd_w_out': 'grad_w', 'grad_ffn_w_gate': 'grad_w', 'grad_ffn_w_up': 'grad_w', 'grad_ffn_w_down': 'grad_w', 'delta_mix_norm': 'delta_w', 'delta_ffn_norm': 'delta_w', 'delta_even_w_in': 'delta_w', 'delta_sg_ln_g': 'delta_w', 'delta_sg_w_s': 'delta_w', 'delta_sg_b_s': 'delta_w', 'delta_sc_conv_w': 'delta_w', 'delta_even_w_out': 'delta_w', 'delta_odd_w_in': 'delta_w', 'delta_pool_w': 'delta_w', 'delta_pool_scale': 'delta_w', 'delta_q_a_norm': 'delta_w', 'delta_q_b': 'delta_w', 'delta_kv_a_norm': 'delta_w', 'delta_kv_b': 'delta_w', 'delta_q_norm': 'delta_w', 'delta_k_norm': 'delta_w', 'delta_odd_w_out': 'delta_w', 'delta_ffn_w_gate': 'delta_w', 'delta_ffn_w_up': 'delta_w', 'delta_ffn_w_down': 'delta_w', 'new_m_mix_norm': 'new_m', 'new_m_ffn_norm': 'new_m', 'new_m_even_w_in': 'new_m', 'new_m_sg_ln_g': 'new_m', 'new_m_sg_w_s': 'new_m', 'new_m_sg_b_s': 'new_m', 'new_m_sc_conv_w': 'new_m', 'new_m_even_w_out': 'new_m', 'new_m_odd_w_in': 'new_m', 'new_m_pool_w': 'new_m', 'new_m_pool_scale': 'new_m', 'new_m_q_a_norm': 'new_m', 'new_m_q_b': 'new_m', 'new_m_kv_a_norm': 'new_m', 'new_m_kv_b': 'new_m', 'new_m_q_norm': 'new_m', 'new_m_k_norm': 'new_m', 'new_m_odd_w_out': 'new_m', 'new_m_ffn_w_gate': 'new_m', 'new_m_ffn_w_up': 'new_m', 'new_m_ffn_w_down': 'new_m', 'new_v_mix_norm': 'new_v', 'new_v_ffn_norm': 'new_v', 'new_v_even_w_in': 'new_v', 'new_v_sg_ln_g': 'new_v', 'new_v_sg_w_s': 'new_v', 'new_v_sg_b_s': 'new_v', 'new_v_sc_conv_w': 'new_v', 'new_v_even_w_out': 'new_v', 'new_v_odd_w_in': 'new_v', 'new_v_pool_w': 'new_v', 'new_v_pool_scale': 'new_v', 'new_v_q_a_norm': 'new_v', 'new_v_q_b': 'new_v', 'new_v_kv_a_norm': 'new_v', 'new_v_kv_b': 'new_v', 'new_v_q_norm': 'new_v', 'new_v_k_norm': 'new_v', 'new_v_odd_w_out': 'new_v', 'new_v_ffn_w_gate': 'new_v', 'new_v_ffn_w_up': 'new_v', 'new_v_ffn_w_down': 'new_v'}


def _forward(args):
    return _fwd_reference(*[args[k] for k in FWD_PARAMS])


def _output_shape():
    out = _jax.eval_shape(lambda: _forward(_fwd_setup_inputs(0)))
    return out.shape, out.dtype

N_MICROBATCH = 1
ADAM_LR = 0.001
ADAM_B1 = 0.9
ADAM_B2 = 0.999
ADAM_EPS = 1e-08
ADAM_WD = 0.01
ADAM_STEP = 10
PER_EXAMPLE_BATCH_AXIS = {'x': 0, 'positions': 0, 'loss_target': 0}
SHARED_INPUTS = []
_WEIGHT_DTYPES = {'mix_norm': _jnp.float32, 'ffn_norm': _jnp.float32, 'even_w_in': _jnp.float32, 'sg_ln_g': _jnp.float32, 'sg_w_s': _jnp.float32, 'sg_b_s': _jnp.float32, 'sc_conv_w': _jnp.float32, 'even_w_out': _jnp.float32, 'odd_w_in': _jnp.float32, 'pool_w': _jnp.float32, 'pool_scale': _jnp.float32, 'q_a_norm': _jnp.float32, 'q_b': _jnp.float32, 'kv_a_norm': _jnp.float32, 'kv_b': _jnp.float32, 'q_norm': _jnp.float32, 'k_norm': _jnp.float32, 'odd_w_out': _jnp.float32, 'ffn_w_gate': _jnp.float32, 'ffn_w_up': _jnp.float32, 'ffn_w_down': _jnp.float32}
MOMENT_SCALE = {'mix_norm': 4.079986e+01, 'ffn_norm': 2.469790e+01, 'even_w_in': 8.876125e-01, 'sg_ln_g': 6.599816e+00, 'sg_w_s': 5.980205e-01, 'sg_b_s': 1.400342e+01, 'sc_conv_w': 1.767104e+01, 'even_w_out': 2.250163e+00, 'odd_w_in': 1.423187e+00, 'pool_w': 3.409945e+00, 'pool_scale': 2.568405e+01, 'q_a_norm': 1.269667e-01, 'q_b': 7.251627e-02, 'kv_a_norm': 2.720420e+00, 'kv_b': 7.957144e-01, 'q_norm': 6.508599e-01, 'k_norm': 6.492458e-01, 'odd_w_out': 1.142040e+00, 'ffn_w_gate': 4.039585e-01, 'ffn_w_up': 2.830093e-01, 'ffn_w_down': 4.389344e-01}


def _to_microbatches(a, axis):
    t = _jnp.moveaxis(a, axis, 0)
    t = t.reshape((N_MICROBATCH, t.shape[0] // N_MICROBATCH) + t.shape[1:])
    return _jnp.moveaxis(t, 1, axis + 1)


def setup_inputs(seed: int = 0) -> dict:
    inp = _fwd_setup_inputs(seed)
    key = _jax.random.fold_in(_jax.random.key(seed), 7919)
    shape, _ = _output_shape()
    out = dict(inp)
    out["loss_target"] = _jax.random.normal(_jax.random.fold_in(key, 0), shape, _jnp.float32)
    for i, name in enumerate(TWIN_WEIGHTS):
        w = inp[name].astype(_jnp.float32)
        if MOMENT_SCALE is None:
            s = _jnp.sqrt(_jnp.mean(_jnp.square(w)) + 1e-30)
        else:
            s = MOMENT_SCALE[name]
        km, kv = _jax.random.split(_jax.random.fold_in(key, i + 1))
        out[name] = w
        out["m_" + name] = s * _jax.random.normal(km, w.shape, _jnp.float32)
        out["v_" + name] = (s * s) * _jax.random.uniform(kv, w.shape, _jnp.float32, 0.5, 1.5)
    if N_MICROBATCH > 1:
        for name, axis in PER_EXAMPLE_BATCH_AXIS.items():
            out[name] = _to_microbatches(out[name], axis)
    return {'x': out['x'], 'positions': out['positions'], 'mix_norm': out['mix_norm'], 'ffn_norm': out['ffn_norm'], 'even_w_in': out['even_w_in'], 'sg_ln_g': out['sg_ln_g'], 'sg_w_s': out['sg_w_s'], 'sg_b_s': out['sg_b_s'], 'sc_conv_w': out['sc_conv_w'], 'even_w_out': out['even_w_out'], 'odd_w_in': out['odd_w_in'], 'pool_w': out['pool_w'], 'pool_scale': out['pool_scale'], 'q_a_norm': out['q_a_norm'], 'q_b': out['q_b'], 'kv_a_norm': out['kv_a_norm'], 'kv_b': out['kv_b'], 'q_norm': out['q_norm'], 'k_norm': out['k_norm'], 'odd_w_out': out['odd_w_out'], 'ffn_w_gate': out['ffn_w_gate'], 'ffn_w_up': out['ffn_w_up'], 'ffn_w_down': out['ffn_w_down'], 'loss_target': out['loss_target'], 'm_mix_norm': out['m_mix_norm'], 'm_ffn_norm': out['m_ffn_norm'], 'm_even_w_in': out['m_even_w_in'], 'm_sg_ln_g': out['m_sg_ln_g'], 'm_sg_w_s': out['m_sg_w_s'], 'm_sg_b_s': out['m_sg_b_s'], 'm_sc_conv_w': out['m_sc_conv_w'], 'm_even_w_out': out['m_even_w_out'], 'm_odd_w_in': out['m_odd_w_in'], 'm_pool_w': out['m_pool_w'], 'm_pool_scale': out['m_pool_scale'], 'm_q_a_norm': out['m_q_a_norm'], 'm_q_b': out['m_q_b'], 'm_kv_a_norm': out['m_kv_a_norm'], 'm_kv_b': out['m_kv_b'], 'm_q_norm': out['m_q_norm'], 'm_k_norm': out['m_k_norm'], 'm_odd_w_out': out['m_odd_w_out'], 'm_ffn_w_gate': out['m_ffn_w_gate'], 'm_ffn_w_up': out['m_ffn_w_up'], 'm_ffn_w_down': out['m_ffn_w_down'], 'v_mix_norm': out['v_mix_norm'], 'v_ffn_norm': out['v_ffn_norm'], 'v_even_w_in': out['v_even_w_in'], 'v_sg_ln_g': out['v_sg_ln_g'], 'v_sg_w_s': out['v_sg_w_s'], 'v_sg_b_s': out['v_sg_b_s'], 'v_sc_conv_w': out['v_sc_conv_w'], 'v_even_w_out': out['v_even_w_out'], 'v_odd_w_in': out['v_odd_w_in'], 'v_pool_w': out['v_pool_w'], 'v_pool_scale': out['v_pool_scale'], 'v_q_a_norm': out['v_q_a_norm'], 'v_q_b': out['v_q_b'], 'v_kv_a_norm': out['v_kv_a_norm'], 'v_kv_b': out['v_kv_b'], 'v_q_norm': out['v_q_norm'], 'v_k_norm': out['v_k_norm'], 'v_odd_w_out': out['v_odd_w_out'], 'v_ffn_w_gate': out['v_ffn_w_gate'], 'v_ffn_w_up': out['v_ffn_w_up'], 'v_ffn_w_down': out['v_ffn_w_down']}


def _loss(weights, diff, rest, loss_target):
    with _jax.named_scope("forward"):
        args = {**rest, TWIN_DIFF_INPUT: diff, **{k: w.astype(_WEIGHT_DTYPES[k]) for k, w in weights.items()}}
        y = _forward(args)
    with _jax.named_scope("loss_head"):
        err = _jnp.square(y.astype(_jnp.float32) - loss_target)
        return 0.5 * _jnp.sum(_jnp.mean(err, axis=-1)) if err.ndim else 0.5 * err


def _adamw(w, g, m, v):
    m = ADAM_B1 * m + (1.0 - ADAM_B1) * g
    v = ADAM_B2 * v + (1.0 - ADAM_B2) * _jnp.square(g)
    m_hat = m / (1.0 - ADAM_B1 ** ADAM_STEP)
    v_hat = v / (1.0 - ADAM_B2 ** ADAM_STEP)
    delta = -ADAM_LR * (m_hat / (_jnp.sqrt(v_hat) + ADAM_EPS) + ADAM_WD * w)
    return delta, m, v


def reference(x, positions, mix_norm, ffn_norm, even_w_in, sg_ln_g, sg_w_s, sg_b_s, sc_conv_w, even_w_out, odd_w_in, pool_w, pool_scale, q_a_norm, q_b, kv_a_norm, kv_b, q_norm, k_norm, odd_w_out, ffn_w_gate, ffn_w_up, ffn_w_down, loss_target, m_mix_norm, m_ffn_norm, m_even_w_in, m_sg_ln_g, m_sg_w_s, m_sg_b_s, m_sc_conv_w, m_even_w_out, m_odd_w_in, m_pool_w, m_pool_scale, m_q_a_norm, m_q_b, m_kv_a_norm, m_kv_b, m_q_norm, m_k_norm, m_odd_w_out, m_ffn_w_gate, m_ffn_w_up, m_ffn_w_down, v_mix_norm, v_ffn_norm, v_even_w_in, v_sg_ln_g, v_sg_w_s, v_sg_b_s, v_sc_conv_w, v_even_w_out, v_odd_w_in, v_pool_w, v_pool_scale, v_q_a_norm, v_q_b, v_kv_a_norm, v_kv_b, v_q_norm, v_k_norm, v_odd_w_out, v_ffn_w_gate, v_ffn_w_up, v_ffn_w_down):
    given = dict(x=x, positions=positions, mix_norm=mix_norm, ffn_norm=ffn_norm, even_w_in=even_w_in, sg_ln_g=sg_ln_g, sg_w_s=sg_w_s, sg_b_s=sg_b_s, sc_conv_w=sc_conv_w, even_w_out=even_w_out, odd_w_in=odd_w_in, pool_w=pool_w, pool_scale=pool_scale, q_a_norm=q_a_norm, q_b=q_b, kv_a_norm=kv_a_norm, kv_b=kv_b, q_norm=q_norm, k_norm=k_norm, odd_w_out=odd_w_out, ffn_w_gate=ffn_w_gate, ffn_w_up=ffn_w_up, ffn_w_down=ffn_w_down, loss_target=loss_target, m_mix_norm=m_mix_norm, m_ffn_norm=m_ffn_norm, m_even_w_in=m_even_w_in, m_sg_ln_g=m_sg_ln_g, m_sg_w_s=m_sg_w_s, m_sg_b_s=m_sg_b_s, m_sc_conv_w=m_sc_conv_w, m_even_w_out=m_even_w_out, m_odd_w_in=m_odd_w_in, m_pool_w=m_pool_w, m_pool_scale=m_pool_scale, m_q_a_norm=m_q_a_norm, m_q_b=m_q_b, m_kv_a_norm=m_kv_a_norm, m_kv_b=m_kv_b, m_q_norm=m_q_norm, m_k_norm=m_k_norm, m_odd_w_out=m_odd_w_out, m_ffn_w_gate=m_ffn_w_gate, m_ffn_w_up=m_ffn_w_up, m_ffn_w_down=m_ffn_w_down, v_mix_norm=v_mix_norm, v_ffn_norm=v_ffn_norm, v_even_w_in=v_even_w_in, v_sg_ln_g=v_sg_ln_g, v_sg_w_s=v_sg_w_s, v_sg_b_s=v_sg_b_s, v_sc_conv_w=v_sc_conv_w, v_even_w_out=v_even_w_out, v_odd_w_in=v_odd_w_in, v_pool_w=v_pool_w, v_pool_scale=v_pool_scale, v_q_a_norm=v_q_a_norm, v_q_b=v_q_b, v_kv_a_norm=v_kv_a_norm, v_kv_b=v_kv_b, v_q_norm=v_q_norm, v_k_norm=v_k_norm, v_odd_w_out=v_odd_w_out, v_ffn_w_gate=v_ffn_w_gate, v_ffn_w_up=v_ffn_w_up, v_ffn_w_down=v_ffn_w_down)
    weights = {n: given[n] for n in TWIN_WEIGHTS}
    shared = {n: given[n] for n in SHARED_INPUTS}
    per_example = {n: given[n] for n in ['x', 'positions']}
    grad_fn = _jax.value_and_grad(_loss, argnums=(0, 1))

    def one_microbatch(ex, loss_target):
        ex = dict(ex)
        diff = ex.pop(TWIN_DIFF_INPUT)
        return grad_fn(weights, diff, {**shared, **ex}, loss_target)

    if N_MICROBATCH == 1:
        loss, (grad_w, grad_x) = one_microbatch(per_example, given["loss_target"])
    else:
        def body(carry, xs):
            loss_sum, grad_sum = carry
            l_k, (gw_k, gx_k) = one_microbatch(xs[0], xs[1])
            with _jax.named_scope("update"):
                return (loss_sum + l_k, _jax.tree.map(_jnp.add, grad_sum, gw_k)), gx_k

        init = (_jnp.zeros((), _jnp.float32), _jax.tree.map(_jnp.zeros_like, weights))
        (loss, grad_w), grad_x = _jax.lax.scan(body, init, (per_example, given["loss_target"]))
    with _jax.named_scope("update"):
        delta_w, new_m, new_v = {}, {}, {}
        for n in TWIN_WEIGHTS:
            delta_w[n], new_m[n], new_v[n] = _adamw(weights[n], grad_w[n], given["m_" + n], given["v_" + n])
    return (loss, grad_x, *[grad_w[n] for n in TWIN_WEIGHTS], *[delta_w[n] for n in TWIN_WEIGHTS],
            *[new_m[n] for n in TWIN_WEIGHTS], *[new_v[n] for n in TWIN_WEIGHTS])
```

```python
import functools
import math

import jax
import jax.numpy as jnp
from jax import lax
from jax.experimental import pallas as pl
from jax.experimental.pallas import tpu as pltpu

F32, BF16 = jnp.float32, jnp.bfloat16
BS = pl.BlockSpec

D_MODEL = 1024
EPS = 1e-6
NEG_INF = -1e30
SG_HEADS, SG_DIM, SG_WIDTH, SG_CHUNK = 4, 128, 512, 128
SC_WIDTH, CONV_TAPS = 512, 3
EVEN_IN = 2 * SG_WIDTH + 3 * SC_WIDTH
POOL_WINDOWS = (2, 4, 8, 16)
POOL_DIM, POOL_WIDTH = 64, 256
POOL_HALO = 16
HEADS, Q_LORA, KV_LORA, QK_NOPE, QK_ROPE, V_DIM = 6, 384, 256, 128, 64, 128
QK_DIM = QK_NOPE + QK_ROPE
QK_PAD = 256
ODD_IN = POOL_WIDTH + Q_LORA + KV_LORA + QK_ROPE
ODD_IN_PAD = 1024
ROPE_THETA = 10000.0
ATTN_SCALE = QK_DIM ** -0.5
D_FF, N_CHIPS = 2816, 4
FF_SHARD = D_FF // N_CHIPS
ADAM_LR, ADAM_B1, ADAM_B2, ADAM_EPS, ADAM_WD, ADAM_STEP = 0.001, 0.9, 0.999, 1e-08, 0.01, 10
VMEM_LIMIT_V7X = 48 * 2**20
LANES, SUBLANES = 128, 8
MESH = pl.DeviceIdType.MESH
HBM = pl.BlockSpec(memory_space=pltpu.HBM)
VMEM = pl.BlockSpec(memory_space=pltpu.VMEM)

_DIMS = {"nn": (((1,), (0,)), ((), ())), "nt": (((1,), (1,)), ((), ())), "tn": (((0,), (0,)), ((), ()))}


def _dot(a, b, mode="nn"):
    return lax.dot_general(a.astype(BF16), b.astype(BF16), _DIMS[mode], preferred_element_type=F32)


def _call(body, *, name, out_shape, in_specs, out_specs, grid=(), scratch=(), aliases=None):
    params = pltpu.CompilerParams(vmem_limit_bytes=VMEM_LIMIT_V7X,
                                  **({"dimension_semantics": ("arbitrary",) * len(grid)} if grid else {}))
    return pl.pallas_call(body, name=name, grid=grid, in_specs=in_specs, out_specs=out_specs, out_shape=out_shape,
                          scratch_shapes=list(scratch), input_output_aliases=aliases or {}, compiler_params=params)


def _sds(shape, dtype):
    return jax.ShapeDtypeStruct(tuple(shape), dtype)


def _token_tile(seq):
    return 512 if seq % 512 == 0 else seq


def _matmul(name, mode, pairs, pair_specs, grid, out_shape, out_spec, acc_shape, add=None, add_spec=None):
    n, nk = len(pairs), grid[-1]

    def body(*refs):
        ab = refs[:2 * n]
        add_ref = refs[2 * n] if add is not None else None
        o_ref, acc = refs[-2], refs[-1]
        k = pl.program_id(len(grid) - 1)

        @pl.when(k == 0)
        def _():
            acc[...] = jnp.zeros_like(acc)

        for p in range(n):
            acc[...] += _dot(ab[2 * p][...], ab[2 * p + 1][...], mode)

        @pl.when(k == nk - 1)
        def _():
            r = acc[...]
            if add_ref is not None:
                r = r + add_ref[...]
            o_ref[...] = r.astype(o_ref.dtype)

    ops = [t for pr in pairs for t in pr] + ([add] if add is not None else [])
    specs = [s for pr in pair_specs for s in pr] + ([add_spec] if add is not None else [])
    return _call(body, name=name, grid=grid, in_specs=specs, out_specs=out_spec, out_shape=out_shape,
                 scratch=[pltpu.VMEM(acc_shape, F32)])(*ops)


def _mm(name, mode, a, b, out_dtype, tm=512, tn=512, tk=512, add=None):
    if mode == "tn":
        (K, M), N = a.shape, b.shape[1]
    else:
        (M, K), N = a.shape, (b.shape[1] if mode == "nn" else b.shape[0])
    tm, tn, tk = min(tm, M), min(tn, N), min(tk, K)
    a_spec = BS((tk, tm), lambda i, j, k: (k, i)) if mode == "tn" else BS((tm, tk), lambda i, j, k: (i, k))
    b_spec = BS((tn, tk), lambda i, j, k: (j, k)) if mode == "nt" else BS((tk, tn), lambda i, j, k: (k, j))
    o_spec = BS((tm, tn), lambda i, j, k: (i, j))
    return _matmul(name, mode, [(a, b)], [(a_spec, b_spec)], (M // tm, N // tn, K // tk), _sds((M, N), out_dtype),
                   o_spec, (tm, tn), add=add, add_spec=o_spec if add is not None else None)


def _rmsnorm_fwd(name, x, g, tm):
    T, d = x.shape

    def body(x_ref, g_ref, o_ref):
        xv = x_ref[...]
        y = xv * lax.rsqrt(jnp.mean(xv * xv, axis=-1, keepdims=True) + EPS)
        o_ref[...] = (y * g_ref[...]).astype(o_ref.dtype)

    return _call(body, name=name, grid=(T // tm,), in_specs=[BS((tm, d), lambda i: (i, 0)), BS((1, d), lambda i: (0, 0))],
                 out_specs=BS((tm, d), lambda i: (i, 0)), out_shape=_sds((T, d), BF16))(x, g.reshape(1, d))


def _rmsnorm_bwd(name, x, g, dh, dres, tm):
    T, d = x.shape

    def body(x_ref, g_ref, dh_ref, dres_ref, dx_ref, dg_ref):
        xv = x_ref[...]
        r = lax.rsqrt(jnp.mean(xv * xv, axis=-1, keepdims=True) + EPS)
        xhat = xv * r
        dhv = dh_ref[...]

        @pl.when(pl.program_id(0) == 0)
        def _():
            dg_ref[...] = jnp.zeros_like(dg_ref)

        dg_ref[...] += jnp.sum(dhv * xhat, axis=0, keepdims=True)
        dxhat = dhv * g_ref[...]
        dx_ref[...] = dres_ref[...] + r * (dxhat - xhat * jnp.mean(dxhat * xhat, axis=-1, keepdims=True))

    row = BS((tm, d), lambda i: (i, 0))
    vec = BS((1, d), lambda i: (0, 0))
    return _call(body, name=name, grid=(T // tm,), in_specs=[row, vec, row, row], out_specs=[row, vec],
                 out_shape=[_sds((T, d), F32), _sds((1, d), F32)])(x, g.reshape(1, d), dh, dres)


def _ffn_up(name, h, wg, wu, tm):
    T = h.shape[0]

    def body(h_ref, wg_ref, wu_ref, g_ref, u_ref, a_ref):
        hv = h_ref[...]
        g = _dot(hv, wg_ref[...])
        u = _dot(hv, wu_ref[...])
        g_ref[...] = g
        u_ref[...] = u
        a_ref[...] = (g * (1.0 / (1.0 + jnp.exp(-g))) * u).astype(BF16)

    w_spec = BS((None, D_MODEL, FF_SHARD), lambda i, j: (j, 0, 0))
    o_spec = BS((None, tm, FF_SHARD), lambda i, j: (j, i, 0))
    sh = (N_CHIPS, T, FF_SHARD)
    return _call(body, name=name, grid=(T // tm, N_CHIPS), in_specs=[BS((tm, D_MODEL), lambda i, j: (i, 0)), w_spec, w_spec],
                 out_specs=[o_spec, o_spec, o_spec], out_shape=[_sds(sh, F32), _sds(sh, F32), _sds(sh, BF16)])(h, wg, wu)


def _ffn_act_bwd(name, dxo, wd, g, u, tm):
    T = dxo.shape[0]

    def body(dx_ref, wd_ref, g_ref, u_ref, dg_ref, du_ref):
        da = _dot(dx_ref[...], wd_ref[...], "nt")
        g = g_ref[...]
        sig = 1.0 / (1.0 + jnp.exp(-g))
        dg_ref[...] = (da * u_ref[...] * (sig * (1.0 + g * (1.0 - sig)))).astype(BF16)
        du_ref[...] = (da * (g * sig)).astype(BF16)

    t_spec = BS((None, tm, FF_SHARD), lambda i, j: (j, i, 0))
    sh = _sds((N_CHIPS, T, FF_SHARD), BF16)
    return _call(body, name=name, grid=(T // tm, N_CHIPS),
                 in_specs=[BS((tm, D_MODEL), lambda i, j: (i, 0)), BS((None, FF_SHARD, D_MODEL), lambda i, j: (j, 0, 0)), t_spec, t_spec],
                 out_specs=[t_spec, t_spec], out_shape=[sh, sh])(dxo, wd, g, u)


def _ffn_fwd(l, x, gain, wg, wu, wd, tm):
    T = x.shape[0]
    h = _rmsnorm_fwd(f"ffn{l}_norm", x, gain, tm)
    g, u, a = _ffn_up(f"ffn{l}_up", h, wg, wu, tm)
    tn = 512
    out = _matmul(f"ffn{l}_down", "nn", [(a, wd.reshape(D_FF, D_MODEL))],
                  [(BS((None, tm, FF_SHARD), lambda i, j, k: (k, i, 0)), BS((FF_SHARD, tn), lambda i, j, k: (k, j)))],
                  (T // tm, D_MODEL // tn, N_CHIPS), _sds((T, D_MODEL), F32), BS((tm, tn), lambda i, j, k: (i, j)), (tm, tn),
                  add=x, add_spec=BS((tm, tn), lambda i, j, k: (i, j)))
    return out, (h, g, u, a)


def _ffn_bwd(l, x, gain, wg, wu, wd, saved, dxo, tm):
    h, g, u, a = saved
    T = x.shape[0]
    dg, du = _ffn_act_bwd(f"ffn{l}_act_bwd", dxo, wd, g, u, tm)
    tk = min(512, T)
    tn = 512
    dwd = _matmul(f"ffn{l}_dwd", "tn", [(a, dxo)],
                  [(BS((None, tk, FF_SHARD), lambda j, n, k: (j, k, 0)), BS((tk, tn), lambda j, n, k: (k, n)))],
                  (N_CHIPS, D_MODEL // tn, T // tk), _sds((N_CHIPS, FF_SHARD, D_MODEL), F32),
                  BS((None, FF_SHARD, tn), lambda j, n, k: (j, 0, n)), (FF_SHARD, tn))

    def dw_in(nm, dact):
        return _matmul(nm, "tn", [(h, dact)],
                       [(BS((tk, 512), lambda j, i, k: (k, i)), BS((None, tk, FF_SHARD), lambda j, i, k: (j, k, 0)))],
                       (N_CHIPS, D_MODEL // 512, T // tk), _sds((N_CHIPS, D_MODEL, FF_SHARD), F32),
                       BS((None, 512, FF_SHARD), lambda j, i, k: (j, i, 0)), (512, FF_SHARD))

    dwg = dw_in(f"ffn{l}_dwg", dg)
    dwu = dw_in(f"ffn{l}_dwu", du)
    act_spec = BS((None, tm, FF_SHARD), lambda i, j, k: (k, i, 0))
    w_spec = BS((None, tn, FF_SHARD), lambda i, j, k: (k, j, 0))
    dh = _matmul(f"ffn{l}_dh", "nt", [(dg, wg), (du, wu)], [(act_spec, w_spec), (act_spec, w_spec)],
                 (T // tm, D_MODEL // tn, N_CHIPS), _sds((T, D_MODEL), F32), BS((tm, tn), lambda i, j, k: (i, j)), (tm, tn))
    dx, dgain = _rmsnorm_bwd(f"ffn{l}_norm_bwd", x, gain, dh, dxo, tm)
    return dx, dgain, dwg, dwu, dwd


_INV_SQRT2 = 1.0 / math.sqrt(2.0)
_INV_SQRT_2PI = 1.0 / math.sqrt(2.0 * math.pi)


def _gelu(x):
    return 0.5 * x * (1.0 + lax.erf(x * _INV_SQRT2))


def _gelu_grad(x):
    return 0.5 * (1.0 + lax.erf(x * _INV_SQRT2)) + x * jnp.exp(-0.5 * x * x) * _INV_SQRT_2PI


def _shift_down(x, k):
    return pltpu.roll(x, k, 0)


def _shift_up(x, k):
    return pltpu.roll(x, x.shape[0] - k, 0)


def _layer_norm_head(xh):
    xc = xh - jnp.mean(xh, axis=-1, keepdims=True)
    rstd = lax.rsqrt(jnp.mean(xc * xc, axis=-1, keepdims=True) + EPS)
    return xc * rstd, rstd


def _even_halo_specs(tm, n_tiles, col_blocks, after):
    rows = tm // SUBLANES
    last = n_tiles * rows - 1
    if after:
        return [BS((SUBLANES, 512), functools.partial(lambda cb, i: (jnp.minimum((i + 1) * rows, last), cb), cb)) for cb in col_blocks]
    return [BS((SUBLANES, 512), functools.partial(lambda cb, i: (jnp.maximum(i * rows - 1, 0), cb), cb)) for cb in col_blocks]


def _even_mixer_fwd(proj, ln_g, w_tril, b_lanes, conv_w, seq, tm):
    T = proj.shape[0]
    tiles_per_seq = seq // tm

    def body(p_ref, hc_ref, hh_ref, lng_ref, w_ref, bb_ref, cw_ref, o_ref):
        first = pl.program_id(0) % tiles_per_seq == 0
        for h in range(SG_HEADS):
            cols = slice(SG_DIM * h, SG_DIM * (h + 1))
            vhat, _ = _layer_norm_head(_gelu(p_ref[:, SG_WIDTH + SG_DIM * h:SG_WIDTH + SG_DIM * (h + 1)]))
            vln = (vhat * lng_ref[:, cols]).astype(BF16)
            for k in range(tm // SG_CHUNK):
                rows = slice(SG_CHUNK * k, SG_CHUNK * (k + 1))
                mixed = _dot(w_ref[h], vln[rows]) + bb_ref[h]
                o_ref[rows, cols] = (_gelu(p_ref[rows, cols]) * mixed).astype(BF16)
        z = p_ref[:, 1536:2048] * p_ref[:, 2048:2560]
        zz = jnp.concatenate([jnp.where(first, 0.0, hc_ref[...] * hh_ref[...]), z], axis=0)
        y = cw_ref[0:1, :] * _shift_down(zz, 2)[SUBLANES:] + cw_ref[1:2, :] * _shift_down(zz, 1)[SUBLANES:] + cw_ref[2:3, :] * z
        o_ref[:, SG_WIDTH:] = (p_ref[:, 1024:1536] * y).astype(BF16)

    full = lambda shape: BS(shape, lambda i: (0,) * len(shape))
    return _call(body, name="even_mixer_fwd", grid=(T // tm,),
                 in_specs=[BS((tm, EVEN_IN), lambda i: (i, 0))] + _even_halo_specs(tm, T // tm, (3, 4), after=False)
                 + [full((1, SG_WIDTH)), full((SG_HEADS, SG_CHUNK, SG_CHUNK)), full((SG_HEADS, SG_CHUNK, SG_DIM)), full((SUBLANES, SC_WIDTH))],
                 out_specs=BS((tm, D_MODEL), lambda i: (i, 0)), out_shape=_sds((T, D_MODEL), BF16))(
        proj, proj, proj, ln_g, w_tril, b_lanes, conv_w)


def _even_mixer_bwd(proj, dmix, ln_g, w_tril, b_lanes, conv_w, seq, tm):
    T = proj.shape[0]
    n_tiles, tiles_per_seq = T // tm, seq // tm

    def body(p_ref, dm_ref, hc_ref, hh_ref, nd_ref, nb_ref, lng_ref, w_ref, bb_ref, cw_ref,
             dp_ref, dw_ref, db_ref, dlng_ref, dcw_ref):
        i = pl.program_id(0)
        first = i % tiles_per_seq == 0
        last = i % tiles_per_seq == tiles_per_seq - 1

        @pl.when(i == 0)
        def _():
            dw_ref[...] = jnp.zeros_like(dw_ref)
            db_ref[...] = jnp.zeros_like(db_ref)
            dlng_ref[...] = jnp.zeros_like(dlng_ref)
            dcw_ref[...] = jnp.zeros_like(dcw_ref)

        for h in range(SG_HEADS):
            cols = slice(SG_DIM * h, SG_DIM * (h + 1))
            vcols = slice(SG_WIDTH + SG_DIM * h, SG_WIDTH + SG_DIM * (h + 1))
            lng = lng_ref[:, cols]
            for k in range(tm // SG_CHUNK):
                rows = slice(SG_CHUNK * k, SG_CHUNK * (k + 1))
                v = p_ref[rows, vcols]
                vhat, rstd = _layer_norm_head(_gelu(v))
                vln = (vhat * lng).astype(BF16)
                mixed = _dot(w_ref[h], vln) + bb_ref[h]
                u = p_ref[rows, cols]
                da = dm_ref[rows, cols]
                dp_ref[rows, cols] = (da * mixed * _gelu_grad(u)).astype(BF16)
                dmixed = da * _gelu(u)
                db_ref[h] += dmixed
                dw_ref[h] += _dot(dmixed, vln, "nt")
                dvln = _dot(w_ref[h], dmixed, "tn")
                dlng_ref[:, cols] += jnp.sum(dvln * vhat, axis=0, keepdims=True)
                dvhat = dvln * lng
                dgv = rstd * (dvhat - jnp.mean(dvhat, axis=-1, keepdims=True)
                              - vhat * jnp.mean(dvhat * vhat, axis=-1, keepdims=True))
                dp_ref[rows, vcols] = (dgv * _gelu_grad(v)).astype(BF16)

        b = p_ref[:, 1024:1536]
        c = p_ref[:, 1536:2048]
        hv = p_ref[:, 2048:2560]
        z = c * hv
        zz = jnp.concatenate([jnp.where(first, 0.0, hc_ref[...] * hh_ref[...]), z], axis=0)
        z1 = _shift_down(zz, 1)[SUBLANES:]
        z2 = _shift_down(zz, 2)[SUBLANES:]
        w0, w1, w2 = cw_ref[0:1, :], cw_ref[1:2, :], cw_ref[2:3, :]
        dbo = dm_ref[:, SG_WIDTH:]
        dy = dbo * b
        dd = jnp.concatenate([dy, jnp.where(last, 0.0, nd_ref[...] * nb_ref[...])], axis=0)
        dz = w2 * dy + w1 * _shift_up(dd, 1)[:tm] + w0 * _shift_up(dd, 2)[:tm]
        dp_ref[:, 1024:1536] = (dbo * (w0 * z2 + w1 * z1 + w2 * z)).astype(BF16)
        dp_ref[:, 1536:2048] = (dz * hv).astype(BF16)
        dp_ref[:, 2048:2560] = (dz * c).astype(BF16)
        dcw_ref[0:1, :] += jnp.sum(dy * z2, axis=0, keepdims=True)
        dcw_ref[1:2, :] += jnp.sum(dy * z1, axis=0, keepdims=True)
        dcw_ref[2:3, :] += jnp.sum(dy * z, axis=0, keepdims=True)

        @pl.when(i == n_tiles - 1)
        def _():
            t_idx = lax.broadcasted_iota(jnp.int32, (SG_CHUNK, SG_CHUNK), 0)
            s_idx = lax.broadcasted_iota(jnp.int32, (SG_CHUNK, SG_CHUNK), 1)
            for h in range(SG_HEADS):
                dw_ref[h] = jnp.where(t_idx >= s_idx, dw_ref[h], 0.0)

    full = lambda shape: BS(shape, lambda i: (0,) * len(shape))
    sq = (SG_HEADS, SG_CHUNK, SG_CHUNK)
    return _call(body, name="even_mixer_bwd", grid=(n_tiles,),
                 in_specs=[BS((tm, EVEN_IN), lambda i: (i, 0)), BS((tm, D_MODEL), lambda i: (i, 0))]
                 + _even_halo_specs(tm, n_tiles, (3, 4), after=False)
                 + _even_halo_specs(tm, n_tiles, (1,), after=True) + _even_halo_specs(tm, n_tiles, (2,), after=True)
                 + [full((1, SG_WIDTH)), full(sq), full(sq), full((SUBLANES, SC_WIDTH))],
                 out_specs=[BS((tm, EVEN_IN), lambda i: (i, 0)), full(sq), full(sq), full((1, SG_WIDTH)), full((SUBLANES, SC_WIDTH))],
                 out_shape=[_sds((T, EVEN_IN), BF16), _sds(sq, F32), _sds(sq, F32), _sds((1, SG_WIDTH), F32), _sds((SUBLANES, SC_WIDTH), F32)])(
        proj, dmix, proj, proj, dmix, proj, ln_g, w_tril, b_lanes, conv_w)


def _pool_select(vals):
    lane = lax.broadcasted_iota(jnp.int32, vals[0].shape, 1)
    out = vals[-1]
    for g in range(len(vals) - 2, -1, -1):
        out = jnp.where(lane < POOL_DIM * (g + 1), vals[g], out)
    return out


def _pool_counts(pos1):
    lane = lax.broadcasted_iota(jnp.int32, (pos1.shape[0], POOL_WIDTH), 1)
    win = _pool_select([jnp.full(lane.shape, float(w), F32) for w in POOL_WINDOWS])
    return jnp.minimum(pos1, win)


def _pool_means(zz, counts):
    s2 = zz + _shift_down(zz, 1)
    s4 = s2 + _shift_down(s2, 2)
    s8 = s4 + _shift_down(s4, 4)
    s16 = s8 + _shift_down(s8, 8)
    return _pool_select([s2, s4, s8, s16])[POOL_HALO:] / counts


def _pool_halo_spec(tm, n_tiles, after):
    rows = tm // POOL_HALO
    if after:
        return BS((POOL_HALO, POOL_WIDTH), lambda i: (jnp.minimum((i + 1) * rows, n_tiles * rows - 1), 0))
    return BS((POOL_HALO, POOL_WIDTH), lambda i: (jnp.maximum(i * rows - 1, 0), 0))


def _pool_fwd(proj, w_diag, scale, seq, tm):
    T = proj.shape[0]
    tiles_per_seq = seq // tm

    def body(z_ref, zh_ref, w_ref, s_ref, o_ref):
        t = pl.program_id(0) % tiles_per_seq
        z = z_ref[...]
        zz = jnp.concatenate([jnp.where(t == 0, 0.0, zh_ref[...]), z], axis=0)
        pos1 = (lax.broadcasted_iota(jnp.int32, (tm, 1), 0) + (t * tm + 1)).astype(F32)
        pooled = _pool_means(zz, _pool_counts(pos1)) - z
        o_ref[...] = (_dot(pooled, w_ref[...]) * s_ref[...]).astype(BF16)

    full = lambda shape: BS(shape, lambda i: (0,) * len(shape))
    return _call(body, name="pool_fwd", grid=(T // tm,),
                 in_specs=[BS((tm, POOL_WIDTH), lambda i: (i, 0)), _pool_halo_spec(tm, T // tm, False),
                           full((POOL_WIDTH, POOL_WIDTH)), full((1, POOL_WIDTH))],
                 out_specs=BS((tm, POOL_WIDTH), lambda i: (i, 0)), out_shape=_sds((T, D_MODEL), BF16))(proj, proj, w_diag, scale)


def _pool_bwd(proj, dmix, w_diag, scale, seq, tm):
    T = proj.shape[0]
    n_tiles, tiles_per_seq = T // tm, seq // tm

    def body(z_ref, zh_ref, do_ref, don_ref, w_ref, s_ref, dz_ref, dw_ref, ds_ref):
        i = pl.program_id(0)
        t = i % tiles_per_seq

        @pl.when(i == 0)
        def _():
            dw_ref[...] = jnp.zeros_like(dw_ref)
            ds_ref[...] = jnp.zeros_like(ds_ref)

        z = z_ref[...]
        zz = jnp.concatenate([jnp.where(t == 0, 0.0, zh_ref[...]), z], axis=0)
        pos1 = (lax.broadcasted_iota(jnp.int32, (tm, 1), 0) + (t * tm + 1)).astype(F32)
        counts = _pool_counts(pos1)
        pooled = _pool_means(zz, counts) - z
        dout = do_ref[...].astype(F32)
        ds_ref[...] += jnp.sum(dout * _dot(pooled, w_ref[...]), axis=0, keepdims=True)
        dlin = dout * s_ref[...]
        dw_ref[...] += _dot(pooled, dlin, "tn")
        dpooled = _dot(dlin, w_ref[...], "nt")
        dpooled_n = _dot(don_ref[...].astype(F32) * s_ref[...], w_ref[...], "nt")
        pos1_n = (lax.broadcasted_iota(jnp.int32, (POOL_HALO, 1), 0) + ((t + 1) * tm + 1)).astype(F32)
        dmean_n = jnp.where(t == tiles_per_seq - 1, 0.0, dpooled_n / _pool_counts(pos1_n))
        dd = jnp.concatenate([dpooled / counts, dmean_n], axis=0)
        r2 = dd + _shift_up(dd, 1)
        r4 = r2 + _shift_up(r2, 2)
        r8 = r4 + _shift_up(r4, 4)
        r16 = r8 + _shift_up(r8, 8)
        dz_ref[...] = (_pool_select([r2, r4, r8, r16])[:tm] - dpooled).astype(BF16)

    full = lambda shape: BS(shape, lambda i: (0,) * len(shape))
    return _call(body, name="pool_bwd", grid=(n_tiles,),
                 in_specs=[BS((tm, POOL_WIDTH), lambda i: (i, 0)), _pool_halo_spec(tm, n_tiles, False),
                           BS((tm, POOL_WIDTH), lambda i: (i, 0)), _pool_halo_spec(tm, n_tiles, True),
                           full((POOL_WIDTH, POOL_WIDTH)), full((1, POOL_WIDTH))],
                 out_specs=[BS((tm, POOL_WIDTH), lambda i: (i, 0)), full((POOL_WIDTH, POOL_WIDTH)), full((1, POOL_WIDTH))],
                 out_shape=[_sds((T, POOL_WIDTH), BF16), _sds((POOL_WIDTH, POOL_WIDTH), F32), _sds((1, POOL_WIDTH), F32)])(
        proj, proj, dmix, dmix, w_diag, scale)


def _rope_partner(r):
    lane = lax.broadcasted_iota(jnp.int32, r.shape, 1)
    return jnp.where(lane < QK_ROPE // 2, pltpu.roll(r, LANES - QK_ROPE // 2, 1), pltpu.roll(r, QK_ROPE // 2, 1))


def _rope(x, cos, sin_signed):
    r = x[:, QK_NOPE:]
    return jnp.concatenate([x[:, :QK_NOPE], r * cos + _rope_partner(r) * sin_signed], axis=1)


def _rope_transposed(dx, cos, sin_signed):
    dr = dx[:, QK_NOPE:]
    return jnp.concatenate([dx[:, :QK_NOPE], dr * cos + _rope_partner(dr * sin_signed)], axis=1)


def _head_norm(x):
    r = lax.rsqrt(jnp.sum(x * x, axis=-1, keepdims=True) * (1.0 / QK_DIM) + EPS)
    return x * r, r


def _head_norm_bwd(dy, xhat, r, gain):
    dxhat = dy * gain
    return r * (dxhat - xhat * (jnp.sum(dxhat * xhat, axis=-1, keepdims=True) * (1.0 / QK_DIM)))


def _latents(p_ref, qag_ref, kvag_ref):
    ql = p_ref[:, POOL_WIDTH:POOL_WIDTH + Q_LORA]
    kvl = p_ref[:, POOL_WIDTH + Q_LORA:POOL_WIDTH + Q_LORA + KV_LORA]
    rq = lax.rsqrt(jnp.mean(ql * ql, axis=-1, keepdims=True) + EPS)
    rkv = lax.rsqrt(jnp.mean(kvl * kvl, axis=-1, keepdims=True) + EPS)
    return ql * rq, rq, kvl * rkv, rkv


def _mla_specs(tm):
    full = lambda shape: BS(shape, lambda i, h: (0,) * len(shape))
    return [BS((tm, ODD_IN_PAD), lambda i, h: (i, 0)), BS((tm, LANES), lambda i, h: (i, 0)), BS((tm, LANES), lambda i, h: (i, 0)),
            full((1, Q_LORA)), full((1, KV_LORA)), BS((None, Q_LORA, QK_PAD), lambda i, h: (h, 0, 0)),
            BS((None, KV_LORA, QK_PAD), lambda i, h: (h, 0, 0)), full((1, QK_PAD)), full((1, QK_PAD))]


def _mla_qkv_fwd(proj, cos, sin_signed, qa_g, kva_g, q_b, kv_b, q_g, k_g, tm):
    T = proj.shape[0]

    def body(p_ref, cos_ref, sin_ref, qag_ref, kvag_ref, qb_ref, kvb_ref, qg_ref, kg_ref, q_ref, k_ref, v_ref, qn_s, kvn_s):
        @pl.when(pl.program_id(1) == 0)
        def _():
            qhat, _, kvhat, _ = _latents(p_ref, qag_ref, kvag_ref)
            qn_s[...] = (qhat * qag_ref[...]).astype(BF16)
            kvn_s[...] = (kvhat * kvag_ref[...]).astype(BF16)

        cos, sin = cos_ref[...], sin_ref[...]
        qhat, _ = _head_norm(_dot(qn_s[...], qb_ref[...]))
        q_ref[...] = _rope(qhat * qg_ref[...], cos, sin).astype(BF16)
        kv = _dot(kvn_s[...], kvb_ref[...])
        khat, _ = _head_norm(jnp.concatenate([kv[:, :QK_NOPE], p_ref[:, ODD_IN_PAD - LANES:]], axis=1))
        k_ref[...] = _rope(khat * kg_ref[...], cos, sin).astype(BF16)
        v_ref[...] = kv[:, QK_NOPE:].astype(BF16)

    qk_spec = BS((None, tm, QK_PAD), lambda i, h: (h, i, 0))
    return _call(body, name="mla_qkv_fwd", grid=(T // tm, HEADS), in_specs=_mla_specs(tm),
                 out_specs=[qk_spec, qk_spec, BS((None, tm, V_DIM), lambda i, h: (h, i, 0))],
                 out_shape=[_sds((HEADS, T, QK_PAD), BF16), _sds((HEADS, T, QK_PAD), BF16), _sds((HEADS, T, V_DIM), BF16)],
                 scratch=[pltpu.VMEM((tm, Q_LORA), BF16), pltpu.VMEM((tm, KV_LORA), BF16)])(
        proj, cos, sin_signed, qa_g, kva_g, q_b, kv_b, q_g, k_g)


def _mla_qkv_bwd(proj, cos, sin_signed, qa_g, kva_g, q_b, kv_b, q_g, k_g, dq, dk, dv, dz_pool, tm):
    T = proj.shape[0]
    n_tiles = T // tm

    def body(p_ref, cos_ref, sin_ref, qag_ref, kvag_ref, qb_ref, kvb_ref, qg_ref, kg_ref, dq_ref, dk_ref, dv_ref, dzp_ref,
             dp_ref, dqb_ref, dkvb_ref, dqg_ref, dkg_ref, dqag_ref, dkvag_ref, qn_s, kvn_s, dqn_s, dkvn_s, dkr_s):
        i, h = pl.program_id(0), pl.program_id(1)

        @pl.when((i == 0) & (h == 0))
        def _():
            for ref in (dqb_ref, dkvb_ref, dqg_ref, dkg_ref, dqag_ref, dkvag_ref):
                ref[...] = jnp.zeros_like(ref)

        @pl.when(h == 0)
        def _():
            qhat, _, kvhat, _ = _latents(p_ref, qag_ref, kvag_ref)
            qn_s[...] = (qhat * qag_ref[...]).astype(BF16)
            kvn_s[...] = (kvhat * kvag_ref[...]).astype(BF16)
            dqn_s[...] = jnp.zeros_like(dqn_s)
            dkvn_s[...] = jnp.zeros_like(dkvn_s)
            dkr_s[...] = jnp.zeros_like(dkr_s)

        cos, sin = cos_ref[...], sin_ref[...]
        qhat, rq = _head_norm(_dot(qn_s[...], qb_ref[...]))
        dqn_head = _rope_transposed(dq_ref[...], cos, sin)
        dqg_ref[...] += jnp.sum(dqn_head * qhat, axis=0, keepdims=True)
        dqh = _head_norm_bwd(dqn_head, qhat, rq, qg_ref[...])
        dqb_ref[h] += _dot(qn_s[...], dqh, "tn")
        dqn_s[...] += _dot(dqh, qb_ref[...], "nt")

        kv = _dot(kvn_s[...], kvb_ref[...])
        khat, rk = _head_norm(jnp.concatenate([kv[:, :QK_NOPE], p_ref[:, ODD_IN_PAD - LANES:]], axis=1))
        dkn_head = _rope_transposed(dk_ref[...], cos, sin)
        dkg_ref[...] += jnp.sum(dkn_head * khat, axis=0, keepdims=True)
        dkf = _head_norm_bwd(dkn_head, khat, rk, kg_ref[...])
        dkr_s[...] += dkf[:, QK_NOPE:]
        dkv = jnp.concatenate([dkf[:, :QK_NOPE], dv_ref[...]], axis=1)
        dkvb_ref[h] += _dot(kvn_s[...], dkv, "tn")
        dkvn_s[...] += _dot(dkv, kvb_ref[...], "nt")

        @pl.when(h == HEADS - 1)
        def _():
            qhat_l, rql, kvhat_l, rkvl = _latents(p_ref, qag_ref, kvag_ref)
            dqn, dkvn = dqn_s[...], dkvn_s[...]
            dqag_ref[...] += jnp.sum(dqn * qhat_l, axis=0, keepdims=True)
            dkvag_ref[...] += jnp.sum(dkvn * kvhat_l, axis=0, keepdims=True)
            dqx, dkvx = dqn * qag_ref[...], dkvn * kvag_ref[...]
            dp_ref[:, :POOL_WIDTH] = dzp_ref[...]
            dp_ref[:, POOL_WIDTH:POOL_WIDTH + Q_LORA] = (
                rql * (dqx - qhat_l * jnp.mean(dqx * qhat_l, axis=-1, keepdims=True))).astype(BF16)
            dp_ref[:, POOL_WIDTH + Q_LORA:ODD_IN_PAD - LANES] = (
                rkvl * (dkvx - kvhat_l * jnp.mean(dkvx * kvhat_l, axis=-1, keepdims=True))).astype(BF16)
            dp_ref[:, ODD_IN_PAD - LANES:] = dkr_s[...].astype(BF16)

    full = lambda shape: BS(shape, lambda i, h: (0,) * len(shape))
    qk_spec = BS((None, tm, QK_PAD), lambda i, h: (h, i, 0))
    return _call(body, name="mla_qkv_bwd", grid=(n_tiles, HEADS),
                 in_specs=_mla_specs(tm) + [qk_spec, qk_spec, BS((None, tm, V_DIM), lambda i, h: (h, i, 0)),
                                            BS((tm, POOL_WIDTH), lambda i, h: (i, 0))],
                 out_specs=[BS((tm, ODD_IN_PAD), lambda i, h: (i, 0)), full((HEADS, Q_LORA, QK_PAD)), full((HEADS, KV_LORA, QK_PAD)),
                            full((1, QK_PAD)), full((1, QK_PAD)), full((1, Q_LORA)), full((1, KV_LORA))],
                 out_shape=[_sds((T, ODD_IN_PAD), BF16), _sds((HEADS, Q_LORA, QK_PAD), F32), _sds((HEADS, KV_LORA, QK_PAD), F32),
                            _sds((1, QK_PAD), F32), _sds((1, QK_PAD), F32), _sds((1, Q_LORA), F32), _sds((1, KV_LORA), F32)],
                 scratch=[pltpu.VMEM((tm, Q_LORA), BF16), pltpu.VMEM((tm, KV_LORA), BF16), pltpu.VMEM((tm, Q_LORA), F32),
                          pltpu.VMEM((tm, KV_LORA), F32), pltpu.VMEM((tm, LANES), F32)])(
        proj, cos, sin_signed, qa_g, kva_g, q_b, kv_b, q_g, k_g, dq, dk, dv, dz_pool)


def _attn_tile(seq):
    return 256 if seq % 256 == 0 else seq


def _causal_mask(s):
    row = lax.broadcasted_iota(jnp.int32, s.shape, 0)
    col = lax.broadcasted_iota(jnp.int32, s.shape, 1)
    return jnp.where(row >= col, s, NEG_INF)


def _rows(i, t):
    return pl.ds(pl.multiple_of(i * t, t), t)


def _flash_fwd(q, k, v, mix, batch, seq):
    t = _attn_tile(seq)
    nq = seq // t

    def body(q_ref, k_ref, v_ref, _, o_ref, lse_ref, m_s, l_s, acc_s):
        qi = pl.program_id(2)
        qv = q_ref[...]
        m_s[...] = jnp.full_like(m_s, NEG_INF)
        l_s[...] = jnp.zeros_like(l_s)
        acc_s[...] = jnp.zeros_like(acc_s)

        def step(kb, masked):
            s = _dot(qv, k_ref[_rows(kb, t), :], "nt") * ATTN_SCALE
            if masked:
                s = _causal_mask(s)
            m_prev = m_s[...]
            m_new = jnp.maximum(m_prev, jnp.max(s, axis=-1, keepdims=True))
            alpha = jnp.exp(m_prev - m_new)
            p = jnp.exp(s - m_new)
            l_s[...] = alpha * l_s[...] + jnp.sum(p, axis=-1, keepdims=True)
            acc_s[...] = alpha * acc_s[...] + _dot(p, v_ref[_rows(kb, t), :])
            m_s[...] = m_new

        def loop_body(kb, carry):
            step(kb, False)
            return carry

        lax.fori_loop(0, qi, loop_body, 0)
        step(qi, True)
        o_ref[...] = (acc_s[...] / l_s[...]).astype(BF16)
        lse_ref[...] = jnp.broadcast_to(m_s[...] + jnp.log(l_s[...]), (t, LANES))

    T = batch * seq
    return _call(body, name="flash_fwd", grid=(batch, HEADS, nq),
                 in_specs=[BS((None, t, QK_PAD), lambda b, h, i: (h, b * nq + i, 0)), BS((None, seq, QK_PAD), lambda b, h, i: (h, b, 0)),
                           BS((None, seq, V_DIM), lambda b, h, i: (h, b, 0)), pl.BlockSpec(memory_space=pl.ANY)],
                 out_specs=[BS((t, V_DIM), lambda b, h, i: (b * nq + i, POOL_WIDTH // V_DIM + h)),
                            BS((None, t, LANES), lambda b, h, i: (h, b * nq + i, 0))],
                 out_shape=[_sds((T, D_MODEL), BF16), _sds((HEADS, T, LANES), F32)],
                 scratch=[pltpu.VMEM((t, 1), F32), pltpu.VMEM((t, 1), F32), pltpu.VMEM((t, V_DIM), F32)],
                 aliases={3: 0})(q, k, v, mix)


def _flash_bwd_dq(q, k, v, dmix, mix, lse, batch, seq):
    t = _attn_tile(seq)
    nq = seq // t

    def body(q_ref, k_ref, v_ref, do_ref, o_ref, lse_ref, dq_ref, delta_ref, acc_s):
        qi = pl.program_id(2)
        qv, do = q_ref[...], do_ref[...]
        delta = jnp.sum(do.astype(F32) * o_ref[...].astype(F32), axis=-1, keepdims=True)
        delta_ref[...] = jnp.broadcast_to(delta, (t, LANES))
        lse_col = lse_ref[:, 0:1]
        acc_s[...] = jnp.zeros_like(acc_s)

        def step(kb, masked):
            kk = k_ref[_rows(kb, t), :]
            s = _dot(qv, kk, "nt") * ATTN_SCALE
            if masked:
                s = _causal_mask(s)
            p = jnp.exp(s - lse_col)
            ds = p * (_dot(do, v_ref[_rows(kb, t), :], "nt") - delta) * ATTN_SCALE
            acc_s[...] += _dot(ds, kk)

        def loop_body(kb, carry):
            step(kb, False)
            return carry

        lax.fori_loop(0, qi, loop_body, 0)
        step(qi, True)
        dq_ref[...] = acc_s[...]

    T = batch * seq
    head_cols = BS((t, V_DIM), lambda b, h, i: (b * nq + i, POOL_WIDTH // V_DIM + h))
    tile = lambda w: BS((None, t, w), lambda b, h, i: (h, b * nq + i, 0))
    return _call(body, name="flash_bwd_dq", grid=(batch, HEADS, nq),
                 in_specs=[tile(QK_PAD), BS((None, seq, QK_PAD), lambda b, h, i: (h, b, 0)), BS((None, seq, V_DIM), lambda b, h, i: (h, b, 0)),
                           head_cols, head_cols, tile(LANES)],
                 out_specs=[tile(QK_PAD), tile(LANES)],
                 out_shape=[_sds((HEADS, T, QK_PAD), F32), _sds((HEADS, T, LANES), F32)],
                 scratch=[pltpu.VMEM((t, QK_PAD), F32)])(q, k, v, dmix, mix, lse)


def _flash_bwd_dkv(q, k, v, dmix, lse, delta, batch, seq):
    t = _attn_tile(seq)
    nq = seq // t

    def body(q_ref, k_ref, v_ref, do_ref, lse_ref, delta_ref, dk_ref, dv_ref, dk_s, dv_s):
        ki = pl.program_id(2)
        kk, vv = k_ref[...], v_ref[...]
        dk_s[...] = jnp.zeros_like(dk_s)
        dv_s[...] = jnp.zeros_like(dv_s)

        def step(qb, masked):
            rows = _rows(qb, t)
            qv, do = q_ref[rows, :], do_ref[rows, :]
            s = _dot(qv, kk, "nt") * ATTN_SCALE
            if masked:
                s = _causal_mask(s)
            p = jnp.exp(s - lse_ref[rows, 0:1])
            dv_s[...] += _dot(p, do, "tn")
            ds = p * (_dot(do, vv, "nt") - delta_ref[rows, 0:1]) * ATTN_SCALE
            dk_s[...] += _dot(ds, qv, "tn")

        def loop_body(qb, carry):
            step(qb, False)
            return carry

        step(ki, True)
        lax.fori_loop(ki + 1, nq, loop_body, 0)
        dk_ref[...] = dk_s[...]
        dv_ref[...] = dv_s[...]

    T = batch * seq
    tile = lambda w: BS((None, t, w), lambda b, h, i: (h, b * nq + i, 0))
    whole = lambda w: BS((None, seq, w), lambda b, h, i: (h, b, 0))
    return _call(body, name="flash_bwd_dkv", grid=(batch, HEADS, nq),
                 in_specs=[whole(QK_PAD), tile(QK_PAD), tile(V_DIM), BS((seq, V_DIM), lambda b, h, i: (b, POOL_WIDTH // V_DIM + h)),
                           whole(LANES), whole(LANES)],
                 out_specs=[tile(QK_PAD), tile(V_DIM)],
                 out_shape=[_sds((HEADS, T, QK_PAD), F32), _sds((HEADS, T, V_DIM), F32)],
                 scratch=[pltpu.VMEM((t, QK_PAD), F32), pltpu.VMEM((t, V_DIM), F32)])(q, k, v, dmix, lse, delta)


def _loss_head(y, target, tm):
    T, d = y.shape

    def body(y_ref, t_ref, dy_ref, sq_ref):
        @pl.when(pl.program_id(0) == 0)
        def _():
            sq_ref[...] = jnp.zeros_like(sq_ref)

        e = y_ref[...] - t_ref[...]
        sq_ref[...] += jnp.sum(e * e)
        dy_ref[...] = e * (1.0 / d)

    row = BS((tm, d), lambda i: (i, 0))
    return _call(body, name="loss_head", grid=(T // tm,), in_specs=[row, row],
                 out_specs=[row, BS((SUBLANES, LANES), lambda i: (0, 0))],
                 out_shape=[_sds((T, d), F32), _sds((SUBLANES, LANES), F32)])(y, target)


def _adamw_math(w, g, m, v):
    m = ADAM_B1 * m + (1.0 - ADAM_B1) * g
    v = ADAM_B2 * v + (1.0 - ADAM_B2) * (g * g)
    m_hat = m / (1.0 - ADAM_B1 ** ADAM_STEP)
    v_hat = v / (1.0 - ADAM_B2 ** ADAM_STEP)
    return -ADAM_LR * (m_hat / (jnp.sqrt(v_hat) + ADAM_EPS) + ADAM_WD * w), m, v


def _adamw(name, w, g, m, v):
    L, R, C = w.shape
    tr = 256 if R % 256 == 0 else R
    outs = None
    for l in range(L):
        def body(w_ref, g_ref, m_ref, v_ref, *rest):
            go_ref, d_ref, mo_ref, vo_ref = rest[-4:]
            gv = g_ref[...]
            d_ref[...], mo_ref[...], vo_ref[...] = _adamw_math(w_ref[...], gv, m_ref[...], v_ref[...])
            go_ref[...] = gv

        layer = BS((None, tr, C), functools.partial(lambda l, i: (l, i, 0), l))
        prev = [] if outs is None else list(outs)
        outs = _call(body, name=f"{name}_{l}", grid=(R // tr,),
                     in_specs=[layer, BS((tr, C), lambda i: (i, 0)), layer, layer] + [pl.BlockSpec(memory_space=pl.ANY)] * len(prev),
                     out_specs=[layer] * 4, out_shape=[_sds((L, R, C), F32)] * 4,
                     aliases={4 + n: n for n in range(len(prev))})(w, g[l], m, v, *prev)
    return outs


def _place():
    x, y, c = lax.axis_index("x"), lax.axis_index("y"), lax.axis_index("c")
    other_chips = [(1 - x, y), (x, 1 - y), (1 - x, 1 - y)]
    return x, y, c, other_chips


def _remote(src, dst, send_sem, recv_sem, dev):
    return pltpu.make_async_remote_copy(src_ref=src, dst_ref=dst, send_sem=send_sem, recv_sem=recv_sem,
                                        device_id=dev, device_id_type=MESH)


def _all_gather_chips(w):
    def body(w_ref, g_ref, send_sems, recv_sems, local_sem):
        x, y, c, chips = _place()
        me, sibling = 2 * x + y, (x, y, 1 - c)
        own = pltpu.make_async_copy(w_ref, g_ref.at[me], local_sem)
        own.start()
        sends = [_remote(w_ref.at[c], g_ref.at[me, c], send_sems.at[k], recv_sems.at[k], (px, py, c))
                 for k, (px, py) in enumerate(chips)]
        for cp in sends:
            cp.start()
        passed = []
        for k, (px, py) in enumerate(chips):
            landed = g_ref.at[2 * px + py, c]
            _remote(w_ref.at[c], landed, send_sems.at[k], recv_sems.at[k], (px, py, c)).wait_recv()
            passed.append(_remote(landed, landed, send_sems.at[3 + k], recv_sems.at[3 + k], sibling))
            passed[-1].start()
        for k, (px, py) in enumerate(chips):
            theirs = g_ref.at[2 * px + py, 1 - c]
            _remote(theirs, theirs, send_sems.at[3 + k], recv_sems.at[3 + k], sibling).wait_recv()
        for cp in sends + passed:
            cp.wait_send()
        own.wait()

    return _call(body, name="all_gather_weights", in_specs=[HBM], out_specs=HBM, out_shape=_sds((N_CHIPS,) + w.shape, w.dtype),
                 scratch=[pltpu.SemaphoreType.DMA((6,)), pltpu.SemaphoreType.DMA((6,)), pltpu.SemaphoreType.DMA])(w)


def _sibling_exchange(gs):
    n = len(gs)

    def body(*refs):
        g, r, send_sems, recv_sems = refs[:n], refs[n:2 * n], refs[-2], refs[-1]
        x, y, c, _ = _place()
        copies = [_remote(g[i].at[:, 1 - c], r[i], send_sems.at[i], recv_sems.at[i], (x, y, 1 - c)) for i in range(n)]
        for cp in copies:
            cp.start()
        for cp in copies:
            cp.wait()

    return _call(body, name="grad_sibling_exchange", in_specs=[HBM] * n, out_specs=[HBM] * n,
                 out_shape=[_sds((a.shape[0],) + a.shape[2:], a.dtype) for a in gs],
                 scratch=[pltpu.SemaphoreType.DMA((n,)), pltpu.SemaphoreType.DMA((n,))])(*gs)


def _chip_scatter(ps):
    n = len(ps)

    def body(*refs):
        p, r, send_sems, recv_sems, local_sems = refs[:n], refs[n:2 * n], refs[-3], refs[-2], refs[-1]
        x, y, c, chips = _place()
        me = 2 * x + y
        local = [pltpu.make_async_copy(p[i].at[me], r[i].at[me], local_sems.at[i]) for i in range(n)]
        for cp in local:
            cp.start()
        sends = [_remote(p[i].at[2 * px + py], r[i].at[me], send_sems.at[3 * i + k], recv_sems.at[3 * i + k], (px, py, c))
                 for i in range(n) for k, (px, py) in enumerate(chips)]
        for cp in sends:
            cp.start()
        for i in range(n):
            for k, (px, py) in enumerate(chips):
                theirs = r[i].at[2 * px + py]
                _remote(theirs, theirs, send_sems.at[3 * i + k], recv_sems.at[3 * i + k], (px, py, c)).wait_recv()
        for cp in sends:
            cp.wait_send()
        for cp in local:
            cp.wait()

    return _call(body, name="grad_chip_scatter", in_specs=[HBM] * n, out_specs=[HBM] * n,
                 out_shape=[_sds(a.shape, a.dtype) for a in ps],
                 scratch=[pltpu.SemaphoreType.DMA((3 * n,)), pltpu.SemaphoreType.DMA((3 * n,)), pltpu.SemaphoreType.DMA((n,))])(*ps)


def _sibling_share(ss):
    n = len(ss)

    def body(*refs):
        s, f, send_sems, recv_sems, local_sems = refs[:n], refs[n:2 * n], refs[-3], refs[-2], refs[-1]
        x, y, c, _ = _place()
        local = [pltpu.make_async_copy(s[i], f[i].at[c], local_sems.at[i]) for i in range(n)]
        sends = [_remote(s[i], f[i].at[c], send_sems.at[i], recv_sems.at[i], (x, y, 1 - c)) for i in range(n)]
        for cp in local + sends:
            cp.start()
        for i in range(n):
            theirs = f[i].at[1 - c]
            _remote(theirs, theirs, send_sems.at[i], recv_sems.at[i], (x, y, 1 - c)).wait_recv()
        for cp in sends:
            cp.wait_send()
        for cp in local:
            cp.wait()

    return _call(body, name="grad_sibling_share", in_specs=[HBM] * n, out_specs=[HBM] * n,
                 out_shape=[_sds((2,) + a.shape, a.dtype) for a in ss],
                 scratch=[pltpu.SemaphoreType.DMA((n,)), pltpu.SemaphoreType.DMA((n,)), pltpu.SemaphoreType.DMA((n,))])(*ss)


def _all_reduce_small(name, v):
    n_dev = 8
    flips = [(fx, fy, fc) for fx in (0, 1) for fy in (0, 1) for fc in (0, 1)][1:]

    def body(v_ref, o_ref, buf, send_sems, recv_sems):
        x, y, c, _ = _place()
        peers = [(1 - x if fx else x, 1 - y if fy else y, 1 - c if fc else c) for fx, fy, fc in flips]
        me = 4 * x + 2 * y + c
        buf[me] = v_ref[...]
        sends = [_remote(v_ref, buf.at[me], send_sems.at[k], recv_sems.at[k], peer) for k, peer in enumerate(peers)]
        for cp in sends:
            cp.start()
        for k, (px, py, pc) in enumerate(peers):
            theirs = buf.at[4 * px + 2 * py + pc]
            _remote(v_ref, theirs, send_sems.at[k], recv_sems.at[k], (px, py, pc)).wait_recv()
        for cp in sends:
            cp.wait_send()
        acc = buf[0]
        for d in range(1, n_dev):
            acc = acc + buf[d]
        o_ref[...] = acc

    return _call(body, name=name, in_specs=[VMEM], out_specs=VMEM, out_shape=_sds(v.shape, F32),
                 scratch=[pltpu.VMEM((n_dev,) + v.shape, F32), pltpu.SemaphoreType.DMA((7,)), pltpu.SemaphoreType.DMA((7,))])(v)


def _add_halves(name, g, r, c):
    _, _, rows, C = g.shape
    tr = 256 if rows % 256 == 0 else rows

    def body(c_ref, g_ref, r_ref, o_ref):
        o_ref[...] = (g_ref[...] + r_ref[...]).astype(BF16)

    spec = BS((None, tr, C), lambda j, i, c_ref: (j, i, 0))
    grid_spec = pltpu.PrefetchScalarGridSpec(
        num_scalar_prefetch=1, grid=(N_CHIPS, rows // tr),
        in_specs=[BS((None, None, tr, C), lambda j, i, c_ref: (j, c_ref[0], i, 0)), spec], out_specs=spec)
    return pl.pallas_call(body, name=name, grid_spec=grid_spec, out_shape=_sds((N_CHIPS, rows, C), BF16),
                          compiler_params=pltpu.CompilerParams(vmem_limit_bytes=VMEM_LIMIT_V7X,
                                                               dimension_semantics=("arbitrary", "arbitrary")))(c, g, r)


def _sum_chips(name, r):
    _, rows, C = r.shape
    tr = 256 if rows % 256 == 0 else rows

    def body(r_ref, o_ref):
        acc = r_ref[0].astype(F32)
        for j in range(1, N_CHIPS):
            acc = acc + r_ref[j].astype(F32)
        o_ref[...] = acc

    return _call(body, name=name, grid=(rows // tr,), in_specs=[BS((N_CHIPS, tr, C), lambda i: (0, i, 0))],
                 out_specs=BS((tr, C), lambda i: (i, 0)), out_shape=_sds((rows, C), F32))(r)


_SHARDED = ("even_w_in", "even_w_out", "odd_w_in", "q_b", "kv_b", "odd_w_out", "ffn_w_gate", "ffn_w_up", "ffn_w_down")
_COLUMN_SHARDED = ("even_w_in", "q_b", "kv_b")
_REPLICATED = ("mix_norm", "ffn_norm", "sg_ln_g", "sg_w_s", "sg_b_s", "pool_w", "q_norm", "k_norm")
_SMALL_SHARDED = ("sc_conv_w", "pool_scale", "q_a_norm", "kv_a_norm")
_WEIGHTS = ("mix_norm", "ffn_norm", "even_w_in", "sg_ln_g", "sg_w_s", "sg_b_s", "sc_conv_w", "even_w_out", "odd_w_in", "pool_w",
            "pool_scale", "q_a_norm", "q_b", "kv_a_norm", "kv_b", "q_norm", "k_norm", "odd_w_out", "ffn_w_gate", "ffn_w_up",
            "ffn_w_down")
_GATHER_ROW = 1024
_GATHER_ROWS_ALIGN = 2 * 16


def _pad_rows(flat, width, align):
    n = flat.shape[0]
    rows = -(-n // (width * align)) * align
    return jnp.pad(flat, (0, rows * width - n)).reshape(rows, width)


def _gather_weights(shards):
    flat = jnp.concatenate([shards[n].astype(BF16).reshape(-1) for n in _SHARDED])
    packed = _pad_rows(flat, _GATHER_ROW, _GATHER_ROWS_ALIGN)
    gathered = _all_gather_chips(packed.reshape(2, packed.shape[0] // 2, _GATHER_ROW)).reshape(N_CHIPS, -1)
    out, off = {}, 0
    for n in _SHARDED:
        shp = shards[n].shape[1:] if shards[n].shape[0] == 1 else shards[n].shape
        size = math.prod(shp)
        out[n] = gathered[:, off:off + size].reshape((N_CHIPS,) + tuple(shp))
        off += size
    for n in _COLUMN_SHARDED:
        out[n] = out[n].transpose(1, 0, 2).reshape(out[n].shape[1], -1)
    for n in ("even_w_out", "odd_w_in", "odd_w_out"):
        out[n] = out[n].reshape(-1, out[n].shape[-1])
    return out


def _forward_backward(x, positions, target, W, small):
    batch, seq, _ = x.shape
    T = batch * seq
    tm = _token_tile(seq)
    x0 = x.reshape(T, D_MODEL)

    inv_freq = ROPE_THETA ** (-jnp.arange(0, QK_ROPE, 2, dtype=F32) / QK_ROPE)
    ang = (positions.astype(F32)[..., None] * inv_freq).reshape(T, QK_ROPE // 2)
    cos, sin = jnp.cos(ang), jnp.sin(ang)
    pad = jnp.zeros((T, LANES - QK_ROPE), F32)
    cos_t = jnp.concatenate([cos, cos, pad], axis=1)
    sin_t = jnp.concatenate([-sin, sin, pad], axis=1)

    tril = jnp.tril(jnp.ones((SG_CHUNK, SG_CHUNK), bool))
    w_tril = jnp.where(tril[None], small["sg_w_s"][0], 0.0).astype(BF16)
    b_lanes = jnp.broadcast_to(small["sg_b_s"][0][:, :, None], (SG_HEADS, SG_CHUNK, SG_DIM))
    conv_w = jnp.pad(small["sc_conv_w"][0], ((0, SUBLANES - CONV_TAPS), (0, 0)))
    ln_g = small["sg_ln_g"]
    pool_diag = jnp.zeros((POOL_WIDTH, POOL_WIDTH), F32)
    for g in range(len(POOL_WINDOWS)):
        pool_diag = pool_diag.at[POOL_DIM * g:POOL_DIM * (g + 1), POOL_DIM * g:POOL_DIM * (g + 1)].set(small["pool_w"][0, g])
    pool_diag = pool_diag.astype(BF16)
    pool_scale = small["pool_scale"]
    w_in_odd = jnp.pad(W["odd_w_in"], ((0, 0), (0, ODD_IN_PAD - ODD_IN)))
    q_b = jnp.pad(W["q_b"].reshape(Q_LORA, HEADS, QK_DIM).transpose(1, 0, 2), ((0, 0), (0, 0), (0, QK_PAD - QK_DIM)))
    kv_b = W["kv_b"].reshape(KV_LORA, HEADS, QK_NOPE + V_DIM).transpose(1, 0, 2)
    q_g = jnp.pad(small["q_norm"], ((0, 0), (0, QK_PAD - QK_DIM)))
    k_g = jnp.pad(small["k_norm"], ((0, 0), (0, QK_PAD - QK_DIM)))
    qa_g, kva_g = small["q_a_norm"], small["kv_a_norm"]
    ffn = [(small["ffn_norm"][l], W["ffn_w_gate"][:, l], W["ffn_w_up"][:, l], W["ffn_w_down"][:, l]) for l in range(2)]

    h0 = _rmsnorm_fwd("mix0_norm", x0, small["mix_norm"][0], tm)
    proj0 = _mm("even_in", "nn", h0, W["even_w_in"], F32, tn=640, tk=1024)
    mix0 = _even_mixer_fwd(proj0, ln_g, w_tril, b_lanes, conv_w, seq, tm)
    x1 = _mm("even_out", "nn", mix0, W["even_w_out"], F32, tk=1024, add=x0)
    x2, ffn0_saved = _ffn_fwd(0, x1, *ffn[0], tm)
    h2 = _rmsnorm_fwd("mix1_norm", x2, small["mix_norm"][1], tm)
    proj1 = _mm("odd_in", "nn", h2, w_in_odd, F32, tk=1024)
    mix1 = _pool_fwd(proj1, pool_diag, pool_scale, seq, tm)
    q, k, v = _mla_qkv_fwd(proj1, cos_t, sin_t, qa_g, kva_g, q_b, kv_b, q_g, k_g, tm)
    mix1, lse = _flash_fwd(q, k, v, mix1, batch, seq)
    x3 = _mm("odd_out", "nn", mix1, W["odd_w_out"], F32, tk=1024, add=x2)
    x4, ffn1_saved = _ffn_fwd(1, x3, *ffn[1], tm)
    dy, sq = _loss_head(x4, target.reshape(T, D_MODEL), tm)

    G = {}
    dx3, dffn_g1, dwg1, dwu1, dwd1 = _ffn_bwd(1, x3, *ffn[1], ffn1_saved, dy, tm)
    dmix1 = _mm("odd_out_dx", "nt", dx3, W["odd_w_out"], BF16, tk=1024)
    G["odd_w_out"] = _mm("odd_out_dw", "tn", mix1, dx3, F32)
    dq, delta = _flash_bwd_dq(q, k, v, dmix1, mix1, lse, batch, seq)
    dk, dv = _flash_bwd_dkv(q, k, v, dmix1, lse, delta, batch, seq)
    dz_pool, dpool_diag, G["pool_scale"] = _pool_bwd(proj1, dmix1, pool_diag, pool_scale, seq, tm)
    dproj1, dq_b, dkv_b, dq_g, dk_g, G["q_a_norm"], G["kv_a_norm"] = _mla_qkv_bwd(
        proj1, cos_t, sin_t, qa_g, kva_g, q_b, kv_b, q_g, k_g, dq, dk, dv, dz_pool, tm)
    G["pool_w"] = jnp.stack([dpool_diag[POOL_DIM * g:POOL_DIM * (g + 1), POOL_DIM * g:POOL_DIM * (g + 1)]
                             for g in range(len(POOL_WINDOWS))])[None]
    G["q_b"] = dq_b[:, :, :QK_DIM].transpose(1, 0, 2).reshape(Q_LORA, HEADS * QK_DIM)
    G["kv_b"] = dkv_b.transpose(1, 0, 2).reshape(KV_LORA, HEADS * (QK_NOPE + V_DIM))
    G["q_norm"], G["k_norm"] = dq_g[:, :QK_DIM], dk_g[:, :QK_DIM]
    dh2 = _mm("odd_in_dx", "nt", dproj1, w_in_odd, F32, tk=1024)
    G["odd_w_in"] = _mm("odd_in_dw", "tn", h2, dproj1, F32)[:, :ODD_IN]
    dx2, dmix_g1 = _rmsnorm_bwd("mix1_norm_bwd", x2, small["mix_norm"][1], dh2, dx3, tm)
    dx1, dffn_g0, dwg0, dwu0, dwd0 = _ffn_bwd(0, x1, *ffn[0], ffn0_saved, dx2, tm)
    dmix0 = _mm("even_out_dx", "nt", dx1, W["even_w_out"], F32, tk=1024)
    G["even_w_out"] = _mm("even_out_dw", "tn", mix0, dx1, F32)
    dproj0, dw_s, db_lanes, G["sg_ln_g"], dconv = _even_mixer_bwd(proj0, dmix0, ln_g, w_tril, b_lanes, conv_w, seq, tm)
    G["sg_w_s"] = dw_s[None]
    G["sg_b_s"] = jnp.sum(db_lanes, axis=-1)[None]
    G["sc_conv_w"] = dconv[None, :CONV_TAPS]
    dh0 = _mm("even_in_dx", "nt", dproj0, W["even_w_in"], F32, tk=640)
    G["even_w_in"] = _mm("even_in_dw", "tn", h0, dproj0, F32, tn=640)
    dx0, dmix_g0 = _rmsnorm_bwd("mix0_norm_bwd", x0, small["mix_norm"][0], dh0, dx1, tm)
    G["mix_norm"] = jnp.concatenate([dmix_g0, dmix_g1], axis=0)
    G["ffn_norm"] = jnp.concatenate([dffn_g0, dffn_g1], axis=0)
    G["ffn"] = [(dwg0, dwu0, dwd0), (dwg1, dwu1, dwd1)]
    return sq[0, 0], dx0.reshape(batch, seq, D_MODEL), G


def _small_vector(parts, names):
    flat = jnp.concatenate([parts[n].astype(F32).reshape(-1) for n in names])
    return _pad_rows(flat, LANES, SUBLANES)


def _split_small(vec, like, names):
    out, off, flat = {}, 0, vec.reshape(-1)
    for n in names:
        size = math.prod(like[n].shape)
        out[n] = flat[off:off + size].reshape(like[n].shape)
        off += size
    return out


def _whole_shape(a):
    return a.shape[:-1] + (a.shape[-1] * N_CHIPS,)


def kernel(x, positions, mix_norm, ffn_norm, even_w_in, sg_ln_g, sg_w_s, sg_b_s, sc_conv_w, even_w_out, odd_w_in, pool_w, pool_scale, q_a_norm, q_b, kv_a_norm, kv_b, q_norm, k_norm, odd_w_out, ffn_w_gate, ffn_w_up, ffn_w_down, loss_target, m_mix_norm, m_ffn_norm, m_even_w_in, m_sg_ln_g, m_sg_w_s, m_sg_b_s, m_sc_conv_w, m_even_w_out, m_odd_w_in, m_pool_w, m_pool_scale, m_q_a_norm, m_q_b, m_kv_a_norm, m_kv_b, m_q_norm, m_k_norm, m_odd_w_out, m_ffn_w_gate, m_ffn_w_up, m_ffn_w_down, v_mix_norm, v_ffn_norm, v_even_w_in, v_sg_ln_g, v_sg_w_s, v_sg_b_s, v_sc_conv_w, v_even_w_out, v_odd_w_in, v_pool_w, v_pool_scale, v_q_a_norm, v_q_b, v_kv_a_norm, v_kv_b, v_q_norm, v_k_norm, v_odd_w_out, v_ffn_w_gate, v_ffn_w_up, v_ffn_w_down):
    args = dict(locals())
    w = {n: args[n] for n in _WEIGHTS}
    m = {n: args["m_" + n] for n in _WEIGHTS}
    v = {n: args["v_" + n] for n in _WEIGHTS}
    cx, cy, cc = lax.axis_index("x"), lax.axis_index("y"), lax.axis_index("c")
    chip = 2 * cx + cy

    W = _gather_weights(w)
    placed = {}
    for n in _SMALL_SHARDED:
        a = w[n]
        whole = jnp.zeros(a.shape[:-1] + (N_CHIPS, a.shape[-1]), F32)
        whole = lax.dynamic_update_slice_in_dim(whole, a[..., None, :], chip, axis=a.ndim - 1)
        placed[n] = jnp.where(cc == 0, whole, 0.0).reshape(_whole_shape(a))
    small = dict({n: w[n] for n in _REPLICATED},
                 **_split_small(_all_reduce_small("gather_small_weights", _small_vector(placed, _SMALL_SHARDED)), placed, _SMALL_SHARDED))

    sq, grad_x, G = _forward_backward(x, positions, loss_target, W, small)
    loss = lax.psum(0.5 * sq / D_MODEL, ("x", "y", "c"))

    small_names = _REPLICATED + _SMALL_SHARDED
    summed = _split_small(_all_reduce_small("reduce_small_grads", _small_vector(G, small_names)), G, small_names)
    grads = {n: summed[n] for n in _REPLICATED}
    for n in _SMALL_SHARDED:
        a = w[n]
        grads[n] = lax.dynamic_slice_in_dim(summed[n].reshape(a.shape[:-1] + (N_CHIPS, a.shape[-1])), chip, 1,
                                            axis=a.ndim - 1).reshape(a.shape)

    def shard_major(g, cols):
        return g.reshape(g.shape[0], N_CHIPS, cols).transpose(1, 0, 2)

    big = [("even_w_in", shard_major(G["even_w_in"], EVEN_IN // N_CHIPS)),
           ("even_w_out", G["even_w_out"].reshape(N_CHIPS, -1, D_MODEL)),
           ("odd_w_in", G["odd_w_in"].reshape(N_CHIPS, -1, ODD_IN)),
           ("q_b", shard_major(G["q_b"], HEADS * QK_DIM // N_CHIPS)),
           ("kv_b", shard_major(G["kv_b"], HEADS * (QK_NOPE + V_DIM) // N_CHIPS)),
           ("odd_w_out", G["odd_w_out"].reshape(N_CHIPS, -1, D_MODEL))]
    for l in range(2):
        big += [(f"ffn_w_gate{l}", G["ffn"][l][0]), (f"ffn_w_up{l}", G["ffn"][l][1]), (f"ffn_w_down{l}", G["ffn"][l][2])]
    names = [n for n, _ in big]
    halves = [g.reshape(N_CHIPS, 2, g.shape[1] // 2, g.shape[2]) for _, g in big]
    from_sibling = _sibling_exchange(halves)
    c_arr = cc.astype(jnp.int32).reshape(1)
    partial = [_add_halves(f"add_{n}", g, r, c_arr) for n, g, r in zip(names, halves, from_sibling)]
    scattered = _chip_scatter(partial)
    sums = [_sum_chips(f"sum_{n}", r) for n, r in zip(names, scattered)]
    shard_grad = {n: f.reshape(1, -1, f.shape[-1]) for n, f in zip(names, _sibling_share(sums))}

    out = {}
    for n in ("even_w_in", "even_w_out", "odd_w_in", "q_b", "kv_b", "odd_w_out"):
        out[n] = _adamw(f"adamw_{n}", w[n], [shard_grad[n][0]], m[n], v[n])
    for n in ("ffn_w_gate", "ffn_w_up", "ffn_w_down"):
        out[n] = _adamw(f"adamw_{n}", w[n], [shard_grad[f"{n}{l}"][0] for l in range(2)], m[n], v[n])
    packed = [_small_vector(d, small_names) for d in (w, grads, m, v)]
    res = _adamw("adamw_small", packed[0][None], [packed[1]], packed[2][None], packed[3][None])
    delta_s, m_s, v_s = (_split_small(r, w, small_names) for r in res[1:])
    for n in small_names:
        out[n] = (grads[n], delta_s[n], m_s[n], v_s[n])

    return (loss, grad_x, *[out[n][0] for n in _WEIGHTS], *[out[n][1] for n in _WEIGHTS],
            *[out[n][2] for n in _WEIGHTS], *[out[n][3] for n in _WEIGHTS])
```

```python
import functools
import math

import jax
import jax.numpy as jnp
from jax import lax
from jax.experimental import pallas as pl
from jax.experimental.pallas import tpu as pltpu

F32, BF16 = jnp.float32, jnp.bfloat16
BS = pl.BlockSpec

D_MODEL = 1024
EPS = 1e-6
NEG_INF = -1e30
SG_HEADS, SG_DIM, SG_WIDTH, SG_CHUNK = 4, 128, 512, 128
SC_WIDTH, CONV_TAPS = 512, 3
EVEN_IN = 2 * SG_WIDTH + 3 * SC_WIDTH
POOL_WINDOWS = (2, 4, 8, 16)
POOL_DIM, POOL_WIDTH = 64, 256
POOL_HALO = 16
HEADS, Q_LORA, KV_LORA, QK_NOPE, QK_ROPE, V_DIM = 6, 384, 256, 128, 64, 128
QK_DIM = QK_NOPE + QK_ROPE
QK_PAD = 256
ODD_IN = POOL_WIDTH + Q_LORA + KV_LORA + QK_ROPE
ODD_IN_PAD = 1024
ROPE_THETA = 10000.0
ATTN_SCALE = QK_DIM ** -0.5
D_FF, N_CHIPS = 2816, 4
FF_SHARD = D_FF // N_CHIPS
ADAM_LR, ADAM_B1, ADAM_B2, ADAM_EPS, ADAM_WD, ADAM_STEP = 0.001, 0.9, 0.999, 1e-08, 0.01, 10
VMEM_LIMIT_V7X = 48 * 2**20
LANES, SUBLANES = 128, 8
MESH = pl.DeviceIdType.MESH
HBM = pl.BlockSpec(memory_space=pltpu.HBM)
VMEM = pl.BlockSpec(memory_space=pltpu.VMEM)

_DIMS = {"nn": (((1,), (0,)), ((), ())), "nt": (((1,), (1,)), ((), ())), "tn": (((0,), (0,)), ((), ()))}


def _dot(a, b, mode="nn"):
    return lax.dot_general(a.astype(BF16), b.astype(BF16), _DIMS[mode], preferred_element_type=F32)


def _call(body, *, name, out_shape, in_specs, out_specs, grid=(), scratch=(), aliases=None):
    params = pltpu.CompilerParams(vmem_limit_bytes=VMEM_LIMIT_V7X,
                                  **({"dimension_semantics": ("arbitrary",) * len(grid)} if grid else {}))
    return pl.pallas_call(body, name=name, grid=grid, in_specs=in_specs, out_specs=out_specs, out_shape=out_shape,
                          scratch_shapes=list(scratch), input_output_aliases=aliases or {}, compiler_params=params)


def _sds(shape, dtype):
    return jax.ShapeDtypeStruct(tuple(shape), dtype)


def _token_tile(seq):
    return 512 if seq % 512 == 0 else seq


def _matmul(name, mode, pairs, pair_specs, grid, out_shape, out_spec, acc_shape, add=None, add_spec=None):
    n, nk = len(pairs), grid[-1]

    def body(*refs):
        ab = refs[:2 * n]
        add_ref = refs[2 * n] if add is not None else None
        o_ref, acc = refs[-2], refs[-1]
        k = pl.program_id(len(grid) - 1)

        @pl.when(k == 0)
        def _():
            acc[...] = jnp.zeros_like(acc)

        for p in range(n):
            acc[...] += _dot(ab[2 * p][...], ab[2 * p + 1][...], mode)

        @pl.when(k == nk - 1)
        def _():
            r = acc[...]
            if add_ref is not None:
                r = r + add_ref[...]
            o_ref[...] = r.astype(o_ref.dtype)

    ops = [t for pr in pairs for t in pr] + ([add] if add is not None else [])
    specs = [s for pr in pair_specs for s in pr] + ([add_spec] if add is not None else [])
    return _call(body, name=name, grid=grid, in_specs=specs, out_specs=out_spec, out_shape=out_shape,
                 scratch=[pltpu.VMEM(acc_shape, F32)])(*ops)


def _mm(name, mode, a, b, out_dtype, tm=512, tn=512, tk=512, add=None):
    if mode == "tn":
        (K, M), N = a.shape, b.shape[1]
    else:
        (M, K), N = a.shape, (b.shape[1] if mode == "nn" else b.shape[0])
    tm, tn, tk = min(tm, M), min(tn, N), min(tk, K)
    a_spec = BS((tk, tm), lambda i, j, k: (k, i)) if mode == "tn" else BS((tm, tk), lambda i, j, k: (i, k))
    b_spec = BS((tn, tk), lambda i, j, k: (j, k)) if mode == "nt" else BS((tk, tn), lambda i, j, k: (k, j))
    o_spec = BS((tm, tn), lambda i, j, k: (i, j))
    return _matmul(name, mode, [(a, b)], [(a_spec, b_spec)], (M // tm, N // tn, K // tk), _sds((M, N), out_dtype),
                   o_spec, (tm, tn), add=add, add_spec=o_spec if add is not None else None)


def _rmsnorm_fwd(name, x, g, tm):
    T, d = x.shape

    def body(x_ref, g_ref, o_ref):
        xv = x_ref[...]
        y = xv * lax.rsqrt(jnp.mean(xv * xv, axis=-1, keepdims=True) + EPS)
        o_ref[...] = (y * g_ref[...]).astype(o_ref.dtype)

    return _call(body, name=name, grid=(T // tm,), in_specs=[BS((tm, d), lambda i: (i, 0)), BS((1, d), lambda i: (0, 0))],
                 out_specs=BS((tm, d), lambda i: (i, 0)), out_shape=_sds((T, d), BF16))(x, g.reshape(1, d))


def _rmsnorm_bwd(name, x, g, dh, dres, tm):
    T, d = x.shape

    def body(x_ref, g_ref, dh_ref, dres_ref, dx_ref, dg_ref):
        xv = x_ref[...]
        r = lax.rsqrt(jnp.mean(xv * xv, axis=-1, keepdims=True) + EPS)
        xhat = xv * r
        dhv = dh_ref[...]

        @pl.when(pl.program_id(0) == 0)
        def _():
            dg_ref[...] = jnp.zeros_like(dg_ref)

        dg_ref[...] += jnp.sum(dhv * xhat, axis=0, keepdims=True)
        dxhat = dhv * g_ref[...]
        dx_ref[...] = dres_ref[...] + r * (dxhat - xhat * jnp.mean(dxhat * xhat, axis=-1, keepdims=True))

    row = BS((tm, d), lambda i: (i, 0))
    vec = BS((1, d), lambda i: (0, 0))
    return _call(body, name=name, grid=(T // tm,), in_specs=[row, vec, row, row], out_specs=[row, vec],
                 out_shape=[_sds((T, d), F32), _sds((1, d), F32)])(x, g.reshape(1, d), dh, dres)


def _ffn_up(name, h, wg, wu, l, tm):
    T = h.shape[0]

    def body(h_ref, wg_ref, wu_ref, g_ref, u_ref, a_ref):
        hv = h_ref[...]
        g = _dot(hv, wg_ref[...])
        u = _dot(hv, wu_ref[...])
        g_ref[...] = g
        u_ref[...] = u
        a_ref[...] = (g * (1.0 / (1.0 + jnp.exp(-g))) * u).astype(BF16)

    w_spec = BS((None, None, D_MODEL, FF_SHARD), lambda i, j: (j, l, 0, 0))
    o_spec = BS((None, tm, FF_SHARD), lambda i, j: (j, i, 0))
    sh = (N_CHIPS, T, FF_SHARD)
    return _call(body, name=name, grid=(T // tm, N_CHIPS), in_specs=[BS((tm, D_MODEL), lambda i, j: (i, 0)), w_spec, w_spec],
                 out_specs=[o_spec, o_spec, o_spec], out_shape=[_sds(sh, F32), _sds(sh, F32), _sds(sh, BF16)])(h, wg, wu)


def _ffn_act_bwd(name, dxo, wd, l, g, u, tm):
    T = dxo.shape[0]

    def body(dx_ref, wd_ref, g_ref, u_ref, dg_ref, du_ref):
        da = _dot(dx_ref[...], wd_ref[...], "nt")
        g = g_ref[...]
        sig = 1.0 / (1.0 + jnp.exp(-g))
        dg_ref[...] = (da * u_ref[...] * (sig * (1.0 + g * (1.0 - sig)))).astype(BF16)
        du_ref[...] = (da * (g * sig)).astype(BF16)

    t_spec = BS((None, tm, FF_SHARD), lambda i, j: (j, i, 0))
    sh = _sds((N_CHIPS, T, FF_SHARD), BF16)
    return _call(body, name=name, grid=(T // tm, N_CHIPS),
                 in_specs=[BS((tm, D_MODEL), lambda i, j: (i, 0)), BS((None, None, FF_SHARD, D_MODEL), lambda i, j: (j, l, 0, 0)), t_spec, t_spec],
                 out_specs=[t_spec, t_spec], out_shape=[sh, sh])(dxo, wd, g, u)


def _ffn_fwd(l, x, gain, wg, wu, wd, tm):
    T = x.shape[0]
    h = _rmsnorm_fwd(f"ffn{l}_norm", x, gain, tm)
    g, u, a = _ffn_up(f"ffn{l}_up", h, wg, wu, l, tm)
    tn = 512
    out = _matmul(f"ffn{l}_down", "nn", [(a, wd)],
                  [(BS((None, tm, FF_SHARD), lambda i, j, k: (k, i, 0)), BS((None, None, FF_SHARD, tn), lambda i, j, k: (k, l, 0, j)))],
                  (T // tm, D_MODEL // tn, N_CHIPS), _sds((T, D_MODEL), F32), BS((tm, tn), lambda i, j, k: (i, j)), (tm, tn),
                  add=x, add_spec=BS((tm, tn), lambda i, j, k: (i, j)))
    return out, (h, g, u, a)


def _ffn_bwd(l, x, gain, wg, wu, wd, saved, dxo, tm):
    h, g, u, a = saved
    T = x.shape[0]
    dg, du = _ffn_act_bwd(f"ffn{l}_act_bwd", dxo, wd, l, g, u, tm)
    tk = min(512, T)
    tn = 512
    dwd = _matmul(f"ffn{l}_dwd", "tn", [(a, dxo)],
                  [(BS((None, tk, FF_SHARD), lambda j, n, k: (j, k, 0)), BS((tk, tn), lambda j, n, k: (k, n)))],
                  (N_CHIPS, D_MODEL // tn, T // tk), _sds((N_CHIPS, FF_SHARD, D_MODEL), F32),
                  BS((None, FF_SHARD, tn), lambda j, n, k: (j, 0, n)), (FF_SHARD, tn))

    def dw_in(nm, dact):
        return _matmul(nm, "tn", [(h, dact)],
                       [(BS((tk, 512), lambda j, i, k: (k, i)), BS((None, tk, FF_SHARD), lambda j, i, k: (j, k, 0)))],
                       (N_CHIPS, D_MODEL // 512, T // tk), _sds((N_CHIPS, D_MODEL, FF_SHARD), F32),
                       BS((None, 512, FF_SHARD), lambda j, i, k: (j, i, 0)), (512, FF_SHARD))

    dwg = dw_in(f"ffn{l}_dwg", dg)
    dwu = dw_in(f"ffn{l}_dwu", du)
    act_spec = BS((None, tm, FF_SHARD), lambda i, j, k: (k, i, 0))
    w_spec = BS((None, None, tn, FF_SHARD), lambda i, j, k: (k, l, j, 0))
    dh = _matmul(f"ffn{l}_dh", "nt", [(dg, wg), (du, wu)], [(act_spec, w_spec), (act_spec, w_spec)],
                 (T // tm, D_MODEL // tn, N_CHIPS), _sds((T, D_MODEL), F32), BS((tm, tn), lambda i, j, k: (i, j)), (tm, tn))
    dx, dgain = _rmsnorm_bwd(f"ffn{l}_norm_bwd", x, gain, dh, dxo, tm)
    return dx, dgain, dwg, dwu, dwd


_INV_SQRT2 = 1.0 / math.sqrt(2.0)
_INV_SQRT_2PI = 1.0 / math.sqrt(2.0 * math.pi)


def _gelu(x):
    return 0.5 * x * (1.0 + lax.erf(x * _INV_SQRT2))


def _gelu_grad(x):
    return 0.5 * (1.0 + lax.erf(x * _INV_SQRT2)) + x * jnp.exp(-0.5 * x * x) * _INV_SQRT_2PI


def _shift_down(x, k):
    return pltpu.roll(x, k, 0)


def _shift_up(x, k):
    return pltpu.roll(x, x.shape[0] - k, 0)


def _layer_norm_head(xh):
    xc = xh - jnp.mean(xh, axis=-1, keepdims=True)
    rstd = lax.rsqrt(jnp.mean(xc * xc, axis=-1, keepdims=True) + EPS)
    return xc * rstd, rstd


def _even_halo_specs(tm, n_tiles, col_blocks, after):
    rows = tm // SUBLANES
    last = n_tiles * rows - 1
    if after:
        return [BS((SUBLANES, 512), functools.partial(lambda cb, i: (jnp.minimum((i + 1) * rows, last), cb), cb)) for cb in col_blocks]
    return [BS((SUBLANES, 512), functools.partial(lambda cb, i: (jnp.maximum(i * rows - 1, 0), cb), cb)) for cb in col_blocks]


def _even_mixer_fwd(proj, ln_g, w_tril, b_lanes, conv_w, seq, tm):
    T = proj.shape[0]
    tiles_per_seq = seq // tm

    def body(p_ref, hc_ref, hh_ref, lng_ref, w_ref, bb_ref, cw_ref, o_ref):
        first = pl.program_id(0) % tiles_per_seq == 0
        for h in range(SG_HEADS):
            cols = slice(SG_DIM * h, SG_DIM * (h + 1))
            vhat, _ = _layer_norm_head(_gelu(p_ref[:, SG_WIDTH + SG_DIM * h:SG_WIDTH + SG_DIM * (h + 1)]))
            vln = (vhat * lng_ref[:, cols]).astype(BF16)
            for k in range(tm // SG_CHUNK):
                rows = slice(SG_CHUNK * k, SG_CHUNK * (k + 1))
                mixed = _dot(w_ref[h], vln[rows]) + bb_ref[h]
                o_ref[rows, cols] = (_gelu(p_ref[rows, cols]) * mixed).astype(BF16)
        z = p_ref[:, 1536:2048] * p_ref[:, 2048:2560]
        zz = jnp.concatenate([jnp.where(first, 0.0, hc_ref[...] * hh_ref[...]), z], axis=0)
        y = cw_ref[0:1, :] * _shift_down(zz, 2)[SUBLANES:] + cw_ref[1:2, :] * _shift_down(zz, 1)[SUBLANES:] + cw_ref[2:3, :] * z
        o_ref[:, SG_WIDTH:] = (p_ref[:, 1024:1536] * y).astype(BF16)

    full = lambda shape: BS(shape, lambda i: (0,) * len(shape))
    return _call(body, name="even_mixer_fwd", grid=(T // tm,),
                 in_specs=[BS((tm, EVEN_IN), lambda i: (i, 0))] + _even_halo_specs(tm, T // tm, (3, 4), after=False)
                 + [full((1, SG_WIDTH)), full((SG_HEADS, SG_CHUNK, SG_CHUNK)), full((SG_HEADS, SG_CHUNK, SG_DIM)), full((SUBLANES, SC_WIDTH))],
                 out_specs=BS((tm, D_MODEL), lambda i: (i, 0)), out_shape=_sds((T, D_MODEL), BF16))(
        proj, proj, proj, ln_g, w_tril, b_lanes, conv_w)


def _even_mixer_bwd(proj, dmix, ln_g, w_tril, b_lanes, conv_w, seq, tm):
    T = proj.shape[0]
    n_tiles, tiles_per_seq = T // tm, seq // tm

    def body(p_ref, dm_ref, hc_ref, hh_ref, nd_ref, nb_ref, lng_ref, w_ref, bb_ref, cw_ref,
             dp_ref, dw_ref, db_ref, dlng_ref, dcw_ref):
        i = pl.program_id(0)
        first = i % tiles_per_seq == 0
        last = i % tiles_per_seq == tiles_per_seq - 1

        @pl.when(i == 0)
        def _():
            dw_ref[...] = jnp.zeros_like(dw_ref)
            db_ref[...] = jnp.zeros_like(db_ref)
            dlng_ref[...] = jnp.zeros_like(dlng_ref)
            dcw_ref[...] = jnp.zeros_like(dcw_ref)

        for h in range(SG_HEADS):
            cols = slice(SG_DIM * h, SG_DIM * (h + 1))
            vcols = slice(SG_WIDTH + SG_DIM * h, SG_WIDTH + SG_DIM * (h + 1))
            lng = lng_ref[:, cols]
            for k in range(tm // SG_CHUNK):
                rows = slice(SG_CHUNK * k, SG_CHUNK * (k + 1))
                v = p_ref[rows, vcols]
                vhat, rstd = _layer_norm_head(_gelu(v))
                vln = (vhat * lng).astype(BF16)
                mixed = _dot(w_ref[h], vln) + bb_ref[h]
                u = p_ref[rows, cols]
                da = dm_ref[rows, cols]
                dp_ref[rows, cols] = (da * mixed * _gelu_grad(u)).astype(BF16)
                dmixed = da * _gelu(u)
                db_ref[h] += dmixed
                dw_ref[h] += _dot(dmixed, vln, "nt")
                dvln = _dot(w_ref[h], dmixed, "tn")
                dlng_ref[:, cols] += jnp.sum(dvln * vhat, axis=0, keepdims=True)
                dvhat = dvln * lng
                dgv = rstd * (dvhat - jnp.mean(dvhat, axis=-1, keepdims=True)
                              - vhat * jnp.mean(dvhat * vhat, axis=-1, keepdims=True))
                dp_ref[rows, vcols] = (dgv * _gelu_grad(v)).astype(BF16)

        b = p_ref[:, 1024:1536]
        c = p_ref[:, 1536:2048]
        hv = p_ref[:, 2048:2560]
        z = c * hv
        zz = jnp.concatenate([jnp.where(first, 0.0, hc_ref[...] * hh_ref[...]), z], axis=0)
        z1 = _shift_down(zz, 1)[SUBLANES:]
        z2 = _shift_down(zz, 2)[SUBLANES:]
        w0, w1, w2 = cw_ref[0:1, :], cw_ref[1:2, :], cw_ref[2:3, :]
        dbo = dm_ref[:, SG_WIDTH:]
        dy = dbo * b
        dd = jnp.concatenate([dy, jnp.where(last, 0.0, nd_ref[...] * nb_ref[...])], axis=0)
        dz = w2 * dy + w1 * _shift_up(dd, 1)[:tm] + w0 * _shift_up(dd, 2)[:tm]
        dp_ref[:, 1024:1536] = (dbo * (w0 * z2 + w1 * z1 + w2 * z)).astype(BF16)
        dp_ref[:, 1536:2048] = (dz * hv).astype(BF16)
        dp_ref[:, 2048:2560] = (dz * c).astype(BF16)
        dcw_ref[0:1, :] += jnp.sum(dy * z2, axis=0, keepdims=True)
        dcw_ref[1:2, :] += jnp.sum(dy * z1, axis=0, keepdims=True)
        dcw_ref[2:3, :] += jnp.sum(dy * z, axis=0, keepdims=True)

        @pl.when(i == n_tiles - 1)
        def _():
            t_idx = lax.broadcasted_iota(jnp.int32, (SG_CHUNK, SG_CHUNK), 0)
            s_idx = lax.broadcasted_iota(jnp.int32, (SG_CHUNK, SG_CHUNK), 1)
            for h in range(SG_HEADS):
                dw_ref[h] = jnp.where(t_idx >= s_idx, dw_ref[h], 0.0)

    full = lambda shape: BS(shape, lambda i: (0,) * len(shape))
    sq = (SG_HEADS, SG_CHUNK, SG_CHUNK)
    return _call(body, name="even_mixer_bwd", grid=(n_tiles,),
                 in_specs=[BS((tm, EVEN_IN), lambda i: (i, 0)), BS((tm, D_MODEL), lambda i: (i, 0))]
                 + _even_halo_specs(tm, n_tiles, (3, 4), after=False)
                 + _even_halo_specs(tm, n_tiles, (1,), after=True) + _even_halo_specs(tm, n_tiles, (2,), after=True)
                 + [full((1, SG_WIDTH)), full(sq), full(sq), full((SUBLANES, SC_WIDTH))],
                 out_specs=[BS((tm, EVEN_IN), lambda i: (i, 0)), full(sq), full(sq), full((1, SG_WIDTH)), full((SUBLANES, SC_WIDTH))],
                 out_shape=[_sds((T, EVEN_IN), BF16), _sds(sq, F32), _sds(sq, F32), _sds((1, SG_WIDTH), F32), _sds((SUBLANES, SC_WIDTH), F32)])(
        proj, dmix, proj, proj, dmix, proj, ln_g, w_tril, b_lanes, conv_w)


def _pool_select(vals):
    lane = lax.broadcasted_iota(jnp.int32, vals[0].shape, 1)
    out = vals[-1]
    for g in range(len(vals) - 2, -1, -1):
        out = jnp.where(lane < POOL_DIM * (g + 1), vals[g], out)
    return out


def _pool_counts(pos1):
    lane = lax.broadcasted_iota(jnp.int32, (pos1.shape[0], POOL_WIDTH), 1)
    win = _pool_select([jnp.full(lane.shape, float(w), F32) for w in POOL_WINDOWS])
    return jnp.minimum(pos1, win)


def _pool_means(zz, counts):
    s2 = zz + _shift_down(zz, 1)
    s4 = s2 + _shift_down(s2, 2)
    s8 = s4 + _shift_down(s4, 4)
    s16 = s8 + _shift_down(s8, 8)
    return _pool_select([s2, s4, s8, s16])[POOL_HALO:] / counts


def _pool_halo_spec(tm, n_tiles, after):
    rows = tm // POOL_HALO
    if after:
        return BS((POOL_HALO, POOL_WIDTH), lambda i: (jnp.minimum((i + 1) * rows, n_tiles * rows - 1), 0))
    return BS((POOL_HALO, POOL_WIDTH), lambda i: (jnp.maximum(i * rows - 1, 0), 0))


def _pool_fwd(proj, w_diag, scale, seq, tm):
    T = proj.shape[0]
    tiles_per_seq = seq // tm

    def body(z_ref, zh_ref, w_ref, s_ref, o_ref):
        t = pl.program_id(0) % tiles_per_seq
        z = z_ref[...]
        zz = jnp.concatenate([jnp.where(t == 0, 0.0, zh_ref[...]), z], axis=0)
        pos1 = (lax.broadcasted_iota(jnp.int32, (tm, 1), 0) + (t * tm + 1)).astype(F32)
        pooled = _pool_means(zz, _pool_counts(pos1)) - z
        o_ref[...] = (_dot(pooled, w_ref[...]) * s_ref[...]).astype(BF16)

    full = lambda shape: BS(shape, lambda i: (0,) * len(shape))
    return _call(body, name="pool_fwd", grid=(T // tm,),
                 in_specs=[BS((tm, POOL_WIDTH), lambda i: (i, 0)), _pool_halo_spec(tm, T // tm, False),
                           full((POOL_WIDTH, POOL_WIDTH)), full((1, POOL_WIDTH))],
                 out_specs=BS((tm, POOL_WIDTH), lambda i: (i, 0)), out_shape=_sds((T, D_MODEL), BF16))(proj, proj, w_diag, scale)


def _pool_bwd(proj, dmix, w_diag, scale, seq, tm):
    T = proj.shape[0]
    n_tiles, tiles_per_seq = T // tm, seq // tm

    def body(z_ref, zh_ref, do_ref, don_ref, w_ref, s_ref, dz_ref, dw_ref, ds_ref):
        i = pl.program_id(0)
        t = i % tiles_per_seq

        @pl.when(i == 0)
        def _():
            dw_ref[...] = jnp.zeros_like(dw_ref)
            ds_ref[...] = jnp.zeros_like(ds_ref)

        z = z_ref[...]
        zz = jnp.concatenate([jnp.where(t == 0, 0.0, zh_ref[...]), z], axis=0)
        pos1 = (lax.broadcasted_iota(jnp.int32, (tm, 1), 0) + (t * tm + 1)).astype(F32)
        counts = _pool_counts(pos1)
        pooled = _pool_means(zz, counts) - z
        dout = do_ref[...].astype(F32)
        ds_ref[...] += jnp.sum(dout * _dot(pooled, w_ref[...]), axis=0, keepdims=True)
        dlin = dout * s_ref[...]
        dw_ref[...] += _dot(pooled, dlin, "tn")
        dpooled = _dot(dlin, w_ref[...], "nt")
        dpooled_n = _dot(don_ref[...].astype(F32) * s_ref[...], w_ref[...], "nt")
        pos1_n = (lax.broadcasted_iota(jnp.int32, (POOL_HALO, 1), 0) + ((t + 1) * tm + 1)).astype(F32)
        dmean_n = jnp.where(t == tiles_per_seq - 1, 0.0, dpooled_n / _pool_counts(pos1_n))
        dd = jnp.concatenate([dpooled / counts, dmean_n], axis=0)
        r2 = dd + _shift_up(dd, 1)
        r4 = r2 + _shift_up(r2, 2)
        r8 = r4 + _shift_up(r4, 4)
        r16 = r8 + _shift_up(r8, 8)
        dz_ref[...] = (_pool_select([r2, r4, r8, r16])[:tm] - dpooled).astype(BF16)

    full = lambda shape: BS(shape, lambda i: (0,) * len(shape))
    return _call(body, name="pool_bwd", grid=(n_tiles,),
                 in_specs=[BS((tm, POOL_WIDTH), lambda i: (i, 0)), _pool_halo_spec(tm, n_tiles, False),
                           BS((tm, POOL_WIDTH), lambda i: (i, 0)), _pool_halo_spec(tm, n_tiles, True),
                           full((POOL_WIDTH, POOL_WIDTH)), full((1, POOL_WIDTH))],
                 out_specs=[BS((tm, POOL_WIDTH), lambda i: (i, 0)), full((POOL_WIDTH, POOL_WIDTH)), full((1, POOL_WIDTH))],
                 out_shape=[_sds((T, POOL_WIDTH), BF16), _sds((POOL_WIDTH, POOL_WIDTH), F32), _sds((1, POOL_WIDTH), F32)])(
        proj, proj, dmix, dmix, w_diag, scale)


def _rope_partner(r):
    lane = lax.broadcasted_iota(jnp.int32, r.shape, 1)
    return jnp.where(lane < QK_ROPE // 2, pltpu.roll(r, LANES - QK_ROPE // 2, 1), pltpu.roll(r, QK_ROPE // 2, 1))


def _rope(x, cos, sin_signed):
    r = x[:, QK_NOPE:]
    return jnp.concatenate([x[:, :QK_NOPE], r * cos + _rope_partner(r) * sin_signed], axis=1)


def _rope_transposed(dx, cos, sin_signed):
    dr = dx[:, QK_NOPE:]
    return jnp.concatenate([dx[:, :QK_NOPE], dr * cos + _rope_partner(dr * sin_signed)], axis=1)


def _head_norm(x):
    r = lax.rsqrt(jnp.sum(x * x, axis=-1, keepdims=True) * (1.0 / QK_DIM) + EPS)
    return x * r, r


def _head_norm_bwd(dy, xhat, r, gain):
    dxhat = dy * gain
    return r * (dxhat - xhat * (jnp.sum(dxhat * xhat, axis=-1, keepdims=True) * (1.0 / QK_DIM)))


def _latents(p_ref, qag_ref, kvag_ref):
    ql = p_ref[:, POOL_WIDTH:POOL_WIDTH + Q_LORA]
    kvl = p_ref[:, POOL_WIDTH + Q_LORA:POOL_WIDTH + Q_LORA + KV_LORA]
    rq = lax.rsqrt(jnp.mean(ql * ql, axis=-1, keepdims=True) + EPS)
    rkv = lax.rsqrt(jnp.mean(kvl * kvl, axis=-1, keepdims=True) + EPS)
    return ql * rq, rq, kvl * rkv, rkv


def _mla_specs(tm):
    full = lambda shape: BS(shape, lambda i, h: (0,) * len(shape))
    return [BS((tm, ODD_IN_PAD), lambda i, h: (i, 0)), BS((tm, LANES), lambda i, h: (i, 0)), BS((tm, LANES), lambda i, h: (i, 0)),
            full((1, Q_LORA)), full((1, KV_LORA)), BS((None, Q_LORA, QK_PAD), lambda i, h: (h, 0, 0)),
            BS((None, KV_LORA, QK_PAD), lambda i, h: (h, 0, 0)), full((1, QK_PAD)), full((1, QK_PAD))]


def _mla_qkv_fwd(proj, cos, sin_signed, qa_g, kva_g, q_b, kv_b, q_g, k_g, tm):
    T = proj.shape[0]

    def body(p_ref, cos_ref, sin_ref, qag_ref, kvag_ref, qb_ref, kvb_ref, qg_ref, kg_ref, q_ref, k_ref, v_ref, qn_s, kvn_s):
        @pl.when(pl.program_id(1) == 0)
        def _():
            qhat, _, kvhat, _ = _latents(p_ref, qag_ref, kvag_ref)
            qn_s[...] = (qhat * qag_ref[...]).astype(BF16)
            kvn_s[...] = (kvhat * kvag_ref[...]).astype(BF16)

        cos, sin = cos_ref[...], sin_ref[...]
        qhat, _ = _head_norm(_dot(qn_s[...], qb_ref[...]))
        q_ref[...] = _rope(qhat * qg_ref[...], cos, sin).astype(BF16)
        kv = _dot(kvn_s[...], kvb_ref[...])
        khat, _ = _head_norm(jnp.concatenate([kv[:, :QK_NOPE], p_ref[:, ODD_IN_PAD - LANES:]], axis=1))
        k_ref[...] = _rope(khat * kg_ref[...], cos, sin).astype(BF16)
        v_ref[...] = kv[:, QK_NOPE:].astype(BF16)

    qk_spec = BS((None, tm, QK_PAD), lambda i, h: (h, i, 0))
    return _call(body, name="mla_qkv_fwd", grid=(T // tm, HEADS), in_specs=_mla_specs(tm),
                 out_specs=[qk_spec, qk_spec, BS((None, tm, V_DIM), lambda i, h: (h, i, 0))],
                 out_shape=[_sds((HEADS, T, QK_PAD), BF16), _sds((HEADS, T, QK_PAD), BF16), _sds((HEADS, T, V_DIM), BF16)],
                 scratch=[pltpu.VMEM((tm, Q_LORA), BF16), pltpu.VMEM((tm, KV_LORA), BF16)])(
        proj, cos, sin_signed, qa_g, kva_g, q_b, kv_b, q_g, k_g)


def _mla_qkv_bwd(proj, cos, sin_signed, qa_g, kva_g, q_b, kv_b, q_g, k_g, dq, dk, dv, dz_pool, tm):
    T = proj.shape[0]
    n_tiles = T // tm

    def body(p_ref, cos_ref, sin_ref, qag_ref, kvag_ref, qb_ref, kvb_ref, qg_ref, kg_ref, dq_ref, dk_ref, dv_ref, dzp_ref,
             dp_ref, dqb_ref, dkvb_ref, dqg_ref, dkg_ref, dqag_ref, dkvag_ref, qn_s, kvn_s, dqn_s, dkvn_s, dkr_s):
        i, h = pl.program_id(0), pl.program_id(1)

        @pl.when((i == 0) & (h == 0))
        def _():
            for ref in (dqb_ref, dkvb_ref, dqg_ref, dkg_ref, dqag_ref, dkvag_ref):
                ref[...] = jnp.zeros_like(ref)

        @pl.when(h == 0)
        def _():
            qhat, _, kvhat, _ = _latents(p_ref, qag_ref, kvag_ref)
            qn_s[...] = (qhat * qag_ref[...]).astype(BF16)
            kvn_s[...] = (kvhat * kvag_ref[...]).astype(BF16)
            dqn_s[...] = jnp.zeros_like(dqn_s)
            dkvn_s[...] = jnp.zeros_like(dkvn_s)
            dkr_s[...] = jnp.zeros_like(dkr_s)

        cos, sin = cos_ref[...], sin_ref[...]
        qhat, rq = _head_norm(_dot(qn_s[...], qb_ref[...]))
        dqn_head = _rope_transposed(dq_ref[...], cos, sin)
        dqg_ref[...] += jnp.sum(dqn_head * qhat, axis=0, keepdims=True)
        dqh = _head_norm_bwd(dqn_head, qhat, rq, qg_ref[...])
        dqb_ref[h] += _dot(qn_s[...], dqh, "tn")
        dqn_s[...] += _dot(dqh, qb_ref[...], "nt")

        kv = _dot(kvn_s[...], kvb_ref[...])
        khat, rk = _head_norm(jnp.concatenate([kv[:, :QK_NOPE], p_ref[:, ODD_IN_PAD - LANES:]], axis=1))
        dkn_head = _rope_transposed(dk_ref[...], cos, sin)
        dkg_ref[...] += jnp.sum(dkn_head * khat, axis=0, keepdims=True)
        dkf = _head_norm_bwd(dkn_head, khat, rk, kg_ref[...])
        dkr_s[...] += dkf[:, QK_NOPE:]
        dkv = jnp.concatenate([dkf[:, :QK_NOPE], dv_ref[...]], axis=1)
        dkvb_ref[h] += _dot(kvn_s[...], dkv, "tn")
        dkvn_s[...] += _dot(dkv, kvb_ref[...], "nt")

        @pl.when(h == HEADS - 1)
        def _():
            qhat_l, rql, kvhat_l, rkvl = _latents(p_ref, qag_ref, kvag_ref)
            dqn, dkvn = dqn_s[...], dkvn_s[...]
            dqag_ref[...] += jnp.sum(dqn * qhat_l, axis=0, keepdims=True)
            dkvag_ref[...] += jnp.sum(dkvn * kvhat_l, axis=0, keepdims=True)
            dqx, dkvx = dqn * qag_ref[...], dkvn * kvag_ref[...]
            dp_ref[:, :POOL_WIDTH] = dzp_ref[...]
            dp_ref[:, POOL_WIDTH:POOL_WIDTH + Q_LORA] = (
                rql * (dqx - qhat_l * jnp.mean(dqx * qhat_l, axis=-1, keepdims=True))).astype(BF16)
            dp_ref[:, POOL_WIDTH + Q_LORA:ODD_IN_PAD - LANES] = (
                rkvl * (dkvx - kvhat_l * jnp.mean(dkvx * kvhat_l, axis=-1, keepdims=True))).astype(BF16)
            dp_ref[:, ODD_IN_PAD - LANES:] = dkr_s[:, :QK_ROPE].astype(BF16)

    full = lambda shape: BS(shape, lambda i, h: (0,) * len(shape))
    qk_spec = BS((None, tm, QK_PAD), lambda i, h: (h, i, 0))
    return _call(body, name="mla_qkv_bwd", grid=(n_tiles, HEADS),
                 in_specs=_mla_specs(tm) + [qk_spec, qk_spec, BS((None, tm, V_DIM), lambda i, h: (h, i, 0)),
                                            BS((tm, POOL_WIDTH), lambda i, h: (i, 0))],
                 out_specs=[BS((tm, ODD_IN), lambda i, h: (i, 0)), full((HEADS, Q_LORA, QK_PAD)), full((HEADS, KV_LORA, QK_PAD)),
                            full((1, QK_PAD)), full((1, QK_PAD)), full((1, Q_LORA)), full((1, KV_LORA))],
                 out_shape=[_sds((T, ODD_IN), BF16),_sds((HEADS, Q_LORA, QK_PAD), F32), _sds((HEADS, KV_LORA, QK_PAD), F32),
                            _sds((1, QK_PAD), F32), _sds((1, QK_PAD), F32), _sds((1, Q_LORA), F32), _sds((1, KV_LORA), F32)],
                 scratch=[pltpu.VMEM((tm, Q_LORA), BF16), pltpu.VMEM((tm, KV_LORA), BF16), pltpu.VMEM((tm, Q_LORA), F32),
                          pltpu.VMEM((tm, KV_LORA), F32), pltpu.VMEM((tm, LANES), F32)])(
        proj, cos, sin_signed, qa_g, kva_g, q_b, kv_b, q_g, k_g, dq, dk, dv, dz_pool)


def _attn_tile(seq):
    return 512 if seq % 512 == 0 else seq


def _causal_mask(s):
    row = lax.broadcasted_iota(jnp.int32, s.shape, 0)
    col = lax.broadcasted_iota(jnp.int32, s.shape, 1)
    return jnp.where(row >= col, s, NEG_INF)


def _rows(i, t):
    return pl.ds(pl.multiple_of(i * t, t), t)


def _flash_fwd(q, k, v, mix, batch, seq):
    t = _attn_tile(seq)
    nq = seq // t

    def body(q_ref, k_ref, v_ref, _, o_ref, lse_ref, m_s, l_s, acc_s):
        qi = pl.program_id(2)
        qv = q_ref[...]
        m_s[...] = jnp.full_like(m_s, NEG_INF)
        l_s[...] = jnp.zeros_like(l_s)
        acc_s[...] = jnp.zeros_like(acc_s)

        def step(kb, masked):
            s = _dot(qv, k_ref[_rows(kb, t), :], "nt") * ATTN_SCALE
            if masked:
                s = _causal_mask(s)
            m_prev = m_s[...]
            m_new = jnp.maximum(m_prev, jnp.max(s, axis=-1, keepdims=True))
            alpha = jnp.exp(m_prev - m_new)
            p = jnp.exp(s - m_new)
            l_s[...] = alpha * l_s[...] + jnp.sum(p, axis=-1, keepdims=True)
            acc_s[...] = alpha * acc_s[...] + _dot(p, v_ref[_rows(kb, t), :])
            m_s[...] = m_new

        def loop_body(kb, carry):
            step(kb, False)
            return carry

        lax.fori_loop(0, qi, loop_body, 0)
        step(qi, True)
        o_ref[...] = (acc_s[...] / l_s[...]).astype(BF16)
        lse_ref[...] = jnp.broadcast_to(m_s[...] + jnp.log(l_s[...]), (t, LANES))

    T = batch * seq
    return _call(body, name="flash_fwd", grid=(batch, HEADS, nq),
                 in_specs=[BS((None, t, QK_PAD), lambda b, h, i: (h, b * nq + i, 0)), BS((None, seq, QK_PAD), lambda b, h, i: (h, b, 0)),
                           BS((None, seq, V_DIM), lambda b, h, i: (h, b, 0)), pl.BlockSpec(memory_space=pl.ANY)],
                 out_specs=[BS((t, V_DIM), lambda b, h, i: (b * nq + i, POOL_WIDTH // V_DIM + h)),
                            BS((None, t, LANES), lambda b, h, i: (h, b * nq + i, 0))],
                 out_shape=[_sds((T, D_MODEL), BF16), _sds((HEADS, T, LANES), F32)],
                 scratch=[pltpu.VMEM((t, 1), F32), pltpu.VMEM((t, 1), F32), pltpu.VMEM((t, V_DIM), F32)],
                 aliases={3: 0})(q, k, v, mix)


def _flash_bwd_dq(q, k, v, dmix, mix, lse, batch, seq):
    t = _attn_tile(seq)
    nq = seq // t

    def body(q_ref, k_ref, v_ref, do_ref, o_ref, lse_ref, dq_ref, delta_ref, acc_s):
        qi = pl.program_id(2)
        qv, do = q_ref[...], do_ref[...]
        delta = jnp.sum(do.astype(F32) * o_ref[...].astype(F32), axis=-1, keepdims=True)
        delta_ref[...] = jnp.broadcast_to(delta, (t, LANES))
        lse_col = lse_ref[:, 0:1]
        acc_s[...] = jnp.zeros_like(acc_s)

        def step(kb, masked):
            kk = k_ref[_rows(kb, t), :]
            s = _dot(qv, kk, "nt") * ATTN_SCALE
            if masked:
                s = _causal_mask(s)
            p = jnp.exp(s - lse_col)
            ds = p * (_dot(do, v_ref[_rows(kb, t), :], "nt") - delta) * ATTN_SCALE
            acc_s[...] += _dot(ds, kk)

        def loop_body(kb, carry):
            step(kb, False)
            return carry

        lax.fori_loop(0, qi, loop_body, 0)
        step(qi, True)
        dq_ref[...] = acc_s[...]

    T = batch * seq
    head_cols = BS((t, V_DIM), lambda b, h, i: (b * nq + i, POOL_WIDTH // V_DIM + h))
    tile = lambda w: BS((None, t, w), lambda b, h, i: (h, b * nq + i, 0))
    return _call(body, name="flash_bwd_dq", grid=(batch, HEADS, nq),
                 in_specs=[tile(QK_PAD), BS((None, seq, QK_PAD), lambda b, h, i: (h, b, 0)), BS((None, seq, V_DIM), lambda b, h, i: (h, b, 0)),
                           head_cols, head_cols, tile(LANES)],
                 out_specs=[tile(QK_PAD), tile(LANES)],
                 out_shape=[_sds((HEADS, T, QK_PAD), F32), _sds((HEADS, T, LANES), F32)],
                 scratch=[pltpu.VMEM((t, QK_PAD), F32)])(q, k, v, dmix, mix, lse)


def _flash_bwd_dkv(q, k, v, dmix, lse, delta, batch, seq):
    t = _attn_tile(seq)
    nq = seq // t

    def body(q_ref, k_ref, v_ref, do_ref, lse_ref, delta_ref, dk_ref, dv_ref, dk_s, dv_s):
        ki = pl.program_id(2)
        kk, vv = k_ref[...], v_ref[...]
        dk_s[...] = jnp.zeros_like(dk_s)
        dv_s[...] = jnp.zeros_like(dv_s)

        def step(qb, masked):
            rows = _rows(qb, t)
            qv, do = q_ref[rows, :], do_ref[rows, :]
            s = _dot(qv, kk, "nt") * ATTN_SCALE
            if masked:
                s = _causal_mask(s)
            p = jnp.exp(s - lse_ref[rows, 0:1])
            dv_s[...] += _dot(p, do, "tn")
            ds = p * (_dot(do, vv, "nt") - delta_ref[rows, 0:1]) * ATTN_SCALE
            dk_s[...] += _dot(ds, qv, "tn")

        def loop_body(qb, carry):
            step(qb, False)
            return carry

        step(ki, True)
        lax.fori_loop(ki + 1, nq, loop_body, 0)
        dk_ref[...] = dk_s[...]
        dv_ref[...] = dv_s[...]

    T = batch * seq
    tile = lambda w: BS((None, t, w), lambda b, h, i: (h, b * nq + i, 0))
    whole = lambda w: BS((None, seq, w), lambda b, h, i: (h, b, 0))
    return _call(body, name="flash_bwd_dkv", grid=(batch, HEADS, nq),
                 in_specs=[whole(QK_PAD), tile(QK_PAD), tile(V_DIM), BS((seq, V_DIM), lambda b, h, i: (b, POOL_WIDTH // V_DIM + h)),
                           whole(LANES), whole(LANES)],
                 out_specs=[tile(QK_PAD), tile(V_DIM)],
                 out_shape=[_sds((HEADS, T, QK_PAD), F32), _sds((HEADS, T, V_DIM), F32)],
                 scratch=[pltpu.VMEM((t, QK_PAD), F32), pltpu.VMEM((t, V_DIM), F32)])(q, k, v, dmix, lse, delta)


def _loss_head(y, target, tm):
    T, d = y.shape

    def body(y_ref, t_ref, dy_ref, sq_ref):
        @pl.when(pl.program_id(0) == 0)
        def _():
            sq_ref[...] = jnp.zeros_like(sq_ref)

        e = y_ref[...] - t_ref[...]
        sq_ref[...] += jnp.sum(e * e)
        dy_ref[...] = e * (1.0 / d)

    row = BS((tm, d), lambda i: (i, 0))
    return _call(body, name="loss_head", grid=(T // tm,), in_specs=[row, row],
                 out_specs=[row, BS((SUBLANES, LANES), lambda i: (0, 0))],
                 out_shape=[_sds((T, d), F32), _sds((SUBLANES, LANES), F32)])(y, target)


def _adamw_math(w, g, m, v):
    m = ADAM_B1 * m + (1.0 - ADAM_B1) * g
    v = ADAM_B2 * v + (1.0 - ADAM_B2) * (g * g)
    m_hat = m / (1.0 - ADAM_B1 ** ADAM_STEP)
    v_hat = v / (1.0 - ADAM_B2 ** ADAM_STEP)
    return -ADAM_LR * (m_hat / (jnp.sqrt(v_hat) + ADAM_EPS) + ADAM_WD * w), m, v


def _adamw(name, w, g, m, v):
    L, R, C = w.shape
    tr = 256 if R % 256 == 0 else R
    outs = None
    for l in range(L):
        def body(w_ref, g_ref, m_ref, v_ref, *rest):
            go_ref, d_ref, mo_ref, vo_ref = rest[-4:]
            gv = g_ref[...]
            d_ref[...], mo_ref[...], vo_ref[...] = _adamw_math(w_ref[...], gv, m_ref[...], v_ref[...])
            go_ref[...] = gv

        layer = BS((None, tr, C), functools.partial(lambda l, i: (l, i, 0), l))
        prev = [] if outs is None else list(outs)
        outs = _call(body, name=f"{name}_{l}", grid=(R // tr,),
                     in_specs=[layer, BS((tr, C), lambda i: (i, 0)), layer, layer] + [pl.BlockSpec(memory_space=pl.ANY)] * len(prev),
                     out_specs=[layer] * 4, out_shape=[_sds((L, R, C), F32)] * 4,
                     aliases={4 + n: n for n in range(len(prev))})(w, g[l], m, v, *prev)
    return outs


def _place():
    x, y, c = lax.axis_index("x"), lax.axis_index("y"), lax.axis_index("c")
    other_chips = [(1 - x, y), (x, 1 - y), (1 - x, 1 - y)]
    return x, y, c, other_chips


def _remote(src, dst, send_sem, recv_sem, dev):
    return pltpu.make_async_remote_copy(src_ref=src, dst_ref=dst, send_sem=send_sem, recv_sem=recv_sem,
                                        device_id=dev, device_id_type=MESH)


def _prefetch_call(body, *, name, grid, in_specs, out_specs, out_shape):
    grid_spec = pltpu.PrefetchScalarGridSpec(num_scalar_prefetch=1, grid=grid, in_specs=in_specs, out_specs=out_specs)
    params = pltpu.CompilerParams(vmem_limit_bytes=VMEM_LIMIT_V7X, dimension_semantics=("arbitrary",) * len(grid))
    return pl.pallas_call(body, name=name, grid_spec=grid_spec, out_shape=out_shape, compiler_params=params)


def _row_tile(rows):
    return 256 if rows % 256 == 0 else rows


def _cast_place(name, w, chip):
    _, rows, C = w.shape
    tr = _row_tile(rows)

    def body(chip_ref, w_ref, o_ref):
        o_ref[...] = w_ref[...].astype(BF16)

    return _prefetch_call(body, name=name, grid=(2, rows // tr),
                          in_specs=[BS((None, tr, C), lambda h, i, chip_ref: (h, i, 0))],
                          out_specs=BS((None, None, tr, C), lambda h, i, chip_ref: (chip_ref[0], h, i, 0)),
                          out_shape=_sds((N_CHIPS, 2, rows, C), BF16))(chip, w)


def _all_gather_chips(gs):
    n = len(gs)

    def body(*refs):
        g, send_sems, recv_sems = refs[n:2 * n], refs[-2], refs[-1]
        x, y, c, chips = _place()
        me, sibling = 2 * x + y, (x, y, 1 - c)
        sends = [_remote(g[i].at[me, c], g[i].at[me, c], send_sems.at[6 * i + k], recv_sems.at[6 * i + k], (px, py, c))
                 for i in range(n) for k, (px, py) in enumerate(chips)]
        for cp in sends:
            cp.start()
        passed = []
        for i in range(n):
            for k, (px, py) in enumerate(chips):
                landed = g[i].at[2 * px + py, c]
                _remote(landed, landed, send_sems.at[6 * i + k], recv_sems.at[6 * i + k], (px, py, c)).wait_recv()
                passed.append(_remote(landed, landed, send_sems.at[6 * i + 3 + k], recv_sems.at[6 * i + 3 + k], sibling))
                passed[-1].start()
        for i in range(n):
            for k, (px, py) in enumerate(chips):
                theirs = g[i].at[2 * px + py, 1 - c]
                _remote(theirs, theirs, send_sems.at[6 * i + 3 + k], recv_sems.at[6 * i + 3 + k], sibling).wait_recv()
        for cp in sends + passed:
            cp.wait_send()

    return _call(body, name="all_gather_weights", in_specs=[HBM] * n, out_specs=[HBM] * n,
                 out_shape=[_sds(a.shape, a.dtype) for a in gs], aliases={i: i for i in range(n)},
                 scratch=[pltpu.SemaphoreType.DMA((6 * n,)), pltpu.SemaphoreType.DMA((6 * n,))])(*gs)


def _sibling_exchange(gs):
    n = len(gs)

    def body(*refs):
        g, r, send_sems, recv_sems = refs[:n], refs[n:2 * n], refs[-2], refs[-1]
        x, y, c, _ = _place()
        copies = [_remote(g[i].at[:, 1 - c], r[i], send_sems.at[i], recv_sems.at[i], (x, y, 1 - c)) for i in range(n)]
        for cp in copies:
            cp.start()
        for cp in copies:
            cp.wait()

    return _call(body, name="grad_sibling_exchange", in_specs=[HBM] * n, out_specs=[HBM] * n,
                 out_shape=[_sds((a.shape[0],) + a.shape[2:], a.dtype) for a in gs],
                 scratch=[pltpu.SemaphoreType.DMA((n,)), pltpu.SemaphoreType.DMA((n,))])(*gs)


def _chip_scatter(ps):
    n = len(ps)

    def body(*refs):
        p, r, send_sems, recv_sems = refs[:n], refs[n:2 * n], refs[-2], refs[-1]
        x, y, c, chips = _place()
        sends = [_remote(p[i].at[2 * px + py], r[i].at[k], send_sems.at[3 * i + k], recv_sems.at[3 * i + k], (px, py, c))
                 for i in range(n) for k, (px, py) in enumerate(chips)]
        for cp in sends:
            cp.start()
        for i in range(n):
            for k, (px, py) in enumerate(chips):
                _remote(r[i].at[k], r[i].at[k], send_sems.at[3 * i + k], recv_sems.at[3 * i + k], (px, py, c)).wait_recv()
        for cp in sends:
            cp.wait_send()

    return _call(body, name="grad_chip_scatter", in_specs=[HBM] * n, out_specs=[HBM] * n,
                 out_shape=[_sds((N_CHIPS - 1,) + a.shape[1:], a.dtype) for a in ps],
                 scratch=[pltpu.SemaphoreType.DMA((3 * n,)), pltpu.SemaphoreType.DMA((3 * n,))])(*ps)


def _sibling_share(fs):
    n = len(fs)

    def body(*refs):
        f, send_sems, recv_sems = refs[n:2 * n], refs[-2], refs[-1]
        x, y, c, _ = _place()
        sends = [_remote(f[i].at[c], f[i].at[c], send_sems.at[i], recv_sems.at[i], (x, y, 1 - c)) for i in range(n)]
        for cp in sends:
            cp.start()
        for i in range(n):
            theirs = f[i].at[1 - c]
            _remote(theirs, theirs, send_sems.at[i], recv_sems.at[i], (x, y, 1 - c)).wait_recv()
        for cp in sends:
            cp.wait_send()

    return _call(body, name="grad_sibling_share", in_specs=[HBM] * n, out_specs=[HBM] * n,
                 out_shape=[_sds(a.shape, a.dtype) for a in fs], aliases={i: i for i in range(n)},
                 scratch=[pltpu.SemaphoreType.DMA((n,)), pltpu.SemaphoreType.DMA((n,))])(*fs)


def _all_reduce_small(name, v):
    n_dev = 8
    flips = [(fx, fy, fc) for fx in (0, 1) for fy in (0, 1) for fc in (0, 1)][1:]

    def body(v_ref, o_ref, buf, send_sems, recv_sems):
        x, y, c, _ = _place()
        peers = [(1 - x if fx else x, 1 - y if fy else y, 1 - c if fc else c) for fx, fy, fc in flips]
        me = 4 * x + 2 * y + c
        buf[me] = v_ref[...]
        sends = [_remote(v_ref, buf.at[me], send_sems.at[k], recv_sems.at[k], peer) for k, peer in enumerate(peers)]
        for cp in sends:
            cp.start()
        for k, (px, py, pc) in enumerate(peers):
            theirs = buf.at[4 * px + 2 * py + pc]
            _remote(v_ref, theirs, send_sems.at[k], recv_sems.at[k], (px, py, pc)).wait_recv()
        for cp in sends:
            cp.wait_send()
        acc = buf[0]
        for d in range(1, n_dev):
            acc = acc + buf[d]
        o_ref[...] = acc

    return _call(body, name=name, in_specs=[VMEM], out_specs=VMEM, out_shape=_sds(v.shape, F32),
                 scratch=[pltpu.VMEM((n_dev,) + v.shape, F32), pltpu.SemaphoreType.DMA((7,)), pltpu.SemaphoreType.DMA((7,))])(v)


def _add_halves(name, g, r, c):
    _, _, rows, C = g.shape
    tr = _row_tile(rows)

    def body(c_ref, g_ref, r_ref, o_ref):
        o_ref[...] = (g_ref[...] + r_ref[...]).astype(BF16)

    spec = BS((None, tr, C), lambda j, i, c_ref: (j, i, 0))
    return _prefetch_call(body, name=name, grid=(N_CHIPS, rows // tr),
                          in_specs=[BS((None, None, tr, C), lambda j, i, c_ref: (j, c_ref[0], i, 0)), spec], out_specs=spec,
                          out_shape=_sds((N_CHIPS, rows, C), BF16))(c, g, r)


def _sum_partials(name, p, r, chip_c):
    _, rows, C = p.shape
    tr = _row_tile(rows)

    def body(s_ref, p_ref, r_ref, o_ref):
        acc = p_ref[...].astype(F32)
        for k in range(N_CHIPS - 1):
            acc = acc + r_ref[k].astype(F32)
        o_ref[...] = acc

    return _prefetch_call(body, name=name, grid=(rows // tr,),
                          in_specs=[BS((None, tr, C), lambda i, s: (s[0], i, 0)), BS((N_CHIPS - 1, tr, C), lambda i, s: (0, i, 0))],
                          out_specs=BS((None, tr, C), lambda i, s: (s[1], i, 0)), out_shape=_sds((2, rows, C), F32))(chip_c, p, r)


_SHARDED = ("even_w_in", "even_w_out", "odd_w_in", "q_b", "kv_b", "odd_w_out", "ffn_w_gate", "ffn_w_up", "ffn_w_down")
_REPLICATED = ("mix_norm", "ffn_norm", "sg_ln_g", "sg_w_s", "sg_b_s", "pool_w", "q_norm", "k_norm")
_SMALL_SHARDED = ("sc_conv_w", "pool_scale", "q_a_norm", "kv_a_norm")
_WEIGHTS = ("mix_norm", "ffn_norm", "even_w_in", "sg_ln_g", "sg_w_s", "sg_b_s", "sc_conv_w", "even_w_out", "odd_w_in", "pool_w",
            "pool_scale", "q_a_norm", "q_b", "kv_a_norm", "kv_b", "q_norm", "k_norm", "odd_w_out", "ffn_w_gate", "ffn_w_up",
            "ffn_w_down")


def _pad_rows(flat, width, align):
    n = flat.shape[0]
    rows = -(-n // (width * align)) * align
    return jnp.pad(flat, (0, rows * width - n)).reshape(rows, width)


def _gather_weights(shards, chip):
    halves = []
    for n in _SHARDED:
        a = shards[n]
        halves.append(a if a.shape[0] == 2 else a.reshape(2, a.shape[1] // 2, a.shape[2]))
    placed = [_cast_place(f"place_{n}", a, chip) for n, a in zip(_SHARDED, halves)]
    out = dict(zip(_SHARDED, _all_gather_chips(placed)))
    for n in ("even_w_in", "even_w_out", "odd_w_in", "q_b", "kv_b", "odd_w_out"):
        out[n] = out[n].reshape(N_CHIPS, -1, out[n].shape[-1])
    for n in ("q_b", "kv_b"):
        out[n] = out[n].transpose(1, 0, 2).reshape(out[n].shape[1], -1)
    for n in ("even_w_out", "odd_w_in", "odd_w_out"):
        out[n] = out[n].reshape(-1, out[n].shape[-1])
    return out


def _forward_backward(x, positions, target, W, small):
    batch, seq, _ = x.shape
    T = batch * seq
    tm = _token_tile(seq)
    x0 = x.reshape(T, D_MODEL)

    inv_freq = ROPE_THETA ** (-jnp.arange(0, QK_ROPE, 2, dtype=F32) / QK_ROPE)
    ang = (positions.astype(F32)[..., None] * inv_freq).reshape(T, QK_ROPE // 2)
    cos, sin = jnp.cos(ang), jnp.sin(ang)
    pad = jnp.zeros((T, LANES - QK_ROPE), F32)
    cos_t = jnp.concatenate([cos, cos, pad], axis=1)
    sin_t = jnp.concatenate([-sin, sin, pad], axis=1)

    tril = jnp.tril(jnp.ones((SG_CHUNK, SG_CHUNK), bool))
    w_tril = jnp.where(tril[None], small["sg_w_s"][0], 0.0).astype(BF16)
    b_lanes = jnp.broadcast_to(small["sg_b_s"][0][:, :, None], (SG_HEADS, SG_CHUNK, SG_DIM))
    conv_w = jnp.pad(small["sc_conv_w"][0], ((0, SUBLANES - CONV_TAPS), (0, 0)))
    ln_g = small["sg_ln_g"]
    pool_diag = jnp.zeros((POOL_WIDTH, POOL_WIDTH), F32)
    for g in range(len(POOL_WINDOWS)):
        pool_diag = pool_diag.at[POOL_DIM * g:POOL_DIM * (g + 1), POOL_DIM * g:POOL_DIM * (g + 1)].set(small["pool_w"][0, g])
    pool_diag = pool_diag.astype(BF16)
    pool_scale = small["pool_scale"]
    w_in_odd = jnp.pad(W["odd_w_in"], ((0, 0), (0, ODD_IN_PAD - ODD_IN)))
    q_b = jnp.pad(W["q_b"].reshape(Q_LORA, HEADS, QK_DIM).transpose(1, 0, 2), ((0, 0), (0, 0), (0, QK_PAD - QK_DIM)))
    kv_b = W["kv_b"].reshape(KV_LORA, HEADS, QK_NOPE + V_DIM).transpose(1, 0, 2)
    q_g = jnp.pad(small["q_norm"], ((0, 0), (0, QK_PAD - QK_DIM)))
    k_g = jnp.pad(small["k_norm"], ((0, 0), (0, QK_PAD - QK_DIM)))
    qa_g, kva_g = small["q_a_norm"], small["kv_a_norm"]
    ffn = [(small["ffn_norm"][l], W["ffn_w_gate"], W["ffn_w_up"], W["ffn_w_down"]) for l in range(2)]
    w_in_even = W["even_w_in"]
    in_shard = EVEN_IN // N_CHIPS

    h0 = _rmsnorm_fwd("mix0_norm", x0, small["mix_norm"][0], tm)
    proj0 = _matmul("even_in", "nn", [(h0, w_in_even)],
                    [(BS((tm, D_MODEL), lambda i, j, k: (i, 0)), BS((None, D_MODEL, in_shard), lambda i, j, k: (j, 0, 0)))],
                    (T // tm, N_CHIPS, 1), _sds((T, EVEN_IN), F32), BS((tm, in_shard), lambda i, j, k: (i, j)), (tm, in_shard))
    mix0 = _even_mixer_fwd(proj0, ln_g, w_tril, b_lanes, conv_w, seq, tm)
    x1 = _mm("even_out", "nn", mix0, W["even_w_out"], F32, tk=1024, add=x0)
    x2, ffn0_saved = _ffn_fwd(0, x1, *ffn[0], tm)
    h2 = _rmsnorm_fwd("mix1_norm", x2, small["mix_norm"][1], tm)
    proj1 = _mm("odd_in", "nn", h2, w_in_odd, F32, tk=1024)
    mix1 = _pool_fwd(proj1, pool_diag, pool_scale, seq, tm)
    q, k, v = _mla_qkv_fwd(proj1, cos_t, sin_t, qa_g, kva_g, q_b, kv_b, q_g, k_g, tm)
    mix1, lse = _flash_fwd(q, k, v, mix1, batch, seq)
    x3 = _mm("odd_out", "nn", mix1, W["odd_w_out"], F32, tk=1024, add=x2)
    x4, ffn1_saved = _ffn_fwd(1, x3, *ffn[1], tm)
    dy, sq = _loss_head(x4, target.reshape(T, D_MODEL), tm)

    G = {}
    dx3, dffn_g1, dwg1, dwu1, dwd1 = _ffn_bwd(1, x3, *ffn[1], ffn1_saved, dy, tm)
    dmix1 = _mm("odd_out_dx", "nt", dx3, W["odd_w_out"], BF16, tk=1024)
    G["odd_w_out"] = _mm("odd_out_dw", "tn", mix1, dx3, F32)
    dq, delta = _flash_bwd_dq(q, k, v, dmix1, mix1, lse, batch, seq)
    dk, dv = _flash_bwd_dkv(q, k, v, dmix1, lse, delta, batch, seq)
    dz_pool, dpool_diag, G["pool_scale"] = _pool_bwd(proj1, dmix1, pool_diag, pool_scale, seq, tm)
    dproj1, dq_b, dkv_b, dq_g, dk_g, G["q_a_norm"], G["kv_a_norm"] = _mla_qkv_bwd(
        proj1, cos_t, sin_t, qa_g, kva_g, q_b, kv_b, q_g, k_g, dq, dk, dv, dz_pool, tm)
    G["pool_w"] = jnp.stack([dpool_diag[POOL_DIM * g:POOL_DIM * (g + 1), POOL_DIM * g:POOL_DIM * (g + 1)]
                             for g in range(len(POOL_WINDOWS))])[None]
    G["q_b"] = dq_b[:, :, :QK_DIM].transpose(1, 0, 2).reshape(Q_LORA, HEADS * QK_DIM)
    G["kv_b"] = dkv_b.transpose(1, 0, 2).reshape(KV_LORA, HEADS * (QK_NOPE + V_DIM))
    G["q_norm"], G["k_norm"] = dq_g[:, :QK_DIM], dk_g[:, :QK_DIM]
    dh2 = _mm("odd_in_dx", "nt", dproj1, W["odd_w_in"], F32, tk=ODD_IN)
    G["odd_w_in"] = _mm("odd_in_dw", "tn", h2, dproj1, F32, tn=ODD_IN)
    dx2, dmix_g1 = _rmsnorm_bwd("mix1_norm_bwd", x2, small["mix_norm"][1], dh2, dx3, tm)
    dx1, dffn_g0, dwg0, dwu0, dwd0 = _ffn_bwd(0, x1, *ffn[0], ffn0_saved, dx2, tm)
    dmix0 = _mm("even_out_dx", "nt", dx1, W["even_w_out"], F32, tk=1024)
    G["even_w_out"] = _mm("even_out_dw", "tn", mix0, dx1, F32)
    dproj0, dw_s, db_lanes, G["sg_ln_g"], dconv = _even_mixer_bwd(proj0, dmix0, ln_g, w_tril, b_lanes, conv_w, seq, tm)
    G["sg_w_s"] = dw_s[None]
    G["sg_b_s"] = jnp.sum(db_lanes, axis=-1)[None]
    G["sc_conv_w"] = dconv[None, :CONV_TAPS]
    tn = 512
    dh0 = _matmul("even_in_dx", "nt", [(dproj0, w_in_even)],
                  [(BS((tm, in_shard), lambda i, j, k: (i, k)), BS((None, tn, in_shard), lambda i, j, k: (k, j, 0)))],
                  (T // tm, D_MODEL // tn, N_CHIPS), _sds((T, D_MODEL), F32), BS((tm, tn), lambda i, j, k: (i, j)), (tm, tn))
    tk = min(512, T)
    G["even_w_in"] = _matmul("even_in_dw", "tn", [(h0, dproj0)],
                             [(BS((tk, 512), lambda j, i, k: (k, i)), BS((tk, in_shard), lambda j, i, k: (k, j)))],
                             (N_CHIPS, D_MODEL // 512, T // tk), _sds((N_CHIPS, D_MODEL, in_shard), F32),
                             BS((None, 512, in_shard), lambda j, i, k: (j, i, 0)), (512, in_shard))
    dx0, dmix_g0 = _rmsnorm_bwd("mix0_norm_bwd", x0, small["mix_norm"][0], dh0, dx1, tm)
    G["mix_norm"] = jnp.concatenate([dmix_g0, dmix_g1], axis=0)
    G["ffn_norm"] = jnp.concatenate([dffn_g0, dffn_g1], axis=0)
    G["ffn"] = [(dwg0, dwu0, dwd0), (dwg1, dwu1, dwd1)]
    return sq[0, 0], dx0.reshape(batch, seq, D_MODEL), G


def _small_vector(parts, names):
    flat = jnp.concatenate([parts[n].astype(F32).reshape(-1) for n in names])
    return _pad_rows(flat, LANES, SUBLANES)


def _split_small(vec, like, names):
    out, off, flat = {}, 0, vec.reshape(-1)
    for n in names:
        size = math.prod(like[n].shape)
        out[n] = flat[off:off + size].reshape(like[n].shape)
        off += size
    return out


def _whole_shape(a):
    return a.shape[:-1] + (a.shape[-1] * N_CHIPS,)


def kernel(x, positions, mix_norm, ffn_norm, even_w_in, sg_ln_g, sg_w_s, sg_b_s, sc_conv_w, even_w_out, odd_w_in, pool_w, pool_scale, q_a_norm, q_b, kv_a_norm, kv_b, q_norm, k_norm, odd_w_out, ffn_w_gate, ffn_w_up, ffn_w_down, loss_target, m_mix_norm, m_ffn_norm, m_even_w_in, m_sg_ln_g, m_sg_w_s, m_sg_b_s, m_sc_conv_w, m_even_w_out, m_odd_w_in, m_pool_w, m_pool_scale, m_q_a_norm, m_q_b, m_kv_a_norm, m_kv_b, m_q_norm, m_k_norm, m_odd_w_out, m_ffn_w_gate, m_ffn_w_up, m_ffn_w_down, v_mix_norm, v_ffn_norm, v_even_w_in, v_sg_ln_g, v_sg_w_s, v_sg_b_s, v_sc_conv_w, v_even_w_out, v_odd_w_in, v_pool_w, v_pool_scale, v_q_a_norm, v_q_b, v_kv_a_norm, v_kv_b, v_q_norm, v_k_norm, v_odd_w_out, v_ffn_w_gate, v_ffn_w_up, v_ffn_w_down):
    args = dict(locals())
    w = {n: args[n] for n in _WEIGHTS}
    m = {n: args["m_" + n] for n in _WEIGHTS}
    v = {n: args["v_" + n] for n in _WEIGHTS}
    cx, cy, cc = lax.axis_index("x"), lax.axis_index("y"), lax.axis_index("c")
    chip = 2 * cx + cy

    chip_arr = chip.astype(jnp.int32).reshape(1)
    W = _gather_weights(w, chip_arr)
    placed = {}
    for n in _SMALL_SHARDED:
        a = w[n]
        whole = jnp.zeros(a.shape[:-1] + (N_CHIPS, a.shape[-1]), F32)
        whole = lax.dynamic_update_slice_in_dim(whole, a[..., None, :], chip, axis=a.ndim - 1)
        placed[n] = jnp.where(cc == 0, whole, 0.0).reshape(_whole_shape(a))
    small = dict({n: w[n] for n in _REPLICATED},
                 **_split_small(_all_reduce_small("gather_small_weights", _small_vector(placed, _SMALL_SHARDED)), placed, _SMALL_SHARDED))

    sq, grad_x, G = _forward_backward(x, positions, loss_target, W, small)
    loss = lax.psum(0.5 * sq / D_MODEL, ("x", "y", "c"))

    small_names = _REPLICATED + _SMALL_SHARDED
    summed = _split_small(_all_reduce_small("reduce_small_grads", _small_vector(G, small_names)), G, small_names)
    grads = {n: summed[n] for n in _REPLICATED}
    for n in _SMALL_SHARDED:
        a = w[n]
        grads[n] = lax.dynamic_slice_in_dim(summed[n].reshape(a.shape[:-1] + (N_CHIPS, a.shape[-1])), chip, 1,
                                            axis=a.ndim - 1).reshape(a.shape)

    def shard_major(g, cols):
        return g.reshape(g.shape[0], N_CHIPS, cols).transpose(1, 0, 2)

    big = [("even_w_in", G["even_w_in"]),
           ("even_w_out", G["even_w_out"].reshape(N_CHIPS, -1, D_MODEL)),
           ("odd_w_in", G["odd_w_in"].reshape(N_CHIPS, -1, ODD_IN)),
           ("q_b", shard_major(G["q_b"], HEADS * QK_DIM // N_CHIPS)),
           ("kv_b", shard_major(G["kv_b"], HEADS * (QK_NOPE + V_DIM) // N_CHIPS)),
           ("odd_w_out", G["odd_w_out"].reshape(N_CHIPS, -1, D_MODEL))]
    for l in range(2):
        big += [(f"ffn_w_gate{l}", G["ffn"][l][0]), (f"ffn_w_up{l}", G["ffn"][l][1]), (f"ffn_w_down{l}", G["ffn"][l][2])]
    names = [n for n, _ in big]
    halves = [g.reshape(N_CHIPS, 2, g.shape[1] // 2, g.shape[2]) for _, g in big]
    from_sibling = _sibling_exchange(halves)
    c_arr = cc.astype(jnp.int32).reshape(1)
    partial = [_add_halves(f"add_{n}", g, r, c_arr) for n, g, r in zip(names, halves, from_sibling)]
    scattered = _chip_scatter(partial)
    chip_c = jnp.stack([chip, cc]).astype(jnp.int32)
    sums = [_sum_partials(f"sum_{n}", p, r, chip_c) for n, p, r in zip(names, partial, scattered)]
    shard_grad = {n: f.reshape(1, -1, f.shape[-1]) for n, f in zip(names, _sibling_share(sums))}

    out = {}
    for n in ("even_w_in", "even_w_out", "odd_w_in", "q_b", "kv_b", "odd_w_out"):
        out[n] = _adamw(f"adamw_{n}", w[n], [shard_grad[n][0]], m[n], v[n])
    for n in ("ffn_w_gate", "ffn_w_up", "ffn_w_down"):
        out[n] = _adamw(f"adamw_{n}", w[n], [shard_grad[f"{n}{l}"][0] for l in range(2)], m[n], v[n])
    packed = [_small_vector(d, small_names) for d in (w, grads, m, v)]
    res = _adamw("adamw_small", packed[0][None], [packed[1]], packed[2][None], packed[3][None])
    delta_s, m_s, v_s = (_split_small(r, w, small_names) for r in res[1:])
    for n in small_names:
        out[n] = (grads[n], delta_s[n], m_s[n], v_s[n])

    return (loss, grad_x, *[out[n][0] for n in _WEIGHTS], *[out[n][1] for n in _WEIGHTS],
            *[out[n][2] for n in _WEIGHTS], *[out[n][3] for n in _WEIGHTS])
```

```python
import functools
import math

import jax
import jax.numpy as jnp
from jax import lax
from jax.experimental import pallas as pl
from jax.experimental.pallas import tpu as pltpu

F32, BF16 = jnp.float32, jnp.bfloat16
BS = pl.BlockSpec

D_MODEL = 1024
EPS = 1e-6
NEG_INF = -1e30
SG_HEADS, SG_DIM, SG_WIDTH, SG_CHUNK = 4, 128, 512, 128
SC_WIDTH, CONV_TAPS = 512, 3
EVEN_IN = 2 * SG_WIDTH + 3 * SC_WIDTH
POOL_WINDOWS = (2, 4, 8, 16)
POOL_DIM, POOL_WIDTH = 64, 256
POOL_HALO = 16
HEADS, Q_LORA, KV_LORA, QK_NOPE, QK_ROPE, V_DIM = 6, 384, 256, 128, 64, 128
QK_DIM = QK_NOPE + QK_ROPE
QK_PAD = 256
ODD_IN = POOL_WIDTH + Q_LORA + KV_LORA + QK_ROPE
ODD_IN_PAD = 1024
ROPE_THETA = 10000.0
ATTN_SCALE = QK_DIM ** -0.5
D_FF, N_CHIPS = 2816, 4
FF_SHARD = D_FF // N_CHIPS
ADAM_LR, ADAM_B1, ADAM_B2, ADAM_EPS, ADAM_WD, ADAM_STEP = 0.001, 0.9, 0.999, 1e-08, 0.01, 10
VMEM_LIMIT_V7X = 48 * 2**20
LANES, SUBLANES = 128, 8
MESH = pl.DeviceIdType.MESH
HBM = pl.BlockSpec(memory_space=pltpu.HBM)
VMEM = pl.BlockSpec(memory_space=pltpu.VMEM)

_DIMS = {"nn": (((1,), (0,)), ((), ())), "nt": (((1,), (1,)), ((), ())), "tn": (((0,), (0,)), ((), ()))}


def _dot(a, b, mode="nn"):
    return lax.dot_general(a.astype(BF16), b.astype(BF16), _DIMS[mode], preferred_element_type=F32)


def _call(body, *, name, out_shape, in_specs, out_specs, grid=(), scratch=(), aliases=None):
    params = pltpu.CompilerParams(vmem_limit_bytes=VMEM_LIMIT_V7X,
                                  **({"dimension_semantics": ("arbitrary",) * len(grid)} if grid else {}))
    return pl.pallas_call(body, name=name, grid=grid, in_specs=in_specs, out_specs=out_specs, out_shape=out_shape,
                          scratch_shapes=list(scratch), input_output_aliases=aliases or {}, compiler_params=params)


def _sds(shape, dtype):
    return jax.ShapeDtypeStruct(tuple(shape), dtype)


def _token_tile(seq):
    return 512 if seq % 512 == 0 else seq


def _matmul(name, mode, pairs, pair_specs, grid, out_shape, out_spec, acc_shape, add=None, add_spec=None):
    n, nk = len(pairs), grid[-1]

    def body(*refs):
        ab = refs[:2 * n]
        add_ref = refs[2 * n] if add is not None else None

        def finish(r, o_ref):
            if add_ref is not None:
                r = r + add_ref[...]
            o_ref[...] = r.astype(o_ref.dtype)

        if nk == 1:
            r = _dot(ab[0][...], ab[1][...], mode)
            for p in range(1, n):
                r = r + _dot(ab[2 * p][...], ab[2 * p + 1][...], mode)
            finish(r, refs[-1])
            return
        o_ref, acc = refs[-2], refs[-1]
        k = pl.program_id(len(grid) - 1)

        @pl.when(k == 0)
        def _():
            acc[...] = jnp.zeros_like(acc)

        for p in range(n):
            acc[...] += _dot(ab[2 * p][...], ab[2 * p + 1][...], mode)

        @pl.when(k == nk - 1)
        def _():
            finish(acc[...], o_ref)

    ops = [t for pr in pairs for t in pr] + ([add] if add is not None else [])
    specs = [s for pr in pair_specs for s in pr] + ([add_spec] if add is not None else [])
    return _call(body, name=name, grid=grid, in_specs=specs, out_specs=out_spec, out_shape=out_shape,
                 scratch=[pltpu.VMEM(acc_shape, F32)] if nk > 1 else [])(*ops)


def _grad_shards(name, a, b, a_spec, b_spec, pick, out_shape, n_steps):
    def body(a_ref, b_ref, o_ref):
        @pl.when(pl.program_id(0) == 0)
        def _():
            o_ref[...] = jnp.zeros_like(o_ref)

        for j in range(N_CHIPS):
            aj, bj = pick(a_ref, b_ref, j)
            o_ref[j] += _dot(aj, bj, "tn")

    return _call(body, name=name, grid=(n_steps,), in_specs=[a_spec, b_spec],
                 out_specs=BS(out_shape, lambda k: (0, 0, 0)), out_shape=_sds(out_shape, F32))(a, b)


def _mm(name, mode, a, b, out_dtype, tm=1024, tn=1024, tk=512, add=None):
    if mode == "tn":
        (K, M), N = a.shape, b.shape[1]
    else:
        (M, K), N = a.shape, (b.shape[1] if mode == "nn" else b.shape[0])
    tm, tn, tk = min(tm, M), min(tn, N), min(tk, K)
    a_spec = BS((tk, tm), lambda i, j, k: (k, i)) if mode == "tn" else BS((tm, tk), lambda i, j, k: (i, k))
    b_spec = BS((tn, tk), lambda i, j, k: (j, k)) if mode == "nt" else BS((tk, tn), lambda i, j, k: (k, j))
    o_spec = BS((tm, tn), lambda i, j, k: (i, j))
    return _matmul(name, mode, [(a, b)], [(a_spec, b_spec)], (M // tm, N // tn, K // tk), _sds((M, N), out_dtype),
                   o_spec, (tm, tn), add=add, add_spec=o_spec if add is not None else None)


def _rmsnorm_fwd(name, x, g, tm):
    T, d = x.shape

    def body(x_ref, g_ref, o_ref):
        xv = x_ref[...]
        y = xv * lax.rsqrt(jnp.mean(xv * xv, axis=-1, keepdims=True) + EPS)
        o_ref[...] = (y * g_ref[...]).astype(o_ref.dtype)

    return _call(body, name=name, grid=(T // tm,), in_specs=[BS((tm, d), lambda i: (i, 0)), BS((1, d), lambda i: (0, 0))],
                 out_specs=BS((tm, d), lambda i: (i, 0)), out_shape=_sds((T, d), BF16))(x, g.reshape(1, d))


def _rmsnorm_bwd(name, x, g, dh, dres, tm):
    T, d = x.shape

    def body(x_ref, g_ref, dh_ref, dres_ref, dx_ref, dg_ref):
        xv = x_ref[...]
        r = lax.rsqrt(jnp.mean(xv * xv, axis=-1, keepdims=True) + EPS)
        xhat = xv * r
        dhv = dh_ref[...]

        @pl.when(pl.program_id(0) == 0)
        def _():
            dg_ref[...] = jnp.zeros_like(dg_ref)

        dg_ref[...] += jnp.sum(dhv * xhat, axis=0, keepdims=True)
        dxhat = dhv * g_ref[...]
        dx_ref[...] = dres_ref[...] + r * (dxhat - xhat * jnp.mean(dxhat * xhat, axis=-1, keepdims=True))

    row = BS((tm, d), lambda i: (i, 0))
    vec = BS((1, d), lambda i: (0, 0))
    return _call(body, name=name, grid=(T // tm,), in_specs=[row, vec, row, row], out_specs=[row, vec],
                 out_shape=[_sds((T, d), F32), _sds((1, d), F32)])(x, g.reshape(1, d), dh, dres)


def _ffn_up(name, h, wg, wu, l, tm):
    T = h.shape[0]

    def body(h_ref, wg_ref, wu_ref, g_ref, u_ref, a_ref):
        hv = h_ref[...]
        g = _dot(hv, wg_ref[...])
        u = _dot(hv, wu_ref[...])
        g_ref[...] = g
        u_ref[...] = u
        a_ref[...] = (g * (1.0 / (1.0 + jnp.exp(-g))) * u).astype(BF16)

    w_spec = BS((None, None, D_MODEL, FF_SHARD), lambda j, i: (j, l, 0, 0))
    o_spec = BS((None, tm, FF_SHARD), lambda j, i: (j, i, 0))
    sh = (N_CHIPS, T, FF_SHARD)
    return _call(body, name=name, grid=(N_CHIPS, T // tm), in_specs=[BS((tm, D_MODEL), lambda j, i: (i, 0)), w_spec, w_spec],
                 out_specs=[o_spec, o_spec, o_spec], out_shape=[_sds(sh, F32), _sds(sh, F32), _sds(sh, BF16)])(h, wg, wu)


def _ffn_act_bwd(name, dxo, wd, l, g, u, tm):
    T = dxo.shape[0]

    def body(dx_ref, wd_ref, g_ref, u_ref, dg_ref, du_ref):
        da = _dot(dx_ref[...], wd_ref[...], "nt")
        g = g_ref[...]
        sig = 1.0 / (1.0 + jnp.exp(-g))
        dg_ref[...] = (da * u_ref[...] * (sig * (1.0 + g * (1.0 - sig)))).astype(BF16)
        du_ref[...] = (da * (g * sig)).astype(BF16)

    t_spec = BS((None, tm, FF_SHARD), lambda i, j: (j, i, 0))
    sh = _sds((N_CHIPS, T, FF_SHARD), BF16)
    return _call(body, name=name, grid=(T // tm, N_CHIPS),
                 in_specs=[BS((tm, D_MODEL), lambda i, j: (i, 0)), BS((None, None, FF_SHARD, D_MODEL), lambda i, j: (j, l, 0, 0)), t_spec, t_spec],
                 out_specs=[t_spec, t_spec], out_shape=[sh, sh])(dxo, wd, g, u)


def _big_tile(n):
    return min(1024, n)


def _ffn_fwd(l, x, gain, wg, wu, wd, tm):
    T = x.shape[0]
    h = _rmsnorm_fwd(f"ffn{l}_norm", x, gain, tm)
    tm = _big_tile(T)
    g, u, a = _ffn_up(f"ffn{l}_up", h, wg, wu, l, tm)
    tn = D_MODEL
    out = _matmul(f"ffn{l}_down", "nn", [(a, wd)],
                  [(BS((None, tm, FF_SHARD), lambda i, j, k: (k, i, 0)), BS((None, None, FF_SHARD, tn), lambda i, j, k: (k, l, 0, j)))],
                  (T // tm, D_MODEL // tn, N_CHIPS), _sds((T, D_MODEL), F32), BS((tm, tn), lambda i, j, k: (i, j)), (tm, tn),
                  add=x, add_spec=BS((tm, tn), lambda i, j, k: (i, j)))
    return out, (h, g, u, a)


def _ffn_bwd(l, x, gain, wg, wu, wd, saved, dxo, tm):
    h, g, u, a = saved
    T = x.shape[0]
    tm_norm, tm = tm, _big_tile(T)
    dg, du = _ffn_act_bwd(f"ffn{l}_act_bwd", dxo, wd, l, g, u, tm)
    tk = min(512, T)
    tn = D_MODEL
    shards_spec = BS((N_CHIPS, tk, FF_SHARD), lambda k: (0, k, 0))
    rows_spec = BS((tk, D_MODEL), lambda k: (k, 0))
    dwd = _grad_shards(f"ffn{l}_dwd", a, dxo, shards_spec, rows_spec, lambda a_ref, b_ref, j: (a_ref[j], b_ref[...]),
                       (N_CHIPS, FF_SHARD, D_MODEL), T // tk)
    dwg = _grad_shards(f"ffn{l}_dwg", h, dg, rows_spec, shards_spec, lambda a_ref, b_ref, j: (a_ref[...], b_ref[j]),
                       (N_CHIPS, D_MODEL, FF_SHARD), T // tk)
    dwu = _grad_shards(f"ffn{l}_dwu", h, du, rows_spec, shards_spec, lambda a_ref, b_ref, j: (a_ref[...], b_ref[j]),
                       (N_CHIPS, D_MODEL, FF_SHARD), T // tk)
    act_spec = BS((None, tm, FF_SHARD), lambda i, j, k: (k, i, 0))
    w_spec = BS((None, None, tn, FF_SHARD), lambda i, j, k: (k, l, j, 0))
    dh = _matmul(f"ffn{l}_dh", "nt", [(dg, wg), (du, wu)], [(act_spec, w_spec), (act_spec, w_spec)],
                 (T // tm, D_MODEL // tn, N_CHIPS), _sds((T, D_MODEL), F32), BS((tm, tn), lambda i, j, k: (i, j)), (tm, tn))
    dx, dgain = _rmsnorm_bwd(f"ffn{l}_norm_bwd", x, gain, dh, dxo, tm_norm)
    return dx, dgain, dwg, dwu, dwd


_INV_SQRT2 = 1.0 / math.sqrt(2.0)
_INV_SQRT_2PI = 1.0 / math.sqrt(2.0 * math.pi)


def _gelu(x):
    return 0.5 * x * (1.0 + lax.erf(x * _INV_SQRT2))


def _gelu_grad(x):
    return 0.5 * (1.0 + lax.erf(x * _INV_SQRT2)) + x * jnp.exp(-0.5 * x * x) * _INV_SQRT_2PI


def _shift_down(x, k):
    return pltpu.roll(x, k, 0)


def _shift_up(x, k):
    return pltpu.roll(x, x.shape[0] - k, 0)


def _layer_norm_head(xh):
    xc = xh - jnp.mean(xh, axis=-1, keepdims=True)
    rstd = lax.rsqrt(jnp.mean(xc * xc, axis=-1, keepdims=True) + EPS)
    return xc * rstd, rstd


def _even_halo_specs(tm, n_tiles, col_blocks, after):
    rows = tm // SUBLANES
    last = n_tiles * rows - 1
    if after:
        return [BS((SUBLANES, 512), functools.partial(lambda cb, i: (jnp.minimum((i + 1) * rows, last), cb), cb)) for cb in col_blocks]
    return [BS((SUBLANES, 512), functools.partial(lambda cb, i: (jnp.maximum(i * rows - 1, 0), cb), cb)) for cb in col_blocks]


def _even_mixer_fwd(proj, ln_g, w_tril, b_lanes, conv_w, seq, tm):
    T = proj.shape[0]
    tiles_per_seq = seq // tm

    def body(p_ref, hc_ref, hh_ref, lng_ref, w_ref, bb_ref, cw_ref, o_ref):
        first = pl.program_id(0) % tiles_per_seq == 0
        for h in range(SG_HEADS):
            cols = slice(SG_DIM * h, SG_DIM * (h + 1))
            vhat, _ = _layer_norm_head(_gelu(p_ref[:, SG_WIDTH + SG_DIM * h:SG_WIDTH + SG_DIM * (h + 1)]))
            vln = (vhat * lng_ref[:, cols]).astype(BF16)
            for k in range(tm // SG_CHUNK):
                rows = slice(SG_CHUNK * k, SG_CHUNK * (k + 1))
                mixed = _dot(w_ref[h], vln[rows]) + bb_ref[h]
                o_ref[rows, cols] = (_gelu(p_ref[rows, cols]) * mixed).astype(BF16)
        z = p_ref[:, 1536:2048] * p_ref[:, 2048:2560]
        zz = jnp.concatenate([jnp.where(first, 0.0, hc_ref[...] * hh_ref[...]), z], axis=0)
        y = cw_ref[0:1, :] * _shift_down(zz, 2)[SUBLANES:] + cw_ref[1:2, :] * _shift_down(zz, 1)[SUBLANES:] + cw_ref[2:3, :] * z
        o_ref[:, SG_WIDTH:] = (p_ref[:, 1024:1536] * y).astype(BF16)

    full = lambda shape: BS(shape, lambda i: (0,) * len(shape))
    return _call(body, name="even_mixer_fwd", grid=(T // tm,),
                 in_specs=[BS((tm, EVEN_IN), lambda i: (i, 0))] + _even_halo_specs(tm, T // tm, (3, 4), after=False)
                 + [full((1, SG_WIDTH)), full((SG_HEADS, SG_CHUNK, SG_CHUNK)), full((SG_HEADS, SG_CHUNK, SG_DIM)), full((SUBLANES, SC_WIDTH))],
                 out_specs=BS((tm, D_MODEL), lambda i: (i, 0)), out_shape=_sds((T, D_MODEL), BF16))(
        proj, proj, proj, ln_g, w_tril, b_lanes, conv_w)


def _even_mixer_bwd(proj, dmix, ln_g, w_tril, b_lanes, conv_w, seq, tm):
    T = proj.shape[0]
    n_tiles, tiles_per_seq = T // tm, seq // tm

    def body(p_ref, dm_ref, hc_ref, hh_ref, nd_ref, nb_ref, lng_ref, w_ref, bb_ref, cw_ref,
             dp_ref, dw_ref, db_ref, dlng_ref, dcw_ref):
        i = pl.program_id(0)
        first = i % tiles_per_seq == 0
        last = i % tiles_per_seq == tiles_per_seq - 1

        @pl.when(i == 0)
        def _():
            dw_ref[...] = jnp.zeros_like(dw_ref)
            db_ref[...] = jnp.zeros_like(db_ref)
            dlng_ref[...] = jnp.zeros_like(dlng_ref)
            dcw_ref[...] = jnp.zeros_like(dcw_ref)

        for h in range(SG_HEADS):
            cols = slice(SG_DIM * h, SG_DIM * (h + 1))
            vcols = slice(SG_WIDTH + SG_DIM * h, SG_WIDTH + SG_DIM * (h + 1))
            lng = lng_ref[:, cols]
            for k in range(tm // SG_CHUNK):
                rows = slice(SG_CHUNK * k, SG_CHUNK * (k + 1))
                v = p_ref[rows, vcols]
                vhat, rstd = _layer_norm_head(_gelu(v))
                vln = (vhat * lng).astype(BF16)
                mixed = _dot(w_ref[h], vln) + bb_ref[h]
                u = p_ref[rows, cols]
                da = dm_ref[rows, cols]
                dp_ref[rows, cols] = (da * mixed * _gelu_grad(u)).astype(BF16)
                dmixed = da * _gelu(u)
                db_ref[h] += dmixed
                dw_ref[h] += _dot(dmixed, vln, "nt")
                dvln = _dot(w_ref[h], dmixed, "tn")
                dlng_ref[:, cols] += jnp.sum(dvln * vhat, axis=0, keepdims=True)
                dvhat = dvln * lng
                dgv = rstd * (dvhat - jnp.mean(dvhat, axis=-1, keepdims=True)
                              - vhat * jnp.mean(dvhat * vhat, axis=-1, keepdims=True))
                dp_ref[rows, vcols] = (dgv * _gelu_grad(v)).astype(BF16)

        b = p_ref[:, 1024:1536]
        c = p_ref[:, 1536:2048]
        hv = p_ref[:, 2048:2560]
        z = c * hv
        zz = jnp.concatenate([jnp.where(first, 0.0, hc_ref[...] * hh_ref[...]), z], axis=0)
        z1 = _shift_down(zz, 1)[SUBLANES:]
        z2 = _shift_down(zz, 2)[SUBLANES:]
        w0, w1, w2 = cw_ref[0:1, :], cw_ref[1:2, :], cw_ref[2:3, :]
        dbo = dm_ref[:, SG_WIDTH:]
        dy = dbo * b
        dd = jnp.concatenate([dy, jnp.where(last, 0.0, nd_ref[...] * nb_ref[...])], axis=0)
        dz = w2 * dy + w1 * _shift_up(dd, 1)[:tm] + w0 * _shift_up(dd, 2)[:tm]
        dp_ref[:, 1024:1536] = (dbo * (w0 * z2 + w1 * z1 + w2 * z)).astype(BF16)
        dp_ref[:, 1536:2048] = (dz * hv).astype(BF16)
        dp_ref[:, 2048:2560] = (dz * c).astype(BF16)
        dcw_ref[0:1, :] += jnp.sum(dy * z2, axis=0, keepdims=True)
        dcw_ref[1:2, :] += jnp.sum(dy * z1, axis=0, keepdims=True)
        dcw_ref[2:3, :] += jnp.sum(dy * z, axis=0, keepdims=True)

        @pl.when(i == n_tiles - 1)
        def _():
            t_idx = lax.broadcasted_iota(jnp.int32, (SG_CHUNK, SG_CHUNK), 0)
            s_idx = lax.broadcasted_iota(jnp.int32, (SG_CHUNK, SG_CHUNK), 1)
            for h in range(SG_HEADS):
                dw_ref[h] = jnp.where(t_idx >= s_idx, dw_ref[h], 0.0)

    full = lambda shape: BS(shape, lambda i: (0,) * len(shape))
    sq = (SG_HEADS, SG_CHUNK, SG_CHUNK)
    return _call(body, name="even_mixer_bwd", grid=(n_tiles,),
                 in_specs=[BS((tm, EVEN_IN), lambda i: (i, 0)), BS((tm, D_MODEL), lambda i: (i, 0))]
                 + _even_halo_specs(tm, n_tiles, (3, 4), after=False)
                 + _even_halo_specs(tm, n_tiles, (1,), after=True) + _even_halo_specs(tm, n_tiles, (2,), after=True)
                 + [full((1, SG_WIDTH)), full(sq), full(sq), full((SUBLANES, SC_WIDTH))],
                 out_specs=[BS((tm, EVEN_IN), lambda i: (i, 0)), full(sq), full(sq), full((1, SG_WIDTH)), full((SUBLANES, SC_WIDTH))],
                 out_shape=[_sds((T, EVEN_IN), BF16), _sds(sq, F32), _sds(sq, F32), _sds((1, SG_WIDTH), F32), _sds((SUBLANES, SC_WIDTH), F32)])(
        proj, dmix, proj, proj, dmix, proj, ln_g, w_tril, b_lanes, conv_w)


def _pool_select(vals):
    lane = lax.broadcasted_iota(jnp.int32, vals[0].shape, 1)
    out = vals[-1]
    for g in range(len(vals) - 2, -1, -1):
        out = jnp.where(lane < POOL_DIM * (g + 1), vals[g], out)
    return out


def _pool_counts(pos1):
    lane = lax.broadcasted_iota(jnp.int32, (pos1.shape[0], POOL_WIDTH), 1)
    win = _pool_select([jnp.full(lane.shape, float(w), F32) for w in POOL_WINDOWS])
    return jnp.minimum(pos1, win)


def _pool_means(zz, counts):
    s2 = zz + _shift_down(zz, 1)
    s4 = s2 + _shift_down(s2, 2)
    s8 = s4 + _shift_down(s4, 4)
    s16 = s8 + _shift_down(s8, 8)
    return _pool_select([s2, s4, s8, s16])[POOL_HALO:] / counts


def _pool_halo_spec(tm, n_tiles, after):
    rows = tm // POOL_HALO
    if after:
        return BS((POOL_HALO, POOL_WIDTH), lambda i: (jnp.minimum((i + 1) * rows, n_tiles * rows - 1), 0))
    return BS((POOL_HALO, POOL_WIDTH), lambda i: (jnp.maximum(i * rows - 1, 0), 0))


def _pool_fwd(proj, w_diag, scale, seq, tm):
    T = proj.shape[0]
    tiles_per_seq = seq // tm

    def body(z_ref, zh_ref, w_ref, s_ref, o_ref):
        t = pl.program_id(0) % tiles_per_seq
        z = z_ref[...]
        zz = jnp.concatenate([jnp.where(t == 0, 0.0, zh_ref[...]), z], axis=0)
        pos1 = (lax.broadcasted_iota(jnp.int32, (tm, 1), 0) + (t * tm + 1)).astype(F32)
        pooled = _pool_means(zz, _pool_counts(pos1)) - z
        o_ref[...] = (_dot(pooled, w_ref[...]) * s_ref[...]).astype(BF16)

    full = lambda shape: BS(shape, lambda i: (0,) * len(shape))
    return _call(body, name="pool_fwd", grid=(T // tm,),
                 in_specs=[BS((tm, POOL_WIDTH), lambda i: (i, 0)), _pool_halo_spec(tm, T // tm, False),
                           full((POOL_WIDTH, POOL_WIDTH)), full((1, POOL_WIDTH))],
                 out_specs=BS((tm, POOL_WIDTH), lambda i: (i, 0)), out_shape=_sds((T, D_MODEL), BF16))(proj, proj, w_diag, scale)


def _pool_bwd(proj, dmix, w_diag, scale, seq, tm):
    T = proj.shape[0]
    n_tiles, tiles_per_seq = T // tm, seq // tm

    def body(z_ref, zh_ref, do_ref, don_ref, w_ref, s_ref, dz_ref, dw_ref, ds_ref):
        i = pl.program_id(0)
        t = i % tiles_per_seq

        @pl.when(i == 0)
        def _():
            dw_ref[...] = jnp.zeros_like(dw_ref)
            ds_ref[...] = jnp.zeros_like(ds_ref)

        z = z_ref[...]
        zz = jnp.concatenate([jnp.where(t == 0, 0.0, zh_ref[...]), z], axis=0)
        pos1 = (lax.broadcasted_iota(jnp.int32, (tm, 1), 0) + (t * tm + 1)).astype(F32)
        counts = _pool_counts(pos1)
        pooled = _pool_means(zz, counts) - z
        dout = do_ref[...].astype(F32)
        ds_ref[...] += jnp.sum(dout * _dot(pooled, w_ref[...]), axis=0, keepdims=True)
        dlin = dout * s_ref[...]
        dw_ref[...] += _dot(pooled, dlin, "tn")
        dpooled = _dot(dlin, w_ref[...], "nt")
        dpooled_n = _dot(don_ref[...].astype(F32) * s_ref[...], w_ref[...], "nt")
        pos1_n = (lax.broadcasted_iota(jnp.int32, (POOL_HALO, 1), 0) + ((t + 1) * tm + 1)).astype(F32)
        dmean_n = jnp.where(t == tiles_per_seq - 1, 0.0, dpooled_n / _pool_counts(pos1_n))
        dd = jnp.concatenate([dpooled / counts, dmean_n], axis=0)
        r2 = dd + _shift_up(dd, 1)
        r4 = r2 + _shift_up(r2, 2)
        r8 = r4 + _shift_up(r4, 4)
        r16 = r8 + _shift_up(r8, 8)
        dz_ref[...] = (_pool_select([r2, r4, r8, r16])[:tm] - dpooled).astype(BF16)

    full = lambda shape: BS(shape, lambda i: (0,) * len(shape))
    return _call(body, name="pool_bwd", grid=(n_tiles,),
                 in_specs=[BS((tm, POOL_WIDTH), lambda i: (i, 0)), _pool_halo_spec(tm, n_tiles, False),
                           BS((tm, POOL_WIDTH), lambda i: (i, 0)), _pool_halo_spec(tm, n_tiles, True),
                           full((POOL_WIDTH, POOL_WIDTH)), full((1, POOL_WIDTH))],
                 out_specs=[BS((tm, POOL_WIDTH), lambda i: (i, 0)), full((POOL_WIDTH, POOL_WIDTH)), full((1, POOL_WIDTH))],
                 out_shape=[_sds((T, POOL_WIDTH), BF16), _sds((POOL_WIDTH, POOL_WIDTH), F32), _sds((1, POOL_WIDTH), F32)])(
        proj, proj, dmix, dmix, w_diag, scale)


def _rope_partner(r):
    lane = lax.broadcasted_iota(jnp.int32, r.shape, 1)
    return jnp.where(lane < QK_ROPE // 2, pltpu.roll(r, LANES - QK_ROPE // 2, 1), pltpu.roll(r, QK_ROPE // 2, 1))


def _rope(x, cos, sin_signed):
    r = x[:, QK_NOPE:]
    return jnp.concatenate([x[:, :QK_NOPE], r * cos + _rope_partner(r) * sin_signed], axis=1)


def _rope_transposed(dx, cos, sin_signed):
    dr = dx[:, QK_NOPE:]
    return jnp.concatenate([dx[:, :QK_NOPE], dr * cos + _rope_partner(dr * sin_signed)], axis=1)


def _head_norm(x):
    r = lax.rsqrt(jnp.sum(x * x, axis=-1, keepdims=True) * (1.0 / QK_DIM) + EPS)
    return x * r, r


def _head_norm_bwd(dy, xhat, r, gain):
    dxhat = dy * gain
    return r * (dxhat - xhat * (jnp.sum(dxhat * xhat, axis=-1, keepdims=True) * (1.0 / QK_DIM)))


def _latents(p_ref, qag_ref, kvag_ref):
    ql = p_ref[:, POOL_WIDTH:POOL_WIDTH + Q_LORA]
    kvl = p_ref[:, POOL_WIDTH + Q_LORA:POOL_WIDTH + Q_LORA + KV_LORA]
    rq = lax.rsqrt(jnp.mean(ql * ql, axis=-1, keepdims=True) + EPS)
    rkv = lax.rsqrt(jnp.mean(kvl * kvl, axis=-1, keepdims=True) + EPS)
    return ql * rq, rq, kvl * rkv, rkv


def _mla_specs(tm):
    full = lambda shape: BS(shape, lambda i, h: (0,) * len(shape))
    return [BS((tm, ODD_IN_PAD), lambda i, h: (i, 0)), BS((tm, LANES), lambda i, h: (i, 0)), BS((tm, LANES), lambda i, h: (i, 0)),
            full((1, Q_LORA)), full((1, KV_LORA)), BS((None, Q_LORA, QK_PAD), lambda i, h: (h, 0, 0)),
            BS((None, KV_LORA, QK_PAD), lambda i, h: (h, 0, 0)), full((1, QK_PAD)), full((1, QK_PAD))]


def _mla_qkv_fwd(proj, cos, sin_signed, qa_g, kva_g, q_b, kv_b, q_g, k_g, tm):
    T = proj.shape[0]

    def body(p_ref, cos_ref, sin_ref, qag_ref, kvag_ref, qb_ref, kvb_ref, qg_ref, kg_ref, q_ref, k_ref, v_ref, qn_s, kvn_s):
        @pl.when(pl.program_id(1) == 0)
        def _():
            qhat, _, kvhat, _ = _latents(p_ref, qag_ref, kvag_ref)
            qn_s[...] = (qhat * qag_ref[...]).astype(BF16)
            kvn_s[...] = (kvhat * kvag_ref[...]).astype(BF16)

        cos, sin = cos_ref[...], sin_ref[...]
        qhat, _ = _head_norm(_dot(qn_s[...], qb_ref[...]))
        q_ref[...] = _rope(qhat * qg_ref[...], cos, sin).astype(BF16)
        kv = _dot(kvn_s[...], kvb_ref[...])
        khat, _ = _head_norm(jnp.concatenate([kv[:, :QK_NOPE], p_ref[:, ODD_IN_PAD - LANES:]], axis=1))
        k_ref[...] = _rope(khat * kg_ref[...], cos, sin).astype(BF16)
        v_ref[...] = kv[:, QK_NOPE:].astype(BF16)

    qk_spec = BS((None, tm, QK_PAD), lambda i, h: (h, i, 0))
    return _call(body, name="mla_qkv_fwd", grid=(T // tm, HEADS), in_specs=_mla_specs(tm),
                 out_specs=[qk_spec, qk_spec, BS((None, tm, V_DIM), lambda i, h: (h, i, 0))],
                 out_shape=[_sds((HEADS, T, QK_PAD), BF16), _sds((HEADS, T, QK_PAD), BF16), _sds((HEADS, T, V_DIM), BF16)],
                 scratch=[pltpu.VMEM((tm, Q_LORA), BF16), pltpu.VMEM((tm, KV_LORA), BF16)])(
        proj, cos, sin_signed, qa_g, kva_g, q_b, kv_b, q_g, k_g)


def _mla_qkv_bwd(proj, cos, sin_signed, qa_g, kva_g, q_b, kv_b, q_g, k_g, dq, dk, dv, dz_pool, tm):
    T = proj.shape[0]
    n_tiles = T // tm

    def body(p_ref, cos_ref, sin_ref, qag_ref, kvag_ref, qb_ref, kvb_ref, qg_ref, kg_ref, dq_ref, dk_ref, dv_ref, dzp_ref,
             dp_ref, dqb_ref, dkvb_ref, dqg_ref, dkg_ref, dqag_ref, dkvag_ref, qn_s, kvn_s, dqn_s, dkvn_s, dkr_s):
        i, h = pl.program_id(0), pl.program_id(1)

        @pl.when((i == 0) & (h == 0))
        def _():
            for ref in (dqb_ref, dkvb_ref, dqg_ref, dkg_ref, dqag_ref, dkvag_ref):
                ref[...] = jnp.zeros_like(ref)

        @pl.when(h == 0)
        def _():
            qhat, _, kvhat, _ = _latents(p_ref, qag_ref, kvag_ref)
            qn_s[...] = (qhat * qag_ref[...]).astype(BF16)
            kvn_s[...] = (kvhat * kvag_ref[...]).astype(BF16)
            dqn_s[...] = jnp.zeros_like(dqn_s)
            dkvn_s[...] = jnp.zeros_like(dkvn_s)
            dkr_s[...] = jnp.zeros_like(dkr_s)

        cos, sin = cos_ref[...], sin_ref[...]
        qhat, rq = _head_norm(_dot(qn_s[...], qb_ref[...]))
        dqn_head = _rope_transposed(dq_ref[...], cos, sin)
        dqg_ref[...] += jnp.sum(dqn_head * qhat, axis=0, keepdims=True)
        dqh = _head_norm_bwd(dqn_head, qhat, rq, qg_ref[...])
        dqb_ref[h] += _dot(qn_s[...], dqh, "tn")
        dqn_s[...] += _dot(dqh, qb_ref[...], "nt")

        kv = _dot(kvn_s[...], kvb_ref[...])
        khat, rk = _head_norm(jnp.concatenate([kv[:, :QK_NOPE], p_ref[:, ODD_IN_PAD - LANES:]], axis=1))
        dkn_head = _rope_transposed(dk_ref[...], cos, sin)
        dkg_ref[...] += jnp.sum(dkn_head * khat, axis=0, keepdims=True)
        dkf = _head_norm_bwd(dkn_head, khat, rk, kg_ref[...])
        dkr_s[...] += dkf[:, QK_NOPE:]
        dkv = jnp.concatenate([dkf[:, :QK_NOPE], dv_ref[...]], axis=1)
        dkvb_ref[h] += _dot(kvn_s[...], dkv, "tn")
        dkvn_s[...] += _dot(dkv, kvb_ref[...], "nt")

        @pl.when(h == HEADS - 1)
        def _():
            qhat_l, rql, kvhat_l, rkvl = _latents(p_ref, qag_ref, kvag_ref)
            dqn, dkvn = dqn_s[...], dkvn_s[...]
            dqag_ref[...] += jnp.sum(dqn * qhat_l, axis=0, keepdims=True)
            dkvag_ref[...] += jnp.sum(dkvn * kvhat_l, axis=0, keepdims=True)
            dqx, dkvx = dqn * qag_ref[...], dkvn * kvag_ref[...]
            dp_ref[:, :POOL_WIDTH] = dzp_ref[...]
            dp_ref[:, POOL_WIDTH:POOL_WIDTH + Q_LORA] = (
                rql * (dqx - qhat_l * jnp.mean(dqx * qhat_l, axis=-1, keepdims=True))).astype(BF16)
            dp_ref[:, POOL_WIDTH + Q_LORA:ODD_IN_PAD - LANES] = (
                rkvl * (dkvx - kvhat_l * jnp.mean(dkvx * kvhat_l, axis=-1, keepdims=True))).astype(BF16)
            dp_ref[:, ODD_IN_PAD - LANES:] = dkr_s[:, :QK_ROPE].astype(BF16)

    full = lambda shape: BS(shape, lambda i, h: (0,) * len(shape))
    qk_spec = BS((None, tm, QK_PAD), lambda i, h: (h, i, 0))
    return _call(body, name="mla_qkv_bwd", grid=(n_tiles, HEADS),
                 in_specs=_mla_specs(tm) + [qk_spec, qk_spec, BS((None, tm, V_DIM), lambda i, h: (h, i, 0)),
                                            BS((tm, POOL_WIDTH), lambda i, h: (i, 0))],
                 out_specs=[BS((tm, ODD_IN), lambda i, h: (i, 0)), full((HEADS, Q_LORA, QK_PAD)), full((HEADS, KV_LORA, QK_PAD)),
                            full((1, QK_PAD)), full((1, QK_PAD)), full((1, Q_LORA)), full((1, KV_LORA))],
                 out_shape=[_sds((T, ODD_IN), BF16),_sds((HEADS, Q_LORA, QK_PAD), F32), _sds((HEADS, KV_LORA, QK_PAD), F32),
                            _sds((1, QK_PAD), F32), _sds((1, QK_PAD), F32), _sds((1, Q_LORA), F32), _sds((1, KV_LORA), F32)],
                 scratch=[pltpu.VMEM((tm, Q_LORA), BF16), pltpu.VMEM((tm, KV_LORA), BF16), pltpu.VMEM((tm, Q_LORA), F32),
                          pltpu.VMEM((tm, KV_LORA), F32), pltpu.VMEM((tm, LANES), F32)])(
        proj, cos, sin_signed, qa_g, kva_g, q_b, kv_b, q_g, k_g, dq, dk, dv, dz_pool)


def _attn_tile(seq):
    return 512 if seq % 512 == 0 else seq


def _causal_mask(s):
    row = lax.broadcasted_iota(jnp.int32, s.shape, 0)
    col = lax.broadcasted_iota(jnp.int32, s.shape, 1)
    return jnp.where(row >= col, s, NEG_INF)


def _rows(i, t):
    return pl.ds(pl.multiple_of(i * t, t), t)


def _flash_fwd(q, k, v, mix, batch, seq):
    t = _attn_tile(seq)
    nq = seq // t

    def body(q_ref, k_ref, v_ref, _, o_ref, lse_ref, m_s, l_s, acc_s):
        qi = pl.program_id(2)
        qv = q_ref[...]
        m_s[...] = jnp.full_like(m_s, NEG_INF)
        l_s[...] = jnp.zeros_like(l_s)
        acc_s[...] = jnp.zeros_like(acc_s)

        def step(kb, masked):
            s = _dot(qv, k_ref[_rows(kb, t), :], "nt") * ATTN_SCALE
            if masked:
                s = _causal_mask(s)
            m_prev = m_s[...]
            m_new = jnp.maximum(m_prev, jnp.max(s, axis=-1, keepdims=True))
            alpha = jnp.exp(m_prev - m_new)
            p = jnp.exp(s - m_new)
            l_s[...] = alpha * l_s[...] + jnp.sum(p, axis=-1, keepdims=True)
            acc_s[...] = alpha * acc_s[...] + _dot(p, v_ref[_rows(kb, t), :])
            m_s[...] = m_new

        def loop_body(kb, carry):
            step(kb, False)
            return carry

        lax.fori_loop(0, qi, loop_body, 0)
        step(qi, True)
        o_ref[...] = (acc_s[...] / l_s[...]).astype(BF16)
        lse_ref[...] = jnp.broadcast_to(m_s[...] + jnp.log(l_s[...]), (t, LANES))

    T = batch * seq
    return _call(body, name="flash_fwd", grid=(batch, HEADS, nq),
                 in_specs=[BS((None, t, QK_PAD), lambda b, h, i: (h, b * nq + i, 0)), BS((None, seq, QK_PAD), lambda b, h, i: (h, b, 0)),
                           BS((None, seq, V_DIM), lambda b, h, i: (h, b, 0)), pl.BlockSpec(memory_space=pl.ANY)],
                 out_specs=[BS((t, V_DIM), lambda b, h, i: (b * nq + i, POOL_WIDTH // V_DIM + h)),
                            BS((None, t, LANES), lambda b, h, i: (h, b * nq + i, 0))],
                 out_shape=[_sds((T, D_MODEL), BF16), _sds((HEADS, T, LANES), F32)],
                 scratch=[pltpu.VMEM((t, 1), F32), pltpu.VMEM((t, 1), F32), pltpu.VMEM((t, V_DIM), F32)],
                 aliases={3: 0})(q, k, v, mix)


def _flash_bwd_dq(q, k, v, dmix, mix, lse, batch, seq):
    t = _attn_tile(seq)
    nq = seq // t

    def body(q_ref, k_ref, v_ref, do_ref, o_ref, lse_ref, dq_ref, delta_ref, acc_s):
        qi = pl.program_id(2)
        qv, do = q_ref[...], do_ref[...]
        delta = jnp.sum(do.astype(F32) * o_ref[...].astype(F32), axis=-1, keepdims=True)
        delta_ref[...] = jnp.broadcast_to(delta, (t, LANES))
        lse_col = lse_ref[:, 0:1]
        acc_s[...] = jnp.zeros_like(acc_s)

        def step(kb, masked):
            kk = k_ref[_rows(kb, t), :]
            s = _dot(qv, kk, "nt") * ATTN_SCALE
            if masked:
                s = _causal_mask(s)
            p = jnp.exp(s - lse_col)
            ds = p * (_dot(do, v_ref[_rows(kb, t), :], "nt") - delta) * ATTN_SCALE
            acc_s[...] += _dot(ds, kk)

        def loop_body(kb, carry):
            step(kb, False)
            return carry

        lax.fori_loop(0, qi, loop_body, 0)
        step(qi, True)
        dq_ref[...] = acc_s[...]

    T = batch * seq
    head_cols = BS((t, V_DIM), lambda b, h, i: (b * nq + i, POOL_WIDTH // V_DIM + h))
    tile = lambda w: BS((None, t, w), lambda b, h, i: (h, b * nq + i, 0))
    return _call(body, name="flash_bwd_dq", grid=(batch, HEADS, nq),
                 in_specs=[tile(QK_PAD), BS((None, seq, QK_PAD), lambda b, h, i: (h, b, 0)), BS((None, seq, V_DIM), lambda b, h, i: (h, b, 0)),
                           head_cols, head_cols, tile(LANES)],
                 out_specs=[tile(QK_PAD), tile(LANES)],
                 out_shape=[_sds((HEADS, T, QK_PAD), F32), _sds((HEADS, T, LANES), F32)],
                 scratch=[pltpu.VMEM((t, QK_PAD), F32)])(q, k, v, dmix, mix, lse)


def _flash_bwd_dkv(q, k, v, dmix, lse, delta, batch, seq):
    t = _attn_tile(seq)
    nq = seq // t

    def body(q_ref, k_ref, v_ref, do_ref, lse_ref, delta_ref, dk_ref, dv_ref, dk_s, dv_s):
        ki = pl.program_id(2)
        kk, vv = k_ref[...], v_ref[...]
        dk_s[...] = jnp.zeros_like(dk_s)
        dv_s[...] = jnp.zeros_like(dv_s)

        def step(qb, masked):
            rows = _rows(qb, t)
            qv, do = q_ref[rows, :], do_ref[rows, :]
            s = _dot(qv, kk, "nt") * ATTN_SCALE
            if masked:
                s = _causal_mask(s)
            p = jnp.exp(s - lse_ref[rows, 0:1])
            dv_s[...] += _dot(p, do, "tn")
            ds = p * (_dot(do, vv, "nt") - delta_ref[rows, 0:1]) * ATTN_SCALE
            dk_s[...] += _dot(ds, qv, "tn")

        def loop_body(qb, carry):
            step(qb, False)
            return carry

        step(ki, True)
        lax.fori_loop(ki + 1, nq, loop_body, 0)
        dk_ref[...] = dk_s[...]
        dv_ref[...] = dv_s[...]

    T = batch * seq
    tile = lambda w: BS((None, t, w), lambda b, h, i: (h, b * nq + i, 0))
    whole = lambda w: BS((None, seq, w), lambda b, h, i: (h, b, 0))
    return _call(body, name="flash_bwd_dkv", grid=(batch, HEADS, nq),
                 in_specs=[whole(QK_PAD), tile(QK_PAD), tile(V_DIM), BS((seq, V_DIM), lambda b, h, i: (b, POOL_WIDTH // V_DIM + h)),
                           whole(LANES), whole(LANES)],
                 out_specs=[tile(QK_PAD), tile(V_DIM)],
                 out_shape=[_sds((HEADS, T, QK_PAD), F32), _sds((HEADS, T, V_DIM), F32)],
                 scratch=[pltpu.VMEM((t, QK_PAD), F32), pltpu.VMEM((t, V_DIM), F32)])(q, k, v, dmix, lse, delta)


def _loss_head(y, target, tm):
    T, d = y.shape

    def body(y_ref, t_ref, dy_ref, sq_ref):
        @pl.when(pl.program_id(0) == 0)
        def _():
            sq_ref[...] = jnp.zeros_like(sq_ref)

        e = y_ref[...] - t_ref[...]
        sq_ref[...] += jnp.sum(e * e)
        dy_ref[...] = e * (1.0 / d)

    row = BS((tm, d), lambda i: (i, 0))
    return _call(body, name="loss_head", grid=(T // tm,), in_specs=[row, row],
                 out_specs=[row, BS((SUBLANES, LANES), lambda i: (0, 0))],
                 out_shape=[_sds((T, d), F32), _sds((SUBLANES, LANES), F32)])(y, target)


def _adamw_math(w, g, m, v):
    m = ADAM_B1 * m + (1.0 - ADAM_B1) * g
    v = ADAM_B2 * v + (1.0 - ADAM_B2) * (g * g)
    m_hat = m / (1.0 - ADAM_B1 ** ADAM_STEP)
    v_hat = v / (1.0 - ADAM_B2 ** ADAM_STEP)
    return -ADAM_LR * (m_hat / (jnp.sqrt(v_hat) + ADAM_EPS) + ADAM_WD * w), m, v


def _adamw(name, w, g, m, v):
    L, R, C = w.shape
    tr = 256 if R % 256 == 0 else R
    outs = None
    for l in range(L):
        def body(w_ref, g_ref, m_ref, v_ref, *rest):
            go_ref, d_ref, mo_ref, vo_ref = rest[-4:]
            gv = g_ref[...]
            d_ref[...], mo_ref[...], vo_ref[...] = _adamw_math(w_ref[...], gv, m_ref[...], v_ref[...])
            go_ref[...] = gv

        layer = BS((None, tr, C), functools.partial(lambda l, i: (l, i, 0), l))
        prev = [] if outs is None else list(outs)
        outs = _call(body, name=f"{name}_{l}", grid=(R // tr,),
                     in_specs=[layer, BS((tr, C), lambda i: (i, 0)), layer, layer] + [pl.BlockSpec(memory_space=pl.ANY)] * len(prev),
                     out_specs=[layer] * 4, out_shape=[_sds((L, R, C), F32)] * 4,
                     aliases={4 + n: n for n in range(len(prev))})(w, g[l], m, v, *prev)
    return outs


def _place():
    x, y, c = lax.axis_index("x"), lax.axis_index("y"), lax.axis_index("c")
    other_chips = [(1 - x, y), (x, 1 - y), (1 - x, 1 - y)]
    return x, y, c, other_chips


def _remote(src, dst, send_sem, recv_sem, dev):
    return pltpu.make_async_remote_copy(src_ref=src, dst_ref=dst, send_sem=send_sem, recv_sem=recv_sem,
                                        device_id=dev, device_id_type=MESH)


def _prefetch_call(body, *, name, grid, in_specs, out_specs, out_shape):
    grid_spec = pltpu.PrefetchScalarGridSpec(num_scalar_prefetch=1, grid=grid, in_specs=in_specs, out_specs=out_specs)
    params = pltpu.CompilerParams(vmem_limit_bytes=VMEM_LIMIT_V7X, dimension_semantics=("arbitrary",) * len(grid))
    return pl.pallas_call(body, name=name, grid_spec=grid_spec, out_shape=out_shape, compiler_params=params)


def _row_tile(rows):
    return 256 if rows % 256 == 0 else rows


def _cast_place(name, w, chip):
    _, rows, C = w.shape
    tr = _row_tile(rows)

    def body(chip_ref, w_ref, o_ref):
        o_ref[...] = w_ref[...].astype(BF16)

    return _prefetch_call(body, name=name, grid=(2, rows // tr),
                          in_specs=[BS((None, tr, C), lambda h, i, chip_ref: (h, i, 0))],
                          out_specs=BS((None, None, tr, C), lambda h, i, chip_ref: (chip_ref[0], h, i, 0)),
                          out_shape=_sds((N_CHIPS, 2, rows, C), BF16))(chip, w)


def _all_gather_chips(gs):
    n = len(gs)

    def body(*refs):
        g, send_sems, recv_sems = refs[n:2 * n], refs[-2], refs[-1]
        x, y, c, chips = _place()
        me, sibling = 2 * x + y, (x, y, 1 - c)
        sends = [_remote(g[i].at[me, c], g[i].at[me, c], send_sems.at[6 * i + k], recv_sems.at[6 * i + k], (px, py, c))
                 for i in range(n) for k, (px, py) in enumerate(chips)]
        for cp in sends:
            cp.start()
        passed = []
        for i in range(n):
            for k, (px, py) in enumerate(chips):
                landed = g[i].at[2 * px + py, c]
                _remote(landed, landed, send_sems.at[6 * i + k], recv_sems.at[6 * i + k], (px, py, c)).wait_recv()
                passed.append(_remote(landed, landed, send_sems.at[6 * i + 3 + k], recv_sems.at[6 * i + 3 + k], sibling))
                passed[-1].start()
        for i in range(n):
            for k, (px, py) in enumerate(chips):
                theirs = g[i].at[2 * px + py, 1 - c]
                _remote(theirs, theirs, send_sems.at[6 * i + 3 + k], recv_sems.at[6 * i + 3 + k], sibling).wait_recv()
        for cp in sends + passed:
            cp.wait_send()

    return _call(body, name="all_gather_weights", in_specs=[HBM] * n, out_specs=[HBM] * n,
                 out_shape=[_sds(a.shape, a.dtype) for a in gs], aliases={i: i for i in range(n)},
                 scratch=[pltpu.SemaphoreType.DMA((6 * n,)), pltpu.SemaphoreType.DMA((6 * n,))])(*gs)


def _sibling_exchange(gs):
    n = len(gs)

    def body(*refs):
        g, r, send_sems, recv_sems = refs[:n], refs[n:2 * n], refs[-2], refs[-1]
        x, y, c, _ = _place()
        copies = [_remote(g[i].at[:, 1 - c], r[i], send_sems.at[i], recv_sems.at[i], (x, y, 1 - c)) for i in range(n)]
        for cp in copies:
            cp.start()
        for cp in copies:
            cp.wait()

    return _call(body, name="grad_sibling_exchange", in_specs=[HBM] * n, out_specs=[HBM] * n,
                 out_shape=[_sds((a.shape[0],) + a.shape[2:], a.dtype) for a in gs],
                 scratch=[pltpu.SemaphoreType.DMA((n,)), pltpu.SemaphoreType.DMA((n,))])(*gs)


def _chip_scatter(ps):
    n = len(ps)

    def body(*refs):
        p, r, send_sems, recv_sems = refs[:n], refs[n:2 * n], refs[-2], refs[-1]
        x, y, c, chips = _place()
        sends = [_remote(p[i].at[2 * px + py], r[i].at[k], send_sems.at[3 * i + k], recv_sems.at[3 * i + k], (px, py, c))
                 for i in range(n) for k, (px, py) in enumerate(chips)]
        for cp in sends:
            cp.start()
        for i in range(n):
            for k, (px, py) in enumerate(chips):
                _remote(r[i].at[k], r[i].at[k], send_sems.at[3 * i + k], recv_sems.at[3 * i + k], (px, py, c)).wait_recv()
        for cp in sends:
            cp.wait_send()

    return _call(body, name="grad_chip_scatter", in_specs=[HBM] * n, out_specs=[HBM] * n,
                 out_shape=[_sds((N_CHIPS - 1,) + a.shape[1:], a.dtype) for a in ps],
                 scratch=[pltpu.SemaphoreType.DMA((3 * n,)), pltpu.SemaphoreType.DMA((3 * n,))])(*ps)


def _sibling_share(fs):
    n = len(fs)

    def body(*refs):
        f, send_sems, recv_sems = refs[n:2 * n], refs[-2], refs[-1]
        x, y, c, _ = _place()
        sends = [_remote(f[i].at[c], f[i].at[c], send_sems.at[i], recv_sems.at[i], (x, y, 1 - c)) for i in range(n)]
        for cp in sends:
            cp.start()
        for i in range(n):
            theirs = f[i].at[1 - c]
            _remote(theirs, theirs, send_sems.at[i], recv_sems.at[i], (x, y, 1 - c)).wait_recv()
        for cp in sends:
            cp.wait_send()

    return _call(body, name="grad_sibling_share", in_specs=[HBM] * n, out_specs=[HBM] * n,
                 out_shape=[_sds(a.shape, a.dtype) for a in fs], aliases={i: i for i in range(n)},
                 scratch=[pltpu.SemaphoreType.DMA((n,)), pltpu.SemaphoreType.DMA((n,))])(*fs)


def _all_reduce_small(name, v):
    n_dev = 8
    flips = [(fx, fy, fc) for fx in (0, 1) for fy in (0, 1) for fc in (0, 1)][1:]

    def body(v_ref, o_ref, buf, send_sems, recv_sems):
        x, y, c, _ = _place()
        peers = [(1 - x if fx else x, 1 - y if fy else y, 1 - c if fc else c) for fx, fy, fc in flips]
        me = 4 * x + 2 * y + c
        buf[me] = v_ref[...]
        sends = [_remote(v_ref, buf.at[me], send_sems.at[k], recv_sems.at[k], peer) for k, peer in enumerate(peers)]
        for cp in sends:
            cp.start()
        for k, (px, py, pc) in enumerate(peers):
            theirs = buf.at[4 * px + 2 * py + pc]
            _remote(v_ref, theirs, send_sems.at[k], recv_sems.at[k], (px, py, pc)).wait_recv()
        for cp in sends:
            cp.wait_send()
        acc = buf[0]
        for d in range(1, n_dev):
            acc = acc + buf[d]
        o_ref[...] = acc

    return _call(body, name=name, in_specs=[VMEM], out_specs=VMEM, out_shape=_sds(v.shape, F32),
                 scratch=[pltpu.VMEM((n_dev,) + v.shape, F32), pltpu.SemaphoreType.DMA((7,)), pltpu.SemaphoreType.DMA((7,))])(v)


def _add_halves(name, g, r, c):
    _, _, rows, C = g.shape
    tr = _row_tile(rows)

    def body(c_ref, g_ref, r_ref, o_ref):
        o_ref[...] = (g_ref[...] + r_ref[...]).astype(BF16)

    spec = BS((None, tr, C), lambda j, i, c_ref: (j, i, 0))
    return _prefetch_call(body, name=name, grid=(N_CHIPS, rows // tr),
                          in_specs=[BS((None, None, tr, C), lambda j, i, c_ref: (j, c_ref[0], i, 0)), spec], out_specs=spec,
                          out_shape=_sds((N_CHIPS, rows, C), BF16))(c, g, r)


def _sum_partials(name, p, r, chip_c):
    _, rows, C = p.shape
    tr = _row_tile(rows)

    def body(s_ref, p_ref, r_ref, o_ref):
        acc = p_ref[...].astype(F32)
        for k in range(N_CHIPS - 1):
            acc = acc + r_ref[k].astype(F32)
        o_ref[...] = acc

    return _prefetch_call(body, name=name, grid=(rows // tr,),
                          in_specs=[BS((None, tr, C), lambda i, s: (s[0], i, 0)), BS((N_CHIPS - 1, tr, C), lambda i, s: (0, i, 0))],
                          out_specs=BS((None, tr, C), lambda i, s: (s[1], i, 0)), out_shape=_sds((2, rows, C), F32))(chip_c, p, r)


_SHARDED = ("even_w_in", "even_w_out", "odd_w_in", "q_b", "kv_b", "odd_w_out", "ffn_w_gate", "ffn_w_up", "ffn_w_down")
_REPLICATED = ("mix_norm", "ffn_norm", "sg_ln_g", "sg_w_s", "sg_b_s", "pool_w", "q_norm", "k_norm")
_SMALL_SHARDED = ("sc_conv_w", "pool_scale", "q_a_norm", "kv_a_norm")
_WEIGHTS = ("mix_norm", "ffn_norm", "even_w_in", "sg_ln_g", "sg_w_s", "sg_b_s", "sc_conv_w", "even_w_out", "odd_w_in", "pool_w",
            "pool_scale", "q_a_norm", "q_b", "kv_a_norm", "kv_b", "q_norm", "k_norm", "odd_w_out", "ffn_w_gate", "ffn_w_up",
            "ffn_w_down")


def _pad_rows(flat, width, align):
    n = flat.shape[0]
    rows = -(-n // (width * align)) * align
    return jnp.pad(flat, (0, rows * width - n)).reshape(rows, width)


def _gather_weights(shards, chip):
    halves = []
    for n in _SHARDED:
        a = shards[n]
        halves.append(a if a.shape[0] == 2 else a.reshape(2, a.shape[1] // 2, a.shape[2]))
    placed = [_cast_place(f"place_{n}", a, chip) for n, a in zip(_SHARDED, halves)]
    out = dict(zip(_SHARDED, _all_gather_chips(placed)))
    for n in ("even_w_in", "even_w_out", "odd_w_in", "q_b", "kv_b", "odd_w_out"):
        out[n] = out[n].reshape(N_CHIPS, -1, out[n].shape[-1])
    for n in ("q_b", "kv_b"):
        out[n] = out[n].transpose(1, 0, 2).reshape(out[n].shape[1], -1)
    for n in ("even_w_out", "odd_w_in", "odd_w_out"):
        out[n] = out[n].reshape(-1, out[n].shape[-1])
    return out


def _forward_backward(x, positions, target, W, small):
    batch, seq, _ = x.shape
    T = batch * seq
    tm = _token_tile(seq)
    x0 = x.reshape(T, D_MODEL)

    inv_freq = ROPE_THETA ** (-jnp.arange(0, QK_ROPE, 2, dtype=F32) / QK_ROPE)
    ang = (positions.astype(F32)[..., None] * inv_freq).reshape(T, QK_ROPE // 2)
    cos, sin = jnp.cos(ang), jnp.sin(ang)
    pad = jnp.zeros((T, LANES - QK_ROPE), F32)
    cos_t = jnp.concatenate([cos, cos, pad], axis=1)
    sin_t = jnp.concatenate([-sin, sin, pad], axis=1)

    tril = jnp.tril(jnp.ones((SG_CHUNK, SG_CHUNK), bool))
    w_tril = jnp.where(tril[None], small["sg_w_s"][0], 0.0).astype(BF16)
    b_lanes = jnp.broadcast_to(small["sg_b_s"][0][:, :, None], (SG_HEADS, SG_CHUNK, SG_DIM))
    conv_w = jnp.pad(small["sc_conv_w"][0], ((0, SUBLANES - CONV_TAPS), (0, 0)))
    ln_g = small["sg_ln_g"]
    pool_diag = jnp.zeros((POOL_WIDTH, POOL_WIDTH), F32)
    for g in range(len(POOL_WINDOWS)):
        pool_diag = pool_diag.at[POOL_DIM * g:POOL_DIM * (g + 1), POOL_DIM * g:POOL_DIM * (g + 1)].set(small["pool_w"][0, g])
    pool_diag = pool_diag.astype(BF16)
    pool_scale = small["pool_scale"]
    w_in_odd = jnp.pad(W["odd_w_in"], ((0, 0), (0, ODD_IN_PAD - ODD_IN)))
    q_b = jnp.pad(W["q_b"].reshape(Q_LORA, HEADS, QK_DIM).transpose(1, 0, 2), ((0, 0), (0, 0), (0, QK_PAD - QK_DIM)))
    kv_b = W["kv_b"].reshape(KV_LORA, HEADS, QK_NOPE + V_DIM).transpose(1, 0, 2)
    q_g = jnp.pad(small["q_norm"], ((0, 0), (0, QK_PAD - QK_DIM)))
    k_g = jnp.pad(small["k_norm"], ((0, 0), (0, QK_PAD - QK_DIM)))
    qa_g, kva_g = small["q_a_norm"], small["kv_a_norm"]
    ffn = [(small["ffn_norm"][l], W["ffn_w_gate"], W["ffn_w_up"], W["ffn_w_down"]) for l in range(2)]
    w_in_even = W["even_w_in"]
    in_shard = EVEN_IN // N_CHIPS

    h0 = _rmsnorm_fwd("mix0_norm", x0, small["mix_norm"][0], tm)
    tb = _big_tile(T)
    proj0 = _matmul("even_in", "nn", [(h0, w_in_even)],
                    [(BS((tb, D_MODEL), lambda i, j, k: (i, 0)), BS((None, D_MODEL, in_shard), lambda i, j, k: (j, 0, 0)))],
                    (T // tb, N_CHIPS, 1), _sds((T, EVEN_IN), F32), BS((tb, in_shard), lambda i, j, k: (i, j)), (tb, in_shard))
    mix0 = _even_mixer_fwd(proj0, ln_g, w_tril, b_lanes, conv_w, seq, tm)
    x1 = _mm("even_out", "nn", mix0, W["even_w_out"], F32, tk=1024, add=x0)
    x2, ffn0_saved = _ffn_fwd(0, x1, *ffn[0], tm)
    h2 = _rmsnorm_fwd("mix1_norm", x2, small["mix_norm"][1], tm)
    proj1 = _mm("odd_in", "nn", h2, w_in_odd, F32, tk=1024)
    mix1 = _pool_fwd(proj1, pool_diag, pool_scale, seq, tm)
    q, k, v = _mla_qkv_fwd(proj1, cos_t, sin_t, qa_g, kva_g, q_b, kv_b, q_g, k_g, tm)
    mix1, lse = _flash_fwd(q, k, v, mix1, batch, seq)
    x3 = _mm("odd_out", "nn", mix1, W["odd_w_out"], F32, tk=1024, add=x2)
    x4, ffn1_saved = _ffn_fwd(1, x3, *ffn[1], tm)
    dy, sq = _loss_head(x4, target.reshape(T, D_MODEL), tm)

    G = {}
    dx3, dffn_g1, dwg1, dwu1, dwd1 = _ffn_bwd(1, x3, *ffn[1], ffn1_saved, dy, tm)
    dmix1 = _mm("odd_out_dx", "nt", dx3, W["odd_w_out"], BF16, tk=1024)
    G["odd_w_out"] = _mm("odd_out_dw", "tn", mix1, dx3, F32)
    dq, delta = _flash_bwd_dq(q, k, v, dmix1, mix1, lse, batch, seq)
    dk, dv = _flash_bwd_dkv(q, k, v, dmix1, lse, delta, batch, seq)
    dz_pool, dpool_diag, G["pool_scale"] = _pool_bwd(proj1, dmix1, pool_diag, pool_scale, seq, tm)
    dproj1, dq_b, dkv_b, dq_g, dk_g, G["q_a_norm"], G["kv_a_norm"] = _mla_qkv_bwd(
        proj1, cos_t, sin_t, qa_g, kva_g, q_b, kv_b, q_g, k_g, dq, dk, dv, dz_pool, tm)
    G["pool_w"] = jnp.stack([dpool_diag[POOL_DIM * g:POOL_DIM * (g + 1), POOL_DIM * g:POOL_DIM * (g + 1)]
                             for g in range(len(POOL_WINDOWS))])[None]
    G["q_b"] = dq_b[:, :, :QK_DIM].transpose(1, 0, 2).reshape(Q_LORA, HEADS * QK_DIM)
    G["kv_b"] = dkv_b.transpose(1, 0, 2).reshape(KV_LORA, HEADS * (QK_NOPE + V_DIM))
    G["q_norm"], G["k_norm"] = dq_g[:, :QK_DIM], dk_g[:, :QK_DIM]
    dh2 = _mm("odd_in_dx", "nt", dproj1, W["odd_w_in"], F32, tk=ODD_IN)
    G["odd_w_in"] = _mm("odd_in_dw", "tn", h2, dproj1, F32, tn=ODD_IN)
    dx2, dmix_g1 = _rmsnorm_bwd("mix1_norm_bwd", x2, small["mix_norm"][1], dh2, dx3, tm)
    dx1, dffn_g0, dwg0, dwu0, dwd0 = _ffn_bwd(0, x1, *ffn[0], ffn0_saved, dx2, tm)
    dmix0 = _mm("even_out_dx", "nt", dx1, W["even_w_out"], F32, tk=1024)
    G["even_w_out"] = _mm("even_out_dw", "tn", mix0, dx1, F32)
    dproj0, dw_s, db_lanes, G["sg_ln_g"], dconv = _even_mixer_bwd(proj0, dmix0, ln_g, w_tril, b_lanes, conv_w, seq, tm)
    G["sg_w_s"] = dw_s[None]
    G["sg_b_s"] = jnp.sum(db_lanes, axis=-1)[None]
    G["sc_conv_w"] = dconv[None, :CONV_TAPS]
    dh0 = _matmul("even_in_dx", "nt", [(dproj0, w_in_even)],
                  [(BS((tb, in_shard), lambda i, j, k: (i, k)), BS((None, D_MODEL, in_shard), lambda i, j, k: (k, 0, 0)))],
                  (T // tb, 1, N_CHIPS), _sds((T, D_MODEL), F32), BS((tb, D_MODEL), lambda i, j, k: (i, 0)), (tb, D_MODEL))
    tk = min(512, T)
    G["even_w_in"] = _grad_shards(
        "even_in_dw", h0, dproj0, BS((tk, D_MODEL), lambda k: (k, 0)), BS((tk, EVEN_IN), lambda k: (k, 0)),
        lambda a_ref, b_ref, j: (a_ref[...], b_ref[:, in_shard * j:in_shard * (j + 1)]), (N_CHIPS, D_MODEL, in_shard), T // tk)
    dx0, dmix_g0 = _rmsnorm_bwd("mix0_norm_bwd", x0, small["mix_norm"][0], dh0, dx1, tm)
    G["mix_norm"] = jnp.concatenate([dmix_g0, dmix_g1], axis=0)
    G["ffn_norm"] = jnp.concatenate([dffn_g0, dffn_g1], axis=0)
    G["ffn"] = [(dwg0, dwu0, dwd0), (dwg1, dwu1, dwd1)]
    return sq[0, 0], dx0.reshape(batch, seq, D_MODEL), G


def _small_vector(parts, names):
    flat = jnp.concatenate([parts[n].astype(F32).reshape(-1) for n in names])
    return _pad_rows(flat, LANES, SUBLANES)


def _split_small(vec, like, names):
    out, off, flat = {}, 0, vec.reshape(-1)
    for n in names:
        size = math.prod(like[n].shape)
        out[n] = flat[off:off + size].reshape(like[n].shape)
        off += size
    return out


def _whole_shape(a):
    return a.shape[:-1] + (a.shape[-1] * N_CHIPS,)


def kernel(x, positions, mix_norm, ffn_norm, even_w_in, sg_ln_g, sg_w_s, sg_b_s, sc_conv_w, even_w_out, odd_w_in, pool_w, pool_scale, q_a_norm, q_b, kv_a_norm, kv_b, q_norm, k_norm, odd_w_out, ffn_w_gate, ffn_w_up, ffn_w_down, loss_target, m_mix_norm, m_ffn_norm, m_even_w_in, m_sg_ln_g, m_sg_w_s, m_sg_b_s, m_sc_conv_w, m_even_w_out, m_odd_w_in, m_pool_w, m_pool_scale, m_q_a_norm, m_q_b, m_kv_a_norm, m_kv_b, m_q_norm, m_k_norm, m_odd_w_out, m_ffn_w_gate, m_ffn_w_up, m_ffn_w_down, v_mix_norm, v_ffn_norm, v_even_w_in, v_sg_ln_g, v_sg_w_s, v_sg_b_s, v_sc_conv_w, v_even_w_out, v_odd_w_in, v_pool_w, v_pool_scale, v_q_a_norm, v_q_b, v_kv_a_norm, v_kv_b, v_q_norm, v_k_norm, v_odd_w_out, v_ffn_w_gate, v_ffn_w_up, v_ffn_w_down):
    args = dict(locals())
    w = {n: args[n] for n in _WEIGHTS}
    m = {n: args["m_" + n] for n in _WEIGHTS}
    v = {n: args["v_" + n] for n in _WEIGHTS}
    cx, cy, cc = lax.axis_index("x"), lax.axis_index("y"), lax.axis_index("c")
    chip = 2 * cx + cy

    chip_arr = chip.astype(jnp.int32).reshape(1)
    W = _gather_weights(w, chip_arr)
    placed = {}
    for n in _SMALL_SHARDED:
        a = w[n]
        whole = jnp.zeros(a.shape[:-1] + (N_CHIPS, a.shape[-1]), F32)
        whole = lax.dynamic_update_slice_in_dim(whole, a[..., None, :], chip, axis=a.ndim - 1)
        placed[n] = jnp.where(cc == 0, whole, 0.0).reshape(_whole_shape(a))
    small = dict({n: w[n] for n in _REPLICATED},
                 **_split_small(_all_reduce_small("gather_small_weights", _small_vector(placed, _SMALL_SHARDED)), placed, _SMALL_SHARDED))

    sq, grad_x, G = _forward_backward(x, positions, loss_target, W, small)
    loss = lax.psum(0.5 * sq / D_MODEL, ("x", "y", "c"))

    small_names = _REPLICATED + _SMALL_SHARDED
    summed = _split_small(_all_reduce_small("reduce_small_grads", _small_vector(G, small_names)), G, small_names)
    grads = {n: summed[n] for n in _REPLICATED}
    for n in _SMALL_SHARDED:
        a = w[n]
        grads[n] = lax.dynamic_slice_in_dim(summed[n].reshape(a.shape[:-1] + (N_CHIPS, a.shape[-1])), chip, 1,
                                            axis=a.ndim - 1).reshape(a.shape)

    def shard_major(g, cols):
        return g.reshape(g.shape[0], N_CHIPS, cols).transpose(1, 0, 2)

    big = [("even_w_in", G["even_w_in"]),
           ("even_w_out", G["even_w_out"].reshape(N_CHIPS, -1, D_MODEL)),
           ("odd_w_in", G["odd_w_in"].reshape(N_CHIPS, -1, ODD_IN)),
           ("q_b", shard_major(G["q_b"], HEADS * QK_DIM // N_CHIPS)),
           ("kv_b", shard_major(G["kv_b"], HEADS * (QK_NOPE + V_DIM) // N_CHIPS)),
           ("odd_w_out", G["odd_w_out"].reshape(N_CHIPS, -1, D_MODEL))]
    for l in range(2):
        big += [(f"ffn_w_gate{l}", G["ffn"][l][0]), (f"ffn_w_up{l}", G["ffn"][l][1]), (f"ffn_w_down{l}", G["ffn"][l][2])]
    names = [n for n, _ in big]
    halves = [g.reshape(N_CHIPS, 2, g.shape[1] // 2, g.shape[2]) for _, g in big]
    from_sibling = _sibling_exchange(halves)
    c_arr = cc.astype(jnp.int32).reshape(1)
    partial = [_add_halves(f"add_{n}", g, r, c_arr) for n, g, r in zip(names, halves, from_sibling)]
    scattered = _chip_scatter(partial)
    chip_c = jnp.stack([chip, cc]).astype(jnp.int32)
    sums = [_sum_partials(f"sum_{n}", p, r, chip_c) for n, p, r in zip(names, partial, scattered)]
    shard_grad = {n: f.reshape(1, -1, f.shape[-1]) for n, f in zip(names, _sibling_share(sums))}

    out = {}
    for n in ("even_w_in", "even_w_out", "odd_w_in", "q_b", "kv_b", "odd_w_out"):
        out[n] = _adamw(f"adamw_{n}", w[n], [shard_grad[n][0]], m[n], v[n])
    for n in ("ffn_w_gate", "ffn_w_up", "ffn_w_down"):
        out[n] = _adamw(f"adamw_{n}", w[n], [shard_grad[f"{n}{l}"][0] for l in range(2)], m[n], v[n])
    packed = [_small_vector(d, small_names) for d in (w, grads, m, v)]
    res = _adamw("adamw_small", packed[0][None], [packed[1]], packed[2][None], packed[3][None])
    delta_s, m_s, v_s = (_split_small(r, w, small_names) for r in res[1:])
    for n in small_names:
        out[n] = (grads[n], delta_s[n], m_s[n], v_s[n])

    return (loss, grad_x, *[out[n][0] for n in _WEIGHTS], *[out[n][1] for n in _WEIGHTS],
            *[out[n][2] for n in _WEIGHTS], *[out[n][3] for n in _WEIGHTS])
```

```python
import functools
import math

import jax
import jax.numpy as jnp
from jax import lax
from jax.experimental import pallas as pl
from jax.experimental.pallas import tpu as pltpu

F32, BF16 = jnp.float32, jnp.bfloat16
BS = pl.BlockSpec

D_MODEL = 1024
EPS = 1e-6
NEG_INF = -1e30
SG_HEADS, SG_DIM, SG_WIDTH, SG_CHUNK = 4, 128, 512, 128
SC_WIDTH, CONV_TAPS = 512, 3
EVEN_IN = 2 * SG_WIDTH + 3 * SC_WIDTH
POOL_WINDOWS = (2, 4, 8, 16)
POOL_DIM, POOL_WIDTH = 64, 256
POOL_HALO = 16
HEADS, Q_LORA, KV_LORA, QK_NOPE, QK_ROPE, V_DIM = 6, 384, 256, 128, 64, 128
QK_DIM = QK_NOPE + QK_ROPE
QK_PAD = 256
ODD_IN = POOL_WIDTH + Q_LORA + KV_LORA + QK_ROPE
ODD_IN_PAD = 1024
ROPE_THETA = 10000.0
ATTN_SCALE = QK_DIM ** -0.5
D_FF, N_CHIPS = 2816, 4
FF_SHARD = D_FF // N_CHIPS
ADAM_LR, ADAM_B1, ADAM_B2, ADAM_EPS, ADAM_WD, ADAM_STEP = 0.001, 0.9, 0.999, 1e-08, 0.01, 10
VMEM_LIMIT_V7X = 48 * 2**20
LANES, SUBLANES = 128, 8
MESH = pl.DeviceIdType.MESH
HBM = pl.BlockSpec(memory_space=pltpu.HBM)
VMEM = pl.BlockSpec(memory_space=pltpu.VMEM)

_DIMS = {"nn": (((1,), (0,)), ((), ())), "nt": (((1,), (1,)), ((), ())), "tn": (((0,), (0,)), ((), ()))}


def _dot(a, b, mode="nn"):
    return lax.dot_general(a.astype(BF16), b.astype(BF16), _DIMS[mode], preferred_element_type=F32)


def _call(body, *, name, out_shape, in_specs, out_specs, grid=(), scratch=(), aliases=None):
    params = pltpu.CompilerParams(vmem_limit_bytes=VMEM_LIMIT_V7X,
                                  **({"dimension_semantics": ("arbitrary",) * len(grid)} if grid else {}))
    return pl.pallas_call(body, name=name, grid=grid, in_specs=in_specs, out_specs=out_specs, out_shape=out_shape,
                          scratch_shapes=list(scratch), input_output_aliases=aliases or {}, compiler_params=params)


def _sds(shape, dtype):
    return jax.ShapeDtypeStruct(tuple(shape), dtype)


def _token_tile(seq):
    return 512 if seq % 512 == 0 else seq


def _matmul(name, mode, pairs, pair_specs, grid, out_shape, out_spec, acc_shape, add=None, add_spec=None):
    n, nk = len(pairs), grid[-1]

    def body(*refs):
        ab = refs[:2 * n]
        add_ref = refs[2 * n] if add is not None else None

        def finish(r, o_ref):
            if add_ref is not None:
                r = r + add_ref[...]
            o_ref[...] = r.astype(o_ref.dtype)

        if nk == 1:
            r = _dot(ab[0][...], ab[1][...], mode)
            for p in range(1, n):
                r = r + _dot(ab[2 * p][...], ab[2 * p + 1][...], mode)
            finish(r, refs[-1])
            return
        o_ref, acc = refs[-2], refs[-1]
        k = pl.program_id(len(grid) - 1)

        @pl.when(k == 0)
        def _():
            acc[...] = jnp.zeros_like(acc)

        for p in range(n):
            acc[...] += _dot(ab[2 * p][...], ab[2 * p + 1][...], mode)

        @pl.when(k == nk - 1)
        def _():
            finish(acc[...], o_ref)

    ops = [t for pr in pairs for t in pr] + ([add] if add is not None else [])
    specs = [s for pr in pair_specs for s in pr] + ([add_spec] if add is not None else [])
    return _call(body, name=name, grid=grid, in_specs=specs, out_specs=out_spec, out_shape=out_shape,
                 scratch=[pltpu.VMEM(acc_shape, F32)] if nk > 1 else [])(*ops)


def _grad_shards(name, a, b, a_spec, b_spec, pick, out_shape, n_steps):
    def body(a_ref, b_ref, o_ref):
        @pl.when(pl.program_id(0) == 0)
        def _():
            o_ref[...] = jnp.zeros_like(o_ref)

        for j in range(N_CHIPS):
            aj, bj = pick(a_ref, b_ref, j)
            o_ref[j] += _dot(aj, bj, "tn")

    return _call(body, name=name, grid=(n_steps,), in_specs=[a_spec, b_spec],
                 out_specs=BS(out_shape, lambda k: (0, 0, 0)), out_shape=_sds(out_shape, F32))(a, b)


def _mm(name, mode, a, b, out_dtype, tm=1024, tn=1024, tk=512, add=None):
    if mode == "tn":
        (K, M), N = a.shape, b.shape[1]
    else:
        (M, K), N = a.shape, (b.shape[1] if mode == "nn" else b.shape[0])
    tm, tn, tk = min(tm, M), min(tn, N), min(tk, K)
    a_spec = BS((tk, tm), lambda i, j, k: (k, i)) if mode == "tn" else BS((tm, tk), lambda i, j, k: (i, k))
    b_spec = BS((tn, tk), lambda i, j, k: (j, k)) if mode == "nt" else BS((tk, tn), lambda i, j, k: (k, j))
    o_spec = BS((tm, tn), lambda i, j, k: (i, j))
    return _matmul(name, mode, [(a, b)], [(a_spec, b_spec)], (M // tm, N // tn, K // tk), _sds((M, N), out_dtype),
                   o_spec, (tm, tn), add=add, add_spec=o_spec if add is not None else None)


def _rmsnorm_fwd(name, x, g, tm):
    T, d = x.shape

    def body(x_ref, g_ref, o_ref):
        xv = x_ref[...]
        y = xv * lax.rsqrt(jnp.mean(xv * xv, axis=-1, keepdims=True) + EPS)
        o_ref[...] = (y * g_ref[...]).astype(o_ref.dtype)

    return _call(body, name=name, grid=(T // tm,), in_specs=[BS((tm, d), lambda i: (i, 0)), BS((1, d), lambda i: (0, 0))],
                 out_specs=BS((tm, d), lambda i: (i, 0)), out_shape=_sds((T, d), BF16))(x, g.reshape(1, d))


def _rmsnorm_bwd(name, x, g, dh, dres, tm):
    T, d = x.shape

    def body(x_ref, g_ref, dh_ref, dres_ref, dx_ref, dg_ref):
        xv = x_ref[...]
        r = lax.rsqrt(jnp.mean(xv * xv, axis=-1, keepdims=True) + EPS)
        xhat = xv * r
        dhv = dh_ref[...]

        @pl.when(pl.program_id(0) == 0)
        def _():
            dg_ref[...] = jnp.zeros_like(dg_ref)

        dg_ref[...] += jnp.sum(dhv * xhat, axis=0, keepdims=True)
        dxhat = dhv * g_ref[...]
        dx_ref[...] = dres_ref[...] + r * (dxhat - xhat * jnp.mean(dxhat * xhat, axis=-1, keepdims=True))

    row = BS((tm, d), lambda i: (i, 0))
    vec = BS((1, d), lambda i: (0, 0))
    return _call(body, name=name, grid=(T // tm,), in_specs=[row, vec, row, row], out_specs=[row, vec],
                 out_shape=[_sds((T, d), F32), _sds((1, d), F32)])(x, g.reshape(1, d), dh, dres)


def _ffn_up(name, h, wg, wu, l, tm):
    T = h.shape[0]

    def body(h_ref, wg_ref, wu_ref, g_ref, u_ref, a_ref):
        hv = h_ref[...]
        g = _dot(hv, wg_ref[...], "nt")
        u = _dot(hv, wu_ref[...], "nt")
        g_ref[...] = g.astype(BF16)
        u_ref[...] = u.astype(BF16)
        a_ref[...] = (g * (1.0 / (1.0 + jnp.exp(-g))) * u).astype(BF16)

    w_spec = BS((None, None, FF_SHARD, D_MODEL), lambda j, i: (j, l, 0, 0))
    o_spec = BS((None, tm, FF_SHARD), lambda j, i: (j, i, 0))
    sh = _sds((N_CHIPS, T, FF_SHARD), BF16)
    return _call(body, name=name, grid=(N_CHIPS, T // tm), in_specs=[BS((tm, D_MODEL), lambda j, i: (i, 0)), w_spec, w_spec],
                 out_specs=[o_spec, o_spec, o_spec], out_shape=[sh, sh, sh])(h, wg, wu)


def _ffn_act_bwd(name, dxo, wd, l, g, u, tm):
    T = dxo.shape[0]

    def body(dx_ref, wd_ref, g_ref, u_ref, dg_ref, du_ref):
        da = _dot(dx_ref[...], wd_ref[...], "nt")
        g = g_ref[...].astype(F32)
        sig = 1.0 / (1.0 + jnp.exp(-g))
        dg_ref[...] = (da * u_ref[...].astype(F32) * (sig * (1.0 + g * (1.0 - sig)))).astype(BF16)
        du_ref[...] = (da * (g * sig)).astype(BF16)

    t_spec = BS((None, tm, FF_SHARD), lambda i, j: (j, i, 0))
    sh = _sds((N_CHIPS, T, FF_SHARD), BF16)
    return _call(body, name=name, grid=(T // tm, N_CHIPS),
                 in_specs=[BS((tm, D_MODEL), lambda i, j: (i, 0)), BS((None, None, FF_SHARD, D_MODEL), lambda i, j: (j, l, 0, 0)), t_spec, t_spec],
                 out_specs=[t_spec, t_spec], out_shape=[sh, sh])(dxo, wd, g, u)


def _big_tile(n):
    return min(1024, n)


def _ffn_fwd(l, x, gain, wg, wu, wd, tm):
    T = x.shape[0]
    h = _rmsnorm_fwd(f"ffn{l}_norm", x, gain, tm)
    tm = _big_tile(T)
    g, u, a = _ffn_up(f"ffn{l}_up", h, wg, wu, l, tm)
    tn = D_MODEL
    out = _matmul(f"ffn{l}_down", "nn", [(a, wd)],
                  [(BS((None, tm, FF_SHARD), lambda i, j, k: (k, i, 0)), BS((None, None, FF_SHARD, tn), lambda i, j, k: (k, l, 0, j)))],
                  (T // tm, D_MODEL // tn, N_CHIPS), _sds((T, D_MODEL), F32), BS((tm, tn), lambda i, j, k: (i, j)), (tm, tn),
                  add=x, add_spec=BS((tm, tn), lambda i, j, k: (i, j)))
    return out, (h, g, u, a)


def _ffn_bwd(l, x, gain, wg, wu, wd, saved, dxo, tm):
    h, g, u, a = saved
    T = x.shape[0]
    tm_norm, tm = tm, _big_tile(T)
    dg, du = _ffn_act_bwd(f"ffn{l}_act_bwd", dxo, wd, l, g, u, tm)
    tk = min(512, T)
    tn = D_MODEL
    shards_spec = BS((N_CHIPS, tk, FF_SHARD), lambda k: (0, k, 0))
    rows_spec = BS((tk, D_MODEL), lambda k: (k, 0))

    def dw(nm, act, rows):
        return _grad_shards(nm, act, rows, shards_spec, rows_spec, lambda a_ref, b_ref, j: (a_ref[j], b_ref[...]),
                            (N_CHIPS, FF_SHARD, D_MODEL), T // tk)

    dwd, dwg, dwu = dw(f"ffn{l}_dwd", a, dxo), dw(f"ffn{l}_dwg", dg, h), dw(f"ffn{l}_dwu", du, h)
    act_spec = BS((None, tm, FF_SHARD), lambda i, j, k: (k, i, 0))
    w_spec = BS((None, None, FF_SHARD, tn), lambda i, j, k: (k, l, 0, j))
    dh = _matmul(f"ffn{l}_dh", "nn", [(dg, wg), (du, wu)], [(act_spec, w_spec), (act_spec, w_spec)],
                 (T // tm, D_MODEL // tn, N_CHIPS), _sds((T, D_MODEL), F32), BS((tm, tn), lambda i, j, k: (i, j)), (tm, tn))
    dx, dgain = _rmsnorm_bwd(f"ffn{l}_norm_bwd", x, gain, dh, dxo, tm_norm)
    return dx, dgain, dwg, dwu, dwd


_INV_SQRT2 = 1.0 / math.sqrt(2.0)
_INV_SQRT_2PI = 1.0 / math.sqrt(2.0 * math.pi)


def _gelu(x):
    return 0.5 * x * (1.0 + lax.erf(x * _INV_SQRT2))


def _gelu_grad(x):
    return 0.5 * (1.0 + lax.erf(x * _INV_SQRT2)) + x * jnp.exp(-0.5 * x * x) * _INV_SQRT_2PI


def _shift_down(x, k):
    return pltpu.roll(x, k, 0)


def _shift_up(x, k):
    return pltpu.roll(x, x.shape[0] - k, 0)


def _layer_norm_head(xh):
    xc = xh - jnp.mean(xh, axis=-1, keepdims=True)
    rstd = lax.rsqrt(jnp.mean(xc * xc, axis=-1, keepdims=True) + EPS)
    return xc * rstd, rstd


def _even_halo_specs(tm, n_tiles, col_blocks, after):
    rows = tm // SUBLANES
    last = n_tiles * rows - 1
    if after:
        return [BS((SUBLANES, 512), functools.partial(lambda cb, i: (jnp.minimum((i + 1) * rows, last), cb), cb)) for cb in col_blocks]
    return [BS((SUBLANES, 512), functools.partial(lambda cb, i: (jnp.maximum(i * rows - 1, 0), cb), cb)) for cb in col_blocks]


def _even_mixer_fwd(proj, ln_g, w_tril, b_lanes, conv_w, seq, tm):
    T = proj.shape[0]
    tiles_per_seq = seq // tm

    def body(p_ref, hc_ref, hh_ref, lng_ref, w_ref, bb_ref, cw_ref, o_ref):
        first = pl.program_id(0) % tiles_per_seq == 0
        for h in range(SG_HEADS):
            cols = slice(SG_DIM * h, SG_DIM * (h + 1))
            vhat, _ = _layer_norm_head(_gelu(p_ref[:, SG_WIDTH + SG_DIM * h:SG_WIDTH + SG_DIM * (h + 1)]))
            vln = (vhat * lng_ref[:, cols]).astype(BF16)
            for k in range(tm // SG_CHUNK):
                rows = slice(SG_CHUNK * k, SG_CHUNK * (k + 1))
                mixed = _dot(w_ref[h], vln[rows]) + bb_ref[h]
                o_ref[rows, cols] = (_gelu(p_ref[rows, cols]) * mixed).astype(BF16)
        z = p_ref[:, 1536:2048] * p_ref[:, 2048:2560]
        zz = jnp.concatenate([jnp.where(first, 0.0, hc_ref[...] * hh_ref[...]), z], axis=0)
        y = cw_ref[0:1, :] * _shift_down(zz, 2)[SUBLANES:] + cw_ref[1:2, :] * _shift_down(zz, 1)[SUBLANES:] + cw_ref[2:3, :] * z
        o_ref[:, SG_WIDTH:] = (p_ref[:, 1024:1536] * y).astype(BF16)

    full = lambda shape: BS(shape, lambda i: (0,) * len(shape))
    return _call(body, name="even_mixer_fwd", grid=(T // tm,),
                 in_specs=[BS((tm, EVEN_IN), lambda i: (i, 0))] + _even_halo_specs(tm, T // tm, (3, 4), after=False)
                 + [full((1, SG_WIDTH)), full((SG_HEADS, SG_CHUNK, SG_CHUNK)), full((SG_HEADS, SG_CHUNK, SG_DIM)), full((SUBLANES, SC_WIDTH))],
                 out_specs=BS((tm, D_MODEL), lambda i: (i, 0)), out_shape=_sds((T, D_MODEL), BF16))(
        proj, proj, proj, ln_g, w_tril, b_lanes, conv_w)


def _even_mixer_bwd(proj, dmix, ln_g, w_tril, b_lanes, conv_w, seq, tm):
    T = proj.shape[0]
    n_tiles, tiles_per_seq = T // tm, seq // tm

    def body(p_ref, dm_ref, hc_ref, hh_ref, nd_ref, nb_ref, lng_ref, w_ref, bb_ref, cw_ref,
             dp_ref, dw_ref, db_ref, dlng_ref, dcw_ref):
        i = pl.program_id(0)
        first = i % tiles_per_seq == 0
        last = i % tiles_per_seq == tiles_per_seq - 1

        @pl.when(i == 0)
        def _():
            dw_ref[...] = jnp.zeros_like(dw_ref)
            db_ref[...] = jnp.zeros_like(db_ref)
            dlng_ref[...] = jnp.zeros_like(dlng_ref)
            dcw_ref[...] = jnp.zeros_like(dcw_ref)

        for h in range(SG_HEADS):
            cols = slice(SG_DIM * h, SG_DIM * (h + 1))
            vcols = slice(SG_WIDTH + SG_DIM * h, SG_WIDTH + SG_DIM * (h + 1))
            lng = lng_ref[:, cols]
            for k in range(tm // SG_CHUNK):
                rows = slice(SG_CHUNK * k, SG_CHUNK * (k + 1))
                v = p_ref[rows, vcols]
                vhat, rstd = _layer_norm_head(_gelu(v))
                vln = (vhat * lng).astype(BF16)
                mixed = _dot(w_ref[h], vln) + bb_ref[h]
                u = p_ref[rows, cols]
                da = dm_ref[rows, cols]
                dp_ref[rows, cols] = (da * mixed * _gelu_grad(u)).astype(BF16)
                dmixed = da * _gelu(u)
                db_ref[h] += dmixed
                dw_ref[h] += _dot(dmixed, vln, "nt")
                dvln = _dot(w_ref[h], dmixed, "tn")
                dlng_ref[:, cols] += jnp.sum(dvln * vhat, axis=0, keepdims=True)
                dvhat = dvln * lng
                dgv = rstd * (dvhat - jnp.mean(dvhat, axis=-1, keepdims=True)
                              - vhat * jnp.mean(dvhat * vhat, axis=-1, keepdims=True))
                dp_ref[rows, vcols] = (dgv * _gelu_grad(v)).astype(BF16)

        b = p_ref[:, 1024:1536]
        c = p_ref[:, 1536:2048]
        hv = p_ref[:, 2048:2560]
        z = c * hv
        zz = jnp.concatenate([jnp.where(first, 0.0, hc_ref[...] * hh_ref[...]), z], axis=0)
        z1 = _shift_down(zz, 1)[SUBLANES:]
        z2 = _shift_down(zz, 2)[SUBLANES:]
        w0, w1, w2 = cw_ref[0:1, :], cw_ref[1:2, :], cw_ref[2:3, :]
        dbo = dm_ref[:, SG_WIDTH:]
        dy = dbo * b
        dd = jnp.concatenate([dy, jnp.where(last, 0.0, nd_ref[...] * nb_ref[...])], axis=0)
        dz = w2 * dy + w1 * _shift_up(dd, 1)[:tm] + w0 * _shift_up(dd, 2)[:tm]
        dp_ref[:, 1024:1536] = (dbo * (w0 * z2 + w1 * z1 + w2 * z)).astype(BF16)
        dp_ref[:, 1536:2048] = (dz * hv).astype(BF16)
        dp_ref[:, 2048:2560] = (dz * c).astype(BF16)
        dcw_ref[0:1, :] += jnp.sum(dy * z2, axis=0, keepdims=True)
        dcw_ref[1:2, :] += jnp.sum(dy * z1, axis=0, keepdims=True)
        dcw_ref[2:3, :] += jnp.sum(dy * z, axis=0, keepdims=True)

        @pl.when(i == n_tiles - 1)
        def _():
            t_idx = lax.broadcasted_iota(jnp.int32, (SG_CHUNK, SG_CHUNK), 0)
            s_idx = lax.broadcasted_iota(jnp.int32, (SG_CHUNK, SG_CHUNK), 1)
            for h in range(SG_HEADS):
                dw_ref[h] = jnp.where(t_idx >= s_idx, dw_ref[h], 0.0)

    full = lambda shape: BS(shape, lambda i: (0,) * len(shape))
    sq = (SG_HEADS, SG_CHUNK, SG_CHUNK)
    return _call(body, name="even_mixer_bwd", grid=(n_tiles,),
                 in_specs=[BS((tm, EVEN_IN), lambda i: (i, 0)), BS((tm, D_MODEL), lambda i: (i, 0))]
                 + _even_halo_specs(tm, n_tiles, (3, 4), after=False)
                 + _even_halo_specs(tm, n_tiles, (1,), after=True) + _even_halo_specs(tm, n_tiles, (2,), after=True)
                 + [full((1, SG_WIDTH)), full(sq), full(sq), full((SUBLANES, SC_WIDTH))],
                 out_specs=[BS((tm, EVEN_IN), lambda i: (i, 0)), full(sq), full(sq), full((1, SG_WIDTH)), full((SUBLANES, SC_WIDTH))],
                 out_shape=[_sds((T, EVEN_IN), BF16), _sds(sq, F32), _sds(sq, F32), _sds((1, SG_WIDTH), F32), _sds((SUBLANES, SC_WIDTH), F32)])(
        proj, dmix, proj, proj, dmix, proj, ln_g, w_tril, b_lanes, conv_w)


def _pool_select(vals):
    lane = lax.broadcasted_iota(jnp.int32, vals[0].shape, 1)
    out = vals[-1]
    for g in range(len(vals) - 2, -1, -1):
        out = jnp.where(lane < POOL_DIM * (g + 1), vals[g], out)
    return out


def _pool_counts(pos1):
    lane = lax.broadcasted_iota(jnp.int32, (pos1.shape[0], POOL_WIDTH), 1)
    win = _pool_select([jnp.full(lane.shape, float(w), F32) for w in POOL_WINDOWS])
    return jnp.minimum(pos1, win)


def _pool_means(zz, counts):
    s2 = zz + _shift_down(zz, 1)
    s4 = s2 + _shift_down(s2, 2)
    s8 = s4 + _shift_down(s4, 4)
    s16 = s8 + _shift_down(s8, 8)
    return _pool_select([s2, s4, s8, s16])[POOL_HALO:] / counts


def _pool_halo_spec(tm, n_tiles, after):
    rows = tm // POOL_HALO
    if after:
        return BS((POOL_HALO, POOL_WIDTH), lambda i: (jnp.minimum((i + 1) * rows, n_tiles * rows - 1), 0))
    return BS((POOL_HALO, POOL_WIDTH), lambda i: (jnp.maximum(i * rows - 1, 0), 0))


def _pool_fwd(proj, w_diag, scale, seq, tm):
    T = proj.shape[0]
    tiles_per_seq = seq // tm

    def body(z_ref, zh_ref, w_ref, s_ref, o_ref):
        t = pl.program_id(0) % tiles_per_seq
        z = z_ref[...]
        zz = jnp.concatenate([jnp.where(t == 0, 0.0, zh_ref[...]), z], axis=0)
        pos1 = (lax.broadcasted_iota(jnp.int32, (tm, 1), 0) + (t * tm + 1)).astype(F32)
        pooled = _pool_means(zz, _pool_counts(pos1)) - z
        o_ref[...] = (_dot(pooled, w_ref[...]) * s_ref[...]).astype(BF16)

    full = lambda shape: BS(shape, lambda i: (0,) * len(shape))
    return _call(body, name="pool_fwd", grid=(T // tm,),
                 in_specs=[BS((tm, POOL_WIDTH), lambda i: (i, 0)), _pool_halo_spec(tm, T // tm, False),
                           full((POOL_WIDTH, POOL_WIDTH)), full((1, POOL_WIDTH))],
                 out_specs=BS((tm, POOL_WIDTH), lambda i: (i, 0)), out_shape=_sds((T, D_MODEL), BF16))(proj, proj, w_diag, scale)


def _pool_bwd(proj, dmix, w_diag, scale, seq, tm):
    T = proj.shape[0]
    n_tiles, tiles_per_seq = T // tm, seq // tm

    def body(z_ref, zh_ref, do_ref, don_ref, w_ref, s_ref, dz_ref, dw_ref, ds_ref):
        i = pl.program_id(0)
        t = i % tiles_per_seq

        @pl.when(i == 0)
        def _():
            dw_ref[...] = jnp.zeros_like(dw_ref)
            ds_ref[...] = jnp.zeros_like(ds_ref)

        z = z_ref[...]
        zz = jnp.concatenate([jnp.where(t == 0, 0.0, zh_ref[...]), z], axis=0)
        pos1 = (lax.broadcasted_iota(jnp.int32, (tm, 1), 0) + (t * tm + 1)).astype(F32)
        counts = _pool_counts(pos1)
        pooled = _pool_means(zz, counts) - z
        dout = do_ref[...].astype(F32)
        ds_ref[...] += jnp.sum(dout * _dot(pooled, w_ref[...]), axis=0, keepdims=True)
        dlin = dout * s_ref[...]
        dw_ref[...] += _dot(pooled, dlin, "tn")
        dpooled = _dot(dlin, w_ref[...], "nt")
        dpooled_n = _dot(don_ref[...].astype(F32) * s_ref[...], w_ref[...], "nt")
        pos1_n = (lax.broadcasted_iota(jnp.int32, (POOL_HALO, 1), 0) + ((t + 1) * tm + 1)).astype(F32)
        dmean_n = jnp.where(t == tiles_per_seq - 1, 0.0, dpooled_n / _pool_counts(pos1_n))
        dd = jnp.concatenate([dpooled / counts, dmean_n], axis=0)
        r2 = dd + _shift_up(dd, 1)
        r4 = r2 + _shift_up(r2, 2)
        r8 = r4 + _shift_up(r4, 4)
        r16 = r8 + _shift_up(r8, 8)
        dz_ref[...] = (_pool_select([r2, r4, r8, r16])[:tm] - dpooled).astype(BF16)

    full = lambda shape: BS(shape, lambda i: (0,) * len(shape))
    return _call(body, name="pool_bwd", grid=(n_tiles,),
                 in_specs=[BS((tm, POOL_WIDTH), lambda i: (i, 0)), _pool_halo_spec(tm, n_tiles, False),
                           BS((tm, POOL_WIDTH), lambda i: (i, 0)), _pool_halo_spec(tm, n_tiles, True),
                           full((POOL_WIDTH, POOL_WIDTH)), full((1, POOL_WIDTH))],
                 out_specs=[BS((tm, POOL_WIDTH), lambda i: (i, 0)), full((POOL_WIDTH, POOL_WIDTH)), full((1, POOL_WIDTH))],
                 out_shape=[_sds((T, POOL_WIDTH), BF16), _sds((POOL_WIDTH, POOL_WIDTH), F32), _sds((1, POOL_WIDTH), F32)])(
        proj, proj, dmix, dmix, w_diag, scale)


def _rope_partner(r):
    lane = lax.broadcasted_iota(jnp.int32, r.shape, 1)
    return jnp.where(lane < QK_ROPE // 2, pltpu.roll(r, LANES - QK_ROPE // 2, 1), pltpu.roll(r, QK_ROPE // 2, 1))


def _rope(x, cos, sin_signed):
    r = x[:, QK_NOPE:]
    return jnp.concatenate([x[:, :QK_NOPE], r * cos + _rope_partner(r) * sin_signed], axis=1)


def _rope_transposed(dx, cos, sin_signed):
    dr = dx[:, QK_NOPE:]
    return jnp.concatenate([dx[:, :QK_NOPE], dr * cos + _rope_partner(dr * sin_signed)], axis=1)


def _head_norm(x):
    r = lax.rsqrt(jnp.sum(x * x, axis=-1, keepdims=True) * (1.0 / QK_DIM) + EPS)
    return x * r, r


def _head_norm_bwd(dy, xhat, r, gain):
    dxhat = dy * gain
    return r * (dxhat - xhat * (jnp.sum(dxhat * xhat, axis=-1, keepdims=True) * (1.0 / QK_DIM)))


def _latents(p_ref, qag_ref, kvag_ref):
    ql = p_ref[:, POOL_WIDTH:POOL_WIDTH + Q_LORA]
    kvl = p_ref[:, POOL_WIDTH + Q_LORA:POOL_WIDTH + Q_LORA + KV_LORA]
    rq = lax.rsqrt(jnp.mean(ql * ql, axis=-1, keepdims=True) + EPS)
    rkv = lax.rsqrt(jnp.mean(kvl * kvl, axis=-1, keepdims=True) + EPS)
    return ql * rq, rq, kvl * rkv, rkv


def _mla_specs(tm):
    full = lambda shape: BS(shape, lambda i, h: (0,) * len(shape))
    return [BS((tm, ODD_IN_PAD), lambda i, h: (i, 0)), BS((tm, LANES), lambda i, h: (i, 0)), BS((tm, LANES), lambda i, h: (i, 0)),
            full((1, Q_LORA)), full((1, KV_LORA)), BS((None, Q_LORA, QK_PAD), lambda i, h: (h, 0, 0)),
            BS((None, KV_LORA, QK_PAD), lambda i, h: (h, 0, 0)), full((1, QK_PAD)), full((1, QK_PAD))]


def _mla_qkv_fwd(proj, cos, sin_signed, qa_g, kva_g, q_b, kv_b, q_g, k_g, tm):
    T = proj.shape[0]

    def body(p_ref, cos_ref, sin_ref, qag_ref, kvag_ref, qb_ref, kvb_ref, qg_ref, kg_ref, q_ref, k_ref, v_ref, qn_s, kvn_s):
        @pl.when(pl.program_id(1) == 0)
        def _():
            qhat, _, kvhat, _ = _latents(p_ref, qag_ref, kvag_ref)
            qn_s[...] = (qhat * qag_ref[...]).astype(BF16)
            kvn_s[...] = (kvhat * kvag_ref[...]).astype(BF16)

        cos, sin = cos_ref[...], sin_ref[...]
        qhat, _ = _head_norm(_dot(qn_s[...], qb_ref[...]))
        q_ref[...] = _rope(qhat * qg_ref[...], cos, sin).astype(BF16)
        kv = _dot(kvn_s[...], kvb_ref[...])
        khat, _ = _head_norm(jnp.concatenate([kv[:, :QK_NOPE], p_ref[:, ODD_IN_PAD - LANES:]], axis=1))
        k_ref[...] = _rope(khat * kg_ref[...], cos, sin).astype(BF16)
        v_ref[...] = kv[:, QK_NOPE:].astype(BF16)

    qk_spec = BS((None, tm, QK_PAD), lambda i, h: (h, i, 0))
    return _call(body, name="mla_qkv_fwd", grid=(T // tm, HEADS), in_specs=_mla_specs(tm),
                 out_specs=[qk_spec, qk_spec, BS((None, tm, V_DIM), lambda i, h: (h, i, 0))],
                 out_shape=[_sds((HEADS, T, QK_PAD), BF16), _sds((HEADS, T, QK_PAD), BF16), _sds((HEADS, T, V_DIM), BF16)],
                 scratch=[pltpu.VMEM((tm, Q_LORA), BF16), pltpu.VMEM((tm, KV_LORA), BF16)])(
        proj, cos, sin_signed, qa_g, kva_g, q_b, kv_b, q_g, k_g)


def _mla_qkv_bwd(proj, cos, sin_signed, qa_g, kva_g, q_b, kv_b, q_g, k_g, dq, dk, dv, dz_pool, tm):
    T = proj.shape[0]
    n_tiles = T // tm

    def body(p_ref, cos_ref, sin_ref, qag_ref, kvag_ref, qb_ref, kvb_ref, qg_ref, kg_ref, dq_ref, dk_ref, dv_ref, dzp_ref,
             dp_ref, dqb_ref, dkvb_ref, dqg_ref, dkg_ref, dqag_ref, dkvag_ref, qn_s, kvn_s, dqn_s, dkvn_s, dkr_s):
        i, h = pl.program_id(0), pl.program_id(1)

        @pl.when((i == 0) & (h == 0))
        def _():
            for ref in (dqb_ref, dkvb_ref, dqg_ref, dkg_ref, dqag_ref, dkvag_ref):
                ref[...] = jnp.zeros_like(ref)

        @pl.when(h == 0)
        def _():
            qhat, _, kvhat, _ = _latents(p_ref, qag_ref, kvag_ref)
            qn_s[...] = (qhat * qag_ref[...]).astype(BF16)
            kvn_s[...] = (kvhat * kvag_ref[...]).astype(BF16)
            dqn_s[...] = jnp.zeros_like(dqn_s)
            dkvn_s[...] = jnp.zeros_like(dkvn_s)
            dkr_s[...] = jnp.zeros_like(dkr_s)

        cos, sin = cos_ref[...], sin_ref[...]
        qhat, rq = _head_norm(_dot(qn_s[...], qb_ref[...]))
        dqn_head = _rope_transposed(dq_ref[...], cos, sin)
        dqg_ref[...] += jnp.sum(dqn_head * qhat, axis=0, keepdims=True)
        dqh = _head_norm_bwd(dqn_head, qhat, rq, qg_ref[...])
        dqb_ref[h] += _dot(qn_s[...], dqh, "tn")
        dqn_s[...] += _dot(dqh, qb_ref[...], "nt")

        kv = _dot(kvn_s[...], kvb_ref[...])
        khat, rk = _head_norm(jnp.concatenate([kv[:, :QK_NOPE], p_ref[:, ODD_IN_PAD - LANES:]], axis=1))
        dkn_head = _rope_transposed(dk_ref[...], cos, sin)
        dkg_ref[...] += jnp.sum(dkn_head * khat, axis=0, keepdims=True)
        dkf = _head_norm_bwd(dkn_head, khat, rk, kg_ref[...])
        dkr_s[...] += dkf[:, QK_NOPE:]
        dkv = jnp.concatenate([dkf[:, :QK_NOPE], dv_ref[...]], axis=1)
        dkvb_ref[h] += _dot(kvn_s[...], dkv, "tn")
        dkvn_s[...] += _dot(dkv, kvb_ref[...], "nt")

        @pl.when(h == HEADS - 1)
        def _():
            qhat_l, rql, kvhat_l, rkvl = _latents(p_ref, qag_ref, kvag_ref)
            dqn, dkvn = dqn_s[...], dkvn_s[...]
            dqag_ref[...] += jnp.sum(dqn * qhat_l, axis=0, keepdims=True)
            dkvag_ref[...] += jnp.sum(dkvn * kvhat_l, axis=0, keepdims=True)
            dqx, dkvx = dqn * qag_ref[...], dkvn * kvag_ref[...]
            dp_ref[:, :POOL_WIDTH] = dzp_ref[...]
            dp_ref[:, POOL_WIDTH:POOL_WIDTH + Q_LORA] = (
                rql * (dqx - qhat_l * jnp.mean(dqx * qhat_l, axis=-1, keepdims=True))).astype(BF16)
            dp_ref[:, POOL_WIDTH + Q_LORA:ODD_IN_PAD - LANES] = (
                rkvl * (dkvx - kvhat_l * jnp.mean(dkvx * kvhat_l, axis=-1, keepdims=True))).astype(BF16)
            dp_ref[:, ODD_IN_PAD - LANES:] = dkr_s[:, :QK_ROPE].astype(BF16)

    full = lambda shape: BS(shape, lambda i, h: (0,) * len(shape))
    qk_spec = BS((None, tm, QK_PAD), lambda i, h: (h, i, 0))
    return _call(body, name="mla_qkv_bwd", grid=(n_tiles, HEADS),
                 in_specs=_mla_specs(tm) + [qk_spec, qk_spec, BS((None, tm, V_DIM), lambda i, h: (h, i, 0)),
                                            BS((tm, POOL_WIDTH), lambda i, h: (i, 0))],
                 out_specs=[BS((tm, ODD_IN), lambda i, h: (i, 0)), full((HEADS, Q_LORA, QK_PAD)), full((HEADS, KV_LORA, QK_PAD)),
                            full((1, QK_PAD)), full((1, QK_PAD)), full((1, Q_LORA)), full((1, KV_LORA))],
                 out_shape=[_sds((T, ODD_IN), BF16),_sds((HEADS, Q_LORA, QK_PAD), F32), _sds((HEADS, KV_LORA, QK_PAD), F32),
                            _sds((1, QK_PAD), F32), _sds((1, QK_PAD), F32), _sds((1, Q_LORA), F32), _sds((1, KV_LORA), F32)],
                 scratch=[pltpu.VMEM((tm, Q_LORA), BF16), pltpu.VMEM((tm, KV_LORA), BF16), pltpu.VMEM((tm, Q_LORA), F32),
                          pltpu.VMEM((tm, KV_LORA), F32), pltpu.VMEM((tm, LANES), F32)])(
        proj, cos, sin_signed, qa_g, kva_g, q_b, kv_b, q_g, k_g, dq, dk, dv, dz_pool)


def _attn_tile(seq):
    return 512 if seq % 512 == 0 else seq


def _causal_mask(s):
    row = lax.broadcasted_iota(jnp.int32, s.shape, 0)
    col = lax.broadcasted_iota(jnp.int32, s.shape, 1)
    return jnp.where(row >= col, s, NEG_INF)


def _tile(i, t):
    return slice(i * t, (i + 1) * t)


def _flash_fwd(q, k, v, mix, batch, seq):
    t = _attn_tile(seq)
    nq = seq // t

    def body(q_ref, k_ref, v_ref, _, o_ref, lse_ref):
        for qi in range(nq):
            rows, before = _tile(qi, t), slice(0, qi * t)
            qv = q_ref[rows, :]
            s_diag = _causal_mask(_dot(qv, k_ref[rows, :], "nt") * ATTN_SCALE)
            m = jnp.max(s_diag, axis=-1, keepdims=True)
            if qi:
                s_before = _dot(qv, k_ref[before, :], "nt") * ATTN_SCALE
                m = jnp.maximum(m, jnp.max(s_before, axis=-1, keepdims=True))
            p = jnp.exp(s_diag - m)
            l = jnp.sum(p, axis=-1, keepdims=True)
            acc = _dot(p, v_ref[rows, :])
            if qi:
                p = jnp.exp(s_before - m)
                l = l + jnp.sum(p, axis=-1, keepdims=True)
                acc = acc + _dot(p, v_ref[before, :])
            o_ref[rows, :] = (acc / l).astype(BF16)
            lse_ref[rows, :] = jnp.broadcast_to(m + jnp.log(l), (t, LANES))

    T = batch * seq
    whole = lambda w: BS((None, seq, w), lambda b, h: (h, b, 0))
    return _call(body, name="flash_fwd", grid=(batch, HEADS),
                 in_specs=[whole(QK_PAD), whole(QK_PAD), whole(V_DIM), pl.BlockSpec(memory_space=pl.ANY)],
                 out_specs=[BS((seq, V_DIM), lambda b, h: (b, POOL_WIDTH // V_DIM + h)), whole(LANES)],
                 out_shape=[_sds((T, D_MODEL), BF16), _sds((HEADS, T, LANES), F32)],
                 aliases={3: 0})(q, k, v, mix)


def _flash_bwd(q, k, v, dmix, mix, lse, batch, seq):
    t = _attn_tile(seq)
    nq = seq // t

    def body(q_ref, k_ref, v_ref, do_ref, o_ref, lse_ref, dq_ref, dk_ref, dv_ref, delta_s):
        dq_ref[...] = jnp.zeros_like(dq_ref)
        dk_ref[...] = jnp.zeros_like(dk_ref)
        dv_ref[...] = jnp.zeros_like(dv_ref)
        for qi in range(nq):
            rows = _tile(qi, t)
            delta_s[qi] = jnp.sum(do_ref[rows, :].astype(F32) * o_ref[rows, :].astype(F32), axis=-1, keepdims=True)
        for kb in range(nq):
            keys = _tile(kb, t)
            for qi in range(kb, nq):
                rows = _tile(qi, t)
                qv, kk, do = q_ref[rows, :], k_ref[keys, :], do_ref[rows, :]
                s = _dot(qv, kk, "nt") * ATTN_SCALE
                if kb == qi:
                    s = _causal_mask(s)
                p = jnp.exp(s - lse_ref[rows, 0:1])
                dv_ref[keys, :] += _dot(p, do, "tn")
                ds = p * (_dot(do, v_ref[keys, :], "nt") - delta_s[qi]) * ATTN_SCALE
                dq_ref[rows, :] += _dot(ds, kk)
                dk_ref[keys, :] += _dot(ds, qv, "tn")

    T = batch * seq
    whole = lambda w: BS((None, seq, w), lambda b, h: (h, b, 0))
    head_cols = BS((seq, V_DIM), lambda b, h: (b, POOL_WIDTH // V_DIM + h))
    return _call(body, name="flash_bwd", grid=(batch, HEADS),
                 in_specs=[whole(QK_PAD), whole(QK_PAD), whole(V_DIM), head_cols, head_cols, whole(LANES)],
                 out_specs=[whole(QK_PAD), whole(QK_PAD), whole(V_DIM)],
                 out_shape=[_sds((HEADS, T, QK_PAD), F32), _sds((HEADS, T, QK_PAD), F32), _sds((HEADS, T, V_DIM), F32)],
                 scratch=[pltpu.VMEM((nq, t, 1), F32)])(q, k, v, dmix, mix, lse)


def _loss_head(y, target, tm):
    T, d = y.shape

    def body(y_ref, t_ref, dy_ref, sq_ref):
        @pl.when(pl.program_id(0) == 0)
        def _():
            sq_ref[...] = jnp.zeros_like(sq_ref)

        e = y_ref[...] - t_ref[...]
        sq_ref[...] += jnp.sum(e * e)
        dy_ref[...] = e * (1.0 / d)

    row = BS((tm, d), lambda i: (i, 0))
    return _call(body, name="loss_head", grid=(T // tm,), in_specs=[row, row],
                 out_specs=[row, BS((SUBLANES, LANES), lambda i: (0, 0))],
                 out_shape=[_sds((T, d), F32), _sds((SUBLANES, LANES), F32)])(y, target)


def _adamw_math(w, g, m, v):
    m = ADAM_B1 * m + (1.0 - ADAM_B1) * g
    v = ADAM_B2 * v + (1.0 - ADAM_B2) * (g * g)
    m_hat = m / (1.0 - ADAM_B1 ** ADAM_STEP)
    v_hat = v / (1.0 - ADAM_B2 ** ADAM_STEP)
    return -ADAM_LR * (m_hat / (jnp.sqrt(v_hat) + ADAM_EPS) + ADAM_WD * w), m, v


def _adamw(name, w, g, m, v):
    L, R, C = w.shape
    tr = 256 if R % 256 == 0 else R
    outs = None
    for l in range(L):
        def body(w_ref, g_ref, m_ref, v_ref, *rest):
            go_ref, d_ref, mo_ref, vo_ref = rest[-4:]
            gv = g_ref[...]
            d_ref[...], mo_ref[...], vo_ref[...] = _adamw_math(w_ref[...], gv, m_ref[...], v_ref[...])
            go_ref[...] = gv

        layer = BS((None, tr, C), functools.partial(lambda l, i: (l, i, 0), l))
        prev = [] if outs is None else list(outs)
        outs = _call(body, name=f"{name}_{l}", grid=(R // tr,),
                     in_specs=[layer, BS((tr, C), lambda i: (i, 0)), layer, layer] + [pl.BlockSpec(memory_space=pl.ANY)] * len(prev),
                     out_specs=[layer] * 4, out_shape=[_sds((L, R, C), F32)] * 4,
                     aliases={4 + n: n for n in range(len(prev))})(w, g[l], m, v, *prev)
    return outs


def _place():
    x, y, c = lax.axis_index("x"), lax.axis_index("y"), lax.axis_index("c")
    other_chips = [(1 - x, y), (x, 1 - y), (1 - x, 1 - y)]
    return x, y, c, other_chips


def _remote(src, dst, send_sem, recv_sem, dev):
    return pltpu.make_async_remote_copy(src_ref=src, dst_ref=dst, send_sem=send_sem, recv_sem=recv_sem,
                                        device_id=dev, device_id_type=MESH)


def _prefetch_call(body, *, name, grid, in_specs, out_specs, out_shape):
    grid_spec = pltpu.PrefetchScalarGridSpec(num_scalar_prefetch=1, grid=grid, in_specs=in_specs, out_specs=out_specs)
    params = pltpu.CompilerParams(vmem_limit_bytes=VMEM_LIMIT_V7X, dimension_semantics=("arbitrary",) * len(grid))
    return pl.pallas_call(body, name=name, grid_spec=grid_spec, out_shape=out_shape, compiler_params=params)


def _row_tile(rows):
    return 256 if rows % 256 == 0 else rows


def _cast_place(name, w, chip):
    _, rows, C = w.shape
    tr = _row_tile(rows)

    def body(chip_ref, w_ref, o_ref):
        o_ref[...] = w_ref[...].astype(BF16)

    return _prefetch_call(body, name=name, grid=(2, rows // tr),
                          in_specs=[BS((None, tr, C), lambda h, i, chip_ref: (h, i, 0))],
                          out_specs=BS((None, None, tr, C), lambda h, i, chip_ref: (chip_ref[0], h, i, 0)),
                          out_shape=_sds((N_CHIPS, 2, rows, C), BF16))(chip, w)


def _all_gather_chips(gs):
    n = len(gs)

    def body(*refs):
        g, send_sems, recv_sems = refs[n:2 * n], refs[-2], refs[-1]
        x, y, c, chips = _place()
        me, sibling = 2 * x + y, (x, y, 1 - c)
        sends = [_remote(g[i].at[me, c], g[i].at[me, c], send_sems.at[6 * i + k], recv_sems.at[6 * i + k], (px, py, c))
                 for i in range(n) for k, (px, py) in enumerate(chips)]
        for cp in sends:
            cp.start()
        passed = []
        for i in range(n):
            for k, (px, py) in enumerate(chips):
                landed = g[i].at[2 * px + py, c]
                _remote(landed, landed, send_sems.at[6 * i + k], recv_sems.at[6 * i + k], (px, py, c)).wait_recv()
                passed.append(_remote(landed, landed, send_sems.at[6 * i + 3 + k], recv_sems.at[6 * i + 3 + k], sibling))
                passed[-1].start()
        for i in range(n):
            for k, (px, py) in enumerate(chips):
                theirs = g[i].at[2 * px + py, 1 - c]
                _remote(theirs, theirs, send_sems.at[6 * i + 3 + k], recv_sems.at[6 * i + 3 + k], sibling).wait_recv()
        for cp in sends + passed:
            cp.wait_send()

    return _call(body, name="all_gather_weights", in_specs=[HBM] * n, out_specs=[HBM] * n,
                 out_shape=[_sds(a.shape, a.dtype) for a in gs], aliases={i: i for i in range(n)},
                 scratch=[pltpu.SemaphoreType.DMA((6 * n,)), pltpu.SemaphoreType.DMA((6 * n,))])(*gs)


def _sibling_exchange(gs):
    n = len(gs)

    def body(*refs):
        g, r, send_sems, recv_sems = refs[:n], refs[n:2 * n], refs[-2], refs[-1]
        x, y, c, _ = _place()
        copies = [_remote(g[i].at[:, 1 - c], r[i], send_sems.at[i], recv_sems.at[i], (x, y, 1 - c)) for i in range(n)]
        for cp in copies:
            cp.start()
        for cp in copies:
            cp.wait()

    return _call(body, name="grad_sibling_exchange", in_specs=[HBM] * n, out_specs=[HBM] * n,
                 out_shape=[_sds((a.shape[0],) + a.shape[2:], a.dtype) for a in gs],
                 scratch=[pltpu.SemaphoreType.DMA((n,)), pltpu.SemaphoreType.DMA((n,))])(*gs)


def _chip_scatter(ps):
    n = len(ps)

    def body(*refs):
        p, r, send_sems, recv_sems = refs[:n], refs[n:2 * n], refs[-2], refs[-1]
        x, y, c, chips = _place()
        sends = [_remote(p[i].at[2 * px + py], r[i].at[k], send_sems.at[3 * i + k], recv_sems.at[3 * i + k], (px, py, c))
                 for i in range(n) for k, (px, py) in enumerate(chips)]
        for cp in sends:
            cp.start()
        for i in range(n):
            for k, (px, py) in enumerate(chips):
                _remote(r[i].at[k], r[i].at[k], send_sems.at[3 * i + k], recv_sems.at[3 * i + k], (px, py, c)).wait_recv()
        for cp in sends:
            cp.wait_send()

    return _call(body, name="grad_chip_scatter", in_specs=[HBM] * n, out_specs=[HBM] * n,
                 out_shape=[_sds((N_CHIPS - 1,) + a.shape[1:], a.dtype) for a in ps],
                 scratch=[pltpu.SemaphoreType.DMA((3 * n,)), pltpu.SemaphoreType.DMA((3 * n,))])(*ps)


def _sibling_share(fs):
    n = len(fs)

    def body(*refs):
        f, send_sems, recv_sems = refs[n:2 * n], refs[-2], refs[-1]
        x, y, c, _ = _place()
        sends = [_remote(f[i].at[c], f[i].at[c], send_sems.at[i], recv_sems.at[i], (x, y, 1 - c)) for i in range(n)]
        for cp in sends:
            cp.start()
        for i in range(n):
            theirs = f[i].at[1 - c]
            _remote(theirs, theirs, send_sems.at[i], recv_sems.at[i], (x, y, 1 - c)).wait_recv()
        for cp in sends:
            cp.wait_send()

    return _call(body, name="grad_sibling_share", in_specs=[HBM] * n, out_specs=[HBM] * n,
                 out_shape=[_sds(a.shape, a.dtype) for a in fs], aliases={i: i for i in range(n)},
                 scratch=[pltpu.SemaphoreType.DMA((n,)), pltpu.SemaphoreType.DMA((n,))])(*fs)


def _all_reduce_small(name, v):
    n_dev = 8
    flips = [(fx, fy, fc) for fx in (0, 1) for fy in (0, 1) for fc in (0, 1)][1:]

    def body(v_ref, o_ref, buf, send_sems, recv_sems):
        x, y, c, _ = _place()
        peers = [(1 - x if fx else x, 1 - y if fy else y, 1 - c if fc else c) for fx, fy, fc in flips]
        me = 4 * x + 2 * y + c
        buf[me] = v_ref[...]
        sends = [_remote(v_ref, buf.at[me], send_sems.at[k], recv_sems.at[k], peer) for k, peer in enumerate(peers)]
        for cp in sends:
            cp.start()
        for k, (px, py, pc) in enumerate(peers):
            theirs = buf.at[4 * px + 2 * py + pc]
            _remote(v_ref, theirs, send_sems.at[k], recv_sems.at[k], (px, py, pc)).wait_recv()
        for cp in sends:
            cp.wait_send()
        acc = buf[0]
        for d in range(1, n_dev):
            acc = acc + buf[d]
        o_ref[...] = acc

    return _call(body, name=name, in_specs=[VMEM], out_specs=VMEM, out_shape=_sds(v.shape, F32),
                 scratch=[pltpu.VMEM((n_dev,) + v.shape, F32), pltpu.SemaphoreType.DMA((7,)), pltpu.SemaphoreType.DMA((7,))])(v)


def _add_halves(name, g, r, c):
    _, _, rows, C = g.shape
    tr = _row_tile(rows)

    def body(c_ref, g_ref, r_ref, o_ref):
        o_ref[...] = (g_ref[...] + r_ref[...]).astype(BF16)

    spec = BS((None, tr, C), lambda j, i, c_ref: (j, i, 0))
    return _prefetch_call(body, name=name, grid=(N_CHIPS, rows // tr),
                          in_specs=[BS((None, None, tr, C), lambda j, i, c_ref: (j, c_ref[0], i, 0)), spec], out_specs=spec,
                          out_shape=_sds((N_CHIPS, rows, C), BF16))(c, g, r)


def _sum_partials(name, p, r, chip_c):
    _, rows, C = p.shape
    tr = _row_tile(rows)

    def body(s_ref, p_ref, r_ref, o_ref):
        acc = p_ref[...].astype(F32)
        for k in range(N_CHIPS - 1):
            acc = acc + r_ref[k].astype(F32)
        o_ref[...] = acc

    return _prefetch_call(body, name=name, grid=(rows // tr,),
                          in_specs=[BS((None, tr, C), lambda i, s: (s[0], i, 0)), BS((N_CHIPS - 1, tr, C), lambda i, s: (0, i, 0))],
                          out_specs=BS((None, tr, C), lambda i, s: (s[1], i, 0)), out_shape=_sds((2, rows, C), F32))(chip_c, p, r)


_SHARDED = ("even_w_in", "even_w_out", "odd_w_in", "q_b", "kv_b", "odd_w_out", "ffn_w_gate", "ffn_w_up", "ffn_w_down")
_REPLICATED = ("mix_norm", "ffn_norm", "sg_ln_g", "sg_w_s", "sg_b_s", "pool_w", "q_norm", "k_norm")
_SMALL_SHARDED = ("sc_conv_w", "pool_scale", "q_a_norm", "kv_a_norm")
_WEIGHTS = ("mix_norm", "ffn_norm", "even_w_in", "sg_ln_g", "sg_w_s", "sg_b_s", "sc_conv_w", "even_w_out", "odd_w_in", "pool_w",
            "pool_scale", "q_a_norm", "q_b", "kv_a_norm", "kv_b", "q_norm", "k_norm", "odd_w_out", "ffn_w_gate", "ffn_w_up",
            "ffn_w_down")


def _pad_rows(flat, width, align):
    n = flat.shape[0]
    rows = -(-n // (width * align)) * align
    return jnp.pad(flat, (0, rows * width - n)).reshape(rows, width)


def _gather_weights(shards, chip):
    halves = []
    for n in _SHARDED:
        a = shards[n]
        halves.append(a if a.shape[0] == 2 else a.reshape(2, a.shape[1] // 2, a.shape[2]))
    placed = [_cast_place(f"place_{n}", a, chip) for n, a in zip(_SHARDED, halves)]
    out = dict(zip(_SHARDED, _all_gather_chips(placed)))
    for n in ("even_w_in", "even_w_out", "odd_w_in", "q_b", "kv_b", "odd_w_out"):
        out[n] = out[n].reshape(N_CHIPS, -1, out[n].shape[-1])
    for n in ("q_b", "kv_b"):
        out[n] = out[n].transpose(1, 0, 2).reshape(out[n].shape[1], -1)
    for n in ("even_w_out", "odd_w_in", "odd_w_out"):
        out[n] = out[n].reshape(-1, out[n].shape[-1])
    return out


def _forward_backward(x, positions, target, W, small):
    batch, seq, _ = x.shape
    T = batch * seq
    tm = _token_tile(seq)
    x0 = x.reshape(T, D_MODEL)

    inv_freq = ROPE_THETA ** (-jnp.arange(0, QK_ROPE, 2, dtype=F32) / QK_ROPE)
    ang = (positions.astype(F32)[..., None] * inv_freq).reshape(T, QK_ROPE // 2)
    cos, sin = jnp.cos(ang), jnp.sin(ang)
    pad = jnp.zeros((T, LANES - QK_ROPE), F32)
    cos_t = jnp.concatenate([cos, cos, pad], axis=1)
    sin_t = jnp.concatenate([-sin, sin, pad], axis=1)

    tril = jnp.tril(jnp.ones((SG_CHUNK, SG_CHUNK), bool))
    w_tril = jnp.where(tril[None], small["sg_w_s"][0], 0.0).astype(BF16)
    b_lanes = jnp.broadcast_to(small["sg_b_s"][0][:, :, None], (SG_HEADS, SG_CHUNK, SG_DIM))
    conv_w = jnp.pad(small["sc_conv_w"][0], ((0, SUBLANES - CONV_TAPS), (0, 0)))
    ln_g = small["sg_ln_g"]
    pool_diag = jnp.zeros((POOL_WIDTH, POOL_WIDTH), F32)
    for g in range(len(POOL_WINDOWS)):
        pool_diag = pool_diag.at[POOL_DIM * g:POOL_DIM * (g + 1), POOL_DIM * g:POOL_DIM * (g + 1)].set(small["pool_w"][0, g])
    pool_diag = pool_diag.astype(BF16)
    pool_scale = small["pool_scale"]
    w_in_odd = jnp.pad(W["odd_w_in"], ((0, 0), (0, ODD_IN_PAD - ODD_IN)))
    q_b = jnp.pad(W["q_b"].reshape(Q_LORA, HEADS, QK_DIM).transpose(1, 0, 2), ((0, 0), (0, 0), (0, QK_PAD - QK_DIM)))
    kv_b = W["kv_b"].reshape(KV_LORA, HEADS, QK_NOPE + V_DIM).transpose(1, 0, 2)
    q_g = jnp.pad(small["q_norm"], ((0, 0), (0, QK_PAD - QK_DIM)))
    k_g = jnp.pad(small["k_norm"], ((0, 0), (0, QK_PAD - QK_DIM)))
    qa_g, kva_g = small["q_a_norm"], small["kv_a_norm"]
    ffn = [(small["ffn_norm"][l], W["ffn_w_gate"], W["ffn_w_up"], W["ffn_w_down"]) for l in range(2)]
    w_in_even = W["even_w_in"]
    in_shard = EVEN_IN // N_CHIPS

    h0 = _rmsnorm_fwd("mix0_norm", x0, small["mix_norm"][0], tm)
    tb = _big_tile(T)
    proj0 = _matmul("even_in", "nn", [(h0, w_in_even)],
                    [(BS((tb, D_MODEL), lambda i, j, k: (i, 0)), BS((None, D_MODEL, in_shard), lambda i, j, k: (j, 0, 0)))],
                    (T // tb, N_CHIPS, 1), _sds((T, EVEN_IN), F32), BS((tb, in_shard), lambda i, j, k: (i, j)), (tb, in_shard))
    mix0 = _even_mixer_fwd(proj0, ln_g, w_tril, b_lanes, conv_w, seq, tm)
    x1 = _mm("even_out", "nn", mix0, W["even_w_out"], F32, tk=1024, add=x0)
    x2, ffn0_saved = _ffn_fwd(0, x1, *ffn[0], tm)
    h2 = _rmsnorm_fwd("mix1_norm", x2, small["mix_norm"][1], tm)
    proj1 = _mm("odd_in", "nn", h2, w_in_odd, F32, tk=1024)
    mix1 = _pool_fwd(proj1, pool_diag, pool_scale, seq, tm)
    q, k, v = _mla_qkv_fwd(proj1, cos_t, sin_t, qa_g, kva_g, q_b, kv_b, q_g, k_g, tm)
    mix1, lse = _flash_fwd(q, k, v, mix1, batch, seq)
    x3 = _mm("odd_out", "nn", mix1, W["odd_w_out"], F32, tk=1024, add=x2)
    x4, ffn1_saved = _ffn_fwd(1, x3, *ffn[1], tm)
    dy, sq = _loss_head(x4, target.reshape(T, D_MODEL), tm)

    G = {}
    dx3, dffn_g1, dwg1, dwu1, dwd1 = _ffn_bwd(1, x3, *ffn[1], ffn1_saved, dy, tm)
    dmix1 = _mm("odd_out_dx", "nt", dx3, W["odd_w_out"], BF16, tk=1024)
    G["odd_w_out"] = _mm("odd_out_dw", "tn", mix1, dx3, F32)
    dq, dk, dv = _flash_bwd(q, k, v, dmix1, mix1, lse, batch, seq)
    dz_pool, dpool_diag, G["pool_scale"] = _pool_bwd(proj1, dmix1, pool_diag, pool_scale, seq, tm)
    dproj1, dq_b, dkv_b, dq_g, dk_g, G["q_a_norm"], G["kv_a_norm"] = _mla_qkv_bwd(
        proj1, cos_t, sin_t, qa_g, kva_g, q_b, kv_b, q_g, k_g, dq, dk, dv, dz_pool, tm)
    G["pool_w"] = jnp.stack([dpool_diag[POOL_DIM * g:POOL_DIM * (g + 1), POOL_DIM * g:POOL_DIM * (g + 1)]
                             for g in range(len(POOL_WINDOWS))])[None]
    G["q_b"] = dq_b[:, :, :QK_DIM].transpose(1, 0, 2).reshape(Q_LORA, HEADS * QK_DIM)
    G["kv_b"] = dkv_b.transpose(1, 0, 2).reshape(KV_LORA, HEADS * (QK_NOPE + V_DIM))
    G["q_norm"], G["k_norm"] = dq_g[:, :QK_DIM], dk_g[:, :QK_DIM]
    dh2 = _mm("odd_in_dx", "nt", dproj1, W["odd_w_in"], F32, tk=ODD_IN)
    G["odd_w_in"] = _mm("odd_in_dw", "tn", h2, dproj1, F32, tn=ODD_IN)
    dx2, dmix_g1 = _rmsnorm_bwd("mix1_norm_bwd", x2, small["mix_norm"][1], dh2, dx3, tm)
    dx1, dffn_g0, dwg0, dwu0, dwd0 = _ffn_bwd(0, x1, *ffn[0], ffn0_saved, dx2, tm)
    dmix0 = _mm("even_out_dx", "nt", dx1, W["even_w_out"], F32, tk=1024)
    G["even_w_out"] = _mm("even_out_dw", "tn", mix0, dx1, F32)
    dproj0, dw_s, db_lanes, G["sg_ln_g"], dconv = _even_mixer_bwd(proj0, dmix0, ln_g, w_tril, b_lanes, conv_w, seq, tm)
    G["sg_w_s"] = dw_s[None]
    G["sg_b_s"] = jnp.sum(db_lanes, axis=-1)[None]
    G["sc_conv_w"] = dconv[None, :CONV_TAPS]
    dh0 = _matmul("even_in_dx", "nt", [(dproj0, w_in_even)],
                  [(BS((tb, in_shard), lambda i, j, k: (i, k)), BS((None, D_MODEL, in_shard), lambda i, j, k: (k, 0, 0)))],
                  (T // tb, 1, N_CHIPS), _sds((T, D_MODEL), F32), BS((tb, D_MODEL), lambda i, j, k: (i, 0)), (tb, D_MODEL))
    tk = min(512, T)
    G["even_w_in"] = _grad_shards(
        "even_in_dw", h0, dproj0, BS((tk, D_MODEL), lambda k: (k, 0)), BS((tk, EVEN_IN), lambda k: (k, 0)),
        lambda a_ref, b_ref, j: (a_ref[...], b_ref[:, in_shard * j:in_shard * (j + 1)]), (N_CHIPS, D_MODEL, in_shard), T // tk)
    dx0, dmix_g0 = _rmsnorm_bwd("mix0_norm_bwd", x0, small["mix_norm"][0], dh0, dx1, tm)
    G["mix_norm"] = jnp.concatenate([dmix_g0, dmix_g1], axis=0)
    G["ffn_norm"] = jnp.concatenate([dffn_g0, dffn_g1], axis=0)
    G["ffn"] = [(dwg0, dwu0, dwd0), (dwg1, dwu1, dwd1)]
    return sq[0, 0], dx0.reshape(batch, seq, D_MODEL), G


def _small_vector(parts, names):
    flat = jnp.concatenate([parts[n].astype(F32).reshape(-1) for n in names])
    return _pad_rows(flat, LANES, SUBLANES)


def _split_small(vec, like, names):
    out, off, flat = {}, 0, vec.reshape(-1)
    for n in names:
        size = math.prod(like[n].shape)
        out[n] = flat[off:off + size].reshape(like[n].shape)
        off += size
    return out


def _whole_shape(a):
    return a.shape[:-1] + (a.shape[-1] * N_CHIPS,)


def kernel(x, positions, mix_norm, ffn_norm, even_w_in, sg_ln_g, sg_w_s, sg_b_s, sc_conv_w, even_w_out, odd_w_in, pool_w, pool_scale, q_a_norm, q_b, kv_a_norm, kv_b, q_norm, k_norm, odd_w_out, ffn_w_gate, ffn_w_up, ffn_w_down, loss_target, m_mix_norm, m_ffn_norm, m_even_w_in, m_sg_ln_g, m_sg_w_s, m_sg_b_s, m_sc_conv_w, m_even_w_out, m_odd_w_in, m_pool_w, m_pool_scale, m_q_a_norm, m_q_b, m_kv_a_norm, m_kv_b, m_q_norm, m_k_norm, m_odd_w_out, m_ffn_w_gate, m_ffn_w_up, m_ffn_w_down, v_mix_norm, v_ffn_norm, v_even_w_in, v_sg_ln_g, v_sg_w_s, v_sg_b_s, v_sc_conv_w, v_even_w_out, v_odd_w_in, v_pool_w, v_pool_scale, v_q_a_norm, v_q_b, v_kv_a_norm, v_kv_b, v_q_norm, v_k_norm, v_odd_w_out, v_ffn_w_gate, v_ffn_w_up, v_ffn_w_down):
    args = dict(locals())
    w = {n: args[n] for n in _WEIGHTS}
    m = {n: args["m_" + n] for n in _WEIGHTS}
    v = {n: args["v_" + n] for n in _WEIGHTS}
    cx, cy, cc = lax.axis_index("x"), lax.axis_index("y"), lax.axis_index("c")
    chip = 2 * cx + cy
    transposed = ("ffn_w_gate", "ffn_w_up")
    for n in transposed:
        w[n], m[n], v[n] = (jnp.swapaxes(t[n], 1, 2) for t in (w, m, v))

    chip_arr = chip.astype(jnp.int32).reshape(1)
    W = _gather_weights(w, chip_arr)
    placed = {}
    for n in _SMALL_SHARDED:
        a = w[n]
        whole = jnp.zeros(a.shape[:-1] + (N_CHIPS, a.shape[-1]), F32)
        whole = lax.dynamic_update_slice_in_dim(whole, a[..., None, :], chip, axis=a.ndim - 1)
        placed[n] = jnp.where(cc == 0, whole, 0.0).reshape(_whole_shape(a))
    small = dict({n: w[n] for n in _REPLICATED},
                 **_split_small(_all_reduce_small("gather_small_weights", _small_vector(placed, _SMALL_SHARDED)), placed, _SMALL_SHARDED))

    sq, grad_x, G = _forward_backward(x, positions, loss_target, W, small)
    loss = lax.psum(0.5 * sq / D_MODEL, ("x", "y", "c"))

    small_names = _REPLICATED + _SMALL_SHARDED
    summed = _split_small(_all_reduce_small("reduce_small_grads", _small_vector(G, small_names)), G, small_names)
    grads = {n: summed[n] for n in _REPLICATED}
    for n in _SMALL_SHARDED:
        a = w[n]
        grads[n] = lax.dynamic_slice_in_dim(summed[n].reshape(a.shape[:-1] + (N_CHIPS, a.shape[-1])), chip, 1,
                                            axis=a.ndim - 1).reshape(a.shape)

    def shard_major(g, cols):
        return g.reshape(g.shape[0], N_CHIPS, cols).transpose(1, 0, 2)

    big = [("even_w_in", G["even_w_in"]),
           ("even_w_out", G["even_w_out"].reshape(N_CHIPS, -1, D_MODEL)),
           ("odd_w_in", G["odd_w_in"].reshape(N_CHIPS, -1, ODD_IN)),
           ("q_b", shard_major(G["q_b"], HEADS * QK_DIM // N_CHIPS)),
           ("kv_b", shard_major(G["kv_b"], HEADS * (QK_NOPE + V_DIM) // N_CHIPS)),
           ("odd_w_out", G["odd_w_out"].reshape(N_CHIPS, -1, D_MODEL))]
    for l in range(2):
        big += [(f"ffn_w_gate{l}", G["ffn"][l][0]), (f"ffn_w_up{l}", G["ffn"][l][1]), (f"ffn_w_down{l}", G["ffn"][l][2])]
    names = [n for n, _ in big]
    halves = [g.reshape(N_CHIPS, 2, g.shape[1] // 2, g.shape[2]) for _, g in big]
    from_sibling = _sibling_exchange(halves)
    c_arr = cc.astype(jnp.int32).reshape(1)
    partial = [_add_halves(f"add_{n}", g, r, c_arr) for n, g, r in zip(names, halves, from_sibling)]
    scattered = _chip_scatter(partial)
    chip_c = jnp.stack([chip, cc]).astype(jnp.int32)
    sums = [_sum_partials(f"sum_{n}", p, r, chip_c) for n, p, r in zip(names, partial, scattered)]
    shard_grad = {n: f.reshape(1, -1, f.shape[-1]) for n, f in zip(names, _sibling_share(sums))}

    out = {}
    for n in ("even_w_in", "even_w_out", "odd_w_in", "q_b", "kv_b", "odd_w_out"):
        out[n] = _adamw(f"adamw_{n}", w[n], [shard_grad[n][0]], m[n], v[n])
    for n in ("ffn_w_gate", "ffn_w_up", "ffn_w_down"):
        out[n] = _adamw(f"adamw_{n}", w[n], [shard_grad[f"{n}{l}"][0] for l in range(2)], m[n], v[n])
    packed = [_small_vector(d, small_names) for d in (w, grads, m, v)]
    res = _adamw("adamw_small", packed[0][None], [packed[1]], packed[2][None], packed[3][None])
    delta_s, m_s, v_s = (_split_small(r, w, small_names) for r in res[1:])
    for n in small_names:
        out[n] = (grads[n], delta_s[n], m_s[n], v_s[n])
    for n in transposed:
        out[n] = tuple(jnp.swapaxes(t, 1, 2) for t in out[n])

    return (loss, grad_x, *[out[n][0] for n in _WEIGHTS], *[out[n][1] for n in _WEIGHTS],
            *[out[n][2] for n in _WEIGHTS], *[out[n][3] for n in _WEIGHTS])
```

```python
import functools
import math

import jax
import jax.numpy as jnp
from jax import lax
from jax.experimental import pallas as pl
from jax.experimental.pallas import tpu as pltpu

F32, BF16 = jnp.float32, jnp.bfloat16
BS = pl.BlockSpec

D_MODEL = 1024
EPS = 1e-6
NEG_INF = -1e30
SG_HEADS, SG_DIM, SG_WIDTH, SG_CHUNK = 4, 128, 512, 128
SC_WIDTH, CONV_TAPS = 512, 3
EVEN_IN = 2 * SG_WIDTH + 3 * SC_WIDTH
POOL_WINDOWS = (2, 4, 8, 16)
POOL_DIM, POOL_WIDTH = 64, 256
POOL_HALO = 16
HEADS, Q_LORA, KV_LORA, QK_NOPE, QK_ROPE, V_DIM = 6, 384, 256, 128, 64, 128
QK_DIM = QK_NOPE + QK_ROPE
QK_PAD = 256
ODD_IN = POOL_WIDTH + Q_LORA + KV_LORA + QK_ROPE
ODD_IN_PAD = 1024
ROPE_THETA = 10000.0
ATTN_SCALE = QK_DIM ** -0.5
D_FF, N_CHIPS = 2816, 4
FF_SHARD = D_FF // N_CHIPS
ADAM_LR, ADAM_B1, ADAM_B2, ADAM_EPS, ADAM_WD, ADAM_STEP = 0.001, 0.9, 0.999, 1e-08, 0.01, 10
VMEM_LIMIT_V7X = 48 * 2**20
LANES, SUBLANES = 128, 8
MESH = pl.DeviceIdType.MESH
HBM = pl.BlockSpec(memory_space=pltpu.HBM)
VMEM = pl.BlockSpec(memory_space=pltpu.VMEM)

_DIMS = {"nn": (((1,), (0,)), ((), ())), "nt": (((1,), (1,)), ((), ())), "tn": (((0,), (0,)), ((), ()))}


def _dot(a, b, mode="nn"):
    return lax.dot_general(a.astype(BF16), b.astype(BF16), _DIMS[mode], preferred_element_type=F32)


def _call(body, *, name, out_shape, in_specs, out_specs, grid=(), scratch=(), aliases=None, after=()):
    params = pltpu.CompilerParams(vmem_limit_bytes=VMEM_LIMIT_V7X,
                                  **({"dimension_semantics": ("arbitrary",) * len(grid)} if grid else {}))
    n_in, n_after = len(in_specs), len(after)
    kernel_body = body if not after else (lambda *refs: body(*refs[:n_in], *refs[n_in + n_after:]))
    call = pl.pallas_call(kernel_body, name=name, grid=grid, in_specs=list(in_specs) + [pl.BlockSpec(memory_space=pl.ANY)] * n_after,
                          out_specs=out_specs, out_shape=out_shape, scratch_shapes=list(scratch),
                          input_output_aliases=aliases or {}, compiler_params=params)
    return (lambda *ops: call(*ops, *after)) if after else call


def _sds(shape, dtype):
    return jax.ShapeDtypeStruct(tuple(shape), dtype)


def _token_tile(seq):
    return 512 if seq % 512 == 0 else seq


def _matmul(name, mode, pairs, pair_specs, grid, out_shape, out_spec, acc_shape, add=None, add_spec=None, after=()):
    n, nk = len(pairs), grid[-1]

    def body(*refs):
        ab = refs[:2 * n]
        add_ref = refs[2 * n] if add is not None else None

        def finish(r, o_ref):
            if add_ref is not None:
                r = r + add_ref[...]
            o_ref[...] = r.astype(o_ref.dtype)

        if nk == 1:
            r = _dot(ab[0][...], ab[1][...], mode)
            for p in range(1, n):
                r = r + _dot(ab[2 * p][...], ab[2 * p + 1][...], mode)
            finish(r, refs[-1])
            return
        o_ref, acc = refs[-2], refs[-1]
        k = pl.program_id(len(grid) - 1)

        @pl.when(k == 0)
        def _():
            acc[...] = jnp.zeros_like(acc)

        for p in range(n):
            acc[...] += _dot(ab[2 * p][...], ab[2 * p + 1][...], mode)

        @pl.when(k == nk - 1)
        def _():
            finish(acc[...], o_ref)

    ops = [t for pr in pairs for t in pr] + ([add] if add is not None else [])
    specs = [s for pr in pair_specs for s in pr] + ([add_spec] if add is not None else [])
    return _call(body, name=name, grid=grid, in_specs=specs, out_specs=out_spec, out_shape=out_shape,
                 scratch=[pltpu.VMEM(acc_shape, F32)] if nk > 1 else [], after=after)(*ops)


def _grad_shards(name, a, b, a_spec, b_spec, pick, out_shape, n_steps):
    def body(a_ref, b_ref, o_ref):
        @pl.when(pl.program_id(0) == 0)
        def _():
            o_ref[...] = jnp.zeros_like(o_ref)

        for j in range(N_CHIPS):
            aj, bj = pick(a_ref, b_ref, j)
            o_ref[j] += _dot(aj, bj, "tn")

    return _call(body, name=name, grid=(n_steps,), in_specs=[a_spec, b_spec],
                 out_specs=BS(out_shape, lambda k: (0, 0, 0)), out_shape=_sds(out_shape, F32))(a, b)


def _mm(name, mode, a, b, out_dtype, tm=1024, tn=1024, tk=512, add=None, after=()):
    if mode == "tn":
        (K, M), N = a.shape, b.shape[1]
    else:
        (M, K), N = a.shape, (b.shape[1] if mode == "nn" else b.shape[0])
    tm, tn, tk = min(tm, M), min(tn, N), min(tk, K)
    a_spec = BS((tk, tm), lambda i, j, k: (k, i)) if mode == "tn" else BS((tm, tk), lambda i, j, k: (i, k))
    b_spec = BS((tn, tk), lambda i, j, k: (j, k)) if mode == "nt" else BS((tk, tn), lambda i, j, k: (k, j))
    o_spec = BS((tm, tn), lambda i, j, k: (i, j))
    return _matmul(name, mode, [(a, b)], [(a_spec, b_spec)], (M // tm, N // tn, K // tk), _sds((M, N), out_dtype),
                   o_spec, (tm, tn), add=add, add_spec=o_spec if add is not None else None, after=after)


def _rmsnorm_fwd(name, x, g, tm):
    T, d = x.shape

    def body(x_ref, g_ref, o_ref):
        xv = x_ref[...]
        y = xv * lax.rsqrt(jnp.mean(xv * xv, axis=-1, keepdims=True) + EPS)
        o_ref[...] = (y * g_ref[...]).astype(o_ref.dtype)

    return _call(body, name=name, grid=(T // tm,), in_specs=[BS((tm, d), lambda i: (i, 0)), BS((1, d), lambda i: (0, 0))],
                 out_specs=BS((tm, d), lambda i: (i, 0)), out_shape=_sds((T, d), BF16))(x, g.reshape(1, d))


def _rmsnorm_bwd(name, x, g, dh, dres, tm):
    T, d = x.shape

    def body(x_ref, g_ref, dh_ref, dres_ref, dx_ref, dg_ref):
        xv = x_ref[...]
        r = lax.rsqrt(jnp.mean(xv * xv, axis=-1, keepdims=True) + EPS)
        xhat = xv * r
        dhv = dh_ref[...]

        @pl.when(pl.program_id(0) == 0)
        def _():
            dg_ref[...] = jnp.zeros_like(dg_ref)

        dg_ref[...] += jnp.sum(dhv * xhat, axis=0, keepdims=True)
        dxhat = dhv * g_ref[...]
        dx_ref[...] = dres_ref[...] + r * (dxhat - xhat * jnp.mean(dxhat * xhat, axis=-1, keepdims=True))

    row = BS((tm, d), lambda i: (i, 0))
    vec = BS((1, d), lambda i: (0, 0))
    return _call(body, name=name, grid=(T // tm,), in_specs=[row, vec, row, row], out_specs=[row, vec],
                 out_shape=[_sds((T, d), F32), _sds((1, d), F32)])(x, g.reshape(1, d), dh, dres)


def _ffn_up(name, h, wg, wu, tm):
    T = h.shape[0]

    def body(h_ref, wg_ref, wu_ref, g_ref, u_ref, a_ref):
        hv = h_ref[...]
        g = _dot(hv, wg_ref[...], "nt")
        u = _dot(hv, wu_ref[...], "nt")
        g_ref[...] = g.astype(BF16)
        u_ref[...] = u.astype(BF16)
        a_ref[...] = (g * (1.0 / (1.0 + jnp.exp(-g))) * u).astype(BF16)

    w_spec = BS((None, FF_SHARD, D_MODEL), lambda j, i: (j, 0, 0))
    o_spec = BS((None, tm, FF_SHARD), lambda j, i: (j, i, 0))
    sh = _sds((N_CHIPS, T, FF_SHARD), BF16)
    return _call(body, name=name, grid=(N_CHIPS, T // tm), in_specs=[BS((tm, D_MODEL), lambda j, i: (i, 0)), w_spec, w_spec],
                 out_specs=[o_spec, o_spec, o_spec], out_shape=[sh, sh, sh])(h, wg, wu)


def _ffn_act_bwd(name, dxo, wd, g, u, tm, after=()):
    T = dxo.shape[0]

    def body(dx_ref, wd_ref, g_ref, u_ref, dg_ref, du_ref):
        da = _dot(dx_ref[...], wd_ref[...], "nt")
        g = g_ref[...].astype(F32)
        sig = 1.0 / (1.0 + jnp.exp(-g))
        dg_ref[...] = (da * u_ref[...].astype(F32) * (sig * (1.0 + g * (1.0 - sig)))).astype(BF16)
        du_ref[...] = (da * (g * sig)).astype(BF16)

    t_spec = BS((None, tm, FF_SHARD), lambda i, j: (j, i, 0))
    sh = _sds((N_CHIPS, T, FF_SHARD), BF16)
    return _call(body, name=name, grid=(T // tm, N_CHIPS),
                 in_specs=[BS((tm, D_MODEL), lambda i, j: (i, 0)), BS((None, FF_SHARD, D_MODEL), lambda i, j: (j, 0, 0)), t_spec, t_spec],
                 out_specs=[t_spec, t_spec], out_shape=[sh, sh], after=after)(dxo, wd, g, u)


def _big_tile(n):
    return min(1024, n)


def _ffn_fwd(l, x, gain, wg, wu, wd, tm):
    T = x.shape[0]
    h = _rmsnorm_fwd(f"ffn{l}_norm", x, gain, tm)
    tm = _big_tile(T)
    g, u, a = _ffn_up(f"ffn{l}_up", h, wg, wu, tm)
    tn = D_MODEL
    out = _matmul(f"ffn{l}_down", "nn", [(a, wd)],
                  [(BS((None, tm, FF_SHARD), lambda i, j, k: (k, i, 0)), BS((None, FF_SHARD, tn), lambda i, j, k: (k, 0, j)))],
                  (T // tm, D_MODEL // tn, N_CHIPS), _sds((T, D_MODEL), F32), BS((tm, tn), lambda i, j, k: (i, j)), (tm, tn),
                  add=x, add_spec=BS((tm, tn), lambda i, j, k: (i, j)))
    return out, (h, g, u, a)


def _ffn_bwd(l, x, gain, wg, wu, wd, saved, dxo, tm, after=()):
    h, g, u, a = saved
    T = x.shape[0]
    tm_norm, tm = tm, _big_tile(T)
    dg, du = _ffn_act_bwd(f"ffn{l}_act_bwd", dxo, wd, g, u, tm, after=after)
    tk = min(512, T)
    tn = D_MODEL
    shards_spec = BS((N_CHIPS, tk, FF_SHARD), lambda k: (0, k, 0))
    rows_spec = BS((tk, D_MODEL), lambda k: (k, 0))

    def dw(nm, act, rows):
        return _grad_shards(nm, act, rows, shards_spec, rows_spec, lambda a_ref, b_ref, j: (a_ref[j], b_ref[...]),
                            (N_CHIPS, FF_SHARD, D_MODEL), T // tk)

    dwd, dwg, dwu = dw(f"ffn{l}_dwd", a, dxo), dw(f"ffn{l}_dwg", dg, h), dw(f"ffn{l}_dwu", du, h)
    act_spec = BS((None, tm, FF_SHARD), lambda i, j, k: (k, i, 0))
    w_spec = BS((None, FF_SHARD, tn), lambda i, j, k: (k, 0, j))
    dh = _matmul(f"ffn{l}_dh", "nn", [(dg, wg), (du, wu)], [(act_spec, w_spec), (act_spec, w_spec)],
                 (T // tm, D_MODEL // tn, N_CHIPS), _sds((T, D_MODEL), F32), BS((tm, tn), lambda i, j, k: (i, j)), (tm, tn))
    dx, dgain = _rmsnorm_bwd(f"ffn{l}_norm_bwd", x, gain, dh, dxo, tm_norm)
    return dx, dgain, dwg, dwu, dwd


_INV_SQRT2 = 1.0 / math.sqrt(2.0)
_INV_SQRT_2PI = 1.0 / math.sqrt(2.0 * math.pi)


def _gelu(x):
    return 0.5 * x * (1.0 + lax.erf(x * _INV_SQRT2))


def _gelu_grad(x):
    return 0.5 * (1.0 + lax.erf(x * _INV_SQRT2)) + x * jnp.exp(-0.5 * x * x) * _INV_SQRT_2PI


def _shift_down(x, k):
    return pltpu.roll(x, k, 0)


def _shift_up(x, k):
    return pltpu.roll(x, x.shape[0] - k, 0)


def _layer_norm_head(xh):
    xc = xh - jnp.mean(xh, axis=-1, keepdims=True)
    rstd = lax.rsqrt(jnp.mean(xc * xc, axis=-1, keepdims=True) + EPS)
    return xc * rstd, rstd


def _even_halo_specs(tm, n_tiles, col_blocks, after):
    rows = tm // SUBLANES
    last = n_tiles * rows - 1
    if after:
        return [BS((SUBLANES, 512), functools.partial(lambda cb, i: (jnp.minimum((i + 1) * rows, last), cb), cb)) for cb in col_blocks]
    return [BS((SUBLANES, 512), functools.partial(lambda cb, i: (jnp.maximum(i * rows - 1, 0), cb), cb)) for cb in col_blocks]


def _even_mixer_fwd(proj, ln_g, w_tril, b_lanes, conv_w, seq, tm):
    T = proj.shape[0]
    tiles_per_seq = seq // tm

    def body(p_ref, hc_ref, hh_ref, lng_ref, w_ref, bb_ref, cw_ref, o_ref):
        first = pl.program_id(0) % tiles_per_seq == 0
        for h in range(SG_HEADS):
            cols = slice(SG_DIM * h, SG_DIM * (h + 1))
            vhat, _ = _layer_norm_head(_gelu(p_ref[:, SG_WIDTH + SG_DIM * h:SG_WIDTH + SG_DIM * (h + 1)]))
            vln = (vhat * lng_ref[:, cols]).astype(BF16)
            for k in range(tm // SG_CHUNK):
                rows = slice(SG_CHUNK * k, SG_CHUNK * (k + 1))
                mixed = _dot(w_ref[h], vln[rows]) + bb_ref[h]
                o_ref[rows, cols] = (_gelu(p_ref[rows, cols]) * mixed).astype(BF16)
        z = p_ref[:, 1536:2048] * p_ref[:, 2048:2560]
        zz = jnp.concatenate([jnp.where(first, 0.0, hc_ref[...] * hh_ref[...]), z], axis=0)
        y = cw_ref[0:1, :] * _shift_down(zz, 2)[SUBLANES:] + cw_ref[1:2, :] * _shift_down(zz, 1)[SUBLANES:] + cw_ref[2:3, :] * z
        o_ref[:, SG_WIDTH:] = (p_ref[:, 1024:1536] * y).astype(BF16)

    full = lambda shape: BS(shape, lambda i: (0,) * len(shape))
    return _call(body, name="even_mixer_fwd", grid=(T // tm,),
                 in_specs=[BS((tm, EVEN_IN), lambda i: (i, 0))] + _even_halo_specs(tm, T // tm, (3, 4), after=False)
                 + [full((1, SG_WIDTH)), full((SG_HEADS, SG_CHUNK, SG_CHUNK)), full((SG_HEADS, SG_CHUNK, SG_DIM)), full((SUBLANES, SC_WIDTH))],
                 out_specs=BS((tm, D_MODEL), lambda i: (i, 0)), out_shape=_sds((T, D_MODEL), BF16))(
        proj, proj, proj, ln_g, w_tril, b_lanes, conv_w)


def _even_mixer_bwd(proj, dmix, ln_g, w_tril, b_lanes, conv_w, seq, tm):
    T = proj.shape[0]
    n_tiles, tiles_per_seq = T // tm, seq // tm

    def body(p_ref, dm_ref, hc_ref, hh_ref, nd_ref, nb_ref, lng_ref, w_ref, bb_ref, cw_ref,
             dp_ref, dw_ref, db_ref, dlng_ref, dcw_ref):
        i = pl.program_id(0)
        first = i % tiles_per_seq == 0
        last = i % tiles_per_seq == tiles_per_seq - 1

        @pl.when(i == 0)
        def _():
            dw_ref[...] = jnp.zeros_like(dw_ref)
            db_ref[...] = jnp.zeros_like(db_ref)
            dlng_ref[...] = jnp.zeros_like(dlng_ref)
            dcw_ref[...] = jnp.zeros_like(dcw_ref)

        for h in range(SG_HEADS):
            cols = slice(SG_DIM * h, SG_DIM * (h + 1))
            vcols = slice(SG_WIDTH + SG_DIM * h, SG_WIDTH + SG_DIM * (h + 1))
            lng = lng_ref[:, cols]
            for k in range(tm // SG_CHUNK):
                rows = slice(SG_CHUNK * k, SG_CHUNK * (k + 1))
                v = p_ref[rows, vcols]
                vhat, rstd = _layer_norm_head(_gelu(v))
                vln = (vhat * lng).astype(BF16)
                mixed = _dot(w_ref[h], vln) + bb_ref[h]
                u = p_ref[rows, cols]
                da = dm_ref[rows, cols]
                dp_ref[rows, cols] = (da * mixed * _gelu_grad(u)).astype(BF16)
                dmixed = da * _gelu(u)
                db_ref[h] += dmixed
                dw_ref[h] += _dot(dmixed, vln, "nt")
                dvln = _dot(w_ref[h], dmixed, "tn")
                dlng_ref[:, cols] += jnp.sum(dvln * vhat, axis=0, keepdims=True)
                dvhat = dvln * lng
                dgv = rstd * (dvhat - jnp.mean(dvhat, axis=-1, keepdims=True)
                              - vhat * jnp.mean(dvhat * vhat, axis=-1, keepdims=True))
                dp_ref[rows, vcols] = (dgv * _gelu_grad(v)).astype(BF16)

        b = p_ref[:, 1024:1536]
        c = p_ref[:, 1536:2048]
        hv = p_ref[:, 2048:2560]
        z = c * hv
        zz = jnp.concatenate([jnp.where(first, 0.0, hc_ref[...] * hh_ref[...]), z], axis=0)
        z1 = _shift_down(zz, 1)[SUBLANES:]
        z2 = _shift_down(zz, 2)[SUBLANES:]
        w0, w1, w2 = cw_ref[0:1, :], cw_ref[1:2, :], cw_ref[2:3, :]
        dbo = dm_ref[:, SG_WIDTH:]
        dy = dbo * b
        dd = jnp.concatenate([dy, jnp.where(last, 0.0, nd_ref[...] * nb_ref[...])], axis=0)
        dz = w2 * dy + w1 * _shift_up(dd, 1)[:tm] + w0 * _shift_up(dd, 2)[:tm]
        dp_ref[:, 1024:1536] = (dbo * (w0 * z2 + w1 * z1 + w2 * z)).astype(BF16)
        dp_ref[:, 1536:2048] = (dz * hv).astype(BF16)
        dp_ref[:, 2048:2560] = (dz * c).astype(BF16)
        dcw_ref[0:1, :] += jnp.sum(dy * z2, axis=0, keepdims=True)
        dcw_ref[1:2, :] += jnp.sum(dy * z1, axis=0, keepdims=True)
        dcw_ref[2:3, :] += jnp.sum(dy * z, axis=0, keepdims=True)

        @pl.when(i == n_tiles - 1)
        def _():
            t_idx = lax.broadcasted_iota(jnp.int32, (SG_CHUNK, SG_CHUNK), 0)
            s_idx = lax.broadcasted_iota(jnp.int32, (SG_CHUNK, SG_CHUNK), 1)
            for h in range(SG_HEADS):
                dw_ref[h] = jnp.where(t_idx >= s_idx, dw_ref[h], 0.0)

    full = lambda shape: BS(shape, lambda i: (0,) * len(shape))
    sq = (SG_HEADS, SG_CHUNK, SG_CHUNK)
    return _call(body, name="even_mixer_bwd", grid=(n_tiles,),
                 in_specs=[BS((tm, EVEN_IN), lambda i: (i, 0)), BS((tm, D_MODEL), lambda i: (i, 0))]
                 + _even_halo_specs(tm, n_tiles, (3, 4), after=False)
                 + _even_halo_specs(tm, n_tiles, (1,), after=True) + _even_halo_specs(tm, n_tiles, (2,), after=True)
                 + [full((1, SG_WIDTH)), full(sq), full(sq), full((SUBLANES, SC_WIDTH))],
                 out_specs=[BS((tm, EVEN_IN), lambda i: (i, 0)), full(sq), full(sq), full((1, SG_WIDTH)), full((SUBLANES, SC_WIDTH))],
                 out_shape=[_sds((T, EVEN_IN), BF16), _sds(sq, F32), _sds(sq, F32), _sds((1, SG_WIDTH), F32), _sds((SUBLANES, SC_WIDTH), F32)])(
        proj, dmix, proj, proj, dmix, proj, ln_g, w_tril, b_lanes, conv_w)


def _pool_select(vals):
    lane = lax.broadcasted_iota(jnp.int32, vals[0].shape, 1)
    out = vals[-1]
    for g in range(len(vals) - 2, -1, -1):
        out = jnp.where(lane < POOL_DIM * (g + 1), vals[g], out)
    return out


def _pool_counts(pos1):
    lane = lax.broadcasted_iota(jnp.int32, (pos1.shape[0], POOL_WIDTH), 1)
    win = _pool_select([jnp.full(lane.shape, float(w), F32) for w in POOL_WINDOWS])
    return jnp.minimum(pos1, win)


def _pool_means(zz, counts):
    s2 = zz + _shift_down(zz, 1)
    s4 = s2 + _shift_down(s2, 2)
    s8 = s4 + _shift_down(s4, 4)
    s16 = s8 + _shift_down(s8, 8)
    return _pool_select([s2, s4, s8, s16])[POOL_HALO:] / counts


def _pool_halo_spec(tm, n_tiles, after):
    rows = tm // POOL_HALO
    if after:
        return BS((POOL_HALO, POOL_WIDTH), lambda i: (jnp.minimum((i + 1) * rows, n_tiles * rows - 1), 0))
    return BS((POOL_HALO, POOL_WIDTH), lambda i: (jnp.maximum(i * rows - 1, 0), 0))


def _pool_fwd(proj, w_diag, scale, seq, tm):
    T = proj.shape[0]
    tiles_per_seq = seq // tm

    def body(z_ref, zh_ref, w_ref, s_ref, o_ref):
        t = pl.program_id(0) % tiles_per_seq
        z = z_ref[...]
        zz = jnp.concatenate([jnp.where(t == 0, 0.0, zh_ref[...]), z], axis=0)
        pos1 = (lax.broadcasted_iota(jnp.int32, (tm, 1), 0) + (t * tm + 1)).astype(F32)
        pooled = _pool_means(zz, _pool_counts(pos1)) - z
        o_ref[...] = (_dot(pooled, w_ref[...]) * s_ref[...]).astype(BF16)

    full = lambda shape: BS(shape, lambda i: (0,) * len(shape))
    return _call(body, name="pool_fwd", grid=(T // tm,),
                 in_specs=[BS((tm, POOL_WIDTH), lambda i: (i, 0)), _pool_halo_spec(tm, T // tm, False),
                           full((POOL_WIDTH, POOL_WIDTH)), full((1, POOL_WIDTH))],
                 out_specs=BS((tm, POOL_WIDTH), lambda i: (i, 0)), out_shape=_sds((T, D_MODEL), BF16))(proj, proj, w_diag, scale)


def _pool_bwd(proj, dmix, w_diag, scale, seq, tm):
    T = proj.shape[0]
    n_tiles, tiles_per_seq = T // tm, seq // tm

    def body(z_ref, zh_ref, do_ref, don_ref, w_ref, s_ref, dz_ref, dw_ref, ds_ref):
        i = pl.program_id(0)
        t = i % tiles_per_seq

        @pl.when(i == 0)
        def _():
            dw_ref[...] = jnp.zeros_like(dw_ref)
            ds_ref[...] = jnp.zeros_like(ds_ref)

        z = z_ref[...]
        zz = jnp.concatenate([jnp.where(t == 0, 0.0, zh_ref[...]), z], axis=0)
        pos1 = (lax.broadcasted_iota(jnp.int32, (tm, 1), 0) + (t * tm + 1)).astype(F32)
        counts = _pool_counts(pos1)
        pooled = _pool_means(zz, counts) - z
        dout = do_ref[...].astype(F32)
        ds_ref[...] += jnp.sum(dout * _dot(pooled, w_ref[...]), axis=0, keepdims=True)
        dlin = dout * s_ref[...]
        dw_ref[...] += _dot(pooled, dlin, "tn")
        dpooled = _dot(dlin, w_ref[...], "nt")
        dpooled_n = _dot(don_ref[...].astype(F32) * s_ref[...], w_ref[...], "nt")
        pos1_n = (lax.broadcasted_iota(jnp.int32, (POOL_HALO, 1), 0) + ((t + 1) * tm + 1)).astype(F32)
        dmean_n = jnp.where(t == tiles_per_seq - 1, 0.0, dpooled_n / _pool_counts(pos1_n))
        dd = jnp.concatenate([dpooled / counts, dmean_n], axis=0)
        r2 = dd + _shift_up(dd, 1)
        r4 = r2 + _shift_up(r2, 2)
        r8 = r4 + _shift_up(r4, 4)
        r16 = r8 + _shift_up(r8, 8)
        dz_ref[...] = (_pool_select([r2, r4, r8, r16])[:tm] - dpooled).astype(BF16)

    full = lambda shape: BS(shape, lambda i: (0,) * len(shape))
    return _call(body, name="pool_bwd", grid=(n_tiles,),
                 in_specs=[BS((tm, POOL_WIDTH), lambda i: (i, 0)), _pool_halo_spec(tm, n_tiles, False),
                           BS((tm, POOL_WIDTH), lambda i: (i, 0)), _pool_halo_spec(tm, n_tiles, True),
                           full((POOL_WIDTH, POOL_WIDTH)), full((1, POOL_WIDTH))],
                 out_specs=[BS((tm, POOL_WIDTH), lambda i: (i, 0)), full((POOL_WIDTH, POOL_WIDTH)), full((1, POOL_WIDTH))],
                 out_shape=[_sds((T, POOL_WIDTH), BF16), _sds((POOL_WIDTH, POOL_WIDTH), F32), _sds((1, POOL_WIDTH), F32)])(
        proj, proj, dmix, dmix, w_diag, scale)


def _rope_partner(r):
    lane = lax.broadcasted_iota(jnp.int32, r.shape, 1)
    return jnp.where(lane < QK_ROPE // 2, pltpu.roll(r, LANES - QK_ROPE // 2, 1), pltpu.roll(r, QK_ROPE // 2, 1))


def _rope(x, cos, sin_signed):
    r = x[:, QK_NOPE:]
    return jnp.concatenate([x[:, :QK_NOPE], r * cos + _rope_partner(r) * sin_signed], axis=1)


def _rope_transposed(dx, cos, sin_signed):
    dr = dx[:, QK_NOPE:]
    return jnp.concatenate([dx[:, :QK_NOPE], dr * cos + _rope_partner(dr * sin_signed)], axis=1)


def _head_norm(x):
    r = lax.rsqrt(jnp.sum(x * x, axis=-1, keepdims=True) * (1.0 / QK_DIM) + EPS)
    return x * r, r


def _head_norm_bwd(dy, xhat, r, gain):
    dxhat = dy * gain
    return r * (dxhat - xhat * (jnp.sum(dxhat * xhat, axis=-1, keepdims=True) * (1.0 / QK_DIM)))


def _latents(p_ref, qag_ref, kvag_ref):
    ql = p_ref[:, POOL_WIDTH:POOL_WIDTH + Q_LORA]
    kvl = p_ref[:, POOL_WIDTH + Q_LORA:POOL_WIDTH + Q_LORA + KV_LORA]
    rq = lax.rsqrt(jnp.mean(ql * ql, axis=-1, keepdims=True) + EPS)
    rkv = lax.rsqrt(jnp.mean(kvl * kvl, axis=-1, keepdims=True) + EPS)
    return ql * rq, rq, kvl * rkv, rkv


def _mla_specs(tm):
    full = lambda shape: BS(shape, lambda i, h: (0,) * len(shape))
    return [BS((tm, ODD_IN_PAD), lambda i, h: (i, 0)), BS((tm, LANES), lambda i, h: (i, 0)), BS((tm, LANES), lambda i, h: (i, 0)),
            full((1, Q_LORA)), full((1, KV_LORA)), BS((None, Q_LORA, QK_PAD), lambda i, h: (h, 0, 0)),
            BS((None, KV_LORA, QK_PAD), lambda i, h: (h, 0, 0)), full((1, QK_PAD)), full((1, QK_PAD))]


def _mla_qkv_fwd(proj, cos, sin_signed, qa_g, kva_g, q_b, kv_b, q_g, k_g, tm):
    T = proj.shape[0]

    def body(p_ref, cos_ref, sin_ref, qag_ref, kvag_ref, qb_ref, kvb_ref, qg_ref, kg_ref, q_ref, k_ref, v_ref, qn_s, kvn_s):
        @pl.when(pl.program_id(1) == 0)
        def _():
            qhat, _, kvhat, _ = _latents(p_ref, qag_ref, kvag_ref)
            qn_s[...] = (qhat * qag_ref[...]).astype(BF16)
            kvn_s[...] = (kvhat * kvag_ref[...]).astype(BF16)

        cos, sin = cos_ref[...], sin_ref[...]
        qhat, _ = _head_norm(_dot(qn_s[...], qb_ref[...]))
        q_ref[...] = _rope(qhat * qg_ref[...], cos, sin).astype(BF16)
        kv = _dot(kvn_s[...], kvb_ref[...])
        khat, _ = _head_norm(jnp.concatenate([kv[:, :QK_NOPE], p_ref[:, ODD_IN_PAD - LANES:]], axis=1))
        k_ref[...] = _rope(khat * kg_ref[...], cos, sin).astype(BF16)
        v_ref[...] = kv[:, QK_NOPE:].astype(BF16)

    qk_spec = BS((None, tm, QK_PAD), lambda i, h: (h, i, 0))
    return _call(body, name="mla_qkv_fwd", grid=(T // tm, HEADS), in_specs=_mla_specs(tm),
                 out_specs=[qk_spec, qk_spec, BS((None, tm, V_DIM), lambda i, h: (h, i, 0))],
                 out_shape=[_sds((HEADS, T, QK_PAD), BF16), _sds((HEADS, T, QK_PAD), BF16), _sds((HEADS, T, V_DIM), BF16)],
                 scratch=[pltpu.VMEM((tm, Q_LORA), BF16), pltpu.VMEM((tm, KV_LORA), BF16)])(
        proj, cos, sin_signed, qa_g, kva_g, q_b, kv_b, q_g, k_g)


def _mla_qkv_bwd(proj, cos, sin_signed, qa_g, kva_g, q_b, kv_b, q_g, k_g, dq, dk, dv, dz_pool, tm):
    T = proj.shape[0]
    n_tiles = T // tm

    def body(p_ref, cos_ref, sin_ref, qag_ref, kvag_ref, qb_ref, kvb_ref, qg_ref, kg_ref, dq_ref, dk_ref, dv_ref, dzp_ref,
             dp_ref, dqb_ref, dkvb_ref, dqg_ref, dkg_ref, dqag_ref, dkvag_ref, qn_s, kvn_s, dqn_s, dkvn_s, dkr_s):
        i, h = pl.program_id(0), pl.program_id(1)

        @pl.when((i == 0) & (h == 0))
        def _():
            for ref in (dqb_ref, dkvb_ref, dqg_ref, dkg_ref, dqag_ref, dkvag_ref):
                ref[...] = jnp.zeros_like(ref)

        @pl.when(h == 0)
        def _():
            qhat, _, kvhat, _ = _latents(p_ref, qag_ref, kvag_ref)
            qn_s[...] = (qhat * qag_ref[...]).astype(BF16)
            kvn_s[...] = (kvhat * kvag_ref[...]).astype(BF16)
            dqn_s[...] = jnp.zeros_like(dqn_s)
            dkvn_s[...] = jnp.zeros_like(dkvn_s)
            dkr_s[...] = jnp.zeros_like(dkr_s)

        cos, sin = cos_ref[...], sin_ref[...]
        qhat, rq = _head_norm(_dot(qn_s[...], qb_ref[...]))
        dqn_head = _rope_transposed(dq_ref[...], cos, sin)
        dqg_ref[...] += jnp.sum(dqn_head * qhat, axis=0, keepdims=True)
        dqh = _head_norm_bwd(dqn_head, qhat, rq, qg_ref[...])
        dqb_ref[h] += _dot(qn_s[...], dqh, "tn")
        dqn_s[...] += _dot(dqh, qb_ref[...], "nt")

        kv = _dot(kvn_s[...], kvb_ref[...])
        khat, rk = _head_norm(jnp.concatenate([kv[:, :QK_NOPE], p_ref[:, ODD_IN_PAD - LANES:]], axis=1))
        dkn_head = _rope_transposed(dk_ref[...], cos, sin)
        dkg_ref[...] += jnp.sum(dkn_head * khat, axis=0, keepdims=True)
        dkf = _head_norm_bwd(dkn_head, khat, rk, kg_ref[...])
        dkr_s[...] += dkf[:, QK_NOPE:]
        dkv = jnp.concatenate([dkf[:, :QK_NOPE], dv_ref[...]], axis=1)
        dkvb_ref[h] += _dot(kvn_s[...], dkv, "tn")
        dkvn_s[...] += _dot(dkv, kvb_ref[...], "nt")

        @pl.when(h == HEADS - 1)
        def _():
            qhat_l, rql, kvhat_l, rkvl = _latents(p_ref, qag_ref, kvag_ref)
            dqn, dkvn = dqn_s[...], dkvn_s[...]
            dqag_ref[...] += jnp.sum(dqn * qhat_l, axis=0, keepdims=True)
            dkvag_ref[...] += jnp.sum(dkvn * kvhat_l, axis=0, keepdims=True)
            dqx, dkvx = dqn * qag_ref[...], dkvn * kvag_ref[...]
            dp_ref[:, :POOL_WIDTH] = dzp_ref[...]
            dp_ref[:, POOL_WIDTH:POOL_WIDTH + Q_LORA] = (
                rql * (dqx - qhat_l * jnp.mean(dqx * qhat_l, axis=-1, keepdims=True))).astype(BF16)
            dp_ref[:, POOL_WIDTH + Q_LORA:ODD_IN_PAD - LANES] = (
                rkvl * (dkvx - kvhat_l * jnp.mean(dkvx * kvhat_l, axis=-1, keepdims=True))).astype(BF16)
            dp_ref[:, ODD_IN_PAD - LANES:] = dkr_s[:, :QK_ROPE].astype(BF16)

    full = lambda shape: BS(shape, lambda i, h: (0,) * len(shape))
    qk_spec = BS((None, tm, QK_PAD), lambda i, h: (h, i, 0))
    return _call(body, name="mla_qkv_bwd", grid=(n_tiles, HEADS),
                 in_specs=_mla_specs(tm) + [qk_spec, qk_spec, BS((None, tm, V_DIM), lambda i, h: (h, i, 0)),
                                            BS((tm, POOL_WIDTH), lambda i, h: (i, 0))],
                 out_specs=[BS((tm, ODD_IN), lambda i, h: (i, 0)), full((HEADS, Q_LORA, QK_PAD)), full((HEADS, KV_LORA, QK_PAD)),
                            full((1, QK_PAD)), full((1, QK_PAD)), full((1, Q_LORA)), full((1, KV_LORA))],
                 out_shape=[_sds((T, ODD_IN), BF16),_sds((HEADS, Q_LORA, QK_PAD), F32), _sds((HEADS, KV_LORA, QK_PAD), F32),
                            _sds((1, QK_PAD), F32), _sds((1, QK_PAD), F32), _sds((1, Q_LORA), F32), _sds((1, KV_LORA), F32)],
                 scratch=[pltpu.VMEM((tm, Q_LORA), BF16), pltpu.VMEM((tm, KV_LORA), BF16), pltpu.VMEM((tm, Q_LORA), F32),
                          pltpu.VMEM((tm, KV_LORA), F32), pltpu.VMEM((tm, LANES), F32)])(
        proj, cos, sin_signed, qa_g, kva_g, q_b, kv_b, q_g, k_g, dq, dk, dv, dz_pool)


def _attn_tile(seq):
    return 512 if seq % 512 == 0 else seq


def _causal_mask(s):
    row = lax.broadcasted_iota(jnp.int32, s.shape, 0)
    col = lax.broadcasted_iota(jnp.int32, s.shape, 1)
    return jnp.where(row >= col, s, NEG_INF)


def _tile(i, t):
    return slice(i * t, (i + 1) * t)


def _flash_fwd(q, k, v, mix, batch, seq):
    t = _attn_tile(seq)
    nq = seq // t

    def body(q_ref, k_ref, v_ref, _, o_ref, lse_ref):
        for qi in range(nq):
            rows, before = _tile(qi, t), slice(0, qi * t)
            qv = q_ref[rows, :]
            s_diag = _causal_mask(_dot(qv, k_ref[rows, :], "nt") * ATTN_SCALE)
            m = jnp.max(s_diag, axis=-1, keepdims=True)
            if qi:
                s_before = _dot(qv, k_ref[before, :], "nt") * ATTN_SCALE
                m = jnp.maximum(m, jnp.max(s_before, axis=-1, keepdims=True))
            p = jnp.exp(s_diag - m)
            l = jnp.sum(p, axis=-1, keepdims=True)
            acc = _dot(p, v_ref[rows, :])
            if qi:
                p = jnp.exp(s_before - m)
                l = l + jnp.sum(p, axis=-1, keepdims=True)
                acc = acc + _dot(p, v_ref[before, :])
            o_ref[rows, :] = (acc / l).astype(BF16)
            lse_ref[rows, :] = jnp.broadcast_to(m + jnp.log(l), (t, LANES))

    T = batch * seq
    whole = lambda w: BS((None, seq, w), lambda b, h: (h, b, 0))
    return _call(body, name="flash_fwd", grid=(batch, HEADS),
                 in_specs=[whole(QK_PAD), whole(QK_PAD), whole(V_DIM), pl.BlockSpec(memory_space=pl.ANY)],
                 out_specs=[BS((seq, V_DIM), lambda b, h: (b, POOL_WIDTH // V_DIM + h)), whole(LANES)],
                 out_shape=[_sds((T, D_MODEL), BF16), _sds((HEADS, T, LANES), F32)],
                 aliases={3: 0})(q, k, v, mix)


def _flash_bwd(q, k, v, dmix, mix, lse, batch, seq):
    t = _attn_tile(seq)
    nq = seq // t

    def body(q_ref, k_ref, v_ref, do_ref, o_ref, lse_ref, dq_ref, dk_ref, dv_ref, delta_s):
        dq_ref[...] = jnp.zeros_like(dq_ref)
        dk_ref[...] = jnp.zeros_like(dk_ref)
        dv_ref[...] = jnp.zeros_like(dv_ref)
        for qi in range(nq):
            rows = _tile(qi, t)
            delta_s[qi] = jnp.sum(do_ref[rows, :].astype(F32) * o_ref[rows, :].astype(F32), axis=-1, keepdims=True)
        for kb in range(nq):
            keys = _tile(kb, t)
            for qi in range(kb, nq):
                rows = _tile(qi, t)
                qv, kk, do = q_ref[rows, :], k_ref[keys, :], do_ref[rows, :]
                s = _dot(qv, kk, "nt") * ATTN_SCALE
                if kb == qi:
                    s = _causal_mask(s)
                p = jnp.exp(s - lse_ref[rows, 0:1])
                dv_ref[keys, :] += _dot(p, do, "tn")
                ds = p * (_dot(do, v_ref[keys, :], "nt") - delta_s[qi]) * ATTN_SCALE
                dq_ref[rows, :] += _dot(ds, kk)
                dk_ref[keys, :] += _dot(ds, qv, "tn")

    T = batch * seq
    whole = lambda w: BS((None, seq, w), lambda b, h: (h, b, 0))
    head_cols = BS((seq, V_DIM), lambda b, h: (b, POOL_WIDTH // V_DIM + h))
    return _call(body, name="flash_bwd", grid=(batch, HEADS),
                 in_specs=[whole(QK_PAD), whole(QK_PAD), whole(V_DIM), head_cols, head_cols, whole(LANES)],
                 out_specs=[whole(QK_PAD), whole(QK_PAD), whole(V_DIM)],
                 out_shape=[_sds((HEADS, T, QK_PAD), F32), _sds((HEADS, T, QK_PAD), F32), _sds((HEADS, T, V_DIM), F32)],
                 scratch=[pltpu.VMEM((nq, t, 1), F32)])(q, k, v, dmix, mix, lse)


def _loss_head(y, target, tm):
    T, d = y.shape

    def body(y_ref, t_ref, dy_ref, sq_ref):
        @pl.when(pl.program_id(0) == 0)
        def _():
            sq_ref[...] = jnp.zeros_like(sq_ref)

        e = y_ref[...] - t_ref[...]
        sq_ref[...] += jnp.sum(e * e)
        dy_ref[...] = e * (1.0 / d)

    row = BS((tm, d), lambda i: (i, 0))
    return _call(body, name="loss_head", grid=(T // tm,), in_specs=[row, row],
                 out_specs=[row, BS((SUBLANES, LANES), lambda i: (0, 0))],
                 out_shape=[_sds((T, d), F32), _sds((SUBLANES, LANES), F32)])(y, target)


def _adamw_math(w, g, m, v):
    m = ADAM_B1 * m + (1.0 - ADAM_B1) * g
    v = ADAM_B2 * v + (1.0 - ADAM_B2) * (g * g)
    m_hat = m / (1.0 - ADAM_B1 ** ADAM_STEP)
    v_hat = v / (1.0 - ADAM_B2 ** ADAM_STEP)
    return -ADAM_LR * (m_hat / (jnp.sqrt(v_hat) + ADAM_EPS) + ADAM_WD * w), m, v


def _adamw(name, w, g, m, v):
    L, R, C = w.shape
    tr = 256 if R % 256 == 0 else R
    outs = None
    for l in range(L):
        def body(w_ref, g_ref, m_ref, v_ref, *rest):
            go_ref, d_ref, mo_ref, vo_ref = rest[-4:]
            gv = g_ref[...]
            d_ref[...], mo_ref[...], vo_ref[...] = _adamw_math(w_ref[...], gv, m_ref[...], v_ref[...])
            go_ref[...] = gv

        layer = BS((None, tr, C), functools.partial(lambda l, i: (l, i, 0), l))
        prev = [] if outs is None else list(outs)
        outs = _call(body, name=f"{name}_{l}", grid=(R // tr,),
                     in_specs=[layer, BS((tr, C), lambda i: (i, 0)), layer, layer] + [pl.BlockSpec(memory_space=pl.ANY)] * len(prev),
                     out_specs=[layer] * 4, out_shape=[_sds((L, R, C), F32)] * 4,
                     aliases={4 + n: n for n in range(len(prev))})(w, g[l], m, v, *prev)
    return outs


def _place():
    x, y, c = lax.axis_index("x"), lax.axis_index("y"), lax.axis_index("c")
    other_chips = [(1 - x, y), (x, 1 - y), (1 - x, 1 - y)]
    return x, y, c, other_chips


def _remote(src, dst, send_sem, recv_sem, dev):
    return pltpu.make_async_remote_copy(src_ref=src, dst_ref=dst, send_sem=send_sem, recv_sem=recv_sem,
                                        device_id=dev, device_id_type=MESH)


def _prefetch_call(body, *, name, grid, in_specs, out_specs, out_shape):
    grid_spec = pltpu.PrefetchScalarGridSpec(num_scalar_prefetch=1, grid=grid, in_specs=in_specs, out_specs=out_specs)
    params = pltpu.CompilerParams(vmem_limit_bytes=VMEM_LIMIT_V7X, dimension_semantics=("arbitrary",) * len(grid))
    return pl.pallas_call(body, name=name, grid_spec=grid_spec, out_shape=out_shape, compiler_params=params)


def _row_tile(rows):
    return 256 if rows % 256 == 0 else rows


def _cast_place(name, w, layer, chip):
    _, _, rows, C = w.shape
    tr = _row_tile(rows)

    def body(chip_ref, w_ref, o_ref):
        o_ref[...] = w_ref[...].astype(BF16)

    return _prefetch_call(body, name=name, grid=(2, rows // tr),
                          in_specs=[BS((None, None, tr, C), lambda h, i, chip_ref: (layer, h, i, 0))],
                          out_specs=BS((None, None, tr, C), lambda h, i, chip_ref: (chip_ref[0], h, i, 0)),
                          out_shape=_sds((N_CHIPS, 2, rows, C), BF16))(chip, w)


SEM = pl.BlockSpec(memory_space=pltpu.SEMAPHORE)


def _split_copy_call(body, *, name, in_specs, out_specs, out_shape, aliases):
    return pl.pallas_call(body, name=name, in_specs=in_specs, out_specs=out_specs, out_shape=out_shape,
                          input_output_aliases=aliases,
                          compiler_params=pltpu.CompilerParams(has_side_effects=pltpu.SideEffectType.DATAFLOW_SIDE_EFFECTING))


def _hbm(arrays):
    return [pltpu.with_memory_space_constraint(a, pltpu.HBM) for a in arrays]


def _gather_send(gs, groups):
    n = len(gs)

    def body(*refs):
        g, sems = refs[:n], refs[n:n + 2 * len(groups)]
        x, y, c, chips = _place()
        me = 2 * x + y
        for gi, members in enumerate(groups):
            for a, i in enumerate(members):
                for k, (px, py) in enumerate(chips):
                    _remote(g[i].at[me, c], g[i].at[me, c], sems[2 * gi].at[3 * a + k], sems[2 * gi + 1].at[3 * a + k],
                            (px, py, c)).start()

    sem_shapes = [pltpu.SemaphoreType.DMA((3 * len(members),)) for members in groups for _ in range(2)]
    out = _split_copy_call(body, name="gather_send", in_specs=[HBM] * n, out_specs=[SEM] * len(sem_shapes) + [HBM] * n,
                           out_shape=sem_shapes + [pltpu.HBM(a.shape, a.dtype) for a in gs],
                           aliases={i: len(sem_shapes) + i for i in range(n)})(*_hbm(gs))
    return out[:len(sem_shapes)], out[len(sem_shapes):]


def _gather_wait(name, gs, send_sems, recv_sems, after):
    n = len(gs)

    def body(*refs):
        g, ssem, rsem = refs[:n], refs[n], refs[n + 1]
        x, y, c, chips = _place()
        me = 2 * x + y
        for a in range(n):
            for k, (px, py) in enumerate(chips):
                landed = g[a].at[2 * px + py, c]
                cp = _remote(g[a].at[me, c], landed, ssem.at[3 * a + k], rsem.at[3 * a + k], (px, py, c))
                cp.wait_recv()
                cp.wait_send()

    return _split_copy_call(body, name=name, in_specs=[HBM] * n + [SEM, SEM] + [pl.BlockSpec(memory_space=pl.ANY)] * len(after),
                            out_specs=[HBM] * n, out_shape=[pltpu.HBM(a.shape, a.dtype) for a in gs],
                            aliases={i: i for i in range(n)})(*gs, send_sems, recv_sems, *after)


def _gather_pass(name, gs):
    n = len(gs)

    def body(*refs):
        g, send_sems, recv_sems = refs[n:2 * n], refs[-2], refs[-1]
        x, y, c, chips = _place()
        sibling = (x, y, 1 - c)
        passed = [_remote(g[i].at[2 * px + py, c], g[i].at[2 * px + py, c], send_sems.at[3 * i + k], recv_sems.at[3 * i + k], sibling)
                  for i in range(n) for k, (px, py) in enumerate(chips)]
        for cp in passed:
            cp.start()
        for i in range(n):
            for k, (px, py) in enumerate(chips):
                theirs = g[i].at[2 * px + py, 1 - c]
                _remote(theirs, theirs, send_sems.at[3 * i + k], recv_sems.at[3 * i + k], sibling).wait_recv()
        for cp in passed:
            cp.wait_send()

    return _call(body, name=name, in_specs=[HBM] * n, out_specs=[HBM] * n, out_shape=[_sds(a.shape, a.dtype) for a in gs],
                 aliases={i: i for i in range(n)},
                 scratch=[pltpu.SemaphoreType.DMA((3 * n,)), pltpu.SemaphoreType.DMA((3 * n,))])(*gs)


def _scatter_send(name, ps):
    n = len(ps)

    def body(*refs):
        p, r, ssem, rsem, token = refs[:n], refs[n:2 * n], refs[2 * n], refs[2 * n + 1], refs[-1]
        x, y, c, chips = _place()
        for i in range(n):
            for k, (px, py) in enumerate(chips):
                _remote(p[i].at[2 * px + py], r[i].at[k], ssem.at[3 * i + k], rsem.at[3 * i + k], (px, py, c)).start()
        token[...] = jnp.zeros_like(token)

    lands = [lax.empty((N_CHIPS - 1,) + a.shape[1:], a.dtype) for a in ps]
    sem = pltpu.SemaphoreType.DMA((3 * n,))
    out = _split_copy_call(body, name=name, in_specs=[HBM] * (2 * n), out_specs=[SEM, SEM] + [HBM] * (2 * n) + [VMEM],
                           out_shape=[sem, sem] + [pltpu.HBM(a.shape, a.dtype) for a in list(ps) + lands] + [_sds((SUBLANES, LANES), F32)],
                           aliases={i: 2 + i for i in range(2 * n)})(*_hbm(list(ps) + lands))
    return out[0], out[1], out[2:2 + n], out[2 + n:2 + 2 * n], out[-1]


def _scatter_wait(name, ps, lands, send_sems, recv_sems, after):
    n = len(ps)

    def body(*refs):
        p, r, ssem, rsem = refs[:n], refs[n:2 * n], refs[2 * n], refs[2 * n + 1]
        x, y, c, chips = _place()
        for i in range(n):
            for k, (px, py) in enumerate(chips):
                cp = _remote(p[i].at[2 * px + py], r[i].at[k], ssem.at[3 * i + k], rsem.at[3 * i + k], (px, py, c))
                cp.wait_recv()
                cp.wait_send()

    out = _split_copy_call(body, name=name, in_specs=[HBM] * (2 * n) + [SEM, SEM] + [pl.BlockSpec(memory_space=pl.ANY)] * len(after),
                           out_specs=[HBM] * (2 * n), out_shape=[pltpu.HBM(a.shape, a.dtype) for a in list(ps) + list(lands)],
                           aliases={i: i for i in range(2 * n)})(*ps, *lands, send_sems, recv_sems, *after)
    return out[:n], out[n:]


def _sibling_exchange(name, gs):
    n = len(gs)

    def body(*refs):
        g, r, send_sems, recv_sems = refs[:n], refs[n:2 * n], refs[-2], refs[-1]
        x, y, c, _ = _place()
        copies = [_remote(g[i].at[:, 1 - c], r[i], send_sems.at[i], recv_sems.at[i], (x, y, 1 - c)) for i in range(n)]
        for cp in copies:
            cp.start()
        for cp in copies:
            cp.wait()

    return _call(body, name=name, in_specs=[HBM] * n, out_specs=[HBM] * n,
                 out_shape=[_sds((a.shape[0],) + a.shape[2:], a.dtype) for a in gs],
                 scratch=[pltpu.SemaphoreType.DMA((n,)), pltpu.SemaphoreType.DMA((n,))])(*gs)


def _sibling_share(fs):
    n = len(fs)

    def body(*refs):
        f, send_sems, recv_sems = refs[n:2 * n], refs[-2], refs[-1]
        x, y, c, _ = _place()
        sends = [_remote(f[i].at[c], f[i].at[c], send_sems.at[i], recv_sems.at[i], (x, y, 1 - c)) for i in range(n)]
        for cp in sends:
            cp.start()
        for i in range(n):
            theirs = f[i].at[1 - c]
            _remote(theirs, theirs, send_sems.at[i], recv_sems.at[i], (x, y, 1 - c)).wait_recv()
        for cp in sends:
            cp.wait_send()

    return _call(body, name="grad_sibling_share", in_specs=[HBM] * n, out_specs=[HBM] * n,
                 out_shape=[_sds(a.shape, a.dtype) for a in fs], aliases={i: i for i in range(n)},
                 scratch=[pltpu.SemaphoreType.DMA((n,)), pltpu.SemaphoreType.DMA((n,))])(*fs)


def _all_reduce_small(name, v):
    n_dev = 8
    flips = [(fx, fy, fc) for fx in (0, 1) for fy in (0, 1) for fc in (0, 1)][1:]

    def body(v_ref, o_ref, buf, send_sems, recv_sems):
        x, y, c, _ = _place()
        peers = [(1 - x if fx else x, 1 - y if fy else y, 1 - c if fc else c) for fx, fy, fc in flips]
        me = 4 * x + 2 * y + c
        buf[me] = v_ref[...]
        sends = [_remote(v_ref, buf.at[me], send_sems.at[k], recv_sems.at[k], peer) for k, peer in enumerate(peers)]
        for cp in sends:
            cp.start()
        for k, (px, py, pc) in enumerate(peers):
            theirs = buf.at[4 * px + 2 * py + pc]
            _remote(v_ref, theirs, send_sems.at[k], recv_sems.at[k], (px, py, pc)).wait_recv()
        for cp in sends:
            cp.wait_send()
        acc = buf[0]
        for d in range(1, n_dev):
            acc = acc + buf[d]
        o_ref[...] = acc

    return _call(body, name=name, in_specs=[VMEM], out_specs=VMEM, out_shape=_sds(v.shape, F32),
                 scratch=[pltpu.VMEM((n_dev,) + v.shape, F32), pltpu.SemaphoreType.DMA((7,)), pltpu.SemaphoreType.DMA((7,))])(v)


def _add_halves(name, g, r, c):
    _, _, rows, C = g.shape
    tr = _row_tile(rows)

    def body(c_ref, g_ref, r_ref, o_ref):
        o_ref[...] = (g_ref[...] + r_ref[...]).astype(BF16)

    spec = BS((None, tr, C), lambda j, i, c_ref: (j, i, 0))
    return _prefetch_call(body, name=name, grid=(N_CHIPS, rows // tr),
                          in_specs=[BS((None, None, tr, C), lambda j, i, c_ref: (j, c_ref[0], i, 0)), spec], out_specs=spec,
                          out_shape=_sds((N_CHIPS, rows, C), BF16))(c, g, r)


def _sum_partials(name, p, r, chip_c):
    _, rows, C = p.shape
    tr = _row_tile(rows)

    def body(s_ref, p_ref, r_ref, o_ref):
        acc = p_ref[...].astype(F32)
        for k in range(N_CHIPS - 1):
            acc = acc + r_ref[k].astype(F32)
        o_ref[...] = acc

    return _prefetch_call(body, name=name, grid=(rows // tr,),
                          in_specs=[BS((None, tr, C), lambda i, s: (s[0], i, 0)), BS((N_CHIPS - 1, tr, C), lambda i, s: (0, i, 0))],
                          out_specs=BS((None, tr, C), lambda i, s: (s[1], i, 0)), out_shape=_sds((2, rows, C), F32))(chip_c, p, r)


_SHARDED = ("even_w_in", "even_w_out", "odd_w_in", "q_b", "kv_b", "odd_w_out", "ffn_w_gate", "ffn_w_up", "ffn_w_down")
_REPLICATED = ("mix_norm", "ffn_norm", "sg_ln_g", "sg_w_s", "sg_b_s", "pool_w", "q_norm", "k_norm")
_SMALL_SHARDED = ("sc_conv_w", "pool_scale", "q_a_norm", "kv_a_norm")
_WEIGHTS = ("mix_norm", "ffn_norm", "even_w_in", "sg_ln_g", "sg_w_s", "sg_b_s", "sc_conv_w", "even_w_out", "odd_w_in", "pool_w",
            "pool_scale", "q_a_norm", "q_b", "kv_a_norm", "kv_b", "q_norm", "k_norm", "odd_w_out", "ffn_w_gate", "ffn_w_up",
            "ffn_w_down")


def _pad_rows(flat, width, align):
    n = flat.shape[0]
    rows = -(-n // (width * align)) * align
    return jnp.pad(flat, (0, rows * width - n)).reshape(rows, width)


_GROUPS = {"even": ("even_w_in", "even_w_out"),
           "ffn0": ("ffn_w_gate0", "ffn_w_up0", "ffn_w_down0"),
           "odd": ("odd_w_in", "q_b", "kv_b", "odd_w_out"),
           "ffn1": ("ffn_w_gate1", "ffn_w_up1", "ffn_w_down1")}


def _place_shards(shards, chip):
    placed = {}
    for n in _SHARDED:
        a = shards[n]
        halves = a.reshape(a.shape[0], 2, a.shape[1] // 2, a.shape[2])
        if a.shape[0] == 1:
            placed[n] = _cast_place(f"place_{n}", halves, 0, chip)
        else:
            for l in range(a.shape[0]):
                placed[f"{n}{l}"] = _cast_place(f"place_{n}{l}", halves, l, chip)
    return placed


def _whole_weights(gathered):
    out = {n: a.reshape(N_CHIPS, -1, a.shape[-1]) for n, a in gathered.items()}
    for n in ("q_b", "kv_b"):
        if n in out:
            out[n] = out[n].transpose(1, 0, 2).reshape(out[n].shape[1], -1)
    for n in ("even_w_out", "odd_w_in", "odd_w_out"):
        if n in out:
            out[n] = out[n].reshape(-1, out[n].shape[-1])
    return out


def _forward_backward(x, positions, target, small, fetch, emit):
    batch, seq, _ = x.shape
    T = batch * seq
    tm = _token_tile(seq)
    x0 = x.reshape(T, D_MODEL)

    inv_freq = ROPE_THETA ** (-jnp.arange(0, QK_ROPE, 2, dtype=F32) / QK_ROPE)
    ang = (positions.astype(F32)[..., None] * inv_freq).reshape(T, QK_ROPE // 2)
    cos, sin = jnp.cos(ang), jnp.sin(ang)
    pad = jnp.zeros((T, LANES - QK_ROPE), F32)
    cos_t = jnp.concatenate([cos, cos, pad], axis=1)
    sin_t = jnp.concatenate([-sin, sin, pad], axis=1)

    tril = jnp.tril(jnp.ones((SG_CHUNK, SG_CHUNK), bool))
    w_tril = jnp.where(tril[None], small["sg_w_s"][0], 0.0).astype(BF16)
    b_lanes = jnp.broadcast_to(small["sg_b_s"][0][:, :, None], (SG_HEADS, SG_CHUNK, SG_DIM))
    conv_w = jnp.pad(small["sc_conv_w"][0], ((0, SUBLANES - CONV_TAPS), (0, 0)))
    ln_g = small["sg_ln_g"]
    pool_diag = jnp.zeros((POOL_WIDTH, POOL_WIDTH), F32)
    for g in range(len(POOL_WINDOWS)):
        pool_diag = pool_diag.at[POOL_DIM * g:POOL_DIM * (g + 1), POOL_DIM * g:POOL_DIM * (g + 1)].set(small["pool_w"][0, g])
    pool_diag = pool_diag.astype(BF16)
    pool_scale = small["pool_scale"]
    q_g = jnp.pad(small["q_norm"], ((0, 0), (0, QK_PAD - QK_DIM)))
    k_g = jnp.pad(small["k_norm"], ((0, 0), (0, QK_PAD - QK_DIM)))
    qa_g, kva_g = small["q_a_norm"], small["kv_a_norm"]
    in_shard = EVEN_IN // N_CHIPS

    def ffn_weights(l, w):
        return small["ffn_norm"][l], w[f"ffn_w_gate{l}"], w[f"ffn_w_up{l}"], w[f"ffn_w_down{l}"]

    W = fetch("even", ())
    w_in_even = W["even_w_in"]
    h0 = _rmsnorm_fwd("mix0_norm", x0, small["mix_norm"][0], tm)
    tb = _big_tile(T)
    proj0 = _matmul("even_in", "nn", [(h0, w_in_even)],
                    [(BS((tb, D_MODEL), lambda i, j, k: (i, 0)), BS((None, D_MODEL, in_shard), lambda i, j, k: (j, 0, 0)))],
                    (T // tb, N_CHIPS, 1), _sds((T, EVEN_IN), F32), BS((tb, in_shard), lambda i, j, k: (i, j)), (tb, in_shard))
    mix0 = _even_mixer_fwd(proj0, ln_g, w_tril, b_lanes, conv_w, seq, tm)
    w_out_even = W["even_w_out"]
    x1 = _mm("even_out", "nn", mix0, w_out_even, F32, tk=1024, add=x0)
    ffn0 = ffn_weights(0, fetch("ffn0", (x1,)))
    x2, ffn0_saved = _ffn_fwd(0, x1, *ffn0, tm)
    W = fetch("odd", (x2,))
    w_in_odd = jnp.pad(W["odd_w_in"], ((0, 0), (0, ODD_IN_PAD - ODD_IN)))
    q_b = jnp.pad(W["q_b"].reshape(Q_LORA, HEADS, QK_DIM).transpose(1, 0, 2), ((0, 0), (0, 0), (0, QK_PAD - QK_DIM)))
    kv_b = W["kv_b"].reshape(KV_LORA, HEADS, QK_NOPE + V_DIM).transpose(1, 0, 2)
    h2 = _rmsnorm_fwd("mix1_norm", x2, small["mix_norm"][1], tm)
    proj1 = _mm("odd_in", "nn", h2, w_in_odd, F32, tk=1024)
    mix1 = _pool_fwd(proj1, pool_diag, pool_scale, seq, tm)
    q, k, v = _mla_qkv_fwd(proj1, cos_t, sin_t, qa_g, kva_g, q_b, kv_b, q_g, k_g, tm)
    mix1, lse = _flash_fwd(q, k, v, mix1, batch, seq)
    x3 = _mm("odd_out", "nn", mix1, W["odd_w_out"], F32, tk=1024, add=x2)
    ffn1 = ffn_weights(1, fetch("ffn1", (x3,)))
    x4, ffn1_saved = _ffn_fwd(1, x3, *ffn1, tm)
    dy, sq = _loss_head(x4, target.reshape(T, D_MODEL), tm)

    G = {}
    dx3, dffn_g1, dwg1, dwu1, dwd1 = _ffn_bwd(1, x3, *ffn1, ffn1_saved, dy, tm)
    behind = emit("ffn1", {"ffn_w_gate1": dwg1, "ffn_w_up1": dwu1, "ffn_w_down1": dwd1}, dx3)
    dmix1 = _mm("odd_out_dx", "nt", dx3, W["odd_w_out"], BF16, tk=1024, after=behind)
    dw_out_odd = _mm("odd_out_dw", "tn", mix1, dx3, F32)
    dq, dk, dv = _flash_bwd(q, k, v, dmix1, mix1, lse, batch, seq)
    dz_pool, dpool_diag, G["pool_scale"] = _pool_bwd(proj1, dmix1, pool_diag, pool_scale, seq, tm)
    dproj1, dq_b, dkv_b, dq_g, dk_g, G["q_a_norm"], G["kv_a_norm"] = _mla_qkv_bwd(
        proj1, cos_t, sin_t, qa_g, kva_g, q_b, kv_b, q_g, k_g, dq, dk, dv, dz_pool, tm)
    G["pool_w"] = jnp.stack([dpool_diag[POOL_DIM * g:POOL_DIM * (g + 1), POOL_DIM * g:POOL_DIM * (g + 1)]
                             for g in range(len(POOL_WINDOWS))])[None]
    G["q_norm"], G["k_norm"] = dq_g[:, :QK_DIM], dk_g[:, :QK_DIM]
    dh2 = _mm("odd_in_dx", "nt", dproj1, W["odd_w_in"], F32, tk=ODD_IN)
    dw_in_odd = _mm("odd_in_dw", "tn", h2, dproj1, F32, tn=ODD_IN)
    dx2, dmix_g1 = _rmsnorm_bwd("mix1_norm_bwd", x2, small["mix_norm"][1], dh2, dx3, tm)

    def shard_major(g, cols):
        return g.reshape(g.shape[0], N_CHIPS, cols).transpose(1, 0, 2)

    behind = emit("odd", {"odd_w_in": dw_in_odd.reshape(N_CHIPS, -1, ODD_IN),
                          "q_b": shard_major(dq_b[:, :, :QK_DIM].transpose(1, 0, 2).reshape(Q_LORA, HEADS * QK_DIM), HEADS * QK_DIM // N_CHIPS),
                          "kv_b": shard_major(dkv_b.transpose(1, 0, 2).reshape(KV_LORA, HEADS * (QK_NOPE + V_DIM)),
                                              HEADS * (QK_NOPE + V_DIM) // N_CHIPS),
                          "odd_w_out": dw_out_odd.reshape(N_CHIPS, -1, D_MODEL)}, dx2)
    dx1, dffn_g0, dwg0, dwu0, dwd0 = _ffn_bwd(0, x1, *ffn0, ffn0_saved, dx2, tm, after=behind)
    behind = emit("ffn0", {"ffn_w_gate0": dwg0, "ffn_w_up0": dwu0, "ffn_w_down0": dwd0}, dx1)
    dmix0 = _mm("even_out_dx", "nt", dx1, w_out_even, F32, tk=1024, after=behind)
    dw_out_even = _mm("even_out_dw", "tn", mix0, dx1, F32)
    dproj0, dw_s, db_lanes, G["sg_ln_g"], dconv = _even_mixer_bwd(proj0, dmix0, ln_g, w_tril, b_lanes, conv_w, seq, tm)
    G["sg_w_s"] = dw_s[None]
    G["sg_b_s"] = jnp.sum(db_lanes, axis=-1)[None]
    G["sc_conv_w"] = dconv[None, :CONV_TAPS]
    dh0 = _matmul("even_in_dx", "nt", [(dproj0, w_in_even)],
                  [(BS((tb, in_shard), lambda i, j, k: (i, k)), BS((None, D_MODEL, in_shard), lambda i, j, k: (k, 0, 0)))],
                  (T // tb, 1, N_CHIPS), _sds((T, D_MODEL), F32), BS((tb, D_MODEL), lambda i, j, k: (i, 0)), (tb, D_MODEL))
    tk = min(512, T)
    dw_in_even = _grad_shards(
        "even_in_dw", h0, dproj0, BS((tk, D_MODEL), lambda k: (k, 0)), BS((tk, EVEN_IN), lambda k: (k, 0)),
        lambda a_ref, b_ref, j: (a_ref[...], b_ref[:, in_shard * j:in_shard * (j + 1)]), (N_CHIPS, D_MODEL, in_shard), T // tk)
    dx0, dmix_g0 = _rmsnorm_bwd("mix0_norm_bwd", x0, small["mix_norm"][0], dh0, dx1, tm)
    emit("even", {"even_w_in": dw_in_even, "even_w_out": dw_out_even.reshape(N_CHIPS, -1, D_MODEL)}, dx0)
    G["mix_norm"] = jnp.concatenate([dmix_g0, dmix_g1], axis=0)
    G["ffn_norm"] = jnp.concatenate([dffn_g0, dffn_g1], axis=0)
    return sq[0, 0], dx0.reshape(batch, seq, D_MODEL), G


def _small_vector(parts, names):
    flat = jnp.concatenate([parts[n].astype(F32).reshape(-1) for n in names])
    return _pad_rows(flat, LANES, SUBLANES)


def _split_small(vec, like, names):
    out, off, flat = {}, 0, vec.reshape(-1)
    for n in names:
        size = math.prod(like[n].shape)
        out[n] = flat[off:off + size].reshape(like[n].shape)
        off += size
    return out


def _whole_shape(a):
    return a.shape[:-1] + (a.shape[-1] * N_CHIPS,)


def kernel(x, positions, mix_norm, ffn_norm, even_w_in, sg_ln_g, sg_w_s, sg_b_s, sc_conv_w, even_w_out, odd_w_in, pool_w, pool_scale, q_a_norm, q_b, kv_a_norm, kv_b, q_norm, k_norm, odd_w_out, ffn_w_gate, ffn_w_up, ffn_w_down, loss_target, m_mix_norm, m_ffn_norm, m_even_w_in, m_sg_ln_g, m_sg_w_s, m_sg_b_s, m_sc_conv_w, m_even_w_out, m_odd_w_in, m_pool_w, m_pool_scale, m_q_a_norm, m_q_b, m_kv_a_norm, m_kv_b, m_q_norm, m_k_norm, m_odd_w_out, m_ffn_w_gate, m_ffn_w_up, m_ffn_w_down, v_mix_norm, v_ffn_norm, v_even_w_in, v_sg_ln_g, v_sg_w_s, v_sg_b_s, v_sc_conv_w, v_even_w_out, v_odd_w_in, v_pool_w, v_pool_scale, v_q_a_norm, v_q_b, v_kv_a_norm, v_kv_b, v_q_norm, v_k_norm, v_odd_w_out, v_ffn_w_gate, v_ffn_w_up, v_ffn_w_down):
    args = dict(locals())
    w = {n: args[n] for n in _WEIGHTS}
    m = {n: args["m_" + n] for n in _WEIGHTS}
    v = {n: args["v_" + n] for n in _WEIGHTS}
    cx, cy, cc = lax.axis_index("x"), lax.axis_index("y"), lax.axis_index("c")
    chip = 2 * cx + cy
    transposed = ("ffn_w_gate", "ffn_w_up")
    for n in transposed:
        w[n], m[n], v[n] = (jnp.swapaxes(t[n], 1, 2) for t in (w, m, v))

    chip_arr = chip.astype(jnp.int32).reshape(1)
    c_arr = cc.astype(jnp.int32).reshape(1)
    group_names = list(_GROUPS)
    order = [n for g in group_names for n in _GROUPS[g]]
    shards = _place_shards(w, chip_arr)
    sems, in_flight = _gather_send([shards[n] for n in order], [[order.index(n) for n in _GROUPS[g]] for g in group_names])
    in_flight = dict(zip(order, in_flight))

    def fetch(group, after):
        gi, members = group_names.index(group), _GROUPS[group]
        landed = _gather_wait(f"gather_wait_{group}", [in_flight[n] for n in members], sems[2 * gi], sems[2 * gi + 1], after)
        return _whole_weights(dict(zip(members, _gather_pass(f"gather_pass_{group}", landed))))

    pending, arrived = [], {}

    def settle(after):
        names, ps, lands, send_sems, recv_sems = pending.pop()
        ps, lands = _scatter_wait(f"scatter_wait_{names[0]}", ps, lands, send_sems, recv_sems, after)
        arrived.update({n: (p, r) for n, p, r in zip(names, ps, lands)})

    def emit(group, grads, done):
        names = _GROUPS[group]
        halves = [grads[n].reshape(N_CHIPS, 2, grads[n].shape[1] // 2, grads[n].shape[2]) for n in names]
        from_sibling = _sibling_exchange(f"grad_exchange_{group}", halves)
        partial = [_add_halves(f"add_{n}", g, r, c_arr) for n, g, r in zip(names, halves, from_sibling)]
        if pending:
            settle((done,))
        send_sems, recv_sems, ps, lands, token = _scatter_send(f"scatter_send_{group}", partial)
        pending.append((names, ps, lands, send_sems, recv_sems))
        return (token,)

    placed = {}
    for n in _SMALL_SHARDED:
        a = w[n]
        whole = jnp.zeros(a.shape[:-1] + (N_CHIPS, a.shape[-1]), F32)
        whole = lax.dynamic_update_slice_in_dim(whole, a[..., None, :], chip, axis=a.ndim - 1)
        placed[n] = jnp.where(cc == 0, whole, 0.0).reshape(_whole_shape(a))
    small = dict({n: w[n] for n in _REPLICATED},
                 **_split_small(_all_reduce_small("gather_small_weights", _small_vector(placed, _SMALL_SHARDED)), placed, _SMALL_SHARDED))

    sq, grad_x, G = _forward_backward(x, positions, loss_target, small, fetch, emit)
    loss = lax.psum(0.5 * sq / D_MODEL, ("x", "y", "c"))

    small_names = _REPLICATED + _SMALL_SHARDED
    summed = _split_small(_all_reduce_small("reduce_small_grads", _small_vector(G, small_names)), G, small_names)
    grads = {n: summed[n] for n in _REPLICATED}
    for n in _SMALL_SHARDED:
        a = w[n]
        grads[n] = lax.dynamic_slice_in_dim(summed[n].reshape(a.shape[:-1] + (N_CHIPS, a.shape[-1])), chip, 1,
                                            axis=a.ndim - 1).reshape(a.shape)

    settle(())
    chip_c = jnp.stack([chip, cc]).astype(jnp.int32)
    sums = [_sum_partials(f"sum_{n}", *arrived[n], chip_c) for n in order]
    shard_grad = {n: f.reshape(1, -1, f.shape[-1]) for n, f in zip(order, _sibling_share(sums))}

    out = {}
    for n in ("even_w_in", "even_w_out", "odd_w_in", "q_b", "kv_b", "odd_w_out"):
        out[n] = _adamw(f"adamw_{n}", w[n], [shard_grad[n][0]], m[n], v[n])
    for n in ("ffn_w_gate", "ffn_w_up", "ffn_w_down"):
        out[n] = _adamw(f"adamw_{n}", w[n], [shard_grad[f"{n}{l}"][0] for l in range(2)], m[n], v[n])
    packed = [_small_vector(d, small_names) for d in (w, grads, m, v)]
    res = _adamw("adamw_small", packed[0][None], [packed[1]], packed[2][None], packed[3][None])
    delta_s, m_s, v_s = (_split_small(r, w, small_names) for r in res[1:])
    for n in small_names:
        out[n] = (grads[n], delta_s[n], m_s[n], v_s[n])
    for n in transposed:
        out[n] = tuple(jnp.swapaxes(t, 1, 2) for t in out[n])

    return (loss, grad_x, *[out[n][0] for n in _WEIGHTS], *[out[n][1] for n in _WEIGHTS],
            *[out[n][2] for n in _WEIGHTS], *[out[n][3] for n in _WEIGHTS])
```

```python
import functools
import math

import jax
import jax.numpy as jnp
from jax import lax
from jax.experimental import pallas as pl
from jax.experimental.pallas import tpu as pltpu

F32, BF16 = jnp.float32, jnp.bfloat16
BS = pl.BlockSpec

D_MODEL = 1024
EPS = 1e-6
NEG_INF = -1e30
SG_HEADS, SG_DIM, SG_WIDTH, SG_CHUNK = 4, 128, 512, 128
SC_WIDTH, CONV_TAPS = 512, 3
EVEN_IN = 2 * SG_WIDTH + 3 * SC_WIDTH
POOL_WINDOWS = (2, 4, 8, 16)
POOL_DIM, POOL_WIDTH = 64, 256
POOL_HALO = 16
HEADS, Q_LORA, KV_LORA, QK_NOPE, QK_ROPE, V_DIM = 6, 384, 256, 128, 64, 128
QK_DIM = QK_NOPE + QK_ROPE
QK_PAD = 256
ODD_IN = POOL_WIDTH + Q_LORA + KV_LORA + QK_ROPE
ODD_IN_PAD = 1024
ROPE_THETA = 10000.0
ATTN_SCALE = QK_DIM ** -0.5
D_FF, N_CHIPS = 2816, 4
FF_SHARD = D_FF // N_CHIPS
ADAM_LR, ADAM_B1, ADAM_B2, ADAM_EPS, ADAM_WD, ADAM_STEP = 0.001, 0.9, 0.999, 1e-08, 0.01, 10
VMEM_LIMIT_V7X = 48 * 2**20
LANES, SUBLANES = 128, 8
MESH = pl.DeviceIdType.MESH
HBM = pl.BlockSpec(memory_space=pltpu.HBM)
VMEM = pl.BlockSpec(memory_space=pltpu.VMEM)

_DIMS = {"nn": (((1,), (0,)), ((), ())), "nt": (((1,), (1,)), ((), ())), "tn": (((0,), (0,)), ((), ()))}


def _dot(a, b, mode="nn"):
    return lax.dot_general(a.astype(BF16), b.astype(BF16), _DIMS[mode], preferred_element_type=F32)


def _call(body, *, name, out_shape, in_specs, out_specs, grid=(), scratch=(), aliases=None, after=()):
    params = pltpu.CompilerParams(vmem_limit_bytes=VMEM_LIMIT_V7X,
                                  **({"dimension_semantics": ("arbitrary",) * len(grid)} if grid else {}))
    n_in, n_after = len(in_specs), len(after)
    kernel_body = body if not after else (lambda *refs: body(*refs[:n_in], *refs[n_in + n_after:]))
    call = pl.pallas_call(kernel_body, name=name, grid=grid, in_specs=list(in_specs) + [pl.BlockSpec(memory_space=pl.ANY)] * n_after,
                          out_specs=out_specs, out_shape=out_shape, scratch_shapes=list(scratch),
                          input_output_aliases=aliases or {}, compiler_params=params)
    return (lambda *ops: call(*ops, *after)) if after else call


def _sds(shape, dtype):
    return jax.ShapeDtypeStruct(tuple(shape), dtype)


def _token_tile(seq):
    return 512 if seq % 512 == 0 else seq


def _matmul(name, mode, pairs, pair_specs, grid, out_shape, out_spec, acc_shape, add=None, add_spec=None, after=()):
    n, nk = len(pairs), grid[-1]

    def body(*refs):
        ab = refs[:2 * n]
        add_ref = refs[2 * n] if add is not None else None

        def finish(r, o_ref):
            if add_ref is not None:
                r = r + add_ref[...]
            o_ref[...] = r.astype(o_ref.dtype)

        if nk == 1:
            r = _dot(ab[0][...], ab[1][...], mode)
            for p in range(1, n):
                r = r + _dot(ab[2 * p][...], ab[2 * p + 1][...], mode)
            finish(r, refs[-1])
            return
        o_ref, acc = refs[-2], refs[-1]
        k = pl.program_id(len(grid) - 1)

        @pl.when(k == 0)
        def _():
            acc[...] = jnp.zeros_like(acc)

        for p in range(n):
            acc[...] += _dot(ab[2 * p][...], ab[2 * p + 1][...], mode)

        @pl.when(k == nk - 1)
        def _():
            finish(acc[...], o_ref)

    ops = [t for pr in pairs for t in pr] + ([add] if add is not None else [])
    specs = [s for pr in pair_specs for s in pr] + ([add_spec] if add is not None else [])
    return _call(body, name=name, grid=grid, in_specs=specs, out_specs=out_spec, out_shape=out_shape,
                 scratch=[pltpu.VMEM(acc_shape, F32)] if nk > 1 else [], after=after)(*ops)


def _grad_shards(name, a, b, a_spec, b_spec, pick, out_shape, n_steps):
    def body(a_ref, b_ref, o_ref):
        @pl.when(pl.program_id(0) == 0)
        def _():
            o_ref[...] = jnp.zeros_like(o_ref)

        for j in range(N_CHIPS):
            aj, bj = pick(a_ref, b_ref, j)
            o_ref[j] += _dot(aj, bj, "tn")

    return _call(body, name=name, grid=(n_steps,), in_specs=[a_spec, b_spec],
                 out_specs=BS(out_shape, lambda k: (0, 0, 0)), out_shape=_sds(out_shape, F32))(a, b)


def _mm(name, mode, a, b, out_dtype, tm=1024, tn=1024, tk=512, add=None, after=()):
    if mode == "tn":
        (K, M), N = a.shape, b.shape[1]
    else:
        (M, K), N = a.shape, (b.shape[1] if mode == "nn" else b.shape[0])
    tm, tn, tk = min(tm, M), min(tn, N), min(tk, K)
    a_spec = BS((tk, tm), lambda i, j, k: (k, i)) if mode == "tn" else BS((tm, tk), lambda i, j, k: (i, k))
    b_spec = BS((tn, tk), lambda i, j, k: (j, k)) if mode == "nt" else BS((tk, tn), lambda i, j, k: (k, j))
    o_spec = BS((tm, tn), lambda i, j, k: (i, j))
    return _matmul(name, mode, [(a, b)], [(a_spec, b_spec)], (M // tm, N // tn, K // tk), _sds((M, N), out_dtype),
                   o_spec, (tm, tn), add=add, add_spec=o_spec if add is not None else None, after=after)


def _rmsnorm_fwd(name, x, g, tm):
    T, d = x.shape

    def body(x_ref, g_ref, o_ref):
        xv = x_ref[...]
        y = xv * lax.rsqrt(jnp.mean(xv * xv, axis=-1, keepdims=True) + EPS)
        o_ref[...] = (y * g_ref[...]).astype(o_ref.dtype)

    return _call(body, name=name, grid=(T // tm,), in_specs=[BS((tm, d), lambda i: (i, 0)), BS((1, d), lambda i: (0, 0))],
                 out_specs=BS((tm, d), lambda i: (i, 0)), out_shape=_sds((T, d), BF16))(x, g.reshape(1, d))


def _rmsnorm_bwd(name, x, g, dh, dres, tm):
    T, d = x.shape

    def body(x_ref, g_ref, dh_ref, dres_ref, dx_ref, dg_ref):
        xv = x_ref[...]
        r = lax.rsqrt(jnp.mean(xv * xv, axis=-1, keepdims=True) + EPS)
        xhat = xv * r
        dhv = dh_ref[...]

        @pl.when(pl.program_id(0) == 0)
        def _():
            dg_ref[...] = jnp.zeros_like(dg_ref)

        dg_ref[...] += jnp.sum(dhv * xhat, axis=0, keepdims=True)
        dxhat = dhv * g_ref[...]
        dx_ref[...] = dres_ref[...] + r * (dxhat - xhat * jnp.mean(dxhat * xhat, axis=-1, keepdims=True))

    row = BS((tm, d), lambda i: (i, 0))
    vec = BS((1, d), lambda i: (0, 0))
    return _call(body, name=name, grid=(T // tm,), in_specs=[row, vec, row, row], out_specs=[row, vec],
                 out_shape=[_sds((T, d), F32), _sds((1, d), F32)])(x, g.reshape(1, d), dh, dres)


def _ffn_up(name, h, wg, wu, tm):
    T = h.shape[0]

    def body(h_ref, wg_ref, wu_ref, g_ref, u_ref, a_ref):
        hv = h_ref[...]
        g = _dot(hv, wg_ref[...], "nt")
        u = _dot(hv, wu_ref[...], "nt")
        g_ref[...] = g.astype(BF16)
        u_ref[...] = u.astype(BF16)
        a_ref[...] = (g * (1.0 / (1.0 + jnp.exp(-g))) * u).astype(BF16)

    w_spec = BS((None, FF_SHARD, D_MODEL), lambda j, i: (j, 0, 0))
    o_spec = BS((None, tm, FF_SHARD), lambda j, i: (j, i, 0))
    sh = _sds((N_CHIPS, T, FF_SHARD), BF16)
    return _call(body, name=name, grid=(N_CHIPS, T // tm), in_specs=[BS((tm, D_MODEL), lambda j, i: (i, 0)), w_spec, w_spec],
                 out_specs=[o_spec, o_spec, o_spec], out_shape=[sh, sh, sh])(h, wg, wu)


def _ffn_act_bwd(name, dxo, wd, g, u, tm, after=()):
    T = dxo.shape[0]

    def body(dx_ref, wd_ref, g_ref, u_ref, dg_ref, du_ref):
        da = _dot(dx_ref[...], wd_ref[...], "nt")
        g = g_ref[...].astype(F32)
        sig = 1.0 / (1.0 + jnp.exp(-g))
        dg_ref[...] = (da * u_ref[...].astype(F32) * (sig * (1.0 + g * (1.0 - sig)))).astype(BF16)
        du_ref[...] = (da * (g * sig)).astype(BF16)

    t_spec = BS((None, tm, FF_SHARD), lambda i, j: (j, i, 0))
    sh = _sds((N_CHIPS, T, FF_SHARD), BF16)
    return _call(body, name=name, grid=(T // tm, N_CHIPS),
                 in_specs=[BS((tm, D_MODEL), lambda i, j: (i, 0)), BS((None, FF_SHARD, D_MODEL), lambda i, j: (j, 0, 0)), t_spec, t_spec],
                 out_specs=[t_spec, t_spec], out_shape=[sh, sh], after=after)(dxo, wd, g, u)


def _big_tile(n):
    return min(1024, n)


def _ffn_fwd(l, x, gain, wg, wu, wd, tm):
    T = x.shape[0]
    h = _rmsnorm_fwd(f"ffn{l}_norm", x, gain, tm)
    tm = _big_tile(T)
    g, u, a = _ffn_up(f"ffn{l}_up", h, wg, wu, tm)
    tn = D_MODEL
    out = _matmul(f"ffn{l}_down", "nn", [(a, wd)],
                  [(BS((None, tm, FF_SHARD), lambda i, j, k: (k, i, 0)), BS((None, FF_SHARD, tn), lambda i, j, k: (k, 0, j)))],
                  (T // tm, D_MODEL // tn, N_CHIPS), _sds((T, D_MODEL), F32), BS((tm, tn), lambda i, j, k: (i, j)), (tm, tn),
                  add=x, add_spec=BS((tm, tn), lambda i, j, k: (i, j)))
    return out, (h, g, u, a)


def _ffn_bwd(l, x, gain, wg, wu, wd, saved, dxo, tm, emit, after=()):
    h, g, u, a = saved
    T = x.shape[0]
    tm_norm, tm = tm, _big_tile(T)
    dg, du = _ffn_act_bwd(f"ffn{l}_act_bwd", dxo, wd, g, u, tm, after=after)
    tk = min(512, T)
    tn = D_MODEL
    shards_spec = BS((N_CHIPS, tk, FF_SHARD), lambda k: (0, k, 0))
    rows_spec = BS((tk, D_MODEL), lambda k: (k, 0))

    def dw(nm, act, rows):
        return _grad_shards(nm, act, rows, shards_spec, rows_spec, lambda a_ref, b_ref, j: (a_ref[j], b_ref[...]),
                            (N_CHIPS, FF_SHARD, D_MODEL), T // tk)

    behind = emit(f"ffn{l}", {f"ffn_w_gate{l}": dw(f"ffn{l}_dwg", dg, h), f"ffn_w_up{l}": dw(f"ffn{l}_dwu", du, h),
                              f"ffn_w_down{l}": dw(f"ffn{l}_dwd", a, dxo)})
    act_spec = BS((None, tm, FF_SHARD), lambda i, j, k: (k, i, 0))
    w_spec = BS((None, FF_SHARD, tn), lambda i, j, k: (k, 0, j))
    dh = _matmul(f"ffn{l}_dh", "nn", [(dg, wg), (du, wu)], [(act_spec, w_spec), (act_spec, w_spec)],
                 (T // tm, D_MODEL // tn, N_CHIPS), _sds((T, D_MODEL), F32), BS((tm, tn), lambda i, j, k: (i, j)), (tm, tn),
                 after=behind)
    return _rmsnorm_bwd(f"ffn{l}_norm_bwd", x, gain, dh, dxo, tm_norm)


_INV_SQRT2 = 1.0 / math.sqrt(2.0)
_INV_SQRT_2PI = 1.0 / math.sqrt(2.0 * math.pi)


def _gelu(x):
    return 0.5 * x * (1.0 + lax.erf(x * _INV_SQRT2))


def _gelu_grad(x):
    return 0.5 * (1.0 + lax.erf(x * _INV_SQRT2)) + x * jnp.exp(-0.5 * x * x) * _INV_SQRT_2PI


def _shift_down(x, k):
    return pltpu.roll(x, k, 0)


def _shift_up(x, k):
    return pltpu.roll(x, x.shape[0] - k, 0)


def _layer_norm_head(xh):
    xc = xh - jnp.mean(xh, axis=-1, keepdims=True)
    rstd = lax.rsqrt(jnp.mean(xc * xc, axis=-1, keepdims=True) + EPS)
    return xc * rstd, rstd


def _even_halo_specs(tm, n_tiles, col_blocks, after):
    rows = tm // SUBLANES
    last = n_tiles * rows - 1
    if after:
        return [BS((SUBLANES, 512), functools.partial(lambda cb, i: (jnp.minimum((i + 1) * rows, last), cb), cb)) for cb in col_blocks]
    return [BS((SUBLANES, 512), functools.partial(lambda cb, i: (jnp.maximum(i * rows - 1, 0), cb), cb)) for cb in col_blocks]


def _even_mixer_fwd(proj, ln_g, w_tril, b_lanes, conv_w, seq, tm):
    T = proj.shape[0]
    tiles_per_seq = seq // tm

    def body(p_ref, hc_ref, hh_ref, lng_ref, w_ref, bb_ref, cw_ref, o_ref):
        first = pl.program_id(0) % tiles_per_seq == 0
        for h in range(SG_HEADS):
            cols = slice(SG_DIM * h, SG_DIM * (h + 1))
            vhat, _ = _layer_norm_head(_gelu(p_ref[:, SG_WIDTH + SG_DIM * h:SG_WIDTH + SG_DIM * (h + 1)]))
            vln = (vhat * lng_ref[:, cols]).astype(BF16)
            for k in range(tm // SG_CHUNK):
                rows = slice(SG_CHUNK * k, SG_CHUNK * (k + 1))
                mixed = _dot(w_ref[h], vln[rows]) + bb_ref[h]
                o_ref[rows, cols] = (_gelu(p_ref[rows, cols]) * mixed).astype(BF16)
        z = p_ref[:, 1536:2048] * p_ref[:, 2048:2560]
        zz = jnp.concatenate([jnp.where(first, 0.0, hc_ref[...] * hh_ref[...]), z], axis=0)
        y = cw_ref[0:1, :] * _shift_down(zz, 2)[SUBLANES:] + cw_ref[1:2, :] * _shift_down(zz, 1)[SUBLANES:] + cw_ref[2:3, :] * z
        o_ref[:, SG_WIDTH:] = (p_ref[:, 1024:1536] * y).astype(BF16)

    full = lambda shape: BS(shape, lambda i: (0,) * len(shape))
    return _call(body, name="even_mixer_fwd", grid=(T // tm,),
                 in_specs=[BS((tm, EVEN_IN), lambda i: (i, 0))] + _even_halo_specs(tm, T // tm, (3, 4), after=False)
                 + [full((1, SG_WIDTH)), full((SG_HEADS, SG_CHUNK, SG_CHUNK)), full((SG_HEADS, SG_CHUNK, SG_DIM)), full((SUBLANES, SC_WIDTH))],
                 out_specs=BS((tm, D_MODEL), lambda i: (i, 0)), out_shape=_sds((T, D_MODEL), BF16))(
        proj, proj, proj, ln_g, w_tril, b_lanes, conv_w)


def _even_mixer_bwd(proj, dmix, ln_g, w_tril, b_lanes, conv_w, seq, tm):
    T = proj.shape[0]
    n_tiles, tiles_per_seq = T // tm, seq // tm

    def body(p_ref, dm_ref, hc_ref, hh_ref, nd_ref, nb_ref, lng_ref, w_ref, bb_ref, cw_ref,
             dp_ref, dw_ref, db_ref, dlng_ref, dcw_ref):
        i = pl.program_id(0)
        first = i % tiles_per_seq == 0
        last = i % tiles_per_seq == tiles_per_seq - 1

        @pl.when(i == 0)
        def _():
            dw_ref[...] = jnp.zeros_like(dw_ref)
            db_ref[...] = jnp.zeros_like(db_ref)
            dlng_ref[...] = jnp.zeros_like(dlng_ref)
            dcw_ref[...] = jnp.zeros_like(dcw_ref)

        for h in range(SG_HEADS):
            cols = slice(SG_DIM * h, SG_DIM * (h + 1))
            vcols = slice(SG_WIDTH + SG_DIM * h, SG_WIDTH + SG_DIM * (h + 1))
            lng = lng_ref[:, cols]
            for k in range(tm // SG_CHUNK):
                rows = slice(SG_CHUNK * k, SG_CHUNK * (k + 1))
                v = p_ref[rows, vcols]
                vhat, rstd = _layer_norm_head(_gelu(v))
                vln = (vhat * lng).astype(BF16)
                mixed = _dot(w_ref[h], vln) + bb_ref[h]
                u = p_ref[rows, cols]
                da = dm_ref[rows, cols]
                dp_ref[rows, cols] = (da * mixed * _gelu_grad(u)).astype(BF16)
                dmixed = da * _gelu(u)
                db_ref[h] += dmixed
                dw_ref[h] += _dot(dmixed, vln, "nt")
                dvln = _dot(w_ref[h], dmixed, "tn")
                dlng_ref[:, cols] += jnp.sum(dvln * vhat, axis=0, keepdims=True)
                dvhat = dvln * lng
                dgv = rstd * (dvhat - jnp.mean(dvhat, axis=-1, keepdims=True)
                              - vhat * jnp.mean(dvhat * vhat, axis=-1, keepdims=True))
                dp_ref[rows, vcols] = (dgv * _gelu_grad(v)).astype(BF16)

        b = p_ref[:, 1024:1536]
        c = p_ref[:, 1536:2048]
        hv = p_ref[:, 2048:2560]
        z = c * hv
        zz = jnp.concatenate([jnp.where(first, 0.0, hc_ref[...] * hh_ref[...]), z], axis=0)
        z1 = _shift_down(zz, 1)[SUBLANES:]
        z2 = _shift_down(zz, 2)[SUBLANES:]
        w0, w1, w2 = cw_ref[0:1, :], cw_ref[1:2, :], cw_ref[2:3, :]
        dbo = dm_ref[:, SG_WIDTH:]
        dy = dbo * b
        dd = jnp.concatenate([dy, jnp.where(last, 0.0, nd_ref[...] * nb_ref[...])], axis=0)
        dz = w2 * dy + w1 * _shift_up(dd, 1)[:tm] + w0 * _shift_up(dd, 2)[:tm]
        dp_ref[:, 1024:1536] = (dbo * (w0 * z2 + w1 * z1 + w2 * z)).astype(BF16)
        dp_ref[:, 1536:2048] = (dz * hv).astype(BF16)
        dp_ref[:, 2048:2560] = (dz * c).astype(BF16)
        dcw_ref[0:1, :] += jnp.sum(dy * z2, axis=0, keepdims=True)
        dcw_ref[1:2, :] += jnp.sum(dy * z1, axis=0, keepdims=True)
        dcw_ref[2:3, :] += jnp.sum(dy * z, axis=0, keepdims=True)

        @pl.when(i == n_tiles - 1)
        def _():
            t_idx = lax.broadcasted_iota(jnp.int32, (SG_CHUNK, SG_CHUNK), 0)
            s_idx = lax.broadcasted_iota(jnp.int32, (SG_CHUNK, SG_CHUNK), 1)
            for h in range(SG_HEADS):
                dw_ref[h] = jnp.where(t_idx >= s_idx, dw_ref[h], 0.0)

    full = lambda shape: BS(shape, lambda i: (0,) * len(shape))
    sq = (SG_HEADS, SG_CHUNK, SG_CHUNK)
    return _call(body, name="even_mixer_bwd", grid=(n_tiles,),
                 in_specs=[BS((tm, EVEN_IN), lambda i: (i, 0)), BS((tm, D_MODEL), lambda i: (i, 0))]
                 + _even_halo_specs(tm, n_tiles, (3, 4), after=False)
                 + _even_halo_specs(tm, n_tiles, (1,), after=True) + _even_halo_specs(tm, n_tiles, (2,), after=True)
                 + [full((1, SG_WIDTH)), full(sq), full(sq), full((SUBLANES, SC_WIDTH))],
                 out_specs=[BS((tm, EVEN_IN), lambda i: (i, 0)), full(sq), full(sq), full((1, SG_WIDTH)), full((SUBLANES, SC_WIDTH))],
                 out_shape=[_sds((T, EVEN_IN), BF16), _sds(sq, F32), _sds(sq, F32), _sds((1, SG_WIDTH), F32), _sds((SUBLANES, SC_WIDTH), F32)])(
        proj, dmix, proj, proj, dmix, proj, ln_g, w_tril, b_lanes, conv_w)


def _pool_select(vals):
    lane = lax.broadcasted_iota(jnp.int32, vals[0].shape, 1)
    out = vals[-1]
    for g in range(len(vals) - 2, -1, -1):
        out = jnp.where(lane < POOL_DIM * (g + 1), vals[g], out)
    return out


def _pool_counts(pos1):
    lane = lax.broadcasted_iota(jnp.int32, (pos1.shape[0], POOL_WIDTH), 1)
    win = _pool_select([jnp.full(lane.shape, float(w), F32) for w in POOL_WINDOWS])
    return jnp.minimum(pos1, win)


def _pool_means(zz, counts):
    s2 = zz + _shift_down(zz, 1)
    s4 = s2 + _shift_down(s2, 2)
    s8 = s4 + _shift_down(s4, 4)
    s16 = s8 + _shift_down(s8, 8)
    return _pool_select([s2, s4, s8, s16])[POOL_HALO:] / counts


def _pool_halo_spec(tm, n_tiles, after):
    rows = tm // POOL_HALO
    if after:
        return BS((POOL_HALO, POOL_WIDTH), lambda i: (jnp.minimum((i + 1) * rows, n_tiles * rows - 1), 0))
    return BS((POOL_HALO, POOL_WIDTH), lambda i: (jnp.maximum(i * rows - 1, 0), 0))


def _pool_fwd(proj, w_diag, scale, seq, tm):
    T = proj.shape[0]
    tiles_per_seq = seq // tm

    def body(z_ref, zh_ref, w_ref, s_ref, o_ref):
        t = pl.program_id(0) % tiles_per_seq
        z = z_ref[...]
        zz = jnp.concatenate([jnp.where(t == 0, 0.0, zh_ref[...]), z], axis=0)
        pos1 = (lax.broadcasted_iota(jnp.int32, (tm, 1), 0) + (t * tm + 1)).astype(F32)
        pooled = _pool_means(zz, _pool_counts(pos1)) - z
        o_ref[...] = (_dot(pooled, w_ref[...]) * s_ref[...]).astype(BF16)

    full = lambda shape: BS(shape, lambda i: (0,) * len(shape))
    return _call(body, name="pool_fwd", grid=(T // tm,),
                 in_specs=[BS((tm, POOL_WIDTH), lambda i: (i, 0)), _pool_halo_spec(tm, T // tm, False),
                           full((POOL_WIDTH, POOL_WIDTH)), full((1, POOL_WIDTH))],
                 out_specs=BS((tm, POOL_WIDTH), lambda i: (i, 0)), out_shape=_sds((T, D_MODEL), BF16))(proj, proj, w_diag, scale)


def _pool_bwd(proj, dmix, w_diag, scale, seq, tm):
    T = proj.shape[0]
    n_tiles, tiles_per_seq = T // tm, seq // tm

    def body(z_ref, zh_ref, do_ref, don_ref, w_ref, s_ref, dz_ref, dw_ref, ds_ref):
        i = pl.program_id(0)
        t = i % tiles_per_seq

        @pl.when(i == 0)
        def _():
            dw_ref[...] = jnp.zeros_like(dw_ref)
            ds_ref[...] = jnp.zeros_like(ds_ref)

        z = z_ref[...]
        zz = jnp.concatenate([jnp.where(t == 0, 0.0, zh_ref[...]), z], axis=0)
        pos1 = (lax.broadcasted_iota(jnp.int32, (tm, 1), 0) + (t * tm + 1)).astype(F32)
        counts = _pool_counts(pos1)
        pooled = _pool_means(zz, counts) - z
        dout = do_ref[...].astype(F32)
        ds_ref[...] += jnp.sum(dout * _dot(pooled, w_ref[...]), axis=0, keepdims=True)
        dlin = dout * s_ref[...]
        dw_ref[...] += _dot(pooled, dlin, "tn")
        dpooled = _dot(dlin, w_ref[...], "nt")
        dpooled_n = _dot(don_ref[...].astype(F32) * s_ref[...], w_ref[...], "nt")
        pos1_n = (lax.broadcasted_iota(jnp.int32, (POOL_HALO, 1), 0) + ((t + 1) * tm + 1)).astype(F32)
        dmean_n = jnp.where(t == tiles_per_seq - 1, 0.0, dpooled_n / _pool_counts(pos1_n))
        dd = jnp.concatenate([dpooled / counts, dmean_n], axis=0)
        r2 = dd + _shift_up(dd, 1)
        r4 = r2 + _shift_up(r2, 2)
        r8 = r4 + _shift_up(r4, 4)
        r16 = r8 + _shift_up(r8, 8)
        dz_ref[...] = (_pool_select([r2, r4, r8, r16])[:tm] - dpooled).astype(BF16)

    full = lambda shape: BS(shape, lambda i: (0,) * len(shape))
    return _call(body, name="pool_bwd", grid=(n_tiles,),
                 in_specs=[BS((tm, POOL_WIDTH), lambda i: (i, 0)), _pool_halo_spec(tm, n_tiles, False),
                           BS((tm, POOL_WIDTH), lambda i: (i, 0)), _pool_halo_spec(tm, n_tiles, True),
                           full((POOL_WIDTH, POOL_WIDTH)), full((1, POOL_WIDTH))],
                 out_specs=[BS((tm, POOL_WIDTH), lambda i: (i, 0)), full((POOL_WIDTH, POOL_WIDTH)), full((1, POOL_WIDTH))],
                 out_shape=[_sds((T, POOL_WIDTH), BF16), _sds((POOL_WIDTH, POOL_WIDTH), F32), _sds((1, POOL_WIDTH), F32)])(
        proj, proj, dmix, dmix, w_diag, scale)


def _rope_partner(r):
    lane = lax.broadcasted_iota(jnp.int32, r.shape, 1)
    return jnp.where(lane < QK_ROPE // 2, pltpu.roll(r, LANES - QK_ROPE // 2, 1), pltpu.roll(r, QK_ROPE // 2, 1))


def _rope(x, cos, sin_signed):
    r = x[:, QK_NOPE:]
    return jnp.concatenate([x[:, :QK_NOPE], r * cos + _rope_partner(r) * sin_signed], axis=1)


def _rope_transposed(dx, cos, sin_signed):
    dr = dx[:, QK_NOPE:]
    return jnp.concatenate([dx[:, :QK_NOPE], dr * cos + _rope_partner(dr * sin_signed)], axis=1)


def _head_norm(x):
    r = lax.rsqrt(jnp.sum(x * x, axis=-1, keepdims=True) * (1.0 / QK_DIM) + EPS)
    return x * r, r


def _head_norm_bwd(dy, xhat, r, gain):
    dxhat = dy * gain
    return r * (dxhat - xhat * (jnp.sum(dxhat * xhat, axis=-1, keepdims=True) * (1.0 / QK_DIM)))


def _latents(p_ref, qag_ref, kvag_ref):
    ql = p_ref[:, POOL_WIDTH:POOL_WIDTH + Q_LORA]
    kvl = p_ref[:, POOL_WIDTH + Q_LORA:POOL_WIDTH + Q_LORA + KV_LORA]
    rq = lax.rsqrt(jnp.mean(ql * ql, axis=-1, keepdims=True) + EPS)
    rkv = lax.rsqrt(jnp.mean(kvl * kvl, axis=-1, keepdims=True) + EPS)
    return ql * rq, rq, kvl * rkv, rkv


def _mla_specs(tm):
    full = lambda shape: BS(shape, lambda i, h: (0,) * len(shape))
    return [BS((tm, ODD_IN_PAD), lambda i, h: (i, 0)), BS((tm, LANES), lambda i, h: (i, 0)), BS((tm, LANES), lambda i, h: (i, 0)),
            full((1, Q_LORA)), full((1, KV_LORA)), BS((None, Q_LORA, QK_PAD), lambda i, h: (h, 0, 0)),
            BS((None, KV_LORA, QK_PAD), lambda i, h: (h, 0, 0)), full((1, QK_PAD)), full((1, QK_PAD))]


def _mla_qkv_fwd(proj, cos, sin_signed, qa_g, kva_g, q_b, kv_b, q_g, k_g, tm):
    T = proj.shape[0]

    def body(p_ref, cos_ref, sin_ref, qag_ref, kvag_ref, qb_ref, kvb_ref, qg_ref, kg_ref, q_ref, k_ref, v_ref, qn_s, kvn_s):
        @pl.when(pl.program_id(1) == 0)
        def _():
            qhat, _, kvhat, _ = _latents(p_ref, qag_ref, kvag_ref)
            qn_s[...] = (qhat * qag_ref[...]).astype(BF16)
            kvn_s[...] = (kvhat * kvag_ref[...]).astype(BF16)

        cos, sin = cos_ref[...], sin_ref[...]
        qhat, _ = _head_norm(_dot(qn_s[...], qb_ref[...]))
        q_ref[...] = _rope(qhat * qg_ref[...], cos, sin).astype(BF16)
        kv = _dot(kvn_s[...], kvb_ref[...])
        khat, _ = _head_norm(jnp.concatenate([kv[:, :QK_NOPE], p_ref[:, ODD_IN_PAD - LANES:]], axis=1))
        k_ref[...] = _rope(khat * kg_ref[...], cos, sin).astype(BF16)
        v_ref[...] = kv[:, QK_NOPE:].astype(BF16)

    qk_spec = BS((None, tm, QK_PAD), lambda i, h: (h, i, 0))
    return _call(body, name="mla_qkv_fwd", grid=(T // tm, HEADS), in_specs=_mla_specs(tm),
                 out_specs=[qk_spec, qk_spec, BS((None, tm, V_DIM), lambda i, h: (h, i, 0))],
                 out_shape=[_sds((HEADS, T, QK_PAD), BF16), _sds((HEADS, T, QK_PAD), BF16), _sds((HEADS, T, V_DIM), BF16)],
                 scratch=[pltpu.VMEM((tm, Q_LORA), BF16), pltpu.VMEM((tm, KV_LORA), BF16)])(
        proj, cos, sin_signed, qa_g, kva_g, q_b, kv_b, q_g, k_g)


def _mla_qkv_bwd(proj, cos, sin_signed, qa_g, kva_g, q_b, kv_b, q_g, k_g, dq, dk, dv, dz_pool, tm):
    T = proj.shape[0]
    n_tiles = T // tm

    def body(p_ref, cos_ref, sin_ref, qag_ref, kvag_ref, qb_ref, kvb_ref, qg_ref, kg_ref, dq_ref, dk_ref, dv_ref, dzp_ref,
             dp_ref, dqb_ref, dkvb_ref, dqg_ref, dkg_ref, dqag_ref, dkvag_ref, qn_s, kvn_s, dqn_s, dkvn_s, dkr_s):
        i, h = pl.program_id(0), pl.program_id(1)

        @pl.when((i == 0) & (h == 0))
        def _():
            for ref in (dqb_ref, dkvb_ref, dqg_ref, dkg_ref, dqag_ref, dkvag_ref):
                ref[...] = jnp.zeros_like(ref)

        @pl.when(h == 0)
        def _():
            qhat, _, kvhat, _ = _latents(p_ref, qag_ref, kvag_ref)
            qn_s[...] = (qhat * qag_ref[...]).astype(BF16)
            kvn_s[...] = (kvhat * kvag_ref[...]).astype(BF16)
            dqn_s[...] = jnp.zeros_like(dqn_s)
            dkvn_s[...] = jnp.zeros_like(dkvn_s)
            dkr_s[...] = jnp.zeros_like(dkr_s)

        cos, sin = cos_ref[...], sin_ref[...]
        qhat, rq = _head_norm(_dot(qn_s[...], qb_ref[...]))
        dqn_head = _rope_transposed(dq_ref[...], cos, sin)
        dqg_ref[...] += jnp.sum(dqn_head * qhat, axis=0, keepdims=True)
        dqh = _head_norm_bwd(dqn_head, qhat, rq, qg_ref[...])
        dqb_ref[h] += _dot(qn_s[...], dqh, "tn")
        dqn_s[...] += _dot(dqh, qb_ref[...], "nt")

        kv = _dot(kvn_s[...], kvb_ref[...])
        khat, rk = _head_norm(jnp.concatenate([kv[:, :QK_NOPE], p_ref[:, ODD_IN_PAD - LANES:]], axis=1))
        dkn_head = _rope_transposed(dk_ref[...], cos, sin)
        dkg_ref[...] += jnp.sum(dkn_head * khat, axis=0, keepdims=True)
        dkf = _head_norm_bwd(dkn_head, khat, rk, kg_ref[...])
        dkr_s[...] += dkf[:, QK_NOPE:]
        dkv = jnp.concatenate([dkf[:, :QK_NOPE], dv_ref[...]], axis=1)
        dkvb_ref[h] += _dot(kvn_s[...], dkv, "tn")
        dkvn_s[...] += _dot(dkv, kvb_ref[...], "nt")

        @pl.when(h == HEADS - 1)
        def _():
            qhat_l, rql, kvhat_l, rkvl = _latents(p_ref, qag_ref, kvag_ref)
            dqn, dkvn = dqn_s[...], dkvn_s[...]
            dqag_ref[...] += jnp.sum(dqn * qhat_l, axis=0, keepdims=True)
            dkvag_ref[...] += jnp.sum(dkvn * kvhat_l, axis=0, keepdims=True)
            dqx, dkvx = dqn * qag_ref[...], dkvn * kvag_ref[...]
            dp_ref[:, :POOL_WIDTH] = dzp_ref[...]
            dp_ref[:, POOL_WIDTH:POOL_WIDTH + Q_LORA] = (
                rql * (dqx - qhat_l * jnp.mean(dqx * qhat_l, axis=-1, keepdims=True))).astype(BF16)
            dp_ref[:, POOL_WIDTH + Q_LORA:ODD_IN_PAD - LANES] = (
                rkvl * (dkvx - kvhat_l * jnp.mean(dkvx * kvhat_l, axis=-1, keepdims=True))).astype(BF16)
            dp_ref[:, ODD_IN_PAD - LANES:] = dkr_s[:, :QK_ROPE].astype(BF16)

    full = lambda shape: BS(shape, lambda i, h: (0,) * len(shape))
    qk_spec = BS((None, tm, QK_PAD), lambda i, h: (h, i, 0))
    return _call(body, name="mla_qkv_bwd", grid=(n_tiles, HEADS),
                 in_specs=_mla_specs(tm) + [qk_spec, qk_spec, BS((None, tm, V_DIM), lambda i, h: (h, i, 0)),
                                            BS((tm, POOL_WIDTH), lambda i, h: (i, 0))],
                 out_specs=[BS((tm, ODD_IN), lambda i, h: (i, 0)), full((HEADS, Q_LORA, QK_PAD)), full((HEADS, KV_LORA, QK_PAD)),
                            full((1, QK_PAD)), full((1, QK_PAD)), full((1, Q_LORA)), full((1, KV_LORA))],
                 out_shape=[_sds((T, ODD_IN), BF16),_sds((HEADS, Q_LORA, QK_PAD), F32), _sds((HEADS, KV_LORA, QK_PAD), F32),
                            _sds((1, QK_PAD), F32), _sds((1, QK_PAD), F32), _sds((1, Q_LORA), F32), _sds((1, KV_LORA), F32)],
                 scratch=[pltpu.VMEM((tm, Q_LORA), BF16), pltpu.VMEM((tm, KV_LORA), BF16), pltpu.VMEM((tm, Q_LORA), F32),
                          pltpu.VMEM((tm, KV_LORA), F32), pltpu.VMEM((tm, LANES), F32)])(
        proj, cos, sin_signed, qa_g, kva_g, q_b, kv_b, q_g, k_g, dq, dk, dv, dz_pool)


def _attn_tile(seq):
    return 512 if seq % 512 == 0 else seq


def _causal_mask(s):
    row = lax.broadcasted_iota(jnp.int32, s.shape, 0)
    col = lax.broadcasted_iota(jnp.int32, s.shape, 1)
    return jnp.where(row >= col, s, NEG_INF)


def _tile(i, t):
    return slice(i * t, (i + 1) * t)


def _flash_fwd(q, k, v, mix, batch, seq):
    t = _attn_tile(seq)
    nq = seq // t

    def body(q_ref, k_ref, v_ref, _, o_ref, lse_ref):
        for qi in range(nq):
            rows, before = _tile(qi, t), slice(0, qi * t)
            qv = q_ref[rows, :]
            s_diag = _causal_mask(_dot(qv, k_ref[rows, :], "nt") * ATTN_SCALE)
            m = jnp.max(s_diag, axis=-1, keepdims=True)
            if qi:
                s_before = _dot(qv, k_ref[before, :], "nt") * ATTN_SCALE
                m = jnp.maximum(m, jnp.max(s_before, axis=-1, keepdims=True))
            p = jnp.exp(s_diag - m)
            l = jnp.sum(p, axis=-1, keepdims=True)
            acc = _dot(p, v_ref[rows, :])
            if qi:
                p = jnp.exp(s_before - m)
                l = l + jnp.sum(p, axis=-1, keepdims=True)
                acc = acc + _dot(p, v_ref[before, :])
            o_ref[rows, :] = (acc / l).astype(BF16)
            lse_ref[rows, :] = jnp.broadcast_to(m + jnp.log(l), (t, LANES))

    T = batch * seq
    whole = lambda w: BS((None, seq, w), lambda b, h: (h, b, 0))
    return _call(body, name="flash_fwd", grid=(batch, HEADS),
                 in_specs=[whole(QK_PAD), whole(QK_PAD), whole(V_DIM), pl.BlockSpec(memory_space=pl.ANY)],
                 out_specs=[BS((seq, V_DIM), lambda b, h: (b, POOL_WIDTH // V_DIM + h)), whole(LANES)],
                 out_shape=[_sds((T, D_MODEL), BF16), _sds((HEADS, T, LANES), F32)],
                 aliases={3: 0})(q, k, v, mix)


def _flash_bwd(q, k, v, dmix, mix, lse, batch, seq):
    t = _attn_tile(seq)
    nq = seq // t

    def body(q_ref, k_ref, v_ref, do_ref, o_ref, lse_ref, dq_ref, dk_ref, dv_ref, delta_s):
        dq_ref[...] = jnp.zeros_like(dq_ref)
        dk_ref[...] = jnp.zeros_like(dk_ref)
        dv_ref[...] = jnp.zeros_like(dv_ref)
        for qi in range(nq):
            rows = _tile(qi, t)
            delta_s[qi] = jnp.sum(do_ref[rows, :].astype(F32) * o_ref[rows, :].astype(F32), axis=-1, keepdims=True)
        for kb in range(nq):
            keys = _tile(kb, t)
            for qi in range(kb, nq):
                rows = _tile(qi, t)
                qv, kk, do = q_ref[rows, :], k_ref[keys, :], do_ref[rows, :]
                s = _dot(qv, kk, "nt") * ATTN_SCALE
                if kb == qi:
                    s = _causal_mask(s)
                p = jnp.exp(s - lse_ref[rows, 0:1])
                dv_ref[keys, :] += _dot(p, do, "tn")
                ds = p * (_dot(do, v_ref[keys, :], "nt") - delta_s[qi]) * ATTN_SCALE
                dq_ref[rows, :] += _dot(ds, kk)
                dk_ref[keys, :] += _dot(ds, qv, "tn")

    T = batch * seq
    whole = lambda w: BS((None, seq, w), lambda b, h: (h, b, 0))
    head_cols = BS((seq, V_DIM), lambda b, h: (b, POOL_WIDTH // V_DIM + h))
    return _call(body, name="flash_bwd", grid=(batch, HEADS),
                 in_specs=[whole(QK_PAD), whole(QK_PAD), whole(V_DIM), head_cols, head_cols, whole(LANES)],
                 out_specs=[whole(QK_PAD), whole(QK_PAD), whole(V_DIM)],
                 out_shape=[_sds((HEADS, T, QK_PAD), F32), _sds((HEADS, T, QK_PAD), F32), _sds((HEADS, T, V_DIM), F32)],
                 scratch=[pltpu.VMEM((nq, t, 1), F32)])(q, k, v, dmix, mix, lse)


def _loss_head(y, target, tm):
    T, d = y.shape

    def body(y_ref, t_ref, dy_ref, sq_ref):
        @pl.when(pl.program_id(0) == 0)
        def _():
            sq_ref[...] = jnp.zeros_like(sq_ref)

        e = y_ref[...] - t_ref[...]
        sq_ref[...] += jnp.sum(e * e)
        dy_ref[...] = e * (1.0 / d)

    row = BS((tm, d), lambda i: (i, 0))
    return _call(body, name="loss_head", grid=(T // tm,), in_specs=[row, row],
                 out_specs=[row, BS((SUBLANES, LANES), lambda i: (0, 0))],
                 out_shape=[_sds((T, d), F32), _sds((SUBLANES, LANES), F32)])(y, target)


def _adamw_math(w, g, m, v):
    m = ADAM_B1 * m + (1.0 - ADAM_B1) * g
    v = ADAM_B2 * v + (1.0 - ADAM_B2) * (g * g)
    m_hat = m / (1.0 - ADAM_B1 ** ADAM_STEP)
    v_hat = v / (1.0 - ADAM_B2 ** ADAM_STEP)
    return -ADAM_LR * (m_hat / (jnp.sqrt(v_hat) + ADAM_EPS) + ADAM_WD * w), m, v


def _adamw(name, w, g, m, v):
    L, R, C = w.shape
    tr = 256 if R % 256 == 0 else R
    outs = None
    for l in range(L):
        def body(w_ref, g_ref, m_ref, v_ref, *rest):
            go_ref, d_ref, mo_ref, vo_ref = rest[-4:]
            gv = g_ref[...]
            d_ref[...], mo_ref[...], vo_ref[...] = _adamw_math(w_ref[...], gv, m_ref[...], v_ref[...])
            go_ref[...] = gv

        layer = BS((None, tr, C), functools.partial(lambda l, i: (l, i, 0), l))
        prev = [] if outs is None else list(outs)
        outs = _call(body, name=f"{name}_{l}", grid=(R // tr,),
                     in_specs=[layer, BS((tr, C), lambda i: (i, 0)), layer, layer] + [pl.BlockSpec(memory_space=pl.ANY)] * len(prev),
                     out_specs=[layer] * 4, out_shape=[_sds((L, R, C), F32)] * 4,
                     aliases={4 + n: n for n in range(len(prev))})(w, g[l], m, v, *prev)
    return outs


def _place():
    x, y, c = lax.axis_index("x"), lax.axis_index("y"), lax.axis_index("c")
    other_chips = [(1 - x, y), (x, 1 - y), (1 - x, 1 - y)]
    return x, y, c, other_chips


def _remote(src, dst, send_sem, recv_sem, dev):
    return pltpu.make_async_remote_copy(src_ref=src, dst_ref=dst, send_sem=send_sem, recv_sem=recv_sem,
                                        device_id=dev, device_id_type=MESH)


def _prefetch_call(body, *, name, grid, in_specs, out_specs, out_shape):
    grid_spec = pltpu.PrefetchScalarGridSpec(num_scalar_prefetch=1, grid=grid, in_specs=in_specs, out_specs=out_specs)
    params = pltpu.CompilerParams(vmem_limit_bytes=VMEM_LIMIT_V7X, dimension_semantics=("arbitrary",) * len(grid))
    return pl.pallas_call(body, name=name, grid_spec=grid_spec, out_shape=out_shape, compiler_params=params)


def _row_tile(rows):
    return 256 if rows % 256 == 0 else rows


def _cast_place(name, w, layer, chip):
    _, _, rows, C = w.shape
    tr = _row_tile(rows)

    def body(chip_ref, w_ref, o_ref):
        o_ref[...] = w_ref[...].astype(BF16)

    return _prefetch_call(body, name=name, grid=(2, rows // tr),
                          in_specs=[BS((None, None, tr, C), lambda h, i, chip_ref: (layer, h, i, 0))],
                          out_specs=BS((None, None, tr, C), lambda h, i, chip_ref: (chip_ref[0], h, i, 0)),
                          out_shape=_sds((N_CHIPS, 2, rows, C), BF16))(chip, w)


SEM = pl.BlockSpec(memory_space=pltpu.SEMAPHORE)


def _split_copy_call(body, *, name, in_specs, out_specs, out_shape, aliases):
    return pl.pallas_call(body, name=name, in_specs=in_specs, out_specs=out_specs, out_shape=out_shape,
                          input_output_aliases=aliases,
                          compiler_params=pltpu.CompilerParams(has_side_effects=pltpu.SideEffectType.DATAFLOW_SIDE_EFFECTING))


def _hbm(arrays):
    return [pltpu.with_memory_space_constraint(a, pltpu.HBM) for a in arrays]


def _gather_send(gs, groups, after):
    n = len(gs)

    def body(*refs):
        g, sems = refs[:n], refs[n + len(after):n + len(after) + 2 * len(groups)]
        x, y, c, chips = _place()
        me = 2 * x + y
        for gi, members in enumerate(groups):
            for a, i in enumerate(members):
                for k, (px, py) in enumerate(chips):
                    _remote(g[i].at[me, c], g[i].at[me, c], sems[2 * gi].at[3 * a + k], sems[2 * gi + 1].at[3 * a + k],
                            (px, py, c)).start()

    sem_shapes = [pltpu.SemaphoreType.DMA((3 * len(members),)) for members in groups for _ in range(2)]
    out = _split_copy_call(body, name="gather_send", in_specs=[HBM] * n + [pl.BlockSpec(memory_space=pl.ANY)] * len(after),
                           out_specs=[SEM] * len(sem_shapes) + [HBM] * n,
                           out_shape=sem_shapes + [pltpu.HBM(a.shape, a.dtype) for a in gs],
                           aliases={i: len(sem_shapes) + i for i in range(n)})(*_hbm(gs), *after)
    return out[:len(sem_shapes)], out[len(sem_shapes):]


def _gather_wait(name, gs, send_sems, recv_sems, after):
    n = len(gs)

    def body(*refs):
        g, ssem, rsem = refs[:n], refs[n], refs[n + 1]
        x, y, c, chips = _place()
        me = 2 * x + y
        for a in range(n):
            for k, (px, py) in enumerate(chips):
                landed = g[a].at[2 * px + py, c]
                cp = _remote(g[a].at[me, c], landed, ssem.at[3 * a + k], rsem.at[3 * a + k], (px, py, c))
                cp.wait_recv()
                cp.wait_send()

    return _split_copy_call(body, name=name, in_specs=[HBM] * n + [SEM, SEM] + [pl.BlockSpec(memory_space=pl.ANY)] * len(after),
                            out_specs=[HBM] * n, out_shape=[pltpu.HBM(a.shape, a.dtype) for a in gs],
                            aliases={i: i for i in range(n)})(*gs, send_sems, recv_sems, *after)


def _gather_pass(name, gs):
    n = len(gs)

    def body(*refs):
        g, send_sems, recv_sems = refs[n:2 * n], refs[-2], refs[-1]
        x, y, c, chips = _place()
        sibling = (x, y, 1 - c)
        passed = [_remote(g[i].at[2 * px + py, c], g[i].at[2 * px + py, c], send_sems.at[3 * i + k], recv_sems.at[3 * i + k], sibling)
                  for i in range(n) for k, (px, py) in enumerate(chips)]
        for cp in passed:
            cp.start()
        for i in range(n):
            for k, (px, py) in enumerate(chips):
                theirs = g[i].at[2 * px + py, 1 - c]
                _remote(theirs, theirs, send_sems.at[3 * i + k], recv_sems.at[3 * i + k], sibling).wait_recv()
        for cp in passed:
            cp.wait_send()

    return _call(body, name=name, in_specs=[HBM] * n, out_specs=[HBM] * n, out_shape=[_sds(a.shape, a.dtype) for a in gs],
                 aliases={i: i for i in range(n)},
                 scratch=[pltpu.SemaphoreType.DMA((3 * n,)), pltpu.SemaphoreType.DMA((3 * n,))])(*gs)


def _scatter_send(name, ps):
    n = len(ps)

    def body(*refs):
        p, r, ssem, rsem, token = refs[:n], refs[n:2 * n], refs[2 * n], refs[2 * n + 1], refs[-1]
        x, y, c, chips = _place()
        for i in range(n):
            for k, (px, py) in enumerate(chips):
                _remote(p[i].at[2 * px + py], r[i].at[k], ssem.at[3 * i + k], rsem.at[3 * i + k], (px, py, c)).start()
        token[...] = jnp.zeros_like(token)

    lands = [lax.empty((N_CHIPS - 1,) + a.shape[1:], a.dtype) for a in ps]
    sem = pltpu.SemaphoreType.DMA((3 * n,))
    out = _split_copy_call(body, name=name, in_specs=[HBM] * (2 * n), out_specs=[SEM, SEM] + [HBM] * (2 * n) + [VMEM],
                           out_shape=[sem, sem] + [pltpu.HBM(a.shape, a.dtype) for a in list(ps) + lands] + [_sds((SUBLANES, LANES), F32)],
                           aliases={i: 2 + i for i in range(2 * n)})(*_hbm(list(ps) + lands))
    return out[0], out[1], out[2:2 + n], out[2 + n:2 + 2 * n], out[-1]


def _scatter_wait(name, ps, lands, send_sems, recv_sems, after):
    n = len(ps)

    def body(*refs):
        p, r, ssem, rsem = refs[:n], refs[n:2 * n], refs[2 * n], refs[2 * n + 1]
        x, y, c, chips = _place()
        for i in range(n):
            for k, (px, py) in enumerate(chips):
                cp = _remote(p[i].at[2 * px + py], r[i].at[k], ssem.at[3 * i + k], rsem.at[3 * i + k], (px, py, c))
                cp.wait_recv()
                cp.wait_send()

    out = _split_copy_call(body, name=name, in_specs=[HBM] * (2 * n) + [SEM, SEM] + [pl.BlockSpec(memory_space=pl.ANY)] * len(after),
                           out_specs=[HBM] * (2 * n), out_shape=[pltpu.HBM(a.shape, a.dtype) for a in list(ps) + list(lands)],
                           aliases={i: i for i in range(2 * n)})(*ps, *lands, send_sems, recv_sems, *after)
    return out[:n], out[n:]


def _exchange_send(name, gs):
    n = len(gs)

    def body(*refs):
        g, r, ssem, rsem, token = refs[:n], refs[n:2 * n], refs[2 * n], refs[2 * n + 1], refs[-1]
        x, y, c, _ = _place()
        for i in range(n):
            _remote(g[i].at[:, 1 - c], r[i], ssem.at[i], rsem.at[i], (x, y, 1 - c)).start()
        token[...] = jnp.zeros_like(token)

    lands = [lax.empty((a.shape[0],) + a.shape[2:], a.dtype) for a in gs]
    sem = pltpu.SemaphoreType.DMA((n,))
    out = _split_copy_call(body, name=name, in_specs=[HBM] * (2 * n), out_specs=[SEM, SEM] + [HBM] * (2 * n) + [VMEM],
                           out_shape=[sem, sem] + [pltpu.HBM(a.shape, a.dtype) for a in list(gs) + lands] + [_sds((SUBLANES, LANES), F32)],
                           aliases={i: 2 + i for i in range(2 * n)})(*_hbm(list(gs) + lands))
    return out[0], out[1], out[2:2 + n], out[2 + n:2 + 2 * n], out[-1]


def _exchange_wait(name, gs, lands, send_sems, recv_sems, after):
    n = len(gs)

    def body(*refs):
        g, r, ssem, rsem = refs[:n], refs[n:2 * n], refs[2 * n], refs[2 * n + 1]
        x, y, c, _ = _place()
        for i in range(n):
            cp = _remote(g[i].at[:, 1 - c], r[i], ssem.at[i], rsem.at[i], (x, y, 1 - c))
            cp.wait_recv()
            cp.wait_send()

    out = _split_copy_call(body, name=name, in_specs=[HBM] * (2 * n) + [SEM, SEM] + [pl.BlockSpec(memory_space=pl.ANY)] * len(after),
                           out_specs=[HBM] * (2 * n), out_shape=[pltpu.HBM(a.shape, a.dtype) for a in list(gs) + list(lands)],
                           aliases={i: i for i in range(2 * n)})(*gs, *lands, send_sems, recv_sems, *after)
    return out[:n], out[n:]


def _sibling_share(fs):
    n = len(fs)

    def body(*refs):
        f, send_sems, recv_sems = refs[n:2 * n], refs[-2], refs[-1]
        x, y, c, _ = _place()
        sends = [_remote(f[i].at[c], f[i].at[c], send_sems.at[i], recv_sems.at[i], (x, y, 1 - c)) for i in range(n)]
        for cp in sends:
            cp.start()
        for i in range(n):
            theirs = f[i].at[1 - c]
            _remote(theirs, theirs, send_sems.at[i], recv_sems.at[i], (x, y, 1 - c)).wait_recv()
        for cp in sends:
            cp.wait_send()

    return _call(body, name="grad_sibling_share", in_specs=[HBM] * n, out_specs=[HBM] * n,
                 out_shape=[_sds(a.shape, a.dtype) for a in fs], aliases={i: i for i in range(n)},
                 scratch=[pltpu.SemaphoreType.DMA((n,)), pltpu.SemaphoreType.DMA((n,))])(*fs)


def _all_reduce_small(name, v):
    rows = v.shape[0] // 2
    halves = (2, rows, LANES)

    def body(v_ref, o_ref, from_sibling, chip_sums, send_sems, recv_sems):
        x, y, c, chips = _place()
        me, sibling = 2 * x + y, (x, y, 1 - c)
        swap = _remote(v_ref.at[1 - c], from_sibling, send_sems.at[0], recv_sems.at[0], sibling)
        swap.start()
        swap.wait()
        chip_sums[me] = v_ref[c] + from_sibling[...]
        sends = [_remote(chip_sums.at[me], chip_sums.at[me], send_sems.at[1 + k], recv_sems.at[1 + k], (px, py, c))
                 for k, (px, py) in enumerate(chips)]
        for cp in sends:
            cp.start()
        for k, (px, py) in enumerate(chips):
            theirs = chip_sums.at[2 * px + py]
            _remote(theirs, theirs, send_sems.at[1 + k], recv_sems.at[1 + k], (px, py, c)).wait_recv()
        for cp in sends:
            cp.wait_send()
        acc = chip_sums[0]
        for j in range(1, N_CHIPS):
            acc = acc + chip_sums[j]
        o_ref[c] = acc
        share = _remote(o_ref.at[c], o_ref.at[c], send_sems.at[4], recv_sems.at[4], sibling)
        share.start()
        share.wait_send()
        _remote(o_ref.at[1 - c], o_ref.at[1 - c], send_sems.at[4], recv_sems.at[4], sibling).wait_recv()

    return _call(body, name=name, in_specs=[VMEM], out_specs=VMEM, out_shape=_sds(halves, F32),
                 scratch=[pltpu.VMEM((rows, LANES), F32), pltpu.VMEM((N_CHIPS, rows, LANES), F32),
                          pltpu.SemaphoreType.DMA((5,)), pltpu.SemaphoreType.DMA((5,))])(v.reshape(halves)).reshape(v.shape)


def _add_halves(name, g, r, c):
    _, _, rows, C = g.shape
    tr = _row_tile(rows)

    def body(c_ref, g_ref, r_ref, o_ref):
        o_ref[...] = (g_ref[...] + r_ref[...]).astype(BF16)

    spec = BS((None, tr, C), lambda j, i, c_ref: (j, i, 0))
    return _prefetch_call(body, name=name, grid=(N_CHIPS, rows // tr),
                          in_specs=[BS((None, None, tr, C), lambda j, i, c_ref: (j, c_ref[0], i, 0)), spec], out_specs=spec,
                          out_shape=_sds((N_CHIPS, rows, C), BF16))(c, g, r)


def _sum_partials(name, p, r, chip_c):
    _, rows, C = p.shape
    tr = _row_tile(rows)

    def body(s_ref, p_ref, r_ref, o_ref):
        acc = p_ref[...].astype(F32)
        for k in range(N_CHIPS - 1):
            acc = acc + r_ref[k].astype(F32)
        o_ref[...] = acc

    return _prefetch_call(body, name=name, grid=(rows // tr,),
                          in_specs=[BS((None, tr, C), lambda i, s: (s[0], i, 0)), BS((N_CHIPS - 1, tr, C), lambda i, s: (0, i, 0))],
                          out_specs=BS((None, tr, C), lambda i, s: (s[1], i, 0)), out_shape=_sds((2, rows, C), F32))(chip_c, p, r)


_SHARDED = ("even_w_in", "even_w_out", "odd_w_in", "q_b", "kv_b", "odd_w_out", "ffn_w_gate", "ffn_w_up", "ffn_w_down")
_REPLICATED = ("mix_norm", "ffn_norm", "sg_ln_g", "sg_w_s", "sg_b_s", "pool_w", "q_norm", "k_norm")
_SMALL_SHARDED = ("sc_conv_w", "pool_scale", "q_a_norm", "kv_a_norm")
_WEIGHTS = ("mix_norm", "ffn_norm", "even_w_in", "sg_ln_g", "sg_w_s", "sg_b_s", "sc_conv_w", "even_w_out", "odd_w_in", "pool_w",
            "pool_scale", "q_a_norm", "q_b", "kv_a_norm", "kv_b", "q_norm", "k_norm", "odd_w_out", "ffn_w_gate", "ffn_w_up",
            "ffn_w_down")


def _pad_rows(flat, width, align):
    n = flat.shape[0]
    rows = -(-n // (width * align)) * align
    return jnp.pad(flat, (0, rows * width - n)).reshape(rows, width)


_GROUPS = {"even": ("even_w_in", "even_w_out"),
           "ffn0": ("ffn_w_gate0", "ffn_w_up0", "ffn_w_down0"),
           "odd": ("odd_w_in", "q_b", "kv_b", "odd_w_out"),
           "ffn1": ("ffn_w_gate1", "ffn_w_up1", "ffn_w_down1")}


def _place_shards(shards, chip):
    placed = {}
    for n in _SHARDED:
        a = shards[n]
        halves = a.reshape(a.shape[0], 2, a.shape[1] // 2, a.shape[2])
        if a.shape[0] == 1:
            placed[n] = _cast_place(f"place_{n}", halves, 0, chip)
        else:
            for l in range(a.shape[0]):
                placed[f"{n}{l}"] = _cast_place(f"place_{n}{l}", halves, l, chip)
    return placed


def _whole_weights(gathered):
    out = {n: a.reshape(N_CHIPS, -1, a.shape[-1]) for n, a in gathered.items()}
    for n in ("q_b", "kv_b"):
        if n in out:
            out[n] = out[n].transpose(1, 0, 2).reshape(out[n].shape[1], -1)
    for n in ("even_w_out", "odd_w_in", "odd_w_out"):
        if n in out:
            out[n] = out[n].reshape(-1, out[n].shape[-1])
    return out


def _forward_backward(x, positions, target, small, fetch, emit, advance):
    batch, seq, _ = x.shape
    T = batch * seq
    tm = _token_tile(seq)
    x0 = x.reshape(T, D_MODEL)

    inv_freq = ROPE_THETA ** (-jnp.arange(0, QK_ROPE, 2, dtype=F32) / QK_ROPE)
    ang = (positions.astype(F32)[..., None] * inv_freq).reshape(T, QK_ROPE // 2)
    cos, sin = jnp.cos(ang), jnp.sin(ang)
    pad = jnp.zeros((T, LANES - QK_ROPE), F32)
    cos_t = jnp.concatenate([cos, cos, pad], axis=1)
    sin_t = jnp.concatenate([-sin, sin, pad], axis=1)

    tril = jnp.tril(jnp.ones((SG_CHUNK, SG_CHUNK), bool))
    w_tril = jnp.where(tril[None], small["sg_w_s"][0], 0.0).astype(BF16)
    b_lanes = jnp.broadcast_to(small["sg_b_s"][0][:, :, None], (SG_HEADS, SG_CHUNK, SG_DIM))
    conv_w = jnp.pad(small["sc_conv_w"][0], ((0, SUBLANES - CONV_TAPS), (0, 0)))
    ln_g = small["sg_ln_g"]
    pool_diag = jnp.zeros((POOL_WIDTH, POOL_WIDTH), F32)
    for g in range(len(POOL_WINDOWS)):
        pool_diag = pool_diag.at[POOL_DIM * g:POOL_DIM * (g + 1), POOL_DIM * g:POOL_DIM * (g + 1)].set(small["pool_w"][0, g])
    pool_diag = pool_diag.astype(BF16)
    pool_scale = small["pool_scale"]
    q_g = jnp.pad(small["q_norm"], ((0, 0), (0, QK_PAD - QK_DIM)))
    k_g = jnp.pad(small["k_norm"], ((0, 0), (0, QK_PAD - QK_DIM)))
    qa_g, kva_g = small["q_a_norm"], small["kv_a_norm"]
    in_shard = EVEN_IN // N_CHIPS

    def ffn_weights(l, w):
        return small["ffn_norm"][l], w[f"ffn_w_gate{l}"], w[f"ffn_w_up{l}"], w[f"ffn_w_down{l}"]

    W = fetch("even", ())
    w_in_even = W["even_w_in"]
    h0 = _rmsnorm_fwd("mix0_norm", x0, small["mix_norm"][0], tm)
    tb = _big_tile(T)
    proj0 = _matmul("even_in", "nn", [(h0, w_in_even)],
                    [(BS((tb, D_MODEL), lambda i, j, k: (i, 0)), BS((None, D_MODEL, in_shard), lambda i, j, k: (j, 0, 0)))],
                    (T // tb, N_CHIPS, 1), _sds((T, EVEN_IN), F32), BS((tb, in_shard), lambda i, j, k: (i, j)), (tb, in_shard))
    mix0 = _even_mixer_fwd(proj0, ln_g, w_tril, b_lanes, conv_w, seq, tm)
    w_out_even = W["even_w_out"]
    x1 = _mm("even_out", "nn", mix0, w_out_even, F32, tk=1024, add=x0)
    ffn0 = ffn_weights(0, fetch("ffn0", (x1,)))
    x2, ffn0_saved = _ffn_fwd(0, x1, *ffn0, tm)
    W = fetch("odd", (x2,))
    w_in_odd = jnp.pad(W["odd_w_in"], ((0, 0), (0, ODD_IN_PAD - ODD_IN)))
    q_b = jnp.pad(W["q_b"].reshape(Q_LORA, HEADS, QK_DIM).transpose(1, 0, 2), ((0, 0), (0, 0), (0, QK_PAD - QK_DIM)))
    kv_b = W["kv_b"].reshape(KV_LORA, HEADS, QK_NOPE + V_DIM).transpose(1, 0, 2)
    h2 = _rmsnorm_fwd("mix1_norm", x2, small["mix_norm"][1], tm)
    proj1 = _mm("odd_in", "nn", h2, w_in_odd, F32, tk=1024)
    mix1 = _pool_fwd(proj1, pool_diag, pool_scale, seq, tm)
    q, k, v = _mla_qkv_fwd(proj1, cos_t, sin_t, qa_g, kva_g, q_b, kv_b, q_g, k_g, tm)
    mix1, lse = _flash_fwd(q, k, v, mix1, batch, seq)
    x3 = _mm("odd_out", "nn", mix1, W["odd_w_out"], F32, tk=1024, add=x2)
    ffn1 = ffn_weights(1, fetch("ffn1", (x3,)))
    x4, ffn1_saved = _ffn_fwd(1, x3, *ffn1, tm)
    dy, sq = _loss_head(x4, target.reshape(T, D_MODEL), tm)

    G = {}
    dx3, dffn_g1 = _ffn_bwd(1, x3, *ffn1, ffn1_saved, dy, tm, emit)
    dmix1 = _mm("odd_out_dx", "nt", dx3, W["odd_w_out"], BF16, tk=1024, after=advance(dx3))
    dw_out_odd = _mm("odd_out_dw", "tn", mix1, dx3, F32)
    dq, dk, dv = _flash_bwd(q, k, v, dmix1, mix1, lse, batch, seq)
    dz_pool, dpool_diag, G["pool_scale"] = _pool_bwd(proj1, dmix1, pool_diag, pool_scale, seq, tm)
    dproj1, dq_b, dkv_b, dq_g, dk_g, G["q_a_norm"], G["kv_a_norm"] = _mla_qkv_bwd(
        proj1, cos_t, sin_t, qa_g, kva_g, q_b, kv_b, q_g, k_g, dq, dk, dv, dz_pool, tm)
    G["pool_w"] = jnp.stack([dpool_diag[POOL_DIM * g:POOL_DIM * (g + 1), POOL_DIM * g:POOL_DIM * (g + 1)]
                             for g in range(len(POOL_WINDOWS))])[None]
    G["q_norm"], G["k_norm"] = dq_g[:, :QK_DIM], dk_g[:, :QK_DIM]
    dw_in_odd = _mm("odd_in_dw", "tn", h2, dproj1, F32, tn=ODD_IN)

    def shard_major(g, cols):
        return g.reshape(g.shape[0], N_CHIPS, cols).transpose(1, 0, 2)

    behind = emit("odd", {"odd_w_in": dw_in_odd.reshape(N_CHIPS, -1, ODD_IN),
                          "q_b": shard_major(dq_b[:, :, :QK_DIM].transpose(1, 0, 2).reshape(Q_LORA, HEADS * QK_DIM), HEADS * QK_DIM // N_CHIPS),
                          "kv_b": shard_major(dkv_b.transpose(1, 0, 2).reshape(KV_LORA, HEADS * (QK_NOPE + V_DIM)),
                                              HEADS * (QK_NOPE + V_DIM) // N_CHIPS),
                          "odd_w_out": dw_out_odd.reshape(N_CHIPS, -1, D_MODEL)})
    dh2 = _mm("odd_in_dx", "nt", dproj1, W["odd_w_in"], F32, tk=ODD_IN, after=behind)
    dx2, dmix_g1 = _rmsnorm_bwd("mix1_norm_bwd", x2, small["mix_norm"][1], dh2, dx3, tm)
    dx1, dffn_g0 = _ffn_bwd(0, x1, *ffn0, ffn0_saved, dx2, tm, emit, after=advance(dx2))
    dmix0 = _mm("even_out_dx", "nt", dx1, w_out_even, F32, tk=1024, after=advance(dx1))
    dw_out_even = _mm("even_out_dw", "tn", mix0, dx1, F32)
    dproj0, dw_s, db_lanes, G["sg_ln_g"], dconv = _even_mixer_bwd(proj0, dmix0, ln_g, w_tril, b_lanes, conv_w, seq, tm)
    G["sg_w_s"] = dw_s[None]
    G["sg_b_s"] = jnp.sum(db_lanes, axis=-1)[None]
    G["sc_conv_w"] = dconv[None, :CONV_TAPS]
    dh0 = _matmul("even_in_dx", "nt", [(dproj0, w_in_even)],
                  [(BS((tb, in_shard), lambda i, j, k: (i, k)), BS((None, D_MODEL, in_shard), lambda i, j, k: (k, 0, 0)))],
                  (T // tb, 1, N_CHIPS), _sds((T, D_MODEL), F32), BS((tb, D_MODEL), lambda i, j, k: (i, 0)), (tb, D_MODEL))
    tk = min(512, T)
    dw_in_even = _grad_shards(
        "even_in_dw", h0, dproj0, BS((tk, D_MODEL), lambda k: (k, 0)), BS((tk, EVEN_IN), lambda k: (k, 0)),
        lambda a_ref, b_ref, j: (a_ref[...], b_ref[:, in_shard * j:in_shard * (j + 1)]), (N_CHIPS, D_MODEL, in_shard), T // tk)
    dx0, dmix_g0 = _rmsnorm_bwd("mix0_norm_bwd", x0, small["mix_norm"][0], dh0, dx1, tm)
    emit("even", {"even_w_in": dw_in_even, "even_w_out": dw_out_even.reshape(N_CHIPS, -1, D_MODEL)})
    advance(dx0)
    G["mix_norm"] = jnp.concatenate([dmix_g0, dmix_g1], axis=0)
    G["ffn_norm"] = jnp.concatenate([dffn_g0, dffn_g1], axis=0)
    return sq[0, 0], dx0.reshape(batch, seq, D_MODEL), G


def _small_vector(parts, names):
    flat = jnp.concatenate([parts[n].astype(F32).reshape(-1) for n in names])
    return _pad_rows(flat, LANES, 2 * SUBLANES)


def _split_small(vec, like, names):
    out, off, flat = {}, 0, vec.reshape(-1)
    for n in names:
        size = math.prod(like[n].shape)
        out[n] = flat[off:off + size].reshape(like[n].shape)
        off += size
    return out


def _whole_shape(a):
    return a.shape[:-1] + (a.shape[-1] * N_CHIPS,)


def kernel(x, positions, mix_norm, ffn_norm, even_w_in, sg_ln_g, sg_w_s, sg_b_s, sc_conv_w, even_w_out, odd_w_in, pool_w, pool_scale, q_a_norm, q_b, kv_a_norm, kv_b, q_norm, k_norm, odd_w_out, ffn_w_gate, ffn_w_up, ffn_w_down, loss_target, m_mix_norm, m_ffn_norm, m_even_w_in, m_sg_ln_g, m_sg_w_s, m_sg_b_s, m_sc_conv_w, m_even_w_out, m_odd_w_in, m_pool_w, m_pool_scale, m_q_a_norm, m_q_b, m_kv_a_norm, m_kv_b, m_q_norm, m_k_norm, m_odd_w_out, m_ffn_w_gate, m_ffn_w_up, m_ffn_w_down, v_mix_norm, v_ffn_norm, v_even_w_in, v_sg_ln_g, v_sg_w_s, v_sg_b_s, v_sc_conv_w, v_even_w_out, v_odd_w_in, v_pool_w, v_pool_scale, v_q_a_norm, v_q_b, v_kv_a_norm, v_kv_b, v_q_norm, v_k_norm, v_odd_w_out, v_ffn_w_gate, v_ffn_w_up, v_ffn_w_down):
    args = dict(locals())
    w = {n: args[n] for n in _WEIGHTS}
    m = {n: args["m_" + n] for n in _WEIGHTS}
    v = {n: args["v_" + n] for n in _WEIGHTS}
    cx, cy, cc = lax.axis_index("x"), lax.axis_index("y"), lax.axis_index("c")
    chip = 2 * cx + cy
    transposed = ("ffn_w_gate", "ffn_w_up")
    for n in transposed:
        w[n], m[n], v[n] = (jnp.swapaxes(t[n], 1, 2) for t in (w, m, v))

    chip_arr = chip.astype(jnp.int32).reshape(1)
    c_arr = cc.astype(jnp.int32).reshape(1)
    group_names = list(_GROUPS)
    order = [n for g in group_names for n in _GROUPS[g]]
    placed = {}
    for n in _SMALL_SHARDED:
        a = w[n]
        whole = jnp.zeros(a.shape[:-1] + (N_CHIPS, a.shape[-1]), F32)
        whole = lax.dynamic_update_slice_in_dim(whole, a[..., None, :], chip, axis=a.ndim - 1)
        placed[n] = jnp.where(cc == 0, whole, 0.0).reshape(_whole_shape(a))
    small_whole = _all_reduce_small("gather_small_weights", _small_vector(placed, _SMALL_SHARDED))
    small = dict({n: w[n] for n in _REPLICATED}, **_split_small(small_whole, placed, _SMALL_SHARDED))

    shards = _place_shards(w, chip_arr)
    sems, in_flight = _gather_send([shards[n] for n in order], [[order.index(n) for n in _GROUPS[g]] for g in group_names],
                                   (small_whole,))
    in_flight = dict(zip(order, in_flight))

    def fetch(group, after):
        gi, members = group_names.index(group), _GROUPS[group]
        landed = _gather_wait(f"gather_wait_{group}", [in_flight[n] for n in members], sems[2 * gi], sems[2 * gi + 1], after)
        return _whole_weights(dict(zip(members, _gather_pass(f"gather_pass_{group}", landed))))

    swapping, pending, arrived = [], [], {}

    def settle(after):
        names, ps, lands, send_sems, recv_sems = pending.pop()
        ps, lands = _scatter_wait(f"scatter_wait_{names[0]}", ps, lands, send_sems, recv_sems, after)
        arrived.update({n: (p, r) for n, p, r in zip(names, ps, lands)})

    def emit(group, grads):
        names = _GROUPS[group]
        halves = [grads[n].reshape(N_CHIPS, 2, grads[n].shape[1] // 2, grads[n].shape[2]) for n in names]
        send_sems, recv_sems, halves, lands, token = _exchange_send(f"exchange_send_{group}", halves)
        swapping.append((group, halves, lands, send_sems, recv_sems))
        return (token,)

    def advance(done):
        group, halves, lands, send_sems, recv_sems = swapping.pop()
        names = _GROUPS[group]
        halves, lands = _exchange_wait(f"exchange_wait_{group}", halves, lands, send_sems, recv_sems, (done,))
        partial = [_add_halves(f"add_{n}", g, r, c_arr) for n, g, r in zip(names, halves, lands)]
        if pending:
            settle((done,))
        send_sems, recv_sems, ps, lands, token = _scatter_send(f"scatter_send_{group}", partial)
        pending.append((names, ps, lands, send_sems, recv_sems))
        return (token,)

    sq, grad_x, G = _forward_backward(x, positions, loss_target, small, fetch, emit, advance)
    loss = lax.psum(0.5 * sq / D_MODEL, ("x", "y", "c"))

    small_names = _REPLICATED + _SMALL_SHARDED
    summed = _split_small(_all_reduce_small("reduce_small_grads", _small_vector(G, small_names)), G, small_names)
    grads = {n: summed[n] for n in _REPLICATED}
    for n in _SMALL_SHARDED:
        a = w[n]
        grads[n] = lax.dynamic_slice_in_dim(summed[n].reshape(a.shape[:-1] + (N_CHIPS, a.shape[-1])), chip, 1,
                                            axis=a.ndim - 1).reshape(a.shape)

    settle(())
    chip_c = jnp.stack([chip, cc]).astype(jnp.int32)
    sums = [_sum_partials(f"sum_{n}", *arrived[n], chip_c) for n in order]
    shard_grad = {n: f.reshape(1, -1, f.shape[-1]) for n, f in zip(order, _sibling_share(sums))}

    out = {}
    for n in ("even_w_in", "even_w_out", "odd_w_in", "q_b", "kv_b", "odd_w_out"):
        out[n] = _adamw(f"adamw_{n}", w[n], [shard_grad[n][0]], m[n], v[n])
    for n in ("ffn_w_gate", "ffn_w_up", "ffn_w_down"):
        out[n] = _adamw(f"adamw_{n}", w[n], [shard_grad[f"{n}{l}"][0] for l in range(2)], m[n], v[n])
    packed = [_small_vector(d, small_names) for d in (w, grads, m, v)]
    res = _adamw("adamw_small", packed[0][None], [packed[1]], packed[2][None], packed[3][None])
    delta_s, m_s, v_s = (_split_small(r, w, small_names) for r in res[1:])
    for n in small_names:
        out[n] = (grads[n], delta_s[n], m_s[n], v_s[n])
    for n in transposed:
        out[n] = tuple(jnp.swapaxes(t, 1, 2) for t in out[n])

    return (loss, grad_x, *[out[n][0] for n in _WEIGHTS], *[out[n][1] for n in _WEIGHTS],
            *[out[n][2] for n in _WEIGHTS], *[out[n][3] for n in _WEIGHTS])
```

```python
import functools
import math

import jax
import jax.numpy as jnp
from jax import lax
from jax.experimental import pallas as pl
from jax.experimental.pallas import tpu as pltpu

F32, BF16 = jnp.float32, jnp.bfloat16
BS = pl.BlockSpec

D_MODEL = 1024
EPS = 1e-6
NEG_INF = -1e30
SG_HEADS, SG_DIM, SG_WIDTH, SG_CHUNK = 4, 128, 512, 128
SC_WIDTH, CONV_TAPS = 512, 3
EVEN_IN = 2 * SG_WIDTH + 3 * SC_WIDTH
POOL_WINDOWS = (2, 4, 8, 16)
POOL_DIM, POOL_WIDTH = 64, 256
POOL_HALO = 16
HEADS, Q_LORA, KV_LORA, QK_NOPE, QK_ROPE, V_DIM = 6, 384, 256, 128, 64, 128
QK_DIM = QK_NOPE + QK_ROPE
QK_PAD = 256
ODD_IN = POOL_WIDTH + Q_LORA + KV_LORA + QK_ROPE
ODD_IN_PAD = 1024
ROPE_THETA = 10000.0
ATTN_SCALE = QK_DIM ** -0.5
D_FF, N_CHIPS = 2816, 4
FF_SHARD = D_FF // N_CHIPS
ADAM_LR, ADAM_B1, ADAM_B2, ADAM_EPS, ADAM_WD, ADAM_STEP = 0.001, 0.9, 0.999, 1e-08, 0.01, 10
VMEM_LIMIT_V7X = 48 * 2**20
LANES, SUBLANES = 128, 8
MESH = pl.DeviceIdType.MESH
HBM = pl.BlockSpec(memory_space=pltpu.HBM)
VMEM = pl.BlockSpec(memory_space=pltpu.VMEM)

_DIMS = {"nn": (((1,), (0,)), ((), ())), "nt": (((1,), (1,)), ((), ())), "tn": (((0,), (0,)), ((), ()))}


def _dot(a, b, mode="nn"):
    return lax.dot_general(a.astype(BF16), b.astype(BF16), _DIMS[mode], preferred_element_type=F32)


def _call(body, *, name, out_shape, in_specs, out_specs, grid=(), scratch=(), aliases=None, after=()):
    params = pltpu.CompilerParams(vmem_limit_bytes=VMEM_LIMIT_V7X,
                                  **({"dimension_semantics": ("arbitrary",) * len(grid)} if grid else {}))
    n_in, n_after = len(in_specs), len(after)
    kernel_body = body if not after else (lambda *refs: body(*refs[:n_in], *refs[n_in + n_after:]))
    call = pl.pallas_call(kernel_body, name=name, grid=grid, in_specs=list(in_specs) + [pl.BlockSpec(memory_space=pl.ANY)] * n_after,
                          out_specs=out_specs, out_shape=out_shape, scratch_shapes=list(scratch),
                          input_output_aliases=aliases or {}, compiler_params=params)
    return (lambda *ops: call(*ops, *after)) if after else call


def _sds(shape, dtype):
    return jax.ShapeDtypeStruct(tuple(shape), dtype)


def _token_tile(seq):
    return 512 if seq % 512 == 0 else seq


_TAIL_ROWS = 256


def _matmul(name, mode, pairs, pair_specs, grid, out_shape, out_spec, acc_shape, add=None, add_spec=None, after=(), tail=None):
    n, nk = len(pairs), grid[-1]
    n_add = int(add is not None)
    n_tail = len(tail[0]) if tail else 0
    n_in = 2 * n + n_add + n_tail
    n_out = len(out_shape) if tail else 1

    def body(*refs):
        ab = refs[:2 * n]
        add_ref = refs[2 * n] if n_add else None
        tail_refs, outs = refs[2 * n + n_add:n_in], refs[n_in:n_in + n_out]
        first = pl.program_id(0) == 0

        def finish(result):
            if tail is None:
                r = result(slice(None))
                outs[0][...] = (r if add_ref is None else r + add_ref[...]).astype(outs[0].dtype)
                return
            for lo in range(0, acc_shape[0], _TAIL_ROWS):
                rows = slice(lo, min(lo + _TAIL_ROWS, acc_shape[0]))
                r = result(rows)
                tail[2](rows, r if add_ref is None else r + add_ref[rows, :], first, tail_refs, outs)

        if nk == 1:
            r = _dot(ab[0][...], ab[1][...], mode)
            for p in range(1, n):
                r = r + _dot(ab[2 * p][...], ab[2 * p + 1][...], mode)
            finish(lambda rows: r[rows])
            return
        acc = refs[-1]
        k = pl.program_id(len(grid) - 1)

        @pl.when(k == 0)
        def _():
            acc[...] = jnp.zeros_like(acc)

        for p in range(n):
            acc[...] += _dot(ab[2 * p][...], ab[2 * p + 1][...], mode)

        @pl.when(k == nk - 1)
        def _():
            finish(lambda rows: acc[rows, :])

    ops = [t for pr in pairs for t in pr] + ([add] if n_add else []) + (list(tail[0]) if tail else [])
    specs = [s for pr in pair_specs for s in pr] + ([add_spec] if n_add else []) + (list(tail[1]) if tail else [])
    return _call(body, name=name, grid=grid, in_specs=specs, out_specs=out_spec, out_shape=out_shape,
                 scratch=[pltpu.VMEM(acc_shape, F32)] if nk > 1 else [], after=after)(*ops)


def _row_spec(tm, d):
    return BS((tm, d), lambda i, j, k: (i, 0))


def _vec_spec(d):
    return BS((1, d), lambda i, j, k: (0, 0))


def _norm_tail(gain, T, tm):
    d = gain.shape[-1]

    def fn(rows, r, first, tail_refs, outs):
        outs[0][rows, :] = r
        outs[1][rows, :] = (r * lax.rsqrt(jnp.mean(r * r, axis=-1, keepdims=True) + EPS) * tail_refs[0][...]).astype(BF16)

    return ([gain.reshape(1, d)], [_vec_spec(d)], fn), [_sds((T, d), F32), _sds((T, d), BF16)], [_row_spec(tm, d), _row_spec(tm, d)]


def _norm_bwd_tail(x, gain, dres, tm):
    T, d = x.shape

    def fn(rows, r, first, tail_refs, outs):
        x_ref, g_ref, dres_ref = tail_refs
        xv = x_ref[rows, :]
        rstd = lax.rsqrt(jnp.mean(xv * xv, axis=-1, keepdims=True) + EPS)
        xhat = xv * rstd
        if rows.start == 0:
            @pl.when(first)
            def _():
                outs[1][...] = jnp.zeros_like(outs[1])

        outs[1][...] += jnp.sum(r * xhat, axis=0, keepdims=True)
        dxhat = r * g_ref[...]
        outs[0][rows, :] = dres_ref[rows, :] + rstd * (dxhat - xhat * jnp.mean(dxhat * xhat, axis=-1, keepdims=True))

    return (([x, gain.reshape(1, d), dres], [_row_spec(tm, d), _vec_spec(d), _row_spec(tm, d)], fn),
            [_sds((T, d), F32), _sds((1, d), F32)], [_row_spec(tm, d), _vec_spec(d)])


def _loss_tail(target, tm):
    T, d = target.shape

    def fn(rows, r, first, tail_refs, outs):
        e = r - tail_refs[0][rows, :]
        if rows.start == 0:
            @pl.when(first)
            def _():
                outs[1][...] = jnp.zeros_like(outs[1])

        outs[1][...] += jnp.sum(e * e)
        outs[0][rows, :] = e * (1.0 / d)

    return (([target], [_row_spec(tm, d)], fn), [_sds((T, d), F32), _sds((SUBLANES, LANES), F32)],
            [_row_spec(tm, d), BS((SUBLANES, LANES), lambda i, j, k: (0, 0))])


def _grad_shards(name, a, b, a_spec, b_spec, pick, out_shape, n_steps):
    def body(a_ref, b_ref, o_ref):
        @pl.when(pl.program_id(0) == 0)
        def _():
            o_ref[...] = jnp.zeros_like(o_ref)

        for j in range(N_CHIPS):
            aj, bj = pick(a_ref, b_ref, j)
            o_ref[j] += _dot(aj, bj, "tn")

    return _call(body, name=name, grid=(n_steps,), in_specs=[a_spec, b_spec],
                 out_specs=BS(out_shape, lambda k: (0, 0, 0)), out_shape=pltpu.HBM(tuple(out_shape), F32))(a, b)


def _mm(name, mode, a, b, out_dtype, tm=1024, tn=1024, tk=512, add=None, after=(), fused=None, hbm_out=False):
    if mode == "tn":
        (K, M), N = a.shape, b.shape[1]
    else:
        (M, K), N = a.shape, (b.shape[1] if mode == "nn" else b.shape[0])
    tm, tn, tk = min(tm, M), min(tn, N), min(tk, K)
    a_spec = BS((tk, tm), lambda i, j, k: (k, i)) if mode == "tn" else BS((tm, tk), lambda i, j, k: (i, k))
    b_spec = BS((tn, tk), lambda i, j, k: (j, k)) if mode == "nt" else BS((tk, tn), lambda i, j, k: (k, j))
    o_spec = BS((tm, tn), lambda i, j, k: (i, j))
    tail, shapes, specs = fused if fused else (None, pltpu.HBM((M, N), out_dtype) if hbm_out else _sds((M, N), out_dtype), o_spec)
    return _matmul(name, mode, [(a, b)], [(a_spec, b_spec)], (M // tm, N // tn, K // tk), shapes, specs, (tm, tn),
                   add=add, add_spec=o_spec if add is not None else None, after=after, tail=tail)


def _rmsnorm_fwd(name, x, g, tm):
    T, d = x.shape

    def body(x_ref, g_ref, o_ref):
        xv = x_ref[...]
        y = xv * lax.rsqrt(jnp.mean(xv * xv, axis=-1, keepdims=True) + EPS)
        o_ref[...] = (y * g_ref[...]).astype(o_ref.dtype)

    return _call(body, name=name, grid=(T // tm,), in_specs=[BS((tm, d), lambda i: (i, 0)), BS((1, d), lambda i: (0, 0))],
                 out_specs=BS((tm, d), lambda i: (i, 0)), out_shape=_sds((T, d), BF16))(x, g.reshape(1, d))


def _ffn_up(name, h, wg, wu, tm):
    T = h.shape[0]

    def body(h_ref, wg_ref, wu_ref, g_ref, u_ref, a_ref):
        hv = h_ref[...]
        g = _dot(hv, wg_ref[...], "nt")
        u = _dot(hv, wu_ref[...], "nt")
        g_ref[...] = g.astype(BF16)
        u_ref[...] = u.astype(BF16)
        a_ref[...] = (g * (1.0 / (1.0 + jnp.exp(-g))) * u).astype(BF16)

    w_spec = BS((None, FF_SHARD, D_MODEL), lambda j, i: (j, 0, 0))
    o_spec = BS((None, tm, FF_SHARD), lambda j, i: (j, i, 0))
    sh = _sds((N_CHIPS, T, FF_SHARD), BF16)
    return _call(body, name=name, grid=(N_CHIPS, T // tm), in_specs=[BS((tm, D_MODEL), lambda j, i: (i, 0)), w_spec, w_spec],
                 out_specs=[o_spec, o_spec, o_spec], out_shape=[sh, sh, sh])(h, wg, wu)


def _ffn_act_bwd(name, dxo, wd, g, u, tm, after=()):
    T = dxo.shape[0]

    def body(dx_ref, wd_ref, g_ref, u_ref, dg_ref, du_ref):
        da = _dot(dx_ref[...], wd_ref[...], "nt")
        g = g_ref[...].astype(F32)
        sig = 1.0 / (1.0 + jnp.exp(-g))
        dg_ref[...] = (da * u_ref[...].astype(F32) * (sig * (1.0 + g * (1.0 - sig)))).astype(BF16)
        du_ref[...] = (da * (g * sig)).astype(BF16)

    t_spec = BS((None, tm, FF_SHARD), lambda i, j: (j, i, 0))
    sh = _sds((N_CHIPS, T, FF_SHARD), BF16)
    return _call(body, name=name, grid=(T // tm, N_CHIPS),
                 in_specs=[BS((tm, D_MODEL), lambda i, j: (i, 0)), BS((None, FF_SHARD, D_MODEL), lambda i, j: (j, 0, 0)), t_spec, t_spec],
                 out_specs=[t_spec, t_spec], out_shape=[sh, sh], after=after)(dxo, wd, g, u)


def _big_tile(n):
    return min(1024, n)


def _ffn_fwd(l, x, h, wg, wu, wd, fused):
    T = x.shape[0]
    tm = _big_tile(T)
    g, u, a = _ffn_up(f"ffn{l}_up", h, wg, wu, tm)
    tn = D_MODEL
    tail, shapes, specs = fused
    outs = _matmul(f"ffn{l}_down", "nn", [(a, wd)],
                   [(BS((None, tm, FF_SHARD), lambda i, j, k: (k, i, 0)), BS((None, FF_SHARD, tn), lambda i, j, k: (k, 0, j)))],
                   (T // tm, D_MODEL // tn, N_CHIPS), shapes, specs, (tm, tn),
                   add=x, add_spec=BS((tm, tn), lambda i, j, k: (i, j)), tail=tail)
    return outs, (h, g, u, a)


def _ffn_bwd(l, x, gain, wg, wu, wd, saved, dxo, emit, after=()):
    h, g, u, a = saved
    T = x.shape[0]
    tm = _big_tile(T)
    dg, du = _ffn_act_bwd(f"ffn{l}_act_bwd", dxo, wd, g, u, tm, after=after)
    tk = min(512, T)
    tn = D_MODEL
    shards_spec = BS((N_CHIPS, tk, FF_SHARD), lambda k: (0, k, 0))
    rows_spec = BS((tk, D_MODEL), lambda k: (k, 0))

    def dw(nm, act, rows):
        return _grad_shards(nm, act, rows, shards_spec, rows_spec, lambda a_ref, b_ref, j: (a_ref[j], b_ref[...]),
                            (N_CHIPS, FF_SHARD, D_MODEL), T // tk)

    behind = emit(f"ffn{l}", {f"ffn_w_gate{l}": dw(f"ffn{l}_dwg", dg, h), f"ffn_w_up{l}": dw(f"ffn{l}_dwu", du, h),
                              f"ffn_w_down{l}": dw(f"ffn{l}_dwd", a, dxo)})
    act_spec = BS((None, tm, FF_SHARD), lambda i, j, k: (k, i, 0))
    w_spec = BS((None, FF_SHARD, tn), lambda i, j, k: (k, 0, j))
    tail, shapes, specs = _norm_bwd_tail(x, gain, dxo, tm)
    return _matmul(f"ffn{l}_dh", "nn", [(dg, wg), (du, wu)], [(act_spec, w_spec), (act_spec, w_spec)],
                   (T // tm, D_MODEL // tn, N_CHIPS), shapes, specs, (tm, tn), after=behind, tail=tail)


_INV_SQRT2 = 1.0 / math.sqrt(2.0)
_INV_SQRT_2PI = 1.0 / math.sqrt(2.0 * math.pi)


def _gelu(x):
    return 0.5 * x * (1.0 + lax.erf(x * _INV_SQRT2))


def _gelu_grad(x):
    return 0.5 * (1.0 + lax.erf(x * _INV_SQRT2)) + x * jnp.exp(-0.5 * x * x) * _INV_SQRT_2PI


def _shift_down(x, k):
    return pltpu.roll(x, k, 0)


def _shift_up(x, k):
    return pltpu.roll(x, x.shape[0] - k, 0)


def _layer_norm_head(xh):
    xc = xh - jnp.mean(xh, axis=-1, keepdims=True)
    rstd = lax.rsqrt(jnp.mean(xc * xc, axis=-1, keepdims=True) + EPS)
    return xc * rstd, rstd


def _even_halo_specs(tm, n_tiles, col_blocks, after):
    rows = tm // SUBLANES
    last = n_tiles * rows - 1
    if after:
        return [BS((SUBLANES, 512), functools.partial(lambda cb, i: (jnp.minimum((i + 1) * rows, last), cb), cb)) for cb in col_blocks]
    return [BS((SUBLANES, 512), functools.partial(lambda cb, i: (jnp.maximum(i * rows - 1, 0), cb), cb)) for cb in col_blocks]


def _even_mixer_fwd(proj, ln_g, w_tril, b_lanes, conv_w, seq, tm):
    T = proj.shape[0]
    tiles_per_seq = seq // tm

    def body(p_ref, hc_ref, hh_ref, lng_ref, w_ref, bb_ref, cw_ref, o_ref):
        first = pl.program_id(0) % tiles_per_seq == 0
        for h in range(SG_HEADS):
            cols = slice(SG_DIM * h, SG_DIM * (h + 1))
            vhat, _ = _layer_norm_head(_gelu(p_ref[:, SG_WIDTH + SG_DIM * h:SG_WIDTH + SG_DIM * (h + 1)]))
            vln = (vhat * lng_ref[:, cols]).astype(BF16)
            for k in range(tm // SG_CHUNK):
                rows = slice(SG_CHUNK * k, SG_CHUNK * (k + 1))
                mixed = _dot(w_ref[h], vln[rows]) + bb_ref[h]
                o_ref[rows, cols] = (_gelu(p_ref[rows, cols]) * mixed).astype(BF16)
        z = p_ref[:, 1536:2048] * p_ref[:, 2048:2560]
        zz = jnp.concatenate([jnp.where(first, 0.0, hc_ref[...] * hh_ref[...]), z], axis=0)
        y = cw_ref[0:1, :] * _shift_down(zz, 2)[SUBLANES:] + cw_ref[1:2, :] * _shift_down(zz, 1)[SUBLANES:] + cw_ref[2:3, :] * z
        o_ref[:, SG_WIDTH:] = (p_ref[:, 1024:1536] * y).astype(BF16)

    full = lambda shape: BS(shape, lambda i: (0,) * len(shape))
    return _call(body, name="even_mixer_fwd", grid=(T // tm,),
                 in_specs=[BS((tm, EVEN_IN), lambda i: (i, 0))] + _even_halo_specs(tm, T // tm, (3, 4), after=False)
                 + [full((1, SG_WIDTH)), full((SG_HEADS, SG_CHUNK, SG_CHUNK)), full((SG_HEADS, SG_CHUNK, SG_DIM)), full((SUBLANES, SC_WIDTH))],
                 out_specs=BS((tm, D_MODEL), lambda i: (i, 0)), out_shape=_sds((T, D_MODEL), BF16))(
        proj, proj, proj, ln_g, w_tril, b_lanes, conv_w)


def _even_mixer_bwd(proj, dmix, ln_g, w_tril, b_lanes, conv_w, seq, tm):
    T = proj.shape[0]
    n_tiles, tiles_per_seq = T // tm, seq // tm

    def body(p_ref, dm_ref, hc_ref, hh_ref, nd_ref, nb_ref, lng_ref, w_ref, bb_ref, cw_ref,
             dp_ref, dw_ref, db_ref, dlng_ref, dcw_ref):
        i = pl.program_id(0)
        first = i % tiles_per_seq == 0
        last = i % tiles_per_seq == tiles_per_seq - 1

        @pl.when(i == 0)
        def _():
            dw_ref[...] = jnp.zeros_like(dw_ref)
            db_ref[...] = jnp.zeros_like(db_ref)
            dlng_ref[...] = jnp.zeros_like(dlng_ref)
            dcw_ref[...] = jnp.zeros_like(dcw_ref)

        for h in range(SG_HEADS):
            cols = slice(SG_DIM * h, SG_DIM * (h + 1))
            vcols = slice(SG_WIDTH + SG_DIM * h, SG_WIDTH + SG_DIM * (h + 1))
            lng = lng_ref[:, cols]
            for k in range(tm // SG_CHUNK):
                rows = slice(SG_CHUNK * k, SG_CHUNK * (k + 1))
                v = p_ref[rows, vcols]
                vhat, rstd = _layer_norm_head(_gelu(v))
                vln = (vhat * lng).astype(BF16)
                mixed = _dot(w_ref[h], vln) + bb_ref[h]
                u = p_ref[rows, cols]
                da = dm_ref[rows, cols]
                dp_ref[rows, cols] = (da * mixed * _gelu_grad(u)).astype(BF16)
                dmixed = da * _gelu(u)
                db_ref[h] += dmixed
                dw_ref[h] += _dot(dmixed, vln, "nt")
                dvln = _dot(w_ref[h], dmixed, "tn")
                dlng_ref[:, cols] += jnp.sum(dvln * vhat, axis=0, keepdims=True)
                dvhat = dvln * lng
                dgv = rstd * (dvhat - jnp.mean(dvhat, axis=-1, keepdims=True)
                              - vhat * jnp.mean(dvhat * vhat, axis=-1, keepdims=True))
                dp_ref[rows, vcols] = (dgv * _gelu_grad(v)).astype(BF16)

        b = p_ref[:, 1024:1536]
        c = p_ref[:, 1536:2048]
        hv = p_ref[:, 2048:2560]
        z = c * hv
        zz = jnp.concatenate([jnp.where(first, 0.0, hc_ref[...] * hh_ref[...]), z], axis=0)
        z1 = _shift_down(zz, 1)[SUBLANES:]
        z2 = _shift_down(zz, 2)[SUBLANES:]
        w0, w1, w2 = cw_ref[0:1, :], cw_ref[1:2, :], cw_ref[2:3, :]
        dbo = dm_ref[:, SG_WIDTH:]
        dy = dbo * b
        dd = jnp.concatenate([dy, jnp.where(last, 0.0, nd_ref[...] * nb_ref[...])], axis=0)
        dz = w2 * dy + w1 * _shift_up(dd, 1)[:tm] + w0 * _shift_up(dd, 2)[:tm]
        dp_ref[:, 1024:1536] = (dbo * (w0 * z2 + w1 * z1 + w2 * z)).astype(BF16)
        dp_ref[:, 1536:2048] = (dz * hv).astype(BF16)
        dp_ref[:, 2048:2560] = (dz * c).astype(BF16)
        dcw_ref[0:1, :] += jnp.sum(dy * z2, axis=0, keepdims=True)
        dcw_ref[1:2, :] += jnp.sum(dy * z1, axis=0, keepdims=True)
        dcw_ref[2:3, :] += jnp.sum(dy * z, axis=0, keepdims=True)

        @pl.when(i == n_tiles - 1)
        def _():
            t_idx = lax.broadcasted_iota(jnp.int32, (SG_CHUNK, SG_CHUNK), 0)
            s_idx = lax.broadcasted_iota(jnp.int32, (SG_CHUNK, SG_CHUNK), 1)
            for h in range(SG_HEADS):
                dw_ref[h] = jnp.where(t_idx >= s_idx, dw_ref[h], 0.0)

    full = lambda shape: BS(shape, lambda i: (0,) * len(shape))
    sq = (SG_HEADS, SG_CHUNK, SG_CHUNK)
    return _call(body, name="even_mixer_bwd", grid=(n_tiles,),
                 in_specs=[BS((tm, EVEN_IN), lambda i: (i, 0)), BS((tm, D_MODEL), lambda i: (i, 0))]
                 + _even_halo_specs(tm, n_tiles, (3, 4), after=False)
                 + _even_halo_specs(tm, n_tiles, (1,), after=True) + _even_halo_specs(tm, n_tiles, (2,), after=True)
                 + [full((1, SG_WIDTH)), full(sq), full(sq), full((SUBLANES, SC_WIDTH))],
                 out_specs=[BS((tm, EVEN_IN), lambda i: (i, 0)), full(sq), full(sq), full((1, SG_WIDTH)), full((SUBLANES, SC_WIDTH))],
                 out_shape=[_sds((T, EVEN_IN), BF16), _sds(sq, F32), _sds(sq, F32), _sds((1, SG_WIDTH), F32), _sds((SUBLANES, SC_WIDTH), F32)])(
        proj, dmix, proj, proj, dmix, proj, ln_g, w_tril, b_lanes, conv_w)


def _pool_select(vals):
    lane = lax.broadcasted_iota(jnp.int32, vals[0].shape, 1)
    out = vals[-1]
    for g in range(len(vals) - 2, -1, -1):
        out = jnp.where(lane < POOL_DIM * (g + 1), vals[g], out)
    return out


def _pool_counts(pos1):
    lane = lax.broadcasted_iota(jnp.int32, (pos1.shape[0], POOL_WIDTH), 1)
    win = _pool_select([jnp.full(lane.shape, float(w), F32) for w in POOL_WINDOWS])
    return jnp.minimum(pos1, win)


def _pool_means(zz, counts):
    s2 = zz + _shift_down(zz, 1)
    s4 = s2 + _shift_down(s2, 2)
    s8 = s4 + _shift_down(s4, 4)
    s16 = s8 + _shift_down(s8, 8)
    return _pool_select([s2, s4, s8, s16])[POOL_HALO:] / counts


def _pool_halo_spec(tm, n_tiles, after):
    rows = tm // POOL_HALO
    if after:
        return BS((POOL_HALO, POOL_WIDTH), lambda i: (jnp.minimum((i + 1) * rows, n_tiles * rows - 1), 0))
    return BS((POOL_HALO, POOL_WIDTH), lambda i: (jnp.maximum(i * rows - 1, 0), 0))


def _pool_fwd(proj, w_diag, scale, seq, tm):
    T = proj.shape[0]
    tiles_per_seq = seq // tm

    def body(z_ref, zh_ref, w_ref, s_ref, o_ref):
        t = pl.program_id(0) % tiles_per_seq
        z = z_ref[...]
        zz = jnp.concatenate([jnp.where(t == 0, 0.0, zh_ref[...]), z], axis=0)
        pos1 = (lax.broadcasted_iota(jnp.int32, (tm, 1), 0) + (t * tm + 1)).astype(F32)
        pooled = _pool_means(zz, _pool_counts(pos1)) - z
        o_ref[...] = (_dot(pooled, w_ref[...]) * s_ref[...]).astype(BF16)

    full = lambda shape: BS(shape, lambda i: (0,) * len(shape))
    return _call(body, name="pool_fwd", grid=(T // tm,),
                 in_specs=[BS((tm, POOL_WIDTH), lambda i: (i, 0)), _pool_halo_spec(tm, T // tm, False),
                           full((POOL_WIDTH, POOL_WIDTH)), full((1, POOL_WIDTH))],
                 out_specs=BS((tm, POOL_WIDTH), lambda i: (i, 0)), out_shape=_sds((T, D_MODEL), BF16))(proj, proj, w_diag, scale)


def _pool_bwd(proj, dmix, w_diag, scale, seq, tm):
    T = proj.shape[0]
    n_tiles, tiles_per_seq = T // tm, seq // tm

    def body(z_ref, zh_ref, do_ref, don_ref, w_ref, s_ref, dz_ref, dw_ref, ds_ref):
        i = pl.program_id(0)
        t = i % tiles_per_seq

        @pl.when(i == 0)
        def _():
            dw_ref[...] = jnp.zeros_like(dw_ref)
            ds_ref[...] = jnp.zeros_like(ds_ref)

        z = z_ref[...]
        zz = jnp.concatenate([jnp.where(t == 0, 0.0, zh_ref[...]), z], axis=0)
        pos1 = (lax.broadcasted_iota(jnp.int32, (tm, 1), 0) + (t * tm + 1)).astype(F32)
        counts = _pool_counts(pos1)
        pooled = _pool_means(zz, counts) - z
        dout = do_ref[...].astype(F32)
        ds_ref[...] += jnp.sum(dout * _dot(pooled, w_ref[...]), axis=0, keepdims=True)
        dlin = dout * s_ref[...]
        dw_ref[...] += _dot(pooled, dlin, "tn")
        dpooled = _dot(dlin, w_ref[...], "nt")
        dpooled_n = _dot(don_ref[...].astype(F32) * s_ref[...], w_ref[...], "nt")
        pos1_n = (lax.broadcasted_iota(jnp.int32, (POOL_HALO, 1), 0) + ((t + 1) * tm + 1)).astype(F32)
        dmean_n = jnp.where(t == tiles_per_seq - 1, 0.0, dpooled_n / _pool_counts(pos1_n))
        dd = jnp.concatenate([dpooled / counts, dmean_n], axis=0)
        r2 = dd + _shift_up(dd, 1)
        r4 = r2 + _shift_up(r2, 2)
        r8 = r4 + _shift_up(r4, 4)
        r16 = r8 + _shift_up(r8, 8)
        dz_ref[...] = (_pool_select([r2, r4, r8, r16])[:tm] - dpooled).astype(BF16)

    full = lambda shape: BS(shape, lambda i: (0,) * len(shape))
    return _call(body, name="pool_bwd", grid=(n_tiles,),
                 in_specs=[BS((tm, POOL_WIDTH), lambda i: (i, 0)), _pool_halo_spec(tm, n_tiles, False),
                           BS((tm, POOL_WIDTH), lambda i: (i, 0)), _pool_halo_spec(tm, n_tiles, True),
                           full((POOL_WIDTH, POOL_WIDTH)), full((1, POOL_WIDTH))],
                 out_specs=[BS((tm, POOL_WIDTH), lambda i: (i, 0)), full((POOL_WIDTH, POOL_WIDTH)), full((1, POOL_WIDTH))],
                 out_shape=[_sds((T, POOL_WIDTH), BF16), _sds((POOL_WIDTH, POOL_WIDTH), F32), _sds((1, POOL_WIDTH), F32)])(
        proj, proj, dmix, dmix, w_diag, scale)


def _rope_partner(r):
    lane = lax.broadcasted_iota(jnp.int32, r.shape, 1)
    return jnp.where(lane < QK_ROPE // 2, pltpu.roll(r, LANES - QK_ROPE // 2, 1), pltpu.roll(r, QK_ROPE // 2, 1))


def _rope(x, cos, sin_signed):
    r = x[:, QK_NOPE:]
    return jnp.concatenate([x[:, :QK_NOPE], r * cos + _rope_partner(r) * sin_signed], axis=1)


def _rope_transposed(dx, cos, sin_signed):
    dr = dx[:, QK_NOPE:]
    return jnp.concatenate([dx[:, :QK_NOPE], dr * cos + _rope_partner(dr * sin_signed)], axis=1)


def _head_norm(x):
    r = lax.rsqrt(jnp.sum(x * x, axis=-1, keepdims=True) * (1.0 / QK_DIM) + EPS)
    return x * r, r


def _head_norm_bwd(dy, xhat, r, gain):
    dxhat = dy * gain
    return r * (dxhat - xhat * (jnp.sum(dxhat * xhat, axis=-1, keepdims=True) * (1.0 / QK_DIM)))


def _latents(p_ref, qag_ref, kvag_ref):
    ql = p_ref[:, POOL_WIDTH:POOL_WIDTH + Q_LORA]
    kvl = p_ref[:, POOL_WIDTH + Q_LORA:POOL_WIDTH + Q_LORA + KV_LORA]
    rq = lax.rsqrt(jnp.mean(ql * ql, axis=-1, keepdims=True) + EPS)
    rkv = lax.rsqrt(jnp.mean(kvl * kvl, axis=-1, keepdims=True) + EPS)
    return ql * rq, rq, kvl * rkv, rkv


def _mla_specs(tm):
    full = lambda shape: BS(shape, lambda i, h: (0,) * len(shape))
    return [BS((tm, ODD_IN_PAD), lambda i, h: (i, 0)), BS((tm, LANES), lambda i, h: (i, 0)), BS((tm, LANES), lambda i, h: (i, 0)),
            full((1, Q_LORA)), full((1, KV_LORA)), BS((None, Q_LORA, QK_PAD), lambda i, h: (h, 0, 0)),
            BS((None, KV_LORA, QK_PAD), lambda i, h: (h, 0, 0)), full((1, QK_PAD)), full((1, QK_PAD))]


def _mla_qkv_fwd(proj, cos, sin_signed, qa_g, kva_g, q_b, kv_b, q_g, k_g, tm):
    T = proj.shape[0]

    def body(p_ref, cos_ref, sin_ref, qag_ref, kvag_ref, qb_ref, kvb_ref, qg_ref, kg_ref, q_ref, k_ref, v_ref, qn_s, kvn_s):
        @pl.when(pl.program_id(1) == 0)
        def _():
            qhat, _, kvhat, _ = _latents(p_ref, qag_ref, kvag_ref)
            qn_s[...] = (qhat * qag_ref[...]).astype(BF16)
            kvn_s[...] = (kvhat * kvag_ref[...]).astype(BF16)

        cos, sin = cos_ref[...], sin_ref[...]
        qhat, _ = _head_norm(_dot(qn_s[...], qb_ref[...]))
        q_ref[...] = _rope(qhat * qg_ref[...], cos, sin).astype(BF16)
        kv = _dot(kvn_s[...], kvb_ref[...])
        khat, _ = _head_norm(jnp.concatenate([kv[:, :QK_NOPE], p_ref[:, ODD_IN_PAD - LANES:]], axis=1))
        k_ref[...] = _rope(khat * kg_ref[...], cos, sin).astype(BF16)
        v_ref[...] = kv[:, QK_NOPE:].astype(BF16)

    qk_spec = BS((None, tm, QK_PAD), lambda i, h: (h, i, 0))
    return _call(body, name="mla_qkv_fwd", grid=(T // tm, HEADS), in_specs=_mla_specs(tm),
                 out_specs=[qk_spec, qk_spec, BS((None, tm, V_DIM), lambda i, h: (h, i, 0))],
                 out_shape=[_sds((HEADS, T, QK_PAD), BF16), _sds((HEADS, T, QK_PAD), BF16), _sds((HEADS, T, V_DIM), BF16)],
                 scratch=[pltpu.VMEM((tm, Q_LORA), BF16), pltpu.VMEM((tm, KV_LORA), BF16)])(
        proj, cos, sin_signed, qa_g, kva_g, q_b, kv_b, q_g, k_g)


def _mla_qkv_bwd(proj, cos, sin_signed, qa_g, kva_g, q_b, kv_b, q_g, k_g, dq, dk, dv, dz_pool, tm):
    T = proj.shape[0]
    n_tiles = T // tm

    def body(p_ref, cos_ref, sin_ref, qag_ref, kvag_ref, qb_ref, kvb_ref, qg_ref, kg_ref, dq_ref, dk_ref, dv_ref, dzp_ref,
             dp_ref, dqb_ref, dkvb_ref, dqg_ref, dkg_ref, dqag_ref, dkvag_ref, qn_s, kvn_s, dqn_s, dkvn_s, dkr_s):
        i, h = pl.program_id(0), pl.program_id(1)

        @pl.when((i == 0) & (h == 0))
        def _():
            for ref in (dqb_ref, dkvb_ref, dqg_ref, dkg_ref, dqag_ref, dkvag_ref):
                ref[...] = jnp.zeros_like(ref)

        @pl.when(h == 0)
        def _():
            qhat, _, kvhat, _ = _latents(p_ref, qag_ref, kvag_ref)
            qn_s[...] = (qhat * qag_ref[...]).astype(BF16)
            kvn_s[...] = (kvhat * kvag_ref[...]).astype(BF16)
            dqn_s[...] = jnp.zeros_like(dqn_s)
            dkvn_s[...] = jnp.zeros_like(dkvn_s)
            dkr_s[...] = jnp.zeros_like(dkr_s)

        cos, sin = cos_ref[...], sin_ref[...]
        qhat, rq = _head_norm(_dot(qn_s[...], qb_ref[...]))
        dqn_head = _rope_transposed(dq_ref[...], cos, sin)
        dqg_ref[...] += jnp.sum(dqn_head * qhat, axis=0, keepdims=True)
        dqh = _head_norm_bwd(dqn_head, qhat, rq, qg_ref[...])
        dqb_ref[h] += _dot(qn_s[...], dqh, "tn")
        dqn_s[...] += _dot(dqh, qb_ref[...], "nt")

        kv = _dot(kvn_s[...], kvb_ref[...])
        khat, rk = _head_norm(jnp.concatenate([kv[:, :QK_NOPE], p_ref[:, ODD_IN_PAD - LANES:]], axis=1))
        dkn_head = _rope_transposed(dk_ref[...], cos, sin)
        dkg_ref[...] += jnp.sum(dkn_head * khat, axis=0, keepdims=True)
        dkf = _head_norm_bwd(dkn_head, khat, rk, kg_ref[...])
        dkr_s[...] += dkf[:, QK_NOPE:]
        dkv = jnp.concatenate([dkf[:, :QK_NOPE], dv_ref[...]], axis=1)
        dkvb_ref[h] += _dot(kvn_s[...], dkv, "tn")
        dkvn_s[...] += _dot(dkv, kvb_ref[...], "nt")

        @pl.when(h == HEADS - 1)
        def _():
            qhat_l, rql, kvhat_l, rkvl = _latents(p_ref, qag_ref, kvag_ref)
            dqn, dkvn = dqn_s[...], dkvn_s[...]
            dqag_ref[...] += jnp.sum(dqn * qhat_l, axis=0, keepdims=True)
            dkvag_ref[...] += jnp.sum(dkvn * kvhat_l, axis=0, keepdims=True)
            dqx, dkvx = dqn * qag_ref[...], dkvn * kvag_ref[...]
            dp_ref[:, :POOL_WIDTH] = dzp_ref[...]
            dp_ref[:, POOL_WIDTH:POOL_WIDTH + Q_LORA] = (
                rql * (dqx - qhat_l * jnp.mean(dqx * qhat_l, axis=-1, keepdims=True))).astype(BF16)
            dp_ref[:, POOL_WIDTH + Q_LORA:ODD_IN_PAD - LANES] = (
                rkvl * (dkvx - kvhat_l * jnp.mean(dkvx * kvhat_l, axis=-1, keepdims=True))).astype(BF16)
            dp_ref[:, ODD_IN_PAD - LANES:] = dkr_s[:, :QK_ROPE].astype(BF16)

    full = lambda shape: BS(shape, lambda i, h: (0,) * len(shape))
    qk_spec = BS((None, tm, QK_PAD), lambda i, h: (h, i, 0))
    return _call(body, name="mla_qkv_bwd", grid=(n_tiles, HEADS),
                 in_specs=_mla_specs(tm) + [qk_spec, qk_spec, BS((None, tm, V_DIM), lambda i, h: (h, i, 0)),
                                            BS((tm, POOL_WIDTH), lambda i, h: (i, 0))],
                 out_specs=[BS((tm, ODD_IN), lambda i, h: (i, 0)), full((HEADS, Q_LORA, QK_PAD)), full((HEADS, KV_LORA, QK_PAD)),
                            full((1, QK_PAD)), full((1, QK_PAD)), full((1, Q_LORA)), full((1, KV_LORA))],
                 out_shape=[_sds((T, ODD_IN), BF16),_sds((HEADS, Q_LORA, QK_PAD), F32), _sds((HEADS, KV_LORA, QK_PAD), F32),
                            _sds((1, QK_PAD), F32), _sds((1, QK_PAD), F32), _sds((1, Q_LORA), F32), _sds((1, KV_LORA), F32)],
                 scratch=[pltpu.VMEM((tm, Q_LORA), BF16), pltpu.VMEM((tm, KV_LORA), BF16), pltpu.VMEM((tm, Q_LORA), F32),
                          pltpu.VMEM((tm, KV_LORA), F32), pltpu.VMEM((tm, LANES), F32)])(
        proj, cos, sin_signed, qa_g, kva_g, q_b, kv_b, q_g, k_g, dq, dk, dv, dz_pool)


def _attn_tile(seq):
    return 512 if seq % 512 == 0 else seq


def _causal_mask(s):
    row = lax.broadcasted_iota(jnp.int32, s.shape, 0)
    col = lax.broadcasted_iota(jnp.int32, s.shape, 1)
    return jnp.where(row >= col, s, NEG_INF)


def _tile(i, t):
    return slice(i * t, (i + 1) * t)


def _flash_fwd(q, k, v, mix, batch, seq):
    t = _attn_tile(seq)
    nq = seq // t

    def body(q_ref, k_ref, v_ref, _, o_ref, lse_ref):
        for qi in range(nq):
            rows, before = _tile(qi, t), slice(0, qi * t)
            qv = q_ref[rows, :]
            s_diag = _causal_mask(_dot(qv, k_ref[rows, :], "nt") * ATTN_SCALE)
            m = jnp.max(s_diag, axis=-1, keepdims=True)
            if qi:
                s_before = _dot(qv, k_ref[before, :], "nt") * ATTN_SCALE
                m = jnp.maximum(m, jnp.max(s_before, axis=-1, keepdims=True))
            p = jnp.exp(s_diag - m)
            l = jnp.sum(p, axis=-1, keepdims=True)
            acc = _dot(p, v_ref[rows, :])
            if qi:
                p = jnp.exp(s_before - m)
                l = l + jnp.sum(p, axis=-1, keepdims=True)
                acc = acc + _dot(p, v_ref[before, :])
            o_ref[rows, :] = (acc / l).astype(BF16)
            lse_ref[rows, :] = jnp.broadcast_to(m + jnp.log(l), (t, LANES))

    T = batch * seq
    whole = lambda w: BS((None, seq, w), lambda b, h: (h, b, 0))
    return _call(body, name="flash_fwd", grid=(batch, HEADS),
                 in_specs=[whole(QK_PAD), whole(QK_PAD), whole(V_DIM), pl.BlockSpec(memory_space=pl.ANY)],
                 out_specs=[BS((seq, V_DIM), lambda b, h: (b, POOL_WIDTH // V_DIM + h)), whole(LANES)],
                 out_shape=[_sds((T, D_MODEL), BF16), _sds((HEADS, T, LANES), F32)],
                 aliases={3: 0})(q, k, v, mix)


def _flash_bwd(q, k, v, dmix, mix, lse, batch, seq):
    t = _attn_tile(seq)
    nq = seq // t

    def body(q_ref, k_ref, v_ref, do_ref, o_ref, lse_ref, dq_ref, dk_ref, dv_ref, delta_s):
        dq_ref[...] = jnp.zeros_like(dq_ref)
        dk_ref[...] = jnp.zeros_like(dk_ref)
        dv_ref[...] = jnp.zeros_like(dv_ref)
        for qi in range(nq):
            rows = _tile(qi, t)
            delta_s[qi] = jnp.sum(do_ref[rows, :].astype(F32) * o_ref[rows, :].astype(F32), axis=-1, keepdims=True)
        for kb in range(nq):
            keys = _tile(kb, t)
            for qi in range(kb, nq):
                rows = _tile(qi, t)
                qv, kk, do = q_ref[rows, :], k_ref[keys, :], do_ref[rows, :]
                s = _dot(qv, kk, "nt") * ATTN_SCALE
                if kb == qi:
                    s = _causal_mask(s)
                p = jnp.exp(s - lse_ref[rows, 0:1])
                dv_ref[keys, :] += _dot(p, do, "tn")
                ds = p * (_dot(do, v_ref[keys, :], "nt") - delta_s[qi]) * ATTN_SCALE
                dq_ref[rows, :] += _dot(ds, kk)
                dk_ref[keys, :] += _dot(ds, qv, "tn")

    T = batch * seq
    whole = lambda w: BS((None, seq, w), lambda b, h: (h, b, 0))
    head_cols = BS((seq, V_DIM), lambda b, h: (b, POOL_WIDTH // V_DIM + h))
    return _call(body, name="flash_bwd", grid=(batch, HEADS),
                 in_specs=[whole(QK_PAD), whole(QK_PAD), whole(V_DIM), head_cols, head_cols, whole(LANES)],
                 out_specs=[whole(QK_PAD), whole(QK_PAD), whole(V_DIM)],
                 out_shape=[_sds((HEADS, T, QK_PAD), F32), _sds((HEADS, T, QK_PAD), F32), _sds((HEADS, T, V_DIM), F32)],
                 scratch=[pltpu.VMEM((nq, t, 1), F32)])(q, k, v, dmix, mix, lse)


def _adamw_math(w, g, m, v):
    m = ADAM_B1 * m + (1.0 - ADAM_B1) * g
    v = ADAM_B2 * v + (1.0 - ADAM_B2) * (g * g)
    m_hat = m / (1.0 - ADAM_B1 ** ADAM_STEP)
    v_hat = v / (1.0 - ADAM_B2 ** ADAM_STEP)
    return -ADAM_LR * (m_hat / (jnp.sqrt(v_hat) + ADAM_EPS) + ADAM_WD * w), m, v


def _adamw(name, w, g, m, v):
    L, R, C = w.shape
    tr = 256 if R % 256 == 0 else R
    outs = None
    for l in range(L):
        def body(w_ref, g_ref, m_ref, v_ref, *rest):
            go_ref, d_ref, mo_ref, vo_ref = rest[-4:]
            gv = g_ref[...]
            d_ref[...], mo_ref[...], vo_ref[...] = _adamw_math(w_ref[...], gv, m_ref[...], v_ref[...])
            go_ref[...] = gv

        layer = BS((None, tr, C), functools.partial(lambda l, i: (l, i, 0), l))
        prev = [] if outs is None else list(outs)
        outs = _call(body, name=f"{name}_{l}", grid=(R // tr,),
                     in_specs=[layer, BS((tr, C), lambda i: (i, 0)), layer, layer] + [pl.BlockSpec(memory_space=pl.ANY)] * len(prev),
                     out_specs=[layer] * 4, out_shape=[_sds((L, R, C), F32)] * 4,
                     aliases={4 + n: n for n in range(len(prev))})(w, g[l], m, v, *prev)
    return outs


def _place():
    x, y, c = lax.axis_index("x"), lax.axis_index("y"), lax.axis_index("c")
    other_chips = [(1 - x, y), (x, 1 - y), (1 - x, 1 - y)]
    return x, y, c, other_chips


def _remote(src, dst, send_sem, recv_sem, dev):
    return pltpu.make_async_remote_copy(src_ref=src, dst_ref=dst, send_sem=send_sem, recv_sem=recv_sem,
                                        device_id=dev, device_id_type=MESH)


def _prefetch_call(body, *, name, grid, in_specs, out_specs, out_shape):
    grid_spec = pltpu.PrefetchScalarGridSpec(num_scalar_prefetch=1, grid=grid, in_specs=in_specs, out_specs=out_specs)
    params = pltpu.CompilerParams(vmem_limit_bytes=VMEM_LIMIT_V7X, dimension_semantics=("arbitrary",) * len(grid))
    return pl.pallas_call(body, name=name, grid_spec=grid_spec, out_shape=out_shape, compiler_params=params)


def _row_tile(rows):
    return 256 if rows % 256 == 0 else rows


def _cast_place(name, w, layer, chip):
    _, _, rows, C = w.shape
    tr = _row_tile(rows)

    def body(chip_ref, w_ref, o_ref):
        o_ref[...] = w_ref[...].astype(BF16)

    return _prefetch_call(body, name=name, grid=(2, rows // tr),
                          in_specs=[BS((None, None, tr, C), lambda h, i, chip_ref: (layer, h, i, 0))],
                          out_specs=BS((None, None, tr, C), lambda h, i, chip_ref: (chip_ref[0], h, i, 0)),
                          out_shape=pltpu.HBM((N_CHIPS, 2, rows, C), BF16))(chip, w)


SEM = pl.BlockSpec(memory_space=pltpu.SEMAPHORE)


def _split_copy_call(body, *, name, in_specs, out_specs, out_shape, aliases):
    return pl.pallas_call(body, name=name, in_specs=in_specs, out_specs=out_specs, out_shape=out_shape,
                          input_output_aliases=aliases,
                          compiler_params=pltpu.CompilerParams(has_side_effects=pltpu.SideEffectType.DATAFLOW_SIDE_EFFECTING))


def _hbm(arrays):
    return [pltpu.with_memory_space_constraint(a, pltpu.HBM) for a in arrays]


def _gather_send(gs, groups, after):
    n = len(gs)

    def body(*refs):
        g, sems = refs[:n], refs[n + len(after):n + len(after) + 2 * len(groups)]
        x, y, c, chips = _place()
        me = 2 * x + y
        for gi, members in enumerate(groups):
            for a, i in enumerate(members):
                for k, (px, py) in enumerate(chips):
                    _remote(g[i].at[me, c], g[i].at[me, c], sems[2 * gi].at[3 * a + k], sems[2 * gi + 1].at[3 * a + k],
                            (px, py, c)).start()

    sem_shapes = [pltpu.SemaphoreType.DMA((3 * len(members),)) for members in groups for _ in range(2)]
    out = _split_copy_call(body, name="gather_send", in_specs=[HBM] * n + [pl.BlockSpec(memory_space=pl.ANY)] * len(after),
                           out_specs=[SEM] * len(sem_shapes) + [HBM] * n,
                           out_shape=sem_shapes + [pltpu.HBM(a.shape, a.dtype) for a in gs],
                           aliases={i: len(sem_shapes) + i for i in range(n)})(*_hbm(gs), *after)
    return out[:len(sem_shapes)], out[len(sem_shapes):]


def _gather_wait(name, gs, send_sems, recv_sems, after):
    n = len(gs)

    def body(*refs):
        g, ssem, rsem = refs[:n], refs[n], refs[n + 1]
        x, y, c, chips = _place()
        me = 2 * x + y
        for a in range(n):
            for k, (px, py) in enumerate(chips):
                landed = g[a].at[2 * px + py, c]
                cp = _remote(g[a].at[me, c], landed, ssem.at[3 * a + k], rsem.at[3 * a + k], (px, py, c))
                cp.wait_recv()
                cp.wait_send()

    return _split_copy_call(body, name=name, in_specs=[HBM] * n + [SEM, SEM] + [pl.BlockSpec(memory_space=pl.ANY)] * len(after),
                            out_specs=[HBM] * n, out_shape=[pltpu.HBM(a.shape, a.dtype) for a in gs],
                            aliases={i: i for i in range(n)})(*gs, send_sems, recv_sems, *after)


def _gather_pass(name, gs):
    n = len(gs)

    def body(*refs):
        g, send_sems, recv_sems = refs[n:2 * n], refs[-2], refs[-1]
        x, y, c, chips = _place()
        sibling = (x, y, 1 - c)
        passed = [_remote(g[i].at[2 * px + py, c], g[i].at[2 * px + py, c], send_sems.at[3 * i + k], recv_sems.at[3 * i + k], sibling)
                  for i in range(n) for k, (px, py) in enumerate(chips)]
        for cp in passed:
            cp.start()
        for i in range(n):
            for k, (px, py) in enumerate(chips):
                theirs = g[i].at[2 * px + py, 1 - c]
                _remote(theirs, theirs, send_sems.at[3 * i + k], recv_sems.at[3 * i + k], sibling).wait_recv()
        for cp in passed:
            cp.wait_send()

    return _call(body, name=name, in_specs=[HBM] * n, out_specs=[HBM] * n, out_shape=[_sds(a.shape, a.dtype) for a in gs],
                 aliases={i: i for i in range(n)},
                 scratch=[pltpu.SemaphoreType.DMA((3 * n,)), pltpu.SemaphoreType.DMA((3 * n,))])(*gs)


def _scatter_send(name, ps):
    n = len(ps)

    def body(*refs):
        p, r, ssem, rsem, token = refs[:n], refs[n:2 * n], refs[2 * n], refs[2 * n + 1], refs[-1]
        x, y, c, chips = _place()
        for i in range(n):
            for k, (px, py) in enumerate(chips):
                _remote(p[i].at[2 * px + py], r[i].at[k], ssem.at[3 * i + k], rsem.at[3 * i + k], (px, py, c)).start()
        token[...] = jnp.zeros_like(token)

    lands = [lax.empty((N_CHIPS - 1,) + a.shape[1:], a.dtype) for a in ps]
    sem = pltpu.SemaphoreType.DMA((3 * n,))
    out = _split_copy_call(body, name=name, in_specs=[HBM] * (2 * n), out_specs=[SEM, SEM] + [HBM] * (2 * n) + [VMEM],
                           out_shape=[sem, sem] + [pltpu.HBM(a.shape, a.dtype) for a in list(ps) + lands] + [_sds((SUBLANES, LANES), F32)],
                           aliases={i: 2 + i for i in range(2 * n)})(*_hbm(list(ps) + lands))
    return out[0], out[1], out[2:2 + n], out[2 + n:2 + 2 * n], out[-1]


def _scatter_wait(name, ps, lands, send_sems, recv_sems, after):
    n = len(ps)

    def body(*refs):
        p, r, ssem, rsem = refs[:n], refs[n:2 * n], refs[2 * n], refs[2 * n + 1]
        x, y, c, chips = _place()
        for i in range(n):
            for k, (px, py) in enumerate(chips):
                cp = _remote(p[i].at[2 * px + py], r[i].at[k], ssem.at[3 * i + k], rsem.at[3 * i + k], (px, py, c))
                cp.wait_recv()
                cp.wait_send()

    out = _split_copy_call(body, name=name, in_specs=[HBM] * (2 * n) + [SEM, SEM] + [pl.BlockSpec(memory_space=pl.ANY)] * len(after),
                           out_specs=[HBM] * (2 * n), out_shape=[pltpu.HBM(a.shape, a.dtype) for a in list(ps) + list(lands)],
                           aliases={i: i for i in range(2 * n)})(*ps, *lands, send_sems, recv_sems, *after)
    return out[:n], out[n:]


def _exchange_send(name, gs):
    n = len(gs)

    def body(*refs):
        g, r, ssem, rsem, token = refs[:n], refs[n:2 * n], refs[2 * n], refs[2 * n + 1], refs[-1]
        x, y, c, _ = _place()
        for i in range(n):
            _remote(g[i].at[:, 1 - c], r[i], ssem.at[i], rsem.at[i], (x, y, 1 - c)).start()
        token[...] = jnp.zeros_like(token)

    lands = [lax.empty((a.shape[0],) + a.shape[2:], a.dtype) for a in gs]
    sem = pltpu.SemaphoreType.DMA((n,))
    out = _split_copy_call(body, name=name, in_specs=[HBM] * (2 * n), out_specs=[SEM, SEM] + [HBM] * (2 * n) + [VMEM],
                           out_shape=[sem, sem] + [pltpu.HBM(a.shape, a.dtype) for a in list(gs) + lands] + [_sds((SUBLANES, LANES), F32)],
                           aliases={i: 2 + i for i in range(2 * n)})(*_hbm(list(gs) + lands))
    return out[0], out[1], out[2:2 + n], out[2 + n:2 + 2 * n], out[-1]


def _exchange_wait(name, gs, lands, send_sems, recv_sems, after):
    n = len(gs)

    def body(*refs):
        g, r, ssem, rsem = refs[:n], refs[n:2 * n], refs[2 * n], refs[2 * n + 1]
        x, y, c, _ = _place()
        for i in range(n):
            cp = _remote(g[i].at[:, 1 - c], r[i], ssem.at[i], rsem.at[i], (x, y, 1 - c))
            cp.wait_recv()
            cp.wait_send()

    out = _split_copy_call(body, name=name, in_specs=[HBM] * (2 * n) + [SEM, SEM] + [pl.BlockSpec(memory_space=pl.ANY)] * len(after),
                           out_specs=[HBM] * (2 * n), out_shape=[pltpu.HBM(a.shape, a.dtype) for a in list(gs) + list(lands)],
                           aliases={i: i for i in range(2 * n)})(*gs, *lands, send_sems, recv_sems, *after)
    return out[:n], out[n:]


def _sibling_share(fs):
    n = len(fs)

    def body(*refs):
        f, send_sems, recv_sems = refs[n:2 * n], refs[-2], refs[-1]
        x, y, c, _ = _place()
        sends = [_remote(f[i].at[c], f[i].at[c], send_sems.at[i], recv_sems.at[i], (x, y, 1 - c)) for i in range(n)]
        for cp in sends:
            cp.start()
        for i in range(n):
            theirs = f[i].at[1 - c]
            _remote(theirs, theirs, send_sems.at[i], recv_sems.at[i], (x, y, 1 - c)).wait_recv()
        for cp in sends:
            cp.wait_send()

    return _call(body, name="grad_sibling_share", in_specs=[HBM] * n, out_specs=[HBM] * n,
                 out_shape=[_sds(a.shape, a.dtype) for a in fs], aliases={i: i for i in range(n)},
                 scratch=[pltpu.SemaphoreType.DMA((n,)), pltpu.SemaphoreType.DMA((n,))])(*fs)


def _all_reduce_small(name, v):
    rows = v.shape[0] // 2
    halves = (2, rows, LANES)

    def body(v_ref, o_ref, from_sibling, chip_sums, send_sems, recv_sems):
        x, y, c, chips = _place()
        me, sibling = 2 * x + y, (x, y, 1 - c)
        swap = _remote(v_ref.at[1 - c], from_sibling, send_sems.at[0], recv_sems.at[0], sibling)
        swap.start()
        swap.wait()
        chip_sums[me] = v_ref[c] + from_sibling[...]
        sends = [_remote(chip_sums.at[me], chip_sums.at[me], send_sems.at[1 + k], recv_sems.at[1 + k], (px, py, c))
                 for k, (px, py) in enumerate(chips)]
        for cp in sends:
            cp.start()
        for k, (px, py) in enumerate(chips):
            theirs = chip_sums.at[2 * px + py]
            _remote(theirs, theirs, send_sems.at[1 + k], recv_sems.at[1 + k], (px, py, c)).wait_recv()
        for cp in sends:
            cp.wait_send()
        acc = chip_sums[0]
        for j in range(1, N_CHIPS):
            acc = acc + chip_sums[j]
        o_ref[c] = acc
        share = _remote(o_ref.at[c], o_ref.at[c], send_sems.at[4], recv_sems.at[4], sibling)
        share.start()
        share.wait_send()
        _remote(o_ref.at[1 - c], o_ref.at[1 - c], send_sems.at[4], recv_sems.at[4], sibling).wait_recv()

    return _call(body, name=name, in_specs=[VMEM], out_specs=VMEM, out_shape=_sds(halves, F32),
                 scratch=[pltpu.VMEM((rows, LANES), F32), pltpu.VMEM((N_CHIPS, rows, LANES), F32),
                          pltpu.SemaphoreType.DMA((5,)), pltpu.SemaphoreType.DMA((5,))])(v.reshape(halves)).reshape(v.shape)


def _add_halves(name, g, r, c):
    _, _, rows, C = g.shape
    tr = _row_tile(rows)

    def body(c_ref, g_ref, r_ref, o_ref):
        o_ref[...] = (g_ref[...] + r_ref[...]).astype(BF16)

    spec = BS((None, tr, C), lambda j, i, c_ref: (j, i, 0))
    return _prefetch_call(body, name=name, grid=(N_CHIPS, rows // tr),
                          in_specs=[BS((None, None, tr, C), lambda j, i, c_ref: (j, c_ref[0], i, 0)), spec], out_specs=spec,
                          out_shape=pltpu.HBM((N_CHIPS, rows, C), BF16))(c, g, r)


def _sum_partials(name, p, r, chip_c):
    _, rows, C = p.shape
    tr = _row_tile(rows)

    def body(s_ref, p_ref, r_ref, o_ref):
        acc = p_ref[...].astype(F32)
        for k in range(N_CHIPS - 1):
            acc = acc + r_ref[k].astype(F32)
        o_ref[...] = acc

    return _prefetch_call(body, name=name, grid=(rows // tr,),
                          in_specs=[BS((None, tr, C), lambda i, s: (s[0], i, 0)), BS((N_CHIPS - 1, tr, C), lambda i, s: (0, i, 0))],
                          out_specs=BS((None, tr, C), lambda i, s: (s[1], i, 0)), out_shape=pltpu.HBM((2, rows, C), F32))(chip_c, p, r)


_SHARDED = ("even_w_in", "even_w_out", "odd_w_in", "q_b", "kv_b", "odd_w_out", "ffn_w_gate", "ffn_w_up", "ffn_w_down")
_REPLICATED = ("mix_norm", "ffn_norm", "sg_ln_g", "sg_w_s", "sg_b_s", "pool_w", "q_norm", "k_norm")
_SMALL_SHARDED = ("sc_conv_w", "pool_scale", "q_a_norm", "kv_a_norm")
_WEIGHTS = ("mix_norm", "ffn_norm", "even_w_in", "sg_ln_g", "sg_w_s", "sg_b_s", "sc_conv_w", "even_w_out", "odd_w_in", "pool_w",
            "pool_scale", "q_a_norm", "q_b", "kv_a_norm", "kv_b", "q_norm", "k_norm", "odd_w_out", "ffn_w_gate", "ffn_w_up",
            "ffn_w_down")


def _pad_rows(flat, width, align):
    n = flat.shape[0]
    rows = -(-n // (width * align)) * align
    return jnp.pad(flat, (0, rows * width - n)).reshape(rows, width)


_GROUPS = {"even": ("even_w_in", "even_w_out"),
           "ffn0": ("ffn_w_gate0", "ffn_w_up0", "ffn_w_down0"),
           "odd": ("odd_w_in", "q_b", "kv_b", "odd_w_out"),
           "ffn1": ("ffn_w_gate1", "ffn_w_up1", "ffn_w_down1")}


def _place_shards(shards, chip):
    placed = {}
    for n in _SHARDED:
        a = shards[n]
        halves = a.reshape(a.shape[0], 2, a.shape[1] // 2, a.shape[2])
        if a.shape[0] == 1:
            placed[n] = _cast_place(f"place_{n}", halves, 0, chip)
        else:
            for l in range(a.shape[0]):
                placed[f"{n}{l}"] = _cast_place(f"place_{n}{l}", halves, l, chip)
    return placed


def _whole_weights(gathered):
    out = {n: a.reshape(N_CHIPS, -1, a.shape[-1]) for n, a in gathered.items()}
    for n in ("q_b", "kv_b"):
        if n in out:
            out[n] = out[n].transpose(1, 0, 2).reshape(out[n].shape[1], -1)
    for n in ("even_w_out", "odd_w_in", "odd_w_out"):
        if n in out:
            out[n] = out[n].reshape(-1, out[n].shape[-1])
    return out


def _forward_backward(x, positions, target, small, fetch, emit, advance):
    batch, seq, _ = x.shape
    T = batch * seq
    tm = _token_tile(seq)
    x0 = x.reshape(T, D_MODEL)

    inv_freq = ROPE_THETA ** (-jnp.arange(0, QK_ROPE, 2, dtype=F32) / QK_ROPE)
    ang = (positions.astype(F32)[..., None] * inv_freq).reshape(T, QK_ROPE // 2)
    cos, sin = jnp.cos(ang), jnp.sin(ang)
    pad = jnp.zeros((T, LANES - QK_ROPE), F32)
    cos_t = jnp.concatenate([cos, cos, pad], axis=1)
    sin_t = jnp.concatenate([-sin, sin, pad], axis=1)

    tril = jnp.tril(jnp.ones((SG_CHUNK, SG_CHUNK), bool))
    w_tril = jnp.where(tril[None], small["sg_w_s"][0], 0.0).astype(BF16)
    b_lanes = jnp.broadcast_to(small["sg_b_s"][0][:, :, None], (SG_HEADS, SG_CHUNK, SG_DIM))
    conv_w = jnp.pad(small["sc_conv_w"][0], ((0, SUBLANES - CONV_TAPS), (0, 0)))
    ln_g = small["sg_ln_g"]
    pool_diag = jnp.zeros((POOL_WIDTH, POOL_WIDTH), F32)
    for g in range(len(POOL_WINDOWS)):
        pool_diag = pool_diag.at[POOL_DIM * g:POOL_DIM * (g + 1), POOL_DIM * g:POOL_DIM * (g + 1)].set(small["pool_w"][0, g])
    pool_diag = pool_diag.astype(BF16)
    pool_scale = small["pool_scale"]
    q_g = jnp.pad(small["q_norm"], ((0, 0), (0, QK_PAD - QK_DIM)))
    k_g = jnp.pad(small["k_norm"], ((0, 0), (0, QK_PAD - QK_DIM)))
    qa_g, kva_g = small["q_a_norm"], small["kv_a_norm"]
    in_shard = EVEN_IN // N_CHIPS

    def ffn_weights(l, w):
        return w[f"ffn_w_gate{l}"], w[f"ffn_w_up{l}"], w[f"ffn_w_down{l}"]

    W = fetch("even", ())
    w_in_even = W["even_w_in"]
    h0 = _rmsnorm_fwd("mix0_norm", x0, small["mix_norm"][0], tm)
    tb = _big_tile(T)
    proj0 = _matmul("even_in", "nn", [(h0, w_in_even)],
                    [(BS((tb, D_MODEL), lambda i, j, k: (i, 0)), BS((None, D_MODEL, in_shard), lambda i, j, k: (j, 0, 0)))],
                    (T // tb, N_CHIPS, 1), _sds((T, EVEN_IN), F32), BS((tb, in_shard), lambda i, j, k: (i, j)), (tb, in_shard))
    mix0 = _even_mixer_fwd(proj0, ln_g, w_tril, b_lanes, conv_w, seq, tm)
    w_out_even = W["even_w_out"]
    x1, h1 = _mm("even_out", "nn", mix0, w_out_even, F32, tk=1024, add=x0, fused=_norm_tail(small["ffn_norm"][0], T, tb))
    ffn0 = ffn_weights(0, fetch("ffn0", (x1,)))
    (x2, h2), ffn0_saved = _ffn_fwd(0, x1, h1, *ffn0, _norm_tail(small["mix_norm"][1], T, tb))
    W = fetch("odd", (x2,))
    w_in_odd = jnp.pad(W["odd_w_in"], ((0, 0), (0, ODD_IN_PAD - ODD_IN)))
    q_b = jnp.pad(W["q_b"].reshape(Q_LORA, HEADS, QK_DIM).transpose(1, 0, 2), ((0, 0), (0, 0), (0, QK_PAD - QK_DIM)))
    kv_b = W["kv_b"].reshape(KV_LORA, HEADS, QK_NOPE + V_DIM).transpose(1, 0, 2)
    proj1 = _mm("odd_in", "nn", h2, w_in_odd, F32, tk=1024)
    mix1 = _pool_fwd(proj1, pool_diag, pool_scale, seq, tm)
    q, k, v = _mla_qkv_fwd(proj1, cos_t, sin_t, qa_g, kva_g, q_b, kv_b, q_g, k_g, tm)
    mix1, lse = _flash_fwd(q, k, v, mix1, batch, seq)
    x3, h3 = _mm("odd_out", "nn", mix1, W["odd_w_out"], F32, tk=1024, add=x2, fused=_norm_tail(small["ffn_norm"][1], T, tb))
    ffn1 = ffn_weights(1, fetch("ffn1", (x3,)))
    (dy, sq), ffn1_saved = _ffn_fwd(1, x3, h3, *ffn1, _loss_tail(target.reshape(T, D_MODEL), tb))

    G = {}
    dx3, dffn_g1 = _ffn_bwd(1, x3, small["ffn_norm"][1], *ffn1, ffn1_saved, dy, emit)
    dmix1 = _mm("odd_out_dx", "nt", dx3, W["odd_w_out"], BF16, tk=1024, after=advance(dx3))
    dw_out_odd = _mm("odd_out_dw", "tn", mix1, dx3, F32, hbm_out=True)
    dq, dk, dv = _flash_bwd(q, k, v, dmix1, mix1, lse, batch, seq)
    dz_pool, dpool_diag, G["pool_scale"] = _pool_bwd(proj1, dmix1, pool_diag, pool_scale, seq, tm)
    dproj1, dq_b, dkv_b, dq_g, dk_g, G["q_a_norm"], G["kv_a_norm"] = _mla_qkv_bwd(
        proj1, cos_t, sin_t, qa_g, kva_g, q_b, kv_b, q_g, k_g, dq, dk, dv, dz_pool, tm)
    G["pool_w"] = jnp.stack([dpool_diag[POOL_DIM * g:POOL_DIM * (g + 1), POOL_DIM * g:POOL_DIM * (g + 1)]
                             for g in range(len(POOL_WINDOWS))])[None]
    G["q_norm"], G["k_norm"] = dq_g[:, :QK_DIM], dk_g[:, :QK_DIM]
    dw_in_odd = _mm("odd_in_dw", "tn", h2, dproj1, F32, tn=ODD_IN, hbm_out=True)

    def shard_major(g, cols):
        return g.reshape(g.shape[0], N_CHIPS, cols).transpose(1, 0, 2)

    behind = emit("odd", {"odd_w_in": dw_in_odd.reshape(N_CHIPS, -1, ODD_IN),
                          "q_b": shard_major(dq_b[:, :, :QK_DIM].transpose(1, 0, 2).reshape(Q_LORA, HEADS * QK_DIM), HEADS * QK_DIM // N_CHIPS),
                          "kv_b": shard_major(dkv_b.transpose(1, 0, 2).reshape(KV_LORA, HEADS * (QK_NOPE + V_DIM)),
                                              HEADS * (QK_NOPE + V_DIM) // N_CHIPS),
                          "odd_w_out": dw_out_odd.reshape(N_CHIPS, -1, D_MODEL)})
    dx2, dmix_g1 = _mm("odd_in_dx", "nt", dproj1, W["odd_w_in"], F32, tk=ODD_IN, after=behind,
                       fused=_norm_bwd_tail(x2, small["mix_norm"][1], dx3, tb))
    dx1, dffn_g0 = _ffn_bwd(0, x1, small["ffn_norm"][0], *ffn0, ffn0_saved, dx2, emit, after=advance(dx2))
    dmix0 = _mm("even_out_dx", "nt", dx1, w_out_even, F32, tk=1024, after=advance(dx1))
    dw_out_even = _mm("even_out_dw", "tn", mix0, dx1, F32, hbm_out=True)
    dproj0, dw_s, db_lanes, G["sg_ln_g"], dconv = _even_mixer_bwd(proj0, dmix0, ln_g, w_tril, b_lanes, conv_w, seq, tm)
    G["sg_w_s"] = dw_s[None]
    G["sg_b_s"] = jnp.sum(db_lanes, axis=-1)[None]
    G["sc_conv_w"] = dconv[None, :CONV_TAPS]
    tail, shapes, specs = _norm_bwd_tail(x0, small["mix_norm"][0], dx1, tb)
    dx0, dmix_g0 = _matmul("even_in_dx", "nt", [(dproj0, w_in_even)],
                           [(BS((tb, in_shard), lambda i, j, k: (i, k)), BS((None, D_MODEL, in_shard), lambda i, j, k: (k, 0, 0)))],
                           (T // tb, 1, N_CHIPS), shapes, specs, (tb, D_MODEL), tail=tail)
    tk = min(512, T)
    dw_in_even = _grad_shards(
        "even_in_dw", h0, dproj0, BS((tk, D_MODEL), lambda k: (k, 0)), BS((tk, EVEN_IN), lambda k: (k, 0)),
        lambda a_ref, b_ref, j: (a_ref[...], b_ref[:, in_shard * j:in_shard * (j + 1)]), (N_CHIPS, D_MODEL, in_shard), T // tk)
    emit("even", {"even_w_in": dw_in_even, "even_w_out": dw_out_even.reshape(N_CHIPS, -1, D_MODEL)})
    advance(dx0)
    G["mix_norm"] = jnp.concatenate([dmix_g0, dmix_g1], axis=0)
    G["ffn_norm"] = jnp.concatenate([dffn_g0, dffn_g1], axis=0)
    return sq[0, 0], dx0.reshape(batch, seq, D_MODEL), G


def _small_vector(parts, names):
    flat = jnp.concatenate([parts[n].astype(F32).reshape(-1) for n in names])
    return _pad_rows(flat, LANES, 2 * SUBLANES)


def _split_small(vec, like, names):
    out, off, flat = {}, 0, vec.reshape(-1)
    for n in names:
        size = math.prod(like[n].shape)
        out[n] = flat[off:off + size].reshape(like[n].shape)
        off += size
    return out


def _whole_shape(a):
    return a.shape[:-1] + (a.shape[-1] * N_CHIPS,)


def kernel(x, positions, mix_norm, ffn_norm, even_w_in, sg_ln_g, sg_w_s, sg_b_s, sc_conv_w, even_w_out, odd_w_in, pool_w, pool_scale, q_a_norm, q_b, kv_a_norm, kv_b, q_norm, k_norm, odd_w_out, ffn_w_gate, ffn_w_up, ffn_w_down, loss_target, m_mix_norm, m_ffn_norm, m_even_w_in, m_sg_ln_g, m_sg_w_s, m_sg_b_s, m_sc_conv_w, m_even_w_out, m_odd_w_in, m_pool_w, m_pool_scale, m_q_a_norm, m_q_b, m_kv_a_norm, m_kv_b, m_q_norm, m_k_norm, m_odd_w_out, m_ffn_w_gate, m_ffn_w_up, m_ffn_w_down, v_mix_norm, v_ffn_norm, v_even_w_in, v_sg_ln_g, v_sg_w_s, v_sg_b_s, v_sc_conv_w, v_even_w_out, v_odd_w_in, v_pool_w, v_pool_scale, v_q_a_norm, v_q_b, v_kv_a_norm, v_kv_b, v_q_norm, v_k_norm, v_odd_w_out, v_ffn_w_gate, v_ffn_w_up, v_ffn_w_down):
    args = dict(locals())
    w = {n: args[n] for n in _WEIGHTS}
    m = {n: args["m_" + n] for n in _WEIGHTS}
    v = {n: args["v_" + n] for n in _WEIGHTS}
    cx, cy, cc = lax.axis_index("x"), lax.axis_index("y"), lax.axis_index("c")
    chip = 2 * cx + cy
    transposed = ("ffn_w_gate", "ffn_w_up")
    for n in transposed:
        w[n], m[n], v[n] = (jnp.swapaxes(t[n], 1, 2) for t in (w, m, v))

    chip_arr = chip.astype(jnp.int32).reshape(1)
    c_arr = cc.astype(jnp.int32).reshape(1)
    group_names = list(_GROUPS)
    order = [n for g in group_names for n in _GROUPS[g]]
    placed = {}
    for n in _SMALL_SHARDED:
        a = w[n]
        whole = jnp.zeros(a.shape[:-1] + (N_CHIPS, a.shape[-1]), F32)
        whole = lax.dynamic_update_slice_in_dim(whole, a[..., None, :], chip, axis=a.ndim - 1)
        placed[n] = jnp.where(cc == 0, whole, 0.0).reshape(_whole_shape(a))
    small_whole = _all_reduce_small("gather_small_weights", _small_vector(placed, _SMALL_SHARDED))
    small = dict({n: w[n] for n in _REPLICATED}, **_split_small(small_whole, placed, _SMALL_SHARDED))

    shards = _place_shards(w, chip_arr)
    sems, in_flight = _gather_send([shards[n] for n in order], [[order.index(n) for n in _GROUPS[g]] for g in group_names],
                                   (small_whole,))
    in_flight = dict(zip(order, in_flight))

    def fetch(group, after):
        gi, members = group_names.index(group), _GROUPS[group]
        landed = _gather_wait(f"gather_wait_{group}", [in_flight[n] for n in members], sems[2 * gi], sems[2 * gi + 1], after)
        return _whole_weights(dict(zip(members, _gather_pass(f"gather_pass_{group}", landed))))

    swapping, pending, arrived = [], [], {}

    def settle(after):
        names, ps, lands, send_sems, recv_sems = pending.pop()
        ps, lands = _scatter_wait(f"scatter_wait_{names[0]}", ps, lands, send_sems, recv_sems, after)
        arrived.update({n: (p, r) for n, p, r in zip(names, ps, lands)})

    def emit(group, grads):
        names = _GROUPS[group]
        halves = [grads[n].reshape(N_CHIPS, 2, grads[n].shape[1] // 2, grads[n].shape[2]) for n in names]
        send_sems, recv_sems, halves, lands, token = _exchange_send(f"exchange_send_{group}", halves)
        swapping.append((group, halves, lands, send_sems, recv_sems))
        return (token,)

    def advance(done):
        group, halves, lands, send_sems, recv_sems = swapping.pop()
        names = _GROUPS[group]
        halves, lands = _exchange_wait(f"exchange_wait_{group}", halves, lands, send_sems, recv_sems, (done,))
        partial = [_add_halves(f"add_{n}", g, r, c_arr) for n, g, r in zip(names, halves, lands)]
        if pending:
            settle((done,))
        send_sems, recv_sems, ps, lands, token = _scatter_send(f"scatter_send_{group}", partial)
        pending.append((names, ps, lands, send_sems, recv_sems))
        return (token,)

    sq, grad_x, G = _forward_backward(x, positions, loss_target, small, fetch, emit, advance)
    loss = lax.psum(0.5 * sq / D_MODEL, ("x", "y", "c"))

    small_names = _REPLICATED + _SMALL_SHARDED
    summed = _split_small(_all_reduce_small("reduce_small_grads", _small_vector(G, small_names)), G, small_names)
    grads = {n: summed[n] for n in _REPLICATED}
    for n in _SMALL_SHARDED:
        a = w[n]
        grads[n] = lax.dynamic_slice_in_dim(summed[n].reshape(a.shape[:-1] + (N_CHIPS, a.shape[-1])), chip, 1,
                                            axis=a.ndim - 1).reshape(a.shape)

    settle(())
    chip_c = jnp.stack([chip, cc]).astype(jnp.int32)
    sums = [_sum_partials(f"sum_{n}", *arrived[n], chip_c) for n in order]
    shard_grad = {n: f.reshape(1, -1, f.shape[-1]) for n, f in zip(order, _sibling_share(sums))}

    out = {}
    for n in ("even_w_in", "even_w_out", "odd_w_in", "q_b", "kv_b", "odd_w_out"):
        out[n] = _adamw(f"adamw_{n}", w[n], [shard_grad[n][0]], m[n], v[n])
    for n in ("ffn_w_gate", "ffn_w_up", "ffn_w_down"):
        out[n] = _adamw(f"adamw_{n}", w[n], [shard_grad[f"{n}{l}"][0] for l in range(2)], m[n], v[n])
    packed = [_small_vector(d, small_names) for d in (w, grads, m, v)]
    res = _adamw("adamw_small", packed[0][None], [packed[1]], packed[2][None], packed[3][None])
    delta_s, m_s, v_s = (_split_small(r, w, small_names) for r in res[1:])
    for n in small_names:
        out[n] = (grads[n], delta_s[n], m_s[n], v_s[n])
    for n in transposed:
        out[n] = tuple(jnp.swapaxes(t, 1, 2) for t in out[n])

    return (loss, grad_x, *[out[n][0] for n in _WEIGHTS], *[out[n][1] for n in _WEIGHTS],
            *[out[n][2] for n in _WEIGHTS], *[out[n][3] for n in _WEIGHTS])
```

```python
import functools
import math

import jax
import jax.numpy as jnp
from jax import lax
from jax.experimental import pallas as pl
from jax.experimental.pallas import tpu as pltpu

F32, BF16 = jnp.float32, jnp.bfloat16
BS = pl.BlockSpec

D_MODEL = 1024
EPS = 1e-6
NEG_INF = -1e30
SG_HEADS, SG_DIM, SG_WIDTH, SG_CHUNK = 4, 128, 512, 128
SC_WIDTH, CONV_TAPS = 512, 3
EVEN_IN = 2 * SG_WIDTH + 3 * SC_WIDTH
POOL_WINDOWS = (2, 4, 8, 16)
POOL_DIM, POOL_WIDTH = 64, 256
POOL_HALO = 16
HEADS, Q_LORA, KV_LORA, QK_NOPE, QK_ROPE, V_DIM = 6, 384, 256, 128, 64, 128
QK_DIM = QK_NOPE + QK_ROPE
QK_PAD = 256
ODD_IN = POOL_WIDTH + Q_LORA + KV_LORA + QK_ROPE
ODD_IN_PAD = 1024
ROPE_THETA = 10000.0
ATTN_SCALE = QK_DIM ** -0.5
D_FF, N_CHIPS = 2816, 4
FF_SHARD = D_FF // N_CHIPS
ADAM_LR, ADAM_B1, ADAM_B2, ADAM_EPS, ADAM_WD, ADAM_STEP = 0.001, 0.9, 0.999, 1e-08, 0.01, 10
VMEM_LIMIT_V7X = 48 * 2**20
LANES, SUBLANES = 128, 8
MESH = pl.DeviceIdType.MESH
HBM = pl.BlockSpec(memory_space=pltpu.HBM)
VMEM = pl.BlockSpec(memory_space=pltpu.VMEM)

_DIMS = {"nn": (((1,), (0,)), ((), ())), "nt": (((1,), (1,)), ((), ())), "tn": (((0,), (0,)), ((), ()))}


def _dot(a, b, mode="nn"):
    return lax.dot_general(a.astype(BF16), b.astype(BF16), _DIMS[mode], preferred_element_type=F32)


def _in_hbm(shape):
    if isinstance(shape, (list, tuple)):
        return [_in_hbm(s) for s in shape]
    return pltpu.HBM(tuple(shape.shape), shape.dtype) if isinstance(shape, jax.ShapeDtypeStruct) else shape


def _call(body, *, name, out_shape, in_specs, out_specs, grid=(), scratch=(), aliases=None, after=(), in_vmem=False):
    params = pltpu.CompilerParams(vmem_limit_bytes=VMEM_LIMIT_V7X,
                                  **({"dimension_semantics": ("arbitrary",) * len(grid)} if grid else {}))
    n_in, n_after = len(in_specs), len(after)
    kernel_body = body if not after else (lambda *refs: body(*refs[:n_in], *refs[n_in + n_after:]))
    call = pl.pallas_call(kernel_body, name=name, grid=grid, in_specs=list(in_specs) + [pl.BlockSpec(memory_space=pl.ANY)] * n_after,
                          out_specs=out_specs, out_shape=out_shape if in_vmem else _in_hbm(out_shape), scratch_shapes=list(scratch),
                          input_output_aliases=aliases or {}, compiler_params=params)
    if in_vmem:
        return call
    return lambda *ops: call(*[pltpu.with_memory_space_constraint(o, pltpu.HBM) for o in ops], *after)


def _sds(shape, dtype):
    return jax.ShapeDtypeStruct(tuple(shape), dtype)


def _token_tile(seq):
    return 512 if seq % 512 == 0 else seq


_TAIL_ROWS = 256


def _matmul(name, mode, pairs, pair_specs, grid, out_shape, out_spec, acc_shape, add=None, add_spec=None, after=(), tail=None):
    n, nk = len(pairs), grid[-1]
    n_add = int(add is not None)
    n_tail = len(tail[0]) if tail else 0
    n_in = 2 * n + n_add + n_tail
    n_out = len(out_shape) if tail else 1

    def body(*refs):
        ab = refs[:2 * n]
        add_ref = refs[2 * n] if n_add else None
        tail_refs, outs = refs[2 * n + n_add:n_in], refs[n_in:n_in + n_out]
        first = pl.program_id(0) == 0

        def finish(result):
            if tail is None:
                r = result(slice(None))
                outs[0][...] = (r if add_ref is None else r + add_ref[...]).astype(outs[0].dtype)
                return
            for lo in range(0, acc_shape[0], _TAIL_ROWS):
                rows = slice(lo, min(lo + _TAIL_ROWS, acc_shape[0]))
                r = result(rows)
                tail[2](rows, r if add_ref is None else r + add_ref[rows, :], first, tail_refs, outs)

        if nk == 1:
            r = _dot(ab[0][...], ab[1][...], mode)
            for p in range(1, n):
                r = r + _dot(ab[2 * p][...], ab[2 * p + 1][...], mode)
            finish(lambda rows: r[rows])
            return
        acc = refs[-1]
        k = pl.program_id(len(grid) - 1)

        @pl.when(k == 0)
        def _():
            acc[...] = jnp.zeros_like(acc)

        for p in range(n):
            acc[...] += _dot(ab[2 * p][...], ab[2 * p + 1][...], mode)

        @pl.when(k == nk - 1)
        def _():
            finish(lambda rows: acc[rows, :])

    ops = [t for pr in pairs for t in pr] + ([add] if n_add else []) + (list(tail[0]) if tail else [])
    specs = [s for pr in pair_specs for s in pr] + ([add_spec] if n_add else []) + (list(tail[1]) if tail else [])
    return _call(body, name=name, grid=grid, in_specs=specs, out_specs=out_spec, out_shape=out_shape,
                 scratch=[pltpu.VMEM(acc_shape, F32)] if nk > 1 else [], after=after)(*ops)


def _row_spec(tm, d):
    return BS((tm, d), lambda i, j, k: (i, 0))


def _vec_spec(d):
    return BS((1, d), lambda i, j, k: (0, 0))


def _norm_tail(gain, T, tm):
    d = gain.shape[-1]

    def fn(rows, r, first, tail_refs, outs):
        outs[0][rows, :] = r
        outs[1][rows, :] = (r * lax.rsqrt(jnp.mean(r * r, axis=-1, keepdims=True) + EPS) * tail_refs[0][...]).astype(BF16)

    return ([gain.reshape(1, d)], [_vec_spec(d)], fn), [_sds((T, d), F32), _sds((T, d), BF16)], [_row_spec(tm, d), _row_spec(tm, d)]


def _norm_bwd_tail(x, gain, dres, tm):
    T, d = x.shape

    def fn(rows, r, first, tail_refs, outs):
        x_ref, g_ref, dres_ref = tail_refs
        xv = x_ref[rows, :]
        rstd = lax.rsqrt(jnp.mean(xv * xv, axis=-1, keepdims=True) + EPS)
        xhat = xv * rstd
        if rows.start == 0:
            @pl.when(first)
            def _():
                outs[1][...] = jnp.zeros_like(outs[1])

        outs[1][...] += jnp.sum(r * xhat, axis=0, keepdims=True)
        dxhat = r * g_ref[...]
        outs[0][rows, :] = dres_ref[rows, :] + rstd * (dxhat - xhat * jnp.mean(dxhat * xhat, axis=-1, keepdims=True))

    return (([x, gain.reshape(1, d), dres], [_row_spec(tm, d), _vec_spec(d), _row_spec(tm, d)], fn),
            [_sds((T, d), F32), _sds((1, d), F32)], [_row_spec(tm, d), _vec_spec(d)])


def _loss_tail(target, tm):
    T, d = target.shape

    def fn(rows, r, first, tail_refs, outs):
        e = r - tail_refs[0][rows, :]
        if rows.start == 0:
            @pl.when(first)
            def _():
                outs[1][...] = jnp.zeros_like(outs[1])

        outs[1][...] += jnp.sum(e * e)
        outs[0][rows, :] = e * (1.0 / d)

    return (([target], [_row_spec(tm, d)], fn), [_sds((T, d), F32), _sds((SUBLANES, LANES), F32)],
            [_row_spec(tm, d), BS((SUBLANES, LANES), lambda i, j, k: (0, 0))])


def _grad_shards(name, a, b, a_spec, b_spec, pick, out_shape, n_steps):
    def body(a_ref, b_ref, o_ref):
        @pl.when(pl.program_id(0) == 0)
        def _():
            o_ref[...] = jnp.zeros_like(o_ref)

        for j in range(N_CHIPS):
            aj, bj = pick(a_ref, b_ref, j)
            o_ref[j] += _dot(aj, bj, "tn")

    return _call(body, name=name, grid=(n_steps,), in_specs=[a_spec, b_spec],
                 out_specs=BS(out_shape, lambda k: (0, 0, 0)), out_shape=pltpu.HBM(tuple(out_shape), F32))(a, b)


def _mm(name, mode, a, b, out_dtype, tm=1024, tn=1024, tk=512, add=None, after=(), fused=None, hbm_out=False):
    if mode == "tn":
        (K, M), N = a.shape, b.shape[1]
    else:
        (M, K), N = a.shape, (b.shape[1] if mode == "nn" else b.shape[0])
    tm, tn, tk = min(tm, M), min(tn, N), min(tk, K)
    a_spec = BS((tk, tm), lambda i, j, k: (k, i)) if mode == "tn" else BS((tm, tk), lambda i, j, k: (i, k))
    b_spec = BS((tn, tk), lambda i, j, k: (j, k)) if mode == "nt" else BS((tk, tn), lambda i, j, k: (k, j))
    o_spec = BS((tm, tn), lambda i, j, k: (i, j))
    tail, shapes, specs = fused if fused else (None, pltpu.HBM((M, N), out_dtype) if hbm_out else _sds((M, N), out_dtype), o_spec)
    return _matmul(name, mode, [(a, b)], [(a_spec, b_spec)], (M // tm, N // tn, K // tk), shapes, specs, (tm, tn),
                   add=add, add_spec=o_spec if add is not None else None, after=after, tail=tail)


def _rmsnorm_fwd(name, x, g, tm):
    T, d = x.shape

    def body(x_ref, g_ref, o_ref):
        xv = x_ref[...]
        y = xv * lax.rsqrt(jnp.mean(xv * xv, axis=-1, keepdims=True) + EPS)
        o_ref[...] = (y * g_ref[...]).astype(o_ref.dtype)

    return _call(body, name=name, grid=(T // tm,), in_specs=[BS((tm, d), lambda i: (i, 0)), BS((1, d), lambda i: (0, 0))],
                 out_specs=BS((tm, d), lambda i: (i, 0)), out_shape=_sds((T, d), BF16))(x, g.reshape(1, d))


def _ffn_up(name, h, wg, wu, tm):
    T = h.shape[0]

    def body(h_ref, wg_ref, wu_ref, g_ref, u_ref, a_ref):
        hv = h_ref[...]
        g = _dot(hv, wg_ref[...], "nt")
        u = _dot(hv, wu_ref[...], "nt")
        g_ref[...] = g.astype(BF16)
        u_ref[...] = u.astype(BF16)
        a_ref[...] = (g * (1.0 / (1.0 + jnp.exp(-g))) * u).astype(BF16)

    w_spec = BS((None, FF_SHARD, D_MODEL), lambda j, i: (j, 0, 0))
    o_spec = BS((None, tm, FF_SHARD), lambda j, i: (j, i, 0))
    sh = _sds((N_CHIPS, T, FF_SHARD), BF16)
    return _call(body, name=name, grid=(N_CHIPS, T // tm), in_specs=[BS((tm, D_MODEL), lambda j, i: (i, 0)), w_spec, w_spec],
                 out_specs=[o_spec, o_spec, o_spec], out_shape=[sh, sh, sh])(h, wg, wu)


def _ffn_act_bwd(name, dxo, wd, g, u, tm, after=()):
    T = dxo.shape[0]

    def body(dx_ref, wd_ref, g_ref, u_ref, dg_ref, du_ref):
        da = _dot(dx_ref[...], wd_ref[...], "nt")
        g = g_ref[...].astype(F32)
        sig = 1.0 / (1.0 + jnp.exp(-g))
        dg_ref[...] = (da * u_ref[...].astype(F32) * (sig * (1.0 + g * (1.0 - sig)))).astype(BF16)
        du_ref[...] = (da * (g * sig)).astype(BF16)

    t_spec = BS((None, tm, FF_SHARD), lambda i, j: (j, i, 0))
    sh = _sds((N_CHIPS, T, FF_SHARD), BF16)
    return _call(body, name=name, grid=(T // tm, N_CHIPS),
                 in_specs=[BS((tm, D_MODEL), lambda i, j: (i, 0)), BS((None, FF_SHARD, D_MODEL), lambda i, j: (j, 0, 0)), t_spec, t_spec],
                 out_specs=[t_spec, t_spec], out_shape=[sh, sh], after=after)(dxo, wd, g, u)


def _big_tile(n):
    return min(1024, n)


def _ffn_fwd(l, x, h, wg, wu, wd, fused):
    T = x.shape[0]
    tm = _big_tile(T)
    g, u, a = _ffn_up(f"ffn{l}_up", h, wg, wu, tm)
    tn = D_MODEL
    tail, shapes, specs = fused
    outs = _matmul(f"ffn{l}_down", "nn", [(a, wd)],
                   [(BS((None, tm, FF_SHARD), lambda i, j, k: (k, i, 0)), BS((None, FF_SHARD, tn), lambda i, j, k: (k, 0, j)))],
                   (T // tm, D_MODEL // tn, N_CHIPS), shapes, specs, (tm, tn),
                   add=x, add_spec=BS((tm, tn), lambda i, j, k: (i, j)), tail=tail)
    return outs, (h, g, u, a)


def _ffn_bwd(l, x, gain, wg, wu, wd, saved, dxo, emit, after=()):
    h, g, u, a = saved
    T = x.shape[0]
    tm = _big_tile(T)
    dg, du = _ffn_act_bwd(f"ffn{l}_act_bwd", dxo, wd, g, u, tm, after=after)
    tk = min(512, T)
    tn = D_MODEL
    shards_spec = BS((N_CHIPS, tk, FF_SHARD), lambda k: (0, k, 0))
    rows_spec = BS((tk, D_MODEL), lambda k: (k, 0))

    def dw(nm, act, rows):
        return _grad_shards(nm, act, rows, shards_spec, rows_spec, lambda a_ref, b_ref, j: (a_ref[j], b_ref[...]),
                            (N_CHIPS, FF_SHARD, D_MODEL), T // tk)

    behind = emit(f"ffn{l}", {f"ffn_w_gate{l}": dw(f"ffn{l}_dwg", dg, h), f"ffn_w_up{l}": dw(f"ffn{l}_dwu", du, h),
                              f"ffn_w_down{l}": dw(f"ffn{l}_dwd", a, dxo)})
    act_spec = BS((None, tm, FF_SHARD), lambda i, j, k: (k, i, 0))
    w_spec = BS((None, FF_SHARD, tn), lambda i, j, k: (k, 0, j))
    tail, shapes, specs = _norm_bwd_tail(x, gain, dxo, tm)
    return _matmul(f"ffn{l}_dh", "nn", [(dg, wg), (du, wu)], [(act_spec, w_spec), (act_spec, w_spec)],
                   (T // tm, D_MODEL // tn, N_CHIPS), shapes, specs, (tm, tn), after=behind, tail=tail)


_INV_SQRT2 = 1.0 / math.sqrt(2.0)
_INV_SQRT_2PI = 1.0 / math.sqrt(2.0 * math.pi)


def _gelu(x):
    return 0.5 * x * (1.0 + lax.erf(x * _INV_SQRT2))


def _gelu_grad(x):
    return 0.5 * (1.0 + lax.erf(x * _INV_SQRT2)) + x * jnp.exp(-0.5 * x * x) * _INV_SQRT_2PI


def _shift_down(x, k):
    return pltpu.roll(x, k, 0)


def _shift_up(x, k):
    return pltpu.roll(x, x.shape[0] - k, 0)


def _layer_norm_head(xh):
    xc = xh - jnp.mean(xh, axis=-1, keepdims=True)
    rstd = lax.rsqrt(jnp.mean(xc * xc, axis=-1, keepdims=True) + EPS)
    return xc * rstd, rstd


def _even_halo_specs(tm, n_tiles, col_blocks, after):
    rows = tm // SUBLANES
    last = n_tiles * rows - 1
    if after:
        return [BS((SUBLANES, 512), functools.partial(lambda cb, i: (jnp.minimum((i + 1) * rows, last), cb), cb)) for cb in col_blocks]
    return [BS((SUBLANES, 512), functools.partial(lambda cb, i: (jnp.maximum(i * rows - 1, 0), cb), cb)) for cb in col_blocks]


def _even_mixer_fwd(proj, ln_g, w_tril, b_lanes, conv_w, seq, tm):
    T = proj.shape[0]
    tiles_per_seq = seq // tm

    def body(p_ref, hc_ref, hh_ref, lng_ref, w_ref, bb_ref, cw_ref, o_ref):
        first = pl.program_id(0) % tiles_per_seq == 0
        for h in range(SG_HEADS):
            cols = slice(SG_DIM * h, SG_DIM * (h + 1))
            vhat, _ = _layer_norm_head(_gelu(p_ref[:, SG_WIDTH + SG_DIM * h:SG_WIDTH + SG_DIM * (h + 1)]))
            vln = (vhat * lng_ref[:, cols]).astype(BF16)
            for k in range(tm // SG_CHUNK):
                rows = slice(SG_CHUNK * k, SG_CHUNK * (k + 1))
                mixed = _dot(w_ref[h], vln[rows]) + bb_ref[h]
                o_ref[rows, cols] = (_gelu(p_ref[rows, cols]) * mixed).astype(BF16)
        z = p_ref[:, 1536:2048] * p_ref[:, 2048:2560]
        zz = jnp.concatenate([jnp.where(first, 0.0, hc_ref[...] * hh_ref[...]), z], axis=0)
        y = cw_ref[0:1, :] * _shift_down(zz, 2)[SUBLANES:] + cw_ref[1:2, :] * _shift_down(zz, 1)[SUBLANES:] + cw_ref[2:3, :] * z
        o_ref[:, SG_WIDTH:] = (p_ref[:, 1024:1536] * y).astype(BF16)

    full = lambda shape: BS(shape, lambda i: (0,) * len(shape))
    return _call(body, name="even_mixer_fwd", grid=(T // tm,),
                 in_specs=[BS((tm, EVEN_IN), lambda i: (i, 0))] + _even_halo_specs(tm, T // tm, (3, 4), after=False)
                 + [full((1, SG_WIDTH)), full((SG_HEADS, SG_CHUNK, SG_CHUNK)), full((SG_HEADS, SG_CHUNK, SG_DIM)), full((SUBLANES, SC_WIDTH))],
                 out_specs=BS((tm, D_MODEL), lambda i: (i, 0)), out_shape=_sds((T, D_MODEL), BF16))(
        proj, proj, proj, ln_g, w_tril, b_lanes, conv_w)


def _even_mixer_bwd(proj, dmix, ln_g, w_tril, b_lanes, conv_w, seq, tm):
    T = proj.shape[0]
    n_tiles, tiles_per_seq = T // tm, seq // tm

    def body(p_ref, dm_ref, hc_ref, hh_ref, nd_ref, nb_ref, lng_ref, w_ref, bb_ref, cw_ref,
             dp_ref, dw_ref, db_ref, dlng_ref, dcw_ref):
        i = pl.program_id(0)
        first = i % tiles_per_seq == 0
        last = i % tiles_per_seq == tiles_per_seq - 1

        @pl.when(i == 0)
        def _():
            dw_ref[...] = jnp.zeros_like(dw_ref)
            db_ref[...] = jnp.zeros_like(db_ref)
            dlng_ref[...] = jnp.zeros_like(dlng_ref)
            dcw_ref[...] = jnp.zeros_like(dcw_ref)

        for h in range(SG_HEADS):
            cols = slice(SG_DIM * h, SG_DIM * (h + 1))
            vcols = slice(SG_WIDTH + SG_DIM * h, SG_WIDTH + SG_DIM * (h + 1))
            lng = lng_ref[:, cols]
            for k in range(tm // SG_CHUNK):
                rows = slice(SG_CHUNK * k, SG_CHUNK * (k + 1))
                v = p_ref[rows, vcols]
                vhat, rstd = _layer_norm_head(_gelu(v))
                vln = (vhat * lng).astype(BF16)
                mixed = _dot(w_ref[h], vln) + bb_ref[h]
                u = p_ref[rows, cols]
                da = dm_ref[rows, cols]
                dp_ref[rows, cols] = (da * mixed * _gelu_grad(u)).astype(BF16)
                dmixed = da * _gelu(u)
                db_ref[h] += dmixed
                dw_ref[h] += _dot(dmixed, vln, "nt")
                dvln = _dot(w_ref[h], dmixed, "tn")
                dlng_ref[:, cols] += jnp.sum(dvln * vhat, axis=0, keepdims=True)
                dvhat = dvln * lng
                dgv = rstd * (dvhat - jnp.mean(dvhat, axis=-1, keepdims=True)
                              - vhat * jnp.mean(dvhat * vhat, axis=-1, keepdims=True))
                dp_ref[rows, vcols] = (dgv * _gelu_grad(v)).astype(BF16)

        b = p_ref[:, 1024:1536]
        c = p_ref[:, 1536:2048]
        hv = p_ref[:, 2048:2560]
        z = c * hv
        zz = jnp.concatenate([jnp.where(first, 0.0, hc_ref[...] * hh_ref[...]), z], axis=0)
        z1 = _shift_down(zz, 1)[SUBLANES:]
        z2 = _shift_down(zz, 2)[SUBLANES:]
        w0, w1, w2 = cw_ref[0:1, :], cw_ref[1:2, :], cw_ref[2:3, :]
        dbo = dm_ref[:, SG_WIDTH:]
        dy = dbo * b
        dd = jnp.concatenate([dy, jnp.where(last, 0.0, nd_ref[...] * nb_ref[...])], axis=0)
        dz = w2 * dy + w1 * _shift_up(dd, 1)[:tm] + w0 * _shift_up(dd, 2)[:tm]
        dp_ref[:, 1024:1536] = (dbo * (w0 * z2 + w1 * z1 + w2 * z)).astype(BF16)
        dp_ref[:, 1536:2048] = (dz * hv).astype(BF16)
        dp_ref[:, 2048:2560] = (dz * c).astype(BF16)
        dcw_ref[0:1, :] += jnp.sum(dy * z2, axis=0, keepdims=True)
        dcw_ref[1:2, :] += jnp.sum(dy * z1, axis=0, keepdims=True)
        dcw_ref[2:3, :] += jnp.sum(dy * z, axis=0, keepdims=True)

        @pl.when(i == n_tiles - 1)
        def _():
            t_idx = lax.broadcasted_iota(jnp.int32, (SG_CHUNK, SG_CHUNK), 0)
            s_idx = lax.broadcasted_iota(jnp.int32, (SG_CHUNK, SG_CHUNK), 1)
            for h in range(SG_HEADS):
                dw_ref[h] = jnp.where(t_idx >= s_idx, dw_ref[h], 0.0)

    full = lambda shape: BS(shape, lambda i: (0,) * len(shape))
    sq = (SG_HEADS, SG_CHUNK, SG_CHUNK)
    return _call(body, name="even_mixer_bwd", grid=(n_tiles,),
                 in_specs=[BS((tm, EVEN_IN), lambda i: (i, 0)), BS((tm, D_MODEL), lambda i: (i, 0))]
                 + _even_halo_specs(tm, n_tiles, (3, 4), after=False)
                 + _even_halo_specs(tm, n_tiles, (1,), after=True) + _even_halo_specs(tm, n_tiles, (2,), after=True)
                 + [full((1, SG_WIDTH)), full(sq), full(sq), full((SUBLANES, SC_WIDTH))],
                 out_specs=[BS((tm, EVEN_IN), lambda i: (i, 0)), full(sq), full(sq), full((1, SG_WIDTH)), full((SUBLANES, SC_WIDTH))],
                 out_shape=[_sds((T, EVEN_IN), BF16), _sds(sq, F32), _sds(sq, F32), _sds((1, SG_WIDTH), F32), _sds((SUBLANES, SC_WIDTH), F32)])(
        proj, dmix, proj, proj, dmix, proj, ln_g, w_tril, b_lanes, conv_w)


def _pool_select(vals):
    lane = lax.broadcasted_iota(jnp.int32, vals[0].shape, 1)
    out = vals[-1]
    for g in range(len(vals) - 2, -1, -1):
        out = jnp.where(lane < POOL_DIM * (g + 1), vals[g], out)
    return out


def _pool_counts(pos1):
    lane = lax.broadcasted_iota(jnp.int32, (pos1.shape[0], POOL_WIDTH), 1)
    win = _pool_select([jnp.full(lane.shape, float(w), F32) for w in POOL_WINDOWS])
    return jnp.minimum(pos1, win)


def _pool_means(zz, counts):
    s2 = zz + _shift_down(zz, 1)
    s4 = s2 + _shift_down(s2, 2)
    s8 = s4 + _shift_down(s4, 4)
    s16 = s8 + _shift_down(s8, 8)
    return _pool_select([s2, s4, s8, s16])[POOL_HALO:] / counts


def _pool_halo_spec(tm, n_tiles, after):
    rows = tm // POOL_HALO
    if after:
        return BS((POOL_HALO, POOL_WIDTH), lambda i: (jnp.minimum((i + 1) * rows, n_tiles * rows - 1), 0))
    return BS((POOL_HALO, POOL_WIDTH), lambda i: (jnp.maximum(i * rows - 1, 0), 0))


def _pool_fwd(proj, w_diag, scale, seq, tm):
    T = proj.shape[0]
    tiles_per_seq = seq // tm

    def body(z_ref, zh_ref, w_ref, s_ref, o_ref):
        t = pl.program_id(0) % tiles_per_seq
        z = z_ref[...]
        zz = jnp.concatenate([jnp.where(t == 0, 0.0, zh_ref[...]), z], axis=0)
        pos1 = (lax.broadcasted_iota(jnp.int32, (tm, 1), 0) + (t * tm + 1)).astype(F32)
        pooled = _pool_means(zz, _pool_counts(pos1)) - z
        o_ref[...] = (_dot(pooled, w_ref[...]) * s_ref[...]).astype(BF16)

    full = lambda shape: BS(shape, lambda i: (0,) * len(shape))
    return _call(body, name="pool_fwd", grid=(T // tm,),
                 in_specs=[BS((tm, POOL_WIDTH), lambda i: (i, 0)), _pool_halo_spec(tm, T // tm, False),
                           full((POOL_WIDTH, POOL_WIDTH)), full((1, POOL_WIDTH))],
                 out_specs=BS((tm, POOL_WIDTH), lambda i: (i, 0)), out_shape=_sds((T, D_MODEL), BF16))(proj, proj, w_diag, scale)


def _pool_bwd(proj, dmix, w_diag, scale, seq, tm):
    T = proj.shape[0]
    n_tiles, tiles_per_seq = T // tm, seq // tm

    def body(z_ref, zh_ref, do_ref, don_ref, w_ref, s_ref, dz_ref, dw_ref, ds_ref):
        i = pl.program_id(0)
        t = i % tiles_per_seq

        @pl.when(i == 0)
        def _():
            dw_ref[...] = jnp.zeros_like(dw_ref)
            ds_ref[...] = jnp.zeros_like(ds_ref)

        z = z_ref[...]
        zz = jnp.concatenate([jnp.where(t == 0, 0.0, zh_ref[...]), z], axis=0)
        pos1 = (lax.broadcasted_iota(jnp.int32, (tm, 1), 0) + (t * tm + 1)).astype(F32)
        counts = _pool_counts(pos1)
        pooled = _pool_means(zz, counts) - z
        dout = do_ref[...].astype(F32)
        ds_ref[...] += jnp.sum(dout * _dot(pooled, w_ref[...]), axis=0, keepdims=True)
        dlin = dout * s_ref[...]
        dw_ref[...] += _dot(pooled, dlin, "tn")
        dpooled = _dot(dlin, w_ref[...], "nt")
        dpooled_n = _dot(don_ref[...].astype(F32) * s_ref[...], w_ref[...], "nt")
        pos1_n = (lax.broadcasted_iota(jnp.int32, (POOL_HALO, 1), 0) + ((t + 1) * tm + 1)).astype(F32)
        dmean_n = jnp.where(t == tiles_per_seq - 1, 0.0, dpooled_n / _pool_counts(pos1_n))
        dd = jnp.concatenate([dpooled / counts, dmean_n], axis=0)
        r2 = dd + _shift_up(dd, 1)
        r4 = r2 + _shift_up(r2, 2)
        r8 = r4 + _shift_up(r4, 4)
        r16 = r8 + _shift_up(r8, 8)
        dz_ref[...] = (_pool_select([r2, r4, r8, r16])[:tm] - dpooled).astype(BF16)

    full = lambda shape: BS(shape, lambda i: (0,) * len(shape))
    return _call(body, name="pool_bwd", grid=(n_tiles,),
                 in_specs=[BS((tm, POOL_WIDTH), lambda i: (i, 0)), _pool_halo_spec(tm, n_tiles, False),
                           BS((tm, POOL_WIDTH), lambda i: (i, 0)), _pool_halo_spec(tm, n_tiles, True),
                           full((POOL_WIDTH, POOL_WIDTH)), full((1, POOL_WIDTH))],
                 out_specs=[BS((tm, POOL_WIDTH), lambda i: (i, 0)), full((POOL_WIDTH, POOL_WIDTH)), full((1, POOL_WIDTH))],
                 out_shape=[_sds((T, POOL_WIDTH), BF16), _sds((POOL_WIDTH, POOL_WIDTH), F32), _sds((1, POOL_WIDTH), F32)])(
        proj, proj, dmix, dmix, w_diag, scale)


def _rope_partner(r):
    lane = lax.broadcasted_iota(jnp.int32, r.shape, 1)
    return jnp.where(lane < QK_ROPE // 2, pltpu.roll(r, LANES - QK_ROPE // 2, 1), pltpu.roll(r, QK_ROPE // 2, 1))


def _rope(x, cos, sin_signed):
    r = x[:, QK_NOPE:]
    return jnp.concatenate([x[:, :QK_NOPE], r * cos + _rope_partner(r) * sin_signed], axis=1)


def _rope_transposed(dx, cos, sin_signed):
    dr = dx[:, QK_NOPE:]
    return jnp.concatenate([dx[:, :QK_NOPE], dr * cos + _rope_partner(dr * sin_signed)], axis=1)


def _head_norm(x):
    r = lax.rsqrt(jnp.sum(x * x, axis=-1, keepdims=True) * (1.0 / QK_DIM) + EPS)
    return x * r, r


def _head_norm_bwd(dy, xhat, r, gain):
    dxhat = dy * gain
    return r * (dxhat - xhat * (jnp.sum(dxhat * xhat, axis=-1, keepdims=True) * (1.0 / QK_DIM)))


def _latents(p_ref, qag_ref, kvag_ref):
    ql = p_ref[:, POOL_WIDTH:POOL_WIDTH + Q_LORA]
    kvl = p_ref[:, POOL_WIDTH + Q_LORA:POOL_WIDTH + Q_LORA + KV_LORA]
    rq = lax.rsqrt(jnp.mean(ql * ql, axis=-1, keepdims=True) + EPS)
    rkv = lax.rsqrt(jnp.mean(kvl * kvl, axis=-1, keepdims=True) + EPS)
    return ql * rq, rq, kvl * rkv, rkv


def _mla_specs(tm):
    full = lambda shape: BS(shape, lambda i, h: (0,) * len(shape))
    return [BS((tm, ODD_IN_PAD), lambda i, h: (i, 0)), BS((tm, LANES), lambda i, h: (i, 0)), BS((tm, LANES), lambda i, h: (i, 0)),
            full((1, Q_LORA)), full((1, KV_LORA)), BS((None, Q_LORA, QK_PAD), lambda i, h: (h, 0, 0)),
            BS((None, KV_LORA, QK_PAD), lambda i, h: (h, 0, 0)), full((1, QK_PAD)), full((1, QK_PAD))]


def _mla_qkv_fwd(proj, cos, sin_signed, qa_g, kva_g, q_b, kv_b, q_g, k_g, tm):
    T = proj.shape[0]

    def body(p_ref, cos_ref, sin_ref, qag_ref, kvag_ref, qb_ref, kvb_ref, qg_ref, kg_ref, q_ref, k_ref, v_ref, qn_s, kvn_s):
        @pl.when(pl.program_id(1) == 0)
        def _():
            qhat, _, kvhat, _ = _latents(p_ref, qag_ref, kvag_ref)
            qn_s[...] = (qhat * qag_ref[...]).astype(BF16)
            kvn_s[...] = (kvhat * kvag_ref[...]).astype(BF16)

        cos, sin = cos_ref[...], sin_ref[...]
        qhat, _ = _head_norm(_dot(qn_s[...], qb_ref[...]))
        q_ref[...] = _rope(qhat * qg_ref[...], cos, sin).astype(BF16)
        kv = _dot(kvn_s[...], kvb_ref[...])
        khat, _ = _head_norm(jnp.concatenate([kv[:, :QK_NOPE], p_ref[:, ODD_IN_PAD - LANES:]], axis=1))
        k_ref[...] = _rope(khat * kg_ref[...], cos, sin).astype(BF16)
        v_ref[...] = kv[:, QK_NOPE:].astype(BF16)

    qk_spec = BS((None, tm, QK_PAD), lambda i, h: (h, i, 0))
    return _call(body, name="mla_qkv_fwd", grid=(T // tm, HEADS), in_specs=_mla_specs(tm),
                 out_specs=[qk_spec, qk_spec, BS((None, tm, V_DIM), lambda i, h: (h, i, 0))],
                 out_shape=[_sds((HEADS, T, QK_PAD), BF16), _sds((HEADS, T, QK_PAD), BF16), _sds((HEADS, T, V_DIM), BF16)],
                 scratch=[pltpu.VMEM((tm, Q_LORA), BF16), pltpu.VMEM((tm, KV_LORA), BF16)])(
        proj, cos, sin_signed, qa_g, kva_g, q_b, kv_b, q_g, k_g)


def _mla_qkv_bwd(proj, cos, sin_signed, qa_g, kva_g, q_b, kv_b, q_g, k_g, dq, dk, dv, dz_pool, tm):
    T = proj.shape[0]
    n_tiles = T // tm

    def body(p_ref, cos_ref, sin_ref, qag_ref, kvag_ref, qb_ref, kvb_ref, qg_ref, kg_ref, dq_ref, dk_ref, dv_ref, dzp_ref,
             dp_ref, dqb_ref, dkvb_ref, dqg_ref, dkg_ref, dqag_ref, dkvag_ref, qn_s, kvn_s, dqn_s, dkvn_s, dkr_s):
        i, h = pl.program_id(0), pl.program_id(1)

        @pl.when((i == 0) & (h == 0))
        def _():
            for ref in (dqb_ref, dkvb_ref, dqg_ref, dkg_ref, dqag_ref, dkvag_ref):
                ref[...] = jnp.zeros_like(ref)

        @pl.when(h == 0)
        def _():
            qhat, _, kvhat, _ = _latents(p_ref, qag_ref, kvag_ref)
            qn_s[...] = (qhat * qag_ref[...]).astype(BF16)
            kvn_s[...] = (kvhat * kvag_ref[...]).astype(BF16)
            dqn_s[...] = jnp.zeros_like(dqn_s)
            dkvn_s[...] = jnp.zeros_like(dkvn_s)
            dkr_s[...] = jnp.zeros_like(dkr_s)

        cos, sin = cos_ref[...], sin_ref[...]
        qhat, rq = _head_norm(_dot(qn_s[...], qb_ref[...]))
        dqn_head = _rope_transposed(dq_ref[...].astype(F32), cos, sin)
        dqg_ref[...] += jnp.sum(dqn_head * qhat, axis=0, keepdims=True)
        dqh = _head_norm_bwd(dqn_head, qhat, rq, qg_ref[...])
        dqb_ref[h] += _dot(qn_s[...], dqh, "tn")
        dqn_s[...] += _dot(dqh, qb_ref[...], "nt")

        kv = _dot(kvn_s[...], kvb_ref[...])
        khat, rk = _head_norm(jnp.concatenate([kv[:, :QK_NOPE], p_ref[:, ODD_IN_PAD - LANES:]], axis=1))
        dkn_head = _rope_transposed(dk_ref[...].astype(F32), cos, sin)
        dkg_ref[...] += jnp.sum(dkn_head * khat, axis=0, keepdims=True)
        dkf = _head_norm_bwd(dkn_head, khat, rk, kg_ref[...])
        dkr_s[...] += dkf[:, QK_NOPE:]
        dkv = jnp.concatenate([dkf[:, :QK_NOPE], dv_ref[...].astype(F32)], axis=1)
        dkvb_ref[h] += _dot(kvn_s[...], dkv, "tn")
        dkvn_s[...] += _dot(dkv, kvb_ref[...], "nt")

        @pl.when(h == HEADS - 1)
        def _():
            qhat_l, rql, kvhat_l, rkvl = _latents(p_ref, qag_ref, kvag_ref)
            dqn, dkvn = dqn_s[...], dkvn_s[...]
            dqag_ref[...] += jnp.sum(dqn * qhat_l, axis=0, keepdims=True)
            dkvag_ref[...] += jnp.sum(dkvn * kvhat_l, axis=0, keepdims=True)
            dqx, dkvx = dqn * qag_ref[...], dkvn * kvag_ref[...]
            dp_ref[:, :POOL_WIDTH] = dzp_ref[...]
            dp_ref[:, POOL_WIDTH:POOL_WIDTH + Q_LORA] = (
                rql * (dqx - qhat_l * jnp.mean(dqx * qhat_l, axis=-1, keepdims=True))).astype(BF16)
            dp_ref[:, POOL_WIDTH + Q_LORA:ODD_IN_PAD - LANES] = (
                rkvl * (dkvx - kvhat_l * jnp.mean(dkvx * kvhat_l, axis=-1, keepdims=True))).astype(BF16)
            dp_ref[:, ODD_IN_PAD - LANES:] = dkr_s[:, :QK_ROPE].astype(BF16)

    full = lambda shape: BS(shape, lambda i, h: (0,) * len(shape))
    qk_spec = BS((None, tm, QK_PAD), lambda i, h: (h, i, 0))
    return _call(body, name="mla_qkv_bwd", grid=(n_tiles, HEADS),
                 in_specs=_mla_specs(tm) + [qk_spec, qk_spec, BS((None, tm, V_DIM), lambda i, h: (h, i, 0)),
                                            BS((tm, POOL_WIDTH), lambda i, h: (i, 0))],
                 out_specs=[BS((tm, ODD_IN), lambda i, h: (i, 0)), full((HEADS, Q_LORA, QK_PAD)), full((HEADS, KV_LORA, QK_PAD)),
                            full((1, QK_PAD)), full((1, QK_PAD)), full((1, Q_LORA)), full((1, KV_LORA))],
                 out_shape=[_sds((T, ODD_IN), BF16),_sds((HEADS, Q_LORA, QK_PAD), F32), _sds((HEADS, KV_LORA, QK_PAD), F32),
                            _sds((1, QK_PAD), F32), _sds((1, QK_PAD), F32), _sds((1, Q_LORA), F32), _sds((1, KV_LORA), F32)],
                 scratch=[pltpu.VMEM((tm, Q_LORA), BF16), pltpu.VMEM((tm, KV_LORA), BF16), pltpu.VMEM((tm, Q_LORA), F32),
                          pltpu.VMEM((tm, KV_LORA), F32), pltpu.VMEM((tm, LANES), F32)])(
        proj, cos, sin_signed, qa_g, kva_g, q_b, kv_b, q_g, k_g, dq, dk, dv, dz_pool)


def _attn_tile(seq):
    return 512 if seq % 512 == 0 else seq


def _causal_mask(s):
    row = lax.broadcasted_iota(jnp.int32, s.shape, 0)
    col = lax.broadcasted_iota(jnp.int32, s.shape, 1)
    return jnp.where(row >= col, s, NEG_INF)


def _tile(i, t):
    return slice(i * t, (i + 1) * t)


def _flash_fwd(q, k, v, mix, batch, seq):
    t = _attn_tile(seq)
    nq = seq // t

    def body(q_ref, k_ref, v_ref, _, o_ref, lse_ref):
        for qi in range(nq):
            rows, before = _tile(qi, t), slice(0, qi * t)
            qv = q_ref[rows, :]
            s_diag = _causal_mask(_dot(qv, k_ref[rows, :], "nt") * ATTN_SCALE)
            m = jnp.max(s_diag, axis=-1, keepdims=True)
            if qi:
                s_before = _dot(qv, k_ref[before, :], "nt") * ATTN_SCALE
                m = jnp.maximum(m, jnp.max(s_before, axis=-1, keepdims=True))
            p = jnp.exp(s_diag - m)
            l = jnp.sum(p, axis=-1, keepdims=True)
            acc = _dot(p, v_ref[rows, :])
            if qi:
                p = jnp.exp(s_before - m)
                l = l + jnp.sum(p, axis=-1, keepdims=True)
                acc = acc + _dot(p, v_ref[before, :])
            o_ref[rows, :] = (acc / l).astype(BF16)
            lse_ref[rows, :] = jnp.broadcast_to(m + jnp.log(l), (t, LANES))

    T = batch * seq
    whole = lambda w: BS((None, seq, w), lambda b, h: (h, b, 0))
    return _call(body, name="flash_fwd", grid=(batch, HEADS),
                 in_specs=[whole(QK_PAD), whole(QK_PAD), whole(V_DIM), pl.BlockSpec(memory_space=pl.ANY)],
                 out_specs=[BS((seq, V_DIM), lambda b, h: (b, POOL_WIDTH // V_DIM + h)), whole(LANES)],
                 out_shape=[_sds((T, D_MODEL), BF16), _sds((HEADS, T, LANES), F32)],
                 aliases={3: 0})(q, k, v, mix)


def _flash_bwd(q, k, v, dmix, mix, lse, batch, seq):
    t = _attn_tile(seq)
    nq = seq // t

    def body(q_ref, k_ref, v_ref, do_ref, o_ref, lse_ref, dq_out, dk_out, dv_out, delta_s, dq_ref, dk_ref, dv_ref):
        dq_ref[...] = jnp.zeros_like(dq_ref)
        dk_ref[...] = jnp.zeros_like(dk_ref)
        dv_ref[...] = jnp.zeros_like(dv_ref)
        for qi in range(nq):
            rows = _tile(qi, t)
            delta_s[qi] = jnp.sum(do_ref[rows, :].astype(F32) * o_ref[rows, :].astype(F32), axis=-1, keepdims=True)
        for kb in range(nq):
            keys = _tile(kb, t)
            for qi in range(kb, nq):
                rows = _tile(qi, t)
                qv, kk, do = q_ref[rows, :], k_ref[keys, :], do_ref[rows, :]
                s = _dot(qv, kk, "nt") * ATTN_SCALE
                if kb == qi:
                    s = _causal_mask(s)
                p = jnp.exp(s - lse_ref[rows, 0:1])
                dv_ref[keys, :] += _dot(p, do, "tn")
                ds = p * (_dot(do, v_ref[keys, :], "nt") - delta_s[qi]) * ATTN_SCALE
                dq_ref[rows, :] += _dot(ds, kk)
                dk_ref[keys, :] += _dot(ds, qv, "tn")
        dq_out[...] = dq_ref[...].astype(BF16)
        dk_out[...] = dk_ref[...].astype(BF16)
        dv_out[...] = dv_ref[...].astype(BF16)

    T = batch * seq
    whole = lambda w: BS((None, seq, w), lambda b, h: (h, b, 0))
    head_cols = BS((seq, V_DIM), lambda b, h: (b, POOL_WIDTH // V_DIM + h))
    return _call(body, name="flash_bwd", grid=(batch, HEADS),
                 in_specs=[whole(QK_PAD), whole(QK_PAD), whole(V_DIM), head_cols, head_cols, whole(LANES)],
                 out_specs=[whole(QK_PAD), whole(QK_PAD), whole(V_DIM)],
                 out_shape=[_sds((HEADS, T, QK_PAD), BF16), _sds((HEADS, T, QK_PAD), BF16), _sds((HEADS, T, V_DIM), BF16)],
                 scratch=[pltpu.VMEM((nq, t, 1), F32), pltpu.VMEM((seq, QK_PAD), F32), pltpu.VMEM((seq, QK_PAD), F32),
                          pltpu.VMEM((seq, V_DIM), F32)])(q, k, v, dmix, mix, lse)


def _adamw_math(w, g, m, v):
    m = ADAM_B1 * m + (1.0 - ADAM_B1) * g
    v = ADAM_B2 * v + (1.0 - ADAM_B2) * (g * g)
    m_hat = m / (1.0 - ADAM_B1 ** ADAM_STEP)
    v_hat = v / (1.0 - ADAM_B2 ** ADAM_STEP)
    return -ADAM_LR * (m_hat / (jnp.sqrt(v_hat) + ADAM_EPS) + ADAM_WD * w), m, v


def _adamw(name, w, g, m, v):
    L, R, C = w.shape
    tr = 256 if R % 256 == 0 else R
    outs = None
    for l in range(L):
        def body(w_ref, g_ref, m_ref, v_ref, *rest):
            go_ref, d_ref, mo_ref, vo_ref = rest[-4:]
            gv = g_ref[...]
            d_ref[...], mo_ref[...], vo_ref[...] = _adamw_math(w_ref[...], gv, m_ref[...], v_ref[...])
            go_ref[...] = gv

        layer = BS((None, tr, C), functools.partial(lambda l, i: (l, i, 0), l))
        prev = [] if outs is None else list(outs)
        outs = _call(body, name=f"{name}_{l}", grid=(R // tr,),
                     in_specs=[layer, BS((tr, C), lambda i: (i, 0)), layer, layer] + [pl.BlockSpec(memory_space=pl.ANY)] * len(prev),
                     out_specs=[layer] * 4, out_shape=[_sds((L, R, C), F32)] * 4,
                     aliases={4 + n: n for n in range(len(prev))})(w, g[l], m, v, *prev)
    return outs


def _place():
    x, y, c = lax.axis_index("x"), lax.axis_index("y"), lax.axis_index("c")
    other_chips = [(1 - x, y), (x, 1 - y), (1 - x, 1 - y)]
    return x, y, c, other_chips


def _remote(src, dst, send_sem, recv_sem, dev):
    return pltpu.make_async_remote_copy(src_ref=src, dst_ref=dst, send_sem=send_sem, recv_sem=recv_sem,
                                        device_id=dev, device_id_type=MESH)


def _prefetch_call(body, *, name, grid, in_specs, out_specs, out_shape):
    grid_spec = pltpu.PrefetchScalarGridSpec(num_scalar_prefetch=1, grid=grid, in_specs=in_specs, out_specs=out_specs)
    params = pltpu.CompilerParams(vmem_limit_bytes=VMEM_LIMIT_V7X, dimension_semantics=("arbitrary",) * len(grid))
    call = pl.pallas_call(body, name=name, grid_spec=grid_spec, out_shape=_in_hbm(out_shape), compiler_params=params)
    return lambda scalars, *ops: call(scalars, *[pltpu.with_memory_space_constraint(o, pltpu.HBM) for o in ops])


def _row_tile(rows):
    return 256 if rows % 256 == 0 else rows


def _cast_place(name, w, layer, chip):
    _, _, rows, C = w.shape
    tr = _row_tile(rows)

    def body(chip_ref, w_ref, o_ref):
        o_ref[...] = w_ref[...].astype(BF16)

    return _prefetch_call(body, name=name, grid=(2, rows // tr),
                          in_specs=[BS((None, None, tr, C), lambda h, i, chip_ref: (layer, h, i, 0))],
                          out_specs=BS((None, None, tr, C), lambda h, i, chip_ref: (chip_ref[0], h, i, 0)),
                          out_shape=pltpu.HBM((N_CHIPS, 2, rows, C), BF16))(chip, w)


SEM = pl.BlockSpec(memory_space=pltpu.SEMAPHORE)


def _split_copy_call(body, *, name, in_specs, out_specs, out_shape, aliases):
    return pl.pallas_call(body, name=name, in_specs=in_specs, out_specs=out_specs, out_shape=out_shape,
                          input_output_aliases=aliases,
                          compiler_params=pltpu.CompilerParams(has_side_effects=pltpu.SideEffectType.DATAFLOW_SIDE_EFFECTING))


def _hbm(arrays):
    return [pltpu.with_memory_space_constraint(a, pltpu.HBM) for a in arrays]


def _gather_send(gs, groups, after):
    n = len(gs)

    def body(*refs):
        g, sems = refs[:n], refs[n + len(after):n + len(after) + 2 * len(groups)]
        x, y, c, chips = _place()
        me = 2 * x + y
        for gi, members in enumerate(groups):
            for a, i in enumerate(members):
                for k, (px, py) in enumerate(chips):
                    _remote(g[i].at[me, c], g[i].at[me, c], sems[2 * gi].at[3 * a + k], sems[2 * gi + 1].at[3 * a + k],
                            (px, py, c)).start()

    sem_shapes = [pltpu.SemaphoreType.DMA((3 * len(members),)) for members in groups for _ in range(2)]
    out = _split_copy_call(body, name="gather_send", in_specs=[HBM] * n + [pl.BlockSpec(memory_space=pl.ANY)] * len(after),
                           out_specs=[SEM] * len(sem_shapes) + [HBM] * n,
                           out_shape=sem_shapes + [pltpu.HBM(a.shape, a.dtype) for a in gs],
                           aliases={i: len(sem_shapes) + i for i in range(n)})(*_hbm(gs), *after)
    return out[:len(sem_shapes)], out[len(sem_shapes):]


def _gather_wait(name, gs, send_sems, recv_sems, after):
    n = len(gs)

    def body(*refs):
        g, ssem, rsem = refs[:n], refs[n], refs[n + 1]
        x, y, c, chips = _place()
        me = 2 * x + y
        for a in range(n):
            for k, (px, py) in enumerate(chips):
                landed = g[a].at[2 * px + py, c]
                cp = _remote(g[a].at[me, c], landed, ssem.at[3 * a + k], rsem.at[3 * a + k], (px, py, c))
                cp.wait_recv()
                cp.wait_send()

    return _split_copy_call(body, name=name, in_specs=[HBM] * n + [SEM, SEM] + [pl.BlockSpec(memory_space=pl.ANY)] * len(after),
                            out_specs=[HBM] * n, out_shape=[pltpu.HBM(a.shape, a.dtype) for a in gs],
                            aliases={i: i for i in range(n)})(*gs, send_sems, recv_sems, *after)


def _gather_pass(name, gs):
    n = len(gs)

    def body(*refs):
        g, send_sems, recv_sems = refs[n:2 * n], refs[-2], refs[-1]
        x, y, c, chips = _place()
        sibling = (x, y, 1 - c)
        passed = [_remote(g[i].at[2 * px + py, c], g[i].at[2 * px + py, c], send_sems.at[3 * i + k], recv_sems.at[3 * i + k], sibling)
                  for i in range(n) for k, (px, py) in enumerate(chips)]
        for cp in passed:
            cp.start()
        for i in range(n):
            for k, (px, py) in enumerate(chips):
                theirs = g[i].at[2 * px + py, 1 - c]
                _remote(theirs, theirs, send_sems.at[3 * i + k], recv_sems.at[3 * i + k], sibling).wait_recv()
        for cp in passed:
            cp.wait_send()

    return _call(body, name=name, in_specs=[HBM] * n, out_specs=[HBM] * n, out_shape=[_sds(a.shape, a.dtype) for a in gs],
                 aliases={i: i for i in range(n)},
                 scratch=[pltpu.SemaphoreType.DMA((3 * n,)), pltpu.SemaphoreType.DMA((3 * n,))])(*gs)


def _scatter_send(name, ps):
    n = len(ps)

    def body(*refs):
        p, r, ssem, rsem, token = refs[:n], refs[n:2 * n], refs[2 * n], refs[2 * n + 1], refs[-1]
        x, y, c, chips = _place()
        for i in range(n):
            for k, (px, py) in enumerate(chips):
                _remote(p[i].at[2 * px + py], r[i].at[k], ssem.at[3 * i + k], rsem.at[3 * i + k], (px, py, c)).start()
        token[...] = jnp.zeros_like(token)

    lands = [lax.empty((N_CHIPS - 1,) + a.shape[1:], a.dtype) for a in ps]
    sem = pltpu.SemaphoreType.DMA((3 * n,))
    out = _split_copy_call(body, name=name, in_specs=[HBM] * (2 * n), out_specs=[SEM, SEM] + [HBM] * (2 * n) + [VMEM],
                           out_shape=[sem, sem] + [pltpu.HBM(a.shape, a.dtype) for a in list(ps) + lands] + [_sds((SUBLANES, LANES), F32)],
                           aliases={i: 2 + i for i in range(2 * n)})(*_hbm(list(ps) + lands))
    return out[0], out[1], out[2:2 + n], out[2 + n:2 + 2 * n], out[-1]


def _scatter_wait(name, ps, lands, send_sems, recv_sems, after):
    n = len(ps)

    def body(*refs):
        p, r, ssem, rsem = refs[:n], refs[n:2 * n], refs[2 * n], refs[2 * n + 1]
        x, y, c, chips = _place()
        for i in range(n):
            for k, (px, py) in enumerate(chips):
                cp = _remote(p[i].at[2 * px + py], r[i].at[k], ssem.at[3 * i + k], rsem.at[3 * i + k], (px, py, c))
                cp.wait_recv()
                cp.wait_send()

    out = _split_copy_call(body, name=name, in_specs=[HBM] * (2 * n) + [SEM, SEM] + [pl.BlockSpec(memory_space=pl.ANY)] * len(after),
                           out_specs=[HBM] * (2 * n), out_shape=[pltpu.HBM(a.shape, a.dtype) for a in list(ps) + list(lands)],
                           aliases={i: i for i in range(2 * n)})(*ps, *lands, send_sems, recv_sems, *after)
    return out[:n], out[n:]


def _exchange_send(name, gs):
    n = len(gs)

    def body(*refs):
        g, r, ssem, rsem, token = refs[:n], refs[n:2 * n], refs[2 * n], refs[2 * n + 1], refs[-1]
        x, y, c, _ = _place()
        for i in range(n):
            _remote(g[i].at[:, 1 - c], r[i], ssem.at[i], rsem.at[i], (x, y, 1 - c)).start()
        token[...] = jnp.zeros_like(token)

    lands = [lax.empty((a.shape[0],) + a.shape[2:], a.dtype) for a in gs]
    sem = pltpu.SemaphoreType.DMA((n,))
    out = _split_copy_call(body, name=name, in_specs=[HBM] * (2 * n), out_specs=[SEM, SEM] + [HBM] * (2 * n) + [VMEM],
                           out_shape=[sem, sem] + [pltpu.HBM(a.shape, a.dtype) for a in list(gs) + lands] + [_sds((SUBLANES, LANES), F32)],
                           aliases={i: 2 + i for i in range(2 * n)})(*_hbm(list(gs) + lands))
    return out[0], out[1], out[2:2 + n], out[2 + n:2 + 2 * n], out[-1]


def _exchange_wait(name, gs, lands, send_sems, recv_sems, after):
    n = len(gs)

    def body(*refs):
        g, r, ssem, rsem = refs[:n], refs[n:2 * n], refs[2 * n], refs[2 * n + 1]
        x, y, c, _ = _place()
        for i in range(n):
            cp = _remote(g[i].at[:, 1 - c], r[i], ssem.at[i], rsem.at[i], (x, y, 1 - c))
            cp.wait_recv()
            cp.wait_send()

    out = _split_copy_call(body, name=name, in_specs=[HBM] * (2 * n) + [SEM, SEM] + [pl.BlockSpec(memory_space=pl.ANY)] * len(after),
                           out_specs=[HBM] * (2 * n), out_shape=[pltpu.HBM(a.shape, a.dtype) for a in list(gs) + list(lands)],
                           aliases={i: i for i in range(2 * n)})(*gs, *lands, send_sems, recv_sems, *after)
    return out[:n], out[n:]


def _sibling_share(fs):
    n = len(fs)

    def body(*refs):
        f, send_sems, recv_sems = refs[n:2 * n], refs[-2], refs[-1]
        x, y, c, _ = _place()
        sends = [_remote(f[i].at[c], f[i].at[c], send_sems.at[i], recv_sems.at[i], (x, y, 1 - c)) for i in range(n)]
        for cp in sends:
            cp.start()
        for i in range(n):
            theirs = f[i].at[1 - c]
            _remote(theirs, theirs, send_sems.at[i], recv_sems.at[i], (x, y, 1 - c)).wait_recv()
        for cp in sends:
            cp.wait_send()

    return _call(body, name="grad_sibling_share", in_specs=[HBM] * n, out_specs=[HBM] * n,
                 out_shape=[_sds(a.shape, a.dtype) for a in fs], aliases={i: i for i in range(n)},
                 scratch=[pltpu.SemaphoreType.DMA((n,)), pltpu.SemaphoreType.DMA((n,))])(*fs)


def _all_reduce_small(name, v):
    rows = v.shape[0] // 2
    halves = (2, rows, LANES)

    def body(v_ref, o_ref, from_sibling, chip_sums, send_sems, recv_sems):
        x, y, c, chips = _place()
        me, sibling = 2 * x + y, (x, y, 1 - c)
        swap = _remote(v_ref.at[1 - c], from_sibling, send_sems.at[0], recv_sems.at[0], sibling)
        swap.start()
        swap.wait()
        chip_sums[me] = v_ref[c] + from_sibling[...]
        sends = [_remote(chip_sums.at[me], chip_sums.at[me], send_sems.at[1 + k], recv_sems.at[1 + k], (px, py, c))
                 for k, (px, py) in enumerate(chips)]
        for cp in sends:
            cp.start()
        for k, (px, py) in enumerate(chips):
            theirs = chip_sums.at[2 * px + py]
            _remote(theirs, theirs, send_sems.at[1 + k], recv_sems.at[1 + k], (px, py, c)).wait_recv()
        for cp in sends:
            cp.wait_send()
        acc = chip_sums[0]
        for j in range(1, N_CHIPS):
            acc = acc + chip_sums[j]
        o_ref[c] = acc
        share = _remote(o_ref.at[c], o_ref.at[c], send_sems.at[4], recv_sems.at[4], sibling)
        share.start()
        share.wait_send()
        _remote(o_ref.at[1 - c], o_ref.at[1 - c], send_sems.at[4], recv_sems.at[4], sibling).wait_recv()

    return _call(body, name=name, in_specs=[VMEM], out_specs=VMEM, out_shape=_sds(halves, F32), in_vmem=True,
                 scratch=[pltpu.VMEM((rows, LANES), F32), pltpu.VMEM((N_CHIPS, rows, LANES), F32),
                          pltpu.SemaphoreType.DMA((5,)), pltpu.SemaphoreType.DMA((5,))])(v.reshape(halves)).reshape(v.shape)


def _add_halves(name, g, r, c):
    _, _, rows, C = g.shape
    tr = _row_tile(rows)

    def body(c_ref, g_ref, r_ref, o_ref):
        o_ref[...] = (g_ref[...] + r_ref[...]).astype(BF16)

    spec = BS((None, tr, C), lambda j, i, c_ref: (j, i, 0))
    return _prefetch_call(body, name=name, grid=(N_CHIPS, rows // tr),
                          in_specs=[BS((None, None, tr, C), lambda j, i, c_ref: (j, c_ref[0], i, 0)), spec], out_specs=spec,
                          out_shape=pltpu.HBM((N_CHIPS, rows, C), BF16))(c, g, r)


def _sum_partials(name, p, r, chip_c):
    _, rows, C = p.shape
    tr = _row_tile(rows)

    def body(s_ref, p_ref, r_ref, o_ref):
        acc = p_ref[...].astype(F32)
        for k in range(N_CHIPS - 1):
            acc = acc + r_ref[k].astype(F32)
        o_ref[...] = acc

    return _prefetch_call(body, name=name, grid=(rows // tr,),
                          in_specs=[BS((None, tr, C), lambda i, s: (s[0], i, 0)), BS((N_CHIPS - 1, tr, C), lambda i, s: (0, i, 0))],
                          out_specs=BS((None, tr, C), lambda i, s: (s[1], i, 0)), out_shape=pltpu.HBM((2, rows, C), F32))(chip_c, p, r)


_SHARDED = ("even_w_in", "even_w_out", "odd_w_in", "q_b", "kv_b", "odd_w_out", "ffn_w_gate", "ffn_w_up", "ffn_w_down")
_REPLICATED = ("mix_norm", "ffn_norm", "sg_ln_g", "sg_w_s", "sg_b_s", "pool_w", "q_norm", "k_norm")
_SMALL_SHARDED = ("sc_conv_w", "pool_scale", "q_a_norm", "kv_a_norm")
_WEIGHTS = ("mix_norm", "ffn_norm", "even_w_in", "sg_ln_g", "sg_w_s", "sg_b_s", "sc_conv_w", "even_w_out", "odd_w_in", "pool_w",
            "pool_scale", "q_a_norm", "q_b", "kv_a_norm", "kv_b", "q_norm", "k_norm", "odd_w_out", "ffn_w_gate", "ffn_w_up",
            "ffn_w_down")


def _pad_rows(flat, width, align):
    n = flat.shape[0]
    rows = -(-n // (width * align)) * align
    return jnp.pad(flat, (0, rows * width - n)).reshape(rows, width)


_GROUPS = {"even": ("even_w_in", "even_w_out"),
           "ffn0": ("ffn_w_gate0", "ffn_w_up0", "ffn_w_down0"),
           "odd": ("odd_w_in", "q_b", "kv_b", "odd_w_out"),
           "ffn1": ("ffn_w_gate1", "ffn_w_up1", "ffn_w_down1")}


def _place_shards(shards, chip):
    placed = {}
    for n in _SHARDED:
        a = shards[n]
        halves = a.reshape(a.shape[0], 2, a.shape[1] // 2, a.shape[2])
        if a.shape[0] == 1:
            placed[n] = _cast_place(f"place_{n}", halves, 0, chip)
        else:
            for l in range(a.shape[0]):
                placed[f"{n}{l}"] = _cast_place(f"place_{n}{l}", halves, l, chip)
    return placed


def _whole_weights(gathered):
    out = {n: a.reshape(N_CHIPS, -1, a.shape[-1]) for n, a in gathered.items()}
    for n in ("q_b", "kv_b"):
        if n in out:
            out[n] = out[n].transpose(1, 0, 2).reshape(out[n].shape[1], -1)
    for n in ("even_w_out", "odd_w_in", "odd_w_out"):
        if n in out:
            out[n] = out[n].reshape(-1, out[n].shape[-1])
    return out


def _forward_backward(x, positions, target, small, fetch, emit, advance):
    batch, seq, _ = x.shape
    T = batch * seq
    tm = _token_tile(seq)
    x0 = x.reshape(T, D_MODEL)

    inv_freq = ROPE_THETA ** (-jnp.arange(0, QK_ROPE, 2, dtype=F32) / QK_ROPE)
    ang = (positions.astype(F32)[..., None] * inv_freq).reshape(T, QK_ROPE // 2)
    cos, sin = jnp.cos(ang), jnp.sin(ang)
    pad = jnp.zeros((T, LANES - QK_ROPE), F32)
    cos_t = jnp.concatenate([cos, cos, pad], axis=1)
    sin_t = jnp.concatenate([-sin, sin, pad], axis=1)

    tril = jnp.tril(jnp.ones((SG_CHUNK, SG_CHUNK), bool))
    w_tril = jnp.where(tril[None], small["sg_w_s"][0], 0.0).astype(BF16)
    b_lanes = jnp.broadcast_to(small["sg_b_s"][0][:, :, None], (SG_HEADS, SG_CHUNK, SG_DIM))
    conv_w = jnp.pad(small["sc_conv_w"][0], ((0, SUBLANES - CONV_TAPS), (0, 0)))
    ln_g = small["sg_ln_g"]
    pool_diag = jnp.zeros((POOL_WIDTH, POOL_WIDTH), F32)
    for g in range(len(POOL_WINDOWS)):
        pool_diag = pool_diag.at[POOL_DIM * g:POOL_DIM * (g + 1), POOL_DIM * g:POOL_DIM * (g + 1)].set(small["pool_w"][0, g])
    pool_diag = pool_diag.astype(BF16)
    pool_scale = small["pool_scale"]
    q_g = jnp.pad(small["q_norm"], ((0, 0), (0, QK_PAD - QK_DIM)))
    k_g = jnp.pad(small["k_norm"], ((0, 0), (0, QK_PAD - QK_DIM)))
    qa_g, kva_g = small["q_a_norm"], small["kv_a_norm"]
    in_shard = EVEN_IN // N_CHIPS

    def ffn_weights(l, w):
        return w[f"ffn_w_gate{l}"], w[f"ffn_w_up{l}"], w[f"ffn_w_down{l}"]

    W = fetch("even", ())
    w_in_even = W["even_w_in"]
    h0 = _rmsnorm_fwd("mix0_norm", x0, small["mix_norm"][0], tm)
    tb = _big_tile(T)
    proj0 = _matmul("even_in", "nn", [(h0, w_in_even)],
                    [(BS((tb, D_MODEL), lambda i, j, k: (i, 0)), BS((None, D_MODEL, in_shard), lambda i, j, k: (j, 0, 0)))],
                    (T // tb, N_CHIPS, 1), _sds((T, EVEN_IN), F32), BS((tb, in_shard), lambda i, j, k: (i, j)), (tb, in_shard))
    mix0 = _even_mixer_fwd(proj0, ln_g, w_tril, b_lanes, conv_w, seq, tm)
    w_out_even = W["even_w_out"]
    x1, h1 = _mm("even_out", "nn", mix0, w_out_even, F32, tk=1024, add=x0, fused=_norm_tail(small["ffn_norm"][0], T, tb))
    ffn0 = ffn_weights(0, fetch("ffn0", (x1,)))
    (x2, h2), ffn0_saved = _ffn_fwd(0, x1, h1, *ffn0, _norm_tail(small["mix_norm"][1], T, tb))
    W = fetch("odd", (x2,))
    w_in_odd = jnp.pad(W["odd_w_in"], ((0, 0), (0, ODD_IN_PAD - ODD_IN)))
    q_b = jnp.pad(W["q_b"].reshape(Q_LORA, HEADS, QK_DIM).transpose(1, 0, 2), ((0, 0), (0, 0), (0, QK_PAD - QK_DIM)))
    kv_b = W["kv_b"].reshape(KV_LORA, HEADS, QK_NOPE + V_DIM).transpose(1, 0, 2)
    proj1 = _mm("odd_in", "nn", h2, w_in_odd, F32, tk=1024)
    mix1 = _pool_fwd(proj1, pool_diag, pool_scale, seq, tm)
    q, k, v = _mla_qkv_fwd(proj1, cos_t, sin_t, qa_g, kva_g, q_b, kv_b, q_g, k_g, tm)
    mix1, lse = _flash_fwd(q, k, v, mix1, batch, seq)
    x3, h3 = _mm("odd_out", "nn", mix1, W["odd_w_out"], F32, tk=1024, add=x2, fused=_norm_tail(small["ffn_norm"][1], T, tb))
    ffn1 = ffn_weights(1, fetch("ffn1", (x3,)))
    (dy, sq), ffn1_saved = _ffn_fwd(1, x3, h3, *ffn1, _loss_tail(target.reshape(T, D_MODEL), tb))

    G = {}
    dx3, dffn_g1 = _ffn_bwd(1, x3, small["ffn_norm"][1], *ffn1, ffn1_saved, dy, emit)
    dmix1 = _mm("odd_out_dx", "nt", dx3, W["odd_w_out"], BF16, tk=1024, after=advance(dx3))
    dw_out_odd = _mm("odd_out_dw", "tn", mix1, dx3, F32, hbm_out=True)
    dq, dk, dv = _flash_bwd(q, k, v, dmix1, mix1, lse, batch, seq)
    dz_pool, dpool_diag, G["pool_scale"] = _pool_bwd(proj1, dmix1, pool_diag, pool_scale, seq, tm)
    dproj1, dq_b, dkv_b, dq_g, dk_g, G["q_a_norm"], G["kv_a_norm"] = _mla_qkv_bwd(
        proj1, cos_t, sin_t, qa_g, kva_g, q_b, kv_b, q_g, k_g, dq, dk, dv, dz_pool, tm)
    G["pool_w"] = jnp.stack([dpool_diag[POOL_DIM * g:POOL_DIM * (g + 1), POOL_DIM * g:POOL_DIM * (g + 1)]
                             for g in range(len(POOL_WINDOWS))])[None]
    G["q_norm"], G["k_norm"] = dq_g[:, :QK_DIM], dk_g[:, :QK_DIM]
    dw_in_odd = _mm("odd_in_dw", "tn", h2, dproj1, F32, tn=ODD_IN, hbm_out=True)

    def shard_major(g, cols):
        return g.reshape(g.shape[0], N_CHIPS, cols).transpose(1, 0, 2)

    behind = emit("odd", {"odd_w_in": dw_in_odd.reshape(N_CHIPS, -1, ODD_IN),
                          "q_b": shard_major(dq_b[:, :, :QK_DIM].transpose(1, 0, 2).reshape(Q_LORA, HEADS * QK_DIM), HEADS * QK_DIM // N_CHIPS),
                          "kv_b": shard_major(dkv_b.transpose(1, 0, 2).reshape(KV_LORA, HEADS * (QK_NOPE + V_DIM)),
                                              HEADS * (QK_NOPE + V_DIM) // N_CHIPS),
                          "odd_w_out": dw_out_odd.reshape(N_CHIPS, -1, D_MODEL)})
    dx2, dmix_g1 = _mm("odd_in_dx", "nt", dproj1, W["odd_w_in"], F32, tk=ODD_IN, after=behind,
                       fused=_norm_bwd_tail(x2, small["mix_norm"][1], dx3, tb))
    dx1, dffn_g0 = _ffn_bwd(0, x1, small["ffn_norm"][0], *ffn0, ffn0_saved, dx2, emit, after=advance(dx2))
    dmix0 = _mm("even_out_dx", "nt", dx1, w_out_even, F32, tk=1024, after=advance(dx1))
    dw_out_even = _mm("even_out_dw", "tn", mix0, dx1, F32, hbm_out=True)
    dproj0, dw_s, db_lanes, G["sg_ln_g"], dconv = _even_mixer_bwd(proj0, dmix0, ln_g, w_tril, b_lanes, conv_w, seq, tm)
    G["sg_w_s"] = dw_s[None]
    G["sg_b_s"] = jnp.sum(db_lanes, axis=-1)[None]
    G["sc_conv_w"] = dconv[None, :CONV_TAPS]
    tail, shapes, specs = _norm_bwd_tail(x0, small["mix_norm"][0], dx1, tb)
    dx0, dmix_g0 = _matmul("even_in_dx", "nt", [(dproj0, w_in_even)],
                           [(BS((tb, in_shard), lambda i, j, k: (i, k)), BS((None, D_MODEL, in_shard), lambda i, j, k: (k, 0, 0)))],
                           (T // tb, 1, N_CHIPS), shapes, specs, (tb, D_MODEL), tail=tail)
    tk = min(512, T)
    dw_in_even = _grad_shards(
        "even_in_dw", h0, dproj0, BS((tk, D_MODEL), lambda k: (k, 0)), BS((tk, EVEN_IN), lambda k: (k, 0)),
        lambda a_ref, b_ref, j: (a_ref[...], b_ref[:, in_shard * j:in_shard * (j + 1)]), (N_CHIPS, D_MODEL, in_shard), T // tk)
    emit("even", {"even_w_in": dw_in_even, "even_w_out": dw_out_even.reshape(N_CHIPS, -1, D_MODEL)})
    advance(dx0)
    G["mix_norm"] = jnp.concatenate([dmix_g0, dmix_g1], axis=0)
    G["ffn_norm"] = jnp.concatenate([dffn_g0, dffn_g1], axis=0)
    return sq[0, 0], dx0.reshape(batch, seq, D_MODEL), G


def _small_vector(parts, names):
    flat = jnp.concatenate([parts[n].astype(F32).reshape(-1) for n in names])
    return _pad_rows(flat, LANES, 2 * SUBLANES)


def _split_small(vec, like, names):
    out, off, flat = {}, 0, vec.reshape(-1)
    for n in names:
        size = math.prod(like[n].shape)
        out[n] = flat[off:off + size].reshape(like[n].shape)
        off += size
    return out


def _whole_shape(a):
    return a.shape[:-1] + (a.shape[-1] * N_CHIPS,)


def kernel(x, positions, mix_norm, ffn_norm, even_w_in, sg_ln_g, sg_w_s, sg_b_s, sc_conv_w, even_w_out, odd_w_in, pool_w, pool_scale, q_a_norm, q_b, kv_a_norm, kv_b, q_norm, k_norm, odd_w_out, ffn_w_gate, ffn_w_up, ffn_w_down, loss_target, m_mix_norm, m_ffn_norm, m_even_w_in, m_sg_ln_g, m_sg_w_s, m_sg_b_s, m_sc_conv_w, m_even_w_out, m_odd_w_in, m_pool_w, m_pool_scale, m_q_a_norm, m_q_b, m_kv_a_norm, m_kv_b, m_q_norm, m_k_norm, m_odd_w_out, m_ffn_w_gate, m_ffn_w_up, m_ffn_w_down, v_mix_norm, v_ffn_norm, v_even_w_in, v_sg_ln_g, v_sg_w_s, v_sg_b_s, v_sc_conv_w, v_even_w_out, v_odd_w_in, v_pool_w, v_pool_scale, v_q_a_norm, v_q_b, v_kv_a_norm, v_kv_b, v_q_norm, v_k_norm, v_odd_w_out, v_ffn_w_gate, v_ffn_w_up, v_ffn_w_down):
    args = dict(locals())
    w = {n: args[n] for n in _WEIGHTS}
    m = {n: args["m_" + n] for n in _WEIGHTS}
    v = {n: args["v_" + n] for n in _WEIGHTS}
    cx, cy, cc = lax.axis_index("x"), lax.axis_index("y"), lax.axis_index("c")
    chip = 2 * cx + cy
    transposed = ("ffn_w_gate", "ffn_w_up")
    for n in transposed:
        w[n], m[n], v[n] = (jnp.swapaxes(t[n], 1, 2) for t in (w, m, v))

    chip_arr = chip.astype(jnp.int32).reshape(1)
    c_arr = cc.astype(jnp.int32).reshape(1)
    group_names = list(_GROUPS)
    order = [n for g in group_names for n in _GROUPS[g]]
    placed = {}
    for n in _SMALL_SHARDED:
        a = w[n]
        whole = jnp.zeros(a.shape[:-1] + (N_CHIPS, a.shape[-1]), F32)
        whole = lax.dynamic_update_slice_in_dim(whole, a[..., None, :], chip, axis=a.ndim - 1)
        placed[n] = jnp.where(cc == 0, whole, 0.0).reshape(_whole_shape(a))
    small_whole = _all_reduce_small("gather_small_weights", _small_vector(placed, _SMALL_SHARDED))
    small = dict({n: w[n] for n in _REPLICATED}, **_split_small(small_whole, placed, _SMALL_SHARDED))

    shards = _place_shards(w, chip_arr)
    sems, in_flight = _gather_send([shards[n] for n in order], [[order.index(n) for n in _GROUPS[g]] for g in group_names],
                                   (small_whole,))
    in_flight = dict(zip(order, in_flight))

    def fetch(group, after):
        gi, members = group_names.index(group), _GROUPS[group]
        landed = _gather_wait(f"gather_wait_{group}", [in_flight[n] for n in members], sems[2 * gi], sems[2 * gi + 1], after)
        return _whole_weights(dict(zip(members, _gather_pass(f"gather_pass_{group}", landed))))

    swapping, pending, arrived = [], [], {}

    def settle(after):
        names, ps, lands, send_sems, recv_sems = pending.pop()
        ps, lands = _scatter_wait(f"scatter_wait_{names[0]}", ps, lands, send_sems, recv_sems, after)
        arrived.update({n: (p, r) for n, p, r in zip(names, ps, lands)})

    def emit(group, grads):
        names = _GROUPS[group]
        halves = [grads[n].reshape(N_CHIPS, 2, grads[n].shape[1] // 2, grads[n].shape[2]) for n in names]
        send_sems, recv_sems, halves, lands, token = _exchange_send(f"exchange_send_{group}", halves)
        swapping.append((group, halves, lands, send_sems, recv_sems))
        return (token,)

    def advance(done):
        group, halves, lands, send_sems, recv_sems = swapping.pop()
        names = _GROUPS[group]
        halves, lands = _exchange_wait(f"exchange_wait_{group}", halves, lands, send_sems, recv_sems, (done,))
        partial = [_add_halves(f"add_{n}", g, r, c_arr) for n, g, r in zip(names, halves, lands)]
        if pending:
            settle((done,))
        send_sems, recv_sems, ps, lands, token = _scatter_send(f"scatter_send_{group}", partial)
        pending.append((names, ps, lands, send_sems, recv_sems))
        return (token,)

    sq, grad_x, G = _forward_backward(x, positions, loss_target, small, fetch, emit, advance)
    loss = lax.psum(0.5 * sq / D_MODEL, ("x", "y", "c"))

    small_names = _REPLICATED + _SMALL_SHARDED
    summed = _split_small(_all_reduce_small("reduce_small_grads", _small_vector(G, small_names)), G, small_names)
    grads = {n: summed[n] for n in _REPLICATED}
    for n in _SMALL_SHARDED:
        a = w[n]
        grads[n] = lax.dynamic_slice_in_dim(summed[n].reshape(a.shape[:-1] + (N_CHIPS, a.shape[-1])), chip, 1,
                                            axis=a.ndim - 1).reshape(a.shape)

    settle(())
    chip_c = jnp.stack([chip, cc]).astype(jnp.int32)
    sums = [_sum_partials(f"sum_{n}", *arrived[n], chip_c) for n in order]
    shard_grad = {n: f.reshape(1, -1, f.shape[-1]) for n, f in zip(order, _sibling_share(sums))}

    out = {}
    for n in ("even_w_in", "even_w_out", "odd_w_in", "q_b", "kv_b", "odd_w_out"):
        out[n] = _adamw(f"adamw_{n}", w[n], [shard_grad[n][0]], m[n], v[n])
    for n in ("ffn_w_gate", "ffn_w_up", "ffn_w_down"):
        out[n] = _adamw(f"adamw_{n}", w[n], [shard_grad[f"{n}{l}"][0] for l in range(2)], m[n], v[n])
    packed = [_small_vector(d, small_names) for d in (w, grads, m, v)]
    res = _adamw("adamw_small", packed[0][None], [packed[1]], packed[2][None], packed[3][None])
    delta_s, m_s, v_s = (_split_small(r, w, small_names) for r in res[1:])
    for n in small_names:
        out[n] = (grads[n], delta_s[n], m_s[n], v_s[n])
    for n in transposed:
        out[n] = tuple(jnp.swapaxes(t, 1, 2) for t in out[n])

    return (loss, grad_x, *[out[n][0] for n in _WEIGHTS], *[out[n][1] for n in _WEIGHTS],
            *[out[n][2] for n in _WEIGHTS], *[out[n][3] for n in _WEIGHTS])
```

```python
import functools
import math

import jax
import jax.numpy as jnp
from jax import lax
from jax.experimental import pallas as pl
from jax.experimental.pallas import tpu as pltpu

F32, BF16 = jnp.float32, jnp.bfloat16
BS = pl.BlockSpec

D_MODEL = 1024
EPS = 1e-6
NEG_INF = -1e30
SG_HEADS, SG_DIM, SG_WIDTH, SG_CHUNK = 4, 128, 512, 128
SC_WIDTH, CONV_TAPS = 512, 3
EVEN_IN = 2 * SG_WIDTH + 3 * SC_WIDTH
POOL_WINDOWS = (2, 4, 8, 16)
POOL_DIM, POOL_WIDTH = 64, 256
POOL_HALO = 16
HEADS, Q_LORA, KV_LORA, QK_NOPE, QK_ROPE, V_DIM = 6, 384, 256, 128, 64, 128
QK_DIM = QK_NOPE + QK_ROPE
QK_PAD = 256
ODD_IN = POOL_WIDTH + Q_LORA + KV_LORA + QK_ROPE
ODD_IN_PAD = 1024
ROPE_THETA = 10000.0
ATTN_SCALE = QK_DIM ** -0.5
D_FF, N_CHIPS = 2816, 4
FF_SHARD = D_FF // N_CHIPS
ADAM_LR, ADAM_B1, ADAM_B2, ADAM_EPS, ADAM_WD, ADAM_STEP = 0.001, 0.9, 0.999, 1e-08, 0.01, 10
VMEM_LIMIT_V7X = 48 * 2**20
LANES, SUBLANES = 128, 8
MESH = pl.DeviceIdType.MESH
HBM = pl.BlockSpec(memory_space=pltpu.HBM)
VMEM = pl.BlockSpec(memory_space=pltpu.VMEM)

_DIMS = {"nn": (((1,), (0,)), ((), ())), "nt": (((1,), (1,)), ((), ())), "tn": (((0,), (0,)), ((), ()))}


def _dot(a, b, mode="nn"):
    return lax.dot_general(a.astype(BF16), b.astype(BF16), _DIMS[mode], preferred_element_type=F32)


def _call(body, *, name, out_shape, in_specs, out_specs, grid=(), scratch=(), aliases=None, after=()):
    params = pltpu.CompilerParams(vmem_limit_bytes=VMEM_LIMIT_V7X,
                                  **({"dimension_semantics": ("arbitrary",) * len(grid)} if grid else {}))
    n_in, n_after = len(in_specs), len(after)
    kernel_body = body if not after else (lambda *refs: body(*refs[:n_in], *refs[n_in + n_after:]))
    call = pl.pallas_call(kernel_body, name=name, grid=grid, in_specs=list(in_specs) + [pl.BlockSpec(memory_space=pl.ANY)] * n_after,
                          out_specs=out_specs, out_shape=out_shape, scratch_shapes=list(scratch),
                          input_output_aliases=aliases or {}, compiler_params=params)
    return (lambda *ops: call(*ops, *after)) if after else call


def _sds(shape, dtype):
    return jax.ShapeDtypeStruct(tuple(shape), dtype)


def _token_tile(seq):
    return 512 if seq % 512 == 0 else seq


_TAIL_ROWS = 256


def _matmul(name, mode, pairs, pair_specs, grid, out_shape, out_spec, acc_shape, add=None, add_spec=None, after=(), tail=None):
    n, nk = len(pairs), grid[-1]
    n_add = int(add is not None)
    n_tail = len(tail[0]) if tail else 0
    n_in = 2 * n + n_add + n_tail
    n_out = len(out_shape) if tail else 1

    def body(*refs):
        ab = refs[:2 * n]
        add_ref = refs[2 * n] if n_add else None
        tail_refs, outs = refs[2 * n + n_add:n_in], refs[n_in:n_in + n_out]
        first = pl.program_id(0) == 0

        def finish(result):
            if tail is None:
                r = result(slice(None))
                outs[0][...] = (r if add_ref is None else r + add_ref[...]).astype(outs[0].dtype)
                return
            for lo in range(0, acc_shape[0], _TAIL_ROWS):
                rows = slice(lo, min(lo + _TAIL_ROWS, acc_shape[0]))
                r = result(rows)
                tail[2](rows, r if add_ref is None else r + add_ref[rows, :], first, tail_refs, outs)

        if nk == 1:
            r = _dot(ab[0][...], ab[1][...], mode)
            for p in range(1, n):
                r = r + _dot(ab[2 * p][...], ab[2 * p + 1][...], mode)
            finish(lambda rows: r[rows])
            return
        acc = refs[-1]
        k = pl.program_id(len(grid) - 1)

        @pl.when(k == 0)
        def _():
            acc[...] = jnp.zeros_like(acc)

        for p in range(n):
            acc[...] += _dot(ab[2 * p][...], ab[2 * p + 1][...], mode)

        @pl.when(k == nk - 1)
        def _():
            finish(lambda rows: acc[rows, :])

    ops = [t for pr in pairs for t in pr] + ([add] if n_add else []) + (list(tail[0]) if tail else [])
    specs = [s for pr in pair_specs for s in pr] + ([add_spec] if n_add else []) + (list(tail[1]) if tail else [])
    return _call(body, name=name, grid=grid, in_specs=specs, out_specs=out_spec, out_shape=out_shape,
                 scratch=[pltpu.VMEM(acc_shape, F32)] if nk > 1 else [], after=after)(*ops)


def _row_spec(tm, d):
    return BS((tm, d), lambda i, j, k: (i, 0))


def _vec_spec(d):
    return BS((1, d), lambda i, j, k: (0, 0))


def _norm_tail(gain, T, tm):
    d = gain.shape[-1]

    def fn(rows, r, first, tail_refs, outs):
        outs[0][rows, :] = r
        outs[1][rows, :] = (r * lax.rsqrt(jnp.mean(r * r, axis=-1, keepdims=True) + EPS) * tail_refs[0][...]).astype(BF16)

    return ([gain.reshape(1, d)], [_vec_spec(d)], fn), [_sds((T, d), F32), _sds((T, d), BF16)], [_row_spec(tm, d), _row_spec(tm, d)]


def _norm_bwd_tail(x, gain, dres, tm):
    T, d = x.shape

    def fn(rows, r, first, tail_refs, outs):
        x_ref, g_ref, dres_ref = tail_refs
        xv = x_ref[rows, :]
        rstd = lax.rsqrt(jnp.mean(xv * xv, axis=-1, keepdims=True) + EPS)
        xhat = xv * rstd
        if rows.start == 0:
            @pl.when(first)
            def _():
                outs[1][...] = jnp.zeros_like(outs[1])

        outs[1][...] += jnp.sum(r * xhat, axis=0, keepdims=True)
        dxhat = r * g_ref[...]
        outs[0][rows, :] = dres_ref[rows, :] + rstd * (dxhat - xhat * jnp.mean(dxhat * xhat, axis=-1, keepdims=True))

    return (([x, gain.reshape(1, d), dres], [_row_spec(tm, d), _vec_spec(d), _row_spec(tm, d)], fn),
            [_sds((T, d), F32), _sds((1, d), F32)], [_row_spec(tm, d), _vec_spec(d)])


def _loss_tail(target, tm):
    T, d = target.shape

    def fn(rows, r, first, tail_refs, outs):
        e = r - tail_refs[0][rows, :]
        if rows.start == 0:
            @pl.when(first)
            def _():
                outs[1][...] = jnp.zeros_like(outs[1])

        outs[1][...] += jnp.sum(e * e)
        outs[0][rows, :] = e * (1.0 / d)

    return (([target], [_row_spec(tm, d)], fn), [_sds((T, d), F32), _sds((SUBLANES, LANES), F32)],
            [_row_spec(tm, d), BS((SUBLANES, LANES), lambda i, j, k: (0, 0))])


def _grad_shards(name, a, b, a_spec, b_spec, pick, out_shape, n_steps):
    def body(a_ref, b_ref, o_ref):
        @pl.when(pl.program_id(0) == 0)
        def _():
            o_ref[...] = jnp.zeros_like(o_ref)

        for j in range(N_CHIPS):
            aj, bj = pick(a_ref, b_ref, j)
            o_ref[j] += _dot(aj, bj, "tn")

    return _call(body, name=name, grid=(n_steps,), in_specs=[a_spec, b_spec],
                 out_specs=BS(out_shape, lambda k: (0, 0, 0)), out_shape=pltpu.HBM(tuple(out_shape), F32))(a, b)


def _mm(name, mode, a, b, out_dtype, tm=1024, tn=1024, tk=512, add=None, after=(), fused=None, hbm_out=False):
    if mode == "tn":
        (K, M), N = a.shape, b.shape[1]
    else:
        (M, K), N = a.shape, (b.shape[1] if mode == "nn" else b.shape[0])
    tm, tn, tk = min(tm, M), min(tn, N), min(tk, K)
    a_spec = BS((tk, tm), lambda i, j, k: (k, i)) if mode == "tn" else BS((tm, tk), lambda i, j, k: (i, k))
    b_spec = BS((tn, tk), lambda i, j, k: (j, k)) if mode == "nt" else BS((tk, tn), lambda i, j, k: (k, j))
    o_spec = BS((tm, tn), lambda i, j, k: (i, j))
    tail, shapes, specs = fused if fused else (None, pltpu.HBM((M, N), out_dtype) if hbm_out else _sds((M, N), out_dtype), o_spec)
    return _matmul(name, mode, [(a, b)], [(a_spec, b_spec)], (M // tm, N // tn, K // tk), shapes, specs, (tm, tn),
                   add=add, add_spec=o_spec if add is not None else None, after=after, tail=tail)


def _rmsnorm_fwd(name, x, g, tm):
    T, d = x.shape

    def body(x_ref, g_ref, o_ref):
        xv = x_ref[...]
        y = xv * lax.rsqrt(jnp.mean(xv * xv, axis=-1, keepdims=True) + EPS)
        o_ref[...] = (y * g_ref[...]).astype(o_ref.dtype)

    return _call(body, name=name, grid=(T // tm,), in_specs=[BS((tm, d), lambda i: (i, 0)), BS((1, d), lambda i: (0, 0))],
                 out_specs=BS((tm, d), lambda i: (i, 0)), out_shape=_sds((T, d), BF16))(x, g.reshape(1, d))


def _ffn_up(name, h, wg, wu, tm):
    T = h.shape[0]

    def body(h_ref, wg_ref, wu_ref, g_ref, u_ref, a_ref):
        hv = h_ref[...]
        g = _dot(hv, wg_ref[...], "nt")
        u = _dot(hv, wu_ref[...], "nt")
        g_ref[...] = g.astype(BF16)
        u_ref[...] = u.astype(BF16)
        a_ref[...] = (g * (1.0 / (1.0 + jnp.exp(-g))) * u).astype(BF16)

    w_spec = BS((None, FF_SHARD, D_MODEL), lambda j, i: (j, 0, 0))
    o_spec = BS((None, tm, FF_SHARD), lambda j, i: (j, i, 0))
    sh = _sds((N_CHIPS, T, FF_SHARD), BF16)
    return _call(body, name=name, grid=(N_CHIPS, T // tm), in_specs=[BS((tm, D_MODEL), lambda j, i: (i, 0)), w_spec, w_spec],
                 out_specs=[o_spec, o_spec, o_spec], out_shape=[sh, sh, sh])(h, wg, wu)


def _ffn_act_bwd(name, dxo, wd, g, u, tm, after=()):
    T = dxo.shape[0]

    def body(dx_ref, wd_ref, g_ref, u_ref, dg_ref, du_ref):
        da = _dot(dx_ref[...], wd_ref[...], "nt")
        g = g_ref[...].astype(F32)
        sig = 1.0 / (1.0 + jnp.exp(-g))
        dg_ref[...] = (da * u_ref[...].astype(F32) * (sig * (1.0 + g * (1.0 - sig)))).astype(BF16)
        du_ref[...] = (da * (g * sig)).astype(BF16)

    t_spec = BS((None, tm, FF_SHARD), lambda i, j: (j, i, 0))
    sh = _sds((N_CHIPS, T, FF_SHARD), BF16)
    return _call(body, name=name, grid=(T // tm, N_CHIPS),
                 in_specs=[BS((tm, D_MODEL), lambda i, j: (i, 0)), BS((None, FF_SHARD, D_MODEL), lambda i, j: (j, 0, 0)), t_spec, t_spec],
                 out_specs=[t_spec, t_spec], out_shape=[sh, sh], after=after)(dxo, wd, g, u)


def _big_tile(n):
    return min(1024, n)


def _ffn_fwd(l, x, h, wg, wu, wd, fused):
    T = x.shape[0]
    tm = _big_tile(T)
    g, u, a = _ffn_up(f"ffn{l}_up", h, wg, wu, tm)
    tn = D_MODEL
    tail, shapes, specs = fused
    outs = _matmul(f"ffn{l}_down", "nn", [(a, wd)],
                   [(BS((None, tm, FF_SHARD), lambda i, j, k: (k, i, 0)), BS((None, FF_SHARD, tn), lambda i, j, k: (k, 0, j)))],
                   (T // tm, D_MODEL // tn, N_CHIPS), shapes, specs, (tm, tn),
                   add=x, add_spec=BS((tm, tn), lambda i, j, k: (i, j)), tail=tail)
    return outs, (h, g, u, a)


def _ffn_bwd(l, x, gain, wg, wu, wd, saved, dxo, emit, after=()):
    h, g, u, a = saved
    T = x.shape[0]
    tm = _big_tile(T)
    dg, du = _ffn_act_bwd(f"ffn{l}_act_bwd", dxo, wd, g, u, tm, after=after)
    tk = min(512, T)
    tn = D_MODEL
    shards_spec = BS((N_CHIPS, tk, FF_SHARD), lambda k: (0, k, 0))
    rows_spec = BS((tk, D_MODEL), lambda k: (k, 0))

    def dw(nm, act, rows):
        return _grad_shards(nm, act, rows, shards_spec, rows_spec, lambda a_ref, b_ref, j: (a_ref[j], b_ref[...]),
                            (N_CHIPS, FF_SHARD, D_MODEL), T // tk)

    behind = emit(f"ffn{l}", {f"ffn_w_gate{l}": dw(f"ffn{l}_dwg", dg, h), f"ffn_w_up{l}": dw(f"ffn{l}_dwu", du, h),
                              f"ffn_w_down{l}": dw(f"ffn{l}_dwd", a, dxo)})
    act_spec = BS((None, tm, FF_SHARD), lambda i, j, k: (k, i, 0))
    w_spec = BS((None, FF_SHARD, tn), lambda i, j, k: (k, 0, j))
    tail, shapes, specs = _norm_bwd_tail(x, gain, dxo, tm)
    return _matmul(f"ffn{l}_dh", "nn", [(dg, wg), (du, wu)], [(act_spec, w_spec), (act_spec, w_spec)],
                   (T // tm, D_MODEL // tn, N_CHIPS), shapes, specs, (tm, tn), after=behind, tail=tail)


_INV_SQRT2 = 1.0 / math.sqrt(2.0)
_INV_SQRT_2PI = 1.0 / math.sqrt(2.0 * math.pi)


def _gelu(x):
    return 0.5 * x * (1.0 + lax.erf(x * _INV_SQRT2))


def _gelu_grad(x):
    return 0.5 * (1.0 + lax.erf(x * _INV_SQRT2)) + x * jnp.exp(-0.5 * x * x) * _INV_SQRT_2PI


def _shift_down(x, k):
    return pltpu.roll(x, k, 0)


def _shift_up(x, k):
    return pltpu.roll(x, x.shape[0] - k, 0)


def _layer_norm_head(xh):
    xc = xh - jnp.mean(xh, axis=-1, keepdims=True)
    rstd = lax.rsqrt(jnp.mean(xc * xc, axis=-1, keepdims=True) + EPS)
    return xc * rstd, rstd


def _even_halo_specs(tm, n_tiles, col_blocks, after):
    rows = tm // SUBLANES
    last = n_tiles * rows - 1
    if after:
        return [BS((SUBLANES, 512), functools.partial(lambda cb, i: (jnp.minimum((i + 1) * rows, last), cb), cb)) for cb in col_blocks]
    return [BS((SUBLANES, 512), functools.partial(lambda cb, i: (jnp.maximum(i * rows - 1, 0), cb), cb)) for cb in col_blocks]


def _even_mixer_fwd(proj, ln_g, w_tril, b_lanes, conv_w, seq, tm):
    T = proj.shape[0]
    tiles_per_seq = seq // tm

    def body(p_ref, hc_ref, hh_ref, lng_ref, w_ref, bb_ref, cw_ref, o_ref):
        first = pl.program_id(0) % tiles_per_seq == 0
        for h in range(SG_HEADS):
            cols = slice(SG_DIM * h, SG_DIM * (h + 1))
            vhat, _ = _layer_norm_head(_gelu(p_ref[:, SG_WIDTH + SG_DIM * h:SG_WIDTH + SG_DIM * (h + 1)]))
            vln = (vhat * lng_ref[:, cols]).astype(BF16)
            for k in range(tm // SG_CHUNK):
                rows = slice(SG_CHUNK * k, SG_CHUNK * (k + 1))
                mixed = _dot(w_ref[h], vln[rows]) + bb_ref[h]
                o_ref[rows, cols] = (_gelu(p_ref[rows, cols]) * mixed).astype(BF16)
        z = p_ref[:, 1536:2048] * p_ref[:, 2048:2560]
        zz = jnp.concatenate([jnp.where(first, 0.0, hc_ref[...] * hh_ref[...]), z], axis=0)
        y = cw_ref[0:1, :] * _shift_down(zz, 2)[SUBLANES:] + cw_ref[1:2, :] * _shift_down(zz, 1)[SUBLANES:] + cw_ref[2:3, :] * z
        o_ref[:, SG_WIDTH:] = (p_ref[:, 1024:1536] * y).astype(BF16)

    full = lambda shape: BS(shape, lambda i: (0,) * len(shape))
    return _call(body, name="even_mixer_fwd", grid=(T // tm,),
                 in_specs=[BS((tm, EVEN_IN), lambda i: (i, 0))] + _even_halo_specs(tm, T // tm, (3, 4), after=False)
                 + [full((1, SG_WIDTH)), full((SG_HEADS, SG_CHUNK, SG_CHUNK)), full((SG_HEADS, SG_CHUNK, SG_DIM)), full((SUBLANES, SC_WIDTH))],
                 out_specs=BS((tm, D_MODEL), lambda i: (i, 0)), out_shape=_sds((T, D_MODEL), BF16))(
        proj, proj, proj, ln_g, w_tril, b_lanes, conv_w)


def _even_mixer_bwd(proj, dmix, ln_g, w_tril, b_lanes, conv_w, seq, tm):
    T = proj.shape[0]
    n_tiles, tiles_per_seq = T // tm, seq // tm

    def body(p_ref, dm_ref, hc_ref, hh_ref, nd_ref, nb_ref, lng_ref, w_ref, bb_ref, cw_ref,
             dp_ref, dw_ref, db_ref, dlng_ref, dcw_ref):
        i = pl.program_id(0)
        first = i % tiles_per_seq == 0
        last = i % tiles_per_seq == tiles_per_seq - 1

        @pl.when(i == 0)
        def _():
            dw_ref[...] = jnp.zeros_like(dw_ref)
            db_ref[...] = jnp.zeros_like(db_ref)
            dlng_ref[...] = jnp.zeros_like(dlng_ref)
            dcw_ref[...] = jnp.zeros_like(dcw_ref)

        for h in range(SG_HEADS):
            cols = slice(SG_DIM * h, SG_DIM * (h + 1))
            vcols = slice(SG_WIDTH + SG_DIM * h, SG_WIDTH + SG_DIM * (h + 1))
            lng = lng_ref[:, cols]
            for k in range(tm // SG_CHUNK):
                rows = slice(SG_CHUNK * k, SG_CHUNK * (k + 1))
                v = p_ref[rows, vcols]
                vhat, rstd = _layer_norm_head(_gelu(v))
                vln = (vhat * lng).astype(BF16)
                mixed = _dot(w_ref[h], vln) + bb_ref[h]
                u = p_ref[rows, cols]
                da = dm_ref[rows, cols]
                dp_ref[rows, cols] = (da * mixed * _gelu_grad(u)).astype(BF16)
                dmixed = da * _gelu(u)
                db_ref[h] += dmixed
                dw_ref[h] += _dot(dmixed, vln, "nt")
                dvln = _dot(w_ref[h], dmixed, "tn")
                dlng_ref[:, cols] += jnp.sum(dvln * vhat, axis=0, keepdims=True)
                dvhat = dvln * lng
                dgv = rstd * (dvhat - jnp.mean(dvhat, axis=-1, keepdims=True)
                              - vhat * jnp.mean(dvhat * vhat, axis=-1, keepdims=True))
                dp_ref[rows, vcols] = (dgv * _gelu_grad(v)).astype(BF16)

        b = p_ref[:, 1024:1536]
        c = p_ref[:, 1536:2048]
        hv = p_ref[:, 2048:2560]
        z = c * hv
        zz = jnp.concatenate([jnp.where(first, 0.0, hc_ref[...] * hh_ref[...]), z], axis=0)
        z1 = _shift_down(zz, 1)[SUBLANES:]
        z2 = _shift_down(zz, 2)[SUBLANES:]
        w0, w1, w2 = cw_ref[0:1, :], cw_ref[1:2, :], cw_ref[2:3, :]
        dbo = dm_ref[:, SG_WIDTH:]
        dy = dbo * b
        dd = jnp.concatenate([dy, jnp.where(last, 0.0, nd_ref[...] * nb_ref[...])], axis=0)
        dz = w2 * dy + w1 * _shift_up(dd, 1)[:tm] + w0 * _shift_up(dd, 2)[:tm]
        dp_ref[:, 1024:1536] = (dbo * (w0 * z2 + w1 * z1 + w2 * z)).astype(BF16)
        dp_ref[:, 1536:2048] = (dz * hv).astype(BF16)
        dp_ref[:, 2048:2560] = (dz * c).astype(BF16)
        dcw_ref[0:1, :] += jnp.sum(dy * z2, axis=0, keepdims=True)
        dcw_ref[1:2, :] += jnp.sum(dy * z1, axis=0, keepdims=True)
        dcw_ref[2:3, :] += jnp.sum(dy * z, axis=0, keepdims=True)

        @pl.when(i == n_tiles - 1)
        def _():
            t_idx = lax.broadcasted_iota(jnp.int32, (SG_CHUNK, SG_CHUNK), 0)
            s_idx = lax.broadcasted_iota(jnp.int32, (SG_CHUNK, SG_CHUNK), 1)
            for h in range(SG_HEADS):
                dw_ref[h] = jnp.where(t_idx >= s_idx, dw_ref[h], 0.0)

    full = lambda shape: BS(shape, lambda i: (0,) * len(shape))
    sq = (SG_HEADS, SG_CHUNK, SG_CHUNK)
    return _call(body, name="even_mixer_bwd", grid=(n_tiles,),
                 in_specs=[BS((tm, EVEN_IN), lambda i: (i, 0)), BS((tm, D_MODEL), lambda i: (i, 0))]
                 + _even_halo_specs(tm, n_tiles, (3, 4), after=False)
                 + _even_halo_specs(tm, n_tiles, (1,), after=True) + _even_halo_specs(tm, n_tiles, (2,), after=True)
                 + [full((1, SG_WIDTH)), full(sq), full(sq), full((SUBLANES, SC_WIDTH))],
                 out_specs=[BS((tm, EVEN_IN), lambda i: (i, 0)), full(sq), full(sq), full((1, SG_WIDTH)), full((SUBLANES, SC_WIDTH))],
                 out_shape=[_sds((T, EVEN_IN), BF16), _sds(sq, F32), _sds(sq, F32), _sds((1, SG_WIDTH), F32), _sds((SUBLANES, SC_WIDTH), F32)])(
        proj, dmix, proj, proj, dmix, proj, ln_g, w_tril, b_lanes, conv_w)


def _pool_select(vals):
    lane = lax.broadcasted_iota(jnp.int32, vals[0].shape, 1)
    out = vals[-1]
    for g in range(len(vals) - 2, -1, -1):
        out = jnp.where(lane < POOL_DIM * (g + 1), vals[g], out)
    return out


def _pool_counts(pos1):
    lane = lax.broadcasted_iota(jnp.int32, (pos1.shape[0], POOL_WIDTH), 1)
    win = _pool_select([jnp.full(lane.shape, float(w), F32) for w in POOL_WINDOWS])
    return jnp.minimum(pos1, win)


def _pool_means(zz, counts):
    s2 = zz + _shift_down(zz, 1)
    s4 = s2 + _shift_down(s2, 2)
    s8 = s4 + _shift_down(s4, 4)
    s16 = s8 + _shift_down(s8, 8)
    return _pool_select([s2, s4, s8, s16])[POOL_HALO:] / counts


def _pool_halo_spec(tm, n_tiles, after):
    rows = tm // POOL_HALO
    if after:
        return BS((POOL_HALO, POOL_WIDTH), lambda i: (jnp.minimum((i + 1) * rows, n_tiles * rows - 1), 0))
    return BS((POOL_HALO, POOL_WIDTH), lambda i: (jnp.maximum(i * rows - 1, 0), 0))


def _pool_fwd(proj, w_diag, scale, seq, tm):
    T = proj.shape[0]
    tiles_per_seq = seq // tm

    def body(z_ref, zh_ref, w_ref, s_ref, o_ref):
        t = pl.program_id(0) % tiles_per_seq
        z = z_ref[...]
        zz = jnp.concatenate([jnp.where(t == 0, 0.0, zh_ref[...]), z], axis=0)
        pos1 = (lax.broadcasted_iota(jnp.int32, (tm, 1), 0) + (t * tm + 1)).astype(F32)
        pooled = _pool_means(zz, _pool_counts(pos1)) - z
        o_ref[...] = (_dot(pooled, w_ref[...]) * s_ref[...]).astype(BF16)

    full = lambda shape: BS(shape, lambda i: (0,) * len(shape))
    return _call(body, name="pool_fwd", grid=(T // tm,),
                 in_specs=[BS((tm, POOL_WIDTH), lambda i: (i, 0)), _pool_halo_spec(tm, T // tm, False),
                           full((POOL_WIDTH, POOL_WIDTH)), full((1, POOL_WIDTH))],
                 out_specs=BS((tm, POOL_WIDTH), lambda i: (i, 0)), out_shape=_sds((T, D_MODEL), BF16))(proj, proj, w_diag, scale)


def _pool_bwd(proj, dmix, w_diag, scale, seq, tm):
    T = proj.shape[0]
    n_tiles, tiles_per_seq = T // tm, seq // tm

    def body(z_ref, zh_ref, do_ref, don_ref, w_ref, s_ref, dz_ref, dw_ref, ds_ref):
        i = pl.program_id(0)
        t = i % tiles_per_seq

        @pl.when(i == 0)
        def _():
            dw_ref[...] = jnp.zeros_like(dw_ref)
            ds_ref[...] = jnp.zeros_like(ds_ref)

        z = z_ref[...]
        zz = jnp.concatenate([jnp.where(t == 0, 0.0, zh_ref[...]), z], axis=0)
        pos1 = (lax.broadcasted_iota(jnp.int32, (tm, 1), 0) + (t * tm + 1)).astype(F32)
        counts = _pool_counts(pos1)
        pooled = _pool_means(zz, counts) - z
        dout = do_ref[...].astype(F32)
        ds_ref[...] += jnp.sum(dout * _dot(pooled, w_ref[...]), axis=0, keepdims=True)
        dlin = dout * s_ref[...]
        dw_ref[...] += _dot(pooled, dlin, "tn")
        dpooled = _dot(dlin, w_ref[...], "nt")
        dpooled_n = _dot(don_ref[...].astype(F32) * s_ref[...], w_ref[...], "nt")
        pos1_n = (lax.broadcasted_iota(jnp.int32, (POOL_HALO, 1), 0) + ((t + 1) * tm + 1)).astype(F32)
        dmean_n = jnp.where(t == tiles_per_seq - 1, 0.0, dpooled_n / _pool_counts(pos1_n))
        dd = jnp.concatenate([dpooled / counts, dmean_n], axis=0)
        r2 = dd + _shift_up(dd, 1)
        r4 = r2 + _shift_up(r2, 2)
        r8 = r4 + _shift_up(r4, 4)
        r16 = r8 + _shift_up(r8, 8)
        dz_ref[...] = (_pool_select([r2, r4, r8, r16])[:tm] - dpooled).astype(BF16)

    full = lambda shape: BS(shape, lambda i: (0,) * len(shape))
    return _call(body, name="pool_bwd", grid=(n_tiles,),
                 in_specs=[BS((tm, POOL_WIDTH), lambda i: (i, 0)), _pool_halo_spec(tm, n_tiles, False),
                           BS((tm, POOL_WIDTH), lambda i: (i, 0)), _pool_halo_spec(tm, n_tiles, True),
                           full((POOL_WIDTH, POOL_WIDTH)), full((1, POOL_WIDTH))],
                 out_specs=[BS((tm, POOL_WIDTH), lambda i: (i, 0)), full((POOL_WIDTH, POOL_WIDTH)), full((1, POOL_WIDTH))],
                 out_shape=[_sds((T, POOL_WIDTH), BF16), _sds((POOL_WIDTH, POOL_WIDTH), F32), _sds((1, POOL_WIDTH), F32)])(
        proj, proj, dmix, dmix, w_diag, scale)


def _rope_partner(r):
    lane = lax.broadcasted_iota(jnp.int32, r.shape, 1)
    return jnp.where(lane < QK_ROPE // 2, pltpu.roll(r, LANES - QK_ROPE // 2, 1), pltpu.roll(r, QK_ROPE // 2, 1))


def _rope(x, cos, sin_signed):
    r = x[:, QK_NOPE:]
    return jnp.concatenate([x[:, :QK_NOPE], r * cos + _rope_partner(r) * sin_signed], axis=1)


def _rope_transposed(dx, cos, sin_signed):
    dr = dx[:, QK_NOPE:]
    return jnp.concatenate([dx[:, :QK_NOPE], dr * cos + _rope_partner(dr * sin_signed)], axis=1)


def _head_norm(x):
    r = lax.rsqrt(jnp.sum(x * x, axis=-1, keepdims=True) * (1.0 / QK_DIM) + EPS)
    return x * r, r


def _head_norm_bwd(dy, xhat, r, gain):
    dxhat = dy * gain
    return r * (dxhat - xhat * (jnp.sum(dxhat * xhat, axis=-1, keepdims=True) * (1.0 / QK_DIM)))


def _latents(p_ref, qag_ref, kvag_ref):
    ql = p_ref[:, POOL_WIDTH:POOL_WIDTH + Q_LORA]
    kvl = p_ref[:, POOL_WIDTH + Q_LORA:POOL_WIDTH + Q_LORA + KV_LORA]
    rq = lax.rsqrt(jnp.mean(ql * ql, axis=-1, keepdims=True) + EPS)
    rkv = lax.rsqrt(jnp.mean(kvl * kvl, axis=-1, keepdims=True) + EPS)
    return ql * rq, rq, kvl * rkv, rkv


def _mla_specs(tm):
    full = lambda shape: BS(shape, lambda i, h: (0,) * len(shape))
    return [BS((tm, ODD_IN_PAD), lambda i, h: (i, 0)), BS((tm, LANES), lambda i, h: (i, 0)), BS((tm, LANES), lambda i, h: (i, 0)),
            full((1, Q_LORA)), full((1, KV_LORA)), BS((None, Q_LORA, QK_PAD), lambda i, h: (h, 0, 0)),
            BS((None, KV_LORA, QK_PAD), lambda i, h: (h, 0, 0)), full((1, QK_PAD)), full((1, QK_PAD))]


def _mla_qkv_fwd(proj, cos, sin_signed, qa_g, kva_g, q_b, kv_b, q_g, k_g, tm):
    T = proj.shape[0]

    def body(p_ref, cos_ref, sin_ref, qag_ref, kvag_ref, qb_ref, kvb_ref, qg_ref, kg_ref, q_ref, k_ref, v_ref, qn_s, kvn_s):
        @pl.when(pl.program_id(1) == 0)
        def _():
            qhat, _, kvhat, _ = _latents(p_ref, qag_ref, kvag_ref)
            qn_s[...] = (qhat * qag_ref[...]).astype(BF16)
            kvn_s[...] = (kvhat * kvag_ref[...]).astype(BF16)

        cos, sin = cos_ref[...], sin_ref[...]
        qhat, _ = _head_norm(_dot(qn_s[...], qb_ref[...]))
        q_ref[...] = _rope(qhat * qg_ref[...], cos, sin).astype(BF16)
        kv = _dot(kvn_s[...], kvb_ref[...])
        khat, _ = _head_norm(jnp.concatenate([kv[:, :QK_NOPE], p_ref[:, ODD_IN_PAD - LANES:]], axis=1))
        k_ref[...] = _rope(khat * kg_ref[...], cos, sin).astype(BF16)
        v_ref[...] = kv[:, QK_NOPE:].astype(BF16)

    qk_spec = BS((None, tm, QK_PAD), lambda i, h: (h, i, 0))
    return _call(body, name="mla_qkv_fwd", grid=(T // tm, HEADS), in_specs=_mla_specs(tm),
                 out_specs=[qk_spec, qk_spec, BS((None, tm, V_DIM), lambda i, h: (h, i, 0))],
                 out_shape=[_sds((HEADS, T, QK_PAD), BF16), _sds((HEADS, T, QK_PAD), BF16), _sds((HEADS, T, V_DIM), BF16)],
                 scratch=[pltpu.VMEM((tm, Q_LORA), BF16), pltpu.VMEM((tm, KV_LORA), BF16)])(
        proj, cos, sin_signed, qa_g, kva_g, q_b, kv_b, q_g, k_g)


def _mla_qkv_bwd(proj, cos, sin_signed, qa_g, kva_g, q_b, kv_b, q_g, k_g, dq, dk, dv, dz_pool, tm):
    T = proj.shape[0]
    n_tiles = T // tm

    def body(p_ref, cos_ref, sin_ref, qag_ref, kvag_ref, qb_ref, kvb_ref, qg_ref, kg_ref, dq_ref, dk_ref, dv_ref, dzp_ref,
             dp_ref, dqb_ref, dkvb_ref, dqg_ref, dkg_ref, dqag_ref, dkvag_ref, qn_s, kvn_s, dqn_s, dkvn_s, dkr_s):
        i, h = pl.program_id(0), pl.program_id(1)

        @pl.when((i == 0) & (h == 0))
        def _():
            for ref in (dqb_ref, dkvb_ref, dqg_ref, dkg_ref, dqag_ref, dkvag_ref):
                ref[...] = jnp.zeros_like(ref)

        @pl.when(h == 0)
        def _():
            qhat, _, kvhat, _ = _latents(p_ref, qag_ref, kvag_ref)
            qn_s[...] = (qhat * qag_ref[...]).astype(BF16)
            kvn_s[...] = (kvhat * kvag_ref[...]).astype(BF16)
            dqn_s[...] = jnp.zeros_like(dqn_s)
            dkvn_s[...] = jnp.zeros_like(dkvn_s)
            dkr_s[...] = jnp.zeros_like(dkr_s)

        cos, sin = cos_ref[...], sin_ref[...]
        qhat, rq = _head_norm(_dot(qn_s[...], qb_ref[...]))
        dqn_head = _rope_transposed(dq_ref[...], cos, sin)
        dqg_ref[...] += jnp.sum(dqn_head * qhat, axis=0, keepdims=True)
        dqh = _head_norm_bwd(dqn_head, qhat, rq, qg_ref[...])
        dqb_ref[h] += _dot(qn_s[...], dqh, "tn")
        dqn_s[...] += _dot(dqh, qb_ref[...], "nt")

        kv = _dot(kvn_s[...], kvb_ref[...])
        khat, rk = _head_norm(jnp.concatenate([kv[:, :QK_NOPE], p_ref[:, ODD_IN_PAD - LANES:]], axis=1))
        dkn_head = _rope_transposed(dk_ref[...], cos, sin)
        dkg_ref[...] += jnp.sum(dkn_head * khat, axis=0, keepdims=True)
        dkf = _head_norm_bwd(dkn_head, khat, rk, kg_ref[...])
        dkr_s[...] += dkf[:, QK_NOPE:]
        dkv = jnp.concatenate([dkf[:, :QK_NOPE], dv_ref[...]], axis=1)
        dkvb_ref[h] += _dot(kvn_s[...], dkv, "tn")
        dkvn_s[...] += _dot(dkv, kvb_ref[...], "nt")

        @pl.when(h == HEADS - 1)
        def _():
            qhat_l, rql, kvhat_l, rkvl = _latents(p_ref, qag_ref, kvag_ref)
            dqn, dkvn = dqn_s[...], dkvn_s[...]
            dqag_ref[...] += jnp.sum(dqn * qhat_l, axis=0, keepdims=True)
            dkvag_ref[...] += jnp.sum(dkvn * kvhat_l, axis=0, keepdims=True)
            dqx, dkvx = dqn * qag_ref[...], dkvn * kvag_ref[...]
            dp_ref[:, :POOL_WIDTH] = dzp_ref[...]
            dp_ref[:, POOL_WIDTH:POOL_WIDTH + Q_LORA] = (
                rql * (dqx - qhat_l * jnp.mean(dqx * qhat_l, axis=-1, keepdims=True))).astype(BF16)
            dp_ref[:, POOL_WIDTH + Q_LORA:ODD_IN_PAD - LANES] = (
                rkvl * (dkvx - kvhat_l * jnp.mean(dkvx * kvhat_l, axis=-1, keepdims=True))).astype(BF16)
            dp_ref[:, ODD_IN_PAD - LANES:] = dkr_s[:, :QK_ROPE].astype(BF16)

    full = lambda shape: BS(shape, lambda i, h: (0,) * len(shape))
    qk_spec = BS((None, tm, QK_PAD), lambda i, h: (h, i, 0))
    return _call(body, name="mla_qkv_bwd", grid=(n_tiles, HEADS),
                 in_specs=_mla_specs(tm) + [qk_spec, qk_spec, BS((None, tm, V_DIM), lambda i, h: (h, i, 0)),
                                            BS((tm, POOL_WIDTH), lambda i, h: (i, 0))],
                 out_specs=[BS((tm, ODD_IN), lambda i, h: (i, 0)), full((HEADS, Q_LORA, QK_PAD)), full((HEADS, KV_LORA, QK_PAD)),
                            full((1, QK_PAD)), full((1, QK_PAD)), full((1, Q_LORA)), full((1, KV_LORA))],
                 out_shape=[_sds((T, ODD_IN), BF16),_sds((HEADS, Q_LORA, QK_PAD), F32), _sds((HEADS, KV_LORA, QK_PAD), F32),
                            _sds((1, QK_PAD), F32), _sds((1, QK_PAD), F32), _sds((1, Q_LORA), F32), _sds((1, KV_LORA), F32)],
                 scratch=[pltpu.VMEM((tm, Q_LORA), BF16), pltpu.VMEM((tm, KV_LORA), BF16), pltpu.VMEM((tm, Q_LORA), F32),
                          pltpu.VMEM((tm, KV_LORA), F32), pltpu.VMEM((tm, LANES), F32)])(
        proj, cos, sin_signed, qa_g, kva_g, q_b, kv_b, q_g, k_g, dq, dk, dv, dz_pool)


def _attn_tile(seq):
    return 512 if seq % 512 == 0 else seq


def _causal_mask(s):
    row = lax.broadcasted_iota(jnp.int32, s.shape, 0)
    col = lax.broadcasted_iota(jnp.int32, s.shape, 1)
    return jnp.where(row >= col, s, NEG_INF)


def _tile(i, t):
    return slice(i * t, (i + 1) * t)


def _flash_fwd(q, k, v, mix, batch, seq):
    t = _attn_tile(seq)
    nq = seq // t

    def body(q_ref, k_ref, v_ref, _, o_ref, lse_ref):
        for qi in range(nq):
            rows, before = _tile(qi, t), slice(0, qi * t)
            qv = q_ref[rows, :]
            s_diag = _causal_mask(_dot(qv, k_ref[rows, :], "nt") * ATTN_SCALE)
            m = jnp.max(s_diag, axis=-1, keepdims=True)
            if qi:
                s_before = _dot(qv, k_ref[before, :], "nt") * ATTN_SCALE
                m = jnp.maximum(m, jnp.max(s_before, axis=-1, keepdims=True))
            p = jnp.exp(s_diag - m)
            l = jnp.sum(p, axis=-1, keepdims=True)
            acc = _dot(p, v_ref[rows, :])
            if qi:
                p = jnp.exp(s_before - m)
                l = l + jnp.sum(p, axis=-1, keepdims=True)
                acc = acc + _dot(p, v_ref[before, :])
            o_ref[rows, :] = (acc / l).astype(BF16)
            lse_ref[rows, :] = jnp.broadcast_to(m + jnp.log(l), (t, LANES))

    T = batch * seq
    whole = lambda w: BS((None, seq, w), lambda b, h: (h, b, 0))
    return _call(body, name="flash_fwd", grid=(batch, HEADS),
                 in_specs=[whole(QK_PAD), whole(QK_PAD), whole(V_DIM), pl.BlockSpec(memory_space=pl.ANY)],
                 out_specs=[BS((seq, V_DIM), lambda b, h: (b, POOL_WIDTH // V_DIM + h)), whole(LANES)],
                 out_shape=[_sds((T, D_MODEL), BF16), _sds((HEADS, T, LANES), F32)],
                 aliases={3: 0})(q, k, v, mix)


def _flash_bwd(q, k, v, dmix, mix, lse, batch, seq):
    t = _attn_tile(seq)
    nq = seq // t

    def body(q_ref, k_ref, v_ref, do_ref, o_ref, lse_ref, dq_ref, dk_ref, dv_ref, delta_s):
        dq_ref[...] = jnp.zeros_like(dq_ref)
        dk_ref[...] = jnp.zeros_like(dk_ref)
        dv_ref[...] = jnp.zeros_like(dv_ref)
        for qi in range(nq):
            rows = _tile(qi, t)
            delta_s[qi] = jnp.sum(do_ref[rows, :].astype(F32) * o_ref[rows, :].astype(F32), axis=-1, keepdims=True)
        for kb in range(nq):
            keys = _tile(kb, t)
            for qi in range(kb, nq):
                rows = _tile(qi, t)
                qv, kk, do = q_ref[rows, :], k_ref[keys, :], do_ref[rows, :]
                s = _dot(qv, kk, "nt") * ATTN_SCALE
                if kb == qi:
                    s = _causal_mask(s)
                p = jnp.exp(s - lse_ref[rows, 0:1])
                dv_ref[keys, :] += _dot(p, do, "tn")
                ds = p * (_dot(do, v_ref[keys, :], "nt") - delta_s[qi]) * ATTN_SCALE
                dq_ref[rows, :] += _dot(ds, kk)
                dk_ref[keys, :] += _dot(ds, qv, "tn")

    T = batch * seq
    whole = lambda w: BS((None, seq, w), lambda b, h: (h, b, 0))
    head_cols = BS((seq, V_DIM), lambda b, h: (b, POOL_WIDTH // V_DIM + h))
    return _call(body, name="flash_bwd", grid=(batch, HEADS),
                 in_specs=[whole(QK_PAD), whole(QK_PAD), whole(V_DIM), head_cols, head_cols, whole(LANES)],
                 out_specs=[whole(QK_PAD), whole(QK_PAD), whole(V_DIM)],
                 out_shape=[_sds((HEADS, T, QK_PAD), F32), _sds((HEADS, T, QK_PAD), F32), _sds((HEADS, T, V_DIM), F32)],
                 scratch=[pltpu.VMEM((nq, t, 1), F32)])(q, k, v, dmix, mix, lse)


def _adamw_math(w, g, m, v):
    m = ADAM_B1 * m + (1.0 - ADAM_B1) * g
    v = ADAM_B2 * v + (1.0 - ADAM_B2) * (g * g)
    m_hat = m / (1.0 - ADAM_B1 ** ADAM_STEP)
    v_hat = v / (1.0 - ADAM_B2 ** ADAM_STEP)
    return -ADAM_LR * (m_hat / (jnp.sqrt(v_hat) + ADAM_EPS) + ADAM_WD * w), m, v


def _adamw(name, w, g, m, v, l=0, prev=()):
    L, R, C = w.shape
    tr = 256 if R % 256 == 0 else R

    def body(w_ref, g_ref, m_ref, v_ref, *rest):
        go_ref, d_ref, mo_ref, vo_ref, token = rest[-5:]
        gv = g_ref[...]
        d_ref[...], mo_ref[...], vo_ref[...] = _adamw_math(w_ref[...], gv, m_ref[...], v_ref[...])
        go_ref[...] = gv
        token[...] = jnp.zeros_like(token)

    layer = BS((None, tr, C), lambda i: (l, i, 0))
    return _call(body, name=f"{name}_{l}", grid=(R // tr,),
                 in_specs=[layer, BS((tr, C), lambda i: (i, 0)), layer, layer] + [pl.BlockSpec(memory_space=pl.ANY)] * len(prev),
                 out_specs=[layer] * 4 + [BS((SUBLANES, LANES), lambda i: (0, 0))],
                 out_shape=[_sds((L, R, C), F32)] * 4 + [_sds((SUBLANES, LANES), F32)],
                 aliases={4 + n: n for n in range(len(prev))})(w, g, m, v, *prev)


def _place():
    x, y, c = lax.axis_index("x"), lax.axis_index("y"), lax.axis_index("c")
    other_chips = [(1 - x, y), (x, 1 - y), (1 - x, 1 - y)]
    return x, y, c, other_chips


def _remote(src, dst, send_sem, recv_sem, dev):
    return pltpu.make_async_remote_copy(src_ref=src, dst_ref=dst, send_sem=send_sem, recv_sem=recv_sem,
                                        device_id=dev, device_id_type=MESH)


def _prefetch_call(body, *, name, grid, in_specs, out_specs, out_shape):
    grid_spec = pltpu.PrefetchScalarGridSpec(num_scalar_prefetch=1, grid=grid, in_specs=in_specs, out_specs=out_specs)
    params = pltpu.CompilerParams(vmem_limit_bytes=VMEM_LIMIT_V7X, dimension_semantics=("arbitrary",) * len(grid))
    return pl.pallas_call(body, name=name, grid_spec=grid_spec, out_shape=out_shape, compiler_params=params)


def _row_tile(rows):
    return 256 if rows % 256 == 0 else rows


def _cast_place(name, w, layer, chip, after=()):
    _, _, rows, C = w.shape
    tr = _row_tile(rows)

    def body(chip_ref, w_ref, *rest):
        rest[-1][...] = w_ref[...].astype(BF16)

    return _prefetch_call(body, name=name, grid=(2, rows // tr),
                          in_specs=[BS((None, None, tr, C), lambda h, i, chip_ref: (layer, h, i, 0))]
                          + [pl.BlockSpec(memory_space=pl.ANY)] * len(after),
                          out_specs=BS((None, None, tr, C), lambda h, i, chip_ref: (chip_ref[0], h, i, 0)),
                          out_shape=pltpu.HBM((N_CHIPS, 2, rows, C), BF16))(chip, w, *after)


SEM = pl.BlockSpec(memory_space=pltpu.SEMAPHORE)


def _split_copy_call(body, *, name, in_specs, out_specs, out_shape, aliases):
    return pl.pallas_call(body, name=name, in_specs=in_specs, out_specs=out_specs, out_shape=out_shape,
                          input_output_aliases=aliases,
                          compiler_params=pltpu.CompilerParams(has_side_effects=pltpu.SideEffectType.DATAFLOW_SIDE_EFFECTING))


def _hbm(arrays):
    return [pltpu.with_memory_space_constraint(a, pltpu.HBM) for a in arrays]


def _gather_send(name, gs, groups, after):
    n = len(gs)

    def body(*refs):
        g, sems, token = refs[:n], refs[n + len(after):n + len(after) + 2 * len(groups)], refs[-1]
        x, y, c, chips = _place()
        me = 2 * x + y
        for gi, members in enumerate(groups):
            for a, i in enumerate(members):
                for k, (px, py) in enumerate(chips):
                    _remote(g[i].at[me, c], g[i].at[me, c], sems[2 * gi].at[3 * a + k], sems[2 * gi + 1].at[3 * a + k],
                            (px, py, c)).start()
        token[...] = jnp.zeros_like(token)

    sem_shapes = [pltpu.SemaphoreType.DMA((3 * len(members),)) for members in groups for _ in range(2)]
    out = _split_copy_call(body, name=name, in_specs=[HBM] * n + [pl.BlockSpec(memory_space=pl.ANY)] * len(after),
                           out_specs=[SEM] * len(sem_shapes) + [HBM] * n + [VMEM],
                           out_shape=sem_shapes + [pltpu.HBM(a.shape, a.dtype) for a in gs] + [_sds((SUBLANES, LANES), F32)],
                           aliases={i: len(sem_shapes) + i for i in range(n)})(*_hbm(gs), *after)
    return out[:len(sem_shapes)], out[len(sem_shapes):-1], out[-1]


def _gather_wait(name, gs, send_sems, recv_sems, after):
    n = len(gs)

    def body(*refs):
        g, ssem, rsem = refs[:n], refs[n], refs[n + 1]
        x, y, c, chips = _place()
        me = 2 * x + y
        for a in range(n):
            for k, (px, py) in enumerate(chips):
                landed = g[a].at[2 * px + py, c]
                cp = _remote(g[a].at[me, c], landed, ssem.at[3 * a + k], rsem.at[3 * a + k], (px, py, c))
                cp.wait_recv()
                cp.wait_send()

    return _split_copy_call(body, name=name, in_specs=[HBM] * n + [SEM, SEM] + [pl.BlockSpec(memory_space=pl.ANY)] * len(after),
                            out_specs=[HBM] * n, out_shape=[pltpu.HBM(a.shape, a.dtype) for a in gs],
                            aliases={i: i for i in range(n)})(*gs, send_sems, recv_sems, *after)


def _gather_pass(name, gs):
    n = len(gs)

    def body(*refs):
        g, send_sems, recv_sems = refs[n:2 * n], refs[-2], refs[-1]
        x, y, c, chips = _place()
        sibling = (x, y, 1 - c)
        passed = [_remote(g[i].at[2 * px + py, c], g[i].at[2 * px + py, c], send_sems.at[3 * i + k], recv_sems.at[3 * i + k], sibling)
                  for i in range(n) for k, (px, py) in enumerate(chips)]
        for cp in passed:
            cp.start()
        for i in range(n):
            for k, (px, py) in enumerate(chips):
                theirs = g[i].at[2 * px + py, 1 - c]
                _remote(theirs, theirs, send_sems.at[3 * i + k], recv_sems.at[3 * i + k], sibling).wait_recv()
        for cp in passed:
            cp.wait_send()

    return _call(body, name=name, in_specs=[HBM] * n, out_specs=[HBM] * n, out_shape=[_sds(a.shape, a.dtype) for a in gs],
                 aliases={i: i for i in range(n)},
                 scratch=[pltpu.SemaphoreType.DMA((3 * n,)), pltpu.SemaphoreType.DMA((3 * n,))])(*gs)


def _scatter_send(name, ps):
    n = len(ps)

    def body(*refs):
        p, r, ssem, rsem, token = refs[:n], refs[n:2 * n], refs[2 * n], refs[2 * n + 1], refs[-1]
        x, y, c, chips = _place()
        for i in range(n):
            for k, (px, py) in enumerate(chips):
                _remote(p[i].at[2 * px + py], r[i].at[k], ssem.at[3 * i + k], rsem.at[3 * i + k], (px, py, c)).start()
        token[...] = jnp.zeros_like(token)

    lands = [lax.empty((N_CHIPS - 1,) + a.shape[1:], a.dtype) for a in ps]
    sem = pltpu.SemaphoreType.DMA((3 * n,))
    out = _split_copy_call(body, name=name, in_specs=[HBM] * (2 * n), out_specs=[SEM, SEM] + [HBM] * (2 * n) + [VMEM],
                           out_shape=[sem, sem] + [pltpu.HBM(a.shape, a.dtype) for a in list(ps) + lands] + [_sds((SUBLANES, LANES), F32)],
                           aliases={i: 2 + i for i in range(2 * n)})(*_hbm(list(ps) + lands))
    return out[0], out[1], out[2:2 + n], out[2 + n:2 + 2 * n], out[-1]


def _scatter_wait(name, ps, lands, send_sems, recv_sems, after):
    n = len(ps)

    def body(*refs):
        p, r, ssem, rsem = refs[:n], refs[n:2 * n], refs[2 * n], refs[2 * n + 1]
        x, y, c, chips = _place()
        for i in range(n):
            for k, (px, py) in enumerate(chips):
                cp = _remote(p[i].at[2 * px + py], r[i].at[k], ssem.at[3 * i + k], rsem.at[3 * i + k], (px, py, c))
                cp.wait_recv()
                cp.wait_send()

    out = _split_copy_call(body, name=name, in_specs=[HBM] * (2 * n) + [SEM, SEM] + [pl.BlockSpec(memory_space=pl.ANY)] * len(after),
                           out_specs=[HBM] * (2 * n), out_shape=[pltpu.HBM(a.shape, a.dtype) for a in list(ps) + list(lands)],
                           aliases={i: i for i in range(2 * n)})(*ps, *lands, send_sems, recv_sems, *after)
    return out[:n], out[n:]


def _exchange_send(name, gs):
    n = len(gs)

    def body(*refs):
        g, r, ssem, rsem, token = refs[:n], refs[n:2 * n], refs[2 * n], refs[2 * n + 1], refs[-1]
        x, y, c, _ = _place()
        for i in range(n):
            _remote(g[i].at[:, 1 - c], r[i], ssem.at[i], rsem.at[i], (x, y, 1 - c)).start()
        token[...] = jnp.zeros_like(token)

    lands = [lax.empty((a.shape[0],) + a.shape[2:], a.dtype) for a in gs]
    sem = pltpu.SemaphoreType.DMA((n,))
    out = _split_copy_call(body, name=name, in_specs=[HBM] * (2 * n), out_specs=[SEM, SEM] + [HBM] * (2 * n) + [VMEM],
                           out_shape=[sem, sem] + [pltpu.HBM(a.shape, a.dtype) for a in list(gs) + lands] + [_sds((SUBLANES, LANES), F32)],
                           aliases={i: 2 + i for i in range(2 * n)})(*_hbm(list(gs) + lands))
    return out[0], out[1], out[2:2 + n], out[2 + n:2 + 2 * n], out[-1]


def _exchange_wait(name, gs, lands, send_sems, recv_sems, after):
    n = len(gs)

    def body(*refs):
        g, r, ssem, rsem = refs[:n], refs[n:2 * n], refs[2 * n], refs[2 * n + 1]
        x, y, c, _ = _place()
        for i in range(n):
            cp = _remote(g[i].at[:, 1 - c], r[i], ssem.at[i], rsem.at[i], (x, y, 1 - c))
            cp.wait_recv()
            cp.wait_send()

    out = _split_copy_call(body, name=name, in_specs=[HBM] * (2 * n) + [SEM, SEM] + [pl.BlockSpec(memory_space=pl.ANY)] * len(after),
                           out_specs=[HBM] * (2 * n), out_shape=[pltpu.HBM(a.shape, a.dtype) for a in list(gs) + list(lands)],
                           aliases={i: i for i in range(2 * n)})(*gs, *lands, send_sems, recv_sems, *after)
    return out[:n], out[n:]


def _sibling_share(name, fs):
    n = len(fs)

    def body(*refs):
        f, send_sems, recv_sems = refs[n:2 * n], refs[-2], refs[-1]
        x, y, c, _ = _place()
        sends = [_remote(f[i].at[c], f[i].at[c], send_sems.at[i], recv_sems.at[i], (x, y, 1 - c)) for i in range(n)]
        for cp in sends:
            cp.start()
        for i in range(n):
            theirs = f[i].at[1 - c]
            _remote(theirs, theirs, send_sems.at[i], recv_sems.at[i], (x, y, 1 - c)).wait_recv()
        for cp in sends:
            cp.wait_send()

    return _call(body, name=name, in_specs=[HBM] * n, out_specs=[HBM] * n,
                 out_shape=[_sds(a.shape, a.dtype) for a in fs], aliases={i: i for i in range(n)},
                 scratch=[pltpu.SemaphoreType.DMA((n,)), pltpu.SemaphoreType.DMA((n,))])(*fs)


def _all_reduce_small(name, v):
    rows = v.shape[0] // 2
    halves = (2, rows, LANES)

    def body(v_ref, o_ref, from_sibling, chip_sums, send_sems, recv_sems):
        x, y, c, chips = _place()
        me, sibling = 2 * x + y, (x, y, 1 - c)
        swap = _remote(v_ref.at[1 - c], from_sibling, send_sems.at[0], recv_sems.at[0], sibling)
        swap.start()
        swap.wait()
        chip_sums[me] = v_ref[c] + from_sibling[...]
        sends = [_remote(chip_sums.at[me], chip_sums.at[me], send_sems.at[1 + k], recv_sems.at[1 + k], (px, py, c))
                 for k, (px, py) in enumerate(chips)]
        for cp in sends:
            cp.start()
        for k, (px, py) in enumerate(chips):
            theirs = chip_sums.at[2 * px + py]
            _remote(theirs, theirs, send_sems.at[1 + k], recv_sems.at[1 + k], (px, py, c)).wait_recv()
        for cp in sends:
            cp.wait_send()
        acc = chip_sums[0]
        for j in range(1, N_CHIPS):
            acc = acc + chip_sums[j]
        o_ref[c] = acc
        share = _remote(o_ref.at[c], o_ref.at[c], send_sems.at[4], recv_sems.at[4], sibling)
        share.start()
        share.wait_send()
        _remote(o_ref.at[1 - c], o_ref.at[1 - c], send_sems.at[4], recv_sems.at[4], sibling).wait_recv()

    return _call(body, name=name, in_specs=[VMEM], out_specs=VMEM, out_shape=_sds(halves, F32),
                 scratch=[pltpu.VMEM((rows, LANES), F32), pltpu.VMEM((N_CHIPS, rows, LANES), F32),
                          pltpu.SemaphoreType.DMA((5,)), pltpu.SemaphoreType.DMA((5,))])(v.reshape(halves)).reshape(v.shape)


def _add_halves(name, g, r, c):
    _, _, rows, C = g.shape
    tr = _row_tile(rows)

    def body(c_ref, g_ref, r_ref, o_ref):
        o_ref[...] = (g_ref[...] + r_ref[...]).astype(BF16)

    spec = BS((None, tr, C), lambda j, i, c_ref: (j, i, 0))
    return _prefetch_call(body, name=name, grid=(N_CHIPS, rows // tr),
                          in_specs=[BS((None, None, tr, C), lambda j, i, c_ref: (j, c_ref[0], i, 0)), spec], out_specs=spec,
                          out_shape=pltpu.HBM((N_CHIPS, rows, C), BF16))(c, g, r)


def _sum_partials(name, p, r, chip_c):
    _, rows, C = p.shape
    tr = _row_tile(rows)

    def body(s_ref, p_ref, r_ref, o_ref):
        acc = p_ref[...].astype(F32)
        for k in range(N_CHIPS - 1):
            acc = acc + r_ref[k].astype(F32)
        o_ref[...] = acc

    return _prefetch_call(body, name=name, grid=(rows // tr,),
                          in_specs=[BS((None, tr, C), lambda i, s: (s[0], i, 0)), BS((N_CHIPS - 1, tr, C), lambda i, s: (0, i, 0))],
                          out_specs=BS((None, tr, C), lambda i, s: (s[1], i, 0)), out_shape=pltpu.HBM((2, rows, C), F32))(chip_c, p, r)


_SHARDED = ("even_w_in", "even_w_out", "odd_w_in", "q_b", "kv_b", "odd_w_out", "ffn_w_gate", "ffn_w_up", "ffn_w_down")
_REPLICATED = ("mix_norm", "ffn_norm", "sg_ln_g", "sg_w_s", "sg_b_s", "pool_w", "q_norm", "k_norm")
_SMALL_SHARDED = ("sc_conv_w", "pool_scale", "q_a_norm", "kv_a_norm")
_WEIGHTS = ("mix_norm", "ffn_norm", "even_w_in", "sg_ln_g", "sg_w_s", "sg_b_s", "sc_conv_w", "even_w_out", "odd_w_in", "pool_w",
            "pool_scale", "q_a_norm", "q_b", "kv_a_norm", "kv_b", "q_norm", "k_norm", "odd_w_out", "ffn_w_gate", "ffn_w_up",
            "ffn_w_down")


def _pad_rows(flat, width, align):
    n = flat.shape[0]
    rows = -(-n // (width * align)) * align
    return jnp.pad(flat, (0, rows * width - n)).reshape(rows, width)


_GROUPS = {"even": ("even_w_in", "even_w_out"),
           "ffn0": ("ffn_w_gate0", "ffn_w_up0", "ffn_w_down0"),
           "odd": ("odd_w_in", "q_b", "kv_b", "odd_w_out"),
           "ffn1": ("ffn_w_gate1", "ffn_w_up1", "ffn_w_down1")}


def _place_shards(shards, names, chip, after):
    placed = []
    for n in names:
        weight, layer = (n[:-1], int(n[-1])) if n[-1].isdigit() else (n, 0)
        a = shards[weight]
        placed.append(_cast_place(f"place_{n}", a.reshape(a.shape[0], 2, a.shape[1] // 2, a.shape[2]), layer, chip, after))
    return placed


def _whole_weights(gathered):
    out = {n: a.reshape(N_CHIPS, -1, a.shape[-1]) for n, a in gathered.items()}
    for n in ("q_b", "kv_b"):
        if n in out:
            out[n] = out[n].transpose(1, 0, 2).reshape(out[n].shape[1], -1)
    for n in ("even_w_out", "odd_w_in", "odd_w_out"):
        if n in out:
            out[n] = out[n].reshape(-1, out[n].shape[-1])
    return out


def _forward_backward(x, positions, target, small, fetch, emit, advance):
    batch, seq, _ = x.shape
    T = batch * seq
    tm = _token_tile(seq)
    x0 = x.reshape(T, D_MODEL)

    inv_freq = ROPE_THETA ** (-jnp.arange(0, QK_ROPE, 2, dtype=F32) / QK_ROPE)
    ang = (positions.astype(F32)[..., None] * inv_freq).reshape(T, QK_ROPE // 2)
    cos, sin = jnp.cos(ang), jnp.sin(ang)
    pad = jnp.zeros((T, LANES - QK_ROPE), F32)
    cos_t = jnp.concatenate([cos, cos, pad], axis=1)
    sin_t = jnp.concatenate([-sin, sin, pad], axis=1)

    tril = jnp.tril(jnp.ones((SG_CHUNK, SG_CHUNK), bool))
    w_tril = jnp.where(tril[None], small["sg_w_s"][0], 0.0).astype(BF16)
    b_lanes = jnp.broadcast_to(small["sg_b_s"][0][:, :, None], (SG_HEADS, SG_CHUNK, SG_DIM))
    conv_w = jnp.pad(small["sc_conv_w"][0], ((0, SUBLANES - CONV_TAPS), (0, 0)))
    ln_g = small["sg_ln_g"]
    pool_diag = jnp.zeros((POOL_WIDTH, POOL_WIDTH), F32)
    for g in range(len(POOL_WINDOWS)):
        pool_diag = pool_diag.at[POOL_DIM * g:POOL_DIM * (g + 1), POOL_DIM * g:POOL_DIM * (g + 1)].set(small["pool_w"][0, g])
    pool_diag = pool_diag.astype(BF16)
    pool_scale = small["pool_scale"]
    q_g = jnp.pad(small["q_norm"], ((0, 0), (0, QK_PAD - QK_DIM)))
    k_g = jnp.pad(small["k_norm"], ((0, 0), (0, QK_PAD - QK_DIM)))
    qa_g, kva_g = small["q_a_norm"], small["kv_a_norm"]
    in_shard = EVEN_IN // N_CHIPS

    def ffn_weights(l, w):
        return w[f"ffn_w_gate{l}"], w[f"ffn_w_up{l}"], w[f"ffn_w_down{l}"]

    W = fetch("even", ())
    w_in_even = W["even_w_in"]
    h0 = _rmsnorm_fwd("mix0_norm", x0, small["mix_norm"][0], tm)
    tb = _big_tile(T)
    proj0 = _matmul("even_in", "nn", [(h0, w_in_even)],
                    [(BS((tb, D_MODEL), lambda i, j, k: (i, 0)), BS((None, D_MODEL, in_shard), lambda i, j, k: (j, 0, 0)))],
                    (T // tb, N_CHIPS, 1), _sds((T, EVEN_IN), F32), BS((tb, in_shard), lambda i, j, k: (i, j)), (tb, in_shard))
    mix0 = _even_mixer_fwd(proj0, ln_g, w_tril, b_lanes, conv_w, seq, tm)
    w_out_even = W["even_w_out"]
    x1, h1 = _mm("even_out", "nn", mix0, w_out_even, F32, tk=1024, add=x0, fused=_norm_tail(small["ffn_norm"][0], T, tb))
    ffn0 = ffn_weights(0, fetch("ffn0", (x1,)))
    (x2, h2), ffn0_saved = _ffn_fwd(0, x1, h1, *ffn0, _norm_tail(small["mix_norm"][1], T, tb))
    W = fetch("odd", (x2,))
    w_in_odd = jnp.pad(W["odd_w_in"], ((0, 0), (0, ODD_IN_PAD - ODD_IN)))
    q_b = jnp.pad(W["q_b"].reshape(Q_LORA, HEADS, QK_DIM).transpose(1, 0, 2), ((0, 0), (0, 0), (0, QK_PAD - QK_DIM)))
    kv_b = W["kv_b"].reshape(KV_LORA, HEADS, QK_NOPE + V_DIM).transpose(1, 0, 2)
    proj1 = _mm("odd_in", "nn", h2, w_in_odd, F32, tk=1024)
    mix1 = _pool_fwd(proj1, pool_diag, pool_scale, seq, tm)
    q, k, v = _mla_qkv_fwd(proj1, cos_t, sin_t, qa_g, kva_g, q_b, kv_b, q_g, k_g, tm)
    mix1, lse = _flash_fwd(q, k, v, mix1, batch, seq)
    x3, h3 = _mm("odd_out", "nn", mix1, W["odd_w_out"], F32, tk=1024, add=x2, fused=_norm_tail(small["ffn_norm"][1], T, tb))
    ffn1 = ffn_weights(1, fetch("ffn1", (x3,)))
    (dy, sq), ffn1_saved = _ffn_fwd(1, x3, h3, *ffn1, _loss_tail(target.reshape(T, D_MODEL), tb))

    G = {}
    dx3, dffn_g1 = _ffn_bwd(1, x3, small["ffn_norm"][1], *ffn1, ffn1_saved, dy, emit)
    dmix1 = _mm("odd_out_dx", "nt", dx3, W["odd_w_out"], BF16, tk=1024, after=advance(dx3))
    dw_out_odd = _mm("odd_out_dw", "tn", mix1, dx3, F32, hbm_out=True)
    dq, dk, dv = _flash_bwd(q, k, v, dmix1, mix1, lse, batch, seq)
    dz_pool, dpool_diag, G["pool_scale"] = _pool_bwd(proj1, dmix1, pool_diag, pool_scale, seq, tm)
    dproj1, dq_b, dkv_b, dq_g, dk_g, G["q_a_norm"], G["kv_a_norm"] = _mla_qkv_bwd(
        proj1, cos_t, sin_t, qa_g, kva_g, q_b, kv_b, q_g, k_g, dq, dk, dv, dz_pool, tm)
    G["pool_w"] = jnp.stack([dpool_diag[POOL_DIM * g:POOL_DIM * (g + 1), POOL_DIM * g:POOL_DIM * (g + 1)]
                             for g in range(len(POOL_WINDOWS))])[None]
    G["q_norm"], G["k_norm"] = dq_g[:, :QK_DIM], dk_g[:, :QK_DIM]
    dw_in_odd = _mm("odd_in_dw", "tn", h2, dproj1, F32, tn=ODD_IN, hbm_out=True)

    def shard_major(g, cols):
        return g.reshape(g.shape[0], N_CHIPS, cols).transpose(1, 0, 2)

    behind = emit("odd", {"odd_w_in": dw_in_odd.reshape(N_CHIPS, -1, ODD_IN),
                          "q_b": shard_major(dq_b[:, :, :QK_DIM].transpose(1, 0, 2).reshape(Q_LORA, HEADS * QK_DIM), HEADS * QK_DIM // N_CHIPS),
                          "kv_b": shard_major(dkv_b.transpose(1, 0, 2).reshape(KV_LORA, HEADS * (QK_NOPE + V_DIM)),
                                              HEADS * (QK_NOPE + V_DIM) // N_CHIPS),
                          "odd_w_out": dw_out_odd.reshape(N_CHIPS, -1, D_MODEL)})
    dx2, dmix_g1 = _mm("odd_in_dx", "nt", dproj1, W["odd_w_in"], F32, tk=ODD_IN, after=behind,
                       fused=_norm_bwd_tail(x2, small["mix_norm"][1], dx3, tb))
    dx1, dffn_g0 = _ffn_bwd(0, x1, small["ffn_norm"][0], *ffn0, ffn0_saved, dx2, emit, after=advance(dx2))
    dmix0 = _mm("even_out_dx", "nt", dx1, w_out_even, F32, tk=1024, after=advance(dx1))
    dw_out_even = _mm("even_out_dw", "tn", mix0, dx1, F32, hbm_out=True)
    dproj0, dw_s, db_lanes, G["sg_ln_g"], dconv = _even_mixer_bwd(proj0, dmix0, ln_g, w_tril, b_lanes, conv_w, seq, tm)
    G["sg_w_s"] = dw_s[None]
    G["sg_b_s"] = jnp.sum(db_lanes, axis=-1)[None]
    G["sc_conv_w"] = dconv[None, :CONV_TAPS]
    tail, shapes, specs = _norm_bwd_tail(x0, small["mix_norm"][0], dx1, tb)
    dx0, dmix_g0 = _matmul("even_in_dx", "nt", [(dproj0, w_in_even)],
                           [(BS((tb, in_shard), lambda i, j, k: (i, k)), BS((None, D_MODEL, in_shard), lambda i, j, k: (k, 0, 0)))],
                           (T // tb, 1, N_CHIPS), shapes, specs, (tb, D_MODEL), tail=tail)
    tk = min(512, T)
    dw_in_even = _grad_shards(
        "even_in_dw", h0, dproj0, BS((tk, D_MODEL), lambda k: (k, 0)), BS((tk, EVEN_IN), lambda k: (k, 0)),
        lambda a_ref, b_ref, j: (a_ref[...], b_ref[:, in_shard * j:in_shard * (j + 1)]), (N_CHIPS, D_MODEL, in_shard), T // tk)
    emit("even", {"even_w_in": dw_in_even, "even_w_out": dw_out_even.reshape(N_CHIPS, -1, D_MODEL)})
    G["mix_norm"] = jnp.concatenate([dmix_g0, dmix_g1], axis=0)
    G["ffn_norm"] = jnp.concatenate([dffn_g0, dffn_g1], axis=0)
    return sq[0, 0], dx0.reshape(batch, seq, D_MODEL), G


def _small_vector(parts, names):
    flat = jnp.concatenate([parts[n].astype(F32).reshape(-1) for n in names])
    return _pad_rows(flat, LANES, 2 * SUBLANES)


def _split_small(vec, like, names):
    out, off, flat = {}, 0, vec.reshape(-1)
    for n in names:
        size = math.prod(like[n].shape)
        out[n] = flat[off:off + size].reshape(like[n].shape)
        off += size
    return out


def _whole_shape(a):
    return a.shape[:-1] + (a.shape[-1] * N_CHIPS,)


def kernel(x, positions, mix_norm, ffn_norm, even_w_in, sg_ln_g, sg_w_s, sg_b_s, sc_conv_w, even_w_out, odd_w_in, pool_w, pool_scale, q_a_norm, q_b, kv_a_norm, kv_b, q_norm, k_norm, odd_w_out, ffn_w_gate, ffn_w_up, ffn_w_down, loss_target, m_mix_norm, m_ffn_norm, m_even_w_in, m_sg_ln_g, m_sg_w_s, m_sg_b_s, m_sc_conv_w, m_even_w_out, m_odd_w_in, m_pool_w, m_pool_scale, m_q_a_norm, m_q_b, m_kv_a_norm, m_kv_b, m_q_norm, m_k_norm, m_odd_w_out, m_ffn_w_gate, m_ffn_w_up, m_ffn_w_down, v_mix_norm, v_ffn_norm, v_even_w_in, v_sg_ln_g, v_sg_w_s, v_sg_b_s, v_sc_conv_w, v_even_w_out, v_odd_w_in, v_pool_w, v_pool_scale, v_q_a_norm, v_q_b, v_kv_a_norm, v_kv_b, v_q_norm, v_k_norm, v_odd_w_out, v_ffn_w_gate, v_ffn_w_up, v_ffn_w_down):
    args = dict(locals())
    w = {n: args[n] for n in _WEIGHTS}
    m = {n: args["m_" + n] for n in _WEIGHTS}
    v = {n: args["v_" + n] for n in _WEIGHTS}
    cx, cy, cc = lax.axis_index("x"), lax.axis_index("y"), lax.axis_index("c")
    chip = 2 * cx + cy
    transposed = ("ffn_w_gate", "ffn_w_up")
    for n in transposed:
        w[n], m[n], v[n] = (jnp.swapaxes(t[n], 1, 2) for t in (w, m, v))

    chip_arr = chip.astype(jnp.int32).reshape(1)
    c_arr = cc.astype(jnp.int32).reshape(1)
    group_names = list(_GROUPS)
    placed = {}
    for n in _SMALL_SHARDED:
        a = w[n]
        whole = jnp.zeros(a.shape[:-1] + (N_CHIPS, a.shape[-1]), F32)
        whole = lax.dynamic_update_slice_in_dim(whole, a[..., None, :], chip, axis=a.ndim - 1)
        placed[n] = jnp.where(cc == 0, whole, 0.0).reshape(_whole_shape(a))
    small_whole = _all_reduce_small("gather_small_weights", _small_vector(placed, _SMALL_SHARDED))
    small = dict({n: w[n] for n in _REPLICATED}, **_split_small(small_whole, placed, _SMALL_SHARDED))

    first, rest = list(_GROUPS[group_names[0]]), [n for g in group_names[1:] for n in _GROUPS[g]]
    sems_first, flight_first, token = _gather_send("gather_send_first", _place_shards(w, first, chip_arr, (small_whole,)),
                                                   [list(range(len(first)))], (small_whole,))
    sems_rest, flight_rest, all_sent = _gather_send("gather_send_rest", _place_shards(w, rest, chip_arr, (token,)),
                                                    [[rest.index(n) for n in _GROUPS[g]] for g in group_names[1:]], ())
    sems = list(sems_first) + list(sems_rest)
    in_flight = dict(zip(first + rest, list(flight_first) + list(flight_rest)))

    def fetch(group, after):
        gi, members = group_names.index(group), _GROUPS[group]
        after = after if gi else (all_sent,)
        landed = _gather_wait(f"gather_wait_{group}", [in_flight[n] for n in members], sems[2 * gi], sems[2 * gi + 1], after)
        return _whole_weights(dict(zip(members, _gather_pass(f"gather_pass_{group}", landed))))

    swapping, pending, arrived = [], [], {}

    def settle(after):
        names, ps, lands, send_sems, recv_sems = pending.pop()
        ps, lands = _scatter_wait(f"scatter_wait_{names[0]}", ps, lands, send_sems, recv_sems, after)
        arrived.update({n: (p, r) for n, p, r in zip(names, ps, lands)})

    def emit(group, grads):
        names = _GROUPS[group]
        halves = [grads[n].reshape(N_CHIPS, 2, grads[n].shape[1] // 2, grads[n].shape[2]) for n in names]
        send_sems, recv_sems, halves, lands, token = _exchange_send(f"exchange_send_{group}", halves)
        swapping.append((group, halves, lands, send_sems, recv_sems))
        return (token,)

    def advance(done):
        done = done if isinstance(done, tuple) else (done,)
        group, halves, lands, send_sems, recv_sems = swapping.pop()
        names = _GROUPS[group]
        halves, lands = _exchange_wait(f"exchange_wait_{group}", halves, lands, send_sems, recv_sems, done)
        partial = [_add_halves(f"add_{n}", g, r, c_arr) for n, g, r in zip(names, halves, lands)]
        if pending:
            settle(done)
        send_sems, recv_sems, ps, lands, token = _scatter_send(f"scatter_send_{group}", partial)
        pending.append((names, ps, lands, send_sems, recv_sems))
        return (token,)

    sq, grad_x, G = _forward_backward(x, positions, loss_target, small, fetch, emit, advance)
    loss = lax.psum(0.5 * sq / D_MODEL, ("x", "y", "c"))

    small_names = _REPLICATED + _SMALL_SHARDED
    summed = _split_small(_all_reduce_small("reduce_small_grads", _small_vector(G, small_names)), G, small_names)
    grads = {n: summed[n] for n in _REPLICATED}
    for n in _SMALL_SHARDED:
        a = w[n]
        grads[n] = lax.dynamic_slice_in_dim(summed[n].reshape(a.shape[:-1] + (N_CHIPS, a.shape[-1])), chip, 1,
                                            axis=a.ndim - 1).reshape(a.shape)

    chip_c = jnp.stack([chip, cc]).astype(jnp.int32)
    out = {}

    def finish(group):
        names, tokens = _GROUPS[group], []
        sums = [_sum_partials(f"sum_{n}", *arrived[n], chip_c) for n in names]
        for n, f in zip(names, _sibling_share(f"grad_share_{group}", sums)):
            weight, layer = (n[:-1], int(n[-1])) if n[-1].isdigit() else (n, 0)
            *out[weight], token = _adamw(f"adamw_{weight}", w[weight], f.reshape(-1, f.shape[-1]), m[weight], v[weight], layer,
                                         out.get(weight, ()))
            tokens.append(token)
        return tuple(tokens)

    advance(finish(group_names[3]) + finish(group_names[2]))
    settle(finish(group_names[1]))
    finish(group_names[0])
    packed = [_small_vector(d, small_names) for d in (w, grads, m, v)]
    res = _adamw("adamw_small", packed[0][None], packed[1], packed[2][None], packed[3][None])
    delta_s, m_s, v_s = (_split_small(r, w, small_names) for r in res[1:4])
    for n in small_names:
        out[n] = (grads[n], delta_s[n], m_s[n], v_s[n])
    for n in transposed:
        out[n] = tuple(jnp.swapaxes(t, 1, 2) for t in out[n])

    return (loss, grad_x, *[out[n][0] for n in _WEIGHTS], *[out[n][1] for n in _WEIGHTS],
            *[out[n][2] for n in _WEIGHTS], *[out[n][3] for n in _WEIGHTS])
```

```python
import functools
import math

import jax
import jax.numpy as jnp
from jax import lax
from jax.experimental import pallas as pl
from jax.experimental.pallas import tpu as pltpu

F32, BF16 = jnp.float32, jnp.bfloat16
BS = pl.BlockSpec

D_MODEL = 1024
EPS = 1e-6
NEG_INF = -1e30
SG_HEADS, SG_DIM, SG_WIDTH, SG_CHUNK = 4, 128, 512, 128
SC_WIDTH, CONV_TAPS = 512, 3
EVEN_IN = 2 * SG_WIDTH + 3 * SC_WIDTH
POOL_WINDOWS = (2, 4, 8, 16)
POOL_DIM, POOL_WIDTH = 64, 256
POOL_HALO = 16
HEADS, Q_LORA, KV_LORA, QK_NOPE, QK_ROPE, V_DIM = 6, 384, 256, 128, 64, 128
QK_DIM = QK_NOPE + QK_ROPE
QK_PAD = 256
ODD_IN = POOL_WIDTH + Q_LORA + KV_LORA + QK_ROPE
ODD_IN_PAD = 1024
ROPE_THETA = 10000.0
ATTN_SCALE = QK_DIM ** -0.5
D_FF, N_CHIPS = 2816, 4
FF_SHARD = D_FF // N_CHIPS
ADAM_LR, ADAM_B1, ADAM_B2, ADAM_EPS, ADAM_WD, ADAM_STEP = 0.001, 0.9, 0.999, 1e-08, 0.01, 10
VMEM_LIMIT_V7X = 48 * 2**20
LANES, SUBLANES = 128, 8
MESH = pl.DeviceIdType.MESH
HBM = pl.BlockSpec(memory_space=pltpu.HBM)
VMEM = pl.BlockSpec(memory_space=pltpu.VMEM)

_DIMS = {"nn": (((1,), (0,)), ((), ())), "nt": (((1,), (1,)), ((), ())), "tn": (((0,), (0,)), ((), ()))}


def _dot(a, b, mode="nn"):
    return lax.dot_general(a.astype(BF16), b.astype(BF16), _DIMS[mode], preferred_element_type=F32)


def _in_hbm(shape):
    if isinstance(shape, (list, tuple)):
        return [_in_hbm(s) for s in shape]
    return pltpu.HBM(tuple(shape.shape), shape.dtype) if isinstance(shape, jax.ShapeDtypeStruct) else shape


def _hbm(arrays):
    return [pltpu.with_memory_space_constraint(a, pltpu.HBM) for a in arrays]


def _call(body, *, name, out_shape, in_specs, out_specs, grid=(), scratch=(), aliases=None, after=(), stream=False):
    params = pltpu.CompilerParams(vmem_limit_bytes=VMEM_LIMIT_V7X,
                                  **({"dimension_semantics": ("arbitrary",) * len(grid)} if grid else {}))
    n_in, n_after = len(in_specs), len(after)
    kernel_body = body if not after else (lambda *refs: body(*refs[:n_in], *refs[n_in + n_after:]))
    call = pl.pallas_call(kernel_body, name=name, grid=grid, in_specs=list(in_specs) + [pl.BlockSpec(memory_space=pl.ANY)] * n_after,
                          out_specs=out_specs, out_shape=_in_hbm(out_shape) if stream else out_shape, scratch_shapes=list(scratch),
                          input_output_aliases=aliases or {}, compiler_params=params)
    if stream:
        return lambda *ops: call(*_hbm(ops), *after)
    return (lambda *ops: call(*ops, *after)) if after else call


def _sds(shape, dtype):
    return jax.ShapeDtypeStruct(tuple(shape), dtype)


def _token_tile(seq):
    return 512 if seq % 512 == 0 else seq


_TAIL_ROWS = 256


def _matmul(name, mode, pairs, pair_specs, grid, out_shape, out_spec, acc_shape, add=None, add_spec=None, after=(), tail=None):
    n, nk = len(pairs), grid[-1]
    n_add = int(add is not None)
    n_tail = len(tail[0]) if tail else 0
    n_in = 2 * n + n_add + n_tail
    n_out = len(out_shape) if tail else 1

    def body(*refs):
        ab = refs[:2 * n]
        add_ref = refs[2 * n] if n_add else None
        tail_refs, outs = refs[2 * n + n_add:n_in], refs[n_in:n_in + n_out]
        first = pl.program_id(0) == 0

        def finish(result):
            if tail is None:
                r = result(slice(None))
                outs[0][...] = (r if add_ref is None else r + add_ref[...]).astype(outs[0].dtype)
                return
            for lo in range(0, acc_shape[0], _TAIL_ROWS):
                rows = slice(lo, min(lo + _TAIL_ROWS, acc_shape[0]))
                r = result(rows)
                tail[2](rows, r if add_ref is None else r + add_ref[rows, :], first, tail_refs, outs)

        r = _dot(ab[0][...], ab[1][...], mode)
        for p in range(1, n):
            r = r + _dot(ab[2 * p][...], ab[2 * p + 1][...], mode)
        if nk == 1:
            finish(lambda rows: r[rows])
            return
        acc = refs[-1]
        k = pl.program_id(len(grid) - 1)

        @pl.when(k == 0)
        def _():
            acc[...] = r

        @pl.when(k > 0)
        def _():
            acc[...] += r

        @pl.when(k == nk - 1)
        def _():
            finish(lambda rows: acc[rows, :])

    ops = [t for pr in pairs for t in pr] + ([add] if n_add else []) + (list(tail[0]) if tail else [])
    specs = [s for pr in pair_specs for s in pr] + ([add_spec] if n_add else []) + (list(tail[1]) if tail else [])
    return _call(body, name=name, grid=grid, in_specs=specs, out_specs=out_spec, out_shape=out_shape,
                 scratch=[pltpu.VMEM(acc_shape, F32)] if nk > 1 else [], after=after)(*ops)


def _row_spec(tm, d):
    return BS((tm, d), lambda i, j, k: (i, 0))


def _vec_spec(d):
    return BS((1, d), lambda i, j, k: (0, 0))


def _norm_tail(gain, T, tm):
    d = gain.shape[-1]

    def fn(rows, r, first, tail_refs, outs):
        outs[0][rows, :] = r
        outs[1][rows, :] = (r * lax.rsqrt(jnp.mean(r * r, axis=-1, keepdims=True) + EPS) * tail_refs[0][...]).astype(BF16)

    return ([gain.reshape(1, d)], [_vec_spec(d)], fn), [_sds((T, d), F32), _sds((T, d), BF16)], [_row_spec(tm, d), _row_spec(tm, d)]


def _norm_bwd_tail(x, gain, dres, tm):
    T, d = x.shape

    def fn(rows, r, first, tail_refs, outs):
        x_ref, g_ref, dres_ref = tail_refs
        xv = x_ref[rows, :]
        rstd = lax.rsqrt(jnp.mean(xv * xv, axis=-1, keepdims=True) + EPS)
        xhat = xv * rstd
        if rows.start == 0:
            @pl.when(first)
            def _():
                outs[1][...] = jnp.zeros_like(outs[1])

        outs[1][...] += jnp.sum(r * xhat, axis=0, keepdims=True)
        dxhat = r * g_ref[...]
        outs[0][rows, :] = dres_ref[rows, :] + rstd * (dxhat - xhat * jnp.mean(dxhat * xhat, axis=-1, keepdims=True))

    return (([x, gain.reshape(1, d), dres], [_row_spec(tm, d), _vec_spec(d), _row_spec(tm, d)], fn),
            [_sds((T, d), F32), _sds((1, d), F32)], [_row_spec(tm, d), _vec_spec(d)])


def _loss_tail(target, tm):
    T, d = target.shape

    def fn(rows, r, first, tail_refs, outs):
        e = r - tail_refs[0][rows, :]
        if rows.start == 0:
            @pl.when(first)
            def _():
                outs[1][...] = jnp.zeros_like(outs[1])

        outs[1][...] += jnp.sum(e * e)
        outs[0][rows, :] = e * (1.0 / d)

    return (([target], [_row_spec(tm, d)], fn), [_sds((T, d), F32), _sds((SUBLANES, LANES), F32)],
            [_row_spec(tm, d), BS((SUBLANES, LANES), lambda i, j, k: (0, 0))])


def _grad_shards(name, a, b, a_spec, b_spec, pick, out_shape, n_steps):
    def body(a_ref, b_ref, o_ref):
        @pl.when(pl.program_id(0) == 0)
        def _():
            o_ref[...] = jnp.zeros_like(o_ref)

        for j in range(N_CHIPS):
            aj, bj = pick(a_ref, b_ref, j)
            o_ref[j] += _dot(aj, bj, "tn")

    return _call(body, name=name, grid=(n_steps,), in_specs=[a_spec, b_spec],
                 out_specs=BS(out_shape, lambda k: (0, 0, 0)), out_shape=pltpu.HBM(tuple(out_shape), F32))(a, b)


def _mm(name, mode, a, b, out_dtype, tm=1024, tn=1024, tk=512, add=None, after=(), fused=None, hbm_out=False):
    if mode == "tn":
        (K, M), N = a.shape, b.shape[1]
    else:
        (M, K), N = a.shape, (b.shape[1] if mode == "nn" else b.shape[0])
    tm, tn, tk = min(tm, M), min(tn, N), min(tk, K)
    a_spec = BS((tk, tm), lambda i, j, k: (k, i)) if mode == "tn" else BS((tm, tk), lambda i, j, k: (i, k))
    b_spec = BS((tn, tk), lambda i, j, k: (j, k)) if mode == "nt" else BS((tk, tn), lambda i, j, k: (k, j))
    o_spec = BS((tm, tn), lambda i, j, k: (i, j))
    tail, shapes, specs = fused if fused else (None, pltpu.HBM((M, N), out_dtype) if hbm_out else _sds((M, N), out_dtype), o_spec)
    return _matmul(name, mode, [(a, b)], [(a_spec, b_spec)], (M // tm, N // tn, K // tk), shapes, specs, (tm, tn),
                   add=add, add_spec=o_spec if add is not None else None, after=after, tail=tail)


def _rmsnorm_fwd(name, x, g, tm):
    T, d = x.shape

    def body(x_ref, g_ref, o_ref):
        xv = x_ref[...]
        y = xv * lax.rsqrt(jnp.mean(xv * xv, axis=-1, keepdims=True) + EPS)
        o_ref[...] = (y * g_ref[...]).astype(o_ref.dtype)

    return _call(body, name=name, grid=(T // tm,), in_specs=[BS((tm, d), lambda i: (i, 0)), BS((1, d), lambda i: (0, 0))],
                 out_specs=BS((tm, d), lambda i: (i, 0)), out_shape=_sds((T, d), BF16))(x, g.reshape(1, d))


def _ffn_up(name, h, wg, wu, tm):
    T = h.shape[0]

    def body(h_ref, wg_ref, wu_ref, g_ref, u_ref, a_ref):
        hv = h_ref[...]
        g = _dot(hv, wg_ref[...], "nt")
        u = _dot(hv, wu_ref[...], "nt")
        g_ref[...] = g.astype(BF16)
        u_ref[...] = u.astype(BF16)
        a_ref[...] = (g * (1.0 / (1.0 + jnp.exp(-g))) * u).astype(BF16)

    w_spec = BS((None, FF_SHARD, D_MODEL), lambda j, i: (j, 0, 0))
    o_spec = BS((None, tm, FF_SHARD), lambda j, i: (j, i, 0))
    sh = _sds((N_CHIPS, T, FF_SHARD), BF16)
    return _call(body, name=name, grid=(N_CHIPS, T // tm), in_specs=[BS((tm, D_MODEL), lambda j, i: (i, 0)), w_spec, w_spec],
                 out_specs=[o_spec, o_spec, o_spec], out_shape=[sh, sh, sh])(h, wg, wu)


def _ffn_act_bwd(name, dxo, wd, g, u, tm, after=()):
    T = dxo.shape[0]

    def body(dx_ref, wd_ref, g_ref, u_ref, dg_ref, du_ref):
        da = _dot(dx_ref[...], wd_ref[...], "nt")
        g = g_ref[...].astype(F32)
        sig = 1.0 / (1.0 + jnp.exp(-g))
        dg_ref[...] = (da * u_ref[...].astype(F32) * (sig * (1.0 + g * (1.0 - sig)))).astype(BF16)
        du_ref[...] = (da * (g * sig)).astype(BF16)

    t_spec = BS((None, tm, FF_SHARD), lambda i, j: (j, i, 0))
    sh = _sds((N_CHIPS, T, FF_SHARD), BF16)
    return _call(body, name=name, grid=(T // tm, N_CHIPS),
                 in_specs=[BS((tm, D_MODEL), lambda i, j: (i, 0)), BS((None, FF_SHARD, D_MODEL), lambda i, j: (j, 0, 0)), t_spec, t_spec],
                 out_specs=[t_spec, t_spec], out_shape=[sh, sh], after=after)(dxo, wd, g, u)


def _big_tile(n):
    return min(1024, n)


def _ffn_fwd(l, x, h, wg, wu, wd, fused):
    T = x.shape[0]
    tm = _big_tile(T)
    g, u, a = _ffn_up(f"ffn{l}_up", h, wg, wu, tm)
    tn = D_MODEL
    tail, shapes, specs = fused
    outs = _matmul(f"ffn{l}_down", "nn", [(a, wd)],
                   [(BS((None, tm, FF_SHARD), lambda i, j, k: (k, i, 0)), BS((None, FF_SHARD, tn), lambda i, j, k: (k, 0, j)))],
                   (T // tm, D_MODEL // tn, N_CHIPS), shapes, specs, (tm, tn),
                   add=x, add_spec=BS((tm, tn), lambda i, j, k: (i, j)), tail=tail)
    return outs, (h, g, u, a)


def _ffn_bwd(l, x, gain, wg, wu, wd, saved, dxo, emit, after=()):
    h, g, u, a = saved
    T = x.shape[0]
    tm = _big_tile(T)
    dg, du = _ffn_act_bwd(f"ffn{l}_act_bwd", dxo, wd, g, u, tm, after=after)
    tk = min(512, T)
    tn = D_MODEL
    shards_spec = BS((N_CHIPS, tk, FF_SHARD), lambda k: (0, k, 0))
    rows_spec = BS((tk, D_MODEL), lambda k: (k, 0))

    def dw(nm, act, rows):
        return _grad_shards(nm, act, rows, shards_spec, rows_spec, lambda a_ref, b_ref, j: (a_ref[j], b_ref[...]),
                            (N_CHIPS, FF_SHARD, D_MODEL), T // tk)

    behind = emit(f"ffn{l}", {f"ffn_w_gate{l}": dw(f"ffn{l}_dwg", dg, h), f"ffn_w_up{l}": dw(f"ffn{l}_dwu", du, h),
                              f"ffn_w_down{l}": dw(f"ffn{l}_dwd", a, dxo)})
    act_spec = BS((None, tm, FF_SHARD), lambda i, j, k: (k, i, 0))
    w_spec = BS((None, FF_SHARD, tn), lambda i, j, k: (k, 0, j))
    tail, shapes, specs = _norm_bwd_tail(x, gain, dxo, tm)
    return _matmul(f"ffn{l}_dh", "nn", [(dg, wg), (du, wu)], [(act_spec, w_spec), (act_spec, w_spec)],
                   (T // tm, D_MODEL // tn, N_CHIPS), shapes, specs, (tm, tn), after=behind, tail=tail)


_INV_SQRT2 = 1.0 / math.sqrt(2.0)
_INV_SQRT_2PI = 1.0 / math.sqrt(2.0 * math.pi)


def _gelu(x):
    return 0.5 * x * (1.0 + lax.erf(x * _INV_SQRT2))


def _gelu_grad(x):
    return 0.5 * (1.0 + lax.erf(x * _INV_SQRT2)) + x * jnp.exp(-0.5 * x * x) * _INV_SQRT_2PI


def _shift_down(x, k):
    return pltpu.roll(x, k, 0)


def _shift_up(x, k):
    return pltpu.roll(x, x.shape[0] - k, 0)


def _layer_norm_head(xh):
    xc = xh - jnp.mean(xh, axis=-1, keepdims=True)
    rstd = lax.rsqrt(jnp.mean(xc * xc, axis=-1, keepdims=True) + EPS)
    return xc * rstd, rstd


def _even_halo_specs(tm, n_tiles, col_blocks, after):
    rows = tm // SUBLANES
    last = n_tiles * rows - 1
    if after:
        return [BS((SUBLANES, 512), functools.partial(lambda cb, i: (jnp.minimum((i + 1) * rows, last), cb), cb)) for cb in col_blocks]
    return [BS((SUBLANES, 512), functools.partial(lambda cb, i: (jnp.maximum(i * rows - 1, 0), cb), cb)) for cb in col_blocks]


def _even_mixer_fwd(proj, ln_g, w_tril, b_lanes, conv_w, seq, tm):
    T = proj.shape[0]
    tiles_per_seq = seq // tm

    def body(p_ref, hc_ref, hh_ref, lng_ref, w_ref, bb_ref, cw_ref, o_ref):
        first = pl.program_id(0) % tiles_per_seq == 0
        for h in range(SG_HEADS):
            cols = slice(SG_DIM * h, SG_DIM * (h + 1))
            vhat, _ = _layer_norm_head(_gelu(p_ref[:, SG_WIDTH + SG_DIM * h:SG_WIDTH + SG_DIM * (h + 1)]))
            vln = (vhat * lng_ref[:, cols]).astype(BF16)
            for k in range(tm // SG_CHUNK):
                rows = slice(SG_CHUNK * k, SG_CHUNK * (k + 1))
                mixed = _dot(w_ref[h], vln[rows]) + bb_ref[h]
                o_ref[rows, cols] = (_gelu(p_ref[rows, cols]) * mixed).astype(BF16)
        z = p_ref[:, 1536:2048] * p_ref[:, 2048:2560]
        zz = jnp.concatenate([jnp.where(first, 0.0, hc_ref[...] * hh_ref[...]), z], axis=0)
        y = cw_ref[0:1, :] * _shift_down(zz, 2)[SUBLANES:] + cw_ref[1:2, :] * _shift_down(zz, 1)[SUBLANES:] + cw_ref[2:3, :] * z
        o_ref[:, SG_WIDTH:] = (p_ref[:, 1024:1536] * y).astype(BF16)

    full = lambda shape: BS(shape, lambda i: (0,) * len(shape))
    return _call(body, name="even_mixer_fwd", grid=(T // tm,),
                 in_specs=[BS((tm, EVEN_IN), lambda i: (i, 0))] + _even_halo_specs(tm, T // tm, (3, 4), after=False)
                 + [full((1, SG_WIDTH)), full((SG_HEADS, SG_CHUNK, SG_CHUNK)), full((SG_HEADS, SG_CHUNK, SG_DIM)), full((SUBLANES, SC_WIDTH))],
                 out_specs=BS((tm, D_MODEL), lambda i: (i, 0)), out_shape=_sds((T, D_MODEL), BF16))(
        proj, proj, proj, ln_g, w_tril, b_lanes, conv_w)


def _even_mixer_bwd(proj, dmix, ln_g, w_tril, b_lanes, conv_w, seq, tm):
    T = proj.shape[0]
    n_tiles, tiles_per_seq = T // tm, seq // tm

    def body(p_ref, dm_ref, hc_ref, hh_ref, nd_ref, nb_ref, lng_ref, w_ref, bb_ref, cw_ref,
             dp_ref, dw_ref, db_ref, dlng_ref, dcw_ref):
        i = pl.program_id(0)
        first = i % tiles_per_seq == 0
        last = i % tiles_per_seq == tiles_per_seq - 1

        @pl.when(i == 0)
        def _():
            dw_ref[...] = jnp.zeros_like(dw_ref)
            db_ref[...] = jnp.zeros_like(db_ref)
            dlng_ref[...] = jnp.zeros_like(dlng_ref)
            dcw_ref[...] = jnp.zeros_like(dcw_ref)

        for h in range(SG_HEADS):
            cols = slice(SG_DIM * h, SG_DIM * (h + 1))
            vcols = slice(SG_WIDTH + SG_DIM * h, SG_WIDTH + SG_DIM * (h + 1))
            lng = lng_ref[:, cols]
            for k in range(tm // SG_CHUNK):
                rows = slice(SG_CHUNK * k, SG_CHUNK * (k + 1))
                v = p_ref[rows, vcols]
                vhat, rstd = _layer_norm_head(_gelu(v))
                vln = (vhat * lng).astype(BF16)
                mixed = _dot(w_ref[h], vln) + bb_ref[h]
                u = p_ref[rows, cols]
                da = dm_ref[rows, cols]
                dp_ref[rows, cols] = (da * mixed * _gelu_grad(u)).astype(BF16)
                dmixed = da * _gelu(u)
                db_ref[h] += dmixed
                dw_ref[h] += _dot(dmixed, vln, "nt")
                dvln = _dot(w_ref[h], dmixed, "tn")
                dlng_ref[:, cols] += jnp.sum(dvln * vhat, axis=0, keepdims=True)
                dvhat = dvln * lng
                dgv = rstd * (dvhat - jnp.mean(dvhat, axis=-1, keepdims=True)
                              - vhat * jnp.mean(dvhat * vhat, axis=-1, keepdims=True))
                dp_ref[rows, vcols] = (dgv * _gelu_grad(v)).astype(BF16)

        b = p_ref[:, 1024:1536]
        c = p_ref[:, 1536:2048]
        hv = p_ref[:, 2048:2560]
        z = c * hv
        zz = jnp.concatenate([jnp.where(first, 0.0, hc_ref[...] * hh_ref[...]), z], axis=0)
        z1 = _shift_down(zz, 1)[SUBLANES:]
        z2 = _shift_down(zz, 2)[SUBLANES:]
        w0, w1, w2 = cw_ref[0:1, :], cw_ref[1:2, :], cw_ref[2:3, :]
        dbo = dm_ref[:, SG_WIDTH:]
        dy = dbo * b
        dd = jnp.concatenate([dy, jnp.where(last, 0.0, nd_ref[...] * nb_ref[...])], axis=0)
        dz = w2 * dy + w1 * _shift_up(dd, 1)[:tm] + w0 * _shift_up(dd, 2)[:tm]
        dp_ref[:, 1024:1536] = (dbo * (w0 * z2 + w1 * z1 + w2 * z)).astype(BF16)
        dp_ref[:, 1536:2048] = (dz * hv).astype(BF16)
        dp_ref[:, 2048:2560] = (dz * c).astype(BF16)
        dcw_ref[0:1, :] += jnp.sum(dy * z2, axis=0, keepdims=True)
        dcw_ref[1:2, :] += jnp.sum(dy * z1, axis=0, keepdims=True)
        dcw_ref[2:3, :] += jnp.sum(dy * z, axis=0, keepdims=True)

        @pl.when(i == n_tiles - 1)
        def _():
            t_idx = lax.broadcasted_iota(jnp.int32, (SG_CHUNK, SG_CHUNK), 0)
            s_idx = lax.broadcasted_iota(jnp.int32, (SG_CHUNK, SG_CHUNK), 1)
            for h in range(SG_HEADS):
                dw_ref[h] = jnp.where(t_idx >= s_idx, dw_ref[h], 0.0)

    full = lambda shape: BS(shape, lambda i: (0,) * len(shape))
    sq = (SG_HEADS, SG_CHUNK, SG_CHUNK)
    return _call(body, name="even_mixer_bwd", grid=(n_tiles,),
                 in_specs=[BS((tm, EVEN_IN), lambda i: (i, 0)), BS((tm, D_MODEL), lambda i: (i, 0))]
                 + _even_halo_specs(tm, n_tiles, (3, 4), after=False)
                 + _even_halo_specs(tm, n_tiles, (1,), after=True) + _even_halo_specs(tm, n_tiles, (2,), after=True)
                 + [full((1, SG_WIDTH)), full(sq), full(sq), full((SUBLANES, SC_WIDTH))],
                 out_specs=[BS((tm, EVEN_IN), lambda i: (i, 0)), full(sq), full(sq), full((1, SG_WIDTH)), full((SUBLANES, SC_WIDTH))],
                 out_shape=[_sds((T, EVEN_IN), BF16), _sds(sq, F32), _sds(sq, F32), _sds((1, SG_WIDTH), F32), _sds((SUBLANES, SC_WIDTH), F32)])(
        proj, dmix, proj, proj, dmix, proj, ln_g, w_tril, b_lanes, conv_w)


def _pool_select(vals):
    lane = lax.broadcasted_iota(jnp.int32, vals[0].shape, 1)
    out = vals[-1]
    for g in range(len(vals) - 2, -1, -1):
        out = jnp.where(lane < POOL_DIM * (g + 1), vals[g], out)
    return out


def _pool_counts(pos1):
    lane = lax.broadcasted_iota(jnp.int32, (pos1.shape[0], POOL_WIDTH), 1)
    win = _pool_select([jnp.full(lane.shape, float(w), F32) for w in POOL_WINDOWS])
    return jnp.minimum(pos1, win)


def _pool_means(zz, counts):
    s2 = zz + _shift_down(zz, 1)
    s4 = s2 + _shift_down(s2, 2)
    s8 = s4 + _shift_down(s4, 4)
    s16 = s8 + _shift_down(s8, 8)
    return _pool_select([s2, s4, s8, s16])[POOL_HALO:] / counts


def _pool_halo_spec(tm, n_tiles, after):
    rows = tm // POOL_HALO
    if after:
        return BS((POOL_HALO, POOL_WIDTH), lambda i: (jnp.minimum((i + 1) * rows, n_tiles * rows - 1), 0))
    return BS((POOL_HALO, POOL_WIDTH), lambda i: (jnp.maximum(i * rows - 1, 0), 0))


def _pool_fwd(proj, w_diag, scale, seq, tm):
    T = proj.shape[0]
    tiles_per_seq = seq // tm

    def body(z_ref, zh_ref, w_ref, s_ref, o_ref):
        t = pl.program_id(0) % tiles_per_seq
        z = z_ref[...]
        zz = jnp.concatenate([jnp.where(t == 0, 0.0, zh_ref[...]), z], axis=0)
        pos1 = (lax.broadcasted_iota(jnp.int32, (tm, 1), 0) + (t * tm + 1)).astype(F32)
        pooled = _pool_means(zz, _pool_counts(pos1)) - z
        o_ref[...] = (_dot(pooled, w_ref[...]) * s_ref[...]).astype(BF16)

    full = lambda shape: BS(shape, lambda i: (0,) * len(shape))
    return _call(body, name="pool_fwd", grid=(T // tm,),
                 in_specs=[BS((tm, POOL_WIDTH), lambda i: (i, 0)), _pool_halo_spec(tm, T // tm, False),
                           full((POOL_WIDTH, POOL_WIDTH)), full((1, POOL_WIDTH))],
                 out_specs=BS((tm, POOL_WIDTH), lambda i: (i, 0)), out_shape=_sds((T, D_MODEL), BF16))(proj, proj, w_diag, scale)


def _pool_bwd(proj, dmix, w_diag, scale, seq, tm):
    T = proj.shape[0]
    n_tiles, tiles_per_seq = T // tm, seq // tm

    def body(z_ref, zh_ref, do_ref, don_ref, w_ref, s_ref, dz_ref, dw_ref, ds_ref):
        i = pl.program_id(0)
        t = i % tiles_per_seq

        @pl.when(i == 0)
        def _():
            dw_ref[...] = jnp.zeros_like(dw_ref)
            ds_ref[...] = jnp.zeros_like(ds_ref)

        z = z_ref[...]
        zz = jnp.concatenate([jnp.where(t == 0, 0.0, zh_ref[...]), z], axis=0)
        pos1 = (lax.broadcasted_iota(jnp.int32, (tm, 1), 0) + (t * tm + 1)).astype(F32)
        counts = _pool_counts(pos1)
        pooled = _pool_means(zz, counts) - z
        dout = do_ref[...].astype(F32)
        ds_ref[...] += jnp.sum(dout * _dot(pooled, w_ref[...]), axis=0, keepdims=True)
        dlin = dout * s_ref[...]
        dw_ref[...] += _dot(pooled, dlin, "tn")
        dpooled = _dot(dlin, w_ref[...], "nt")
        dpooled_n = _dot(don_ref[...].astype(F32) * s_ref[...], w_ref[...], "nt")
        pos1_n = (lax.broadcasted_iota(jnp.int32, (POOL_HALO, 1), 0) + ((t + 1) * tm + 1)).astype(F32)
        dmean_n = jnp.where(t == tiles_per_seq - 1, 0.0, dpooled_n / _pool_counts(pos1_n))
        dd = jnp.concatenate([dpooled / counts, dmean_n], axis=0)
        r2 = dd + _shift_up(dd, 1)
        r4 = r2 + _shift_up(r2, 2)
        r8 = r4 + _shift_up(r4, 4)
        r16 = r8 + _shift_up(r8, 8)
        dz_ref[...] = (_pool_select([r2, r4, r8, r16])[:tm] - dpooled).astype(BF16)

    full = lambda shape: BS(shape, lambda i: (0,) * len(shape))
    return _call(body, name="pool_bwd", grid=(n_tiles,),
                 in_specs=[BS((tm, POOL_WIDTH), lambda i: (i, 0)), _pool_halo_spec(tm, n_tiles, False),
                           BS((tm, POOL_WIDTH), lambda i: (i, 0)), _pool_halo_spec(tm, n_tiles, True),
                           full((POOL_WIDTH, POOL_WIDTH)), full((1, POOL_WIDTH))],
                 out_specs=[BS((tm, POOL_WIDTH), lambda i: (i, 0)), full((POOL_WIDTH, POOL_WIDTH)), full((1, POOL_WIDTH))],
                 out_shape=[_sds((T, POOL_WIDTH), BF16), _sds((POOL_WIDTH, POOL_WIDTH), F32), _sds((1, POOL_WIDTH), F32)])(
        proj, proj, dmix, dmix, w_diag, scale)


def _rope_partner(r):
    lane = lax.broadcasted_iota(jnp.int32, r.shape, 1)
    return jnp.where(lane < QK_ROPE // 2, pltpu.roll(r, LANES - QK_ROPE // 2, 1), pltpu.roll(r, QK_ROPE // 2, 1))


def _rope(x, cos, sin_signed):
    r = x[:, QK_NOPE:]
    return jnp.concatenate([x[:, :QK_NOPE], r * cos + _rope_partner(r) * sin_signed], axis=1)


def _rope_transposed(dx, cos, sin_signed):
    dr = dx[:, QK_NOPE:]
    return jnp.concatenate([dx[:, :QK_NOPE], dr * cos + _rope_partner(dr * sin_signed)], axis=1)


def _head_norm(x):
    r = lax.rsqrt(jnp.sum(x * x, axis=-1, keepdims=True) * (1.0 / QK_DIM) + EPS)
    return x * r, r


def _head_norm_bwd(dy, xhat, r, gain):
    dxhat = dy * gain
    return r * (dxhat - xhat * (jnp.sum(dxhat * xhat, axis=-1, keepdims=True) * (1.0 / QK_DIM)))


def _latents(p_ref, qag_ref, kvag_ref):
    ql = p_ref[:, POOL_WIDTH:POOL_WIDTH + Q_LORA]
    kvl = p_ref[:, POOL_WIDTH + Q_LORA:POOL_WIDTH + Q_LORA + KV_LORA]
    rq = lax.rsqrt(jnp.mean(ql * ql, axis=-1, keepdims=True) + EPS)
    rkv = lax.rsqrt(jnp.mean(kvl * kvl, axis=-1, keepdims=True) + EPS)
    return ql * rq, rq, kvl * rkv, rkv


def _mla_specs(tm):
    full = lambda shape: BS(shape, lambda i, h: (0,) * len(shape))
    return [BS((tm, ODD_IN_PAD), lambda i, h: (i, 0)), BS((tm, LANES), lambda i, h: (i, 0)), BS((tm, LANES), lambda i, h: (i, 0)),
            full((1, Q_LORA)), full((1, KV_LORA)), BS((None, Q_LORA, QK_PAD), lambda i, h: (h, 0, 0)),
            BS((None, KV_LORA, QK_PAD), lambda i, h: (h, 0, 0)), full((1, QK_PAD)), full((1, QK_PAD))]


def _mla_qkv_fwd(proj, cos, sin_signed, qa_g, kva_g, q_b, kv_b, q_g, k_g, tm):
    T = proj.shape[0]

    def body(p_ref, cos_ref, sin_ref, qag_ref, kvag_ref, qb_ref, kvb_ref, qg_ref, kg_ref, q_ref, k_ref, v_ref, qn_s, kvn_s):
        @pl.when(pl.program_id(1) == 0)
        def _():
            qhat, _, kvhat, _ = _latents(p_ref, qag_ref, kvag_ref)
            qn_s[...] = (qhat * qag_ref[...]).astype(BF16)
            kvn_s[...] = (kvhat * kvag_ref[...]).astype(BF16)

        cos, sin = cos_ref[...], sin_ref[...]
        qhat, _ = _head_norm(_dot(qn_s[...], qb_ref[...]))
        q_ref[...] = _rope(qhat * qg_ref[...], cos, sin).astype(BF16)
        kv = _dot(kvn_s[...], kvb_ref[...])
        khat, _ = _head_norm(jnp.concatenate([kv[:, :QK_NOPE], p_ref[:, ODD_IN_PAD - LANES:]], axis=1))
        k_ref[...] = _rope(khat * kg_ref[...], cos, sin).astype(BF16)
        v_ref[...] = kv[:, QK_NOPE:].astype(BF16)

    qk_spec = BS((None, tm, QK_PAD), lambda i, h: (h, i, 0))
    return _call(body, name="mla_qkv_fwd", grid=(T // tm, HEADS), in_specs=_mla_specs(tm),
                 out_specs=[qk_spec, qk_spec, BS((None, tm, V_DIM), lambda i, h: (h, i, 0))],
                 out_shape=[_sds((HEADS, T, QK_PAD), BF16), _sds((HEADS, T, QK_PAD), BF16), _sds((HEADS, T, V_DIM), BF16)],
                 scratch=[pltpu.VMEM((tm, Q_LORA), BF16), pltpu.VMEM((tm, KV_LORA), BF16)])(
        proj, cos, sin_signed, qa_g, kva_g, q_b, kv_b, q_g, k_g)


def _mla_qkv_bwd(proj, cos, sin_signed, qa_g, kva_g, q_b, kv_b, q_g, k_g, dq, dk, dv, dz_pool, tm):
    T = proj.shape[0]
    n_tiles = T // tm

    def body(p_ref, cos_ref, sin_ref, qag_ref, kvag_ref, qb_ref, kvb_ref, qg_ref, kg_ref, dq_ref, dk_ref, dv_ref, dzp_ref,
             dp_ref, dqb_ref, dkvb_ref, dqg_ref, dkg_ref, dqag_ref, dkvag_ref, qn_s, kvn_s, dqn_s, dkvn_s, dkr_s):
        i, h = pl.program_id(0), pl.program_id(1)

        @pl.when((i == 0) & (h == 0))
        def _():
            for ref in (dqb_ref, dkvb_ref, dqg_ref, dkg_ref, dqag_ref, dkvag_ref):
                ref[...] = jnp.zeros_like(ref)

        @pl.when(h == 0)
        def _():
            qhat, _, kvhat, _ = _latents(p_ref, qag_ref, kvag_ref)
            qn_s[...] = (qhat * qag_ref[...]).astype(BF16)
            kvn_s[...] = (kvhat * kvag_ref[...]).astype(BF16)
            dqn_s[...] = jnp.zeros_like(dqn_s)
            dkvn_s[...] = jnp.zeros_like(dkvn_s)
            dkr_s[...] = jnp.zeros_like(dkr_s)

        cos, sin = cos_ref[...], sin_ref[...]
        qhat, rq = _head_norm(_dot(qn_s[...], qb_ref[...]))
        dqn_head = _rope_transposed(dq_ref[...], cos, sin)
        dqg_ref[...] += jnp.sum(dqn_head * qhat, axis=0, keepdims=True)
        dqh = _head_norm_bwd(dqn_head, qhat, rq, qg_ref[...])
        dqb_ref[h] += _dot(qn_s[...], dqh, "tn")
        dqn_s[...] += _dot(dqh, qb_ref[...], "nt")

        kv = _dot(kvn_s[...], kvb_ref[...])
        khat, rk = _head_norm(jnp.concatenate([kv[:, :QK_NOPE], p_ref[:, ODD_IN_PAD - LANES:]], axis=1))
        dkn_head = _rope_transposed(dk_ref[...], cos, sin)
        dkg_ref[...] += jnp.sum(dkn_head * khat, axis=0, keepdims=True)
        dkf = _head_norm_bwd(dkn_head, khat, rk, kg_ref[...])
        dkr_s[...] += dkf[:, QK_NOPE:]
        dkv = jnp.concatenate([dkf[:, :QK_NOPE], dv_ref[...]], axis=1)
        dkvb_ref[h] += _dot(kvn_s[...], dkv, "tn")
        dkvn_s[...] += _dot(dkv, kvb_ref[...], "nt")

        @pl.when(h == HEADS - 1)
        def _():
            qhat_l, rql, kvhat_l, rkvl = _latents(p_ref, qag_ref, kvag_ref)
            dqn, dkvn = dqn_s[...], dkvn_s[...]
            dqag_ref[...] += jnp.sum(dqn * qhat_l, axis=0, keepdims=True)
            dkvag_ref[...] += jnp.sum(dkvn * kvhat_l, axis=0, keepdims=True)
            dqx, dkvx = dqn * qag_ref[...], dkvn * kvag_ref[...]
            dp_ref[:, :POOL_WIDTH] = dzp_ref[...]
            dp_ref[:, POOL_WIDTH:POOL_WIDTH + Q_LORA] = (
                rql * (dqx - qhat_l * jnp.mean(dqx * qhat_l, axis=-1, keepdims=True))).astype(BF16)
            dp_ref[:, POOL_WIDTH + Q_LORA:ODD_IN_PAD - LANES] = (
                rkvl * (dkvx - kvhat_l * jnp.mean(dkvx * kvhat_l, axis=-1, keepdims=True))).astype(BF16)
            dp_ref[:, ODD_IN_PAD - LANES:] = dkr_s[:, :QK_ROPE].astype(BF16)

    full = lambda shape: BS(shape, lambda i, h: (0,) * len(shape))
    qk_spec = BS((None, tm, QK_PAD), lambda i, h: (h, i, 0))
    return _call(body, name="mla_qkv_bwd", grid=(n_tiles, HEADS),
                 in_specs=_mla_specs(tm) + [qk_spec, qk_spec, BS((None, tm, V_DIM), lambda i, h: (h, i, 0)),
                                            BS((tm, POOL_WIDTH), lambda i, h: (i, 0))],
                 out_specs=[BS((tm, ODD_IN), lambda i, h: (i, 0)), full((HEADS, Q_LORA, QK_PAD)), full((HEADS, KV_LORA, QK_PAD)),
                            full((1, QK_PAD)), full((1, QK_PAD)), full((1, Q_LORA)), full((1, KV_LORA))],
                 out_shape=[_sds((T, ODD_IN), BF16),_sds((HEADS, Q_LORA, QK_PAD), F32), _sds((HEADS, KV_LORA, QK_PAD), F32),
                            _sds((1, QK_PAD), F32), _sds((1, QK_PAD), F32), _sds((1, Q_LORA), F32), _sds((1, KV_LORA), F32)],
                 scratch=[pltpu.VMEM((tm, Q_LORA), BF16), pltpu.VMEM((tm, KV_LORA), BF16), pltpu.VMEM((tm, Q_LORA), F32),
                          pltpu.VMEM((tm, KV_LORA), F32), pltpu.VMEM((tm, LANES), F32)])(
        proj, cos, sin_signed, qa_g, kva_g, q_b, kv_b, q_g, k_g, dq, dk, dv, dz_pool)


def _attn_tile(seq):
    return 512 if seq % 512 == 0 else seq


def _causal_mask(s):
    row = lax.broadcasted_iota(jnp.int32, s.shape, 0)
    col = lax.broadcasted_iota(jnp.int32, s.shape, 1)
    return jnp.where(row >= col, s, NEG_INF)


def _tile(i, t):
    return slice(i * t, (i + 1) * t)


def _flash_fwd(q, k, v, mix, batch, seq):
    t = _attn_tile(seq)
    nq = seq // t

    def body(q_ref, k_ref, v_ref, _, o_ref, lse_ref):
        for qi in range(nq):
            rows, before = _tile(qi, t), slice(0, qi * t)
            qv = q_ref[rows, :]
            s_diag = _causal_mask(_dot(qv, k_ref[rows, :], "nt") * ATTN_SCALE)
            m = jnp.max(s_diag, axis=-1, keepdims=True)
            if qi:
                s_before = _dot(qv, k_ref[before, :], "nt") * ATTN_SCALE
                m = jnp.maximum(m, jnp.max(s_before, axis=-1, keepdims=True))
            p = jnp.exp(s_diag - m)
            l = jnp.sum(p, axis=-1, keepdims=True)
            acc = _dot(p, v_ref[rows, :])
            if qi:
                p = jnp.exp(s_before - m)
                l = l + jnp.sum(p, axis=-1, keepdims=True)
                acc = acc + _dot(p, v_ref[before, :])
            o_ref[rows, :] = (acc / l).astype(BF16)
            lse_ref[rows, :] = jnp.broadcast_to(m + jnp.log(l), (t, LANES))

    T = batch * seq
    whole = lambda w: BS((None, seq, w), lambda b, h: (h, b, 0))
    return _call(body, name="flash_fwd", grid=(batch, HEADS),
                 in_specs=[whole(QK_PAD), whole(QK_PAD), whole(V_DIM), pl.BlockSpec(memory_space=pl.ANY)],
                 out_specs=[BS((seq, V_DIM), lambda b, h: (b, POOL_WIDTH // V_DIM + h)), whole(LANES)],
                 out_shape=[_sds((T, D_MODEL), BF16), _sds((HEADS, T, LANES), F32)],
                 aliases={3: 0})(q, k, v, mix)


def _flash_bwd(q, k, v, dmix, mix, lse, batch, seq):
    t = _attn_tile(seq)
    nq = seq // t

    def body(q_ref, k_ref, v_ref, do_ref, o_ref, lse_ref, dq_ref, dk_ref, dv_ref, delta_s):
        dq_ref[...] = jnp.zeros_like(dq_ref)
        dk_ref[...] = jnp.zeros_like(dk_ref)
        dv_ref[...] = jnp.zeros_like(dv_ref)
        for qi in range(nq):
            rows = _tile(qi, t)
            delta_s[qi] = jnp.sum(do_ref[rows, :].astype(F32) * o_ref[rows, :].astype(F32), axis=-1, keepdims=True)
        for kb in range(nq):
            keys = _tile(kb, t)
            for qi in range(kb, nq):
                rows = _tile(qi, t)
                qv, kk, do = q_ref[rows, :], k_ref[keys, :], do_ref[rows, :]
                s = _dot(qv, kk, "nt") * ATTN_SCALE
                if kb == qi:
                    s = _causal_mask(s)
                p = jnp.exp(s - lse_ref[rows, 0:1])
                dv_ref[keys, :] += _dot(p, do, "tn")
                ds = p * (_dot(do, v_ref[keys, :], "nt") - delta_s[qi]) * ATTN_SCALE
                dq_ref[rows, :] += _dot(ds, kk)
                dk_ref[keys, :] += _dot(ds, qv, "tn")

    T = batch * seq
    whole = lambda w: BS((None, seq, w), lambda b, h: (h, b, 0))
    head_cols = BS((seq, V_DIM), lambda b, h: (b, POOL_WIDTH // V_DIM + h))
    return _call(body, name="flash_bwd", grid=(batch, HEADS),
                 in_specs=[whole(QK_PAD), whole(QK_PAD), whole(V_DIM), head_cols, head_cols, whole(LANES)],
                 out_specs=[whole(QK_PAD), whole(QK_PAD), whole(V_DIM)],
                 out_shape=[_sds((HEADS, T, QK_PAD), F32), _sds((HEADS, T, QK_PAD), F32), _sds((HEADS, T, V_DIM), F32)],
                 scratch=[pltpu.VMEM((nq, t, 1), F32)])(q, k, v, dmix, mix, lse)


def _adamw_math(w, g, m, v):
    m = ADAM_B1 * m + (1.0 - ADAM_B1) * g
    v = ADAM_B2 * v + (1.0 - ADAM_B2) * (g * g)
    m_hat = m / (1.0 - ADAM_B1 ** ADAM_STEP)
    v_hat = v / (1.0 - ADAM_B2 ** ADAM_STEP)
    return -ADAM_LR * (m_hat / (jnp.sqrt(v_hat) + ADAM_EPS) + ADAM_WD * w), m, v


def _adamw(name, w, g, m, v, l=0, prev=()):
    L, R, C = w.shape
    tr = 256 if R % 256 == 0 else R

    def body(w_ref, g_ref, m_ref, v_ref, *rest):
        go_ref, d_ref, mo_ref, vo_ref, token = rest[-5:]
        gv = g_ref[...]
        d_ref[...], mo_ref[...], vo_ref[...] = _adamw_math(w_ref[...], gv, m_ref[...], v_ref[...])
        go_ref[...] = gv
        token[...] = jnp.zeros_like(token)

    layer = BS((None, tr, C), lambda i: (l, i, 0))
    return _call(body, name=f"{name}_{l}", grid=(R // tr,),
                 in_specs=[layer, BS((tr, C), lambda i: (i, 0)), layer, layer] + [pl.BlockSpec(memory_space=pl.ANY)] * len(prev),
                 out_specs=[layer] * 4 + [BS((SUBLANES, LANES), lambda i: (0, 0))],
                 out_shape=[_sds((L, R, C), F32)] * 4 + [_sds((SUBLANES, LANES), F32)],
                 aliases={4 + n: n for n in range(len(prev))}, stream=True)(w, g, m, v, *prev)


def _place():
    x, y, c = lax.axis_index("x"), lax.axis_index("y"), lax.axis_index("c")
    other_chips = [(1 - x, y), (x, 1 - y), (1 - x, 1 - y)]
    return x, y, c, other_chips


def _remote(src, dst, send_sem, recv_sem, dev):
    return pltpu.make_async_remote_copy(src_ref=src, dst_ref=dst, send_sem=send_sem, recv_sem=recv_sem,
                                        device_id=dev, device_id_type=MESH)


def _prefetch_call(body, *, name, grid, in_specs, out_specs, out_shape):
    grid_spec = pltpu.PrefetchScalarGridSpec(num_scalar_prefetch=1, grid=grid, in_specs=in_specs, out_specs=out_specs)
    params = pltpu.CompilerParams(vmem_limit_bytes=VMEM_LIMIT_V7X, dimension_semantics=("arbitrary",) * len(grid))
    call = pl.pallas_call(body, name=name, grid_spec=grid_spec, out_shape=out_shape, compiler_params=params)
    return lambda scalars, *ops: call(scalars, *_hbm(ops))


def _row_tile(rows):
    return 256 if rows % 256 == 0 else rows


def _cast_place(name, w, layer, chip, after=()):
    _, _, rows, C = w.shape
    tr = _row_tile(rows)

    def body(chip_ref, w_ref, *rest):
        rest[-1][...] = w_ref[...].astype(BF16)

    return _prefetch_call(body, name=name, grid=(2, rows // tr),
                          in_specs=[BS((None, None, tr, C), lambda h, i, chip_ref: (layer, h, i, 0))]
                          + [pl.BlockSpec(memory_space=pl.ANY)] * len(after),
                          out_specs=BS((None, None, tr, C), lambda h, i, chip_ref: (chip_ref[0], h, i, 0)),
                          out_shape=pltpu.HBM((N_CHIPS, 2, rows, C), BF16))(chip, w, *after)


SEM = pl.BlockSpec(memory_space=pltpu.SEMAPHORE)


def _split_copy_call(body, *, name, in_specs, out_specs, out_shape, aliases):
    return pl.pallas_call(body, name=name, in_specs=in_specs, out_specs=out_specs, out_shape=out_shape,
                          input_output_aliases=aliases,
                          compiler_params=pltpu.CompilerParams(has_side_effects=pltpu.SideEffectType.DATAFLOW_SIDE_EFFECTING))


def _gather_send(name, gs, groups, after):
    n = len(gs)

    def body(*refs):
        g, sems, token = refs[:n], refs[n + len(after):n + len(after) + 2 * len(groups)], refs[-1]
        x, y, c, chips = _place()
        me = 2 * x + y
        for gi, members in enumerate(groups):
            for a, i in enumerate(members):
                for k, (px, py) in enumerate(chips):
                    _remote(g[i].at[me, c], g[i].at[me, c], sems[2 * gi].at[3 * a + k], sems[2 * gi + 1].at[3 * a + k],
                            (px, py, c)).start()
        token[...] = jnp.zeros_like(token)

    sem_shapes = [pltpu.SemaphoreType.DMA((3 * len(members),)) for members in groups for _ in range(2)]
    out = _split_copy_call(body, name=name, in_specs=[HBM] * n + [pl.BlockSpec(memory_space=pl.ANY)] * len(after),
                           out_specs=[SEM] * len(sem_shapes) + [HBM] * n + [VMEM],
                           out_shape=sem_shapes + [pltpu.HBM(a.shape, a.dtype) for a in gs] + [_sds((SUBLANES, LANES), F32)],
                           aliases={i: len(sem_shapes) + i for i in range(n)})(*_hbm(gs), *after)
    return out[:len(sem_shapes)], out[len(sem_shapes):-1], out[-1]


def _gather_wait(name, gs, send_sems, recv_sems, after):
    n = len(gs)

    def body(*refs):
        g, ssem, rsem = refs[:n], refs[n], refs[n + 1]
        x, y, c, chips = _place()
        me = 2 * x + y
        for a in range(n):
            for k, (px, py) in enumerate(chips):
                landed = g[a].at[2 * px + py, c]
                cp = _remote(g[a].at[me, c], landed, ssem.at[3 * a + k], rsem.at[3 * a + k], (px, py, c))
                cp.wait_recv()
                cp.wait_send()

    return _split_copy_call(body, name=name, in_specs=[HBM] * n + [SEM, SEM] + [pl.BlockSpec(memory_space=pl.ANY)] * len(after),
                            out_specs=[HBM] * n, out_shape=[pltpu.HBM(a.shape, a.dtype) for a in gs],
                            aliases={i: i for i in range(n)})(*gs, send_sems, recv_sems, *after)


def _gather_pass(name, gs):
    n = len(gs)

    def body(*refs):
        g, send_sems, recv_sems = refs[n:2 * n], refs[-2], refs[-1]
        x, y, c, chips = _place()
        sibling = (x, y, 1 - c)
        passed = [_remote(g[i].at[2 * px + py, c], g[i].at[2 * px + py, c], send_sems.at[3 * i + k], recv_sems.at[3 * i + k], sibling)
                  for i in range(n) for k, (px, py) in enumerate(chips)]
        for cp in passed:
            cp.start()
        for i in range(n):
            for k, (px, py) in enumerate(chips):
                theirs = g[i].at[2 * px + py, 1 - c]
                _remote(theirs, theirs, send_sems.at[3 * i + k], recv_sems.at[3 * i + k], sibling).wait_recv()
        for cp in passed:
            cp.wait_send()

    return _call(body, name=name, in_specs=[HBM] * n, out_specs=[HBM] * n, out_shape=[_sds(a.shape, a.dtype) for a in gs],
                 aliases={i: i for i in range(n)},
                 scratch=[pltpu.SemaphoreType.DMA((3 * n,)), pltpu.SemaphoreType.DMA((3 * n,))])(*gs)


def _scatter_send(name, ps):
    n = len(ps)

    def body(*refs):
        p, r, ssem, rsem, token = refs[:n], refs[n:2 * n], refs[2 * n], refs[2 * n + 1], refs[-1]
        x, y, c, chips = _place()
        for i in range(n):
            for k, (px, py) in enumerate(chips):
                _remote(p[i].at[2 * px + py], r[i].at[k], ssem.at[3 * i + k], rsem.at[3 * i + k], (px, py, c)).start()
        token[...] = jnp.zeros_like(token)

    lands = [lax.empty((N_CHIPS - 1,) + a.shape[1:], a.dtype) for a in ps]
    sem = pltpu.SemaphoreType.DMA((3 * n,))
    out = _split_copy_call(body, name=name, in_specs=[HBM] * (2 * n), out_specs=[SEM, SEM] + [HBM] * (2 * n) + [VMEM],
                           out_shape=[sem, sem] + [pltpu.HBM(a.shape, a.dtype) for a in list(ps) + lands] + [_sds((SUBLANES, LANES), F32)],
                           aliases={i: 2 + i for i in range(2 * n)})(*_hbm(list(ps) + lands))
    return out[0], out[1], out[2:2 + n], out[2 + n:2 + 2 * n], out[-1]


def _scatter_wait(name, ps, lands, send_sems, recv_sems, after):
    n = len(ps)

    def body(*refs):
        p, r, ssem, rsem = refs[:n], refs[n:2 * n], refs[2 * n], refs[2 * n + 1]
        x, y, c, chips = _place()
        for i in range(n):
            for k, (px, py) in enumerate(chips):
                cp = _remote(p[i].at[2 * px + py], r[i].at[k], ssem.at[3 * i + k], rsem.at[3 * i + k], (px, py, c))
                cp.wait_recv()
                cp.wait_send()

    out = _split_copy_call(body, name=name, in_specs=[HBM] * (2 * n) + [SEM, SEM] + [pl.BlockSpec(memory_space=pl.ANY)] * len(after),
                           out_specs=[HBM] * (2 * n), out_shape=[pltpu.HBM(a.shape, a.dtype) for a in list(ps) + list(lands)],
                           aliases={i: i for i in range(2 * n)})(*ps, *lands, send_sems, recv_sems, *after)
    return out[:n], out[n:]


def _exchange_send(name, gs):
    n = len(gs)

    def body(*refs):
        g, r, ssem, rsem, token = refs[:n], refs[n:2 * n], refs[2 * n], refs[2 * n + 1], refs[-1]
        x, y, c, _ = _place()
        for i in range(n):
            _remote(g[i].at[:, 1 - c], r[i], ssem.at[i], rsem.at[i], (x, y, 1 - c)).start()
        token[...] = jnp.zeros_like(token)

    lands = [lax.empty((a.shape[0],) + a.shape[2:], a.dtype) for a in gs]
    sem = pltpu.SemaphoreType.DMA((n,))
    out = _split_copy_call(body, name=name, in_specs=[HBM] * (2 * n), out_specs=[SEM, SEM] + [HBM] * (2 * n) + [VMEM],
                           out_shape=[sem, sem] + [pltpu.HBM(a.shape, a.dtype) for a in list(gs) + lands] + [_sds((SUBLANES, LANES), F32)],
                           aliases={i: 2 + i for i in range(2 * n)})(*_hbm(list(gs) + lands))
    return out[0], out[1], out[2:2 + n], out[2 + n:2 + 2 * n], out[-1]


def _exchange_wait(name, gs, lands, send_sems, recv_sems, after):
    n = len(gs)

    def body(*refs):
        g, r, ssem, rsem = refs[:n], refs[n:2 * n], refs[2 * n], refs[2 * n + 1]
        x, y, c, _ = _place()
        for i in range(n):
            cp = _remote(g[i].at[:, 1 - c], r[i], ssem.at[i], rsem.at[i], (x, y, 1 - c))
            cp.wait_recv()
            cp.wait_send()

    out = _split_copy_call(body, name=name, in_specs=[HBM] * (2 * n) + [SEM, SEM] + [pl.BlockSpec(memory_space=pl.ANY)] * len(after),
                           out_specs=[HBM] * (2 * n), out_shape=[pltpu.HBM(a.shape, a.dtype) for a in list(gs) + list(lands)],
                           aliases={i: i for i in range(2 * n)})(*gs, *lands, send_sems, recv_sems, *after)
    return out[:n], out[n:]


def _sibling_share(name, fs):
    n = len(fs)

    def body(*refs):
        f, send_sems, recv_sems = refs[n:2 * n], refs[-2], refs[-1]
        x, y, c, _ = _place()
        sends = [_remote(f[i].at[c], f[i].at[c], send_sems.at[i], recv_sems.at[i], (x, y, 1 - c)) for i in range(n)]
        for cp in sends:
            cp.start()
        for i in range(n):
            theirs = f[i].at[1 - c]
            _remote(theirs, theirs, send_sems.at[i], recv_sems.at[i], (x, y, 1 - c)).wait_recv()
        for cp in sends:
            cp.wait_send()

    return _call(body, name=name, in_specs=[HBM] * n, out_specs=[HBM] * n,
                 out_shape=[_sds(a.shape, a.dtype) for a in fs], aliases={i: i for i in range(n)},
                 scratch=[pltpu.SemaphoreType.DMA((n,)), pltpu.SemaphoreType.DMA((n,))])(*fs)


def _all_reduce_small(name, v):
    rows = v.shape[0] // 2
    halves = (2, rows, LANES)

    def body(v_ref, o_ref, from_sibling, chip_sums, send_sems, recv_sems):
        x, y, c, chips = _place()
        me, sibling = 2 * x + y, (x, y, 1 - c)
        swap = _remote(v_ref.at[1 - c], from_sibling, send_sems.at[0], recv_sems.at[0], sibling)
        swap.start()
        swap.wait()
        chip_sums[me] = v_ref[c] + from_sibling[...]
        sends = [_remote(chip_sums.at[me], chip_sums.at[me], send_sems.at[1 + k], recv_sems.at[1 + k], (px, py, c))
                 for k, (px, py) in enumerate(chips)]
        for cp in sends:
            cp.start()
        for k, (px, py) in enumerate(chips):
            theirs = chip_sums.at[2 * px + py]
            _remote(theirs, theirs, send_sems.at[1 + k], recv_sems.at[1 + k], (px, py, c)).wait_recv()
        for cp in sends:
            cp.wait_send()
        acc = chip_sums[0]
        for j in range(1, N_CHIPS):
            acc = acc + chip_sums[j]
        o_ref[c] = acc
        share = _remote(o_ref.at[c], o_ref.at[c], send_sems.at[4], recv_sems.at[4], sibling)
        share.start()
        share.wait_send()
        _remote(o_ref.at[1 - c], o_ref.at[1 - c], send_sems.at[4], recv_sems.at[4], sibling).wait_recv()

    return _call(body, name=name, in_specs=[VMEM], out_specs=VMEM, out_shape=_sds(halves, F32),
                 scratch=[pltpu.VMEM((rows, LANES), F32), pltpu.VMEM((N_CHIPS, rows, LANES), F32),
                          pltpu.SemaphoreType.DMA((5,)), pltpu.SemaphoreType.DMA((5,))])(v.reshape(halves)).reshape(v.shape)


def _add_halves(name, g, r, c):
    _, _, rows, C = g.shape
    tr = _row_tile(rows)

    def body(c_ref, g_ref, r_ref, o_ref):
        o_ref[...] = (g_ref[...] + r_ref[...]).astype(BF16)

    spec = BS((None, tr, C), lambda j, i, c_ref: (j, i, 0))
    return _prefetch_call(body, name=name, grid=(N_CHIPS, rows // tr),
                          in_specs=[BS((None, None, tr, C), lambda j, i, c_ref: (j, c_ref[0], i, 0)), spec], out_specs=spec,
                          out_shape=pltpu.HBM((N_CHIPS, rows, C), BF16))(c, g, r)


def _sum_partials(name, p, r, chip_c):
    _, rows, C = p.shape
    tr = _row_tile(rows)

    def body(s_ref, p_ref, r_ref, o_ref):
        acc = p_ref[...].astype(F32)
        for k in range(N_CHIPS - 1):
            acc = acc + r_ref[k].astype(F32)
        o_ref[...] = acc

    return _prefetch_call(body, name=name, grid=(rows // tr,),
                          in_specs=[BS((None, tr, C), lambda i, s: (s[0], i, 0)), BS((N_CHIPS - 1, tr, C), lambda i, s: (0, i, 0))],
                          out_specs=BS((None, tr, C), lambda i, s: (s[1], i, 0)), out_shape=pltpu.HBM((2, rows, C), F32))(chip_c, p, r)


_SHARDED = ("even_w_in", "even_w_out", "odd_w_in", "q_b", "kv_b", "odd_w_out", "ffn_w_gate", "ffn_w_up", "ffn_w_down")
_REPLICATED = ("mix_norm", "ffn_norm", "sg_ln_g", "sg_w_s", "sg_b_s", "pool_w", "q_norm", "k_norm")
_SMALL_SHARDED = ("sc_conv_w", "pool_scale", "q_a_norm", "kv_a_norm")
_WEIGHTS = ("mix_norm", "ffn_norm", "even_w_in", "sg_ln_g", "sg_w_s", "sg_b_s", "sc_conv_w", "even_w_out", "odd_w_in", "pool_w",
            "pool_scale", "q_a_norm", "q_b", "kv_a_norm", "kv_b", "q_norm", "k_norm", "odd_w_out", "ffn_w_gate", "ffn_w_up",
            "ffn_w_down")


def _pad_rows(flat, width, align):
    n = flat.shape[0]
    rows = -(-n // (width * align)) * align
    return jnp.pad(flat, (0, rows * width - n)).reshape(rows, width)


_GROUPS = {"even": ("even_w_in", "even_w_out"),
           "ffn0": ("ffn_w_gate0", "ffn_w_up0", "ffn_w_down0"),
           "odd": ("odd_w_in", "q_b", "kv_b", "odd_w_out"),
           "ffn1": ("ffn_w_gate1", "ffn_w_up1", "ffn_w_down1")}


def _place_shards(shards, names, chip, after):
    placed = []
    for n in names:
        weight, layer = (n[:-1], int(n[-1])) if n[-1].isdigit() else (n, 0)
        a = shards[weight]
        placed.append(_cast_place(f"place_{n}", a.reshape(a.shape[0], 2, a.shape[1] // 2, a.shape[2]), layer, chip, after))
    return placed


def _whole_weights(gathered):
    out = {n: a.reshape(N_CHIPS, -1, a.shape[-1]) for n, a in gathered.items()}
    for n in ("q_b", "kv_b"):
        if n in out:
            out[n] = out[n].transpose(1, 0, 2).reshape(out[n].shape[1], -1)
    for n in ("even_w_out", "odd_w_in", "odd_w_out"):
        if n in out:
            out[n] = out[n].reshape(-1, out[n].shape[-1])
    return out


def _forward_backward(x, positions, target, small, fetch, emit, advance):
    batch, seq, _ = x.shape
    T = batch * seq
    tm = _token_tile(seq)
    x0 = x.reshape(T, D_MODEL)

    inv_freq = ROPE_THETA ** (-jnp.arange(0, QK_ROPE, 2, dtype=F32) / QK_ROPE)
    ang = (positions.astype(F32)[..., None] * inv_freq).reshape(T, QK_ROPE // 2)
    cos, sin = jnp.cos(ang), jnp.sin(ang)
    pad = jnp.zeros((T, LANES - QK_ROPE), F32)
    cos_t = jnp.concatenate([cos, cos, pad], axis=1)
    sin_t = jnp.concatenate([-sin, sin, pad], axis=1)

    tril = jnp.tril(jnp.ones((SG_CHUNK, SG_CHUNK), bool))
    w_tril = jnp.where(tril[None], small["sg_w_s"][0], 0.0).astype(BF16)
    b_lanes = jnp.broadcast_to(small["sg_b_s"][0][:, :, None], (SG_HEADS, SG_CHUNK, SG_DIM))
    conv_w = jnp.pad(small["sc_conv_w"][0], ((0, SUBLANES - CONV_TAPS), (0, 0)))
    ln_g = small["sg_ln_g"]
    pool_diag = jnp.zeros((POOL_WIDTH, POOL_WIDTH), F32)
    for g in range(len(POOL_WINDOWS)):
        pool_diag = pool_diag.at[POOL_DIM * g:POOL_DIM * (g + 1), POOL_DIM * g:POOL_DIM * (g + 1)].set(small["pool_w"][0, g])
    pool_diag = pool_diag.astype(BF16)
    pool_scale = small["pool_scale"]
    q_g = jnp.pad(small["q_norm"], ((0, 0), (0, QK_PAD - QK_DIM)))
    k_g = jnp.pad(small["k_norm"], ((0, 0), (0, QK_PAD - QK_DIM)))
    qa_g, kva_g = small["q_a_norm"], small["kv_a_norm"]
    in_shard = EVEN_IN // N_CHIPS

    def ffn_weights(l, w):
        return w[f"ffn_w_gate{l}"], w[f"ffn_w_up{l}"], w[f"ffn_w_down{l}"]

    W = fetch("even", ())
    w_in_even = W["even_w_in"]
    h0 = _rmsnorm_fwd("mix0_norm", x0, small["mix_norm"][0], tm)
    tb = _big_tile(T)
    proj0 = _matmul("even_in", "nn", [(h0, w_in_even)],
                    [(BS((tb, D_MODEL), lambda i, j, k: (i, 0)), BS((None, D_MODEL, in_shard), lambda i, j, k: (j, 0, 0)))],
                    (T // tb, N_CHIPS, 1), _sds((T, EVEN_IN), F32), BS((tb, in_shard), lambda i, j, k: (i, j)), (tb, in_shard))
    mix0 = _even_mixer_fwd(proj0, ln_g, w_tril, b_lanes, conv_w, seq, tm)
    w_out_even = W["even_w_out"]
    x1, h1 = _mm("even_out", "nn", mix0, w_out_even, F32, tk=1024, add=x0, fused=_norm_tail(small["ffn_norm"][0], T, tb))
    ffn0 = ffn_weights(0, fetch("ffn0", (x1,)))
    (x2, h2), ffn0_saved = _ffn_fwd(0, x1, h1, *ffn0, _norm_tail(small["mix_norm"][1], T, tb))
    W = fetch("odd", (x2,))
    w_in_odd = jnp.pad(W["odd_w_in"], ((0, 0), (0, ODD_IN_PAD - ODD_IN)))
    q_b = jnp.pad(W["q_b"].reshape(Q_LORA, HEADS, QK_DIM).transpose(1, 0, 2), ((0, 0), (0, 0), (0, QK_PAD - QK_DIM)))
    kv_b = W["kv_b"].reshape(KV_LORA, HEADS, QK_NOPE + V_DIM).transpose(1, 0, 2)
    proj1 = _mm("odd_in", "nn", h2, w_in_odd, F32, tk=1024)
    mix1 = _pool_fwd(proj1, pool_diag, pool_scale, seq, tm)
    q, k, v = _mla_qkv_fwd(proj1, cos_t, sin_t, qa_g, kva_g, q_b, kv_b, q_g, k_g, tm)
    mix1, lse = _flash_fwd(q, k, v, mix1, batch, seq)
    x3, h3 = _mm("odd_out", "nn", mix1, W["odd_w_out"], F32, tk=1024, add=x2, fused=_norm_tail(small["ffn_norm"][1], T, tb))
    ffn1 = ffn_weights(1, fetch("ffn1", (x3,)))
    (dy, sq), ffn1_saved = _ffn_fwd(1, x3, h3, *ffn1, _loss_tail(target.reshape(T, D_MODEL), tb))

    G = {}
    dx3, dffn_g1 = _ffn_bwd(1, x3, small["ffn_norm"][1], *ffn1, ffn1_saved, dy, emit)
    dmix1 = _mm("odd_out_dx", "nt", dx3, W["odd_w_out"], BF16, tk=1024, after=advance(dx3))
    dw_out_odd = _mm("odd_out_dw", "tn", mix1, dx3, F32, hbm_out=True)
    dq, dk, dv = _flash_bwd(q, k, v, dmix1, mix1, lse, batch, seq)
    dz_pool, dpool_diag, G["pool_scale"] = _pool_bwd(proj1, dmix1, pool_diag, pool_scale, seq, tm)
    dproj1, dq_b, dkv_b, dq_g, dk_g, G["q_a_norm"], G["kv_a_norm"] = _mla_qkv_bwd(
        proj1, cos_t, sin_t, qa_g, kva_g, q_b, kv_b, q_g, k_g, dq, dk, dv, dz_pool, tm)
    G["pool_w"] = jnp.stack([dpool_diag[POOL_DIM * g:POOL_DIM * (g + 1), POOL_DIM * g:POOL_DIM * (g + 1)]
                             for g in range(len(POOL_WINDOWS))])[None]
    G["q_norm"], G["k_norm"] = dq_g[:, :QK_DIM], dk_g[:, :QK_DIM]
    dw_in_odd = _mm("odd_in_dw", "tn", h2, dproj1, F32, tn=ODD_IN, hbm_out=True)

    def shard_major(g, cols):
        return g.reshape(g.shape[0], N_CHIPS, cols).transpose(1, 0, 2)

    behind = emit("odd", {"odd_w_in": dw_in_odd.reshape(N_CHIPS, -1, ODD_IN),
                          "q_b": shard_major(dq_b[:, :, :QK_DIM].transpose(1, 0, 2).reshape(Q_LORA, HEADS * QK_DIM), HEADS * QK_DIM // N_CHIPS),
                          "kv_b": shard_major(dkv_b.transpose(1, 0, 2).reshape(KV_LORA, HEADS * (QK_NOPE + V_DIM)),
                                              HEADS * (QK_NOPE + V_DIM) // N_CHIPS),
                          "odd_w_out": dw_out_odd.reshape(N_CHIPS, -1, D_MODEL)})
    dx2, dmix_g1 = _mm("odd_in_dx", "nt", dproj1, W["odd_w_in"], F32, tk=ODD_IN, after=behind,
                       fused=_norm_bwd_tail(x2, small["mix_norm"][1], dx3, tb))
    dx1, dffn_g0 = _ffn_bwd(0, x1, small["ffn_norm"][0], *ffn0, ffn0_saved, dx2, emit, after=advance(dx2))
    dmix0 = _mm("even_out_dx", "nt", dx1, w_out_even, F32, tk=1024, after=advance(dx1))
    dw_out_even = _mm("even_out_dw", "tn", mix0, dx1, F32, hbm_out=True)
    dproj0, dw_s, db_lanes, G["sg_ln_g"], dconv = _even_mixer_bwd(proj0, dmix0, ln_g, w_tril, b_lanes, conv_w, seq, tm)
    G["sg_w_s"] = dw_s[None]
    G["sg_b_s"] = jnp.sum(db_lanes, axis=-1)[None]
    G["sc_conv_w"] = dconv[None, :CONV_TAPS]
    tail, shapes, specs = _norm_bwd_tail(x0, small["mix_norm"][0], dx1, tb)
    dx0, dmix_g0 = _matmul("even_in_dx", "nt", [(dproj0, w_in_even)],
                           [(BS((tb, in_shard), lambda i, j, k: (i, k)), BS((None, D_MODEL, in_shard), lambda i, j, k: (k, 0, 0)))],
                           (T // tb, 1, N_CHIPS), shapes, specs, (tb, D_MODEL), tail=tail)
    tk = min(512, T)
    dw_in_even = _grad_shards(
        "even_in_dw", h0, dproj0, BS((tk, D_MODEL), lambda k: (k, 0)), BS((tk, EVEN_IN), lambda k: (k, 0)),
        lambda a_ref, b_ref, j: (a_ref[...], b_ref[:, in_shard * j:in_shard * (j + 1)]), (N_CHIPS, D_MODEL, in_shard), T // tk)
    emit("even", {"even_w_in": dw_in_even, "even_w_out": dw_out_even.reshape(N_CHIPS, -1, D_MODEL)})
    G["mix_norm"] = jnp.concatenate([dmix_g0, dmix_g1], axis=0)
    G["ffn_norm"] = jnp.concatenate([dffn_g0, dffn_g1], axis=0)
    return sq[0, 0], dx0.reshape(batch, seq, D_MODEL), G


def _small_vector(parts, names):
    flat = jnp.concatenate([parts[n].astype(F32).reshape(-1) for n in names])
    return _pad_rows(flat, LANES, 2 * SUBLANES)


def _split_small(vec, like, names):
    out, off, flat = {}, 0, vec.reshape(-1)
    for n in names:
        size = math.prod(like[n].shape)
        out[n] = flat[off:off + size].reshape(like[n].shape)
        off += size
    return out


def _whole_shape(a):
    return a.shape[:-1] + (a.shape[-1] * N_CHIPS,)


def kernel(x, positions, mix_norm, ffn_norm, even_w_in, sg_ln_g, sg_w_s, sg_b_s, sc_conv_w, even_w_out, odd_w_in, pool_w, pool_scale, q_a_norm, q_b, kv_a_norm, kv_b, q_norm, k_norm, odd_w_out, ffn_w_gate, ffn_w_up, ffn_w_down, loss_target, m_mix_norm, m_ffn_norm, m_even_w_in, m_sg_ln_g, m_sg_w_s, m_sg_b_s, m_sc_conv_w, m_even_w_out, m_odd_w_in, m_pool_w, m_pool_scale, m_q_a_norm, m_q_b, m_kv_a_norm, m_kv_b, m_q_norm, m_k_norm, m_odd_w_out, m_ffn_w_gate, m_ffn_w_up, m_ffn_w_down, v_mix_norm, v_ffn_norm, v_even_w_in, v_sg_ln_g, v_sg_w_s, v_sg_b_s, v_sc_conv_w, v_even_w_out, v_odd_w_in, v_pool_w, v_pool_scale, v_q_a_norm, v_q_b, v_kv_a_norm, v_kv_b, v_q_norm, v_k_norm, v_odd_w_out, v_ffn_w_gate, v_ffn_w_up, v_ffn_w_down):
    args = dict(locals())
    w = {n: args[n] for n in _WEIGHTS}
    m = {n: args["m_" + n] for n in _WEIGHTS}
    v = {n: args["v_" + n] for n in _WEIGHTS}
    cx, cy, cc = lax.axis_index("x"), lax.axis_index("y"), lax.axis_index("c")
    chip = 2 * cx + cy
    transposed = ("ffn_w_gate", "ffn_w_up")
    for n in transposed:
        w[n], m[n], v[n] = (jnp.swapaxes(t[n], 1, 2) for t in (w, m, v))

    chip_arr = chip.astype(jnp.int32).reshape(1)
    c_arr = cc.astype(jnp.int32).reshape(1)
    group_names = list(_GROUPS)
    placed = {}
    for n in _SMALL_SHARDED:
        a = w[n]
        whole = jnp.zeros(a.shape[:-1] + (N_CHIPS, a.shape[-1]), F32)
        whole = lax.dynamic_update_slice_in_dim(whole, a[..., None, :], chip, axis=a.ndim - 1)
        placed[n] = jnp.where(cc == 0, whole, 0.0).reshape(_whole_shape(a))
    small_whole = _all_reduce_small("gather_small_weights", _small_vector(placed, _SMALL_SHARDED))
    small = dict({n: w[n] for n in _REPLICATED}, **_split_small(small_whole, placed, _SMALL_SHARDED))

    first, rest = list(_GROUPS[group_names[0]]), [n for g in group_names[1:] for n in _GROUPS[g]]
    sems_first, flight_first, token = _gather_send("gather_send_first", _place_shards(w, first, chip_arr, (small_whole,)),
                                                   [list(range(len(first)))], (small_whole,))
    sems_rest, flight_rest, all_sent = _gather_send("gather_send_rest", _place_shards(w, rest, chip_arr, (token,)),
                                                    [[rest.index(n) for n in _GROUPS[g]] for g in group_names[1:]], ())
    sems = list(sems_first) + list(sems_rest)
    in_flight = dict(zip(first + rest, list(flight_first) + list(flight_rest)))

    def fetch(group, after):
        gi, members = group_names.index(group), _GROUPS[group]
        after = after if gi else (all_sent,)
        landed = _gather_wait(f"gather_wait_{group}", [in_flight[n] for n in members], sems[2 * gi], sems[2 * gi + 1], after)
        return _whole_weights(dict(zip(members, _gather_pass(f"gather_pass_{group}", landed))))

    swapping, pending, arrived = [], [], {}

    def settle(after):
        names, ps, lands, send_sems, recv_sems = pending.pop()
        ps, lands = _scatter_wait(f"scatter_wait_{names[0]}", ps, lands, send_sems, recv_sems, after)
        arrived.update({n: (p, r) for n, p, r in zip(names, ps, lands)})

    def emit(group, grads):
        names = _GROUPS[group]
        halves = [grads[n].reshape(N_CHIPS, 2, grads[n].shape[1] // 2, grads[n].shape[2]) for n in names]
        send_sems, recv_sems, halves, lands, token = _exchange_send(f"exchange_send_{group}", halves)
        swapping.append((group, halves, lands, send_sems, recv_sems))
        return (token,)

    def advance(done):
        done = done if isinstance(done, tuple) else (done,)
        group, halves, lands, send_sems, recv_sems = swapping.pop()
        names = _GROUPS[group]
        halves, lands = _exchange_wait(f"exchange_wait_{group}", halves, lands, send_sems, recv_sems, done)
        partial = [_add_halves(f"add_{n}", g, r, c_arr) for n, g, r in zip(names, halves, lands)]
        if pending:
            settle(done)
        send_sems, recv_sems, ps, lands, token = _scatter_send(f"scatter_send_{group}", partial)
        pending.append((names, ps, lands, send_sems, recv_sems))
        return (token,)

    sq, grad_x, G = _forward_backward(x, positions, loss_target, small, fetch, emit, advance)
    loss = lax.psum(0.5 * sq / D_MODEL, ("x", "y", "c"))

    small_names = _REPLICATED + _SMALL_SHARDED
    summed = _split_small(_all_reduce_small("reduce_small_grads", _small_vector(G, small_names)), G, small_names)
    grads = {n: summed[n] for n in _REPLICATED}
    for n in _SMALL_SHARDED:
        a = w[n]
        grads[n] = lax.dynamic_slice_in_dim(summed[n].reshape(a.shape[:-1] + (N_CHIPS, a.shape[-1])), chip, 1,
                                            axis=a.ndim - 1).reshape(a.shape)

    chip_c = jnp.stack([chip, cc]).astype(jnp.int32)
    out = {}

    def finish(group):
        names, tokens = _GROUPS[group], []
        sums = [_sum_partials(f"sum_{n}", *arrived[n], chip_c) for n in names]
        for n, f in zip(names, _sibling_share(f"grad_share_{group}", sums)):
            weight, layer = (n[:-1], int(n[-1])) if n[-1].isdigit() else (n, 0)
            *out[weight], token = _adamw(f"adamw_{weight}", w[weight], f.reshape(-1, f.shape[-1]), m[weight], v[weight], layer,
                                         out.get(weight, ()))
            tokens.append(token)
        return tuple(tokens)

    advance(finish(group_names[3]) + finish(group_names[2]))
    settle(finish(group_names[1]))
    finish(group_names[0])
    packed = [_small_vector(d, small_names) for d in (w, grads, m, v)]
    res = _adamw("adamw_small", packed[0][None], packed[1], packed[2][None], packed[3][None])
    delta_s, m_s, v_s = (_split_small(r, w, small_names) for r in res[1:4])
    for n in small_names:
        out[n] = (grads[n], delta_s[n], m_s[n], v_s[n])
    for n in transposed:
        out[n] = tuple(jnp.swapaxes(t, 1, 2) for t in out[n])

    return (loss, grad_x, *[out[n][0] for n in _WEIGHTS], *[out[n][1] for n in _WEIGHTS],
            *[out[n][2] for n in _WEIGHTS], *[out[n][3] for n in _WEIGHTS])
```

```python
import functools
import math

import jax
import jax.numpy as jnp
from jax import lax
from jax.experimental import pallas as pl
from jax.experimental.pallas import tpu as pltpu

F32, BF16 = jnp.float32, jnp.bfloat16
BS = pl.BlockSpec

D_MODEL = 1024
EPS = 1e-6
NEG_INF = -1e30
SG_HEADS, SG_DIM, SG_WIDTH, SG_CHUNK = 4, 128, 512, 128
SC_WIDTH, CONV_TAPS = 512, 3
EVEN_IN = 2 * SG_WIDTH + 3 * SC_WIDTH
POOL_WINDOWS = (2, 4, 8, 16)
POOL_DIM, POOL_WIDTH = 64, 256
POOL_HALO = 16
HEADS, Q_LORA, KV_LORA, QK_NOPE, QK_ROPE, V_DIM = 6, 384, 256, 128, 64, 128
QK_DIM = QK_NOPE + QK_ROPE
QK_PAD = 256
ODD_IN = POOL_WIDTH + Q_LORA + KV_LORA + QK_ROPE
ODD_IN_PAD = 1024
ROPE_THETA = 10000.0
ATTN_SCALE = QK_DIM ** -0.5
D_FF, N_CHIPS = 2816, 4
FF_SHARD = D_FF // N_CHIPS
ADAM_LR, ADAM_B1, ADAM_B2, ADAM_EPS, ADAM_WD, ADAM_STEP = 0.001, 0.9, 0.999, 1e-08, 0.01, 10
VMEM_LIMIT_V7X = 48 * 2**20
LANES, SUBLANES = 128, 8
MESH = pl.DeviceIdType.MESH
HBM = pl.BlockSpec(memory_space=pltpu.HBM)
VMEM = pl.BlockSpec(memory_space=pltpu.VMEM)

_DIMS = {"nn": (((1,), (0,)), ((), ())), "nt": (((1,), (1,)), ((), ())), "tn": (((0,), (0,)), ((), ()))}


def _dot(a, b, mode="nn"):
    return lax.dot_general(a.astype(BF16), b.astype(BF16), _DIMS[mode], preferred_element_type=F32)


def _call(body, *, name, out_shape, in_specs, out_specs, grid=(), scratch=(), aliases=None, after=()):
    params = pltpu.CompilerParams(vmem_limit_bytes=VMEM_LIMIT_V7X,
                                  **({"dimension_semantics": ("arbitrary",) * len(grid)} if grid else {}))
    n_in, n_after = len(in_specs), len(after)
    kernel_body = body if not after else (lambda *refs: body(*refs[:n_in], *refs[n_in + n_after:]))
    call = pl.pallas_call(kernel_body, name=name, grid=grid, in_specs=list(in_specs) + [pl.BlockSpec(memory_space=pl.ANY)] * n_after,
                          out_specs=out_specs, out_shape=out_shape, scratch_shapes=list(scratch),
                          input_output_aliases=aliases or {}, compiler_params=params)
    return (lambda *ops: call(*ops, *after)) if after else call


def _sds(shape, dtype):
    return jax.ShapeDtypeStruct(tuple(shape), dtype)


def _token_tile(seq):
    return 512 if seq % 512 == 0 else seq


_TAIL_ROWS = 256


def _matmul(name, mode, pairs, pair_specs, grid, out_shape, out_spec, acc_shape, add=None, add_spec=None, after=(), tail=None):
    n, nk = len(pairs), grid[-1]
    n_add = int(add is not None)
    n_tail = len(tail[0]) if tail else 0
    n_in = 2 * n + n_add + n_tail
    n_out = len(out_shape) if tail else 1

    def body(*refs):
        ab = refs[:2 * n]
        add_ref = refs[2 * n] if n_add else None
        tail_refs, outs = refs[2 * n + n_add:n_in], refs[n_in:n_in + n_out]
        first = pl.program_id(0) == 0

        def finish(result):
            if tail is None:
                r = result(slice(None))
                outs[0][...] = (r if add_ref is None else r + add_ref[...]).astype(outs[0].dtype)
                return
            for lo in range(0, acc_shape[0], _TAIL_ROWS):
                rows = slice(lo, min(lo + _TAIL_ROWS, acc_shape[0]))
                r = result(rows)
                tail[2](rows, r if add_ref is None else r + add_ref[rows, :], first, tail_refs, outs)

        if nk == 1:
            r = _dot(ab[0][...], ab[1][...], mode)
            for p in range(1, n):
                r = r + _dot(ab[2 * p][...], ab[2 * p + 1][...], mode)
            finish(lambda rows: r[rows])
            return
        acc = refs[-1]
        k = pl.program_id(len(grid) - 1)

        @pl.when(k == 0)
        def _():
            acc[...] = jnp.zeros_like(acc)

        for p in range(n):
            acc[...] += _dot(ab[2 * p][...], ab[2 * p + 1][...], mode)

        @pl.when(k == nk - 1)
        def _():
            finish(lambda rows: acc[rows, :])

    ops = [t for pr in pairs for t in pr] + ([add] if n_add else []) + (list(tail[0]) if tail else [])
    specs = [s for pr in pair_specs for s in pr] + ([add_spec] if n_add else []) + (list(tail[1]) if tail else [])
    return _call(body, name=name, grid=grid, in_specs=specs, out_specs=out_spec, out_shape=out_shape,
                 scratch=[pltpu.VMEM(acc_shape, F32)] if nk > 1 else [], after=after)(*ops)


def _row_spec(tm, d):
    return BS((tm, d), lambda i, j, k: (i, 0))


def _vec_spec(d):
    return BS((1, d), lambda i, j, k: (0, 0))


def _norm_tail(gain, T, tm):
    d = gain.shape[-1]

    def fn(rows, r, first, tail_refs, outs):
        outs[0][rows, :] = r
        outs[1][rows, :] = (r * lax.rsqrt(jnp.mean(r * r, axis=-1, keepdims=True) + EPS) * tail_refs[0][...]).astype(BF16)

    return ([gain.reshape(1, d)], [_vec_spec(d)], fn), [_sds((T, d), F32), _sds((T, d), BF16)], [_row_spec(tm, d), _row_spec(tm, d)]


def _norm_bwd_tail(x, gain, dres, tm):
    T, d = x.shape

    def fn(rows, r, first, tail_refs, outs):
        x_ref, g_ref, dres_ref = tail_refs
        xv = x_ref[rows, :]
        rstd = lax.rsqrt(jnp.mean(xv * xv, axis=-1, keepdims=True) + EPS)
        xhat = xv * rstd
        if rows.start == 0:
            @pl.when(first)
            def _():
                outs[1][...] = jnp.zeros_like(outs[1])

        outs[1][...] += jnp.sum(r * xhat, axis=0, keepdims=True)
        dxhat = r * g_ref[...]
        outs[0][rows, :] = dres_ref[rows, :] + rstd * (dxhat - xhat * jnp.mean(dxhat * xhat, axis=-1, keepdims=True))

    return (([x, gain.reshape(1, d), dres], [_row_spec(tm, d), _vec_spec(d), _row_spec(tm, d)], fn),
            [_sds((T, d), F32), _sds((1, d), F32)], [_row_spec(tm, d), _vec_spec(d)])


def _loss_tail(target, tm):
    T, d = target.shape

    def fn(rows, r, first, tail_refs, outs):
        e = r - tail_refs[0][rows, :]
        if rows.start == 0:
            @pl.when(first)
            def _():
                outs[1][...] = jnp.zeros_like(outs[1])

        outs[1][...] += jnp.sum(e * e)
        outs[0][rows, :] = e * (1.0 / d)

    return (([target], [_row_spec(tm, d)], fn), [_sds((T, d), F32), _sds((SUBLANES, LANES), F32)],
            [_row_spec(tm, d), BS((SUBLANES, LANES), lambda i, j, k: (0, 0))])


def _grad_shards(name, a, b, a_spec, b_spec, pick, out_shape, n_steps):
    def body(a_ref, b_ref, o_ref):
        @pl.when(pl.program_id(0) == 0)
        def _():
            o_ref[...] = jnp.zeros_like(o_ref)

        for j in range(N_CHIPS):
            aj, bj = pick(a_ref, b_ref, j)
            o_ref[j] += _dot(aj, bj, "tn")

    return _call(body, name=name, grid=(n_steps,), in_specs=[a_spec, b_spec],
                 out_specs=BS(out_shape, lambda k: (0, 0, 0)), out_shape=pltpu.HBM(tuple(out_shape), F32))(a, b)


def _mm(name, mode, a, b, out_dtype, tm=1024, tn=1024, tk=512, add=None, after=(), fused=None, hbm_out=False):
    if mode == "tn":
        (K, M), N = a.shape, b.shape[1]
    else:
        (M, K), N = a.shape, (b.shape[1] if mode == "nn" else b.shape[0])
    tm, tn, tk = min(tm, M), min(tn, N), min(tk, K)
    a_spec = BS((tk, tm), lambda i, j, k: (k, i)) if mode == "tn" else BS((tm, tk), lambda i, j, k: (i, k))
    b_spec = BS((tn, tk), lambda i, j, k: (j, k)) if mode == "nt" else BS((tk, tn), lambda i, j, k: (k, j))
    o_spec = BS((tm, tn), lambda i, j, k: (i, j))
    tail, shapes, specs = fused if fused else (None, pltpu.HBM((M, N), out_dtype) if hbm_out else _sds((M, N), out_dtype), o_spec)
    return _matmul(name, mode, [(a, b)], [(a_spec, b_spec)], (M // tm, N // tn, K // tk), shapes, specs, (tm, tn),
                   add=add, add_spec=o_spec if add is not None else None, after=after, tail=tail)


def _rmsnorm_fwd(name, x, g, tm):
    T, d = x.shape

    def body(x_ref, g_ref, o_ref):
        xv = x_ref[...]
        y = xv * lax.rsqrt(jnp.mean(xv * xv, axis=-1, keepdims=True) + EPS)
        o_ref[...] = (y * g_ref[...]).astype(o_ref.dtype)

    return _call(body, name=name, grid=(T // tm,), in_specs=[BS((tm, d), lambda i: (i, 0)), BS((1, d), lambda i: (0, 0))],
                 out_specs=BS((tm, d), lambda i: (i, 0)), out_shape=_sds((T, d), BF16))(x, g.reshape(1, d))


def _ffn_up(name, h, wg, wu, tm):
    T = h.shape[0]

    def body(h_ref, wg_ref, wu_ref, g_ref, u_ref, a_ref):
        hv = h_ref[...]
        g = _dot(hv, wg_ref[...], "nt")
        u = _dot(hv, wu_ref[...], "nt")
        g_ref[...] = g.astype(BF16)
        u_ref[...] = u.astype(BF16)
        a_ref[...] = (g * (1.0 / (1.0 + jnp.exp(-g))) * u).astype(BF16)

    w_spec = BS((None, FF_SHARD, D_MODEL), lambda j, i: (j, 0, 0))
    o_spec = BS((None, tm, FF_SHARD), lambda j, i: (j, i, 0))
    sh = _sds((N_CHIPS, T, FF_SHARD), BF16)
    return _call(body, name=name, grid=(N_CHIPS, T // tm), in_specs=[BS((tm, D_MODEL), lambda j, i: (i, 0)), w_spec, w_spec],
                 out_specs=[o_spec, o_spec, o_spec], out_shape=[sh, sh, sh])(h, wg, wu)


def _ffn_act_bwd(name, dxo, wd, g, u, tm, after=()):
    T = dxo.shape[0]

    def body(dx_ref, wd_ref, g_ref, u_ref, dg_ref, du_ref):
        da = _dot(dx_ref[...], wd_ref[...], "nt")
        g = g_ref[...].astype(F32)
        sig = 1.0 / (1.0 + jnp.exp(-g))
        dg_ref[...] = (da * u_ref[...].astype(F32) * (sig * (1.0 + g * (1.0 - sig)))).astype(BF16)
        du_ref[...] = (da * (g * sig)).astype(BF16)

    t_spec = BS((None, tm, FF_SHARD), lambda i, j: (j, i, 0))
    sh = _sds((N_CHIPS, T, FF_SHARD), BF16)
    return _call(body, name=name, grid=(T // tm, N_CHIPS),
                 in_specs=[BS((tm, D_MODEL), lambda i, j: (i, 0)), BS((None, FF_SHARD, D_MODEL), lambda i, j: (j, 0, 0)), t_spec, t_spec],
                 out_specs=[t_spec, t_spec], out_shape=[sh, sh], after=after)(dxo, wd, g, u)


def _big_tile(n):
    return min(1024, n)


def _ffn_fwd(l, x, h, wg, wu, wd, fused):
    T = x.shape[0]
    tm = _big_tile(T)
    g, u, a = _ffn_up(f"ffn{l}_up", h, wg, wu, tm)
    tn = D_MODEL
    tail, shapes, specs = fused
    outs = _matmul(f"ffn{l}_down", "nn", [(a, wd)],
                   [(BS((None, tm, FF_SHARD), lambda i, j, k: (k, i, 0)), BS((None, FF_SHARD, tn), lambda i, j, k: (k, 0, j)))],
                   (T // tm, D_MODEL // tn, N_CHIPS), shapes, specs, (tm, tn),
                   add=x, add_spec=BS((tm, tn), lambda i, j, k: (i, j)), tail=tail)
    return outs, (h, g, u, a)


def _ffn_bwd(l, x, gain, wg, wu, wd, saved, dxo, emit, after=()):
    h, g, u, a = saved
    T = x.shape[0]
    tm = _big_tile(T)
    dg, du = _ffn_act_bwd(f"ffn{l}_act_bwd", dxo, wd, g, u, tm, after=after)
    tk = min(512, T)
    tn = D_MODEL
    shards_spec = BS((N_CHIPS, tk, FF_SHARD), lambda k: (0, k, 0))
    rows_spec = BS((tk, D_MODEL), lambda k: (k, 0))

    def dw(nm, act, rows):
        return _grad_shards(nm, act, rows, shards_spec, rows_spec, lambda a_ref, b_ref, j: (a_ref[j], b_ref[...]),
                            (N_CHIPS, FF_SHARD, D_MODEL), T // tk)

    behind = emit(f"ffn{l}", {f"ffn_w_gate{l}": dw(f"ffn{l}_dwg", dg, h), f"ffn_w_up{l}": dw(f"ffn{l}_dwu", du, h),
                              f"ffn_w_down{l}": dw(f"ffn{l}_dwd", a, dxo)})
    act_spec = BS((None, tm, FF_SHARD), lambda i, j, k: (k, i, 0))
    w_spec = BS((None, FF_SHARD, tn), lambda i, j, k: (k, 0, j))
    tail, shapes, specs = _norm_bwd_tail(x, gain, dxo, tm)
    return _matmul(f"ffn{l}_dh", "nn", [(dg, wg), (du, wu)], [(act_spec, w_spec), (act_spec, w_spec)],
                   (T // tm, D_MODEL // tn, N_CHIPS), shapes, specs, (tm, tn), after=behind, tail=tail)


_INV_SQRT2 = 1.0 / math.sqrt(2.0)
_INV_SQRT_2PI = 1.0 / math.sqrt(2.0 * math.pi)


def _gelu(x):
    return 0.5 * x * (1.0 + lax.erf(x * _INV_SQRT2))


def _gelu_grad(x):
    return 0.5 * (1.0 + lax.erf(x * _INV_SQRT2)) + x * jnp.exp(-0.5 * x * x) * _INV_SQRT_2PI


def _shift_down(x, k):
    return pltpu.roll(x, k, 0)


def _shift_up(x, k):
    return pltpu.roll(x, x.shape[0] - k, 0)


def _layer_norm_head(xh):
    xc = xh - jnp.mean(xh, axis=-1, keepdims=True)
    rstd = lax.rsqrt(jnp.mean(xc * xc, axis=-1, keepdims=True) + EPS)
    return xc * rstd, rstd


def _even_halo_specs(tm, n_tiles, col_blocks, after):
    rows = tm // SUBLANES
    last = n_tiles * rows - 1
    if after:
        return [BS((SUBLANES, 512), functools.partial(lambda cb, i: (jnp.minimum((i + 1) * rows, last), cb), cb)) for cb in col_blocks]
    return [BS((SUBLANES, 512), functools.partial(lambda cb, i: (jnp.maximum(i * rows - 1, 0), cb), cb)) for cb in col_blocks]


def _even_mixer_fwd(proj, ln_g, w_tril, b_lanes, conv_w, seq, tm):
    T = proj.shape[0]
    tiles_per_seq = seq // tm

    def body(p_ref, hc_ref, hh_ref, lng_ref, w_ref, bb_ref, cw_ref, o_ref):
        first = pl.program_id(0) % tiles_per_seq == 0
        for h in range(SG_HEADS):
            cols = slice(SG_DIM * h, SG_DIM * (h + 1))
            vhat, _ = _layer_norm_head(_gelu(p_ref[:, SG_WIDTH + SG_DIM * h:SG_WIDTH + SG_DIM * (h + 1)]))
            vln = (vhat * lng_ref[:, cols]).astype(BF16)
            for k in range(tm // SG_CHUNK):
                rows = slice(SG_CHUNK * k, SG_CHUNK * (k + 1))
                mixed = _dot(w_ref[h], vln[rows]) + bb_ref[h]
                o_ref[rows, cols] = (_gelu(p_ref[rows, cols]) * mixed).astype(BF16)
        z = p_ref[:, 1536:2048] * p_ref[:, 2048:2560]
        zz = jnp.concatenate([jnp.where(first, 0.0, hc_ref[...] * hh_ref[...]), z], axis=0)
        y = cw_ref[0:1, :] * _shift_down(zz, 2)[SUBLANES:] + cw_ref[1:2, :] * _shift_down(zz, 1)[SUBLANES:] + cw_ref[2:3, :] * z
        o_ref[:, SG_WIDTH:] = (p_ref[:, 1024:1536] * y).astype(BF16)

    full = lambda shape: BS(shape, lambda i: (0,) * len(shape))
    return _call(body, name="even_mixer_fwd", grid=(T // tm,),
                 in_specs=[BS((tm, EVEN_IN), lambda i: (i, 0))] + _even_halo_specs(tm, T // tm, (3, 4), after=False)
                 + [full((1, SG_WIDTH)), full((SG_HEADS, SG_CHUNK, SG_CHUNK)), full((SG_HEADS, SG_CHUNK, SG_DIM)), full((SUBLANES, SC_WIDTH))],
                 out_specs=BS((tm, D_MODEL), lambda i: (i, 0)), out_shape=_sds((T, D_MODEL), BF16))(
        proj, proj, proj, ln_g, w_tril, b_lanes, conv_w)


def _even_mixer_bwd(proj, dmix, ln_g, w_tril, b_lanes, conv_w, seq, tm):
    T = proj.shape[0]
    n_tiles, tiles_per_seq = T // tm, seq // tm

    def body(p_ref, dm_ref, hc_ref, hh_ref, nd_ref, nb_ref, lng_ref, w_ref, bb_ref, cw_ref,
             dp_ref, dw_ref, db_ref, dlng_ref, dcw_ref):
        i = pl.program_id(0)
        first = i % tiles_per_seq == 0
        last = i % tiles_per_seq == tiles_per_seq - 1

        @pl.when(i == 0)
        def _():
            dw_ref[...] = jnp.zeros_like(dw_ref)
            db_ref[...] = jnp.zeros_like(db_ref)
            dlng_ref[...] = jnp.zeros_like(dlng_ref)
            dcw_ref[...] = jnp.zeros_like(dcw_ref)

        for h in range(SG_HEADS):
            cols = slice(SG_DIM * h, SG_DIM * (h + 1))
            vcols = slice(SG_WIDTH + SG_DIM * h, SG_WIDTH + SG_DIM * (h + 1))
            lng = lng_ref[:, cols]
            for k in range(tm // SG_CHUNK):
                rows = slice(SG_CHUNK * k, SG_CHUNK * (k + 1))
                v = p_ref[rows, vcols]
                vhat, rstd = _layer_norm_head(_gelu(v))
                vln = (vhat * lng).astype(BF16)
                mixed = _dot(w_ref[h], vln) + bb_ref[h]
                u = p_ref[rows, cols]
                da = dm_ref[rows, cols]
                dp_ref[rows, cols] = (da * mixed * _gelu_grad(u)).astype(BF16)
                dmixed = da * _gelu(u)
                db_ref[h] += dmixed
                dw_ref[h] += _dot(dmixed, vln, "nt")
                dvln = _dot(w_ref[h], dmixed, "tn")
                dlng_ref[:, cols] += jnp.sum(dvln * vhat, axis=0, keepdims=True)
                dvhat = dvln * lng
                dgv = rstd * (dvhat - jnp.mean(dvhat, axis=-1, keepdims=True)
                              - vhat * jnp.mean(dvhat * vhat, axis=-1, keepdims=True))
                dp_ref[rows, vcols] = (dgv * _gelu_grad(v)).astype(BF16)

        b = p_ref[:, 1024:1536]
        c = p_ref[:, 1536:2048]
        hv = p_ref[:, 2048:2560]
        z = c * hv
        zz = jnp.concatenate([jnp.where(first, 0.0, hc_ref[...] * hh_ref[...]), z], axis=0)
        z1 = _shift_down(zz, 1)[SUBLANES:]
        z2 = _shift_down(zz, 2)[SUBLANES:]
        w0, w1, w2 = cw_ref[0:1, :], cw_ref[1:2, :], cw_ref[2:3, :]
        dbo = dm_ref[:, SG_WIDTH:]
        dy = dbo * b
        dd = jnp.concatenate([dy, jnp.where(last, 0.0, nd_ref[...] * nb_ref[...])], axis=0)
        dz = w2 * dy + w1 * _shift_up(dd, 1)[:tm] + w0 * _shift_up(dd, 2)[:tm]
        dp_ref[:, 1024:1536] = (dbo * (w0 * z2 + w1 * z1 + w2 * z)).astype(BF16)
        dp_ref[:, 1536:2048] = (dz * hv).astype(BF16)
        dp_ref[:, 2048:2560] = (dz * c).astype(BF16)
        dcw_ref[0:1, :] += jnp.sum(dy * z2, axis=0, keepdims=True)
        dcw_ref[1:2, :] += jnp.sum(dy * z1, axis=0, keepdims=True)
        dcw_ref[2:3, :] += jnp.sum(dy * z, axis=0, keepdims=True)

        @pl.when(i == n_tiles - 1)
        def _():
            t_idx = lax.broadcasted_iota(jnp.int32, (SG_CHUNK, SG_CHUNK), 0)
            s_idx = lax.broadcasted_iota(jnp.int32, (SG_CHUNK, SG_CHUNK), 1)
            for h in range(SG_HEADS):
                dw_ref[h] = jnp.where(t_idx >= s_idx, dw_ref[h], 0.0)

    full = lambda shape: BS(shape, lambda i: (0,) * len(shape))
    sq = (SG_HEADS, SG_CHUNK, SG_CHUNK)
    return _call(body, name="even_mixer_bwd", grid=(n_tiles,),
                 in_specs=[BS((tm, EVEN_IN), lambda i: (i, 0)), BS((tm, D_MODEL), lambda i: (i, 0))]
                 + _even_halo_specs(tm, n_tiles, (3, 4), after=False)
                 + _even_halo_specs(tm, n_tiles, (1,), after=True) + _even_halo_specs(tm, n_tiles, (2,), after=True)
                 + [full((1, SG_WIDTH)), full(sq), full(sq), full((SUBLANES, SC_WIDTH))],
                 out_specs=[BS((tm, EVEN_IN), lambda i: (i, 0)), full(sq), full(sq), full((1, SG_WIDTH)), full((SUBLANES, SC_WIDTH))],
                 out_shape=[_sds((T, EVEN_IN), BF16), _sds(sq, F32), _sds(sq, F32), _sds((1, SG_WIDTH), F32), _sds((SUBLANES, SC_WIDTH), F32)])(
        proj, dmix, proj, proj, dmix, proj, ln_g, w_tril, b_lanes, conv_w)


def _pool_select(vals):
    lane = lax.broadcasted_iota(jnp.int32, vals[0].shape, 1)
    out = vals[-1]
    for g in range(len(vals) - 2, -1, -1):
        out = jnp.where(lane < POOL_DIM * (g + 1), vals[g], out)
    return out


def _pool_counts(pos1):
    lane = lax.broadcasted_iota(jnp.int32, (pos1.shape[0], POOL_WIDTH), 1)
    win = _pool_select([jnp.full(lane.shape, float(w), F32) for w in POOL_WINDOWS])
    return jnp.minimum(pos1, win)


def _pool_means(zz, counts):
    s2 = zz + _shift_down(zz, 1)
    s4 = s2 + _shift_down(s2, 2)
    s8 = s4 + _shift_down(s4, 4)
    s16 = s8 + _shift_down(s8, 8)
    return _pool_select([s2, s4, s8, s16])[POOL_HALO:] / counts


def _pool_halo_spec(tm, n_tiles, after):
    rows = tm // POOL_HALO
    if after:
        return BS((POOL_HALO, POOL_WIDTH), lambda i: (jnp.minimum((i + 1) * rows, n_tiles * rows - 1), 0))
    return BS((POOL_HALO, POOL_WIDTH), lambda i: (jnp.maximum(i * rows - 1, 0), 0))


def _pool_fwd(proj, w_diag, scale, seq, tm):
    T = proj.shape[0]
    tiles_per_seq = seq // tm

    def body(z_ref, zh_ref, w_ref, s_ref, o_ref):
        t = pl.program_id(0) % tiles_per_seq
        z = z_ref[...]
        zz = jnp.concatenate([jnp.where(t == 0, 0.0, zh_ref[...]), z], axis=0)
        pos1 = (lax.broadcasted_iota(jnp.int32, (tm, 1), 0) + (t * tm + 1)).astype(F32)
        pooled = _pool_means(zz, _pool_counts(pos1)) - z
        o_ref[...] = (_dot(pooled, w_ref[...]) * s_ref[...]).astype(BF16)

    full = lambda shape: BS(shape, lambda i: (0,) * len(shape))
    return _call(body, name="pool_fwd", grid=(T // tm,),
                 in_specs=[BS((tm, POOL_WIDTH), lambda i: (i, 0)), _pool_halo_spec(tm, T // tm, False),
                           full((POOL_WIDTH, POOL_WIDTH)), full((1, POOL_WIDTH))],
                 out_specs=BS((tm, POOL_WIDTH), lambda i: (i, 0)), out_shape=_sds((T, D_MODEL), BF16))(proj, proj, w_diag, scale)


def _pool_bwd(proj, dmix, w_diag, scale, seq, tm):
    T = proj.shape[0]
    n_tiles, tiles_per_seq = T // tm, seq // tm

    def body(z_ref, zh_ref, do_ref, don_ref, w_ref, s_ref, dz_ref, dw_ref, ds_ref):
        i = pl.program_id(0)
        t = i % tiles_per_seq

        @pl.when(i == 0)
        def _():
            dw_ref[...] = jnp.zeros_like(dw_ref)
            ds_ref[...] = jnp.zeros_like(ds_ref)

        z = z_ref[...]
        zz = jnp.concatenate([jnp.where(t == 0, 0.0, zh_ref[...]), z], axis=0)
        pos1 = (lax.broadcasted_iota(jnp.int32, (tm, 1), 0) + (t * tm + 1)).astype(F32)
        counts = _pool_counts(pos1)
        pooled = _pool_means(zz, counts) - z
        dout = do_ref[...].astype(F32)
        ds_ref[...] += jnp.sum(dout * _dot(pooled, w_ref[...]), axis=0, keepdims=True)
        dlin = dout * s_ref[...]
        dw_ref[...] += _dot(pooled, dlin, "tn")
        dpooled = _dot(dlin, w_ref[...], "nt")
        dpooled_n = _dot(don_ref[...].astype(F32) * s_ref[...], w_ref[...], "nt")
        pos1_n = (lax.broadcasted_iota(jnp.int32, (POOL_HALO, 1), 0) + ((t + 1) * tm + 1)).astype(F32)
        dmean_n = jnp.where(t == tiles_per_seq - 1, 0.0, dpooled_n / _pool_counts(pos1_n))
        dd = jnp.concatenate([dpooled / counts, dmean_n], axis=0)
        r2 = dd + _shift_up(dd, 1)
        r4 = r2 + _shift_up(r2, 2)
        r8 = r4 + _shift_up(r4, 4)
        r16 = r8 + _shift_up(r8, 8)
        dz_ref[...] = (_pool_select([r2, r4, r8, r16])[:tm] - dpooled).astype(BF16)

    full = lambda shape: BS(shape, lambda i: (0,) * len(shape))
    return _call(body, name="pool_bwd", grid=(n_tiles,),
                 in_specs=[BS((tm, POOL_WIDTH), lambda i: (i, 0)), _pool_halo_spec(tm, n_tiles, False),
                           BS((tm, POOL_WIDTH), lambda i: (i, 0)), _pool_halo_spec(tm, n_tiles, True),
                           full((POOL_WIDTH, POOL_WIDTH)), full((1, POOL_WIDTH))],
                 out_specs=[BS((tm, POOL_WIDTH), lambda i: (i, 0)), full((POOL_WIDTH, POOL_WIDTH)), full((1, POOL_WIDTH))],
                 out_shape=[_sds((T, POOL_WIDTH), BF16), _sds((POOL_WIDTH, POOL_WIDTH), F32), _sds((1, POOL_WIDTH), F32)])(
        proj, proj, dmix, dmix, w_diag, scale)


def _rope_partner(r):
    lane = lax.broadcasted_iota(jnp.int32, r.shape, 1)
    return jnp.where(lane < QK_ROPE // 2, pltpu.roll(r, LANES - QK_ROPE // 2, 1), pltpu.roll(r, QK_ROPE // 2, 1))


def _rope(x, cos, sin_signed):
    r = x[:, QK_NOPE:]
    return jnp.concatenate([x[:, :QK_NOPE], r * cos + _rope_partner(r) * sin_signed], axis=1)


def _rope_transposed(dx, cos, sin_signed):
    dr = dx[:, QK_NOPE:]
    return jnp.concatenate([dx[:, :QK_NOPE], dr * cos + _rope_partner(dr * sin_signed)], axis=1)


def _head_norm(x):
    r = lax.rsqrt(jnp.sum(x * x, axis=-1, keepdims=True) * (1.0 / QK_DIM) + EPS)
    return x * r, r


def _head_norm_bwd(dy, xhat, r, gain):
    dxhat = dy * gain
    return r * (dxhat - xhat * (jnp.sum(dxhat * xhat, axis=-1, keepdims=True) * (1.0 / QK_DIM)))


def _latents(p_ref, qag_ref, kvag_ref):
    ql = p_ref[:, POOL_WIDTH:POOL_WIDTH + Q_LORA]
    kvl = p_ref[:, POOL_WIDTH + Q_LORA:POOL_WIDTH + Q_LORA + KV_LORA]
    rq = lax.rsqrt(jnp.mean(ql * ql, axis=-1, keepdims=True) + EPS)
    rkv = lax.rsqrt(jnp.mean(kvl * kvl, axis=-1, keepdims=True) + EPS)
    return ql * rq, rq, kvl * rkv, rkv


def _mla_specs(tm):
    full = lambda shape: BS(shape, lambda i, h: (0,) * len(shape))
    return [BS((tm, ODD_IN_PAD), lambda i, h: (i, 0)), BS((tm, LANES), lambda i, h: (i, 0)), BS((tm, LANES), lambda i, h: (i, 0)),
            full((1, Q_LORA)), full((1, KV_LORA)), BS((None, Q_LORA, QK_PAD), lambda i, h: (h, 0, 0)),
            BS((None, KV_LORA, QK_PAD), lambda i, h: (h, 0, 0)), full((1, QK_PAD)), full((1, QK_PAD))]


_CHAIN_ROWS = 256


def _mla_qkv_fwd(proj, cos, sin_signed, qa_g, kva_g, q_b, kv_b, q_g, k_g, tm):
    T = proj.shape[0]

    def body(p_ref, cos_ref, sin_ref, qag_ref, kvag_ref, qb_ref, kvb_ref, qg_ref, kg_ref, q_ref, k_ref, v_ref, qn_s, kvn_s):
        @pl.when(pl.program_id(1) == 0)
        def _():
            qhat, _, kvhat, _ = _latents(p_ref, qag_ref, kvag_ref)
            qn_s[...] = (qhat * qag_ref[...]).astype(BF16)
            kvn_s[...] = (kvhat * kvag_ref[...]).astype(BF16)

        cos, sin = cos_ref[...], sin_ref[...]
        qhat, _ = _head_norm(_dot(qn_s[...], qb_ref[...]))
        q_ref[...] = _rope(qhat * qg_ref[...], cos, sin).astype(BF16)
        kv = _dot(kvn_s[...], kvb_ref[...])
        khat, _ = _head_norm(jnp.concatenate([kv[:, :QK_NOPE], p_ref[:, ODD_IN_PAD - LANES:]], axis=1))
        k_ref[...] = _rope(khat * kg_ref[...], cos, sin).astype(BF16)
        v_ref[...] = kv[:, QK_NOPE:].astype(BF16)

    qk_spec = BS((None, tm, QK_PAD), lambda i, h: (h, i, 0))
    return _call(body, name="mla_qkv_fwd", grid=(T // tm, HEADS), in_specs=_mla_specs(tm),
                 out_specs=[qk_spec, qk_spec, BS((None, tm, V_DIM), lambda i, h: (h, i, 0))],
                 out_shape=[_sds((HEADS, T, QK_PAD), BF16), _sds((HEADS, T, QK_PAD), BF16), _sds((HEADS, T, V_DIM), BF16)],
                 scratch=[pltpu.VMEM((tm, Q_LORA), BF16), pltpu.VMEM((tm, KV_LORA), BF16)])(
        proj, cos, sin_signed, qa_g, kva_g, q_b, kv_b, q_g, k_g)


def _mla_qkv_bwd(proj, cos, sin_signed, qa_g, kva_g, q_b, kv_b, q_g, k_g, dq, dk, dv, dz_pool, tm):
    T = proj.shape[0]
    n_tiles = T // tm

    def body(p_ref, cos_ref, sin_ref, qag_ref, kvag_ref, qb_ref, kvb_ref, qg_ref, kg_ref, dq_ref, dk_ref, dv_ref, dzp_ref,
             dp_ref, dqb_ref, dkvb_ref, dqg_ref, dkg_ref, dqag_ref, dkvag_ref, qn_s, kvn_s, dqn_s, dkvn_s, dkr_s,
             qh_s, kv_s, dqh_s, dkv_s):
        i, h = pl.program_id(0), pl.program_id(1)

        @pl.when((i == 0) & (h == 0))
        def _():
            for ref in (dqb_ref, dkvb_ref, dqg_ref, dkg_ref, dqag_ref, dkvag_ref):
                ref[...] = jnp.zeros_like(ref)

        @pl.when(h == 0)
        def _():
            qhat, _, kvhat, _ = _latents(p_ref, qag_ref, kvag_ref)
            qn_s[...] = (qhat * qag_ref[...]).astype(BF16)
            kvn_s[...] = (kvhat * kvag_ref[...]).astype(BF16)
            dqn_s[...] = jnp.zeros_like(dqn_s)
            dkvn_s[...] = jnp.zeros_like(dkvn_s)
            dkr_s[...] = jnp.zeros_like(dkr_s)

        qh_s[...] = _dot(qn_s[...], qb_ref[...])
        kv_s[...] = _dot(kvn_s[...], kvb_ref[...])
        qg, kg = qg_ref[...], kg_ref[...]

        def chunk(c, gains):
            dqg, dkg = gains
            rows = slice(c * _CHAIN_ROWS, (c + 1) * _CHAIN_ROWS)
            cos, sin = cos_ref[rows, :], sin_ref[rows, :]
            qhat, rq = _head_norm(qh_s[rows, :])
            dqn_head = _rope_transposed(dq_ref[rows, :], cos, sin)
            dqh_s[rows, :] = _head_norm_bwd(dqn_head, qhat, rq, qg).astype(BF16)
            kv = kv_s[rows, :]
            khat, rk = _head_norm(jnp.concatenate([kv[:, :QK_NOPE], p_ref[rows, ODD_IN_PAD - LANES:]], axis=1))
            dkn_head = _rope_transposed(dk_ref[rows, :], cos, sin)
            dkf = _head_norm_bwd(dkn_head, khat, rk, kg)
            dkr_s[rows, :] += dkf[:, QK_NOPE:]
            dkv_s[rows, :] = jnp.concatenate([dkf[:, :QK_NOPE], dv_ref[rows, :]], axis=1).astype(BF16)
            return dqg + dqn_head * qhat, dkg + dkn_head * khat

        dqg = dkg = jnp.zeros((_CHAIN_ROWS, QK_PAD), F32)
        for c in range(tm // _CHAIN_ROWS):
            dqg, dkg = chunk(c, (dqg, dkg))
        dqg_ref[...] += jnp.sum(dqg, axis=0, keepdims=True)
        dkg_ref[...] += jnp.sum(dkg, axis=0, keepdims=True)
        dqb_ref[h] += _dot(qn_s[...], dqh_s[...], "tn")
        dqn_s[...] += _dot(dqh_s[...], qb_ref[...], "nt")
        dkvb_ref[h] += _dot(kvn_s[...], dkv_s[...], "tn")
        dkvn_s[...] += _dot(dkv_s[...], kvb_ref[...], "nt")

        @pl.when(h == HEADS - 1)
        def _():
            qhat_l, rql, kvhat_l, rkvl = _latents(p_ref, qag_ref, kvag_ref)
            dqn, dkvn = dqn_s[...], dkvn_s[...]
            dqag_ref[...] += jnp.sum(dqn * qhat_l, axis=0, keepdims=True)
            dkvag_ref[...] += jnp.sum(dkvn * kvhat_l, axis=0, keepdims=True)
            dqx, dkvx = dqn * qag_ref[...], dkvn * kvag_ref[...]
            dp_ref[:, :POOL_WIDTH] = dzp_ref[...]
            dp_ref[:, POOL_WIDTH:POOL_WIDTH + Q_LORA] = (
                rql * (dqx - qhat_l * jnp.mean(dqx * qhat_l, axis=-1, keepdims=True))).astype(BF16)
            dp_ref[:, POOL_WIDTH + Q_LORA:ODD_IN_PAD - LANES] = (
                rkvl * (dkvx - kvhat_l * jnp.mean(dkvx * kvhat_l, axis=-1, keepdims=True))).astype(BF16)
            dp_ref[:, ODD_IN_PAD - LANES:] = dkr_s[:, :QK_ROPE].astype(BF16)

    full = lambda shape: BS(shape, lambda i, h: (0,) * len(shape))
    qk_spec = BS((None, tm, QK_PAD), lambda i, h: (h, i, 0))
    return _call(body, name="mla_qkv_bwd", grid=(n_tiles, HEADS),
                 in_specs=_mla_specs(tm) + [qk_spec, qk_spec, BS((None, tm, V_DIM), lambda i, h: (h, i, 0)),
                                            BS((tm, POOL_WIDTH), lambda i, h: (i, 0))],
                 out_specs=[BS((tm, ODD_IN), lambda i, h: (i, 0)), full((HEADS, Q_LORA, QK_PAD)), full((HEADS, KV_LORA, QK_PAD)),
                            full((1, QK_PAD)), full((1, QK_PAD)), full((1, Q_LORA)), full((1, KV_LORA))],
                 out_shape=[_sds((T, ODD_IN), BF16),_sds((HEADS, Q_LORA, QK_PAD), F32), _sds((HEADS, KV_LORA, QK_PAD), F32),
                            _sds((1, QK_PAD), F32), _sds((1, QK_PAD), F32), _sds((1, Q_LORA), F32), _sds((1, KV_LORA), F32)],
                 scratch=[pltpu.VMEM((tm, Q_LORA), BF16), pltpu.VMEM((tm, KV_LORA), BF16), pltpu.VMEM((tm, Q_LORA), F32),
                          pltpu.VMEM((tm, KV_LORA), F32), pltpu.VMEM((tm, LANES), F32), pltpu.VMEM((tm, QK_PAD), F32),
                          pltpu.VMEM((tm, QK_PAD), F32), pltpu.VMEM((tm, QK_PAD), BF16), pltpu.VMEM((tm, QK_PAD), BF16)])(
        proj, cos, sin_signed, qa_g, kva_g, q_b, kv_b, q_g, k_g, dq, dk, dv, dz_pool)


def _attn_tile(seq):
    return 512 if seq % 512 == 0 else seq


def _causal_mask(s):
    row = lax.broadcasted_iota(jnp.int32, s.shape, 0)
    col = lax.broadcasted_iota(jnp.int32, s.shape, 1)
    return jnp.where(row >= col, s, NEG_INF)


def _tile(i, t):
    return slice(i * t, (i + 1) * t)


def _flash_fwd(q, k, v, mix, batch, seq):
    t = _attn_tile(seq)
    nq = seq // t

    def body(q_ref, k_ref, v_ref, _, o_ref, lse_ref):
        for qi in range(nq):
            rows, before = _tile(qi, t), slice(0, qi * t)
            qv = q_ref[rows, :]
            s_diag = _causal_mask(_dot(qv, k_ref[rows, :], "nt") * ATTN_SCALE)
            m = jnp.max(s_diag, axis=-1, keepdims=True)
            if qi:
                s_before = _dot(qv, k_ref[before, :], "nt") * ATTN_SCALE
                m = jnp.maximum(m, jnp.max(s_before, axis=-1, keepdims=True))
            p = jnp.exp(s_diag - m)
            l = jnp.sum(p, axis=-1, keepdims=True)
            acc = _dot(p, v_ref[rows, :])
            if qi:
                p = jnp.exp(s_before - m)
                l = l + jnp.sum(p, axis=-1, keepdims=True)
                acc = acc + _dot(p, v_ref[before, :])
            o_ref[rows, :] = (acc / l).astype(BF16)
            lse_ref[rows, :] = jnp.broadcast_to(m + jnp.log(l), (t, LANES))

    T = batch * seq
    whole = lambda w: BS((None, seq, w), lambda b, h: (h, b, 0))
    return _call(body, name="flash_fwd", grid=(batch, HEADS),
                 in_specs=[whole(QK_PAD), whole(QK_PAD), whole(V_DIM), pl.BlockSpec(memory_space=pl.ANY)],
                 out_specs=[BS((seq, V_DIM), lambda b, h: (b, POOL_WIDTH // V_DIM + h)), whole(LANES)],
                 out_shape=[_sds((T, D_MODEL), BF16), _sds((HEADS, T, LANES), F32)],
                 aliases={3: 0})(q, k, v, mix)


def _flash_bwd(q, k, v, dmix, mix, lse, batch, seq):
    t = _attn_tile(seq)
    nq = seq // t

    def body(q_ref, k_ref, v_ref, do_ref, o_ref, lse_ref, dq_ref, dk_ref, dv_ref):
        for qi in range(nq):
            rows, before = _tile(qi, t), slice(0, qi * t)
            qv, do = q_ref[rows, :], do_ref[rows, :]
            lse = lse_ref[rows, 0:1]
            delta = jnp.sum(do.astype(F32) * o_ref[rows, :].astype(F32), axis=-1, keepdims=True)

            def block(keys, masked):
                kk = k_ref[keys, :]
                s = _dot(qv, kk, "nt") * ATTN_SCALE
                p = jnp.exp((_causal_mask(s) if masked else s) - lse)
                ds = p * (_dot(do, v_ref[keys, :], "nt") - delta) * ATTN_SCALE
                return _dot(p, do, "tn"), _dot(ds, qv, "tn"), _dot(ds, kk)

            dv_ref[rows, :], dk_ref[rows, :], dq = block(rows, True)
            if qi:
                dv, dk, dq_before = block(before, False)
                dv_ref[before, :] += dv
                dk_ref[before, :] += dk
                dq = dq + dq_before
            dq_ref[rows, :] = dq

    T = batch * seq
    whole = lambda w: BS((None, seq, w), lambda b, h: (h, b, 0))
    head_cols = BS((seq, V_DIM), lambda b, h: (b, POOL_WIDTH // V_DIM + h))
    return _call(body, name="flash_bwd", grid=(batch, HEADS),
                 in_specs=[whole(QK_PAD), whole(QK_PAD), whole(V_DIM), head_cols, head_cols, whole(LANES)],
                 out_specs=[whole(QK_PAD), whole(QK_PAD), whole(V_DIM)],
                 out_shape=[_sds((HEADS, T, QK_PAD), F32), _sds((HEADS, T, QK_PAD), F32), _sds((HEADS, T, V_DIM), F32)])(
        q, k, v, dmix, mix, lse)


def _adamw_math(w, g, m, v):
    m = ADAM_B1 * m + (1.0 - ADAM_B1) * g
    v = ADAM_B2 * v + (1.0 - ADAM_B2) * (g * g)
    m_hat = m / (1.0 - ADAM_B1 ** ADAM_STEP)
    v_hat = v / (1.0 - ADAM_B2 ** ADAM_STEP)
    return -ADAM_LR * (m_hat / (jnp.sqrt(v_hat) + ADAM_EPS) + ADAM_WD * w), m, v


def _adamw(name, w, g, m, v, l=0, prev=()):
    L, R, C = w.shape
    tr = 256 if R % 256 == 0 else R

    def body(w_ref, g_ref, m_ref, v_ref, *rest):
        go_ref, d_ref, mo_ref, vo_ref, token = rest[-5:]
        gv = g_ref[...]
        d_ref[...], mo_ref[...], vo_ref[...] = _adamw_math(w_ref[...], gv, m_ref[...], v_ref[...])
        go_ref[...] = gv
        token[...] = jnp.zeros_like(token)

    layer = BS((None, tr, C), lambda i: (l, i, 0))
    return _call(body, name=f"{name}_{l}", grid=(R // tr,),
                 in_specs=[layer, BS((tr, C), lambda i: (i, 0)), layer, layer] + [pl.BlockSpec(memory_space=pl.ANY)] * len(prev),
                 out_specs=[layer] * 4 + [BS((SUBLANES, LANES), lambda i: (0, 0))],
                 out_shape=[_sds((L, R, C), F32)] * 4 + [_sds((SUBLANES, LANES), F32)],
                 aliases={4 + n: n for n in range(len(prev))})(w, g, m, v, *prev)


def _place():
    x, y, c = lax.axis_index("x"), lax.axis_index("y"), lax.axis_index("c")
    other_chips = [(1 - x, y), (x, 1 - y), (1 - x, 1 - y)]
    return x, y, c, other_chips


def _remote(src, dst, send_sem, recv_sem, dev):
    return pltpu.make_async_remote_copy(src_ref=src, dst_ref=dst, send_sem=send_sem, recv_sem=recv_sem,
                                        device_id=dev, device_id_type=MESH)


def _prefetch_call(body, *, name, grid, in_specs, out_specs, out_shape):
    grid_spec = pltpu.PrefetchScalarGridSpec(num_scalar_prefetch=1, grid=grid, in_specs=in_specs, out_specs=out_specs)
    params = pltpu.CompilerParams(vmem_limit_bytes=VMEM_LIMIT_V7X, dimension_semantics=("arbitrary",) * len(grid))
    return pl.pallas_call(body, name=name, grid_spec=grid_spec, out_shape=out_shape, compiler_params=params)


def _row_tile(rows):
    return 256 if rows % 256 == 0 else rows


def _cast_place(name, w, layer, chip, after=()):
    _, _, rows, C = w.shape
    tr = _row_tile(rows)

    def body(chip_ref, w_ref, *rest):
        rest[-1][...] = w_ref[...].astype(BF16)

    return _prefetch_call(body, name=name, grid=(2, rows // tr),
                          in_specs=[BS((None, None, tr, C), lambda h, i, chip_ref: (layer, h, i, 0))]
                          + [pl.BlockSpec(memory_space=pl.ANY)] * len(after),
                          out_specs=BS((None, None, tr, C), lambda h, i, chip_ref: (chip_ref[0], h, i, 0)),
                          out_shape=pltpu.HBM((N_CHIPS, 2, rows, C), BF16))(chip, w, *after)


SEM = pl.BlockSpec(memory_space=pltpu.SEMAPHORE)


def _split_copy_call(body, *, name, in_specs, out_specs, out_shape, aliases):
    return pl.pallas_call(body, name=name, in_specs=in_specs, out_specs=out_specs, out_shape=out_shape,
                          input_output_aliases=aliases,
                          compiler_params=pltpu.CompilerParams(has_side_effects=pltpu.SideEffectType.DATAFLOW_SIDE_EFFECTING))


def _hbm(arrays):
    return [pltpu.with_memory_space_constraint(a, pltpu.HBM) for a in arrays]


def _gather_send(name, gs, groups, after):
    n = len(gs)

    def body(*refs):
        g, sems, token = refs[:n], refs[n + len(after):n + len(after) + 2 * len(groups)], refs[-1]
        x, y, c, chips = _place()
        me = 2 * x + y
        for gi, members in enumerate(groups):
            for a, i in enumerate(members):
                for k, (px, py) in enumerate(chips):
                    _remote(g[i].at[me, c], g[i].at[me, c], sems[2 * gi].at[3 * a + k], sems[2 * gi + 1].at[3 * a + k],
                            (px, py, c)).start()
        token[...] = jnp.zeros_like(token)

    sem_shapes = [pltpu.SemaphoreType.DMA((3 * len(members),)) for members in groups for _ in range(2)]
    out = _split_copy_call(body, name=name, in_specs=[HBM] * n + [pl.BlockSpec(memory_space=pl.ANY)] * len(after),
                           out_specs=[SEM] * len(sem_shapes) + [HBM] * n + [VMEM],
                           out_shape=sem_shapes + [pltpu.HBM(a.shape, a.dtype) for a in gs] + [_sds((SUBLANES, LANES), F32)],
                           aliases={i: len(sem_shapes) + i for i in range(n)})(*_hbm(gs), *after)
    return out[:len(sem_shapes)], out[len(sem_shapes):-1], out[-1]


def _gather_wait(name, gs, send_sems, recv_sems, after):
    n = len(gs)

    def body(*refs):
        g, ssem, rsem = refs[:n], refs[n], refs[n + 1]
        x, y, c, chips = _place()
        me = 2 * x + y
        for a in range(n):
            for k, (px, py) in enumerate(chips):
                landed = g[a].at[2 * px + py, c]
                cp = _remote(g[a].at[me, c], landed, ssem.at[3 * a + k], rsem.at[3 * a + k], (px, py, c))
                cp.wait_recv()
                cp.wait_send()

    return _split_copy_call(body, name=name, in_specs=[HBM] * n + [SEM, SEM] + [pl.BlockSpec(memory_space=pl.ANY)] * len(after),
                            out_specs=[HBM] * n, out_shape=[pltpu.HBM(a.shape, a.dtype) for a in gs],
                            aliases={i: i for i in range(n)})(*gs, send_sems, recv_sems, *after)


def _gather_pass(name, gs):
    n = len(gs)

    def body(*refs):
        g, send_sems, recv_sems = refs[n:2 * n], refs[-2], refs[-1]
        x, y, c, chips = _place()
        sibling = (x, y, 1 - c)
        passed = [_remote(g[i].at[2 * px + py, c], g[i].at[2 * px + py, c], send_sems.at[3 * i + k], recv_sems.at[3 * i + k], sibling)
                  for i in range(n) for k, (px, py) in enumerate(chips)]
        for cp in passed:
            cp.start()
        for i in range(n):
            for k, (px, py) in enumerate(chips):
                theirs = g[i].at[2 * px + py, 1 - c]
                _remote(theirs, theirs, send_sems.at[3 * i + k], recv_sems.at[3 * i + k], sibling).wait_recv()
        for cp in passed:
            cp.wait_send()

    return _call(body, name=name, in_specs=[HBM] * n, out_specs=[HBM] * n, out_shape=[_sds(a.shape, a.dtype) for a in gs],
                 aliases={i: i for i in range(n)},
                 scratch=[pltpu.SemaphoreType.DMA((3 * n,)), pltpu.SemaphoreType.DMA((3 * n,))])(*gs)


def _scatter_send(name, ps):
    n = len(ps)

    def body(*refs):
        p, r, ssem, rsem, token = refs[:n], refs[n:2 * n], refs[2 * n], refs[2 * n + 1], refs[-1]
        x, y, c, chips = _place()
        for i in range(n):
            for k, (px, py) in enumerate(chips):
                _remote(p[i].at[2 * px + py], r[i].at[k], ssem.at[3 * i + k], rsem.at[3 * i + k], (px, py, c)).start()
        token[...] = jnp.zeros_like(token)

    lands = [lax.empty((N_CHIPS - 1,) + a.shape[1:], a.dtype) for a in ps]
    sem = pltpu.SemaphoreType.DMA((3 * n,))
    out = _split_copy_call(body, name=name, in_specs=[HBM] * (2 * n), out_specs=[SEM, SEM] + [HBM] * (2 * n) + [VMEM],
                           out_shape=[sem, sem] + [pltpu.HBM(a.shape, a.dtype) for a in list(ps) + lands] + [_sds((SUBLANES, LANES), F32)],
                           aliases={i: 2 + i for i in range(2 * n)})(*_hbm(list(ps) + lands))
    return out[0], out[1], out[2:2 + n], out[2 + n:2 + 2 * n], out[-1]


def _scatter_wait(name, ps, lands, send_sems, recv_sems, after):
    n = len(ps)

    def body(*refs):
        p, r, ssem, rsem = refs[:n], refs[n:2 * n], refs[2 * n], refs[2 * n + 1]
        x, y, c, chips = _place()
        for i in range(n):
            for k, (px, py) in enumerate(chips):
                cp = _remote(p[i].at[2 * px + py], r[i].at[k], ssem.at[3 * i + k], rsem.at[3 * i + k], (px, py, c))
                cp.wait_recv()
                cp.wait_send()

    out = _split_copy_call(body, name=name, in_specs=[HBM] * (2 * n) + [SEM, SEM] + [pl.BlockSpec(memory_space=pl.ANY)] * len(after),
                           out_specs=[HBM] * (2 * n), out_shape=[pltpu.HBM(a.shape, a.dtype) for a in list(ps) + list(lands)],
                           aliases={i: i for i in range(2 * n)})(*ps, *lands, send_sems, recv_sems, *after)
    return out[:n], out[n:]


def _exchange_send(name, gs):
    n = len(gs)

    def body(*refs):
        g, r, ssem, rsem, token = refs[:n], refs[n:2 * n], refs[2 * n], refs[2 * n + 1], refs[-1]
        x, y, c, _ = _place()
        for i in range(n):
            _remote(g[i].at[:, 1 - c], r[i], ssem.at[i], rsem.at[i], (x, y, 1 - c)).start()
        token[...] = jnp.zeros_like(token)

    lands = [lax.empty((a.shape[0],) + a.shape[2:], a.dtype) for a in gs]
    sem = pltpu.SemaphoreType.DMA((n,))
    out = _split_copy_call(body, name=name, in_specs=[HBM] * (2 * n), out_specs=[SEM, SEM] + [HBM] * (2 * n) + [VMEM],
                           out_shape=[sem, sem] + [pltpu.HBM(a.shape, a.dtype) for a in list(gs) + lands] + [_sds((SUBLANES, LANES), F32)],
                           aliases={i: 2 + i for i in range(2 * n)})(*_hbm(list(gs) + lands))
    return out[0], out[1], out[2:2 + n], out[2 + n:2 + 2 * n], out[-1]


def _exchange_wait(name, gs, lands, send_sems, recv_sems, after):
    n = len(gs)

    def body(*refs):
        g, r, ssem, rsem = refs[:n], refs[n:2 * n], refs[2 * n], refs[2 * n + 1]
        x, y, c, _ = _place()
        for i in range(n):
            cp = _remote(g[i].at[:, 1 - c], r[i], ssem.at[i], rsem.at[i], (x, y, 1 - c))
            cp.wait_recv()
            cp.wait_send()

    out = _split_copy_call(body, name=name, in_specs=[HBM] * (2 * n) + [SEM, SEM] + [pl.BlockSpec(memory_space=pl.ANY)] * len(after),
                           out_specs=[HBM] * (2 * n), out_shape=[pltpu.HBM(a.shape, a.dtype) for a in list(gs) + list(lands)],
                           aliases={i: i for i in range(2 * n)})(*gs, *lands, send_sems, recv_sems, *after)
    return out[:n], out[n:]


def _sibling_share(name, fs):
    n = len(fs)

    def body(*refs):
        f, send_sems, recv_sems = refs[n:2 * n], refs[-2], refs[-1]
        x, y, c, _ = _place()
        sends = [_remote(f[i].at[c], f[i].at[c], send_sems.at[i], recv_sems.at[i], (x, y, 1 - c)) for i in range(n)]
        for cp in sends:
            cp.start()
        for i in range(n):
            theirs = f[i].at[1 - c]
            _remote(theirs, theirs, send_sems.at[i], recv_sems.at[i], (x, y, 1 - c)).wait_recv()
        for cp in sends:
            cp.wait_send()

    return _call(body, name=name, in_specs=[HBM] * n, out_specs=[HBM] * n,
                 out_shape=[_sds(a.shape, a.dtype) for a in fs], aliases={i: i for i in range(n)},
                 scratch=[pltpu.SemaphoreType.DMA((n,)), pltpu.SemaphoreType.DMA((n,))])(*fs)


def _all_reduce_small(name, v):
    rows = v.shape[0] // 2
    halves = (2, rows, LANES)

    def body(v_ref, o_ref, from_sibling, chip_sums, send_sems, recv_sems):
        x, y, c, chips = _place()
        me, sibling = 2 * x + y, (x, y, 1 - c)
        swap = _remote(v_ref.at[1 - c], from_sibling, send_sems.at[0], recv_sems.at[0], sibling)
        swap.start()
        swap.wait()
        chip_sums[me] = v_ref[c] + from_sibling[...]
        sends = [_remote(chip_sums.at[me], chip_sums.at[me], send_sems.at[1 + k], recv_sems.at[1 + k], (px, py, c))
                 for k, (px, py) in enumerate(chips)]
        for cp in sends:
            cp.start()
        for k, (px, py) in enumerate(chips):
            theirs = chip_sums.at[2 * px + py]
            _remote(theirs, theirs, send_sems.at[1 + k], recv_sems.at[1 + k], (px, py, c)).wait_recv()
        for cp in sends:
            cp.wait_send()
        acc = chip_sums[0]
        for j in range(1, N_CHIPS):
            acc = acc + chip_sums[j]
        o_ref[c] = acc
        share = _remote(o_ref.at[c], o_ref.at[c], send_sems.at[4], recv_sems.at[4], sibling)
        share.start()
        share.wait_send()
        _remote(o_ref.at[1 - c], o_ref.at[1 - c], send_sems.at[4], recv_sems.at[4], sibling).wait_recv()

    return _call(body, name=name, in_specs=[VMEM], out_specs=VMEM, out_shape=_sds(halves, F32),
                 scratch=[pltpu.VMEM((rows, LANES), F32), pltpu.VMEM((N_CHIPS, rows, LANES), F32),
                          pltpu.SemaphoreType.DMA((5,)), pltpu.SemaphoreType.DMA((5,))])(v.reshape(halves)).reshape(v.shape)


def _add_halves(name, g, r, c):
    _, _, rows, C = g.shape
    tr = _row_tile(rows)

    def body(c_ref, g_ref, r_ref, o_ref):
        o_ref[...] = (g_ref[...] + r_ref[...]).astype(BF16)

    spec = BS((None, tr, C), lambda j, i, c_ref: (j, i, 0))
    return _prefetch_call(body, name=name, grid=(N_CHIPS, rows // tr),
                          in_specs=[BS((None, None, tr, C), lambda j, i, c_ref: (j, c_ref[0], i, 0)), spec], out_specs=spec,
                          out_shape=pltpu.HBM((N_CHIPS, rows, C), BF16))(c, g, r)


def _sum_partials(name, p, r, chip_c):
    _, rows, C = p.shape
    tr = _row_tile(rows)

    def body(s_ref, p_ref, r_ref, o_ref):
        acc = p_ref[...].astype(F32)
        for k in range(N_CHIPS - 1):
            acc = acc + r_ref[k].astype(F32)
        o_ref[...] = acc

    return _prefetch_call(body, name=name, grid=(rows // tr,),
                          in_specs=[BS((None, tr, C), lambda i, s: (s[0], i, 0)), BS((N_CHIPS - 1, tr, C), lambda i, s: (0, i, 0))],
                          out_specs=BS((None, tr, C), lambda i, s: (s[1], i, 0)), out_shape=pltpu.HBM((2, rows, C), F32))(chip_c, p, r)


_SHARDED = ("even_w_in", "even_w_out", "odd_w_in", "q_b", "kv_b", "odd_w_out", "ffn_w_gate", "ffn_w_up", "ffn_w_down")
_REPLICATED = ("mix_norm", "ffn_norm", "sg_ln_g", "sg_w_s", "sg_b_s", "pool_w", "q_norm", "k_norm")
_SMALL_SHARDED = ("sc_conv_w", "pool_scale", "q_a_norm", "kv_a_norm")
_WEIGHTS = ("mix_norm", "ffn_norm", "even_w_in", "sg_ln_g", "sg_w_s", "sg_b_s", "sc_conv_w", "even_w_out", "odd_w_in", "pool_w",
            "pool_scale", "q_a_norm", "q_b", "kv_a_norm", "kv_b", "q_norm", "k_norm", "odd_w_out", "ffn_w_gate", "ffn_w_up",
            "ffn_w_down")


def _pad_rows(flat, width, align):
    n = flat.shape[0]
    rows = -(-n // (width * align)) * align
    return jnp.pad(flat, (0, rows * width - n)).reshape(rows, width)


_GROUPS = {"even": ("even_w_in", "even_w_out"),
           "ffn0": ("ffn_w_gate0", "ffn_w_up0", "ffn_w_down0"),
           "odd": ("odd_w_in", "q_b", "kv_b", "odd_w_out"),
           "ffn1": ("ffn_w_gate1", "ffn_w_up1", "ffn_w_down1")}


def _place_shards(shards, names, chip, after):
    placed = []
    for n in names:
        weight, layer = (n[:-1], int(n[-1])) if n[-1].isdigit() else (n, 0)
        a = shards[weight]
        placed.append(_cast_place(f"place_{n}", a.reshape(a.shape[0], 2, a.shape[1] // 2, a.shape[2]), layer, chip, after))
    return placed


def _whole_weights(gathered):
    out = {n: a.reshape(N_CHIPS, -1, a.shape[-1]) for n, a in gathered.items()}
    for n in ("q_b", "kv_b"):
        if n in out:
            out[n] = out[n].transpose(1, 0, 2).reshape(out[n].shape[1], -1)
    for n in ("even_w_out", "odd_w_in", "odd_w_out"):
        if n in out:
            out[n] = out[n].reshape(-1, out[n].shape[-1])
    return out


def _forward_backward(x, positions, target, small, fetch, emit, advance):
    batch, seq, _ = x.shape
    T = batch * seq
    tm = _token_tile(seq)
    x0 = x.reshape(T, D_MODEL)

    inv_freq = ROPE_THETA ** (-jnp.arange(0, QK_ROPE, 2, dtype=F32) / QK_ROPE)
    ang = (positions.astype(F32)[..., None] * inv_freq).reshape(T, QK_ROPE // 2)
    cos, sin = jnp.cos(ang), jnp.sin(ang)
    pad = jnp.zeros((T, LANES - QK_ROPE), F32)
    cos_t = jnp.concatenate([cos, cos, pad], axis=1)
    sin_t = jnp.concatenate([-sin, sin, pad], axis=1)

    tril = jnp.tril(jnp.ones((SG_CHUNK, SG_CHUNK), bool))
    w_tril = jnp.where(tril[None], small["sg_w_s"][0], 0.0).astype(BF16)
    b_lanes = jnp.broadcast_to(small["sg_b_s"][0][:, :, None], (SG_HEADS, SG_CHUNK, SG_DIM))
    conv_w = jnp.pad(small["sc_conv_w"][0], ((0, SUBLANES - CONV_TAPS), (0, 0)))
    ln_g = small["sg_ln_g"]
    pool_diag = jnp.zeros((POOL_WIDTH, POOL_WIDTH), F32)
    for g in range(len(POOL_WINDOWS)):
        pool_diag = pool_diag.at[POOL_DIM * g:POOL_DIM * (g + 1), POOL_DIM * g:POOL_DIM * (g + 1)].set(small["pool_w"][0, g])
    pool_diag = pool_diag.astype(BF16)
    pool_scale = small["pool_scale"]
    q_g = jnp.pad(small["q_norm"], ((0, 0), (0, QK_PAD - QK_DIM)))
    k_g = jnp.pad(small["k_norm"], ((0, 0), (0, QK_PAD - QK_DIM)))
    qa_g, kva_g = small["q_a_norm"], small["kv_a_norm"]
    in_shard = EVEN_IN // N_CHIPS

    def ffn_weights(l, w):
        return w[f"ffn_w_gate{l}"], w[f"ffn_w_up{l}"], w[f"ffn_w_down{l}"]

    W = fetch("even", ())
    w_in_even = W["even_w_in"]
    h0 = _rmsnorm_fwd("mix0_norm", x0, small["mix_norm"][0], tm)
    tb = _big_tile(T)
    proj0 = _matmul("even_in", "nn", [(h0, w_in_even)],
                    [(BS((tb, D_MODEL), lambda i, j, k: (i, 0)), BS((None, D_MODEL, in_shard), lambda i, j, k: (j, 0, 0)))],
                    (T // tb, N_CHIPS, 1), _sds((T, EVEN_IN), F32), BS((tb, in_shard), lambda i, j, k: (i, j)), (tb, in_shard))
    mix0 = _even_mixer_fwd(proj0, ln_g, w_tril, b_lanes, conv_w, seq, tm)
    w_out_even = W["even_w_out"]
    x1, h1 = _mm("even_out", "nn", mix0, w_out_even, F32, tk=1024, add=x0, fused=_norm_tail(small["ffn_norm"][0], T, tb))
    ffn0 = ffn_weights(0, fetch("ffn0", (x1,)))
    (x2, h2), ffn0_saved = _ffn_fwd(0, x1, h1, *ffn0, _norm_tail(small["mix_norm"][1], T, tb))
    W = fetch("odd", (x2,))
    w_in_odd = jnp.pad(W["odd_w_in"], ((0, 0), (0, ODD_IN_PAD - ODD_IN)))
    q_b = jnp.pad(W["q_b"].reshape(Q_LORA, HEADS, QK_DIM).transpose(1, 0, 2), ((0, 0), (0, 0), (0, QK_PAD - QK_DIM)))
    kv_b = W["kv_b"].reshape(KV_LORA, HEADS, QK_NOPE + V_DIM).transpose(1, 0, 2)
    proj1 = _mm("odd_in", "nn", h2, w_in_odd, F32, tk=1024)
    mix1 = _pool_fwd(proj1, pool_diag, pool_scale, seq, tm)
    q, k, v = _mla_qkv_fwd(proj1, cos_t, sin_t, qa_g, kva_g, q_b, kv_b, q_g, k_g, tm)
    mix1, lse = _flash_fwd(q, k, v, mix1, batch, seq)
    x3, h3 = _mm("odd_out", "nn", mix1, W["odd_w_out"], F32, tk=1024, add=x2, fused=_norm_tail(small["ffn_norm"][1], T, tb))
    ffn1 = ffn_weights(1, fetch("ffn1", (x3,)))
    (dy, sq), ffn1_saved = _ffn_fwd(1, x3, h3, *ffn1, _loss_tail(target.reshape(T, D_MODEL), tb))

    G = {}
    dx3, dffn_g1 = _ffn_bwd(1, x3, small["ffn_norm"][1], *ffn1, ffn1_saved, dy, emit)
    dmix1 = _mm("odd_out_dx", "nt", dx3, W["odd_w_out"], BF16, tk=1024, after=advance(dx3))
    dw_out_odd = _mm("odd_out_dw", "tn", mix1, dx3, F32, hbm_out=True)
    dq, dk, dv = _flash_bwd(q, k, v, dmix1, mix1, lse, batch, seq)
    dz_pool, dpool_diag, G["pool_scale"] = _pool_bwd(proj1, dmix1, pool_diag, pool_scale, seq, tm)
    dproj1, dq_b, dkv_b, dq_g, dk_g, G["q_a_norm"], G["kv_a_norm"] = _mla_qkv_bwd(
        proj1, cos_t, sin_t, qa_g, kva_g, q_b, kv_b, q_g, k_g, dq, dk, dv, dz_pool, tm)
    G["pool_w"] = jnp.stack([dpool_diag[POOL_DIM * g:POOL_DIM * (g + 1), POOL_DIM * g:POOL_DIM * (g + 1)]
                             for g in range(len(POOL_WINDOWS))])[None]
    G["q_norm"], G["k_norm"] = dq_g[:, :QK_DIM], dk_g[:, :QK_DIM]
    dw_in_odd = _mm("odd_in_dw", "tn", h2, dproj1, F32, tn=ODD_IN, hbm_out=True)

    def shard_major(g, cols):
        return g.reshape(g.shape[0], N_CHIPS, cols).transpose(1, 0, 2)

    behind = emit("odd", {"odd_w_in": dw_in_odd.reshape(N_CHIPS, -1, ODD_IN),
                          "q_b": shard_major(dq_b[:, :, :QK_DIM].transpose(1, 0, 2).reshape(Q_LORA, HEADS * QK_DIM), HEADS * QK_DIM // N_CHIPS),
                          "kv_b": shard_major(dkv_b.transpose(1, 0, 2).reshape(KV_LORA, HEADS * (QK_NOPE + V_DIM)),
                                              HEADS * (QK_NOPE + V_DIM) // N_CHIPS),
                          "odd_w_out": dw_out_odd.reshape(N_CHIPS, -1, D_MODEL)})
    dx2, dmix_g1 = _mm("odd_in_dx", "nt", dproj1, W["odd_w_in"], F32, tk=ODD_IN, after=behind,
                       fused=_norm_bwd_tail(x2, small["mix_norm"][1], dx3, tb))
    dx1, dffn_g0 = _ffn_bwd(0, x1, small["ffn_norm"][0], *ffn0, ffn0_saved, dx2, emit, after=advance(dx2))
    dmix0 = _mm("even_out_dx", "nt", dx1, w_out_even, F32, tk=1024, after=advance(dx1))
    dw_out_even = _mm("even_out_dw", "tn", mix0, dx1, F32, hbm_out=True)
    dproj0, dw_s, db_lanes, G["sg_ln_g"], dconv = _even_mixer_bwd(proj0, dmix0, ln_g, w_tril, b_lanes, conv_w, seq, tm)
    G["sg_w_s"] = dw_s[None]
    G["sg_b_s"] = jnp.sum(db_lanes, axis=-1)[None]
    G["sc_conv_w"] = dconv[None, :CONV_TAPS]
    tail, shapes, specs = _norm_bwd_tail(x0, small["mix_norm"][0], dx1, tb)
    dx0, dmix_g0 = _matmul("even_in_dx", "nt", [(dproj0, w_in_even)],
                           [(BS((tb, in_shard), lambda i, j, k: (i, k)), BS((None, D_MODEL, in_shard), lambda i, j, k: (k, 0, 0)))],
                           (T // tb, 1, N_CHIPS), shapes, specs, (tb, D_MODEL), tail=tail)
    tk = min(512, T)
    dw_in_even = _grad_shards(
        "even_in_dw", h0, dproj0, BS((tk, D_MODEL), lambda k: (k, 0)), BS((tk, EVEN_IN), lambda k: (k, 0)),
        lambda a_ref, b_ref, j: (a_ref[...], b_ref[:, in_shard * j:in_shard * (j + 1)]), (N_CHIPS, D_MODEL, in_shard), T // tk)
    emit("even", {"even_w_in": dw_in_even, "even_w_out": dw_out_even.reshape(N_CHIPS, -1, D_MODEL)})
    G["mix_norm"] = jnp.concatenate([dmix_g0, dmix_g1], axis=0)
    G["ffn_norm"] = jnp.concatenate([dffn_g0, dffn_g1], axis=0)
    return sq[0, 0], dx0.reshape(batch, seq, D_MODEL), G


def _small_vector(parts, names):
    flat = jnp.concatenate([parts[n].astype(F32).reshape(-1) for n in names])
    return _pad_rows(flat, LANES, 2 * SUBLANES)


def _split_small(vec, like, names):
    out, off, flat = {}, 0, vec.reshape(-1)
    for n in names:
        size = math.prod(like[n].shape)
        out[n] = flat[off:off + size].reshape(like[n].shape)
        off += size
    return out


def _whole_shape(a):
    return a.shape[:-1] + (a.shape[-1] * N_CHIPS,)


def kernel(x, positions, mix_norm, ffn_norm, even_w_in, sg_ln_g, sg_w_s, sg_b_s, sc_conv_w, even_w_out, odd_w_in, pool_w, pool_scale, q_a_norm, q_b, kv_a_norm, kv_b, q_norm, k_norm, odd_w_out, ffn_w_gate, ffn_w_up, ffn_w_down, loss_target, m_mix_norm, m_ffn_norm, m_even_w_in, m_sg_ln_g, m_sg_w_s, m_sg_b_s, m_sc_conv_w, m_even_w_out, m_odd_w_in, m_pool_w, m_pool_scale, m_q_a_norm, m_q_b, m_kv_a_norm, m_kv_b, m_q_norm, m_k_norm, m_odd_w_out, m_ffn_w_gate, m_ffn_w_up, m_ffn_w_down, v_mix_norm, v_ffn_norm, v_even_w_in, v_sg_ln_g, v_sg_w_s, v_sg_b_s, v_sc_conv_w, v_even_w_out, v_odd_w_in, v_pool_w, v_pool_scale, v_q_a_norm, v_q_b, v_kv_a_norm, v_kv_b, v_q_norm, v_k_norm, v_odd_w_out, v_ffn_w_gate, v_ffn_w_up, v_ffn_w_down):
    args = dict(locals())
    w = {n: args[n] for n in _WEIGHTS}
    m = {n: args["m_" + n] for n in _WEIGHTS}
    v = {n: args["v_" + n] for n in _WEIGHTS}
    cx, cy, cc = lax.axis_index("x"), lax.axis_index("y"), lax.axis_index("c")
    chip = 2 * cx + cy
    transposed = ("ffn_w_gate", "ffn_w_up")
    for n in transposed:
        w[n], m[n], v[n] = (jnp.swapaxes(t[n], 1, 2) for t in (w, m, v))

    chip_arr = chip.astype(jnp.int32).reshape(1)
    c_arr = cc.astype(jnp.int32).reshape(1)
    group_names = list(_GROUPS)
    placed = {}
    for n in _SMALL_SHARDED:
        a = w[n]
        whole = jnp.zeros(a.shape[:-1] + (N_CHIPS, a.shape[-1]), F32)
        whole = lax.dynamic_update_slice_in_dim(whole, a[..., None, :], chip, axis=a.ndim - 1)
        placed[n] = jnp.where(cc == 0, whole, 0.0).reshape(_whole_shape(a))
    small_whole = _all_reduce_small("gather_small_weights", _small_vector(placed, _SMALL_SHARDED))
    small = dict({n: w[n] for n in _REPLICATED}, **_split_small(small_whole, placed, _SMALL_SHARDED))

    first, rest = list(_GROUPS[group_names[0]]), [n for g in group_names[1:] for n in _GROUPS[g]]
    sems_first, flight_first, token = _gather_send("gather_send_first", _place_shards(w, first, chip_arr, (small_whole,)),
                                                   [list(range(len(first)))], (small_whole,))
    sems_rest, flight_rest, all_sent = _gather_send("gather_send_rest", _place_shards(w, rest, chip_arr, (token,)),
                                                    [[rest.index(n) for n in _GROUPS[g]] for g in group_names[1:]], ())
    sems = list(sems_first) + list(sems_rest)
    in_flight = dict(zip(first + rest, list(flight_first) + list(flight_rest)))

    def fetch(group, after):
        gi, members = group_names.index(group), _GROUPS[group]
        after = after if gi else (all_sent,)
        landed = _gather_wait(f"gather_wait_{group}", [in_flight[n] for n in members], sems[2 * gi], sems[2 * gi + 1], after)
        return _whole_weights(dict(zip(members, _gather_pass(f"gather_pass_{group}", landed))))

    swapping, pending, arrived = [], [], {}

    def settle(after):
        names, ps, lands, send_sems, recv_sems = pending.pop()
        ps, lands = _scatter_wait(f"scatter_wait_{names[0]}", ps, lands, send_sems, recv_sems, after)
        arrived.update({n: (p, r) for n, p, r in zip(names, ps, lands)})

    def emit(group, grads):
        names = _GROUPS[group]
        halves = [grads[n].reshape(N_CHIPS, 2, grads[n].shape[1] // 2, grads[n].shape[2]) for n in names]
        send_sems, recv_sems, halves, lands, token = _exchange_send(f"exchange_send_{group}", halves)
        swapping.append((group, halves, lands, send_sems, recv_sems))
        return (token,)

    def advance(done):
        done = done if isinstance(done, tuple) else (done,)
        group, halves, lands, send_sems, recv_sems = swapping.pop()
        names = _GROUPS[group]
        halves, lands = _exchange_wait(f"exchange_wait_{group}", halves, lands, send_sems, recv_sems, done)
        partial = [_add_halves(f"add_{n}", g, r, c_arr) for n, g, r in zip(names, halves, lands)]
        if pending:
            settle(done)
        send_sems, recv_sems, ps, lands, token = _scatter_send(f"scatter_send_{group}", partial)
        pending.append((names, ps, lands, send_sems, recv_sems))
        return (token,)

    sq, grad_x, G = _forward_backward(x, positions, loss_target, small, fetch, emit, advance)
    loss = lax.psum(0.5 * sq / D_MODEL, ("x", "y", "c"))

    small_names = _REPLICATED + _SMALL_SHARDED
    summed = _split_small(_all_reduce_small("reduce_small_grads", _small_vector(G, small_names)), G, small_names)
    grads = {n: summed[n] for n in _REPLICATED}
    for n in _SMALL_SHARDED:
        a = w[n]
        grads[n] = lax.dynamic_slice_in_dim(summed[n].reshape(a.shape[:-1] + (N_CHIPS, a.shape[-1])), chip, 1,
                                            axis=a.ndim - 1).reshape(a.shape)

    chip_c = jnp.stack([chip, cc]).astype(jnp.int32)
    out = {}

    def finish(group):
        names, tokens = _GROUPS[group], []
        sums = [_sum_partials(f"sum_{n}", *arrived[n], chip_c) for n in names]
        for n, f in zip(names, _sibling_share(f"grad_share_{group}", sums)):
            weight, layer = (n[:-1], int(n[-1])) if n[-1].isdigit() else (n, 0)
            *out[weight], token = _adamw(f"adamw_{weight}", w[weight], f.reshape(-1, f.shape[-1]), m[weight], v[weight], layer,
                                         out.get(weight, ()))
            tokens.append(token)
        return tuple(tokens)

    advance(finish(group_names[3]) + finish(group_names[2]))
    settle(finish(group_names[1]))
    finish(group_names[0])
    packed = [_small_vector(d, small_names) for d in (w, grads, m, v)]
    res = _adamw("adamw_small", packed[0][None], packed[1], packed[2][None], packed[3][None])
    delta_s, m_s, v_s = (_split_small(r, w, small_names) for r in res[1:4])
    for n in small_names:
        out[n] = (grads[n], delta_s[n], m_s[n], v_s[n])
    for n in transposed:
        out[n] = tuple(jnp.swapaxes(t, 1, 2) for t in out[n])

    return (loss, grad_x, *[out[n][0] for n in _WEIGHTS], *[out[n][1] for n in _WEIGHTS],
            *[out[n][2] for n in _WEIGHTS], *[out[n][3] for n in _WEIGHTS])
```

```python
import functools
import math

import jax
import jax.numpy as jnp
from jax import lax
from jax.experimental import pallas as pl
from jax.experimental.pallas import tpu as pltpu

F32, BF16 = jnp.float32, jnp.bfloat16
BS = pl.BlockSpec

D_MODEL = 1024
EPS = 1e-6
NEG_INF = -1e30
SG_HEADS, SG_DIM, SG_WIDTH, SG_CHUNK = 4, 128, 512, 128
SC_WIDTH, CONV_TAPS = 512, 3
EVEN_IN = 2 * SG_WIDTH + 3 * SC_WIDTH
POOL_WINDOWS = (2, 4, 8, 16)
POOL_DIM, POOL_WIDTH = 64, 256
POOL_HALO = 16
HEADS, Q_LORA, KV_LORA, QK_NOPE, QK_ROPE, V_DIM = 6, 384, 256, 128, 64, 128
QK_DIM = QK_NOPE + QK_ROPE
QK_PAD = 256
ODD_IN = POOL_WIDTH + Q_LORA + KV_LORA + QK_ROPE
ODD_IN_PAD = 1024
ROPE_THETA = 10000.0
ATTN_SCALE = QK_DIM ** -0.5
D_FF, N_CHIPS = 2816, 4
FF_SHARD = D_FF // N_CHIPS
ADAM_LR, ADAM_B1, ADAM_B2, ADAM_EPS, ADAM_WD, ADAM_STEP = 0.001, 0.9, 0.999, 1e-08, 0.01, 10
VMEM_LIMIT_V7X = 48 * 2**20
LANES, SUBLANES = 128, 8
MESH = pl.DeviceIdType.MESH
HBM = pl.BlockSpec(memory_space=pltpu.HBM)
VMEM = pl.BlockSpec(memory_space=pltpu.VMEM)

_DIMS = {"nn": (((1,), (0,)), ((), ())), "nt": (((1,), (1,)), ((), ())), "tn": (((0,), (0,)), ((), ()))}


def _dot(a, b, mode="nn"):
    return lax.dot_general(a.astype(BF16), b.astype(BF16), _DIMS[mode], preferred_element_type=F32)


def _call(body, *, name, out_shape, in_specs, out_specs, grid=(), scratch=(), aliases=None, after=()):
    params = pltpu.CompilerParams(vmem_limit_bytes=VMEM_LIMIT_V7X,
                                  **({"dimension_semantics": ("arbitrary",) * len(grid)} if grid else {}))
    n_in, n_after = len(in_specs), len(after)
    kernel_body = body if not after else (lambda *refs: body(*refs[:n_in], *refs[n_in + n_after:]))
    call = pl.pallas_call(kernel_body, name=name, grid=grid, in_specs=list(in_specs) + [pl.BlockSpec(memory_space=pl.ANY)] * n_after,
                          out_specs=out_specs, out_shape=out_shape, scratch_shapes=list(scratch),
                          input_output_aliases=aliases or {}, compiler_params=params)
    return (lambda *ops: call(*ops, *after)) if after else call


def _sds(shape, dtype):
    return jax.ShapeDtypeStruct(tuple(shape), dtype)


def _token_tile(seq):
    return 512 if seq % 512 == 0 else seq


_TAIL_ROWS = 256


def _matmul(name, mode, pairs, pair_specs, grid, out_shape, out_spec, acc_shape, add=None, add_spec=None, after=(), tail=None):
    n, nk = len(pairs), grid[-1]
    n_add = int(add is not None)
    n_tail = len(tail[0]) if tail else 0
    n_in = 2 * n + n_add + n_tail
    n_out = len(out_shape) if tail else 1

    def body(*refs):
        ab = refs[:2 * n]
        add_ref = refs[2 * n] if n_add else None
        tail_refs, outs = refs[2 * n + n_add:n_in], refs[n_in:n_in + n_out]
        first = pl.program_id(0) == 0

        def finish(result):
            if tail is None:
                r = result(slice(None))
                outs[0][...] = (r if add_ref is None else r + add_ref[...]).astype(outs[0].dtype)
                return
            for lo in range(0, acc_shape[0], _TAIL_ROWS):
                rows = slice(lo, min(lo + _TAIL_ROWS, acc_shape[0]))
                r = result(rows)
                tail[2](rows, r if add_ref is None else r + add_ref[rows, :], first, tail_refs, outs)

        if nk == 1:
            r = _dot(ab[0][...], ab[1][...], mode)
            for p in range(1, n):
                r = r + _dot(ab[2 * p][...], ab[2 * p + 1][...], mode)
            finish(lambda rows: r[rows])
            return
        acc = refs[-1]
        k = pl.program_id(len(grid) - 1)

        @pl.when(k == 0)
        def _():
            acc[...] = jnp.zeros_like(acc)

        for p in range(n):
            acc[...] += _dot(ab[2 * p][...], ab[2 * p + 1][...], mode)

        @pl.when(k == nk - 1)
        def _():
            finish(lambda rows: acc[rows, :])

    ops = [t for pr in pairs for t in pr] + ([add] if n_add else []) + (list(tail[0]) if tail else [])
    specs = [s for pr in pair_specs for s in pr] + ([add_spec] if n_add else []) + (list(tail[1]) if tail else [])
    return _call(body, name=name, grid=grid, in_specs=specs, out_specs=out_spec, out_shape=out_shape,
                 scratch=[pltpu.VMEM(acc_shape, F32)] if nk > 1 else [], after=after)(*ops)


def _row_spec(tm, d):
    return BS((tm, d), lambda i, j, k: (i, 0))


def _vec_spec(d):
    return BS((1, d), lambda i, j, k: (0, 0))


def _norm_tail(gain, T, tm):
    d = gain.shape[-1]

    def fn(rows, r, first, tail_refs, outs):
        outs[0][rows, :] = r
        outs[1][rows, :] = (r * lax.rsqrt(jnp.mean(r * r, axis=-1, keepdims=True) + EPS) * tail_refs[0][...]).astype(BF16)

    return ([gain.reshape(1, d)], [_vec_spec(d)], fn), [_sds((T, d), F32), _sds((T, d), BF16)], [_row_spec(tm, d), _row_spec(tm, d)]


def _norm_bwd_tail(x, gain, dres, tm):
    T, d = x.shape

    def fn(rows, r, first, tail_refs, outs):
        x_ref, g_ref, dres_ref = tail_refs
        xv = x_ref[rows, :]
        rstd = lax.rsqrt(jnp.mean(xv * xv, axis=-1, keepdims=True) + EPS)
        xhat = xv * rstd
        if rows.start == 0:
            @pl.when(first)
            def _():
                outs[1][...] = jnp.zeros_like(outs[1])

        outs[1][...] += jnp.sum(r * xhat, axis=0, keepdims=True)
        dxhat = r * g_ref[...]
        outs[0][rows, :] = dres_ref[rows, :] + rstd * (dxhat - xhat * jnp.mean(dxhat * xhat, axis=-1, keepdims=True))

    return (([x, gain.reshape(1, d), dres], [_row_spec(tm, d), _vec_spec(d), _row_spec(tm, d)], fn),
            [_sds((T, d), F32), _sds((1, d), F32)], [_row_spec(tm, d), _vec_spec(d)])


def _loss_tail(target, tm):
    T, d = target.shape

    def fn(rows, r, first, tail_refs, outs):
        e = r - tail_refs[0][rows, :]
        if rows.start == 0:
            @pl.when(first)
            def _():
                outs[1][...] = jnp.zeros_like(outs[1])

        outs[1][...] += jnp.sum(e * e)
        outs[0][rows, :] = e * (1.0 / d)

    return (([target], [_row_spec(tm, d)], fn), [_sds((T, d), F32), _sds((SUBLANES, LANES), F32)],
            [_row_spec(tm, d), BS((SUBLANES, LANES), lambda i, j, k: (0, 0))])


def _grad_shards(name, a, b, a_spec, b_spec, pick, out_shape, n_steps):
    def body(a_ref, b_ref, o_ref):
        @pl.when(pl.program_id(0) == 0)
        def _():
            o_ref[...] = jnp.zeros_like(o_ref)

        for j in range(N_CHIPS):
            aj, bj = pick(a_ref, b_ref, j)
            o_ref[j] += _dot(aj, bj, "tn")

    return _call(body, name=name, grid=(n_steps,), in_specs=[a_spec, b_spec],
                 out_specs=BS(out_shape, lambda k: (0, 0, 0)), out_shape=pltpu.HBM(tuple(out_shape), F32))(a, b)


def _mm(name, mode, a, b, out_dtype, tm=1024, tn=1024, tk=512, add=None, after=(), fused=None, hbm_out=False):
    if mode == "tn":
        (K, M), N = a.shape, b.shape[1]
    else:
        (M, K), N = a.shape, (b.shape[1] if mode == "nn" else b.shape[0])
    tm, tn, tk = min(tm, M), min(tn, N), min(tk, K)
    a_spec = BS((tk, tm), lambda i, j, k: (k, i)) if mode == "tn" else BS((tm, tk), lambda i, j, k: (i, k))
    b_spec = BS((tn, tk), lambda i, j, k: (j, k)) if mode == "nt" else BS((tk, tn), lambda i, j, k: (k, j))
    o_spec = BS((tm, tn), lambda i, j, k: (i, j))
    tail, shapes, specs = fused if fused else (None, pltpu.HBM((M, N), out_dtype) if hbm_out else _sds((M, N), out_dtype), o_spec)
    return _matmul(name, mode, [(a, b)], [(a_spec, b_spec)], (M // tm, N // tn, K // tk), shapes, specs, (tm, tn),
                   add=add, add_spec=o_spec if add is not None else None, after=after, tail=tail)


def _rmsnorm_fwd(name, x, g, tm):
    T, d = x.shape

    def body(x_ref, g_ref, o_ref):
        xv = x_ref[...]
        y = xv * lax.rsqrt(jnp.mean(xv * xv, axis=-1, keepdims=True) + EPS)
        o_ref[...] = (y * g_ref[...]).astype(o_ref.dtype)

    return _call(body, name=name, grid=(T // tm,), in_specs=[BS((tm, d), lambda i: (i, 0)), BS((1, d), lambda i: (0, 0))],
                 out_specs=BS((tm, d), lambda i: (i, 0)), out_shape=_sds((T, d), BF16))(x, g.reshape(1, d))


_PASS_ROWS = 256


def _ffn_up(name, h, wg, wu, tm):
    T = h.shape[0]

    def body(h_ref, wg_ref, wu_ref, g_ref, u_ref, a_ref):
        wg, wu = wg_ref[...], wu_ref[...]
        for lo in range(0, tm, _PASS_ROWS):
            rows = slice(lo, min(lo + _PASS_ROWS, tm))
            hv = h_ref[rows, :]
            g = _dot(hv, wg, "nt")
            u = _dot(hv, wu, "nt")
            g_ref[rows, :] = g.astype(BF16)
            u_ref[rows, :] = u.astype(BF16)
            a_ref[rows, :] = (g * (1.0 / (1.0 + jnp.exp(-g))) * u).astype(BF16)

    w_spec = BS((None, FF_SHARD, D_MODEL), lambda j, i: (j, 0, 0))
    o_spec = BS((None, tm, FF_SHARD), lambda j, i: (j, i, 0))
    sh = _sds((N_CHIPS, T, FF_SHARD), BF16)
    return _call(body, name=name, grid=(N_CHIPS, T // tm), in_specs=[BS((tm, D_MODEL), lambda j, i: (i, 0)), w_spec, w_spec],
                 out_specs=[o_spec, o_spec, o_spec], out_shape=[sh, sh, sh])(h, wg, wu)


def _ffn_act_bwd(name, dxo, wd, g, u, tm, after=()):
    T = dxo.shape[0]

    def body(dx_ref, wd_ref, g_ref, u_ref, dg_ref, du_ref):
        wd = wd_ref[...]
        for lo in range(0, tm, _PASS_ROWS):
            rows = slice(lo, min(lo + _PASS_ROWS, tm))
            da = _dot(dx_ref[rows, :], wd, "nt")
            g = g_ref[rows, :].astype(F32)
            sig = 1.0 / (1.0 + jnp.exp(-g))
            dg_ref[rows, :] = (da * u_ref[rows, :].astype(F32) * (sig * (1.0 + g * (1.0 - sig)))).astype(BF16)
            du_ref[rows, :] = (da * (g * sig)).astype(BF16)

    t_spec = BS((None, tm, FF_SHARD), lambda i, j: (j, i, 0))
    sh = _sds((N_CHIPS, T, FF_SHARD), BF16)
    return _call(body, name=name, grid=(T // tm, N_CHIPS),
                 in_specs=[BS((tm, D_MODEL), lambda i, j: (i, 0)), BS((None, FF_SHARD, D_MODEL), lambda i, j: (j, 0, 0)), t_spec, t_spec],
                 out_specs=[t_spec, t_spec], out_shape=[sh, sh], after=after)(dxo, wd, g, u)


def _big_tile(n):
    return min(1024, n)


def _ffn_fwd(l, x, h, wg, wu, wd, fused):
    T = x.shape[0]
    tm = _big_tile(T)
    g, u, a = _ffn_up(f"ffn{l}_up", h, wg, wu, tm)
    tn = D_MODEL
    tail, shapes, specs = fused
    outs = _matmul(f"ffn{l}_down", "nn", [(a, wd)],
                   [(BS((None, tm, FF_SHARD), lambda i, j, k: (k, i, 0)), BS((None, FF_SHARD, tn), lambda i, j, k: (k, 0, j)))],
                   (T // tm, D_MODEL // tn, N_CHIPS), shapes, specs, (tm, tn),
                   add=x, add_spec=BS((tm, tn), lambda i, j, k: (i, j)), tail=tail)
    return outs, (h, g, u, a)


def _ffn_bwd(l, x, gain, wg, wu, wd, saved, dxo, emit, after=()):
    h, g, u, a = saved
    T = x.shape[0]
    tm = _big_tile(T)
    dg, du = _ffn_act_bwd(f"ffn{l}_act_bwd", dxo, wd, g, u, tm, after=after)
    tk = min(512, T)
    tn = D_MODEL
    shards_spec = BS((N_CHIPS, tk, FF_SHARD), lambda k: (0, k, 0))
    rows_spec = BS((tk, D_MODEL), lambda k: (k, 0))

    def dw(nm, act, rows):
        return _grad_shards(nm, act, rows, shards_spec, rows_spec, lambda a_ref, b_ref, j: (a_ref[j], b_ref[...]),
                            (N_CHIPS, FF_SHARD, D_MODEL), T // tk)

    behind = emit(f"ffn{l}", {f"ffn_w_gate{l}": dw(f"ffn{l}_dwg", dg, h), f"ffn_w_up{l}": dw(f"ffn{l}_dwu", du, h),
                              f"ffn_w_down{l}": dw(f"ffn{l}_dwd", a, dxo)})
    act_spec = BS((None, tm, FF_SHARD), lambda i, j, k: (k, i, 0))
    w_spec = BS((None, FF_SHARD, tn), lambda i, j, k: (k, 0, j))
    tail, shapes, specs = _norm_bwd_tail(x, gain, dxo, tm)
    return _matmul(f"ffn{l}_dh", "nn", [(dg, wg), (du, wu)], [(act_spec, w_spec), (act_spec, w_spec)],
                   (T // tm, D_MODEL // tn, N_CHIPS), shapes, specs, (tm, tn), after=behind, tail=tail)


_INV_SQRT2 = 1.0 / math.sqrt(2.0)
_INV_SQRT_2PI = 1.0 / math.sqrt(2.0 * math.pi)


def _gelu(x):
    return 0.5 * x * (1.0 + lax.erf(x * _INV_SQRT2))


def _gelu_and_grad(x):
    cdf = 0.5 * (1.0 + lax.erf(x * _INV_SQRT2))
    return x * cdf, cdf + x * jnp.exp(-0.5 * x * x) * _INV_SQRT_2PI


def _shift_down(x, k):
    return pltpu.roll(x, k, 0)


def _shift_up(x, k):
    return pltpu.roll(x, x.shape[0] - k, 0)


def _layer_norm_head(xh):
    xc = xh - jnp.mean(xh, axis=-1, keepdims=True)
    rstd = lax.rsqrt(jnp.mean(xc * xc, axis=-1, keepdims=True) + EPS)
    return xc * rstd, rstd


def _even_halo_specs(tm, n_tiles, col_blocks, after):
    rows = tm // SUBLANES
    last = n_tiles * rows - 1
    if after:
        return [BS((SUBLANES, 512), functools.partial(lambda cb, i: (jnp.minimum((i + 1) * rows, last), cb), cb)) for cb in col_blocks]
    return [BS((SUBLANES, 512), functools.partial(lambda cb, i: (jnp.maximum(i * rows - 1, 0), cb), cb)) for cb in col_blocks]


def _even_mixer_fwd(proj, ln_g, w_tril, b_lanes, conv_w, seq, tm):
    T = proj.shape[0]
    tiles_per_seq = seq // tm

    def body(p_ref, hc_ref, hh_ref, lng_ref, w_ref, bb_ref, cw_ref, o_ref):
        first = pl.program_id(0) % tiles_per_seq == 0
        for h in range(SG_HEADS):
            cols = slice(SG_DIM * h, SG_DIM * (h + 1))
            vhat, _ = _layer_norm_head(_gelu(p_ref[:, SG_WIDTH + SG_DIM * h:SG_WIDTH + SG_DIM * (h + 1)]))
            vln = (vhat * lng_ref[:, cols]).astype(BF16)
            for k in range(tm // SG_CHUNK):
                rows = slice(SG_CHUNK * k, SG_CHUNK * (k + 1))
                mixed = _dot(w_ref[h], vln[rows]) + bb_ref[h]
                o_ref[rows, cols] = (_gelu(p_ref[rows, cols]) * mixed).astype(BF16)
        z = p_ref[:, 1536:2048] * p_ref[:, 2048:2560]
        zz = jnp.concatenate([jnp.where(first, 0.0, hc_ref[...] * hh_ref[...]), z], axis=0)
        y = cw_ref[0:1, :] * _shift_down(zz, 2)[SUBLANES:] + cw_ref[1:2, :] * _shift_down(zz, 1)[SUBLANES:] + cw_ref[2:3, :] * z
        o_ref[:, SG_WIDTH:] = (p_ref[:, 1024:1536] * y).astype(BF16)

    full = lambda shape: BS(shape, lambda i: (0,) * len(shape))
    return _call(body, name="even_mixer_fwd", grid=(T // tm,),
                 in_specs=[BS((tm, EVEN_IN), lambda i: (i, 0))] + _even_halo_specs(tm, T // tm, (3, 4), after=False)
                 + [full((1, SG_WIDTH)), full((SG_HEADS, SG_CHUNK, SG_CHUNK)), full((SG_HEADS, SG_CHUNK, SG_DIM)), full((SUBLANES, SC_WIDTH))],
                 out_specs=BS((tm, D_MODEL), lambda i: (i, 0)), out_shape=_sds((T, D_MODEL), BF16))(
        proj, proj, proj, ln_g, w_tril, b_lanes, conv_w)


def _even_mixer_bwd(proj, dmix, ln_g, w_tril, b_lanes, conv_w, seq, tm):
    T = proj.shape[0]
    n_tiles, tiles_per_seq = T // tm, seq // tm

    def body(p_ref, dm_ref, hc_ref, hh_ref, nd_ref, nb_ref, lng_ref, w_ref, bb_ref, cw_ref,
             dp_ref, dw_ref, db_ref, dlng_ref, dcw_ref):
        i = pl.program_id(0)
        first = i % tiles_per_seq == 0
        last = i % tiles_per_seq == tiles_per_seq - 1

        @pl.when(i == 0)
        def _():
            dw_ref[...] = jnp.zeros_like(dw_ref)
            db_ref[...] = jnp.zeros_like(db_ref)
            dlng_ref[...] = jnp.zeros_like(dlng_ref)
            dcw_ref[...] = jnp.zeros_like(dcw_ref)

        for h in range(SG_HEADS):
            cols = slice(SG_DIM * h, SG_DIM * (h + 1))
            vcols = slice(SG_WIDTH + SG_DIM * h, SG_WIDTH + SG_DIM * (h + 1))
            lng = lng_ref[:, cols]
            for k in range(tm // SG_CHUNK):
                rows = slice(SG_CHUNK * k, SG_CHUNK * (k + 1))
                gelu_v, dgelu_v = _gelu_and_grad(p_ref[rows, vcols])
                vhat, rstd = _layer_norm_head(gelu_v)
                vln = (vhat * lng).astype(BF16)
                mixed = _dot(w_ref[h], vln) + bb_ref[h]
                gelu_u, dgelu_u = _gelu_and_grad(p_ref[rows, cols])
                da = dm_ref[rows, cols]
                dp_ref[rows, cols] = (da * mixed * dgelu_u).astype(BF16)
                dmixed = da * gelu_u
                db_ref[h] += dmixed
                dw_ref[h] += _dot(dmixed, vln, "nt")
                dvln = _dot(w_ref[h], dmixed, "tn")
                dlng_ref[:, cols] += jnp.sum(dvln * vhat, axis=0, keepdims=True)
                dvhat = dvln * lng
                dgv = rstd * (dvhat - jnp.mean(dvhat, axis=-1, keepdims=True)
                              - vhat * jnp.mean(dvhat * vhat, axis=-1, keepdims=True))
                dp_ref[rows, vcols] = (dgv * dgelu_v).astype(BF16)

        b = p_ref[:, 1024:1536]
        c = p_ref[:, 1536:2048]
        hv = p_ref[:, 2048:2560]
        z = c * hv
        zz = jnp.concatenate([jnp.where(first, 0.0, hc_ref[...] * hh_ref[...]), z], axis=0)
        z1 = _shift_down(zz, 1)[SUBLANES:]
        z2 = _shift_down(zz, 2)[SUBLANES:]
        w0, w1, w2 = cw_ref[0:1, :], cw_ref[1:2, :], cw_ref[2:3, :]
        dbo = dm_ref[:, SG_WIDTH:]
        dy = dbo * b
        dd = jnp.concatenate([dy, jnp.where(last, 0.0, nd_ref[...] * nb_ref[...])], axis=0)
        dz = w2 * dy + w1 * _shift_up(dd, 1)[:tm] + w0 * _shift_up(dd, 2)[:tm]
        dp_ref[:, 1024:1536] = (dbo * (w0 * z2 + w1 * z1 + w2 * z)).astype(BF16)
        dp_ref[:, 1536:2048] = (dz * hv).astype(BF16)
        dp_ref[:, 2048:2560] = (dz * c).astype(BF16)
        dcw_ref[0:1, :] += jnp.sum(dy * z2, axis=0, keepdims=True)
        dcw_ref[1:2, :] += jnp.sum(dy * z1, axis=0, keepdims=True)
        dcw_ref[2:3, :] += jnp.sum(dy * z, axis=0, keepdims=True)

        @pl.when(i == n_tiles - 1)
        def _():
            t_idx = lax.broadcasted_iota(jnp.int32, (SG_CHUNK, SG_CHUNK), 0)
            s_idx = lax.broadcasted_iota(jnp.int32, (SG_CHUNK, SG_CHUNK), 1)
            for h in range(SG_HEADS):
                dw_ref[h] = jnp.where(t_idx >= s_idx, dw_ref[h], 0.0)

    full = lambda shape: BS(shape, lambda i: (0,) * len(shape))
    sq = (SG_HEADS, SG_CHUNK, SG_CHUNK)
    return _call(body, name="even_mixer_bwd", grid=(n_tiles,),
                 in_specs=[BS((tm, EVEN_IN), lambda i: (i, 0)), BS((tm, D_MODEL), lambda i: (i, 0))]
                 + _even_halo_specs(tm, n_tiles, (3, 4), after=False)
                 + _even_halo_specs(tm, n_tiles, (1,), after=True) + _even_halo_specs(tm, n_tiles, (2,), after=True)
                 + [full((1, SG_WIDTH)), full(sq), full(sq), full((SUBLANES, SC_WIDTH))],
                 out_specs=[BS((tm, EVEN_IN), lambda i: (i, 0)), full(sq), full(sq), full((1, SG_WIDTH)), full((SUBLANES, SC_WIDTH))],
                 out_shape=[_sds((T, EVEN_IN), BF16), _sds(sq, F32), _sds(sq, F32), _sds((1, SG_WIDTH), F32), _sds((SUBLANES, SC_WIDTH), F32)])(
        proj, dmix, proj, proj, dmix, proj, ln_g, w_tril, b_lanes, conv_w)


def _pool_select(vals):
    lane = lax.broadcasted_iota(jnp.int32, vals[0].shape, 1)
    out = vals[-1]
    for g in range(len(vals) - 2, -1, -1):
        out = jnp.where(lane < POOL_DIM * (g + 1), vals[g], out)
    return out


def _pool_counts(pos1):
    lane = lax.broadcasted_iota(jnp.int32, (pos1.shape[0], POOL_WIDTH), 1)
    win = _pool_select([jnp.full(lane.shape, float(w), F32) for w in POOL_WINDOWS])
    return jnp.minimum(pos1, win)


def _pool_means(zz, counts):
    s2 = zz + _shift_down(zz, 1)
    s4 = s2 + _shift_down(s2, 2)
    s8 = s4 + _shift_down(s4, 4)
    s16 = s8 + _shift_down(s8, 8)
    return _pool_select([s2, s4, s8, s16])[POOL_HALO:] / counts


def _pool_halo_spec(tm, n_tiles, after):
    rows = tm // POOL_HALO
    if after:
        return BS((POOL_HALO, POOL_WIDTH), lambda i: (jnp.minimum((i + 1) * rows, n_tiles * rows - 1), 0))
    return BS((POOL_HALO, POOL_WIDTH), lambda i: (jnp.maximum(i * rows - 1, 0), 0))


def _pool_fwd(proj, w_diag, scale, seq, tm):
    T = proj.shape[0]
    tiles_per_seq = seq // tm

    def body(z_ref, zh_ref, w_ref, s_ref, o_ref):
        t = pl.program_id(0) % tiles_per_seq
        z = z_ref[...]
        zz = jnp.concatenate([jnp.where(t == 0, 0.0, zh_ref[...]), z], axis=0)
        pos1 = (lax.broadcasted_iota(jnp.int32, (tm, 1), 0) + (t * tm + 1)).astype(F32)
        pooled = _pool_means(zz, _pool_counts(pos1)) - z
        o_ref[...] = (_dot(pooled, w_ref[...]) * s_ref[...]).astype(BF16)

    full = lambda shape: BS(shape, lambda i: (0,) * len(shape))
    return _call(body, name="pool_fwd", grid=(T // tm,),
                 in_specs=[BS((tm, POOL_WIDTH), lambda i: (i, 0)), _pool_halo_spec(tm, T // tm, False),
                           full((POOL_WIDTH, POOL_WIDTH)), full((1, POOL_WIDTH))],
                 out_specs=BS((tm, POOL_WIDTH), lambda i: (i, 0)), out_shape=_sds((T, D_MODEL), BF16))(proj, proj, w_diag, scale)


def _pool_bwd(proj, dmix, w_diag, scale, seq, tm):
    T = proj.shape[0]
    n_tiles, tiles_per_seq = T // tm, seq // tm

    def body(z_ref, zh_ref, do_ref, don_ref, w_ref, s_ref, dz_ref, dw_ref, ds_ref):
        i = pl.program_id(0)
        t = i % tiles_per_seq

        @pl.when(i == 0)
        def _():
            dw_ref[...] = jnp.zeros_like(dw_ref)
            ds_ref[...] = jnp.zeros_like(ds_ref)

        z = z_ref[...]
        zz = jnp.concatenate([jnp.where(t == 0, 0.0, zh_ref[...]), z], axis=0)
        pos1 = (lax.broadcasted_iota(jnp.int32, (tm, 1), 0) + (t * tm + 1)).astype(F32)
        counts = _pool_counts(pos1)
        pooled = _pool_means(zz, counts) - z
        dout = do_ref[...].astype(F32)
        ds_ref[...] += jnp.sum(dout * _dot(pooled, w_ref[...]), axis=0, keepdims=True)
        dlin = dout * s_ref[...]
        dw_ref[...] += _dot(pooled, dlin, "tn")
        dpooled = _dot(dlin, w_ref[...], "nt")
        dpooled_n = _dot(don_ref[...].astype(F32) * s_ref[...], w_ref[...], "nt")
        pos1_n = (lax.broadcasted_iota(jnp.int32, (POOL_HALO, 1), 0) + ((t + 1) * tm + 1)).astype(F32)
        dmean_n = jnp.where(t == tiles_per_seq - 1, 0.0, dpooled_n / _pool_counts(pos1_n))
        dd = jnp.concatenate([dpooled / counts, dmean_n], axis=0)
        r2 = dd + _shift_up(dd, 1)
        r4 = r2 + _shift_up(r2, 2)
        r8 = r4 + _shift_up(r4, 4)
        r16 = r8 + _shift_up(r8, 8)
        dz_ref[...] = (_pool_select([r2, r4, r8, r16])[:tm] - dpooled).astype(BF16)

    full = lambda shape: BS(shape, lambda i: (0,) * len(shape))
    return _call(body, name="pool_bwd", grid=(n_tiles,),
                 in_specs=[BS((tm, POOL_WIDTH), lambda i: (i, 0)), _pool_halo_spec(tm, n_tiles, False),
                           BS((tm, POOL_WIDTH), lambda i: (i, 0)), _pool_halo_spec(tm, n_tiles, True),
                           full((POOL_WIDTH, POOL_WIDTH)), full((1, POOL_WIDTH))],
                 out_specs=[BS((tm, POOL_WIDTH), lambda i: (i, 0)), full((POOL_WIDTH, POOL_WIDTH)), full((1, POOL_WIDTH))],
                 out_shape=[_sds((T, POOL_WIDTH), BF16), _sds((POOL_WIDTH, POOL_WIDTH), F32), _sds((1, POOL_WIDTH), F32)])(
        proj, proj, dmix, dmix, w_diag, scale)


def _rope_partner(r):
    lane = lax.broadcasted_iota(jnp.int32, r.shape, 1)
    return jnp.where(lane < QK_ROPE // 2, pltpu.roll(r, LANES - QK_ROPE // 2, 1), pltpu.roll(r, QK_ROPE // 2, 1))


def _rope(x, cos, sin_signed):
    r = x[:, QK_NOPE:]
    return jnp.concatenate([x[:, :QK_NOPE], r * cos + _rope_partner(r) * sin_signed], axis=1)


def _rope_transposed(dx, cos, sin_signed):
    dr = dx[:, QK_NOPE:]
    return jnp.concatenate([dx[:, :QK_NOPE], dr * cos + _rope_partner(dr * sin_signed)], axis=1)


def _head_norm(x):
    r = lax.rsqrt(jnp.sum(x * x, axis=-1, keepdims=True) * (1.0 / QK_DIM) + EPS)
    return x * r, r


def _head_norm_bwd(dy, xhat, r, gain):
    dxhat = dy * gain
    return r * (dxhat - xhat * (jnp.sum(dxhat * xhat, axis=-1, keepdims=True) * (1.0 / QK_DIM)))


def _latents(p_ref, qag_ref, kvag_ref):
    ql = p_ref[:, POOL_WIDTH:POOL_WIDTH + Q_LORA]
    kvl = p_ref[:, POOL_WIDTH + Q_LORA:POOL_WIDTH + Q_LORA + KV_LORA]
    rq = lax.rsqrt(jnp.mean(ql * ql, axis=-1, keepdims=True) + EPS)
    rkv = lax.rsqrt(jnp.mean(kvl * kvl, axis=-1, keepdims=True) + EPS)
    return ql * rq, rq, kvl * rkv, rkv


def _mla_specs(tm):
    full = lambda shape: BS(shape, lambda i, h: (0,) * len(shape))
    return [BS((tm, ODD_IN_PAD), lambda i, h: (i, 0)), BS((tm, LANES), lambda i, h: (i, 0)), BS((tm, LANES), lambda i, h: (i, 0)),
            full((1, Q_LORA)), full((1, KV_LORA)), BS((None, Q_LORA, QK_PAD), lambda i, h: (h, 0, 0)),
            BS((None, KV_LORA, QK_PAD), lambda i, h: (h, 0, 0)), full((1, QK_PAD)), full((1, QK_PAD))]


def _mla_qkv_fwd(proj, cos, sin_signed, qa_g, kva_g, q_b, kv_b, q_g, k_g, tm):
    T = proj.shape[0]

    def body(p_ref, cos_ref, sin_ref, qag_ref, kvag_ref, qb_ref, kvb_ref, qg_ref, kg_ref, q_ref, k_ref, v_ref, qn_s, kvn_s):
        @pl.when(pl.program_id(1) == 0)
        def _():
            qhat, _, kvhat, _ = _latents(p_ref, qag_ref, kvag_ref)
            qn_s[...] = (qhat * qag_ref[...]).astype(BF16)
            kvn_s[...] = (kvhat * kvag_ref[...]).astype(BF16)

        cos, sin = cos_ref[...], sin_ref[...]
        qhat, _ = _head_norm(_dot(qn_s[...], qb_ref[...]))
        q_ref[...] = _rope(qhat * qg_ref[...], cos, sin).astype(BF16)
        kv = _dot(kvn_s[...], kvb_ref[...])
        khat, _ = _head_norm(jnp.concatenate([kv[:, :QK_NOPE], p_ref[:, ODD_IN_PAD - LANES:]], axis=1))
        k_ref[...] = _rope(khat * kg_ref[...], cos, sin).astype(BF16)
        v_ref[...] = kv[:, QK_NOPE:].astype(BF16)

    qk_spec = BS((None, tm, QK_PAD), lambda i, h: (h, i, 0))
    return _call(body, name="mla_qkv_fwd", grid=(T // tm, HEADS), in_specs=_mla_specs(tm),
                 out_specs=[qk_spec, qk_spec, BS((None, tm, V_DIM), lambda i, h: (h, i, 0))],
                 out_shape=[_sds((HEADS, T, QK_PAD), BF16), _sds((HEADS, T, QK_PAD), BF16), _sds((HEADS, T, V_DIM), BF16)],
                 scratch=[pltpu.VMEM((tm, Q_LORA), BF16), pltpu.VMEM((tm, KV_LORA), BF16)])(
        proj, cos, sin_signed, qa_g, kva_g, q_b, kv_b, q_g, k_g)


def _mla_qkv_bwd(proj, cos, sin_signed, qa_g, kva_g, q_b, kv_b, q_g, k_g, dq, dk, dv, dz_pool, tm):
    T = proj.shape[0]
    n_tiles = T // tm
    chain_rows = min(_PASS_ROWS, tm)

    def body(p_ref, cos_ref, sin_ref, qag_ref, kvag_ref, qb_ref, kvb_ref, qg_ref, kg_ref, dq_ref, dk_ref, dv_ref, dzp_ref,
             dp_ref, dqb_ref, dkvb_ref, dqg_ref, dkg_ref, dqag_ref, dkvag_ref, qn_s, kvn_s, dqn_s, dkvn_s, dkr_s,
             qh_s, kv_s, dqh_s, dkv_s):
        i, h = pl.program_id(0), pl.program_id(1)

        @pl.when((i == 0) & (h == 0))
        def _():
            for ref in (dqb_ref, dkvb_ref, dqg_ref, dkg_ref, dqag_ref, dkvag_ref):
                ref[...] = jnp.zeros_like(ref)

        @pl.when(h == 0)
        def _():
            qhat, _, kvhat, _ = _latents(p_ref, qag_ref, kvag_ref)
            qn_s[...] = (qhat * qag_ref[...]).astype(BF16)
            kvn_s[...] = (kvhat * kvag_ref[...]).astype(BF16)
            dqn_s[...] = jnp.zeros_like(dqn_s)
            dkvn_s[...] = jnp.zeros_like(dkvn_s)
            dkr_s[...] = jnp.zeros_like(dkr_s)

        qh_s[...] = _dot(qn_s[...], qb_ref[...])
        kv_s[...] = _dot(kvn_s[...], kvb_ref[...])
        qg, kg = qg_ref[...], kg_ref[...]

        def chunk(c, gains):
            dqg, dkg = gains
            rows = slice(c * chain_rows, (c + 1) * chain_rows)
            cos, sin = cos_ref[rows, :], sin_ref[rows, :]
            qhat, rq = _head_norm(qh_s[rows, :])
            dqn_head = _rope_transposed(dq_ref[rows, :], cos, sin)
            dqh_s[rows, :] = _head_norm_bwd(dqn_head, qhat, rq, qg).astype(BF16)
            kv = kv_s[rows, :]
            khat, rk = _head_norm(jnp.concatenate([kv[:, :QK_NOPE], p_ref[rows, ODD_IN_PAD - LANES:]], axis=1))
            dkn_head = _rope_transposed(dk_ref[rows, :], cos, sin)
            dkf = _head_norm_bwd(dkn_head, khat, rk, kg)
            dkr_s[rows, :] += dkf[:, QK_NOPE:]
            dkv_s[rows, :] = jnp.concatenate([dkf[:, :QK_NOPE], dv_ref[rows, :]], axis=1).astype(BF16)
            return dqg + dqn_head * qhat, dkg + dkn_head * khat

        dqg = dkg = jnp.zeros((chain_rows, QK_PAD), F32)
        for c in range(tm // chain_rows):
            dqg, dkg = chunk(c, (dqg, dkg))
        dqg_ref[...] += jnp.sum(dqg, axis=0, keepdims=True)
        dkg_ref[...] += jnp.sum(dkg, axis=0, keepdims=True)
        dqb_ref[h] += _dot(qn_s[...], dqh_s[...], "tn")
        dqn_s[...] += _dot(dqh_s[...], qb_ref[...], "nt")
        dkvb_ref[h] += _dot(kvn_s[...], dkv_s[...], "tn")
        dkvn_s[...] += _dot(dkv_s[...], kvb_ref[...], "nt")

        @pl.when(h == HEADS - 1)
        def _():
            qhat_l, rql, kvhat_l, rkvl = _latents(p_ref, qag_ref, kvag_ref)
            dqn, dkvn = dqn_s[...], dkvn_s[...]
            dqag_ref[...] += jnp.sum(dqn * qhat_l, axis=0, keepdims=True)
            dkvag_ref[...] += jnp.sum(dkvn * kvhat_l, axis=0, keepdims=True)
            dqx, dkvx = dqn * qag_ref[...], dkvn * kvag_ref[...]
            dp_ref[:, :POOL_WIDTH] = dzp_ref[...]
            dp_ref[:, POOL_WIDTH:POOL_WIDTH + Q_LORA] = (
                rql * (dqx - qhat_l * jnp.mean(dqx * qhat_l, axis=-1, keepdims=True))).astype(BF16)
            dp_ref[:, POOL_WIDTH + Q_LORA:ODD_IN_PAD - LANES] = (
                rkvl * (dkvx - kvhat_l * jnp.mean(dkvx * kvhat_l, axis=-1, keepdims=True))).astype(BF16)
            dp_ref[:, ODD_IN_PAD - LANES:] = dkr_s[:, :QK_ROPE].astype(BF16)

    full = lambda shape: BS(shape, lambda i, h: (0,) * len(shape))
    qk_spec = BS((None, tm, QK_PAD), lambda i, h: (h, i, 0))
    return _call(body, name="mla_qkv_bwd", grid=(n_tiles, HEADS),
                 in_specs=_mla_specs(tm) + [qk_spec, qk_spec, BS((None, tm, V_DIM), lambda i, h: (h, i, 0)),
                                            BS((tm, POOL_WIDTH), lambda i, h: (i, 0))],
                 out_specs=[BS((tm, ODD_IN), lambda i, h: (i, 0)), full((HEADS, Q_LORA, QK_PAD)), full((HEADS, KV_LORA, QK_PAD)),
                            full((1, QK_PAD)), full((1, QK_PAD)), full((1, Q_LORA)), full((1, KV_LORA))],
                 out_shape=[_sds((T, ODD_IN), BF16),_sds((HEADS, Q_LORA, QK_PAD), F32), _sds((HEADS, KV_LORA, QK_PAD), F32),
                            _sds((1, QK_PAD), F32), _sds((1, QK_PAD), F32), _sds((1, Q_LORA), F32), _sds((1, KV_LORA), F32)],
                 scratch=[pltpu.VMEM((tm, Q_LORA), BF16), pltpu.VMEM((tm, KV_LORA), BF16), pltpu.VMEM((tm, Q_LORA), F32),
                          pltpu.VMEM((tm, KV_LORA), F32), pltpu.VMEM((tm, LANES), F32), pltpu.VMEM((tm, QK_PAD), F32),
                          pltpu.VMEM((tm, QK_PAD), F32), pltpu.VMEM((tm, QK_PAD), BF16), pltpu.VMEM((tm, QK_PAD), BF16)])(
        proj, cos, sin_signed, qa_g, kva_g, q_b, kv_b, q_g, k_g, dq, dk, dv, dz_pool)


def _attn_tile(seq):
    return 512 if seq % 512 == 0 else seq


def _causal_mask(s):
    row = lax.broadcasted_iota(jnp.int32, s.shape, 0)
    col = lax.broadcasted_iota(jnp.int32, s.shape, 1)
    return jnp.where(row >= col, s, NEG_INF)


def _tile(i, t):
    return slice(i * t, (i + 1) * t)


def _flash_fwd(q, k, v, mix, batch, seq):
    t = _attn_tile(seq)
    nq = seq // t

    def body(q_ref, k_ref, v_ref, _, o_ref, lse_ref):
        for qi in range(nq):
            rows, before = _tile(qi, t), slice(0, qi * t)
            qv = q_ref[rows, :]
            s_diag = _causal_mask(_dot(qv, k_ref[rows, :], "nt") * ATTN_SCALE)
            m = jnp.max(s_diag, axis=-1, keepdims=True)
            if qi:
                s_before = _dot(qv, k_ref[before, :], "nt") * ATTN_SCALE
                m = jnp.maximum(m, jnp.max(s_before, axis=-1, keepdims=True))
            p = jnp.exp(s_diag - m)
            l = jnp.sum(p, axis=-1, keepdims=True)
            acc = _dot(p, v_ref[rows, :])
            if qi:
                p = jnp.exp(s_before - m)
                l = l + jnp.sum(p, axis=-1, keepdims=True)
                acc = acc + _dot(p, v_ref[before, :])
            o_ref[rows, :] = (acc / l).astype(BF16)
            lse_ref[rows, :] = jnp.broadcast_to(m + jnp.log(l), (t, LANES))

    T = batch * seq
    whole = lambda w: BS((None, seq, w), lambda b, h: (h, b, 0))
    return _call(body, name="flash_fwd", grid=(batch, HEADS),
                 in_specs=[whole(QK_PAD), whole(QK_PAD), whole(V_DIM), pl.BlockSpec(memory_space=pl.ANY)],
                 out_specs=[BS((seq, V_DIM), lambda b, h: (b, POOL_WIDTH // V_DIM + h)), whole(LANES)],
                 out_shape=[_sds((T, D_MODEL), BF16), _sds((HEADS, T, LANES), F32)],
                 aliases={3: 0})(q, k, v, mix)


def _flash_bwd(q, k, v, dmix, mix, lse, batch, seq):
    t = _attn_tile(seq)
    nq = seq // t

    def body(q_ref, k_ref, v_ref, do_ref, o_ref, lse_ref, dq_ref, dk_ref, dv_ref):
        for qi in range(nq):
            rows, before = _tile(qi, t), slice(0, qi * t)
            qv, do = q_ref[rows, :], do_ref[rows, :]
            lse = lse_ref[rows, 0:1]
            delta = jnp.sum(do.astype(F32) * o_ref[rows, :].astype(F32), axis=-1, keepdims=True)

            def block(keys, masked):
                kk = k_ref[keys, :]
                s = _dot(qv, kk, "nt") * ATTN_SCALE
                p = jnp.exp((_causal_mask(s) if masked else s) - lse)
                ds = p * (_dot(do, v_ref[keys, :], "nt") - delta) * ATTN_SCALE
                return _dot(p, do, "tn"), _dot(ds, qv, "tn"), _dot(ds, kk)

            dv_ref[rows, :], dk_ref[rows, :], dq = block(rows, True)
            if qi:
                dv, dk, dq_before = block(before, False)
                dv_ref[before, :] += dv
                dk_ref[before, :] += dk
                dq = dq + dq_before
            dq_ref[rows, :] = dq

    T = batch * seq
    whole = lambda w: BS((None, seq, w), lambda b, h: (h, b, 0))
    head_cols = BS((seq, V_DIM), lambda b, h: (b, POOL_WIDTH // V_DIM + h))
    return _call(body, name="flash_bwd", grid=(batch, HEADS),
                 in_specs=[whole(QK_PAD), whole(QK_PAD), whole(V_DIM), head_cols, head_cols, whole(LANES)],
                 out_specs=[whole(QK_PAD), whole(QK_PAD), whole(V_DIM)],
                 out_shape=[_sds((HEADS, T, QK_PAD), F32), _sds((HEADS, T, QK_PAD), F32), _sds((HEADS, T, V_DIM), F32)])(
        q, k, v, dmix, mix, lse)


def _adamw_math(w, g, m, v):
    m = ADAM_B1 * m + (1.0 - ADAM_B1) * g
    v = ADAM_B2 * v + (1.0 - ADAM_B2) * (g * g)
    m_hat = m / (1.0 - ADAM_B1 ** ADAM_STEP)
    v_hat = v / (1.0 - ADAM_B2 ** ADAM_STEP)
    return -ADAM_LR * (m_hat / (jnp.sqrt(v_hat) + ADAM_EPS) + ADAM_WD * w), m, v


def _adamw(name, w, g, m, v, l=0, prev=()):
    L, R, C = w.shape
    tr = 256 if R % 256 == 0 else R

    def body(w_ref, g_ref, m_ref, v_ref, *rest):
        go_ref, d_ref, mo_ref, vo_ref, token = rest[-5:]
        gv = g_ref[...]
        d_ref[...], mo_ref[...], vo_ref[...] = _adamw_math(w_ref[...], gv, m_ref[...], v_ref[...])
        go_ref[...] = gv
        token[...] = jnp.zeros_like(token)

    layer = BS((None, tr, C), lambda i: (l, i, 0))
    return _call(body, name=f"{name}_{l}", grid=(R // tr,),
                 in_specs=[layer, BS((tr, C), lambda i: (i, 0)), layer, layer] + [pl.BlockSpec(memory_space=pl.ANY)] * len(prev),
                 out_specs=[layer] * 4 + [BS((SUBLANES, LANES), lambda i: (0, 0))],
                 out_shape=[_sds((L, R, C), F32)] * 4 + [_sds((SUBLANES, LANES), F32)],
                 aliases={4 + n: n for n in range(len(prev))})(w, g, m, v, *prev)


def _place():
    x, y, c = lax.axis_index("x"), lax.axis_index("y"), lax.axis_index("c")
    other_chips = [(1 - x, y), (x, 1 - y), (1 - x, 1 - y)]
    return x, y, c, other_chips


def _remote(src, dst, send_sem, recv_sem, dev):
    return pltpu.make_async_remote_copy(src_ref=src, dst_ref=dst, send_sem=send_sem, recv_sem=recv_sem,
                                        device_id=dev, device_id_type=MESH)


def _prefetch_call(body, *, name, grid, in_specs, out_specs, out_shape):
    grid_spec = pltpu.PrefetchScalarGridSpec(num_scalar_prefetch=1, grid=grid, in_specs=in_specs, out_specs=out_specs)
    params = pltpu.CompilerParams(vmem_limit_bytes=VMEM_LIMIT_V7X, dimension_semantics=("arbitrary",) * len(grid))
    return pl.pallas_call(body, name=name, grid_spec=grid_spec, out_shape=out_shape, compiler_params=params)


def _row_tile(rows):
    return 256 if rows % 256 == 0 else rows


def _cast_place(name, w, layer, chip, after=()):
    _, _, rows, C = w.shape
    tr = _row_tile(rows)

    def body(chip_ref, w_ref, *rest):
        rest[-1][...] = w_ref[...].astype(BF16)

    return _prefetch_call(body, name=name, grid=(2, rows // tr),
                          in_specs=[BS((None, None, tr, C), lambda h, i, chip_ref: (layer, h, i, 0))]
                          + [pl.BlockSpec(memory_space=pl.ANY)] * len(after),
                          out_specs=BS((None, None, tr, C), lambda h, i, chip_ref: (chip_ref[0], h, i, 0)),
                          out_shape=pltpu.HBM((N_CHIPS, 2, rows, C), BF16))(chip, w, *after)


SEM = pl.BlockSpec(memory_space=pltpu.SEMAPHORE)


def _split_copy_call(body, *, name, in_specs, out_specs, out_shape, aliases):
    return pl.pallas_call(body, name=name, in_specs=in_specs, out_specs=out_specs, out_shape=out_shape,
                          input_output_aliases=aliases,
                          compiler_params=pltpu.CompilerParams(has_side_effects=pltpu.SideEffectType.DATAFLOW_SIDE_EFFECTING))


def _hbm(arrays):
    return [pltpu.with_memory_space_constraint(a, pltpu.HBM) for a in arrays]


def _gather_send(name, gs, groups, after):
    n = len(gs)

    def body(*refs):
        g, sems, token = refs[:n], refs[n + len(after):n + len(after) + 2 * len(groups)], refs[-1]
        x, y, c, chips = _place()
        me = 2 * x + y
        for gi, members in enumerate(groups):
            for a, i in enumerate(members):
                for k, (px, py) in enumerate(chips):
                    _remote(g[i].at[me, c], g[i].at[me, c], sems[2 * gi].at[3 * a + k], sems[2 * gi + 1].at[3 * a + k],
                            (px, py, c)).start()
        token[...] = jnp.zeros_like(token)

    sem_shapes = [pltpu.SemaphoreType.DMA((3 * len(members),)) for members in groups for _ in range(2)]
    out = _split_copy_call(body, name=name, in_specs=[HBM] * n + [pl.BlockSpec(memory_space=pl.ANY)] * len(after),
                           out_specs=[SEM] * len(sem_shapes) + [HBM] * n + [VMEM],
                           out_shape=sem_shapes + [pltpu.HBM(a.shape, a.dtype) for a in gs] + [_sds((SUBLANES, LANES), F32)],
                           aliases={i: len(sem_shapes) + i for i in range(n)})(*_hbm(gs), *after)
    return out[:len(sem_shapes)], out[len(sem_shapes):-1], out[-1]


def _gather_wait(name, gs, send_sems, recv_sems, after):
    n = len(gs)

    def body(*refs):
        g, ssem, rsem = refs[:n], refs[n], refs[n + 1]
        x, y, c, chips = _place()
        me = 2 * x + y
        for a in range(n):
            for k, (px, py) in enumerate(chips):
                landed = g[a].at[2 * px + py, c]
                cp = _remote(g[a].at[me, c], landed, ssem.at[3 * a + k], rsem.at[3 * a + k], (px, py, c))
                cp.wait_recv()
                cp.wait_send()

    return _split_copy_call(body, name=name, in_specs=[HBM] * n + [SEM, SEM] + [pl.BlockSpec(memory_space=pl.ANY)] * len(after),
                            out_specs=[HBM] * n, out_shape=[pltpu.HBM(a.shape, a.dtype) for a in gs],
                            aliases={i: i for i in range(n)})(*gs, send_sems, recv_sems, *after)


def _gather_pass(name, gs):
    n = len(gs)

    def body(*refs):
        g, send_sems, recv_sems = refs[n:2 * n], refs[-2], refs[-1]
        x, y, c, chips = _place()
        sibling = (x, y, 1 - c)
        passed = [_remote(g[i].at[2 * px + py, c], g[i].at[2 * px + py, c], send_sems.at[3 * i + k], recv_sems.at[3 * i + k], sibling)
                  for i in range(n) for k, (px, py) in enumerate(chips)]
        for cp in passed:
            cp.start()
        for i in range(n):
            for k, (px, py) in enumerate(chips):
                theirs = g[i].at[2 * px + py, 1 - c]
                _remote(theirs, theirs, send_sems.at[3 * i + k], recv_sems.at[3 * i + k], sibling).wait_recv()
        for cp in passed:
            cp.wait_send()

    return _call(body, name=name, in_specs=[HBM] * n, out_specs=[HBM] * n, out_shape=[_sds(a.shape, a.dtype) for a in gs],
                 aliases={i: i for i in range(n)},
                 scratch=[pltpu.SemaphoreType.DMA((3 * n,)), pltpu.SemaphoreType.DMA((3 * n,))])(*gs)


def _scatter_send(name, ps):
    n = len(ps)

    def body(*refs):
        p, r, ssem, rsem, token = refs[:n], refs[n:2 * n], refs[2 * n], refs[2 * n + 1], refs[-1]
        x, y, c, chips = _place()
        for i in range(n):
            for k, (px, py) in enumerate(chips):
                _remote(p[i].at[2 * px + py], r[i].at[k], ssem.at[3 * i + k], rsem.at[3 * i + k], (px, py, c)).start()
        token[...] = jnp.zeros_like(token)

    lands = [lax.empty((N_CHIPS - 1,) + a.shape[1:], a.dtype) for a in ps]
    sem = pltpu.SemaphoreType.DMA((3 * n,))
    out = _split_copy_call(body, name=name, in_specs=[HBM] * (2 * n), out_specs=[SEM, SEM] + [HBM] * (2 * n) + [VMEM],
                           out_shape=[sem, sem] + [pltpu.HBM(a.shape, a.dtype) for a in list(ps) + lands] + [_sds((SUBLANES, LANES), F32)],
                           aliases={i: 2 + i for i in range(2 * n)})(*_hbm(list(ps) + lands))
    return out[0], out[1], out[2:2 + n], out[2 + n:2 + 2 * n], out[-1]


def _scatter_wait(name, ps, lands, send_sems, recv_sems, after):
    n = len(ps)

    def body(*refs):
        p, r, ssem, rsem = refs[:n], refs[n:2 * n], refs[2 * n], refs[2 * n + 1]
        x, y, c, chips = _place()
        for i in range(n):
            for k, (px, py) in enumerate(chips):
                cp = _remote(p[i].at[2 * px + py], r[i].at[k], ssem.at[3 * i + k], rsem.at[3 * i + k], (px, py, c))
                cp.wait_recv()
                cp.wait_send()

    out = _split_copy_call(body, name=name, in_specs=[HBM] * (2 * n) + [SEM, SEM] + [pl.BlockSpec(memory_space=pl.ANY)] * len(after),
                           out_specs=[HBM] * (2 * n), out_shape=[pltpu.HBM(a.shape, a.dtype) for a in list(ps) + list(lands)],
                           aliases={i: i for i in range(2 * n)})(*ps, *lands, send_sems, recv_sems, *after)
    return out[:n], out[n:]


def _exchange_send(name, gs):
    n = len(gs)

    def body(*refs):
        g, r, ssem, rsem, token = refs[:n], refs[n:2 * n], refs[2 * n], refs[2 * n + 1], refs[-1]
        x, y, c, _ = _place()
        for i in range(n):
            _remote(g[i].at[:, 1 - c], r[i], ssem.at[i], rsem.at[i], (x, y, 1 - c)).start()
        token[...] = jnp.zeros_like(token)

    lands = [lax.empty((a.shape[0],) + a.shape[2:], a.dtype) for a in gs]
    sem = pltpu.SemaphoreType.DMA((n,))
    out = _split_copy_call(body, name=name, in_specs=[HBM] * (2 * n), out_specs=[SEM, SEM] + [HBM] * (2 * n) + [VMEM],
                           out_shape=[sem, sem] + [pltpu.HBM(a.shape, a.dtype) for a in list(gs) + lands] + [_sds((SUBLANES, LANES), F32)],
                           aliases={i: 2 + i for i in range(2 * n)})(*_hbm(list(gs) + lands))
    return out[0], out[1], out[2:2 + n], out[2 + n:2 + 2 * n], out[-1]


def _exchange_wait(name, gs, lands, send_sems, recv_sems, after):
    n = len(gs)

    def body(*refs):
        g, r, ssem, rsem = refs[:n], refs[n:2 * n], refs[2 * n], refs[2 * n + 1]
        x, y, c, _ = _place()
        for i in range(n):
            cp = _remote(g[i].at[:, 1 - c], r[i], ssem.at[i], rsem.at[i], (x, y, 1 - c))
            cp.wait_recv()
            cp.wait_send()

    out = _split_copy_call(body, name=name, in_specs=[HBM] * (2 * n) + [SEM, SEM] + [pl.BlockSpec(memory_space=pl.ANY)] * len(after),
                           out_specs=[HBM] * (2 * n), out_shape=[pltpu.HBM(a.shape, a.dtype) for a in list(gs) + list(lands)],
                           aliases={i: i for i in range(2 * n)})(*gs, *lands, send_sems, recv_sems, *after)
    return out[:n], out[n:]


def _sibling_share(name, fs):
    n = len(fs)

    def body(*refs):
        f, send_sems, recv_sems = refs[n:2 * n], refs[-2], refs[-1]
        x, y, c, _ = _place()
        sends = [_remote(f[i].at[c], f[i].at[c], send_sems.at[i], recv_sems.at[i], (x, y, 1 - c)) for i in range(n)]
        for cp in sends:
            cp.start()
        for i in range(n):
            theirs = f[i].at[1 - c]
            _remote(theirs, theirs, send_sems.at[i], recv_sems.at[i], (x, y, 1 - c)).wait_recv()
        for cp in sends:
            cp.wait_send()

    return _call(body, name=name, in_specs=[HBM] * n, out_specs=[HBM] * n,
                 out_shape=[_sds(a.shape, a.dtype) for a in fs], aliases={i: i for i in range(n)},
                 scratch=[pltpu.SemaphoreType.DMA((n,)), pltpu.SemaphoreType.DMA((n,))])(*fs)


def _all_reduce_small(name, v):
    rows = v.shape[0] // 2
    halves = (2, rows, LANES)

    def body(v_ref, o_ref, from_sibling, chip_sums, send_sems, recv_sems):
        x, y, c, chips = _place()
        me, sibling = 2 * x + y, (x, y, 1 - c)
        swap = _remote(v_ref.at[1 - c], from_sibling, send_sems.at[0], recv_sems.at[0], sibling)
        swap.start()
        swap.wait()
        chip_sums[me] = v_ref[c] + from_sibling[...]
        sends = [_remote(chip_sums.at[me], chip_sums.at[me], send_sems.at[1 + k], recv_sems.at[1 + k], (px, py, c))
                 for k, (px, py) in enumerate(chips)]
        for cp in sends:
            cp.start()
        for k, (px, py) in enumerate(chips):
            theirs = chip_sums.at[2 * px + py]
            _remote(theirs, theirs, send_sems.at[1 + k], recv_sems.at[1 + k], (px, py, c)).wait_recv()
        for cp in sends:
            cp.wait_send()
        acc = chip_sums[0]
        for j in range(1, N_CHIPS):
            acc = acc + chip_sums[j]
        o_ref[c] = acc
        share = _remote(o_ref.at[c], o_ref.at[c], send_sems.at[4], recv_sems.at[4], sibling)
        share.start()
        share.wait_send()
        _remote(o_ref.at[1 - c], o_ref.at[1 - c], send_sems.at[4], recv_sems.at[4], sibling).wait_recv()

    return _call(body, name=name, in_specs=[VMEM], out_specs=VMEM, out_shape=_sds(halves, F32),
                 scratch=[pltpu.VMEM((rows, LANES), F32), pltpu.VMEM((N_CHIPS, rows, LANES), F32),
                          pltpu.SemaphoreType.DMA((5,)), pltpu.SemaphoreType.DMA((5,))])(v.reshape(halves)).reshape(v.shape)


def _add_halves(name, g, r, c):
    _, _, rows, C = g.shape
    tr = _row_tile(rows)

    def body(c_ref, g_ref, r_ref, o_ref):
        o_ref[...] = (g_ref[...] + r_ref[...]).astype(BF16)

    spec = BS((None, tr, C), lambda j, i, c_ref: (j, i, 0))
    return _prefetch_call(body, name=name, grid=(N_CHIPS, rows // tr),
                          in_specs=[BS((None, None, tr, C), lambda j, i, c_ref: (j, c_ref[0], i, 0)), spec], out_specs=spec,
                          out_shape=pltpu.HBM((N_CHIPS, rows, C), BF16))(c, g, r)


def _sum_partials(name, p, r, chip_c):
    _, rows, C = p.shape
    tr = _row_tile(rows)

    def body(s_ref, p_ref, r_ref, o_ref):
        acc = p_ref[...].astype(F32)
        for k in range(N_CHIPS - 1):
            acc = acc + r_ref[k].astype(F32)
        o_ref[...] = acc

    return _prefetch_call(body, name=name, grid=(rows // tr,),
                          in_specs=[BS((None, tr, C), lambda i, s: (s[0], i, 0)), BS((N_CHIPS - 1, tr, C), lambda i, s: (0, i, 0))],
                          out_specs=BS((None, tr, C), lambda i, s: (s[1], i, 0)), out_shape=pltpu.HBM((2, rows, C), F32))(chip_c, p, r)


_SHARDED = ("even_w_in", "even_w_out", "odd_w_in", "q_b", "kv_b", "odd_w_out", "ffn_w_gate", "ffn_w_up", "ffn_w_down")
_REPLICATED = ("mix_norm", "ffn_norm", "sg_ln_g", "sg_w_s", "sg_b_s", "pool_w", "q_norm", "k_norm")
_SMALL_SHARDED = ("sc_conv_w", "pool_scale", "q_a_norm", "kv_a_norm")
_WEIGHTS = ("mix_norm", "ffn_norm", "even_w_in", "sg_ln_g", "sg_w_s", "sg_b_s", "sc_conv_w", "even_w_out", "odd_w_in", "pool_w",
            "pool_scale", "q_a_norm", "q_b", "kv_a_norm", "kv_b", "q_norm", "k_norm", "odd_w_out", "ffn_w_gate", "ffn_w_up",
            "ffn_w_down")


def _pad_rows(flat, width, align):
    n = flat.shape[0]
    rows = -(-n // (width * align)) * align
    return jnp.pad(flat, (0, rows * width - n)).reshape(rows, width)


_GROUPS = {"even": ("even_w_in", "even_w_out"),
           "ffn0": ("ffn_w_gate0", "ffn_w_up0", "ffn_w_down0"),
           "odd": ("odd_w_in", "q_b", "kv_b", "odd_w_out"),
           "ffn1": ("ffn_w_gate1", "ffn_w_up1", "ffn_w_down1")}


def _place_shards(shards, names, chip, after):
    placed = []
    for n in names:
        weight, layer = (n[:-1], int(n[-1])) if n[-1].isdigit() else (n, 0)
        a = shards[weight]
        placed.append(_cast_place(f"place_{n}", a.reshape(a.shape[0], 2, a.shape[1] // 2, a.shape[2]), layer, chip, after))
    return placed


def _whole_weights(gathered):
    out = {n: a.reshape(N_CHIPS, -1, a.shape[-1]) for n, a in gathered.items()}
    for n in ("q_b", "kv_b"):
        if n in out:
            out[n] = out[n].transpose(1, 0, 2).reshape(out[n].shape[1], -1)
    for n in ("even_w_out", "odd_w_in", "odd_w_out"):
        if n in out:
            out[n] = out[n].reshape(-1, out[n].shape[-1])
    return out


def _forward_backward(x, positions, target, small, fetch, emit, advance):
    batch, seq, _ = x.shape
    T = batch * seq
    tm = _token_tile(seq)
    x0 = x.reshape(T, D_MODEL)

    inv_freq = ROPE_THETA ** (-jnp.arange(0, QK_ROPE, 2, dtype=F32) / QK_ROPE)
    ang = (positions.astype(F32)[..., None] * inv_freq).reshape(T, QK_ROPE // 2)
    cos, sin = jnp.cos(ang), jnp.sin(ang)
    pad = jnp.zeros((T, LANES - QK_ROPE), F32)
    cos_t = jnp.concatenate([cos, cos, pad], axis=1)
    sin_t = jnp.concatenate([-sin, sin, pad], axis=1)

    tril = jnp.tril(jnp.ones((SG_CHUNK, SG_CHUNK), bool))
    w_tril = jnp.where(tril[None], small["sg_w_s"][0], 0.0).astype(BF16)
    b_lanes = jnp.broadcast_to(small["sg_b_s"][0][:, :, None], (SG_HEADS, SG_CHUNK, SG_DIM))
    conv_w = jnp.pad(small["sc_conv_w"][0], ((0, SUBLANES - CONV_TAPS), (0, 0)))
    ln_g = small["sg_ln_g"]
    pool_diag = jnp.zeros((POOL_WIDTH, POOL_WIDTH), F32)
    for g in range(len(POOL_WINDOWS)):
        pool_diag = pool_diag.at[POOL_DIM * g:POOL_DIM * (g + 1), POOL_DIM * g:POOL_DIM * (g + 1)].set(small["pool_w"][0, g])
    pool_diag = pool_diag.astype(BF16)
    pool_scale = small["pool_scale"]
    q_g = jnp.pad(small["q_norm"], ((0, 0), (0, QK_PAD - QK_DIM)))
    k_g = jnp.pad(small["k_norm"], ((0, 0), (0, QK_PAD - QK_DIM)))
    qa_g, kva_g = small["q_a_norm"], small["kv_a_norm"]
    in_shard = EVEN_IN // N_CHIPS

    def ffn_weights(l, w):
        return w[f"ffn_w_gate{l}"], w[f"ffn_w_up{l}"], w[f"ffn_w_down{l}"]

    W = fetch("even", ())
    w_in_even = W["even_w_in"]
    h0 = _rmsnorm_fwd("mix0_norm", x0, small["mix_norm"][0], tm)
    tb = _big_tile(T)
    proj0 = _matmul("even_in", "nn", [(h0, w_in_even)],
                    [(BS((tb, D_MODEL), lambda i, j, k: (i, 0)), BS((None, D_MODEL, in_shard), lambda i, j, k: (j, 0, 0)))],
                    (T // tb, N_CHIPS, 1), _sds((T, EVEN_IN), F32), BS((tb, in_shard), lambda i, j, k: (i, j)), (tb, in_shard))
    mix0 = _even_mixer_fwd(proj0, ln_g, w_tril, b_lanes, conv_w, seq, tm)
    w_out_even = W["even_w_out"]
    x1, h1 = _mm("even_out", "nn", mix0, w_out_even, F32, tk=1024, add=x0, fused=_norm_tail(small["ffn_norm"][0], T, tb))
    ffn0 = ffn_weights(0, fetch("ffn0", (x1,)))
    (x2, h2), ffn0_saved = _ffn_fwd(0, x1, h1, *ffn0, _norm_tail(small["mix_norm"][1], T, tb))
    W = fetch("odd", (x2,))
    w_in_odd = jnp.pad(W["odd_w_in"], ((0, 0), (0, ODD_IN_PAD - ODD_IN)))
    q_b = jnp.pad(W["q_b"].reshape(Q_LORA, HEADS, QK_DIM).transpose(1, 0, 2), ((0, 0), (0, 0), (0, QK_PAD - QK_DIM)))
    kv_b = W["kv_b"].reshape(KV_LORA, HEADS, QK_NOPE + V_DIM).transpose(1, 0, 2)
    proj1 = _mm("odd_in", "nn", h2, w_in_odd, F32, tk=1024)
    mix1 = _pool_fwd(proj1, pool_diag, pool_scale, seq, tm)
    q, k, v = _mla_qkv_fwd(proj1, cos_t, sin_t, qa_g, kva_g, q_b, kv_b, q_g, k_g, tm)
    mix1, lse = _flash_fwd(q, k, v, mix1, batch, seq)
    x3, h3 = _mm("odd_out", "nn", mix1, W["odd_w_out"], F32, tk=1024, add=x2, fused=_norm_tail(small["ffn_norm"][1], T, tb))
    ffn1 = ffn_weights(1, fetch("ffn1", (x3,)))
    (dy, sq), ffn1_saved = _ffn_fwd(1, x3, h3, *ffn1, _loss_tail(target.reshape(T, D_MODEL), tb))

    G = {}
    dx3, dffn_g1 = _ffn_bwd(1, x3, small["ffn_norm"][1], *ffn1, ffn1_saved, dy, emit)
    dmix1 = _mm("odd_out_dx", "nt", dx3, W["odd_w_out"], BF16, tk=1024, after=advance(dx3))
    dw_out_odd = _mm("odd_out_dw", "tn", mix1, dx3, F32, hbm_out=True)
    dq, dk, dv = _flash_bwd(q, k, v, dmix1, mix1, lse, batch, seq)
    dz_pool, dpool_diag, G["pool_scale"] = _pool_bwd(proj1, dmix1, pool_diag, pool_scale, seq, tm)
    dproj1, dq_b, dkv_b, dq_g, dk_g, G["q_a_norm"], G["kv_a_norm"] = _mla_qkv_bwd(
        proj1, cos_t, sin_t, qa_g, kva_g, q_b, kv_b, q_g, k_g, dq, dk, dv, dz_pool, tm)
    G["pool_w"] = jnp.stack([dpool_diag[POOL_DIM * g:POOL_DIM * (g + 1), POOL_DIM * g:POOL_DIM * (g + 1)]
                             for g in range(len(POOL_WINDOWS))])[None]
    G["q_norm"], G["k_norm"] = dq_g[:, :QK_DIM], dk_g[:, :QK_DIM]
    dw_in_odd = _mm("odd_in_dw", "tn", h2, dproj1, F32, tn=ODD_IN, hbm_out=True)

    def shard_major(g, cols):
        return g.reshape(g.shape[0], N_CHIPS, cols).transpose(1, 0, 2)

    behind = emit("odd", {"odd_w_in": dw_in_odd.reshape(N_CHIPS, -1, ODD_IN),
                          "q_b": shard_major(dq_b[:, :, :QK_DIM].transpose(1, 0, 2).reshape(Q_LORA, HEADS * QK_DIM), HEADS * QK_DIM // N_CHIPS),
                          "kv_b": shard_major(dkv_b.transpose(1, 0, 2).reshape(KV_LORA, HEADS * (QK_NOPE + V_DIM)),
                                              HEADS * (QK_NOPE + V_DIM) // N_CHIPS),
                          "odd_w_out": dw_out_odd.reshape(N_CHIPS, -1, D_MODEL)})
    dx2, dmix_g1 = _mm("odd_in_dx", "nt", dproj1, W["odd_w_in"], F32, tk=ODD_IN, after=behind,
                       fused=_norm_bwd_tail(x2, small["mix_norm"][1], dx3, tb))
    dx1, dffn_g0 = _ffn_bwd(0, x1, small["ffn_norm"][0], *ffn0, ffn0_saved, dx2, emit, after=advance(dx2))
    dmix0 = _mm("even_out_dx", "nt", dx1, w_out_even, F32, tk=1024, after=advance(dx1))
    dw_out_even = _mm("even_out_dw", "tn", mix0, dx1, F32, hbm_out=True)
    dproj0, dw_s, db_lanes, G["sg_ln_g"], dconv = _even_mixer_bwd(proj0, dmix0, ln_g, w_tril, b_lanes, conv_w, seq, tm)
    G["sg_w_s"] = dw_s[None]
    G["sg_b_s"] = jnp.sum(db_lanes, axis=-1)[None]
    G["sc_conv_w"] = dconv[None, :CONV_TAPS]
    tail, shapes, specs = _norm_bwd_tail(x0, small["mix_norm"][0], dx1, tb)
    dx0, dmix_g0 = _matmul("even_in_dx", "nt", [(dproj0, w_in_even)],
                           [(BS((tb, in_shard), lambda i, j, k: (i, k)), BS((None, D_MODEL, in_shard), lambda i, j, k: (k, 0, 0)))],
                           (T // tb, 1, N_CHIPS), shapes, specs, (tb, D_MODEL), tail=tail)
    tk = min(512, T)
    dw_in_even = _grad_shards(
        "even_in_dw", h0, dproj0, BS((tk, D_MODEL), lambda k: (k, 0)), BS((tk, EVEN_IN), lambda k: (k, 0)),
        lambda a_ref, b_ref, j: (a_ref[...], b_ref[:, in_shard * j:in_shard * (j + 1)]), (N_CHIPS, D_MODEL, in_shard), T // tk)
    emit("even", {"even_w_in": dw_in_even, "even_w_out": dw_out_even.reshape(N_CHIPS, -1, D_MODEL)})
    G["mix_norm"] = jnp.concatenate([dmix_g0, dmix_g1], axis=0)
    G["ffn_norm"] = jnp.concatenate([dffn_g0, dffn_g1], axis=0)
    return sq[0, 0], dx0.reshape(batch, seq, D_MODEL), G


def _small_vector(parts, names):
    flat = jnp.concatenate([parts[n].astype(F32).reshape(-1) for n in names])
    return _pad_rows(flat, LANES, 2 * SUBLANES)


def _split_small(vec, like, names):
    out, off, flat = {}, 0, vec.reshape(-1)
    for n in names:
        size = math.prod(like[n].shape)
        out[n] = flat[off:off + size].reshape(like[n].shape)
        off += size
    return out


def _whole_shape(a):
    return a.shape[:-1] + (a.shape[-1] * N_CHIPS,)


def kernel(x, positions, mix_norm, ffn_norm, even_w_in, sg_ln_g, sg_w_s, sg_b_s, sc_conv_w, even_w_out, odd_w_in, pool_w, pool_scale, q_a_norm, q_b, kv_a_norm, kv_b, q_norm, k_norm, odd_w_out, ffn_w_gate, ffn_w_up, ffn_w_down, loss_target, m_mix_norm, m_ffn_norm, m_even_w_in, m_sg_ln_g, m_sg_w_s, m_sg_b_s, m_sc_conv_w, m_even_w_out, m_odd_w_in, m_pool_w, m_pool_scale, m_q_a_norm, m_q_b, m_kv_a_norm, m_kv_b, m_q_norm, m_k_norm, m_odd_w_out, m_ffn_w_gate, m_ffn_w_up, m_ffn_w_down, v_mix_norm, v_ffn_norm, v_even_w_in, v_sg_ln_g, v_sg_w_s, v_sg_b_s, v_sc_conv_w, v_even_w_out, v_odd_w_in, v_pool_w, v_pool_scale, v_q_a_norm, v_q_b, v_kv_a_norm, v_kv_b, v_q_norm, v_k_norm, v_odd_w_out, v_ffn_w_gate, v_ffn_w_up, v_ffn_w_down):
    args = dict(locals())
    w = {n: args[n] for n in _WEIGHTS}
    m = {n: args["m_" + n] for n in _WEIGHTS}
    v = {n: args["v_" + n] for n in _WEIGHTS}
    cx, cy, cc = lax.axis_index("x"), lax.axis_index("y"), lax.axis_index("c")
    chip = 2 * cx + cy
    transposed = ("ffn_w_gate", "ffn_w_up")
    for n in transposed:
        w[n], m[n], v[n] = (jnp.swapaxes(t[n], 1, 2) for t in (w, m, v))

    chip_arr = chip.astype(jnp.int32).reshape(1)
    c_arr = cc.astype(jnp.int32).reshape(1)
    group_names = list(_GROUPS)
    placed = {}
    for n in _SMALL_SHARDED:
        a = w[n]
        whole = jnp.zeros(a.shape[:-1] + (N_CHIPS, a.shape[-1]), F32)
        whole = lax.dynamic_update_slice_in_dim(whole, a[..., None, :], chip, axis=a.ndim - 1)
        placed[n] = jnp.where(cc == 0, whole, 0.0).reshape(_whole_shape(a))
    small_whole = _all_reduce_small("gather_small_weights", _small_vector(placed, _SMALL_SHARDED))
    small = dict({n: w[n] for n in _REPLICATED}, **_split_small(small_whole, placed, _SMALL_SHARDED))

    first, rest = list(_GROUPS[group_names[0]]), [n for g in group_names[1:] for n in _GROUPS[g]]
    sems_first, flight_first, token = _gather_send("gather_send_first", _place_shards(w, first, chip_arr, (small_whole,)),
                                                   [list(range(len(first)))], (small_whole,))
    sems_rest, flight_rest, all_sent = _gather_send("gather_send_rest", _place_shards(w, rest, chip_arr, (token,)),
                                                    [[rest.index(n) for n in _GROUPS[g]] for g in group_names[1:]], ())
    sems = list(sems_first) + list(sems_rest)
    in_flight = dict(zip(first + rest, list(flight_first) + list(flight_rest)))

    def fetch(group, after):
        gi, members = group_names.index(group), _GROUPS[group]
        after = after if gi else (all_sent,)
        landed = _gather_wait(f"gather_wait_{group}", [in_flight[n] for n in members], sems[2 * gi], sems[2 * gi + 1], after)
        return _whole_weights(dict(zip(members, _gather_pass(f"gather_pass_{group}", landed))))

    swapping, pending, arrived = [], [], {}

    def settle(after):
        names, ps, lands, send_sems, recv_sems = pending.pop()
        ps, lands = _scatter_wait(f"scatter_wait_{names[0]}", ps, lands, send_sems, recv_sems, after)
        arrived.update({n: (p, r) for n, p, r in zip(names, ps, lands)})

    def emit(group, grads):
        names = _GROUPS[group]
        halves = [grads[n].reshape(N_CHIPS, 2, grads[n].shape[1] // 2, grads[n].shape[2]) for n in names]
        send_sems, recv_sems, halves, lands, token = _exchange_send(f"exchange_send_{group}", halves)
        swapping.append((group, halves, lands, send_sems, recv_sems))
        return (token,)

    def advance(done):
        done = done if isinstance(done, tuple) else (done,)
        group, halves, lands, send_sems, recv_sems = swapping.pop()
        names = _GROUPS[group]
        halves, lands = _exchange_wait(f"exchange_wait_{group}", halves, lands, send_sems, recv_sems, done)
        partial = [_add_halves(f"add_{n}", g, r, c_arr) for n, g, r in zip(names, halves, lands)]
        if pending:
            settle(done)
        send_sems, recv_sems, ps, lands, token = _scatter_send(f"scatter_send_{group}", partial)
        pending.append((names, ps, lands, send_sems, recv_sems))
        return (token,)

    sq, grad_x, G = _forward_backward(x, positions, loss_target, small, fetch, emit, advance)
    loss = lax.psum(0.5 * sq / D_MODEL, ("x", "y", "c"))

    small_names = _REPLICATED + _SMALL_SHARDED
    summed = _split_small(_all_reduce_small("reduce_small_grads", _small_vector(G, small_names)), G, small_names)
    grads = {n: summed[n] for n in _REPLICATED}
    for n in _SMALL_SHARDED:
        a = w[n]
        grads[n] = lax.dynamic_slice_in_dim(summed[n].reshape(a.shape[:-1] + (N_CHIPS, a.shape[-1])), chip, 1,
                                            axis=a.ndim - 1).reshape(a.shape)

    chip_c = jnp.stack([chip, cc]).astype(jnp.int32)
    out = {}

    def finish(group):
        names, tokens = _GROUPS[group], []
        sums = [_sum_partials(f"sum_{n}", *arrived[n], chip_c) for n in names]
        for n, f in zip(names, _sibling_share(f"grad_share_{group}", sums)):
            weight, layer = (n[:-1], int(n[-1])) if n[-1].isdigit() else (n, 0)
            *out[weight], token = _adamw(f"adamw_{weight}", w[weight], f.reshape(-1, f.shape[-1]), m[weight], v[weight], layer,
                                         out.get(weight, ()))
            tokens.append(token)
        return tuple(tokens)

    advance(finish(group_names[3]) + finish(group_names[2]))
    settle(finish(group_names[1]))
    finish(group_names[0])
    packed = [_small_vector(d, small_names) for d in (w, grads, m, v)]
    res = _adamw("adamw_small", packed[0][None], packed[1], packed[2][None], packed[3][None])
    delta_s, m_s, v_s = (_split_small(r, w, small_names) for r in res[1:4])
    for n in small_names:
        out[n] = (grads[n], delta_s[n], m_s[n], v_s[n])
    for n in transposed:
        out[n] = tuple(jnp.swapaxes(t, 1, 2) for t in out[n])

    return (loss, grad_x, *[out[n][0] for n in _WEIGHTS], *[out[n][1] for n in _WEIGHTS],
            *[out[n][2] for n in _WEIGHTS], *[out[n][3] for n in _WEIGHTS])
```

```python
import functools
import math

import jax
import jax.numpy as jnp
from jax import lax
from jax.experimental import pallas as pl
from jax.experimental.pallas import tpu as pltpu

F32, BF16 = jnp.float32, jnp.bfloat16
BS = pl.BlockSpec

D_MODEL = 1024
EPS = 1e-6
NEG_INF = -1e30
SG_HEADS, SG_DIM, SG_WIDTH, SG_CHUNK = 4, 128, 512, 128
SC_WIDTH, CONV_TAPS = 512, 3
EVEN_IN = 2 * SG_WIDTH + 3 * SC_WIDTH
POOL_WINDOWS = (2, 4, 8, 16)
POOL_DIM, POOL_WIDTH = 64, 256
POOL_HALO = 16
HEADS, Q_LORA, KV_LORA, QK_NOPE, QK_ROPE, V_DIM = 6, 384, 256, 128, 64, 128
QK_DIM = QK_NOPE + QK_ROPE
QK_PAD = 256
ODD_IN = POOL_WIDTH + Q_LORA + KV_LORA + QK_ROPE
ODD_IN_PAD = 1024
ROPE_THETA = 10000.0
ATTN_SCALE = QK_DIM ** -0.5
D_FF, N_CHIPS = 2816, 4
FF_SHARD = D_FF // N_CHIPS
ADAM_LR, ADAM_B1, ADAM_B2, ADAM_EPS, ADAM_WD, ADAM_STEP = 0.001, 0.9, 0.999, 1e-08, 0.01, 10
VMEM_LIMIT_V7X = 48 * 2**20
LANES, SUBLANES = 128, 8
MESH = pl.DeviceIdType.MESH
HBM = pl.BlockSpec(memory_space=pltpu.HBM)
VMEM = pl.BlockSpec(memory_space=pltpu.VMEM)

_DIMS = {"nn": (((1,), (0,)), ((), ())), "nt": (((1,), (1,)), ((), ())), "tn": (((0,), (0,)), ((), ()))}


def _dot(a, b, mode="nn"):
    return lax.dot_general(a.astype(BF16), b.astype(BF16), _DIMS[mode], preferred_element_type=F32)


def _call(body, *, name, out_shape, in_specs, out_specs, grid=(), scratch=(), aliases=None, after=()):
    params = pltpu.CompilerParams(vmem_limit_bytes=VMEM_LIMIT_V7X,
                                  **({"dimension_semantics": ("arbitrary",) * len(grid)} if grid else {}))
    n_in, n_after = len(in_specs), len(after)
    kernel_body = body if not after else (lambda *refs: body(*refs[:n_in], *refs[n_in + n_after:]))
    call = pl.pallas_call(kernel_body, name=name, grid=grid, in_specs=list(in_specs) + [pl.BlockSpec(memory_space=pl.ANY)] * n_after,
                          out_specs=out_specs, out_shape=out_shape, scratch_shapes=list(scratch),
                          input_output_aliases=aliases or {}, compiler_params=params)
    return (lambda *ops: call(*ops, *after)) if after else call


def _sds(shape, dtype):
    return jax.ShapeDtypeStruct(tuple(shape), dtype)


def _token_tile(seq):
    return 512 if seq % 512 == 0 else seq


_TAIL_ROWS = 256


def _matmul(name, mode, pairs, pair_specs, grid, out_shape, out_spec, acc_shape, add=None, add_spec=None, after=(), tail=None):
    n, nk = len(pairs), grid[-1]
    n_add = int(add is not None)
    n_tail = len(tail[0]) if tail else 0
    n_in = 2 * n + n_add + n_tail
    n_out = len(out_shape) if tail else 1

    def body(*refs):
        ab = refs[:2 * n]
        add_ref = refs[2 * n] if n_add else None
        tail_refs, outs = refs[2 * n + n_add:n_in], refs[n_in:n_in + n_out]
        first = pl.program_id(0) == 0

        def finish(result):
            if tail is None:
                r = result(slice(None))
                outs[0][...] = (r if add_ref is None else r + add_ref[...]).astype(outs[0].dtype)
                return
            for lo in range(0, acc_shape[0], _TAIL_ROWS):
                rows = slice(lo, min(lo + _TAIL_ROWS, acc_shape[0]))
                r = result(rows)
                tail[2](rows, r if add_ref is None else r + add_ref[rows, :], first, tail_refs, outs)

        def terms(a_ref, b_ref):
            if len(a_ref.shape) == 2 and len(b_ref.shape) == 2:
                return [(a_ref[...], b_ref[...])]
            cols = a_ref.shape[-1] // N_CHIPS
            return [(a_ref[j] if len(a_ref.shape) == 3 else a_ref[:, cols * j:cols * (j + 1)], b_ref[j]) for j in range(N_CHIPS)]

        if nk == 1:
            r = None
            for p in range(n):
                for a_blk, b_blk in terms(ab[2 * p], ab[2 * p + 1]):
                    d = _dot(a_blk, b_blk, mode)
                    r = d if r is None else r + d
            finish(lambda rows: r[rows])
            return
        acc = refs[-1]
        k = pl.program_id(len(grid) - 1)

        @pl.when(k == 0)
        def _():
            acc[...] = jnp.zeros_like(acc)

        for p in range(n):
            acc[...] += _dot(ab[2 * p][...], ab[2 * p + 1][...], mode)

        @pl.when(k == nk - 1)
        def _():
            finish(lambda rows: acc[rows, :])

    ops = [t for pr in pairs for t in pr] + ([add] if n_add else []) + (list(tail[0]) if tail else [])
    specs = [s for pr in pair_specs for s in pr] + ([add_spec] if n_add else []) + (list(tail[1]) if tail else [])
    return _call(body, name=name, grid=grid, in_specs=specs, out_specs=out_spec, out_shape=out_shape,
                 scratch=[pltpu.VMEM(acc_shape, F32)] if nk > 1 else [], after=after)(*ops)


def _row_spec(tm, d):
    return BS((tm, d), lambda i, j, k: (i, 0))


def _vec_spec(d):
    return BS((1, d), lambda i, j, k: (0, 0))


def _norm_tail(gain, T, tm):
    d = gain.shape[-1]

    def fn(rows, r, first, tail_refs, outs):
        outs[0][rows, :] = r
        outs[1][rows, :] = (r * lax.rsqrt(jnp.mean(r * r, axis=-1, keepdims=True) + EPS) * tail_refs[0][...]).astype(BF16)

    return ([gain.reshape(1, d)], [_vec_spec(d)], fn), [_sds((T, d), F32), _sds((T, d), BF16)], [_row_spec(tm, d), _row_spec(tm, d)]


def _norm_bwd_tail(x, gain, dres, tm):
    T, d = x.shape

    def fn(rows, r, first, tail_refs, outs):
        x_ref, g_ref, dres_ref = tail_refs
        xv = x_ref[rows, :]
        rstd = lax.rsqrt(jnp.mean(xv * xv, axis=-1, keepdims=True) + EPS)
        xhat = xv * rstd
        if rows.start == 0:
            @pl.when(first)
            def _():
                outs[1][...] = jnp.zeros_like(outs[1])

        outs[1][...] += jnp.sum(r * xhat, axis=0, keepdims=True)
        dxhat = r * g_ref[...]
        outs[0][rows, :] = dres_ref[rows, :] + rstd * (dxhat - xhat * jnp.mean(dxhat * xhat, axis=-1, keepdims=True))

    return (([x, gain.reshape(1, d), dres], [_row_spec(tm, d), _vec_spec(d), _row_spec(tm, d)], fn),
            [_sds((T, d), F32), _sds((1, d), F32)], [_row_spec(tm, d), _vec_spec(d)])


def _loss_tail(target, tm):
    T, d = target.shape

    def fn(rows, r, first, tail_refs, outs):
        e = r - tail_refs[0][rows, :]
        if rows.start == 0:
            @pl.when(first)
            def _():
                outs[1][...] = jnp.zeros_like(outs[1])

        outs[1][...] += jnp.sum(e * e)
        outs[0][rows, :] = e * (1.0 / d)

    return (([target], [_row_spec(tm, d)], fn), [_sds((T, d), F32), _sds((SUBLANES, LANES), F32)],
            [_row_spec(tm, d), BS((SUBLANES, LANES), lambda i, j, k: (0, 0))])


def _grad_shards(name, a, b, a_spec, b_spec, pick, out_shape, n_steps):
    def body(a_ref, b_ref, o_ref):
        @pl.when(pl.program_id(0) == 0)
        def _():
            o_ref[...] = jnp.zeros_like(o_ref)

        for j in range(N_CHIPS):
            aj, bj = pick(a_ref, b_ref, j)
            o_ref[j] += _dot(aj, bj, "tn")

    return _call(body, name=name, grid=(n_steps,), in_specs=[a_spec, b_spec],
                 out_specs=BS(out_shape, lambda k: (0, 0, 0)), out_shape=pltpu.HBM(tuple(out_shape), F32))(a, b)


def _mm(name, mode, a, b, out_dtype, tm=1024, tn=1024, tk=512, add=None, after=(), fused=None, hbm_out=False):
    if mode == "tn":
        (K, M), N = a.shape, b.shape[1]
    else:
        (M, K), N = a.shape, (b.shape[1] if mode == "nn" else b.shape[0])
    tm, tn, tk = min(tm, M), min(tn, N), min(tk, K)
    a_spec = BS((tk, tm), lambda i, j, k: (k, i)) if mode == "tn" else BS((tm, tk), lambda i, j, k: (i, k))
    b_spec = BS((tn, tk), lambda i, j, k: (j, k)) if mode == "nt" else BS((tk, tn), lambda i, j, k: (k, j))
    o_spec = BS((tm, tn), lambda i, j, k: (i, j))
    tail, shapes, specs = fused if fused else (None, pltpu.HBM((M, N), out_dtype) if hbm_out else _sds((M, N), out_dtype), o_spec)
    return _matmul(name, mode, [(a, b)], [(a_spec, b_spec)], (M // tm, N // tn, K // tk), shapes, specs, (tm, tn),
                   add=add, add_spec=o_spec if add is not None else None, after=after, tail=tail)


def _rmsnorm_fwd(name, x, g, tm):
    T, d = x.shape

    def body(x_ref, g_ref, o_ref):
        xv = x_ref[...]
        y = xv * lax.rsqrt(jnp.mean(xv * xv, axis=-1, keepdims=True) + EPS)
        o_ref[...] = (y * g_ref[...]).astype(o_ref.dtype)

    return _call(body, name=name, grid=(T // tm,), in_specs=[BS((tm, d), lambda i: (i, 0)), BS((1, d), lambda i: (0, 0))],
                 out_specs=BS((tm, d), lambda i: (i, 0)), out_shape=_sds((T, d), BF16))(x, g.reshape(1, d))


_PASS_ROWS = 256


def _ffn_up(name, h, wg, wu, tm):
    T = h.shape[0]

    def body(h_ref, wg_ref, wu_ref, g_ref, u_ref, a_ref):
        hv = h_ref[...]
        g = _dot(hv, wg_ref[...], "nt")
        u = _dot(hv, wu_ref[...], "nt")
        g_ref[...] = g.astype(BF16)
        u_ref[...] = u.astype(BF16)
        a_ref[...] = (g * (1.0 / (1.0 + jnp.exp(-g))) * u).astype(BF16)

    w_spec = BS((None, FF_SHARD, D_MODEL), lambda j, i: (j, 0, 0))
    o_spec = BS((None, tm, FF_SHARD), lambda j, i: (j, i, 0))
    sh = _sds((N_CHIPS, T, FF_SHARD), BF16)
    return _call(body, name=name, grid=(N_CHIPS, T // tm), in_specs=[BS((tm, D_MODEL), lambda j, i: (i, 0)), w_spec, w_spec],
                 out_specs=[o_spec, o_spec, o_spec], out_shape=[sh, sh, sh])(h, wg, wu)


def _ffn_act_bwd(name, dxo, wd, g, u, tm, after=()):
    T = dxo.shape[0]

    def body(dx_ref, wd_ref, g_ref, u_ref, dg_ref, du_ref):
        da = _dot(dx_ref[...], wd_ref[...], "nt")
        g = g_ref[...].astype(F32)
        sig = 1.0 / (1.0 + jnp.exp(-g))
        dg_ref[...] = (da * u_ref[...].astype(F32) * (sig * (1.0 + g * (1.0 - sig)))).astype(BF16)
        du_ref[...] = (da * (g * sig)).astype(BF16)

    t_spec = BS((None, tm, FF_SHARD), lambda i, j: (j, i, 0))
    sh = _sds((N_CHIPS, T, FF_SHARD), BF16)
    return _call(body, name=name, grid=(T // tm, N_CHIPS),
                 in_specs=[BS((tm, D_MODEL), lambda i, j: (i, 0)), BS((None, FF_SHARD, D_MODEL), lambda i, j: (j, 0, 0)), t_spec, t_spec],
                 out_specs=[t_spec, t_spec], out_shape=[sh, sh], after=after)(dxo, wd, g, u)


def _big_tile(n):
    return min(1024, n)


def _resident_tile(n):
    return min(512, n)


def _resident(shape):
    return BS(shape, lambda i, j, k: (0,) * len(shape), pipeline_mode=pl.Buffered(1))


def _ffn_fwd(l, x, h, wg, wu, wd, fused):
    T = x.shape[0]
    g, u, a = _ffn_up(f"ffn{l}_up", h, wg, wu, _big_tile(T))
    tm = _resident_tile(T)
    tail, shapes, specs = fused(tm)
    outs = _matmul(f"ffn{l}_down", "nn", [(a, wd)],
                   [(BS((N_CHIPS, tm, FF_SHARD), lambda i, j, k: (0, i, 0)), _resident((N_CHIPS, FF_SHARD, D_MODEL)))],
                   (T // tm, 1, 1), shapes, specs, (tm, D_MODEL), add=x, add_spec=_row_spec(tm, D_MODEL), tail=tail)
    return outs, (h, g, u, a)


def _ffn_bwd(l, x, gain, wg, wu, wd, saved, dxo, emit, after=()):
    h, g, u, a = saved
    T = x.shape[0]
    tm = _big_tile(T)
    dg, du = _ffn_act_bwd(f"ffn{l}_act_bwd", dxo, wd, g, u, tm, after=after)
    tk = min(512, T)
    tn = D_MODEL
    shards_spec = BS((N_CHIPS, tk, FF_SHARD), lambda k: (0, k, 0))
    rows_spec = BS((tk, D_MODEL), lambda k: (k, 0))

    def dw(nm, act, rows):
        return _grad_shards(nm, act, rows, shards_spec, rows_spec, lambda a_ref, b_ref, j: (a_ref[j], b_ref[...]),
                            (N_CHIPS, FF_SHARD, D_MODEL), T // tk)

    behind = emit(f"ffn{l}", {f"ffn_w_gate{l}": dw(f"ffn{l}_dwg", dg, h), f"ffn_w_up{l}": dw(f"ffn{l}_dwu", du, h),
                              f"ffn_w_down{l}": dw(f"ffn{l}_dwd", a, dxo)})
    tm = _resident_tile(T)
    act_spec = BS((N_CHIPS, tm, FF_SHARD), lambda i, j, k: (0, i, 0))
    w_spec = _resident((N_CHIPS, FF_SHARD, D_MODEL))
    tail, shapes, specs = _norm_bwd_tail(x, gain, dxo, tm)
    return _matmul(f"ffn{l}_dh", "nn", [(dg, wg), (du, wu)], [(act_spec, w_spec), (act_spec, w_spec)],
                   (T // tm, 1, 1), shapes, specs, (tm, D_MODEL), after=behind, tail=tail)


_INV_SQRT2 = 1.0 / math.sqrt(2.0)
_INV_SQRT_2PI = 1.0 / math.sqrt(2.0 * math.pi)


def _gelu(x):
    return 0.5 * x * (1.0 + lax.erf(x * _INV_SQRT2))


def _gelu_and_grad(x):
    cdf = 0.5 * (1.0 + lax.erf(x * _INV_SQRT2))
    return x * cdf, cdf + x * jnp.exp(-0.5 * x * x) * _INV_SQRT_2PI


def _shift_down(x, k):
    return pltpu.roll(x, k, 0)


def _shift_up(x, k):
    return pltpu.roll(x, x.shape[0] - k, 0)


def _layer_norm_head(xh):
    xc = xh - jnp.mean(xh, axis=-1, keepdims=True)
    rstd = lax.rsqrt(jnp.mean(xc * xc, axis=-1, keepdims=True) + EPS)
    return xc * rstd, rstd


def _even_halo_specs(tm, n_tiles, col_blocks, after):
    rows = tm // SUBLANES
    last = n_tiles * rows - 1
    if after:
        return [BS((SUBLANES, 512), functools.partial(lambda cb, i: (jnp.minimum((i + 1) * rows, last), cb), cb)) for cb in col_blocks]
    return [BS((SUBLANES, 512), functools.partial(lambda cb, i: (jnp.maximum(i * rows - 1, 0), cb), cb)) for cb in col_blocks]


def _even_mixer_fwd(proj, ln_g, w_tril, b_lanes, conv_w, seq, tm):
    T = proj.shape[0]
    tiles_per_seq = seq // tm

    def body(p_ref, hc_ref, hh_ref, lng_ref, w_ref, bb_ref, cw_ref, o_ref):
        first = pl.program_id(0) % tiles_per_seq == 0
        for h in range(SG_HEADS):
            cols = slice(SG_DIM * h, SG_DIM * (h + 1))
            vhat, _ = _layer_norm_head(_gelu(p_ref[:, SG_WIDTH + SG_DIM * h:SG_WIDTH + SG_DIM * (h + 1)]))
            vln = (vhat * lng_ref[:, cols]).astype(BF16)
            for k in range(tm // SG_CHUNK):
                rows = slice(SG_CHUNK * k, SG_CHUNK * (k + 1))
                mixed = _dot(w_ref[h], vln[rows]) + bb_ref[h]
                o_ref[rows, cols] = (_gelu(p_ref[rows, cols]) * mixed).astype(BF16)
        z = p_ref[:, 1536:2048] * p_ref[:, 2048:2560]
        zz = jnp.concatenate([jnp.where(first, 0.0, hc_ref[...] * hh_ref[...]), z], axis=0)
        y = cw_ref[0:1, :] * _shift_down(zz, 2)[SUBLANES:] + cw_ref[1:2, :] * _shift_down(zz, 1)[SUBLANES:] + cw_ref[2:3, :] * z
        o_ref[:, SG_WIDTH:] = (p_ref[:, 1024:1536] * y).astype(BF16)

    full = lambda shape: BS(shape, lambda i: (0,) * len(shape))
    return _call(body, name="even_mixer_fwd", grid=(T // tm,),
                 in_specs=[BS((tm, EVEN_IN), lambda i: (i, 0))] + _even_halo_specs(tm, T // tm, (3, 4), after=False)
                 + [full((1, SG_WIDTH)), full((SG_HEADS, SG_CHUNK, SG_CHUNK)), full((SG_HEADS, SG_CHUNK, SG_DIM)), full((SUBLANES, SC_WIDTH))],
                 out_specs=BS((tm, D_MODEL), lambda i: (i, 0)), out_shape=_sds((T, D_MODEL), BF16))(
        proj, proj, proj, ln_g, w_tril, b_lanes, conv_w)


def _even_mixer_bwd(proj, dmix, ln_g, w_tril, b_lanes, conv_w, seq, tm):
    T = proj.shape[0]
    n_tiles, tiles_per_seq = T // tm, seq // tm

    def body(p_ref, dm_ref, hc_ref, hh_ref, nd_ref, nb_ref, lng_ref, w_ref, bb_ref, cw_ref,
             dp_ref, dw_ref, db_ref, dlng_ref, dcw_ref):
        i = pl.program_id(0)
        first = i % tiles_per_seq == 0
        last = i % tiles_per_seq == tiles_per_seq - 1

        @pl.when(i == 0)
        def _():
            dw_ref[...] = jnp.zeros_like(dw_ref)
            db_ref[...] = jnp.zeros_like(db_ref)
            dlng_ref[...] = jnp.zeros_like(dlng_ref)
            dcw_ref[...] = jnp.zeros_like(dcw_ref)

        for h in range(SG_HEADS):
            cols = slice(SG_DIM * h, SG_DIM * (h + 1))
            vcols = slice(SG_WIDTH + SG_DIM * h, SG_WIDTH + SG_DIM * (h + 1))
            lng = lng_ref[:, cols]
            for k in range(tm // SG_CHUNK):
                rows = slice(SG_CHUNK * k, SG_CHUNK * (k + 1))
                gelu_v, dgelu_v = _gelu_and_grad(p_ref[rows, vcols])
                vhat, rstd = _layer_norm_head(gelu_v)
                vln = (vhat * lng).astype(BF16)
                mixed = _dot(w_ref[h], vln) + bb_ref[h]
                gelu_u, dgelu_u = _gelu_and_grad(p_ref[rows, cols])
                da = dm_ref[rows, cols]
                dp_ref[rows, cols] = (da * mixed * dgelu_u).astype(BF16)
                dmixed = da * gelu_u
                db_ref[h] += dmixed
                dw_ref[h] += _dot(dmixed, vln, "nt")
                dvln = _dot(w_ref[h], dmixed, "tn")
                dlng_ref[:, cols] += jnp.sum(dvln * vhat, axis=0, keepdims=True)
                dvhat = dvln * lng
                dgv = rstd * (dvhat - jnp.mean(dvhat, axis=-1, keepdims=True)
                              - vhat * jnp.mean(dvhat * vhat, axis=-1, keepdims=True))
                dp_ref[rows, vcols] = (dgv * dgelu_v).astype(BF16)

        b = p_ref[:, 1024:1536]
        c = p_ref[:, 1536:2048]
        hv = p_ref[:, 2048:2560]
        z = c * hv
        zz = jnp.concatenate([jnp.where(first, 0.0, hc_ref[...] * hh_ref[...]), z], axis=0)
        z1 = _shift_down(zz, 1)[SUBLANES:]
        z2 = _shift_down(zz, 2)[SUBLANES:]
        w0, w1, w2 = cw_ref[0:1, :], cw_ref[1:2, :], cw_ref[2:3, :]
        dbo = dm_ref[:, SG_WIDTH:]
        dy = dbo * b
        dd = jnp.concatenate([dy, jnp.where(last, 0.0, nd_ref[...] * nb_ref[...])], axis=0)
        dz = w2 * dy + w1 * _shift_up(dd, 1)[:tm] + w0 * _shift_up(dd, 2)[:tm]
        dp_ref[:, 1024:1536] = (dbo * (w0 * z2 + w1 * z1 + w2 * z)).astype(BF16)
        dp_ref[:, 1536:2048] = (dz * hv).astype(BF16)
        dp_ref[:, 2048:2560] = (dz * c).astype(BF16)
        dcw_ref[0:1, :] += jnp.sum(dy * z2, axis=0, keepdims=True)
        dcw_ref[1:2, :] += jnp.sum(dy * z1, axis=0, keepdims=True)
        dcw_ref[2:3, :] += jnp.sum(dy * z, axis=0, keepdims=True)

        @pl.when(i == n_tiles - 1)
        def _():
            t_idx = lax.broadcasted_iota(jnp.int32, (SG_CHUNK, SG_CHUNK), 0)
            s_idx = lax.broadcasted_iota(jnp.int32, (SG_CHUNK, SG_CHUNK), 1)
            for h in range(SG_HEADS):
                dw_ref[h] = jnp.where(t_idx >= s_idx, dw_ref[h], 0.0)

    full = lambda shape: BS(shape, lambda i: (0,) * len(shape))
    sq = (SG_HEADS, SG_CHUNK, SG_CHUNK)
    return _call(body, name="even_mixer_bwd", grid=(n_tiles,),
                 in_specs=[BS((tm, EVEN_IN), lambda i: (i, 0)), BS((tm, D_MODEL), lambda i: (i, 0))]
                 + _even_halo_specs(tm, n_tiles, (3, 4), after=False)
                 + _even_halo_specs(tm, n_tiles, (1,), after=True) + _even_halo_specs(tm, n_tiles, (2,), after=True)
                 + [full((1, SG_WIDTH)), full(sq), full(sq), full((SUBLANES, SC_WIDTH))],
                 out_specs=[BS((tm, EVEN_IN), lambda i: (i, 0)), full(sq), full(sq), full((1, SG_WIDTH)), full((SUBLANES, SC_WIDTH))],
                 out_shape=[_sds((T, EVEN_IN), BF16), _sds(sq, F32), _sds(sq, F32), _sds((1, SG_WIDTH), F32), _sds((SUBLANES, SC_WIDTH), F32)])(
        proj, dmix, proj, proj, dmix, proj, ln_g, w_tril, b_lanes, conv_w)


def _pool_select(vals):
    lane = lax.broadcasted_iota(jnp.int32, vals[0].shape, 1)
    out = vals[-1]
    for g in range(len(vals) - 2, -1, -1):
        out = jnp.where(lane < POOL_DIM * (g + 1), vals[g], out)
    return out


def _pool_counts(pos1):
    lane = lax.broadcasted_iota(jnp.int32, (pos1.shape[0], POOL_WIDTH), 1)
    win = _pool_select([jnp.full(lane.shape, float(w), F32) for w in POOL_WINDOWS])
    return jnp.minimum(pos1, win)


def _pool_means(zz, counts):
    s2 = zz + _shift_down(zz, 1)
    s4 = s2 + _shift_down(s2, 2)
    s8 = s4 + _shift_down(s4, 4)
    s16 = s8 + _shift_down(s8, 8)
    return _pool_select([s2, s4, s8, s16])[POOL_HALO:] / counts


def _pool_halo_spec(tm, n_tiles, after):
    rows = tm // POOL_HALO
    if after:
        return BS((POOL_HALO, POOL_WIDTH), lambda i: (jnp.minimum((i + 1) * rows, n_tiles * rows - 1), 0))
    return BS((POOL_HALO, POOL_WIDTH), lambda i: (jnp.maximum(i * rows - 1, 0), 0))


def _pool_fwd(proj, w_diag, scale, seq, tm):
    T = proj.shape[0]
    tiles_per_seq = seq // tm

    def body(z_ref, zh_ref, w_ref, s_ref, o_ref):
        t = pl.program_id(0) % tiles_per_seq
        z = z_ref[...]
        zz = jnp.concatenate([jnp.where(t == 0, 0.0, zh_ref[...]), z], axis=0)
        pos1 = (lax.broadcasted_iota(jnp.int32, (tm, 1), 0) + (t * tm + 1)).astype(F32)
        pooled = _pool_means(zz, _pool_counts(pos1)) - z
        o_ref[...] = (_dot(pooled, w_ref[...]) * s_ref[...]).astype(BF16)

    full = lambda shape: BS(shape, lambda i: (0,) * len(shape))
    return _call(body, name="pool_fwd", grid=(T // tm,),
                 in_specs=[BS((tm, POOL_WIDTH), lambda i: (i, 0)), _pool_halo_spec(tm, T // tm, False),
                           full((POOL_WIDTH, POOL_WIDTH)), full((1, POOL_WIDTH))],
                 out_specs=BS((tm, POOL_WIDTH), lambda i: (i, 0)), out_shape=_sds((T, D_MODEL), BF16))(proj, proj, w_diag, scale)


def _pool_bwd(proj, dmix, w_diag, scale, seq, tm):
    T = proj.shape[0]
    n_tiles, tiles_per_seq = T // tm, seq // tm

    def body(z_ref, zh_ref, do_ref, don_ref, w_ref, s_ref, dz_ref, dw_ref, ds_ref):
        i = pl.program_id(0)
        t = i % tiles_per_seq

        @pl.when(i == 0)
        def _():
            dw_ref[...] = jnp.zeros_like(dw_ref)
            ds_ref[...] = jnp.zeros_like(ds_ref)

        z = z_ref[...]
        zz = jnp.concatenate([jnp.where(t == 0, 0.0, zh_ref[...]), z], axis=0)
        pos1 = (lax.broadcasted_iota(jnp.int32, (tm, 1), 0) + (t * tm + 1)).astype(F32)
        counts = _pool_counts(pos1)
        pooled = _pool_means(zz, counts) - z
        dout = do_ref[...].astype(F32)
        ds_ref[...] += jnp.sum(dout * _dot(pooled, w_ref[...]), axis=0, keepdims=True)
        dlin = dout * s_ref[...]
        dw_ref[...] += _dot(pooled, dlin, "tn")
        dpooled = _dot(dlin, w_ref[...], "nt")
        dpooled_n = _dot(don_ref[...].astype(F32) * s_ref[...], w_ref[...], "nt")
        pos1_n = (lax.broadcasted_iota(jnp.int32, (POOL_HALO, 1), 0) + ((t + 1) * tm + 1)).astype(F32)
        dmean_n = jnp.where(t == tiles_per_seq - 1, 0.0, dpooled_n / _pool_counts(pos1_n))
        dd = jnp.concatenate([dpooled / counts, dmean_n], axis=0)
        r2 = dd + _shift_up(dd, 1)
        r4 = r2 + _shift_up(r2, 2)
        r8 = r4 + _shift_up(r4, 4)
        r16 = r8 + _shift_up(r8, 8)
        dz_ref[...] = (_pool_select([r2, r4, r8, r16])[:tm] - dpooled).astype(BF16)

    full = lambda shape: BS(shape, lambda i: (0,) * len(shape))
    return _call(body, name="pool_bwd", grid=(n_tiles,),
                 in_specs=[BS((tm, POOL_WIDTH), lambda i: (i, 0)), _pool_halo_spec(tm, n_tiles, False),
                           BS((tm, POOL_WIDTH), lambda i: (i, 0)), _pool_halo_spec(tm, n_tiles, True),
                           full((POOL_WIDTH, POOL_WIDTH)), full((1, POOL_WIDTH))],
                 out_specs=[BS((tm, POOL_WIDTH), lambda i: (i, 0)), full((POOL_WIDTH, POOL_WIDTH)), full((1, POOL_WIDTH))],
                 out_shape=[_sds((T, POOL_WIDTH), BF16), _sds((POOL_WIDTH, POOL_WIDTH), F32), _sds((1, POOL_WIDTH), F32)])(
        proj, proj, dmix, dmix, w_diag, scale)


def _rope_partner(r):
    lane = lax.broadcasted_iota(jnp.int32, r.shape, 1)
    return jnp.where(lane < QK_ROPE // 2, pltpu.roll(r, LANES - QK_ROPE // 2, 1), pltpu.roll(r, QK_ROPE // 2, 1))


def _rope(x, cos, sin_signed):
    r = x[:, QK_NOPE:]
    return jnp.concatenate([x[:, :QK_NOPE], r * cos + _rope_partner(r) * sin_signed], axis=1)


def _rope_transposed(dx, cos, sin_signed):
    dr = dx[:, QK_NOPE:]
    return jnp.concatenate([dx[:, :QK_NOPE], dr * cos + _rope_partner(dr * sin_signed)], axis=1)


def _head_norm(x):
    r = lax.rsqrt(jnp.sum(x * x, axis=-1, keepdims=True) * (1.0 / QK_DIM) + EPS)
    return x * r, r


def _head_norm_bwd(dy, xhat, r, gain):
    dxhat = dy * gain
    return r * (dxhat - xhat * (jnp.sum(dxhat * xhat, axis=-1, keepdims=True) * (1.0 / QK_DIM)))


def _latents(p_ref, qag_ref, kvag_ref):
    ql = p_ref[:, POOL_WIDTH:POOL_WIDTH + Q_LORA]
    kvl = p_ref[:, POOL_WIDTH + Q_LORA:POOL_WIDTH + Q_LORA + KV_LORA]
    rq = lax.rsqrt(jnp.mean(ql * ql, axis=-1, keepdims=True) + EPS)
    rkv = lax.rsqrt(jnp.mean(kvl * kvl, axis=-1, keepdims=True) + EPS)
    return ql * rq, rq, kvl * rkv, rkv


def _mla_specs(tm):
    full = lambda shape: BS(shape, lambda i, h: (0,) * len(shape))
    return [BS((tm, ODD_IN_PAD), lambda i, h: (i, 0)), BS((tm, LANES), lambda i, h: (i, 0)), BS((tm, LANES), lambda i, h: (i, 0)),
            full((1, Q_LORA)), full((1, KV_LORA)), BS((None, Q_LORA, QK_PAD), lambda i, h: (h, 0, 0)),
            BS((None, KV_LORA, QK_PAD), lambda i, h: (h, 0, 0)), full((1, QK_PAD)), full((1, QK_PAD))]


def _mla_qkv_fwd(proj, cos, sin_signed, qa_g, kva_g, q_b, kv_b, q_g, k_g, tm):
    T = proj.shape[0]

    def body(p_ref, cos_ref, sin_ref, qag_ref, kvag_ref, qb_ref, kvb_ref, qg_ref, kg_ref, q_ref, k_ref, v_ref, qn_s, kvn_s):
        @pl.when(pl.program_id(1) == 0)
        def _():
            qhat, _, kvhat, _ = _latents(p_ref, qag_ref, kvag_ref)
            qn_s[...] = (qhat * qag_ref[...]).astype(BF16)
            kvn_s[...] = (kvhat * kvag_ref[...]).astype(BF16)

        cos, sin = cos_ref[...], sin_ref[...]
        qhat, _ = _head_norm(_dot(qn_s[...], qb_ref[...]))
        q_ref[...] = _rope(qhat * qg_ref[...], cos, sin).astype(BF16)
        kv = _dot(kvn_s[...], kvb_ref[...])
        khat, _ = _head_norm(jnp.concatenate([kv[:, :QK_NOPE], p_ref[:, ODD_IN_PAD - LANES:]], axis=1))
        k_ref[...] = _rope(khat * kg_ref[...], cos, sin).astype(BF16)
        v_ref[...] = kv[:, QK_NOPE:].astype(BF16)

    qk_spec = BS((None, tm, QK_PAD), lambda i, h: (h, i, 0))
    return _call(body, name="mla_qkv_fwd", grid=(T // tm, HEADS), in_specs=_mla_specs(tm),
                 out_specs=[qk_spec, qk_spec, BS((None, tm, V_DIM), lambda i, h: (h, i, 0))],
                 out_shape=[_sds((HEADS, T, QK_PAD), BF16), _sds((HEADS, T, QK_PAD), BF16), _sds((HEADS, T, V_DIM), BF16)],
                 scratch=[pltpu.VMEM((tm, Q_LORA), BF16), pltpu.VMEM((tm, KV_LORA), BF16)])(
        proj, cos, sin_signed, qa_g, kva_g, q_b, kv_b, q_g, k_g)


def _mla_qkv_bwd(proj, cos, sin_signed, qa_g, kva_g, q_b, kv_b, q_g, k_g, dq, dk, dv, dz_pool, tm):
    T = proj.shape[0]
    n_tiles = T // tm
    chain_rows = min(_PASS_ROWS, tm)

    def body(p_ref, cos_ref, sin_ref, qag_ref, kvag_ref, qb_ref, kvb_ref, qg_ref, kg_ref, dq_ref, dk_ref, dv_ref, dzp_ref,
             dp_ref, dqb_ref, dkvb_ref, dqg_ref, dkg_ref, dqag_ref, dkvag_ref, qn_s, kvn_s, dqn_s, dkvn_s, dkr_s,
             qh_s, kv_s, dqh_s, dkv_s):
        i, h = pl.program_id(0), pl.program_id(1)

        @pl.when((i == 0) & (h == 0))
        def _():
            for ref in (dqb_ref, dkvb_ref, dqg_ref, dkg_ref, dqag_ref, dkvag_ref):
                ref[...] = jnp.zeros_like(ref)

        @pl.when(h == 0)
        def _():
            qhat, _, kvhat, _ = _latents(p_ref, qag_ref, kvag_ref)
            qn_s[...] = (qhat * qag_ref[...]).astype(BF16)
            kvn_s[...] = (kvhat * kvag_ref[...]).astype(BF16)
            dqn_s[...] = jnp.zeros_like(dqn_s)
            dkvn_s[...] = jnp.zeros_like(dkvn_s)
            dkr_s[...] = jnp.zeros_like(dkr_s)

        qh_s[...] = _dot(qn_s[...], qb_ref[...])
        kv_s[...] = _dot(kvn_s[...], kvb_ref[...])
        qg, kg = qg_ref[...], kg_ref[...]

        def chunk(c, gains):
            dqg, dkg = gains
            rows = slice(c * chain_rows, (c + 1) * chain_rows)
            cos, sin = cos_ref[rows, :], sin_ref[rows, :]
            qhat, rq = _head_norm(qh_s[rows, :])
            dqn_head = _rope_transposed(dq_ref[rows, :], cos, sin)
            dqh_s[rows, :] = _head_norm_bwd(dqn_head, qhat, rq, qg).astype(BF16)
            kv = kv_s[rows, :]
            khat, rk = _head_norm(jnp.concatenate([kv[:, :QK_NOPE], p_ref[rows, ODD_IN_PAD - LANES:]], axis=1))
            dkn_head = _rope_transposed(dk_ref[rows, :], cos, sin)
            dkf = _head_norm_bwd(dkn_head, khat, rk, kg)
            dkr_s[rows, :] += dkf[:, QK_NOPE:]
            dkv_s[rows, :] = jnp.concatenate([dkf[:, :QK_NOPE], dv_ref[rows, :]], axis=1).astype(BF16)
            return dqg + dqn_head * qhat, dkg + dkn_head * khat

        dqg = dkg = jnp.zeros((chain_rows, QK_PAD), F32)
        for c in range(tm // chain_rows):
            dqg, dkg = chunk(c, (dqg, dkg))
        dqg_ref[...] += jnp.sum(dqg, axis=0, keepdims=True)
        dkg_ref[...] += jnp.sum(dkg, axis=0, keepdims=True)
        dqb_ref[h] += _dot(qn_s[...], dqh_s[...], "tn")
        dqn_s[...] += _dot(dqh_s[...], qb_ref[...], "nt")
        dkvb_ref[h] += _dot(kvn_s[...], dkv_s[...], "tn")
        dkvn_s[...] += _dot(dkv_s[...], kvb_ref[...], "nt")

        @pl.when(h == HEADS - 1)
        def _():
            qhat_l, rql, kvhat_l, rkvl = _latents(p_ref, qag_ref, kvag_ref)
            dqn, dkvn = dqn_s[...], dkvn_s[...]
            dqag_ref[...] += jnp.sum(dqn * qhat_l, axis=0, keepdims=True)
            dkvag_ref[...] += jnp.sum(dkvn * kvhat_l, axis=0, keepdims=True)
            dqx, dkvx = dqn * qag_ref[...], dkvn * kvag_ref[...]
            dp_ref[:, :POOL_WIDTH] = dzp_ref[...]
            dp_ref[:, POOL_WIDTH:POOL_WIDTH + Q_LORA] = (
                rql * (dqx - qhat_l * jnp.mean(dqx * qhat_l, axis=-1, keepdims=True))).astype(BF16)
            dp_ref[:, POOL_WIDTH + Q_LORA:ODD_IN_PAD - LANES] = (
                rkvl * (dkvx - kvhat_l * jnp.mean(dkvx * kvhat_l, axis=-1, keepdims=True))).astype(BF16)
            dp_ref[:, ODD_IN_PAD - LANES:] = dkr_s[:, :QK_ROPE].astype(BF16)

    full = lambda shape: BS(shape, lambda i, h: (0,) * len(shape))
    qk_spec = BS((None, tm, QK_PAD), lambda i, h: (h, i, 0))
    return _call(body, name="mla_qkv_bwd", grid=(n_tiles, HEADS),
                 in_specs=_mla_specs(tm) + [qk_spec, qk_spec, BS((None, tm, V_DIM), lambda i, h: (h, i, 0)),
                                            BS((tm, POOL_WIDTH), lambda i, h: (i, 0))],
                 out_specs=[BS((tm, ODD_IN), lambda i, h: (i, 0)), full((HEADS, Q_LORA, QK_PAD)), full((HEADS, KV_LORA, QK_PAD)),
                            full((1, QK_PAD)), full((1, QK_PAD)), full((1, Q_LORA)), full((1, KV_LORA))],
                 out_shape=[_sds((T, ODD_IN), BF16),_sds((HEADS, Q_LORA, QK_PAD), F32), _sds((HEADS, KV_LORA, QK_PAD), F32),
                            _sds((1, QK_PAD), F32), _sds((1, QK_PAD), F32), _sds((1, Q_LORA), F32), _sds((1, KV_LORA), F32)],
                 scratch=[pltpu.VMEM((tm, Q_LORA), BF16), pltpu.VMEM((tm, KV_LORA), BF16), pltpu.VMEM((tm, Q_LORA), F32),
                          pltpu.VMEM((tm, KV_LORA), F32), pltpu.VMEM((tm, LANES), F32), pltpu.VMEM((tm, QK_PAD), F32),
                          pltpu.VMEM((tm, QK_PAD), F32), pltpu.VMEM((tm, QK_PAD), BF16), pltpu.VMEM((tm, QK_PAD), BF16)])(
        proj, cos, sin_signed, qa_g, kva_g, q_b, kv_b, q_g, k_g, dq, dk, dv, dz_pool)


def _attn_tile(seq):
    return 512 if seq % 512 == 0 else seq


def _causal_mask(s):
    row = lax.broadcasted_iota(jnp.int32, s.shape, 0)
    col = lax.broadcasted_iota(jnp.int32, s.shape, 1)
    return jnp.where(row >= col, s, NEG_INF)


def _tile(i, t):
    return slice(i * t, (i + 1) * t)


def _flash_fwd(q, k, v, mix, batch, seq):
    t = _attn_tile(seq)
    nq = seq // t

    def body(q_ref, k_ref, v_ref, _, o_ref, lse_ref):
        for qi in range(nq):
            rows, before = _tile(qi, t), slice(0, qi * t)
            qv = q_ref[rows, :]
            s_diag = _causal_mask(_dot(qv, k_ref[rows, :], "nt") * ATTN_SCALE)
            m = jnp.max(s_diag, axis=-1, keepdims=True)
            if qi:
                s_before = _dot(qv, k_ref[before, :], "nt") * ATTN_SCALE
                m = jnp.maximum(m, jnp.max(s_before, axis=-1, keepdims=True))
            p = jnp.exp(s_diag - m)
            l = jnp.sum(p, axis=-1, keepdims=True)
            acc = _dot(p, v_ref[rows, :])
            if qi:
                p = jnp.exp(s_before - m)
                l = l + jnp.sum(p, axis=-1, keepdims=True)
                acc = acc + _dot(p, v_ref[before, :])
            o_ref[rows, :] = (acc / l).astype(BF16)
            lse_ref[rows, :] = jnp.broadcast_to(m + jnp.log(l), (t, LANES))

    T = batch * seq
    whole = lambda w: BS((None, seq, w), lambda b, h: (h, b, 0))
    return _call(body, name="flash_fwd", grid=(batch, HEADS),
                 in_specs=[whole(QK_PAD), whole(QK_PAD), whole(V_DIM), pl.BlockSpec(memory_space=pl.ANY)],
                 out_specs=[BS((seq, V_DIM), lambda b, h: (b, POOL_WIDTH // V_DIM + h)), whole(LANES)],
                 out_shape=[_sds((T, D_MODEL), BF16), _sds((HEADS, T, LANES), F32)],
                 aliases={3: 0})(q, k, v, mix)


def _flash_bwd(q, k, v, dmix, mix, lse, batch, seq):
    t = _attn_tile(seq)
    nq = seq // t

    def body(q_ref, k_ref, v_ref, do_ref, o_ref, lse_ref, dq_ref, dk_ref, dv_ref):
        for qi in range(nq):
            rows, before = _tile(qi, t), slice(0, qi * t)
            qv, do = q_ref[rows, :], do_ref[rows, :]
            lse = lse_ref[rows, 0:1]
            delta = jnp.sum(do.astype(F32) * o_ref[rows, :].astype(F32), axis=-1, keepdims=True)

            def block(keys, masked):
                kk = k_ref[keys, :]
                s = _dot(qv, kk, "nt") * ATTN_SCALE
                p = jnp.exp((_causal_mask(s) if masked else s) - lse)
                ds = p * (_dot(do, v_ref[keys, :], "nt") - delta) * ATTN_SCALE
                return _dot(p, do, "tn"), _dot(ds, qv, "tn"), _dot(ds, kk)

            dv_ref[rows, :], dk_ref[rows, :], dq = block(rows, True)
            if qi:
                dv, dk, dq_before = block(before, False)
                dv_ref[before, :] += dv
                dk_ref[before, :] += dk
                dq = dq + dq_before
            dq_ref[rows, :] = dq

    T = batch * seq
    whole = lambda w: BS((None, seq, w), lambda b, h: (h, b, 0))
    head_cols = BS((seq, V_DIM), lambda b, h: (b, POOL_WIDTH // V_DIM + h))
    return _call(body, name="flash_bwd", grid=(batch, HEADS),
                 in_specs=[whole(QK_PAD), whole(QK_PAD), whole(V_DIM), head_cols, head_cols, whole(LANES)],
                 out_specs=[whole(QK_PAD), whole(QK_PAD), whole(V_DIM)],
                 out_shape=[_sds((HEADS, T, QK_PAD), F32), _sds((HEADS, T, QK_PAD), F32), _sds((HEADS, T, V_DIM), F32)])(
        q, k, v, dmix, mix, lse)


def _adamw_math(w, g, m, v):
    m = ADAM_B1 * m + (1.0 - ADAM_B1) * g
    v = ADAM_B2 * v + (1.0 - ADAM_B2) * (g * g)
    m_hat = m / (1.0 - ADAM_B1 ** ADAM_STEP)
    v_hat = v / (1.0 - ADAM_B2 ** ADAM_STEP)
    return -ADAM_LR * (m_hat / (jnp.sqrt(v_hat) + ADAM_EPS) + ADAM_WD * w), m, v


def _adamw(name, w, g, m, v, l=0, prev=()):
    L, R, C = w.shape
    tr = 256 if R % 256 == 0 else R

    def body(w_ref, g_ref, m_ref, v_ref, *rest):
        go_ref, d_ref, mo_ref, vo_ref, token = rest[-5:]
        gv = g_ref[...]
        d_ref[...], mo_ref[...], vo_ref[...] = _adamw_math(w_ref[...], gv, m_ref[...], v_ref[...])
        go_ref[...] = gv
        token[...] = jnp.zeros_like(token)

    layer = BS((None, tr, C), lambda i: (l, i, 0))
    return _call(body, name=f"{name}_{l}", grid=(R // tr,),
                 in_specs=[layer, BS((tr, C), lambda i: (i, 0)), layer, layer] + [pl.BlockSpec(memory_space=pl.ANY)] * len(prev),
                 out_specs=[layer] * 4 + [BS((SUBLANES, LANES), lambda i: (0, 0))],
                 out_shape=[_sds((L, R, C), F32)] * 4 + [_sds((SUBLANES, LANES), F32)],
                 aliases={4 + n: n for n in range(len(prev))})(w, g, m, v, *prev)


def _place():
    x, y, c = lax.axis_index("x"), lax.axis_index("y"), lax.axis_index("c")
    other_chips = [(1 - x, y), (x, 1 - y), (1 - x, 1 - y)]
    return x, y, c, other_chips


def _remote(src, dst, send_sem, recv_sem, dev):
    return pltpu.make_async_remote_copy(src_ref=src, dst_ref=dst, send_sem=send_sem, recv_sem=recv_sem,
                                        device_id=dev, device_id_type=MESH)


def _prefetch_call(body, *, name, grid, in_specs, out_specs, out_shape):
    grid_spec = pltpu.PrefetchScalarGridSpec(num_scalar_prefetch=1, grid=grid, in_specs=in_specs, out_specs=out_specs)
    params = pltpu.CompilerParams(vmem_limit_bytes=VMEM_LIMIT_V7X, dimension_semantics=("arbitrary",) * len(grid))
    return pl.pallas_call(body, name=name, grid_spec=grid_spec, out_shape=out_shape, compiler_params=params)


def _row_tile(rows):
    return 256 if rows % 256 == 0 else rows


def _cast_place(name, w, layer, chip, after=()):
    _, _, rows, C = w.shape
    tr = _row_tile(rows)

    def body(chip_ref, w_ref, *rest):
        rest[-1][...] = w_ref[...].astype(BF16)

    return _prefetch_call(body, name=name, grid=(2, rows // tr),
                          in_specs=[BS((None, None, tr, C), lambda h, i, chip_ref: (layer, h, i, 0))]
                          + [pl.BlockSpec(memory_space=pl.ANY)] * len(after),
                          out_specs=BS((None, None, tr, C), lambda h, i, chip_ref: (chip_ref[0], h, i, 0)),
                          out_shape=pltpu.HBM((N_CHIPS, 2, rows, C), BF16))(chip, w, *after)


SEM = pl.BlockSpec(memory_space=pltpu.SEMAPHORE)


def _split_copy_call(body, *, name, in_specs, out_specs, out_shape, aliases):
    return pl.pallas_call(body, name=name, in_specs=in_specs, out_specs=out_specs, out_shape=out_shape,
                          input_output_aliases=aliases,
                          compiler_params=pltpu.CompilerParams(has_side_effects=pltpu.SideEffectType.DATAFLOW_SIDE_EFFECTING))


def _hbm(arrays):
    return [pltpu.with_memory_space_constraint(a, pltpu.HBM) for a in arrays]


def _gather_send(name, gs, groups, after):
    n = len(gs)

    def body(*refs):
        g, sems, token = refs[:n], refs[n + len(after):n + len(after) + 2 * len(groups)], refs[-1]
        x, y, c, chips = _place()
        me = 2 * x + y
        for gi, members in enumerate(groups):
            for a, i in enumerate(members):
                for k, (px, py) in enumerate(chips):
                    _remote(g[i].at[me, c], g[i].at[me, c], sems[2 * gi].at[3 * a + k], sems[2 * gi + 1].at[3 * a + k],
                            (px, py, c)).start()
        token[...] = jnp.zeros_like(token)

    sem_shapes = [pltpu.SemaphoreType.DMA((3 * len(members),)) for members in groups for _ in range(2)]
    out = _split_copy_call(body, name=name, in_specs=[HBM] * n + [pl.BlockSpec(memory_space=pl.ANY)] * len(after),
                           out_specs=[SEM] * len(sem_shapes) + [HBM] * n + [VMEM],
                           out_shape=sem_shapes + [pltpu.HBM(a.shape, a.dtype) for a in gs] + [_sds((SUBLANES, LANES), F32)],
                           aliases={i: len(sem_shapes) + i for i in range(n)})(*_hbm(gs), *after)
    return out[:len(sem_shapes)], out[len(sem_shapes):-1], out[-1]


def _gather_wait(name, gs, send_sems, recv_sems, after):
    n = len(gs)

    def body(*refs):
        g, ssem, rsem = refs[:n], refs[n], refs[n + 1]
        x, y, c, chips = _place()
        me = 2 * x + y
        for a in range(n):
            for k, (px, py) in enumerate(chips):
                landed = g[a].at[2 * px + py, c]
                cp = _remote(g[a].at[me, c], landed, ssem.at[3 * a + k], rsem.at[3 * a + k], (px, py, c))
                cp.wait_recv()
                cp.wait_send()

    return _split_copy_call(body, name=name, in_specs=[HBM] * n + [SEM, SEM] + [pl.BlockSpec(memory_space=pl.ANY)] * len(after),
                            out_specs=[HBM] * n, out_shape=[pltpu.HBM(a.shape, a.dtype) for a in gs],
                            aliases={i: i for i in range(n)})(*gs, send_sems, recv_sems, *after)


def _gather_pass(name, gs):
    n = len(gs)

    def body(*refs):
        g, send_sems, recv_sems = refs[n:2 * n], refs[-2], refs[-1]
        x, y, c, chips = _place()
        sibling = (x, y, 1 - c)
        passed = [_remote(g[i].at[2 * px + py, c], g[i].at[2 * px + py, c], send_sems.at[3 * i + k], recv_sems.at[3 * i + k], sibling)
                  for i in range(n) for k, (px, py) in enumerate(chips)]
        for cp in passed:
            cp.start()
        for i in range(n):
            for k, (px, py) in enumerate(chips):
                theirs = g[i].at[2 * px + py, 1 - c]
                _remote(theirs, theirs, send_sems.at[3 * i + k], recv_sems.at[3 * i + k], sibling).wait_recv()
        for cp in passed:
            cp.wait_send()

    return _call(body, name=name, in_specs=[HBM] * n, out_specs=[HBM] * n, out_shape=[_sds(a.shape, a.dtype) for a in gs],
                 aliases={i: i for i in range(n)},
                 scratch=[pltpu.SemaphoreType.DMA((3 * n,)), pltpu.SemaphoreType.DMA((3 * n,))])(*gs)


def _scatter_send(name, ps):
    n = len(ps)

    def body(*refs):
        p, r, ssem, rsem, token = refs[:n], refs[n:2 * n], refs[2 * n], refs[2 * n + 1], refs[-1]
        x, y, c, chips = _place()
        for i in range(n):
            for k, (px, py) in enumerate(chips):
                _remote(p[i].at[2 * px + py], r[i].at[k], ssem.at[3 * i + k], rsem.at[3 * i + k], (px, py, c)).start()
        token[...] = jnp.zeros_like(token)

    lands = [lax.empty((N_CHIPS - 1,) + a.shape[1:], a.dtype) for a in ps]
    sem = pltpu.SemaphoreType.DMA((3 * n,))
    out = _split_copy_call(body, name=name, in_specs=[HBM] * (2 * n), out_specs=[SEM, SEM] + [HBM] * (2 * n) + [VMEM],
                           out_shape=[sem, sem] + [pltpu.HBM(a.shape, a.dtype) for a in list(ps) + lands] + [_sds((SUBLANES, LANES), F32)],
                           aliases={i: 2 + i for i in range(2 * n)})(*_hbm(list(ps) + lands))
    return out[0], out[1], out[2:2 + n], out[2 + n:2 + 2 * n], out[-1]


def _scatter_wait(name, ps, lands, send_sems, recv_sems, after):
    n = len(ps)

    def body(*refs):
        p, r, ssem, rsem = refs[:n], refs[n:2 * n], refs[2 * n], refs[2 * n + 1]
        x, y, c, chips = _place()
        for i in range(n):
            for k, (px, py) in enumerate(chips):
                cp = _remote(p[i].at[2 * px + py], r[i].at[k], ssem.at[3 * i + k], rsem.at[3 * i + k], (px, py, c))
                cp.wait_recv()
                cp.wait_send()

    out = _split_copy_call(body, name=name, in_specs=[HBM] * (2 * n) + [SEM, SEM] + [pl.BlockSpec(memory_space=pl.ANY)] * len(after),
                           out_specs=[HBM] * (2 * n), out_shape=[pltpu.HBM(a.shape, a.dtype) for a in list(ps) + list(lands)],
                           aliases={i: i for i in range(2 * n)})(*ps, *lands, send_sems, recv_sems, *after)
    return out[:n], out[n:]


def _exchange_send(name, gs):
    n = len(gs)

    def body(*refs):
        g, r, ssem, rsem, token = refs[:n], refs[n:2 * n], refs[2 * n], refs[2 * n + 1], refs[-1]
        x, y, c, _ = _place()
        for i in range(n):
            _remote(g[i].at[:, 1 - c], r[i], ssem.at[i], rsem.at[i], (x, y, 1 - c)).start()
        token[...] = jnp.zeros_like(token)

    lands = [lax.empty((a.shape[0],) + a.shape[2:], a.dtype) for a in gs]
    sem = pltpu.SemaphoreType.DMA((n,))
    out = _split_copy_call(body, name=name, in_specs=[HBM] * (2 * n), out_specs=[SEM, SEM] + [HBM] * (2 * n) + [VMEM],
                           out_shape=[sem, sem] + [pltpu.HBM(a.shape, a.dtype) for a in list(gs) + lands] + [_sds((SUBLANES, LANES), F32)],
                           aliases={i: 2 + i for i in range(2 * n)})(*_hbm(list(gs) + lands))
    return out[0], out[1], out[2:2 + n], out[2 + n:2 + 2 * n], out[-1]


def _exchange_wait(name, gs, lands, send_sems, recv_sems, after):
    n = len(gs)

    def body(*refs):
        g, r, ssem, rsem = refs[:n], refs[n:2 * n], refs[2 * n], refs[2 * n + 1]
        x, y, c, _ = _place()
        for i in range(n):
            cp = _remote(g[i].at[:, 1 - c], r[i], ssem.at[i], rsem.at[i], (x, y, 1 - c))
            cp.wait_recv()
            cp.wait_send()

    out = _split_copy_call(body, name=name, in_specs=[HBM] * (2 * n) + [SEM, SEM] + [pl.BlockSpec(memory_space=pl.ANY)] * len(after),
                           out_specs=[HBM] * (2 * n), out_shape=[pltpu.HBM(a.shape, a.dtype) for a in list(gs) + list(lands)],
                           aliases={i: i for i in range(2 * n)})(*gs, *lands, send_sems, recv_sems, *after)
    return out[:n], out[n:]


def _sibling_share(name, fs):
    n = len(fs)

    def body(*refs):
        f, send_sems, recv_sems = refs[n:2 * n], refs[-2], refs[-1]
        x, y, c, _ = _place()
        sends = [_remote(f[i].at[c], f[i].at[c], send_sems.at[i], recv_sems.at[i], (x, y, 1 - c)) for i in range(n)]
        for cp in sends:
            cp.start()
        for i in range(n):
            theirs = f[i].at[1 - c]
            _remote(theirs, theirs, send_sems.at[i], recv_sems.at[i], (x, y, 1 - c)).wait_recv()
        for cp in sends:
            cp.wait_send()

    return _call(body, name=name, in_specs=[HBM] * n, out_specs=[HBM] * n,
                 out_shape=[_sds(a.shape, a.dtype) for a in fs], aliases={i: i for i in range(n)},
                 scratch=[pltpu.SemaphoreType.DMA((n,)), pltpu.SemaphoreType.DMA((n,))])(*fs)


def _all_reduce_small(name, v):
    rows = v.shape[0] // 2
    halves = (2, rows, LANES)

    def body(v_ref, o_ref, from_sibling, chip_sums, send_sems, recv_sems):
        x, y, c, chips = _place()
        me, sibling = 2 * x + y, (x, y, 1 - c)
        swap = _remote(v_ref.at[1 - c], from_sibling, send_sems.at[0], recv_sems.at[0], sibling)
        swap.start()
        swap.wait()
        chip_sums[me] = v_ref[c] + from_sibling[...]
        sends = [_remote(chip_sums.at[me], chip_sums.at[me], send_sems.at[1 + k], recv_sems.at[1 + k], (px, py, c))
                 for k, (px, py) in enumerate(chips)]
        for cp in sends:
            cp.start()
        for k, (px, py) in enumerate(chips):
            theirs = chip_sums.at[2 * px + py]
            _remote(theirs, theirs, send_sems.at[1 + k], recv_sems.at[1 + k], (px, py, c)).wait_recv()
        for cp in sends:
            cp.wait_send()
        acc = chip_sums[0]
        for j in range(1, N_CHIPS):
            acc = acc + chip_sums[j]
        o_ref[c] = acc
        share = _remote(o_ref.at[c], o_ref.at[c], send_sems.at[4], recv_sems.at[4], sibling)
        share.start()
        share.wait_send()
        _remote(o_ref.at[1 - c], o_ref.at[1 - c], send_sems.at[4], recv_sems.at[4], sibling).wait_recv()

    return _call(body, name=name, in_specs=[VMEM], out_specs=VMEM, out_shape=_sds(halves, F32),
                 scratch=[pltpu.VMEM((rows, LANES), F32), pltpu.VMEM((N_CHIPS, rows, LANES), F32),
                          pltpu.SemaphoreType.DMA((5,)), pltpu.SemaphoreType.DMA((5,))])(v.reshape(halves)).reshape(v.shape)


def _add_halves(name, g, r, c):
    _, _, rows, C = g.shape
    tr = _row_tile(rows)

    def body(c_ref, g_ref, r_ref, o_ref):
        o_ref[...] = (g_ref[...] + r_ref[...]).astype(BF16)

    spec = BS((None, tr, C), lambda j, i, c_ref: (j, i, 0))
    return _prefetch_call(body, name=name, grid=(N_CHIPS, rows // tr),
                          in_specs=[BS((None, None, tr, C), lambda j, i, c_ref: (j, c_ref[0], i, 0)), spec], out_specs=spec,
                          out_shape=pltpu.HBM((N_CHIPS, rows, C), BF16))(c, g, r)


def _sum_partials(name, p, r, chip_c):
    _, rows, C = p.shape
    tr = _row_tile(rows)

    def body(s_ref, p_ref, r_ref, o_ref):
        acc = p_ref[...].astype(F32)
        for k in range(N_CHIPS - 1):
            acc = acc + r_ref[k].astype(F32)
        o_ref[...] = acc

    return _prefetch_call(body, name=name, grid=(rows // tr,),
                          in_specs=[BS((None, tr, C), lambda i, s: (s[0], i, 0)), BS((N_CHIPS - 1, tr, C), lambda i, s: (0, i, 0))],
                          out_specs=BS((None, tr, C), lambda i, s: (s[1], i, 0)), out_shape=pltpu.HBM((2, rows, C), F32))(chip_c, p, r)


_SHARDED = ("even_w_in", "even_w_out", "odd_w_in", "q_b", "kv_b", "odd_w_out", "ffn_w_gate", "ffn_w_up", "ffn_w_down")
_REPLICATED = ("mix_norm", "ffn_norm", "sg_ln_g", "sg_w_s", "sg_b_s", "pool_w", "q_norm", "k_norm")
_SMALL_SHARDED = ("sc_conv_w", "pool_scale", "q_a_norm", "kv_a_norm")
_WEIGHTS = ("mix_norm", "ffn_norm", "even_w_in", "sg_ln_g", "sg_w_s", "sg_b_s", "sc_conv_w", "even_w_out", "odd_w_in", "pool_w",
            "pool_scale", "q_a_norm", "q_b", "kv_a_norm", "kv_b", "q_norm", "k_norm", "odd_w_out", "ffn_w_gate", "ffn_w_up",
            "ffn_w_down")


def _pad_rows(flat, width, align):
    n = flat.shape[0]
    rows = -(-n // (width * align)) * align
    return jnp.pad(flat, (0, rows * width - n)).reshape(rows, width)


_GROUPS = {"even": ("even_w_in", "even_w_out"),
           "ffn0": ("ffn_w_gate0", "ffn_w_up0", "ffn_w_down0"),
           "odd": ("odd_w_in", "q_b", "kv_b", "odd_w_out"),
           "ffn1": ("ffn_w_gate1", "ffn_w_up1", "ffn_w_down1")}


def _place_shards(shards, names, chip, after):
    placed = []
    for n in names:
        weight, layer = (n[:-1], int(n[-1])) if n[-1].isdigit() else (n, 0)
        a = shards[weight]
        placed.append(_cast_place(f"place_{n}", a.reshape(a.shape[0], 2, a.shape[1] // 2, a.shape[2]), layer, chip, after))
    return placed


def _whole_weights(gathered):
    out = {n: a.reshape(N_CHIPS, -1, a.shape[-1]) for n, a in gathered.items()}
    for n in ("q_b", "kv_b"):
        if n in out:
            out[n] = out[n].transpose(1, 0, 2).reshape(out[n].shape[1], -1)
    for n in ("even_w_out", "odd_w_in", "odd_w_out"):
        if n in out:
            out[n] = out[n].reshape(-1, out[n].shape[-1])
    return out


def _forward_backward(x, positions, target, small, fetch, emit, advance):
    batch, seq, _ = x.shape
    T = batch * seq
    tm = _token_tile(seq)
    x0 = x.reshape(T, D_MODEL)

    inv_freq = ROPE_THETA ** (-jnp.arange(0, QK_ROPE, 2, dtype=F32) / QK_ROPE)
    ang = (positions.astype(F32)[..., None] * inv_freq).reshape(T, QK_ROPE // 2)
    cos, sin = jnp.cos(ang), jnp.sin(ang)
    pad = jnp.zeros((T, LANES - QK_ROPE), F32)
    cos_t = jnp.concatenate([cos, cos, pad], axis=1)
    sin_t = jnp.concatenate([-sin, sin, pad], axis=1)

    tril = jnp.tril(jnp.ones((SG_CHUNK, SG_CHUNK), bool))
    w_tril = jnp.where(tril[None], small["sg_w_s"][0], 0.0).astype(BF16)
    b_lanes = jnp.broadcast_to(small["sg_b_s"][0][:, :, None], (SG_HEADS, SG_CHUNK, SG_DIM))
    conv_w = jnp.pad(small["sc_conv_w"][0], ((0, SUBLANES - CONV_TAPS), (0, 0)))
    ln_g = small["sg_ln_g"]
    pool_diag = jnp.zeros((POOL_WIDTH, POOL_WIDTH), F32)
    for g in range(len(POOL_WINDOWS)):
        pool_diag = pool_diag.at[POOL_DIM * g:POOL_DIM * (g + 1), POOL_DIM * g:POOL_DIM * (g + 1)].set(small["pool_w"][0, g])
    pool_diag = pool_diag.astype(BF16)
    pool_scale = small["pool_scale"]
    q_g = jnp.pad(small["q_norm"], ((0, 0), (0, QK_PAD - QK_DIM)))
    k_g = jnp.pad(small["k_norm"], ((0, 0), (0, QK_PAD - QK_DIM)))
    qa_g, kva_g = small["q_a_norm"], small["kv_a_norm"]
    in_shard = EVEN_IN // N_CHIPS

    def ffn_weights(l, w):
        return w[f"ffn_w_gate{l}"], w[f"ffn_w_up{l}"], w[f"ffn_w_down{l}"]

    W = fetch("even", ())
    w_in_even = W["even_w_in"]
    h0 = _rmsnorm_fwd("mix0_norm", x0, small["mix_norm"][0], tm)
    tb = _big_tile(T)
    proj0 = _matmul("even_in", "nn", [(h0, w_in_even)],
                    [(BS((tb, D_MODEL), lambda i, j, k: (i, 0)), BS((None, D_MODEL, in_shard), lambda i, j, k: (j, 0, 0)))],
                    (T // tb, N_CHIPS, 1), _sds((T, EVEN_IN), F32), BS((tb, in_shard), lambda i, j, k: (i, j)), (tb, in_shard))
    mix0 = _even_mixer_fwd(proj0, ln_g, w_tril, b_lanes, conv_w, seq, tm)
    w_out_even = W["even_w_out"]
    x1, h1 = _mm("even_out", "nn", mix0, w_out_even, F32, tk=1024, add=x0, fused=_norm_tail(small["ffn_norm"][0], T, tb))
    ffn0 = ffn_weights(0, fetch("ffn0", (x1,)))
    (x2, h2), ffn0_saved = _ffn_fwd(0, x1, h1, *ffn0, lambda tile: _norm_tail(small["mix_norm"][1], T, tile))
    W = fetch("odd", (x2,))
    w_in_odd = jnp.pad(W["odd_w_in"], ((0, 0), (0, ODD_IN_PAD - ODD_IN)))
    q_b = jnp.pad(W["q_b"].reshape(Q_LORA, HEADS, QK_DIM).transpose(1, 0, 2), ((0, 0), (0, 0), (0, QK_PAD - QK_DIM)))
    kv_b = W["kv_b"].reshape(KV_LORA, HEADS, QK_NOPE + V_DIM).transpose(1, 0, 2)
    proj1 = _mm("odd_in", "nn", h2, w_in_odd, F32, tk=1024)
    mix1 = _pool_fwd(proj1, pool_diag, pool_scale, seq, tm)
    q, k, v = _mla_qkv_fwd(proj1, cos_t, sin_t, qa_g, kva_g, q_b, kv_b, q_g, k_g, tm)
    mix1, lse = _flash_fwd(q, k, v, mix1, batch, seq)
    x3, h3 = _mm("odd_out", "nn", mix1, W["odd_w_out"], F32, tk=1024, add=x2, fused=_norm_tail(small["ffn_norm"][1], T, tb))
    ffn1 = ffn_weights(1, fetch("ffn1", (x3,)))
    (dy, sq), ffn1_saved = _ffn_fwd(1, x3, h3, *ffn1, lambda tile: _loss_tail(target.reshape(T, D_MODEL), tile))

    G = {}
    dx3, dffn_g1 = _ffn_bwd(1, x3, small["ffn_norm"][1], *ffn1, ffn1_saved, dy, emit)
    dmix1 = _mm("odd_out_dx", "nt", dx3, W["odd_w_out"], BF16, tk=1024, after=advance(dx3))
    dw_out_odd = _mm("odd_out_dw", "tn", mix1, dx3, F32, hbm_out=True)
    dq, dk, dv = _flash_bwd(q, k, v, dmix1, mix1, lse, batch, seq)
    dz_pool, dpool_diag, G["pool_scale"] = _pool_bwd(proj1, dmix1, pool_diag, pool_scale, seq, tm)
    dproj1, dq_b, dkv_b, dq_g, dk_g, G["q_a_norm"], G["kv_a_norm"] = _mla_qkv_bwd(
        proj1, cos_t, sin_t, qa_g, kva_g, q_b, kv_b, q_g, k_g, dq, dk, dv, dz_pool, tm)
    G["pool_w"] = jnp.stack([dpool_diag[POOL_DIM * g:POOL_DIM * (g + 1), POOL_DIM * g:POOL_DIM * (g + 1)]
                             for g in range(len(POOL_WINDOWS))])[None]
    G["q_norm"], G["k_norm"] = dq_g[:, :QK_DIM], dk_g[:, :QK_DIM]
    dw_in_odd = _mm("odd_in_dw", "tn", h2, dproj1, F32, tn=ODD_IN, hbm_out=True)

    def shard_major(g, cols):
        return g.reshape(g.shape[0], N_CHIPS, cols).transpose(1, 0, 2)

    behind = emit("odd", {"odd_w_in": dw_in_odd.reshape(N_CHIPS, -1, ODD_IN),
                          "q_b": shard_major(dq_b[:, :, :QK_DIM].transpose(1, 0, 2).reshape(Q_LORA, HEADS * QK_DIM), HEADS * QK_DIM // N_CHIPS),
                          "kv_b": shard_major(dkv_b.transpose(1, 0, 2).reshape(KV_LORA, HEADS * (QK_NOPE + V_DIM)),
                                              HEADS * (QK_NOPE + V_DIM) // N_CHIPS),
                          "odd_w_out": dw_out_odd.reshape(N_CHIPS, -1, D_MODEL)})
    dx2, dmix_g1 = _mm("odd_in_dx", "nt", dproj1, W["odd_w_in"], F32, tk=ODD_IN, after=behind,
                       fused=_norm_bwd_tail(x2, small["mix_norm"][1], dx3, tb))
    dx1, dffn_g0 = _ffn_bwd(0, x1, small["ffn_norm"][0], *ffn0, ffn0_saved, dx2, emit, after=advance(dx2))
    dmix0 = _mm("even_out_dx", "nt", dx1, w_out_even, F32, tk=1024, after=advance(dx1))
    dw_out_even = _mm("even_out_dw", "tn", mix0, dx1, F32, hbm_out=True)
    dproj0, dw_s, db_lanes, G["sg_ln_g"], dconv = _even_mixer_bwd(proj0, dmix0, ln_g, w_tril, b_lanes, conv_w, seq, tm)
    G["sg_w_s"] = dw_s[None]
    G["sg_b_s"] = jnp.sum(db_lanes, axis=-1)[None]
    G["sc_conv_w"] = dconv[None, :CONV_TAPS]
    tr = _resident_tile(T)
    tail, shapes, specs = _norm_bwd_tail(x0, small["mix_norm"][0], dx1, tr)
    dx0, dmix_g0 = _matmul("even_in_dx", "nt", [(dproj0, w_in_even)],
                           [(_row_spec(tr, EVEN_IN), _resident((N_CHIPS, D_MODEL, in_shard)))],
                           (T // tr, 1, 1), shapes, specs, (tr, D_MODEL), tail=tail)
    tk = min(512, T)
    dw_in_even = _grad_shards(
        "even_in_dw", h0, dproj0, BS((tk, D_MODEL), lambda k: (k, 0)), BS((tk, EVEN_IN), lambda k: (k, 0)),
        lambda a_ref, b_ref, j: (a_ref[...], b_ref[:, in_shard * j:in_shard * (j + 1)]), (N_CHIPS, D_MODEL, in_shard), T // tk)
    emit("even", {"even_w_in": dw_in_even, "even_w_out": dw_out_even.reshape(N_CHIPS, -1, D_MODEL)})
    G["mix_norm"] = jnp.concatenate([dmix_g0, dmix_g1], axis=0)
    G["ffn_norm"] = jnp.concatenate([dffn_g0, dffn_g1], axis=0)
    return sq[0, 0], dx0.reshape(batch, seq, D_MODEL), G


def _small_vector(parts, names):
    flat = jnp.concatenate([parts[n].astype(F32).reshape(-1) for n in names])
    return _pad_rows(flat, LANES, 2 * SUBLANES)


def _split_small(vec, like, names):
    out, off, flat = {}, 0, vec.reshape(-1)
    for n in names:
        size = math.prod(like[n].shape)
        out[n] = flat[off:off + size].reshape(like[n].shape)
        off += size
    return out


def _whole_shape(a):
    return a.shape[:-1] + (a.shape[-1] * N_CHIPS,)


def kernel(x, positions, mix_norm, ffn_norm, even_w_in, sg_ln_g, sg_w_s, sg_b_s, sc_conv_w, even_w_out, odd_w_in, pool_w, pool_scale, q_a_norm, q_b, kv_a_norm, kv_b, q_norm, k_norm, odd_w_out, ffn_w_gate, ffn_w_up, ffn_w_down, loss_target, m_mix_norm, m_ffn_norm, m_even_w_in, m_sg_ln_g, m_sg_w_s, m_sg_b_s, m_sc_conv_w, m_even_w_out, m_odd_w_in, m_pool_w, m_pool_scale, m_q_a_norm, m_q_b, m_kv_a_norm, m_kv_b, m_q_norm, m_k_norm, m_odd_w_out, m_ffn_w_gate, m_ffn_w_up, m_ffn_w_down, v_mix_norm, v_ffn_norm, v_even_w_in, v_sg_ln_g, v_sg_w_s, v_sg_b_s, v_sc_conv_w, v_even_w_out, v_odd_w_in, v_pool_w, v_pool_scale, v_q_a_norm, v_q_b, v_kv_a_norm, v_kv_b, v_q_norm, v_k_norm, v_odd_w_out, v_ffn_w_gate, v_ffn_w_up, v_ffn_w_down):
    args = dict(locals())
    w = {n: args[n] for n in _WEIGHTS}
    m = {n: args["m_" + n] for n in _WEIGHTS}
    v = {n: args["v_" + n] for n in _WEIGHTS}
    cx, cy, cc = lax.axis_index("x"), lax.axis_index("y"), lax.axis_index("c")
    chip = 2 * cx + cy
    transposed = ("ffn_w_gate", "ffn_w_up")
    for n in transposed:
        w[n], m[n], v[n] = (jnp.swapaxes(t[n], 1, 2) for t in (w, m, v))

    chip_arr = chip.astype(jnp.int32).reshape(1)
    c_arr = cc.astype(jnp.int32).reshape(1)
    group_names = list(_GROUPS)
    placed = {}
    for n in _SMALL_SHARDED:
        a = w[n]
        whole = jnp.zeros(a.shape[:-1] + (N_CHIPS, a.shape[-1]), F32)
        whole = lax.dynamic_update_slice_in_dim(whole, a[..., None, :], chip, axis=a.ndim - 1)
        placed[n] = jnp.where(cc == 0, whole, 0.0).reshape(_whole_shape(a))
    small_whole = _all_reduce_small("gather_small_weights", _small_vector(placed, _SMALL_SHARDED))
    small = dict({n: w[n] for n in _REPLICATED}, **_split_small(small_whole, placed, _SMALL_SHARDED))

    first, rest = list(_GROUPS[group_names[0]]), [n for g in group_names[1:] for n in _GROUPS[g]]
    sems_first, flight_first, token = _gather_send("gather_send_first", _place_shards(w, first, chip_arr, (small_whole,)),
                                                   [list(range(len(first)))], (small_whole,))
    sems_rest, flight_rest, all_sent = _gather_send("gather_send_rest", _place_shards(w, rest, chip_arr, (token,)),
                                                    [[rest.index(n) for n in _GROUPS[g]] for g in group_names[1:]], ())
    sems = list(sems_first) + list(sems_rest)
    in_flight = dict(zip(first + rest, list(flight_first) + list(flight_rest)))

    def fetch(group, after):
        gi, members = group_names.index(group), _GROUPS[group]
        after = after if gi else (all_sent,)
        landed = _gather_wait(f"gather_wait_{group}", [in_flight[n] for n in members], sems[2 * gi], sems[2 * gi + 1], after)
        return _whole_weights(dict(zip(members, _gather_pass(f"gather_pass_{group}", landed))))

    swapping, pending, arrived = [], [], {}

    def settle(after):
        names, ps, lands, send_sems, recv_sems = pending.pop()
        ps, lands = _scatter_wait(f"scatter_wait_{names[0]}", ps, lands, send_sems, recv_sems, after)
        arrived.update({n: (p, r) for n, p, r in zip(names, ps, lands)})

    def emit(group, grads):
        names = _GROUPS[group]
        halves = [grads[n].reshape(N_CHIPS, 2, grads[n].shape[1] // 2, grads[n].shape[2]) for n in names]
        send_sems, recv_sems, halves, lands, token = _exchange_send(f"exchange_send_{group}", halves)
        swapping.append((group, halves, lands, send_sems, recv_sems))
        return (token,)

    def advance(done):
        done = done if isinstance(done, tuple) else (done,)
        group, halves, lands, send_sems, recv_sems = swapping.pop()
        names = _GROUPS[group]
        halves, lands = _exchange_wait(f"exchange_wait_{group}", halves, lands, send_sems, recv_sems, done)
        partial = [_add_halves(f"add_{n}", g, r, c_arr) for n, g, r in zip(names, halves, lands)]
        if pending:
            settle(done)
        send_sems, recv_sems, ps, lands, token = _scatter_send(f"scatter_send_{group}", partial)
        pending.append((names, ps, lands, send_sems, recv_sems))
        return (token,)

    sq, grad_x, G = _forward_backward(x, positions, loss_target, small, fetch, emit, advance)
    loss = lax.psum(0.5 * sq / D_MODEL, ("x", "y", "c"))

    small_names = _REPLICATED + _SMALL_SHARDED
    summed = _split_small(_all_reduce_small("reduce_small_grads", _small_vector(G, small_names)), G, small_names)
    grads = {n: summed[n] for n in _REPLICATED}
    for n in _SMALL_SHARDED:
        a = w[n]
        grads[n] = lax.dynamic_slice_in_dim(summed[n].reshape(a.shape[:-1] + (N_CHIPS, a.shape[-1])), chip, 1,
                                            axis=a.ndim - 1).reshape(a.shape)

    chip_c = jnp.stack([chip, cc]).astype(jnp.int32)
    out = {}

    def finish(group):
        names, tokens = _GROUPS[group], []
        sums = [_sum_partials(f"sum_{n}", *arrived[n], chip_c) for n in names]
        for n, f in zip(names, _sibling_share(f"grad_share_{group}", sums)):
            weight, layer = (n[:-1], int(n[-1])) if n[-1].isdigit() else (n, 0)
            *out[weight], token = _adamw(f"adamw_{weight}", w[weight], f.reshape(-1, f.shape[-1]), m[weight], v[weight], layer,
                                         out.get(weight, ()))
            tokens.append(token)
        return tuple(tokens)

    advance(finish(group_names[3]) + finish(group_names[2]))
    settle(finish(group_names[1]))
    finish(group_names[0])
    packed = [_small_vector(d, small_names) for d in (w, grads, m, v)]
    res = _adamw("adamw_small", packed[0][None], packed[1], packed[2][None], packed[3][None])
    delta_s, m_s, v_s = (_split_small(r, w, small_names) for r in res[1:4])
    for n in small_names:
        out[n] = (grads[n], delta_s[n], m_s[n], v_s[n])
    for n in transposed:
        out[n] = tuple(jnp.swapaxes(t, 1, 2) for t in out[n])

    return (loss, grad_x, *[out[n][0] for n in _WEIGHTS], *[out[n][1] for n in _WEIGHTS],
            *[out[n][2] for n in _WEIGHTS], *[out[n][3] for n in _WEIGHTS])
```

```python
import functools
import math

import jax
import jax.numpy as jnp
from jax import lax
from jax.experimental import pallas as pl
from jax.experimental.pallas import tpu as pltpu

F32, BF16 = jnp.float32, jnp.bfloat16
BS = pl.BlockSpec

D_MODEL = 1024
EPS = 1e-6
NEG_INF = -1e30
SG_HEADS, SG_DIM, SG_WIDTH, SG_CHUNK = 4, 128, 512, 128
SC_WIDTH, CONV_TAPS = 512, 3
EVEN_IN = 2 * SG_WIDTH + 3 * SC_WIDTH
POOL_WINDOWS = (2, 4, 8, 16)
POOL_DIM, POOL_WIDTH = 64, 256
POOL_HALO = 16
HEADS, Q_LORA, KV_LORA, QK_NOPE, QK_ROPE, V_DIM = 6, 384, 256, 128, 64, 128
QK_DIM = QK_NOPE + QK_ROPE
QK_PAD = 256
ODD_IN = POOL_WIDTH + Q_LORA + KV_LORA + QK_ROPE
ODD_IN_PAD = 1024
ROPE_THETA = 10000.0
ATTN_SCALE = QK_DIM ** -0.5
D_FF, N_CHIPS = 2816, 4
FF_SHARD = D_FF // N_CHIPS
ADAM_LR, ADAM_B1, ADAM_B2, ADAM_EPS, ADAM_WD, ADAM_STEP = 0.001, 0.9, 0.999, 1e-08, 0.01, 10
VMEM_LIMIT_V7X = 48 * 2**20
LANES, SUBLANES = 128, 8
MESH = pl.DeviceIdType.MESH
HBM = pl.BlockSpec(memory_space=pltpu.HBM)
VMEM = pl.BlockSpec(memory_space=pltpu.VMEM)

_DIMS = {"nn": (((1,), (0,)), ((), ())), "nt": (((1,), (1,)), ((), ())), "tn": (((0,), (0,)), ((), ()))}


def _dot(a, b, mode="nn"):
    return lax.dot_general(a.astype(BF16), b.astype(BF16), _DIMS[mode], preferred_element_type=F32)


def _call(body, *, name, out_shape, in_specs, out_specs, grid=(), scratch=(), aliases=None, after=()):
    params = pltpu.CompilerParams(vmem_limit_bytes=VMEM_LIMIT_V7X,
                                  **({"dimension_semantics": ("arbitrary",) * len(grid)} if grid else {}))
    n_in, n_after = len(in_specs), len(after)
    kernel_body = body if not after else (lambda *refs: body(*refs[:n_in], *refs[n_in + n_after:]))
    call = pl.pallas_call(kernel_body, name=name, grid=grid, in_specs=list(in_specs) + [pl.BlockSpec(memory_space=pl.ANY)] * n_after,
                          out_specs=out_specs, out_shape=out_shape, scratch_shapes=list(scratch),
                          input_output_aliases=aliases or {}, compiler_params=params)
    return (lambda *ops: call(*ops, *after)) if after else call


def _sds(shape, dtype):
    return jax.ShapeDtypeStruct(tuple(shape), dtype)


def _token_tile(seq):
    return 512 if seq % 512 == 0 else seq


_TAIL_ROWS = 256


def _matmul(name, mode, pairs, pair_specs, grid, out_shape, out_spec, acc_shape, add=None, add_spec=None, after=(), tail=None):
    n, nk = len(pairs), grid[-1]
    n_add = int(add is not None)
    n_tail = len(tail[0]) if tail else 0
    n_in = 2 * n + n_add + n_tail
    n_out = len(out_shape) if tail else 1

    def body(*refs):
        ab = refs[:2 * n]
        add_ref = refs[2 * n] if n_add else None
        tail_refs, outs = refs[2 * n + n_add:n_in], refs[n_in:n_in + n_out]
        first = pl.program_id(0) == 0

        def finish(result):
            if tail is None:
                r = result(slice(None))
                outs[0][...] = (r if add_ref is None else r + add_ref[...]).astype(outs[0].dtype)
                return
            for lo in range(0, acc_shape[0], _TAIL_ROWS):
                rows = slice(lo, min(lo + _TAIL_ROWS, acc_shape[0]))
                r = result(rows)
                tail[2](rows, r if add_ref is None else r + add_ref[rows, :], first, tail_refs, outs)

        def terms(a_ref, b_ref):
            if len(a_ref.shape) == 2 and len(b_ref.shape) == 2:
                return [(a_ref[...], b_ref[...])]
            cols = a_ref.shape[-1] // N_CHIPS
            return [(a_ref[j] if len(a_ref.shape) == 3 else a_ref[:, cols * j:cols * (j + 1)], b_ref[j]) for j in range(N_CHIPS)]

        if nk == 1:
            r = None
            for p in range(n):
                for a_blk, b_blk in terms(ab[2 * p], ab[2 * p + 1]):
                    d = _dot(a_blk, b_blk, mode)
                    r = d if r is None else r + d
            finish(lambda rows: r[rows])
            return
        acc = refs[-1]
        k = pl.program_id(len(grid) - 1)

        @pl.when(k == 0)
        def _():
            acc[...] = jnp.zeros_like(acc)

        for p in range(n):
            acc[...] += _dot(ab[2 * p][...], ab[2 * p + 1][...], mode)

        @pl.when(k == nk - 1)
        def _():
            finish(lambda rows: acc[rows, :])

    ops = [t for pr in pairs for t in pr] + ([add] if n_add else []) + (list(tail[0]) if tail else [])
    specs = [s for pr in pair_specs for s in pr] + ([add_spec] if n_add else []) + (list(tail[1]) if tail else [])
    return _call(body, name=name, grid=grid, in_specs=specs, out_specs=out_spec, out_shape=out_shape,
                 scratch=[pltpu.VMEM(acc_shape, F32)] if nk > 1 else [], after=after)(*ops)


def _row_spec(tm, d):
    return BS((tm, d), lambda i, j, k: (i, 0))


def _vec_spec(d):
    return BS((1, d), lambda i, j, k: (0, 0))


def _norm_tail(gain, T, tm):
    d = gain.shape[-1]

    def fn(rows, r, first, tail_refs, outs):
        outs[0][rows, :] = r
        outs[1][rows, :] = (r * lax.rsqrt(jnp.mean(r * r, axis=-1, keepdims=True) + EPS) * tail_refs[0][...]).astype(BF16)

    return ([gain.reshape(1, d)], [_vec_spec(d)], fn), [_sds((T, d), F32), _sds((T, d), BF16)], [_row_spec(tm, d), _row_spec(tm, d)]


def _norm_bwd_tail(x, gain, dres, tm):
    T, d = x.shape

    def fn(rows, r, first, tail_refs, outs):
        x_ref, g_ref, dres_ref = tail_refs
        xv = x_ref[rows, :]
        rstd = lax.rsqrt(jnp.mean(xv * xv, axis=-1, keepdims=True) + EPS)
        xhat = xv * rstd
        if rows.start == 0:
            @pl.when(first)
            def _():
                outs[1][...] = jnp.zeros_like(outs[1])

        outs[1][...] += jnp.sum(r * xhat, axis=0, keepdims=True)
        dxhat = r * g_ref[...]
        outs[0][rows, :] = dres_ref[rows, :] + rstd * (dxhat - xhat * jnp.mean(dxhat * xhat, axis=-1, keepdims=True))

    return (([x, gain.reshape(1, d), dres], [_row_spec(tm, d), _vec_spec(d), _row_spec(tm, d)], fn),
            [_sds((T, d), F32), _sds((1, d), F32)], [_row_spec(tm, d), _vec_spec(d)])


def _loss_tail(target, tm):
    T, d = target.shape

    def fn(rows, r, first, tail_refs, outs):
        e = r - tail_refs[0][rows, :]
        if rows.start == 0:
            @pl.when(first)
            def _():
                outs[1][...] = jnp.zeros_like(outs[1])

        outs[1][...] += jnp.sum(e * e)
        outs[0][rows, :] = e * (1.0 / d)

    return (([target], [_row_spec(tm, d)], fn), [_sds((T, d), F32), _sds((SUBLANES, LANES), F32)],
            [_row_spec(tm, d), BS((SUBLANES, LANES), lambda i, j, k: (0, 0))])


def _grad_shards(name, a, b, a_spec, b_spec, pick, out_shape, n_steps):
    def body(a_ref, b_ref, o_ref):
        @pl.when(pl.program_id(0) == 0)
        def _():
            o_ref[...] = jnp.zeros_like(o_ref)

        for j in range(N_CHIPS):
            aj, bj = pick(a_ref, b_ref, j)
            o_ref[j] += _dot(aj, bj, "tn")

    return _call(body, name=name, grid=(n_steps,), in_specs=[a_spec, b_spec],
                 out_specs=BS(out_shape, lambda k: (0, 0, 0)), out_shape=pltpu.HBM(tuple(out_shape), F32))(a, b)


def _mm(name, mode, a, b, out_dtype, tm=1024, tn=1024, tk=512, add=None, after=(), fused=None, hbm_out=False):
    if mode == "tn":
        (K, M), N = a.shape, b.shape[1]
    else:
        (M, K), N = a.shape, (b.shape[1] if mode == "nn" else b.shape[0])
    tm, tn, tk = min(tm, M), min(tn, N), min(tk, K)
    a_spec = BS((tk, tm), lambda i, j, k: (k, i)) if mode == "tn" else BS((tm, tk), lambda i, j, k: (i, k))
    b_spec = BS((tn, tk), lambda i, j, k: (j, k)) if mode == "nt" else BS((tk, tn), lambda i, j, k: (k, j))
    o_spec = BS((tm, tn), lambda i, j, k: (i, j))
    tail, shapes, specs = fused if fused else (None, pltpu.HBM((M, N), out_dtype) if hbm_out else _sds((M, N), out_dtype), o_spec)
    return _matmul(name, mode, [(a, b)], [(a_spec, b_spec)], (M // tm, N // tn, K // tk), shapes, specs, (tm, tn),
                   add=add, add_spec=o_spec if add is not None else None, after=after, tail=tail)


def _rmsnorm_fwd(name, x, g, tm):
    T, d = x.shape

    def body(x_ref, g_ref, o_ref):
        xv = x_ref[...]
        y = xv * lax.rsqrt(jnp.mean(xv * xv, axis=-1, keepdims=True) + EPS)
        o_ref[...] = (y * g_ref[...]).astype(o_ref.dtype)

    return _call(body, name=name, grid=(T // tm,), in_specs=[BS((tm, d), lambda i: (i, 0)), BS((1, d), lambda i: (0, 0))],
                 out_specs=BS((tm, d), lambda i: (i, 0)), out_shape=_sds((T, d), BF16))(x, g.reshape(1, d))


_PASS_ROWS = 256


def _ffn_up(name, h, wg, wu, tm):
    T = h.shape[0]

    def body(h_ref, wg_ref, wu_ref, g_ref, u_ref, a_ref):
        hv = h_ref[...]
        for j in range(N_CHIPS):
            g = _dot(hv, wg_ref[j], "nt")
            u = _dot(hv, wu_ref[j], "nt")
            g_ref[j] = g.astype(BF16)
            u_ref[j] = u.astype(BF16)
            a_ref[j] = (g * (1.0 / (1.0 + jnp.exp(-g))) * u).astype(BF16)

    w_spec = BS((N_CHIPS, FF_SHARD, D_MODEL), lambda i: (0, 0, 0), pipeline_mode=pl.Buffered(1))
    o_spec = BS((N_CHIPS, tm, FF_SHARD), lambda i: (0, i, 0))
    sh = _sds((N_CHIPS, T, FF_SHARD), BF16)
    return _call(body, name=name, grid=(T // tm,), in_specs=[BS((tm, D_MODEL), lambda i: (i, 0)), w_spec, w_spec],
                 out_specs=[o_spec, o_spec, o_spec], out_shape=[sh, sh, sh])(h, wg, wu)


def _ffn_act_bwd(name, dxo, wd, g, u, tm, after=()):
    T = dxo.shape[0]

    def body(dx_ref, wd_ref, g_ref, u_ref, dg_ref, du_ref):
        dx = dx_ref[...].astype(BF16)
        for j in range(N_CHIPS):
            da = _dot(dx, wd_ref[j], "nt")
            g = g_ref[j].astype(F32)
            sig = 1.0 / (1.0 + jnp.exp(-g))
            dg_ref[j] = (da * u_ref[j].astype(F32) * (sig * (1.0 + g * (1.0 - sig)))).astype(BF16)
            du_ref[j] = (da * (g * sig)).astype(BF16)

    t_spec = BS((N_CHIPS, tm, FF_SHARD), lambda i: (0, i, 0))
    sh = _sds((N_CHIPS, T, FF_SHARD), BF16)
    return _call(body, name=name, grid=(T // tm,),
                 in_specs=[BS((tm, D_MODEL), lambda i: (i, 0)),
                           BS((N_CHIPS, FF_SHARD, D_MODEL), lambda i: (0, 0, 0), pipeline_mode=pl.Buffered(1)), t_spec, t_spec],
                 out_specs=[t_spec, t_spec], out_shape=[sh, sh], after=after)(dxo, wd, g, u)


def _big_tile(n):
    return min(1024, n)


def _resident_tile(n):
    return min(512, n)


def _resident(shape):
    return BS(shape, lambda i, j, k: (0,) * len(shape), pipeline_mode=pl.Buffered(1))


def _ffn_fwd(l, x, h, wg, wu, wd, fused):
    T = x.shape[0]
    tm = _resident_tile(T)
    g, u, a = _ffn_up(f"ffn{l}_up", h, wg, wu, tm)
    tail, shapes, specs = fused(tm)
    outs = _matmul(f"ffn{l}_down", "nn", [(a, wd)],
                   [(BS((N_CHIPS, tm, FF_SHARD), lambda i, j, k: (0, i, 0)), _resident((N_CHIPS, FF_SHARD, D_MODEL)))],
                   (T // tm, 1, 1), shapes, specs, (tm, D_MODEL), add=x, add_spec=_row_spec(tm, D_MODEL), tail=tail)
    return outs, (h, g, u, a)


def _ffn_bwd(l, x, gain, wg, wu, wd, saved, dxo, emit, after=()):
    h, g, u, a = saved
    T = x.shape[0]
    dg, du = _ffn_act_bwd(f"ffn{l}_act_bwd", dxo, wd, g, u, _resident_tile(T), after=after)
    tk = min(512, T)
    tn = D_MODEL
    shards_spec = BS((N_CHIPS, tk, FF_SHARD), lambda k: (0, k, 0))
    rows_spec = BS((tk, D_MODEL), lambda k: (k, 0))

    def dw(nm, act, rows):
        return _grad_shards(nm, act, rows, shards_spec, rows_spec, lambda a_ref, b_ref, j: (a_ref[j], b_ref[...]),
                            (N_CHIPS, FF_SHARD, D_MODEL), T // tk)

    behind = emit(f"ffn{l}", {f"ffn_w_gate{l}": dw(f"ffn{l}_dwg", dg, h), f"ffn_w_up{l}": dw(f"ffn{l}_dwu", du, h),
                              f"ffn_w_down{l}": dw(f"ffn{l}_dwd", a, dxo)})
    tm = _resident_tile(T)
    act_spec = BS((N_CHIPS, tm, FF_SHARD), lambda i, j, k: (0, i, 0))
    w_spec = _resident((N_CHIPS, FF_SHARD, D_MODEL))
    tail, shapes, specs = _norm_bwd_tail(x, gain, dxo, tm)
    return _matmul(f"ffn{l}_dh", "nn", [(dg, wg), (du, wu)], [(act_spec, w_spec), (act_spec, w_spec)],
                   (T // tm, 1, 1), shapes, specs, (tm, D_MODEL), after=behind, tail=tail)


_INV_SQRT2 = 1.0 / math.sqrt(2.0)
_INV_SQRT_2PI = 1.0 / math.sqrt(2.0 * math.pi)


def _gelu(x):
    return 0.5 * x * (1.0 + lax.erf(x * _INV_SQRT2))


def _gelu_and_grad(x):
    cdf = 0.5 * (1.0 + lax.erf(x * _INV_SQRT2))
    return x * cdf, cdf + x * jnp.exp(-0.5 * x * x) * _INV_SQRT_2PI


def _shift_down(x, k):
    return pltpu.roll(x, k, 0)


def _shift_up(x, k):
    return pltpu.roll(x, x.shape[0] - k, 0)


def _layer_norm_head(xh):
    xc = xh - jnp.mean(xh, axis=-1, keepdims=True)
    rstd = lax.rsqrt(jnp.mean(xc * xc, axis=-1, keepdims=True) + EPS)
    return xc * rstd, rstd


def _even_in(h, w, tm):
    T = h.shape[0]
    shard = w.shape[-1]

    def body(h_ref, w_ref, o_ref):
        hv = h_ref[...]
        for j in range(N_CHIPS):
            o_ref[:, shard * j:shard * (j + 1)] = _dot(hv, w_ref[j])

    return _call(body, name="even_in", grid=(T // tm,),
                 in_specs=[BS((tm, D_MODEL), lambda i: (i, 0)), BS(w.shape, lambda i: (0, 0, 0), pipeline_mode=pl.Buffered(1))],
                 out_specs=BS((tm, N_CHIPS * shard), lambda i: (i, 0)), out_shape=_sds((T, N_CHIPS * shard), F32))(h, w)


def _even_halo_specs(tm, n_tiles, col_blocks, after):
    rows = tm // SUBLANES
    last = n_tiles * rows - 1
    if after:
        return [BS((SUBLANES, 512), functools.partial(lambda cb, i: (jnp.minimum((i + 1) * rows, last), cb), cb)) for cb in col_blocks]
    return [BS((SUBLANES, 512), functools.partial(lambda cb, i: (jnp.maximum(i * rows - 1, 0), cb), cb)) for cb in col_blocks]


def _even_mixer_fwd(proj, ln_g, w_tril, b_lanes, conv_w, seq, tm):
    T = proj.shape[0]
    tiles_per_seq = seq // tm

    def body(p_ref, hc_ref, hh_ref, lng_ref, w_ref, bb_ref, cw_ref, o_ref):
        first = pl.program_id(0) % tiles_per_seq == 0
        for h in range(SG_HEADS):
            cols = slice(SG_DIM * h, SG_DIM * (h + 1))
            vhat, _ = _layer_norm_head(_gelu(p_ref[:, SG_WIDTH + SG_DIM * h:SG_WIDTH + SG_DIM * (h + 1)]))
            vln = (vhat * lng_ref[:, cols]).astype(BF16)
            for k in range(tm // SG_CHUNK):
                rows = slice(SG_CHUNK * k, SG_CHUNK * (k + 1))
                mixed = _dot(w_ref[h], vln[rows]) + bb_ref[h]
                o_ref[rows, cols] = (_gelu(p_ref[rows, cols]) * mixed).astype(BF16)
        z = p_ref[:, 1536:2048] * p_ref[:, 2048:2560]
        zz = jnp.concatenate([jnp.where(first, 0.0, hc_ref[...] * hh_ref[...]), z], axis=0)
        y = cw_ref[0:1, :] * _shift_down(zz, 2)[SUBLANES:] + cw_ref[1:2, :] * _shift_down(zz, 1)[SUBLANES:] + cw_ref[2:3, :] * z
        o_ref[:, SG_WIDTH:] = (p_ref[:, 1024:1536] * y).astype(BF16)

    full = lambda shape: BS(shape, lambda i: (0,) * len(shape))
    return _call(body, name="even_mixer_fwd", grid=(T // tm,),
                 in_specs=[BS((tm, EVEN_IN), lambda i: (i, 0))] + _even_halo_specs(tm, T // tm, (3, 4), after=False)
                 + [full((1, SG_WIDTH)), full((SG_HEADS, SG_CHUNK, SG_CHUNK)), full((SG_HEADS, SG_CHUNK, SG_DIM)), full((SUBLANES, SC_WIDTH))],
                 out_specs=BS((tm, D_MODEL), lambda i: (i, 0)), out_shape=_sds((T, D_MODEL), BF16))(
        proj, proj, proj, ln_g, w_tril, b_lanes, conv_w)


def _even_mixer_bwd(proj, dmix, ln_g, w_tril, b_lanes, conv_w, seq, tm):
    T = proj.shape[0]
    n_tiles, tiles_per_seq = T // tm, seq // tm

    def body(p_ref, dm_ref, hc_ref, hh_ref, nd_ref, nb_ref, lng_ref, w_ref, bb_ref, cw_ref,
             dp_ref, dw_ref, db_ref, dlng_ref, dcw_ref):
        i = pl.program_id(0)
        first = i % tiles_per_seq == 0
        last = i % tiles_per_seq == tiles_per_seq - 1

        @pl.when(i == 0)
        def _():
            dw_ref[...] = jnp.zeros_like(dw_ref)
            db_ref[...] = jnp.zeros_like(db_ref)
            dlng_ref[...] = jnp.zeros_like(dlng_ref)
            dcw_ref[...] = jnp.zeros_like(dcw_ref)

        for h in range(SG_HEADS):
            cols = slice(SG_DIM * h, SG_DIM * (h + 1))
            vcols = slice(SG_WIDTH + SG_DIM * h, SG_WIDTH + SG_DIM * (h + 1))
            lng = lng_ref[:, cols]
            for k in range(tm // SG_CHUNK):
                rows = slice(SG_CHUNK * k, SG_CHUNK * (k + 1))
                gelu_v, dgelu_v = _gelu_and_grad(p_ref[rows, vcols])
                vhat, rstd = _layer_norm_head(gelu_v)
                vln = (vhat * lng).astype(BF16)
                mixed = _dot(w_ref[h], vln) + bb_ref[h]
                gelu_u, dgelu_u = _gelu_and_grad(p_ref[rows, cols])
                da = dm_ref[rows, cols]
                dp_ref[rows, cols] = (da * mixed * dgelu_u).astype(BF16)
                dmixed = da * gelu_u
                db_ref[h] += dmixed
                dw_ref[h] += _dot(dmixed, vln, "nt")
                dvln = _dot(w_ref[h], dmixed, "tn")
                dlng_ref[:, cols] += jnp.sum(dvln * vhat, axis=0, keepdims=True)
                dvhat = dvln * lng
                dgv = rstd * (dvhat - jnp.mean(dvhat, axis=-1, keepdims=True)
                              - vhat * jnp.mean(dvhat * vhat, axis=-1, keepdims=True))
                dp_ref[rows, vcols] = (dgv * dgelu_v).astype(BF16)

        b = p_ref[:, 1024:1536]
        c = p_ref[:, 1536:2048]
        hv = p_ref[:, 2048:2560]
        z = c * hv
        zz = jnp.concatenate([jnp.where(first, 0.0, hc_ref[...] * hh_ref[...]), z], axis=0)
        z1 = _shift_down(zz, 1)[SUBLANES:]
        z2 = _shift_down(zz, 2)[SUBLANES:]
        w0, w1, w2 = cw_ref[0:1, :], cw_ref[1:2, :], cw_ref[2:3, :]
        dbo = dm_ref[:, SG_WIDTH:]
        dy = dbo * b
        dd = jnp.concatenate([dy, jnp.where(last, 0.0, nd_ref[...] * nb_ref[...])], axis=0)
        dz = w2 * dy + w1 * _shift_up(dd, 1)[:tm] + w0 * _shift_up(dd, 2)[:tm]
        dp_ref[:, 1024:1536] = (dbo * (w0 * z2 + w1 * z1 + w2 * z)).astype(BF16)
        dp_ref[:, 1536:2048] = (dz * hv).astype(BF16)
        dp_ref[:, 2048:2560] = (dz * c).astype(BF16)
        dcw_ref[0:1, :] += jnp.sum(dy * z2, axis=0, keepdims=True)
        dcw_ref[1:2, :] += jnp.sum(dy * z1, axis=0, keepdims=True)
        dcw_ref[2:3, :] += jnp.sum(dy * z, axis=0, keepdims=True)

        @pl.when(i == n_tiles - 1)
        def _():
            t_idx = lax.broadcasted_iota(jnp.int32, (SG_CHUNK, SG_CHUNK), 0)
            s_idx = lax.broadcasted_iota(jnp.int32, (SG_CHUNK, SG_CHUNK), 1)
            for h in range(SG_HEADS):
                dw_ref[h] = jnp.where(t_idx >= s_idx, dw_ref[h], 0.0)

    full = lambda shape: BS(shape, lambda i: (0,) * len(shape))
    sq = (SG_HEADS, SG_CHUNK, SG_CHUNK)
    return _call(body, name="even_mixer_bwd", grid=(n_tiles,),
                 in_specs=[BS((tm, EVEN_IN), lambda i: (i, 0)), BS((tm, D_MODEL), lambda i: (i, 0))]
                 + _even_halo_specs(tm, n_tiles, (3, 4), after=False)
                 + _even_halo_specs(tm, n_tiles, (1,), after=True) + _even_halo_specs(tm, n_tiles, (2,), after=True)
                 + [full((1, SG_WIDTH)), full(sq), full(sq), full((SUBLANES, SC_WIDTH))],
                 out_specs=[BS((tm, EVEN_IN), lambda i: (i, 0)), full(sq), full(sq), full((1, SG_WIDTH)), full((SUBLANES, SC_WIDTH))],
                 out_shape=[_sds((T, EVEN_IN), BF16), _sds(sq, F32), _sds(sq, F32), _sds((1, SG_WIDTH), F32), _sds((SUBLANES, SC_WIDTH), F32)])(
        proj, dmix, proj, proj, dmix, proj, ln_g, w_tril, b_lanes, conv_w)


def _pool_select(vals):
    lane = lax.broadcasted_iota(jnp.int32, vals[0].shape, 1)
    out = vals[-1]
    for g in range(len(vals) - 2, -1, -1):
        out = jnp.where(lane < POOL_DIM * (g + 1), vals[g], out)
    return out


def _pool_counts(pos1):
    lane = lax.broadcasted_iota(jnp.int32, (pos1.shape[0], POOL_WIDTH), 1)
    win = _pool_select([jnp.full(lane.shape, float(w), F32) for w in POOL_WINDOWS])
    return jnp.minimum(pos1, win)


def _pool_means(zz, counts):
    s2 = zz + _shift_down(zz, 1)
    s4 = s2 + _shift_down(s2, 2)
    s8 = s4 + _shift_down(s4, 4)
    s16 = s8 + _shift_down(s8, 8)
    return _pool_select([s2, s4, s8, s16])[POOL_HALO:] / counts


def _pool_halo_spec(tm, n_tiles, after):
    rows = tm // POOL_HALO
    if after:
        return BS((POOL_HALO, POOL_WIDTH), lambda i: (jnp.minimum((i + 1) * rows, n_tiles * rows - 1), 0))
    return BS((POOL_HALO, POOL_WIDTH), lambda i: (jnp.maximum(i * rows - 1, 0), 0))


def _pool_fwd(proj, w_diag, scale, seq, tm):
    T = proj.shape[0]
    tiles_per_seq = seq // tm

    def body(z_ref, zh_ref, w_ref, s_ref, o_ref):
        t = pl.program_id(0) % tiles_per_seq
        z = z_ref[...]
        zz = jnp.concatenate([jnp.where(t == 0, 0.0, zh_ref[...]), z], axis=0)
        pos1 = (lax.broadcasted_iota(jnp.int32, (tm, 1), 0) + (t * tm + 1)).astype(F32)
        pooled = _pool_means(zz, _pool_counts(pos1)) - z
        o_ref[...] = (_dot(pooled, w_ref[...]) * s_ref[...]).astype(BF16)

    full = lambda shape: BS(shape, lambda i: (0,) * len(shape))
    return _call(body, name="pool_fwd", grid=(T // tm,),
                 in_specs=[BS((tm, POOL_WIDTH), lambda i: (i, 0)), _pool_halo_spec(tm, T // tm, False),
                           full((POOL_WIDTH, POOL_WIDTH)), full((1, POOL_WIDTH))],
                 out_specs=BS((tm, POOL_WIDTH), lambda i: (i, 0)), out_shape=_sds((T, D_MODEL), BF16))(proj, proj, w_diag, scale)


def _pool_bwd(proj, dmix, w_diag, scale, seq, tm):
    T = proj.shape[0]
    n_tiles, tiles_per_seq = T // tm, seq // tm

    def body(z_ref, zh_ref, do_ref, don_ref, w_ref, s_ref, dz_ref, dw_ref, ds_ref):
        i = pl.program_id(0)
        t = i % tiles_per_seq

        @pl.when(i == 0)
        def _():
            dw_ref[...] = jnp.zeros_like(dw_ref)
            ds_ref[...] = jnp.zeros_like(ds_ref)

        z = z_ref[...]
        zz = jnp.concatenate([jnp.where(t == 0, 0.0, zh_ref[...]), z], axis=0)
        pos1 = (lax.broadcasted_iota(jnp.int32, (tm, 1), 0) + (t * tm + 1)).astype(F32)
        counts = _pool_counts(pos1)
        pooled = _pool_means(zz, counts) - z
        dout = do_ref[...].astype(F32)
        ds_ref[...] += jnp.sum(dout * _dot(pooled, w_ref[...]), axis=0, keepdims=True)
        dlin = dout * s_ref[...]
        dw_ref[...] += _dot(pooled, dlin, "tn")
        dpooled = _dot(dlin, w_ref[...], "nt")
        dpooled_n = _dot(don_ref[...].astype(F32) * s_ref[...], w_ref[...], "nt")
        pos1_n = (lax.broadcasted_iota(jnp.int32, (POOL_HALO, 1), 0) + ((t + 1) * tm + 1)).astype(F32)
        dmean_n = jnp.where(t == tiles_per_seq - 1, 0.0, dpooled_n / _pool_counts(pos1_n))
        dd = jnp.concatenate([dpooled / counts, dmean_n], axis=0)
        r2 = dd + _shift_up(dd, 1)
        r4 = r2 + _shift_up(r2, 2)
        r8 = r4 + _shift_up(r4, 4)
        r16 = r8 + _shift_up(r8, 8)
        dz_ref[...] = (_pool_select([r2, r4, r8, r16])[:tm] - dpooled).astype(BF16)

    full = lambda shape: BS(shape, lambda i: (0,) * len(shape))
    return _call(body, name="pool_bwd", grid=(n_tiles,),
                 in_specs=[BS((tm, POOL_WIDTH), lambda i: (i, 0)), _pool_halo_spec(tm, n_tiles, False),
                           BS((tm, POOL_WIDTH), lambda i: (i, 0)), _pool_halo_spec(tm, n_tiles, True),
                           full((POOL_WIDTH, POOL_WIDTH)), full((1, POOL_WIDTH))],
                 out_specs=[BS((tm, POOL_WIDTH), lambda i: (i, 0)), full((POOL_WIDTH, POOL_WIDTH)), full((1, POOL_WIDTH))],
                 out_shape=[_sds((T, POOL_WIDTH), BF16), _sds((POOL_WIDTH, POOL_WIDTH), F32), _sds((1, POOL_WIDTH), F32)])(
        proj, proj, dmix, dmix, w_diag, scale)


def _rope_partner(r):
    lane = lax.broadcasted_iota(jnp.int32, r.shape, 1)
    return jnp.where(lane < QK_ROPE // 2, pltpu.roll(r, LANES - QK_ROPE // 2, 1), pltpu.roll(r, QK_ROPE // 2, 1))


def _rope(x, cos, sin_signed):
    r = x[:, QK_NOPE:]
    return jnp.concatenate([x[:, :QK_NOPE], r * cos + _rope_partner(r) * sin_signed], axis=1)


def _rope_transposed(dx, cos, sin_signed):
    dr = dx[:, QK_NOPE:]
    return jnp.concatenate([dx[:, :QK_NOPE], dr * cos + _rope_partner(dr * sin_signed)], axis=1)


def _head_norm(x):
    r = lax.rsqrt(jnp.sum(x * x, axis=-1, keepdims=True) * (1.0 / QK_DIM) + EPS)
    return x * r, r


def _head_norm_bwd(dy, xhat, r, gain):
    dxhat = dy * gain
    return r * (dxhat - xhat * (jnp.sum(dxhat * xhat, axis=-1, keepdims=True) * (1.0 / QK_DIM)))


def _latents(p_ref, qag_ref, kvag_ref):
    ql = p_ref[:, POOL_WIDTH:POOL_WIDTH + Q_LORA]
    kvl = p_ref[:, POOL_WIDTH + Q_LORA:POOL_WIDTH + Q_LORA + KV_LORA]
    rq = lax.rsqrt(jnp.mean(ql * ql, axis=-1, keepdims=True) + EPS)
    rkv = lax.rsqrt(jnp.mean(kvl * kvl, axis=-1, keepdims=True) + EPS)
    return ql * rq, rq, kvl * rkv, rkv


def _mla_specs(tm):
    full = lambda shape: BS(shape, lambda i, h: (0,) * len(shape))
    return [BS((tm, ODD_IN_PAD), lambda i, h: (i, 0)), BS((tm, LANES), lambda i, h: (i, 0)), BS((tm, LANES), lambda i, h: (i, 0)),
            full((1, Q_LORA)), full((1, KV_LORA)), BS((None, Q_LORA, QK_PAD), lambda i, h: (h, 0, 0)),
            BS((None, KV_LORA, QK_PAD), lambda i, h: (h, 0, 0)), full((1, QK_PAD)), full((1, QK_PAD))]


def _mla_qkv_fwd(proj, cos, sin_signed, qa_g, kva_g, q_b, kv_b, q_g, k_g, tm):
    T = proj.shape[0]

    def body(p_ref, cos_ref, sin_ref, qag_ref, kvag_ref, qb_ref, kvb_ref, qg_ref, kg_ref, q_ref, k_ref, v_ref, qn_s, kvn_s):
        @pl.when(pl.program_id(1) == 0)
        def _():
            qhat, _, kvhat, _ = _latents(p_ref, qag_ref, kvag_ref)
            qn_s[...] = (qhat * qag_ref[...]).astype(BF16)
            kvn_s[...] = (kvhat * kvag_ref[...]).astype(BF16)

        cos, sin = cos_ref[...], sin_ref[...]
        qhat, _ = _head_norm(_dot(qn_s[...], qb_ref[...]))
        q_ref[...] = _rope(qhat * qg_ref[...], cos, sin).astype(BF16)
        kv = _dot(kvn_s[...], kvb_ref[...])
        khat, _ = _head_norm(jnp.concatenate([kv[:, :QK_NOPE], p_ref[:, ODD_IN_PAD - LANES:]], axis=1))
        k_ref[...] = _rope(khat * kg_ref[...], cos, sin).astype(BF16)
        v_ref[...] = kv[:, QK_NOPE:].astype(BF16)

    qk_spec = BS((None, tm, QK_PAD), lambda i, h: (h, i, 0))
    return _call(body, name="mla_qkv_fwd", grid=(T // tm, HEADS), in_specs=_mla_specs(tm),
                 out_specs=[qk_spec, qk_spec, BS((None, tm, V_DIM), lambda i, h: (h, i, 0))],
                 out_shape=[_sds((HEADS, T, QK_PAD), BF16), _sds((HEADS, T, QK_PAD), BF16), _sds((HEADS, T, V_DIM), BF16)],
                 scratch=[pltpu.VMEM((tm, Q_LORA), BF16), pltpu.VMEM((tm, KV_LORA), BF16)])(
        proj, cos, sin_signed, qa_g, kva_g, q_b, kv_b, q_g, k_g)


def _mla_qkv_bwd(proj, cos, sin_signed, qa_g, kva_g, q_b, kv_b, q_g, k_g, dq, dk, dv, dz_pool, tm):
    T = proj.shape[0]
    n_tiles = T // tm
    chain_rows = min(_PASS_ROWS, tm)

    def body(p_ref, cos_ref, sin_ref, qag_ref, kvag_ref, qb_ref, kvb_ref, qg_ref, kg_ref, dq_ref, dk_ref, dv_ref, dzp_ref,
             dp_ref, dqb_ref, dkvb_ref, dqg_ref, dkg_ref, dqag_ref, dkvag_ref, qn_s, kvn_s, dqn_s, dkvn_s, dkr_s,
             qh_s, kv_s, dqh_s, dkv_s):
        i, h = pl.program_id(0), pl.program_id(1)

        @pl.when((i == 0) & (h == 0))
        def _():
            for ref in (dqb_ref, dkvb_ref, dqg_ref, dkg_ref, dqag_ref, dkvag_ref):
                ref[...] = jnp.zeros_like(ref)

        @pl.when(h == 0)
        def _():
            qhat, _, kvhat, _ = _latents(p_ref, qag_ref, kvag_ref)
            qn_s[...] = (qhat * qag_ref[...]).astype(BF16)
            kvn_s[...] = (kvhat * kvag_ref[...]).astype(BF16)
            dqn_s[...] = jnp.zeros_like(dqn_s)
            dkvn_s[...] = jnp.zeros_like(dkvn_s)
            dkr_s[...] = jnp.zeros_like(dkr_s)

        qh_s[...] = _dot(qn_s[...], qb_ref[...])
        kv_s[...] = _dot(kvn_s[...], kvb_ref[...])
        qg, kg = qg_ref[...], kg_ref[...]

        def chunk(c, gains):
            dqg, dkg = gains
            rows = slice(c * chain_rows, (c + 1) * chain_rows)
            cos, sin = cos_ref[rows, :], sin_ref[rows, :]
            qhat, rq = _head_norm(qh_s[rows, :])
            dqn_head = _rope_transposed(dq_ref[rows, :], cos, sin)
            dqh_s[rows, :] = _head_norm_bwd(dqn_head, qhat, rq, qg).astype(BF16)
            kv = kv_s[rows, :]
            khat, rk = _head_norm(jnp.concatenate([kv[:, :QK_NOPE], p_ref[rows, ODD_IN_PAD - LANES:]], axis=1))
            dkn_head = _rope_transposed(dk_ref[rows, :], cos, sin)
            dkf = _head_norm_bwd(dkn_head, khat, rk, kg)
            dkr_s[rows, :] += dkf[:, QK_NOPE:]
            dkv_s[rows, :] = jnp.concatenate([dkf[:, :QK_NOPE], dv_ref[rows, :]], axis=1).astype(BF16)
            return dqg + dqn_head * qhat, dkg + dkn_head * khat

        dqg = dkg = jnp.zeros((chain_rows, QK_PAD), F32)
        for c in range(tm // chain_rows):
            dqg, dkg = chunk(c, (dqg, dkg))
        dqg_ref[...] += jnp.sum(dqg, axis=0, keepdims=True)
        dkg_ref[...] += jnp.sum(dkg, axis=0, keepdims=True)
        dqb_ref[h] += _dot(qn_s[...], dqh_s[...], "tn")
        dqn_s[...] += _dot(dqh_s[...], qb_ref[...], "nt")
        dkvb_ref[h] += _dot(kvn_s[...], dkv_s[...], "tn")
        dkvn_s[...] += _dot(dkv_s[...], kvb_ref[...], "nt")

        @pl.when(h == HEADS - 1)
        def _():
            qhat_l, rql, kvhat_l, rkvl = _latents(p_ref, qag_ref, kvag_ref)
            dqn, dkvn = dqn_s[...], dkvn_s[...]
            dqag_ref[...] += jnp.sum(dqn * qhat_l, axis=0, keepdims=True)
            dkvag_ref[...] += jnp.sum(dkvn * kvhat_l, axis=0, keepdims=True)
            dqx, dkvx = dqn * qag_ref[...], dkvn * kvag_ref[...]
            dp_ref[:, :POOL_WIDTH] = dzp_ref[...]
            dp_ref[:, POOL_WIDTH:POOL_WIDTH + Q_LORA] = (
                rql * (dqx - qhat_l * jnp.mean(dqx * qhat_l, axis=-1, keepdims=True))).astype(BF16)
            dp_ref[:, POOL_WIDTH + Q_LORA:ODD_IN_PAD - LANES] = (
                rkvl * (dkvx - kvhat_l * jnp.mean(dkvx * kvhat_l, axis=-1, keepdims=True))).astype(BF16)
            dp_ref[:, ODD_IN_PAD - LANES:] = dkr_s[:, :QK_ROPE].astype(BF16)

    full = lambda shape: BS(shape, lambda i, h: (0,) * len(shape))
    qk_spec = BS((None, tm, QK_PAD), lambda i, h: (h, i, 0))
    return _call(body, name="mla_qkv_bwd", grid=(n_tiles, HEADS),
                 in_specs=_mla_specs(tm) + [qk_spec, qk_spec, BS((None, tm, V_DIM), lambda i, h: (h, i, 0)),
                                            BS((tm, POOL_WIDTH), lambda i, h: (i, 0))],
                 out_specs=[BS((tm, ODD_IN), lambda i, h: (i, 0)), full((HEADS, Q_LORA, QK_PAD)), full((HEADS, KV_LORA, QK_PAD)),
                            full((1, QK_PAD)), full((1, QK_PAD)), full((1, Q_LORA)), full((1, KV_LORA))],
                 out_shape=[_sds((T, ODD_IN), BF16),_sds((HEADS, Q_LORA, QK_PAD), F32), _sds((HEADS, KV_LORA, QK_PAD), F32),
                            _sds((1, QK_PAD), F32), _sds((1, QK_PAD), F32), _sds((1, Q_LORA), F32), _sds((1, KV_LORA), F32)],
                 scratch=[pltpu.VMEM((tm, Q_LORA), BF16), pltpu.VMEM((tm, KV_LORA), BF16), pltpu.VMEM((tm, Q_LORA), F32),
                          pltpu.VMEM((tm, KV_LORA), F32), pltpu.VMEM((tm, LANES), F32), pltpu.VMEM((tm, QK_PAD), F32),
                          pltpu.VMEM((tm, QK_PAD), F32), pltpu.VMEM((tm, QK_PAD), BF16), pltpu.VMEM((tm, QK_PAD), BF16)])(
        proj, cos, sin_signed, qa_g, kva_g, q_b, kv_b, q_g, k_g, dq, dk, dv, dz_pool)


def _attn_tile(seq):
    return 512 if seq % 512 == 0 else seq


def _causal_mask(s):
    row = lax.broadcasted_iota(jnp.int32, s.shape, 0)
    col = lax.broadcasted_iota(jnp.int32, s.shape, 1)
    return jnp.where(row >= col, s, NEG_INF)


def _tile(i, t):
    return slice(i * t, (i + 1) * t)


def _flash_fwd(q, k, v, mix, batch, seq):
    t = _attn_tile(seq)
    nq = seq // t

    def body(q_ref, k_ref, v_ref, _, o_ref, lse_ref):
        for qi in range(nq):
            rows, before = _tile(qi, t), slice(0, qi * t)
            qv = q_ref[rows, :]
            s_diag = _causal_mask(_dot(qv, k_ref[rows, :], "nt") * ATTN_SCALE)
            m = jnp.max(s_diag, axis=-1, keepdims=True)
            if qi:
                s_before = _dot(qv, k_ref[before, :], "nt") * ATTN_SCALE
                m = jnp.maximum(m, jnp.max(s_before, axis=-1, keepdims=True))
            p = jnp.exp(s_diag - m)
            l = jnp.sum(p, axis=-1, keepdims=True)
            acc = _dot(p, v_ref[rows, :])
            if qi:
                p = jnp.exp(s_before - m)
                l = l + jnp.sum(p, axis=-1, keepdims=True)
                acc = acc + _dot(p, v_ref[before, :])
            o_ref[rows, :] = (acc / l).astype(BF16)
            lse_ref[rows, :] = jnp.broadcast_to(m + jnp.log(l), (t, LANES))

    T = batch * seq
    whole = lambda w: BS((None, seq, w), lambda b, h: (h, b, 0))
    return _call(body, name="flash_fwd", grid=(batch, HEADS),
                 in_specs=[whole(QK_PAD), whole(QK_PAD), whole(V_DIM), pl.BlockSpec(memory_space=pl.ANY)],
                 out_specs=[BS((seq, V_DIM), lambda b, h: (b, POOL_WIDTH // V_DIM + h)), whole(LANES)],
                 out_shape=[_sds((T, D_MODEL), BF16), _sds((HEADS, T, LANES), F32)],
                 aliases={3: 0})(q, k, v, mix)


def _flash_bwd(q, k, v, dmix, mix, lse, batch, seq):
    t = _attn_tile(seq)
    nq = seq // t

    def body(q_ref, k_ref, v_ref, do_ref, o_ref, lse_ref, dq_ref, dk_ref, dv_ref):
        for qi in range(nq):
            rows, before = _tile(qi, t), slice(0, qi * t)
            qv, do = q_ref[rows, :], do_ref[rows, :]
            lse = lse_ref[rows, 0:1]
            delta = jnp.sum(do.astype(F32) * o_ref[rows, :].astype(F32), axis=-1, keepdims=True)

            def block(keys, masked):
                kk = k_ref[keys, :]
                s = _dot(qv, kk, "nt") * ATTN_SCALE
                p = jnp.exp((_causal_mask(s) if masked else s) - lse)
                ds = p * (_dot(do, v_ref[keys, :], "nt") - delta) * ATTN_SCALE
                return _dot(p, do, "tn"), _dot(ds, qv, "tn"), _dot(ds, kk)

            dv_ref[rows, :], dk_ref[rows, :], dq = block(rows, True)
            if qi:
                dv, dk, dq_before = block(before, False)
                dv_ref[before, :] += dv
                dk_ref[before, :] += dk
                dq = dq + dq_before
            dq_ref[rows, :] = dq

    T = batch * seq
    whole = lambda w: BS((None, seq, w), lambda b, h: (h, b, 0))
    head_cols = BS((seq, V_DIM), lambda b, h: (b, POOL_WIDTH // V_DIM + h))
    return _call(body, name="flash_bwd", grid=(batch, HEADS),
                 in_specs=[whole(QK_PAD), whole(QK_PAD), whole(V_DIM), head_cols, head_cols, whole(LANES)],
                 out_specs=[whole(QK_PAD), whole(QK_PAD), whole(V_DIM)],
                 out_shape=[_sds((HEADS, T, QK_PAD), F32), _sds((HEADS, T, QK_PAD), F32), _sds((HEADS, T, V_DIM), F32)])(
        q, k, v, dmix, mix, lse)


def _adamw_math(w, g, m, v):
    m = ADAM_B1 * m + (1.0 - ADAM_B1) * g
    v = ADAM_B2 * v + (1.0 - ADAM_B2) * (g * g)
    m_hat = m / (1.0 - ADAM_B1 ** ADAM_STEP)
    v_hat = v / (1.0 - ADAM_B2 ** ADAM_STEP)
    return -ADAM_LR * (m_hat / (jnp.sqrt(v_hat) + ADAM_EPS) + ADAM_WD * w), m, v


def _adamw(name, w, g, m, v, l=0, prev=()):
    L, R, C = w.shape
    tr = 256 if R % 256 == 0 else R

    def body(w_ref, g_ref, m_ref, v_ref, *rest):
        go_ref, d_ref, mo_ref, vo_ref, token = rest[-5:]
        gv = g_ref[...]
        d_ref[...], mo_ref[...], vo_ref[...] = _adamw_math(w_ref[...], gv, m_ref[...], v_ref[...])
        go_ref[...] = gv
        token[...] = jnp.zeros_like(token)

    layer = BS((None, tr, C), lambda i: (l, i, 0))
    return _call(body, name=f"{name}_{l}", grid=(R // tr,),
                 in_specs=[layer, BS((tr, C), lambda i: (i, 0)), layer, layer] + [pl.BlockSpec(memory_space=pl.ANY)] * len(prev),
                 out_specs=[layer] * 4 + [BS((SUBLANES, LANES), lambda i: (0, 0))],
                 out_shape=[_sds((L, R, C), F32)] * 4 + [_sds((SUBLANES, LANES), F32)],
                 aliases={4 + n: n for n in range(len(prev))})(w, g, m, v, *prev)


def _place():
    x, y, c = lax.axis_index("x"), lax.axis_index("y"), lax.axis_index("c")
    other_chips = [(1 - x, y), (x, 1 - y), (1 - x, 1 - y)]
    return x, y, c, other_chips


def _remote(src, dst, send_sem, recv_sem, dev):
    return pltpu.make_async_remote_copy(src_ref=src, dst_ref=dst, send_sem=send_sem, recv_sem=recv_sem,
                                        device_id=dev, device_id_type=MESH)


def _prefetch_call(body, *, name, grid, in_specs, out_specs, out_shape):
    grid_spec = pltpu.PrefetchScalarGridSpec(num_scalar_prefetch=1, grid=grid, in_specs=in_specs, out_specs=out_specs)
    params = pltpu.CompilerParams(vmem_limit_bytes=VMEM_LIMIT_V7X, dimension_semantics=("arbitrary",) * len(grid))
    return pl.pallas_call(body, name=name, grid_spec=grid_spec, out_shape=out_shape, compiler_params=params)


def _row_tile(rows):
    return 256 if rows % 256 == 0 else rows


def _cast_place(name, w, layer, chip, after=()):
    _, _, rows, C = w.shape
    tr = _row_tile(rows)

    def body(chip_ref, w_ref, *rest):
        rest[-1][...] = w_ref[...].astype(BF16)

    return _prefetch_call(body, name=name, grid=(2, rows // tr),
                          in_specs=[BS((None, None, tr, C), lambda h, i, chip_ref: (layer, h, i, 0))]
                          + [pl.BlockSpec(memory_space=pl.ANY)] * len(after),
                          out_specs=BS((None, None, tr, C), lambda h, i, chip_ref: (chip_ref[0], h, i, 0)),
                          out_shape=pltpu.HBM((N_CHIPS, 2, rows, C), BF16))(chip, w, *after)


SEM = pl.BlockSpec(memory_space=pltpu.SEMAPHORE)


def _split_copy_call(body, *, name, in_specs, out_specs, out_shape, aliases):
    return pl.pallas_call(body, name=name, in_specs=in_specs, out_specs=out_specs, out_shape=out_shape,
                          input_output_aliases=aliases,
                          compiler_params=pltpu.CompilerParams(has_side_effects=pltpu.SideEffectType.DATAFLOW_SIDE_EFFECTING))


def _hbm(arrays):
    return [pltpu.with_memory_space_constraint(a, pltpu.HBM) for a in arrays]


def _gather_send(name, gs, groups, after):
    n = len(gs)

    def body(*refs):
        g, sems, token = refs[:n], refs[n + len(after):n + len(after) + 2 * len(groups)], refs[-1]
        x, y, c, chips = _place()
        me = 2 * x + y
        for gi, members in enumerate(groups):
            for a, i in enumerate(members):
                for k, (px, py) in enumerate(chips):
                    _remote(g[i].at[me, c], g[i].at[me, c], sems[2 * gi].at[3 * a + k], sems[2 * gi + 1].at[3 * a + k],
                            (px, py, c)).start()
        token[...] = jnp.zeros_like(token)

    sem_shapes = [pltpu.SemaphoreType.DMA((3 * len(members),)) for members in groups for _ in range(2)]
    out = _split_copy_call(body, name=name, in_specs=[HBM] * n + [pl.BlockSpec(memory_space=pl.ANY)] * len(after),
                           out_specs=[SEM] * len(sem_shapes) + [HBM] * n + [VMEM],
                           out_shape=sem_shapes + [pltpu.HBM(a.shape, a.dtype) for a in gs] + [_sds((SUBLANES, LANES), F32)],
                           aliases={i: len(sem_shapes) + i for i in range(n)})(*_hbm(gs), *after)
    return out[:len(sem_shapes)], out[len(sem_shapes):-1], out[-1]


def _gather_wait(name, gs, send_sems, recv_sems, after):
    n = len(gs)

    def body(*refs):
        g, ssem, rsem = refs[:n], refs[n], refs[n + 1]
        x, y, c, chips = _place()
        me = 2 * x + y
        for a in range(n):
            for k, (px, py) in enumerate(chips):
                landed = g[a].at[2 * px + py, c]
                cp = _remote(g[a].at[me, c], landed, ssem.at[3 * a + k], rsem.at[3 * a + k], (px, py, c))
                cp.wait_recv()
                cp.wait_send()

    return _split_copy_call(body, name=name, in_specs=[HBM] * n + [SEM, SEM] + [pl.BlockSpec(memory_space=pl.ANY)] * len(after),
                            out_specs=[HBM] * n, out_shape=[pltpu.HBM(a.shape, a.dtype) for a in gs],
                            aliases={i: i for i in range(n)})(*gs, send_sems, recv_sems, *after)


def _gather_pass(name, gs):
    n = len(gs)

    def body(*refs):
        g, send_sems, recv_sems = refs[n:2 * n], refs[-2], refs[-1]
        x, y, c, chips = _place()
        sibling = (x, y, 1 - c)
        passed = [_remote(g[i].at[2 * px + py, c], g[i].at[2 * px + py, c], send_sems.at[3 * i + k], recv_sems.at[3 * i + k], sibling)
                  for i in range(n) for k, (px, py) in enumerate(chips)]
        for cp in passed:
            cp.start()
        for i in range(n):
            for k, (px, py) in enumerate(chips):
                theirs = g[i].at[2 * px + py, 1 - c]
                _remote(theirs, theirs, send_sems.at[3 * i + k], recv_sems.at[3 * i + k], sibling).wait_recv()
        for cp in passed:
            cp.wait_send()

    return _call(body, name=name, in_specs=[HBM] * n, out_specs=[HBM] * n, out_shape=[_sds(a.shape, a.dtype) for a in gs],
                 aliases={i: i for i in range(n)},
                 scratch=[pltpu.SemaphoreType.DMA((3 * n,)), pltpu.SemaphoreType.DMA((3 * n,))])(*gs)


def _scatter_send(name, ps):
    n = len(ps)

    def body(*refs):
        p, r, ssem, rsem, token = refs[:n], refs[n:2 * n], refs[2 * n], refs[2 * n + 1], refs[-1]
        x, y, c, chips = _place()
        for i in range(n):
            for k, (px, py) in enumerate(chips):
                _remote(p[i].at[2 * px + py], r[i].at[k], ssem.at[3 * i + k], rsem.at[3 * i + k], (px, py, c)).start()
        token[...] = jnp.zeros_like(token)

    lands = [lax.empty((N_CHIPS - 1,) + a.shape[1:], a.dtype) for a in ps]
    sem = pltpu.SemaphoreType.DMA((3 * n,))
    out = _split_copy_call(body, name=name, in_specs=[HBM] * (2 * n), out_specs=[SEM, SEM] + [HBM] * (2 * n) + [VMEM],
                           out_shape=[sem, sem] + [pltpu.HBM(a.shape, a.dtype) for a in list(ps) + lands] + [_sds((SUBLANES, LANES), F32)],
                           aliases={i: 2 + i for i in range(2 * n)})(*_hbm(list(ps) + lands))
    return out[0], out[1], out[2:2 + n], out[2 + n:2 + 2 * n], out[-1]


def _scatter_wait(name, ps, lands, send_sems, recv_sems, after):
    n = len(ps)

    def body(*refs):
        p, r, ssem, rsem = refs[:n], refs[n:2 * n], refs[2 * n], refs[2 * n + 1]
        x, y, c, chips = _place()
        for i in range(n):
            for k, (px, py) in enumerate(chips):
                cp = _remote(p[i].at[2 * px + py], r[i].at[k], ssem.at[3 * i + k], rsem.at[3 * i + k], (px, py, c))
                cp.wait_recv()
                cp.wait_send()

    out = _split_copy_call(body, name=name, in_specs=[HBM] * (2 * n) + [SEM, SEM] + [pl.BlockSpec(memory_space=pl.ANY)] * len(after),
                           out_specs=[HBM] * (2 * n), out_shape=[pltpu.HBM(a.shape, a.dtype) for a in list(ps) + list(lands)],
                           aliases={i: i for i in range(2 * n)})(*ps, *lands, send_sems, recv_sems, *after)
    return out[:n], out[n:]


def _exchange_send(name, gs):
    n = len(gs)

    def body(*refs):
        g, r, ssem, rsem, token = refs[:n], refs[n:2 * n], refs[2 * n], refs[2 * n + 1], refs[-1]
        x, y, c, _ = _place()
        for i in range(n):
            _remote(g[i].at[:, 1 - c], r[i], ssem.at[i], rsem.at[i], (x, y, 1 - c)).start()
        token[...] = jnp.zeros_like(token)

    lands = [lax.empty((a.shape[0],) + a.shape[2:], a.dtype) for a in gs]
    sem = pltpu.SemaphoreType.DMA((n,))
    out = _split_copy_call(body, name=name, in_specs=[HBM] * (2 * n), out_specs=[SEM, SEM] + [HBM] * (2 * n) + [VMEM],
                           out_shape=[sem, sem] + [pltpu.HBM(a.shape, a.dtype) for a in list(gs) + lands] + [_sds((SUBLANES, LANES), F32)],
                           aliases={i: 2 + i for i in range(2 * n)})(*_hbm(list(gs) + lands))
    return out[0], out[1], out[2:2 + n], out[2 + n:2 + 2 * n], out[-1]


def _exchange_wait(name, gs, lands, send_sems, recv_sems, after):
    n = len(gs)

    def body(*refs):
        g, r, ssem, rsem = refs[:n], refs[n:2 * n], refs[2 * n], refs[2 * n + 1]
        x, y, c, _ = _place()
        for i in range(n):
            cp = _remote(g[i].at[:, 1 - c], r[i], ssem.at[i], rsem.at[i], (x, y, 1 - c))
            cp.wait_recv()
            cp.wait_send()

    out = _split_copy_call(body, name=name, in_specs=[HBM] * (2 * n) + [SEM, SEM] + [pl.BlockSpec(memory_space=pl.ANY)] * len(after),
                           out_specs=[HBM] * (2 * n), out_shape=[pltpu.HBM(a.shape, a.dtype) for a in list(gs) + list(lands)],
                           aliases={i: i for i in range(2 * n)})(*gs, *lands, send_sems, recv_sems, *after)
    return out[:n], out[n:]


def _sibling_share(name, fs):
    n = len(fs)

    def body(*refs):
        f, send_sems, recv_sems = refs[n:2 * n], refs[-2], refs[-1]
        x, y, c, _ = _place()
        sends = [_remote(f[i].at[c], f[i].at[c], send_sems.at[i], recv_sems.at[i], (x, y, 1 - c)) for i in range(n)]
        for cp in sends:
            cp.start()
        for i in range(n):
            theirs = f[i].at[1 - c]
            _remote(theirs, theirs, send_sems.at[i], recv_sems.at[i], (x, y, 1 - c)).wait_recv()
        for cp in sends:
            cp.wait_send()

    return _call(body, name=name, in_specs=[HBM] * n, out_specs=[HBM] * n,
                 out_shape=[_sds(a.shape, a.dtype) for a in fs], aliases={i: i for i in range(n)},
                 scratch=[pltpu.SemaphoreType.DMA((n,)), pltpu.SemaphoreType.DMA((n,))])(*fs)


def _all_reduce_small(name, v):
    rows = v.shape[0] // 2
    halves = (2, rows, LANES)

    def body(v_ref, o_ref, from_sibling, chip_sums, send_sems, recv_sems):
        x, y, c, chips = _place()
        me, sibling = 2 * x + y, (x, y, 1 - c)
        swap = _remote(v_ref.at[1 - c], from_sibling, send_sems.at[0], recv_sems.at[0], sibling)
        swap.start()
        swap.wait()
        chip_sums[me] = v_ref[c] + from_sibling[...]
        sends = [_remote(chip_sums.at[me], chip_sums.at[me], send_sems.at[1 + k], recv_sems.at[1 + k], (px, py, c))
                 for k, (px, py) in enumerate(chips)]
        for cp in sends:
            cp.start()
        for k, (px, py) in enumerate(chips):
            theirs = chip_sums.at[2 * px + py]
            _remote(theirs, theirs, send_sems.at[1 + k], recv_sems.at[1 + k], (px, py, c)).wait_recv()
        for cp in sends:
            cp.wait_send()
        acc = chip_sums[0]
        for j in range(1, N_CHIPS):
            acc = acc + chip_sums[j]
        o_ref[c] = acc
        share = _remote(o_ref.at[c], o_ref.at[c], send_sems.at[4], recv_sems.at[4], sibling)
        share.start()
        share.wait_send()
        _remote(o_ref.at[1 - c], o_ref.at[1 - c], send_sems.at[4], recv_sems.at[4], sibling).wait_recv()

    return _call(body, name=name, in_specs=[VMEM], out_specs=VMEM, out_shape=_sds(halves, F32),
                 scratch=[pltpu.VMEM((rows, LANES), F32), pltpu.VMEM((N_CHIPS, rows, LANES), F32),
                          pltpu.SemaphoreType.DMA((5,)), pltpu.SemaphoreType.DMA((5,))])(v.reshape(halves)).reshape(v.shape)


def _add_halves(name, g, r, c):
    _, _, rows, C = g.shape
    tr = _row_tile(rows)

    def body(c_ref, g_ref, r_ref, o_ref):
        o_ref[...] = (g_ref[...] + r_ref[...]).astype(BF16)

    spec = BS((None, tr, C), lambda j, i, c_ref: (j, i, 0))
    return _prefetch_call(body, name=name, grid=(N_CHIPS, rows // tr),
                          in_specs=[BS((None, None, tr, C), lambda j, i, c_ref: (j, c_ref[0], i, 0)), spec], out_specs=spec,
                          out_shape=pltpu.HBM((N_CHIPS, rows, C), BF16))(c, g, r)


def _sum_partials(name, p, r, chip_c):
    _, rows, C = p.shape
    tr = _row_tile(rows)

    def body(s_ref, p_ref, r_ref, o_ref):
        acc = p_ref[...].astype(F32)
        for k in range(N_CHIPS - 1):
            acc = acc + r_ref[k].astype(F32)
        o_ref[...] = acc

    return _prefetch_call(body, name=name, grid=(rows // tr,),
                          in_specs=[BS((None, tr, C), lambda i, s: (s[0], i, 0)), BS((N_CHIPS - 1, tr, C), lambda i, s: (0, i, 0))],
                          out_specs=BS((None, tr, C), lambda i, s: (s[1], i, 0)), out_shape=pltpu.HBM((2, rows, C), F32))(chip_c, p, r)


_SHARDED = ("even_w_in", "even_w_out", "odd_w_in", "q_b", "kv_b", "odd_w_out", "ffn_w_gate", "ffn_w_up", "ffn_w_down")
_REPLICATED = ("mix_norm", "ffn_norm", "sg_ln_g", "sg_w_s", "sg_b_s", "pool_w", "q_norm", "k_norm")
_SMALL_SHARDED = ("sc_conv_w", "pool_scale", "q_a_norm", "kv_a_norm")
_WEIGHTS = ("mix_norm", "ffn_norm", "even_w_in", "sg_ln_g", "sg_w_s", "sg_b_s", "sc_conv_w", "even_w_out", "odd_w_in", "pool_w",
            "pool_scale", "q_a_norm", "q_b", "kv_a_norm", "kv_b", "q_norm", "k_norm", "odd_w_out", "ffn_w_gate", "ffn_w_up",
            "ffn_w_down")


def _pad_rows(flat, width, align):
    n = flat.shape[0]
    rows = -(-n // (width * align)) * align
    return jnp.pad(flat, (0, rows * width - n)).reshape(rows, width)


_GROUPS = {"even": ("even_w_in", "even_w_out"),
           "ffn0": ("ffn_w_gate0", "ffn_w_up0", "ffn_w_down0"),
           "odd": ("odd_w_in", "q_b", "kv_b", "odd_w_out"),
           "ffn1": ("ffn_w_gate1", "ffn_w_up1", "ffn_w_down1")}


def _place_shards(shards, names, chip, after):
    placed = []
    for n in names:
        weight, layer = (n[:-1], int(n[-1])) if n[-1].isdigit() else (n, 0)
        a = shards[weight]
        placed.append(_cast_place(f"place_{n}", a.reshape(a.shape[0], 2, a.shape[1] // 2, a.shape[2]), layer, chip, after))
    return placed


def _whole_weights(gathered):
    out = {n: a.reshape(N_CHIPS, -1, a.shape[-1]) for n, a in gathered.items()}
    for n in ("q_b", "kv_b"):
        if n in out:
            out[n] = out[n].transpose(1, 0, 2).reshape(out[n].shape[1], -1)
    for n in ("even_w_out", "odd_w_in", "odd_w_out"):
        if n in out:
            out[n] = out[n].reshape(-1, out[n].shape[-1])
    return out


def _forward_backward(x, positions, target, small, fetch, emit, advance):
    batch, seq, _ = x.shape
    T = batch * seq
    tm = _token_tile(seq)
    x0 = x.reshape(T, D_MODEL)

    inv_freq = ROPE_THETA ** (-jnp.arange(0, QK_ROPE, 2, dtype=F32) / QK_ROPE)
    ang = (positions.astype(F32)[..., None] * inv_freq).reshape(T, QK_ROPE // 2)
    cos, sin = jnp.cos(ang), jnp.sin(ang)
    pad = jnp.zeros((T, LANES - QK_ROPE), F32)
    cos_t = jnp.concatenate([cos, cos, pad], axis=1)
    sin_t = jnp.concatenate([-sin, sin, pad], axis=1)

    tril = jnp.tril(jnp.ones((SG_CHUNK, SG_CHUNK), bool))
    w_tril = jnp.where(tril[None], small["sg_w_s"][0], 0.0).astype(BF16)
    b_lanes = jnp.broadcast_to(small["sg_b_s"][0][:, :, None], (SG_HEADS, SG_CHUNK, SG_DIM))
    conv_w = jnp.pad(small["sc_conv_w"][0], ((0, SUBLANES - CONV_TAPS), (0, 0)))
    ln_g = small["sg_ln_g"]
    pool_diag = jnp.zeros((POOL_WIDTH, POOL_WIDTH), F32)
    for g in range(len(POOL_WINDOWS)):
        pool_diag = pool_diag.at[POOL_DIM * g:POOL_DIM * (g + 1), POOL_DIM * g:POOL_DIM * (g + 1)].set(small["pool_w"][0, g])
    pool_diag = pool_diag.astype(BF16)
    pool_scale = small["pool_scale"]
    q_g = jnp.pad(small["q_norm"], ((0, 0), (0, QK_PAD - QK_DIM)))
    k_g = jnp.pad(small["k_norm"], ((0, 0), (0, QK_PAD - QK_DIM)))
    qa_g, kva_g = small["q_a_norm"], small["kv_a_norm"]
    in_shard = EVEN_IN // N_CHIPS

    def ffn_weights(l, w):
        return w[f"ffn_w_gate{l}"], w[f"ffn_w_up{l}"], w[f"ffn_w_down{l}"]

    W = fetch("even", ())
    w_in_even = W["even_w_in"]
    h0 = _rmsnorm_fwd("mix0_norm", x0, small["mix_norm"][0], tm)
    tb = _big_tile(T)
    proj0 = _even_in(h0, w_in_even, _resident_tile(T))
    mix0 = _even_mixer_fwd(proj0, ln_g, w_tril, b_lanes, conv_w, seq, tm)
    w_out_even = W["even_w_out"]
    x1, h1 = _mm("even_out", "nn", mix0, w_out_even, F32, tk=1024, add=x0, fused=_norm_tail(small["ffn_norm"][0], T, tb))
    ffn0 = ffn_weights(0, fetch("ffn0", (x1,)))
    (x2, h2), ffn0_saved = _ffn_fwd(0, x1, h1, *ffn0, lambda tile: _norm_tail(small["mix_norm"][1], T, tile))
    W = fetch("odd", (x2,))
    w_in_odd = jnp.pad(W["odd_w_in"], ((0, 0), (0, ODD_IN_PAD - ODD_IN)))
    q_b = jnp.pad(W["q_b"].reshape(Q_LORA, HEADS, QK_DIM).transpose(1, 0, 2), ((0, 0), (0, 0), (0, QK_PAD - QK_DIM)))
    kv_b = W["kv_b"].reshape(KV_LORA, HEADS, QK_NOPE + V_DIM).transpose(1, 0, 2)
    proj1 = _mm("odd_in", "nn", h2, w_in_odd, F32, tk=1024)
    mix1 = _pool_fwd(proj1, pool_diag, pool_scale, seq, tm)
    q, k, v = _mla_qkv_fwd(proj1, cos_t, sin_t, qa_g, kva_g, q_b, kv_b, q_g, k_g, tm)
    mix1, lse = _flash_fwd(q, k, v, mix1, batch, seq)
    x3, h3 = _mm("odd_out", "nn", mix1, W["odd_w_out"], F32, tk=1024, add=x2, fused=_norm_tail(small["ffn_norm"][1], T, tb))
    ffn1 = ffn_weights(1, fetch("ffn1", (x3,)))
    (dy, sq), ffn1_saved = _ffn_fwd(1, x3, h3, *ffn1, lambda tile: _loss_tail(target.reshape(T, D_MODEL), tile))

    G = {}
    dx3, dffn_g1 = _ffn_bwd(1, x3, small["ffn_norm"][1], *ffn1, ffn1_saved, dy, emit)
    dmix1 = _mm("odd_out_dx", "nt", dx3, W["odd_w_out"], BF16, tk=1024, after=advance(dx3))
    dw_out_odd = _mm("odd_out_dw", "tn", mix1, dx3, F32, hbm_out=True)
    dq, dk, dv = _flash_bwd(q, k, v, dmix1, mix1, lse, batch, seq)
    dz_pool, dpool_diag, G["pool_scale"] = _pool_bwd(proj1, dmix1, pool_diag, pool_scale, seq, tm)
    dproj1, dq_b, dkv_b, dq_g, dk_g, G["q_a_norm"], G["kv_a_norm"] = _mla_qkv_bwd(
        proj1, cos_t, sin_t, qa_g, kva_g, q_b, kv_b, q_g, k_g, dq, dk, dv, dz_pool, tm)
    G["pool_w"] = jnp.stack([dpool_diag[POOL_DIM * g:POOL_DIM * (g + 1), POOL_DIM * g:POOL_DIM * (g + 1)]
                             for g in range(len(POOL_WINDOWS))])[None]
    G["q_norm"], G["k_norm"] = dq_g[:, :QK_DIM], dk_g[:, :QK_DIM]
    dw_in_odd = _mm("odd_in_dw", "tn", h2, dproj1, F32, tn=ODD_IN, hbm_out=True)

    def shard_major(g, cols):
        return g.reshape(g.shape[0], N_CHIPS, cols).transpose(1, 0, 2)

    behind = emit("odd", {"odd_w_in": dw_in_odd.reshape(N_CHIPS, -1, ODD_IN),
                          "q_b": shard_major(dq_b[:, :, :QK_DIM].transpose(1, 0, 2).reshape(Q_LORA, HEADS * QK_DIM), HEADS * QK_DIM // N_CHIPS),
                          "kv_b": shard_major(dkv_b.transpose(1, 0, 2).reshape(KV_LORA, HEADS * (QK_NOPE + V_DIM)),
                                              HEADS * (QK_NOPE + V_DIM) // N_CHIPS),
                          "odd_w_out": dw_out_odd.reshape(N_CHIPS, -1, D_MODEL)})
    dx2, dmix_g1 = _mm("odd_in_dx", "nt", dproj1, W["odd_w_in"], F32, tk=ODD_IN, after=behind,
                       fused=_norm_bwd_tail(x2, small["mix_norm"][1], dx3, tb))
    dx1, dffn_g0 = _ffn_bwd(0, x1, small["ffn_norm"][0], *ffn0, ffn0_saved, dx2, emit, after=advance(dx2))
    dmix0 = _mm("even_out_dx", "nt", dx1, w_out_even, F32, tk=1024, after=advance(dx1))
    dw_out_even = _mm("even_out_dw", "tn", mix0, dx1, F32, hbm_out=True)
    dproj0, dw_s, db_lanes, G["sg_ln_g"], dconv = _even_mixer_bwd(proj0, dmix0, ln_g, w_tril, b_lanes, conv_w, seq, tm)
    G["sg_w_s"] = dw_s[None]
    G["sg_b_s"] = jnp.sum(db_lanes, axis=-1)[None]
    G["sc_conv_w"] = dconv[None, :CONV_TAPS]
    tr = _resident_tile(T)
    tail, shapes, specs = _norm_bwd_tail(x0, small["mix_norm"][0], dx1, tr)
    dx0, dmix_g0 = _matmul("even_in_dx", "nt", [(dproj0, w_in_even)],
                           [(_row_spec(tr, EVEN_IN), _resident((N_CHIPS, D_MODEL, in_shard)))],
                           (T // tr, 1, 1), shapes, specs, (tr, D_MODEL), tail=tail)
    tk = min(512, T)
    dw_in_even = _grad_shards(
        "even_in_dw", h0, dproj0, BS((tk, D_MODEL), lambda k: (k, 0)), BS((tk, EVEN_IN), lambda k: (k, 0)),
        lambda a_ref, b_ref, j: (a_ref[...], b_ref[:, in_shard * j:in_shard * (j + 1)]), (N_CHIPS, D_MODEL, in_shard), T // tk)
    emit("even", {"even_w_in": dw_in_even, "even_w_out": dw_out_even.reshape(N_CHIPS, -1, D_MODEL)})
    G["mix_norm"] = jnp.concatenate([dmix_g0, dmix_g1], axis=0)
    G["ffn_norm"] = jnp.concatenate([dffn_g0, dffn_g1], axis=0)
    return sq[0, 0], dx0.reshape(batch, seq, D_MODEL), G


def _small_vector(parts, names):
    flat = jnp.concatenate([parts[n].astype(F32).reshape(-1) for n in names])
    return _pad_rows(flat, LANES, 2 * SUBLANES)


def _split_small(vec, like, names):
    out, off, flat = {}, 0, vec.reshape(-1)
    for n in names:
        size = math.prod(like[n].shape)
        out[n] = flat[off:off + size].reshape(like[n].shape)
        off += size
    return out


def _whole_shape(a):
    return a.shape[:-1] + (a.shape[-1] * N_CHIPS,)


def kernel(x, positions, mix_norm, ffn_norm, even_w_in, sg_ln_g, sg_w_s, sg_b_s, sc_conv_w, even_w_out, odd_w_in, pool_w, pool_scale, q_a_norm, q_b, kv_a_norm, kv_b, q_norm, k_norm, odd_w_out, ffn_w_gate, ffn_w_up, ffn_w_down, loss_target, m_mix_norm, m_ffn_norm, m_even_w_in, m_sg_ln_g, m_sg_w_s, m_sg_b_s, m_sc_conv_w, m_even_w_out, m_odd_w_in, m_pool_w, m_pool_scale, m_q_a_norm, m_q_b, m_kv_a_norm, m_kv_b, m_q_norm, m_k_norm, m_odd_w_out, m_ffn_w_gate, m_ffn_w_up, m_ffn_w_down, v_mix_norm, v_ffn_norm, v_even_w_in, v_sg_ln_g, v_sg_w_s, v_sg_b_s, v_sc_conv_w, v_even_w_out, v_odd_w_in, v_pool_w, v_pool_scale, v_q_a_norm, v_q_b, v_kv_a_norm, v_kv_b, v_q_norm, v_k_norm, v_odd_w_out, v_ffn_w_gate, v_ffn_w_up, v_ffn_w_down):
    args = dict(locals())
    w = {n: args[n] for n in _WEIGHTS}
    m = {n: args["m_" + n] for n in _WEIGHTS}
    v = {n: args["v_" + n] for n in _WEIGHTS}
    cx, cy, cc = lax.axis_index("x"), lax.axis_index("y"), lax.axis_index("c")
    chip = 2 * cx + cy
    transposed = ("ffn_w_gate", "ffn_w_up")
    for n in transposed:
        w[n], m[n], v[n] = (jnp.swapaxes(t[n], 1, 2) for t in (w, m, v))

    chip_arr = chip.astype(jnp.int32).reshape(1)
    c_arr = cc.astype(jnp.int32).reshape(1)
    group_names = list(_GROUPS)
    placed = {}
    for n in _SMALL_SHARDED:
        a = w[n]
        whole = jnp.zeros(a.shape[:-1] + (N_CHIPS, a.shape[-1]), F32)
        whole = lax.dynamic_update_slice_in_dim(whole, a[..., None, :], chip, axis=a.ndim - 1)
        placed[n] = jnp.where(cc == 0, whole, 0.0).reshape(_whole_shape(a))
    small_whole = _all_reduce_small("gather_small_weights", _small_vector(placed, _SMALL_SHARDED))
    small = dict({n: w[n] for n in _REPLICATED}, **_split_small(small_whole, placed, _SMALL_SHARDED))

    first, rest = list(_GROUPS[group_names[0]]), [n for g in group_names[1:] for n in _GROUPS[g]]
    sems_first, flight_first, token = _gather_send("gather_send_first", _place_shards(w, first, chip_arr, (small_whole,)),
                                                   [list(range(len(first)))], (small_whole,))
    sems_rest, flight_rest, all_sent = _gather_send("gather_send_rest", _place_shards(w, rest, chip_arr, (token,)),
                                                    [[rest.index(n) for n in _GROUPS[g]] for g in group_names[1:]], ())
    sems = list(sems_first) + list(sems_rest)
    in_flight = dict(zip(first + rest, list(flight_first) + list(flight_rest)))

    def fetch(group, after):
        gi, members = group_names.index(group), _GROUPS[group]
        after = after if gi else (all_sent,)
        landed = _gather_wait(f"gather_wait_{group}", [in_flight[n] for n in members], sems[2 * gi], sems[2 * gi + 1], after)
        return _whole_weights(dict(zip(members, _gather_pass(f"gather_pass_{group}", landed))))

    swapping, pending, arrived = [], [], {}

    def settle(after):
        names, ps, lands, send_sems, recv_sems = pending.pop()
        ps, lands = _scatter_wait(f"scatter_wait_{names[0]}", ps, lands, send_sems, recv_sems, after)
        arrived.update({n: (p, r) for n, p, r in zip(names, ps, lands)})

    def emit(group, grads):
        names = _GROUPS[group]
        halves = [grads[n].reshape(N_CHIPS, 2, grads[n].shape[1] // 2, grads[n].shape[2]) for n in names]
        send_sems, recv_sems, halves, lands, token = _exchange_send(f"exchange_send_{group}", halves)
        swapping.append((group, halves, lands, send_sems, recv_sems))
        return (token,)

    def advance(done):
        done = done if isinstance(done, tuple) else (done,)
        group, halves, lands, send_sems, recv_sems = swapping.pop()
        names = _GROUPS[group]
        halves, lands = _exchange_wait(f"exchange_wait_{group}", halves, lands, send_sems, recv_sems, done)
        partial = [_add_halves(f"add_{n}", g, r, c_arr) for n, g, r in zip(names, halves, lands)]
        if pending:
            settle(done)
        send_sems, recv_sems, ps, lands, token = _scatter_send(f"scatter_send_{group}", partial)
        pending.append((names, ps, lands, send_sems, recv_sems))
        return (token,)

    sq, grad_x, G = _forward_backward(x, positions, loss_target, small, fetch, emit, advance)
    loss = lax.psum(0.5 * sq / D_MODEL, ("x", "y", "c"))

    small_names = _REPLICATED + _SMALL_SHARDED
    summed = _split_small(_all_reduce_small("reduce_small_grads", _small_vector(G, small_names)), G, small_names)
    grads = {n: summed[n] for n in _REPLICATED}
    for n in _SMALL_SHARDED:
        a = w[n]
        grads[n] = lax.dynamic_slice_in_dim(summed[n].reshape(a.shape[:-1] + (N_CHIPS, a.shape[-1])), chip, 1,
                                            axis=a.ndim - 1).reshape(a.shape)

    chip_c = jnp.stack([chip, cc]).astype(jnp.int32)
    out = {}

    def finish(group):
        names, tokens = _GROUPS[group], []
        sums = [_sum_partials(f"sum_{n}", *arrived[n], chip_c) for n in names]
        for n, f in zip(names, _sibling_share(f"grad_share_{group}", sums)):
            weight, layer = (n[:-1], int(n[-1])) if n[-1].isdigit() else (n, 0)
            *out[weight], token = _adamw(f"adamw_{weight}", w[weight], f.reshape(-1, f.shape[-1]), m[weight], v[weight], layer,
                                         out.get(weight, ()))
            tokens.append(token)
        return tuple(tokens)

    advance(finish(group_names[3]) + finish(group_names[2]))
    settle(finish(group_names[1]))
    finish(group_names[0])
    packed = [_small_vector(d, small_names) for d in (w, grads, m, v)]
    res = _adamw("adamw_small", packed[0][None], packed[1], packed[2][None], packed[3][None])
    delta_s, m_s, v_s = (_split_small(r, w, small_names) for r in res[1:4])
    for n in small_names:
        out[n] = (grads[n], delta_s[n], m_s[n], v_s[n])
    for n in transposed:
        out[n] = tuple(jnp.swapaxes(t, 1, 2) for t in out[n])

    return (loss, grad_x, *[out[n][0] for n in _WEIGHTS], *[out[n][1] for n in _WEIGHTS],
            *[out[n][2] for n in _WEIGHTS], *[out[n][3] for n in _WEIGHTS])
```

```python
import functools
import math

import jax
import jax.numpy as jnp
from jax import lax
from jax.experimental import pallas as pl
from jax.experimental.pallas import tpu as pltpu

F32, BF16 = jnp.float32, jnp.bfloat16
BS = pl.BlockSpec

D_MODEL = 1024
EPS = 1e-6
NEG_INF = -1e30
SG_HEADS, SG_DIM, SG_WIDTH, SG_CHUNK = 4, 128, 512, 128
SC_WIDTH, CONV_TAPS = 512, 3
EVEN_IN = 2 * SG_WIDTH + 3 * SC_WIDTH
POOL_WINDOWS = (2, 4, 8, 16)
POOL_DIM, POOL_WIDTH = 64, 256
POOL_HALO = 16
HEADS, Q_LORA, KV_LORA, QK_NOPE, QK_ROPE, V_DIM = 6, 384, 256, 128, 64, 128
QK_DIM = QK_NOPE + QK_ROPE
QK_PAD = 256
ODD_IN = POOL_WIDTH + Q_LORA + KV_LORA + QK_ROPE
ODD_IN_PAD = 1024
ROPE_THETA = 10000.0
ATTN_SCALE = QK_DIM ** -0.5
D_FF, N_CHIPS = 2816, 4
FF_SHARD = D_FF // N_CHIPS
ADAM_LR, ADAM_B1, ADAM_B2, ADAM_EPS, ADAM_WD, ADAM_STEP = 0.001, 0.9, 0.999, 1e-08, 0.01, 10
VMEM_LIMIT_V7X = 48 * 2**20
LANES, SUBLANES = 128, 8
MESH = pl.DeviceIdType.MESH
HBM = pl.BlockSpec(memory_space=pltpu.HBM)
VMEM = pl.BlockSpec(memory_space=pltpu.VMEM)

_DIMS = {"nn": (((1,), (0,)), ((), ())), "nt": (((1,), (1,)), ((), ())), "tn": (((0,), (0,)), ((), ()))}


def _dot(a, b, mode="nn"):
    return lax.dot_general(a.astype(BF16), b.astype(BF16), _DIMS[mode], preferred_element_type=F32)


def _call(body, *, name, out_shape, in_specs, out_specs, grid=(), scratch=(), aliases=None, after=()):
    params = pltpu.CompilerParams(vmem_limit_bytes=VMEM_LIMIT_V7X,
                                  **({"dimension_semantics": ("arbitrary",) * len(grid)} if grid else {}))
    n_in, n_after = len(in_specs), len(after)
    kernel_body = body if not after else (lambda *refs: body(*refs[:n_in], *refs[n_in + n_after:]))
    call = pl.pallas_call(kernel_body, name=name, grid=grid, in_specs=list(in_specs) + [pl.BlockSpec(memory_space=pl.ANY)] * n_after,
                          out_specs=out_specs, out_shape=out_shape, scratch_shapes=list(scratch),
                          input_output_aliases=aliases or {}, compiler_params=params)
    return (lambda *ops: call(*ops, *after)) if after else call


def _sds(shape, dtype):
    return jax.ShapeDtypeStruct(tuple(shape), dtype)


def _token_tile(seq):
    return 512 if seq % 512 == 0 else seq


_TAIL_ROWS = 256


def _matmul(name, mode, pairs, pair_specs, grid, out_shape, out_spec, acc_shape, add=None, add_spec=None, after=(), tail=None):
    n, nk = len(pairs), grid[-1]
    n_add = int(add is not None)
    n_tail = len(tail[0]) if tail else 0
    n_in = 2 * n + n_add + n_tail
    n_out = len(out_shape) if tail else 1

    def body(*refs):
        ab = refs[:2 * n]
        add_ref = refs[2 * n] if n_add else None
        tail_refs, outs = refs[2 * n + n_add:n_in], refs[n_in:n_in + n_out]
        first = pl.program_id(0) == 0

        def finish(result):
            if tail is None:
                r = result(slice(None))
                outs[0][...] = (r if add_ref is None else r + add_ref[...]).astype(outs[0].dtype)
                return
            for lo in range(0, acc_shape[0], _TAIL_ROWS):
                rows = slice(lo, min(lo + _TAIL_ROWS, acc_shape[0]))
                r = result(rows)
                tail[2](rows, r if add_ref is None else r + add_ref[rows, :], first, tail_refs, outs)

        def terms(a_ref, b_ref):
            if len(a_ref.shape) == 2 and len(b_ref.shape) == 2:
                return [(a_ref[...], b_ref[...])]
            cols = a_ref.shape[-1] // N_CHIPS
            return [(a_ref[j] if len(a_ref.shape) == 3 else a_ref[:, cols * j:cols * (j + 1)], b_ref[j]) for j in range(N_CHIPS)]

        if nk == 1:
            r = None
            for p in range(n):
                for a_blk, b_blk in terms(ab[2 * p], ab[2 * p + 1]):
                    d = _dot(a_blk, b_blk, mode)
                    r = d if r is None else r + d
            finish(lambda rows: r[rows])
            return
        acc = refs[-1]
        k = pl.program_id(len(grid) - 1)

        @pl.when(k == 0)
        def _():
            acc[...] = jnp.zeros_like(acc)

        for p in range(n):
            acc[...] += _dot(ab[2 * p][...], ab[2 * p + 1][...], mode)

        @pl.when(k == nk - 1)
        def _():
            finish(lambda rows: acc[rows, :])

    ops = [t for pr in pairs for t in pr] + ([add] if n_add else []) + (list(tail[0]) if tail else [])
    specs = [s for pr in pair_specs for s in pr] + ([add_spec] if n_add else []) + (list(tail[1]) if tail else [])
    return _call(body, name=name, grid=grid, in_specs=specs, out_specs=out_spec, out_shape=out_shape,
                 scratch=[pltpu.VMEM(acc_shape, F32)] if nk > 1 else [], after=after)(*ops)


def _row_spec(tm, d):
    return BS((tm, d), lambda i, j, k: (i, 0))


def _vec_spec(d):
    return BS((1, d), lambda i, j, k: (0, 0))


def _norm_tail(gain, T, tm):
    d = gain.shape[-1]

    def fn(rows, r, first, tail_refs, outs):
        outs[0][rows, :] = r
        outs[1][rows, :] = (r * lax.rsqrt(jnp.mean(r * r, axis=-1, keepdims=True) + EPS) * tail_refs[0][...]).astype(BF16)

    return ([gain.reshape(1, d)], [_vec_spec(d)], fn), [_sds((T, d), F32), _sds((T, d), BF16)], [_row_spec(tm, d), _row_spec(tm, d)]


def _norm_bwd_tail(x, gain, dres, tm):
    T, d = x.shape

    def fn(rows, r, first, tail_refs, outs):
        x_ref, g_ref, dres_ref = tail_refs
        xv = x_ref[rows, :]
        rstd = lax.rsqrt(jnp.mean(xv * xv, axis=-1, keepdims=True) + EPS)
        xhat = xv * rstd
        if rows.start == 0:
            @pl.when(first)
            def _():
                outs[1][...] = jnp.zeros_like(outs[1])

        outs[1][...] += jnp.sum(r * xhat, axis=0, keepdims=True)
        dxhat = r * g_ref[...]
        outs[0][rows, :] = dres_ref[rows, :] + rstd * (dxhat - xhat * jnp.mean(dxhat * xhat, axis=-1, keepdims=True))

    return (([x, gain.reshape(1, d), dres], [_row_spec(tm, d), _vec_spec(d), _row_spec(tm, d)], fn),
            [_sds((T, d), F32), _sds((1, d), F32)], [_row_spec(tm, d), _vec_spec(d)])


def _loss_tail(target, tm):
    T, d = target.shape

    def fn(rows, r, first, tail_refs, outs):
        e = r - tail_refs[0][rows, :]
        if rows.start == 0:
            @pl.when(first)
            def _():
                outs[1][...] = jnp.zeros_like(outs[1])

        outs[1][...] += jnp.sum(e * e)
        outs[0][rows, :] = e * (1.0 / d)

    return (([target], [_row_spec(tm, d)], fn), [_sds((T, d), F32), _sds((SUBLANES, LANES), F32)],
            [_row_spec(tm, d), BS((SUBLANES, LANES), lambda i, j, k: (0, 0))])


def _grad_shards(name, a, b, a_spec, b_spec, pick, out_shape, n_steps):
    def body(a_ref, b_ref, o_ref, acc):
        k = pl.program_id(0)

        @pl.when(k == 0)
        def _():
            acc[...] = jnp.zeros_like(acc)

        for j in range(N_CHIPS):
            aj, bj = pick(a_ref, b_ref, j)
            acc[j] += _dot(aj, bj, "tn")

        @pl.when(k == n_steps - 1)
        def _():
            o_ref[...] = acc[...].astype(BF16)

    return _call(body, name=name, grid=(n_steps,), in_specs=[a_spec, b_spec], scratch=[pltpu.VMEM(tuple(out_shape), F32)],
                 out_specs=BS(out_shape, lambda k: (0, 0, 0)), out_shape=pltpu.HBM(tuple(out_shape), BF16))(a, b)


def _mm(name, mode, a, b, out_dtype, tm=1024, tn=1024, tk=512, add=None, after=(), fused=None, hbm_out=False):
    if mode == "tn":
        (K, M), N = a.shape, b.shape[1]
    else:
        (M, K), N = a.shape, (b.shape[1] if mode == "nn" else b.shape[0])
    tm, tn, tk = min(tm, M), min(tn, N), min(tk, K)
    a_spec = BS((tk, tm), lambda i, j, k: (k, i)) if mode == "tn" else BS((tm, tk), lambda i, j, k: (i, k))
    b_spec = BS((tn, tk), lambda i, j, k: (j, k)) if mode == "nt" else BS((tk, tn), lambda i, j, k: (k, j))
    o_spec = BS((tm, tn), lambda i, j, k: (i, j))
    tail, shapes, specs = fused if fused else (None, pltpu.HBM((M, N), out_dtype) if hbm_out else _sds((M, N), out_dtype), o_spec)
    return _matmul(name, mode, [(a, b)], [(a_spec, b_spec)], (M // tm, N // tn, K // tk), shapes, specs, (tm, tn),
                   add=add, add_spec=o_spec if add is not None else None, after=after, tail=tail)


def _rmsnorm_fwd(name, x, g, tm):
    T, d = x.shape

    def body(x_ref, g_ref, o_ref):
        xv = x_ref[...]
        y = xv * lax.rsqrt(jnp.mean(xv * xv, axis=-1, keepdims=True) + EPS)
        o_ref[...] = (y * g_ref[...]).astype(o_ref.dtype)

    return _call(body, name=name, grid=(T // tm,), in_specs=[BS((tm, d), lambda i: (i, 0)), BS((1, d), lambda i: (0, 0))],
                 out_specs=BS((tm, d), lambda i: (i, 0)), out_shape=_sds((T, d), BF16))(x, g.reshape(1, d))


_PASS_ROWS = 256


def _ffn_up(name, h, wg, wu, tm):
    T = h.shape[0]

    def body(h_ref, wg_ref, wu_ref, g_ref, u_ref, a_ref):
        hv = h_ref[...]
        g = _dot(hv, wg_ref[...], "nt")
        u = _dot(hv, wu_ref[...], "nt")
        g_ref[...] = g.astype(BF16)
        u_ref[...] = u.astype(BF16)
        a_ref[...] = (g * (1.0 / (1.0 + jnp.exp(-g))) * u).astype(BF16)

    w_spec = BS((None, FF_SHARD, D_MODEL), lambda j, i: (j, 0, 0))
    o_spec = BS((None, tm, FF_SHARD), lambda j, i: (j, i, 0))
    sh = _sds((N_CHIPS, T, FF_SHARD), BF16)
    return _call(body, name=name, grid=(N_CHIPS, T // tm), in_specs=[BS((tm, D_MODEL), lambda j, i: (i, 0)), w_spec, w_spec],
                 out_specs=[o_spec, o_spec, o_spec], out_shape=[sh, sh, sh])(h, wg, wu)


def _ffn_act_bwd(name, dxo, wd, g, u, tm, after=()):
    T = dxo.shape[0]

    def body(dx_ref, wd_ref, g_ref, u_ref, dg_ref, du_ref):
        da = _dot(dx_ref[...], wd_ref[...], "nt")
        g = g_ref[...].astype(F32)
        sig = 1.0 / (1.0 + jnp.exp(-g))
        dg_ref[...] = (da * u_ref[...].astype(F32) * (sig * (1.0 + g * (1.0 - sig)))).astype(BF16)
        du_ref[...] = (da * (g * sig)).astype(BF16)

    t_spec = BS((None, tm, FF_SHARD), lambda i, j: (j, i, 0))
    sh = _sds((N_CHIPS, T, FF_SHARD), BF16)
    return _call(body, name=name, grid=(T // tm, N_CHIPS),
                 in_specs=[BS((tm, D_MODEL), lambda i, j: (i, 0)), BS((None, FF_SHARD, D_MODEL), lambda i, j: (j, 0, 0)), t_spec, t_spec],
                 out_specs=[t_spec, t_spec], out_shape=[sh, sh], after=after)(dxo, wd, g, u)


def _big_tile(n):
    return min(1024, n)


def _resident_tile(n):
    return min(512, n)


def _resident(shape):
    return BS(shape, lambda i, j, k: (0,) * len(shape), pipeline_mode=pl.Buffered(1))


def _ffn_fwd(l, x, h, wg, wu, wd, fused):
    T = x.shape[0]
    g, u, a = _ffn_up(f"ffn{l}_up", h, wg, wu, _big_tile(T))
    tm = _resident_tile(T)
    tail, shapes, specs = fused(tm)
    outs = _matmul(f"ffn{l}_down", "nn", [(a, wd)],
                   [(BS((N_CHIPS, tm, FF_SHARD), lambda i, j, k: (0, i, 0)), _resident((N_CHIPS, FF_SHARD, D_MODEL)))],
                   (T // tm, 1, 1), shapes, specs, (tm, D_MODEL), add=x, add_spec=_row_spec(tm, D_MODEL), tail=tail)
    return outs, (h, g, u, a)


def _ffn_bwd(l, x, gain, wg, wu, wd, saved, dxo, emit, after=()):
    h, g, u, a = saved
    T = x.shape[0]
    tm = _big_tile(T)
    dg, du = _ffn_act_bwd(f"ffn{l}_act_bwd", dxo, wd, g, u, tm, after=after)
    tk = min(512, T)
    tn = D_MODEL
    shards_spec = BS((N_CHIPS, tk, FF_SHARD), lambda k: (0, k, 0))
    rows_spec = BS((tk, D_MODEL), lambda k: (k, 0))

    def dw(nm, act, rows):
        return _grad_shards(nm, act, rows, shards_spec, rows_spec, lambda a_ref, b_ref, j: (a_ref[j], b_ref[...]),
                            (N_CHIPS, FF_SHARD, D_MODEL), T // tk)

    behind = emit(f"ffn{l}", {f"ffn_w_gate{l}": dw(f"ffn{l}_dwg", dg, h), f"ffn_w_up{l}": dw(f"ffn{l}_dwu", du, h),
                              f"ffn_w_down{l}": dw(f"ffn{l}_dwd", a, dxo)})
    tm = _resident_tile(T)
    act_spec = BS((N_CHIPS, tm, FF_SHARD), lambda i, j, k: (0, i, 0))
    w_spec = _resident((N_CHIPS, FF_SHARD, D_MODEL))
    tail, shapes, specs = _norm_bwd_tail(x, gain, dxo, tm)
    return _matmul(f"ffn{l}_dh", "nn", [(dg, wg), (du, wu)], [(act_spec, w_spec), (act_spec, w_spec)],
                   (T // tm, 1, 1), shapes, specs, (tm, D_MODEL), after=behind, tail=tail)


_INV_SQRT2 = 1.0 / math.sqrt(2.0)
_INV_SQRT_2PI = 1.0 / math.sqrt(2.0 * math.pi)


def _gelu(x):
    return 0.5 * x * (1.0 + lax.erf(x * _INV_SQRT2))


def _gelu_and_grad(x):
    cdf = 0.5 * (1.0 + lax.erf(x * _INV_SQRT2))
    return x * cdf, cdf + x * jnp.exp(-0.5 * x * x) * _INV_SQRT_2PI


def _shift_down(x, k):
    return pltpu.roll(x, k, 0)


def _shift_up(x, k):
    return pltpu.roll(x, x.shape[0] - k, 0)


def _layer_norm_head(xh):
    xc = xh - jnp.mean(xh, axis=-1, keepdims=True)
    rstd = lax.rsqrt(jnp.mean(xc * xc, axis=-1, keepdims=True) + EPS)
    return xc * rstd, rstd


def _even_in(h, w, tm):
    T = h.shape[0]
    shard = w.shape[-1]

    def body(h_ref, w_ref, o_ref):
        hv = h_ref[...]
        for j in range(N_CHIPS):
            o_ref[:, shard * j:shard * (j + 1)] = _dot(hv, w_ref[j])

    return _call(body, name="even_in", grid=(T // tm,),
                 in_specs=[BS((tm, D_MODEL), lambda i: (i, 0)), BS(w.shape, lambda i: (0, 0, 0), pipeline_mode=pl.Buffered(1))],
                 out_specs=BS((tm, N_CHIPS * shard), lambda i: (i, 0)), out_shape=_sds((T, N_CHIPS * shard), F32))(h, w)


def _even_halo_specs(tm, n_tiles, col_blocks, after):
    rows = tm // SUBLANES
    last = n_tiles * rows - 1
    if after:
        return [BS((SUBLANES, 512), functools.partial(lambda cb, i: (jnp.minimum((i + 1) * rows, last), cb), cb)) for cb in col_blocks]
    return [BS((SUBLANES, 512), functools.partial(lambda cb, i: (jnp.maximum(i * rows - 1, 0), cb), cb)) for cb in col_blocks]


def _even_mixer_fwd(proj, ln_g, w_tril, b_lanes, conv_w, seq, tm):
    T = proj.shape[0]
    tiles_per_seq = seq // tm

    def body(p_ref, hc_ref, hh_ref, lng_ref, w_ref, bb_ref, cw_ref, o_ref):
        first = pl.program_id(0) % tiles_per_seq == 0
        for h in range(SG_HEADS):
            cols = slice(SG_DIM * h, SG_DIM * (h + 1))
            vhat, _ = _layer_norm_head(_gelu(p_ref[:, SG_WIDTH + SG_DIM * h:SG_WIDTH + SG_DIM * (h + 1)]))
            vln = (vhat * lng_ref[:, cols]).astype(BF16)
            for k in range(tm // SG_CHUNK):
                rows = slice(SG_CHUNK * k, SG_CHUNK * (k + 1))
                mixed = _dot(w_ref[h], vln[rows]) + bb_ref[h]
                o_ref[rows, cols] = (_gelu(p_ref[rows, cols]) * mixed).astype(BF16)
        z = p_ref[:, 1536:2048] * p_ref[:, 2048:2560]
        zz = jnp.concatenate([jnp.where(first, 0.0, hc_ref[...] * hh_ref[...]), z], axis=0)
        y = cw_ref[0:1, :] * _shift_down(zz, 2)[SUBLANES:] + cw_ref[1:2, :] * _shift_down(zz, 1)[SUBLANES:] + cw_ref[2:3, :] * z
        o_ref[:, SG_WIDTH:] = (p_ref[:, 1024:1536] * y).astype(BF16)

    full = lambda shape: BS(shape, lambda i: (0,) * len(shape))
    return _call(body, name="even_mixer_fwd", grid=(T // tm,),
                 in_specs=[BS((tm, EVEN_IN), lambda i: (i, 0))] + _even_halo_specs(tm, T // tm, (3, 4), after=False)
                 + [full((1, SG_WIDTH)), full((SG_HEADS, SG_CHUNK, SG_CHUNK)), full((SG_HEADS, SG_CHUNK, SG_DIM)), full((SUBLANES, SC_WIDTH))],
                 out_specs=BS((tm, D_MODEL), lambda i: (i, 0)), out_shape=_sds((T, D_MODEL), BF16))(
        proj, proj, proj, ln_g, w_tril, b_lanes, conv_w)


def _even_mixer_bwd(proj, dmix, ln_g, w_tril, b_lanes, conv_w, seq, tm):
    T = proj.shape[0]
    n_tiles, tiles_per_seq = T // tm, seq // tm

    def body(p_ref, dm_ref, hc_ref, hh_ref, nd_ref, nb_ref, lng_ref, w_ref, bb_ref, cw_ref,
             dp_ref, dw_ref, db_ref, dlng_ref, dcw_ref):
        i = pl.program_id(0)
        first = i % tiles_per_seq == 0
        last = i % tiles_per_seq == tiles_per_seq - 1

        @pl.when(i == 0)
        def _():
            dw_ref[...] = jnp.zeros_like(dw_ref)
            db_ref[...] = jnp.zeros_like(db_ref)
            dlng_ref[...] = jnp.zeros_like(dlng_ref)
            dcw_ref[...] = jnp.zeros_like(dcw_ref)

        for h in range(SG_HEADS):
            cols = slice(SG_DIM * h, SG_DIM * (h + 1))
            vcols = slice(SG_WIDTH + SG_DIM * h, SG_WIDTH + SG_DIM * (h + 1))
            lng = lng_ref[:, cols]
            for k in range(tm // SG_CHUNK):
                rows = slice(SG_CHUNK * k, SG_CHUNK * (k + 1))
                gelu_v, dgelu_v = _gelu_and_grad(p_ref[rows, vcols])
                vhat, rstd = _layer_norm_head(gelu_v)
                vln = (vhat * lng).astype(BF16)
                mixed = _dot(w_ref[h], vln) + bb_ref[h]
                gelu_u, dgelu_u = _gelu_and_grad(p_ref[rows, cols])
                da = dm_ref[rows, cols]
                dp_ref[rows, cols] = (da * mixed * dgelu_u).astype(BF16)
                dmixed = da * gelu_u
                db_ref[h] += dmixed
                dw_ref[h] += _dot(dmixed, vln, "nt")
                dvln = _dot(w_ref[h], dmixed, "tn")
                dlng_ref[:, cols] += jnp.sum(dvln * vhat, axis=0, keepdims=True)
                dvhat = dvln * lng
                dgv = rstd * (dvhat - jnp.mean(dvhat, axis=-1, keepdims=True)
                              - vhat * jnp.mean(dvhat * vhat, axis=-1, keepdims=True))
                dp_ref[rows, vcols] = (dgv * dgelu_v).astype(BF16)

        b = p_ref[:, 1024:1536]
        c = p_ref[:, 1536:2048]
        hv = p_ref[:, 2048:2560]
        z = c * hv
        zz = jnp.concatenate([jnp.where(first, 0.0, hc_ref[...] * hh_ref[...]), z], axis=0)
        z1 = _shift_down(zz, 1)[SUBLANES:]
        z2 = _shift_down(zz, 2)[SUBLANES:]
        w0, w1, w2 = cw_ref[0:1, :], cw_ref[1:2, :], cw_ref[2:3, :]
        dbo = dm_ref[:, SG_WIDTH:]
        dy = dbo * b
        dd = jnp.concatenate([dy, jnp.where(last, 0.0, nd_ref[...] * nb_ref[...])], axis=0)
        dz = w2 * dy + w1 * _shift_up(dd, 1)[:tm] + w0 * _shift_up(dd, 2)[:tm]
        dp_ref[:, 1024:1536] = (dbo * (w0 * z2 + w1 * z1 + w2 * z)).astype(BF16)
        dp_ref[:, 1536:2048] = (dz * hv).astype(BF16)
        dp_ref[:, 2048:2560] = (dz * c).astype(BF16)
        dcw_ref[0:1, :] += jnp.sum(dy * z2, axis=0, keepdims=True)
        dcw_ref[1:2, :] += jnp.sum(dy * z1, axis=0, keepdims=True)
        dcw_ref[2:3, :] += jnp.sum(dy * z, axis=0, keepdims=True)

        @pl.when(i == n_tiles - 1)
        def _():
            t_idx = lax.broadcasted_iota(jnp.int32, (SG_CHUNK, SG_CHUNK), 0)
            s_idx = lax.broadcasted_iota(jnp.int32, (SG_CHUNK, SG_CHUNK), 1)
            for h in range(SG_HEADS):
                dw_ref[h] = jnp.where(t_idx >= s_idx, dw_ref[h], 0.0)

    full = lambda shape: BS(shape, lambda i: (0,) * len(shape))
    sq = (SG_HEADS, SG_CHUNK, SG_CHUNK)
    return _call(body, name="even_mixer_bwd", grid=(n_tiles,),
                 in_specs=[BS((tm, EVEN_IN), lambda i: (i, 0)), BS((tm, D_MODEL), lambda i: (i, 0))]
                 + _even_halo_specs(tm, n_tiles, (3, 4), after=False)
                 + _even_halo_specs(tm, n_tiles, (1,), after=True) + _even_halo_specs(tm, n_tiles, (2,), after=True)
                 + [full((1, SG_WIDTH)), full(sq), full(sq), full((SUBLANES, SC_WIDTH))],
                 out_specs=[BS((tm, EVEN_IN), lambda i: (i, 0)), full(sq), full(sq), full((1, SG_WIDTH)), full((SUBLANES, SC_WIDTH))],
                 out_shape=[_sds((T, EVEN_IN), BF16), _sds(sq, F32), _sds(sq, F32), _sds((1, SG_WIDTH), F32), _sds((SUBLANES, SC_WIDTH), F32)])(
        proj, dmix, proj, proj, dmix, proj, ln_g, w_tril, b_lanes, conv_w)


def _pool_select(vals):
    lane = lax.broadcasted_iota(jnp.int32, vals[0].shape, 1)
    out = vals[-1]
    for g in range(len(vals) - 2, -1, -1):
        out = jnp.where(lane < POOL_DIM * (g + 1), vals[g], out)
    return out


def _pool_counts(pos1):
    lane = lax.broadcasted_iota(jnp.int32, (pos1.shape[0], POOL_WIDTH), 1)
    win = _pool_select([jnp.full(lane.shape, float(w), F32) for w in POOL_WINDOWS])
    return jnp.minimum(pos1, win)


def _pool_means(zz, counts):
    s2 = zz + _shift_down(zz, 1)
    s4 = s2 + _shift_down(s2, 2)
    s8 = s4 + _shift_down(s4, 4)
    s16 = s8 + _shift_down(s8, 8)
    return _pool_select([s2, s4, s8, s16])[POOL_HALO:] / counts


def _pool_halo_spec(tm, n_tiles, after):
    rows = tm // POOL_HALO
    if after:
        return BS((POOL_HALO, POOL_WIDTH), lambda i: (jnp.minimum((i + 1) * rows, n_tiles * rows - 1), 0))
    return BS((POOL_HALO, POOL_WIDTH), lambda i: (jnp.maximum(i * rows - 1, 0), 0))


def _pool_fwd(proj, w_diag, scale, seq, tm):
    T = proj.shape[0]
    tiles_per_seq = seq // tm

    def body(z_ref, zh_ref, w_ref, s_ref, o_ref):
        t = pl.program_id(0) % tiles_per_seq
        z = z_ref[...]
        zz = jnp.concatenate([jnp.where(t == 0, 0.0, zh_ref[...]), z], axis=0)
        pos1 = (lax.broadcasted_iota(jnp.int32, (tm, 1), 0) + (t * tm + 1)).astype(F32)
        pooled = _pool_means(zz, _pool_counts(pos1)) - z
        o_ref[...] = (_dot(pooled, w_ref[...]) * s_ref[...]).astype(BF16)

    full = lambda shape: BS(shape, lambda i: (0,) * len(shape))
    return _call(body, name="pool_fwd", grid=(T // tm,),
                 in_specs=[BS((tm, POOL_WIDTH), lambda i: (i, 0)), _pool_halo_spec(tm, T // tm, False),
                           full((POOL_WIDTH, POOL_WIDTH)), full((1, POOL_WIDTH))],
                 out_specs=BS((tm, POOL_WIDTH), lambda i: (i, 0)), out_shape=_sds((T, D_MODEL), BF16))(proj, proj, w_diag, scale)


def _pool_bwd(proj, dmix, w_diag, scale, seq, tm):
    T = proj.shape[0]
    n_tiles, tiles_per_seq = T // tm, seq // tm

    def body(z_ref, zh_ref, do_ref, don_ref, w_ref, s_ref, dz_ref, dw_ref, ds_ref):
        i = pl.program_id(0)
        t = i % tiles_per_seq

        @pl.when(i == 0)
        def _():
            dw_ref[...] = jnp.zeros_like(dw_ref)
            ds_ref[...] = jnp.zeros_like(ds_ref)

        z = z_ref[...]
        zz = jnp.concatenate([jnp.where(t == 0, 0.0, zh_ref[...]), z], axis=0)
        pos1 = (lax.broadcasted_iota(jnp.int32, (tm, 1), 0) + (t * tm + 1)).astype(F32)
        counts = _pool_counts(pos1)
        pooled = _pool_means(zz, counts) - z
        dout = do_ref[...].astype(F32)
        ds_ref[...] += jnp.sum(dout * _dot(pooled, w_ref[...]), axis=0, keepdims=True)
        dlin = dout * s_ref[...]
        dw_ref[...] += _dot(pooled, dlin, "tn")
        dpooled = _dot(dlin, w_ref[...], "nt")
        dpooled_n = _dot(don_ref[...].astype(F32) * s_ref[...], w_ref[...], "nt")
        pos1_n = (lax.broadcasted_iota(jnp.int32, (POOL_HALO, 1), 0) + ((t + 1) * tm + 1)).astype(F32)
        dmean_n = jnp.where(t == tiles_per_seq - 1, 0.0, dpooled_n / _pool_counts(pos1_n))
        dd = jnp.concatenate([dpooled / counts, dmean_n], axis=0)
        r2 = dd + _shift_up(dd, 1)
        r4 = r2 + _shift_up(r2, 2)
        r8 = r4 + _shift_up(r4, 4)
        r16 = r8 + _shift_up(r8, 8)
        dz_ref[...] = (_pool_select([r2, r4, r8, r16])[:tm] - dpooled).astype(BF16)

    full = lambda shape: BS(shape, lambda i: (0,) * len(shape))
    return _call(body, name="pool_bwd", grid=(n_tiles,),
                 in_specs=[BS((tm, POOL_WIDTH), lambda i: (i, 0)), _pool_halo_spec(tm, n_tiles, False),
                           BS((tm, POOL_WIDTH), lambda i: (i, 0)), _pool_halo_spec(tm, n_tiles, True),
                           full((POOL_WIDTH, POOL_WIDTH)), full((1, POOL_WIDTH))],
                 out_specs=[BS((tm, POOL_WIDTH), lambda i: (i, 0)), full((POOL_WIDTH, POOL_WIDTH)), full((1, POOL_WIDTH))],
                 out_shape=[_sds((T, POOL_WIDTH), BF16), _sds((POOL_WIDTH, POOL_WIDTH), F32), _sds((1, POOL_WIDTH), F32)])(
        proj, proj, dmix, dmix, w_diag, scale)


def _rope_partner(r):
    lane = lax.broadcasted_iota(jnp.int32, r.shape, 1)
    return jnp.where(lane < QK_ROPE // 2, pltpu.roll(r, LANES - QK_ROPE // 2, 1), pltpu.roll(r, QK_ROPE // 2, 1))


def _rope(x, cos, sin_signed):
    r = x[:, QK_NOPE:]
    return jnp.concatenate([x[:, :QK_NOPE], r * cos + _rope_partner(r) * sin_signed], axis=1)


def _rope_transposed(dx, cos, sin_signed):
    dr = dx[:, QK_NOPE:]
    return jnp.concatenate([dx[:, :QK_NOPE], dr * cos + _rope_partner(dr * sin_signed)], axis=1)


def _head_norm(x):
    r = lax.rsqrt(jnp.sum(x * x, axis=-1, keepdims=True) * (1.0 / QK_DIM) + EPS)
    return x * r, r


def _head_norm_bwd(dy, xhat, r, gain):
    dxhat = dy * gain
    return r * (dxhat - xhat * (jnp.sum(dxhat * xhat, axis=-1, keepdims=True) * (1.0 / QK_DIM)))


def _latents(p_ref, qag_ref, kvag_ref):
    ql = p_ref[:, POOL_WIDTH:POOL_WIDTH + Q_LORA]
    kvl = p_ref[:, POOL_WIDTH + Q_LORA:POOL_WIDTH + Q_LORA + KV_LORA]
    rq = lax.rsqrt(jnp.mean(ql * ql, axis=-1, keepdims=True) + EPS)
    rkv = lax.rsqrt(jnp.mean(kvl * kvl, axis=-1, keepdims=True) + EPS)
    return ql * rq, rq, kvl * rkv, rkv


def _mla_specs(tm):
    full = lambda shape: BS(shape, lambda i, h: (0,) * len(shape))
    return [BS((tm, ODD_IN_PAD), lambda i, h: (i, 0)), BS((tm, LANES), lambda i, h: (i, 0)), BS((tm, LANES), lambda i, h: (i, 0)),
            full((1, Q_LORA)), full((1, KV_LORA)), BS((None, Q_LORA, QK_PAD), lambda i, h: (h, 0, 0)),
            BS((None, KV_LORA, QK_PAD), lambda i, h: (h, 0, 0)), full((1, QK_PAD)), full((1, QK_PAD))]


def _mla_qkv_fwd(proj, cos, sin_signed, qa_g, kva_g, q_b, kv_b, q_g, k_g, tm):
    T = proj.shape[0]

    def body(p_ref, cos_ref, sin_ref, qag_ref, kvag_ref, qb_ref, kvb_ref, qg_ref, kg_ref, q_ref, k_ref, v_ref, qn_s, kvn_s):
        @pl.when(pl.program_id(1) == 0)
        def _():
            qhat, _, kvhat, _ = _latents(p_ref, qag_ref, kvag_ref)
            qn_s[...] = (qhat * qag_ref[...]).astype(BF16)
            kvn_s[...] = (kvhat * kvag_ref[...]).astype(BF16)

        cos, sin = cos_ref[...], sin_ref[...]
        qhat, _ = _head_norm(_dot(qn_s[...], qb_ref[...]))
        q_ref[...] = _rope(qhat * qg_ref[...], cos, sin).astype(BF16)
        kv = _dot(kvn_s[...], kvb_ref[...])
        khat, _ = _head_norm(jnp.concatenate([kv[:, :QK_NOPE], p_ref[:, ODD_IN_PAD - LANES:]], axis=1))
        k_ref[...] = _rope(khat * kg_ref[...], cos, sin).astype(BF16)
        v_ref[...] = kv[:, QK_NOPE:].astype(BF16)

    qk_spec = BS((None, tm, QK_PAD), lambda i, h: (h, i, 0))
    return _call(body, name="mla_qkv_fwd", grid=(T // tm, HEADS), in_specs=_mla_specs(tm),
                 out_specs=[qk_spec, qk_spec, BS((None, tm, V_DIM), lambda i, h: (h, i, 0))],
                 out_shape=[_sds((HEADS, T, QK_PAD), BF16), _sds((HEADS, T, QK_PAD), BF16), _sds((HEADS, T, V_DIM), BF16)],
                 scratch=[pltpu.VMEM((tm, Q_LORA), BF16), pltpu.VMEM((tm, KV_LORA), BF16)])(
        proj, cos, sin_signed, qa_g, kva_g, q_b, kv_b, q_g, k_g)


def _mla_qkv_bwd(proj, cos, sin_signed, qa_g, kva_g, q_b, kv_b, q_g, k_g, dq, dk, dv, dz_pool, tm):
    T = proj.shape[0]
    n_tiles = T // tm
    chain_rows = min(_PASS_ROWS, tm)

    def body(p_ref, cos_ref, sin_ref, qag_ref, kvag_ref, qb_ref, kvb_ref, qg_ref, kg_ref, dq_ref, dk_ref, dv_ref, dzp_ref,
             dp_ref, dqb_ref, dkvb_ref, dqg_ref, dkg_ref, dqag_ref, dkvag_ref, qn_s, kvn_s, dqn_s, dkvn_s, dkr_s,
             qh_s, kv_s, dqh_s, dkv_s):
        i, h = pl.program_id(0), pl.program_id(1)

        @pl.when((i == 0) & (h == 0))
        def _():
            for ref in (dqb_ref, dkvb_ref, dqg_ref, dkg_ref, dqag_ref, dkvag_ref):
                ref[...] = jnp.zeros_like(ref)

        @pl.when(h == 0)
        def _():
            qhat, _, kvhat, _ = _latents(p_ref, qag_ref, kvag_ref)
            qn_s[...] = (qhat * qag_ref[...]).astype(BF16)
            kvn_s[...] = (kvhat * kvag_ref[...]).astype(BF16)
            dqn_s[...] = jnp.zeros_like(dqn_s)
            dkvn_s[...] = jnp.zeros_like(dkvn_s)
            dkr_s[...] = jnp.zeros_like(dkr_s)

        qh_s[...] = _dot(qn_s[...], qb_ref[...])
        kv_s[...] = _dot(kvn_s[...], kvb_ref[...])
        qg, kg = qg_ref[...], kg_ref[...]

        def chunk(c, gains):
            dqg, dkg = gains
            rows = slice(c * chain_rows, (c + 1) * chain_rows)
            cos, sin = cos_ref[rows, :], sin_ref[rows, :]
            qhat, rq = _head_norm(qh_s[rows, :])
            dqn_head = _rope_transposed(dq_ref[rows, :], cos, sin)
            dqh_s[rows, :] = _head_norm_bwd(dqn_head, qhat, rq, qg).astype(BF16)
            kv = kv_s[rows, :]
            khat, rk = _head_norm(jnp.concatenate([kv[:, :QK_NOPE], p_ref[rows, ODD_IN_PAD - LANES:]], axis=1))
            dkn_head = _rope_transposed(dk_ref[rows, :], cos, sin)
            dkf = _head_norm_bwd(dkn_head, khat, rk, kg)
            dkr_s[rows, :] += dkf[:, QK_NOPE:]
            dkv_s[rows, :] = jnp.concatenate([dkf[:, :QK_NOPE], dv_ref[rows, :]], axis=1).astype(BF16)
            return dqg + dqn_head * qhat, dkg + dkn_head * khat

        dqg = dkg = jnp.zeros((chain_rows, QK_PAD), F32)
        for c in range(tm // chain_rows):
            dqg, dkg = chunk(c, (dqg, dkg))
        dqg_ref[...] += jnp.sum(dqg, axis=0, keepdims=True)
        dkg_ref[...] += jnp.sum(dkg, axis=0, keepdims=True)
        dqb_ref[h] += _dot(qn_s[...], dqh_s[...], "tn")
        dqn_s[...] += _dot(dqh_s[...], qb_ref[...], "nt")
        dkvb_ref[h] += _dot(kvn_s[...], dkv_s[...], "tn")
        dkvn_s[...] += _dot(dkv_s[...], kvb_ref[...], "nt")

        @pl.when(h == HEADS - 1)
        def _():
            qhat_l, rql, kvhat_l, rkvl = _latents(p_ref, qag_ref, kvag_ref)
            dqn, dkvn = dqn_s[...], dkvn_s[...]
            dqag_ref[...] += jnp.sum(dqn * qhat_l, axis=0, keepdims=True)
            dkvag_ref[...] += jnp.sum(dkvn * kvhat_l, axis=0, keepdims=True)
            dqx, dkvx = dqn * qag_ref[...], dkvn * kvag_ref[...]
            dp_ref[:, :POOL_WIDTH] = dzp_ref[...]
            dp_ref[:, POOL_WIDTH:POOL_WIDTH + Q_LORA] = (
                rql * (dqx - qhat_l * jnp.mean(dqx * qhat_l, axis=-1, keepdims=True))).astype(BF16)
            dp_ref[:, POOL_WIDTH + Q_LORA:ODD_IN_PAD - LANES] = (
                rkvl * (dkvx - kvhat_l * jnp.mean(dkvx * kvhat_l, axis=-1, keepdims=True))).astype(BF16)
            dp_ref[:, ODD_IN_PAD - LANES:] = dkr_s[:, :QK_ROPE].astype(BF16)

    full = lambda shape: BS(shape, lambda i, h: (0,) * len(shape))
    qk_spec = BS((None, tm, QK_PAD), lambda i, h: (h, i, 0))
    return _call(body, name="mla_qkv_bwd", grid=(n_tiles, HEADS),
                 in_specs=_mla_specs(tm) + [qk_spec, qk_spec, BS((None, tm, V_DIM), lambda i, h: (h, i, 0)),
                                            BS((tm, POOL_WIDTH), lambda i, h: (i, 0))],
                 out_specs=[BS((tm, ODD_IN), lambda i, h: (i, 0)), full((HEADS, Q_LORA, QK_PAD)), full((HEADS, KV_LORA, QK_PAD)),
                            full((1, QK_PAD)), full((1, QK_PAD)), full((1, Q_LORA)), full((1, KV_LORA))],
                 out_shape=[_sds((T, ODD_IN), BF16),_sds((HEADS, Q_LORA, QK_PAD), F32), _sds((HEADS, KV_LORA, QK_PAD), F32),
                            _sds((1, QK_PAD), F32), _sds((1, QK_PAD), F32), _sds((1, Q_LORA), F32), _sds((1, KV_LORA), F32)],
                 scratch=[pltpu.VMEM((tm, Q_LORA), BF16), pltpu.VMEM((tm, KV_LORA), BF16), pltpu.VMEM((tm, Q_LORA), F32),
                          pltpu.VMEM((tm, KV_LORA), F32), pltpu.VMEM((tm, LANES), F32), pltpu.VMEM((tm, QK_PAD), F32),
                          pltpu.VMEM((tm, QK_PAD), F32), pltpu.VMEM((tm, QK_PAD), BF16), pltpu.VMEM((tm, QK_PAD), BF16)])(
        proj, cos, sin_signed, qa_g, kva_g, q_b, kv_b, q_g, k_g, dq, dk, dv, dz_pool)


def _attn_tile(seq):
    return 512 if seq % 512 == 0 else seq


def _causal_mask(s):
    row = lax.broadcasted_iota(jnp.int32, s.shape, 0)
    col = lax.broadcasted_iota(jnp.int32, s.shape, 1)
    return jnp.where(row >= col, s, NEG_INF)


def _tile(i, t):
    return slice(i * t, (i + 1) * t)


def _flash_fwd(q, k, v, mix, batch, seq):
    t = _attn_tile(seq)
    nq = seq // t

    def body(q_ref, k_ref, v_ref, _, o_ref, lse_ref):
        for qi in range(nq):
            rows, before = _tile(qi, t), slice(0, qi * t)
            qv = q_ref[rows, :]
            s_diag = _causal_mask(_dot(qv, k_ref[rows, :], "nt") * ATTN_SCALE)
            m = jnp.max(s_diag, axis=-1, keepdims=True)
            if qi:
                s_before = _dot(qv, k_ref[before, :], "nt") * ATTN_SCALE
                m = jnp.maximum(m, jnp.max(s_before, axis=-1, keepdims=True))
            p = jnp.exp(s_diag - m)
            l = jnp.sum(p, axis=-1, keepdims=True)
            acc = _dot(p, v_ref[rows, :])
            if qi:
                p = jnp.exp(s_before - m)
                l = l + jnp.sum(p, axis=-1, keepdims=True)
                acc = acc + _dot(p, v_ref[before, :])
            o_ref[rows, :] = (acc / l).astype(BF16)
            lse_ref[rows, :] = jnp.broadcast_to(m + jnp.log(l), (t, LANES))

    T = batch * seq
    whole = lambda w: BS((None, seq, w), lambda b, h: (h, b, 0))
    return _call(body, name="flash_fwd", grid=(batch, HEADS),
                 in_specs=[whole(QK_PAD), whole(QK_PAD), whole(V_DIM), pl.BlockSpec(memory_space=pl.ANY)],
                 out_specs=[BS((seq, V_DIM), lambda b, h: (b, POOL_WIDTH // V_DIM + h)), whole(LANES)],
                 out_shape=[_sds((T, D_MODEL), BF16), _sds((HEADS, T, LANES), F32)],
                 aliases={3: 0})(q, k, v, mix)


def _flash_bwd(q, k, v, dmix, mix, lse, batch, seq):
    t = _attn_tile(seq)
    nq = seq // t

    def body(q_ref, k_ref, v_ref, do_ref, o_ref, lse_ref, dq_ref, dk_ref, dv_ref):
        for qi in range(nq):
            rows, before = _tile(qi, t), slice(0, qi * t)
            qv, do = q_ref[rows, :], do_ref[rows, :]
            lse = lse_ref[rows, 0:1]
            delta = jnp.sum(do.astype(F32) * o_ref[rows, :].astype(F32), axis=-1, keepdims=True)

            def block(keys, masked):
                kk = k_ref[keys, :]
                s = _dot(qv, kk, "nt") * ATTN_SCALE
                p = jnp.exp((_causal_mask(s) if masked else s) - lse)
                ds = p * (_dot(do, v_ref[keys, :], "nt") - delta) * ATTN_SCALE
                return _dot(p, do, "tn"), _dot(ds, qv, "tn"), _dot(ds, kk)

            dv_ref[rows, :], dk_ref[rows, :], dq = block(rows, True)
            if qi:
                dv, dk, dq_before = block(before, False)
                dv_ref[before, :] += dv
                dk_ref[before, :] += dk
                dq = dq + dq_before
            dq_ref[rows, :] = dq

    T = batch * seq
    whole = lambda w: BS((None, seq, w), lambda b, h: (h, b, 0))
    head_cols = BS((seq, V_DIM), lambda b, h: (b, POOL_WIDTH // V_DIM + h))
    return _call(body, name="flash_bwd", grid=(batch, HEADS),
                 in_specs=[whole(QK_PAD), whole(QK_PAD), whole(V_DIM), head_cols, head_cols, whole(LANES)],
                 out_specs=[whole(QK_PAD), whole(QK_PAD), whole(V_DIM)],
                 out_shape=[_sds((HEADS, T, QK_PAD), F32), _sds((HEADS, T, QK_PAD), F32), _sds((HEADS, T, V_DIM), F32)])(
        q, k, v, dmix, mix, lse)


def _adamw_math(w, g, m, v):
    m = ADAM_B1 * m + (1.0 - ADAM_B1) * g
    v = ADAM_B2 * v + (1.0 - ADAM_B2) * (g * g)
    m_hat = m / (1.0 - ADAM_B1 ** ADAM_STEP)
    v_hat = v / (1.0 - ADAM_B2 ** ADAM_STEP)
    return -ADAM_LR * (m_hat / (jnp.sqrt(v_hat) + ADAM_EPS) + ADAM_WD * w), m, v


def _adamw(name, w, g, m, v, l=0, prev=()):
    L, R, C = w.shape
    tr = 256 if R % 256 == 0 else R

    def body(w_ref, g_ref, m_ref, v_ref, *rest):
        go_ref, d_ref, mo_ref, vo_ref, token = rest[-5:]
        gv = g_ref[...]
        d_ref[...], mo_ref[...], vo_ref[...] = _adamw_math(w_ref[...], gv, m_ref[...], v_ref[...])
        go_ref[...] = gv
        token[...] = jnp.zeros_like(token)

    layer = BS((None, tr, C), lambda i: (l, i, 0))
    return _call(body, name=f"{name}_{l}", grid=(R // tr,),
                 in_specs=[layer, BS((tr, C), lambda i: (i, 0)), layer, layer] + [pl.BlockSpec(memory_space=pl.ANY)] * len(prev),
                 out_specs=[layer] * 4 + [BS((SUBLANES, LANES), lambda i: (0, 0))],
                 out_shape=[_sds((L, R, C), F32)] * 4 + [_sds((SUBLANES, LANES), F32)],
                 aliases={4 + n: n for n in range(len(prev))})(w, g, m, v, *prev)


def _place():
    x, y, c = lax.axis_index("x"), lax.axis_index("y"), lax.axis_index("c")
    other_chips = [(1 - x, y), (x, 1 - y), (1 - x, 1 - y)]
    return x, y, c, other_chips


def _remote(src, dst, send_sem, recv_sem, dev):
    return pltpu.make_async_remote_copy(src_ref=src, dst_ref=dst, send_sem=send_sem, recv_sem=recv_sem,
                                        device_id=dev, device_id_type=MESH)


def _prefetch_call(body, *, name, grid, in_specs, out_specs, out_shape):
    grid_spec = pltpu.PrefetchScalarGridSpec(num_scalar_prefetch=1, grid=grid, in_specs=in_specs, out_specs=out_specs)
    params = pltpu.CompilerParams(vmem_limit_bytes=VMEM_LIMIT_V7X, dimension_semantics=("arbitrary",) * len(grid))
    return pl.pallas_call(body, name=name, grid_spec=grid_spec, out_shape=out_shape, compiler_params=params)


def _row_tile(rows):
    return 256 if rows % 256 == 0 else rows


def _cast_place(name, w, layer, chip, after=()):
    _, _, rows, C = w.shape
    tr = _row_tile(rows)

    def body(chip_ref, w_ref, *rest):
        rest[-1][...] = w_ref[...].astype(BF16)

    return _prefetch_call(body, name=name, grid=(2, rows // tr),
                          in_specs=[BS((None, None, tr, C), lambda h, i, chip_ref: (layer, h, i, 0))]
                          + [pl.BlockSpec(memory_space=pl.ANY)] * len(after),
                          out_specs=BS((None, None, tr, C), lambda h, i, chip_ref: (chip_ref[0], h, i, 0)),
                          out_shape=pltpu.HBM((N_CHIPS, 2, rows, C), BF16))(chip, w, *after)


SEM = pl.BlockSpec(memory_space=pltpu.SEMAPHORE)


def _split_copy_call(body, *, name, in_specs, out_specs, out_shape, aliases):
    return pl.pallas_call(body, name=name, in_specs=in_specs, out_specs=out_specs, out_shape=out_shape,
                          input_output_aliases=aliases,
                          compiler_params=pltpu.CompilerParams(has_side_effects=pltpu.SideEffectType.DATAFLOW_SIDE_EFFECTING))


def _hbm(arrays):
    return [pltpu.with_memory_space_constraint(a, pltpu.HBM) for a in arrays]


def _gather_send(name, gs, groups, after):
    n = len(gs)

    def body(*refs):
        g, sems, token = refs[:n], refs[n + len(after):n + len(after) + 2 * len(groups)], refs[-1]
        x, y, c, chips = _place()
        me = 2 * x + y
        for gi, members in enumerate(groups):
            for a, i in enumerate(members):
                for k, (px, py) in enumerate(chips):
                    _remote(g[i].at[me, c], g[i].at[me, c], sems[2 * gi].at[3 * a + k], sems[2 * gi + 1].at[3 * a + k],
                            (px, py, c)).start()
        token[...] = jnp.zeros_like(token)

    sem_shapes = [pltpu.SemaphoreType.DMA((3 * len(members),)) for members in groups for _ in range(2)]
    out = _split_copy_call(body, name=name, in_specs=[HBM] * n + [pl.BlockSpec(memory_space=pl.ANY)] * len(after),
                           out_specs=[SEM] * len(sem_shapes) + [HBM] * n + [VMEM],
                           out_shape=sem_shapes + [pltpu.HBM(a.shape, a.dtype) for a in gs] + [_sds((SUBLANES, LANES), F32)],
                           aliases={i: len(sem_shapes) + i for i in range(n)})(*_hbm(gs), *after)
    return out[:len(sem_shapes)], out[len(sem_shapes):-1], out[-1]


def _gather_wait(name, gs, send_sems, recv_sems, after):
    n = len(gs)

    def body(*refs):
        g, ssem, rsem = refs[:n], refs[n], refs[n + 1]
        x, y, c, chips = _place()
        me = 2 * x + y
        for a in range(n):
            for k, (px, py) in enumerate(chips):
                landed = g[a].at[2 * px + py, c]
                cp = _remote(g[a].at[me, c], landed, ssem.at[3 * a + k], rsem.at[3 * a + k], (px, py, c))
                cp.wait_recv()
                cp.wait_send()

    return _split_copy_call(body, name=name, in_specs=[HBM] * n + [SEM, SEM] + [pl.BlockSpec(memory_space=pl.ANY)] * len(after),
                            out_specs=[HBM] * n, out_shape=[pltpu.HBM(a.shape, a.dtype) for a in gs],
                            aliases={i: i for i in range(n)})(*gs, send_sems, recv_sems, *after)


def _gather_pass(name, gs):
    n = len(gs)

    def body(*refs):
        g, send_sems, recv_sems = refs[n:2 * n], refs[-2], refs[-1]
        x, y, c, chips = _place()
        sibling = (x, y, 1 - c)
        passed = [_remote(g[i].at[2 * px + py, c], g[i].at[2 * px + py, c], send_sems.at[3 * i + k], recv_sems.at[3 * i + k], sibling)
                  for i in range(n) for k, (px, py) in enumerate(chips)]
        for cp in passed:
            cp.start()
        for i in range(n):
            for k, (px, py) in enumerate(chips):
                theirs = g[i].at[2 * px + py, 1 - c]
                _remote(theirs, theirs, send_sems.at[3 * i + k], recv_sems.at[3 * i + k], sibling).wait_recv()
        for cp in passed:
            cp.wait_send()

    return _call(body, name=name, in_specs=[HBM] * n, out_specs=[HBM] * n, out_shape=[_sds(a.shape, a.dtype) for a in gs],
                 aliases={i: i for i in range(n)},
                 scratch=[pltpu.SemaphoreType.DMA((3 * n,)), pltpu.SemaphoreType.DMA((3 * n,))])(*gs)


def _scatter_send(name, ps):
    n = len(ps)

    def body(*refs):
        p, r, ssem, rsem, token = refs[:n], refs[n:2 * n], refs[2 * n], refs[2 * n + 1], refs[-1]
        x, y, c, chips = _place()
        for i in range(n):
            for k, (px, py) in enumerate(chips):
                _remote(p[i].at[2 * px + py], r[i].at[k], ssem.at[3 * i + k], rsem.at[3 * i + k], (px, py, c)).start()
        token[...] = jnp.zeros_like(token)

    lands = [lax.empty((N_CHIPS - 1,) + a.shape[1:], a.dtype) for a in ps]
    sem = pltpu.SemaphoreType.DMA((3 * n,))
    out = _split_copy_call(body, name=name, in_specs=[HBM] * (2 * n), out_specs=[SEM, SEM] + [HBM] * (2 * n) + [VMEM],
                           out_shape=[sem, sem] + [pltpu.HBM(a.shape, a.dtype) for a in list(ps) + lands] + [_sds((SUBLANES, LANES), F32)],
                           aliases={i: 2 + i for i in range(2 * n)})(*_hbm(list(ps) + lands))
    return out[0], out[1], out[2:2 + n], out[2 + n:2 + 2 * n], out[-1]


def _scatter_wait(name, ps, lands, send_sems, recv_sems, after):
    n = len(ps)

    def body(*refs):
        p, r, ssem, rsem = refs[:n], refs[n:2 * n], refs[2 * n], refs[2 * n + 1]
        x, y, c, chips = _place()
        for i in range(n):
            for k, (px, py) in enumerate(chips):
                cp = _remote(p[i].at[2 * px + py], r[i].at[k], ssem.at[3 * i + k], rsem.at[3 * i + k], (px, py, c))
                cp.wait_recv()
                cp.wait_send()

    out = _split_copy_call(body, name=name, in_specs=[HBM] * (2 * n) + [SEM, SEM] + [pl.BlockSpec(memory_space=pl.ANY)] * len(after),
                           out_specs=[HBM] * (2 * n), out_shape=[pltpu.HBM(a.shape, a.dtype) for a in list(ps) + list(lands)],
                           aliases={i: i for i in range(2 * n)})(*ps, *lands, send_sems, recv_sems, *after)
    return out[:n], out[n:]


def _exchange_send(name, gs):
    n = len(gs)

    def body(*refs):
        g, r, ssem, rsem, token = refs[:n], refs[n:2 * n], refs[2 * n], refs[2 * n + 1], refs[-1]
        x, y, c, _ = _place()
        for i in range(n):
            _remote(g[i].at[:, 1 - c], r[i], ssem.at[i], rsem.at[i], (x, y, 1 - c)).start()
        token[...] = jnp.zeros_like(token)

    lands = [lax.empty((a.shape[0],) + a.shape[2:], a.dtype) for a in gs]
    sem = pltpu.SemaphoreType.DMA((n,))
    out = _split_copy_call(body, name=name, in_specs=[HBM] * (2 * n), out_specs=[SEM, SEM] + [HBM] * (2 * n) + [VMEM],
                           out_shape=[sem, sem] + [pltpu.HBM(a.shape, a.dtype) for a in list(gs) + lands] + [_sds((SUBLANES, LANES), F32)],
                           aliases={i: 2 + i for i in range(2 * n)})(*_hbm(list(gs) + lands))
    return out[0], out[1], out[2:2 + n], out[2 + n:2 + 2 * n], out[-1]


def _exchange_wait(name, gs, lands, send_sems, recv_sems, after):
    n = len(gs)

    def body(*refs):
        g, r, ssem, rsem = refs[:n], refs[n:2 * n], refs[2 * n], refs[2 * n + 1]
        x, y, c, _ = _place()
        for i in range(n):
            cp = _remote(g[i].at[:, 1 - c], r[i], ssem.at[i], rsem.at[i], (x, y, 1 - c))
            cp.wait_recv()
            cp.wait_send()

    out = _split_copy_call(body, name=name, in_specs=[HBM] * (2 * n) + [SEM, SEM] + [pl.BlockSpec(memory_space=pl.ANY)] * len(after),
                           out_specs=[HBM] * (2 * n), out_shape=[pltpu.HBM(a.shape, a.dtype) for a in list(gs) + list(lands)],
                           aliases={i: i for i in range(2 * n)})(*gs, *lands, send_sems, recv_sems, *after)
    return out[:n], out[n:]


def _sibling_share(name, fs):
    n = len(fs)

    def body(*refs):
        f, send_sems, recv_sems = refs[n:2 * n], refs[-2], refs[-1]
        x, y, c, _ = _place()
        sends = [_remote(f[i].at[c], f[i].at[c], send_sems.at[i], recv_sems.at[i], (x, y, 1 - c)) for i in range(n)]
        for cp in sends:
            cp.start()
        for i in range(n):
            theirs = f[i].at[1 - c]
            _remote(theirs, theirs, send_sems.at[i], recv_sems.at[i], (x, y, 1 - c)).wait_recv()
        for cp in sends:
            cp.wait_send()

    return _call(body, name=name, in_specs=[HBM] * n, out_specs=[HBM] * n,
                 out_shape=[_sds(a.shape, a.dtype) for a in fs], aliases={i: i for i in range(n)},
                 scratch=[pltpu.SemaphoreType.DMA((n,)), pltpu.SemaphoreType.DMA((n,))])(*fs)


def _all_reduce_small(name, v):
    rows = v.shape[0] // 2
    halves = (2, rows, LANES)

    def body(v_ref, o_ref, from_sibling, chip_sums, send_sems, recv_sems):
        x, y, c, chips = _place()
        me, sibling = 2 * x + y, (x, y, 1 - c)
        swap = _remote(v_ref.at[1 - c], from_sibling, send_sems.at[0], recv_sems.at[0], sibling)
        swap.start()
        swap.wait()
        chip_sums[me] = v_ref[c] + from_sibling[...]
        sends = [_remote(chip_sums.at[me], chip_sums.at[me], send_sems.at[1 + k], recv_sems.at[1 + k], (px, py, c))
                 for k, (px, py) in enumerate(chips)]
        for cp in sends:
            cp.start()
        for k, (px, py) in enumerate(chips):
            theirs = chip_sums.at[2 * px + py]
            _remote(theirs, theirs, send_sems.at[1 + k], recv_sems.at[1 + k], (px, py, c)).wait_recv()
        for cp in sends:
            cp.wait_send()
        acc = chip_sums[0]
        for j in range(1, N_CHIPS):
            acc = acc + chip_sums[j]
        o_ref[c] = acc
        share = _remote(o_ref.at[c], o_ref.at[c], send_sems.at[4], recv_sems.at[4], sibling)
        share.start()
        share.wait_send()
        _remote(o_ref.at[1 - c], o_ref.at[1 - c], send_sems.at[4], recv_sems.at[4], sibling).wait_recv()

    return _call(body, name=name, in_specs=[VMEM], out_specs=VMEM, out_shape=_sds(halves, F32),
                 scratch=[pltpu.VMEM((rows, LANES), F32), pltpu.VMEM((N_CHIPS, rows, LANES), F32),
                          pltpu.SemaphoreType.DMA((5,)), pltpu.SemaphoreType.DMA((5,))])(v.reshape(halves)).reshape(v.shape)


def _add_halves(name, g, r, c):
    _, _, rows, C = g.shape
    tr = _row_tile(rows)

    def body(c_ref, g_ref, r_ref, o_ref):
        o_ref[...] = (g_ref[...].astype(F32) + r_ref[...].astype(F32)).astype(BF16)

    spec = BS((None, tr, C), lambda j, i, c_ref: (j, i, 0))
    return _prefetch_call(body, name=name, grid=(N_CHIPS, rows // tr),
                          in_specs=[BS((None, None, tr, C), lambda j, i, c_ref: (j, c_ref[0], i, 0)), spec], out_specs=spec,
                          out_shape=pltpu.HBM((N_CHIPS, rows, C), BF16))(c, g, r)


def _sum_partials(name, p, r, chip_c):
    _, rows, C = p.shape
    tr = _row_tile(rows)

    def body(s_ref, p_ref, r_ref, o_ref):
        acc = p_ref[...].astype(F32)
        for k in range(N_CHIPS - 1):
            acc = acc + r_ref[k].astype(F32)
        o_ref[...] = acc

    return _prefetch_call(body, name=name, grid=(rows // tr,),
                          in_specs=[BS((None, tr, C), lambda i, s: (s[0], i, 0)), BS((N_CHIPS - 1, tr, C), lambda i, s: (0, i, 0))],
                          out_specs=BS((None, tr, C), lambda i, s: (s[1], i, 0)), out_shape=pltpu.HBM((2, rows, C), F32))(chip_c, p, r)


_SHARDED = ("even_w_in", "even_w_out", "odd_w_in", "q_b", "kv_b", "odd_w_out", "ffn_w_gate", "ffn_w_up", "ffn_w_down")
_REPLICATED = ("mix_norm", "ffn_norm", "sg_ln_g", "sg_w_s", "sg_b_s", "pool_w", "q_norm", "k_norm")
_SMALL_SHARDED = ("sc_conv_w", "pool_scale", "q_a_norm", "kv_a_norm")
_WEIGHTS = ("mix_norm", "ffn_norm", "even_w_in", "sg_ln_g", "sg_w_s", "sg_b_s", "sc_conv_w", "even_w_out", "odd_w_in", "pool_w",
            "pool_scale", "q_a_norm", "q_b", "kv_a_norm", "kv_b", "q_norm", "k_norm", "odd_w_out", "ffn_w_gate", "ffn_w_up",
            "ffn_w_down")


def _pad_rows(flat, width, align):
    n = flat.shape[0]
    rows = -(-n // (width * align)) * align
    return jnp.pad(flat, (0, rows * width - n)).reshape(rows, width)


_GROUPS = {"even": ("even_w_in", "even_w_out"),
           "ffn0": ("ffn_w_gate0", "ffn_w_up0", "ffn_w_down0"),
           "odd": ("odd_w_in", "q_b", "kv_b", "odd_w_out"),
           "ffn1": ("ffn_w_gate1", "ffn_w_up1", "ffn_w_down1")}


def _place_shards(shards, names, chip, after):
    placed = []
    for n in names:
        weight, layer = (n[:-1], int(n[-1])) if n[-1].isdigit() else (n, 0)
        a = shards[weight]
        placed.append(_cast_place(f"place_{n}", a.reshape(a.shape[0], 2, a.shape[1] // 2, a.shape[2]), layer, chip, after))
    return placed


def _whole_weights(gathered):
    out = {n: a.reshape(N_CHIPS, -1, a.shape[-1]) for n, a in gathered.items()}
    for n in ("q_b", "kv_b"):
        if n in out:
            out[n] = out[n].transpose(1, 0, 2).reshape(out[n].shape[1], -1)
    for n in ("even_w_out", "odd_w_in", "odd_w_out"):
        if n in out:
            out[n] = out[n].reshape(-1, out[n].shape[-1])
    return out


def _forward_backward(x, positions, target, small, fetch, emit, advance):
    batch, seq, _ = x.shape
    T = batch * seq
    tm = _token_tile(seq)
    x0 = x.reshape(T, D_MODEL)

    inv_freq = ROPE_THETA ** (-jnp.arange(0, QK_ROPE, 2, dtype=F32) / QK_ROPE)
    ang = (positions.astype(F32)[..., None] * inv_freq).reshape(T, QK_ROPE // 2)
    cos, sin = jnp.cos(ang), jnp.sin(ang)
    pad = jnp.zeros((T, LANES - QK_ROPE), F32)
    cos_t = jnp.concatenate([cos, cos, pad], axis=1)
    sin_t = jnp.concatenate([-sin, sin, pad], axis=1)

    tril = jnp.tril(jnp.ones((SG_CHUNK, SG_CHUNK), bool))
    w_tril = jnp.where(tril[None], small["sg_w_s"][0], 0.0).astype(BF16)
    b_lanes = jnp.broadcast_to(small["sg_b_s"][0][:, :, None], (SG_HEADS, SG_CHUNK, SG_DIM))
    conv_w = jnp.pad(small["sc_conv_w"][0], ((0, SUBLANES - CONV_TAPS), (0, 0)))
    ln_g = small["sg_ln_g"]
    pool_diag = jnp.zeros((POOL_WIDTH, POOL_WIDTH), F32)
    for g in range(len(POOL_WINDOWS)):
        pool_diag = pool_diag.at[POOL_DIM * g:POOL_DIM * (g + 1), POOL_DIM * g:POOL_DIM * (g + 1)].set(small["pool_w"][0, g])
    pool_diag = pool_diag.astype(BF16)
    pool_scale = small["pool_scale"]
    q_g = jnp.pad(small["q_norm"], ((0, 0), (0, QK_PAD - QK_DIM)))
    k_g = jnp.pad(small["k_norm"], ((0, 0), (0, QK_PAD - QK_DIM)))
    qa_g, kva_g = small["q_a_norm"], small["kv_a_norm"]
    in_shard = EVEN_IN // N_CHIPS

    def ffn_weights(l, w):
        return w[f"ffn_w_gate{l}"], w[f"ffn_w_up{l}"], w[f"ffn_w_down{l}"]

    W = fetch("even", ())
    w_in_even = W["even_w_in"]
    h0 = _rmsnorm_fwd("mix0_norm", x0, small["mix_norm"][0], tm)
    tb = _big_tile(T)
    proj0 = _even_in(h0, w_in_even, _resident_tile(T))
    mix0 = _even_mixer_fwd(proj0, ln_g, w_tril, b_lanes, conv_w, seq, tm)
    w_out_even = W["even_w_out"]
    x1, h1 = _mm("even_out", "nn", mix0, w_out_even, F32, tk=1024, add=x0, fused=_norm_tail(small["ffn_norm"][0], T, tb))
    ffn0 = ffn_weights(0, fetch("ffn0", (x1,)))
    (x2, h2), ffn0_saved = _ffn_fwd(0, x1, h1, *ffn0, lambda tile: _norm_tail(small["mix_norm"][1], T, tile))
    W = fetch("odd", (x2,))
    w_in_odd = jnp.pad(W["odd_w_in"], ((0, 0), (0, ODD_IN_PAD - ODD_IN)))
    q_b = jnp.pad(W["q_b"].reshape(Q_LORA, HEADS, QK_DIM).transpose(1, 0, 2), ((0, 0), (0, 0), (0, QK_PAD - QK_DIM)))
    kv_b = W["kv_b"].reshape(KV_LORA, HEADS, QK_NOPE + V_DIM).transpose(1, 0, 2)
    proj1 = _mm("odd_in", "nn", h2, w_in_odd, F32, tk=1024)
    mix1 = _pool_fwd(proj1, pool_diag, pool_scale, seq, tm)
    q, k, v = _mla_qkv_fwd(proj1, cos_t, sin_t, qa_g, kva_g, q_b, kv_b, q_g, k_g, tm)
    mix1, lse = _flash_fwd(q, k, v, mix1, batch, seq)
    x3, h3 = _mm("odd_out", "nn", mix1, W["odd_w_out"], F32, tk=1024, add=x2, fused=_norm_tail(small["ffn_norm"][1], T, tb))
    ffn1 = ffn_weights(1, fetch("ffn1", (x3,)))
    (dy, sq), ffn1_saved = _ffn_fwd(1, x3, h3, *ffn1, lambda tile: _loss_tail(target.reshape(T, D_MODEL), tile))

    G = {}
    dx3, dffn_g1 = _ffn_bwd(1, x3, small["ffn_norm"][1], *ffn1, ffn1_saved, dy, emit)
    dmix1 = _mm("odd_out_dx", "nt", dx3, W["odd_w_out"], BF16, tk=1024, after=advance(dx3))
    dw_out_odd = _mm("odd_out_dw", "tn", mix1, dx3, BF16, hbm_out=True)
    dq, dk, dv = _flash_bwd(q, k, v, dmix1, mix1, lse, batch, seq)
    dz_pool, dpool_diag, G["pool_scale"] = _pool_bwd(proj1, dmix1, pool_diag, pool_scale, seq, tm)
    dproj1, dq_b, dkv_b, dq_g, dk_g, G["q_a_norm"], G["kv_a_norm"] = _mla_qkv_bwd(
        proj1, cos_t, sin_t, qa_g, kva_g, q_b, kv_b, q_g, k_g, dq, dk, dv, dz_pool, tm)
    G["pool_w"] = jnp.stack([dpool_diag[POOL_DIM * g:POOL_DIM * (g + 1), POOL_DIM * g:POOL_DIM * (g + 1)]
                             for g in range(len(POOL_WINDOWS))])[None]
    G["q_norm"], G["k_norm"] = dq_g[:, :QK_DIM], dk_g[:, :QK_DIM]
    dw_in_odd = _mm("odd_in_dw", "tn", h2, dproj1, BF16, tn=ODD_IN, hbm_out=True)

    def shard_major(g, cols):
        return g.reshape(g.shape[0], N_CHIPS, cols).transpose(1, 0, 2).astype(BF16)

    behind = emit("odd", {"odd_w_in": dw_in_odd.reshape(N_CHIPS, -1, ODD_IN),
                          "q_b": shard_major(dq_b[:, :, :QK_DIM].transpose(1, 0, 2).reshape(Q_LORA, HEADS * QK_DIM), HEADS * QK_DIM // N_CHIPS),
                          "kv_b": shard_major(dkv_b.transpose(1, 0, 2).reshape(KV_LORA, HEADS * (QK_NOPE + V_DIM)),
                                              HEADS * (QK_NOPE + V_DIM) // N_CHIPS),
                          "odd_w_out": dw_out_odd.reshape(N_CHIPS, -1, D_MODEL)})
    dx2, dmix_g1 = _mm("odd_in_dx", "nt", dproj1, W["odd_w_in"], F32, tk=ODD_IN, after=behind,
                       fused=_norm_bwd_tail(x2, small["mix_norm"][1], dx3, tb))
    dx1, dffn_g0 = _ffn_bwd(0, x1, small["ffn_norm"][0], *ffn0, ffn0_saved, dx2, emit, after=advance(dx2))
    dmix0 = _mm("even_out_dx", "nt", dx1, w_out_even, F32, tk=1024, after=advance(dx1))
    dw_out_even = _mm("even_out_dw", "tn", mix0, dx1, BF16, hbm_out=True)
    dproj0, dw_s, db_lanes, G["sg_ln_g"], dconv = _even_mixer_bwd(proj0, dmix0, ln_g, w_tril, b_lanes, conv_w, seq, tm)
    G["sg_w_s"] = dw_s[None]
    G["sg_b_s"] = jnp.sum(db_lanes, axis=-1)[None]
    G["sc_conv_w"] = dconv[None, :CONV_TAPS]
    tr = _resident_tile(T)
    tail, shapes, specs = _norm_bwd_tail(x0, small["mix_norm"][0], dx1, tr)
    dx0, dmix_g0 = _matmul("even_in_dx", "nt", [(dproj0, w_in_even)],
                           [(_row_spec(tr, EVEN_IN), _resident((N_CHIPS, D_MODEL, in_shard)))],
                           (T // tr, 1, 1), shapes, specs, (tr, D_MODEL), tail=tail)
    tk = min(512, T)
    dw_in_even = _grad_shards(
        "even_in_dw", h0, dproj0, BS((tk, D_MODEL), lambda k: (k, 0)), BS((tk, EVEN_IN), lambda k: (k, 0)),
        lambda a_ref, b_ref, j: (a_ref[...], b_ref[:, in_shard * j:in_shard * (j + 1)]), (N_CHIPS, D_MODEL, in_shard), T // tk)
    emit("even", {"even_w_in": dw_in_even, "even_w_out": dw_out_even.reshape(N_CHIPS, -1, D_MODEL)})
    G["mix_norm"] = jnp.concatenate([dmix_g0, dmix_g1], axis=0)
    G["ffn_norm"] = jnp.concatenate([dffn_g0, dffn_g1], axis=0)
    return sq[0, 0], dx0.reshape(batch, seq, D_MODEL), G


def _small_vector(parts, names):
    flat = jnp.concatenate([parts[n].astype(F32).reshape(-1) for n in names])
    return _pad_rows(flat, LANES, 2 * SUBLANES)


def _split_small(vec, like, names):
    out, off, flat = {}, 0, vec.reshape(-1)
    for n in names:
        size = math.prod(like[n].shape)
        out[n] = flat[off:off + size].reshape(like[n].shape)
        off += size
    return out


def _whole_shape(a):
    return a.shape[:-1] + (a.shape[-1] * N_CHIPS,)


def kernel(x, positions, mix_norm, ffn_norm, even_w_in, sg_ln_g, sg_w_s, sg_b_s, sc_conv_w, even_w_out, odd_w_in, pool_w, pool_scale, q_a_norm, q_b, kv_a_norm, kv_b, q_norm, k_norm, odd_w_out, ffn_w_gate, ffn_w_up, ffn_w_down, loss_target, m_mix_norm, m_ffn_norm, m_even_w_in, m_sg_ln_g, m_sg_w_s, m_sg_b_s, m_sc_conv_w, m_even_w_out, m_odd_w_in, m_pool_w, m_pool_scale, m_q_a_norm, m_q_b, m_kv_a_norm, m_kv_b, m_q_norm, m_k_norm, m_odd_w_out, m_ffn_w_gate, m_ffn_w_up, m_ffn_w_down, v_mix_norm, v_ffn_norm, v_even_w_in, v_sg_ln_g, v_sg_w_s, v_sg_b_s, v_sc_conv_w, v_even_w_out, v_odd_w_in, v_pool_w, v_pool_scale, v_q_a_norm, v_q_b, v_kv_a_norm, v_kv_b, v_q_norm, v_k_norm, v_odd_w_out, v_ffn_w_gate, v_ffn_w_up, v_ffn_w_down):
    args = dict(locals())
    w = {n: args[n] for n in _WEIGHTS}
    m = {n: args["m_" + n] for n in _WEIGHTS}
    v = {n: args["v_" + n] for n in _WEIGHTS}
    cx, cy, cc = lax.axis_index("x"), lax.axis_index("y"), lax.axis_index("c")
    chip = 2 * cx + cy
    transposed = ("ffn_w_gate", "ffn_w_up")
    for n in transposed:
        w[n], m[n], v[n] = (jnp.swapaxes(t[n], 1, 2) for t in (w, m, v))

    chip_arr = chip.astype(jnp.int32).reshape(1)
    c_arr = cc.astype(jnp.int32).reshape(1)
    group_names = list(_GROUPS)
    placed = {}
    for n in _SMALL_SHARDED:
        a = w[n]
        whole = jnp.zeros(a.shape[:-1] + (N_CHIPS, a.shape[-1]), F32)
        whole = lax.dynamic_update_slice_in_dim(whole, a[..., None, :], chip, axis=a.ndim - 1)
        placed[n] = jnp.where(cc == 0, whole, 0.0).reshape(_whole_shape(a))
    small_whole = _all_reduce_small("gather_small_weights", _small_vector(placed, _SMALL_SHARDED))
    small = dict({n: w[n] for n in _REPLICATED}, **_split_small(small_whole, placed, _SMALL_SHARDED))

    first, rest = list(_GROUPS[group_names[0]]), [n for g in group_names[1:] for n in _GROUPS[g]]
    sems_first, flight_first, token = _gather_send("gather_send_first", _place_shards(w, first, chip_arr, (small_whole,)),
                                                   [list(range(len(first)))], (small_whole,))
    sems_rest, flight_rest, all_sent = _gather_send("gather_send_rest", _place_shards(w, rest, chip_arr, (token,)),
                                                    [[rest.index(n) for n in _GROUPS[g]] for g in group_names[1:]], ())
    sems = list(sems_first) + list(sems_rest)
    in_flight = dict(zip(first + rest, list(flight_first) + list(flight_rest)))

    def fetch(group, after):
        gi, members = group_names.index(group), _GROUPS[group]
        after = after if gi else (all_sent,)
        landed = _gather_wait(f"gather_wait_{group}", [in_flight[n] for n in members], sems[2 * gi], sems[2 * gi + 1], after)
        return _whole_weights(dict(zip(members, _gather_pass(f"gather_pass_{group}", landed))))

    swapping, pending, arrived = [], [], {}

    def settle(after):
        names, ps, lands, send_sems, recv_sems = pending.pop()
        ps, lands = _scatter_wait(f"scatter_wait_{names[0]}", ps, lands, send_sems, recv_sems, after)
        arrived.update({n: (p, r) for n, p, r in zip(names, ps, lands)})

    def emit(group, grads):
        names = _GROUPS[group]
        halves = [grads[n].reshape(N_CHIPS, 2, grads[n].shape[1] // 2, grads[n].shape[2]) for n in names]
        send_sems, recv_sems, halves, lands, token = _exchange_send(f"exchange_send_{group}", halves)
        swapping.append((group, halves, lands, send_sems, recv_sems))
        return (token,)

    def advance(done):
        done = done if isinstance(done, tuple) else (done,)
        group, halves, lands, send_sems, recv_sems = swapping.pop()
        names = _GROUPS[group]
        halves, lands = _exchange_wait(f"exchange_wait_{group}", halves, lands, send_sems, recv_sems, done)
        partial = [_add_halves(f"add_{n}", g, r, c_arr) for n, g, r in zip(names, halves, lands)]
        if pending:
            settle(done)
        send_sems, recv_sems, ps, lands, token = _scatter_send(f"scatter_send_{group}", partial)
        pending.append((names, ps, lands, send_sems, recv_sems))
        return (token,)

    sq, grad_x, G = _forward_backward(x, positions, loss_target, small, fetch, emit, advance)
    loss = lax.psum(0.5 * sq / D_MODEL, ("x", "y", "c"))

    small_names = _REPLICATED + _SMALL_SHARDED
    summed = _split_small(_all_reduce_small("reduce_small_grads", _small_vector(G, small_names)), G, small_names)
    grads = {n: summed[n] for n in _REPLICATED}
    for n in _SMALL_SHARDED:
        a = w[n]
        grads[n] = lax.dynamic_slice_in_dim(summed[n].reshape(a.shape[:-1] + (N_CHIPS, a.shape[-1])), chip, 1,
                                            axis=a.ndim - 1).reshape(a.shape)

    chip_c = jnp.stack([chip, cc]).astype(jnp.int32)
    out = {}

    def finish(group):
        names, tokens = _GROUPS[group], []
        sums = [_sum_partials(f"sum_{n}", *arrived[n], chip_c) for n in names]
        for n, f in zip(names, _sibling_share(f"grad_share_{group}", sums)):
            weight, layer = (n[:-1], int(n[-1])) if n[-1].isdigit() else (n, 0)
            *out[weight], token = _adamw(f"adamw_{weight}", w[weight], f.reshape(-1, f.shape[-1]), m[weight], v[weight], layer,
                                         out.get(weight, ()))
            tokens.append(token)
        return tuple(tokens)

    advance(finish(group_names[3]) + finish(group_names[2]))
    settle(finish(group_names[1]))
    finish(group_names[0])
    packed = [_small_vector(d, small_names) for d in (w, grads, m, v)]
    res = _adamw("adamw_small", packed[0][None], packed[1], packed[2][None], packed[3][None])
    delta_s, m_s, v_s = (_split_small(r, w, small_names) for r in res[1:4])
    for n in small_names:
        out[n] = (grads[n], delta_s[n], m_s[n], v_s[n])
    for n in transposed:
        out[n] = tuple(jnp.swapaxes(t, 1, 2) for t in out[n])

    return (loss, grad_x, *[out[n][0] for n in _WEIGHTS], *[out[n][1] for n in _WEIGHTS],
            *[out[n][2] for n in _WEIGHTS], *[out[n][3] for n in _WEIGHTS])
```

```python
import functools
import math

import jax
import jax.numpy as jnp
from jax import lax
from jax.experimental import pallas as pl
from jax.experimental.pallas import tpu as pltpu

F32, BF16 = jnp.float32, jnp.bfloat16
BS = pl.BlockSpec

D_MODEL = 1024
EPS = 1e-6
NEG_INF = -1e30
SG_HEADS, SG_DIM, SG_WIDTH, SG_CHUNK = 4, 128, 512, 128
SC_WIDTH, CONV_TAPS = 512, 3
EVEN_IN = 2 * SG_WIDTH + 3 * SC_WIDTH
POOL_WINDOWS = (2, 4, 8, 16)
POOL_DIM, POOL_WIDTH = 64, 256
POOL_HALO = 16
HEADS, Q_LORA, KV_LORA, QK_NOPE, QK_ROPE, V_DIM = 6, 384, 256, 128, 64, 128
QK_DIM = QK_NOPE + QK_ROPE
QK_PAD = 256
ODD_IN = POOL_WIDTH + Q_LORA + KV_LORA + QK_ROPE
ODD_IN_PAD = 1024
ROPE_THETA = 10000.0
ATTN_SCALE = QK_DIM ** -0.5
D_FF, N_CHIPS = 2816, 4
FF_SHARD = D_FF // N_CHIPS
ADAM_LR, ADAM_B1, ADAM_B2, ADAM_EPS, ADAM_WD, ADAM_STEP = 0.001, 0.9, 0.999, 1e-08, 0.01, 10
VMEM_LIMIT_V7X = 48 * 2**20
LANES, SUBLANES = 128, 8
MESH = pl.DeviceIdType.MESH
HBM = pl.BlockSpec(memory_space=pltpu.HBM)
VMEM = pl.BlockSpec(memory_space=pltpu.VMEM)

_DIMS = {"nn": (((1,), (0,)), ((), ())), "nt": (((1,), (1,)), ((), ())), "tn": (((0,), (0,)), ((), ()))}


def _dot(a, b, mode="nn"):
    return lax.dot_general(a.astype(BF16), b.astype(BF16), _DIMS[mode], preferred_element_type=F32)


def _call(body, *, name, out_shape, in_specs, out_specs, grid=(), scratch=(), aliases=None, after=()):
    params = pltpu.CompilerParams(vmem_limit_bytes=VMEM_LIMIT_V7X,
                                  **({"dimension_semantics": ("arbitrary",) * len(grid)} if grid else {}))
    n_in, n_after = len(in_specs), len(after)
    kernel_body = body if not after else (lambda *refs: body(*refs[:n_in], *refs[n_in + n_after:]))
    call = pl.pallas_call(kernel_body, name=name, grid=grid, in_specs=list(in_specs) + [pl.BlockSpec(memory_space=pl.ANY)] * n_after,
                          out_specs=out_specs, out_shape=out_shape, scratch_shapes=list(scratch),
                          input_output_aliases=aliases or {}, compiler_params=params)
    return (lambda *ops: call(*ops, *after)) if after else call


def _sds(shape, dtype):
    return jax.ShapeDtypeStruct(tuple(shape), dtype)


def _token_tile(seq):
    return 512 if seq % 512 == 0 else seq


_TAIL_ROWS = 256


def _matmul(name, mode, pairs, pair_specs, grid, out_shape, out_spec, acc_shape, add=None, add_spec=None, after=(), tail=None):
    n, nk = len(pairs), grid[-1]
    n_add = int(add is not None)
    n_tail = len(tail[0]) if tail else 0
    n_in = 2 * n + n_add + n_tail
    n_out = len(out_shape) if tail else 1

    def body(*refs):
        ab = refs[:2 * n]
        add_ref = refs[2 * n] if n_add else None
        tail_refs, outs = refs[2 * n + n_add:n_in], refs[n_in:n_in + n_out]
        first = pl.program_id(0) == 0

        def finish(result):
            if tail is None:
                r = result(slice(None))
                outs[0][...] = (r if add_ref is None else r + add_ref[...]).astype(outs[0].dtype)
                return
            for lo in range(0, acc_shape[0], _TAIL_ROWS):
                rows = slice(lo, min(lo + _TAIL_ROWS, acc_shape[0]))
                r = result(rows)
                tail[2](rows, r if add_ref is None else r + add_ref[rows, :], first, tail_refs, outs)

        def terms(a_ref, b_ref):
            if len(a_ref.shape) == 2 and len(b_ref.shape) == 2:
                return [(a_ref[...], b_ref[...])]
            cols = a_ref.shape[-1] // N_CHIPS
            return [(a_ref[j] if len(a_ref.shape) == 3 else a_ref[:, cols * j:cols * (j + 1)], b_ref[j]) for j in range(N_CHIPS)]

        if nk == 1:
            r = None
            for p in range(n):
                for a_blk, b_blk in terms(ab[2 * p], ab[2 * p + 1]):
                    d = _dot(a_blk, b_blk, mode)
                    r = d if r is None else r + d
            finish(lambda rows: r[rows])
            return
        acc = refs[-1]
        k = pl.program_id(len(grid) - 1)

        @pl.when(k == 0)
        def _():
            acc[...] = jnp.zeros_like(acc)

        for p in range(n):
            acc[...] += _dot(ab[2 * p][...], ab[2 * p + 1][...], mode)

        @pl.when(k == nk - 1)
        def _():
            finish(lambda rows: acc[rows, :])

    ops = [t for pr in pairs for t in pr] + ([add] if n_add else []) + (list(tail[0]) if tail else [])
    specs = [s for pr in pair_specs for s in pr] + ([add_spec] if n_add else []) + (list(tail[1]) if tail else [])
    return _call(body, name=name, grid=grid, in_specs=specs, out_specs=out_spec, out_shape=out_shape,
                 scratch=[pltpu.VMEM(acc_shape, F32)] if nk > 1 else [], after=after)(*ops)


def _row_spec(tm, d):
    return BS((tm, d), lambda i, j, k: (i, 0))


def _vec_spec(d):
    return BS((1, d), lambda i, j, k: (0, 0))


def _norm_tail(gain, T, tm):
    d = gain.shape[-1]

    def fn(rows, r, first, tail_refs, outs):
        outs[0][rows, :] = r
        outs[1][rows, :] = (r * lax.rsqrt(jnp.mean(r * r, axis=-1, keepdims=True) + EPS) * tail_refs[0][...]).astype(BF16)

    return ([gain.reshape(1, d)], [_vec_spec(d)], fn), [_sds((T, d), F32), _sds((T, d), BF16)], [_row_spec(tm, d), _row_spec(tm, d)]


def _norm_bwd_tail(x, gain, dres, tm):
    T, d = x.shape

    def fn(rows, r, first, tail_refs, outs):
        x_ref, g_ref, dres_ref = tail_refs
        xv = x_ref[rows, :]
        rstd = lax.rsqrt(jnp.mean(xv * xv, axis=-1, keepdims=True) + EPS)
        xhat = xv * rstd
        if rows.start == 0:
            @pl.when(first)
            def _():
                outs[1][...] = jnp.zeros_like(outs[1])

        outs[1][...] += jnp.sum(r * xhat, axis=0, keepdims=True)
        dxhat = r * g_ref[...]
        outs[0][rows, :] = dres_ref[rows, :] + rstd * (dxhat - xhat * jnp.mean(dxhat * xhat, axis=-1, keepdims=True))

    return (([x, gain.reshape(1, d), dres], [_row_spec(tm, d), _vec_spec(d), _row_spec(tm, d)], fn),
            [_sds((T, d), F32), _sds((1, d), F32)], [_row_spec(tm, d), _vec_spec(d)])


def _loss_tail(target, tm):
    T, d = target.shape

    def fn(rows, r, first, tail_refs, outs):
        e = r - tail_refs[0][rows, :]
        if rows.start == 0:
            @pl.when(first)
            def _():
                outs[1][...] = jnp.zeros_like(outs[1])

        outs[1][...] += jnp.sum(e * e)
        outs[0][rows, :] = e * (1.0 / d)

    return (([target], [_row_spec(tm, d)], fn), [_sds((T, d), F32), _sds((SUBLANES, LANES), F32)],
            [_row_spec(tm, d), BS((SUBLANES, LANES), lambda i, j, k: (0, 0))])


def _grad_shards(name, a, b, a_spec, b_spec, pick, out_shape, n_steps):
    def body(a_ref, b_ref, o_ref, acc):
        k = pl.program_id(0)

        @pl.when(k == 0)
        def _():
            acc[...] = jnp.zeros_like(acc)

        for j in range(N_CHIPS):
            aj, bj = pick(a_ref, b_ref, j)
            acc[j] += _dot(aj, bj, "tn")

        @pl.when(k == n_steps - 1)
        def _():
            o_ref[...] = acc[...].astype(BF16)

    return _call(body, name=name, grid=(n_steps,), in_specs=[a_spec, b_spec], scratch=[pltpu.VMEM(tuple(out_shape), F32)],
                 out_specs=BS(out_shape, lambda k: (0, 0, 0)), out_shape=pltpu.HBM(tuple(out_shape), BF16))(a, b)


def _mm(name, mode, a, b, out_dtype, tm=1024, tn=1024, tk=512, add=None, after=(), fused=None, hbm_out=False):
    if mode == "tn":
        (K, M), N = a.shape, b.shape[1]
    else:
        (M, K), N = a.shape, (b.shape[1] if mode == "nn" else b.shape[0])
    tm, tn, tk = min(tm, M), min(tn, N), min(tk, K)
    a_spec = BS((tk, tm), lambda i, j, k: (k, i)) if mode == "tn" else BS((tm, tk), lambda i, j, k: (i, k))
    b_spec = BS((tn, tk), lambda i, j, k: (j, k)) if mode == "nt" else BS((tk, tn), lambda i, j, k: (k, j))
    o_spec = BS((tm, tn), lambda i, j, k: (i, j))
    tail, shapes, specs = fused if fused else (None, pltpu.HBM((M, N), out_dtype) if hbm_out else _sds((M, N), out_dtype), o_spec)
    return _matmul(name, mode, [(a, b)], [(a_spec, b_spec)], (M // tm, N // tn, K // tk), shapes, specs, (tm, tn),
                   add=add, add_spec=o_spec if add is not None else None, after=after, tail=tail)


def _rmsnorm_fwd(name, x, g, tm):
    T, d = x.shape

    def body(x_ref, g_ref, o_ref):
        xv = x_ref[...]
        y = xv * lax.rsqrt(jnp.mean(xv * xv, axis=-1, keepdims=True) + EPS)
        o_ref[...] = (y * g_ref[...]).astype(o_ref.dtype)

    return _call(body, name=name, grid=(T // tm,), in_specs=[BS((tm, d), lambda i: (i, 0)), BS((1, d), lambda i: (0, 0))],
                 out_specs=BS((tm, d), lambda i: (i, 0)), out_shape=_sds((T, d), BF16))(x, g.reshape(1, d))


_PASS_ROWS = 256


def _ffn_up(name, h, wg, wu, tm):
    T = h.shape[0]

    def body(h_ref, wg_ref, wu_ref, g_ref, u_ref, a_ref):
        hv = h_ref[...]
        g = _dot(hv, wg_ref[...], "nt")
        u = _dot(hv, wu_ref[...], "nt")
        g_ref[...] = g.astype(BF16)
        u_ref[...] = u.astype(BF16)
        a_ref[...] = (g * (1.0 / (1.0 + jnp.exp(-g))) * u).astype(BF16)

    w_spec = BS((None, FF_SHARD, D_MODEL), lambda j, i: (j, 0, 0))
    o_spec = BS((None, tm, FF_SHARD), lambda j, i: (j, i, 0))
    sh = _sds((N_CHIPS, T, FF_SHARD), BF16)
    return _call(body, name=name, grid=(N_CHIPS, T // tm), in_specs=[BS((tm, D_MODEL), lambda j, i: (i, 0)), w_spec, w_spec],
                 out_specs=[o_spec, o_spec, o_spec], out_shape=[sh, sh, sh])(h, wg, wu)


def _ffn_act_bwd(name, dxo, wd, g, u, tm, after=()):
    T = dxo.shape[0]

    def body(dx_ref, wd_ref, g_ref, u_ref, dg_ref, du_ref):
        da = _dot(dx_ref[...], wd_ref[...], "nt")
        g = g_ref[...].astype(F32)
        sig = 1.0 / (1.0 + jnp.exp(-g))
        dg_ref[...] = (da * u_ref[...].astype(F32) * (sig * (1.0 + g * (1.0 - sig)))).astype(BF16)
        du_ref[...] = (da * (g * sig)).astype(BF16)

    t_spec = BS((None, tm, FF_SHARD), lambda i, j: (j, i, 0))
    sh = _sds((N_CHIPS, T, FF_SHARD), BF16)
    return _call(body, name=name, grid=(T // tm, N_CHIPS),
                 in_specs=[BS((tm, D_MODEL), lambda i, j: (i, 0)), BS((None, FF_SHARD, D_MODEL), lambda i, j: (j, 0, 0)), t_spec, t_spec],
                 out_specs=[t_spec, t_spec], out_shape=[sh, sh], after=after)(dxo, wd, g, u)


def _big_tile(n):
    return min(1024, n)


def _resident_tile(n):
    return min(512, n)


def _resident(shape):
    return BS(shape, lambda i, j, k: (0,) * len(shape), pipeline_mode=pl.Buffered(1))


def _ffn_fwd(l, x, h, wg, wu, wd, fused):
    T = x.shape[0]
    g, u, a = _ffn_up(f"ffn{l}_up", h, wg, wu, _big_tile(T))
    tm = _resident_tile(T)
    tail, shapes, specs = fused(tm)
    outs = _matmul(f"ffn{l}_down", "nn", [(a, wd)],
                   [(BS((N_CHIPS, tm, FF_SHARD), lambda i, j, k: (0, i, 0)), _resident((N_CHIPS, FF_SHARD, D_MODEL)))],
                   (T // tm, 1, 1), shapes, specs, (tm, D_MODEL), add=x, add_spec=_row_spec(tm, D_MODEL), tail=tail)
    return outs, (h, g, u, a)


def _ffn_bwd(l, x, gain, wg, wu, wd, saved, dxo, emit, after=()):
    h, g, u, a = saved
    T = x.shape[0]
    tm = _big_tile(T)
    dg, du = _ffn_act_bwd(f"ffn{l}_act_bwd", dxo, wd, g, u, tm, after=after)
    tk = min(512, T)
    tn = D_MODEL
    shards_spec = BS((N_CHIPS, tk, FF_SHARD), lambda k: (0, k, 0))
    rows_spec = BS((tk, D_MODEL), lambda k: (k, 0))

    def dw(nm, act, rows):
        return _grad_shards(nm, act, rows, shards_spec, rows_spec, lambda a_ref, b_ref, j: (a_ref[j], b_ref[...]),
                            (N_CHIPS, FF_SHARD, D_MODEL), T // tk)

    behind = emit(f"ffn{l}", {f"ffn_w_gate{l}": dw(f"ffn{l}_dwg", dg, h), f"ffn_w_up{l}": dw(f"ffn{l}_dwu", du, h),
                              f"ffn_w_down{l}": dw(f"ffn{l}_dwd", a, dxo)})
    tm = _resident_tile(T)
    act_spec = BS((N_CHIPS, tm, FF_SHARD), lambda i, j, k: (0, i, 0))
    w_spec = _resident((N_CHIPS, FF_SHARD, D_MODEL))
    tail, shapes, specs = _norm_bwd_tail(x, gain, dxo, tm)
    return _matmul(f"ffn{l}_dh", "nn", [(dg, wg), (du, wu)], [(act_spec, w_spec), (act_spec, w_spec)],
                   (T // tm, 1, 1), shapes, specs, (tm, D_MODEL), after=behind, tail=tail)


_INV_SQRT2 = 1.0 / math.sqrt(2.0)
_INV_SQRT_2PI = 1.0 / math.sqrt(2.0 * math.pi)


def _gelu(x):
    return 0.5 * x * (1.0 + lax.erf(x * _INV_SQRT2))


def _gelu_and_grad(x):
    cdf = 0.5 * (1.0 + lax.erf(x * _INV_SQRT2))
    return x * cdf, cdf + x * jnp.exp(-0.5 * x * x) * _INV_SQRT_2PI


def _shift_down(x, k):
    return pltpu.roll(x, k, 0)


def _shift_up(x, k):
    return pltpu.roll(x, x.shape[0] - k, 0)


def _layer_norm_head(xh):
    xc = xh - jnp.mean(xh, axis=-1, keepdims=True)
    rstd = lax.rsqrt(jnp.mean(xc * xc, axis=-1, keepdims=True) + EPS)
    return xc * rstd, rstd


def _even_in(h, w, tm):
    T = h.shape[0]
    shard = w.shape[-1]

    def body(h_ref, w_ref, o_ref):
        hv = h_ref[...]
        for j in range(N_CHIPS):
            o_ref[:, shard * j:shard * (j + 1)] = _dot(hv, w_ref[j])

    return _call(body, name="even_in", grid=(T // tm,),
                 in_specs=[BS((tm, D_MODEL), lambda i: (i, 0)), BS(w.shape, lambda i: (0, 0, 0), pipeline_mode=pl.Buffered(1))],
                 out_specs=BS((tm, N_CHIPS * shard), lambda i: (i, 0)), out_shape=_sds((T, N_CHIPS * shard), F32))(h, w)


def _even_halo_specs(tm, n_tiles, col_blocks, after):
    rows = tm // SUBLANES
    last = n_tiles * rows - 1
    if after:
        return [BS((SUBLANES, 512), functools.partial(lambda cb, i: (jnp.minimum((i + 1) * rows, last), cb), cb)) for cb in col_blocks]
    return [BS((SUBLANES, 512), functools.partial(lambda cb, i: (jnp.maximum(i * rows - 1, 0), cb), cb)) for cb in col_blocks]


def _even_mixer_fwd(proj, ln_g, w_tril, b_lanes, conv_w, seq, tm):
    T = proj.shape[0]
    tiles_per_seq = seq // tm

    def body(p_ref, hc_ref, hh_ref, lng_ref, w_ref, bb_ref, cw_ref, o_ref):
        first = pl.program_id(0) % tiles_per_seq == 0
        for h in range(SG_HEADS):
            cols = slice(SG_DIM * h, SG_DIM * (h + 1))
            vhat, _ = _layer_norm_head(_gelu(p_ref[:, SG_WIDTH + SG_DIM * h:SG_WIDTH + SG_DIM * (h + 1)]))
            vln = (vhat * lng_ref[:, cols]).astype(BF16)
            for k in range(tm // SG_CHUNK):
                rows = slice(SG_CHUNK * k, SG_CHUNK * (k + 1))
                mixed = _dot(w_ref[h], vln[rows]) + bb_ref[h]
                o_ref[rows, cols] = (_gelu(p_ref[rows, cols]) * mixed).astype(BF16)
        z = p_ref[:, 1536:2048] * p_ref[:, 2048:2560]
        zz = jnp.concatenate([jnp.where(first, 0.0, hc_ref[...] * hh_ref[...]), z], axis=0)
        y = cw_ref[0:1, :] * _shift_down(zz, 2)[SUBLANES:] + cw_ref[1:2, :] * _shift_down(zz, 1)[SUBLANES:] + cw_ref[2:3, :] * z
        o_ref[:, SG_WIDTH:] = (p_ref[:, 1024:1536] * y).astype(BF16)

    full = lambda shape: BS(shape, lambda i: (0,) * len(shape))
    return _call(body, name="even_mixer_fwd", grid=(T // tm,),
                 in_specs=[BS((tm, EVEN_IN), lambda i: (i, 0))] + _even_halo_specs(tm, T // tm, (3, 4), after=False)
                 + [full((1, SG_WIDTH)), full((SG_HEADS, SG_CHUNK, SG_CHUNK)), full((SG_HEADS, SG_CHUNK, SG_DIM)), full((SUBLANES, SC_WIDTH))],
                 out_specs=BS((tm, D_MODEL), lambda i: (i, 0)), out_shape=_sds((T, D_MODEL), BF16))(
        proj, proj, proj, ln_g, w_tril, b_lanes, conv_w)


def _even_mixer_bwd(proj, dmix, ln_g, w_tril, b_lanes, conv_w, seq, tm):
    T = proj.shape[0]
    n_tiles, tiles_per_seq = T // tm, seq // tm

    def body(p_ref, dm_ref, hc_ref, hh_ref, nd_ref, nb_ref, lng_ref, w_ref, bb_ref, cw_ref,
             dp_ref, dw_ref, db_ref, dlng_ref, dcw_ref):
        i = pl.program_id(0)
        first = i % tiles_per_seq == 0
        last = i % tiles_per_seq == tiles_per_seq - 1

        @pl.when(i == 0)
        def _():
            dw_ref[...] = jnp.zeros_like(dw_ref)
            db_ref[...] = jnp.zeros_like(db_ref)
            dlng_ref[...] = jnp.zeros_like(dlng_ref)
            dcw_ref[...] = jnp.zeros_like(dcw_ref)

        for h in range(SG_HEADS):
            cols = slice(SG_DIM * h, SG_DIM * (h + 1))
            vcols = slice(SG_WIDTH + SG_DIM * h, SG_WIDTH + SG_DIM * (h + 1))
            lng = lng_ref[:, cols]
            for k in range(tm // SG_CHUNK):
                rows = slice(SG_CHUNK * k, SG_CHUNK * (k + 1))
                gelu_v, dgelu_v = _gelu_and_grad(p_ref[rows, vcols])
                vhat, rstd = _layer_norm_head(gelu_v)
                vln = (vhat * lng).astype(BF16)
                mixed = _dot(w_ref[h], vln) + bb_ref[h]
                gelu_u, dgelu_u = _gelu_and_grad(p_ref[rows, cols])
                da = dm_ref[rows, cols]
                dp_ref[rows, cols] = (da * mixed * dgelu_u).astype(BF16)
                dmixed = da * gelu_u
                db_ref[h] += dmixed
                dw_ref[h] += _dot(dmixed, vln, "nt")
                dvln = _dot(w_ref[h], dmixed, "tn")
                dlng_ref[:, cols] += jnp.sum(dvln * vhat, axis=0, keepdims=True)
                dvhat = dvln * lng
                dgv = rstd * (dvhat - jnp.mean(dvhat, axis=-1, keepdims=True)
                              - vhat * jnp.mean(dvhat * vhat, axis=-1, keepdims=True))
                dp_ref[rows, vcols] = (dgv * dgelu_v).astype(BF16)

        b = p_ref[:, 1024:1536]
        c = p_ref[:, 1536:2048]
        hv = p_ref[:, 2048:2560]
        z = c * hv
        zz = jnp.concatenate([jnp.where(first, 0.0, hc_ref[...] * hh_ref[...]), z], axis=0)
        z1 = _shift_down(zz, 1)[SUBLANES:]
        z2 = _shift_down(zz, 2)[SUBLANES:]
        w0, w1, w2 = cw_ref[0:1, :], cw_ref[1:2, :], cw_ref[2:3, :]
        dbo = dm_ref[:, SG_WIDTH:]
        dy = dbo * b
        dd = jnp.concatenate([dy, jnp.where(last, 0.0, nd_ref[...] * nb_ref[...])], axis=0)
        dz = w2 * dy + w1 * _shift_up(dd, 1)[:tm] + w0 * _shift_up(dd, 2)[:tm]
        dp_ref[:, 1024:1536] = (dbo * (w0 * z2 + w1 * z1 + w2 * z)).astype(BF16)
        dp_ref[:, 1536:2048] = (dz * hv).astype(BF16)
        dp_ref[:, 2048:2560] = (dz * c).astype(BF16)
        dcw_ref[0:1, :] += jnp.sum(dy * z2, axis=0, keepdims=True)
        dcw_ref[1:2, :] += jnp.sum(dy * z1, axis=0, keepdims=True)
        dcw_ref[2:3, :] += jnp.sum(dy * z, axis=0, keepdims=True)

        @pl.when(i == n_tiles - 1)
        def _():
            t_idx = lax.broadcasted_iota(jnp.int32, (SG_CHUNK, SG_CHUNK), 0)
            s_idx = lax.broadcasted_iota(jnp.int32, (SG_CHUNK, SG_CHUNK), 1)
            for h in range(SG_HEADS):
                dw_ref[h] = jnp.where(t_idx >= s_idx, dw_ref[h], 0.0)

    full = lambda shape: BS(shape, lambda i: (0,) * len(shape))
    sq = (SG_HEADS, SG_CHUNK, SG_CHUNK)
    return _call(body, name="even_mixer_bwd", grid=(n_tiles,),
                 in_specs=[BS((tm, EVEN_IN), lambda i: (i, 0)), BS((tm, D_MODEL), lambda i: (i, 0))]
                 + _even_halo_specs(tm, n_tiles, (3, 4), after=False)
                 + _even_halo_specs(tm, n_tiles, (1,), after=True) + _even_halo_specs(tm, n_tiles, (2,), after=True)
                 + [full((1, SG_WIDTH)), full(sq), full(sq), full((SUBLANES, SC_WIDTH))],
                 out_specs=[BS((tm, EVEN_IN), lambda i: (i, 0)), full(sq), full(sq), full((1, SG_WIDTH)), full((SUBLANES, SC_WIDTH))],
                 out_shape=[_sds((T, EVEN_IN), BF16), _sds(sq, F32), _sds(sq, F32), _sds((1, SG_WIDTH), F32), _sds((SUBLANES, SC_WIDTH), F32)])(
        proj, dmix, proj, proj, dmix, proj, ln_g, w_tril, b_lanes, conv_w)


def _pool_select(vals):
    lane = lax.broadcasted_iota(jnp.int32, vals[0].shape, 1)
    out = vals[-1]
    for g in range(len(vals) - 2, -1, -1):
        out = jnp.where(lane < POOL_DIM * (g + 1), vals[g], out)
    return out


def _pool_counts(pos1):
    lane = lax.broadcasted_iota(jnp.int32, (pos1.shape[0], POOL_WIDTH), 1)
    win = _pool_select([jnp.full(lane.shape, float(w), F32) for w in POOL_WINDOWS])
    return jnp.minimum(pos1, win)


def _pool_means(zz, counts):
    s2 = zz + _shift_down(zz, 1)
    s4 = s2 + _shift_down(s2, 2)
    s8 = s4 + _shift_down(s4, 4)
    s16 = s8 + _shift_down(s8, 8)
    return _pool_select([s2, s4, s8, s16])[POOL_HALO:] / counts


def _pool_halo_spec(tm, n_tiles, after):
    rows = tm // POOL_HALO
    if after:
        return BS((POOL_HALO, POOL_WIDTH), lambda i: (jnp.minimum((i + 1) * rows, n_tiles * rows - 1), 0))
    return BS((POOL_HALO, POOL_WIDTH), lambda i: (jnp.maximum(i * rows - 1, 0), 0))


def _pool_fwd(proj, w_diag, scale, seq, tm):
    T = proj.shape[0]
    tiles_per_seq = seq // tm

    def body(z_ref, zh_ref, w_ref, s_ref, o_ref):
        t = pl.program_id(0) % tiles_per_seq
        z = z_ref[...]
        zz = jnp.concatenate([jnp.where(t == 0, 0.0, zh_ref[...]), z], axis=0)
        pos1 = (lax.broadcasted_iota(jnp.int32, (tm, 1), 0) + (t * tm + 1)).astype(F32)
        pooled = _pool_means(zz, _pool_counts(pos1)) - z
        o_ref[...] = (_dot(pooled, w_ref[...]) * s_ref[...]).astype(BF16)

    full = lambda shape: BS(shape, lambda i: (0,) * len(shape))
    return _call(body, name="pool_fwd", grid=(T // tm,),
                 in_specs=[BS((tm, POOL_WIDTH), lambda i: (i, 0)), _pool_halo_spec(tm, T // tm, False),
                           full((POOL_WIDTH, POOL_WIDTH)), full((1, POOL_WIDTH))],
                 out_specs=BS((tm, POOL_WIDTH), lambda i: (i, 0)), out_shape=_sds((T, D_MODEL), BF16))(proj, proj, w_diag, scale)


def _pool_bwd(proj, dmix, w_diag, scale, seq, tm):
    T = proj.shape[0]
    n_tiles, tiles_per_seq = T // tm, seq // tm

    def body(z_ref, zh_ref, do_ref, don_ref, w_ref, s_ref, dz_ref, dw_ref, ds_ref):
        i = pl.program_id(0)
        t = i % tiles_per_seq

        @pl.when(i == 0)
        def _():
            dw_ref[...] = jnp.zeros_like(dw_ref)
            ds_ref[...] = jnp.zeros_like(ds_ref)

        z = z_ref[...]
        zz = jnp.concatenate([jnp.where(t == 0, 0.0, zh_ref[...]), z], axis=0)
        pos1 = (lax.broadcasted_iota(jnp.int32, (tm, 1), 0) + (t * tm + 1)).astype(F32)
        counts = _pool_counts(pos1)
        pooled = _pool_means(zz, counts) - z
        dout = do_ref[...].astype(F32)
        ds_ref[...] += jnp.sum(dout * _dot(pooled, w_ref[...]), axis=0, keepdims=True)
        dlin = dout * s_ref[...]
        dw_ref[...] += _dot(pooled, dlin, "tn")
        dpooled = _dot(dlin, w_ref[...], "nt")
        dpooled_n = _dot(don_ref[...].astype(F32) * s_ref[...], w_ref[...], "nt")
        pos1_n = (lax.broadcasted_iota(jnp.int32, (POOL_HALO, 1), 0) + ((t + 1) * tm + 1)).astype(F32)
        dmean_n = jnp.where(t == tiles_per_seq - 1, 0.0, dpooled_n / _pool_counts(pos1_n))
        dd = jnp.concatenate([dpooled / counts, dmean_n], axis=0)
        r2 = dd + _shift_up(dd, 1)
        r4 = r2 + _shift_up(r2, 2)
        r8 = r4 + _shift_up(r4, 4)
        r16 = r8 + _shift_up(r8, 8)
        dz_ref[...] = (_pool_select([r2, r4, r8, r16])[:tm] - dpooled).astype(BF16)

    full = lambda shape: BS(shape, lambda i: (0,) * len(shape))
    return _call(body, name="pool_bwd", grid=(n_tiles,),
                 in_specs=[BS((tm, POOL_WIDTH), lambda i: (i, 0)), _pool_halo_spec(tm, n_tiles, False),
                           BS((tm, POOL_WIDTH), lambda i: (i, 0)), _pool_halo_spec(tm, n_tiles, True),
                           full((POOL_WIDTH, POOL_WIDTH)), full((1, POOL_WIDTH))],
                 out_specs=[BS((tm, POOL_WIDTH), lambda i: (i, 0)), full((POOL_WIDTH, POOL_WIDTH)), full((1, POOL_WIDTH))],
                 out_shape=[_sds((T, POOL_WIDTH), BF16), _sds((POOL_WIDTH, POOL_WIDTH), F32), _sds((1, POOL_WIDTH), F32)])(
        proj, proj, dmix, dmix, w_diag, scale)


def _rope_partner(r):
    lane = lax.broadcasted_iota(jnp.int32, r.shape, 1)
    return jnp.where(lane < QK_ROPE // 2, pltpu.roll(r, LANES - QK_ROPE // 2, 1), pltpu.roll(r, QK_ROPE // 2, 1))


def _rope(x, cos, sin_signed):
    r = x[:, QK_NOPE:]
    return jnp.concatenate([x[:, :QK_NOPE], r * cos + _rope_partner(r) * sin_signed], axis=1)


def _rope_transposed(dx, cos, sin_signed):
    dr = dx[:, QK_NOPE:]
    return jnp.concatenate([dx[:, :QK_NOPE], dr * cos + _rope_partner(dr * sin_signed)], axis=1)


def _head_norm(x):
    r = lax.rsqrt(jnp.sum(x * x, axis=-1, keepdims=True) * (1.0 / QK_DIM) + EPS)
    return x * r, r


def _head_norm_bwd(dy, xhat, r, gain):
    dxhat = dy * gain
    return r * (dxhat - xhat * (jnp.sum(dxhat * xhat, axis=-1, keepdims=True) * (1.0 / QK_DIM)))


def _latents(p_ref, qag_ref, kvag_ref):
    ql = p_ref[:, POOL_WIDTH:POOL_WIDTH + Q_LORA]
    kvl = p_ref[:, POOL_WIDTH + Q_LORA:POOL_WIDTH + Q_LORA + KV_LORA]
    rq = lax.rsqrt(jnp.mean(ql * ql, axis=-1, keepdims=True) + EPS)
    rkv = lax.rsqrt(jnp.mean(kvl * kvl, axis=-1, keepdims=True) + EPS)
    return ql * rq, rq, kvl * rkv, rkv


def _mla_specs(tm):
    full = lambda shape: BS(shape, lambda i, h: (0,) * len(shape))
    return [BS((tm, ODD_IN_PAD), lambda i, h: (i, 0)), BS((tm, LANES), lambda i, h: (i, 0)), BS((tm, LANES), lambda i, h: (i, 0)),
            full((1, Q_LORA)), full((1, KV_LORA)), BS((None, Q_LORA, QK_PAD), lambda i, h: (h, 0, 0)),
            BS((None, KV_LORA, QK_PAD), lambda i, h: (h, 0, 0)), full((1, QK_PAD)), full((1, QK_PAD))]


def _mla_qkv_fwd(proj, cos, sin_signed, qa_g, kva_g, q_b, kv_b, q_g, k_g, tm):
    T = proj.shape[0]

    def body(p_ref, cos_ref, sin_ref, qag_ref, kvag_ref, qb_ref, kvb_ref, qg_ref, kg_ref, q_ref, k_ref, v_ref, qn_s, kvn_s):
        @pl.when(pl.program_id(1) == 0)
        def _():
            qhat, _, kvhat, _ = _latents(p_ref, qag_ref, kvag_ref)
            qn_s[...] = (qhat * qag_ref[...]).astype(BF16)
            kvn_s[...] = (kvhat * kvag_ref[...]).astype(BF16)

        cos, sin = cos_ref[...], sin_ref[...]
        qhat, _ = _head_norm(_dot(qn_s[...], qb_ref[...]))
        q_ref[...] = _rope(qhat * qg_ref[...], cos, sin).astype(BF16)
        kv = _dot(kvn_s[...], kvb_ref[...])
        khat, _ = _head_norm(jnp.concatenate([kv[:, :QK_NOPE], p_ref[:, ODD_IN_PAD - LANES:]], axis=1))
        k_ref[...] = _rope(khat * kg_ref[...], cos, sin).astype(BF16)
        v_ref[...] = kv[:, QK_NOPE:].astype(BF16)

    qk_spec = BS((None, tm, QK_PAD), lambda i, h: (h, i, 0))
    return _call(body, name="mla_qkv_fwd", grid=(T // tm, HEADS), in_specs=_mla_specs(tm),
                 out_specs=[qk_spec, qk_spec, BS((None, tm, V_DIM), lambda i, h: (h, i, 0))],
                 out_shape=[_sds((HEADS, T, QK_PAD), BF16), _sds((HEADS, T, QK_PAD), BF16), _sds((HEADS, T, V_DIM), BF16)],
                 scratch=[pltpu.VMEM((tm, Q_LORA), BF16), pltpu.VMEM((tm, KV_LORA), BF16)])(
        proj, cos, sin_signed, qa_g, kva_g, q_b, kv_b, q_g, k_g)


def _mla_qkv_bwd(proj, cos, sin_signed, qa_g, kva_g, q_b, kv_b, q_g, k_g, dq, dk, dv, dz_pool, tm):
    T = proj.shape[0]
    n_tiles = T // tm
    chain_rows = min(_PASS_ROWS, tm)

    def body(p_ref, cos_ref, sin_ref, qag_ref, kvag_ref, qb_ref, kvb_ref, qg_ref, kg_ref, dq_ref, dk_ref, dv_ref, dzp_ref,
             dp_ref, dqb_ref, dkvb_ref, dqg_ref, dkg_ref, dqag_ref, dkvag_ref, qn_s, kvn_s, dqn_s, dkvn_s, dkr_s,
             qh_s, kv_s, dqh_s, dkv_s):
        i, h = pl.program_id(0), pl.program_id(1)

        @pl.when((i == 0) & (h == 0))
        def _():
            for ref in (dqb_ref, dkvb_ref, dqg_ref, dkg_ref, dqag_ref, dkvag_ref):
                ref[...] = jnp.zeros_like(ref)

        @pl.when(h == 0)
        def _():
            qhat, _, kvhat, _ = _latents(p_ref, qag_ref, kvag_ref)
            qn_s[...] = (qhat * qag_ref[...]).astype(BF16)
            kvn_s[...] = (kvhat * kvag_ref[...]).astype(BF16)
            dqn_s[...] = jnp.zeros_like(dqn_s)
            dkvn_s[...] = jnp.zeros_like(dkvn_s)
            dkr_s[...] = jnp.zeros_like(dkr_s)

        qh_s[...] = _dot(qn_s[...], qb_ref[...])
        kv_s[...] = _dot(kvn_s[...], kvb_ref[...])
        qg, kg = qg_ref[...], kg_ref[...]

        def chunk(c, gains):
            dqg, dkg = gains
            rows = slice(c * chain_rows, (c + 1) * chain_rows)
            cos, sin = cos_ref[rows, :], sin_ref[rows, :]
            qhat, rq = _head_norm(qh_s[rows, :])
            dqn_head = _rope_transposed(dq_ref[rows, :], cos, sin)
            dqh_s[rows, :] = _head_norm_bwd(dqn_head, qhat, rq, qg).astype(BF16)
            kv = kv_s[rows, :]
            khat, rk = _head_norm(jnp.concatenate([kv[:, :QK_NOPE], p_ref[rows, ODD_IN_PAD - LANES:]], axis=1))
            dkn_head = _rope_transposed(dk_ref[rows, :], cos, sin)
            dkf = _head_norm_bwd(dkn_head, khat, rk, kg)
            dkr_s[rows, :] += dkf[:, QK_NOPE:]
            dkv_s[rows, :] = jnp.concatenate([dkf[:, :QK_NOPE], dv_ref[rows, :]], axis=1).astype(BF16)
            return dqg + dqn_head * qhat, dkg + dkn_head * khat

        dqg = dkg = jnp.zeros((chain_rows, QK_PAD), F32)
        for c in range(tm // chain_rows):
            dqg, dkg = chunk(c, (dqg, dkg))
        dqg_ref[...] += jnp.sum(dqg, axis=0, keepdims=True)
        dkg_ref[...] += jnp.sum(dkg, axis=0, keepdims=True)
        dqb_ref[h] += _dot(qn_s[...], dqh_s[...], "tn")
        dqn_s[...] += _dot(dqh_s[...], qb_ref[...], "nt")
        dkvb_ref[h] += _dot(kvn_s[...], dkv_s[...], "tn")
        dkvn_s[...] += _dot(dkv_s[...], kvb_ref[...], "nt")

        @pl.when(h == HEADS - 1)
        def _():
            qhat_l, rql, kvhat_l, rkvl = _latents(p_ref, qag_ref, kvag_ref)
            dqn, dkvn = dqn_s[...], dkvn_s[...]
            dqag_ref[...] += jnp.sum(dqn * qhat_l, axis=0, keepdims=True)
            dkvag_ref[...] += jnp.sum(dkvn * kvhat_l, axis=0, keepdims=True)
            dqx, dkvx = dqn * qag_ref[...], dkvn * kvag_ref[...]
            dp_ref[:, :POOL_WIDTH] = dzp_ref[...]
            dp_ref[:, POOL_WIDTH:POOL_WIDTH + Q_LORA] = (
                rql * (dqx - qhat_l * jnp.mean(dqx * qhat_l, axis=-1, keepdims=True))).astype(BF16)
            dp_ref[:, POOL_WIDTH + Q_LORA:ODD_IN_PAD - LANES] = (
                rkvl * (dkvx - kvhat_l * jnp.mean(dkvx * kvhat_l, axis=-1, keepdims=True))).astype(BF16)
            dp_ref[:, ODD_IN_PAD - LANES:] = dkr_s[:, :QK_ROPE].astype(BF16)

    full = lambda shape: BS(shape, lambda i, h: (0,) * len(shape))
    qk_spec = BS((None, tm, QK_PAD), lambda i, h: (h, i, 0))
    return _call(body, name="mla_qkv_bwd", grid=(n_tiles, HEADS),
                 in_specs=_mla_specs(tm) + [qk_spec, qk_spec, BS((None, tm, V_DIM), lambda i, h: (h, i, 0)),
                                            BS((tm, POOL_WIDTH), lambda i, h: (i, 0))],
                 out_specs=[BS((tm, ODD_IN), lambda i, h: (i, 0)), full((HEADS, Q_LORA, QK_PAD)), full((HEADS, KV_LORA, QK_PAD)),
                            full((1, QK_PAD)), full((1, QK_PAD)), full((1, Q_LORA)), full((1, KV_LORA))],
                 out_shape=[_sds((T, ODD_IN), BF16),_sds((HEADS, Q_LORA, QK_PAD), F32), _sds((HEADS, KV_LORA, QK_PAD), F32),
                            _sds((1, QK_PAD), F32), _sds((1, QK_PAD), F32), _sds((1, Q_LORA), F32), _sds((1, KV_LORA), F32)],
                 scratch=[pltpu.VMEM((tm, Q_LORA), BF16), pltpu.VMEM((tm, KV_LORA), BF16), pltpu.VMEM((tm, Q_LORA), F32),
                          pltpu.VMEM((tm, KV_LORA), F32), pltpu.VMEM((tm, LANES), F32), pltpu.VMEM((tm, QK_PAD), F32),
                          pltpu.VMEM((tm, QK_PAD), F32), pltpu.VMEM((tm, QK_PAD), BF16), pltpu.VMEM((tm, QK_PAD), BF16)])(
        proj, cos, sin_signed, qa_g, kva_g, q_b, kv_b, q_g, k_g, dq, dk, dv, dz_pool)


def _attn_tile(seq):
    return 512 if seq % 512 == 0 else seq


def _causal_mask(s):
    row = lax.broadcasted_iota(jnp.int32, s.shape, 0)
    col = lax.broadcasted_iota(jnp.int32, s.shape, 1)
    return jnp.where(row >= col, s, NEG_INF)


def _tile(i, t):
    return slice(i * t, (i + 1) * t)


def _flash_fwd(q, k, v, mix, batch, seq):
    t = _attn_tile(seq)
    nq = seq // t

    def body(q_ref, k_ref, v_ref, _, o_ref, lse_ref):
        for qi in range(nq):
            rows, before = _tile(qi, t), slice(0, qi * t)
            qv = q_ref[rows, :]
            s_diag = _causal_mask(_dot(qv, k_ref[rows, :], "nt") * ATTN_SCALE)
            m = jnp.max(s_diag, axis=-1, keepdims=True)
            if qi:
                s_before = _dot(qv, k_ref[before, :], "nt") * ATTN_SCALE
                m = jnp.maximum(m, jnp.max(s_before, axis=-1, keepdims=True))
            p = jnp.exp(s_diag - m)
            l = jnp.sum(p, axis=-1, keepdims=True)
            acc = _dot(p, v_ref[rows, :])
            if qi:
                p = jnp.exp(s_before - m)
                l = l + jnp.sum(p, axis=-1, keepdims=True)
                acc = acc + _dot(p, v_ref[before, :])
            o_ref[rows, :] = (acc / l).astype(BF16)
            lse_ref[rows, :] = jnp.broadcast_to(m + jnp.log(l), (t, LANES))

    T = batch * seq
    whole = lambda w: BS((None, seq, w), lambda b, h: (h, b, 0))
    return _call(body, name="flash_fwd", grid=(batch, HEADS),
                 in_specs=[whole(QK_PAD), whole(QK_PAD), whole(V_DIM), pl.BlockSpec(memory_space=pl.ANY)],
                 out_specs=[BS((seq, V_DIM), lambda b, h: (b, POOL_WIDTH // V_DIM + h)), whole(LANES)],
                 out_shape=[_sds((T, D_MODEL), BF16), _sds((HEADS, T, LANES), F32)],
                 aliases={3: 0})(q, k, v, mix)


def _flash_bwd(q, k, v, dmix, mix, lse, batch, seq):
    t = _attn_tile(seq)
    nq = seq // t

    def body(q_ref, k_ref, v_ref, do_ref, o_ref, lse_ref, dq_ref, dk_ref, dv_ref):
        for qi in range(nq):
            rows, before = _tile(qi, t), slice(0, qi * t)
            qv, do = q_ref[rows, :], do_ref[rows, :]
            lse = lse_ref[rows, 0:1]
            delta = jnp.sum(do.astype(F32) * o_ref[rows, :].astype(F32), axis=-1, keepdims=True)

            def block(keys, masked):
                kk = k_ref[keys, :]
                s = _dot(qv, kk, "nt") * ATTN_SCALE
                p = jnp.exp((_causal_mask(s) if masked else s) - lse)
                ds = p * (_dot(do, v_ref[keys, :], "nt") - delta) * ATTN_SCALE
                return _dot(p, do, "tn"), _dot(ds, qv, "tn"), _dot(ds, kk)

            dv_ref[rows, :], dk_ref[rows, :], dq = block(rows, True)
            if qi:
                dv, dk, dq_before = block(before, False)
                dv_ref[before, :] += dv
                dk_ref[before, :] += dk
                dq = dq + dq_before
            dq_ref[rows, :] = dq

    T = batch * seq
    whole = lambda w: BS((None, seq, w), lambda b, h: (h, b, 0))
    head_cols = BS((seq, V_DIM), lambda b, h: (b, POOL_WIDTH // V_DIM + h))
    return _call(body, name="flash_bwd", grid=(batch, HEADS),
                 in_specs=[whole(QK_PAD), whole(QK_PAD), whole(V_DIM), head_cols, head_cols, whole(LANES)],
                 out_specs=[whole(QK_PAD), whole(QK_PAD), whole(V_DIM)],
                 out_shape=[_sds((HEADS, T, QK_PAD), F32), _sds((HEADS, T, QK_PAD), F32), _sds((HEADS, T, V_DIM), F32)])(
        q, k, v, dmix, mix, lse)


def _adamw_math(w, g, m, v):
    m = ADAM_B1 * m + (1.0 - ADAM_B1) * g
    v = ADAM_B2 * v + (1.0 - ADAM_B2) * (g * g)
    m_hat = m / (1.0 - ADAM_B1 ** ADAM_STEP)
    v_hat = v / (1.0 - ADAM_B2 ** ADAM_STEP)
    return -ADAM_LR * (m_hat / (jnp.sqrt(v_hat) + ADAM_EPS) + ADAM_WD * w), m, v


def _adamw(name, w, g, m, v, l=0, prev=()):
    L, R, C = w.shape
    tr = 256 if R % 256 == 0 else R

    def body(w_ref, g_ref, m_ref, v_ref, *rest):
        go_ref, d_ref, mo_ref, vo_ref, token = rest[-5:]
        gv = g_ref[...]
        d_ref[...], mo_ref[...], vo_ref[...] = _adamw_math(w_ref[...], gv, m_ref[...], v_ref[...])
        go_ref[...] = gv
        token[...] = jnp.zeros_like(token)

    layer = BS((None, tr, C), lambda i: (l, i, 0))
    return _call(body, name=f"{name}_{l}", grid=(R // tr,),
                 in_specs=[layer, BS((tr, C), lambda i: (i, 0)), layer, layer] + [pl.BlockSpec(memory_space=pl.ANY)] * len(prev),
                 out_specs=[layer] * 4 + [BS((SUBLANES, LANES), lambda i: (0, 0))],
                 out_shape=[_sds((L, R, C), F32)] * 4 + [_sds((SUBLANES, LANES), F32)],
                 aliases={4 + n: n for n in range(len(prev))})(w, g, m, v, *prev)


def _place():
    x, y, c = lax.axis_index("x"), lax.axis_index("y"), lax.axis_index("c")
    other_chips = [(1 - x, y), (x, 1 - y), (1 - x, 1 - y)]
    return x, y, c, other_chips


def _remote(src, dst, send_sem, recv_sem, dev):
    return pltpu.make_async_remote_copy(src_ref=src, dst_ref=dst, send_sem=send_sem, recv_sem=recv_sem,
                                        device_id=dev, device_id_type=MESH)


def _prefetch_call(body, *, name, grid, in_specs, out_specs, out_shape):
    grid_spec = pltpu.PrefetchScalarGridSpec(num_scalar_prefetch=1, grid=grid, in_specs=in_specs, out_specs=out_specs)
    params = pltpu.CompilerParams(vmem_limit_bytes=VMEM_LIMIT_V7X, dimension_semantics=("arbitrary",) * len(grid))
    return pl.pallas_call(body, name=name, grid_spec=grid_spec, out_shape=out_shape, compiler_params=params)


def _row_tile(rows):
    return 256 if rows % 256 == 0 else rows


def _cast_place(name, w, layer, chip, after=()):
    _, _, rows, C = w.shape
    tr = _row_tile(rows)

    def body(chip_ref, w_ref, *rest):
        rest[-1][...] = w_ref[...].astype(BF16)

    return _prefetch_call(body, name=name, grid=(2, rows // tr),
                          in_specs=[BS((None, None, tr, C), lambda h, i, chip_ref: (layer, h, i, 0))]
                          + [pl.BlockSpec(memory_space=pl.ANY)] * len(after),
                          out_specs=BS((None, None, tr, C), lambda h, i, chip_ref: (chip_ref[0], h, i, 0)),
                          out_shape=pltpu.HBM((N_CHIPS, 2, rows, C), BF16))(chip, w, *after)


SEM = pl.BlockSpec(memory_space=pltpu.SEMAPHORE)


def _split_copy_call(body, *, name, in_specs, out_specs, out_shape, aliases):
    return pl.pallas_call(body, name=name, in_specs=in_specs, out_specs=out_specs, out_shape=out_shape,
                          input_output_aliases=aliases,
                          compiler_params=pltpu.CompilerParams(has_side_effects=pltpu.SideEffectType.DATAFLOW_SIDE_EFFECTING))


def _hbm(arrays):
    return [pltpu.with_memory_space_constraint(a, pltpu.HBM) for a in arrays]


def _gather_send(name, gs, groups, after):
    n = len(gs)

    def body(*refs):
        g, sems, token = refs[:n], refs[n + len(after):n + len(after) + 2 * len(groups)], refs[-1]
        x, y, c, chips = _place()
        me = 2 * x + y
        for gi, members in enumerate(groups):
            for a, i in enumerate(members):
                for k, (px, py) in enumerate(chips):
                    _remote(g[i].at[me, c], g[i].at[me, c], sems[2 * gi].at[3 * a + k], sems[2 * gi + 1].at[3 * a + k],
                            (px, py, c)).start()
        token[...] = jnp.zeros_like(token)

    sem_shapes = [pltpu.SemaphoreType.DMA((3 * len(members),)) for members in groups for _ in range(2)]
    out = _split_copy_call(body, name=name, in_specs=[HBM] * n + [pl.BlockSpec(memory_space=pl.ANY)] * len(after),
                           out_specs=[SEM] * len(sem_shapes) + [HBM] * n + [VMEM],
                           out_shape=sem_shapes + [pltpu.HBM(a.shape, a.dtype) for a in gs] + [_sds((SUBLANES, LANES), F32)],
                           aliases={i: len(sem_shapes) + i for i in range(n)})(*_hbm(gs), *after)
    return out[:len(sem_shapes)], out[len(sem_shapes):-1], out[-1]


def _gather_wait(name, gs, send_sems, recv_sems, after):
    n = len(gs)

    def body(*refs):
        g, ssem, rsem = refs[:n], refs[n], refs[n + 1]
        x, y, c, chips = _place()
        me = 2 * x + y
        for a in range(n):
            for k, (px, py) in enumerate(chips):
                landed = g[a].at[2 * px + py, c]
                cp = _remote(g[a].at[me, c], landed, ssem.at[3 * a + k], rsem.at[3 * a + k], (px, py, c))
                cp.wait_recv()
                cp.wait_send()

    return _split_copy_call(body, name=name, in_specs=[HBM] * n + [SEM, SEM] + [pl.BlockSpec(memory_space=pl.ANY)] * len(after),
                            out_specs=[HBM] * n, out_shape=[pltpu.HBM(a.shape, a.dtype) for a in gs],
                            aliases={i: i for i in range(n)})(*gs, send_sems, recv_sems, *after)


def _gather_pass(name, gs):
    n = len(gs)

    def body(*refs):
        g, send_sems, recv_sems = refs[n:2 * n], refs[-2], refs[-1]
        x, y, c, chips = _place()
        sibling = (x, y, 1 - c)
        passed = [_remote(g[i].at[2 * px + py, c], g[i].at[2 * px + py, c], send_sems.at[3 * i + k], recv_sems.at[3 * i + k], sibling)
                  for i in range(n) for k, (px, py) in enumerate(chips)]
        for cp in passed:
            cp.start()
        for i in range(n):
            for k, (px, py) in enumerate(chips):
                theirs = g[i].at[2 * px + py, 1 - c]
                _remote(theirs, theirs, send_sems.at[3 * i + k], recv_sems.at[3 * i + k], sibling).wait_recv()
        for cp in passed:
            cp.wait_send()

    return _call(body, name=name, in_specs=[HBM] * n, out_specs=[HBM] * n, out_shape=[_sds(a.shape, a.dtype) for a in gs],
                 aliases={i: i for i in range(n)},
                 scratch=[pltpu.SemaphoreType.DMA((3 * n,)), pltpu.SemaphoreType.DMA((3 * n,))])(*gs)


def _scatter_send(name, ps):
    n = len(ps)

    def body(*refs):
        p, r, ssem, rsem, token = refs[:n], refs[n:2 * n], refs[2 * n], refs[2 * n + 1], refs[-1]
        x, y, c, chips = _place()
        for i in range(n):
            for k, (px, py) in enumerate(chips):
                _remote(p[i].at[2 * px + py], r[i].at[k], ssem.at[3 * i + k], rsem.at[3 * i + k], (px, py, c)).start()
        token[...] = jnp.zeros_like(token)

    lands = [lax.empty((N_CHIPS - 1,) + a.shape[1:], a.dtype) for a in ps]
    sem = pltpu.SemaphoreType.DMA((3 * n,))
    out = _split_copy_call(body, name=name, in_specs=[HBM] * (2 * n), out_specs=[SEM, SEM] + [HBM] * (2 * n) + [VMEM],
                           out_shape=[sem, sem] + [pltpu.HBM(a.shape, a.dtype) for a in list(ps) + lands] + [_sds((SUBLANES, LANES), F32)],
                           aliases={i: 2 + i for i in range(2 * n)})(*_hbm(list(ps) + lands))
    return out[0], out[1], out[2:2 + n], out[2 + n:2 + 2 * n], out[-1]


def _scatter_wait(name, ps, lands, send_sems, recv_sems, after):
    n = len(ps)

    def body(*refs):
        p, r, ssem, rsem = refs[:n], refs[n:2 * n], refs[2 * n], refs[2 * n + 1]
        x, y, c, chips = _place()
        for i in range(n):
            for k, (px, py) in enumerate(chips):
                cp = _remote(p[i].at[2 * px + py], r[i].at[k], ssem.at[3 * i + k], rsem.at[3 * i + k], (px, py, c))
                cp.wait_recv()
                cp.wait_send()

    out = _split_copy_call(body, name=name, in_specs=[HBM] * (2 * n) + [SEM, SEM] + [pl.BlockSpec(memory_space=pl.ANY)] * len(after),
                           out_specs=[HBM] * (2 * n), out_shape=[pltpu.HBM(a.shape, a.dtype) for a in list(ps) + list(lands)],
                           aliases={i: i for i in range(2 * n)})(*ps, *lands, send_sems, recv_sems, *after)
    return out[:n], out[n:]


def _exchange_send(name, gs):
    n = len(gs)

    def body(*refs):
        g, r, ssem, rsem, token = refs[:n], refs[n:2 * n], refs[2 * n], refs[2 * n + 1], refs[-1]
        x, y, c, _ = _place()
        for i in range(n):
            _remote(g[i].at[:, 1 - c], r[i], ssem.at[i], rsem.at[i], (x, y, 1 - c)).start()
        token[...] = jnp.zeros_like(token)

    lands = [lax.empty((a.shape[0],) + a.shape[2:], a.dtype) for a in gs]
    sem = pltpu.SemaphoreType.DMA((n,))
    out = _split_copy_call(body, name=name, in_specs=[HBM] * (2 * n), out_specs=[SEM, SEM] + [HBM] * (2 * n) + [VMEM],
                           out_shape=[sem, sem] + [pltpu.HBM(a.shape, a.dtype) for a in list(gs) + lands] + [_sds((SUBLANES, LANES), F32)],
                           aliases={i: 2 + i for i in range(2 * n)})(*_hbm(list(gs) + lands))
    return out[0], out[1], out[2:2 + n], out[2 + n:2 + 2 * n], out[-1]


def _exchange_wait(name, gs, lands, send_sems, recv_sems, after):
    n = len(gs)

    def body(*refs):
        g, r, ssem, rsem = refs[:n], refs[n:2 * n], refs[2 * n], refs[2 * n + 1]
        x, y, c, _ = _place()
        for i in range(n):
            cp = _remote(g[i].at[:, 1 - c], r[i], ssem.at[i], rsem.at[i], (x, y, 1 - c))
            cp.wait_recv()
            cp.wait_send()

    out = _split_copy_call(body, name=name, in_specs=[HBM] * (2 * n) + [SEM, SEM] + [pl.BlockSpec(memory_space=pl.ANY)] * len(after),
                           out_specs=[HBM] * (2 * n), out_shape=[pltpu.HBM(a.shape, a.dtype) for a in list(gs) + list(lands)],
                           aliases={i: i for i in range(2 * n)})(*gs, *lands, send_sems, recv_sems, *after)
    return out[:n], out[n:]


def _sibling_share(name, fs, after=()):
    n = len(fs)

    def body(*refs):
        f, send_sems, recv_sems = refs[n:2 * n], refs[-2], refs[-1]
        x, y, c, _ = _place()
        sends = [_remote(f[i].at[c], f[i].at[c], send_sems.at[i], recv_sems.at[i], (x, y, 1 - c)) for i in range(n)]
        for cp in sends:
            cp.start()
        for i in range(n):
            theirs = f[i].at[1 - c]
            _remote(theirs, theirs, send_sems.at[i], recv_sems.at[i], (x, y, 1 - c)).wait_recv()
        for cp in sends:
            cp.wait_send()

    return _call(body, name=name, in_specs=[HBM] * n, out_specs=[HBM] * n,
                 out_shape=[_sds(a.shape, a.dtype) for a in fs], aliases={i: i for i in range(n)}, after=after,
                 scratch=[pltpu.SemaphoreType.DMA((n,)), pltpu.SemaphoreType.DMA((n,))])(*fs)


def _all_reduce_small(name, v):
    rows = v.shape[0] // 2
    halves = (2, rows, LANES)

    def body(v_ref, o_ref, from_sibling, chip_sums, send_sems, recv_sems):
        x, y, c, chips = _place()
        me, sibling = 2 * x + y, (x, y, 1 - c)
        swap = _remote(v_ref.at[1 - c], from_sibling, send_sems.at[0], recv_sems.at[0], sibling)
        swap.start()
        swap.wait()
        chip_sums[me] = v_ref[c] + from_sibling[...]
        sends = [_remote(chip_sums.at[me], chip_sums.at[me], send_sems.at[1 + k], recv_sems.at[1 + k], (px, py, c))
                 for k, (px, py) in enumerate(chips)]
        for cp in sends:
            cp.start()
        for k, (px, py) in enumerate(chips):
            theirs = chip_sums.at[2 * px + py]
            _remote(theirs, theirs, send_sems.at[1 + k], recv_sems.at[1 + k], (px, py, c)).wait_recv()
        for cp in sends:
            cp.wait_send()
        acc = chip_sums[0]
        for j in range(1, N_CHIPS):
            acc = acc + chip_sums[j]
        o_ref[c] = acc
        share = _remote(o_ref.at[c], o_ref.at[c], send_sems.at[4], recv_sems.at[4], sibling)
        share.start()
        share.wait_send()
        _remote(o_ref.at[1 - c], o_ref.at[1 - c], send_sems.at[4], recv_sems.at[4], sibling).wait_recv()

    return _call(body, name=name, in_specs=[VMEM], out_specs=VMEM, out_shape=_sds(halves, F32),
                 scratch=[pltpu.VMEM((rows, LANES), F32), pltpu.VMEM((N_CHIPS, rows, LANES), F32),
                          pltpu.SemaphoreType.DMA((5,)), pltpu.SemaphoreType.DMA((5,))])(v.reshape(halves)).reshape(v.shape)


def _add_halves(name, g, r, c):
    _, _, rows, C = g.shape
    tr = _row_tile(rows)

    def body(c_ref, g_ref, r_ref, o_ref):
        o_ref[...] = (g_ref[...].astype(F32) + r_ref[...].astype(F32)).astype(BF16)

    spec = BS((None, tr, C), lambda j, i, c_ref: (j, i, 0))
    return _prefetch_call(body, name=name, grid=(N_CHIPS, rows // tr),
                          in_specs=[BS((None, None, tr, C), lambda j, i, c_ref: (j, c_ref[0], i, 0)), spec], out_specs=spec,
                          out_shape=pltpu.HBM((N_CHIPS, rows, C), BF16))(c, g, r)


def _sum_partials(name, p, r, chip_c):
    _, rows, C = p.shape
    tr = _row_tile(rows)

    def body(s_ref, p_ref, r_ref, o_ref):
        acc = p_ref[...].astype(F32)
        for k in range(N_CHIPS - 1):
            acc = acc + r_ref[k].astype(F32)
        o_ref[...] = acc

    return _prefetch_call(body, name=name, grid=(rows // tr,),
                          in_specs=[BS((None, tr, C), lambda i, s: (s[0], i, 0)), BS((N_CHIPS - 1, tr, C), lambda i, s: (0, i, 0))],
                          out_specs=BS((None, tr, C), lambda i, s: (s[1], i, 0)), out_shape=pltpu.HBM((2, rows, C), F32))(chip_c, p, r)


_SHARDED = ("even_w_in", "even_w_out", "odd_w_in", "q_b", "kv_b", "odd_w_out", "ffn_w_gate", "ffn_w_up", "ffn_w_down")
_REPLICATED = ("mix_norm", "ffn_norm", "sg_ln_g", "sg_w_s", "sg_b_s", "pool_w", "q_norm", "k_norm")
_SMALL_SHARDED = ("sc_conv_w", "pool_scale", "q_a_norm", "kv_a_norm")
_WEIGHTS = ("mix_norm", "ffn_norm", "even_w_in", "sg_ln_g", "sg_w_s", "sg_b_s", "sc_conv_w", "even_w_out", "odd_w_in", "pool_w",
            "pool_scale", "q_a_norm", "q_b", "kv_a_norm", "kv_b", "q_norm", "k_norm", "odd_w_out", "ffn_w_gate", "ffn_w_up",
            "ffn_w_down")


def _pad_rows(flat, width, align):
    n = flat.shape[0]
    rows = -(-n // (width * align)) * align
    return jnp.pad(flat, (0, rows * width - n)).reshape(rows, width)


_GROUPS = {"even": ("even_w_in", "even_w_out"),
           "ffn0": ("ffn_w_gate0", "ffn_w_up0", "ffn_w_down0"),
           "odd": ("odd_w_in", "q_b", "kv_b", "odd_w_out"),
           "ffn1": ("ffn_w_gate1", "ffn_w_up1", "ffn_w_down1")}


def _place_shards(shards, names, chip, after):
    placed = []
    for n in names:
        weight, layer = (n[:-1], int(n[-1])) if n[-1].isdigit() else (n, 0)
        a = shards[weight]
        placed.append(_cast_place(f"place_{n}", a.reshape(a.shape[0], 2, a.shape[1] // 2, a.shape[2]), layer, chip, after))
    return placed


def _whole_weights(gathered):
    out = {n: a.reshape(N_CHIPS, -1, a.shape[-1]) for n, a in gathered.items()}
    for n in ("q_b", "kv_b"):
        if n in out:
            out[n] = out[n].transpose(1, 0, 2).reshape(out[n].shape[1], -1)
    for n in ("even_w_out", "odd_w_in", "odd_w_out"):
        if n in out:
            out[n] = out[n].reshape(-1, out[n].shape[-1])
    return out


def _forward_backward(x, positions, target, small, fetch, emit, advance):
    batch, seq, _ = x.shape
    T = batch * seq
    tm = _token_tile(seq)
    x0 = x.reshape(T, D_MODEL)

    inv_freq = ROPE_THETA ** (-jnp.arange(0, QK_ROPE, 2, dtype=F32) / QK_ROPE)
    ang = (positions.astype(F32)[..., None] * inv_freq).reshape(T, QK_ROPE // 2)
    cos, sin = jnp.cos(ang), jnp.sin(ang)
    pad = jnp.zeros((T, LANES - QK_ROPE), F32)
    cos_t = jnp.concatenate([cos, cos, pad], axis=1)
    sin_t = jnp.concatenate([-sin, sin, pad], axis=1)

    tril = jnp.tril(jnp.ones((SG_CHUNK, SG_CHUNK), bool))
    w_tril = jnp.where(tril[None], small["sg_w_s"][0], 0.0).astype(BF16)
    b_lanes = jnp.broadcast_to(small["sg_b_s"][0][:, :, None], (SG_HEADS, SG_CHUNK, SG_DIM))
    conv_w = jnp.pad(small["sc_conv_w"][0], ((0, SUBLANES - CONV_TAPS), (0, 0)))
    ln_g = small["sg_ln_g"]
    pool_diag = jnp.zeros((POOL_WIDTH, POOL_WIDTH), F32)
    for g in range(len(POOL_WINDOWS)):
        pool_diag = pool_diag.at[POOL_DIM * g:POOL_DIM * (g + 1), POOL_DIM * g:POOL_DIM * (g + 1)].set(small["pool_w"][0, g])
    pool_diag = pool_diag.astype(BF16)
    pool_scale = small["pool_scale"]
    q_g = jnp.pad(small["q_norm"], ((0, 0), (0, QK_PAD - QK_DIM)))
    k_g = jnp.pad(small["k_norm"], ((0, 0), (0, QK_PAD - QK_DIM)))
    qa_g, kva_g = small["q_a_norm"], small["kv_a_norm"]
    in_shard = EVEN_IN // N_CHIPS

    def ffn_weights(l, w):
        return w[f"ffn_w_gate{l}"], w[f"ffn_w_up{l}"], w[f"ffn_w_down{l}"]

    W = fetch("even", ())
    w_in_even = W["even_w_in"]
    h0 = _rmsnorm_fwd("mix0_norm", x0, small["mix_norm"][0], tm)
    tb = _big_tile(T)
    proj0 = _even_in(h0, w_in_even, _resident_tile(T))
    mix0 = _even_mixer_fwd(proj0, ln_g, w_tril, b_lanes, conv_w, seq, tm)
    w_out_even = W["even_w_out"]
    x1, h1 = _mm("even_out", "nn", mix0, w_out_even, F32, tk=1024, add=x0, fused=_norm_tail(small["ffn_norm"][0], T, tb))
    ffn0 = ffn_weights(0, fetch("ffn0", (x1,)))
    (x2, h2), ffn0_saved = _ffn_fwd(0, x1, h1, *ffn0, lambda tile: _norm_tail(small["mix_norm"][1], T, tile))
    W = fetch("odd", (x2,))
    w_in_odd = jnp.pad(W["odd_w_in"], ((0, 0), (0, ODD_IN_PAD - ODD_IN)))
    q_b = jnp.pad(W["q_b"].reshape(Q_LORA, HEADS, QK_DIM).transpose(1, 0, 2), ((0, 0), (0, 0), (0, QK_PAD - QK_DIM)))
    kv_b = W["kv_b"].reshape(KV_LORA, HEADS, QK_NOPE + V_DIM).transpose(1, 0, 2)
    proj1 = _mm("odd_in", "nn", h2, w_in_odd, F32, tk=1024)
    mix1 = _pool_fwd(proj1, pool_diag, pool_scale, seq, tm)
    q, k, v = _mla_qkv_fwd(proj1, cos_t, sin_t, qa_g, kva_g, q_b, kv_b, q_g, k_g, tm)
    mix1, lse = _flash_fwd(q, k, v, mix1, batch, seq)
    x3, h3 = _mm("odd_out", "nn", mix1, W["odd_w_out"], F32, tk=1024, add=x2, fused=_norm_tail(small["ffn_norm"][1], T, tb))
    ffn1 = ffn_weights(1, fetch("ffn1", (x3,)))
    (dy, sq), ffn1_saved = _ffn_fwd(1, x3, h3, *ffn1, lambda tile: _loss_tail(target.reshape(T, D_MODEL), tile))

    G = {}
    dx3, dffn_g1 = _ffn_bwd(1, x3, small["ffn_norm"][1], *ffn1, ffn1_saved, dy, emit)
    dmix1 = _mm("odd_out_dx", "nt", dx3, W["odd_w_out"], BF16, tk=1024, after=advance(dx3))
    dw_out_odd = _mm("odd_out_dw", "tn", mix1, dx3, BF16, hbm_out=True)
    dq, dk, dv = _flash_bwd(q, k, v, dmix1, mix1, lse, batch, seq)
    dz_pool, dpool_diag, G["pool_scale"] = _pool_bwd(proj1, dmix1, pool_diag, pool_scale, seq, tm)
    dproj1, dq_b, dkv_b, dq_g, dk_g, G["q_a_norm"], G["kv_a_norm"] = _mla_qkv_bwd(
        proj1, cos_t, sin_t, qa_g, kva_g, q_b, kv_b, q_g, k_g, dq, dk, dv, dz_pool, tm)
    G["pool_w"] = jnp.stack([dpool_diag[POOL_DIM * g:POOL_DIM * (g + 1), POOL_DIM * g:POOL_DIM * (g + 1)]
                             for g in range(len(POOL_WINDOWS))])[None]
    G["q_norm"], G["k_norm"] = dq_g[:, :QK_DIM], dk_g[:, :QK_DIM]
    dw_in_odd = _mm("odd_in_dw", "tn", h2, dproj1, BF16, tn=ODD_IN, hbm_out=True)

    def shard_major(g, cols):
        return g.reshape(g.shape[0], N_CHIPS, cols).transpose(1, 0, 2).astype(BF16)

    behind = emit("odd", {"odd_w_in": dw_in_odd.reshape(N_CHIPS, -1, ODD_IN),
                          "q_b": shard_major(dq_b[:, :, :QK_DIM].transpose(1, 0, 2).reshape(Q_LORA, HEADS * QK_DIM), HEADS * QK_DIM // N_CHIPS),
                          "kv_b": shard_major(dkv_b.transpose(1, 0, 2).reshape(KV_LORA, HEADS * (QK_NOPE + V_DIM)),
                                              HEADS * (QK_NOPE + V_DIM) // N_CHIPS),
                          "odd_w_out": dw_out_odd.reshape(N_CHIPS, -1, D_MODEL)})
    dx2, dmix_g1 = _mm("odd_in_dx", "nt", dproj1, W["odd_w_in"], F32, tk=ODD_IN, after=behind,
                       fused=_norm_bwd_tail(x2, small["mix_norm"][1], dx3, tb))
    dx1, dffn_g0 = _ffn_bwd(0, x1, small["ffn_norm"][0], *ffn0, ffn0_saved, dx2, emit, after=advance(dx2))
    dmix0 = _mm("even_out_dx", "nt", dx1, w_out_even, F32, tk=1024, after=advance(dx1))
    dw_out_even = _mm("even_out_dw", "tn", mix0, dx1, BF16, hbm_out=True)
    dproj0, dw_s, db_lanes, G["sg_ln_g"], dconv = _even_mixer_bwd(proj0, dmix0, ln_g, w_tril, b_lanes, conv_w, seq, tm)
    G["sg_w_s"] = dw_s[None]
    G["sg_b_s"] = jnp.sum(db_lanes, axis=-1)[None]
    G["sc_conv_w"] = dconv[None, :CONV_TAPS]
    tr = _resident_tile(T)
    tail, shapes, specs = _norm_bwd_tail(x0, small["mix_norm"][0], dx1, tr)
    dx0, dmix_g0 = _matmul("even_in_dx", "nt", [(dproj0, w_in_even)],
                           [(_row_spec(tr, EVEN_IN), _resident((N_CHIPS, D_MODEL, in_shard)))],
                           (T // tr, 1, 1), shapes, specs, (tr, D_MODEL), tail=tail)
    tk = min(512, T)
    dw_in_even = _grad_shards(
        "even_in_dw", h0, dproj0, BS((tk, D_MODEL), lambda k: (k, 0)), BS((tk, EVEN_IN), lambda k: (k, 0)),
        lambda a_ref, b_ref, j: (a_ref[...], b_ref[:, in_shard * j:in_shard * (j + 1)]), (N_CHIPS, D_MODEL, in_shard), T // tk)
    emit("even", {"even_w_in": dw_in_even, "even_w_out": dw_out_even.reshape(N_CHIPS, -1, D_MODEL)})
    G["mix_norm"] = jnp.concatenate([dmix_g0, dmix_g1], axis=0)
    G["ffn_norm"] = jnp.concatenate([dffn_g0, dffn_g1], axis=0)
    return sq[0, 0], dx0.reshape(batch, seq, D_MODEL), G


def _small_vector(parts, names):
    flat = jnp.concatenate([parts[n].astype(F32).reshape(-1) for n in names])
    return _pad_rows(flat, LANES, 2 * SUBLANES)


def _split_small(vec, like, names):
    out, off, flat = {}, 0, vec.reshape(-1)
    for n in names:
        size = math.prod(like[n].shape)
        out[n] = flat[off:off + size].reshape(like[n].shape)
        off += size
    return out


def _whole_shape(a):
    return a.shape[:-1] + (a.shape[-1] * N_CHIPS,)


def kernel(x, positions, mix_norm, ffn_norm, even_w_in, sg_ln_g, sg_w_s, sg_b_s, sc_conv_w, even_w_out, odd_w_in, pool_w, pool_scale, q_a_norm, q_b, kv_a_norm, kv_b, q_norm, k_norm, odd_w_out, ffn_w_gate, ffn_w_up, ffn_w_down, loss_target, m_mix_norm, m_ffn_norm, m_even_w_in, m_sg_ln_g, m_sg_w_s, m_sg_b_s, m_sc_conv_w, m_even_w_out, m_odd_w_in, m_pool_w, m_pool_scale, m_q_a_norm, m_q_b, m_kv_a_norm, m_kv_b, m_q_norm, m_k_norm, m_odd_w_out, m_ffn_w_gate, m_ffn_w_up, m_ffn_w_down, v_mix_norm, v_ffn_norm, v_even_w_in, v_sg_ln_g, v_sg_w_s, v_sg_b_s, v_sc_conv_w, v_even_w_out, v_odd_w_in, v_pool_w, v_pool_scale, v_q_a_norm, v_q_b, v_kv_a_norm, v_kv_b, v_q_norm, v_k_norm, v_odd_w_out, v_ffn_w_gate, v_ffn_w_up, v_ffn_w_down):
    args = dict(locals())
    w = {n: args[n] for n in _WEIGHTS}
    m = {n: args["m_" + n] for n in _WEIGHTS}
    v = {n: args["v_" + n] for n in _WEIGHTS}
    cx, cy, cc = lax.axis_index("x"), lax.axis_index("y"), lax.axis_index("c")
    chip = 2 * cx + cy
    transposed = ("ffn_w_gate", "ffn_w_up")
    for n in transposed:
        w[n], m[n], v[n] = (jnp.swapaxes(t[n], 1, 2) for t in (w, m, v))

    chip_arr = chip.astype(jnp.int32).reshape(1)
    c_arr = cc.astype(jnp.int32).reshape(1)
    group_names = list(_GROUPS)
    placed = {}
    for n in _SMALL_SHARDED:
        a = w[n]
        whole = jnp.zeros(a.shape[:-1] + (N_CHIPS, a.shape[-1]), F32)
        whole = lax.dynamic_update_slice_in_dim(whole, a[..., None, :], chip, axis=a.ndim - 1)
        placed[n] = jnp.where(cc == 0, whole, 0.0).reshape(_whole_shape(a))
    small_whole = _all_reduce_small("gather_small_weights", _small_vector(placed, _SMALL_SHARDED))
    small = dict({n: w[n] for n in _REPLICATED}, **_split_small(small_whole, placed, _SMALL_SHARDED))

    first, rest = list(_GROUPS[group_names[0]]), [n for g in group_names[1:] for n in _GROUPS[g]]
    sems_first, flight_first, token = _gather_send("gather_send_first", _place_shards(w, first, chip_arr, (small_whole,)),
                                                   [list(range(len(first)))], (small_whole,))
    sems_rest, flight_rest, all_sent = _gather_send("gather_send_rest", _place_shards(w, rest, chip_arr, (token,)),
                                                    [[rest.index(n) for n in _GROUPS[g]] for g in group_names[1:]], ())
    sems = list(sems_first) + list(sems_rest)
    in_flight = dict(zip(first + rest, list(flight_first) + list(flight_rest)))

    def fetch(group, after):
        gi, members = group_names.index(group), _GROUPS[group]
        after = after if gi else (all_sent,)
        landed = _gather_wait(f"gather_wait_{group}", [in_flight[n] for n in members], sems[2 * gi], sems[2 * gi + 1], after)
        return _whole_weights(dict(zip(members, _gather_pass(f"gather_pass_{group}", landed))))

    swapping, pending, arrived, sent = [], [], {}, []

    def settle(after):
        names, ps, lands, send_sems, recv_sems = pending.pop()
        ps, lands = _scatter_wait(f"scatter_wait_{names[0]}", ps, lands, send_sems, recv_sems, after)
        arrived.update({n: (p, r) for n, p, r in zip(names, ps, lands)})

    def emit(group, grads):
        names = _GROUPS[group]
        halves = [grads[n].reshape(N_CHIPS, 2, grads[n].shape[1] // 2, grads[n].shape[2]) for n in names]
        send_sems, recv_sems, halves, lands, token = _exchange_send(f"exchange_send_{group}", halves)
        swapping.append((group, halves, lands, send_sems, recv_sems))
        sent.append(token)
        return (token,)

    def advance(done):
        done = done if isinstance(done, tuple) else (done,)
        group, halves, lands, send_sems, recv_sems = swapping.pop()
        names = _GROUPS[group]
        halves, lands = _exchange_wait(f"exchange_wait_{group}", halves, lands, send_sems, recv_sems, done)
        partial = [_add_halves(f"add_{n}", g, r, c_arr) for n, g, r in zip(names, halves, lands)]
        if pending:
            settle(done)
        send_sems, recv_sems, ps, lands, token = _scatter_send(f"scatter_send_{group}", partial)
        pending.append((names, ps, lands, send_sems, recv_sems))
        return (token,)

    sq, grad_x, G = _forward_backward(x, positions, loss_target, small, fetch, emit, advance)
    loss = lax.psum(0.5 * sq / D_MODEL, ("x", "y", "c"))

    small_names = _REPLICATED + _SMALL_SHARDED
    summed = _split_small(_all_reduce_small("reduce_small_grads", _small_vector(G, small_names)), G, small_names)
    grads = {n: summed[n] for n in _REPLICATED}
    for n in _SMALL_SHARDED:
        a = w[n]
        grads[n] = lax.dynamic_slice_in_dim(summed[n].reshape(a.shape[:-1] + (N_CHIPS, a.shape[-1])), chip, 1,
                                            axis=a.ndim - 1).reshape(a.shape)

    chip_c = jnp.stack([chip, cc]).astype(jnp.int32)
    out = {}

    def finish(group, after):
        names, tokens = _GROUPS[group], []
        sums = [_sum_partials(f"sum_{n}", *arrived[n], chip_c) for n in names]
        for n, f in zip(names, _sibling_share(f"grad_share_{group}", sums, after)):
            weight, layer = (n[:-1], int(n[-1])) if n[-1].isdigit() else (n, 0)
            *out[weight], token = _adamw(f"adamw_{weight}", w[weight], f.reshape(-1, f.shape[-1]), m[weight], v[weight], layer,
                                         out.get(weight, ()))
            tokens.append(token)
        return tuple(tokens)

    last_exchange = tuple(sent[-1:])
    last_scatter = advance(finish(group_names[3], last_exchange) + finish(group_names[2], last_exchange))
    settle(finish(group_names[1], last_scatter))
    finish(group_names[0], ())
    packed = [_small_vector(d, small_names) for d in (w, grads, m, v)]
    res = _adamw("adamw_small", packed[0][None], packed[1], packed[2][None], packed[3][None])
    delta_s, m_s, v_s = (_split_small(r, w, small_names) for r in res[1:4])
    for n in small_names:
        out[n] = (grads[n], delta_s[n], m_s[n], v_s[n])
    for n in transposed:
        out[n] = tuple(jnp.swapaxes(t, 1, 2) for t in out[n])

    return (loss, grad_x, *[out[n][0] for n in _WEIGHTS], *[out[n][1] for n in _WEIGHTS],
            *[out[n][2] for n in _WEIGHTS], *[out[n][3] for n in _WEIGHTS])
```

```python
import functools
import math

import jax
import jax.numpy as jnp
from jax import lax
from jax.experimental import pallas as pl
from jax.experimental.pallas import tpu as pltpu

F32, BF16 = jnp.float32, jnp.bfloat16
BS = pl.BlockSpec

D_MODEL = 1024
EPS = 1e-6
NEG_INF = -1e30
SG_HEADS, SG_DIM, SG_WIDTH, SG_CHUNK = 4, 128, 512, 128
SC_WIDTH, CONV_TAPS = 512, 3
EVEN_IN = 2 * SG_WIDTH + 3 * SC_WIDTH
POOL_WINDOWS = (2, 4, 8, 16)
POOL_DIM, POOL_WIDTH = 64, 256
POOL_HALO = 16
HEADS, Q_LORA, KV_LORA, QK_NOPE, QK_ROPE, V_DIM = 6, 384, 256, 128, 64, 128
QK_DIM = QK_NOPE + QK_ROPE
QK_PAD = 256
ODD_IN = POOL_WIDTH + Q_LORA + KV_LORA + QK_ROPE
ODD_IN_PAD = 1024
ROPE_THETA = 10000.0
ATTN_SCALE = QK_DIM ** -0.5
D_FF, N_CHIPS = 2816, 4
FF_SHARD = D_FF // N_CHIPS
ADAM_LR, ADAM_B1, ADAM_B2, ADAM_EPS, ADAM_WD, ADAM_STEP = 0.001, 0.9, 0.999, 1e-08, 0.01, 10
VMEM_LIMIT_V7X = 48 * 2**20
LANES, SUBLANES = 128, 8
MESH = pl.DeviceIdType.MESH
HBM = pl.BlockSpec(memory_space=pltpu.HBM)
VMEM = pl.BlockSpec(memory_space=pltpu.VMEM)

_DIMS = {"nn": (((1,), (0,)), ((), ())), "nt": (((1,), (1,)), ((), ())), "tn": (((0,), (0,)), ((), ()))}


def _dot(a, b, mode="nn"):
    return lax.dot_general(a.astype(BF16), b.astype(BF16), _DIMS[mode], preferred_element_type=F32)


def _call(body, *, name, out_shape, in_specs, out_specs, grid=(), scratch=(), aliases=None, after=()):
    params = pltpu.CompilerParams(vmem_limit_bytes=VMEM_LIMIT_V7X,
                                  **({"dimension_semantics": ("arbitrary",) * len(grid)} if grid else {}))
    n_in, n_after = len(in_specs), len(after)
    kernel_body = body if not after else (lambda *refs: body(*refs[:n_in], *refs[n_in + n_after:]))
    call = pl.pallas_call(kernel_body, name=name, grid=grid, in_specs=list(in_specs) + [pl.BlockSpec(memory_space=pl.ANY)] * n_after,
                          out_specs=out_specs, out_shape=out_shape, scratch_shapes=list(scratch),
                          input_output_aliases=aliases or {}, compiler_params=params)
    return (lambda *ops: call(*ops, *after)) if after else call


def _sds(shape, dtype):
    return jax.ShapeDtypeStruct(tuple(shape), dtype)


def _token_tile(seq):
    return 512 if seq % 512 == 0 else seq


_TAIL_ROWS = 256


def _matmul(name, mode, pairs, pair_specs, grid, out_shape, out_spec, acc_shape, add=None, add_spec=None, after=(), tail=None):
    n, nk = len(pairs), grid[-1]
    n_add = int(add is not None)
    n_tail = len(tail[0]) if tail else 0
    n_in = 2 * n + n_add + n_tail
    n_out = len(out_shape) if tail else 1

    def body(*refs):
        ab = refs[:2 * n]
        add_ref = refs[2 * n] if n_add else None
        tail_refs, outs = refs[2 * n + n_add:n_in], refs[n_in:n_in + n_out]
        first = pl.program_id(0) == 0

        def finish(result):
            if tail is None:
                r = result(slice(None))
                outs[0][...] = (r if add_ref is None else r + add_ref[...]).astype(outs[0].dtype)
                return
            for lo in range(0, acc_shape[0], _TAIL_ROWS):
                rows = slice(lo, min(lo + _TAIL_ROWS, acc_shape[0]))
                r = result(rows)
                tail[2](rows, r if add_ref is None else r + add_ref[rows, :], first, tail_refs, outs)

        def terms(a_ref, b_ref):
            if len(a_ref.shape) == 2 and len(b_ref.shape) == 2:
                return [(a_ref[...], b_ref[...])]
            cols = a_ref.shape[-1] // N_CHIPS
            return [(a_ref[j] if len(a_ref.shape) == 3 else a_ref[:, cols * j:cols * (j + 1)], b_ref[j]) for j in range(N_CHIPS)]

        if nk == 1:
            r = None
            for p in range(n):
                for a_blk, b_blk in terms(ab[2 * p], ab[2 * p + 1]):
                    d = _dot(a_blk, b_blk, mode)
                    r = d if r is None else r + d
            finish(lambda rows: r[rows])
            return
        acc = refs[-1]
        k = pl.program_id(len(grid) - 1)

        @pl.when(k == 0)
        def _():
            acc[...] = jnp.zeros_like(acc)

        for p in range(n):
            acc[...] += _dot(ab[2 * p][...], ab[2 * p + 1][...], mode)

        @pl.when(k == nk - 1)
        def _():
            finish(lambda rows: acc[rows, :])

    ops = [t for pr in pairs for t in pr] + ([add] if n_add else []) + (list(tail[0]) if tail else [])
    specs = [s for pr in pair_specs for s in pr] + ([add_spec] if n_add else []) + (list(tail[1]) if tail else [])
    return _call(body, name=name, grid=grid, in_specs=specs, out_specs=out_spec, out_shape=out_shape,
                 scratch=[pltpu.VMEM(acc_shape, F32)] if nk > 1 else [], after=after)(*ops)


def _row_spec(tm, d):
    return BS((tm, d), lambda i, j, k: (i, 0))


def _vec_spec(d):
    return BS((1, d), lambda i, j, k: (0, 0))


def _norm_tail(gain, T, tm):
    d = gain.shape[-1]

    def fn(rows, r, first, tail_refs, outs):
        outs[0][rows, :] = r
        outs[1][rows, :] = (r * lax.rsqrt(jnp.mean(r * r, axis=-1, keepdims=True) + EPS) * tail_refs[0][...]).astype(BF16)

    return ([gain.reshape(1, d)], [_vec_spec(d)], fn), [_sds((T, d), F32), _sds((T, d), BF16)], [_row_spec(tm, d), _row_spec(tm, d)]


def _norm_bwd_tail(x, gain, dres, tm):
    T, d = x.shape

    def fn(rows, r, first, tail_refs, outs):
        x_ref, g_ref, dres_ref = tail_refs
        xv = x_ref[rows, :]
        rstd = lax.rsqrt(jnp.mean(xv * xv, axis=-1, keepdims=True) + EPS)
        xhat = xv * rstd
        if rows.start == 0:
            @pl.when(first)
            def _():
                outs[1][...] = jnp.zeros_like(outs[1])

        outs[1][...] += jnp.sum(r * xhat, axis=0, keepdims=True)
        dxhat = r * g_ref[...]
        outs[0][rows, :] = dres_ref[rows, :] + rstd * (dxhat - xhat * jnp.mean(dxhat * xhat, axis=-1, keepdims=True))

    return (([x, gain.reshape(1, d), dres], [_row_spec(tm, d), _vec_spec(d), _row_spec(tm, d)], fn),
            [_sds((T, d), F32), _sds((1, d), F32)], [_row_spec(tm, d), _vec_spec(d)])


def _loss_tail(target, tm):
    T, d = target.shape

    def fn(rows, r, first, tail_refs, outs):
        e = r - tail_refs[0][rows, :]
        if rows.start == 0:
            @pl.when(first)
            def _():
                outs[1][...] = jnp.zeros_like(outs[1])

        outs[1][...] += jnp.sum(e * e)
        outs[0][rows, :] = e * (1.0 / d)

    return (([target], [_row_spec(tm, d)], fn), [_sds((T, d), F32), _sds((SUBLANES, LANES), F32)],
            [_row_spec(tm, d), BS((SUBLANES, LANES), lambda i, j, k: (0, 0))])


def _grad_shards(name, a, b, a_spec, b_spec, pick, out_shape, n_steps):
    def body(a_ref, b_ref, o_ref, acc):
        k = pl.program_id(0)

        @pl.when(k == 0)
        def _():
            acc[...] = jnp.zeros_like(acc)

        for j in range(N_CHIPS):
            aj, bj = pick(a_ref, b_ref, j)
            acc[j] += _dot(aj, bj, "tn")

        @pl.when(k == n_steps - 1)
        def _():
            o_ref[...] = acc[...].astype(BF16)

    return _call(body, name=name, grid=(n_steps,), in_specs=[a_spec, b_spec], scratch=[pltpu.VMEM(tuple(out_shape), F32)],
                 out_specs=BS(out_shape, lambda k: (0, 0, 0)), out_shape=pltpu.HBM(tuple(out_shape), BF16))(a, b)


def _mm(name, mode, a, b, out_dtype, tm=1024, tn=1024, tk=512, add=None, after=(), fused=None, hbm_out=False):
    if mode == "tn":
        (K, M), N = a.shape, b.shape[1]
    else:
        (M, K), N = a.shape, (b.shape[1] if mode == "nn" else b.shape[0])
    tm, tn, tk = min(tm, M), min(tn, N), min(tk, K)
    a_spec = BS((tk, tm), lambda i, j, k: (k, i)) if mode == "tn" else BS((tm, tk), lambda i, j, k: (i, k))
    b_spec = BS((tn, tk), lambda i, j, k: (j, k)) if mode == "nt" else BS((tk, tn), lambda i, j, k: (k, j))
    o_spec = BS((tm, tn), lambda i, j, k: (i, j))
    tail, shapes, specs = fused if fused else (None, pltpu.HBM((M, N), out_dtype) if hbm_out else _sds((M, N), out_dtype), o_spec)
    return _matmul(name, mode, [(a, b)], [(a_spec, b_spec)], (M // tm, N // tn, K // tk), shapes, specs, (tm, tn),
                   add=add, add_spec=o_spec if add is not None else None, after=after, tail=tail)


def _rmsnorm_fwd(name, x, g, tm):
    T, d = x.shape

    def body(x_ref, g_ref, o_ref):
        xv = x_ref[...]
        y = xv * lax.rsqrt(jnp.mean(xv * xv, axis=-1, keepdims=True) + EPS)
        o_ref[...] = (y * g_ref[...]).astype(o_ref.dtype)

    return _call(body, name=name, grid=(T // tm,), in_specs=[BS((tm, d), lambda i: (i, 0)), BS((1, d), lambda i: (0, 0))],
                 out_specs=BS((tm, d), lambda i: (i, 0)), out_shape=_sds((T, d), BF16))(x, g.reshape(1, d))


_PASS_ROWS = 256


def _ffn_up(name, h, wg, wu, tm):
    T = h.shape[0]

    def body(h_ref, wg_ref, wu_ref, g_ref, u_ref, a_ref):
        hv = h_ref[...]
        g = _dot(hv, wg_ref[...], "nt")
        u = _dot(hv, wu_ref[...], "nt")
        g_ref[...] = g.astype(BF16)
        u_ref[...] = u.astype(BF16)
        a_ref[...] = (g * (1.0 / (1.0 + jnp.exp(-g))) * u).astype(BF16)

    w_spec = BS((None, FF_SHARD, D_MODEL), lambda j, i: (j, 0, 0))
    o_spec = BS((None, tm, FF_SHARD), lambda j, i: (j, i, 0))
    sh = _sds((N_CHIPS, T, FF_SHARD), BF16)
    return _call(body, name=name, grid=(N_CHIPS, T // tm), in_specs=[BS((tm, D_MODEL), lambda j, i: (i, 0)), w_spec, w_spec],
                 out_specs=[o_spec, o_spec, o_spec], out_shape=[sh, sh, sh])(h, wg, wu)


def _ffn_act_bwd(name, dxo, wd, g, u, tm, after=()):
    T = dxo.shape[0]

    def body(dx_ref, wd_ref, g_ref, u_ref, dg_ref, du_ref):
        da = _dot(dx_ref[...], wd_ref[...], "nt")
        g = g_ref[...].astype(F32)
        sig = 1.0 / (1.0 + jnp.exp(-g))
        dg_ref[...] = (da * u_ref[...].astype(F32) * (sig * (1.0 + g * (1.0 - sig)))).astype(BF16)
        du_ref[...] = (da * (g * sig)).astype(BF16)

    t_spec = BS((None, tm, FF_SHARD), lambda i, j: (j, i, 0))
    sh = _sds((N_CHIPS, T, FF_SHARD), BF16)
    return _call(body, name=name, grid=(T // tm, N_CHIPS),
                 in_specs=[BS((tm, D_MODEL), lambda i, j: (i, 0)), BS((None, FF_SHARD, D_MODEL), lambda i, j: (j, 0, 0)), t_spec, t_spec],
                 out_specs=[t_spec, t_spec], out_shape=[sh, sh], after=after)(dxo, wd, g, u)


def _big_tile(n):
    return min(1024, n)


def _resident_tile(n):
    return min(512, n)


def _resident(shape):
    return BS(shape, lambda i, j, k: (0,) * len(shape), pipeline_mode=pl.Buffered(1))


def _ffn_fwd(l, x, h, wg, wu, wd, fused):
    T = x.shape[0]
    g, u, a = _ffn_up(f"ffn{l}_up", h, wg, wu, _big_tile(T))
    tm = _resident_tile(T)
    tail, shapes, specs = fused(tm)
    outs = _matmul(f"ffn{l}_down", "nn", [(a, wd)],
                   [(BS((N_CHIPS, tm, FF_SHARD), lambda i, j, k: (0, i, 0)), _resident((N_CHIPS, FF_SHARD, D_MODEL)))],
                   (T // tm, 1, 1), shapes, specs, (tm, D_MODEL), add=x, add_spec=_row_spec(tm, D_MODEL), tail=tail)
    return outs, (h, g, u, a)


def _ffn_bwd(l, x, gain, wg, wu, wd, saved, dxo, emit, after=()):
    h, g, u, a = saved
    T = x.shape[0]
    tm = _big_tile(T)
    dg, du = _ffn_act_bwd(f"ffn{l}_act_bwd", dxo, wd, g, u, tm, after=after)
    tk = _big_tile(T)
    shards_spec = BS((N_CHIPS, tk, FF_SHARD), lambda k: (0, k, 0))
    rows_spec = BS((tk, D_MODEL), lambda k: (k, 0))

    def dw(nm, act, rows):
        return _grad_shards(nm, act, rows, shards_spec, rows_spec, lambda a_ref, b_ref, j: (a_ref[j], b_ref[...]),
                            (N_CHIPS, FF_SHARD, D_MODEL), T // tk)

    behind = emit(f"ffn{l}", {f"ffn_w_gate{l}": dw(f"ffn{l}_dwg", dg, h), f"ffn_w_up{l}": dw(f"ffn{l}_dwu", du, h),
                              f"ffn_w_down{l}": dw(f"ffn{l}_dwd", a, dxo)})
    tm = _resident_tile(T)
    act_spec = BS((N_CHIPS, tm, FF_SHARD), lambda i, j, k: (0, i, 0))
    w_spec = _resident((N_CHIPS, FF_SHARD, D_MODEL))
    tail, shapes, specs = _norm_bwd_tail(x, gain, dxo, tm)
    return _matmul(f"ffn{l}_dh", "nn", [(dg, wg), (du, wu)], [(act_spec, w_spec), (act_spec, w_spec)],
                   (T // tm, 1, 1), shapes, specs, (tm, D_MODEL), after=behind, tail=tail)


_INV_SQRT2 = 1.0 / math.sqrt(2.0)
_INV_SQRT_2PI = 1.0 / math.sqrt(2.0 * math.pi)


def _gelu(x):
    return 0.5 * x * (1.0 + lax.erf(x * _INV_SQRT2))


def _gelu_and_grad(x):
    cdf = 0.5 * (1.0 + lax.erf(x * _INV_SQRT2))
    return x * cdf, cdf + x * jnp.exp(-0.5 * x * x) * _INV_SQRT_2PI


def _shift_down(x, k):
    return pltpu.roll(x, k, 0)


def _shift_up(x, k):
    return pltpu.roll(x, x.shape[0] - k, 0)


def _layer_norm_head(xh):
    xc = xh - jnp.mean(xh, axis=-1, keepdims=True)
    rstd = lax.rsqrt(jnp.mean(xc * xc, axis=-1, keepdims=True) + EPS)
    return xc * rstd, rstd


def _even_in(h, w, tm):
    T = h.shape[0]
    shard = w.shape[-1]

    def body(h_ref, w_ref, o_ref):
        hv = h_ref[...]
        for j in range(N_CHIPS):
            o_ref[:, shard * j:shard * (j + 1)] = _dot(hv, w_ref[j])

    return _call(body, name="even_in", grid=(T // tm,),
                 in_specs=[BS((tm, D_MODEL), lambda i: (i, 0)), BS(w.shape, lambda i: (0, 0, 0), pipeline_mode=pl.Buffered(1))],
                 out_specs=BS((tm, N_CHIPS * shard), lambda i: (i, 0)), out_shape=_sds((T, N_CHIPS * shard), F32))(h, w)


def _even_halo_specs(tm, n_tiles, col_blocks, after):
    rows = tm // SUBLANES
    last = n_tiles * rows - 1
    if after:
        return [BS((SUBLANES, 512), functools.partial(lambda cb, i: (jnp.minimum((i + 1) * rows, last), cb), cb)) for cb in col_blocks]
    return [BS((SUBLANES, 512), functools.partial(lambda cb, i: (jnp.maximum(i * rows - 1, 0), cb), cb)) for cb in col_blocks]


def _even_mixer_fwd(proj, ln_g, w_tril, b_lanes, conv_w, seq, tm):
    T = proj.shape[0]
    tiles_per_seq = seq // tm

    def body(p_ref, hc_ref, hh_ref, lng_ref, w_ref, bb_ref, cw_ref, o_ref):
        first = pl.program_id(0) % tiles_per_seq == 0
        for h in range(SG_HEADS):
            cols = slice(SG_DIM * h, SG_DIM * (h + 1))
            vhat, _ = _layer_norm_head(_gelu(p_ref[:, SG_WIDTH + SG_DIM * h:SG_WIDTH + SG_DIM * (h + 1)]))
            vln = (vhat * lng_ref[:, cols]).astype(BF16)
            for k in range(tm // SG_CHUNK):
                rows = slice(SG_CHUNK * k, SG_CHUNK * (k + 1))
                mixed = _dot(w_ref[h], vln[rows]) + bb_ref[h]
                o_ref[rows, cols] = (_gelu(p_ref[rows, cols]) * mixed).astype(BF16)
        z = p_ref[:, 1536:2048] * p_ref[:, 2048:2560]
        zz = jnp.concatenate([jnp.where(first, 0.0, hc_ref[...] * hh_ref[...]), z], axis=0)
        y = cw_ref[0:1, :] * _shift_down(zz, 2)[SUBLANES:] + cw_ref[1:2, :] * _shift_down(zz, 1)[SUBLANES:] + cw_ref[2:3, :] * z
        o_ref[:, SG_WIDTH:] = (p_ref[:, 1024:1536] * y).astype(BF16)

    full = lambda shape: BS(shape, lambda i: (0,) * len(shape))
    return _call(body, name="even_mixer_fwd", grid=(T // tm,),
                 in_specs=[BS((tm, EVEN_IN), lambda i: (i, 0))] + _even_halo_specs(tm, T // tm, (3, 4), after=False)
                 + [full((1, SG_WIDTH)), full((SG_HEADS, SG_CHUNK, SG_CHUNK)), full((SG_HEADS, SG_CHUNK, SG_DIM)), full((SUBLANES, SC_WIDTH))],
                 out_specs=BS((tm, D_MODEL), lambda i: (i, 0)), out_shape=_sds((T, D_MODEL), BF16))(
        proj, proj, proj, ln_g, w_tril, b_lanes, conv_w)


def _even_mixer_bwd(proj, dmix, ln_g, w_tril, b_lanes, conv_w, seq, tm):
    T = proj.shape[0]
    n_tiles, tiles_per_seq = T // tm, seq // tm

    def body(p_ref, dm_ref, hc_ref, hh_ref, nd_ref, nb_ref, lng_ref, w_ref, bb_ref, cw_ref,
             dp_ref, dw_ref, db_ref, dlng_ref, dcw_ref):
        i = pl.program_id(0)
        first = i % tiles_per_seq == 0
        last = i % tiles_per_seq == tiles_per_seq - 1

        @pl.when(i == 0)
        def _():
            dw_ref[...] = jnp.zeros_like(dw_ref)
            db_ref[...] = jnp.zeros_like(db_ref)
            dlng_ref[...] = jnp.zeros_like(dlng_ref)
            dcw_ref[...] = jnp.zeros_like(dcw_ref)

        for h in range(SG_HEADS):
            cols = slice(SG_DIM * h, SG_DIM * (h + 1))
            vcols = slice(SG_WIDTH + SG_DIM * h, SG_WIDTH + SG_DIM * (h + 1))
            lng = lng_ref[:, cols]
            for k in range(tm // SG_CHUNK):
                rows = slice(SG_CHUNK * k, SG_CHUNK * (k + 1))
                gelu_v, dgelu_v = _gelu_and_grad(p_ref[rows, vcols])
                vhat, rstd = _layer_norm_head(gelu_v)
                vln = (vhat * lng).astype(BF16)
                mixed = _dot(w_ref[h], vln) + bb_ref[h]
                gelu_u, dgelu_u = _gelu_and_grad(p_ref[rows, cols])
                da = dm_ref[rows, cols]
                dp_ref[rows, cols] = (da * mixed * dgelu_u).astype(BF16)
                dmixed = da * gelu_u
                db_ref[h] += dmixed
                dw_ref[h] += _dot(dmixed, vln, "nt")
                dvln = _dot(w_ref[h], dmixed, "tn")
                dlng_ref[:, cols] += jnp.sum(dvln * vhat, axis=0, keepdims=True)
                dvhat = dvln * lng
                dgv = rstd * (dvhat - jnp.mean(dvhat, axis=-1, keepdims=True)
                              - vhat * jnp.mean(dvhat * vhat, axis=-1, keepdims=True))
                dp_ref[rows, vcols] = (dgv * dgelu_v).astype(BF16)

        b = p_ref[:, 1024:1536]
        c = p_ref[:, 1536:2048]
        hv = p_ref[:, 2048:2560]
        z = c * hv
        zz = jnp.concatenate([jnp.where(first, 0.0, hc_ref[...] * hh_ref[...]), z], axis=0)
        z1 = _shift_down(zz, 1)[SUBLANES:]
        z2 = _shift_down(zz, 2)[SUBLANES:]
        w0, w1, w2 = cw_ref[0:1, :], cw_ref[1:2, :], cw_ref[2:3, :]
        dbo = dm_ref[:, SG_WIDTH:]
        dy = dbo * b
        dd = jnp.concatenate([dy, jnp.where(last, 0.0, nd_ref[...] * nb_ref[...])], axis=0)
        dz = w2 * dy + w1 * _shift_up(dd, 1)[:tm] + w0 * _shift_up(dd, 2)[:tm]
        dp_ref[:, 1024:1536] = (dbo * (w0 * z2 + w1 * z1 + w2 * z)).astype(BF16)
        dp_ref[:, 1536:2048] = (dz * hv).astype(BF16)
        dp_ref[:, 2048:2560] = (dz * c).astype(BF16)
        dcw_ref[0:1, :] += jnp.sum(dy * z2, axis=0, keepdims=True)
        dcw_ref[1:2, :] += jnp.sum(dy * z1, axis=0, keepdims=True)
        dcw_ref[2:3, :] += jnp.sum(dy * z, axis=0, keepdims=True)

        @pl.when(i == n_tiles - 1)
        def _():
            t_idx = lax.broadcasted_iota(jnp.int32, (SG_CHUNK, SG_CHUNK), 0)
            s_idx = lax.broadcasted_iota(jnp.int32, (SG_CHUNK, SG_CHUNK), 1)
            for h in range(SG_HEADS):
                dw_ref[h] = jnp.where(t_idx >= s_idx, dw_ref[h], 0.0)

    full = lambda shape: BS(shape, lambda i: (0,) * len(shape))
    sq = (SG_HEADS, SG_CHUNK, SG_CHUNK)
    return _call(body, name="even_mixer_bwd", grid=(n_tiles,),
                 in_specs=[BS((tm, EVEN_IN), lambda i: (i, 0)), BS((tm, D_MODEL), lambda i: (i, 0))]
                 + _even_halo_specs(tm, n_tiles, (3, 4), after=False)
                 + _even_halo_specs(tm, n_tiles, (1,), after=True) + _even_halo_specs(tm, n_tiles, (2,), after=True)
                 + [full((1, SG_WIDTH)), full(sq), full(sq), full((SUBLANES, SC_WIDTH))],
                 out_specs=[BS((tm, EVEN_IN), lambda i: (i, 0)), full(sq), full(sq), full((1, SG_WIDTH)), full((SUBLANES, SC_WIDTH))],
                 out_shape=[_sds((T, EVEN_IN), BF16), _sds(sq, F32), _sds(sq, F32), _sds((1, SG_WIDTH), F32), _sds((SUBLANES, SC_WIDTH), F32)])(
        proj, dmix, proj, proj, dmix, proj, ln_g, w_tril, b_lanes, conv_w)


def _pool_select(vals):
    lane = lax.broadcasted_iota(jnp.int32, vals[0].shape, 1)
    out = vals[-1]
    for g in range(len(vals) - 2, -1, -1):
        out = jnp.where(lane < POOL_DIM * (g + 1), vals[g], out)
    return out


def _pool_counts(pos1):
    lane = lax.broadcasted_iota(jnp.int32, (pos1.shape[0], POOL_WIDTH), 1)
    win = _pool_select([jnp.full(lane.shape, float(w), F32) for w in POOL_WINDOWS])
    return jnp.minimum(pos1, win)


def _pool_means(zz, counts):
    s2 = zz + _shift_down(zz, 1)
    s4 = s2 + _shift_down(s2, 2)
    s8 = s4 + _shift_down(s4, 4)
    s16 = s8 + _shift_down(s8, 8)
    return _pool_select([s2, s4, s8, s16])[POOL_HALO:] / counts


def _pool_halo_spec(tm, n_tiles, after):
    rows = tm // POOL_HALO
    if after:
        return BS((POOL_HALO, POOL_WIDTH), lambda i: (jnp.minimum((i + 1) * rows, n_tiles * rows - 1), 0))
    return BS((POOL_HALO, POOL_WIDTH), lambda i: (jnp.maximum(i * rows - 1, 0), 0))


def _pool_fwd(proj, w_diag, scale, seq, tm):
    T = proj.shape[0]
    tiles_per_seq = seq // tm

    def body(z_ref, zh_ref, w_ref, s_ref, o_ref):
        t = pl.program_id(0) % tiles_per_seq
        z = z_ref[...]
        zz = jnp.concatenate([jnp.where(t == 0, 0.0, zh_ref[...]), z], axis=0)
        pos1 = (lax.broadcasted_iota(jnp.int32, (tm, 1), 0) + (t * tm + 1)).astype(F32)
        pooled = _pool_means(zz, _pool_counts(pos1)) - z
        o_ref[...] = (_dot(pooled, w_ref[...]) * s_ref[...]).astype(BF16)

    full = lambda shape: BS(shape, lambda i: (0,) * len(shape))
    return _call(body, name="pool_fwd", grid=(T // tm,),
                 in_specs=[BS((tm, POOL_WIDTH), lambda i: (i, 0)), _pool_halo_spec(tm, T // tm, False),
                           full((POOL_WIDTH, POOL_WIDTH)), full((1, POOL_WIDTH))],
                 out_specs=BS((tm, POOL_WIDTH), lambda i: (i, 0)), out_shape=_sds((T, D_MODEL), BF16))(proj, proj, w_diag, scale)


def _pool_bwd(proj, dmix, w_diag, scale, seq, tm):
    T = proj.shape[0]
    n_tiles, tiles_per_seq = T // tm, seq // tm

    def body(z_ref, zh_ref, do_ref, don_ref, w_ref, s_ref, dz_ref, dw_ref, ds_ref):
        i = pl.program_id(0)
        t = i % tiles_per_seq

        @pl.when(i == 0)
        def _():
            dw_ref[...] = jnp.zeros_like(dw_ref)
            ds_ref[...] = jnp.zeros_like(ds_ref)

        z = z_ref[...]
        zz = jnp.concatenate([jnp.where(t == 0, 0.0, zh_ref[...]), z], axis=0)
        pos1 = (lax.broadcasted_iota(jnp.int32, (tm, 1), 0) + (t * tm + 1)).astype(F32)
        counts = _pool_counts(pos1)
        pooled = _pool_means(zz, counts) - z
        dout = do_ref[...].astype(F32)
        ds_ref[...] += jnp.sum(dout * _dot(pooled, w_ref[...]), axis=0, keepdims=True)
        dlin = dout * s_ref[...]
        dw_ref[...] += _dot(pooled, dlin, "tn")
        dpooled = _dot(dlin, w_ref[...], "nt")
        dpooled_n = _dot(don_ref[...].astype(F32) * s_ref[...], w_ref[...], "nt")
        pos1_n = (lax.broadcasted_iota(jnp.int32, (POOL_HALO, 1), 0) + ((t + 1) * tm + 1)).astype(F32)
        dmean_n = jnp.where(t == tiles_per_seq - 1, 0.0, dpooled_n / _pool_counts(pos1_n))
        dd = jnp.concatenate([dpooled / counts, dmean_n], axis=0)
        r2 = dd + _shift_up(dd, 1)
        r4 = r2 + _shift_up(r2, 2)
        r8 = r4 + _shift_up(r4, 4)
        r16 = r8 + _shift_up(r8, 8)
        dz_ref[...] = (_pool_select([r2, r4, r8, r16])[:tm] - dpooled).astype(BF16)

    full = lambda shape: BS(shape, lambda i: (0,) * len(shape))
    return _call(body, name="pool_bwd", grid=(n_tiles,),
                 in_specs=[BS((tm, POOL_WIDTH), lambda i: (i, 0)), _pool_halo_spec(tm, n_tiles, False),
                           BS((tm, POOL_WIDTH), lambda i: (i, 0)), _pool_halo_spec(tm, n_tiles, True),
                           full((POOL_WIDTH, POOL_WIDTH)), full((1, POOL_WIDTH))],
                 out_specs=[BS((tm, POOL_WIDTH), lambda i: (i, 0)), full((POOL_WIDTH, POOL_WIDTH)), full((1, POOL_WIDTH))],
                 out_shape=[_sds((T, POOL_WIDTH), BF16), _sds((POOL_WIDTH, POOL_WIDTH), F32), _sds((1, POOL_WIDTH), F32)])(
        proj, proj, dmix, dmix, w_diag, scale)


def _rope_partner(r):
    lane = lax.broadcasted_iota(jnp.int32, r.shape, 1)
    return jnp.where(lane < QK_ROPE // 2, pltpu.roll(r, LANES - QK_ROPE // 2, 1), pltpu.roll(r, QK_ROPE // 2, 1))


def _rope(x, cos, sin_signed):
    r = x[:, QK_NOPE:]
    return jnp.concatenate([x[:, :QK_NOPE], r * cos + _rope_partner(r) * sin_signed], axis=1)


def _rope_transposed(dx, cos, sin_signed):
    dr = dx[:, QK_NOPE:]
    return jnp.concatenate([dx[:, :QK_NOPE], dr * cos + _rope_partner(dr * sin_signed)], axis=1)


def _head_norm(x):
    r = lax.rsqrt(jnp.sum(x * x, axis=-1, keepdims=True) * (1.0 / QK_DIM) + EPS)
    return x * r, r


def _head_norm_bwd(dy, xhat, r, gain):
    dxhat = dy * gain
    return r * (dxhat - xhat * (jnp.sum(dxhat * xhat, axis=-1, keepdims=True) * (1.0 / QK_DIM)))


def _latents(p_ref, qag_ref, kvag_ref):
    ql = p_ref[:, POOL_WIDTH:POOL_WIDTH + Q_LORA]
    kvl = p_ref[:, POOL_WIDTH + Q_LORA:POOL_WIDTH + Q_LORA + KV_LORA]
    rq = lax.rsqrt(jnp.mean(ql * ql, axis=-1, keepdims=True) + EPS)
    rkv = lax.rsqrt(jnp.mean(kvl * kvl, axis=-1, keepdims=True) + EPS)
    return ql * rq, rq, kvl * rkv, rkv


def _mla_specs(tm):
    full = lambda shape: BS(shape, lambda i, h: (0,) * len(shape))
    return [BS((tm, ODD_IN_PAD), lambda i, h: (i, 0)), BS((tm, LANES), lambda i, h: (i, 0)), BS((tm, LANES), lambda i, h: (i, 0)),
            full((1, Q_LORA)), full((1, KV_LORA)), BS((None, Q_LORA, QK_PAD), lambda i, h: (h, 0, 0)),
            BS((None, KV_LORA, QK_PAD), lambda i, h: (h, 0, 0)), full((1, QK_PAD)), full((1, QK_PAD))]


def _mla_qkv_fwd(proj, cos, sin_signed, qa_g, kva_g, q_b, kv_b, q_g, k_g, tm):
    T = proj.shape[0]

    def body(p_ref, cos_ref, sin_ref, qag_ref, kvag_ref, qb_ref, kvb_ref, qg_ref, kg_ref, q_ref, k_ref, v_ref, qn_s, kvn_s):
        @pl.when(pl.program_id(1) == 0)
        def _():
            qhat, _, kvhat, _ = _latents(p_ref, qag_ref, kvag_ref)
            qn_s[...] = (qhat * qag_ref[...]).astype(BF16)
            kvn_s[...] = (kvhat * kvag_ref[...]).astype(BF16)

        cos, sin = cos_ref[...], sin_ref[...]
        qhat, _ = _head_norm(_dot(qn_s[...], qb_ref[...]))
        q_ref[...] = _rope(qhat * qg_ref[...], cos, sin).astype(BF16)
        kv = _dot(kvn_s[...], kvb_ref[...])
        khat, _ = _head_norm(jnp.concatenate([kv[:, :QK_NOPE], p_ref[:, ODD_IN_PAD - LANES:]], axis=1))
        k_ref[...] = _rope(khat * kg_ref[...], cos, sin).astype(BF16)
        v_ref[...] = kv[:, QK_NOPE:].astype(BF16)

    qk_spec = BS((None, tm, QK_PAD), lambda i, h: (h, i, 0))
    return _call(body, name="mla_qkv_fwd", grid=(T // tm, HEADS), in_specs=_mla_specs(tm),
                 out_specs=[qk_spec, qk_spec, BS((None, tm, V_DIM), lambda i, h: (h, i, 0))],
                 out_shape=[_sds((HEADS, T, QK_PAD), BF16), _sds((HEADS, T, QK_PAD), BF16), _sds((HEADS, T, V_DIM), BF16)],
                 scratch=[pltpu.VMEM((tm, Q_LORA), BF16), pltpu.VMEM((tm, KV_LORA), BF16)])(
        proj, cos, sin_signed, qa_g, kva_g, q_b, kv_b, q_g, k_g)


def _mla_qkv_bwd(proj, cos, sin_signed, qa_g, kva_g, q_b, kv_b, q_g, k_g, dq, dk, dv, dz_pool, tm):
    T = proj.shape[0]
    n_tiles = T // tm
    chain_rows = min(_PASS_ROWS, tm)

    def body(p_ref, cos_ref, sin_ref, qag_ref, kvag_ref, qb_ref, kvb_ref, qg_ref, kg_ref, dq_ref, dk_ref, dv_ref, dzp_ref,
             dp_ref, dqb_ref, dkvb_ref, dqg_ref, dkg_ref, dqag_ref, dkvag_ref, qn_s, kvn_s, dqn_s, dkvn_s, dkr_s,
             qh_s, kv_s, dqh_s, dkv_s):
        i, h = pl.program_id(0), pl.program_id(1)

        @pl.when((i == 0) & (h == 0))
        def _():
            for ref in (dqb_ref, dkvb_ref, dqg_ref, dkg_ref, dqag_ref, dkvag_ref):
                ref[...] = jnp.zeros_like(ref)

        @pl.when(h == 0)
        def _():
            qhat, _, kvhat, _ = _latents(p_ref, qag_ref, kvag_ref)
            qn_s[...] = (qhat * qag_ref[...]).astype(BF16)
            kvn_s[...] = (kvhat * kvag_ref[...]).astype(BF16)
            dqn_s[...] = jnp.zeros_like(dqn_s)
            dkvn_s[...] = jnp.zeros_like(dkvn_s)
            dkr_s[...] = jnp.zeros_like(dkr_s)

        qh_s[...] = _dot(qn_s[...], qb_ref[...])
        kv_s[...] = _dot(kvn_s[...], kvb_ref[...])
        qg, kg = qg_ref[...], kg_ref[...]

        def chunk(c, gains):
            dqg, dkg = gains
            rows = slice(c * chain_rows, (c + 1) * chain_rows)
            cos, sin = cos_ref[rows, :], sin_ref[rows, :]
            qhat, rq = _head_norm(qh_s[rows, :])
            dqn_head = _rope_transposed(dq_ref[rows, :], cos, sin)
            dqh_s[rows, :] = _head_norm_bwd(dqn_head, qhat, rq, qg).astype(BF16)
            kv = kv_s[rows, :]
            khat, rk = _head_norm(jnp.concatenate([kv[:, :QK_NOPE], p_ref[rows, ODD_IN_PAD - LANES:]], axis=1))
            dkn_head = _rope_transposed(dk_ref[rows, :], cos, sin)
            dkf = _head_norm_bwd(dkn_head, khat, rk, kg)
            dkr_s[rows, :] += dkf[:, QK_NOPE:]
            dkv_s[rows, :] = jnp.concatenate([dkf[:, :QK_NOPE], dv_ref[rows, :]], axis=1).astype(BF16)
            return dqg + dqn_head * qhat, dkg + dkn_head * khat

        dqg = dkg = jnp.zeros((chain_rows, QK_PAD), F32)
        for c in range(tm // chain_rows):
            dqg, dkg = chunk(c, (dqg, dkg))
        dqg_ref[...] += jnp.sum(dqg, axis=0, keepdims=True)
        dkg_ref[...] += jnp.sum(dkg, axis=0, keepdims=True)
        dqb_ref[h] += _dot(qn_s[...], dqh_s[...], "tn")
        dqn_s[...] += _dot(dqh_s[...], qb_ref[...], "nt")
        dkvb_ref[h] += _dot(kvn_s[...], dkv_s[...], "tn")
        dkvn_s[...] += _dot(dkv_s[...], kvb_ref[...], "nt")

        @pl.when(h == HEADS - 1)
        def _():
            qhat_l, rql, kvhat_l, rkvl = _latents(p_ref, qag_ref, kvag_ref)
            dqn, dkvn = dqn_s[...], dkvn_s[...]
            dqag_ref[...] += jnp.sum(dqn * qhat_l, axis=0, keepdims=True)
            dkvag_ref[...] += jnp.sum(dkvn * kvhat_l, axis=0, keepdims=True)
            dqx, dkvx = dqn * qag_ref[...], dkvn * kvag_ref[...]
            dp_ref[:, :POOL_WIDTH] = dzp_ref[...]
            dp_ref[:, POOL_WIDTH:POOL_WIDTH + Q_LORA] = (
                rql * (dqx - qhat_l * jnp.mean(dqx * qhat_l, axis=-1, keepdims=True))).astype(BF16)
            dp_ref[:, POOL_WIDTH + Q_LORA:ODD_IN_PAD - LANES] = (
                rkvl * (dkvx - kvhat_l * jnp.mean(dkvx * kvhat_l, axis=-1, keepdims=True))).astype(BF16)
            dp_ref[:, ODD_IN_PAD - LANES:] = dkr_s[:, :QK_ROPE].astype(BF16)

    full = lambda shape: BS(shape, lambda i, h: (0,) * len(shape))
    qk_spec = BS((None, tm, QK_PAD), lambda i, h: (h, i, 0))
    return _call(body, name="mla_qkv_bwd", grid=(n_tiles, HEADS),
                 in_specs=_mla_specs(tm) + [qk_spec, qk_spec, BS((None, tm, V_DIM), lambda i, h: (h, i, 0)),
                                            BS((tm, POOL_WIDTH), lambda i, h: (i, 0))],
                 out_specs=[BS((tm, ODD_IN), lambda i, h: (i, 0)), full((HEADS, Q_LORA, QK_PAD)), full((HEADS, KV_LORA, QK_PAD)),
                            full((1, QK_PAD)), full((1, QK_PAD)), full((1, Q_LORA)), full((1, KV_LORA))],
                 out_shape=[_sds((T, ODD_IN), BF16),_sds((HEADS, Q_LORA, QK_PAD), F32), _sds((HEADS, KV_LORA, QK_PAD), F32),
                            _sds((1, QK_PAD), F32), _sds((1, QK_PAD), F32), _sds((1, Q_LORA), F32), _sds((1, KV_LORA), F32)],
                 scratch=[pltpu.VMEM((tm, Q_LORA), BF16), pltpu.VMEM((tm, KV_LORA), BF16), pltpu.VMEM((tm, Q_LORA), F32),
                          pltpu.VMEM((tm, KV_LORA), F32), pltpu.VMEM((tm, LANES), F32), pltpu.VMEM((tm, QK_PAD), F32),
                          pltpu.VMEM((tm, QK_PAD), F32), pltpu.VMEM((tm, QK_PAD), BF16), pltpu.VMEM((tm, QK_PAD), BF16)])(
        proj, cos, sin_signed, qa_g, kva_g, q_b, kv_b, q_g, k_g, dq, dk, dv, dz_pool)


_SCALE_LOG2E = ATTN_SCALE * math.log2(math.e)


def _attn_tile(seq):
    return 512 if seq % 512 == 0 else seq


def _causal_mask(s):
    row = lax.broadcasted_iota(jnp.int32, s.shape, 0)
    col = lax.broadcasted_iota(jnp.int32, s.shape, 1)
    return jnp.where(row >= col, s, NEG_INF)


def _tile(i, t):
    return slice(i * t, (i + 1) * t)


def _flash_fwd(q, k, v, mix, batch, seq):
    t = _attn_tile(seq)
    nq = seq // t

    def body(q_ref, k_ref, v_ref, _, o_ref, lse_ref):
        for qi in range(nq):
            rows, before = _tile(qi, t), slice(0, qi * t)
            qv = q_ref[rows, :]
            s_diag = _causal_mask(_dot(qv, k_ref[rows, :], "nt"))
            m = jnp.max(s_diag, axis=-1, keepdims=True)
            if qi:
                s_before = _dot(qv, k_ref[before, :], "nt")
                m = jnp.maximum(m, jnp.max(s_before, axis=-1, keepdims=True))
            p = jnp.exp2((s_diag - m) * _SCALE_LOG2E)
            l = jnp.sum(p, axis=-1, keepdims=True)
            acc = _dot(p, v_ref[rows, :])
            if qi:
                p = jnp.exp2((s_before - m) * _SCALE_LOG2E)
                l = l + jnp.sum(p, axis=-1, keepdims=True)
                acc = acc + _dot(p, v_ref[before, :])
            o_ref[rows, :] = (acc / l).astype(BF16)
            lse_ref[rows, :] = jnp.broadcast_to(m * ATTN_SCALE + jnp.log(l), (t, LANES))

    T = batch * seq
    whole = lambda w: BS((None, seq, w), lambda b, h: (h, b, 0))
    return _call(body, name="flash_fwd", grid=(batch, HEADS),
                 in_specs=[whole(QK_PAD), whole(QK_PAD), whole(V_DIM), pl.BlockSpec(memory_space=pl.ANY)],
                 out_specs=[BS((seq, V_DIM), lambda b, h: (b, POOL_WIDTH // V_DIM + h)), whole(LANES)],
                 out_shape=[_sds((T, D_MODEL), BF16), _sds((HEADS, T, LANES), F32)],
                 aliases={3: 0})(q, k, v, mix)


def _flash_bwd(q, k, v, dmix, mix, lse, batch, seq):
    t = _attn_tile(seq)
    nq = seq // t

    def body(q_ref, k_ref, v_ref, do_ref, o_ref, lse_ref, dq_ref, dk_ref, dv_ref):
        for qi in range(nq):
            rows, before = _tile(qi, t), slice(0, qi * t)
            qv, do = q_ref[rows, :], do_ref[rows, :]
            lse2 = lse_ref[rows, 0:1] * math.log2(math.e)
            delta = jnp.sum(do.astype(F32) * o_ref[rows, :].astype(F32), axis=-1, keepdims=True)

            def block(keys, masked):
                kk = k_ref[keys, :]
                s = _dot(qv, kk, "nt")
                p = jnp.exp2((_causal_mask(s) if masked else s) * _SCALE_LOG2E - lse2)
                ds = p * (_dot(do, v_ref[keys, :], "nt") - delta)
                return _dot(p, do, "tn"), _dot(ds, qv, "tn") * ATTN_SCALE, _dot(ds, kk) * ATTN_SCALE

            dv_ref[rows, :], dk_ref[rows, :], dq = block(rows, True)
            if qi:
                dv, dk, dq_before = block(before, False)
                dv_ref[before, :] += dv
                dk_ref[before, :] += dk
                dq = dq + dq_before
            dq_ref[rows, :] = dq

    T = batch * seq
    whole = lambda w: BS((None, seq, w), lambda b, h: (h, b, 0))
    head_cols = BS((seq, V_DIM), lambda b, h: (b, POOL_WIDTH // V_DIM + h))
    return _call(body, name="flash_bwd", grid=(batch, HEADS),
                 in_specs=[whole(QK_PAD), whole(QK_PAD), whole(V_DIM), head_cols, head_cols, whole(LANES)],
                 out_specs=[whole(QK_PAD), whole(QK_PAD), whole(V_DIM)],
                 out_shape=[_sds((HEADS, T, QK_PAD), F32), _sds((HEADS, T, QK_PAD), F32), _sds((HEADS, T, V_DIM), F32)])(
        q, k, v, dmix, mix, lse)


def _adamw_math(w, g, m, v):
    m = ADAM_B1 * m + (1.0 - ADAM_B1) * g
    v = ADAM_B2 * v + (1.0 - ADAM_B2) * (g * g)
    m_hat = m / (1.0 - ADAM_B1 ** ADAM_STEP)
    v_hat = v / (1.0 - ADAM_B2 ** ADAM_STEP)
    return -ADAM_LR * (m_hat / (jnp.sqrt(v_hat) + ADAM_EPS) + ADAM_WD * w), m, v


def _adamw(name, w, g, m, v, l=0, prev=()):
    L, R, C = w.shape
    tr = 256 if R % 256 == 0 else R

    def body(w_ref, g_ref, m_ref, v_ref, *rest):
        go_ref, d_ref, mo_ref, vo_ref, token = rest[-5:]
        gv = g_ref[...]
        d_ref[...], mo_ref[...], vo_ref[...] = _adamw_math(w_ref[...], gv, m_ref[...], v_ref[...])
        go_ref[...] = gv
        token[...] = jnp.zeros_like(token)

    layer = BS((None, tr, C), lambda i: (l, i, 0))
    return _call(body, name=f"{name}_{l}", grid=(R // tr,),
                 in_specs=[layer, BS((tr, C), lambda i: (i, 0)), layer, layer] + [pl.BlockSpec(memory_space=pl.ANY)] * len(prev),
                 out_specs=[layer] * 4 + [BS((SUBLANES, LANES), lambda i: (0, 0))],
                 out_shape=[_sds((L, R, C), F32)] * 4 + [_sds((SUBLANES, LANES), F32)],
                 aliases={4 + n: n for n in range(len(prev))})(w, g, m, v, *prev)


def _place():
    x, y, c = lax.axis_index("x"), lax.axis_index("y"), lax.axis_index("c")
    other_chips = [(1 - x, y), (x, 1 - y), (1 - x, 1 - y)]
    return x, y, c, other_chips


def _remote(src, dst, send_sem, recv_sem, dev):
    return pltpu.make_async_remote_copy(src_ref=src, dst_ref=dst, send_sem=send_sem, recv_sem=recv_sem,
                                        device_id=dev, device_id_type=MESH)


def _prefetch_call(body, *, name, grid, in_specs, out_specs, out_shape):
    grid_spec = pltpu.PrefetchScalarGridSpec(num_scalar_prefetch=1, grid=grid, in_specs=in_specs, out_specs=out_specs)
    params = pltpu.CompilerParams(vmem_limit_bytes=VMEM_LIMIT_V7X, dimension_semantics=("arbitrary",) * len(grid))
    return pl.pallas_call(body, name=name, grid_spec=grid_spec, out_shape=out_shape, compiler_params=params)


def _row_tile(rows):
    return 256 if rows % 256 == 0 else rows


def _cast_place(name, w, layer, chip, after=()):
    _, _, rows, C = w.shape
    tr = _row_tile(rows)

    def body(chip_ref, w_ref, *rest):
        rest[-1][...] = w_ref[...].astype(BF16)

    return _prefetch_call(body, name=name, grid=(2, rows // tr),
                          in_specs=[BS((None, None, tr, C), lambda h, i, chip_ref: (layer, h, i, 0))]
                          + [pl.BlockSpec(memory_space=pl.ANY)] * len(after),
                          out_specs=BS((None, None, tr, C), lambda h, i, chip_ref: (chip_ref[0], h, i, 0)),
                          out_shape=pltpu.HBM((N_CHIPS, 2, rows, C), BF16))(chip, w, *after)


SEM = pl.BlockSpec(memory_space=pltpu.SEMAPHORE)


def _split_copy_call(body, *, name, in_specs, out_specs, out_shape, aliases):
    return pl.pallas_call(body, name=name, in_specs=in_specs, out_specs=out_specs, out_shape=out_shape,
                          input_output_aliases=aliases,
                          compiler_params=pltpu.CompilerParams(has_side_effects=pltpu.SideEffectType.DATAFLOW_SIDE_EFFECTING))


def _hbm(arrays):
    return [pltpu.with_memory_space_constraint(a, pltpu.HBM) for a in arrays]


def _gather_send(name, gs, groups, after):
    n = len(gs)

    def body(*refs):
        g, sems, token = refs[:n], refs[n + len(after):n + len(after) + 2 * len(groups)], refs[-1]
        x, y, c, chips = _place()
        me = 2 * x + y
        for gi, members in enumerate(groups):
            for a, i in enumerate(members):
                for k, (px, py) in enumerate(chips):
                    _remote(g[i].at[me, c], g[i].at[me, c], sems[2 * gi].at[3 * a + k], sems[2 * gi + 1].at[3 * a + k],
                            (px, py, c)).start()
        token[...] = jnp.zeros_like(token)

    sem_shapes = [pltpu.SemaphoreType.DMA((3 * len(members),)) for members in groups for _ in range(2)]
    out = _split_copy_call(body, name=name, in_specs=[HBM] * n + [pl.BlockSpec(memory_space=pl.ANY)] * len(after),
                           out_specs=[SEM] * len(sem_shapes) + [HBM] * n + [VMEM],
                           out_shape=sem_shapes + [pltpu.HBM(a.shape, a.dtype) for a in gs] + [_sds((SUBLANES, LANES), F32)],
                           aliases={i: len(sem_shapes) + i for i in range(n)})(*_hbm(gs), *after)
    return out[:len(sem_shapes)], out[len(sem_shapes):-1], out[-1]


def _gather_wait(name, gs, send_sems, recv_sems, after):
    n = len(gs)

    def body(*refs):
        g, ssem, rsem = refs[:n], refs[n], refs[n + 1]
        x, y, c, chips = _place()
        me = 2 * x + y
        for a in range(n):
            for k, (px, py) in enumerate(chips):
                landed = g[a].at[2 * px + py, c]
                cp = _remote(g[a].at[me, c], landed, ssem.at[3 * a + k], rsem.at[3 * a + k], (px, py, c))
                cp.wait_recv()
                cp.wait_send()

    return _split_copy_call(body, name=name, in_specs=[HBM] * n + [SEM, SEM] + [pl.BlockSpec(memory_space=pl.ANY)] * len(after),
                            out_specs=[HBM] * n, out_shape=[pltpu.HBM(a.shape, a.dtype) for a in gs],
                            aliases={i: i for i in range(n)})(*gs, send_sems, recv_sems, *after)


def _gather_pass(name, gs):
    n = len(gs)

    def body(*refs):
        g, send_sems, recv_sems = refs[n:2 * n], refs[-2], refs[-1]
        x, y, c, chips = _place()
        sibling = (x, y, 1 - c)
        passed = [_remote(g[i].at[2 * px + py, c], g[i].at[2 * px + py, c], send_sems.at[3 * i + k], recv_sems.at[3 * i + k], sibling)
                  for i in range(n) for k, (px, py) in enumerate(chips)]
        for cp in passed:
            cp.start()
        for i in range(n):
            for k, (px, py) in enumerate(chips):
                theirs = g[i].at[2 * px + py, 1 - c]
                _remote(theirs, theirs, send_sems.at[3 * i + k], recv_sems.at[3 * i + k], sibling).wait_recv()
        for cp in passed:
            cp.wait_send()

    return _call(body, name=name, in_specs=[HBM] * n, out_specs=[HBM] * n, out_shape=[_sds(a.shape, a.dtype) for a in gs],
                 aliases={i: i for i in range(n)},
                 scratch=[pltpu.SemaphoreType.DMA((3 * n,)), pltpu.SemaphoreType.DMA((3 * n,))])(*gs)


def _scatter_send(name, ps):
    n = len(ps)

    def body(*refs):
        p, r, ssem, rsem, token = refs[:n], refs[n:2 * n], refs[2 * n], refs[2 * n + 1], refs[-1]
        x, y, c, chips = _place()
        for i in range(n):
            for k, (px, py) in enumerate(chips):
                _remote(p[i].at[2 * px + py], r[i].at[k], ssem.at[3 * i + k], rsem.at[3 * i + k], (px, py, c)).start()
        token[...] = jnp.zeros_like(token)

    lands = [lax.empty((N_CHIPS - 1,) + a.shape[1:], a.dtype) for a in ps]
    sem = pltpu.SemaphoreType.DMA((3 * n,))
    out = _split_copy_call(body, name=name, in_specs=[HBM] * (2 * n), out_specs=[SEM, SEM] + [HBM] * (2 * n) + [VMEM],
                           out_shape=[sem, sem] + [pltpu.HBM(a.shape, a.dtype) for a in list(ps) + lands] + [_sds((SUBLANES, LANES), F32)],
                           aliases={i: 2 + i for i in range(2 * n)})(*_hbm(list(ps) + lands))
    return out[0], out[1], out[2:2 + n], out[2 + n:2 + 2 * n], out[-1]


def _scatter_wait(name, ps, lands, send_sems, recv_sems, after):
    n = len(ps)

    def body(*refs):
        p, r, ssem, rsem = refs[:n], refs[n:2 * n], refs[2 * n], refs[2 * n + 1]
        x, y, c, chips = _place()
        for i in range(n):
            for k, (px, py) in enumerate(chips):
                cp = _remote(p[i].at[2 * px + py], r[i].at[k], ssem.at[3 * i + k], rsem.at[3 * i + k], (px, py, c))
                cp.wait_recv()
                cp.wait_send()

    out = _split_copy_call(body, name=name, in_specs=[HBM] * (2 * n) + [SEM, SEM] + [pl.BlockSpec(memory_space=pl.ANY)] * len(after),
                           out_specs=[HBM] * (2 * n), out_shape=[pltpu.HBM(a.shape, a.dtype) for a in list(ps) + list(lands)],
                           aliases={i: i for i in range(2 * n)})(*ps, *lands, send_sems, recv_sems, *after)
    return out[:n], out[n:]


def _exchange_send(name, gs):
    n = len(gs)

    def body(*refs):
        g, r, ssem, rsem, token = refs[:n], refs[n:2 * n], refs[2 * n], refs[2 * n + 1], refs[-1]
        x, y, c, _ = _place()
        for i in range(n):
            _remote(g[i].at[:, 1 - c], r[i], ssem.at[i], rsem.at[i], (x, y, 1 - c)).start()
        token[...] = jnp.zeros_like(token)

    lands = [lax.empty((a.shape[0],) + a.shape[2:], a.dtype) for a in gs]
    sem = pltpu.SemaphoreType.DMA((n,))
    out = _split_copy_call(body, name=name, in_specs=[HBM] * (2 * n), out_specs=[SEM, SEM] + [HBM] * (2 * n) + [VMEM],
                           out_shape=[sem, sem] + [pltpu.HBM(a.shape, a.dtype) for a in list(gs) + lands] + [_sds((SUBLANES, LANES), F32)],
                           aliases={i: 2 + i for i in range(2 * n)})(*_hbm(list(gs) + lands))
    return out[0], out[1], out[2:2 + n], out[2 + n:2 + 2 * n], out[-1]


def _exchange_wait(name, gs, lands, send_sems, recv_sems, after):
    n = len(gs)

    def body(*refs):
        g, r, ssem, rsem = refs[:n], refs[n:2 * n], refs[2 * n], refs[2 * n + 1]
        x, y, c, _ = _place()
        for i in range(n):
            cp = _remote(g[i].at[:, 1 - c], r[i], ssem.at[i], rsem.at[i], (x, y, 1 - c))
            cp.wait_recv()
            cp.wait_send()

    out = _split_copy_call(body, name=name, in_specs=[HBM] * (2 * n) + [SEM, SEM] + [pl.BlockSpec(memory_space=pl.ANY)] * len(after),
                           out_specs=[HBM] * (2 * n), out_shape=[pltpu.HBM(a.shape, a.dtype) for a in list(gs) + list(lands)],
                           aliases={i: i for i in range(2 * n)})(*gs, *lands, send_sems, recv_sems, *after)
    return out[:n], out[n:]


def _sibling_share(name, fs, after=()):
    n = len(fs)

    def body(*refs):
        f, send_sems, recv_sems = refs[n:2 * n], refs[-2], refs[-1]
        x, y, c, _ = _place()
        sends = [_remote(f[i].at[c], f[i].at[c], send_sems.at[i], recv_sems.at[i], (x, y, 1 - c)) for i in range(n)]
        for cp in sends:
            cp.start()
        for i in range(n):
            theirs = f[i].at[1 - c]
            _remote(theirs, theirs, send_sems.at[i], recv_sems.at[i], (x, y, 1 - c)).wait_recv()
        for cp in sends:
            cp.wait_send()

    return _call(body, name=name, in_specs=[HBM] * n, out_specs=[HBM] * n,
                 out_shape=[_sds(a.shape, a.dtype) for a in fs], aliases={i: i for i in range(n)}, after=after,
                 scratch=[pltpu.SemaphoreType.DMA((n,)), pltpu.SemaphoreType.DMA((n,))])(*fs)


def _all_reduce_small(name, v):
    rows = v.shape[0] // 2
    halves = (2, rows, LANES)

    def body(v_ref, o_ref, from_sibling, chip_sums, send_sems, recv_sems):
        x, y, c, chips = _place()
        me, sibling = 2 * x + y, (x, y, 1 - c)
        swap = _remote(v_ref.at[1 - c], from_sibling, send_sems.at[0], recv_sems.at[0], sibling)
        swap.start()
        swap.wait()
        chip_sums[me] = v_ref[c] + from_sibling[...]
        sends = [_remote(chip_sums.at[me], chip_sums.at[me], send_sems.at[1 + k], recv_sems.at[1 + k], (px, py, c))
                 for k, (px, py) in enumerate(chips)]
        for cp in sends:
            cp.start()
        for k, (px, py) in enumerate(chips):
            theirs = chip_sums.at[2 * px + py]
            _remote(theirs, theirs, send_sems.at[1 + k], recv_sems.at[1 + k], (px, py, c)).wait_recv()
        for cp in sends:
            cp.wait_send()
        acc = chip_sums[0]
        for j in range(1, N_CHIPS):
            acc = acc + chip_sums[j]
        o_ref[c] = acc
        share = _remote(o_ref.at[c], o_ref.at[c], send_sems.at[4], recv_sems.at[4], sibling)
        share.start()
        share.wait_send()
        _remote(o_ref.at[1 - c], o_ref.at[1 - c], send_sems.at[4], recv_sems.at[4], sibling).wait_recv()

    return _call(body, name=name, in_specs=[VMEM], out_specs=VMEM, out_shape=_sds(halves, F32),
                 scratch=[pltpu.VMEM((rows, LANES), F32), pltpu.VMEM((N_CHIPS, rows, LANES), F32),
                          pltpu.SemaphoreType.DMA((5,)), pltpu.SemaphoreType.DMA((5,))])(v.reshape(halves)).reshape(v.shape)


def _add_halves(name, g, r, c):
    _, _, rows, C = g.shape
    tr = _row_tile(rows)

    def body(c_ref, g_ref, r_ref, o_ref):
        o_ref[...] = (g_ref[...].astype(F32) + r_ref[...].astype(F32)).astype(BF16)

    spec = BS((None, tr, C), lambda j, i, c_ref: (j, i, 0))
    return _prefetch_call(body, name=name, grid=(N_CHIPS, rows // tr),
                          in_specs=[BS((None, None, tr, C), lambda j, i, c_ref: (j, c_ref[0], i, 0)), spec], out_specs=spec,
                          out_shape=pltpu.HBM((N_CHIPS, rows, C), BF16))(c, g, r)


def _sum_partials(name, p, r, chip_c):
    _, rows, C = p.shape
    tr = _row_tile(rows)

    def body(s_ref, p_ref, r_ref, o_ref):
        acc = p_ref[...].astype(F32)
        for k in range(N_CHIPS - 1):
            acc = acc + r_ref[k].astype(F32)
        o_ref[...] = acc

    return _prefetch_call(body, name=name, grid=(rows // tr,),
                          in_specs=[BS((None, tr, C), lambda i, s: (s[0], i, 0)), BS((N_CHIPS - 1, tr, C), lambda i, s: (0, i, 0))],
                          out_specs=BS((None, tr, C), lambda i, s: (s[1], i, 0)), out_shape=pltpu.HBM((2, rows, C), F32))(chip_c, p, r)


_SHARDED = ("even_w_in", "even_w_out", "odd_w_in", "q_b", "kv_b", "odd_w_out", "ffn_w_gate", "ffn_w_up", "ffn_w_down")
_REPLICATED = ("mix_norm", "ffn_norm", "sg_ln_g", "sg_w_s", "sg_b_s", "pool_w", "q_norm", "k_norm")
_SMALL_SHARDED = ("sc_conv_w", "pool_scale", "q_a_norm", "kv_a_norm")
_WEIGHTS = ("mix_norm", "ffn_norm", "even_w_in", "sg_ln_g", "sg_w_s", "sg_b_s", "sc_conv_w", "even_w_out", "odd_w_in", "pool_w",
            "pool_scale", "q_a_norm", "q_b", "kv_a_norm", "kv_b", "q_norm", "k_norm", "odd_w_out", "ffn_w_gate", "ffn_w_up",
            "ffn_w_down")


def _pad_rows(flat, width, align):
    n = flat.shape[0]
    rows = -(-n // (width * align)) * align
    return jnp.pad(flat, (0, rows * width - n)).reshape(rows, width)


_GROUPS = {"even": ("even_w_in", "even_w_out"),
           "ffn0": ("ffn_w_gate0", "ffn_w_up0", "ffn_w_down0"),
           "odd": ("odd_w_in", "q_b", "kv_b", "odd_w_out"),
           "ffn1": ("ffn_w_gate1", "ffn_w_up1", "ffn_w_down1")}


def _place_shards(shards, names, chip, after):
    placed = []
    for n in names:
        weight, layer = (n[:-1], int(n[-1])) if n[-1].isdigit() else (n, 0)
        a = shards[weight]
        placed.append(_cast_place(f"place_{n}", a.reshape(a.shape[0], 2, a.shape[1] // 2, a.shape[2]), layer, chip, after))
    return placed


def _whole_weights(gathered):
    out = {n: a.reshape(N_CHIPS, -1, a.shape[-1]) for n, a in gathered.items()}
    for n in ("q_b", "kv_b"):
        if n in out:
            out[n] = out[n].transpose(1, 0, 2).reshape(out[n].shape[1], -1)
    for n in ("even_w_out", "odd_w_in", "odd_w_out"):
        if n in out:
            out[n] = out[n].reshape(-1, out[n].shape[-1])
    return out


def _forward_backward(x, positions, target, small, fetch, emit, advance):
    batch, seq, _ = x.shape
    T = batch * seq
    tm = _token_tile(seq)
    x0 = x.reshape(T, D_MODEL)

    inv_freq = ROPE_THETA ** (-jnp.arange(0, QK_ROPE, 2, dtype=F32) / QK_ROPE)
    ang = (positions.astype(F32)[..., None] * inv_freq).reshape(T, QK_ROPE // 2)
    cos, sin = jnp.cos(ang), jnp.sin(ang)
    pad = jnp.zeros((T, LANES - QK_ROPE), F32)
    cos_t = jnp.concatenate([cos, cos, pad], axis=1)
    sin_t = jnp.concatenate([-sin, sin, pad], axis=1)

    tril = jnp.tril(jnp.ones((SG_CHUNK, SG_CHUNK), bool))
    w_tril = jnp.where(tril[None], small["sg_w_s"][0], 0.0).astype(BF16)
    b_lanes = jnp.broadcast_to(small["sg_b_s"][0][:, :, None], (SG_HEADS, SG_CHUNK, SG_DIM))
    conv_w = jnp.pad(small["sc_conv_w"][0], ((0, SUBLANES - CONV_TAPS), (0, 0)))
    ln_g = small["sg_ln_g"]
    pool_diag = jnp.zeros((POOL_WIDTH, POOL_WIDTH), F32)
    for g in range(len(POOL_WINDOWS)):
        pool_diag = pool_diag.at[POOL_DIM * g:POOL_DIM * (g + 1), POOL_DIM * g:POOL_DIM * (g + 1)].set(small["pool_w"][0, g])
    pool_diag = pool_diag.astype(BF16)
    pool_scale = small["pool_scale"]
    q_g = jnp.pad(small["q_norm"], ((0, 0), (0, QK_PAD - QK_DIM)))
    k_g = jnp.pad(small["k_norm"], ((0, 0), (0, QK_PAD - QK_DIM)))
    qa_g, kva_g = small["q_a_norm"], small["kv_a_norm"]
    in_shard = EVEN_IN // N_CHIPS

    def ffn_weights(l, w):
        return w[f"ffn_w_gate{l}"], w[f"ffn_w_up{l}"], w[f"ffn_w_down{l}"]

    W = fetch("even", ())
    w_in_even = W["even_w_in"]
    h0 = _rmsnorm_fwd("mix0_norm", x0, small["mix_norm"][0], tm)
    tb = _big_tile(T)
    proj0 = _even_in(h0, w_in_even, _resident_tile(T))
    mix0 = _even_mixer_fwd(proj0, ln_g, w_tril, b_lanes, conv_w, seq, tm)
    w_out_even = W["even_w_out"]
    x1, h1 = _mm("even_out", "nn", mix0, w_out_even, F32, tk=1024, add=x0, fused=_norm_tail(small["ffn_norm"][0], T, tb))
    ffn0 = ffn_weights(0, fetch("ffn0", (x1,)))
    (x2, h2), ffn0_saved = _ffn_fwd(0, x1, h1, *ffn0, lambda tile: _norm_tail(small["mix_norm"][1], T, tile))
    W = fetch("odd", (x2,))
    w_in_odd = jnp.pad(W["odd_w_in"], ((0, 0), (0, ODD_IN_PAD - ODD_IN)))
    q_b = jnp.pad(W["q_b"].reshape(Q_LORA, HEADS, QK_DIM).transpose(1, 0, 2), ((0, 0), (0, 0), (0, QK_PAD - QK_DIM)))
    kv_b = W["kv_b"].reshape(KV_LORA, HEADS, QK_NOPE + V_DIM).transpose(1, 0, 2)
    proj1 = _mm("odd_in", "nn", h2, w_in_odd, F32, tk=1024)
    mix1 = _pool_fwd(proj1, pool_diag, pool_scale, seq, tm)
    q, k, v = _mla_qkv_fwd(proj1, cos_t, sin_t, qa_g, kva_g, q_b, kv_b, q_g, k_g, tm)
    mix1, lse = _flash_fwd(q, k, v, mix1, batch, seq)
    x3, h3 = _mm("odd_out", "nn", mix1, W["odd_w_out"], F32, tk=1024, add=x2, fused=_norm_tail(small["ffn_norm"][1], T, tb))
    ffn1 = ffn_weights(1, fetch("ffn1", (x3,)))
    (dy, sq), ffn1_saved = _ffn_fwd(1, x3, h3, *ffn1, lambda tile: _loss_tail(target.reshape(T, D_MODEL), tile))

    G = {}
    dx3, dffn_g1 = _ffn_bwd(1, x3, small["ffn_norm"][1], *ffn1, ffn1_saved, dy, emit)
    dmix1 = _mm("odd_out_dx", "nt", dx3, W["odd_w_out"], BF16, tk=1024, after=advance(dx3))
    dw_out_odd = _mm("odd_out_dw", "tn", mix1, dx3, BF16, hbm_out=True)
    dq, dk, dv = _flash_bwd(q, k, v, dmix1, mix1, lse, batch, seq)
    dz_pool, dpool_diag, G["pool_scale"] = _pool_bwd(proj1, dmix1, pool_diag, pool_scale, seq, tm)
    dproj1, dq_b, dkv_b, dq_g, dk_g, G["q_a_norm"], G["kv_a_norm"] = _mla_qkv_bwd(
        proj1, cos_t, sin_t, qa_g, kva_g, q_b, kv_b, q_g, k_g, dq, dk, dv, dz_pool, tm)
    G["pool_w"] = jnp.stack([dpool_diag[POOL_DIM * g:POOL_DIM * (g + 1), POOL_DIM * g:POOL_DIM * (g + 1)]
                             for g in range(len(POOL_WINDOWS))])[None]
    G["q_norm"], G["k_norm"] = dq_g[:, :QK_DIM], dk_g[:, :QK_DIM]
    dw_in_odd = _mm("odd_in_dw", "tn", h2, dproj1, BF16, tn=ODD_IN, hbm_out=True)

    def shard_major(g, cols):
        return g.reshape(g.shape[0], N_CHIPS, cols).transpose(1, 0, 2).astype(BF16)

    behind = emit("odd", {"odd_w_in": dw_in_odd.reshape(N_CHIPS, -1, ODD_IN),
                          "q_b": shard_major(dq_b[:, :, :QK_DIM].transpose(1, 0, 2).reshape(Q_LORA, HEADS * QK_DIM), HEADS * QK_DIM // N_CHIPS),
                          "kv_b": shard_major(dkv_b.transpose(1, 0, 2).reshape(KV_LORA, HEADS * (QK_NOPE + V_DIM)),
                                              HEADS * (QK_NOPE + V_DIM) // N_CHIPS),
                          "odd_w_out": dw_out_odd.reshape(N_CHIPS, -1, D_MODEL)})
    dx2, dmix_g1 = _mm("odd_in_dx", "nt", dproj1, W["odd_w_in"], F32, tk=ODD_IN, after=behind,
                       fused=_norm_bwd_tail(x2, small["mix_norm"][1], dx3, tb))
    dx1, dffn_g0 = _ffn_bwd(0, x1, small["ffn_norm"][0], *ffn0, ffn0_saved, dx2, emit, after=advance(dx2))
    dmix0 = _mm("even_out_dx", "nt", dx1, w_out_even, F32, tk=1024, after=advance(dx1))
    dw_out_even = _mm("even_out_dw", "tn", mix0, dx1, BF16, hbm_out=True)
    dproj0, dw_s, db_lanes, G["sg_ln_g"], dconv = _even_mixer_bwd(proj0, dmix0, ln_g, w_tril, b_lanes, conv_w, seq, tm)
    G["sg_w_s"] = dw_s[None]
    G["sg_b_s"] = jnp.sum(db_lanes, axis=-1)[None]
    G["sc_conv_w"] = dconv[None, :CONV_TAPS]
    tr = _resident_tile(T)
    tail, shapes, specs = _norm_bwd_tail(x0, small["mix_norm"][0], dx1, tr)
    dx0, dmix_g0 = _matmul("even_in_dx", "nt", [(dproj0, w_in_even)],
                           [(_row_spec(tr, EVEN_IN), _resident((N_CHIPS, D_MODEL, in_shard)))],
                           (T // tr, 1, 1), shapes, specs, (tr, D_MODEL), tail=tail)
    tk = min(512, T)
    dw_in_even = _grad_shards(
        "even_in_dw", h0, dproj0, BS((tk, D_MODEL), lambda k: (k, 0)), BS((tk, EVEN_IN), lambda k: (k, 0)),
        lambda a_ref, b_ref, j: (a_ref[...], b_ref[:, in_shard * j:in_shard * (j + 1)]), (N_CHIPS, D_MODEL, in_shard), T // tk)
    emit("even", {"even_w_in": dw_in_even, "even_w_out": dw_out_even.reshape(N_CHIPS, -1, D_MODEL)})
    G["mix_norm"] = jnp.concatenate([dmix_g0, dmix_g1], axis=0)
    G["ffn_norm"] = jnp.concatenate([dffn_g0, dffn_g1], axis=0)
    return sq[0, 0], dx0.reshape(batch, seq, D_MODEL), G


def _small_vector(parts, names):
    flat = jnp.concatenate([parts[n].astype(F32).reshape(-1) for n in names])
    return _pad_rows(flat, LANES, 2 * SUBLANES)


def _split_small(vec, like, names):
    out, off, flat = {}, 0, vec.reshape(-1)
    for n in names:
        size = math.prod(like[n].shape)
        out[n] = flat[off:off + size].reshape(like[n].shape)
        off += size
    return out


def _whole_shape(a):
    return a.shape[:-1] + (a.shape[-1] * N_CHIPS,)


def kernel(x, positions, mix_norm, ffn_norm, even_w_in, sg_ln_g, sg_w_s, sg_b_s, sc_conv_w, even_w_out, odd_w_in, pool_w, pool_scale, q_a_norm, q_b, kv_a_norm, kv_b, q_norm, k_norm, odd_w_out, ffn_w_gate, ffn_w_up, ffn_w_down, loss_target, m_mix_norm, m_ffn_norm, m_even_w_in, m_sg_ln_g, m_sg_w_s, m_sg_b_s, m_sc_conv_w, m_even_w_out, m_odd_w_in, m_pool_w, m_pool_scale, m_q_a_norm, m_q_b, m_kv_a_norm, m_kv_b, m_q_norm, m_k_norm, m_odd_w_out, m_ffn_w_gate, m_ffn_w_up, m_ffn_w_down, v_mix_norm, v_ffn_norm, v_even_w_in, v_sg_ln_g, v_sg_w_s, v_sg_b_s, v_sc_conv_w, v_even_w_out, v_odd_w_in, v_pool_w, v_pool_scale, v_q_a_norm, v_q_b, v_kv_a_norm, v_kv_b, v_q_norm, v_k_norm, v_odd_w_out, v_ffn_w_gate, v_ffn_w_up, v_ffn_w_down):
    args = dict(locals())
    w = {n: args[n] for n in _WEIGHTS}
    m = {n: args["m_" + n] for n in _WEIGHTS}
    v = {n: args["v_" + n] for n in _WEIGHTS}
    cx, cy, cc = lax.axis_index("x"), lax.axis_index("y"), lax.axis_index("c")
    chip = 2 * cx + cy
    transposed = ("ffn_w_gate", "ffn_w_up")
    for n in transposed:
        w[n], m[n], v[n] = (jnp.swapaxes(t[n], 1, 2) for t in (w, m, v))

    chip_arr = chip.astype(jnp.int32).reshape(1)
    c_arr = cc.astype(jnp.int32).reshape(1)
    group_names = list(_GROUPS)
    placed = {}
    for n in _SMALL_SHARDED:
        a = w[n]
        whole = jnp.zeros(a.shape[:-1] + (N_CHIPS, a.shape[-1]), F32)
        whole = lax.dynamic_update_slice_in_dim(whole, a[..., None, :], chip, axis=a.ndim - 1)
        placed[n] = jnp.where(cc == 0, whole, 0.0).reshape(_whole_shape(a))
    small_whole = _all_reduce_small("gather_small_weights", _small_vector(placed, _SMALL_SHARDED))
    small = dict({n: w[n] for n in _REPLICATED}, **_split_small(small_whole, placed, _SMALL_SHARDED))

    first, rest = list(_GROUPS[group_names[0]]), [n for g in group_names[1:] for n in _GROUPS[g]]
    sems_first, flight_first, token = _gather_send("gather_send_first", _place_shards(w, first, chip_arr, (small_whole,)),
                                                   [list(range(len(first)))], (small_whole,))
    sems_rest, flight_rest, all_sent = _gather_send("gather_send_rest", _place_shards(w, rest, chip_arr, (token,)),
                                                    [[rest.index(n) for n in _GROUPS[g]] for g in group_names[1:]], ())
    sems = list(sems_first) + list(sems_rest)
    in_flight = dict(zip(first + rest, list(flight_first) + list(flight_rest)))

    def fetch(group, after):
        gi, members = group_names.index(group), _GROUPS[group]
        after = after if gi else (all_sent,)
        landed = _gather_wait(f"gather_wait_{group}", [in_flight[n] for n in members], sems[2 * gi], sems[2 * gi + 1], after)
        return _whole_weights(dict(zip(members, _gather_pass(f"gather_pass_{group}", landed))))

    swapping, pending, arrived, sent = [], [], {}, []

    def settle(after):
        names, ps, lands, send_sems, recv_sems = pending.pop()
        ps, lands = _scatter_wait(f"scatter_wait_{names[0]}", ps, lands, send_sems, recv_sems, after)
        arrived.update({n: (p, r) for n, p, r in zip(names, ps, lands)})

    def emit(group, grads):
        names = _GROUPS[group]
        halves = [grads[n].reshape(N_CHIPS, 2, grads[n].shape[1] // 2, grads[n].shape[2]) for n in names]
        send_sems, recv_sems, halves, lands, token = _exchange_send(f"exchange_send_{group}", halves)
        swapping.append((group, halves, lands, send_sems, recv_sems))
        sent.append(token)
        return (token,)

    def advance(done):
        done = done if isinstance(done, tuple) else (done,)
        group, halves, lands, send_sems, recv_sems = swapping.pop()
        names = _GROUPS[group]
        halves, lands = _exchange_wait(f"exchange_wait_{group}", halves, lands, send_sems, recv_sems, done)
        partial = [_add_halves(f"add_{n}", g, r, c_arr) for n, g, r in zip(names, halves, lands)]
        if pending:
            settle(done)
        send_sems, recv_sems, ps, lands, token = _scatter_send(f"scatter_send_{group}", partial)
        pending.append((names, ps, lands, send_sems, recv_sems))
        return (token,)

    sq, grad_x, G = _forward_backward(x, positions, loss_target, small, fetch, emit, advance)
    loss = lax.psum(0.5 * sq / D_MODEL, ("x", "y", "c"))

    small_names = _REPLICATED + _SMALL_SHARDED
    summed = _split_small(_all_reduce_small("reduce_small_grads", _small_vector(G, small_names)), G, small_names)
    grads = {n: summed[n] for n in _REPLICATED}
    for n in _SMALL_SHARDED:
        a = w[n]
        grads[n] = lax.dynamic_slice_in_dim(summed[n].reshape(a.shape[:-1] + (N_CHIPS, a.shape[-1])), chip, 1,
                                            axis=a.ndim - 1).reshape(a.shape)

    chip_c = jnp.stack([chip, cc]).astype(jnp.int32)
    out = {}

    def finish(group, after):
        names, tokens = _GROUPS[group], []
        sums = [_sum_partials(f"sum_{n}", *arrived[n], chip_c) for n in names]
        for n, f in zip(names, _sibling_share(f"grad_share_{group}", sums, after)):
            weight, layer = (n[:-1], int(n[-1])) if n[-1].isdigit() else (n, 0)
            *out[weight], token = _adamw(f"adamw_{weight}", w[weight], f.reshape(-1, f.shape[-1]), m[weight], v[weight], layer,
                                         out.get(weight, ()))
            tokens.append(token)
        return tuple(tokens)

    last_exchange = tuple(sent[-1:])
    last_scatter = advance(finish(group_names[3], last_exchange) + finish(group_names[2], last_exchange))
    settle(finish(group_names[1], last_scatter))
    finish(group_names[0], ())
    packed = [_small_vector(d, small_names) for d in (w, grads, m, v)]
    res = _adamw("adamw_small", packed[0][None], packed[1], packed[2][None], packed[3][None])
    delta_s, m_s, v_s = (_split_small(r, w, small_names) for r in res[1:4])
    for n in small_names:
        out[n] = (grads[n], delta_s[n], m_s[n], v_s[n])
    for n in transposed:
        out[n] = tuple(jnp.swapaxes(t, 1, 2) for t in out[n])

    return (loss, grad_x, *[out[n][0] for n in _WEIGHTS], *[out[n][1] for n in _WEIGHTS],
            *[out[n][2] for n in _WEIGHTS], *[out[n][3] for n in _WEIGHTS])
```

```python
import functools
import math

import jax
import jax.numpy as jnp
from jax import lax
from jax.experimental import pallas as pl
from jax.experimental.pallas import tpu as pltpu

F32, BF16 = jnp.float32, jnp.bfloat16
BS = pl.BlockSpec

D_MODEL = 1024
EPS = 1e-6
NEG_INF = -1e30
SG_HEADS, SG_DIM, SG_WIDTH, SG_CHUNK = 4, 128, 512, 128
SC_WIDTH, CONV_TAPS = 512, 3
EVEN_IN = 2 * SG_WIDTH + 3 * SC_WIDTH
POOL_WINDOWS = (2, 4, 8, 16)
POOL_DIM, POOL_WIDTH = 64, 256
POOL_HALO = 16
HEADS, Q_LORA, KV_LORA, QK_NOPE, QK_ROPE, V_DIM = 6, 384, 256, 128, 64, 128
QK_DIM = QK_NOPE + QK_ROPE
QK_PAD = 256
ODD_IN = POOL_WIDTH + Q_LORA + KV_LORA + QK_ROPE
ODD_IN_PAD = 1024
ROPE_THETA = 10000.0
ATTN_SCALE = QK_DIM ** -0.5
D_FF, N_CHIPS = 2816, 4
FF_SHARD = D_FF // N_CHIPS
ADAM_LR, ADAM_B1, ADAM_B2, ADAM_EPS, ADAM_WD, ADAM_STEP = 0.001, 0.9, 0.999, 1e-08, 0.01, 10
VMEM_LIMIT_V7X = 48 * 2**20
LANES, SUBLANES = 128, 8
MESH = pl.DeviceIdType.MESH
HBM = pl.BlockSpec(memory_space=pltpu.HBM)
VMEM = pl.BlockSpec(memory_space=pltpu.VMEM)

_DIMS = {"nn": (((1,), (0,)), ((), ())), "nt": (((1,), (1,)), ((), ())), "tn": (((0,), (0,)), ((), ()))}


def _dot(a, b, mode="nn"):
    return lax.dot_general(a.astype(BF16), b.astype(BF16), _DIMS[mode], preferred_element_type=F32)


def _call(body, *, name, out_shape, in_specs, out_specs, grid=(), scratch=(), aliases=None, after=()):
    params = pltpu.CompilerParams(vmem_limit_bytes=VMEM_LIMIT_V7X,
                                  **({"dimension_semantics": ("arbitrary",) * len(grid)} if grid else {}))
    n_in, n_after = len(in_specs), len(after)
    kernel_body = body if not after else (lambda *refs: body(*refs[:n_in], *refs[n_in + n_after:]))
    call = pl.pallas_call(kernel_body, name=name, grid=grid, in_specs=list(in_specs) + [pl.BlockSpec(memory_space=pl.ANY)] * n_after,
                          out_specs=out_specs, out_shape=out_shape, scratch_shapes=list(scratch),
                          input_output_aliases=aliases or {}, compiler_params=params)
    return (lambda *ops: call(*ops, *after)) if after else call


def _sds(shape, dtype):
    return jax.ShapeDtypeStruct(tuple(shape), dtype)


def _token_tile(seq):
    return 512 if seq % 512 == 0 else seq


_TAIL_ROWS = 256


def _matmul(name, mode, pairs, pair_specs, grid, out_shape, out_spec, acc_shape, add=None, add_spec=None, after=(), tail=None):
    n, nk = len(pairs), grid[-1]
    n_add = int(add is not None)
    n_tail = len(tail[0]) if tail else 0
    n_in = 2 * n + n_add + n_tail
    n_out = len(out_shape) if tail else 1

    def body(*refs):
        ab = refs[:2 * n]
        add_ref = refs[2 * n] if n_add else None
        tail_refs, outs = refs[2 * n + n_add:n_in], refs[n_in:n_in + n_out]
        first = pl.program_id(0) == 0

        def finish(result):
            if tail is None:
                r = result(slice(None))
                outs[0][...] = (r if add_ref is None else r + add_ref[...]).astype(outs[0].dtype)
                return
            for lo in range(0, acc_shape[0], _TAIL_ROWS):
                rows = slice(lo, min(lo + _TAIL_ROWS, acc_shape[0]))
                r = result(rows)
                tail[2](rows, r if add_ref is None else r + add_ref[rows, :], first, tail_refs, outs)

        def terms(a_ref, b_ref):
            if len(a_ref.shape) == 2 and len(b_ref.shape) == 2:
                return [(a_ref[...], b_ref[...])]
            cols = a_ref.shape[-1] // N_CHIPS
            return [(a_ref[j] if len(a_ref.shape) == 3 else a_ref[:, cols * j:cols * (j + 1)], b_ref[j]) for j in range(N_CHIPS)]

        if nk == 1:
            r = None
            for p in range(n):
                for a_blk, b_blk in terms(ab[2 * p], ab[2 * p + 1]):
                    d = _dot(a_blk, b_blk, mode)
                    r = d if r is None else r + d
            finish(lambda rows: r[rows])
            return
        acc = refs[-1]
        k = pl.program_id(len(grid) - 1)

        @pl.when(k == 0)
        def _():
            acc[...] = jnp.zeros_like(acc)

        for p in range(n):
            acc[...] += _dot(ab[2 * p][...], ab[2 * p + 1][...], mode)

        @pl.when(k == nk - 1)
        def _():
            finish(lambda rows: acc[rows, :])

    ops = [t for pr in pairs for t in pr] + ([add] if n_add else []) + (list(tail[0]) if tail else [])
    specs = [s for pr in pair_specs for s in pr] + ([add_spec] if n_add else []) + (list(tail[1]) if tail else [])
    return _call(body, name=name, grid=grid, in_specs=specs, out_specs=out_spec, out_shape=out_shape,
                 scratch=[pltpu.VMEM(acc_shape, F32)] if nk > 1 else [], after=after)(*ops)


def _row_spec(tm, d):
    return BS((tm, d), lambda i, j, k: (i, 0))


def _vec_spec(d):
    return BS((1, d), lambda i, j, k: (0, 0))


def _norm_tail(gain, T, tm):
    d = gain.shape[-1]

    def fn(rows, r, first, tail_refs, outs):
        outs[0][rows, :] = r
        outs[1][rows, :] = (r * lax.rsqrt(jnp.mean(r * r, axis=-1, keepdims=True) + EPS) * tail_refs[0][...]).astype(BF16)

    return ([gain.reshape(1, d)], [_vec_spec(d)], fn), [_sds((T, d), F32), _sds((T, d), BF16)], [_row_spec(tm, d), _row_spec(tm, d)]


def _norm_bwd_tail(x, gain, dres, tm):
    T, d = x.shape

    def fn(rows, r, first, tail_refs, outs):
        x_ref, g_ref, dres_ref = tail_refs
        xv = x_ref[rows, :]
        rstd = lax.rsqrt(jnp.mean(xv * xv, axis=-1, keepdims=True) + EPS)
        xhat = xv * rstd
        if rows.start == 0:
            @pl.when(first)
            def _():
                outs[1][...] = jnp.zeros_like(outs[1])

        outs[1][...] += jnp.sum(r * xhat, axis=0, keepdims=True)
        dxhat = r * g_ref[...]
        outs[0][rows, :] = dres_ref[rows, :] + rstd * (dxhat - xhat * jnp.mean(dxhat * xhat, axis=-1, keepdims=True))

    return (([x, gain.reshape(1, d), dres], [_row_spec(tm, d), _vec_spec(d), _row_spec(tm, d)], fn),
            [_sds((T, d), F32), _sds((1, d), F32)], [_row_spec(tm, d), _vec_spec(d)])


def _loss_tail(target, tm):
    T, d = target.shape

    def fn(rows, r, first, tail_refs, outs):
        e = r - tail_refs[0][rows, :]
        if rows.start == 0:
            @pl.when(first)
            def _():
                outs[1][...] = jnp.zeros_like(outs[1])

        outs[1][...] += jnp.sum(e * e)
        outs[0][rows, :] = e * (1.0 / d)

    return (([target], [_row_spec(tm, d)], fn), [_sds((T, d), F32), _sds((SUBLANES, LANES), F32)],
            [_row_spec(tm, d), BS((SUBLANES, LANES), lambda i, j, k: (0, 0))])


def _grad_shards(name, a, b, a_spec, b_spec, pick, out_shape, n_steps):
    def body(a_ref, b_ref, o_ref, acc):
        k = pl.program_id(0)

        @pl.when(k == 0)
        def _():
            acc[...] = jnp.zeros_like(acc)

        for j in range(N_CHIPS):
            aj, bj = pick(a_ref, b_ref, j)
            acc[j] += _dot(aj, bj, "tn")

        @pl.when(k == n_steps - 1)
        def _():
            o_ref[...] = acc[...].astype(BF16)

    return _call(body, name=name, grid=(n_steps,), in_specs=[a_spec, b_spec], scratch=[pltpu.VMEM(tuple(out_shape), F32)],
                 out_specs=BS(out_shape, lambda k: (0, 0, 0)), out_shape=pltpu.HBM(tuple(out_shape), BF16))(a, b)


def _mm(name, mode, a, b, out_dtype, tm=1024, tn=1024, tk=512, add=None, after=(), fused=None, hbm_out=False):
    if mode == "tn":
        (K, M), N = a.shape, b.shape[1]
    else:
        (M, K), N = a.shape, (b.shape[1] if mode == "nn" else b.shape[0])
    tm, tn, tk = min(tm, M), min(tn, N), min(tk, K)
    a_spec = BS((tk, tm), lambda i, j, k: (k, i)) if mode == "tn" else BS((tm, tk), lambda i, j, k: (i, k))
    b_spec = BS((tn, tk), lambda i, j, k: (j, k)) if mode == "nt" else BS((tk, tn), lambda i, j, k: (k, j))
    o_spec = BS((tm, tn), lambda i, j, k: (i, j))
    tail, shapes, specs = fused if fused else (None, pltpu.HBM((M, N), out_dtype) if hbm_out else _sds((M, N), out_dtype), o_spec)
    return _matmul(name, mode, [(a, b)], [(a_spec, b_spec)], (M // tm, N // tn, K // tk), shapes, specs, (tm, tn),
                   add=add, add_spec=o_spec if add is not None else None, after=after, tail=tail)


_PASS_ROWS = 256


def _ffn_up(name, h, wg, wu, tm):
    T = h.shape[0]

    def body(h_ref, wg_ref, wu_ref, g_ref, u_ref, a_ref):
        hv = h_ref[...]
        g = _dot(hv, wg_ref[...], "nt")
        u = _dot(hv, wu_ref[...], "nt")
        g_ref[...] = g.astype(BF16)
        u_ref[...] = u.astype(BF16)
        a_ref[...] = (g * (1.0 / (1.0 + jnp.exp(-g))) * u).astype(BF16)

    w_spec = BS((None, FF_SHARD, D_MODEL), lambda j, i: (j, 0, 0))
    o_spec = BS((None, tm, FF_SHARD), lambda j, i: (j, i, 0))
    sh = _sds((N_CHIPS, T, FF_SHARD), BF16)
    return _call(body, name=name, grid=(N_CHIPS, T // tm), in_specs=[BS((tm, D_MODEL), lambda j, i: (i, 0)), w_spec, w_spec],
                 out_specs=[o_spec, o_spec, o_spec], out_shape=[sh, sh, sh])(h, wg, wu)


def _ffn_act_bwd(name, dxo, wd, g, u, tm, after=()):
    T = dxo.shape[0]

    def body(dx_ref, wd_ref, g_ref, u_ref, dg_ref, du_ref):
        da = _dot(dx_ref[...], wd_ref[...], "nt")
        g = g_ref[...].astype(F32)
        sig = 1.0 / (1.0 + jnp.exp(-g))
        dg_ref[...] = (da * u_ref[...].astype(F32) * (sig * (1.0 + g * (1.0 - sig)))).astype(BF16)
        du_ref[...] = (da * (g * sig)).astype(BF16)

    t_spec = BS((None, tm, FF_SHARD), lambda i, j: (j, i, 0))
    sh = _sds((N_CHIPS, T, FF_SHARD), BF16)
    return _call(body, name=name, grid=(T // tm, N_CHIPS),
                 in_specs=[BS((tm, D_MODEL), lambda i, j: (i, 0)), BS((None, FF_SHARD, D_MODEL), lambda i, j: (j, 0, 0)), t_spec, t_spec],
                 out_specs=[t_spec, t_spec], out_shape=[sh, sh], after=after)(dxo, wd, g, u)


def _big_tile(n):
    return min(1024, n)


def _resident_tile(n):
    return min(512, n)


def _resident(shape):
    return BS(shape, lambda i, j, k: (0,) * len(shape), pipeline_mode=pl.Buffered(1))


def _ffn_fwd(l, x, h, wg, wu, wd, fused):
    T = x.shape[0]
    g, u, a = _ffn_up(f"ffn{l}_up", h, wg, wu, _big_tile(T))
    tm = _resident_tile(T)
    tail, shapes, specs = fused(tm)
    outs = _matmul(f"ffn{l}_down", "nn", [(a, wd)],
                   [(BS((N_CHIPS, tm, FF_SHARD), lambda i, j, k: (0, i, 0)), _resident((N_CHIPS, FF_SHARD, D_MODEL)))],
                   (T // tm, 1, 1), shapes, specs, (tm, D_MODEL), add=x, add_spec=_row_spec(tm, D_MODEL), tail=tail)
    return outs, (h, g, u, a)


def _ffn_bwd(l, x, gain, wg, wu, wd, saved, dxo, emit, after=()):
    h, g, u, a = saved
    T = x.shape[0]
    tm = _big_tile(T)
    dg, du = _ffn_act_bwd(f"ffn{l}_act_bwd", dxo, wd, g, u, tm, after=after)
    tk = _big_tile(T)
    shards_spec = BS((N_CHIPS, tk, FF_SHARD), lambda k: (0, k, 0))
    rows_spec = BS((tk, D_MODEL), lambda k: (k, 0))

    def dw(nm, act, rows):
        return _grad_shards(nm, act, rows, shards_spec, rows_spec, lambda a_ref, b_ref, j: (a_ref[j], b_ref[...]),
                            (N_CHIPS, FF_SHARD, D_MODEL), T // tk)

    behind = emit(f"ffn{l}", {f"ffn_w_gate{l}": dw(f"ffn{l}_dwg", dg, h), f"ffn_w_up{l}": dw(f"ffn{l}_dwu", du, h),
                              f"ffn_w_down{l}": dw(f"ffn{l}_dwd", a, dxo)})
    tm = _resident_tile(T)
    act_spec = BS((N_CHIPS, tm, FF_SHARD), lambda i, j, k: (0, i, 0))
    w_spec = _resident((N_CHIPS, FF_SHARD, D_MODEL))
    tail, shapes, specs = _norm_bwd_tail(x, gain, dxo, tm)
    return _matmul(f"ffn{l}_dh", "nn", [(dg, wg), (du, wu)], [(act_spec, w_spec), (act_spec, w_spec)],
                   (T // tm, 1, 1), shapes, specs, (tm, D_MODEL), after=behind, tail=tail)


_INV_SQRT2 = 1.0 / math.sqrt(2.0)
_INV_SQRT_2PI = 1.0 / math.sqrt(2.0 * math.pi)


def _gelu(x):
    return 0.5 * x * (1.0 + lax.erf(x * _INV_SQRT2))


def _gelu_and_grad(x):
    cdf = 0.5 * (1.0 + lax.erf(x * _INV_SQRT2))
    return x * cdf, cdf + x * jnp.exp(-0.5 * x * x) * _INV_SQRT_2PI


def _shift_down(x, k):
    return pltpu.roll(x, k, 0)


def _shift_up(x, k):
    return pltpu.roll(x, x.shape[0] - k, 0)


def _layer_norm_head(xh):
    xc = xh - jnp.mean(xh, axis=-1, keepdims=True)
    rstd = lax.rsqrt(jnp.mean(xc * xc, axis=-1, keepdims=True) + EPS)
    return xc * rstd, rstd


def _even_in(x, gain, w, tm):
    T, d = x.shape
    shard = w.shape[-1]

    def body(x_ref, g_ref, w_ref, o_ref, h_ref):
        xv = x_ref[...]
        hv = (xv * lax.rsqrt(jnp.mean(xv * xv, axis=-1, keepdims=True) + EPS) * g_ref[...]).astype(BF16)
        h_ref[...] = hv
        for j in range(N_CHIPS):
            o_ref[:, shard * j:shard * (j + 1)] = _dot(hv, w_ref[j])

    row = BS((tm, d), lambda i: (i, 0))
    return _call(body, name="even_in", grid=(T // tm,),
                 in_specs=[row, BS((1, d), lambda i: (0, 0)), BS(w.shape, lambda i: (0, 0, 0), pipeline_mode=pl.Buffered(1))],
                 out_specs=[BS((tm, N_CHIPS * shard), lambda i: (i, 0)), row],
                 out_shape=[_sds((T, N_CHIPS * shard), F32), _sds((T, d), BF16)])(x, gain.reshape(1, d), w)


def _even_halo_specs(tm, n_tiles, col_blocks, after):
    rows = tm // SUBLANES
    last = n_tiles * rows - 1
    if after:
        return [BS((SUBLANES, 512), functools.partial(lambda cb, i: (jnp.minimum((i + 1) * rows, last), cb), cb)) for cb in col_blocks]
    return [BS((SUBLANES, 512), functools.partial(lambda cb, i: (jnp.maximum(i * rows - 1, 0), cb), cb)) for cb in col_blocks]


def _even_mixer_fwd(proj, ln_g, w_tril, b_lanes, conv_w, seq, tm):
    T = proj.shape[0]
    tiles_per_seq = seq // tm

    def body(p_ref, hc_ref, hh_ref, lng_ref, w_ref, bb_ref, cw_ref, o_ref):
        first = pl.program_id(0) % tiles_per_seq == 0
        for h in range(SG_HEADS):
            cols = slice(SG_DIM * h, SG_DIM * (h + 1))
            vhat, _ = _layer_norm_head(_gelu(p_ref[:, SG_WIDTH + SG_DIM * h:SG_WIDTH + SG_DIM * (h + 1)]))
            vln = (vhat * lng_ref[:, cols]).astype(BF16)
            for k in range(tm // SG_CHUNK):
                rows = slice(SG_CHUNK * k, SG_CHUNK * (k + 1))
                mixed = _dot(w_ref[h], vln[rows]) + bb_ref[h]
                o_ref[rows, cols] = (_gelu(p_ref[rows, cols]) * mixed).astype(BF16)
        z = p_ref[:, 1536:2048] * p_ref[:, 2048:2560]
        zz = jnp.concatenate([jnp.where(first, 0.0, hc_ref[...] * hh_ref[...]), z], axis=0)
        y = cw_ref[0:1, :] * _shift_down(zz, 2)[SUBLANES:] + cw_ref[1:2, :] * _shift_down(zz, 1)[SUBLANES:] + cw_ref[2:3, :] * z
        o_ref[:, SG_WIDTH:] = (p_ref[:, 1024:1536] * y).astype(BF16)

    full = lambda shape: BS(shape, lambda i: (0,) * len(shape))
    return _call(body, name="even_mixer_fwd", grid=(T // tm,),
                 in_specs=[BS((tm, EVEN_IN), lambda i: (i, 0))] + _even_halo_specs(tm, T // tm, (3, 4), after=False)
                 + [full((1, SG_WIDTH)), full((SG_HEADS, SG_CHUNK, SG_CHUNK)), full((SG_HEADS, SG_CHUNK, SG_DIM)), full((SUBLANES, SC_WIDTH))],
                 out_specs=BS((tm, D_MODEL), lambda i: (i, 0)), out_shape=_sds((T, D_MODEL), BF16))(
        proj, proj, proj, ln_g, w_tril, b_lanes, conv_w)


def _even_mixer_bwd(proj, dmix, ln_g, w_tril, b_lanes, conv_w, seq, tm):
    T = proj.shape[0]
    n_tiles, tiles_per_seq = T // tm, seq // tm

    def body(p_ref, dm_ref, hc_ref, hh_ref, nd_ref, nb_ref, lng_ref, w_ref, bb_ref, cw_ref,
             dp_ref, dw_ref, db_ref, dlng_ref, dcw_ref):
        i = pl.program_id(0)
        first = i % tiles_per_seq == 0
        last = i % tiles_per_seq == tiles_per_seq - 1

        @pl.when(i == 0)
        def _():
            dw_ref[...] = jnp.zeros_like(dw_ref)
            db_ref[...] = jnp.zeros_like(db_ref)
            dlng_ref[...] = jnp.zeros_like(dlng_ref)
            dcw_ref[...] = jnp.zeros_like(dcw_ref)

        for h in range(SG_HEADS):
            cols = slice(SG_DIM * h, SG_DIM * (h + 1))
            vcols = slice(SG_WIDTH + SG_DIM * h, SG_WIDTH + SG_DIM * (h + 1))
            lng = lng_ref[:, cols]
            for k in range(tm // SG_CHUNK):
                rows = slice(SG_CHUNK * k, SG_CHUNK * (k + 1))
                gelu_v, dgelu_v = _gelu_and_grad(p_ref[rows, vcols])
                vhat, rstd = _layer_norm_head(gelu_v)
                vln = (vhat * lng).astype(BF16)
                mixed = _dot(w_ref[h], vln) + bb_ref[h]
                gelu_u, dgelu_u = _gelu_and_grad(p_ref[rows, cols])
                da = dm_ref[rows, cols]
                dp_ref[rows, cols] = (da * mixed * dgelu_u).astype(BF16)
                dmixed = da * gelu_u
                db_ref[h] += dmixed
                dw_ref[h] += _dot(dmixed, vln, "nt")
                dvln = _dot(w_ref[h], dmixed, "tn")
                dlng_ref[:, cols] += jnp.sum(dvln * vhat, axis=0, keepdims=True)
                dvhat = dvln * lng
                dgv = rstd * (dvhat - jnp.mean(dvhat, axis=-1, keepdims=True)
                              - vhat * jnp.mean(dvhat * vhat, axis=-1, keepdims=True))
                dp_ref[rows, vcols] = (dgv * dgelu_v).astype(BF16)

        b = p_ref[:, 1024:1536]
        c = p_ref[:, 1536:2048]
        hv = p_ref[:, 2048:2560]
        z = c * hv
        zz = jnp.concatenate([jnp.where(first, 0.0, hc_ref[...] * hh_ref[...]), z], axis=0)
        z1 = _shift_down(zz, 1)[SUBLANES:]
        z2 = _shift_down(zz, 2)[SUBLANES:]
        w0, w1, w2 = cw_ref[0:1, :], cw_ref[1:2, :], cw_ref[2:3, :]
        dbo = dm_ref[:, SG_WIDTH:]
        dy = dbo * b
        dd = jnp.concatenate([dy, jnp.where(last, 0.0, nd_ref[...] * nb_ref[...])], axis=0)
        dz = w2 * dy + w1 * _shift_up(dd, 1)[:tm] + w0 * _shift_up(dd, 2)[:tm]
        dp_ref[:, 1024:1536] = (dbo * (w0 * z2 + w1 * z1 + w2 * z)).astype(BF16)
        dp_ref[:, 1536:2048] = (dz * hv).astype(BF16)
        dp_ref[:, 2048:2560] = (dz * c).astype(BF16)
        dcw_ref[0:1, :] += jnp.sum(dy * z2, axis=0, keepdims=True)
        dcw_ref[1:2, :] += jnp.sum(dy * z1, axis=0, keepdims=True)
        dcw_ref[2:3, :] += jnp.sum(dy * z, axis=0, keepdims=True)

        @pl.when(i == n_tiles - 1)
        def _():
            t_idx = lax.broadcasted_iota(jnp.int32, (SG_CHUNK, SG_CHUNK), 0)
            s_idx = lax.broadcasted_iota(jnp.int32, (SG_CHUNK, SG_CHUNK), 1)
            for h in range(SG_HEADS):
                dw_ref[h] = jnp.where(t_idx >= s_idx, dw_ref[h], 0.0)

    full = lambda shape: BS(shape, lambda i: (0,) * len(shape))
    sq = (SG_HEADS, SG_CHUNK, SG_CHUNK)
    return _call(body, name="even_mixer_bwd", grid=(n_tiles,),
                 in_specs=[BS((tm, EVEN_IN), lambda i: (i, 0)), BS((tm, D_MODEL), lambda i: (i, 0))]
                 + _even_halo_specs(tm, n_tiles, (3, 4), after=False)
                 + _even_halo_specs(tm, n_tiles, (1,), after=True) + _even_halo_specs(tm, n_tiles, (2,), after=True)
                 + [full((1, SG_WIDTH)), full(sq), full(sq), full((SUBLANES, SC_WIDTH))],
                 out_specs=[BS((tm, EVEN_IN), lambda i: (i, 0)), full(sq), full(sq), full((1, SG_WIDTH)), full((SUBLANES, SC_WIDTH))],
                 out_shape=[_sds((T, EVEN_IN), BF16), _sds(sq, F32), _sds(sq, F32), _sds((1, SG_WIDTH), F32), _sds((SUBLANES, SC_WIDTH), F32)])(
        proj, dmix, proj, proj, dmix, proj, ln_g, w_tril, b_lanes, conv_w)


def _pool_select(vals):
    lane = lax.broadcasted_iota(jnp.int32, vals[0].shape, 1)
    out = vals[-1]
    for g in range(len(vals) - 2, -1, -1):
        out = jnp.where(lane < POOL_DIM * (g + 1), vals[g], out)
    return out


def _pool_counts(pos1):
    lane = lax.broadcasted_iota(jnp.int32, (pos1.shape[0], POOL_WIDTH), 1)
    win = _pool_select([jnp.full(lane.shape, float(w), F32) for w in POOL_WINDOWS])
    return jnp.minimum(pos1, win)


def _pool_means(zz, counts):
    s2 = zz + _shift_down(zz, 1)
    s4 = s2 + _shift_down(s2, 2)
    s8 = s4 + _shift_down(s4, 4)
    s16 = s8 + _shift_down(s8, 8)
    return _pool_select([s2, s4, s8, s16])[POOL_HALO:] / counts


def _pool_halo_spec(tm, n_tiles, after):
    rows = tm // POOL_HALO
    if after:
        return BS((POOL_HALO, POOL_WIDTH), lambda i: (jnp.minimum((i + 1) * rows, n_tiles * rows - 1), 0))
    return BS((POOL_HALO, POOL_WIDTH), lambda i: (jnp.maximum(i * rows - 1, 0), 0))


def _pool_fwd(proj, w_diag, scale, seq, tm):
    T = proj.shape[0]
    tiles_per_seq = seq // tm

    def body(z_ref, zh_ref, w_ref, s_ref, o_ref):
        t = pl.program_id(0) % tiles_per_seq
        z = z_ref[...]
        zz = jnp.concatenate([jnp.where(t == 0, 0.0, zh_ref[...]), z], axis=0)
        pos1 = (lax.broadcasted_iota(jnp.int32, (tm, 1), 0) + (t * tm + 1)).astype(F32)
        pooled = _pool_means(zz, _pool_counts(pos1)) - z
        o_ref[...] = (_dot(pooled, w_ref[...]) * s_ref[...]).astype(BF16)

    full = lambda shape: BS(shape, lambda i: (0,) * len(shape))
    return _call(body, name="pool_fwd", grid=(T // tm,),
                 in_specs=[BS((tm, POOL_WIDTH), lambda i: (i, 0)), _pool_halo_spec(tm, T // tm, False),
                           full((POOL_WIDTH, POOL_WIDTH)), full((1, POOL_WIDTH))],
                 out_specs=BS((tm, POOL_WIDTH), lambda i: (i, 0)), out_shape=_sds((T, D_MODEL), BF16))(proj, proj, w_diag, scale)


def _pool_bwd(proj, dmix, w_diag, scale, seq, tm):
    T = proj.shape[0]
    n_tiles, tiles_per_seq = T // tm, seq // tm

    def body(z_ref, zh_ref, do_ref, don_ref, w_ref, s_ref, dz_ref, dw_ref, ds_ref):
        i = pl.program_id(0)
        t = i % tiles_per_seq

        @pl.when(i == 0)
        def _():
            dw_ref[...] = jnp.zeros_like(dw_ref)
            ds_ref[...] = jnp.zeros_like(ds_ref)

        z = z_ref[...]
        zz = jnp.concatenate([jnp.where(t == 0, 0.0, zh_ref[...]), z], axis=0)
        pos1 = (lax.broadcasted_iota(jnp.int32, (tm, 1), 0) + (t * tm + 1)).astype(F32)
        counts = _pool_counts(pos1)
        pooled = _pool_means(zz, counts) - z
        dout = do_ref[...].astype(F32)
        ds_ref[...] += jnp.sum(dout * _dot(pooled, w_ref[...]), axis=0, keepdims=True)
        dlin = dout * s_ref[...]
        dw_ref[...] += _dot(pooled, dlin, "tn")
        dpooled = _dot(dlin, w_ref[...], "nt")
        dpooled_n = _dot(don_ref[...].astype(F32) * s_ref[...], w_ref[...], "nt")
        pos1_n = (lax.broadcasted_iota(jnp.int32, (POOL_HALO, 1), 0) + ((t + 1) * tm + 1)).astype(F32)
        dmean_n = jnp.where(t == tiles_per_seq - 1, 0.0, dpooled_n / _pool_counts(pos1_n))
        dd = jnp.concatenate([dpooled / counts, dmean_n], axis=0)
        r2 = dd + _shift_up(dd, 1)
        r4 = r2 + _shift_up(r2, 2)
        r8 = r4 + _shift_up(r4, 4)
        r16 = r8 + _shift_up(r8, 8)
        dz_ref[...] = (_pool_select([r2, r4, r8, r16])[:tm] - dpooled).astype(BF16)

    full = lambda shape: BS(shape, lambda i: (0,) * len(shape))
    return _call(body, name="pool_bwd", grid=(n_tiles,),
                 in_specs=[BS((tm, POOL_WIDTH), lambda i: (i, 0)), _pool_halo_spec(tm, n_tiles, False),
                           BS((tm, POOL_WIDTH), lambda i: (i, 0)), _pool_halo_spec(tm, n_tiles, True),
                           full((POOL_WIDTH, POOL_WIDTH)), full((1, POOL_WIDTH))],
                 out_specs=[BS((tm, POOL_WIDTH), lambda i: (i, 0)), full((POOL_WIDTH, POOL_WIDTH)), full((1, POOL_WIDTH))],
                 out_shape=[_sds((T, POOL_WIDTH), BF16), _sds((POOL_WIDTH, POOL_WIDTH), F32), _sds((1, POOL_WIDTH), F32)])(
        proj, proj, dmix, dmix, w_diag, scale)


def _rope_partner(r):
    lane = lax.broadcasted_iota(jnp.int32, r.shape, 1)
    return jnp.where(lane < QK_ROPE // 2, pltpu.roll(r, LANES - QK_ROPE // 2, 1), pltpu.roll(r, QK_ROPE // 2, 1))


def _rope(x, cos, sin_signed):
    r = x[:, QK_NOPE:]
    return jnp.concatenate([x[:, :QK_NOPE], r * cos + _rope_partner(r) * sin_signed], axis=1)


def _rope_transposed(dx, cos, sin_signed):
    dr = dx[:, QK_NOPE:]
    return jnp.concatenate([dx[:, :QK_NOPE], dr * cos + _rope_partner(dr * sin_signed)], axis=1)


def _head_norm(x):
    r = lax.rsqrt(jnp.sum(x * x, axis=-1, keepdims=True) * (1.0 / QK_DIM) + EPS)
    return x * r, r


def _head_norm_bwd(dy, xhat, r, gain):
    dxhat = dy * gain
    return r * (dxhat - xhat * (jnp.sum(dxhat * xhat, axis=-1, keepdims=True) * (1.0 / QK_DIM)))


def _latents(p_ref, qag_ref, kvag_ref):
    ql = p_ref[:, POOL_WIDTH:POOL_WIDTH + Q_LORA]
    kvl = p_ref[:, POOL_WIDTH + Q_LORA:POOL_WIDTH + Q_LORA + KV_LORA]
    rq = lax.rsqrt(jnp.mean(ql * ql, axis=-1, keepdims=True) + EPS)
    rkv = lax.rsqrt(jnp.mean(kvl * kvl, axis=-1, keepdims=True) + EPS)
    return ql * rq, rq, kvl * rkv, rkv


def _mla_specs(tm):
    full = lambda shape: BS(shape, lambda i, h: (0,) * len(shape))
    return [BS((tm, ODD_IN_PAD), lambda i, h: (i, 0)), BS((tm, LANES), lambda i, h: (i, 0)), BS((tm, LANES), lambda i, h: (i, 0)),
            full((1, Q_LORA)), full((1, KV_LORA)), BS((None, Q_LORA, QK_PAD), lambda i, h: (h, 0, 0)),
            BS((None, KV_LORA, QK_PAD), lambda i, h: (h, 0, 0)), full((1, QK_PAD)), full((1, QK_PAD))]


def _mla_qkv_fwd(proj, cos, sin_signed, qa_g, kva_g, q_b, kv_b, q_g, k_g, tm):
    T = proj.shape[0]

    def body(p_ref, cos_ref, sin_ref, qag_ref, kvag_ref, qb_ref, kvb_ref, qg_ref, kg_ref, q_ref, k_ref, v_ref, qn_s, kvn_s):
        @pl.when(pl.program_id(1) == 0)
        def _():
            qhat, _, kvhat, _ = _latents(p_ref, qag_ref, kvag_ref)
            qn_s[...] = (qhat * qag_ref[...]).astype(BF16)
            kvn_s[...] = (kvhat * kvag_ref[...]).astype(BF16)

        cos, sin = cos_ref[...], sin_ref[...]
        qhat, _ = _head_norm(_dot(qn_s[...], qb_ref[...]))
        q_ref[...] = _rope(qhat * qg_ref[...], cos, sin).astype(BF16)
        kv = _dot(kvn_s[...], kvb_ref[...])
        khat, _ = _head_norm(jnp.concatenate([kv[:, :QK_NOPE], p_ref[:, ODD_IN_PAD - LANES:]], axis=1))
        k_ref[...] = _rope(khat * kg_ref[...], cos, sin).astype(BF16)
        v_ref[...] = kv[:, QK_NOPE:].astype(BF16)

    qk_spec = BS((None, tm, QK_PAD), lambda i, h: (h, i, 0))
    return _call(body, name="mla_qkv_fwd", grid=(T // tm, HEADS), in_specs=_mla_specs(tm),
                 out_specs=[qk_spec, qk_spec, BS((None, tm, V_DIM), lambda i, h: (h, i, 0))],
                 out_shape=[_sds((HEADS, T, QK_PAD), BF16), _sds((HEADS, T, QK_PAD), BF16), _sds((HEADS, T, V_DIM), BF16)],
                 scratch=[pltpu.VMEM((tm, Q_LORA), BF16), pltpu.VMEM((tm, KV_LORA), BF16)])(
        proj, cos, sin_signed, qa_g, kva_g, q_b, kv_b, q_g, k_g)


def _mla_qkv_bwd(proj, cos, sin_signed, qa_g, kva_g, q_b, kv_b, q_g, k_g, dq, dk, dv, dz_pool, tm):
    T = proj.shape[0]
    n_tiles = T // tm
    chain_rows = min(_PASS_ROWS, tm)

    def body(p_ref, cos_ref, sin_ref, qag_ref, kvag_ref, qb_ref, kvb_ref, qg_ref, kg_ref, dq_ref, dk_ref, dv_ref, dzp_ref,
             dp_ref, dqb_ref, dkvb_ref, dqg_ref, dkg_ref, dqag_ref, dkvag_ref, qn_s, kvn_s, dqn_s, dkvn_s, dkr_s,
             qh_s, kv_s, dqh_s, dkv_s):
        i, h = pl.program_id(0), pl.program_id(1)

        @pl.when((i == 0) & (h == 0))
        def _():
            for ref in (dqb_ref, dkvb_ref, dqg_ref, dkg_ref, dqag_ref, dkvag_ref):
                ref[...] = jnp.zeros_like(ref)

        @pl.when(h == 0)
        def _():
            qhat, _, kvhat, _ = _latents(p_ref, qag_ref, kvag_ref)
            qn_s[...] = (qhat * qag_ref[...]).astype(BF16)
            kvn_s[...] = (kvhat * kvag_ref[...]).astype(BF16)
            dqn_s[...] = jnp.zeros_like(dqn_s)
            dkvn_s[...] = jnp.zeros_like(dkvn_s)
            dkr_s[...] = jnp.zeros_like(dkr_s)

        qh_s[...] = _dot(qn_s[...], qb_ref[...])
        kv_s[...] = _dot(kvn_s[...], kvb_ref[...])
        qg, kg = qg_ref[...], kg_ref[...]

        def chunk(c, gains):
            dqg, dkg = gains
            rows = slice(c * chain_rows, (c + 1) * chain_rows)
            cos, sin = cos_ref[rows, :], sin_ref[rows, :]
            qhat, rq = _head_norm(qh_s[rows, :])
            dqn_head = _rope_transposed(dq_ref[rows, :], cos, sin)
            dqh_s[rows, :] = _head_norm_bwd(dqn_head, qhat, rq, qg).astype(BF16)
            kv = kv_s[rows, :]
            khat, rk = _head_norm(jnp.concatenate([kv[:, :QK_NOPE], p_ref[rows, ODD_IN_PAD - LANES:]], axis=1))
            dkn_head = _rope_transposed(dk_ref[rows, :], cos, sin)
            dkf = _head_norm_bwd(dkn_head, khat, rk, kg)
            dkr_s[rows, :] += dkf[:, QK_NOPE:]
            dkv_s[rows, :] = jnp.concatenate([dkf[:, :QK_NOPE], dv_ref[rows, :]], axis=1).astype(BF16)
            return dqg + dqn_head * qhat, dkg + dkn_head * khat

        dqg = dkg = jnp.zeros((chain_rows, QK_PAD), F32)
        for c in range(tm // chain_rows):
            dqg, dkg = chunk(c, (dqg, dkg))
        dqg_ref[...] += jnp.sum(dqg, axis=0, keepdims=True)
        dkg_ref[...] += jnp.sum(dkg, axis=0, keepdims=True)
        dqb_ref[h] += _dot(qn_s[...], dqh_s[...], "tn")
        dqn_s[...] += _dot(dqh_s[...], qb_ref[...], "nt")
        dkvb_ref[h] += _dot(kvn_s[...], dkv_s[...], "tn")
        dkvn_s[...] += _dot(dkv_s[...], kvb_ref[...], "nt")

        @pl.when(h == HEADS - 1)
        def _():
            qhat_l, rql, kvhat_l, rkvl = _latents(p_ref, qag_ref, kvag_ref)
            dqn, dkvn = dqn_s[...], dkvn_s[...]
            dqag_ref[...] += jnp.sum(dqn * qhat_l, axis=0, keepdims=True)
            dkvag_ref[...] += jnp.sum(dkvn * kvhat_l, axis=0, keepdims=True)
            dqx, dkvx = dqn * qag_ref[...], dkvn * kvag_ref[...]
            dp_ref[:, :POOL_WIDTH] = dzp_ref[...]
            dp_ref[:, POOL_WIDTH:POOL_WIDTH + Q_LORA] = (
                rql * (dqx - qhat_l * jnp.mean(dqx * qhat_l, axis=-1, keepdims=True))).astype(BF16)
            dp_ref[:, POOL_WIDTH + Q_LORA:ODD_IN_PAD - LANES] = (
                rkvl * (dkvx - kvhat_l * jnp.mean(dkvx * kvhat_l, axis=-1, keepdims=True))).astype(BF16)
            dp_ref[:, ODD_IN_PAD - LANES:] = dkr_s[:, :QK_ROPE].astype(BF16)

    full = lambda shape: BS(shape, lambda i, h: (0,) * len(shape))
    qk_spec = BS((None, tm, QK_PAD), lambda i, h: (h, i, 0))
    return _call(body, name="mla_qkv_bwd", grid=(n_tiles, HEADS),
                 in_specs=_mla_specs(tm) + [qk_spec, qk_spec, BS((None, tm, V_DIM), lambda i, h: (h, i, 0)),
                                            BS((tm, POOL_WIDTH), lambda i, h: (i, 0))],
                 out_specs=[BS((tm, ODD_IN), lambda i, h: (i, 0)), full((HEADS, Q_LORA, QK_PAD)), full((HEADS, KV_LORA, QK_PAD)),
                            full((1, QK_PAD)), full((1, QK_PAD)), full((1, Q_LORA)), full((1, KV_LORA))],
                 out_shape=[_sds((T, ODD_IN), BF16),_sds((HEADS, Q_LORA, QK_PAD), F32), _sds((HEADS, KV_LORA, QK_PAD), F32),
                            _sds((1, QK_PAD), F32), _sds((1, QK_PAD), F32), _sds((1, Q_LORA), F32), _sds((1, KV_LORA), F32)],
                 scratch=[pltpu.VMEM((tm, Q_LORA), BF16), pltpu.VMEM((tm, KV_LORA), BF16), pltpu.VMEM((tm, Q_LORA), F32),
                          pltpu.VMEM((tm, KV_LORA), F32), pltpu.VMEM((tm, LANES), F32), pltpu.VMEM((tm, QK_PAD), F32),
                          pltpu.VMEM((tm, QK_PAD), F32), pltpu.VMEM((tm, QK_PAD), BF16), pltpu.VMEM((tm, QK_PAD), BF16)])(
        proj, cos, sin_signed, qa_g, kva_g, q_b, kv_b, q_g, k_g, dq, dk, dv, dz_pool)


_SCALE_LOG2E = ATTN_SCALE * math.log2(math.e)


def _attn_tile(seq):
    return 512 if seq % 512 == 0 else seq


def _causal_mask(s):
    row = lax.broadcasted_iota(jnp.int32, s.shape, 0)
    col = lax.broadcasted_iota(jnp.int32, s.shape, 1)
    return jnp.where(row >= col, s, NEG_INF)


def _tile(i, t):
    return slice(i * t, (i + 1) * t)


def _flash_fwd(q, k, v, mix, batch, seq):
    t = _attn_tile(seq)
    nq = seq // t

    def body(q_ref, k_ref, v_ref, _, o_ref, lse_ref):
        for qi in range(nq):
            rows, before = _tile(qi, t), slice(0, qi * t)
            qv = q_ref[rows, :]
            s_diag = _causal_mask(_dot(qv, k_ref[rows, :], "nt"))
            m = jnp.max(s_diag, axis=-1, keepdims=True)
            if qi:
                s_before = _dot(qv, k_ref[before, :], "nt")
                m = jnp.maximum(m, jnp.max(s_before, axis=-1, keepdims=True))
            p = jnp.exp2((s_diag - m) * _SCALE_LOG2E)
            l = jnp.sum(p, axis=-1, keepdims=True)
            acc = _dot(p, v_ref[rows, :])
            if qi:
                p = jnp.exp2((s_before - m) * _SCALE_LOG2E)
                l = l + jnp.sum(p, axis=-1, keepdims=True)
                acc = acc + _dot(p, v_ref[before, :])
            o_ref[rows, :] = (acc / l).astype(BF16)
            lse_ref[rows, :] = jnp.broadcast_to(m * ATTN_SCALE + jnp.log(l), (t, LANES))

    T = batch * seq
    whole = lambda w: BS((None, seq, w), lambda b, h: (h, b, 0))
    return _call(body, name="flash_fwd", grid=(batch, HEADS),
                 in_specs=[whole(QK_PAD), whole(QK_PAD), whole(V_DIM), pl.BlockSpec(memory_space=pl.ANY)],
                 out_specs=[BS((seq, V_DIM), lambda b, h: (b, POOL_WIDTH // V_DIM + h)), whole(LANES)],
                 out_shape=[_sds((T, D_MODEL), BF16), _sds((HEADS, T, LANES), F32)],
                 aliases={3: 0})(q, k, v, mix)


def _flash_bwd(q, k, v, dmix, mix, lse, batch, seq):
    t = _attn_tile(seq)
    nq = seq // t

    def body(q_ref, k_ref, v_ref, do_ref, o_ref, lse_ref, dq_ref, dk_ref, dv_ref):
        for qi in range(nq):
            rows, before = _tile(qi, t), slice(0, qi * t)
            qv, do = q_ref[rows, :], do_ref[rows, :]
            lse2 = lse_ref[rows, 0:1] * math.log2(math.e)
            delta = jnp.sum(do.astype(F32) * o_ref[rows, :].astype(F32), axis=-1, keepdims=True)

            def block(keys, masked):
                kk = k_ref[keys, :]
                s = _dot(qv, kk, "nt")
                p = jnp.exp2((_causal_mask(s) if masked else s) * _SCALE_LOG2E - lse2)
                ds = p * (_dot(do, v_ref[keys, :], "nt") - delta)
                return _dot(p, do, "tn"), _dot(ds, qv, "tn") * ATTN_SCALE, _dot(ds, kk) * ATTN_SCALE

            dv_ref[rows, :], dk_ref[rows, :], dq = block(rows, True)
            if qi:
                dv, dk, dq_before = block(before, False)
                dv_ref[before, :] += dv
                dk_ref[before, :] += dk
                dq = dq + dq_before
            dq_ref[rows, :] = dq

    T = batch * seq
    whole = lambda w: BS((None, seq, w), lambda b, h: (h, b, 0))
    head_cols = BS((seq, V_DIM), lambda b, h: (b, POOL_WIDTH // V_DIM + h))
    return _call(body, name="flash_bwd", grid=(batch, HEADS),
                 in_specs=[whole(QK_PAD), whole(QK_PAD), whole(V_DIM), head_cols, head_cols, whole(LANES)],
                 out_specs=[whole(QK_PAD), whole(QK_PAD), whole(V_DIM)],
                 out_shape=[_sds((HEADS, T, QK_PAD), F32), _sds((HEADS, T, QK_PAD), F32), _sds((HEADS, T, V_DIM), F32)])(
        q, k, v, dmix, mix, lse)


def _adamw_math(w, g, m, v):
    m = ADAM_B1 * m + (1.0 - ADAM_B1) * g
    v = ADAM_B2 * v + (1.0 - ADAM_B2) * (g * g)
    m_hat = m / (1.0 - ADAM_B1 ** ADAM_STEP)
    v_hat = v / (1.0 - ADAM_B2 ** ADAM_STEP)
    return -ADAM_LR * (m_hat / (jnp.sqrt(v_hat) + ADAM_EPS) + ADAM_WD * w), m, v


def _adamw(name, w, g, m, v, l=0, prev=()):
    L, R, C = w.shape
    tr = 256 if R % 256 == 0 else R

    def body(w_ref, g_ref, m_ref, v_ref, *rest):
        go_ref, d_ref, mo_ref, vo_ref, token = rest[-5:]
        gv = g_ref[...]
        d_ref[...], mo_ref[...], vo_ref[...] = _adamw_math(w_ref[...], gv, m_ref[...], v_ref[...])
        go_ref[...] = gv
        token[...] = jnp.zeros_like(token)

    layer = BS((None, tr, C), lambda i: (l, i, 0))
    return _call(body, name=f"{name}_{l}", grid=(R // tr,),
                 in_specs=[layer, BS((tr, C), lambda i: (i, 0)), layer, layer] + [pl.BlockSpec(memory_space=pl.ANY)] * len(prev),
                 out_specs=[layer] * 4 + [BS((SUBLANES, LANES), lambda i: (0, 0))],
                 out_shape=[_sds((L, R, C), F32)] * 4 + [_sds((SUBLANES, LANES), F32)],
                 aliases={4 + n: n for n in range(len(prev))})(w, g, m, v, *prev)


def _place():
    x, y, c = lax.axis_index("x"), lax.axis_index("y"), lax.axis_index("c")
    other_chips = [(1 - x, y), (x, 1 - y), (1 - x, 1 - y)]
    return x, y, c, other_chips


def _remote(src, dst, send_sem, recv_sem, dev):
    return pltpu.make_async_remote_copy(src_ref=src, dst_ref=dst, send_sem=send_sem, recv_sem=recv_sem,
                                        device_id=dev, device_id_type=MESH)


def _prefetch_call(body, *, name, grid, in_specs, out_specs, out_shape):
    grid_spec = pltpu.PrefetchScalarGridSpec(num_scalar_prefetch=1, grid=grid, in_specs=in_specs, out_specs=out_specs)
    params = pltpu.CompilerParams(vmem_limit_bytes=VMEM_LIMIT_V7X, dimension_semantics=("arbitrary",) * len(grid))
    return pl.pallas_call(body, name=name, grid_spec=grid_spec, out_shape=out_shape, compiler_params=params)


def _row_tile(rows):
    return 256 if rows % 256 == 0 else rows


def _cast_place(name, w, layer, chip, after=()):
    _, _, rows, C = w.shape
    tr = _row_tile(rows)

    def body(chip_ref, w_ref, *rest):
        rest[-1][...] = w_ref[...].astype(BF16)

    return _prefetch_call(body, name=name, grid=(2, rows // tr),
                          in_specs=[BS((None, None, tr, C), lambda h, i, chip_ref: (layer, h, i, 0))]
                          + [pl.BlockSpec(memory_space=pl.ANY)] * len(after),
                          out_specs=BS((None, None, tr, C), lambda h, i, chip_ref: (chip_ref[0], h, i, 0)),
                          out_shape=pltpu.HBM((N_CHIPS, 2, rows, C), BF16))(chip, w, *after)


SEM = pl.BlockSpec(memory_space=pltpu.SEMAPHORE)


def _split_copy_call(body, *, name, in_specs, out_specs, out_shape, aliases):
    return pl.pallas_call(body, name=name, in_specs=in_specs, out_specs=out_specs, out_shape=out_shape,
                          input_output_aliases=aliases,
                          compiler_params=pltpu.CompilerParams(has_side_effects=pltpu.SideEffectType.DATAFLOW_SIDE_EFFECTING))


def _hbm(arrays):
    return [pltpu.with_memory_space_constraint(a, pltpu.HBM) for a in arrays]


def _gather_send(name, gs, groups, after):
    n = len(gs)

    def body(*refs):
        g, sems, token = refs[:n], refs[n + len(after):n + len(after) + 2 * len(groups)], refs[-1]
        x, y, c, chips = _place()
        me = 2 * x + y
        for gi, members in enumerate(groups):
            for a, i in enumerate(members):
                for k, (px, py) in enumerate(chips):
                    _remote(g[i].at[me, c], g[i].at[me, c], sems[2 * gi].at[3 * a + k], sems[2 * gi + 1].at[3 * a + k],
                            (px, py, c)).start()
        token[...] = jnp.zeros_like(token)

    sem_shapes = [pltpu.SemaphoreType.DMA((3 * len(members),)) for members in groups for _ in range(2)]
    out = _split_copy_call(body, name=name, in_specs=[HBM] * n + [pl.BlockSpec(memory_space=pl.ANY)] * len(after),
                           out_specs=[SEM] * len(sem_shapes) + [HBM] * n + [VMEM],
                           out_shape=sem_shapes + [pltpu.HBM(a.shape, a.dtype) for a in gs] + [_sds((SUBLANES, LANES), F32)],
                           aliases={i: len(sem_shapes) + i for i in range(n)})(*_hbm(gs), *after)
    return out[:len(sem_shapes)], out[len(sem_shapes):-1], out[-1]


def _gather_wait(name, gs, send_sems, recv_sems, after):
    n = len(gs)

    def body(*refs):
        g, ssem, rsem = refs[:n], refs[n], refs[n + 1]
        x, y, c, chips = _place()
        me = 2 * x + y
        for a in range(n):
            for k, (px, py) in enumerate(chips):
                landed = g[a].at[2 * px + py, c]
                cp = _remote(g[a].at[me, c], landed, ssem.at[3 * a + k], rsem.at[3 * a + k], (px, py, c))
                cp.wait_recv()
                cp.wait_send()

    return _split_copy_call(body, name=name, in_specs=[HBM] * n + [SEM, SEM] + [pl.BlockSpec(memory_space=pl.ANY)] * len(after),
                            out_specs=[HBM] * n, out_shape=[pltpu.HBM(a.shape, a.dtype) for a in gs],
                            aliases={i: i for i in range(n)})(*gs, send_sems, recv_sems, *after)


def _gather_pass(name, gs):
    n = len(gs)

    def body(*refs):
        g, send_sems, recv_sems = refs[n:2 * n], refs[-2], refs[-1]
        x, y, c, chips = _place()
        sibling = (x, y, 1 - c)
        passed = [_remote(g[i].at[2 * px + py, c], g[i].at[2 * px + py, c], send_sems.at[3 * i + k], recv_sems.at[3 * i + k], sibling)
                  for i in range(n) for k, (px, py) in enumerate(chips)]
        for cp in passed:
            cp.start()
        for i in range(n):
            for k, (px, py) in enumerate(chips):
                theirs = g[i].at[2 * px + py, 1 - c]
                _remote(theirs, theirs, send_sems.at[3 * i + k], recv_sems.at[3 * i + k], sibling).wait_recv()
        for cp in passed:
            cp.wait_send()

    return _call(body, name=name, in_specs=[HBM] * n, out_specs=[HBM] * n, out_shape=[_sds(a.shape, a.dtype) for a in gs],
                 aliases={i: i for i in range(n)},
                 scratch=[pltpu.SemaphoreType.DMA((3 * n,)), pltpu.SemaphoreType.DMA((3 * n,))])(*gs)


def _scatter_send(name, ps):
    n = len(ps)

    def body(*refs):
        p, r, ssem, rsem, token = refs[:n], refs[n:2 * n], refs[2 * n], refs[2 * n + 1], refs[-1]
        x, y, c, chips = _place()
        for i in range(n):
            for k, (px, py) in enumerate(chips):
                _remote(p[i].at[2 * px + py], r[i].at[k], ssem.at[3 * i + k], rsem.at[3 * i + k], (px, py, c)).start()
        token[...] = jnp.zeros_like(token)

    lands = [lax.empty((N_CHIPS - 1,) + a.shape[1:], a.dtype) for a in ps]
    sem = pltpu.SemaphoreType.DMA((3 * n,))
    out = _split_copy_call(body, name=name, in_specs=[HBM] * (2 * n), out_specs=[SEM, SEM] + [HBM] * (2 * n) + [VMEM],
                           out_shape=[sem, sem] + [pltpu.HBM(a.shape, a.dtype) for a in list(ps) + lands] + [_sds((SUBLANES, LANES), F32)],
                           aliases={i: 2 + i for i in range(2 * n)})(*_hbm(list(ps) + lands))
    return out[0], out[1], out[2:2 + n], out[2 + n:2 + 2 * n], out[-1]


def _scatter_wait(name, ps, lands, send_sems, recv_sems, after):
    n = len(ps)

    def body(*refs):
        p, r, ssem, rsem = refs[:n], refs[n:2 * n], refs[2 * n], refs[2 * n + 1]
        x, y, c, chips = _place()
        for i in range(n):
            for k, (px, py) in enumerate(chips):
                cp = _remote(p[i].at[2 * px + py], r[i].at[k], ssem.at[3 * i + k], rsem.at[3 * i + k], (px, py, c))
                cp.wait_recv()
                cp.wait_send()

    out = _split_copy_call(body, name=name, in_specs=[HBM] * (2 * n) + [SEM, SEM] + [pl.BlockSpec(memory_space=pl.ANY)] * len(after),
                           out_specs=[HBM] * (2 * n), out_shape=[pltpu.HBM(a.shape, a.dtype) for a in list(ps) + list(lands)],
                           aliases={i: i for i in range(2 * n)})(*ps, *lands, send_sems, recv_sems, *after)
    return out[:n], out[n:]


def _exchange_send(name, gs):
    n = len(gs)

    def body(*refs):
        g, r, ssem, rsem, token = refs[:n], refs[n:2 * n], refs[2 * n], refs[2 * n + 1], refs[-1]
        x, y, c, _ = _place()
        for i in range(n):
            _remote(g[i].at[:, 1 - c], r[i], ssem.at[i], rsem.at[i], (x, y, 1 - c)).start()
        token[...] = jnp.zeros_like(token)

    lands = [lax.empty((a.shape[0],) + a.shape[2:], a.dtype) for a in gs]
    sem = pltpu.SemaphoreType.DMA((n,))
    out = _split_copy_call(body, name=name, in_specs=[HBM] * (2 * n), out_specs=[SEM, SEM] + [HBM] * (2 * n) + [VMEM],
                           out_shape=[sem, sem] + [pltpu.HBM(a.shape, a.dtype) for a in list(gs) + lands] + [_sds((SUBLANES, LANES), F32)],
                           aliases={i: 2 + i for i in range(2 * n)})(*_hbm(list(gs) + lands))
    return out[0], out[1], out[2:2 + n], out[2 + n:2 + 2 * n], out[-1]


def _exchange_wait(name, gs, lands, send_sems, recv_sems, after):
    n = len(gs)

    def body(*refs):
        g, r, ssem, rsem = refs[:n], refs[n:2 * n], refs[2 * n], refs[2 * n + 1]
        x, y, c, _ = _place()
        for i in range(n):
            cp = _remote(g[i].at[:, 1 - c], r[i], ssem.at[i], rsem.at[i], (x, y, 1 - c))
            cp.wait_recv()
            cp.wait_send()

    out = _split_copy_call(body, name=name, in_specs=[HBM] * (2 * n) + [SEM, SEM] + [pl.BlockSpec(memory_space=pl.ANY)] * len(after),
                           out_specs=[HBM] * (2 * n), out_shape=[pltpu.HBM(a.shape, a.dtype) for a in list(gs) + list(lands)],
                           aliases={i: i for i in range(2 * n)})(*gs, *lands, send_sems, recv_sems, *after)
    return out[:n], out[n:]


def _sibling_share(name, fs, after=()):
    n = len(fs)

    def body(*refs):
        f, send_sems, recv_sems = refs[n:2 * n], refs[-2], refs[-1]
        x, y, c, _ = _place()
        sends = [_remote(f[i].at[c], f[i].at[c], send_sems.at[i], recv_sems.at[i], (x, y, 1 - c)) for i in range(n)]
        for cp in sends:
            cp.start()
        for i in range(n):
            theirs = f[i].at[1 - c]
            _remote(theirs, theirs, send_sems.at[i], recv_sems.at[i], (x, y, 1 - c)).wait_recv()
        for cp in sends:
            cp.wait_send()

    return _call(body, name=name, in_specs=[HBM] * n, out_specs=[HBM] * n,
                 out_shape=[_sds(a.shape, a.dtype) for a in fs], aliases={i: i for i in range(n)}, after=after,
                 scratch=[pltpu.SemaphoreType.DMA((n,)), pltpu.SemaphoreType.DMA((n,))])(*fs)


def _all_reduce_small(name, v):
    rows = v.shape[0] // 2
    halves = (2, rows, LANES)

    def body(v_ref, o_ref, from_sibling, chip_sums, send_sems, recv_sems):
        x, y, c, chips = _place()
        me, sibling = 2 * x + y, (x, y, 1 - c)
        swap = _remote(v_ref.at[1 - c], from_sibling, send_sems.at[0], recv_sems.at[0], sibling)
        swap.start()
        swap.wait()
        chip_sums[me] = v_ref[c] + from_sibling[...]
        sends = [_remote(chip_sums.at[me], chip_sums.at[me], send_sems.at[1 + k], recv_sems.at[1 + k], (px, py, c))
                 for k, (px, py) in enumerate(chips)]
        for cp in sends:
            cp.start()
        for k, (px, py) in enumerate(chips):
            theirs = chip_sums.at[2 * px + py]
            _remote(theirs, theirs, send_sems.at[1 + k], recv_sems.at[1 + k], (px, py, c)).wait_recv()
        for cp in sends:
            cp.wait_send()
        acc = chip_sums[0]
        for j in range(1, N_CHIPS):
            acc = acc + chip_sums[j]
        o_ref[c] = acc
        share = _remote(o_ref.at[c], o_ref.at[c], send_sems.at[4], recv_sems.at[4], sibling)
        share.start()
        share.wait_send()
        _remote(o_ref.at[1 - c], o_ref.at[1 - c], send_sems.at[4], recv_sems.at[4], sibling).wait_recv()

    return _call(body, name=name, in_specs=[VMEM], out_specs=VMEM, out_shape=_sds(halves, F32),
                 scratch=[pltpu.VMEM((rows, LANES), F32), pltpu.VMEM((N_CHIPS, rows, LANES), F32),
                          pltpu.SemaphoreType.DMA((5,)), pltpu.SemaphoreType.DMA((5,))])(v.reshape(halves)).reshape(v.shape)


def _add_halves(name, g, r, c):
    _, _, rows, C = g.shape
    tr = _row_tile(rows)

    def body(c_ref, g_ref, r_ref, o_ref):
        o_ref[...] = (g_ref[...].astype(F32) + r_ref[...].astype(F32)).astype(BF16)

    spec = BS((None, tr, C), lambda j, i, c_ref: (j, i, 0))
    return _prefetch_call(body, name=name, grid=(N_CHIPS, rows // tr),
                          in_specs=[BS((None, None, tr, C), lambda j, i, c_ref: (j, c_ref[0], i, 0)), spec], out_specs=spec,
                          out_shape=pltpu.HBM((N_CHIPS, rows, C), BF16))(c, g, r)


def _sum_partials(name, p, r, chip_c):
    _, rows, C = p.shape
    tr = _row_tile(rows)

    def body(s_ref, p_ref, r_ref, o_ref):
        acc = p_ref[...].astype(F32)
        for k in range(N_CHIPS - 1):
            acc = acc + r_ref[k].astype(F32)
        o_ref[...] = acc

    return _prefetch_call(body, name=name, grid=(rows // tr,),
                          in_specs=[BS((None, tr, C), lambda i, s: (s[0], i, 0)), BS((N_CHIPS - 1, tr, C), lambda i, s: (0, i, 0))],
                          out_specs=BS((None, tr, C), lambda i, s: (s[1], i, 0)), out_shape=pltpu.HBM((2, rows, C), F32))(chip_c, p, r)


_SHARDED = ("even_w_in", "even_w_out", "odd_w_in", "q_b", "kv_b", "odd_w_out", "ffn_w_gate", "ffn_w_up", "ffn_w_down")
_REPLICATED = ("mix_norm", "ffn_norm", "sg_ln_g", "sg_w_s", "sg_b_s", "pool_w", "q_norm", "k_norm")
_SMALL_SHARDED = ("sc_conv_w", "pool_scale", "q_a_norm", "kv_a_norm")
_WEIGHTS = ("mix_norm", "ffn_norm", "even_w_in", "sg_ln_g", "sg_w_s", "sg_b_s", "sc_conv_w", "even_w_out", "odd_w_in", "pool_w",
            "pool_scale", "q_a_norm", "q_b", "kv_a_norm", "kv_b", "q_norm", "k_norm", "odd_w_out", "ffn_w_gate", "ffn_w_up",
            "ffn_w_down")


def _pad_rows(flat, width, align):
    n = flat.shape[0]
    rows = -(-n // (width * align)) * align
    return jnp.pad(flat, (0, rows * width - n)).reshape(rows, width)


_GROUPS = {"even": ("even_w_in", "even_w_out"),
           "ffn0": ("ffn_w_gate0", "ffn_w_up0", "ffn_w_down0"),
           "odd": ("odd_w_in", "q_b", "kv_b", "odd_w_out"),
           "ffn1": ("ffn_w_gate1", "ffn_w_up1", "ffn_w_down1")}


def _place_shards(shards, names, chip, after):
    placed = []
    for n in names:
        weight, layer = (n[:-1], int(n[-1])) if n[-1].isdigit() else (n, 0)
        a = shards[weight]
        placed.append(_cast_place(f"place_{n}", a.reshape(a.shape[0], 2, a.shape[1] // 2, a.shape[2]), layer, chip, after))
    return placed


def _whole_weights(gathered):
    out = {n: a.reshape(N_CHIPS, -1, a.shape[-1]) for n, a in gathered.items()}
    for n in ("q_b", "kv_b"):
        if n in out:
            out[n] = out[n].transpose(1, 0, 2).reshape(out[n].shape[1], -1)
    for n in ("even_w_out", "odd_w_in", "odd_w_out"):
        if n in out:
            out[n] = out[n].reshape(-1, out[n].shape[-1])
    return out


def _forward_backward(x, positions, target, small, fetch, emit, advance):
    batch, seq, _ = x.shape
    T = batch * seq
    tm = _token_tile(seq)
    x0 = x.reshape(T, D_MODEL)

    inv_freq = ROPE_THETA ** (-jnp.arange(0, QK_ROPE, 2, dtype=F32) / QK_ROPE)
    ang = (positions.astype(F32)[..., None] * inv_freq).reshape(T, QK_ROPE // 2)
    cos, sin = jnp.cos(ang), jnp.sin(ang)
    pad = jnp.zeros((T, LANES - QK_ROPE), F32)
    cos_t = jnp.concatenate([cos, cos, pad], axis=1)
    sin_t = jnp.concatenate([-sin, sin, pad], axis=1)

    tril = jnp.tril(jnp.ones((SG_CHUNK, SG_CHUNK), bool))
    w_tril = jnp.where(tril[None], small["sg_w_s"][0], 0.0).astype(BF16)
    b_lanes = jnp.broadcast_to(small["sg_b_s"][0][:, :, None], (SG_HEADS, SG_CHUNK, SG_DIM))
    conv_w = jnp.pad(small["sc_conv_w"][0], ((0, SUBLANES - CONV_TAPS), (0, 0)))
    ln_g = small["sg_ln_g"]
    pool_diag = jnp.zeros((POOL_WIDTH, POOL_WIDTH), F32)
    for g in range(len(POOL_WINDOWS)):
        pool_diag = pool_diag.at[POOL_DIM * g:POOL_DIM * (g + 1), POOL_DIM * g:POOL_DIM * (g + 1)].set(small["pool_w"][0, g])
    pool_diag = pool_diag.astype(BF16)
    pool_scale = small["pool_scale"]
    q_g = jnp.pad(small["q_norm"], ((0, 0), (0, QK_PAD - QK_DIM)))
    k_g = jnp.pad(small["k_norm"], ((0, 0), (0, QK_PAD - QK_DIM)))
    qa_g, kva_g = small["q_a_norm"], small["kv_a_norm"]
    in_shard = EVEN_IN // N_CHIPS

    def ffn_weights(l, w):
        return w[f"ffn_w_gate{l}"], w[f"ffn_w_up{l}"], w[f"ffn_w_down{l}"]

    W = fetch("even", ())
    w_in_even = W["even_w_in"]
    tb = _big_tile(T)
    proj0, h0 = _even_in(x0, small["mix_norm"][0], w_in_even, _resident_tile(T))
    mix0 = _even_mixer_fwd(proj0, ln_g, w_tril, b_lanes, conv_w, seq, tm)
    w_out_even = W["even_w_out"]
    x1, h1 = _mm("even_out", "nn", mix0, w_out_even, F32, tk=1024, add=x0, fused=_norm_tail(small["ffn_norm"][0], T, tb))
    ffn0 = ffn_weights(0, fetch("ffn0", (x1,)))
    (x2, h2), ffn0_saved = _ffn_fwd(0, x1, h1, *ffn0, lambda tile: _norm_tail(small["mix_norm"][1], T, tile))
    W = fetch("odd", (x2,))
    w_in_odd = jnp.pad(W["odd_w_in"], ((0, 0), (0, ODD_IN_PAD - ODD_IN)))
    q_b = jnp.pad(W["q_b"].reshape(Q_LORA, HEADS, QK_DIM).transpose(1, 0, 2), ((0, 0), (0, 0), (0, QK_PAD - QK_DIM)))
    kv_b = W["kv_b"].reshape(KV_LORA, HEADS, QK_NOPE + V_DIM).transpose(1, 0, 2)
    proj1 = _mm("odd_in", "nn", h2, w_in_odd, F32, tk=1024)
    mix1 = _pool_fwd(proj1, pool_diag, pool_scale, seq, tm)
    q, k, v = _mla_qkv_fwd(proj1, cos_t, sin_t, qa_g, kva_g, q_b, kv_b, q_g, k_g, tm)
    mix1, lse = _flash_fwd(q, k, v, mix1, batch, seq)
    x3, h3 = _mm("odd_out", "nn", mix1, W["odd_w_out"], F32, tk=1024, add=x2, fused=_norm_tail(small["ffn_norm"][1], T, tb))
    ffn1 = ffn_weights(1, fetch("ffn1", (x3,)))
    (dy, sq), ffn1_saved = _ffn_fwd(1, x3, h3, *ffn1, lambda tile: _loss_tail(target.reshape(T, D_MODEL), tile))

    G = {}
    dx3, dffn_g1 = _ffn_bwd(1, x3, small["ffn_norm"][1], *ffn1, ffn1_saved, dy, emit)
    dmix1 = _mm("odd_out_dx", "nt", dx3, W["odd_w_out"], BF16, tk=1024, after=advance(dx3))
    dw_out_odd = _mm("odd_out_dw", "tn", mix1, dx3, BF16, hbm_out=True)
    dq, dk, dv = _flash_bwd(q, k, v, dmix1, mix1, lse, batch, seq)
    dz_pool, dpool_diag, G["pool_scale"] = _pool_bwd(proj1, dmix1, pool_diag, pool_scale, seq, tm)
    dproj1, dq_b, dkv_b, dq_g, dk_g, G["q_a_norm"], G["kv_a_norm"] = _mla_qkv_bwd(
        proj1, cos_t, sin_t, qa_g, kva_g, q_b, kv_b, q_g, k_g, dq, dk, dv, dz_pool, tm)
    G["pool_w"] = jnp.stack([dpool_diag[POOL_DIM * g:POOL_DIM * (g + 1), POOL_DIM * g:POOL_DIM * (g + 1)]
                             for g in range(len(POOL_WINDOWS))])[None]
    G["q_norm"], G["k_norm"] = dq_g[:, :QK_DIM], dk_g[:, :QK_DIM]
    dw_in_odd = _mm("odd_in_dw", "tn", h2, dproj1, BF16, tn=ODD_IN, hbm_out=True)

    def shard_major(g, cols):
        return g.reshape(g.shape[0], N_CHIPS, cols).transpose(1, 0, 2).astype(BF16)

    behind = emit("odd", {"odd_w_in": dw_in_odd.reshape(N_CHIPS, -1, ODD_IN),
                          "q_b": shard_major(dq_b[:, :, :QK_DIM].transpose(1, 0, 2).reshape(Q_LORA, HEADS * QK_DIM), HEADS * QK_DIM // N_CHIPS),
                          "kv_b": shard_major(dkv_b.transpose(1, 0, 2).reshape(KV_LORA, HEADS * (QK_NOPE + V_DIM)),
                                              HEADS * (QK_NOPE + V_DIM) // N_CHIPS),
                          "odd_w_out": dw_out_odd.reshape(N_CHIPS, -1, D_MODEL)})
    dx2, dmix_g1 = _mm("odd_in_dx", "nt", dproj1, W["odd_w_in"], F32, tk=ODD_IN, after=behind,
                       fused=_norm_bwd_tail(x2, small["mix_norm"][1], dx3, tb))
    dx1, dffn_g0 = _ffn_bwd(0, x1, small["ffn_norm"][0], *ffn0, ffn0_saved, dx2, emit, after=advance(dx2))
    dmix0 = _mm("even_out_dx", "nt", dx1, w_out_even, F32, tk=1024, after=advance(dx1))
    dw_out_even = _mm("even_out_dw", "tn", mix0, dx1, BF16, hbm_out=True)
    dproj0, dw_s, db_lanes, G["sg_ln_g"], dconv = _even_mixer_bwd(proj0, dmix0, ln_g, w_tril, b_lanes, conv_w, seq, tm)
    G["sg_w_s"] = dw_s[None]
    G["sg_b_s"] = jnp.sum(db_lanes, axis=-1)[None]
    G["sc_conv_w"] = dconv[None, :CONV_TAPS]
    tr = _resident_tile(T)
    tail, shapes, specs = _norm_bwd_tail(x0, small["mix_norm"][0], dx1, tr)
    dx0, dmix_g0 = _matmul("even_in_dx", "nt", [(dproj0, w_in_even)],
                           [(_row_spec(tr, EVEN_IN), _resident((N_CHIPS, D_MODEL, in_shard)))],
                           (T // tr, 1, 1), shapes, specs, (tr, D_MODEL), tail=tail)
    tk = min(512, T)
    dw_in_even = _grad_shards(
        "even_in_dw", h0, dproj0, BS((tk, D_MODEL), lambda k: (k, 0)), BS((tk, EVEN_IN), lambda k: (k, 0)),
        lambda a_ref, b_ref, j: (a_ref[...], b_ref[:, in_shard * j:in_shard * (j + 1)]), (N_CHIPS, D_MODEL, in_shard), T // tk)
    emit("even", {"even_w_in": dw_in_even, "even_w_out": dw_out_even.reshape(N_CHIPS, -1, D_MODEL)})
    G["mix_norm"] = jnp.concatenate([dmix_g0, dmix_g1], axis=0)
    G["ffn_norm"] = jnp.concatenate([dffn_g0, dffn_g1], axis=0)
    return sq[0, 0], dx0.reshape(batch, seq, D_MODEL), G


def _small_vector(parts, names):
    flat = jnp.concatenate([parts[n].astype(F32).reshape(-1) for n in names])
    return _pad_rows(flat, LANES, 2 * SUBLANES)


def _split_small(vec, like, names):
    out, off, flat = {}, 0, vec.reshape(-1)
    for n in names:
        size = math.prod(like[n].shape)
        out[n] = flat[off:off + size].reshape(like[n].shape)
        off += size
    return out


def _whole_shape(a):
    return a.shape[:-1] + (a.shape[-1] * N_CHIPS,)


def kernel(x, positions, mix_norm, ffn_norm, even_w_in, sg_ln_g, sg_w_s, sg_b_s, sc_conv_w, even_w_out, odd_w_in, pool_w, pool_scale, q_a_norm, q_b, kv_a_norm, kv_b, q_norm, k_norm, odd_w_out, ffn_w_gate, ffn_w_up, ffn_w_down, loss_target, m_mix_norm, m_ffn_norm, m_even_w_in, m_sg_ln_g, m_sg_w_s, m_sg_b_s, m_sc_conv_w, m_even_w_out, m_odd_w_in, m_pool_w, m_pool_scale, m_q_a_norm, m_q_b, m_kv_a_norm, m_kv_b, m_q_norm, m_k_norm, m_odd_w_out, m_ffn_w_gate, m_ffn_w_up, m_ffn_w_down, v_mix_norm, v_ffn_norm, v_even_w_in, v_sg_ln_g, v_sg_w_s, v_sg_b_s, v_sc_conv_w, v_even_w_out, v_odd_w_in, v_pool_w, v_pool_scale, v_q_a_norm, v_q_b, v_kv_a_norm, v_kv_b, v_q_norm, v_k_norm, v_odd_w_out, v_ffn_w_gate, v_ffn_w_up, v_ffn_w_down):
    args = dict(locals())
    w = {n: args[n] for n in _WEIGHTS}
    m = {n: args["m_" + n] for n in _WEIGHTS}
    v = {n: args["v_" + n] for n in _WEIGHTS}
    cx, cy, cc = lax.axis_index("x"), lax.axis_index("y"), lax.axis_index("c")
    chip = 2 * cx + cy
    transposed = ("ffn_w_gate", "ffn_w_up")
    for n in transposed:
        w[n], m[n], v[n] = (jnp.swapaxes(t[n], 1, 2) for t in (w, m, v))

    chip_arr = chip.astype(jnp.int32).reshape(1)
    c_arr = cc.astype(jnp.int32).reshape(1)
    group_names = list(_GROUPS)
    placed = {}
    for n in _SMALL_SHARDED:
        a = w[n]
        whole = jnp.zeros(a.shape[:-1] + (N_CHIPS, a.shape[-1]), F32)
        whole = lax.dynamic_update_slice_in_dim(whole, a[..., None, :], chip, axis=a.ndim - 1)
        placed[n] = jnp.where(cc == 0, whole, 0.0).reshape(_whole_shape(a))
    small_whole = _all_reduce_small("gather_small_weights", _small_vector(placed, _SMALL_SHARDED))
    small = dict({n: w[n] for n in _REPLICATED}, **_split_small(small_whole, placed, _SMALL_SHARDED))

    first, rest = list(_GROUPS[group_names[0]]), [n for g in group_names[1:] for n in _GROUPS[g]]
    sems_first, flight_first, token = _gather_send("gather_send_first", _place_shards(w, first, chip_arr, (small_whole,)),
                                                   [list(range(len(first)))], (small_whole,))
    sems_rest, flight_rest, all_sent = _gather_send("gather_send_rest", _place_shards(w, rest, chip_arr, (token,)),
                                                    [[rest.index(n) for n in _GROUPS[g]] for g in group_names[1:]], ())
    sems = list(sems_first) + list(sems_rest)
    in_flight = dict(zip(first + rest, list(flight_first) + list(flight_rest)))

    def fetch(group, after):
        gi, members = group_names.index(group), _GROUPS[group]
        after = after if gi else (all_sent,)
        landed = _gather_wait(f"gather_wait_{group}", [in_flight[n] for n in members], sems[2 * gi], sems[2 * gi + 1], after)
        return _whole_weights(dict(zip(members, _gather_pass(f"gather_pass_{group}", landed))))

    swapping, pending, arrived, sent = [], [], {}, []

    def settle(after):
        names, ps, lands, send_sems, recv_sems = pending.pop()
        ps, lands = _scatter_wait(f"scatter_wait_{names[0]}", ps, lands, send_sems, recv_sems, after)
        arrived.update({n: (p, r) for n, p, r in zip(names, ps, lands)})

    def emit(group, grads):
        names = _GROUPS[group]
        halves = [grads[n].reshape(N_CHIPS, 2, grads[n].shape[1] // 2, grads[n].shape[2]) for n in names]
        send_sems, recv_sems, halves, lands, token = _exchange_send(f"exchange_send_{group}", halves)
        swapping.append((group, halves, lands, send_sems, recv_sems))
        sent.append(token)
        return (token,)

    def advance(done):
        done = done if isinstance(done, tuple) else (done,)
        group, halves, lands, send_sems, recv_sems = swapping.pop()
        names = _GROUPS[group]
        halves, lands = _exchange_wait(f"exchange_wait_{group}", halves, lands, send_sems, recv_sems, done)
        partial = [_add_halves(f"add_{n}", g, r, c_arr) for n, g, r in zip(names, halves, lands)]
        if pending:
            settle(done)
        send_sems, recv_sems, ps, lands, token = _scatter_send(f"scatter_send_{group}", partial)
        pending.append((names, ps, lands, send_sems, recv_sems))
        return (token,)

    sq, grad_x, G = _forward_backward(x, positions, loss_target, small, fetch, emit, advance)
    small_names = _REPLICATED + _SMALL_SHARDED
    G["loss"] = (0.5 * sq / D_MODEL).reshape(1)
    summed = _split_small(_all_reduce_small("reduce_small_grads", _small_vector(G, small_names + ("loss",))), G,
                          small_names + ("loss",))
    loss = summed["loss"][0]
    grads = {n: summed[n] for n in _REPLICATED}
    for n in _SMALL_SHARDED:
        a = w[n]
        grads[n] = lax.dynamic_slice_in_dim(summed[n].reshape(a.shape[:-1] + (N_CHIPS, a.shape[-1])), chip, 1,
                                            axis=a.ndim - 1).reshape(a.shape)

    chip_c = jnp.stack([chip, cc]).astype(jnp.int32)
    out = {}

    def finish(group, after):
        names, tokens = _GROUPS[group], []
        sums = [_sum_partials(f"sum_{n}", *arrived[n], chip_c) for n in names]
        for n, f in zip(names, _sibling_share(f"grad_share_{group}", sums, after)):
            weight, layer = (n[:-1], int(n[-1])) if n[-1].isdigit() else (n, 0)
            *out[weight], token = _adamw(f"adamw_{weight}", w[weight], f.reshape(-1, f.shape[-1]), m[weight], v[weight], layer,
                                         out.get(weight, ()))
            tokens.append(token)
        return tuple(tokens)

    last_exchange = tuple(sent[-1:])
    last_scatter = advance(finish(group_names[3], last_exchange) + finish(group_names[2], last_exchange))
    settle(finish(group_names[1], last_scatter))
    finish(group_names[0], ())
    packed = [_small_vector(d, small_names) for d in (w, grads, m, v)]
    res = _adamw("adamw_small", packed[0][None], packed[1], packed[2][None], packed[3][None])
    delta_s, m_s, v_s = (_split_small(r, w, small_names) for r in res[1:4])
    for n in small_names:
        out[n] = (grads[n], delta_s[n], m_s[n], v_s[n])
    for n in transposed:
        out[n] = tuple(jnp.swapaxes(t, 1, 2) for t in out[n])

    return (loss, grad_x, *[out[n][0] for n in _WEIGHTS], *[out[n][1] for n in _WEIGHTS],
            *[out[n][2] for n in _WEIGHTS], *[out[n][3] for n in _WEIGHTS])
```

```python
import functools
import math

import jax
import jax.numpy as jnp
from jax import lax
from jax.experimental import pallas as pl
from jax.experimental.pallas import tpu as pltpu

F32, BF16 = jnp.float32, jnp.bfloat16
BS = pl.BlockSpec

D_MODEL = 1024
EPS = 1e-6
NEG_INF = -1e30
SG_HEADS, SG_DIM, SG_WIDTH, SG_CHUNK = 4, 128, 512, 128
SC_WIDTH, CONV_TAPS = 512, 3
EVEN_IN = 2 * SG_WIDTH + 3 * SC_WIDTH
POOL_WINDOWS = (2, 4, 8, 16)
POOL_DIM, POOL_WIDTH = 64, 256
POOL_HALO = 16
HEADS, Q_LORA, KV_LORA, QK_NOPE, QK_ROPE, V_DIM = 6, 384, 256, 128, 64, 128
QK_DIM = QK_NOPE + QK_ROPE
QK_PAD = 256
ODD_IN = POOL_WIDTH + Q_LORA + KV_LORA + QK_ROPE
ODD_IN_PAD = 1024
ROPE_THETA = 10000.0
ATTN_SCALE = QK_DIM ** -0.5
D_FF, N_CHIPS = 2816, 4
FF_SHARD = D_FF // N_CHIPS
ADAM_LR, ADAM_B1, ADAM_B2, ADAM_EPS, ADAM_WD, ADAM_STEP = 0.001, 0.9, 0.999, 1e-08, 0.01, 10
VMEM_LIMIT_V7X = 48 * 2**20
LANES, SUBLANES = 128, 8
MESH = pl.DeviceIdType.MESH
HBM = pl.BlockSpec(memory_space=pltpu.HBM)
VMEM = pl.BlockSpec(memory_space=pltpu.VMEM)

_DIMS = {"nn": (((1,), (0,)), ((), ())), "nt": (((1,), (1,)), ((), ())), "tn": (((0,), (0,)), ((), ()))}


def _dot(a, b, mode="nn"):
    return lax.dot_general(a.astype(BF16), b.astype(BF16), _DIMS[mode], preferred_element_type=F32)


def _call(body, *, name, out_shape, in_specs, out_specs, grid=(), scratch=(), aliases=None, after=()):
    params = pltpu.CompilerParams(vmem_limit_bytes=VMEM_LIMIT_V7X,
                                  **({"dimension_semantics": ("arbitrary",) * len(grid)} if grid else {}))
    n_in, n_after = len(in_specs), len(after)
    kernel_body = body if not after else (lambda *refs: body(*refs[:n_in], *refs[n_in + n_after:]))
    call = pl.pallas_call(kernel_body, name=name, grid=grid, in_specs=list(in_specs) + [pl.BlockSpec(memory_space=pl.ANY)] * n_after,
                          out_specs=out_specs, out_shape=out_shape, scratch_shapes=list(scratch),
                          input_output_aliases=aliases or {}, compiler_params=params)
    return (lambda *ops: call(*ops, *after)) if after else call


def _sds(shape, dtype):
    return jax.ShapeDtypeStruct(tuple(shape), dtype)


def _token_tile(seq):
    return 512 if seq % 512 == 0 else seq


_TAIL_ROWS = 256


def _matmul(name, mode, pairs, pair_specs, grid, out_shape, out_spec, acc_shape, add=None, add_spec=None, after=(), tail=None):
    n, nk = len(pairs), grid[-1]
    n_add = int(add is not None)
    n_tail = len(tail[0]) if tail else 0
    n_in = 2 * n + n_add + n_tail
    n_out = len(out_shape) if tail else 1

    def body(*refs):
        ab = refs[:2 * n]
        add_ref = refs[2 * n] if n_add else None
        tail_refs, outs = refs[2 * n + n_add:n_in], refs[n_in:n_in + n_out]
        first = pl.program_id(0) == 0

        def finish(result):
            if tail is None:
                r = result(slice(None))
                outs[0][...] = (r if add_ref is None else r + add_ref[...]).astype(outs[0].dtype)
                return
            for lo in range(0, acc_shape[0], _TAIL_ROWS):
                rows = slice(lo, min(lo + _TAIL_ROWS, acc_shape[0]))
                r = result(rows)
                tail[2](rows, r if add_ref is None else r + add_ref[rows, :], first, tail_refs, outs)

        def terms(a_ref, b_ref):
            if len(a_ref.shape) == 2 and len(b_ref.shape) == 2:
                return [(a_ref[...], b_ref[...])]
            cols = a_ref.shape[-1] // N_CHIPS
            return [(a_ref[j] if len(a_ref.shape) == 3 else a_ref[:, cols * j:cols * (j + 1)], b_ref[j]) for j in range(N_CHIPS)]

        if nk == 1:
            r = None
            for p in range(n):
                for a_blk, b_blk in terms(ab[2 * p], ab[2 * p + 1]):
                    d = _dot(a_blk, b_blk, mode)
                    r = d if r is None else r + d
            finish(lambda rows: r[rows])
            return
        acc = refs[-1]
        k = pl.program_id(len(grid) - 1)

        @pl.when(k == 0)
        def _():
            acc[...] = jnp.zeros_like(acc)

        for p in range(n):
            acc[...] += _dot(ab[2 * p][...], ab[2 * p + 1][...], mode)

        @pl.when(k == nk - 1)
        def _():
            finish(lambda rows: acc[rows, :])

    ops = [t for pr in pairs for t in pr] + ([add] if n_add else []) + (list(tail[0]) if tail else [])
    specs = [s for pr in pair_specs for s in pr] + ([add_spec] if n_add else []) + (list(tail[1]) if tail else [])
    return _call(body, name=name, grid=grid, in_specs=specs, out_specs=out_spec, out_shape=out_shape,
                 scratch=[pltpu.VMEM(acc_shape, F32)] if nk > 1 else [], after=after)(*ops)


def _row_spec(tm, d):
    return BS((tm, d), lambda i, j, k: (i, 0))


def _vec_spec(d):
    return BS((1, d), lambda i, j, k: (0, 0))


def _norm_tail(gain, T, tm):
    d = gain.shape[-1]

    def fn(rows, r, first, tail_refs, outs):
        outs[0][rows, :] = r
        outs[1][rows, :] = (r * lax.rsqrt(jnp.mean(r * r, axis=-1, keepdims=True) + EPS) * tail_refs[0][...]).astype(BF16)

    return ([gain.reshape(1, d)], [_vec_spec(d)], fn), [_sds((T, d), F32), _sds((T, d), BF16)], [_row_spec(tm, d), _row_spec(tm, d)]


def _norm_bwd_tail(x, gain, dres, tm):
    T, d = x.shape

    def fn(rows, r, first, tail_refs, outs):
        x_ref, g_ref, dres_ref = tail_refs
        xv = x_ref[rows, :]
        rstd = lax.rsqrt(jnp.mean(xv * xv, axis=-1, keepdims=True) + EPS)
        xhat = xv * rstd
        if rows.start == 0:
            @pl.when(first)
            def _():
                outs[1][...] = jnp.zeros_like(outs[1])

        outs[1][...] += jnp.sum(r * xhat, axis=0, keepdims=True)
        dxhat = r * g_ref[...]
        outs[0][rows, :] = dres_ref[rows, :] + rstd * (dxhat - xhat * jnp.mean(dxhat * xhat, axis=-1, keepdims=True))

    return (([x, gain.reshape(1, d), dres], [_row_spec(tm, d), _vec_spec(d), _row_spec(tm, d)], fn),
            [_sds((T, d), F32), _sds((1, d), F32)], [_row_spec(tm, d), _vec_spec(d)])


def _loss_tail(target, tm):
    T, d = target.shape

    def fn(rows, r, first, tail_refs, outs):
        e = r - tail_refs[0][rows, :]
        if rows.start == 0:
            @pl.when(first)
            def _():
                outs[1][...] = jnp.zeros_like(outs[1])

        outs[1][...] += jnp.sum(e * e)
        outs[0][rows, :] = e * (1.0 / d)

    return (([target], [_row_spec(tm, d)], fn), [_sds((T, d), F32), _sds((SUBLANES, LANES), F32)],
            [_row_spec(tm, d), BS((SUBLANES, LANES), lambda i, j, k: (0, 0))])


def _grad_shards(name, a, b, a_spec, b_spec, pick, out_shape, n_steps):
    def body(a_ref, b_ref, o_ref, acc):
        k = pl.program_id(0)

        @pl.when(k == 0)
        def _():
            acc[...] = jnp.zeros_like(acc)

        for j in range(N_CHIPS):
            aj, bj = pick(a_ref, b_ref, j)
            acc[j] += _dot(aj, bj, "tn")

        @pl.when(k == n_steps - 1)
        def _():
            o_ref[...] = acc[...].astype(BF16)

    return _call(body, name=name, grid=(n_steps,), in_specs=[a_spec, b_spec], scratch=[pltpu.VMEM(tuple(out_shape), F32)],
                 out_specs=BS(out_shape, lambda k: (0, 0, 0)), out_shape=pltpu.HBM(tuple(out_shape), BF16))(a, b)


def _mm(name, mode, a, b, out_dtype, tm=1024, tn=1024, tk=512, add=None, after=(), fused=None, hbm_out=False):
    if mode == "tn":
        (K, M), N = a.shape, b.shape[1]
    else:
        (M, K), N = a.shape, (b.shape[1] if mode == "nn" else b.shape[0])
    tm, tn, tk = min(tm, M), min(tn, N), min(tk, K)
    a_spec = BS((tk, tm), lambda i, j, k: (k, i)) if mode == "tn" else BS((tm, tk), lambda i, j, k: (i, k))
    b_spec = BS((tn, tk), lambda i, j, k: (j, k)) if mode == "nt" else BS((tk, tn), lambda i, j, k: (k, j))
    o_spec = BS((tm, tn), lambda i, j, k: (i, j))
    tail, shapes, specs = fused if fused else (None, pltpu.HBM((M, N), out_dtype) if hbm_out else _sds((M, N), out_dtype), o_spec)
    return _matmul(name, mode, [(a, b)], [(a_spec, b_spec)], (M // tm, N // tn, K // tk), shapes, specs, (tm, tn),
                   add=add, add_spec=o_spec if add is not None else None, after=after, tail=tail)


_PASS_ROWS = 256


def _ffn_up(name, h, wg, wu, tm):
    T = h.shape[0]

    def body(h_ref, wg_ref, wu_ref, g_ref, u_ref, a_ref):
        hv = h_ref[...]
        g = _dot(hv, wg_ref[...], "nt")
        u = _dot(hv, wu_ref[...], "nt")
        g_ref[...] = g.astype(BF16)
        u_ref[...] = u.astype(BF16)
        a_ref[...] = (g * (1.0 / (1.0 + jnp.exp(-g))) * u).astype(BF16)

    w_spec = BS((None, FF_SHARD, D_MODEL), lambda j, i: (j, 0, 0))
    o_spec = BS((None, tm, FF_SHARD), lambda j, i: (j, i, 0))
    sh = _sds((N_CHIPS, T, FF_SHARD), BF16)
    return _call(body, name=name, grid=(N_CHIPS, T // tm), in_specs=[BS((tm, D_MODEL), lambda j, i: (i, 0)), w_spec, w_spec],
                 out_specs=[o_spec, o_spec, o_spec], out_shape=[sh, sh, sh])(h, wg, wu)


def _ffn_act_bwd(name, dxo, wd, g, u, tm, after=()):
    T = dxo.shape[0]

    def body(dx_ref, wd_ref, g_ref, u_ref, dg_ref, du_ref):
        da = _dot(dx_ref[...], wd_ref[...], "nt")
        g = g_ref[...].astype(F32)
        sig = 1.0 / (1.0 + jnp.exp(-g))
        dg_ref[...] = (da * u_ref[...].astype(F32) * (sig * (1.0 + g * (1.0 - sig)))).astype(BF16)
        du_ref[...] = (da * (g * sig)).astype(BF16)

    t_spec = BS((None, tm, FF_SHARD), lambda i, j: (j, i, 0))
    sh = _sds((N_CHIPS, T, FF_SHARD), BF16)
    return _call(body, name=name, grid=(T // tm, N_CHIPS),
                 in_specs=[BS((tm, D_MODEL), lambda i, j: (i, 0)), BS((None, FF_SHARD, D_MODEL), lambda i, j: (j, 0, 0)), t_spec, t_spec],
                 out_specs=[t_spec, t_spec], out_shape=[sh, sh], after=after)(dxo, wd, g, u)


def _big_tile(n):
    return min(1024, n)


def _resident_tile(n):
    return min(512, n)


def _resident(shape):
    return BS(shape, lambda i, j, k: (0,) * len(shape), pipeline_mode=pl.Buffered(1))


def _ffn_fwd(l, x, h, wg, wu, wd, fused):
    T = x.shape[0]
    g, u, a = _ffn_up(f"ffn{l}_up", h, wg, wu, _big_tile(T))
    tm = _resident_tile(T)
    tail, shapes, specs = fused(tm)
    outs = _matmul(f"ffn{l}_down", "nn", [(a, wd)],
                   [(BS((N_CHIPS, tm, FF_SHARD), lambda i, j, k: (0, i, 0)), _resident((N_CHIPS, FF_SHARD, D_MODEL)))],
                   (T // tm, 1, 1), shapes, specs, (tm, D_MODEL), add=x, add_spec=_row_spec(tm, D_MODEL), tail=tail)
    return outs, (h, g, u, a)


def _ffn_bwd(l, x, gain, wg, wu, wd, saved, dxo, emit, after=()):
    h, g, u, a = saved
    T = x.shape[0]
    tm = _big_tile(T)
    dg, du = _ffn_act_bwd(f"ffn{l}_act_bwd", dxo, wd, g, u, tm, after=after)
    tk = _big_tile(T)
    shards_spec = BS((N_CHIPS, tk, FF_SHARD), lambda k: (0, k, 0))
    rows_spec = BS((tk, D_MODEL), lambda k: (k, 0))

    def dw(nm, act, rows):
        return _grad_shards(nm, act, rows, shards_spec, rows_spec, lambda a_ref, b_ref, j: (a_ref[j], b_ref[...]),
                            (N_CHIPS, FF_SHARD, D_MODEL), T // tk)

    behind = emit(f"ffn{l}", {f"ffn_w_gate{l}": dw(f"ffn{l}_dwg", dg, h), f"ffn_w_up{l}": dw(f"ffn{l}_dwu", du, h),
                              f"ffn_w_down{l}": dw(f"ffn{l}_dwd", a, dxo)})
    tm = _resident_tile(T)
    act_spec = BS((N_CHIPS, tm, FF_SHARD), lambda i, j, k: (0, i, 0))
    w_spec = _resident((N_CHIPS, FF_SHARD, D_MODEL))
    tail, shapes, specs = _norm_bwd_tail(x, gain, dxo, tm)
    return _matmul(f"ffn{l}_dh", "nn", [(dg, wg), (du, wu)], [(act_spec, w_spec), (act_spec, w_spec)],
                   (T // tm, 1, 1), shapes, specs, (tm, D_MODEL), after=behind, tail=tail)


_INV_SQRT2 = 1.0 / math.sqrt(2.0)
_INV_SQRT_2PI = 1.0 / math.sqrt(2.0 * math.pi)


def _gelu(x):
    return 0.5 * x * (1.0 + lax.erf(x * _INV_SQRT2))


def _gelu_and_grad(x):
    cdf = 0.5 * (1.0 + lax.erf(x * _INV_SQRT2))
    return x * cdf, cdf + x * jnp.exp(-0.5 * x * x) * _INV_SQRT_2PI


def _shift_down(x, k):
    return pltpu.roll(x, k, 0)


def _shift_up(x, k):
    return pltpu.roll(x, x.shape[0] - k, 0)


def _layer_norm_head(xh):
    xc = xh - jnp.mean(xh, axis=-1, keepdims=True)
    rstd = lax.rsqrt(jnp.mean(xc * xc, axis=-1, keepdims=True) + EPS)
    return xc * rstd, rstd


def _even_in(x, gain, w, tm):
    T, d = x.shape
    shard = w.shape[-1]

    def body(x_ref, g_ref, w_ref, o_ref, h_ref):
        xv = x_ref[...]
        hv = (xv * lax.rsqrt(jnp.mean(xv * xv, axis=-1, keepdims=True) + EPS) * g_ref[...]).astype(BF16)
        h_ref[...] = hv
        for j in range(N_CHIPS):
            o_ref[:, shard * j:shard * (j + 1)] = _dot(hv, w_ref[j])

    row = BS((tm, d), lambda i: (i, 0))
    return _call(body, name="even_in", grid=(T // tm,),
                 in_specs=[row, BS((1, d), lambda i: (0, 0)), BS(w.shape, lambda i: (0, 0, 0), pipeline_mode=pl.Buffered(1))],
                 out_specs=[BS((tm, N_CHIPS * shard), lambda i: (i, 0)), row],
                 out_shape=[_sds((T, N_CHIPS * shard), F32), _sds((T, d), BF16)])(x, gain.reshape(1, d), w)


def _even_halo_specs(tm, n_tiles, col_blocks, after):
    rows = tm // SUBLANES
    last = n_tiles * rows - 1
    if after:
        return [BS((SUBLANES, 512), functools.partial(lambda cb, i: (jnp.minimum((i + 1) * rows, last), cb), cb)) for cb in col_blocks]
    return [BS((SUBLANES, 512), functools.partial(lambda cb, i: (jnp.maximum(i * rows - 1, 0), cb), cb)) for cb in col_blocks]


def _even_mixer_fwd(proj, ln_g, w_tril, b_lanes, conv_w, seq, tm):
    T = proj.shape[0]
    tiles_per_seq = seq // tm

    def body(p_ref, hc_ref, hh_ref, lng_ref, w_ref, bb_ref, cw_ref, o_ref):
        first = pl.program_id(0) % tiles_per_seq == 0
        for h in range(SG_HEADS):
            cols = slice(SG_DIM * h, SG_DIM * (h + 1))
            vhat, _ = _layer_norm_head(_gelu(p_ref[:, SG_WIDTH + SG_DIM * h:SG_WIDTH + SG_DIM * (h + 1)]))
            vln = (vhat * lng_ref[:, cols]).astype(BF16)
            for k in range(tm // SG_CHUNK):
                rows = slice(SG_CHUNK * k, SG_CHUNK * (k + 1))
                mixed = _dot(w_ref[h], vln[rows]) + bb_ref[h]
                o_ref[rows, cols] = (_gelu(p_ref[rows, cols]) * mixed).astype(BF16)
        z = p_ref[:, 1536:2048] * p_ref[:, 2048:2560]
        zz = jnp.concatenate([jnp.where(first, 0.0, hc_ref[...] * hh_ref[...]), z], axis=0)
        y = cw_ref[0:1, :] * _shift_down(zz, 2)[SUBLANES:] + cw_ref[1:2, :] * _shift_down(zz, 1)[SUBLANES:] + cw_ref[2:3, :] * z
        o_ref[:, SG_WIDTH:] = (p_ref[:, 1024:1536] * y).astype(BF16)

    full = lambda shape: BS(shape, lambda i: (0,) * len(shape))
    return _call(body, name="even_mixer_fwd", grid=(T // tm,),
                 in_specs=[BS((tm, EVEN_IN), lambda i: (i, 0))] + _even_halo_specs(tm, T // tm, (3, 4), after=False)
                 + [full((1, SG_WIDTH)), full((SG_HEADS, SG_CHUNK, SG_CHUNK)), full((SG_HEADS, SG_CHUNK, SG_DIM)), full((SUBLANES, SC_WIDTH))],
                 out_specs=BS((tm, D_MODEL), lambda i: (i, 0)), out_shape=_sds((T, D_MODEL), BF16))(
        proj, proj, proj, ln_g, w_tril, b_lanes, conv_w)


def _even_mixer_bwd(proj, dmix, ln_g, w_tril, b_lanes, conv_w, seq, tm):
    T = proj.shape[0]
    n_tiles, tiles_per_seq = T // tm, seq // tm

    def body(p_ref, dm_ref, hc_ref, hh_ref, nd_ref, nb_ref, lng_ref, w_ref, bb_ref, cw_ref,
             dp_ref, dw_ref, db_ref, dlng_ref, dcw_ref):
        i = pl.program_id(0)
        first = i % tiles_per_seq == 0
        last = i % tiles_per_seq == tiles_per_seq - 1

        @pl.when(i == 0)
        def _():
            dw_ref[...] = jnp.zeros_like(dw_ref)
            db_ref[...] = jnp.zeros_like(db_ref)
            dlng_ref[...] = jnp.zeros_like(dlng_ref)
            dcw_ref[...] = jnp.zeros_like(dcw_ref)

        for h in range(SG_HEADS):
            cols = slice(SG_DIM * h, SG_DIM * (h + 1))
            vcols = slice(SG_WIDTH + SG_DIM * h, SG_WIDTH + SG_DIM * (h + 1))
            lng = lng_ref[:, cols]
            for k in range(tm // SG_CHUNK):
                rows = slice(SG_CHUNK * k, SG_CHUNK * (k + 1))
                gelu_v, dgelu_v = _gelu_and_grad(p_ref[rows, vcols])
                vhat, rstd = _layer_norm_head(gelu_v)
                vln = (vhat * lng).astype(BF16)
                mixed = _dot(w_ref[h], vln) + bb_ref[h]
                gelu_u, dgelu_u = _gelu_and_grad(p_ref[rows, cols])
                da = dm_ref[rows, cols]
                dp_ref[rows, cols] = (da * mixed * dgelu_u).astype(BF16)
                dmixed = da * gelu_u
                db_ref[h] += dmixed
                dw_ref[h] += _dot(dmixed, vln, "nt")
                dvln = _dot(w_ref[h], dmixed, "tn")
                dlng_ref[:, cols] += jnp.sum(dvln * vhat, axis=0, keepdims=True)
                dvhat = dvln * lng
                dgv = rstd * (dvhat - jnp.mean(dvhat, axis=-1, keepdims=True)
                              - vhat * jnp.mean(dvhat * vhat, axis=-1, keepdims=True))
                dp_ref[rows, vcols] = (dgv * dgelu_v).astype(BF16)

        b = p_ref[:, 1024:1536]
        c = p_ref[:, 1536:2048]
        hv = p_ref[:, 2048:2560]
        z = c * hv
        zz = jnp.concatenate([jnp.where(first, 0.0, hc_ref[...] * hh_ref[...]), z], axis=0)
        z1 = _shift_down(zz, 1)[SUBLANES:]
        z2 = _shift_down(zz, 2)[SUBLANES:]
        w0, w1, w2 = cw_ref[0:1, :], cw_ref[1:2, :], cw_ref[2:3, :]
        dbo = dm_ref[:, SG_WIDTH:]
        dy = dbo * b
        dd = jnp.concatenate([dy, jnp.where(last, 0.0, nd_ref[...] * nb_ref[...])], axis=0)
        dz = w2 * dy + w1 * _shift_up(dd, 1)[:tm] + w0 * _shift_up(dd, 2)[:tm]
        dp_ref[:, 1024:1536] = (dbo * (w0 * z2 + w1 * z1 + w2 * z)).astype(BF16)
        dp_ref[:, 1536:2048] = (dz * hv).astype(BF16)
        dp_ref[:, 2048:2560] = (dz * c).astype(BF16)
        dcw_ref[0:1, :] += jnp.sum(dy * z2, axis=0, keepdims=True)
        dcw_ref[1:2, :] += jnp.sum(dy * z1, axis=0, keepdims=True)
        dcw_ref[2:3, :] += jnp.sum(dy * z, axis=0, keepdims=True)

        @pl.when(i == n_tiles - 1)
        def _():
            t_idx = lax.broadcasted_iota(jnp.int32, (SG_CHUNK, SG_CHUNK), 0)
            s_idx = lax.broadcasted_iota(jnp.int32, (SG_CHUNK, SG_CHUNK), 1)
            for h in range(SG_HEADS):
                dw_ref[h] = jnp.where(t_idx >= s_idx, dw_ref[h], 0.0)

    full = lambda shape: BS(shape, lambda i: (0,) * len(shape))
    sq = (SG_HEADS, SG_CHUNK, SG_CHUNK)
    return _call(body, name="even_mixer_bwd", grid=(n_tiles,),
                 in_specs=[BS((tm, EVEN_IN), lambda i: (i, 0)), BS((tm, D_MODEL), lambda i: (i, 0))]
                 + _even_halo_specs(tm, n_tiles, (3, 4), after=False)
                 + _even_halo_specs(tm, n_tiles, (1,), after=True) + _even_halo_specs(tm, n_tiles, (2,), after=True)
                 + [full((1, SG_WIDTH)), full(sq), full(sq), full((SUBLANES, SC_WIDTH))],
                 out_specs=[BS((tm, EVEN_IN), lambda i: (i, 0)), full(sq), full(sq), full((1, SG_WIDTH)), full((SUBLANES, SC_WIDTH))],
                 out_shape=[_sds((T, EVEN_IN), BF16), _sds(sq, F32), _sds(sq, F32), _sds((1, SG_WIDTH), F32), _sds((SUBLANES, SC_WIDTH), F32)])(
        proj, dmix, proj, proj, dmix, proj, ln_g, w_tril, b_lanes, conv_w)


def _pool_select(vals):
    lane = lax.broadcasted_iota(jnp.int32, vals[0].shape, 1)
    out = vals[-1]
    for g in range(len(vals) - 2, -1, -1):
        out = jnp.where(lane < POOL_DIM * (g + 1), vals[g], out)
    return out


def _pool_counts(pos1):
    lane = lax.broadcasted_iota(jnp.int32, (pos1.shape[0], POOL_WIDTH), 1)
    win = _pool_select([jnp.full(lane.shape, float(w), F32) for w in POOL_WINDOWS])
    return jnp.minimum(pos1, win)


def _pool_means(zz, counts):
    s2 = zz + _shift_down(zz, 1)
    s4 = s2 + _shift_down(s2, 2)
    s8 = s4 + _shift_down(s4, 4)
    s16 = s8 + _shift_down(s8, 8)
    return _pool_select([s2, s4, s8, s16])[POOL_HALO:] / counts


def _pool_halo_spec(tm, n_tiles, after):
    rows = tm // POOL_HALO
    if after:
        return BS((POOL_HALO, POOL_WIDTH), lambda i: (jnp.minimum((i + 1) * rows, n_tiles * rows - 1), 0))
    return BS((POOL_HALO, POOL_WIDTH), lambda i: (jnp.maximum(i * rows - 1, 0), 0))


def _pool_fwd(proj, w_diag, scale, seq, tm):
    T = proj.shape[0]
    tiles_per_seq = seq // tm

    def body(z_ref, zh_ref, w_ref, s_ref, o_ref):
        t = pl.program_id(0) % tiles_per_seq
        z = z_ref[...]
        zz = jnp.concatenate([jnp.where(t == 0, 0.0, zh_ref[...]), z], axis=0)
        pos1 = (lax.broadcasted_iota(jnp.int32, (tm, 1), 0) + (t * tm + 1)).astype(F32)
        pooled = _pool_means(zz, _pool_counts(pos1)) - z
        o_ref[...] = (_dot(pooled, w_ref[...]) * s_ref[...]).astype(BF16)

    full = lambda shape: BS(shape, lambda i: (0,) * len(shape))
    return _call(body, name="pool_fwd", grid=(T // tm,),
                 in_specs=[BS((tm, POOL_WIDTH), lambda i: (i, 0)), _pool_halo_spec(tm, T // tm, False),
                           full((POOL_WIDTH, POOL_WIDTH)), full((1, POOL_WIDTH))],
                 out_specs=BS((tm, POOL_WIDTH), lambda i: (i, 0)), out_shape=_sds((T, D_MODEL), BF16))(proj, proj, w_diag, scale)


def _pool_bwd(proj, dmix, w_diag, scale, seq, tm):
    T = proj.shape[0]
    n_tiles, tiles_per_seq = T // tm, seq // tm

    def body(z_ref, zh_ref, do_ref, don_ref, w_ref, s_ref, dz_ref, dw_ref, ds_ref):
        i = pl.program_id(0)
        t = i % tiles_per_seq

        @pl.when(i == 0)
        def _():
            dw_ref[...] = jnp.zeros_like(dw_ref)
            ds_ref[...] = jnp.zeros_like(ds_ref)

        z = z_ref[...]
        zz = jnp.concatenate([jnp.where(t == 0, 0.0, zh_ref[...]), z], axis=0)
        pos1 = (lax.broadcasted_iota(jnp.int32, (tm, 1), 0) + (t * tm + 1)).astype(F32)
        counts = _pool_counts(pos1)
        pooled = _pool_means(zz, counts) - z
        dout = do_ref[...].astype(F32)
        ds_ref[...] += jnp.sum(dout * _dot(pooled, w_ref[...]), axis=0, keepdims=True)
        dlin = dout * s_ref[...]
        dw_ref[...] += _dot(pooled, dlin, "tn")
        dpooled = _dot(dlin, w_ref[...], "nt")
        dpooled_n = _dot(don_ref[...].astype(F32) * s_ref[...], w_ref[...], "nt")
        pos1_n = (lax.broadcasted_iota(jnp.int32, (POOL_HALO, 1), 0) + ((t + 1) * tm + 1)).astype(F32)
        dmean_n = jnp.where(t == tiles_per_seq - 1, 0.0, dpooled_n / _pool_counts(pos1_n))
        dd = jnp.concatenate([dpooled / counts, dmean_n], axis=0)
        r2 = dd + _shift_up(dd, 1)
        r4 = r2 + _shift_up(r2, 2)
        r8 = r4 + _shift_up(r4, 4)
        r16 = r8 + _shift_up(r8, 8)
        dz_ref[...] = (_pool_select([r2, r4, r8, r16])[:tm] - dpooled).astype(BF16)

    full = lambda shape: BS(shape, lambda i: (0,) * len(shape))
    return _call(body, name="pool_bwd", grid=(n_tiles,),
                 in_specs=[BS((tm, POOL_WIDTH), lambda i: (i, 0)), _pool_halo_spec(tm, n_tiles, False),
                           BS((tm, POOL_WIDTH), lambda i: (i, 0)), _pool_halo_spec(tm, n_tiles, True),
                           full((POOL_WIDTH, POOL_WIDTH)), full((1, POOL_WIDTH))],
                 out_specs=[BS((tm, POOL_WIDTH), lambda i: (i, 0)), full((POOL_WIDTH, POOL_WIDTH)), full((1, POOL_WIDTH))],
                 out_shape=[_sds((T, POOL_WIDTH), BF16), _sds((POOL_WIDTH, POOL_WIDTH), F32), _sds((1, POOL_WIDTH), F32)])(
        proj, proj, dmix, dmix, w_diag, scale)


def _rope_partner(r):
    lane = lax.broadcasted_iota(jnp.int32, r.shape, 1)
    return jnp.where(lane < QK_ROPE // 2, pltpu.roll(r, LANES - QK_ROPE // 2, 1), pltpu.roll(r, QK_ROPE // 2, 1))


def _rope(x, cos, sin_signed):
    r = x[:, QK_NOPE:]
    return jnp.concatenate([x[:, :QK_NOPE], r * cos + _rope_partner(r) * sin_signed], axis=1)


def _rope_transposed(dx, cos, sin_signed):
    dr = dx[:, QK_NOPE:]
    return jnp.concatenate([dx[:, :QK_NOPE], dr * cos + _rope_partner(dr * sin_signed)], axis=1)


def _head_norm(x):
    r = lax.rsqrt(jnp.sum(x * x, axis=-1, keepdims=True) * (1.0 / QK_DIM) + EPS)
    return x * r, r


def _head_norm_bwd(dy, xhat, r, gain):
    dxhat = dy * gain
    return r * (dxhat - xhat * (jnp.sum(dxhat * xhat, axis=-1, keepdims=True) * (1.0 / QK_DIM)))


def _latents(p_ref, qag_ref, kvag_ref):
    ql = p_ref[:, POOL_WIDTH:POOL_WIDTH + Q_LORA]
    kvl = p_ref[:, POOL_WIDTH + Q_LORA:POOL_WIDTH + Q_LORA + KV_LORA]
    rq = lax.rsqrt(jnp.mean(ql * ql, axis=-1, keepdims=True) + EPS)
    rkv = lax.rsqrt(jnp.mean(kvl * kvl, axis=-1, keepdims=True) + EPS)
    return ql * rq, rq, kvl * rkv, rkv


def _mla_specs(tm):
    full = lambda shape: BS(shape, lambda i, h: (0,) * len(shape))
    return [BS((tm, ODD_IN_PAD), lambda i, h: (i, 0)), BS((tm, LANES), lambda i, h: (i, 0)), BS((tm, LANES), lambda i, h: (i, 0)),
            full((1, Q_LORA)), full((1, KV_LORA)), BS((None, Q_LORA, QK_PAD), lambda i, h: (h, 0, 0)),
            BS((None, KV_LORA, QK_PAD), lambda i, h: (h, 0, 0)), full((1, QK_PAD)), full((1, QK_PAD))]


def _mla_qkv_fwd(proj, cos, sin_signed, qa_g, kva_g, q_b, kv_b, q_g, k_g, tm):
    T = proj.shape[0]

    def body(p_ref, cos_ref, sin_ref, qag_ref, kvag_ref, qb_ref, kvb_ref, qg_ref, kg_ref, q_ref, k_ref, v_ref, qn_s, kvn_s):
        @pl.when(pl.program_id(1) == 0)
        def _():
            qhat, _, kvhat, _ = _latents(p_ref, qag_ref, kvag_ref)
            qn_s[...] = (qhat * qag_ref[...]).astype(BF16)
            kvn_s[...] = (kvhat * kvag_ref[...]).astype(BF16)

        cos, sin = cos_ref[...], sin_ref[...]
        qhat, _ = _head_norm(_dot(qn_s[...], qb_ref[...]))
        q_ref[...] = _rope(qhat * qg_ref[...], cos, sin).astype(BF16)
        kv = _dot(kvn_s[...], kvb_ref[...])
        khat, _ = _head_norm(jnp.concatenate([kv[:, :QK_NOPE], p_ref[:, ODD_IN_PAD - LANES:]], axis=1))
        k_ref[...] = _rope(khat * kg_ref[...], cos, sin).astype(BF16)
        v_ref[...] = kv[:, QK_NOPE:].astype(BF16)

    qk_spec = BS((None, tm, QK_PAD), lambda i, h: (h, i, 0))
    return _call(body, name="mla_qkv_fwd", grid=(T // tm, HEADS), in_specs=_mla_specs(tm),
                 out_specs=[qk_spec, qk_spec, BS((None, tm, V_DIM), lambda i, h: (h, i, 0))],
                 out_shape=[_sds((HEADS, T, QK_PAD), BF16), _sds((HEADS, T, QK_PAD), BF16), _sds((HEADS, T, V_DIM), BF16)],
                 scratch=[pltpu.VMEM((tm, Q_LORA), BF16), pltpu.VMEM((tm, KV_LORA), BF16)])(
        proj, cos, sin_signed, qa_g, kva_g, q_b, kv_b, q_g, k_g)


def _mla_qkv_bwd(proj, cos, sin_signed, qa_g, kva_g, q_b, kv_b, q_g, k_g, dq, dk, dv, dz_pool, tm):
    T = proj.shape[0]
    n_tiles = T // tm
    chain_rows = min(_PASS_ROWS, tm)

    def body(p_ref, cos_ref, sin_ref, qag_ref, kvag_ref, qb_ref, kvb_ref, qg_ref, kg_ref, dq_ref, dk_ref, dv_ref, dzp_ref,
             dp_ref, dqb_ref, dkvb_ref, dqg_ref, dkg_ref, dqag_ref, dkvag_ref, qn_s, kvn_s, dqn_s, dkvn_s, dkr_s,
             qh_s, kv_s, dqh_s, dkv_s):
        i, h = pl.program_id(0), pl.program_id(1)

        @pl.when((i == 0) & (h == 0))
        def _():
            for ref in (dqb_ref, dkvb_ref, dqg_ref, dkg_ref, dqag_ref, dkvag_ref):
                ref[...] = jnp.zeros_like(ref)

        @pl.when(h == 0)
        def _():
            qhat, _, kvhat, _ = _latents(p_ref, qag_ref, kvag_ref)
            qn_s[...] = (qhat * qag_ref[...]).astype(BF16)
            kvn_s[...] = (kvhat * kvag_ref[...]).astype(BF16)
            dqn_s[...] = jnp.zeros_like(dqn_s)
            dkvn_s[...] = jnp.zeros_like(dkvn_s)
            dkr_s[...] = jnp.zeros_like(dkr_s)

        qh_s[...] = _dot(qn_s[...], qb_ref[...])
        kv_s[...] = _dot(kvn_s[...], kvb_ref[...])
        qg, kg = qg_ref[...], kg_ref[...]

        def chunk(c, gains):
            dqg, dkg = gains
            rows = slice(c * chain_rows, (c + 1) * chain_rows)
            cos, sin = cos_ref[rows, :], sin_ref[rows, :]
            qhat, rq = _head_norm(qh_s[rows, :])
            dqn_head = _rope_transposed(dq_ref[rows, :], cos, sin)
            dqh_s[rows, :] = _head_norm_bwd(dqn_head, qhat, rq, qg).astype(BF16)
            kv = kv_s[rows, :]
            khat, rk = _head_norm(jnp.concatenate([kv[:, :QK_NOPE], p_ref[rows, ODD_IN_PAD - LANES:]], axis=1))
            dkn_head = _rope_transposed(dk_ref[rows, :], cos, sin)
            dkf = _head_norm_bwd(dkn_head, khat, rk, kg)
            dkr_s[rows, :] += dkf[:, QK_NOPE:]
            dkv_s[rows, :] = jnp.concatenate([dkf[:, :QK_NOPE], dv_ref[rows, :]], axis=1).astype(BF16)
            return dqg + dqn_head * qhat, dkg + dkn_head * khat

        dqg = dkg = jnp.zeros((chain_rows, QK_PAD), F32)
        for c in range(tm // chain_rows):
            dqg, dkg = chunk(c, (dqg, dkg))
        dqg_ref[...] += jnp.sum(dqg, axis=0, keepdims=True)
        dkg_ref[...] += jnp.sum(dkg, axis=0, keepdims=True)
        dqb_ref[h] += _dot(qn_s[...], dqh_s[...], "tn")
        dqn_s[...] += _dot(dqh_s[...], qb_ref[...], "nt")
        dkvb_ref[h] += _dot(kvn_s[...], dkv_s[...], "tn")
        dkvn_s[...] += _dot(dkv_s[...], kvb_ref[...], "nt")

        @pl.when(h == HEADS - 1)
        def _():
            qhat_l, rql, kvhat_l, rkvl = _latents(p_ref, qag_ref, kvag_ref)
            dqn, dkvn = dqn_s[...], dkvn_s[...]
            dqag_ref[...] += jnp.sum(dqn * qhat_l, axis=0, keepdims=True)
            dkvag_ref[...] += jnp.sum(dkvn * kvhat_l, axis=0, keepdims=True)
            dqx, dkvx = dqn * qag_ref[...], dkvn * kvag_ref[...]
            dp_ref[:, :POOL_WIDTH] = dzp_ref[...]
            dp_ref[:, POOL_WIDTH:POOL_WIDTH + Q_LORA] = (
                rql * (dqx - qhat_l * jnp.mean(dqx * qhat_l, axis=-1, keepdims=True))).astype(BF16)
            dp_ref[:, POOL_WIDTH + Q_LORA:ODD_IN_PAD - LANES] = (
                rkvl * (dkvx - kvhat_l * jnp.mean(dkvx * kvhat_l, axis=-1, keepdims=True))).astype(BF16)
            dp_ref[:, ODD_IN_PAD - LANES:] = dkr_s[:, :QK_ROPE].astype(BF16)

    full = lambda shape: BS(shape, lambda i, h: (0,) * len(shape))
    qk_spec = BS((None, tm, QK_PAD), lambda i, h: (h, i, 0))
    return _call(body, name="mla_qkv_bwd", grid=(n_tiles, HEADS),
                 in_specs=_mla_specs(tm) + [qk_spec, qk_spec, BS((None, tm, V_DIM), lambda i, h: (h, i, 0)),
                                            BS((tm, POOL_WIDTH), lambda i, h: (i, 0))],
                 out_specs=[BS((tm, ODD_IN), lambda i, h: (i, 0)), full((HEADS, Q_LORA, QK_PAD)), full((HEADS, KV_LORA, QK_PAD)),
                            full((1, QK_PAD)), full((1, QK_PAD)), full((1, Q_LORA)), full((1, KV_LORA))],
                 out_shape=[_sds((T, ODD_IN), BF16),_sds((HEADS, Q_LORA, QK_PAD), F32), _sds((HEADS, KV_LORA, QK_PAD), F32),
                            _sds((1, QK_PAD), F32), _sds((1, QK_PAD), F32), _sds((1, Q_LORA), F32), _sds((1, KV_LORA), F32)],
                 scratch=[pltpu.VMEM((tm, Q_LORA), BF16), pltpu.VMEM((tm, KV_LORA), BF16), pltpu.VMEM((tm, Q_LORA), F32),
                          pltpu.VMEM((tm, KV_LORA), F32), pltpu.VMEM((tm, LANES), F32), pltpu.VMEM((tm, QK_PAD), F32),
                          pltpu.VMEM((tm, QK_PAD), F32), pltpu.VMEM((tm, QK_PAD), BF16), pltpu.VMEM((tm, QK_PAD), BF16)])(
        proj, cos, sin_signed, qa_g, kva_g, q_b, kv_b, q_g, k_g, dq, dk, dv, dz_pool)


_SCALE_LOG2E = ATTN_SCALE * math.log2(math.e)


def _attn_tile(seq):
    return 512 if seq % 512 == 0 else seq


def _causal_mask(s):
    row = lax.broadcasted_iota(jnp.int32, s.shape, 0)
    col = lax.broadcasted_iota(jnp.int32, s.shape, 1)
    return jnp.where(row >= col, s, NEG_INF)


def _tile(i, t):
    return slice(i * t, (i + 1) * t)


def _flash_fwd(q, k, v, mix, batch, seq):
    t = _attn_tile(seq)
    nq = seq // t

    def body(q_ref, k_ref, v_ref, _, o_ref, lse_ref):
        for qi in range(nq):
            rows, before = _tile(qi, t), slice(0, qi * t)
            qv = q_ref[rows, :]
            s_diag = _causal_mask(_dot(qv, k_ref[rows, :], "nt"))
            m = jnp.max(s_diag, axis=-1, keepdims=True)
            if qi:
                s_before = _dot(qv, k_ref[before, :], "nt")
                m = jnp.maximum(m, jnp.max(s_before, axis=-1, keepdims=True))
            p = jnp.exp2((s_diag - m) * _SCALE_LOG2E)
            l = jnp.sum(p, axis=-1, keepdims=True)
            acc = _dot(p, v_ref[rows, :])
            if qi:
                p = jnp.exp2((s_before - m) * _SCALE_LOG2E)
                l = l + jnp.sum(p, axis=-1, keepdims=True)
                acc = acc + _dot(p, v_ref[before, :])
            o_ref[rows, :] = (acc / l).astype(BF16)
            lse_ref[rows, :] = jnp.broadcast_to(m * ATTN_SCALE + jnp.log(l), (t, LANES))

    T = batch * seq
    whole = lambda w: BS((None, seq, w), lambda b, h: (h, b, 0))
    return _call(body, name="flash_fwd", grid=(batch, HEADS),
                 in_specs=[whole(QK_PAD), whole(QK_PAD), whole(V_DIM), pl.BlockSpec(memory_space=pl.ANY)],
                 out_specs=[BS((seq, V_DIM), lambda b, h: (b, POOL_WIDTH // V_DIM + h)), whole(LANES)],
                 out_shape=[_sds((T, D_MODEL), BF16), _sds((HEADS, T, LANES), F32)],
                 aliases={3: 0})(q, k, v, mix)


def _flash_bwd(q, k, v, dmix, mix, lse, batch, seq):
    t = _attn_tile(seq)
    nq = seq // t

    def body(q_ref, k_ref, v_ref, do_ref, o_ref, lse_ref, dq_ref, dk_ref, dv_ref):
        for qi in range(nq):
            rows, before = _tile(qi, t), slice(0, qi * t)
            qv, do = q_ref[rows, :], do_ref[rows, :]
            lse2 = lse_ref[rows, 0:1] * math.log2(math.e)
            delta = jnp.sum(do.astype(F32) * o_ref[rows, :].astype(F32), axis=-1, keepdims=True)

            def block(keys, masked):
                kk = k_ref[keys, :]
                s = _dot(qv, kk, "nt")
                p = jnp.exp2((_causal_mask(s) if masked else s) * _SCALE_LOG2E - lse2)
                ds = p * (_dot(do, v_ref[keys, :], "nt") - delta)
                return _dot(p, do, "tn"), _dot(ds, qv, "tn") * ATTN_SCALE, _dot(ds, kk) * ATTN_SCALE

            dv_ref[rows, :], dk_ref[rows, :], dq = block(rows, True)
            if qi:
                dv, dk, dq_before = block(before, False)
                dv_ref[before, :] += dv
                dk_ref[before, :] += dk
                dq = dq + dq_before
            dq_ref[rows, :] = dq

    T = batch * seq
    whole = lambda w: BS((None, seq, w), lambda b, h: (h, b, 0))
    head_cols = BS((seq, V_DIM), lambda b, h: (b, POOL_WIDTH // V_DIM + h))
    return _call(body, name="flash_bwd", grid=(batch, HEADS),
                 in_specs=[whole(QK_PAD), whole(QK_PAD), whole(V_DIM), head_cols, head_cols, whole(LANES)],
                 out_specs=[whole(QK_PAD), whole(QK_PAD), whole(V_DIM)],
                 out_shape=[_sds((HEADS, T, QK_PAD), F32), _sds((HEADS, T, QK_PAD), F32), _sds((HEADS, T, V_DIM), F32)])(
        q, k, v, dmix, mix, lse)


def _adamw_math(w, g, m, v):
    m = ADAM_B1 * m + (1.0 - ADAM_B1) * g
    v = ADAM_B2 * v + (1.0 - ADAM_B2) * (g * g)
    m_hat = m / (1.0 - ADAM_B1 ** ADAM_STEP)
    v_hat = v / (1.0 - ADAM_B2 ** ADAM_STEP)
    return -ADAM_LR * (m_hat / (jnp.sqrt(v_hat) + ADAM_EPS) + ADAM_WD * w), m, v


def _adamw(name, w, g, m, v, l=0, prev=()):
    L, R, C = w.shape
    tr = 256 if R % 256 == 0 else R

    def body(w_ref, g_ref, m_ref, v_ref, *rest):
        go_ref, d_ref, mo_ref, vo_ref, token = rest[-5:]
        gv = g_ref[...]
        d_ref[...], mo_ref[...], vo_ref[...] = _adamw_math(w_ref[...], gv, m_ref[...], v_ref[...])
        go_ref[...] = gv
        token[...] = jnp.zeros_like(token)

    layer = BS((None, tr, C), lambda i: (l, i, 0))
    return _call(body, name=f"{name}_{l}", grid=(R // tr,),
                 in_specs=[layer, BS((tr, C), lambda i: (i, 0)), layer, layer] + [pl.BlockSpec(memory_space=pl.ANY)] * len(prev),
                 out_specs=[layer] * 4 + [BS((SUBLANES, LANES), lambda i: (0, 0))],
                 out_shape=[_sds((L, R, C), F32)] * 4 + [_sds((SUBLANES, LANES), F32)],
                 aliases={4 + n: n for n in range(len(prev))})(w, g, m, v, *prev)


def _place():
    x, y, c = lax.axis_index("x"), lax.axis_index("y"), lax.axis_index("c")
    other_chips = [(1 - x, y), (x, 1 - y), (1 - x, 1 - y)]
    return x, y, c, other_chips


def _remote(src, dst, send_sem, recv_sem, dev):
    return pltpu.make_async_remote_copy(src_ref=src, dst_ref=dst, send_sem=send_sem, recv_sem=recv_sem,
                                        device_id=dev, device_id_type=MESH)


def _prefetch_call(body, *, name, grid, in_specs, out_specs, out_shape):
    grid_spec = pltpu.PrefetchScalarGridSpec(num_scalar_prefetch=1, grid=grid, in_specs=in_specs, out_specs=out_specs)
    params = pltpu.CompilerParams(vmem_limit_bytes=VMEM_LIMIT_V7X, dimension_semantics=("arbitrary",) * len(grid))
    return pl.pallas_call(body, name=name, grid_spec=grid_spec, out_shape=out_shape, compiler_params=params)


def _row_tile(rows):
    return 256 if rows % 256 == 0 else rows


def _cast_place(name, w, layer, chip, after=()):
    _, _, rows, C = w.shape
    tr = _row_tile(rows)

    def body(chip_ref, w_ref, *rest):
        rest[-1][...] = w_ref[...].astype(BF16)

    return _prefetch_call(body, name=name, grid=(2, rows // tr),
                          in_specs=[BS((None, None, tr, C), lambda h, i, chip_ref: (layer, h, i, 0))]
                          + [pl.BlockSpec(memory_space=pl.ANY)] * len(after),
                          out_specs=BS((None, None, tr, C), lambda h, i, chip_ref: (chip_ref[0], h, i, 0)),
                          out_shape=pltpu.HBM((N_CHIPS, 2, rows, C), BF16))(chip, w, *after)


SEM = pl.BlockSpec(memory_space=pltpu.SEMAPHORE)


def _split_copy_call(body, *, name, in_specs, out_specs, out_shape, aliases):
    return pl.pallas_call(body, name=name, in_specs=in_specs, out_specs=out_specs, out_shape=out_shape,
                          input_output_aliases=aliases,
                          compiler_params=pltpu.CompilerParams(has_side_effects=pltpu.SideEffectType.DATAFLOW_SIDE_EFFECTING))


def _hbm(arrays):
    return [pltpu.with_memory_space_constraint(a, pltpu.HBM) for a in arrays]


def _gather_send(name, gs, groups, after):
    n = len(gs)

    def body(*refs):
        g, sems, token = refs[:n], refs[n + len(after):n + len(after) + 2 * len(groups)], refs[-1]
        x, y, c, chips = _place()
        me = 2 * x + y
        for gi, members in enumerate(groups):
            for a, i in enumerate(members):
                for k, (px, py) in enumerate(chips):
                    _remote(g[i].at[me, c], g[i].at[me, c], sems[2 * gi].at[3 * a + k], sems[2 * gi + 1].at[3 * a + k],
                            (px, py, c)).start()
        token[...] = jnp.zeros_like(token)

    sem_shapes = [pltpu.SemaphoreType.DMA((3 * len(members),)) for members in groups for _ in range(2)]
    out = _split_copy_call(body, name=name, in_specs=[HBM] * n + [pl.BlockSpec(memory_space=pl.ANY)] * len(after),
                           out_specs=[SEM] * len(sem_shapes) + [HBM] * n + [VMEM],
                           out_shape=sem_shapes + [pltpu.HBM(a.shape, a.dtype) for a in gs] + [_sds((SUBLANES, LANES), F32)],
                           aliases={i: len(sem_shapes) + i for i in range(n)})(*_hbm(gs), *after)
    return out[:len(sem_shapes)], out[len(sem_shapes):-1], out[-1]


def _gather_wait(name, gs, send_sems, recv_sems, after):
    n = len(gs)

    def body(*refs):
        g, ssem, rsem = refs[:n], refs[n], refs[n + 1]
        x, y, c, chips = _place()
        me = 2 * x + y
        for a in range(n):
            for k, (px, py) in enumerate(chips):
                landed = g[a].at[2 * px + py, c]
                cp = _remote(g[a].at[me, c], landed, ssem.at[3 * a + k], rsem.at[3 * a + k], (px, py, c))
                cp.wait_recv()
                cp.wait_send()

    return _split_copy_call(body, name=name, in_specs=[HBM] * n + [SEM, SEM] + [pl.BlockSpec(memory_space=pl.ANY)] * len(after),
                            out_specs=[HBM] * n, out_shape=[pltpu.HBM(a.shape, a.dtype) for a in gs],
                            aliases={i: i for i in range(n)})(*gs, send_sems, recv_sems, *after)


def _gather_pass(name, gs):
    n = len(gs)

    def body(*refs):
        g, send_sems, recv_sems = refs[n:2 * n], refs[-2], refs[-1]
        x, y, c, chips = _place()
        sibling = (x, y, 1 - c)
        passed = [_remote(g[i].at[2 * px + py, c], g[i].at[2 * px + py, c], send_sems.at[3 * i + k], recv_sems.at[3 * i + k], sibling)
                  for i in range(n) for k, (px, py) in enumerate(chips)]
        for cp in passed:
            cp.start()
        for i in range(n):
            for k, (px, py) in enumerate(chips):
                theirs = g[i].at[2 * px + py, 1 - c]
                _remote(theirs, theirs, send_sems.at[3 * i + k], recv_sems.at[3 * i + k], sibling).wait_recv()
        for cp in passed:
            cp.wait_send()

    return _call(body, name=name, in_specs=[HBM] * n, out_specs=[HBM] * n, out_shape=[_sds(a.shape, a.dtype) for a in gs],
                 aliases={i: i for i in range(n)},
                 scratch=[pltpu.SemaphoreType.DMA((3 * n,)), pltpu.SemaphoreType.DMA((3 * n,))])(*gs)


def _scatter_send(name, ps):
    n = len(ps)

    def body(*refs):
        p, r, ssem, rsem, token = refs[:n], refs[n:2 * n], refs[2 * n], refs[2 * n + 1], refs[-1]
        x, y, c, chips = _place()
        for i in range(n):
            for k, (px, py) in enumerate(chips):
                _remote(p[i].at[2 * px + py], r[i].at[k], ssem.at[3 * i + k], rsem.at[3 * i + k], (px, py, c)).start()
        token[...] = jnp.zeros_like(token)

    lands = [lax.empty((N_CHIPS - 1,) + a.shape[1:], a.dtype) for a in ps]
    sem = pltpu.SemaphoreType.DMA((3 * n,))
    out = _split_copy_call(body, name=name, in_specs=[HBM] * (2 * n), out_specs=[SEM, SEM] + [HBM] * (2 * n) + [VMEM],
                           out_shape=[sem, sem] + [pltpu.HBM(a.shape, a.dtype) for a in list(ps) + lands] + [_sds((SUBLANES, LANES), F32)],
                           aliases={i: 2 + i for i in range(2 * n)})(*_hbm(list(ps) + lands))
    return out[0], out[1], out[2:2 + n], out[2 + n:2 + 2 * n], out[-1]


def _scatter_wait(name, ps, lands, send_sems, recv_sems, after):
    n = len(ps)

    def body(*refs):
        p, r, ssem, rsem = refs[:n], refs[n:2 * n], refs[2 * n], refs[2 * n + 1]
        x, y, c, chips = _place()
        for i in range(n):
            for k, (px, py) in enumerate(chips):
                cp = _remote(p[i].at[2 * px + py], r[i].at[k], ssem.at[3 * i + k], rsem.at[3 * i + k], (px, py, c))
                cp.wait_recv()
                cp.wait_send()

    out = _split_copy_call(body, name=name, in_specs=[HBM] * (2 * n) + [SEM, SEM] + [pl.BlockSpec(memory_space=pl.ANY)] * len(after),
                           out_specs=[HBM] * (2 * n), out_shape=[pltpu.HBM(a.shape, a.dtype) for a in list(ps) + list(lands)],
                           aliases={i: i for i in range(2 * n)})(*ps, *lands, send_sems, recv_sems, *after)
    return out[:n], out[n:]


def _exchange_send(name, gs):
    n = len(gs)

    def body(*refs):
        g, r, ssem, rsem, token = refs[:n], refs[n:2 * n], refs[2 * n], refs[2 * n + 1], refs[-1]
        x, y, c, _ = _place()
        for i in range(n):
            _remote(g[i].at[:, 1 - c], r[i], ssem.at[i], rsem.at[i], (x, y, 1 - c)).start()
        token[...] = jnp.zeros_like(token)

    lands = [lax.empty((a.shape[0],) + a.shape[2:], a.dtype) for a in gs]
    sem = pltpu.SemaphoreType.DMA((n,))
    out = _split_copy_call(body, name=name, in_specs=[HBM] * (2 * n), out_specs=[SEM, SEM] + [HBM] * (2 * n) + [VMEM],
                           out_shape=[sem, sem] + [pltpu.HBM(a.shape, a.dtype) for a in list(gs) + lands] + [_sds((SUBLANES, LANES), F32)],
                           aliases={i: 2 + i for i in range(2 * n)})(*_hbm(list(gs) + lands))
    return out[0], out[1], out[2:2 + n], out[2 + n:2 + 2 * n], out[-1]


def _exchange_wait(name, gs, lands, send_sems, recv_sems, after):
    n = len(gs)

    def body(*refs):
        g, r, ssem, rsem = refs[:n], refs[n:2 * n], refs[2 * n], refs[2 * n + 1]
        x, y, c, _ = _place()
        for i in range(n):
            cp = _remote(g[i].at[:, 1 - c], r[i], ssem.at[i], rsem.at[i], (x, y, 1 - c))
            cp.wait_recv()
            cp.wait_send()

    out = _split_copy_call(body, name=name, in_specs=[HBM] * (2 * n) + [SEM, SEM] + [pl.BlockSpec(memory_space=pl.ANY)] * len(after),
                           out_specs=[HBM] * (2 * n), out_shape=[pltpu.HBM(a.shape, a.dtype) for a in list(gs) + list(lands)],
                           aliases={i: i for i in range(2 * n)})(*gs, *lands, send_sems, recv_sems, *after)
    return out[:n], out[n:]


def _share_send(name, fs, after):
    n = len(fs)

    def body(*refs):
        f, ssem, rsem, token = refs[:n], refs[n + len(after)], refs[n + len(after) + 1], refs[-1]
        x, y, c, _ = _place()
        for i in range(n):
            _remote(f[i].at[c], f[i].at[c], ssem.at[i], rsem.at[i], (x, y, 1 - c)).start()
        token[...] = jnp.zeros_like(token)

    sem = pltpu.SemaphoreType.DMA((n,))
    out = _split_copy_call(body, name=name, in_specs=[HBM] * n + [pl.BlockSpec(memory_space=pl.ANY)] * len(after),
                           out_specs=[SEM, SEM] + [HBM] * n + [VMEM],
                           out_shape=[sem, sem] + [pltpu.HBM(a.shape, a.dtype) for a in fs] + [_sds((SUBLANES, LANES), F32)],
                           aliases={i: 2 + i for i in range(n)})(*_hbm(fs), *after)
    return out[0], out[1], out[2:2 + n], out[-1]


def _share_wait(name, fs, send_sems, recv_sems, after):
    n = len(fs)

    def body(*refs):
        f, ssem, rsem = refs[:n], refs[n], refs[n + 1]
        x, y, c, _ = _place()
        for i in range(n):
            _remote(f[i].at[c], f[i].at[1 - c], ssem.at[i], rsem.at[i], (x, y, 1 - c)).wait_recv()
            _remote(f[i].at[c], f[i].at[c], ssem.at[i], rsem.at[i], (x, y, 1 - c)).wait_send()

    return _split_copy_call(body, name=name, in_specs=[HBM] * n + [SEM, SEM] + [pl.BlockSpec(memory_space=pl.ANY)] * len(after),
                            out_specs=[HBM] * n, out_shape=[pltpu.HBM(a.shape, a.dtype) for a in fs],
                            aliases={i: i for i in range(n)})(*fs, send_sems, recv_sems, *after)


def _all_reduce_small(name, v):
    rows = v.shape[0] // 2
    halves = (2, rows, LANES)

    def body(v_ref, o_ref, from_sibling, chip_sums, send_sems, recv_sems):
        x, y, c, chips = _place()
        me, sibling = 2 * x + y, (x, y, 1 - c)
        swap = _remote(v_ref.at[1 - c], from_sibling, send_sems.at[0], recv_sems.at[0], sibling)
        swap.start()
        swap.wait()
        chip_sums[me] = v_ref[c] + from_sibling[...]
        sends = [_remote(chip_sums.at[me], chip_sums.at[me], send_sems.at[1 + k], recv_sems.at[1 + k], (px, py, c))
                 for k, (px, py) in enumerate(chips)]
        for cp in sends:
            cp.start()
        for k, (px, py) in enumerate(chips):
            theirs = chip_sums.at[2 * px + py]
            _remote(theirs, theirs, send_sems.at[1 + k], recv_sems.at[1 + k], (px, py, c)).wait_recv()
        for cp in sends:
            cp.wait_send()
        acc = chip_sums[0]
        for j in range(1, N_CHIPS):
            acc = acc + chip_sums[j]
        o_ref[c] = acc
        share = _remote(o_ref.at[c], o_ref.at[c], send_sems.at[4], recv_sems.at[4], sibling)
        share.start()
        share.wait_send()
        _remote(o_ref.at[1 - c], o_ref.at[1 - c], send_sems.at[4], recv_sems.at[4], sibling).wait_recv()

    return _call(body, name=name, in_specs=[VMEM], out_specs=VMEM, out_shape=_sds(halves, F32),
                 scratch=[pltpu.VMEM((rows, LANES), F32), pltpu.VMEM((N_CHIPS, rows, LANES), F32),
                          pltpu.SemaphoreType.DMA((5,)), pltpu.SemaphoreType.DMA((5,))])(v.reshape(halves)).reshape(v.shape)


def _add_halves(name, g, r, c):
    _, _, rows, C = g.shape
    tr = _row_tile(rows)

    def body(c_ref, g_ref, r_ref, o_ref):
        o_ref[...] = (g_ref[...].astype(F32) + r_ref[...].astype(F32)).astype(BF16)

    spec = BS((None, tr, C), lambda j, i, c_ref: (j, i, 0))
    return _prefetch_call(body, name=name, grid=(N_CHIPS, rows // tr),
                          in_specs=[BS((None, None, tr, C), lambda j, i, c_ref: (j, c_ref[0], i, 0)), spec], out_specs=spec,
                          out_shape=pltpu.HBM((N_CHIPS, rows, C), BF16))(c, g, r)


def _sum_partials(name, p, r, chip_c):
    _, rows, C = p.shape
    tr = _row_tile(rows)

    def body(s_ref, p_ref, r_ref, o_ref):
        acc = p_ref[...].astype(F32)
        for k in range(N_CHIPS - 1):
            acc = acc + r_ref[k].astype(F32)
        o_ref[...] = acc

    return _prefetch_call(body, name=name, grid=(rows // tr,),
                          in_specs=[BS((None, tr, C), lambda i, s: (s[0], i, 0)), BS((N_CHIPS - 1, tr, C), lambda i, s: (0, i, 0))],
                          out_specs=BS((None, tr, C), lambda i, s: (s[1], i, 0)), out_shape=pltpu.HBM((2, rows, C), F32))(chip_c, p, r)


_SHARDED = ("even_w_in", "even_w_out", "odd_w_in", "q_b", "kv_b", "odd_w_out", "ffn_w_gate", "ffn_w_up", "ffn_w_down")
_REPLICATED = ("mix_norm", "ffn_norm", "sg_ln_g", "sg_w_s", "sg_b_s", "pool_w", "q_norm", "k_norm")
_SMALL_SHARDED = ("sc_conv_w", "pool_scale", "q_a_norm", "kv_a_norm")
_WEIGHTS = ("mix_norm", "ffn_norm", "even_w_in", "sg_ln_g", "sg_w_s", "sg_b_s", "sc_conv_w", "even_w_out", "odd_w_in", "pool_w",
            "pool_scale", "q_a_norm", "q_b", "kv_a_norm", "kv_b", "q_norm", "k_norm", "odd_w_out", "ffn_w_gate", "ffn_w_up",
            "ffn_w_down")


def _pad_rows(flat, width, align):
    n = flat.shape[0]
    rows = -(-n // (width * align)) * align
    return jnp.pad(flat, (0, rows * width - n)).reshape(rows, width)


_GROUPS = {"even": ("even_w_in", "even_w_out"),
           "ffn0": ("ffn_w_gate0", "ffn_w_up0", "ffn_w_down0"),
           "odd": ("odd_w_in", "q_b", "kv_b", "odd_w_out"),
           "ffn1": ("ffn_w_gate1", "ffn_w_up1", "ffn_w_down1")}


def _place_shards(shards, names, chip, after):
    placed = []
    for n in names:
        weight, layer = (n[:-1], int(n[-1])) if n[-1].isdigit() else (n, 0)
        a = shards[weight]
        placed.append(_cast_place(f"place_{n}", a.reshape(a.shape[0], 2, a.shape[1] // 2, a.shape[2]), layer, chip, after))
    return placed


def _whole_weights(gathered):
    out = {n: a.reshape(N_CHIPS, -1, a.shape[-1]) for n, a in gathered.items()}
    for n in ("q_b", "kv_b"):
        if n in out:
            out[n] = out[n].transpose(1, 0, 2).reshape(out[n].shape[1], -1)
    for n in ("even_w_out", "odd_w_in", "odd_w_out"):
        if n in out:
            out[n] = out[n].reshape(-1, out[n].shape[-1])
    return out


def _forward_backward(x, positions, target, small, fetch, emit, advance):
    batch, seq, _ = x.shape
    T = batch * seq
    tm = _token_tile(seq)
    x0 = x.reshape(T, D_MODEL)

    inv_freq = ROPE_THETA ** (-jnp.arange(0, QK_ROPE, 2, dtype=F32) / QK_ROPE)
    ang = (positions.astype(F32)[..., None] * inv_freq).reshape(T, QK_ROPE // 2)
    cos, sin = jnp.cos(ang), jnp.sin(ang)
    pad = jnp.zeros((T, LANES - QK_ROPE), F32)
    cos_t = jnp.concatenate([cos, cos, pad], axis=1)
    sin_t = jnp.concatenate([-sin, sin, pad], axis=1)

    tril = jnp.tril(jnp.ones((SG_CHUNK, SG_CHUNK), bool))
    w_tril = jnp.where(tril[None], small["sg_w_s"][0], 0.0).astype(BF16)
    b_lanes = jnp.broadcast_to(small["sg_b_s"][0][:, :, None], (SG_HEADS, SG_CHUNK, SG_DIM))
    conv_w = jnp.pad(small["sc_conv_w"][0], ((0, SUBLANES - CONV_TAPS), (0, 0)))
    ln_g = small["sg_ln_g"]
    pool_diag = jnp.zeros((POOL_WIDTH, POOL_WIDTH), F32)
    for g in range(len(POOL_WINDOWS)):
        pool_diag = pool_diag.at[POOL_DIM * g:POOL_DIM * (g + 1), POOL_DIM * g:POOL_DIM * (g + 1)].set(small["pool_w"][0, g])
    pool_diag = pool_diag.astype(BF16)
    pool_scale = small["pool_scale"]
    q_g = jnp.pad(small["q_norm"], ((0, 0), (0, QK_PAD - QK_DIM)))
    k_g = jnp.pad(small["k_norm"], ((0, 0), (0, QK_PAD - QK_DIM)))
    qa_g, kva_g = small["q_a_norm"], small["kv_a_norm"]
    in_shard = EVEN_IN // N_CHIPS

    def ffn_weights(l, w):
        return w[f"ffn_w_gate{l}"], w[f"ffn_w_up{l}"], w[f"ffn_w_down{l}"]

    W = fetch("even", ())
    w_in_even = W["even_w_in"]
    tb = _big_tile(T)
    proj0, h0 = _even_in(x0, small["mix_norm"][0], w_in_even, _resident_tile(T))
    mix0 = _even_mixer_fwd(proj0, ln_g, w_tril, b_lanes, conv_w, seq, tm)
    w_out_even = W["even_w_out"]
    x1, h1 = _mm("even_out", "nn", mix0, w_out_even, F32, tk=1024, add=x0, fused=_norm_tail(small["ffn_norm"][0], T, tb))
    ffn0 = ffn_weights(0, fetch("ffn0", (x1,)))
    (x2, h2), ffn0_saved = _ffn_fwd(0, x1, h1, *ffn0, lambda tile: _norm_tail(small["mix_norm"][1], T, tile))
    W = fetch("odd", (x2,))
    w_in_odd = jnp.pad(W["odd_w_in"], ((0, 0), (0, ODD_IN_PAD - ODD_IN)))
    q_b = jnp.pad(W["q_b"].reshape(Q_LORA, HEADS, QK_DIM).transpose(1, 0, 2), ((0, 0), (0, 0), (0, QK_PAD - QK_DIM)))
    kv_b = W["kv_b"].reshape(KV_LORA, HEADS, QK_NOPE + V_DIM).transpose(1, 0, 2)
    proj1 = _mm("odd_in", "nn", h2, w_in_odd, F32, tk=1024)
    mix1 = _pool_fwd(proj1, pool_diag, pool_scale, seq, tm)
    q, k, v = _mla_qkv_fwd(proj1, cos_t, sin_t, qa_g, kva_g, q_b, kv_b, q_g, k_g, tm)
    mix1, lse = _flash_fwd(q, k, v, mix1, batch, seq)
    x3, h3 = _mm("odd_out", "nn", mix1, W["odd_w_out"], F32, tk=1024, add=x2, fused=_norm_tail(small["ffn_norm"][1], T, tb))
    ffn1 = ffn_weights(1, fetch("ffn1", (x3,)))
    (dy, sq), ffn1_saved = _ffn_fwd(1, x3, h3, *ffn1, lambda tile: _loss_tail(target.reshape(T, D_MODEL), tile))

    G = {}
    dx3, dffn_g1 = _ffn_bwd(1, x3, small["ffn_norm"][1], *ffn1, ffn1_saved, dy, emit)
    dmix1 = _mm("odd_out_dx", "nt", dx3, W["odd_w_out"], BF16, tk=1024, after=advance(dx3))
    dw_out_odd = _mm("odd_out_dw", "tn", mix1, dx3, BF16, hbm_out=True)
    dq, dk, dv = _flash_bwd(q, k, v, dmix1, mix1, lse, batch, seq)
    dz_pool, dpool_diag, G["pool_scale"] = _pool_bwd(proj1, dmix1, pool_diag, pool_scale, seq, tm)
    dproj1, dq_b, dkv_b, dq_g, dk_g, G["q_a_norm"], G["kv_a_norm"] = _mla_qkv_bwd(
        proj1, cos_t, sin_t, qa_g, kva_g, q_b, kv_b, q_g, k_g, dq, dk, dv, dz_pool, tm)
    G["pool_w"] = jnp.stack([dpool_diag[POOL_DIM * g:POOL_DIM * (g + 1), POOL_DIM * g:POOL_DIM * (g + 1)]
                             for g in range(len(POOL_WINDOWS))])[None]
    G["q_norm"], G["k_norm"] = dq_g[:, :QK_DIM], dk_g[:, :QK_DIM]
    dw_in_odd = _mm("odd_in_dw", "tn", h2, dproj1, BF16, tn=ODD_IN, hbm_out=True)

    def shard_major(g, cols):
        return g.reshape(g.shape[0], N_CHIPS, cols).transpose(1, 0, 2).astype(BF16)

    behind = emit("odd", {"odd_w_in": dw_in_odd.reshape(N_CHIPS, -1, ODD_IN),
                          "q_b": shard_major(dq_b[:, :, :QK_DIM].transpose(1, 0, 2).reshape(Q_LORA, HEADS * QK_DIM), HEADS * QK_DIM // N_CHIPS),
                          "kv_b": shard_major(dkv_b.transpose(1, 0, 2).reshape(KV_LORA, HEADS * (QK_NOPE + V_DIM)),
                                              HEADS * (QK_NOPE + V_DIM) // N_CHIPS),
                          "odd_w_out": dw_out_odd.reshape(N_CHIPS, -1, D_MODEL)})
    dx2, dmix_g1 = _mm("odd_in_dx", "nt", dproj1, W["odd_w_in"], F32, tk=ODD_IN, after=behind,
                       fused=_norm_bwd_tail(x2, small["mix_norm"][1], dx3, tb))
    dx1, dffn_g0 = _ffn_bwd(0, x1, small["ffn_norm"][0], *ffn0, ffn0_saved, dx2, emit, after=advance(dx2))
    dmix0 = _mm("even_out_dx", "nt", dx1, w_out_even, F32, tk=1024, after=advance(dx1))
    dw_out_even = _mm("even_out_dw", "tn", mix0, dx1, BF16, hbm_out=True)
    dproj0, dw_s, db_lanes, G["sg_ln_g"], dconv = _even_mixer_bwd(proj0, dmix0, ln_g, w_tril, b_lanes, conv_w, seq, tm)
    G["sg_w_s"] = dw_s[None]
    G["sg_b_s"] = jnp.sum(db_lanes, axis=-1)[None]
    G["sc_conv_w"] = dconv[None, :CONV_TAPS]
    tr = _resident_tile(T)
    tail, shapes, specs = _norm_bwd_tail(x0, small["mix_norm"][0], dx1, tr)
    dx0, dmix_g0 = _matmul("even_in_dx", "nt", [(dproj0, w_in_even)],
                           [(_row_spec(tr, EVEN_IN), _resident((N_CHIPS, D_MODEL, in_shard)))],
                           (T // tr, 1, 1), shapes, specs, (tr, D_MODEL), tail=tail)
    tk = min(512, T)
    dw_in_even = _grad_shards(
        "even_in_dw", h0, dproj0, BS((tk, D_MODEL), lambda k: (k, 0)), BS((tk, EVEN_IN), lambda k: (k, 0)),
        lambda a_ref, b_ref, j: (a_ref[...], b_ref[:, in_shard * j:in_shard * (j + 1)]), (N_CHIPS, D_MODEL, in_shard), T // tk)
    emit("even", {"even_w_in": dw_in_even, "even_w_out": dw_out_even.reshape(N_CHIPS, -1, D_MODEL)})
    G["mix_norm"] = jnp.concatenate([dmix_g0, dmix_g1], axis=0)
    G["ffn_norm"] = jnp.concatenate([dffn_g0, dffn_g1], axis=0)
    return sq[0, 0], dx0.reshape(batch, seq, D_MODEL), G


def _small_vector(parts, names):
    flat = jnp.concatenate([parts[n].astype(F32).reshape(-1) for n in names])
    return _pad_rows(flat, LANES, 2 * SUBLANES)


def _split_small(vec, like, names):
    out, off, flat = {}, 0, vec.reshape(-1)
    for n in names:
        size = math.prod(like[n].shape)
        out[n] = flat[off:off + size].reshape(like[n].shape)
        off += size
    return out


def _whole_shape(a):
    return a.shape[:-1] + (a.shape[-1] * N_CHIPS,)


def kernel(x, positions, mix_norm, ffn_norm, even_w_in, sg_ln_g, sg_w_s, sg_b_s, sc_conv_w, even_w_out, odd_w_in, pool_w, pool_scale, q_a_norm, q_b, kv_a_norm, kv_b, q_norm, k_norm, odd_w_out, ffn_w_gate, ffn_w_up, ffn_w_down, loss_target, m_mix_norm, m_ffn_norm, m_even_w_in, m_sg_ln_g, m_sg_w_s, m_sg_b_s, m_sc_conv_w, m_even_w_out, m_odd_w_in, m_pool_w, m_pool_scale, m_q_a_norm, m_q_b, m_kv_a_norm, m_kv_b, m_q_norm, m_k_norm, m_odd_w_out, m_ffn_w_gate, m_ffn_w_up, m_ffn_w_down, v_mix_norm, v_ffn_norm, v_even_w_in, v_sg_ln_g, v_sg_w_s, v_sg_b_s, v_sc_conv_w, v_even_w_out, v_odd_w_in, v_pool_w, v_pool_scale, v_q_a_norm, v_q_b, v_kv_a_norm, v_kv_b, v_q_norm, v_k_norm, v_odd_w_out, v_ffn_w_gate, v_ffn_w_up, v_ffn_w_down):
    args = dict(locals())
    w = {n: args[n] for n in _WEIGHTS}
    m = {n: args["m_" + n] for n in _WEIGHTS}
    v = {n: args["v_" + n] for n in _WEIGHTS}
    cx, cy, cc = lax.axis_index("x"), lax.axis_index("y"), lax.axis_index("c")
    chip = 2 * cx + cy
    transposed = ("ffn_w_gate", "ffn_w_up")
    for n in transposed:
        w[n], m[n], v[n] = (jnp.swapaxes(t[n], 1, 2) for t in (w, m, v))

    chip_arr = chip.astype(jnp.int32).reshape(1)
    c_arr = cc.astype(jnp.int32).reshape(1)
    group_names = list(_GROUPS)
    placed = {}
    for n in _SMALL_SHARDED:
        a = w[n]
        whole = jnp.zeros(a.shape[:-1] + (N_CHIPS, a.shape[-1]), F32)
        whole = lax.dynamic_update_slice_in_dim(whole, a[..., None, :], chip, axis=a.ndim - 1)
        placed[n] = jnp.where(cc == 0, whole, 0.0).reshape(_whole_shape(a))
    small_whole = _all_reduce_small("gather_small_weights", _small_vector(placed, _SMALL_SHARDED))
    small = dict({n: w[n] for n in _REPLICATED}, **_split_small(small_whole, placed, _SMALL_SHARDED))

    first, rest = list(_GROUPS[group_names[0]]), [n for g in group_names[1:] for n in _GROUPS[g]]
    sems_first, flight_first, token = _gather_send("gather_send_first", _place_shards(w, first, chip_arr, (small_whole,)),
                                                   [list(range(len(first)))], (small_whole,))
    sems_rest, flight_rest, all_sent = _gather_send("gather_send_rest", _place_shards(w, rest, chip_arr, (token,)),
                                                    [[rest.index(n) for n in _GROUPS[g]] for g in group_names[1:]], ())
    sems = list(sems_first) + list(sems_rest)
    in_flight = dict(zip(first + rest, list(flight_first) + list(flight_rest)))

    def fetch(group, after):
        gi, members = group_names.index(group), _GROUPS[group]
        after = after if gi else (all_sent,)
        landed = _gather_wait(f"gather_wait_{group}", [in_flight[n] for n in members], sems[2 * gi], sems[2 * gi + 1], after)
        return _whole_weights(dict(zip(members, _gather_pass(f"gather_pass_{group}", landed))))

    swapping, pending, arrived, sent = [], [], {}, []

    def settle(after):
        names, ps, lands, send_sems, recv_sems = pending.pop()
        ps, lands = _scatter_wait(f"scatter_wait_{names[0]}", ps, lands, send_sems, recv_sems, after)
        arrived.update({n: (p, r) for n, p, r in zip(names, ps, lands)})

    def emit(group, grads):
        names = _GROUPS[group]
        halves = [grads[n].reshape(N_CHIPS, 2, grads[n].shape[1] // 2, grads[n].shape[2]) for n in names]
        send_sems, recv_sems, halves, lands, token = _exchange_send(f"exchange_send_{group}", halves)
        swapping.append((group, halves, lands, send_sems, recv_sems))
        sent.append(token)
        return (token,)

    def advance(done):
        done = done if isinstance(done, tuple) else (done,)
        group, halves, lands, send_sems, recv_sems = swapping.pop()
        names = _GROUPS[group]
        halves, lands = _exchange_wait(f"exchange_wait_{group}", halves, lands, send_sems, recv_sems, done)
        partial = [_add_halves(f"add_{n}", g, r, c_arr) for n, g, r in zip(names, halves, lands)]
        if pending:
            settle(done)
        send_sems, recv_sems, ps, lands, token = _scatter_send(f"scatter_send_{group}", partial)
        pending.append((names, ps, lands, send_sems, recv_sems))
        return (token,)

    sq, grad_x, G = _forward_backward(x, positions, loss_target, small, fetch, emit, advance)
    small_names = _REPLICATED + _SMALL_SHARDED
    G["loss"] = (0.5 * sq / D_MODEL).reshape(1)
    summed = _split_small(_all_reduce_small("reduce_small_grads", _small_vector(G, small_names + ("loss",))), G,
                          small_names + ("loss",))
    loss = summed["loss"][0]
    grads = {n: summed[n] for n in _REPLICATED}
    for n in _SMALL_SHARDED:
        a = w[n]
        grads[n] = lax.dynamic_slice_in_dim(summed[n].reshape(a.shape[:-1] + (N_CHIPS, a.shape[-1])), chip, 1,
                                            axis=a.ndim - 1).reshape(a.shape)

    packed = [_small_vector(d, small_names) for d in (w, grads, m, v)]
    res = _adamw("adamw_small", packed[0][None], packed[1], packed[2][None], packed[3][None])
    delta_s, m_s, v_s = (_split_small(r, w, small_names) for r in res[1:4])

    chip_c = jnp.stack([chip, cc]).astype(jnp.int32)
    out = {}

    def begin(group, after):
        names = _GROUPS[group]
        sums = [_sum_partials(f"sum_{n}", *arrived[n], chip_c) for n in names]
        send_sems, recv_sems, sums, token = _share_send(f"share_send_{group}", sums, after)
        return (names, sums, send_sems, recv_sems), (token,)

    def complete(shared, after):
        names, sums, send_sems, recv_sems = shared
        tokens = []
        for n, f in zip(names, _share_wait(f"share_wait_{names[0]}", sums, send_sems, recv_sems, after)):
            weight, layer = (n[:-1], int(n[-1])) if n[-1].isdigit() else (n, 0)
            *out[weight], token = _adamw(f"adamw_{weight}", w[weight], f.reshape(-1, f.shape[-1]), m[weight], v[weight], layer,
                                         out.get(weight, ()))
            tokens.append(token)
        return tuple(tokens)

    last_exchange = tuple(sent[-1:])
    shared3, behind = begin(group_names[3], last_exchange)
    shared2, behind = begin(group_names[2], behind)
    done = complete(shared3, behind)
    done = done + complete(shared2, done)
    last_scatter = advance(done)
    shared1, behind = begin(group_names[1], last_scatter)
    settle(complete(shared1, behind + (res[4],)))
    shared0, behind = begin(group_names[0], ())
    complete(shared0, behind)
    for n in small_names:
        out[n] = (grads[n], delta_s[n], m_s[n], v_s[n])
    for n in transposed:
        out[n] = tuple(jnp.swapaxes(t, 1, 2) for t in out[n])

    return (loss, grad_x, *[out[n][0] for n in _WEIGHTS], *[out[n][1] for n in _WEIGHTS],
            *[out[n][2] for n in _WEIGHTS], *[out[n][3] for n in _WEIGHTS])
```

```python
import functools
import math

import jax
import jax.numpy as jnp
from jax import lax
from jax.experimental import pallas as pl
from jax.experimental.pallas import tpu as pltpu

F32, BF16 = jnp.float32, jnp.bfloat16
BS = pl.BlockSpec

D_MODEL = 1024
EPS = 1e-6
NEG_INF = -1e30
SG_HEADS, SG_DIM, SG_WIDTH, SG_CHUNK = 4, 128, 512, 128
SC_WIDTH, CONV_TAPS = 512, 3
EVEN_IN = 2 * SG_WIDTH + 3 * SC_WIDTH
POOL_WINDOWS = (2, 4, 8, 16)
POOL_DIM, POOL_WIDTH = 64, 256
POOL_HALO = 16
HEADS, Q_LORA, KV_LORA, QK_NOPE, QK_ROPE, V_DIM = 6, 384, 256, 128, 64, 128
QK_DIM = QK_NOPE + QK_ROPE
QK_PAD = 256
ODD_IN = POOL_WIDTH + Q_LORA + KV_LORA + QK_ROPE
ODD_IN_PAD = 1024
ROPE_THETA = 10000.0
ATTN_SCALE = QK_DIM ** -0.5
D_FF, N_CHIPS = 2816, 4
FF_SHARD = D_FF // N_CHIPS
ADAM_LR, ADAM_B1, ADAM_B2, ADAM_EPS, ADAM_WD, ADAM_STEP = 0.001, 0.9, 0.999, 1e-08, 0.01, 10
VMEM_LIMIT_V7X = 48 * 2**20
LANES, SUBLANES = 128, 8
MESH = pl.DeviceIdType.MESH
HBM = pl.BlockSpec(memory_space=pltpu.HBM)
VMEM = pl.BlockSpec(memory_space=pltpu.VMEM)

_DIMS = {"nn": (((1,), (0,)), ((), ())), "nt": (((1,), (1,)), ((), ())), "tn": (((0,), (0,)), ((), ()))}


def _dot(a, b, mode="nn"):
    return lax.dot_general(a.astype(BF16), b.astype(BF16), _DIMS[mode], preferred_element_type=F32)


def _call(body, *, name, out_shape, in_specs, out_specs, grid=(), scratch=(), aliases=None, after=()):
    params = pltpu.CompilerParams(vmem_limit_bytes=VMEM_LIMIT_V7X,
                                  **({"dimension_semantics": ("arbitrary",) * len(grid)} if grid else {}))
    n_in, n_after = len(in_specs), len(after)
    kernel_body = body if not after else (lambda *refs: body(*refs[:n_in], *refs[n_in + n_after:]))
    call = pl.pallas_call(kernel_body, name=name, grid=grid, in_specs=list(in_specs) + [pl.BlockSpec(memory_space=pl.ANY)] * n_after,
                          out_specs=out_specs, out_shape=out_shape, scratch_shapes=list(scratch),
                          input_output_aliases=aliases or {}, compiler_params=params)
    return (lambda *ops: call(*ops, *after)) if after else call


def _sds(shape, dtype):
    return jax.ShapeDtypeStruct(tuple(shape), dtype)


def _token_tile(seq):
    return 512 if seq % 512 == 0 else seq


_TAIL_ROWS = 256


def _matmul(name, mode, pairs, pair_specs, grid, out_shape, out_spec, acc_shape, add=None, add_spec=None, after=(), tail=None):
    n, nk = len(pairs), grid[-1]
    n_add = int(add is not None)
    n_tail = len(tail[0]) if tail else 0
    n_in = 2 * n + n_add + n_tail
    n_out = len(out_shape) if tail else 1

    def body(*refs):
        ab = refs[:2 * n]
        add_ref = refs[2 * n] if n_add else None
        tail_refs, outs = refs[2 * n + n_add:n_in], refs[n_in:n_in + n_out]
        first = pl.program_id(0) == 0

        def finish(result):
            if tail is None:
                r = result(slice(None))
                outs[0][...] = (r if add_ref is None else r + add_ref[...]).astype(outs[0].dtype)
                return
            for lo in range(0, acc_shape[0], _TAIL_ROWS):
                rows = slice(lo, min(lo + _TAIL_ROWS, acc_shape[0]))
                r = result(rows)
                tail[2](rows, r if add_ref is None else r + add_ref[rows, :], first, tail_refs, outs)

        def terms(a_ref, b_ref):
            if len(a_ref.shape) == 2 and len(b_ref.shape) == 2:
                return [(a_ref[...], b_ref[...])]
            cols = a_ref.shape[-1] // N_CHIPS
            return [(a_ref[j] if len(a_ref.shape) == 3 else a_ref[:, cols * j:cols * (j + 1)], b_ref[j]) for j in range(N_CHIPS)]

        if nk == 1:
            r = None
            for p in range(n):
                for a_blk, b_blk in terms(ab[2 * p], ab[2 * p + 1]):
                    d = _dot(a_blk, b_blk, mode)
                    r = d if r is None else r + d
            finish(lambda rows: r[rows])
            return
        acc = refs[-1]
        k = pl.program_id(len(grid) - 1)

        @pl.when(k == 0)
        def _():
            acc[...] = jnp.zeros_like(acc)

        for p in range(n):
            acc[...] += _dot(ab[2 * p][...], ab[2 * p + 1][...], mode)

        @pl.when(k == nk - 1)
        def _():
            finish(lambda rows: acc[rows, :])

    ops = [t for pr in pairs for t in pr] + ([add] if n_add else []) + (list(tail[0]) if tail else [])
    specs = [s for pr in pair_specs for s in pr] + ([add_spec] if n_add else []) + (list(tail[1]) if tail else [])
    return _call(body, name=name, grid=grid, in_specs=specs, out_specs=out_spec, out_shape=out_shape,
                 scratch=[pltpu.VMEM(acc_shape, F32)] if nk > 1 else [], after=after)(*ops)


def _row_spec(tm, d):
    return BS((tm, d), lambda i, j, k: (i, 0))


def _vec_spec(d):
    return BS((1, d), lambda i, j, k: (0, 0))


def _norm_tail(gain, T, tm):
    d = gain.shape[-1]

    def fn(rows, r, first, tail_refs, outs):
        outs[0][rows, :] = r
        outs[1][rows, :] = (r * lax.rsqrt(jnp.mean(r * r, axis=-1, keepdims=True) + EPS) * tail_refs[0][...]).astype(BF16)

    return ([gain.reshape(1, d)], [_vec_spec(d)], fn), [_sds((T, d), F32), _sds((T, d), BF16)], [_row_spec(tm, d), _row_spec(tm, d)]


def _norm_bwd_tail(x, gain, dres, tm):
    T, d = x.shape

    def fn(rows, r, first, tail_refs, outs):
        x_ref, g_ref, dres_ref = tail_refs
        xv = x_ref[rows, :]
        rstd = lax.rsqrt(jnp.mean(xv * xv, axis=-1, keepdims=True) + EPS)
        xhat = xv * rstd
        if rows.start == 0:
            @pl.when(first)
            def _():
                outs[1][...] = jnp.zeros_like(outs[1])

        outs[1][...] += jnp.sum(r * xhat, axis=0, keepdims=True)
        dxhat = r * g_ref[...]
        outs[0][rows, :] = dres_ref[rows, :] + rstd * (dxhat - xhat * jnp.mean(dxhat * xhat, axis=-1, keepdims=True))

    return (([x, gain.reshape(1, d), dres], [_row_spec(tm, d), _vec_spec(d), _row_spec(tm, d)], fn),
            [_sds((T, d), F32), _sds((1, d), F32)], [_row_spec(tm, d), _vec_spec(d)])


def _loss_tail(target, tm):
    T, d = target.shape

    def fn(rows, r, first, tail_refs, outs):
        e = r - tail_refs[0][rows, :]
        if rows.start == 0:
            @pl.when(first)
            def _():
                outs[1][...] = jnp.zeros_like(outs[1])

        outs[1][...] += jnp.sum(e * e)
        outs[0][rows, :] = e * (1.0 / d)

    return (([target], [_row_spec(tm, d)], fn), [_sds((T, d), F32), _sds((SUBLANES, LANES), F32)],
            [_row_spec(tm, d), BS((SUBLANES, LANES), lambda i, j, k: (0, 0))])


def _grad_shards(name, a, b, a_spec, b_spec, pick, out_shape, n_steps):
    def body(a_ref, b_ref, o_ref, acc):
        k = pl.program_id(0)

        @pl.when(k == 0)
        def _():
            acc[...] = jnp.zeros_like(acc)

        for j in range(N_CHIPS):
            aj, bj = pick(a_ref, b_ref, j)
            acc[j] += _dot(aj, bj, "tn")

        @pl.when(k == n_steps - 1)
        def _():
            o_ref[...] = acc[...].astype(BF16)

    return _call(body, name=name, grid=(n_steps,), in_specs=[a_spec, b_spec], scratch=[pltpu.VMEM(tuple(out_shape), F32)],
                 out_specs=BS(out_shape, lambda k: (0, 0, 0)), out_shape=pltpu.HBM(tuple(out_shape), BF16))(a, b)


def _mm(name, mode, a, b, out_dtype, tm=1024, tn=1024, tk=512, add=None, after=(), fused=None, hbm_out=False):
    if mode == "tn":
        (K, M), N = a.shape, b.shape[1]
    else:
        (M, K), N = a.shape, (b.shape[1] if mode == "nn" else b.shape[0])
    tm, tn, tk = min(tm, M), min(tn, N), min(tk, K)
    a_spec = BS((tk, tm), lambda i, j, k: (k, i)) if mode == "tn" else BS((tm, tk), lambda i, j, k: (i, k))
    b_spec = BS((tn, tk), lambda i, j, k: (j, k)) if mode == "nt" else BS((tk, tn), lambda i, j, k: (k, j))
    o_spec = BS((tm, tn), lambda i, j, k: (i, j))
    tail, shapes, specs = fused if fused else (None, pltpu.HBM((M, N), out_dtype) if hbm_out else _sds((M, N), out_dtype), o_spec)
    return _matmul(name, mode, [(a, b)], [(a_spec, b_spec)], (M // tm, N // tn, K // tk), shapes, specs, (tm, tn),
                   add=add, add_spec=o_spec if add is not None else None, after=after, tail=tail)


_PASS_ROWS = 256


def _ffn_up(name, h, wg, wu, tm):
    T = h.shape[0]

    def body(h_ref, wg_ref, wu_ref, g_ref, u_ref, a_ref):
        hv = h_ref[...]
        g = _dot(hv, wg_ref[...], "nt")
        u = _dot(hv, wu_ref[...], "nt")
        g_ref[...] = g.astype(BF16)
        u_ref[...] = u.astype(BF16)
        a_ref[...] = (g * (1.0 / (1.0 + jnp.exp(-g))) * u).astype(BF16)

    w_spec = BS((None, FF_SHARD, D_MODEL), lambda j, i: (j, 0, 0))
    o_spec = BS((None, tm, FF_SHARD), lambda j, i: (j, i, 0))
    sh = _sds((N_CHIPS, T, FF_SHARD), BF16)
    return _call(body, name=name, grid=(N_CHIPS, T // tm), in_specs=[BS((tm, D_MODEL), lambda j, i: (i, 0)), w_spec, w_spec],
                 out_specs=[o_spec, o_spec, o_spec], out_shape=[sh, sh, sh])(h, wg, wu)


def _ffn_act_bwd(name, dxo, wd, g, u, tm, after=()):
    T = dxo.shape[0]

    def body(dx_ref, wd_ref, g_ref, u_ref, dg_ref, du_ref):
        da = _dot(dx_ref[...], wd_ref[...], "nt")
        g = g_ref[...].astype(F32)
        sig = 1.0 / (1.0 + jnp.exp(-g))
        dg_ref[...] = (da * u_ref[...].astype(F32) * (sig * (1.0 + g * (1.0 - sig)))).astype(BF16)
        du_ref[...] = (da * (g * sig)).astype(BF16)

    t_spec = BS((None, tm, FF_SHARD), lambda i, j: (j, i, 0))
    sh = _sds((N_CHIPS, T, FF_SHARD), BF16)
    return _call(body, name=name, grid=(T // tm, N_CHIPS),
                 in_specs=[BS((tm, D_MODEL), lambda i, j: (i, 0)), BS((None, FF_SHARD, D_MODEL), lambda i, j: (j, 0, 0)), t_spec, t_spec],
                 out_specs=[t_spec, t_spec], out_shape=[sh, sh], after=after)(dxo, wd, g, u)


def _big_tile(n):
    return min(1024, n)


def _resident_tile(n):
    return min(512, n)


def _resident(shape):
    return BS(shape, lambda i, j, k: (0,) * len(shape), pipeline_mode=pl.Buffered(1))


def _ffn_fwd(l, x, h, wg, wu, wd, fused):
    T = x.shape[0]
    g, u, a = _ffn_up(f"ffn{l}_up", h, wg, wu, _big_tile(T))
    tm = _resident_tile(T)
    tail, shapes, specs = fused(tm)
    outs = _matmul(f"ffn{l}_down", "nn", [(a, wd)],
                   [(BS((N_CHIPS, tm, FF_SHARD), lambda i, j, k: (0, i, 0)), _resident((N_CHIPS, FF_SHARD, D_MODEL)))],
                   (T // tm, 1, 1), shapes, specs, (tm, D_MODEL), add=x, add_spec=_row_spec(tm, D_MODEL), tail=tail)
    return outs, (h, g, u, a)


def _ffn_bwd(l, x, gain, wg, wu, wd, saved, dxo, emit, after=()):
    h, g, u, a = saved
    T = x.shape[0]
    tm = _big_tile(T)
    dg, du = _ffn_act_bwd(f"ffn{l}_act_bwd", dxo, wd, g, u, tm, after=after)
    tk = _big_tile(T)
    shards_spec = BS((N_CHIPS, tk, FF_SHARD), lambda k: (0, k, 0))
    rows_spec = BS((tk, D_MODEL), lambda k: (k, 0))

    def dw(nm, act, rows):
        return _grad_shards(nm, act, rows, shards_spec, rows_spec, lambda a_ref, b_ref, j: (a_ref[j], b_ref[...]),
                            (N_CHIPS, FF_SHARD, D_MODEL), T // tk)

    behind = emit(f"ffn{l}", {f"ffn_w_gate{l}": dw(f"ffn{l}_dwg", dg, h), f"ffn_w_up{l}": dw(f"ffn{l}_dwu", du, h),
                              f"ffn_w_down{l}": dw(f"ffn{l}_dwd", a, dxo)})
    tm = _resident_tile(T)
    act_spec = BS((N_CHIPS, tm, FF_SHARD), lambda i, j, k: (0, i, 0))
    w_spec = _resident((N_CHIPS, FF_SHARD, D_MODEL))
    tail, shapes, specs = _norm_bwd_tail(x, gain, dxo, tm)
    return _matmul(f"ffn{l}_dh", "nn", [(dg, wg), (du, wu)], [(act_spec, w_spec), (act_spec, w_spec)],
                   (T // tm, 1, 1), shapes, specs, (tm, D_MODEL), after=behind, tail=tail)


_INV_SQRT2 = 1.0 / math.sqrt(2.0)
_INV_SQRT_2PI = 1.0 / math.sqrt(2.0 * math.pi)


def _gelu(x):
    return 0.5 * x * (1.0 + lax.erf(x * _INV_SQRT2))


def _gelu_and_grad(x):
    cdf = 0.5 * (1.0 + lax.erf(x * _INV_SQRT2))
    return x * cdf, cdf + x * jnp.exp(-0.5 * x * x) * _INV_SQRT_2PI


def _shift_down(x, k):
    return pltpu.roll(x, k, 0)


def _shift_up(x, k):
    return pltpu.roll(x, x.shape[0] - k, 0)


def _layer_norm_head(xh):
    xc = xh - jnp.mean(xh, axis=-1, keepdims=True)
    rstd = lax.rsqrt(jnp.mean(xc * xc, axis=-1, keepdims=True) + EPS)
    return xc * rstd, rstd


def _even_in(x, gain, w, tm):
    T, d = x.shape
    shard = w.shape[-1]

    def body(x_ref, g_ref, w_ref, o_ref, h_ref):
        xv = x_ref[...]
        hv = (xv * lax.rsqrt(jnp.mean(xv * xv, axis=-1, keepdims=True) + EPS) * g_ref[...]).astype(BF16)
        h_ref[...] = hv
        for j in range(N_CHIPS):
            o_ref[:, shard * j:shard * (j + 1)] = _dot(hv, w_ref[j])

    row = BS((tm, d), lambda i: (i, 0))
    return _call(body, name="even_in", grid=(T // tm,),
                 in_specs=[row, BS((1, d), lambda i: (0, 0)), BS(w.shape, lambda i: (0, 0, 0), pipeline_mode=pl.Buffered(1))],
                 out_specs=[BS((tm, N_CHIPS * shard), lambda i: (i, 0)), row],
                 out_shape=[_sds((T, N_CHIPS * shard), F32), _sds((T, d), BF16)])(x, gain.reshape(1, d), w)


def _even_halo_specs(tm, n_tiles, col_blocks, after):
    rows = tm // SUBLANES
    last = n_tiles * rows - 1
    if after:
        return [BS((SUBLANES, 512), functools.partial(lambda cb, i: (jnp.minimum((i + 1) * rows, last), cb), cb)) for cb in col_blocks]
    return [BS((SUBLANES, 512), functools.partial(lambda cb, i: (jnp.maximum(i * rows - 1, 0), cb), cb)) for cb in col_blocks]


def _even_mixer_fwd(proj, ln_g, w_tril, b_lanes, conv_w, seq, tm):
    T = proj.shape[0]
    tiles_per_seq = seq // tm

    def body(p_ref, hc_ref, hh_ref, lng_ref, w_ref, bb_ref, cw_ref, o_ref):
        first = pl.program_id(0) % tiles_per_seq == 0
        for h in range(SG_HEADS):
            cols = slice(SG_DIM * h, SG_DIM * (h + 1))
            vhat, _ = _layer_norm_head(_gelu(p_ref[:, SG_WIDTH + SG_DIM * h:SG_WIDTH + SG_DIM * (h + 1)]))
            vln = (vhat * lng_ref[:, cols]).astype(BF16)
            for k in range(tm // SG_CHUNK):
                rows = slice(SG_CHUNK * k, SG_CHUNK * (k + 1))
                mixed = _dot(w_ref[h], vln[rows]) + bb_ref[h]
                o_ref[rows, cols] = (_gelu(p_ref[rows, cols]) * mixed).astype(BF16)
        z = p_ref[:, 1536:2048] * p_ref[:, 2048:2560]
        zz = jnp.concatenate([jnp.where(first, 0.0, hc_ref[...] * hh_ref[...]), z], axis=0)
        y = cw_ref[0:1, :] * _shift_down(zz, 2)[SUBLANES:] + cw_ref[1:2, :] * _shift_down(zz, 1)[SUBLANES:] + cw_ref[2:3, :] * z
        o_ref[:, SG_WIDTH:] = (p_ref[:, 1024:1536] * y).astype(BF16)

    full = lambda shape: BS(shape, lambda i: (0,) * len(shape))
    return _call(body, name="even_mixer_fwd", grid=(T // tm,),
                 in_specs=[BS((tm, EVEN_IN), lambda i: (i, 0))] + _even_halo_specs(tm, T // tm, (3, 4), after=False)
                 + [full((1, SG_WIDTH)), full((SG_HEADS, SG_CHUNK, SG_CHUNK)), full((SG_HEADS, SG_CHUNK, SG_DIM)), full((SUBLANES, SC_WIDTH))],
                 out_specs=BS((tm, D_MODEL), lambda i: (i, 0)), out_shape=_sds((T, D_MODEL), BF16))(
        proj, proj, proj, ln_g, w_tril, b_lanes, conv_w)


def _even_mixer_bwd(proj, dmix, ln_g, w_tril, b_lanes, conv_w, seq, tm):
    T = proj.shape[0]
    n_tiles, tiles_per_seq = T // tm, seq // tm

    def body(p_ref, dm_ref, hc_ref, hh_ref, nd_ref, nb_ref, lng_ref, w_ref, bb_ref, cw_ref,
             dp_ref, dw_ref, db_ref, dlng_ref, dcw_ref):
        i = pl.program_id(0)
        first = i % tiles_per_seq == 0
        last = i % tiles_per_seq == tiles_per_seq - 1

        @pl.when(i == 0)
        def _():
            dw_ref[...] = jnp.zeros_like(dw_ref)
            db_ref[...] = jnp.zeros_like(db_ref)
            dlng_ref[...] = jnp.zeros_like(dlng_ref)
            dcw_ref[...] = jnp.zeros_like(dcw_ref)

        for h in range(SG_HEADS):
            cols = slice(SG_DIM * h, SG_DIM * (h + 1))
            vcols = slice(SG_WIDTH + SG_DIM * h, SG_WIDTH + SG_DIM * (h + 1))
            lng = lng_ref[:, cols]
            for k in range(tm // SG_CHUNK):
                rows = slice(SG_CHUNK * k, SG_CHUNK * (k + 1))
                gelu_v, dgelu_v = _gelu_and_grad(p_ref[rows, vcols])
                vhat, rstd = _layer_norm_head(gelu_v)
                vln = (vhat * lng).astype(BF16)
                mixed = _dot(w_ref[h], vln) + bb_ref[h]
                gelu_u, dgelu_u = _gelu_and_grad(p_ref[rows, cols])
                da = dm_ref[rows, cols]
                dp_ref[rows, cols] = (da * mixed * dgelu_u).astype(BF16)
                dmixed = da * gelu_u
                db_ref[h] += dmixed
                dw_ref[h] += _dot(dmixed, vln, "nt")
                dvln = _dot(w_ref[h], dmixed, "tn")
                dlng_ref[:, cols] += jnp.sum(dvln * vhat, axis=0, keepdims=True)
                dvhat = dvln * lng
                dgv = rstd * (dvhat - jnp.mean(dvhat, axis=-1, keepdims=True)
                              - vhat * jnp.mean(dvhat * vhat, axis=-1, keepdims=True))
                dp_ref[rows, vcols] = (dgv * dgelu_v).astype(BF16)

        b = p_ref[:, 1024:1536]
        c = p_ref[:, 1536:2048]
        hv = p_ref[:, 2048:2560]
        z = c * hv
        zz = jnp.concatenate([jnp.where(first, 0.0, hc_ref[...] * hh_ref[...]), z], axis=0)
        z1 = _shift_down(zz, 1)[SUBLANES:]
        z2 = _shift_down(zz, 2)[SUBLANES:]
        w0, w1, w2 = cw_ref[0:1, :], cw_ref[1:2, :], cw_ref[2:3, :]
        dbo = dm_ref[:, SG_WIDTH:]
        dy = dbo * b
        dd = jnp.concatenate([dy, jnp.where(last, 0.0, nd_ref[...] * nb_ref[...])], axis=0)
        dz = w2 * dy + w1 * _shift_up(dd, 1)[:tm] + w0 * _shift_up(dd, 2)[:tm]
        dp_ref[:, 1024:1536] = (dbo * (w0 * z2 + w1 * z1 + w2 * z)).astype(BF16)
        dp_ref[:, 1536:2048] = (dz * hv).astype(BF16)
        dp_ref[:, 2048:2560] = (dz * c).astype(BF16)
        dcw_ref[0:1, :] += jnp.sum(dy * z2, axis=0, keepdims=True)
        dcw_ref[1:2, :] += jnp.sum(dy * z1, axis=0, keepdims=True)
        dcw_ref[2:3, :] += jnp.sum(dy * z, axis=0, keepdims=True)

        @pl.when(i == n_tiles - 1)
        def _():
            t_idx = lax.broadcasted_iota(jnp.int32, (SG_CHUNK, SG_CHUNK), 0)
            s_idx = lax.broadcasted_iota(jnp.int32, (SG_CHUNK, SG_CHUNK), 1)
            for h in range(SG_HEADS):
                dw_ref[h] = jnp.where(t_idx >= s_idx, dw_ref[h], 0.0)

    full = lambda shape: BS(shape, lambda i: (0,) * len(shape))
    sq = (SG_HEADS, SG_CHUNK, SG_CHUNK)
    return _call(body, name="even_mixer_bwd", grid=(n_tiles,),
                 in_specs=[BS((tm, EVEN_IN), lambda i: (i, 0)), BS((tm, D_MODEL), lambda i: (i, 0))]
                 + _even_halo_specs(tm, n_tiles, (3, 4), after=False)
                 + _even_halo_specs(tm, n_tiles, (1,), after=True) + _even_halo_specs(tm, n_tiles, (2,), after=True)
                 + [full((1, SG_WIDTH)), full(sq), full(sq), full((SUBLANES, SC_WIDTH))],
                 out_specs=[BS((tm, EVEN_IN), lambda i: (i, 0)), full(sq), full(sq), full((1, SG_WIDTH)), full((SUBLANES, SC_WIDTH))],
                 out_shape=[_sds((T, EVEN_IN), BF16), _sds(sq, F32), _sds(sq, F32), _sds((1, SG_WIDTH), F32), _sds((SUBLANES, SC_WIDTH), F32)])(
        proj, dmix, proj, proj, dmix, proj, ln_g, w_tril, b_lanes, conv_w)


def _pool_select(vals):
    lane = lax.broadcasted_iota(jnp.int32, vals[0].shape, 1)
    out = vals[-1]
    for g in range(len(vals) - 2, -1, -1):
        out = jnp.where(lane < POOL_DIM * (g + 1), vals[g], out)
    return out


def _pool_counts(pos1):
    lane = lax.broadcasted_iota(jnp.int32, (pos1.shape[0], POOL_WIDTH), 1)
    win = _pool_select([jnp.full(lane.shape, float(w), F32) for w in POOL_WINDOWS])
    return jnp.minimum(pos1, win)


def _pool_means(zz, counts):
    s2 = zz + _shift_down(zz, 1)
    s4 = s2 + _shift_down(s2, 2)
    s8 = s4 + _shift_down(s4, 4)
    s16 = s8 + _shift_down(s8, 8)
    return _pool_select([s2, s4, s8, s16])[POOL_HALO:] / counts


def _pool_halo_spec(tm, n_tiles, after):
    rows = tm // POOL_HALO
    if after:
        return BS((POOL_HALO, POOL_WIDTH), lambda i: (jnp.minimum((i + 1) * rows, n_tiles * rows - 1), 0))
    return BS((POOL_HALO, POOL_WIDTH), lambda i: (jnp.maximum(i * rows - 1, 0), 0))


def _pool_fwd(proj, w_diag, scale, seq, tm):
    T = proj.shape[0]
    tiles_per_seq = seq // tm

    def body(z_ref, zh_ref, w_ref, s_ref, o_ref):
        t = pl.program_id(0) % tiles_per_seq
        z = z_ref[...]
        zz = jnp.concatenate([jnp.where(t == 0, 0.0, zh_ref[...]), z], axis=0)
        pos1 = (lax.broadcasted_iota(jnp.int32, (tm, 1), 0) + (t * tm + 1)).astype(F32)
        pooled = _pool_means(zz, _pool_counts(pos1)) - z
        o_ref[...] = (_dot(pooled, w_ref[...]) * s_ref[...]).astype(BF16)

    full = lambda shape: BS(shape, lambda i: (0,) * len(shape))
    return _call(body, name="pool_fwd", grid=(T // tm,),
                 in_specs=[BS((tm, POOL_WIDTH), lambda i: (i, 0)), _pool_halo_spec(tm, T // tm, False),
                           full((POOL_WIDTH, POOL_WIDTH)), full((1, POOL_WIDTH))],
                 out_specs=BS((tm, POOL_WIDTH), lambda i: (i, 0)), out_shape=_sds((T, D_MODEL), BF16))(proj, proj, w_diag, scale)


def _pool_bwd(proj, dmix, w_diag, scale, seq, tm):
    T = proj.shape[0]
    n_tiles, tiles_per_seq = T // tm, seq // tm

    def body(z_ref, zh_ref, do_ref, don_ref, w_ref, s_ref, dz_ref, dw_ref, ds_ref):
        i = pl.program_id(0)
        t = i % tiles_per_seq

        @pl.when(i == 0)
        def _():
            dw_ref[...] = jnp.zeros_like(dw_ref)
            ds_ref[...] = jnp.zeros_like(ds_ref)

        z = z_ref[...]
        zz = jnp.concatenate([jnp.where(t == 0, 0.0, zh_ref[...]), z], axis=0)
        pos1 = (lax.broadcasted_iota(jnp.int32, (tm, 1), 0) + (t * tm + 1)).astype(F32)
        counts = _pool_counts(pos1)
        pooled = _pool_means(zz, counts) - z
        dout = do_ref[...].astype(F32)
        ds_ref[...] += jnp.sum(dout * _dot(pooled, w_ref[...]), axis=0, keepdims=True)
        dlin = dout * s_ref[...]
        dw_ref[...] += _dot(pooled, dlin, "tn")
        dpooled = _dot(dlin, w_ref[...], "nt")
        dpooled_n = _dot(don_ref[...].astype(F32) * s_ref[...], w_ref[...], "nt")
        pos1_n = (lax.broadcasted_iota(jnp.int32, (POOL_HALO, 1), 0) + ((t + 1) * tm + 1)).astype(F32)
        dmean_n = jnp.where(t == tiles_per_seq - 1, 0.0, dpooled_n / _pool_counts(pos1_n))
        dd = jnp.concatenate([dpooled / counts, dmean_n], axis=0)
        r2 = dd + _shift_up(dd, 1)
        r4 = r2 + _shift_up(r2, 2)
        r8 = r4 + _shift_up(r4, 4)
        r16 = r8 + _shift_up(r8, 8)
        dz_ref[...] = (_pool_select([r2, r4, r8, r16])[:tm] - dpooled).astype(BF16)

    full = lambda shape: BS(shape, lambda i: (0,) * len(shape))
    return _call(body, name="pool_bwd", grid=(n_tiles,),
                 in_specs=[BS((tm, POOL_WIDTH), lambda i: (i, 0)), _pool_halo_spec(tm, n_tiles, False),
                           BS((tm, POOL_WIDTH), lambda i: (i, 0)), _pool_halo_spec(tm, n_tiles, True),
                           full((POOL_WIDTH, POOL_WIDTH)), full((1, POOL_WIDTH))],
                 out_specs=[BS((tm, POOL_WIDTH), lambda i: (i, 0)), full((POOL_WIDTH, POOL_WIDTH)), full((1, POOL_WIDTH))],
                 out_shape=[_sds((T, POOL_WIDTH), BF16), _sds((POOL_WIDTH, POOL_WIDTH), F32), _sds((1, POOL_WIDTH), F32)])(
        proj, proj, dmix, dmix, w_diag, scale)


def _rope_partner(r):
    lane = lax.broadcasted_iota(jnp.int32, r.shape, 1)
    return jnp.where(lane < QK_ROPE // 2, pltpu.roll(r, LANES - QK_ROPE // 2, 1), pltpu.roll(r, QK_ROPE // 2, 1))


def _rope(x, cos, sin_signed):
    r = x[:, QK_NOPE:]
    return jnp.concatenate([x[:, :QK_NOPE], r * cos + _rope_partner(r) * sin_signed], axis=1)


def _rope_transposed(dx, cos, sin_signed):
    dr = dx[:, QK_NOPE:]
    return jnp.concatenate([dx[:, :QK_NOPE], dr * cos + _rope_partner(dr * sin_signed)], axis=1)


def _head_norm(x):
    r = lax.rsqrt(jnp.sum(x * x, axis=-1, keepdims=True) * (1.0 / QK_DIM) + EPS)
    return x * r, r


def _head_norm_bwd(dy, xhat, r, gain):
    dxhat = dy * gain
    return r * (dxhat - xhat * (jnp.sum(dxhat * xhat, axis=-1, keepdims=True) * (1.0 / QK_DIM)))


def _latents(p_ref, qag_ref, kvag_ref):
    ql = p_ref[:, POOL_WIDTH:POOL_WIDTH + Q_LORA]
    kvl = p_ref[:, POOL_WIDTH + Q_LORA:POOL_WIDTH + Q_LORA + KV_LORA]
    rq = lax.rsqrt(jnp.mean(ql * ql, axis=-1, keepdims=True) + EPS)
    rkv = lax.rsqrt(jnp.mean(kvl * kvl, axis=-1, keepdims=True) + EPS)
    return ql * rq, rq, kvl * rkv, rkv


def _mla_specs(tm):
    full = lambda shape: BS(shape, lambda i, h: (0,) * len(shape))
    return [BS((tm, ODD_IN_PAD), lambda i, h: (i, 0)), BS((tm, LANES), lambda i, h: (i, 0)), BS((tm, LANES), lambda i, h: (i, 0)),
            full((1, Q_LORA)), full((1, KV_LORA)), BS((None, Q_LORA, QK_PAD), lambda i, h: (h, 0, 0)),
            BS((None, KV_LORA, QK_PAD), lambda i, h: (h, 0, 0)), full((1, QK_PAD)), full((1, QK_PAD))]


def _mla_qkv_fwd(proj, cos, sin_signed, qa_g, kva_g, q_b, kv_b, q_g, k_g, tm):
    T = proj.shape[0]

    def body(p_ref, cos_ref, sin_ref, qag_ref, kvag_ref, qb_ref, kvb_ref, qg_ref, kg_ref, q_ref, k_ref, v_ref, qn_s, kvn_s):
        @pl.when(pl.program_id(1) == 0)
        def _():
            qhat, _, kvhat, _ = _latents(p_ref, qag_ref, kvag_ref)
            qn_s[...] = (qhat * qag_ref[...]).astype(BF16)
            kvn_s[...] = (kvhat * kvag_ref[...]).astype(BF16)

        cos, sin = cos_ref[...], sin_ref[...]
        qhat, _ = _head_norm(_dot(qn_s[...], qb_ref[...]))
        q_ref[...] = _rope(qhat * qg_ref[...], cos, sin).astype(BF16)
        kv = _dot(kvn_s[...], kvb_ref[...])
        khat, _ = _head_norm(jnp.concatenate([kv[:, :QK_NOPE], p_ref[:, ODD_IN_PAD - LANES:]], axis=1))
        k_ref[...] = _rope(khat * kg_ref[...], cos, sin).astype(BF16)
        v_ref[...] = kv[:, QK_NOPE:].astype(BF16)

    qk_spec = BS((None, tm, QK_PAD), lambda i, h: (h, i, 0))
    return _call(body, name="mla_qkv_fwd", grid=(T // tm, HEADS), in_specs=_mla_specs(tm),
                 out_specs=[qk_spec, qk_spec, BS((None, tm, V_DIM), lambda i, h: (h, i, 0))],
                 out_shape=[_sds((HEADS, T, QK_PAD), BF16), _sds((HEADS, T, QK_PAD), BF16), _sds((HEADS, T, V_DIM), BF16)],
                 scratch=[pltpu.VMEM((tm, Q_LORA), BF16), pltpu.VMEM((tm, KV_LORA), BF16)])(
        proj, cos, sin_signed, qa_g, kva_g, q_b, kv_b, q_g, k_g)


def _mla_qkv_bwd(proj, cos, sin_signed, qa_g, kva_g, q_b, kv_b, q_g, k_g, dq, dk, dv, dz_pool, tm):
    T = proj.shape[0]
    n_tiles = T // tm
    chain_rows = min(_PASS_ROWS, tm)

    def body(p_ref, cos_ref, sin_ref, qag_ref, kvag_ref, qb_ref, kvb_ref, qg_ref, kg_ref, dq_ref, dk_ref, dv_ref, dzp_ref,
             dp_ref, dqb_ref, dkvb_ref, dqg_ref, dkg_ref, dqag_ref, dkvag_ref, qn_s, kvn_s, dqn_s, dkvn_s, dkr_s,
             qh_s, kv_s, dqh_s, dkv_s):
        i, h = pl.program_id(0), pl.program_id(1)

        @pl.when((i == 0) & (h == 0))
        def _():
            for ref in (dqb_ref, dkvb_ref, dqg_ref, dkg_ref, dqag_ref, dkvag_ref):
                ref[...] = jnp.zeros_like(ref)

        @pl.when(h == 0)
        def _():
            qhat, _, kvhat, _ = _latents(p_ref, qag_ref, kvag_ref)
            qn_s[...] = (qhat * qag_ref[...]).astype(BF16)
            kvn_s[...] = (kvhat * kvag_ref[...]).astype(BF16)
            dqn_s[...] = jnp.zeros_like(dqn_s)
            dkvn_s[...] = jnp.zeros_like(dkvn_s)
            dkr_s[...] = jnp.zeros_like(dkr_s)

        qh_s[...] = _dot(qn_s[...], qb_ref[...])
        kv_s[...] = _dot(kvn_s[...], kvb_ref[...])
        qg, kg = qg_ref[...], kg_ref[...]

        def chunk(c, gains):
            dqg, dkg = gains
            rows = slice(c * chain_rows, (c + 1) * chain_rows)
            cos, sin = cos_ref[rows, :], sin_ref[rows, :]
            qhat, rq = _head_norm(qh_s[rows, :])
            dqn_head = _rope_transposed(dq_ref[rows, :], cos, sin)
            dqh_s[rows, :] = _head_norm_bwd(dqn_head, qhat, rq, qg).astype(BF16)
            kv = kv_s[rows, :]
            khat, rk = _head_norm(jnp.concatenate([kv[:, :QK_NOPE], p_ref[rows, ODD_IN_PAD - LANES:]], axis=1))
            dkn_head = _rope_transposed(dk_ref[rows, :], cos, sin)
            dkf = _head_norm_bwd(dkn_head, khat, rk, kg)
            dkr_s[rows, :] += dkf[:, QK_NOPE:]
            dkv_s[rows, :] = jnp.concatenate([dkf[:, :QK_NOPE], dv_ref[rows, :]], axis=1).astype(BF16)
            return dqg + dqn_head * qhat, dkg + dkn_head * khat

        dqg = dkg = jnp.zeros((chain_rows, QK_PAD), F32)
        for c in range(tm // chain_rows):
            dqg, dkg = chunk(c, (dqg, dkg))
        dqg_ref[...] += jnp.sum(dqg, axis=0, keepdims=True)
        dkg_ref[...] += jnp.sum(dkg, axis=0, keepdims=True)
        dqb_ref[h] += _dot(qn_s[...], dqh_s[...], "tn")
        dqn_s[...] += _dot(dqh_s[...], qb_ref[...], "nt")
        dkvb_ref[h] += _dot(kvn_s[...], dkv_s[...], "tn")
        dkvn_s[...] += _dot(dkv_s[...], kvb_ref[...], "nt")

        @pl.when(h == HEADS - 1)
        def _():
            qhat_l, rql, kvhat_l, rkvl = _latents(p_ref, qag_ref, kvag_ref)
            dqn, dkvn = dqn_s[...], dkvn_s[...]
            dqag_ref[...] += jnp.sum(dqn * qhat_l, axis=0, keepdims=True)
            dkvag_ref[...] += jnp.sum(dkvn * kvhat_l, axis=0, keepdims=True)
            dqx, dkvx = dqn * qag_ref[...], dkvn * kvag_ref[...]
            dp_ref[:, :POOL_WIDTH] = dzp_ref[...]
            dp_ref[:, POOL_WIDTH:POOL_WIDTH + Q_LORA] = (
                rql * (dqx - qhat_l * jnp.mean(dqx * qhat_l, axis=-1, keepdims=True))).astype(BF16)
            dp_ref[:, POOL_WIDTH + Q_LORA:ODD_IN_PAD - LANES] = (
                rkvl * (dkvx - kvhat_l * jnp.mean(dkvx * kvhat_l, axis=-1, keepdims=True))).astype(BF16)
            dp_ref[:, ODD_IN_PAD - LANES:] = dkr_s[:, :QK_ROPE].astype(BF16)

    full = lambda shape: BS(shape, lambda i, h: (0,) * len(shape))
    qk_spec = BS((None, tm, QK_PAD), lambda i, h: (h, i, 0))
    return _call(body, name="mla_qkv_bwd", grid=(n_tiles, HEADS),
                 in_specs=_mla_specs(tm) + [qk_spec, qk_spec, BS((None, tm, V_DIM), lambda i, h: (h, i, 0)),
                                            BS((tm, POOL_WIDTH), lambda i, h: (i, 0))],
                 out_specs=[BS((tm, ODD_IN), lambda i, h: (i, 0)), full((HEADS, Q_LORA, QK_PAD)), full((HEADS, KV_LORA, QK_PAD)),
                            full((1, QK_PAD)), full((1, QK_PAD)), full((1, Q_LORA)), full((1, KV_LORA))],
                 out_shape=[_sds((T, ODD_IN), BF16),_sds((HEADS, Q_LORA, QK_PAD), F32), _sds((HEADS, KV_LORA, QK_PAD), F32),
                            _sds((1, QK_PAD), F32), _sds((1, QK_PAD), F32), _sds((1, Q_LORA), F32), _sds((1, KV_LORA), F32)],
                 scratch=[pltpu.VMEM((tm, Q_LORA), BF16), pltpu.VMEM((tm, KV_LORA), BF16), pltpu.VMEM((tm, Q_LORA), F32),
                          pltpu.VMEM((tm, KV_LORA), F32), pltpu.VMEM((tm, LANES), F32), pltpu.VMEM((tm, QK_PAD), F32),
                          pltpu.VMEM((tm, QK_PAD), F32), pltpu.VMEM((tm, QK_PAD), BF16), pltpu.VMEM((tm, QK_PAD), BF16)])(
        proj, cos, sin_signed, qa_g, kva_g, q_b, kv_b, q_g, k_g, dq, dk, dv, dz_pool)


_SCALE_LOG2E = ATTN_SCALE * math.log2(math.e)


def _attn_tile(seq):
    return 512 if seq % 512 == 0 else seq


def _causal_mask(s):
    row = lax.broadcasted_iota(jnp.int32, s.shape, 0)
    col = lax.broadcasted_iota(jnp.int32, s.shape, 1)
    return jnp.where(row >= col, s, NEG_INF)


def _tile(i, t):
    return slice(i * t, (i + 1) * t)


def _flash_fwd(q, k, v, mix, batch, seq):
    t = _attn_tile(seq)
    nq = seq // t

    def body(q_ref, k_ref, v_ref, _, o_ref, lse_ref):
        for qi in range(nq):
            rows, before = _tile(qi, t), slice(0, qi * t)
            qv = q_ref[rows, :]
            s_diag = _causal_mask(_dot(qv, k_ref[rows, :], "nt"))
            m = jnp.max(s_diag, axis=-1, keepdims=True)
            if qi:
                s_before = _dot(qv, k_ref[before, :], "nt")
                m = jnp.maximum(m, jnp.max(s_before, axis=-1, keepdims=True))
            p = jnp.exp2((s_diag - m) * _SCALE_LOG2E)
            l = jnp.sum(p, axis=-1, keepdims=True)
            acc = _dot(p, v_ref[rows, :])
            if qi:
                p = jnp.exp2((s_before - m) * _SCALE_LOG2E)
                l = l + jnp.sum(p, axis=-1, keepdims=True)
                acc = acc + _dot(p, v_ref[before, :])
            o_ref[rows, :] = (acc / l).astype(BF16)
            lse_ref[rows, :] = jnp.broadcast_to(m * ATTN_SCALE + jnp.log(l), (t, LANES))

    T = batch * seq
    whole = lambda w: BS((None, seq, w), lambda b, h: (h, b, 0))
    return _call(body, name="flash_fwd", grid=(batch, HEADS),
                 in_specs=[whole(QK_PAD), whole(QK_PAD), whole(V_DIM), pl.BlockSpec(memory_space=pl.ANY)],
                 out_specs=[BS((seq, V_DIM), lambda b, h: (b, POOL_WIDTH // V_DIM + h)), whole(LANES)],
                 out_shape=[_sds((T, D_MODEL), BF16), _sds((HEADS, T, LANES), F32)],
                 aliases={3: 0})(q, k, v, mix)


def _flash_bwd(q, k, v, dmix, mix, lse, batch, seq):
    t = _attn_tile(seq)
    nq = seq // t

    def body(q_ref, k_ref, v_ref, do_ref, o_ref, lse_ref, dq_ref, dk_ref, dv_ref):
        for qi in range(nq):
            rows, before = _tile(qi, t), slice(0, qi * t)
            qv, do = q_ref[rows, :], do_ref[rows, :]
            lse2 = lse_ref[rows, 0:1] * math.log2(math.e)
            delta = jnp.sum(do.astype(F32) * o_ref[rows, :].astype(F32), axis=-1, keepdims=True)

            def block(keys, masked):
                kk = k_ref[keys, :]
                s = _dot(qv, kk, "nt")
                p = jnp.exp2((_causal_mask(s) if masked else s) * _SCALE_LOG2E - lse2)
                ds = p * (_dot(do, v_ref[keys, :], "nt") - delta)
                return _dot(p, do, "tn"), _dot(ds, qv, "tn") * ATTN_SCALE, _dot(ds, kk) * ATTN_SCALE

            dv_ref[rows, :], dk_ref[rows, :], dq = block(rows, True)
            if qi:
                dv, dk, dq_before = block(before, False)
                dv_ref[before, :] += dv
                dk_ref[before, :] += dk
                dq = dq + dq_before
            dq_ref[rows, :] = dq

    T = batch * seq
    whole = lambda w: BS((None, seq, w), lambda b, h: (h, b, 0))
    head_cols = BS((seq, V_DIM), lambda b, h: (b, POOL_WIDTH // V_DIM + h))
    return _call(body, name="flash_bwd", grid=(batch, HEADS),
                 in_specs=[whole(QK_PAD), whole(QK_PAD), whole(V_DIM), head_cols, head_cols, whole(LANES)],
                 out_specs=[whole(QK_PAD), whole(QK_PAD), whole(V_DIM)],
                 out_shape=[_sds((HEADS, T, QK_PAD), F32), _sds((HEADS, T, QK_PAD), F32), _sds((HEADS, T, V_DIM), F32)])(
        q, k, v, dmix, mix, lse)


def _adamw_math(w, g, m, v):
    m = ADAM_B1 * m + (1.0 - ADAM_B1) * g
    v = ADAM_B2 * v + (1.0 - ADAM_B2) * (g * g)
    m_hat = m / (1.0 - ADAM_B1 ** ADAM_STEP)
    v_hat = v / (1.0 - ADAM_B2 ** ADAM_STEP)
    return -ADAM_LR * (m_hat / (jnp.sqrt(v_hat) + ADAM_EPS) + ADAM_WD * w), m, v


def _adamw(name, w, g, m, v, l=0, prev=()):
    L, R, C = w.shape
    tr = 256 if R % 256 == 0 else R

    def body(w_ref, g_ref, m_ref, v_ref, *rest):
        go_ref, d_ref, mo_ref, vo_ref, token = rest[-5:]
        gv = g_ref[...]
        d_ref[...], mo_ref[...], vo_ref[...] = _adamw_math(w_ref[...], gv, m_ref[...], v_ref[...])
        go_ref[...] = gv
        token[...] = jnp.zeros_like(token)

    layer = BS((None, tr, C), lambda i: (l, i, 0))
    return _call(body, name=f"{name}_{l}", grid=(R // tr,),
                 in_specs=[layer, BS((tr, C), lambda i: (i, 0)), layer, layer] + [pl.BlockSpec(memory_space=pl.ANY)] * len(prev),
                 out_specs=[layer] * 4 + [BS((SUBLANES, LANES), lambda i: (0, 0))],
                 out_shape=[_sds((L, R, C), F32)] * 4 + [_sds((SUBLANES, LANES), F32)],
                 aliases={4 + n: n for n in range(len(prev))})(w, g, m, v, *prev)


def _place():
    x, y, c = lax.axis_index("x"), lax.axis_index("y"), lax.axis_index("c")
    other_chips = [(1 - x, y), (x, 1 - y), (1 - x, 1 - y)]
    return x, y, c, other_chips


def _remote(src, dst, send_sem, recv_sem, dev):
    return pltpu.make_async_remote_copy(src_ref=src, dst_ref=dst, send_sem=send_sem, recv_sem=recv_sem,
                                        device_id=dev, device_id_type=MESH)


def _prefetch_call(body, *, name, grid, in_specs, out_specs, out_shape):
    grid_spec = pltpu.PrefetchScalarGridSpec(num_scalar_prefetch=1, grid=grid, in_specs=in_specs, out_specs=out_specs)
    params = pltpu.CompilerParams(vmem_limit_bytes=VMEM_LIMIT_V7X, dimension_semantics=("arbitrary",) * len(grid))
    return pl.pallas_call(body, name=name, grid_spec=grid_spec, out_shape=out_shape, compiler_params=params)


def _row_tile(rows):
    return 256 if rows % 256 == 0 else rows


def _cast_place(name, w, layer, chip, after=()):
    _, _, rows, C = w.shape
    tr = _row_tile(rows)

    def body(chip_ref, w_ref, *rest):
        rest[-1][...] = w_ref[...].astype(BF16)

    return _prefetch_call(body, name=name, grid=(2, rows // tr),
                          in_specs=[BS((None, None, tr, C), lambda h, i, chip_ref: (layer, h, i, 0))]
                          + [pl.BlockSpec(memory_space=pl.ANY)] * len(after),
                          out_specs=BS((None, None, tr, C), lambda h, i, chip_ref: (chip_ref[0], h, i, 0)),
                          out_shape=pltpu.HBM((N_CHIPS, 2, rows, C), BF16))(chip, w, *after)


SEM = pl.BlockSpec(memory_space=pltpu.SEMAPHORE)


def _split_copy_call(body, *, name, in_specs, out_specs, out_shape, aliases):
    return pl.pallas_call(body, name=name, in_specs=in_specs, out_specs=out_specs, out_shape=out_shape,
                          input_output_aliases=aliases,
                          compiler_params=pltpu.CompilerParams(has_side_effects=pltpu.SideEffectType.DATAFLOW_SIDE_EFFECTING))


def _hbm(arrays):
    return [pltpu.with_memory_space_constraint(a, pltpu.HBM) for a in arrays]


def _gather_send(name, gs, groups, after):
    n = len(gs)

    def body(*refs):
        g, sems, token = refs[:n], refs[n + len(after):n + len(after) + 2 * len(groups)], refs[-1]
        x, y, c, chips = _place()
        me = 2 * x + y
        for gi, members in enumerate(groups):
            for a, i in enumerate(members):
                for k, (px, py) in enumerate(chips):
                    _remote(g[i].at[me, c], g[i].at[me, c], sems[2 * gi].at[3 * a + k], sems[2 * gi + 1].at[3 * a + k],
                            (px, py, c)).start()
        token[...] = jnp.zeros_like(token)

    sem_shapes = [pltpu.SemaphoreType.DMA((3 * len(members),)) for members in groups for _ in range(2)]
    out = _split_copy_call(body, name=name, in_specs=[HBM] * n + [pl.BlockSpec(memory_space=pl.ANY)] * len(after),
                           out_specs=[SEM] * len(sem_shapes) + [HBM] * n + [VMEM],
                           out_shape=sem_shapes + [pltpu.HBM(a.shape, a.dtype) for a in gs] + [_sds((SUBLANES, LANES), F32)],
                           aliases={i: len(sem_shapes) + i for i in range(n)})(*_hbm(gs), *after)
    return out[:len(sem_shapes)], out[len(sem_shapes):-1], out[-1]


def _gather_wait(name, gs, send_sems, recv_sems, after):
    n = len(gs)

    def body(*refs):
        g, ssem, rsem = refs[:n], refs[n], refs[n + 1]
        x, y, c, chips = _place()
        me = 2 * x + y
        for a in range(n):
            for k, (px, py) in enumerate(chips):
                landed = g[a].at[2 * px + py, c]
                cp = _remote(g[a].at[me, c], landed, ssem.at[3 * a + k], rsem.at[3 * a + k], (px, py, c))
                cp.wait_recv()
                cp.wait_send()

    return _split_copy_call(body, name=name, in_specs=[HBM] * n + [SEM, SEM] + [pl.BlockSpec(memory_space=pl.ANY)] * len(after),
                            out_specs=[HBM] * n, out_shape=[pltpu.HBM(a.shape, a.dtype) for a in gs],
                            aliases={i: i for i in range(n)})(*gs, send_sems, recv_sems, *after)


def _gather_pass(name, gs):
    n = len(gs)

    def body(*refs):
        g, send_sems, recv_sems = refs[n:2 * n], refs[-2], refs[-1]
        x, y, c, chips = _place()
        sibling = (x, y, 1 - c)
        passed = [_remote(g[i].at[2 * px + py, c], g[i].at[2 * px + py, c], send_sems.at[3 * i + k], recv_sems.at[3 * i + k], sibling)
                  for i in range(n) for k, (px, py) in enumerate(chips)]
        for cp in passed:
            cp.start()
        for i in range(n):
            for k, (px, py) in enumerate(chips):
                theirs = g[i].at[2 * px + py, 1 - c]
                _remote(theirs, theirs, send_sems.at[3 * i + k], recv_sems.at[3 * i + k], sibling).wait_recv()
        for cp in passed:
            cp.wait_send()

    return _call(body, name=name, in_specs=[HBM] * n, out_specs=[HBM] * n, out_shape=[_sds(a.shape, a.dtype) for a in gs],
                 aliases={i: i for i in range(n)},
                 scratch=[pltpu.SemaphoreType.DMA((3 * n,)), pltpu.SemaphoreType.DMA((3 * n,))])(*gs)


def _scatter_send(name, ps):
    n = len(ps)

    def body(*refs):
        p, r, ssem, rsem, token = refs[:n], refs[n:2 * n], refs[2 * n], refs[2 * n + 1], refs[-1]
        x, y, c, chips = _place()
        for i in range(n):
            for k, (px, py) in enumerate(chips):
                _remote(p[i].at[2 * px + py], r[i].at[k], ssem.at[3 * i + k], rsem.at[3 * i + k], (px, py, c)).start()
        token[...] = jnp.zeros_like(token)

    lands = [lax.empty((N_CHIPS - 1,) + a.shape[1:], a.dtype) for a in ps]
    sem = pltpu.SemaphoreType.DMA((3 * n,))
    out = _split_copy_call(body, name=name, in_specs=[HBM] * (2 * n), out_specs=[SEM, SEM] + [HBM] * (2 * n) + [VMEM],
                           out_shape=[sem, sem] + [pltpu.HBM(a.shape, a.dtype) for a in list(ps) + lands] + [_sds((SUBLANES, LANES), F32)],
                           aliases={i: 2 + i for i in range(2 * n)})(*_hbm(list(ps) + lands))
    return out[0], out[1], out[2:2 + n], out[2 + n:2 + 2 * n], out[-1]


def _scatter_wait(name, ps, lands, send_sems, recv_sems, after):
    n = len(ps)

    def body(*refs):
        p, r, ssem, rsem = refs[:n], refs[n:2 * n], refs[2 * n], refs[2 * n + 1]
        x, y, c, chips = _place()
        for i in range(n):
            for k, (px, py) in enumerate(chips):
                cp = _remote(p[i].at[2 * px + py], r[i].at[k], ssem.at[3 * i + k], rsem.at[3 * i + k], (px, py, c))
                cp.wait_recv()
                cp.wait_send()

    out = _split_copy_call(body, name=name, in_specs=[HBM] * (2 * n) + [SEM, SEM] + [pl.BlockSpec(memory_space=pl.ANY)] * len(after),
                           out_specs=[HBM] * (2 * n), out_shape=[pltpu.HBM(a.shape, a.dtype) for a in list(ps) + list(lands)],
                           aliases={i: i for i in range(2 * n)})(*ps, *lands, send_sems, recv_sems, *after)
    return out[:n], out[n:]


def _exchange_send(name, gs):
    n = len(gs)

    def body(*refs):
        g, r, ssem, rsem, token = refs[:n], refs[n:2 * n], refs[2 * n], refs[2 * n + 1], refs[-1]
        x, y, c, _ = _place()
        for i in range(n):
            _remote(g[i].at[:, 1 - c], r[i], ssem.at[i], rsem.at[i], (x, y, 1 - c)).start()
        token[...] = jnp.zeros_like(token)

    lands = [lax.empty((a.shape[0],) + a.shape[2:], a.dtype) for a in gs]
    sem = pltpu.SemaphoreType.DMA((n,))
    out = _split_copy_call(body, name=name, in_specs=[HBM] * (2 * n), out_specs=[SEM, SEM] + [HBM] * (2 * n) + [VMEM],
                           out_shape=[sem, sem] + [pltpu.HBM(a.shape, a.dtype) for a in list(gs) + lands] + [_sds((SUBLANES, LANES), F32)],
                           aliases={i: 2 + i for i in range(2 * n)})(*_hbm(list(gs) + lands))
    return out[0], out[1], out[2:2 + n], out[2 + n:2 + 2 * n], out[-1]


def _exchange_wait(name, gs, lands, send_sems, recv_sems, after):
    n = len(gs)

    def body(*refs):
        g, r, ssem, rsem = refs[:n], refs[n:2 * n], refs[2 * n], refs[2 * n + 1]
        x, y, c, _ = _place()
        for i in range(n):
            cp = _remote(g[i].at[:, 1 - c], r[i], ssem.at[i], rsem.at[i], (x, y, 1 - c))
            cp.wait_recv()
            cp.wait_send()

    out = _split_copy_call(body, name=name, in_specs=[HBM] * (2 * n) + [SEM, SEM] + [pl.BlockSpec(memory_space=pl.ANY)] * len(after),
                           out_specs=[HBM] * (2 * n), out_shape=[pltpu.HBM(a.shape, a.dtype) for a in list(gs) + list(lands)],
                           aliases={i: i for i in range(2 * n)})(*gs, *lands, send_sems, recv_sems, *after)
    return out[:n], out[n:]


def _sibling_share(name, fs, after=()):
    n = len(fs)

    def body(*refs):
        f, send_sems, recv_sems = refs[n:2 * n], refs[-2], refs[-1]
        x, y, c, _ = _place()
        sends = [_remote(f[i].at[c], f[i].at[c], send_sems.at[i], recv_sems.at[i], (x, y, 1 - c)) for i in range(n)]
        for cp in sends:
            cp.start()
        for i in range(n):
            theirs = f[i].at[1 - c]
            _remote(theirs, theirs, send_sems.at[i], recv_sems.at[i], (x, y, 1 - c)).wait_recv()
        for cp in sends:
            cp.wait_send()

    return _call(body, name=name, in_specs=[HBM] * n, out_specs=[HBM] * n,
                 out_shape=[_sds(a.shape, a.dtype) for a in fs], aliases={i: i for i in range(n)}, after=after,
                 scratch=[pltpu.SemaphoreType.DMA((n,)), pltpu.SemaphoreType.DMA((n,))])(*fs)


def _all_reduce_small(name, v):
    rows = v.shape[0] // 2
    halves = (2, rows, LANES)

    def body(v_ref, o_ref, from_sibling, chip_sums, send_sems, recv_sems):
        x, y, c, chips = _place()
        me, sibling = 2 * x + y, (x, y, 1 - c)
        swap = _remote(v_ref.at[1 - c], from_sibling, send_sems.at[0], recv_sems.at[0], sibling)
        swap.start()
        swap.wait()
        chip_sums[me] = v_ref[c] + from_sibling[...]
        sends = [_remote(chip_sums.at[me], chip_sums.at[me], send_sems.at[1 + k], recv_sems.at[1 + k], (px, py, c))
                 for k, (px, py) in enumerate(chips)]
        for cp in sends:
            cp.start()
        for k, (px, py) in enumerate(chips):
            theirs = chip_sums.at[2 * px + py]
            _remote(theirs, theirs, send_sems.at[1 + k], recv_sems.at[1 + k], (px, py, c)).wait_recv()
        for cp in sends:
            cp.wait_send()
        acc = chip_sums[0]
        for j in range(1, N_CHIPS):
            acc = acc + chip_sums[j]
        o_ref[c] = acc
        share = _remote(o_ref.at[c], o_ref.at[c], send_sems.at[4], recv_sems.at[4], sibling)
        share.start()
        share.wait_send()
        _remote(o_ref.at[1 - c], o_ref.at[1 - c], send_sems.at[4], recv_sems.at[4], sibling).wait_recv()

    return _call(body, name=name, in_specs=[VMEM], out_specs=VMEM, out_shape=_sds(halves, F32),
                 scratch=[pltpu.VMEM((rows, LANES), F32), pltpu.VMEM((N_CHIPS, rows, LANES), F32),
                          pltpu.SemaphoreType.DMA((5,)), pltpu.SemaphoreType.DMA((5,))])(v.reshape(halves)).reshape(v.shape)


def _add_halves(name, g, r, c):
    _, _, rows, C = g.shape
    tr = _row_tile(rows)

    def body(c_ref, g_ref, r_ref, o_ref):
        o_ref[...] = (g_ref[...].astype(F32) + r_ref[...].astype(F32)).astype(BF16)

    spec = BS((None, tr, C), lambda j, i, c_ref: (j, i, 0))
    return _prefetch_call(body, name=name, grid=(N_CHIPS, rows // tr),
                          in_specs=[BS((None, None, tr, C), lambda j, i, c_ref: (j, c_ref[0], i, 0)), spec], out_specs=spec,
                          out_shape=pltpu.HBM((N_CHIPS, rows, C), BF16))(c, g, r)


def _sum_partials(name, p, r, chip_c):
    _, rows, C = p.shape
    tr = _row_tile(rows)

    def body(s_ref, p_ref, r_ref, o_ref):
        acc = p_ref[...].astype(F32)
        for k in range(N_CHIPS - 1):
            acc = acc + r_ref[k].astype(F32)
        o_ref[...] = acc

    return _prefetch_call(body, name=name, grid=(rows // tr,),
                          in_specs=[BS((None, tr, C), lambda i, s: (s[0], i, 0)), BS((N_CHIPS - 1, tr, C), lambda i, s: (0, i, 0))],
                          out_specs=BS((None, tr, C), lambda i, s: (s[1], i, 0)), out_shape=pltpu.HBM((2, rows, C), F32))(chip_c, p, r)


_SHARDED = ("even_w_in", "even_w_out", "odd_w_in", "q_b", "kv_b", "odd_w_out", "ffn_w_gate", "ffn_w_up", "ffn_w_down")
_REPLICATED = ("mix_norm", "ffn_norm", "sg_ln_g", "sg_w_s", "sg_b_s", "pool_w", "q_norm", "k_norm")
_SMALL_SHARDED = ("sc_conv_w", "pool_scale", "q_a_norm", "kv_a_norm")
_WEIGHTS = ("mix_norm", "ffn_norm", "even_w_in", "sg_ln_g", "sg_w_s", "sg_b_s", "sc_conv_w", "even_w_out", "odd_w_in", "pool_w",
            "pool_scale", "q_a_norm", "q_b", "kv_a_norm", "kv_b", "q_norm", "k_norm", "odd_w_out", "ffn_w_gate", "ffn_w_up",
            "ffn_w_down")


def _pad_rows(flat, width, align):
    n = flat.shape[0]
    rows = -(-n // (width * align)) * align
    return jnp.pad(flat, (0, rows * width - n)).reshape(rows, width)


_GROUPS = {"even": ("even_w_in", "even_w_out"),
           "ffn0": ("ffn_w_gate0", "ffn_w_up0", "ffn_w_down0"),
           "odd": ("odd_w_in", "q_b", "kv_b", "odd_w_out"),
           "ffn1": ("ffn_w_gate1", "ffn_w_up1", "ffn_w_down1")}


def _place_shards(shards, names, chip, after):
    placed = []
    for n in names:
        weight, layer = (n[:-1], int(n[-1])) if n[-1].isdigit() else (n, 0)
        a = shards[weight]
        placed.append(_cast_place(f"place_{n}", a.reshape(a.shape[0], 2, a.shape[1] // 2, a.shape[2]), layer, chip, after))
    return placed


def _whole_weights(gathered):
    out = {n: a.reshape(N_CHIPS, -1, a.shape[-1]) for n, a in gathered.items()}
    for n in ("q_b", "kv_b"):
        if n in out:
            out[n] = out[n].transpose(1, 0, 2).reshape(out[n].shape[1], -1)
    for n in ("even_w_out", "odd_w_in", "odd_w_out"):
        if n in out:
            out[n] = out[n].reshape(-1, out[n].shape[-1])
    return out


def _forward_backward(x, positions, target, small, fetch, emit, advance):
    batch, seq, _ = x.shape
    T = batch * seq
    tm = _token_tile(seq)
    x0 = x.reshape(T, D_MODEL)

    inv_freq = ROPE_THETA ** (-jnp.arange(0, QK_ROPE, 2, dtype=F32) / QK_ROPE)
    ang = (positions.astype(F32)[..., None] * inv_freq).reshape(T, QK_ROPE // 2)
    cos, sin = jnp.cos(ang), jnp.sin(ang)
    pad = jnp.zeros((T, LANES - QK_ROPE), F32)
    cos_t = jnp.concatenate([cos, cos, pad], axis=1)
    sin_t = jnp.concatenate([-sin, sin, pad], axis=1)

    tril = jnp.tril(jnp.ones((SG_CHUNK, SG_CHUNK), bool))
    w_tril = jnp.where(tril[None], small["sg_w_s"][0], 0.0).astype(BF16)
    b_lanes = jnp.broadcast_to(small["sg_b_s"][0][:, :, None], (SG_HEADS, SG_CHUNK, SG_DIM))
    conv_w = jnp.pad(small["sc_conv_w"][0], ((0, SUBLANES - CONV_TAPS), (0, 0)))
    ln_g = small["sg_ln_g"]
    pool_diag = jnp.zeros((POOL_WIDTH, POOL_WIDTH), F32)
    for g in range(len(POOL_WINDOWS)):
        pool_diag = pool_diag.at[POOL_DIM * g:POOL_DIM * (g + 1), POOL_DIM * g:POOL_DIM * (g + 1)].set(small["pool_w"][0, g])
    pool_diag = pool_diag.astype(BF16)
    pool_scale = small["pool_scale"]
    q_g = jnp.pad(small["q_norm"], ((0, 0), (0, QK_PAD - QK_DIM)))
    k_g = jnp.pad(small["k_norm"], ((0, 0), (0, QK_PAD - QK_DIM)))
    qa_g, kva_g = small["q_a_norm"], small["kv_a_norm"]
    in_shard = EVEN_IN // N_CHIPS

    def ffn_weights(l, w):
        return w[f"ffn_w_gate{l}"], w[f"ffn_w_up{l}"], w[f"ffn_w_down{l}"]

    W = fetch("even", ())
    w_in_even = W["even_w_in"]
    tb = _big_tile(T)
    proj0, h0 = _even_in(x0, small["mix_norm"][0], w_in_even, _resident_tile(T))
    mix0 = _even_mixer_fwd(proj0, ln_g, w_tril, b_lanes, conv_w, seq, tm)
    w_out_even = W["even_w_out"]
    x1, h1 = _mm("even_out", "nn", mix0, w_out_even, F32, tk=1024, add=x0, fused=_norm_tail(small["ffn_norm"][0], T, tb))
    ffn0 = ffn_weights(0, fetch("ffn0", (x1,)))
    (x2, h2), ffn0_saved = _ffn_fwd(0, x1, h1, *ffn0, lambda tile: _norm_tail(small["mix_norm"][1], T, tile))
    W = fetch("odd", (x2,))
    w_in_odd = jnp.pad(W["odd_w_in"], ((0, 0), (0, ODD_IN_PAD - ODD_IN)))
    q_b = jnp.pad(W["q_b"].reshape(Q_LORA, HEADS, QK_DIM).transpose(1, 0, 2), ((0, 0), (0, 0), (0, QK_PAD - QK_DIM)))
    kv_b = W["kv_b"].reshape(KV_LORA, HEADS, QK_NOPE + V_DIM).transpose(1, 0, 2)
    proj1 = _mm("odd_in", "nn", h2, w_in_odd, F32, tk=1024)
    mix1 = _pool_fwd(proj1, pool_diag, pool_scale, seq, tm)
    q, k, v = _mla_qkv_fwd(proj1, cos_t, sin_t, qa_g, kva_g, q_b, kv_b, q_g, k_g, tm)
    mix1, lse = _flash_fwd(q, k, v, mix1, batch, seq)
    x3, h3 = _mm("odd_out", "nn", mix1, W["odd_w_out"], F32, tk=1024, add=x2, fused=_norm_tail(small["ffn_norm"][1], T, tb))
    ffn1 = ffn_weights(1, fetch("ffn1", (x3,)))
    (dy, sq), ffn1_saved = _ffn_fwd(1, x3, h3, *ffn1, lambda tile: _loss_tail(target.reshape(T, D_MODEL), tile))

    G = {}
    dx3, dffn_g1 = _ffn_bwd(1, x3, small["ffn_norm"][1], *ffn1, ffn1_saved, dy, emit)
    dmix1 = _mm("odd_out_dx", "nt", dx3, W["odd_w_out"], BF16, tk=1024, after=advance(dx3))
    dw_out_odd = _mm("odd_out_dw", "tn", mix1, dx3, BF16, hbm_out=True)
    dq, dk, dv = _flash_bwd(q, k, v, dmix1, mix1, lse, batch, seq)
    dz_pool, dpool_diag, G["pool_scale"] = _pool_bwd(proj1, dmix1, pool_diag, pool_scale, seq, tm)
    dproj1, dq_b, dkv_b, dq_g, dk_g, G["q_a_norm"], G["kv_a_norm"] = _mla_qkv_bwd(
        proj1, cos_t, sin_t, qa_g, kva_g, q_b, kv_b, q_g, k_g, dq, dk, dv, dz_pool, tm)
    G["pool_w"] = jnp.stack([dpool_diag[POOL_DIM * g:POOL_DIM * (g + 1), POOL_DIM * g:POOL_DIM * (g + 1)]
                             for g in range(len(POOL_WINDOWS))])[None]
    G["q_norm"], G["k_norm"] = dq_g[:, :QK_DIM], dk_g[:, :QK_DIM]
    dw_in_odd = _mm("odd_in_dw", "tn", h2, dproj1, BF16, tn=ODD_IN, hbm_out=True)

    def shard_major(g, cols):
        return g.reshape(g.shape[0], N_CHIPS, cols).transpose(1, 0, 2).astype(BF16)

    behind = emit("odd", {"odd_w_in": dw_in_odd.reshape(N_CHIPS, -1, ODD_IN),
                          "q_b": shard_major(dq_b[:, :, :QK_DIM].transpose(1, 0, 2).reshape(Q_LORA, HEADS * QK_DIM), HEADS * QK_DIM // N_CHIPS),
                          "kv_b": shard_major(dkv_b.transpose(1, 0, 2).reshape(KV_LORA, HEADS * (QK_NOPE + V_DIM)),
                                              HEADS * (QK_NOPE + V_DIM) // N_CHIPS),
                          "odd_w_out": dw_out_odd.reshape(N_CHIPS, -1, D_MODEL)})
    dx2, dmix_g1 = _mm("odd_in_dx", "nt", dproj1, W["odd_w_in"], F32, tk=ODD_IN, after=behind,
                       fused=_norm_bwd_tail(x2, small["mix_norm"][1], dx3, tb))
    dx1, dffn_g0 = _ffn_bwd(0, x1, small["ffn_norm"][0], *ffn0, ffn0_saved, dx2, emit, after=advance(dx2))
    dmix0 = _mm("even_out_dx", "nt", dx1, w_out_even, F32, tk=1024, after=advance(dx1))
    dw_out_even = _mm("even_out_dw", "tn", mix0, dx1, BF16, hbm_out=True)
    dproj0, dw_s, db_lanes, G["sg_ln_g"], dconv = _even_mixer_bwd(proj0, dmix0, ln_g, w_tril, b_lanes, conv_w, seq, min(tm, 256))
    G["sg_w_s"] = dw_s[None]
    G["sg_b_s"] = jnp.sum(db_lanes, axis=-1)[None]
    G["sc_conv_w"] = dconv[None, :CONV_TAPS]
    tr = _resident_tile(T)
    tail, shapes, specs = _norm_bwd_tail(x0, small["mix_norm"][0], dx1, tr)
    dx0, dmix_g0 = _matmul("even_in_dx", "nt", [(dproj0, w_in_even)],
                           [(_row_spec(tr, EVEN_IN), _resident((N_CHIPS, D_MODEL, in_shard)))],
                           (T // tr, 1, 1), shapes, specs, (tr, D_MODEL), tail=tail)
    tk = min(512, T)
    dw_in_even = _grad_shards(
        "even_in_dw", h0, dproj0, BS((tk, D_MODEL), lambda k: (k, 0)), BS((tk, EVEN_IN), lambda k: (k, 0)),
        lambda a_ref, b_ref, j: (a_ref[...], b_ref[:, in_shard * j:in_shard * (j + 1)]), (N_CHIPS, D_MODEL, in_shard), T // tk)
    emit("even", {"even_w_in": dw_in_even, "even_w_out": dw_out_even.reshape(N_CHIPS, -1, D_MODEL)})
    G["mix_norm"] = jnp.concatenate([dmix_g0, dmix_g1], axis=0)
    G["ffn_norm"] = jnp.concatenate([dffn_g0, dffn_g1], axis=0)
    return sq[0, 0], dx0.reshape(batch, seq, D_MODEL), G


def _small_vector(parts, names):
    flat = jnp.concatenate([parts[n].astype(F32).reshape(-1) for n in names])
    return _pad_rows(flat, LANES, 2 * SUBLANES)


def _split_small(vec, like, names):
    out, off, flat = {}, 0, vec.reshape(-1)
    for n in names:
        size = math.prod(like[n].shape)
        out[n] = flat[off:off + size].reshape(like[n].shape)
        off += size
    return out


def _whole_shape(a):
    return a.shape[:-1] + (a.shape[-1] * N_CHIPS,)


def kernel(x, positions, mix_norm, ffn_norm, even_w_in, sg_ln_g, sg_w_s, sg_b_s, sc_conv_w, even_w_out, odd_w_in, pool_w, pool_scale, q_a_norm, q_b, kv_a_norm, kv_b, q_norm, k_norm, odd_w_out, ffn_w_gate, ffn_w_up, ffn_w_down, loss_target, m_mix_norm, m_ffn_norm, m_even_w_in, m_sg_ln_g, m_sg_w_s, m_sg_b_s, m_sc_conv_w, m_even_w_out, m_odd_w_in, m_pool_w, m_pool_scale, m_q_a_norm, m_q_b, m_kv_a_norm, m_kv_b, m_q_norm, m_k_norm, m_odd_w_out, m_ffn_w_gate, m_ffn_w_up, m_ffn_w_down, v_mix_norm, v_ffn_norm, v_even_w_in, v_sg_ln_g, v_sg_w_s, v_sg_b_s, v_sc_conv_w, v_even_w_out, v_odd_w_in, v_pool_w, v_pool_scale, v_q_a_norm, v_q_b, v_kv_a_norm, v_kv_b, v_q_norm, v_k_norm, v_odd_w_out, v_ffn_w_gate, v_ffn_w_up, v_ffn_w_down):
    args = dict(locals())
    w = {n: args[n] for n in _WEIGHTS}
    m = {n: args["m_" + n] for n in _WEIGHTS}
    v = {n: args["v_" + n] for n in _WEIGHTS}
    cx, cy, cc = lax.axis_index("x"), lax.axis_index("y"), lax.axis_index("c")
    chip = 2 * cx + cy
    transposed = ("ffn_w_gate", "ffn_w_up")
    for n in transposed:
        w[n], m[n], v[n] = (jnp.swapaxes(t[n], 1, 2) for t in (w, m, v))

    chip_arr = chip.astype(jnp.int32).reshape(1)
    c_arr = cc.astype(jnp.int32).reshape(1)
    group_names = list(_GROUPS)
    placed = {}
    for n in _SMALL_SHARDED:
        a = w[n]
        whole = jnp.zeros(a.shape[:-1] + (N_CHIPS, a.shape[-1]), F32)
        whole = lax.dynamic_update_slice_in_dim(whole, a[..., None, :], chip, axis=a.ndim - 1)
        placed[n] = jnp.where(cc == 0, whole, 0.0).reshape(_whole_shape(a))
    small_whole = _all_reduce_small("gather_small_weights", _small_vector(placed, _SMALL_SHARDED))
    small = dict({n: w[n] for n in _REPLICATED}, **_split_small(small_whole, placed, _SMALL_SHARDED))

    first, rest = list(_GROUPS[group_names[0]]), [n for g in group_names[1:] for n in _GROUPS[g]]
    sems_first, flight_first, token = _gather_send("gather_send_first", _place_shards(w, first, chip_arr, (small_whole,)),
                                                   [list(range(len(first)))], (small_whole,))
    sems_rest, flight_rest, all_sent = _gather_send("gather_send_rest", _place_shards(w, rest, chip_arr, (token,)),
                                                    [[rest.index(n) for n in _GROUPS[g]] for g in group_names[1:]], ())
    sems = list(sems_first) + list(sems_rest)
    in_flight = dict(zip(first + rest, list(flight_first) + list(flight_rest)))

    def fetch(group, after):
        gi, members = group_names.index(group), _GROUPS[group]
        after = after if gi else (all_sent,)
        landed = _gather_wait(f"gather_wait_{group}", [in_flight[n] for n in members], sems[2 * gi], sems[2 * gi + 1], after)
        return _whole_weights(dict(zip(members, _gather_pass(f"gather_pass_{group}", landed))))

    swapping, pending, arrived, sent = [], [], {}, []

    def settle(after):
        names, ps, lands, send_sems, recv_sems = pending.pop()
        ps, lands = _scatter_wait(f"scatter_wait_{names[0]}", ps, lands, send_sems, recv_sems, after)
        arrived.update({n: (p, r) for n, p, r in zip(names, ps, lands)})

    def emit(group, grads):
        names = _GROUPS[group]
        halves = [grads[n].reshape(N_CHIPS, 2, grads[n].shape[1] // 2, grads[n].shape[2]) for n in names]
        send_sems, recv_sems, halves, lands, token = _exchange_send(f"exchange_send_{group}", halves)
        swapping.append((group, halves, lands, send_sems, recv_sems))
        sent.append(token)
        return (token,)

    def advance(done):
        done = done if isinstance(done, tuple) else (done,)
        group, halves, lands, send_sems, recv_sems = swapping.pop()
        names = _GROUPS[group]
        halves, lands = _exchange_wait(f"exchange_wait_{group}", halves, lands, send_sems, recv_sems, done)
        partial = [_add_halves(f"add_{n}", g, r, c_arr) for n, g, r in zip(names, halves, lands)]
        if pending:
            settle(done)
        send_sems, recv_sems, ps, lands, token = _scatter_send(f"scatter_send_{group}", partial)
        pending.append((names, ps, lands, send_sems, recv_sems))
        return (token,)

    sq, grad_x, G = _forward_backward(x, positions, loss_target, small, fetch, emit, advance)
    small_names = _REPLICATED + _SMALL_SHARDED
    G["loss"] = (0.5 * sq / D_MODEL).reshape(1)
    summed = _split_small(_all_reduce_small("reduce_small_grads", _small_vector(G, small_names + ("loss",))), G,
                          small_names + ("loss",))
    loss = summed["loss"][0]
    grads = {n: summed[n] for n in _REPLICATED}
    for n in _SMALL_SHARDED:
        a = w[n]
        grads[n] = lax.dynamic_slice_in_dim(summed[n].reshape(a.shape[:-1] + (N_CHIPS, a.shape[-1])), chip, 1,
                                            axis=a.ndim - 1).reshape(a.shape)

    chip_c = jnp.stack([chip, cc]).astype(jnp.int32)
    out = {}

    def finish(group, after):
        names, tokens = _GROUPS[group], []
        sums = [_sum_partials(f"sum_{n}", *arrived[n], chip_c) for n in names]
        for n, f in zip(names, _sibling_share(f"grad_share_{group}", sums, after)):
            weight, layer = (n[:-1], int(n[-1])) if n[-1].isdigit() else (n, 0)
            *out[weight], token = _adamw(f"adamw_{weight}", w[weight], f.reshape(-1, f.shape[-1]), m[weight], v[weight], layer,
                                         out.get(weight, ()))
            tokens.append(token)
        return tuple(tokens)

    last_exchange = tuple(sent[-1:])
    last_scatter = advance(finish(group_names[3], last_exchange) + finish(group_names[2], last_exchange))
    settle(finish(group_names[1], last_scatter))
    finish(group_names[0], ())
    packed = [_small_vector(d, small_names) for d in (w, grads, m, v)]
    res = _adamw("adamw_small", packed[0][None], packed[1], packed[2][None], packed[3][None])
    delta_s, m_s, v_s = (_split_small(r, w, small_names) for r in res[1:4])
    for n in small_names:
        out[n] = (grads[n], delta_s[n], m_s[n], v_s[n])
    for n in transposed:
        out[n] = tuple(jnp.swapaxes(t, 1, 2) for t in out[n])

    return (loss, grad_x, *[out[n][0] for n in _WEIGHTS], *[out[n][1] for n in _WEIGHTS],
            *[out[n][2] for n in _WEIGHTS], *[out[n][3] for n in _WEIGHTS])
```

```python
import functools
import math

import jax
import jax.numpy as jnp
from jax import lax
from jax.experimental import pallas as pl
from jax.experimental.pallas import tpu as pltpu

F32, BF16 = jnp.float32, jnp.bfloat16
BS = pl.BlockSpec

D_MODEL = 1024
EPS = 1e-6
NEG_INF = -1e30
SG_HEADS, SG_DIM, SG_WIDTH, SG_CHUNK = 4, 128, 512, 128
SC_WIDTH, CONV_TAPS = 512, 3
EVEN_IN = 2 * SG_WIDTH + 3 * SC_WIDTH
POOL_WINDOWS = (2, 4, 8, 16)
POOL_DIM, POOL_WIDTH = 64, 256
POOL_HALO = 16
HEADS, Q_LORA, KV_LORA, QK_NOPE, QK_ROPE, V_DIM = 6, 384, 256, 128, 64, 128
QK_DIM = QK_NOPE + QK_ROPE
QK_PAD = 256
ODD_IN = POOL_WIDTH + Q_LORA + KV_LORA + QK_ROPE
ODD_IN_PAD = 1024
ROPE_THETA = 10000.0
ATTN_SCALE = QK_DIM ** -0.5
D_FF, N_CHIPS = 2816, 4
FF_SHARD = D_FF // N_CHIPS
ADAM_LR, ADAM_B1, ADAM_B2, ADAM_EPS, ADAM_WD, ADAM_STEP = 0.001, 0.9, 0.999, 1e-08, 0.01, 10
VMEM_LIMIT_V7X = 48 * 2**20
LANES, SUBLANES = 128, 8
MESH = pl.DeviceIdType.MESH
HBM = pl.BlockSpec(memory_space=pltpu.HBM)
VMEM = pl.BlockSpec(memory_space=pltpu.VMEM)

_DIMS = {"nn": (((1,), (0,)), ((), ())), "nt": (((1,), (1,)), ((), ())), "tn": (((0,), (0,)), ((), ()))}


def _dot(a, b, mode="nn"):
    return lax.dot_general(a.astype(BF16), b.astype(BF16), _DIMS[mode], preferred_element_type=F32)


def _call(body, *, name, out_shape, in_specs, out_specs, grid=(), scratch=(), aliases=None, after=(), collective_id=None):
    params = pltpu.CompilerParams(vmem_limit_bytes=VMEM_LIMIT_V7X,
                                  **({"dimension_semantics": ("arbitrary",) * len(grid)} if grid else {}),
                                  **({"collective_id": collective_id} if collective_id is not None else {}))
    n_in, n_after = len(in_specs), len(after)
    kernel_body = body if not after else (lambda *refs: body(*refs[:n_in], *refs[n_in + n_after:]))
    call = pl.pallas_call(kernel_body, name=name, grid=grid, in_specs=list(in_specs) + [pl.BlockSpec(memory_space=pl.ANY)] * n_after,
                          out_specs=out_specs, out_shape=out_shape, scratch_shapes=list(scratch),
                          input_output_aliases=aliases or {}, compiler_params=params)
    return (lambda *ops: call(*ops, *after)) if after else call


def _sds(shape, dtype):
    return jax.ShapeDtypeStruct(tuple(shape), dtype)


def _token_tile(seq):
    return 512 if seq % 512 == 0 else seq


_TAIL_ROWS = 256


def _matmul(name, mode, pairs, pair_specs, grid, out_shape, out_spec, acc_shape, add=None, add_spec=None, after=(), tail=None):
    n, nk = len(pairs), grid[-1]
    n_add = int(add is not None)
    n_tail = len(tail[0]) if tail else 0
    n_in = 2 * n + n_add + n_tail
    n_out = len(out_shape) if tail else 1

    def body(*refs):
        ab = refs[:2 * n]
        add_ref = refs[2 * n] if n_add else None
        tail_refs, outs = refs[2 * n + n_add:n_in], refs[n_in:n_in + n_out]
        first = pl.program_id(0) == 0

        def finish(result):
            if tail is None:
                r = result(slice(None))
                outs[0][...] = (r if add_ref is None else r + add_ref[...]).astype(outs[0].dtype)
                return
            for lo in range(0, acc_shape[0], _TAIL_ROWS):
                rows = slice(lo, min(lo + _TAIL_ROWS, acc_shape[0]))
                r = result(rows)
                tail[2](rows, r if add_ref is None else r + add_ref[rows, :], first, tail_refs, outs)

        def terms(a_ref, b_ref):
            if len(a_ref.shape) == 2 and len(b_ref.shape) == 2:
                return [(a_ref[...], b_ref[...])]
            cols = a_ref.shape[-1] // N_CHIPS
            return [(a_ref[j] if len(a_ref.shape) == 3 else a_ref[:, cols * j:cols * (j + 1)], b_ref[j]) for j in range(N_CHIPS)]

        if nk == 1:
            r = None
            for p in range(n):
                for a_blk, b_blk in terms(ab[2 * p], ab[2 * p + 1]):
                    d = _dot(a_blk, b_blk, mode)
                    r = d if r is None else r + d
            finish(lambda rows: r[rows])
            return
        acc = refs[-1]
        k = pl.program_id(len(grid) - 1)

        @pl.when(k == 0)
        def _():
            acc[...] = jnp.zeros_like(acc)

        for p in range(n):
            acc[...] += _dot(ab[2 * p][...], ab[2 * p + 1][...], mode)

        @pl.when(k == nk - 1)
        def _():
            finish(lambda rows: acc[rows, :])

    ops = [t for pr in pairs for t in pr] + ([add] if n_add else []) + (list(tail[0]) if tail else [])
    specs = [s for pr in pair_specs for s in pr] + ([add_spec] if n_add else []) + (list(tail[1]) if tail else [])
    return _call(body, name=name, grid=grid, in_specs=specs, out_specs=out_spec, out_shape=out_shape,
                 scratch=[pltpu.VMEM(acc_shape, F32)] if nk > 1 else [], after=after)(*ops)


def _row_spec(tm, d):
    return BS((tm, d), lambda i, j, k: (i, 0))


def _vec_spec(d):
    return BS((1, d), lambda i, j, k: (0, 0))


def _norm_tail(gain, T, tm):
    d = gain.shape[-1]

    def fn(rows, r, first, tail_refs, outs):
        outs[0][rows, :] = r
        outs[1][rows, :] = (r * lax.rsqrt(jnp.mean(r * r, axis=-1, keepdims=True) + EPS) * tail_refs[0][...]).astype(BF16)

    return ([gain.reshape(1, d)], [_vec_spec(d)], fn), [_sds((T, d), F32), _sds((T, d), BF16)], [_row_spec(tm, d), _row_spec(tm, d)]


def _norm_bwd_tail(x, gain, dres, tm):
    T, d = x.shape

    def fn(rows, r, first, tail_refs, outs):
        x_ref, g_ref, dres_ref = tail_refs
        xv = x_ref[rows, :]
        rstd = lax.rsqrt(jnp.mean(xv * xv, axis=-1, keepdims=True) + EPS)
        xhat = xv * rstd
        if rows.start == 0:
            @pl.when(first)
            def _():
                outs[1][...] = jnp.zeros_like(outs[1])

        outs[1][...] += jnp.sum(r * xhat, axis=0, keepdims=True)
        dxhat = r * g_ref[...]
        outs[0][rows, :] = dres_ref[rows, :] + rstd * (dxhat - xhat * jnp.mean(dxhat * xhat, axis=-1, keepdims=True))

    return (([x, gain.reshape(1, d), dres], [_row_spec(tm, d), _vec_spec(d), _row_spec(tm, d)], fn),
            [_sds((T, d), F32), _sds((1, d), F32)], [_row_spec(tm, d), _vec_spec(d)])


def _loss_tail(target, tm):
    T, d = target.shape

    def fn(rows, r, first, tail_refs, outs):
        e = r - tail_refs[0][rows, :]
        if rows.start == 0:
            @pl.when(first)
            def _():
                outs[1][...] = jnp.zeros_like(outs[1])

        outs[1][...] += jnp.sum(e * e)
        outs[0][rows, :] = e * (1.0 / d)

    return (([target], [_row_spec(tm, d)], fn), [_sds((T, d), F32), _sds((SUBLANES, LANES), F32)],
            [_row_spec(tm, d), BS((SUBLANES, LANES), lambda i, j, k: (0, 0))])


def _grad_shards(name, a, b, a_spec, b_spec, pick, out_shape, n_steps):
    def body(a_ref, b_ref, o_ref, acc):
        k = pl.program_id(0)

        @pl.when(k == 0)
        def _():
            acc[...] = jnp.zeros_like(acc)

        for j in range(N_CHIPS):
            aj, bj = pick(a_ref, b_ref, j)
            acc[j] += _dot(aj, bj, "tn")

        @pl.when(k == n_steps - 1)
        def _():
            o_ref[...] = acc[...].astype(BF16)

    return _call(body, name=name, grid=(n_steps,), in_specs=[a_spec, b_spec], scratch=[pltpu.VMEM(tuple(out_shape), F32)],
                 out_specs=BS(out_shape, lambda k: (0, 0, 0)), out_shape=pltpu.HBM(tuple(out_shape), BF16))(a, b)


def _mm(name, mode, a, b, out_dtype, tm=1024, tn=1024, tk=512, add=None, after=(), fused=None, hbm_out=False):
    if mode == "tn":
        (K, M), N = a.shape, b.shape[1]
    else:
        (M, K), N = a.shape, (b.shape[1] if mode == "nn" else b.shape[0])
    tm, tn, tk = min(tm, M), min(tn, N), min(tk, K)
    a_spec = BS((tk, tm), lambda i, j, k: (k, i)) if mode == "tn" else BS((tm, tk), lambda i, j, k: (i, k))
    b_spec = BS((tn, tk), lambda i, j, k: (j, k)) if mode == "nt" else BS((tk, tn), lambda i, j, k: (k, j))
    o_spec = BS((tm, tn), lambda i, j, k: (i, j))
    tail, shapes, specs = fused if fused else (None, pltpu.HBM((M, N), out_dtype) if hbm_out else _sds((M, N), out_dtype), o_spec)
    return _matmul(name, mode, [(a, b)], [(a_spec, b_spec)], (M // tm, N // tn, K // tk), shapes, specs, (tm, tn),
                   add=add, add_spec=o_spec if add is not None else None, after=after, tail=tail)


_PASS_ROWS = 256


def _ffn_up(name, h, wg, wu, tm):
    T = h.shape[0]

    def body(h_ref, wg_ref, wu_ref, g_ref, u_ref, a_ref):
        hv = h_ref[...]
        g = _dot(hv, wg_ref[...], "nt")
        u = _dot(hv, wu_ref[...], "nt")
        g_ref[...] = g.astype(BF16)
        u_ref[...] = u.astype(BF16)
        a_ref[...] = (g * (1.0 / (1.0 + jnp.exp(-g))) * u).astype(BF16)

    w_spec = BS((None, FF_SHARD, D_MODEL), lambda j, i: (j, 0, 0))
    o_spec = BS((None, tm, FF_SHARD), lambda j, i: (j, i, 0))
    sh = _sds((N_CHIPS, T, FF_SHARD), BF16)
    return _call(body, name=name, grid=(N_CHIPS, T // tm), in_specs=[BS((tm, D_MODEL), lambda j, i: (i, 0)), w_spec, w_spec],
                 out_specs=[o_spec, o_spec, o_spec], out_shape=[sh, sh, sh])(h, wg, wu)


def _ffn_act_bwd(name, dxo, wd, g, u, tm, after=()):
    T = dxo.shape[0]

    def body(dx_ref, wd_ref, g_ref, u_ref, dg_ref, du_ref):
        da = _dot(dx_ref[...], wd_ref[...], "nt")
        g = g_ref[...].astype(F32)
        sig = 1.0 / (1.0 + jnp.exp(-g))
        dg_ref[...] = (da * u_ref[...].astype(F32) * (sig * (1.0 + g * (1.0 - sig)))).astype(BF16)
        du_ref[...] = (da * (g * sig)).astype(BF16)

    t_spec = BS((None, tm, FF_SHARD), lambda i, j: (j, i, 0))
    sh = _sds((N_CHIPS, T, FF_SHARD), BF16)
    return _call(body, name=name, grid=(T // tm, N_CHIPS),
                 in_specs=[BS((tm, D_MODEL), lambda i, j: (i, 0)), BS((None, FF_SHARD, D_MODEL), lambda i, j: (j, 0, 0)), t_spec, t_spec],
                 out_specs=[t_spec, t_spec], out_shape=[sh, sh], after=after)(dxo, wd, g, u)


def _big_tile(n):
    return min(1024, n)


def _resident_tile(n):
    return min(512, n)


def _resident(shape):
    return BS(shape, lambda i, j, k: (0,) * len(shape), pipeline_mode=pl.Buffered(1))


def _ffn_fwd(l, x, h, wg, wu, wd, fused):
    T = x.shape[0]
    g, u, a = _ffn_up(f"ffn{l}_up", h, wg, wu, _big_tile(T))
    tm = _resident_tile(T)
    tail, shapes, specs = fused(tm)
    outs = _matmul(f"ffn{l}_down", "nn", [(a, wd)],
                   [(BS((N_CHIPS, tm, FF_SHARD), lambda i, j, k: (0, i, 0)), _resident((N_CHIPS, FF_SHARD, D_MODEL)))],
                   (T // tm, 1, 1), shapes, specs, (tm, D_MODEL), add=x, add_spec=_row_spec(tm, D_MODEL), tail=tail)
    return outs, (h, g, u, a)


def _ffn_bwd(l, x, gain, wg, wu, wd, saved, dxo, emit, after=()):
    h, g, u, a = saved
    T = x.shape[0]
    tm = _big_tile(T)
    dg, du = _ffn_act_bwd(f"ffn{l}_act_bwd", dxo, wd, g, u, tm, after=after)
    tk = _big_tile(T)
    shards_spec = BS((N_CHIPS, tk, FF_SHARD), lambda k: (0, k, 0))
    rows_spec = BS((tk, D_MODEL), lambda k: (k, 0))

    def dw(nm, act, rows):
        return _grad_shards(nm, act, rows, shards_spec, rows_spec, lambda a_ref, b_ref, j: (a_ref[j], b_ref[...]),
                            (N_CHIPS, FF_SHARD, D_MODEL), T // tk)

    behind = emit(f"ffn{l}", {f"ffn_w_gate{l}": dw(f"ffn{l}_dwg", dg, h), f"ffn_w_up{l}": dw(f"ffn{l}_dwu", du, h),
                              f"ffn_w_down{l}": dw(f"ffn{l}_dwd", a, dxo)})
    tm = _resident_tile(T)
    act_spec = BS((N_CHIPS, tm, FF_SHARD), lambda i, j, k: (0, i, 0))
    w_spec = _resident((N_CHIPS, FF_SHARD, D_MODEL))
    tail, shapes, specs = _norm_bwd_tail(x, gain, dxo, tm)
    return _matmul(f"ffn{l}_dh", "nn", [(dg, wg), (du, wu)], [(act_spec, w_spec), (act_spec, w_spec)],
                   (T // tm, 1, 1), shapes, specs, (tm, D_MODEL), after=behind, tail=tail)


_INV_SQRT2 = 1.0 / math.sqrt(2.0)
_INV_SQRT_2PI = 1.0 / math.sqrt(2.0 * math.pi)


def _gelu(x):
    return 0.5 * x * (1.0 + lax.erf(x * _INV_SQRT2))


def _gelu_and_grad(x):
    cdf = 0.5 * (1.0 + lax.erf(x * _INV_SQRT2))
    return x * cdf, cdf + x * jnp.exp(-0.5 * x * x) * _INV_SQRT_2PI


def _shift_down(x, k):
    return pltpu.roll(x, k, 0)


def _shift_up(x, k):
    return pltpu.roll(x, x.shape[0] - k, 0)


def _layer_norm_head(xh):
    xc = xh - jnp.mean(xh, axis=-1, keepdims=True)
    rstd = lax.rsqrt(jnp.mean(xc * xc, axis=-1, keepdims=True) + EPS)
    return xc * rstd, rstd


def _even_in(x, gain, w, tm):
    T, d = x.shape
    shard = w.shape[-1]

    def body(x_ref, g_ref, w_ref, o_ref, h_ref):
        xv = x_ref[...]
        hv = (xv * lax.rsqrt(jnp.mean(xv * xv, axis=-1, keepdims=True) + EPS) * g_ref[...]).astype(BF16)
        h_ref[...] = hv
        for j in range(N_CHIPS):
            o_ref[:, shard * j:shard * (j + 1)] = _dot(hv, w_ref[j])

    row = BS((tm, d), lambda i: (i, 0))
    return _call(body, name="even_in", grid=(T // tm,),
                 in_specs=[row, BS((1, d), lambda i: (0, 0)), BS(w.shape, lambda i: (0, 0, 0), pipeline_mode=pl.Buffered(1))],
                 out_specs=[BS((tm, N_CHIPS * shard), lambda i: (i, 0)), row],
                 out_shape=[_sds((T, N_CHIPS * shard), F32), _sds((T, d), BF16)])(x, gain.reshape(1, d), w)


def _even_halo_specs(tm, n_tiles, col_blocks, after):
    rows = tm // SUBLANES
    last = n_tiles * rows - 1
    if after:
        return [BS((SUBLANES, 512), functools.partial(lambda cb, i: (jnp.minimum((i + 1) * rows, last), cb), cb)) for cb in col_blocks]
    return [BS((SUBLANES, 512), functools.partial(lambda cb, i: (jnp.maximum(i * rows - 1, 0), cb), cb)) for cb in col_blocks]


def _even_mixer_fwd(proj, ln_g, w_tril, b_lanes, conv_w, seq, tm):
    T = proj.shape[0]
    tiles_per_seq = seq // tm

    def body(p_ref, hc_ref, hh_ref, lng_ref, w_ref, bb_ref, cw_ref, o_ref):
        first = pl.program_id(0) % tiles_per_seq == 0
        for h in range(SG_HEADS):
            cols = slice(SG_DIM * h, SG_DIM * (h + 1))
            vhat, _ = _layer_norm_head(_gelu(p_ref[:, SG_WIDTH + SG_DIM * h:SG_WIDTH + SG_DIM * (h + 1)]))
            vln = (vhat * lng_ref[:, cols]).astype(BF16)
            for k in range(tm // SG_CHUNK):
                rows = slice(SG_CHUNK * k, SG_CHUNK * (k + 1))
                mixed = _dot(w_ref[h], vln[rows]) + bb_ref[h]
                o_ref[rows, cols] = (_gelu(p_ref[rows, cols]) * mixed).astype(BF16)
        z = p_ref[:, 1536:2048] * p_ref[:, 2048:2560]
        zz = jnp.concatenate([jnp.where(first, 0.0, hc_ref[...] * hh_ref[...]), z], axis=0)
        y = cw_ref[0:1, :] * _shift_down(zz, 2)[SUBLANES:] + cw_ref[1:2, :] * _shift_down(zz, 1)[SUBLANES:] + cw_ref[2:3, :] * z
        o_ref[:, SG_WIDTH:] = (p_ref[:, 1024:1536] * y).astype(BF16)

    full = lambda shape: BS(shape, lambda i: (0,) * len(shape))
    return _call(body, name="even_mixer_fwd", grid=(T // tm,),
                 in_specs=[BS((tm, EVEN_IN), lambda i: (i, 0))] + _even_halo_specs(tm, T // tm, (3, 4), after=False)
                 + [full((1, SG_WIDTH)), full((SG_HEADS, SG_CHUNK, SG_CHUNK)), full((SG_HEADS, SG_CHUNK, SG_DIM)), full((SUBLANES, SC_WIDTH))],
                 out_specs=BS((tm, D_MODEL), lambda i: (i, 0)), out_shape=_sds((T, D_MODEL), BF16))(
        proj, proj, proj, ln_g, w_tril, b_lanes, conv_w)


def _even_mixer_bwd(proj, dmix, ln_g, w_tril, b_lanes, conv_w, seq, tm):
    T = proj.shape[0]
    n_tiles, tiles_per_seq = T // tm, seq // tm

    def body(p_ref, dm_ref, hc_ref, hh_ref, nd_ref, nb_ref, lng_ref, w_ref, bb_ref, cw_ref,
             dp_ref, dw_ref, db_ref, dlng_ref, dcw_ref):
        i = pl.program_id(0)
        first = i % tiles_per_seq == 0
        last = i % tiles_per_seq == tiles_per_seq - 1

        @pl.when(i == 0)
        def _():
            dw_ref[...] = jnp.zeros_like(dw_ref)
            db_ref[...] = jnp.zeros_like(db_ref)
            dlng_ref[...] = jnp.zeros_like(dlng_ref)
            dcw_ref[...] = jnp.zeros_like(dcw_ref)

        for h in range(SG_HEADS):
            cols = slice(SG_DIM * h, SG_DIM * (h + 1))
            vcols = slice(SG_WIDTH + SG_DIM * h, SG_WIDTH + SG_DIM * (h + 1))
            lng = lng_ref[:, cols]
            for k in range(tm // SG_CHUNK):
                rows = slice(SG_CHUNK * k, SG_CHUNK * (k + 1))
                gelu_v, dgelu_v = _gelu_and_grad(p_ref[rows, vcols])
                vhat, rstd = _layer_norm_head(gelu_v)
                vln = (vhat * lng).astype(BF16)
                mixed = _dot(w_ref[h], vln) + bb_ref[h]
                gelu_u, dgelu_u = _gelu_and_grad(p_ref[rows, cols])
                da = dm_ref[rows, cols]
                dp_ref[rows, cols] = (da * mixed * dgelu_u).astype(BF16)
                dmixed = da * gelu_u
                db_ref[h] += dmixed
                dw_ref[h] += _dot(dmixed, vln, "nt")
                dvln = _dot(w_ref[h], dmixed, "tn")
                dlng_ref[:, cols] += jnp.sum(dvln * vhat, axis=0, keepdims=True)
                dvhat = dvln * lng
                dgv = rstd * (dvhat - jnp.mean(dvhat, axis=-1, keepdims=True)
                              - vhat * jnp.mean(dvhat * vhat, axis=-1, keepdims=True))
                dp_ref[rows, vcols] = (dgv * dgelu_v).astype(BF16)

        b = p_ref[:, 1024:1536]
        c = p_ref[:, 1536:2048]
        hv = p_ref[:, 2048:2560]
        z = c * hv
        zz = jnp.concatenate([jnp.where(first, 0.0, hc_ref[...] * hh_ref[...]), z], axis=0)
        z1 = _shift_down(zz, 1)[SUBLANES:]
        z2 = _shift_down(zz, 2)[SUBLANES:]
        w0, w1, w2 = cw_ref[0:1, :], cw_ref[1:2, :], cw_ref[2:3, :]
        dbo = dm_ref[:, SG_WIDTH:]
        dy = dbo * b
        dd = jnp.concatenate([dy, jnp.where(last, 0.0, nd_ref[...] * nb_ref[...])], axis=0)
        dz = w2 * dy + w1 * _shift_up(dd, 1)[:tm] + w0 * _shift_up(dd, 2)[:tm]
        dp_ref[:, 1024:1536] = (dbo * (w0 * z2 + w1 * z1 + w2 * z)).astype(BF16)
        dp_ref[:, 1536:2048] = (dz * hv).astype(BF16)
        dp_ref[:, 2048:2560] = (dz * c).astype(BF16)
        dcw_ref[0:1, :] += jnp.sum(dy * z2, axis=0, keepdims=True)
        dcw_ref[1:2, :] += jnp.sum(dy * z1, axis=0, keepdims=True)
        dcw_ref[2:3, :] += jnp.sum(dy * z, axis=0, keepdims=True)

        @pl.when(i == n_tiles - 1)
        def _():
            t_idx = lax.broadcasted_iota(jnp.int32, (SG_CHUNK, SG_CHUNK), 0)
            s_idx = lax.broadcasted_iota(jnp.int32, (SG_CHUNK, SG_CHUNK), 1)
            for h in range(SG_HEADS):
                dw_ref[h] = jnp.where(t_idx >= s_idx, dw_ref[h], 0.0)

    full = lambda shape: BS(shape, lambda i: (0,) * len(shape))
    sq = (SG_HEADS, SG_CHUNK, SG_CHUNK)
    return _call(body, name="even_mixer_bwd", grid=(n_tiles,),
                 in_specs=[BS((tm, EVEN_IN), lambda i: (i, 0)), BS((tm, D_MODEL), lambda i: (i, 0))]
                 + _even_halo_specs(tm, n_tiles, (3, 4), after=False)
                 + _even_halo_specs(tm, n_tiles, (1,), after=True) + _even_halo_specs(tm, n_tiles, (2,), after=True)
                 + [full((1, SG_WIDTH)), full(sq), full(sq), full((SUBLANES, SC_WIDTH))],
                 out_specs=[BS((tm, EVEN_IN), lambda i: (i, 0)), full(sq), full(sq), full((1, SG_WIDTH)), full((SUBLANES, SC_WIDTH))],
                 out_shape=[_sds((T, EVEN_IN), BF16), _sds(sq, F32), _sds(sq, F32), _sds((1, SG_WIDTH), F32), _sds((SUBLANES, SC_WIDTH), F32)])(
        proj, dmix, proj, proj, dmix, proj, ln_g, w_tril, b_lanes, conv_w)


def _pool_select(vals):
    lane = lax.broadcasted_iota(jnp.int32, vals[0].shape, 1)
    out = vals[-1]
    for g in range(len(vals) - 2, -1, -1):
        out = jnp.where(lane < POOL_DIM * (g + 1), vals[g], out)
    return out


def _pool_counts(pos1):
    lane = lax.broadcasted_iota(jnp.int32, (pos1.shape[0], POOL_WIDTH), 1)
    win = _pool_select([jnp.full(lane.shape, float(w), F32) for w in POOL_WINDOWS])
    return jnp.minimum(pos1, win)


def _pool_means(zz, counts):
    s2 = zz + _shift_down(zz, 1)
    s4 = s2 + _shift_down(s2, 2)
    s8 = s4 + _shift_down(s4, 4)
    s16 = s8 + _shift_down(s8, 8)
    return _pool_select([s2, s4, s8, s16])[POOL_HALO:] / counts


def _pool_halo_spec(tm, n_tiles, after):
    rows = tm // POOL_HALO
    if after:
        return BS((POOL_HALO, POOL_WIDTH), lambda i: (jnp.minimum((i + 1) * rows, n_tiles * rows - 1), 0))
    return BS((POOL_HALO, POOL_WIDTH), lambda i: (jnp.maximum(i * rows - 1, 0), 0))


def _pool_fwd(proj, w_diag, scale, seq, tm):
    T = proj.shape[0]
    tiles_per_seq = seq // tm

    def body(z_ref, zh_ref, w_ref, s_ref, o_ref):
        t = pl.program_id(0) % tiles_per_seq
        z = z_ref[...]
        zz = jnp.concatenate([jnp.where(t == 0, 0.0, zh_ref[...]), z], axis=0)
        pos1 = (lax.broadcasted_iota(jnp.int32, (tm, 1), 0) + (t * tm + 1)).astype(F32)
        pooled = _pool_means(zz, _pool_counts(pos1)) - z
        o_ref[...] = (_dot(pooled, w_ref[...]) * s_ref[...]).astype(BF16)

    full = lambda shape: BS(shape, lambda i: (0,) * len(shape))
    return _call(body, name="pool_fwd", grid=(T // tm,),
                 in_specs=[BS((tm, POOL_WIDTH), lambda i: (i, 0)), _pool_halo_spec(tm, T // tm, False),
                           full((POOL_WIDTH, POOL_WIDTH)), full((1, POOL_WIDTH))],
                 out_specs=BS((tm, POOL_WIDTH), lambda i: (i, 0)), out_shape=_sds((T, D_MODEL), BF16))(proj, proj, w_diag, scale)


def _pool_bwd(proj, dmix, w_diag, scale, seq, tm):
    T = proj.shape[0]
    n_tiles, tiles_per_seq = T // tm, seq // tm

    def body(z_ref, zh_ref, do_ref, don_ref, w_ref, s_ref, dz_ref, dw_ref, ds_ref):
        i = pl.program_id(0)
        t = i % tiles_per_seq

        @pl.when(i == 0)
        def _():
            dw_ref[...] = jnp.zeros_like(dw_ref)
            ds_ref[...] = jnp.zeros_like(ds_ref)

        z = z_ref[...]
        zz = jnp.concatenate([jnp.where(t == 0, 0.0, zh_ref[...]), z], axis=0)
        pos1 = (lax.broadcasted_iota(jnp.int32, (tm, 1), 0) + (t * tm + 1)).astype(F32)
        counts = _pool_counts(pos1)
        pooled = _pool_means(zz, counts) - z
        dout = do_ref[...].astype(F32)
        ds_ref[...] += jnp.sum(dout * _dot(pooled, w_ref[...]), axis=0, keepdims=True)
        dlin = dout * s_ref[...]
        dw_ref[...] += _dot(pooled, dlin, "tn")
        dpooled = _dot(dlin, w_ref[...], "nt")
        dpooled_n = _dot(don_ref[...].astype(F32) * s_ref[...], w_ref[...], "nt")
        pos1_n = (lax.broadcasted_iota(jnp.int32, (POOL_HALO, 1), 0) + ((t + 1) * tm + 1)).astype(F32)
        dmean_n = jnp.where(t == tiles_per_seq - 1, 0.0, dpooled_n / _pool_counts(pos1_n))
        dd = jnp.concatenate([dpooled / counts, dmean_n], axis=0)
        r2 = dd + _shift_up(dd, 1)
        r4 = r2 + _shift_up(r2, 2)
        r8 = r4 + _shift_up(r4, 4)
        r16 = r8 + _shift_up(r8, 8)
        dz_ref[...] = (_pool_select([r2, r4, r8, r16])[:tm] - dpooled).astype(BF16)

    full = lambda shape: BS(shape, lambda i: (0,) * len(shape))
    return _call(body, name="pool_bwd", grid=(n_tiles,),
                 in_specs=[BS((tm, POOL_WIDTH), lambda i: (i, 0)), _pool_halo_spec(tm, n_tiles, False),
                           BS((tm, POOL_WIDTH), lambda i: (i, 0)), _pool_halo_spec(tm, n_tiles, True),
                           full((POOL_WIDTH, POOL_WIDTH)), full((1, POOL_WIDTH))],
                 out_specs=[BS((tm, POOL_WIDTH), lambda i: (i, 0)), full((POOL_WIDTH, POOL_WIDTH)), full((1, POOL_WIDTH))],
                 out_shape=[_sds((T, POOL_WIDTH), BF16), _sds((POOL_WIDTH, POOL_WIDTH), F32), _sds((1, POOL_WIDTH), F32)])(
        proj, proj, dmix, dmix, w_diag, scale)


def _rope_partner(r):
    lane = lax.broadcasted_iota(jnp.int32, r.shape, 1)
    return jnp.where(lane < QK_ROPE // 2, pltpu.roll(r, LANES - QK_ROPE // 2, 1), pltpu.roll(r, QK_ROPE // 2, 1))


def _rope(x, cos, sin_signed):
    r = x[:, QK_NOPE:]
    return jnp.concatenate([x[:, :QK_NOPE], r * cos + _rope_partner(r) * sin_signed], axis=1)


def _rope_transposed(dx, cos, sin_signed):
    dr = dx[:, QK_NOPE:]
    return jnp.concatenate([dx[:, :QK_NOPE], dr * cos + _rope_partner(dr * sin_signed)], axis=1)


def _head_norm(x):
    r = lax.rsqrt(jnp.sum(x * x, axis=-1, keepdims=True) * (1.0 / QK_DIM) + EPS)
    return x * r, r


def _head_norm_bwd(dy, xhat, r, gain):
    dxhat = dy * gain
    return r * (dxhat - xhat * (jnp.sum(dxhat * xhat, axis=-1, keepdims=True) * (1.0 / QK_DIM)))


def _latents(p_ref, qag_ref, kvag_ref):
    ql = p_ref[:, POOL_WIDTH:POOL_WIDTH + Q_LORA]
    kvl = p_ref[:, POOL_WIDTH + Q_LORA:POOL_WIDTH + Q_LORA + KV_LORA]
    rq = lax.rsqrt(jnp.mean(ql * ql, axis=-1, keepdims=True) + EPS)
    rkv = lax.rsqrt(jnp.mean(kvl * kvl, axis=-1, keepdims=True) + EPS)
    return ql * rq, rq, kvl * rkv, rkv


def _mla_specs(tm):
    full = lambda shape: BS(shape, lambda i, h: (0,) * len(shape))
    return [BS((tm, ODD_IN_PAD), lambda i, h: (i, 0)), BS((tm, LANES), lambda i, h: (i, 0)), BS((tm, LANES), lambda i, h: (i, 0)),
            full((1, Q_LORA)), full((1, KV_LORA)), BS((None, Q_LORA, QK_PAD), lambda i, h: (h, 0, 0)),
            BS((None, KV_LORA, QK_PAD), lambda i, h: (h, 0, 0)), full((1, QK_PAD)), full((1, QK_PAD))]


def _mla_qkv_fwd(proj, cos, sin_signed, qa_g, kva_g, q_b, kv_b, q_g, k_g, tm):
    T = proj.shape[0]

    def body(p_ref, cos_ref, sin_ref, qag_ref, kvag_ref, qb_ref, kvb_ref, qg_ref, kg_ref, q_ref, k_ref, v_ref, qn_s, kvn_s):
        @pl.when(pl.program_id(1) == 0)
        def _():
            qhat, _, kvhat, _ = _latents(p_ref, qag_ref, kvag_ref)
            qn_s[...] = (qhat * qag_ref[...]).astype(BF16)
            kvn_s[...] = (kvhat * kvag_ref[...]).astype(BF16)

        cos, sin = cos_ref[...], sin_ref[...]
        qhat, _ = _head_norm(_dot(qn_s[...], qb_ref[...]))
        q_ref[...] = _rope(qhat * qg_ref[...], cos, sin).astype(BF16)
        kv = _dot(kvn_s[...], kvb_ref[...])
        khat, _ = _head_norm(jnp.concatenate([kv[:, :QK_NOPE], p_ref[:, ODD_IN_PAD - LANES:]], axis=1))
        k_ref[...] = _rope(khat * kg_ref[...], cos, sin).astype(BF16)
        v_ref[...] = kv[:, QK_NOPE:].astype(BF16)

    qk_spec = BS((None, tm, QK_PAD), lambda i, h: (h, i, 0))
    return _call(body, name="mla_qkv_fwd", grid=(T // tm, HEADS), in_specs=_mla_specs(tm),
                 out_specs=[qk_spec, qk_spec, BS((None, tm, V_DIM), lambda i, h: (h, i, 0))],
                 out_shape=[_sds((HEADS, T, QK_PAD), BF16), _sds((HEADS, T, QK_PAD), BF16), _sds((HEADS, T, V_DIM), BF16)],
                 scratch=[pltpu.VMEM((tm, Q_LORA), BF16), pltpu.VMEM((tm, KV_LORA), BF16)])(
        proj, cos, sin_signed, qa_g, kva_g, q_b, kv_b, q_g, k_g)


def _mla_qkv_bwd(proj, cos, sin_signed, qa_g, kva_g, q_b, kv_b, q_g, k_g, dq, dk, dv, dz_pool, tm):
    T = proj.shape[0]
    n_tiles = T // tm
    chain_rows = min(_PASS_ROWS, tm)

    def body(p_ref, cos_ref, sin_ref, qag_ref, kvag_ref, qb_ref, kvb_ref, qg_ref, kg_ref, dq_ref, dk_ref, dv_ref, dzp_ref,
             dp_ref, dqb_ref, dkvb_ref, dqg_ref, dkg_ref, dqag_ref, dkvag_ref, qn_s, kvn_s, dqn_s, dkvn_s, dkr_s,
             qh_s, kv_s, dqh_s, dkv_s):
        i, h = pl.program_id(0), pl.program_id(1)

        @pl.when((i == 0) & (h == 0))
        def _():
            for ref in (dqb_ref, dkvb_ref, dqg_ref, dkg_ref, dqag_ref, dkvag_ref):
                ref[...] = jnp.zeros_like(ref)

        @pl.when(h == 0)
        def _():
            qhat, _, kvhat, _ = _latents(p_ref, qag_ref, kvag_ref)
            qn_s[...] = (qhat * qag_ref[...]).astype(BF16)
            kvn_s[...] = (kvhat * kvag_ref[...]).astype(BF16)
            dqn_s[...] = jnp.zeros_like(dqn_s)
            dkvn_s[...] = jnp.zeros_like(dkvn_s)
            dkr_s[...] = jnp.zeros_like(dkr_s)

        qh_s[...] = _dot(qn_s[...], qb_ref[...])
        kv_s[...] = _dot(kvn_s[...], kvb_ref[...])
        qg, kg = qg_ref[...], kg_ref[...]

        def chunk(c, gains):
            dqg, dkg = gains
            rows = slice(c * chain_rows, (c + 1) * chain_rows)
            cos, sin = cos_ref[rows, :], sin_ref[rows, :]
            qhat, rq = _head_norm(qh_s[rows, :])
            dqn_head = _rope_transposed(dq_ref[rows, :], cos, sin)
            dqh_s[rows, :] = _head_norm_bwd(dqn_head, qhat, rq, qg).astype(BF16)
            kv = kv_s[rows, :]
            khat, rk = _head_norm(jnp.concatenate([kv[:, :QK_NOPE], p_ref[rows, ODD_IN_PAD - LANES:]], axis=1))
            dkn_head = _rope_transposed(dk_ref[rows, :], cos, sin)
            dkf = _head_norm_bwd(dkn_head, khat, rk, kg)
            dkr_s[rows, :] += dkf[:, QK_NOPE:]
            dkv_s[rows, :] = jnp.concatenate([dkf[:, :QK_NOPE], dv_ref[rows, :]], axis=1).astype(BF16)
            return dqg + dqn_head * qhat, dkg + dkn_head * khat

        dqg = dkg = jnp.zeros((chain_rows, QK_PAD), F32)
        for c in range(tm // chain_rows):
            dqg, dkg = chunk(c, (dqg, dkg))
        dqg_ref[...] += jnp.sum(dqg, axis=0, keepdims=True)
        dkg_ref[...] += jnp.sum(dkg, axis=0, keepdims=True)
        dqb_ref[h] += _dot(qn_s[...], dqh_s[...], "tn")
        dqn_s[...] += _dot(dqh_s[...], qb_ref[...], "nt")
        dkvb_ref[h] += _dot(kvn_s[...], dkv_s[...], "tn")
        dkvn_s[...] += _dot(dkv_s[...], kvb_ref[...], "nt")

        @pl.when(h == HEADS - 1)
        def _():
            qhat_l, rql, kvhat_l, rkvl = _latents(p_ref, qag_ref, kvag_ref)
            dqn, dkvn = dqn_s[...], dkvn_s[...]
            dqag_ref[...] += jnp.sum(dqn * qhat_l, axis=0, keepdims=True)
            dkvag_ref[...] += jnp.sum(dkvn * kvhat_l, axis=0, keepdims=True)
            dqx, dkvx = dqn * qag_ref[...], dkvn * kvag_ref[...]
            dp_ref[:, :POOL_WIDTH] = dzp_ref[...]
            dp_ref[:, POOL_WIDTH:POOL_WIDTH + Q_LORA] = (
                rql * (dqx - qhat_l * jnp.mean(dqx * qhat_l, axis=-1, keepdims=True))).astype(BF16)
            dp_ref[:, POOL_WIDTH + Q_LORA:ODD_IN_PAD - LANES] = (
                rkvl * (dkvx - kvhat_l * jnp.mean(dkvx * kvhat_l, axis=-1, keepdims=True))).astype(BF16)
            dp_ref[:, ODD_IN_PAD - LANES:] = dkr_s[:, :QK_ROPE].astype(BF16)

    full = lambda shape: BS(shape, lambda i, h: (0,) * len(shape))
    qk_spec = BS((None, tm, QK_PAD), lambda i, h: (h, i, 0))
    return _call(body, name="mla_qkv_bwd", grid=(n_tiles, HEADS),
                 in_specs=_mla_specs(tm) + [qk_spec, qk_spec, BS((None, tm, V_DIM), lambda i, h: (h, i, 0)),
                                            BS((tm, POOL_WIDTH), lambda i, h: (i, 0))],
                 out_specs=[BS((tm, ODD_IN), lambda i, h: (i, 0)), full((HEADS, Q_LORA, QK_PAD)), full((HEADS, KV_LORA, QK_PAD)),
                            full((1, QK_PAD)), full((1, QK_PAD)), full((1, Q_LORA)), full((1, KV_LORA))],
                 out_shape=[_sds((T, ODD_IN), BF16),_sds((HEADS, Q_LORA, QK_PAD), F32), _sds((HEADS, KV_LORA, QK_PAD), F32),
                            _sds((1, QK_PAD), F32), _sds((1, QK_PAD), F32), _sds((1, Q_LORA), F32), _sds((1, KV_LORA), F32)],
                 scratch=[pltpu.VMEM((tm, Q_LORA), BF16), pltpu.VMEM((tm, KV_LORA), BF16), pltpu.VMEM((tm, Q_LORA), F32),
                          pltpu.VMEM((tm, KV_LORA), F32), pltpu.VMEM((tm, LANES), F32), pltpu.VMEM((tm, QK_PAD), F32),
                          pltpu.VMEM((tm, QK_PAD), F32), pltpu.VMEM((tm, QK_PAD), BF16), pltpu.VMEM((tm, QK_PAD), BF16)])(
        proj, cos, sin_signed, qa_g, kva_g, q_b, kv_b, q_g, k_g, dq, dk, dv, dz_pool)


_SCALE_LOG2E = ATTN_SCALE * math.log2(math.e)


def _attn_tile(seq):
    return 512 if seq % 512 == 0 else seq


def _causal_mask(s):
    row = lax.broadcasted_iota(jnp.int32, s.shape, 0)
    col = lax.broadcasted_iota(jnp.int32, s.shape, 1)
    return jnp.where(row >= col, s, NEG_INF)


def _tile(i, t):
    return slice(i * t, (i + 1) * t)


def _flash_fwd(q, k, v, mix, batch, seq):
    t = _attn_tile(seq)
    nq = seq // t

    def body(q_ref, k_ref, v_ref, _, o_ref, lse_ref):
        for qi in range(nq):
            rows, before = _tile(qi, t), slice(0, qi * t)
            qv = q_ref[rows, :]
            s_diag = _causal_mask(_dot(qv, k_ref[rows, :], "nt"))
            m = jnp.max(s_diag, axis=-1, keepdims=True)
            if qi:
                s_before = _dot(qv, k_ref[before, :], "nt")
                m = jnp.maximum(m, jnp.max(s_before, axis=-1, keepdims=True))
            p = jnp.exp2((s_diag - m) * _SCALE_LOG2E)
            l = jnp.sum(p, axis=-1, keepdims=True)
            acc = _dot(p, v_ref[rows, :])
            if qi:
                p = jnp.exp2((s_before - m) * _SCALE_LOG2E)
                l = l + jnp.sum(p, axis=-1, keepdims=True)
                acc = acc + _dot(p, v_ref[before, :])
            o_ref[rows, :] = (acc / l).astype(BF16)
            lse_ref[rows, :] = jnp.broadcast_to(m * ATTN_SCALE + jnp.log(l), (t, LANES))

    T = batch * seq
    whole = lambda w: BS((None, seq, w), lambda b, h: (h, b, 0))
    return _call(body, name="flash_fwd", grid=(batch, HEADS),
                 in_specs=[whole(QK_PAD), whole(QK_PAD), whole(V_DIM), pl.BlockSpec(memory_space=pl.ANY)],
                 out_specs=[BS((seq, V_DIM), lambda b, h: (b, POOL_WIDTH // V_DIM + h)), whole(LANES)],
                 out_shape=[_sds((T, D_MODEL), BF16), _sds((HEADS, T, LANES), F32)],
                 aliases={3: 0})(q, k, v, mix)


def _flash_bwd(q, k, v, dmix, mix, lse, batch, seq):
    t = _attn_tile(seq)
    nq = seq // t

    def body(q_ref, k_ref, v_ref, do_ref, o_ref, lse_ref, dq_ref, dk_ref, dv_ref):
        for qi in range(nq):
            rows, before = _tile(qi, t), slice(0, qi * t)
            qv, do = q_ref[rows, :], do_ref[rows, :]
            lse2 = lse_ref[rows, 0:1] * math.log2(math.e)
            delta = jnp.sum(do.astype(F32) * o_ref[rows, :].astype(F32), axis=-1, keepdims=True)

            def block(keys, masked):
                kk = k_ref[keys, :]
                s = _dot(qv, kk, "nt")
                p = jnp.exp2((_causal_mask(s) if masked else s) * _SCALE_LOG2E - lse2)
                ds = p * (_dot(do, v_ref[keys, :], "nt") - delta)
                return _dot(p, do, "tn"), _dot(ds, qv, "tn") * ATTN_SCALE, _dot(ds, kk) * ATTN_SCALE

            dv_ref[rows, :], dk_ref[rows, :], dq = block(rows, True)
            if qi:
                dv, dk, dq_before = block(before, False)
                dv_ref[before, :] += dv
                dk_ref[before, :] += dk
                dq = dq + dq_before
            dq_ref[rows, :] = dq

    T = batch * seq
    whole = lambda w: BS((None, seq, w), lambda b, h: (h, b, 0))
    head_cols = BS((seq, V_DIM), lambda b, h: (b, POOL_WIDTH // V_DIM + h))
    return _call(body, name="flash_bwd", grid=(batch, HEADS),
                 in_specs=[whole(QK_PAD), whole(QK_PAD), whole(V_DIM), head_cols, head_cols, whole(LANES)],
                 out_specs=[whole(QK_PAD), whole(QK_PAD), whole(V_DIM)],
                 out_shape=[_sds((HEADS, T, QK_PAD), F32), _sds((HEADS, T, QK_PAD), F32), _sds((HEADS, T, V_DIM), F32)])(
        q, k, v, dmix, mix, lse)


def _adamw_math(w, g, m, v):
    m = ADAM_B1 * m + (1.0 - ADAM_B1) * g
    v = ADAM_B2 * v + (1.0 - ADAM_B2) * (g * g)
    m_hat = m / (1.0 - ADAM_B1 ** ADAM_STEP)
    v_hat = v / (1.0 - ADAM_B2 ** ADAM_STEP)
    return -ADAM_LR * (m_hat / (jnp.sqrt(v_hat) + ADAM_EPS) + ADAM_WD * w), m, v


def _adamw(name, w, g, m, v, l=0, prev=()):
    L, R, C = w.shape
    tr = 256 if R % 256 == 0 else R

    def body(w_ref, g_ref, m_ref, v_ref, *rest):
        go_ref, d_ref, mo_ref, vo_ref, token = rest[-5:]
        gv = g_ref[...]
        d_ref[...], mo_ref[...], vo_ref[...] = _adamw_math(w_ref[...], gv, m_ref[...], v_ref[...])
        go_ref[...] = gv
        token[...] = jnp.zeros_like(token)

    layer = BS((None, tr, C), lambda i: (l, i, 0))
    return _call(body, name=f"{name}_{l}", grid=(R // tr,),
                 in_specs=[layer, BS((tr, C), lambda i: (i, 0)), layer, layer] + [pl.BlockSpec(memory_space=pl.ANY)] * len(prev),
                 out_specs=[layer] * 4 + [BS((SUBLANES, LANES), lambda i: (0, 0))],
                 out_shape=[_sds((L, R, C), F32)] * 4 + [_sds((SUBLANES, LANES), F32)],
                 aliases={4 + n: n for n in range(len(prev))})(w, g, m, v, *prev)


def _place():
    x, y, c = lax.axis_index("x"), lax.axis_index("y"), lax.axis_index("c")
    other_chips = [(1 - x, y), (x, 1 - y), (1 - x, 1 - y)]
    return x, y, c, other_chips


_SIBLING_PAIR_ID = 0


def _sibling_handshake(sibling):
    barrier = pltpu.get_barrier_semaphore()
    pl.semaphore_signal(barrier, inc=1, device_id=sibling, device_id_type=MESH)
    pl.semaphore_wait(barrier, 1)


def _remote(src, dst, send_sem, recv_sem, dev):
    return pltpu.make_async_remote_copy(src_ref=src, dst_ref=dst, send_sem=send_sem, recv_sem=recv_sem,
                                        device_id=dev, device_id_type=MESH)


def _prefetch_call(body, *, name, grid, in_specs, out_specs, out_shape):
    grid_spec = pltpu.PrefetchScalarGridSpec(num_scalar_prefetch=1, grid=grid, in_specs=in_specs, out_specs=out_specs)
    params = pltpu.CompilerParams(vmem_limit_bytes=VMEM_LIMIT_V7X, dimension_semantics=("arbitrary",) * len(grid))
    return pl.pallas_call(body, name=name, grid_spec=grid_spec, out_shape=out_shape, compiler_params=params)


def _row_tile(rows):
    return 256 if rows % 256 == 0 else rows


def _cast_place(name, w, layer, chip, after=()):
    _, _, rows, C = w.shape
    tr = _row_tile(rows)

    def body(chip_ref, w_ref, *rest):
        rest[-1][...] = w_ref[...].astype(BF16)

    return _prefetch_call(body, name=name, grid=(2, rows // tr),
                          in_specs=[BS((None, None, tr, C), lambda h, i, chip_ref: (layer, h, i, 0))]
                          + [pl.BlockSpec(memory_space=pl.ANY)] * len(after),
                          out_specs=BS((None, None, tr, C), lambda h, i, chip_ref: (chip_ref[0], h, i, 0)),
                          out_shape=pltpu.HBM((N_CHIPS, 2, rows, C), BF16))(chip, w, *after)


SEM = pl.BlockSpec(memory_space=pltpu.SEMAPHORE)


def _split_copy_call(body, *, name, in_specs, out_specs, out_shape, aliases):
    return pl.pallas_call(body, name=name, in_specs=in_specs, out_specs=out_specs, out_shape=out_shape,
                          input_output_aliases=aliases,
                          compiler_params=pltpu.CompilerParams(has_side_effects=pltpu.SideEffectType.DATAFLOW_SIDE_EFFECTING))


def _hbm(arrays):
    return [pltpu.with_memory_space_constraint(a, pltpu.HBM) for a in arrays]


def _gather_send(name, gs, groups, after):
    n = len(gs)

    def body(*refs):
        g, sems, token = refs[:n], refs[n + len(after):n + len(after) + 2 * len(groups)], refs[-1]
        x, y, c, chips = _place()
        me = 2 * x + y
        for gi, members in enumerate(groups):
            for a, i in enumerate(members):
                for k, (px, py) in enumerate(chips):
                    _remote(g[i].at[me, c], g[i].at[me, c], sems[2 * gi].at[3 * a + k], sems[2 * gi + 1].at[3 * a + k],
                            (px, py, c)).start()
        token[...] = jnp.zeros_like(token)

    sem_shapes = [pltpu.SemaphoreType.DMA((3 * len(members),)) for members in groups for _ in range(2)]
    out = _split_copy_call(body, name=name, in_specs=[HBM] * n + [pl.BlockSpec(memory_space=pl.ANY)] * len(after),
                           out_specs=[SEM] * len(sem_shapes) + [HBM] * n + [VMEM],
                           out_shape=sem_shapes + [pltpu.HBM(a.shape, a.dtype) for a in gs] + [_sds((SUBLANES, LANES), F32)],
                           aliases={i: len(sem_shapes) + i for i in range(n)})(*_hbm(gs), *after)
    return out[:len(sem_shapes)], out[len(sem_shapes):-1], out[-1]


def _gather_wait(name, gs, send_sems, recv_sems, after):
    n = len(gs)

    def body(*refs):
        g, ssem, rsem = refs[:n], refs[n], refs[n + 1]
        x, y, c, chips = _place()
        me = 2 * x + y
        for a in range(n):
            for k, (px, py) in enumerate(chips):
                landed = g[a].at[2 * px + py, c]
                cp = _remote(g[a].at[me, c], landed, ssem.at[3 * a + k], rsem.at[3 * a + k], (px, py, c))
                cp.wait_recv()
                cp.wait_send()

    return _split_copy_call(body, name=name, in_specs=[HBM] * n + [SEM, SEM] + [pl.BlockSpec(memory_space=pl.ANY)] * len(after),
                            out_specs=[HBM] * n, out_shape=[pltpu.HBM(a.shape, a.dtype) for a in gs],
                            aliases={i: i for i in range(n)})(*gs, send_sems, recv_sems, *after)


def _gather_pass(name, gs):
    n = len(gs)

    def body(*refs):
        g, send_sems, recv_sems = refs[n:2 * n], refs[-2], refs[-1]
        x, y, c, chips = _place()
        sibling = (x, y, 1 - c)
        _sibling_handshake(sibling)
        passed = [_remote(g[i].at[2 * px + py, c], g[i].at[2 * px + py, c], send_sems.at[3 * i + k], recv_sems.at[3 * i + k], sibling)
                  for i in range(n) for k, (px, py) in enumerate(chips)]
        for cp in passed:
            cp.start()
        for i in range(n):
            for k, (px, py) in enumerate(chips):
                theirs = g[i].at[2 * px + py, 1 - c]
                _remote(theirs, theirs, send_sems.at[3 * i + k], recv_sems.at[3 * i + k], sibling).wait_recv()
        for cp in passed:
            cp.wait_send()

    return _call(body, name=name, in_specs=[HBM] * n, out_specs=[HBM] * n, out_shape=[_sds(a.shape, a.dtype) for a in gs],
                 aliases={i: i for i in range(n)}, collective_id=_SIBLING_PAIR_ID,
                 scratch=[pltpu.SemaphoreType.DMA((3 * n,)), pltpu.SemaphoreType.DMA((3 * n,))])(*gs)


def _scatter_send(name, ps):
    n = len(ps)

    def body(*refs):
        p, r, ssem, rsem, token = refs[:n], refs[n:2 * n], refs[2 * n], refs[2 * n + 1], refs[-1]
        x, y, c, chips = _place()
        for i in range(n):
            for k, (px, py) in enumerate(chips):
                _remote(p[i].at[2 * px + py], r[i].at[k], ssem.at[3 * i + k], rsem.at[3 * i + k], (px, py, c)).start()
        token[...] = jnp.zeros_like(token)

    lands = [lax.empty((N_CHIPS - 1,) + a.shape[1:], a.dtype) for a in ps]
    sem = pltpu.SemaphoreType.DMA((3 * n,))
    out = _split_copy_call(body, name=name, in_specs=[HBM] * (2 * n), out_specs=[SEM, SEM] + [HBM] * (2 * n) + [VMEM],
                           out_shape=[sem, sem] + [pltpu.HBM(a.shape, a.dtype) for a in list(ps) + lands] + [_sds((SUBLANES, LANES), F32)],
                           aliases={i: 2 + i for i in range(2 * n)})(*_hbm(list(ps) + lands))
    return out[0], out[1], out[2:2 + n], out[2 + n:2 + 2 * n], out[-1]


def _scatter_wait(name, ps, lands, send_sems, recv_sems, after):
    n = len(ps)

    def body(*refs):
        p, r, ssem, rsem = refs[:n], refs[n:2 * n], refs[2 * n], refs[2 * n + 1]
        x, y, c, chips = _place()
        for i in range(n):
            for k, (px, py) in enumerate(chips):
                cp = _remote(p[i].at[2 * px + py], r[i].at[k], ssem.at[3 * i + k], rsem.at[3 * i + k], (px, py, c))
                cp.wait_recv()
                cp.wait_send()

    out = _split_copy_call(body, name=name, in_specs=[HBM] * (2 * n) + [SEM, SEM] + [pl.BlockSpec(memory_space=pl.ANY)] * len(after),
                           out_specs=[HBM] * (2 * n), out_shape=[pltpu.HBM(a.shape, a.dtype) for a in list(ps) + list(lands)],
                           aliases={i: i for i in range(2 * n)})(*ps, *lands, send_sems, recv_sems, *after)
    return out[:n], out[n:]


def _exchange_send(name, gs):
    n = len(gs)

    def body(*refs):
        g, r, ssem, rsem, token = refs[:n], refs[n:2 * n], refs[2 * n], refs[2 * n + 1], refs[-1]
        x, y, c, _ = _place()
        for i in range(n):
            _remote(g[i].at[:, 1 - c], r[i], ssem.at[i], rsem.at[i], (x, y, 1 - c)).start()
        token[...] = jnp.zeros_like(token)

    lands = [lax.empty((a.shape[0],) + a.shape[2:], a.dtype) for a in gs]
    sem = pltpu.SemaphoreType.DMA((n,))
    out = _split_copy_call(body, name=name, in_specs=[HBM] * (2 * n), out_specs=[SEM, SEM] + [HBM] * (2 * n) + [VMEM],
                           out_shape=[sem, sem] + [pltpu.HBM(a.shape, a.dtype) for a in list(gs) + lands] + [_sds((SUBLANES, LANES), F32)],
                           aliases={i: 2 + i for i in range(2 * n)})(*_hbm(list(gs) + lands))
    return out[0], out[1], out[2:2 + n], out[2 + n:2 + 2 * n], out[-1]


def _exchange_wait(name, gs, lands, send_sems, recv_sems, after):
    n = len(gs)

    def body(*refs):
        g, r, ssem, rsem = refs[:n], refs[n:2 * n], refs[2 * n], refs[2 * n + 1]
        x, y, c, _ = _place()
        for i in range(n):
            cp = _remote(g[i].at[:, 1 - c], r[i], ssem.at[i], rsem.at[i], (x, y, 1 - c))
            cp.wait_recv()
            cp.wait_send()

    out = _split_copy_call(body, name=name, in_specs=[HBM] * (2 * n) + [SEM, SEM] + [pl.BlockSpec(memory_space=pl.ANY)] * len(after),
                           out_specs=[HBM] * (2 * n), out_shape=[pltpu.HBM(a.shape, a.dtype) for a in list(gs) + list(lands)],
                           aliases={i: i for i in range(2 * n)})(*gs, *lands, send_sems, recv_sems, *after)
    return out[:n], out[n:]


def _sibling_share(name, fs, after=()):
    n = len(fs)

    def body(*refs):
        f, send_sems, recv_sems = refs[n:2 * n], refs[-2], refs[-1]
        x, y, c, _ = _place()
        _sibling_handshake((x, y, 1 - c))
        sends = [_remote(f[i].at[c], f[i].at[c], send_sems.at[i], recv_sems.at[i], (x, y, 1 - c)) for i in range(n)]
        for cp in sends:
            cp.start()
        for i in range(n):
            theirs = f[i].at[1 - c]
            _remote(theirs, theirs, send_sems.at[i], recv_sems.at[i], (x, y, 1 - c)).wait_recv()
        for cp in sends:
            cp.wait_send()

    return _call(body, name=name, in_specs=[HBM] * n, out_specs=[HBM] * n,
                 out_shape=[_sds(a.shape, a.dtype) for a in fs], aliases={i: i for i in range(n)}, after=after,
                 collective_id=_SIBLING_PAIR_ID,
                 scratch=[pltpu.SemaphoreType.DMA((n,)), pltpu.SemaphoreType.DMA((n,))])(*fs)


def _all_reduce_small(name, v):
    rows = v.shape[0] // 2
    halves = (2, rows, LANES)

    def body(v_ref, o_ref, from_sibling, chip_sums, send_sems, recv_sems):
        x, y, c, chips = _place()
        me, sibling = 2 * x + y, (x, y, 1 - c)
        swap = _remote(v_ref.at[1 - c], from_sibling, send_sems.at[0], recv_sems.at[0], sibling)
        swap.start()
        swap.wait()
        chip_sums[me] = v_ref[c] + from_sibling[...]
        sends = [_remote(chip_sums.at[me], chip_sums.at[me], send_sems.at[1 + k], recv_sems.at[1 + k], (px, py, c))
                 for k, (px, py) in enumerate(chips)]
        for cp in sends:
            cp.start()
        for k, (px, py) in enumerate(chips):
            theirs = chip_sums.at[2 * px + py]
            _remote(theirs, theirs, send_sems.at[1 + k], recv_sems.at[1 + k], (px, py, c)).wait_recv()
        for cp in sends:
            cp.wait_send()
        acc = chip_sums[0]
        for j in range(1, N_CHIPS):
            acc = acc + chip_sums[j]
        o_ref[c] = acc
        share = _remote(o_ref.at[c], o_ref.at[c], send_sems.at[4], recv_sems.at[4], sibling)
        share.start()
        share.wait_send()
        _remote(o_ref.at[1 - c], o_ref.at[1 - c], send_sems.at[4], recv_sems.at[4], sibling).wait_recv()

    return _call(body, name=name, in_specs=[VMEM], out_specs=VMEM, out_shape=_sds(halves, F32),
                 scratch=[pltpu.VMEM((rows, LANES), F32), pltpu.VMEM((N_CHIPS, rows, LANES), F32),
                          pltpu.SemaphoreType.DMA((5,)), pltpu.SemaphoreType.DMA((5,))])(v.reshape(halves)).reshape(v.shape)


def _add_halves(name, g, r, c):
    _, _, rows, C = g.shape
    tr = _row_tile(rows)

    def body(c_ref, g_ref, r_ref, o_ref):
        o_ref[...] = (g_ref[...].astype(F32) + r_ref[...].astype(F32)).astype(BF16)

    spec = BS((None, tr, C), lambda j, i, c_ref: (j, i, 0))
    return _prefetch_call(body, name=name, grid=(N_CHIPS, rows // tr),
                          in_specs=[BS((None, None, tr, C), lambda j, i, c_ref: (j, c_ref[0], i, 0)), spec], out_specs=spec,
                          out_shape=pltpu.HBM((N_CHIPS, rows, C), BF16))(c, g, r)


def _sum_partials(name, p, r, chip_c):
    _, rows, C = p.shape
    tr = _row_tile(rows)

    def body(s_ref, p_ref, r_ref, o_ref):
        acc = p_ref[...].astype(F32)
        for k in range(N_CHIPS - 1):
            acc = acc + r_ref[k].astype(F32)
        o_ref[...] = acc

    return _prefetch_call(body, name=name, grid=(rows // tr,),
                          in_specs=[BS((None, tr, C), lambda i, s: (s[0], i, 0)), BS((N_CHIPS - 1, tr, C), lambda i, s: (0, i, 0))],
                          out_specs=BS((None, tr, C), lambda i, s: (s[1], i, 0)), out_shape=pltpu.HBM((2, rows, C), F32))(chip_c, p, r)


_SHARDED = ("even_w_in", "even_w_out", "odd_w_in", "q_b", "kv_b", "odd_w_out", "ffn_w_gate", "ffn_w_up", "ffn_w_down")
_REPLICATED = ("mix_norm", "ffn_norm", "sg_ln_g", "sg_w_s", "sg_b_s", "pool_w", "q_norm", "k_norm")
_SMALL_SHARDED = ("sc_conv_w", "pool_scale", "q_a_norm", "kv_a_norm")
_WEIGHTS = ("mix_norm", "ffn_norm", "even_w_in", "sg_ln_g", "sg_w_s", "sg_b_s", "sc_conv_w", "even_w_out", "odd_w_in", "pool_w",
            "pool_scale", "q_a_norm", "q_b", "kv_a_norm", "kv_b", "q_norm", "k_norm", "odd_w_out", "ffn_w_gate", "ffn_w_up",
            "ffn_w_down")


def _pad_rows(flat, width, align):
    n = flat.shape[0]
    rows = -(-n // (width * align)) * align
    return jnp.pad(flat, (0, rows * width - n)).reshape(rows, width)


_GROUPS = {"even": ("even_w_in", "even_w_out"),
           "ffn0": ("ffn_w_gate0", "ffn_w_up0", "ffn_w_down0"),
           "odd": ("odd_w_in", "q_b", "kv_b", "odd_w_out"),
           "ffn1": ("ffn_w_gate1", "ffn_w_up1", "ffn_w_down1")}


def _place_shards(shards, names, chip, after):
    placed = []
    for n in names:
        weight, layer = (n[:-1], int(n[-1])) if n[-1].isdigit() else (n, 0)
        a = shards[weight]
        placed.append(_cast_place(f"place_{n}", a.reshape(a.shape[0], 2, a.shape[1] // 2, a.shape[2]), layer, chip, after))
    return placed


def _whole_weights(gathered):
    out = {n: a.reshape(N_CHIPS, -1, a.shape[-1]) for n, a in gathered.items()}
    for n in ("q_b", "kv_b"):
        if n in out:
            out[n] = out[n].transpose(1, 0, 2).reshape(out[n].shape[1], -1)
    for n in ("even_w_out", "odd_w_in", "odd_w_out"):
        if n in out:
            out[n] = out[n].reshape(-1, out[n].shape[-1])
    return out


def _forward_backward(x, positions, target, small, fetch, emit, advance):
    batch, seq, _ = x.shape
    T = batch * seq
    tm = _token_tile(seq)
    x0 = x.reshape(T, D_MODEL)

    inv_freq = ROPE_THETA ** (-jnp.arange(0, QK_ROPE, 2, dtype=F32) / QK_ROPE)
    ang = (positions.astype(F32)[..., None] * inv_freq).reshape(T, QK_ROPE // 2)
    cos, sin = jnp.cos(ang), jnp.sin(ang)
    pad = jnp.zeros((T, LANES - QK_ROPE), F32)
    cos_t = jnp.concatenate([cos, cos, pad], axis=1)
    sin_t = jnp.concatenate([-sin, sin, pad], axis=1)

    tril = jnp.tril(jnp.ones((SG_CHUNK, SG_CHUNK), bool))
    w_tril = jnp.where(tril[None], small["sg_w_s"][0], 0.0).astype(BF16)
    b_lanes = jnp.broadcast_to(small["sg_b_s"][0][:, :, None], (SG_HEADS, SG_CHUNK, SG_DIM))
    conv_w = jnp.pad(small["sc_conv_w"][0], ((0, SUBLANES - CONV_TAPS), (0, 0)))
    ln_g = small["sg_ln_g"]
    pool_diag = jnp.zeros((POOL_WIDTH, POOL_WIDTH), F32)
    for g in range(len(POOL_WINDOWS)):
        pool_diag = pool_diag.at[POOL_DIM * g:POOL_DIM * (g + 1), POOL_DIM * g:POOL_DIM * (g + 1)].set(small["pool_w"][0, g])
    pool_diag = pool_diag.astype(BF16)
    pool_scale = small["pool_scale"]
    q_g = jnp.pad(small["q_norm"], ((0, 0), (0, QK_PAD - QK_DIM)))
    k_g = jnp.pad(small["k_norm"], ((0, 0), (0, QK_PAD - QK_DIM)))
    qa_g, kva_g = small["q_a_norm"], small["kv_a_norm"]
    in_shard = EVEN_IN // N_CHIPS

    def ffn_weights(l, w):
        return w[f"ffn_w_gate{l}"], w[f"ffn_w_up{l}"], w[f"ffn_w_down{l}"]

    W = fetch("even", ())
    w_in_even = W["even_w_in"]
    tb = _big_tile(T)
    proj0, h0 = _even_in(x0, small["mix_norm"][0], w_in_even, _resident_tile(T))
    mix0 = _even_mixer_fwd(proj0, ln_g, w_tril, b_lanes, conv_w, seq, tm)
    w_out_even = W["even_w_out"]
    x1, h1 = _mm("even_out", "nn", mix0, w_out_even, F32, tk=1024, add=x0, fused=_norm_tail(small["ffn_norm"][0], T, tb))
    ffn0 = ffn_weights(0, fetch("ffn0", (x1,)))
    (x2, h2), ffn0_saved = _ffn_fwd(0, x1, h1, *ffn0, lambda tile: _norm_tail(small["mix_norm"][1], T, tile))
    W = fetch("odd", (x2,))
    w_in_odd = jnp.pad(W["odd_w_in"], ((0, 0), (0, ODD_IN_PAD - ODD_IN)))
    q_b = jnp.pad(W["q_b"].reshape(Q_LORA, HEADS, QK_DIM).transpose(1, 0, 2), ((0, 0), (0, 0), (0, QK_PAD - QK_DIM)))
    kv_b = W["kv_b"].reshape(KV_LORA, HEADS, QK_NOPE + V_DIM).transpose(1, 0, 2)
    proj1 = _mm("odd_in", "nn", h2, w_in_odd, F32, tk=1024)
    mix1 = _pool_fwd(proj1, pool_diag, pool_scale, seq, tm)
    q, k, v = _mla_qkv_fwd(proj1, cos_t, sin_t, qa_g, kva_g, q_b, kv_b, q_g, k_g, tm)
    mix1, lse = _flash_fwd(q, k, v, mix1, batch, seq)
    x3, h3 = _mm("odd_out", "nn", mix1, W["odd_w_out"], F32, tk=1024, add=x2, fused=_norm_tail(small["ffn_norm"][1], T, tb))
    ffn1 = ffn_weights(1, fetch("ffn1", (x3,)))
    (dy, sq), ffn1_saved = _ffn_fwd(1, x3, h3, *ffn1, lambda tile: _loss_tail(target.reshape(T, D_MODEL), tile))

    G = {}
    dx3, dffn_g1 = _ffn_bwd(1, x3, small["ffn_norm"][1], *ffn1, ffn1_saved, dy, emit)
    dmix1 = _mm("odd_out_dx", "nt", dx3, W["odd_w_out"], BF16, tk=1024, after=advance(dx3))
    dw_out_odd = _mm("odd_out_dw", "tn", mix1, dx3, BF16, hbm_out=True)
    dq, dk, dv = _flash_bwd(q, k, v, dmix1, mix1, lse, batch, seq)
    dz_pool, dpool_diag, G["pool_scale"] = _pool_bwd(proj1, dmix1, pool_diag, pool_scale, seq, tm)
    dproj1, dq_b, dkv_b, dq_g, dk_g, G["q_a_norm"], G["kv_a_norm"] = _mla_qkv_bwd(
        proj1, cos_t, sin_t, qa_g, kva_g, q_b, kv_b, q_g, k_g, dq, dk, dv, dz_pool, tm)
    G["pool_w"] = jnp.stack([dpool_diag[POOL_DIM * g:POOL_DIM * (g + 1), POOL_DIM * g:POOL_DIM * (g + 1)]
                             for g in range(len(POOL_WINDOWS))])[None]
    G["q_norm"], G["k_norm"] = dq_g[:, :QK_DIM], dk_g[:, :QK_DIM]
    dw_in_odd = _mm("odd_in_dw", "tn", h2, dproj1, BF16, tn=ODD_IN, hbm_out=True)

    def shard_major(g, cols):
        return g.reshape(g.shape[0], N_CHIPS, cols).transpose(1, 0, 2).astype(BF16)

    behind = emit("odd", {"odd_w_in": dw_in_odd.reshape(N_CHIPS, -1, ODD_IN),
                          "q_b": shard_major(dq_b[:, :, :QK_DIM].transpose(1, 0, 2).reshape(Q_LORA, HEADS * QK_DIM), HEADS * QK_DIM // N_CHIPS),
                          "kv_b": shard_major(dkv_b.transpose(1, 0, 2).reshape(KV_LORA, HEADS * (QK_NOPE + V_DIM)),
                                              HEADS * (QK_NOPE + V_DIM) // N_CHIPS),
                          "odd_w_out": dw_out_odd.reshape(N_CHIPS, -1, D_MODEL)})
    dx2, dmix_g1 = _mm("odd_in_dx", "nt", dproj1, W["odd_w_in"], F32, tk=ODD_IN, after=behind,
                       fused=_norm_bwd_tail(x2, small["mix_norm"][1], dx3, tb))
    dx1, dffn_g0 = _ffn_bwd(0, x1, small["ffn_norm"][0], *ffn0, ffn0_saved, dx2, emit, after=advance(dx2))
    dmix0 = _mm("even_out_dx", "nt", dx1, w_out_even, F32, tk=1024, after=advance(dx1))
    dw_out_even = _mm("even_out_dw", "tn", mix0, dx1, BF16, hbm_out=True)
    dproj0, dw_s, db_lanes, G["sg_ln_g"], dconv = _even_mixer_bwd(proj0, dmix0, ln_g, w_tril, b_lanes, conv_w, seq, min(tm, 256))
    G["sg_w_s"] = dw_s[None]
    G["sg_b_s"] = jnp.sum(db_lanes, axis=-1)[None]
    G["sc_conv_w"] = dconv[None, :CONV_TAPS]
    tr = _resident_tile(T)
    tail, shapes, specs = _norm_bwd_tail(x0, small["mix_norm"][0], dx1, tr)
    dx0, dmix_g0 = _matmul("even_in_dx", "nt", [(dproj0, w_in_even)],
                           [(_row_spec(tr, EVEN_IN), _resident((N_CHIPS, D_MODEL, in_shard)))],
                           (T // tr, 1, 1), shapes, specs, (tr, D_MODEL), tail=tail)
    tk = min(512, T)
    dw_in_even = _grad_shards(
        "even_in_dw", h0, dproj0, BS((tk, D_MODEL), lambda k: (k, 0)), BS((tk, EVEN_IN), lambda k: (k, 0)),
        lambda a_ref, b_ref, j: (a_ref[...], b_ref[:, in_shard * j:in_shard * (j + 1)]), (N_CHIPS, D_MODEL, in_shard), T // tk)
    emit("even", {"even_w_in": dw_in_even, "even_w_out": dw_out_even.reshape(N_CHIPS, -1, D_MODEL)})
    G["mix_norm"] = jnp.concatenate([dmix_g0, dmix_g1], axis=0)
    G["ffn_norm"] = jnp.concatenate([dffn_g0, dffn_g1], axis=0)
    return sq[0, 0], dx0.reshape(batch, seq, D_MODEL), G


def _small_vector(parts, names):
    flat = jnp.concatenate([parts[n].astype(F32).reshape(-1) for n in names])
    return _pad_rows(flat, LANES, 2 * SUBLANES)


def _split_small(vec, like, names):
    out, off, flat = {}, 0, vec.reshape(-1)
    for n in names:
        size = math.prod(like[n].shape)
        out[n] = flat[off:off + size].reshape(like[n].shape)
        off += size
    return out


def _whole_shape(a):
    return a.shape[:-1] + (a.shape[-1] * N_CHIPS,)


def kernel(x, positions, mix_norm, ffn_norm, even_w_in, sg_ln_g, sg_w_s, sg_b_s, sc_conv_w, even_w_out, odd_w_in, pool_w, pool_scale, q_a_norm, q_b, kv_a_norm, kv_b, q_norm, k_norm, odd_w_out, ffn_w_gate, ffn_w_up, ffn_w_down, loss_target, m_mix_norm, m_ffn_norm, m_even_w_in, m_sg_ln_g, m_sg_w_s, m_sg_b_s, m_sc_conv_w, m_even_w_out, m_odd_w_in, m_pool_w, m_pool_scale, m_q_a_norm, m_q_b, m_kv_a_norm, m_kv_b, m_q_norm, m_k_norm, m_odd_w_out, m_ffn_w_gate, m_ffn_w_up, m_ffn_w_down, v_mix_norm, v_ffn_norm, v_even_w_in, v_sg_ln_g, v_sg_w_s, v_sg_b_s, v_sc_conv_w, v_even_w_out, v_odd_w_in, v_pool_w, v_pool_scale, v_q_a_norm, v_q_b, v_kv_a_norm, v_kv_b, v_q_norm, v_k_norm, v_odd_w_out, v_ffn_w_gate, v_ffn_w_up, v_ffn_w_down):
    args = dict(locals())
    w = {n: args[n] for n in _WEIGHTS}
    m = {n: args["m_" + n] for n in _WEIGHTS}
    v = {n: args["v_" + n] for n in _WEIGHTS}
    cx, cy, cc = lax.axis_index("x"), lax.axis_index("y"), lax.axis_index("c")
    chip = 2 * cx + cy
    transposed = ("ffn_w_gate", "ffn_w_up")
    for n in transposed:
        w[n], m[n], v[n] = (jnp.swapaxes(t[n], 1, 2) for t in (w, m, v))

    chip_arr = chip.astype(jnp.int32).reshape(1)
    c_arr = cc.astype(jnp.int32).reshape(1)
    group_names = list(_GROUPS)
    placed = {}
    for n in _SMALL_SHARDED:
        a = w[n]
        whole = jnp.zeros(a.shape[:-1] + (N_CHIPS, a.shape[-1]), F32)
        whole = lax.dynamic_update_slice_in_dim(whole, a[..., None, :], chip, axis=a.ndim - 1)
        placed[n] = jnp.where(cc == 0, whole, 0.0).reshape(_whole_shape(a))
    small_whole = _all_reduce_small("gather_small_weights", _small_vector(placed, _SMALL_SHARDED))
    small = dict({n: w[n] for n in _REPLICATED}, **_split_small(small_whole, placed, _SMALL_SHARDED))

    first, rest = list(_GROUPS[group_names[0]]), [n for g in group_names[1:] for n in _GROUPS[g]]
    sems_first, flight_first, token = _gather_send("gather_send_first", _place_shards(w, first, chip_arr, (small_whole,)),
                                                   [list(range(len(first)))], (small_whole,))
    sems_rest, flight_rest, all_sent = _gather_send("gather_send_rest", _place_shards(w, rest, chip_arr, (token,)),
                                                    [[rest.index(n) for n in _GROUPS[g]] for g in group_names[1:]], ())
    sems = list(sems_first) + list(sems_rest)
    in_flight = dict(zip(first + rest, list(flight_first) + list(flight_rest)))

    def fetch(group, after):
        gi, members = group_names.index(group), _GROUPS[group]
        after = after if gi else (all_sent,)
        landed = _gather_wait(f"gather_wait_{group}", [in_flight[n] for n in members], sems[2 * gi], sems[2 * gi + 1], after)
        return _whole_weights(dict(zip(members, _gather_pass(f"gather_pass_{group}", landed))))

    swapping, pending, arrived, sent = [], [], {}, []

    def settle(after):
        names, ps, lands, send_sems, recv_sems = pending.pop()
        ps, lands = _scatter_wait(f"scatter_wait_{names[0]}", ps, lands, send_sems, recv_sems, after)
        arrived.update({n: (p, r) for n, p, r in zip(names, ps, lands)})

    def emit(group, grads):
        names = _GROUPS[group]
        halves = [grads[n].reshape(N_CHIPS, 2, grads[n].shape[1] // 2, grads[n].shape[2]) for n in names]
        send_sems, recv_sems, halves, lands, token = _exchange_send(f"exchange_send_{group}", halves)
        swapping.append((group, halves, lands, send_sems, recv_sems))
        sent.append(token)
        return (token,)

    def advance(done):
        done = done if isinstance(done, tuple) else (done,)
        group, halves, lands, send_sems, recv_sems = swapping.pop()
        names = _GROUPS[group]
        halves, lands = _exchange_wait(f"exchange_wait_{group}", halves, lands, send_sems, recv_sems, done)
        partial = [_add_halves(f"add_{n}", g, r, c_arr) for n, g, r in zip(names, halves, lands)]
        if pending:
            settle(done)
        send_sems, recv_sems, ps, lands, token = _scatter_send(f"scatter_send_{group}", partial)
        pending.append((names, ps, lands, send_sems, recv_sems))
        return (token,)

    sq, grad_x, G = _forward_backward(x, positions, loss_target, small, fetch, emit, advance)
    small_names = _REPLICATED + _SMALL_SHARDED
    G["loss"] = (0.5 * sq / D_MODEL).reshape(1)
    summed = _split_small(_all_reduce_small("reduce_small_grads", _small_vector(G, small_names + ("loss",))), G,
                          small_names + ("loss",))
    loss = summed["loss"][0]
    grads = {n: summed[n] for n in _REPLICATED}
    for n in _SMALL_SHARDED:
        a = w[n]
        grads[n] = lax.dynamic_slice_in_dim(summed[n].reshape(a.shape[:-1] + (N_CHIPS, a.shape[-1])), chip, 1,
                                            axis=a.ndim - 1).reshape(a.shape)

    chip_c = jnp.stack([chip, cc]).astype(jnp.int32)
    out = {}

    def finish(group, after):
        names, tokens = _GROUPS[group], []
        sums = [_sum_partials(f"sum_{n}", *arrived[n], chip_c) for n in names]
        for n, f in zip(names, _sibling_share(f"grad_share_{group}", sums, after)):
            weight, layer = (n[:-1], int(n[-1])) if n[-1].isdigit() else (n, 0)
            *out[weight], token = _adamw(f"adamw_{weight}", w[weight], f.reshape(-1, f.shape[-1]), m[weight], v[weight], layer,
                                         out.get(weight, ()))
            tokens.append(token)
        return tuple(tokens)

    last_exchange = tuple(sent[-1:])
    last_scatter = advance(finish(group_names[3], last_exchange) + finish(group_names[2], last_exchange))
    settle(finish(group_names[1], last_scatter))
    finish(group_names[0], ())
    packed = [_small_vector(d, small_names) for d in (w, grads, m, v)]
    res = _adamw("adamw_small", packed[0][None], packed[1], packed[2][None], packed[3][None])
    delta_s, m_s, v_s = (_split_small(r, w, small_names) for r in res[1:4])
    for n in small_names:
        out[n] = (grads[n], delta_s[n], m_s[n], v_s[n])
    for n in transposed:
        out[n] = tuple(jnp.swapaxes(t, 1, 2) for t in out[n])

    return (loss, grad_x, *[out[n][0] for n in _WEIGHTS], *[out[n][1] for n in _WEIGHTS],
            *[out[n][2] for n in _WEIGHTS], *[out[n][3] for n in _WEIGHTS])
```

```python
import functools
import math

import jax
import jax.numpy as jnp
from jax import lax
from jax.experimental import pallas as pl
from jax.experimental.pallas import tpu as pltpu

F32, BF16 = jnp.float32, jnp.bfloat16
BS = pl.BlockSpec

D_MODEL = 1024
EPS = 1e-6
NEG_INF = -1e30
SG_HEADS, SG_DIM, SG_WIDTH, SG_CHUNK = 4, 128, 512, 128
SC_WIDTH, CONV_TAPS = 512, 3
EVEN_IN = 2 * SG_WIDTH + 3 * SC_WIDTH
POOL_WINDOWS = (2, 4, 8, 16)
POOL_DIM, POOL_WIDTH = 64, 256
POOL_HALO = 16
HEADS, Q_LORA, KV_LORA, QK_NOPE, QK_ROPE, V_DIM = 6, 384, 256, 128, 64, 128
QK_DIM = QK_NOPE + QK_ROPE
QK_PAD = 256
ODD_IN = POOL_WIDTH + Q_LORA + KV_LORA + QK_ROPE
ODD_IN_PAD = 1024
ROPE_THETA = 10000.0
ATTN_SCALE = QK_DIM ** -0.5
D_FF, N_CHIPS = 2816, 4
FF_SHARD = D_FF // N_CHIPS
ADAM_LR, ADAM_B1, ADAM_B2, ADAM_EPS, ADAM_WD, ADAM_STEP = 0.001, 0.9, 0.999, 1e-08, 0.01, 10
VMEM_LIMIT_V7X = 48 * 2**20
LANES, SUBLANES = 128, 8
MESH = pl.DeviceIdType.MESH
HBM = pl.BlockSpec(memory_space=pltpu.HBM)
VMEM = pl.BlockSpec(memory_space=pltpu.VMEM)

_DIMS = {"nn": (((1,), (0,)), ((), ())), "nt": (((1,), (1,)), ((), ())), "tn": (((0,), (0,)), ((), ()))}


def _dot(a, b, mode="nn"):
    return lax.dot_general(a.astype(BF16), b.astype(BF16), _DIMS[mode], preferred_element_type=F32)


def _call(body, *, name, out_shape, in_specs, out_specs, grid=(), scratch=(), aliases=None, after=(), collective_id=None):
    params = pltpu.CompilerParams(vmem_limit_bytes=VMEM_LIMIT_V7X,
                                  **({"dimension_semantics": ("arbitrary",) * len(grid)} if grid else {}),
                                  **({"collective_id": collective_id} if collective_id is not None else {}))
    n_in, n_after = len(in_specs), len(after)
    kernel_body = body if not after else (lambda *refs: body(*refs[:n_in], *refs[n_in + n_after:]))
    call = pl.pallas_call(kernel_body, name=name, grid=grid, in_specs=list(in_specs) + [pl.BlockSpec(memory_space=pl.ANY)] * n_after,
                          out_specs=out_specs, out_shape=out_shape, scratch_shapes=list(scratch),
                          input_output_aliases=aliases or {}, compiler_params=params)
    return (lambda *ops: call(*ops, *after)) if after else call


def _sds(shape, dtype):
    return jax.ShapeDtypeStruct(tuple(shape), dtype)


def _token_tile(seq):
    return 512 if seq % 512 == 0 else seq


_TAIL_ROWS = 256


def _matmul(name, mode, pairs, pair_specs, grid, out_shape, out_spec, acc_shape, add=None, add_spec=None, after=(), tail=None):
    n, nk = len(pairs), grid[-1]
    n_add = int(add is not None)
    n_tail = len(tail[0]) if tail else 0
    n_in = 2 * n + n_add + n_tail
    n_out = len(out_shape) if tail else 1

    def body(*refs):
        ab = refs[:2 * n]
        add_ref = refs[2 * n] if n_add else None
        tail_refs, outs = refs[2 * n + n_add:n_in], refs[n_in:n_in + n_out]
        first = pl.program_id(0) == 0

        def finish(result):
            if tail is None:
                r = result(slice(None))
                outs[0][...] = (r if add_ref is None else r + add_ref[...]).astype(outs[0].dtype)
                return
            for lo in range(0, acc_shape[0], _TAIL_ROWS):
                rows = slice(lo, min(lo + _TAIL_ROWS, acc_shape[0]))
                r = result(rows)
                tail[2](rows, r if add_ref is None else r + add_ref[rows, :], first, tail_refs, outs)

        def terms(a_ref, b_ref):
            if len(a_ref.shape) == 2 and len(b_ref.shape) == 2:
                return [(a_ref[...], b_ref[...])]
            cols = a_ref.shape[-1] // N_CHIPS
            return [(a_ref[j] if len(a_ref.shape) == 3 else a_ref[:, cols * j:cols * (j + 1)], b_ref[j]) for j in range(N_CHIPS)]

        if nk == 1:
            r = None
            for p in range(n):
                for a_blk, b_blk in terms(ab[2 * p], ab[2 * p + 1]):
                    d = _dot(a_blk, b_blk, mode)
                    r = d if r is None else r + d
            finish(lambda rows: r[rows])
            return
        acc = refs[-1]
        k = pl.program_id(len(grid) - 1)

        @pl.when(k == 0)
        def _():
            acc[...] = jnp.zeros_like(acc)

        for p in range(n):
            acc[...] += _dot(ab[2 * p][...], ab[2 * p + 1][...], mode)

        @pl.when(k == nk - 1)
        def _():
            finish(lambda rows: acc[rows, :])

    ops = [t for pr in pairs for t in pr] + ([add] if n_add else []) + (list(tail[0]) if tail else [])
    specs = [s for pr in pair_specs for s in pr] + ([add_spec] if n_add else []) + (list(tail[1]) if tail else [])
    return _call(body, name=name, grid=grid, in_specs=specs, out_specs=out_spec, out_shape=out_shape,
                 scratch=[pltpu.VMEM(acc_shape, F32)] if nk > 1 else [], after=after)(*ops)


def _row_spec(tm, d):
    return BS((tm, d), lambda i, j, k: (i, 0))


def _vec_spec(d):
    return BS((1, d), lambda i, j, k: (0, 0))


def _norm_tail(gain, T, tm):
    d = gain.shape[-1]

    def fn(rows, r, first, tail_refs, outs):
        outs[0][rows, :] = r
        outs[1][rows, :] = (r * lax.rsqrt(jnp.mean(r * r, axis=-1, keepdims=True) + EPS) * tail_refs[0][...]).astype(BF16)

    return ([gain.reshape(1, d)], [_vec_spec(d)], fn), [_sds((T, d), F32), _sds((T, d), BF16)], [_row_spec(tm, d), _row_spec(tm, d)]


def _norm_bwd_tail(x, gain, dres, tm):
    T, d = x.shape

    def fn(rows, r, first, tail_refs, outs):
        x_ref, g_ref, dres_ref = tail_refs
        xv = x_ref[rows, :]
        rstd = lax.rsqrt(jnp.mean(xv * xv, axis=-1, keepdims=True) + EPS)
        xhat = xv * rstd
        if rows.start == 0:
            @pl.when(first)
            def _():
                outs[1][...] = jnp.zeros_like(outs[1])

        outs[1][...] += jnp.sum(r * xhat, axis=0, keepdims=True)
        dxhat = r * g_ref[...]
        outs[0][rows, :] = dres_ref[rows, :] + rstd * (dxhat - xhat * jnp.mean(dxhat * xhat, axis=-1, keepdims=True))

    return (([x, gain.reshape(1, d), dres], [_row_spec(tm, d), _vec_spec(d), _row_spec(tm, d)], fn),
            [_sds((T, d), F32), _sds((1, d), F32)], [_row_spec(tm, d), _vec_spec(d)])


def _loss_tail(target, tm):
    T, d = target.shape

    def fn(rows, r, first, tail_refs, outs):
        e = r - tail_refs[0][rows, :]
        if rows.start == 0:
            @pl.when(first)
            def _():
                outs[1][...] = jnp.zeros_like(outs[1])

        outs[1][...] += jnp.sum(e * e)
        outs[0][rows, :] = e * (1.0 / d)

    return (([target], [_row_spec(tm, d)], fn), [_sds((T, d), F32), _sds((SUBLANES, LANES), F32)],
            [_row_spec(tm, d), BS((SUBLANES, LANES), lambda i, j, k: (0, 0))])


def _grad_shards(name, a, b, a_spec, b_spec, pick, out_shape, n_steps):
    def body(a_ref, b_ref, o_ref, acc):
        k = pl.program_id(0)

        @pl.when(k == 0)
        def _():
            acc[...] = jnp.zeros_like(acc)

        for j in range(N_CHIPS):
            aj, bj = pick(a_ref, b_ref, j)
            acc[j] += _dot(aj, bj, "tn")

        @pl.when(k == n_steps - 1)
        def _():
            o_ref[...] = acc[...].astype(BF16)

    return _call(body, name=name, grid=(n_steps,), in_specs=[a_spec, b_spec], scratch=[pltpu.VMEM(tuple(out_shape), F32)],
                 out_specs=BS(out_shape, lambda k: (0, 0, 0)), out_shape=pltpu.HBM(tuple(out_shape), BF16))(a, b)


def _mm(name, mode, a, b, out_dtype, tm=1024, tn=1024, tk=512, add=None, after=(), fused=None, hbm_out=False):
    if mode == "tn":
        (K, M), N = a.shape, b.shape[1]
    else:
        (M, K), N = a.shape, (b.shape[1] if mode == "nn" else b.shape[0])
    tm, tn, tk = min(tm, M), min(tn, N), min(tk, K)
    a_spec = BS((tk, tm), lambda i, j, k: (k, i)) if mode == "tn" else BS((tm, tk), lambda i, j, k: (i, k))
    b_spec = BS((tn, tk), lambda i, j, k: (j, k)) if mode == "nt" else BS((tk, tn), lambda i, j, k: (k, j))
    o_spec = BS((tm, tn), lambda i, j, k: (i, j))
    tail, shapes, specs = fused if fused else (None, pltpu.HBM((M, N), out_dtype) if hbm_out else _sds((M, N), out_dtype), o_spec)
    return _matmul(name, mode, [(a, b)], [(a_spec, b_spec)], (M // tm, N // tn, K // tk), shapes, specs, (tm, tn),
                   add=add, add_spec=o_spec if add is not None else None, after=after, tail=tail)


_PASS_ROWS = 256


def _ffn_up(name, h, wg, wu, tm):
    T = h.shape[0]

    def body(h_ref, wg_ref, wu_ref, g_ref, u_ref, a_ref):
        hv = h_ref[...]
        g = _dot(hv, wg_ref[...], "nt")
        u = _dot(hv, wu_ref[...], "nt")
        g_ref[...] = g.astype(BF16)
        u_ref[...] = u.astype(BF16)
        a_ref[...] = (g * (1.0 / (1.0 + jnp.exp(-g))) * u).astype(BF16)

    w_spec = BS((None, FF_SHARD, D_MODEL), lambda j, i: (j, 0, 0))
    o_spec = BS((None, tm, FF_SHARD), lambda j, i: (j, i, 0))
    sh = _sds((N_CHIPS, T, FF_SHARD), BF16)
    return _call(body, name=name, grid=(N_CHIPS, T // tm), in_specs=[BS((tm, D_MODEL), lambda j, i: (i, 0)), w_spec, w_spec],
                 out_specs=[o_spec, o_spec, o_spec], out_shape=[sh, sh, sh])(h, wg, wu)


def _ffn_act_bwd(name, dxo, wd, g, u, tm, after=()):
    T = dxo.shape[0]

    def body(dx_ref, wd_ref, g_ref, u_ref, dg_ref, du_ref):
        da = _dot(dx_ref[...], wd_ref[...], "nt")
        g = g_ref[...].astype(F32)
        sig = 1.0 / (1.0 + jnp.exp(-g))
        dg_ref[...] = (da * u_ref[...].astype(F32) * (sig * (1.0 + g * (1.0 - sig)))).astype(BF16)
        du_ref[...] = (da * (g * sig)).astype(BF16)

    t_spec = BS((None, tm, FF_SHARD), lambda i, j: (j, i, 0))
    sh = _sds((N_CHIPS, T, FF_SHARD), BF16)
    return _call(body, name=name, grid=(T // tm, N_CHIPS),
                 in_specs=[BS((tm, D_MODEL), lambda i, j: (i, 0)), BS((None, FF_SHARD, D_MODEL), lambda i, j: (j, 0, 0)), t_spec, t_spec],
                 out_specs=[t_spec, t_spec], out_shape=[sh, sh], after=after)(dxo, wd, g, u)


def _big_tile(n):
    return min(1024, n)


def _resident_tile(n):
    return min(512, n)


def _resident(shape):
    return BS(shape, lambda i, j, k: (0,) * len(shape), pipeline_mode=pl.Buffered(1))


def _ffn_fwd(l, x, h, wg, wu, wd, fused):
    T = x.shape[0]
    g, u, a = _ffn_up(f"ffn{l}_up", h, wg, wu, _big_tile(T))
    tm = _resident_tile(T)
    tail, shapes, specs = fused(tm)
    outs = _matmul(f"ffn{l}_down", "nn", [(a, wd)],
                   [(BS((N_CHIPS, tm, FF_SHARD), lambda i, j, k: (0, i, 0)), _resident((N_CHIPS, FF_SHARD, D_MODEL)))],
                   (T // tm, 1, 1), shapes, specs, (tm, D_MODEL), add=x, add_spec=_row_spec(tm, D_MODEL), tail=tail)
    return outs, (h, g, u, a)


def _ffn_bwd(l, x, gain, wg, wu, wd, saved, dxo, emit, after=()):
    h, g, u, a = saved
    T = x.shape[0]
    tm = _big_tile(T)
    dg, du = _ffn_act_bwd(f"ffn{l}_act_bwd", dxo, wd, g, u, tm, after=after)
    tk = _big_tile(T)
    shards_spec = BS((N_CHIPS, tk, FF_SHARD), lambda k: (0, k, 0))
    rows_spec = BS((tk, D_MODEL), lambda k: (k, 0))

    def dw(nm, act, rows):
        return _grad_shards(nm, act, rows, shards_spec, rows_spec, lambda a_ref, b_ref, j: (a_ref[j], b_ref[...]),
                            (N_CHIPS, FF_SHARD, D_MODEL), T // tk)

    behind = emit(f"ffn{l}", {f"ffn_w_gate{l}": dw(f"ffn{l}_dwg", dg, h), f"ffn_w_up{l}": dw(f"ffn{l}_dwu", du, h),
                              f"ffn_w_down{l}": dw(f"ffn{l}_dwd", a, dxo)})
    tm = _resident_tile(T)
    act_spec = BS((N_CHIPS, tm, FF_SHARD), lambda i, j, k: (0, i, 0))
    w_spec = _resident((N_CHIPS, FF_SHARD, D_MODEL))
    tail, shapes, specs = _norm_bwd_tail(x, gain, dxo, tm)
    return _matmul(f"ffn{l}_dh", "nn", [(dg, wg), (du, wu)], [(act_spec, w_spec), (act_spec, w_spec)],
                   (T // tm, 1, 1), shapes, specs, (tm, D_MODEL), after=behind, tail=tail)


_INV_SQRT2 = 1.0 / math.sqrt(2.0)
_INV_SQRT_2PI = 1.0 / math.sqrt(2.0 * math.pi)


def _gelu(x):
    return 0.5 * x * (1.0 + lax.erf(x * _INV_SQRT2))


def _gelu_and_grad(x):
    cdf = 0.5 * (1.0 + lax.erf(x * _INV_SQRT2))
    return x * cdf, cdf + x * jnp.exp(-0.5 * x * x) * _INV_SQRT_2PI


def _shift_down(x, k):
    return pltpu.roll(x, k, 0)


def _shift_up(x, k):
    return pltpu.roll(x, x.shape[0] - k, 0)


def _layer_norm_head(xh):
    xc = xh - jnp.mean(xh, axis=-1, keepdims=True)
    rstd = lax.rsqrt(jnp.mean(xc * xc, axis=-1, keepdims=True) + EPS)
    return xc * rstd, rstd


def _even_in(x, gain, w, tm):
    T, d = x.shape
    shard = w.shape[-1]

    def body(x_ref, g_ref, w_ref, o_ref, h_ref):
        xv = x_ref[...]
        hv = (xv * lax.rsqrt(jnp.mean(xv * xv, axis=-1, keepdims=True) + EPS) * g_ref[...]).astype(BF16)
        h_ref[...] = hv
        for j in range(N_CHIPS):
            o_ref[:, shard * j:shard * (j + 1)] = _dot(hv, w_ref[j])

    row = BS((tm, d), lambda i: (i, 0))
    return _call(body, name="even_in", grid=(T // tm,),
                 in_specs=[row, BS((1, d), lambda i: (0, 0)), BS(w.shape, lambda i: (0, 0, 0), pipeline_mode=pl.Buffered(1))],
                 out_specs=[BS((tm, N_CHIPS * shard), lambda i: (i, 0)), row],
                 out_shape=[_sds((T, N_CHIPS * shard), F32), _sds((T, d), BF16)])(x, gain.reshape(1, d), w)


def _even_halo_specs(tm, n_tiles, col_blocks, after):
    rows = tm // SUBLANES
    last = n_tiles * rows - 1
    if after:
        return [BS((SUBLANES, 512), functools.partial(lambda cb, i: (jnp.minimum((i + 1) * rows, last), cb), cb)) for cb in col_blocks]
    return [BS((SUBLANES, 512), functools.partial(lambda cb, i: (jnp.maximum(i * rows - 1, 0), cb), cb)) for cb in col_blocks]


def _even_mixer_fwd(proj, ln_g, w_tril, b_lanes, conv_w, seq, tm):
    T = proj.shape[0]
    tiles_per_seq = seq // tm

    def body(p_ref, hc_ref, hh_ref, lng_ref, w_ref, bb_ref, cw_ref, o_ref):
        first = pl.program_id(0) % tiles_per_seq == 0
        for h in range(SG_HEADS):
            cols = slice(SG_DIM * h, SG_DIM * (h + 1))
            vhat, _ = _layer_norm_head(_gelu(p_ref[:, SG_WIDTH + SG_DIM * h:SG_WIDTH + SG_DIM * (h + 1)]))
            vln = (vhat * lng_ref[:, cols]).astype(BF16)
            for k in range(tm // SG_CHUNK):
                rows = slice(SG_CHUNK * k, SG_CHUNK * (k + 1))
                mixed = _dot(w_ref[h], vln[rows]) + bb_ref[h]
                o_ref[rows, cols] = (_gelu(p_ref[rows, cols]) * mixed).astype(BF16)
        z = p_ref[:, 1536:2048] * p_ref[:, 2048:2560]
        zz = jnp.concatenate([jnp.where(first, 0.0, hc_ref[...] * hh_ref[...]), z], axis=0)
        y = cw_ref[0:1, :] * _shift_down(zz, 2)[SUBLANES:] + cw_ref[1:2, :] * _shift_down(zz, 1)[SUBLANES:] + cw_ref[2:3, :] * z
        o_ref[:, SG_WIDTH:] = (p_ref[:, 1024:1536] * y).astype(BF16)

    full = lambda shape: BS(shape, lambda i: (0,) * len(shape))
    return _call(body, name="even_mixer_fwd", grid=(T // tm,),
                 in_specs=[BS((tm, EVEN_IN), lambda i: (i, 0))] + _even_halo_specs(tm, T // tm, (3, 4), after=False)
                 + [full((1, SG_WIDTH)), full((SG_HEADS, SG_CHUNK, SG_CHUNK)), full((SG_HEADS, SG_CHUNK, SG_DIM)), full((SUBLANES, SC_WIDTH))],
                 out_specs=BS((tm, D_MODEL), lambda i: (i, 0)), out_shape=_sds((T, D_MODEL), BF16))(
        proj, proj, proj, ln_g, w_tril, b_lanes, conv_w)


def _even_mixer_bwd(proj, dmix, ln_g, w_tril, b_lanes, conv_w, seq, tm):
    T = proj.shape[0]
    n_tiles, tiles_per_seq = T // tm, seq // tm

    def body(p_ref, dm_ref, hc_ref, hh_ref, nd_ref, nb_ref, lng_ref, w_ref, bb_ref, cw_ref,
             dp_ref, dw_ref, db_ref, dlng_ref, dcw_ref):
        i = pl.program_id(0)
        first = i % tiles_per_seq == 0
        last = i % tiles_per_seq == tiles_per_seq - 1

        @pl.when(i == 0)
        def _():
            dw_ref[...] = jnp.zeros_like(dw_ref)
            db_ref[...] = jnp.zeros_like(db_ref)
            dlng_ref[...] = jnp.zeros_like(dlng_ref)
            dcw_ref[...] = jnp.zeros_like(dcw_ref)

        for h in range(SG_HEADS):
            cols = slice(SG_DIM * h, SG_DIM * (h + 1))
            vcols = slice(SG_WIDTH + SG_DIM * h, SG_WIDTH + SG_DIM * (h + 1))
            lng = lng_ref[:, cols]
            for k in range(tm // SG_CHUNK):
                rows = slice(SG_CHUNK * k, SG_CHUNK * (k + 1))
                gelu_v, dgelu_v = _gelu_and_grad(p_ref[rows, vcols])
                vhat, rstd = _layer_norm_head(gelu_v)
                vln = (vhat * lng).astype(BF16)
                mixed = _dot(w_ref[h], vln) + bb_ref[h]
                gelu_u, dgelu_u = _gelu_and_grad(p_ref[rows, cols])
                da = dm_ref[rows, cols]
                dp_ref[rows, cols] = (da * mixed * dgelu_u).astype(BF16)
                dmixed = da * gelu_u
                db_ref[h] += dmixed
                dw_ref[h] += _dot(dmixed, vln, "nt")
                dvln = _dot(w_ref[h], dmixed, "tn")
                dlng_ref[:, cols] += jnp.sum(dvln * vhat, axis=0, keepdims=True)
                dvhat = dvln * lng
                dgv = rstd * (dvhat - jnp.mean(dvhat, axis=-1, keepdims=True)
                              - vhat * jnp.mean(dvhat * vhat, axis=-1, keepdims=True))
                dp_ref[rows, vcols] = (dgv * dgelu_v).astype(BF16)

        b = p_ref[:, 1024:1536]
        c = p_ref[:, 1536:2048]
        hv = p_ref[:, 2048:2560]
        z = c * hv
        zz = jnp.concatenate([jnp.where(first, 0.0, hc_ref[...] * hh_ref[...]), z], axis=0)
        z1 = _shift_down(zz, 1)[SUBLANES:]
        z2 = _shift_down(zz, 2)[SUBLANES:]
        w0, w1, w2 = cw_ref[0:1, :], cw_ref[1:2, :], cw_ref[2:3, :]
        dbo = dm_ref[:, SG_WIDTH:]
        dy = dbo * b
        dd = jnp.concatenate([dy, jnp.where(last, 0.0, nd_ref[...] * nb_ref[...])], axis=0)
        dz = w2 * dy + w1 * _shift_up(dd, 1)[:tm] + w0 * _shift_up(dd, 2)[:tm]
        dp_ref[:, 1024:1536] = (dbo * (w0 * z2 + w1 * z1 + w2 * z)).astype(BF16)
        dp_ref[:, 1536:2048] = (dz * hv).astype(BF16)
        dp_ref[:, 2048:2560] = (dz * c).astype(BF16)
        dcw_ref[0:1, :] += jnp.sum(dy * z2, axis=0, keepdims=True)
        dcw_ref[1:2, :] += jnp.sum(dy * z1, axis=0, keepdims=True)
        dcw_ref[2:3, :] += jnp.sum(dy * z, axis=0, keepdims=True)

        @pl.when(i == n_tiles - 1)
        def _():
            t_idx = lax.broadcasted_iota(jnp.int32, (SG_CHUNK, SG_CHUNK), 0)
            s_idx = lax.broadcasted_iota(jnp.int32, (SG_CHUNK, SG_CHUNK), 1)
            for h in range(SG_HEADS):
                dw_ref[h] = jnp.where(t_idx >= s_idx, dw_ref[h], 0.0)

    full = lambda shape: BS(shape, lambda i: (0,) * len(shape))
    sq = (SG_HEADS, SG_CHUNK, SG_CHUNK)
    return _call(body, name="even_mixer_bwd", grid=(n_tiles,),
                 in_specs=[BS((tm, EVEN_IN), lambda i: (i, 0)), BS((tm, D_MODEL), lambda i: (i, 0))]
                 + _even_halo_specs(tm, n_tiles, (3, 4), after=False)
                 + _even_halo_specs(tm, n_tiles, (1,), after=True) + _even_halo_specs(tm, n_tiles, (2,), after=True)
                 + [full((1, SG_WIDTH)), full(sq), full(sq), full((SUBLANES, SC_WIDTH))],
                 out_specs=[BS((tm, EVEN_IN), lambda i: (i, 0)), full(sq), full(sq), full((1, SG_WIDTH)), full((SUBLANES, SC_WIDTH))],
                 out_shape=[_sds((T, EVEN_IN), BF16), _sds(sq, F32), _sds(sq, F32), _sds((1, SG_WIDTH), F32), _sds((SUBLANES, SC_WIDTH), F32)])(
        proj, dmix, proj, proj, dmix, proj, ln_g, w_tril, b_lanes, conv_w)


def _pool_select(vals):
    lane = lax.broadcasted_iota(jnp.int32, vals[0].shape, 1)
    out = vals[-1]
    for g in range(len(vals) - 2, -1, -1):
        out = jnp.where(lane < POOL_DIM * (g + 1), vals[g], out)
    return out


def _pool_counts(pos1):
    lane = lax.broadcasted_iota(jnp.int32, (pos1.shape[0], POOL_WIDTH), 1)
    win = _pool_select([jnp.full(lane.shape, float(w), F32) for w in POOL_WINDOWS])
    return jnp.minimum(pos1, win)


def _pool_means(zz, counts):
    s2 = zz + _shift_down(zz, 1)
    s4 = s2 + _shift_down(s2, 2)
    s8 = s4 + _shift_down(s4, 4)
    s16 = s8 + _shift_down(s8, 8)
    return _pool_select([s2, s4, s8, s16])[POOL_HALO:] / counts


def _pool_halo_spec(tm, n_tiles, after):
    rows = tm // POOL_HALO
    if after:
        return BS((POOL_HALO, POOL_WIDTH), lambda i: (jnp.minimum((i + 1) * rows, n_tiles * rows - 1), 0))
    return BS((POOL_HALO, POOL_WIDTH), lambda i: (jnp.maximum(i * rows - 1, 0), 0))


def _pool_fwd(proj, w_diag, scale, seq, tm):
    T = proj.shape[0]
    tiles_per_seq = seq // tm

    def body(z_ref, zh_ref, w_ref, s_ref, o_ref):
        t = pl.program_id(0) % tiles_per_seq
        z = z_ref[...]
        zz = jnp.concatenate([jnp.where(t == 0, 0.0, zh_ref[...]), z], axis=0)
        pos1 = (lax.broadcasted_iota(jnp.int32, (tm, 1), 0) + (t * tm + 1)).astype(F32)
        pooled = _pool_means(zz, _pool_counts(pos1)) - z
        o_ref[...] = (_dot(pooled, w_ref[...]) * s_ref[...]).astype(BF16)

    full = lambda shape: BS(shape, lambda i: (0,) * len(shape))
    return _call(body, name="pool_fwd", grid=(T // tm,),
                 in_specs=[BS((tm, POOL_WIDTH), lambda i: (i, 0)), _pool_halo_spec(tm, T // tm, False),
                           full((POOL_WIDTH, POOL_WIDTH)), full((1, POOL_WIDTH))],
                 out_specs=BS((tm, POOL_WIDTH), lambda i: (i, 0)), out_shape=_sds((T, D_MODEL), BF16))(proj, proj, w_diag, scale)


def _pool_bwd(proj, dmix, w_diag, scale, seq, tm):
    T = proj.shape[0]
    n_tiles, tiles_per_seq = T // tm, seq // tm

    def body(z_ref, zh_ref, do_ref, don_ref, w_ref, s_ref, dz_ref, dw_ref, ds_ref):
        i = pl.program_id(0)
        t = i % tiles_per_seq

        @pl.when(i == 0)
        def _():
            dw_ref[...] = jnp.zeros_like(dw_ref)
            ds_ref[...] = jnp.zeros_like(ds_ref)

        z = z_ref[...]
        zz = jnp.concatenate([jnp.where(t == 0, 0.0, zh_ref[...]), z], axis=0)
        pos1 = (lax.broadcasted_iota(jnp.int32, (tm, 1), 0) + (t * tm + 1)).astype(F32)
        counts = _pool_counts(pos1)
        pooled = _pool_means(zz, counts) - z
        dout = do_ref[...].astype(F32)
        ds_ref[...] += jnp.sum(dout * _dot(pooled, w_ref[...]), axis=0, keepdims=True)
        dlin = dout * s_ref[...]
        dw_ref[...] += _dot(pooled, dlin, "tn")
        dpooled = _dot(dlin, w_ref[...], "nt")
        dpooled_n = _dot(don_ref[...].astype(F32) * s_ref[...], w_ref[...], "nt")
        pos1_n = (lax.broadcasted_iota(jnp.int32, (POOL_HALO, 1), 0) + ((t + 1) * tm + 1)).astype(F32)
        dmean_n = jnp.where(t == tiles_per_seq - 1, 0.0, dpooled_n / _pool_counts(pos1_n))
        dd = jnp.concatenate([dpooled / counts, dmean_n], axis=0)
        r2 = dd + _shift_up(dd, 1)
        r4 = r2 + _shift_up(r2, 2)
        r8 = r4 + _shift_up(r4, 4)
        r16 = r8 + _shift_up(r8, 8)
        dz_ref[...] = (_pool_select([r2, r4, r8, r16])[:tm] - dpooled).astype(BF16)

    full = lambda shape: BS(shape, lambda i: (0,) * len(shape))
    return _call(body, name="pool_bwd", grid=(n_tiles,),
                 in_specs=[BS((tm, POOL_WIDTH), lambda i: (i, 0)), _pool_halo_spec(tm, n_tiles, False),
                           BS((tm, POOL_WIDTH), lambda i: (i, 0)), _pool_halo_spec(tm, n_tiles, True),
                           full((POOL_WIDTH, POOL_WIDTH)), full((1, POOL_WIDTH))],
                 out_specs=[BS((tm, POOL_WIDTH), lambda i: (i, 0)), full((POOL_WIDTH, POOL_WIDTH)), full((1, POOL_WIDTH))],
                 out_shape=[_sds((T, POOL_WIDTH), BF16), _sds((POOL_WIDTH, POOL_WIDTH), F32), _sds((1, POOL_WIDTH), F32)])(
        proj, proj, dmix, dmix, w_diag, scale)


def _rope_partner(r):
    lane = lax.broadcasted_iota(jnp.int32, r.shape, 1)
    return jnp.where(lane < QK_ROPE // 2, pltpu.roll(r, LANES - QK_ROPE // 2, 1), pltpu.roll(r, QK_ROPE // 2, 1))


def _rope(x, cos, sin_signed):
    r = x[:, QK_NOPE:]
    return jnp.concatenate([x[:, :QK_NOPE], r * cos + _rope_partner(r) * sin_signed], axis=1)


def _rope_transposed(dx, cos, sin_signed):
    dr = dx[:, QK_NOPE:]
    return jnp.concatenate([dx[:, :QK_NOPE], dr * cos + _rope_partner(dr * sin_signed)], axis=1)


def _head_norm(x):
    r = lax.rsqrt(jnp.sum(x * x, axis=-1, keepdims=True) * (1.0 / QK_DIM) + EPS)
    return x * r, r


def _head_norm_bwd(dy, xhat, r, gain):
    dxhat = dy * gain
    return r * (dxhat - xhat * (jnp.sum(dxhat * xhat, axis=-1, keepdims=True) * (1.0 / QK_DIM)))


def _latents(p_ref, qag_ref, kvag_ref):
    ql = p_ref[:, POOL_WIDTH:POOL_WIDTH + Q_LORA]
    kvl = p_ref[:, POOL_WIDTH + Q_LORA:POOL_WIDTH + Q_LORA + KV_LORA]
    rq = lax.rsqrt(jnp.mean(ql * ql, axis=-1, keepdims=True) + EPS)
    rkv = lax.rsqrt(jnp.mean(kvl * kvl, axis=-1, keepdims=True) + EPS)
    return ql * rq, rq, kvl * rkv, rkv


def _mla_specs(tm):
    full = lambda shape: BS(shape, lambda i, h: (0,) * len(shape))
    return [BS((tm, ODD_IN_PAD), lambda i, h: (i, 0)), BS((tm, LANES), lambda i, h: (i, 0)), BS((tm, LANES), lambda i, h: (i, 0)),
            full((1, Q_LORA)), full((1, KV_LORA)), BS((None, Q_LORA, QK_PAD), lambda i, h: (h, 0, 0)),
            BS((None, KV_LORA, QK_PAD), lambda i, h: (h, 0, 0)), full((1, QK_PAD)), full((1, QK_PAD))]


def _mla_qkv_fwd(proj, cos, sin_signed, qa_g, kva_g, q_b, kv_b, q_g, k_g, tm):
    T = proj.shape[0]

    def body(p_ref, cos_ref, sin_ref, qag_ref, kvag_ref, qb_ref, kvb_ref, qg_ref, kg_ref, q_ref, k_ref, v_ref, qn_s, kvn_s):
        @pl.when(pl.program_id(1) == 0)
        def _():
            qhat, _, kvhat, _ = _latents(p_ref, qag_ref, kvag_ref)
            qn_s[...] = (qhat * qag_ref[...]).astype(BF16)
            kvn_s[...] = (kvhat * kvag_ref[...]).astype(BF16)

        cos, sin = cos_ref[...], sin_ref[...]
        qhat, _ = _head_norm(_dot(qn_s[...], qb_ref[...]))
        q_ref[...] = _rope(qhat * qg_ref[...], cos, sin).astype(BF16)
        kv = _dot(kvn_s[...], kvb_ref[...])
        khat, _ = _head_norm(jnp.concatenate([kv[:, :QK_NOPE], p_ref[:, ODD_IN_PAD - LANES:]], axis=1))
        k_ref[...] = _rope(khat * kg_ref[...], cos, sin).astype(BF16)
        v_ref[...] = kv[:, QK_NOPE:].astype(BF16)

    qk_spec = BS((None, tm, QK_PAD), lambda i, h: (h, i, 0))
    return _call(body, name="mla_qkv_fwd", grid=(T // tm, HEADS), in_specs=_mla_specs(tm),
                 out_specs=[qk_spec, qk_spec, BS((None, tm, V_DIM), lambda i, h: (h, i, 0))],
                 out_shape=[_sds((HEADS, T, QK_PAD), BF16), _sds((HEADS, T, QK_PAD), BF16), _sds((HEADS, T, V_DIM), BF16)],
                 scratch=[pltpu.VMEM((tm, Q_LORA), BF16), pltpu.VMEM((tm, KV_LORA), BF16)])(
        proj, cos, sin_signed, qa_g, kva_g, q_b, kv_b, q_g, k_g)


def _mla_qkv_bwd(proj, cos, sin_signed, qa_g, kva_g, q_b, kv_b, q_g, k_g, dq, dk, dv, dz_pool, tm):
    T = proj.shape[0]
    n_tiles = T // tm
    chain_rows = min(_PASS_ROWS, tm)

    def body(p_ref, cos_ref, sin_ref, qag_ref, kvag_ref, qb_ref, kvb_ref, qg_ref, kg_ref, dq_ref, dk_ref, dv_ref, dzp_ref,
             dp_ref, dqb_ref, dkvb_ref, dqg_ref, dkg_ref, dqag_ref, dkvag_ref, qn_s, kvn_s, dqn_s, dkvn_s, dkr_s,
             qh_s, kv_s, dqh_s, dkv_s):
        i, h = pl.program_id(0), pl.program_id(1)

        @pl.when((i == 0) & (h == 0))
        def _():
            for ref in (dqb_ref, dkvb_ref, dqg_ref, dkg_ref, dqag_ref, dkvag_ref):
                ref[...] = jnp.zeros_like(ref)

        @pl.when(h == 0)
        def _():
            qhat, _, kvhat, _ = _latents(p_ref, qag_ref, kvag_ref)
            qn_s[...] = (qhat * qag_ref[...]).astype(BF16)
            kvn_s[...] = (kvhat * kvag_ref[...]).astype(BF16)
            dqn_s[...] = jnp.zeros_like(dqn_s)
            dkvn_s[...] = jnp.zeros_like(dkvn_s)
            dkr_s[...] = jnp.zeros_like(dkr_s)

        qh_s[...] = _dot(qn_s[...], qb_ref[...])
        kv_s[...] = _dot(kvn_s[...], kvb_ref[...])
        qg, kg = qg_ref[...], kg_ref[...]

        def chunk(c, gains):
            dqg, dkg = gains
            rows = slice(c * chain_rows, (c + 1) * chain_rows)
            cos, sin = cos_ref[rows, :], sin_ref[rows, :]
            qhat, rq = _head_norm(qh_s[rows, :])
            dqn_head = _rope_transposed(dq_ref[rows, :], cos, sin)
            dqh_s[rows, :] = _head_norm_bwd(dqn_head, qhat, rq, qg).astype(BF16)
            kv = kv_s[rows, :]
            khat, rk = _head_norm(jnp.concatenate([kv[:, :QK_NOPE], p_ref[rows, ODD_IN_PAD - LANES:]], axis=1))
            dkn_head = _rope_transposed(dk_ref[rows, :], cos, sin)
            dkf = _head_norm_bwd(dkn_head, khat, rk, kg)
            dkr_s[rows, :] += dkf[:, QK_NOPE:]
            dkv_s[rows, :] = jnp.concatenate([dkf[:, :QK_NOPE], dv_ref[rows, :]], axis=1).astype(BF16)
            return dqg + dqn_head * qhat, dkg + dkn_head * khat

        dqg = dkg = jnp.zeros((chain_rows, QK_PAD), F32)
        for c in range(tm // chain_rows):
            dqg, dkg = chunk(c, (dqg, dkg))
        dqg_ref[...] += jnp.sum(dqg, axis=0, keepdims=True)
        dkg_ref[...] += jnp.sum(dkg, axis=0, keepdims=True)
        dqb_ref[h] += _dot(qn_s[...], dqh_s[...], "tn")
        dqn_s[...] += _dot(dqh_s[...], qb_ref[...], "nt")
        dkvb_ref[h] += _dot(kvn_s[...], dkv_s[...], "tn")
        dkvn_s[...] += _dot(dkv_s[...], kvb_ref[...], "nt")

        @pl.when(h == HEADS - 1)
        def _():
            qhat_l, rql, kvhat_l, rkvl = _latents(p_ref, qag_ref, kvag_ref)
            dqn, dkvn = dqn_s[...], dkvn_s[...]
            dqag_ref[...] += jnp.sum(dqn * qhat_l, axis=0, keepdims=True)
            dkvag_ref[...] += jnp.sum(dkvn * kvhat_l, axis=0, keepdims=True)
            dqx, dkvx = dqn * qag_ref[...], dkvn * kvag_ref[...]
            dp_ref[:, :POOL_WIDTH] = dzp_ref[...]
            dp_ref[:, POOL_WIDTH:POOL_WIDTH + Q_LORA] = (
                rql * (dqx - qhat_l * jnp.mean(dqx * qhat_l, axis=-1, keepdims=True))).astype(BF16)
            dp_ref[:, POOL_WIDTH + Q_LORA:ODD_IN_PAD - LANES] = (
                rkvl * (dkvx - kvhat_l * jnp.mean(dkvx * kvhat_l, axis=-1, keepdims=True))).astype(BF16)
            dp_ref[:, ODD_IN_PAD - LANES:] = dkr_s[:, :QK_ROPE].astype(BF16)

    full = lambda shape: BS(shape, lambda i, h: (0,) * len(shape))
    qk_spec = BS((None, tm, QK_PAD), lambda i, h: (h, i, 0))
    return _call(body, name="mla_qkv_bwd", grid=(n_tiles, HEADS),
                 in_specs=_mla_specs(tm) + [qk_spec, qk_spec, BS((None, tm, V_DIM), lambda i, h: (h, i, 0)),
                                            BS((tm, POOL_WIDTH), lambda i, h: (i, 0))],
                 out_specs=[BS((tm, ODD_IN), lambda i, h: (i, 0)), full((HEADS, Q_LORA, QK_PAD)), full((HEADS, KV_LORA, QK_PAD)),
                            full((1, QK_PAD)), full((1, QK_PAD)), full((1, Q_LORA)), full((1, KV_LORA))],
                 out_shape=[_sds((T, ODD_IN), BF16),_sds((HEADS, Q_LORA, QK_PAD), F32), _sds((HEADS, KV_LORA, QK_PAD), F32),
                            _sds((1, QK_PAD), F32), _sds((1, QK_PAD), F32), _sds((1, Q_LORA), F32), _sds((1, KV_LORA), F32)],
                 scratch=[pltpu.VMEM((tm, Q_LORA), BF16), pltpu.VMEM((tm, KV_LORA), BF16), pltpu.VMEM((tm, Q_LORA), F32),
                          pltpu.VMEM((tm, KV_LORA), F32), pltpu.VMEM((tm, LANES), F32), pltpu.VMEM((tm, QK_PAD), F32),
                          pltpu.VMEM((tm, QK_PAD), F32), pltpu.VMEM((tm, QK_PAD), BF16), pltpu.VMEM((tm, QK_PAD), BF16)])(
        proj, cos, sin_signed, qa_g, kva_g, q_b, kv_b, q_g, k_g, dq, dk, dv, dz_pool)


_SCALE_LOG2E = ATTN_SCALE * math.log2(math.e)


def _attn_tile(seq):
    return 512 if seq % 512 == 0 else seq


def _causal_mask(s):
    row = lax.broadcasted_iota(jnp.int32, s.shape, 0)
    col = lax.broadcasted_iota(jnp.int32, s.shape, 1)
    return jnp.where(row >= col, s, NEG_INF)


def _tile(i, t):
    return slice(i * t, (i + 1) * t)


def _flash_fwd(q, k, v, mix, batch, seq):
    t = _attn_tile(seq)
    nq = seq // t

    def body(q_ref, k_ref, v_ref, _, o_ref, lse_ref):
        for qi in range(nq):
            rows, before = _tile(qi, t), slice(0, qi * t)
            qv = q_ref[rows, :]
            s_diag = _causal_mask(_dot(qv, k_ref[rows, :], "nt"))
            m = jnp.max(s_diag, axis=-1, keepdims=True)
            if qi:
                s_before = _dot(qv, k_ref[before, :], "nt")
                m = jnp.maximum(m, jnp.max(s_before, axis=-1, keepdims=True))
            p = jnp.exp2((s_diag - m) * _SCALE_LOG2E)
            l = jnp.sum(p, axis=-1, keepdims=True)
            acc = _dot(p, v_ref[rows, :])
            if qi:
                p = jnp.exp2((s_before - m) * _SCALE_LOG2E)
                l = l + jnp.sum(p, axis=-1, keepdims=True)
                acc = acc + _dot(p, v_ref[before, :])
            o_ref[rows, :] = (acc / l).astype(BF16)
            lse_ref[rows, :] = jnp.broadcast_to(m * ATTN_SCALE + jnp.log(l), (t, LANES))

    T = batch * seq
    whole = lambda w: BS((None, seq, w), lambda b, h: (h, b, 0))
    return _call(body, name="flash_fwd", grid=(batch, HEADS),
                 in_specs=[whole(QK_PAD), whole(QK_PAD), whole(V_DIM), pl.BlockSpec(memory_space=pl.ANY)],
                 out_specs=[BS((seq, V_DIM), lambda b, h: (b, POOL_WIDTH // V_DIM + h)), whole(LANES)],
                 out_shape=[_sds((T, D_MODEL), BF16), _sds((HEADS, T, LANES), F32)],
                 aliases={3: 0})(q, k, v, mix)


def _flash_bwd(q, k, v, dmix, mix, lse, batch, seq):
    t = _attn_tile(seq)
    nq = seq // t

    def body(q_ref, k_ref, v_ref, do_ref, o_ref, lse_ref, dq_ref, dk_ref, dv_ref):
        for qi in range(nq):
            rows, before = _tile(qi, t), slice(0, qi * t)
            qv, do = q_ref[rows, :], do_ref[rows, :]
            lse2 = lse_ref[rows, 0:1] * math.log2(math.e)
            delta = jnp.sum(do.astype(F32) * o_ref[rows, :].astype(F32), axis=-1, keepdims=True)

            def block(keys, masked):
                kk = k_ref[keys, :]
                s = _dot(qv, kk, "nt")
                p = jnp.exp2((_causal_mask(s) if masked else s) * _SCALE_LOG2E - lse2)
                ds = p * (_dot(do, v_ref[keys, :], "nt") - delta)
                return _dot(p, do, "tn"), _dot(ds, qv, "tn") * ATTN_SCALE, _dot(ds, kk) * ATTN_SCALE

            dv_ref[rows, :], dk_ref[rows, :], dq = block(rows, True)
            if qi:
                dv, dk, dq_before = block(before, False)
                dv_ref[before, :] += dv
                dk_ref[before, :] += dk
                dq = dq + dq_before
            dq_ref[rows, :] = dq

    T = batch * seq
    whole = lambda w: BS((None, seq, w), lambda b, h: (h, b, 0))
    head_cols = BS((seq, V_DIM), lambda b, h: (b, POOL_WIDTH // V_DIM + h))
    return _call(body, name="flash_bwd", grid=(batch, HEADS),
                 in_specs=[whole(QK_PAD), whole(QK_PAD), whole(V_DIM), head_cols, head_cols, whole(LANES)],
                 out_specs=[whole(QK_PAD), whole(QK_PAD), whole(V_DIM)],
                 out_shape=[_sds((HEADS, T, QK_PAD), F32), _sds((HEADS, T, QK_PAD), F32), _sds((HEADS, T, V_DIM), F32)])(
        q, k, v, dmix, mix, lse)


def _adamw_math(w, g, m, v):
    m = ADAM_B1 * m + (1.0 - ADAM_B1) * g
    v = ADAM_B2 * v + (1.0 - ADAM_B2) * (g * g)
    m_hat = m / (1.0 - ADAM_B1 ** ADAM_STEP)
    v_hat = v / (1.0 - ADAM_B2 ** ADAM_STEP)
    return -ADAM_LR * (m_hat / (jnp.sqrt(v_hat) + ADAM_EPS) + ADAM_WD * w), m, v


def _adamw(name, w, g, m, v, l=0, prev=()):
    L, R, C = w.shape
    tr = 256 if R % 256 == 0 else R

    def body(w_ref, g_ref, m_ref, v_ref, *rest):
        go_ref, d_ref, mo_ref, vo_ref, token = rest[-5:]
        gv = g_ref[...]
        d_ref[...], mo_ref[...], vo_ref[...] = _adamw_math(w_ref[...], gv, m_ref[...], v_ref[...])
        go_ref[...] = gv
        token[...] = jnp.zeros_like(token)

    layer = BS((None, tr, C), lambda i: (l, i, 0))
    return _call(body, name=f"{name}_{l}", grid=(R // tr,),
                 in_specs=[layer, BS((tr, C), lambda i: (i, 0)), layer, layer] + [pl.BlockSpec(memory_space=pl.ANY)] * len(prev),
                 out_specs=[layer] * 4 + [BS((SUBLANES, LANES), lambda i: (0, 0))],
                 out_shape=[_sds((L, R, C), F32)] * 4 + [_sds((SUBLANES, LANES), F32)],
                 aliases={4 + n: n for n in range(len(prev))})(w, g, m, v, *prev)


def _place():
    x, y, c = lax.axis_index("x"), lax.axis_index("y"), lax.axis_index("c")
    other_chips = [(1 - x, y), (x, 1 - y), (1 - x, 1 - y)]
    return x, y, c, other_chips


_SIBLING_PAIR_ID = 0


def _sibling_handshake(sibling):
    barrier = pltpu.get_barrier_semaphore()
    pl.semaphore_signal(barrier, inc=1, device_id=sibling, device_id_type=MESH)
    pl.semaphore_wait(barrier, 1)


def _remote(src, dst, send_sem, recv_sem, dev):
    return pltpu.make_async_remote_copy(src_ref=src, dst_ref=dst, send_sem=send_sem, recv_sem=recv_sem,
                                        device_id=dev, device_id_type=MESH)


def _prefetch_call(body, *, name, grid, in_specs, out_specs, out_shape):
    grid_spec = pltpu.PrefetchScalarGridSpec(num_scalar_prefetch=1, grid=grid, in_specs=in_specs, out_specs=out_specs)
    params = pltpu.CompilerParams(vmem_limit_bytes=VMEM_LIMIT_V7X, dimension_semantics=("arbitrary",) * len(grid))
    return pl.pallas_call(body, name=name, grid_spec=grid_spec, out_shape=out_shape, compiler_params=params)


def _row_tile(rows):
    return 256 if rows % 256 == 0 else rows


def _cast_place(name, w, layer, chip, after=()):
    _, _, rows, C = w.shape
    tr = _row_tile(rows)

    def body(chip_ref, w_ref, *rest):
        rest[-1][...] = w_ref[...].astype(BF16)

    return _prefetch_call(body, name=name, grid=(2, rows // tr),
                          in_specs=[BS((None, None, tr, C), lambda h, i, chip_ref: (layer, h, i, 0))]
                          + [pl.BlockSpec(memory_space=pl.ANY)] * len(after),
                          out_specs=BS((None, None, tr, C), lambda h, i, chip_ref: (chip_ref[0], h, i, 0)),
                          out_shape=pltpu.HBM((N_CHIPS, 2, rows, C), BF16))(chip, w, *after)


SEM = pl.BlockSpec(memory_space=pltpu.SEMAPHORE)


def _split_copy_call(body, *, name, in_specs, out_specs, out_shape, aliases, collective_id=None):
    params = pltpu.CompilerParams(has_side_effects=pltpu.SideEffectType.DATAFLOW_SIDE_EFFECTING,
                                  **({"collective_id": collective_id} if collective_id is not None else {}))
    return pl.pallas_call(body, name=name, in_specs=in_specs, out_specs=out_specs, out_shape=out_shape,
                          input_output_aliases=aliases, compiler_params=params)


def _hbm(arrays):
    return [pltpu.with_memory_space_constraint(a, pltpu.HBM) for a in arrays]


def _gather_send(name, gs, groups, after):
    n = len(gs)

    def body(*refs):
        g, sems, token = refs[:n], refs[n + len(after):n + len(after) + 2 * len(groups)], refs[-1]
        x, y, c, chips = _place()
        me = 2 * x + y
        for gi, members in enumerate(groups):
            for a, i in enumerate(members):
                for k, (px, py) in enumerate(chips):
                    _remote(g[i].at[me, c], g[i].at[me, c], sems[2 * gi].at[3 * a + k], sems[2 * gi + 1].at[3 * a + k],
                            (px, py, c)).start()
        token[...] = jnp.zeros_like(token)

    sem_shapes = [pltpu.SemaphoreType.DMA((3 * len(members),)) for members in groups for _ in range(2)]
    out = _split_copy_call(body, name=name, in_specs=[HBM] * n + [pl.BlockSpec(memory_space=pl.ANY)] * len(after),
                           out_specs=[SEM] * len(sem_shapes) + [HBM] * n + [VMEM],
                           out_shape=sem_shapes + [pltpu.HBM(a.shape, a.dtype) for a in gs] + [_sds((SUBLANES, LANES), F32)],
                           aliases={i: len(sem_shapes) + i for i in range(n)})(*_hbm(gs), *after)
    return out[:len(sem_shapes)], out[len(sem_shapes):-1], out[-1]


def _gather_wait(name, gs, send_sems, recv_sems, after):
    n = len(gs)

    def body(*refs):
        g, ssem, rsem = refs[:n], refs[n], refs[n + 1]
        x, y, c, chips = _place()
        me = 2 * x + y
        for a in range(n):
            for k, (px, py) in enumerate(chips):
                landed = g[a].at[2 * px + py, c]
                cp = _remote(g[a].at[me, c], landed, ssem.at[3 * a + k], rsem.at[3 * a + k], (px, py, c))
                cp.wait_recv()
                cp.wait_send()

    return _split_copy_call(body, name=name, in_specs=[HBM] * n + [SEM, SEM] + [pl.BlockSpec(memory_space=pl.ANY)] * len(after),
                            out_specs=[HBM] * n, out_shape=[pltpu.HBM(a.shape, a.dtype) for a in gs],
                            aliases={i: i for i in range(n)})(*gs, send_sems, recv_sems, *after)


def _gather_pass(name, gs):
    n = len(gs)

    def body(*refs):
        g, send_sems, recv_sems = refs[n:2 * n], refs[-2], refs[-1]
        x, y, c, chips = _place()
        sibling = (x, y, 1 - c)
        _sibling_handshake(sibling)
        passed = [_remote(g[i].at[2 * px + py, c], g[i].at[2 * px + py, c], send_sems.at[3 * i + k], recv_sems.at[3 * i + k], sibling)
                  for i in range(n) for k, (px, py) in enumerate(chips)]
        for cp in passed:
            cp.start()
        for i in range(n):
            for k, (px, py) in enumerate(chips):
                theirs = g[i].at[2 * px + py, 1 - c]
                _remote(theirs, theirs, send_sems.at[3 * i + k], recv_sems.at[3 * i + k], sibling).wait_recv()
        for cp in passed:
            cp.wait_send()

    return _call(body, name=name, in_specs=[HBM] * n, out_specs=[HBM] * n, out_shape=[_sds(a.shape, a.dtype) for a in gs],
                 aliases={i: i for i in range(n)}, collective_id=_SIBLING_PAIR_ID,
                 scratch=[pltpu.SemaphoreType.DMA((3 * n,)), pltpu.SemaphoreType.DMA((3 * n,))])(*gs)


def _scatter_send(name, ps):
    n = len(ps)

    def body(*refs):
        p, r, ssem, rsem, token = refs[:n], refs[n:2 * n], refs[2 * n], refs[2 * n + 1], refs[-1]
        x, y, c, chips = _place()
        for i in range(n):
            for k, (px, py) in enumerate(chips):
                _remote(p[i].at[2 * px + py], r[i].at[k], ssem.at[3 * i + k], rsem.at[3 * i + k], (px, py, c)).start()
        token[...] = jnp.zeros_like(token)

    lands = [lax.empty((N_CHIPS - 1,) + a.shape[1:], a.dtype) for a in ps]
    sem = pltpu.SemaphoreType.DMA((3 * n,))
    out = _split_copy_call(body, name=name, in_specs=[HBM] * (2 * n), out_specs=[SEM, SEM] + [HBM] * (2 * n) + [VMEM],
                           out_shape=[sem, sem] + [pltpu.HBM(a.shape, a.dtype) for a in list(ps) + lands] + [_sds((SUBLANES, LANES), F32)],
                           aliases={i: 2 + i for i in range(2 * n)})(*_hbm(list(ps) + lands))
    return out[0], out[1], out[2:2 + n], out[2 + n:2 + 2 * n], out[-1]


def _scatter_wait(name, ps, lands, send_sems, recv_sems, after):
    n = len(ps)

    def body(*refs):
        p, r, ssem, rsem = refs[:n], refs[n:2 * n], refs[2 * n], refs[2 * n + 1]
        x, y, c, chips = _place()
        for i in range(n):
            for k, (px, py) in enumerate(chips):
                cp = _remote(p[i].at[2 * px + py], r[i].at[k], ssem.at[3 * i + k], rsem.at[3 * i + k], (px, py, c))
                cp.wait_recv()
                cp.wait_send()

    out = _split_copy_call(body, name=name, in_specs=[HBM] * (2 * n) + [SEM, SEM] + [pl.BlockSpec(memory_space=pl.ANY)] * len(after),
                           out_specs=[HBM] * (2 * n), out_shape=[pltpu.HBM(a.shape, a.dtype) for a in list(ps) + list(lands)],
                           aliases={i: i for i in range(2 * n)})(*ps, *lands, send_sems, recv_sems, *after)
    return out[:n], out[n:]


def _exchange_send(name, gs):
    n = len(gs)

    def body(*refs):
        g, r, ssem, rsem, token = refs[:n], refs[n:2 * n], refs[2 * n], refs[2 * n + 1], refs[-1]
        x, y, c, _ = _place()
        _sibling_handshake((x, y, 1 - c))
        for i in range(n):
            _remote(g[i].at[:, 1 - c], r[i], ssem.at[i], rsem.at[i], (x, y, 1 - c)).start()
        token[...] = jnp.zeros_like(token)

    lands = [lax.empty((a.shape[0],) + a.shape[2:], a.dtype) for a in gs]
    sem = pltpu.SemaphoreType.DMA((n,))
    out = _split_copy_call(body, name=name, in_specs=[HBM] * (2 * n), out_specs=[SEM, SEM] + [HBM] * (2 * n) + [VMEM],
                           out_shape=[sem, sem] + [pltpu.HBM(a.shape, a.dtype) for a in list(gs) + lands] + [_sds((SUBLANES, LANES), F32)],
                           aliases={i: 2 + i for i in range(2 * n)}, collective_id=_SIBLING_PAIR_ID)(*_hbm(list(gs) + lands))
    return out[0], out[1], out[2:2 + n], out[2 + n:2 + 2 * n], out[-1]


def _exchange_wait(name, gs, lands, send_sems, recv_sems, after):
    n = len(gs)

    def body(*refs):
        g, r, ssem, rsem = refs[:n], refs[n:2 * n], refs[2 * n], refs[2 * n + 1]
        x, y, c, _ = _place()
        for i in range(n):
            cp = _remote(g[i].at[:, 1 - c], r[i], ssem.at[i], rsem.at[i], (x, y, 1 - c))
            cp.wait_recv()
            cp.wait_send()

    out = _split_copy_call(body, name=name, in_specs=[HBM] * (2 * n) + [SEM, SEM] + [pl.BlockSpec(memory_space=pl.ANY)] * len(after),
                           out_specs=[HBM] * (2 * n), out_shape=[pltpu.HBM(a.shape, a.dtype) for a in list(gs) + list(lands)],
                           aliases={i: i for i in range(2 * n)})(*gs, *lands, send_sems, recv_sems, *after)
    return out[:n], out[n:]


def _sibling_share(name, fs, after=()):
    n = len(fs)

    def body(*refs):
        f, send_sems, recv_sems = refs[n:2 * n], refs[-2], refs[-1]
        x, y, c, _ = _place()
        _sibling_handshake((x, y, 1 - c))
        sends = [_remote(f[i].at[c], f[i].at[c], send_sems.at[i], recv_sems.at[i], (x, y, 1 - c)) for i in range(n)]
        for cp in sends:
            cp.start()
        for i in range(n):
            theirs = f[i].at[1 - c]
            _remote(theirs, theirs, send_sems.at[i], recv_sems.at[i], (x, y, 1 - c)).wait_recv()
        for cp in sends:
            cp.wait_send()

    return _call(body, name=name, in_specs=[HBM] * n, out_specs=[HBM] * n,
                 out_shape=[_sds(a.shape, a.dtype) for a in fs], aliases={i: i for i in range(n)}, after=after,
                 collective_id=_SIBLING_PAIR_ID,
                 scratch=[pltpu.SemaphoreType.DMA((n,)), pltpu.SemaphoreType.DMA((n,))])(*fs)


def _all_reduce_small(name, v):
    rows = v.shape[0] // 2
    halves = (2, rows, LANES)

    def body(v_ref, o_ref, from_sibling, chip_sums, send_sems, recv_sems):
        x, y, c, chips = _place()
        me, sibling = 2 * x + y, (x, y, 1 - c)
        swap = _remote(v_ref.at[1 - c], from_sibling, send_sems.at[0], recv_sems.at[0], sibling)
        swap.start()
        swap.wait()
        chip_sums[me] = v_ref[c] + from_sibling[...]
        sends = [_remote(chip_sums.at[me], chip_sums.at[me], send_sems.at[1 + k], recv_sems.at[1 + k], (px, py, c))
                 for k, (px, py) in enumerate(chips)]
        for cp in sends:
            cp.start()
        for k, (px, py) in enumerate(chips):
            theirs = chip_sums.at[2 * px + py]
            _remote(theirs, theirs, send_sems.at[1 + k], recv_sems.at[1 + k], (px, py, c)).wait_recv()
        for cp in sends:
            cp.wait_send()
        acc = chip_sums[0]
        for j in range(1, N_CHIPS):
            acc = acc + chip_sums[j]
        o_ref[c] = acc
        share = _remote(o_ref.at[c], o_ref.at[c], send_sems.at[4], recv_sems.at[4], sibling)
        share.start()
        share.wait_send()
        _remote(o_ref.at[1 - c], o_ref.at[1 - c], send_sems.at[4], recv_sems.at[4], sibling).wait_recv()

    return _call(body, name=name, in_specs=[VMEM], out_specs=VMEM, out_shape=_sds(halves, F32),
                 scratch=[pltpu.VMEM((rows, LANES), F32), pltpu.VMEM((N_CHIPS, rows, LANES), F32),
                          pltpu.SemaphoreType.DMA((5,)), pltpu.SemaphoreType.DMA((5,))])(v.reshape(halves)).reshape(v.shape)


def _add_halves(name, g, r, c):
    _, _, rows, C = g.shape
    tr = _row_tile(rows)

    def body(c_ref, g_ref, r_ref, o_ref):
        o_ref[...] = (g_ref[...].astype(F32) + r_ref[...].astype(F32)).astype(BF16)

    spec = BS((None, tr, C), lambda j, i, c_ref: (j, i, 0))
    return _prefetch_call(body, name=name, grid=(N_CHIPS, rows // tr),
                          in_specs=[BS((None, None, tr, C), lambda j, i, c_ref: (j, c_ref[0], i, 0)), spec], out_specs=spec,
                          out_shape=pltpu.HBM((N_CHIPS, rows, C), BF16))(c, g, r)


def _sum_partials(name, p, r, chip_c):
    _, rows, C = p.shape
    tr = _row_tile(rows)

    def body(s_ref, p_ref, r_ref, o_ref):
        acc = p_ref[...].astype(F32)
        for k in range(N_CHIPS - 1):
            acc = acc + r_ref[k].astype(F32)
        o_ref[...] = acc

    return _prefetch_call(body, name=name, grid=(rows // tr,),
                          in_specs=[BS((None, tr, C), lambda i, s: (s[0], i, 0)), BS((N_CHIPS - 1, tr, C), lambda i, s: (0, i, 0))],
                          out_specs=BS((None, tr, C), lambda i, s: (s[1], i, 0)), out_shape=pltpu.HBM((2, rows, C), F32))(chip_c, p, r)


_SHARDED = ("even_w_in", "even_w_out", "odd_w_in", "q_b", "kv_b", "odd_w_out", "ffn_w_gate", "ffn_w_up", "ffn_w_down")
_REPLICATED = ("mix_norm", "ffn_norm", "sg_ln_g", "sg_w_s", "sg_b_s", "pool_w", "q_norm", "k_norm")
_SMALL_SHARDED = ("sc_conv_w", "pool_scale", "q_a_norm", "kv_a_norm")
_WEIGHTS = ("mix_norm", "ffn_norm", "even_w_in", "sg_ln_g", "sg_w_s", "sg_b_s", "sc_conv_w", "even_w_out", "odd_w_in", "pool_w",
            "pool_scale", "q_a_norm", "q_b", "kv_a_norm", "kv_b", "q_norm", "k_norm", "odd_w_out", "ffn_w_gate", "ffn_w_up",
            "ffn_w_down")


def _pad_rows(flat, width, align):
    n = flat.shape[0]
    rows = -(-n // (width * align)) * align
    return jnp.pad(flat, (0, rows * width - n)).reshape(rows, width)


_GROUPS = {"even": ("even_w_in", "even_w_out"),
           "ffn0": ("ffn_w_gate0", "ffn_w_up0", "ffn_w_down0"),
           "odd": ("odd_w_in", "q_b", "kv_b", "odd_w_out"),
           "ffn1": ("ffn_w_gate1", "ffn_w_up1", "ffn_w_down1")}


def _place_shards(shards, names, chip, after):
    placed = []
    for n in names:
        weight, layer = (n[:-1], int(n[-1])) if n[-1].isdigit() else (n, 0)
        a = shards[weight]
        placed.append(_cast_place(f"place_{n}", a.reshape(a.shape[0], 2, a.shape[1] // 2, a.shape[2]), layer, chip, after))
    return placed


def _whole_weights(gathered):
    out = {n: a.reshape(N_CHIPS, -1, a.shape[-1]) for n, a in gathered.items()}
    for n in ("q_b", "kv_b"):
        if n in out:
            out[n] = out[n].transpose(1, 0, 2).reshape(out[n].shape[1], -1)
    for n in ("even_w_out", "odd_w_in", "odd_w_out"):
        if n in out:
            out[n] = out[n].reshape(-1, out[n].shape[-1])
    return out


def _forward_backward(x, positions, target, small, fetch, emit, advance):
    batch, seq, _ = x.shape
    T = batch * seq
    tm = _token_tile(seq)
    x0 = x.reshape(T, D_MODEL)

    inv_freq = ROPE_THETA ** (-jnp.arange(0, QK_ROPE, 2, dtype=F32) / QK_ROPE)
    ang = (positions.astype(F32)[..., None] * inv_freq).reshape(T, QK_ROPE // 2)
    cos, sin = jnp.cos(ang), jnp.sin(ang)
    pad = jnp.zeros((T, LANES - QK_ROPE), F32)
    cos_t = jnp.concatenate([cos, cos, pad], axis=1)
    sin_t = jnp.concatenate([-sin, sin, pad], axis=1)

    tril = jnp.tril(jnp.ones((SG_CHUNK, SG_CHUNK), bool))
    w_tril = jnp.where(tril[None], small["sg_w_s"][0], 0.0).astype(BF16)
    b_lanes = jnp.broadcast_to(small["sg_b_s"][0][:, :, None], (SG_HEADS, SG_CHUNK, SG_DIM))
    conv_w = jnp.pad(small["sc_conv_w"][0], ((0, SUBLANES - CONV_TAPS), (0, 0)))
    ln_g = small["sg_ln_g"]
    pool_diag = jnp.zeros((POOL_WIDTH, POOL_WIDTH), F32)
    for g in range(len(POOL_WINDOWS)):
        pool_diag = pool_diag.at[POOL_DIM * g:POOL_DIM * (g + 1), POOL_DIM * g:POOL_DIM * (g + 1)].set(small["pool_w"][0, g])
    pool_diag = pool_diag.astype(BF16)
    pool_scale = small["pool_scale"]
    q_g = jnp.pad(small["q_norm"], ((0, 0), (0, QK_PAD - QK_DIM)))
    k_g = jnp.pad(small["k_norm"], ((0, 0), (0, QK_PAD - QK_DIM)))
    qa_g, kva_g = small["q_a_norm"], small["kv_a_norm"]
    in_shard = EVEN_IN // N_CHIPS

    def ffn_weights(l, w):
        return w[f"ffn_w_gate{l}"], w[f"ffn_w_up{l}"], w[f"ffn_w_down{l}"]

    W = fetch("even", ())
    w_in_even = W["even_w_in"]
    tb = _big_tile(T)
    proj0, h0 = _even_in(x0, small["mix_norm"][0], w_in_even, _resident_tile(T))
    mix0 = _even_mixer_fwd(proj0, ln_g, w_tril, b_lanes, conv_w, seq, tm)
    w_out_even = W["even_w_out"]
    x1, h1 = _mm("even_out", "nn", mix0, w_out_even, F32, tk=1024, add=x0, fused=_norm_tail(small["ffn_norm"][0], T, tb))
    ffn0 = ffn_weights(0, fetch("ffn0", (x1,)))
    (x2, h2), ffn0_saved = _ffn_fwd(0, x1, h1, *ffn0, lambda tile: _norm_tail(small["mix_norm"][1], T, tile))
    W = fetch("odd", (x2,))
    w_in_odd = jnp.pad(W["odd_w_in"], ((0, 0), (0, ODD_IN_PAD - ODD_IN)))
    q_b = jnp.pad(W["q_b"].reshape(Q_LORA, HEADS, QK_DIM).transpose(1, 0, 2), ((0, 0), (0, 0), (0, QK_PAD - QK_DIM)))
    kv_b = W["kv_b"].reshape(KV_LORA, HEADS, QK_NOPE + V_DIM).transpose(1, 0, 2)
    proj1 = _mm("odd_in", "nn", h2, w_in_odd, F32, tk=1024)
    mix1 = _pool_fwd(proj1, pool_diag, pool_scale, seq, tm)
    q, k, v = _mla_qkv_fwd(proj1, cos_t, sin_t, qa_g, kva_g, q_b, kv_b, q_g, k_g, tm)
    mix1, lse = _flash_fwd(q, k, v, mix1, batch, seq)
    x3, h3 = _mm("odd_out", "nn", mix1, W["odd_w_out"], F32, tk=1024, add=x2, fused=_norm_tail(small["ffn_norm"][1], T, tb))
    ffn1 = ffn_weights(1, fetch("ffn1", (x3,)))
    (dy, sq), ffn1_saved = _ffn_fwd(1, x3, h3, *ffn1, lambda tile: _loss_tail(target.reshape(T, D_MODEL), tile))

    G = {}
    dx3, dffn_g1 = _ffn_bwd(1, x3, small["ffn_norm"][1], *ffn1, ffn1_saved, dy, emit)
    dmix1 = _mm("odd_out_dx", "nt", dx3, W["odd_w_out"], BF16, tk=1024, after=advance(dx3))
    dw_out_odd = _mm("odd_out_dw", "tn", mix1, dx3, BF16, hbm_out=True)
    dq, dk, dv = _flash_bwd(q, k, v, dmix1, mix1, lse, batch, seq)
    dz_pool, dpool_diag, G["pool_scale"] = _pool_bwd(proj1, dmix1, pool_diag, pool_scale, seq, tm)
    dproj1, dq_b, dkv_b, dq_g, dk_g, G["q_a_norm"], G["kv_a_norm"] = _mla_qkv_bwd(
        proj1, cos_t, sin_t, qa_g, kva_g, q_b, kv_b, q_g, k_g, dq, dk, dv, dz_pool, tm)
    G["pool_w"] = jnp.stack([dpool_diag[POOL_DIM * g:POOL_DIM * (g + 1), POOL_DIM * g:POOL_DIM * (g + 1)]
                             for g in range(len(POOL_WINDOWS))])[None]
    G["q_norm"], G["k_norm"] = dq_g[:, :QK_DIM], dk_g[:, :QK_DIM]
    dw_in_odd = _mm("odd_in_dw", "tn", h2, dproj1, BF16, tn=ODD_IN, hbm_out=True)

    def shard_major(g, cols):
        return g.reshape(g.shape[0], N_CHIPS, cols).transpose(1, 0, 2).astype(BF16)

    behind = emit("odd", {"odd_w_in": dw_in_odd.reshape(N_CHIPS, -1, ODD_IN),
                          "q_b": shard_major(dq_b[:, :, :QK_DIM].transpose(1, 0, 2).reshape(Q_LORA, HEADS * QK_DIM), HEADS * QK_DIM // N_CHIPS),
                          "kv_b": shard_major(dkv_b.transpose(1, 0, 2).reshape(KV_LORA, HEADS * (QK_NOPE + V_DIM)),
                                              HEADS * (QK_NOPE + V_DIM) // N_CHIPS),
                          "odd_w_out": dw_out_odd.reshape(N_CHIPS, -1, D_MODEL)})
    dx2, dmix_g1 = _mm("odd_in_dx", "nt", dproj1, W["odd_w_in"], F32, tk=ODD_IN, after=behind,
                       fused=_norm_bwd_tail(x2, small["mix_norm"][1], dx3, tb))
    dx1, dffn_g0 = _ffn_bwd(0, x1, small["ffn_norm"][0], *ffn0, ffn0_saved, dx2, emit, after=advance(dx2))
    dmix0 = _mm("even_out_dx", "nt", dx1, w_out_even, F32, tk=1024, after=advance(dx1))
    dw_out_even = _mm("even_out_dw", "tn", mix0, dx1, BF16, hbm_out=True)
    dproj0, dw_s, db_lanes, G["sg_ln_g"], dconv = _even_mixer_bwd(proj0, dmix0, ln_g, w_tril, b_lanes, conv_w, seq, min(tm, 256))
    G["sg_w_s"] = dw_s[None]
    G["sg_b_s"] = jnp.sum(db_lanes, axis=-1)[None]
    G["sc_conv_w"] = dconv[None, :CONV_TAPS]
    tr = _resident_tile(T)
    tail, shapes, specs = _norm_bwd_tail(x0, small["mix_norm"][0], dx1, tr)
    dx0, dmix_g0 = _matmul("even_in_dx", "nt", [(dproj0, w_in_even)],
                           [(_row_spec(tr, EVEN_IN), _resident((N_CHIPS, D_MODEL, in_shard)))],
                           (T // tr, 1, 1), shapes, specs, (tr, D_MODEL), tail=tail)
    tk = min(512, T)
    dw_in_even = _grad_shards(
        "even_in_dw", h0, dproj0, BS((tk, D_MODEL), lambda k: (k, 0)), BS((tk, EVEN_IN), lambda k: (k, 0)),
        lambda a_ref, b_ref, j: (a_ref[...], b_ref[:, in_shard * j:in_shard * (j + 1)]), (N_CHIPS, D_MODEL, in_shard), T // tk)
    emit("even", {"even_w_in": dw_in_even, "even_w_out": dw_out_even.reshape(N_CHIPS, -1, D_MODEL)})
    G["mix_norm"] = jnp.concatenate([dmix_g0, dmix_g1], axis=0)
    G["ffn_norm"] = jnp.concatenate([dffn_g0, dffn_g1], axis=0)
    return sq[0, 0], dx0.reshape(batch, seq, D_MODEL), G


def _small_vector(parts, names):
    flat = jnp.concatenate([parts[n].astype(F32).reshape(-1) for n in names])
    return _pad_rows(flat, LANES, 2 * SUBLANES)


def _split_small(vec, like, names):
    out, off, flat = {}, 0, vec.reshape(-1)
    for n in names:
        size = math.prod(like[n].shape)
        out[n] = flat[off:off + size].reshape(like[n].shape)
        off += size
    return out


def _whole_shape(a):
    return a.shape[:-1] + (a.shape[-1] * N_CHIPS,)


def kernel(x, positions, mix_norm, ffn_norm, even_w_in, sg_ln_g, sg_w_s, sg_b_s, sc_conv_w, even_w_out, odd_w_in, pool_w, pool_scale, q_a_norm, q_b, kv_a_norm, kv_b, q_norm, k_norm, odd_w_out, ffn_w_gate, ffn_w_up, ffn_w_down, loss_target, m_mix_norm, m_ffn_norm, m_even_w_in, m_sg_ln_g, m_sg_w_s, m_sg_b_s, m_sc_conv_w, m_even_w_out, m_odd_w_in, m_pool_w, m_pool_scale, m_q_a_norm, m_q_b, m_kv_a_norm, m_kv_b, m_q_norm, m_k_norm, m_odd_w_out, m_ffn_w_gate, m_ffn_w_up, m_ffn_w_down, v_mix_norm, v_ffn_norm, v_even_w_in, v_sg_ln_g, v_sg_w_s, v_sg_b_s, v_sc_conv_w, v_even_w_out, v_odd_w_in, v_pool_w, v_pool_scale, v_q_a_norm, v_q_b, v_kv_a_norm, v_kv_b, v_q_norm, v_k_norm, v_odd_w_out, v_ffn_w_gate, v_ffn_w_up, v_ffn_w_down):
    args = dict(locals())
    w = {n: args[n] for n in _WEIGHTS}
    m = {n: args["m_" + n] for n in _WEIGHTS}
    v = {n: args["v_" + n] for n in _WEIGHTS}
    cx, cy, cc = lax.axis_index("x"), lax.axis_index("y"), lax.axis_index("c")
    chip = 2 * cx + cy
    transposed = ("ffn_w_gate", "ffn_w_up")
    for n in transposed:
        w[n], m[n], v[n] = (jnp.swapaxes(t[n], 1, 2) for t in (w, m, v))

    chip_arr = chip.astype(jnp.int32).reshape(1)
    c_arr = cc.astype(jnp.int32).reshape(1)
    group_names = list(_GROUPS)
    placed = {}
    for n in _SMALL_SHARDED:
        a = w[n]
        whole = jnp.zeros(a.shape[:-1] + (N_CHIPS, a.shape[-1]), F32)
        whole = lax.dynamic_update_slice_in_dim(whole, a[..., None, :], chip, axis=a.ndim - 1)
        placed[n] = jnp.where(cc == 0, whole, 0.0).reshape(_whole_shape(a))
    small_whole = _all_reduce_small("gather_small_weights", _small_vector(placed, _SMALL_SHARDED))
    small = dict({n: w[n] for n in _REPLICATED}, **_split_small(small_whole, placed, _SMALL_SHARDED))

    first, rest = list(_GROUPS[group_names[0]]), [n for g in group_names[1:] for n in _GROUPS[g]]
    sems_first, flight_first, token = _gather_send("gather_send_first", _place_shards(w, first, chip_arr, (small_whole,)),
                                                   [list(range(len(first)))], (small_whole,))
    sems_rest, flight_rest, all_sent = _gather_send("gather_send_rest", _place_shards(w, rest, chip_arr, (token,)),
                                                    [[rest.index(n) for n in _GROUPS[g]] for g in group_names[1:]], ())
    sems = list(sems_first) + list(sems_rest)
    in_flight = dict(zip(first + rest, list(flight_first) + list(flight_rest)))

    def fetch(group, after):
        gi, members = group_names.index(group), _GROUPS[group]
        after = after if gi else (all_sent,)
        landed = _gather_wait(f"gather_wait_{group}", [in_flight[n] for n in members], sems[2 * gi], sems[2 * gi + 1], after)
        return _whole_weights(dict(zip(members, _gather_pass(f"gather_pass_{group}", landed))))

    swapping, pending, arrived, sent = [], [], {}, []

    def settle(after):
        names, ps, lands, send_sems, recv_sems = pending.pop()
        ps, lands = _scatter_wait(f"scatter_wait_{names[0]}", ps, lands, send_sems, recv_sems, after)
        arrived.update({n: (p, r) for n, p, r in zip(names, ps, lands)})

    def emit(group, grads):
        names = _GROUPS[group]
        halves = [grads[n].reshape(N_CHIPS, 2, grads[n].shape[1] // 2, grads[n].shape[2]) for n in names]
        send_sems, recv_sems, halves, lands, token = _exchange_send(f"exchange_send_{group}", halves)
        swapping.append((group, halves, lands, send_sems, recv_sems))
        sent.append(token)
        return (token,)

    def advance(done):
        done = done if isinstance(done, tuple) else (done,)
        group, halves, lands, send_sems, recv_sems = swapping.pop()
        names = _GROUPS[group]
        halves, lands = _exchange_wait(f"exchange_wait_{group}", halves, lands, send_sems, recv_sems, done)
        partial = [_add_halves(f"add_{n}", g, r, c_arr) for n, g, r in zip(names, halves, lands)]
        if pending:
            settle(done)
        send_sems, recv_sems, ps, lands, token = _scatter_send(f"scatter_send_{group}", partial)
        pending.append((names, ps, lands, send_sems, recv_sems))
        return (token,)

    sq, grad_x, G = _forward_backward(x, positions, loss_target, small, fetch, emit, advance)
    small_names = _REPLICATED + _SMALL_SHARDED
    G["loss"] = (0.5 * sq / D_MODEL).reshape(1)
    summed = _split_small(_all_reduce_small("reduce_small_grads", _small_vector(G, small_names + ("loss",))), G,
                          small_names + ("loss",))
    loss = summed["loss"][0]
    grads = {n: summed[n] for n in _REPLICATED}
    for n in _SMALL_SHARDED:
        a = w[n]
        grads[n] = lax.dynamic_slice_in_dim(summed[n].reshape(a.shape[:-1] + (N_CHIPS, a.shape[-1])), chip, 1,
                                            axis=a.ndim - 1).reshape(a.shape)

    chip_c = jnp.stack([chip, cc]).astype(jnp.int32)
    out = {}

    def finish(group, after):
        names, tokens = _GROUPS[group], []
        sums = [_sum_partials(f"sum_{n}", *arrived[n], chip_c) for n in names]
        for n, f in zip(names, _sibling_share(f"grad_share_{group}", sums, after)):
            weight, layer = (n[:-1], int(n[-1])) if n[-1].isdigit() else (n, 0)
            *out[weight], token = _adamw(f"adamw_{weight}", w[weight], f.reshape(-1, f.shape[-1]), m[weight], v[weight], layer,
                                         out.get(weight, ()))
            tokens.append(token)
        return tuple(tokens)

    last_exchange = tuple(sent[-1:])
    last_scatter = advance(finish(group_names[3], last_exchange) + finish(group_names[2], last_exchange))
    settle(finish(group_names[1], last_scatter))
    finish(group_names[0], ())
    packed = [_small_vector(d, small_names) for d in (w, grads, m, v)]
    res = _adamw("adamw_small", packed[0][None], packed[1], packed[2][None], packed[3][None])
    delta_s, m_s, v_s = (_split_small(r, w, small_names) for r in res[1:4])
    for n in small_names:
        out[n] = (grads[n], delta_s[n], m_s[n], v_s[n])
    for n in transposed:
        out[n] = tuple(jnp.swapaxes(t, 1, 2) for t in out[n])

    return (loss, grad_x, *[out[n][0] for n in _WEIGHTS], *[out[n][1] for n in _WEIGHTS],
            *[out[n][2] for n in _WEIGHTS], *[out[n][3] for n in _WEIGHTS])
```
